```python
import jax, jax.numpy as jnp
from jax import lax
import numpy as np

D_MODEL = 1024
BATCH = 8
SEQ = 4096
DEPTH = 2

CHUNK = 64
N_MIXERS = 2
N_RET = (DEPTH + 1) // 2
N_MLA = DEPTH // 2
ROPE_THETA = 10000.0
EPS = 1e-6
RET_HEADS = 4
RET_DK = D_MODEL // RET_HEADS
RET_DV = 2 * RET_DK
RET_QK_W = RET_HEADS * RET_DK
RET_V_W = RET_HEADS * RET_DV
RET_IN = 2 * RET_QK_W + 2 * RET_V_W
MLA_HEADS = 8
MLA_NOPE = 128
MLA_ROPE = 64
MLA_QKD = MLA_NOPE + MLA_ROPE
MLA_VD = 128
MLA_Q_RANK = 384
MLA_KV_RANK = 256
MLA_IN = MLA_Q_RANK + MLA_KV_RANK + MLA_ROPE
Q_BLOCK = 128
D_FF = 4 * D_MODEL
PLE_DIM = 256

kernel_name = "hybrid_retention_mla_trunk"


def _rmsnorm(x, g):
    xf = x.astype(jnp.float32)
    y = xf * lax.rsqrt(jnp.mean(xf * xf, axis=-1, keepdims=True) + EPS)
    return (y * g.astype(jnp.float32)).astype(x.dtype)


def _rope_tables(seq, dim, dtype):
    inv = 1.0 / (ROPE_THETA ** (jnp.arange(0, dim, 2, dtype=jnp.float32) / dim))
    ang = jnp.arange(seq, dtype=jnp.float32)[:, None] * inv[None, :]
    return jnp.cos(ang)[:, None, :].astype(dtype), jnp.sin(ang)[:, None, :].astype(dtype)


def _rope(x, cos, sin):
    x1, x2 = jnp.split(x, 2, axis=-1)
    return jnp.concatenate([x1 * cos - x2 * sin, x2 * cos + x1 * sin], axis=-1)


def _retention(h, w_in, gn_gain, w_out):
    B, S, _ = h.shape
    nc = S // CHUNK
    proj = h @ w_in
    q, k, v, g = jnp.split(proj, [RET_QK_W, 2 * RET_QK_W, 2 * RET_QK_W + RET_V_W], axis=-1)
    cos, sin = _rope_tables(S, RET_DK, h.dtype)
    q = _rope(q.reshape(B, S, RET_HEADS, RET_DK), cos, sin)
    k = _rope(k.reshape(B, S, RET_HEADS, RET_DK), cos, sin) * (RET_DK ** -0.5)
    v = v.reshape(B, S, RET_HEADS, RET_DV)

    def to_chunks(t):
        return t.reshape(B, nc, CHUNK, RET_HEADS, t.shape[-1]).transpose(1, 0, 3, 2, 4)

    qc, kc, vc = to_chunks(q), to_chunks(k), to_chunks(v)
    log_gamma = jnp.log(1.0 - 2.0 ** (-5.0 - jnp.arange(RET_HEADS, dtype=jnp.float32)))
    idx = jnp.arange(CHUNK, dtype=jnp.float32)
    dist = jnp.abs(idx[:, None] - idx[None, :])
    intra = jnp.exp(log_gamma[:, None, None] * dist).astype(h.dtype)
    q_decay = jnp.exp(log_gamma[:, None] * (idx + 1.0))[None, :, :, None].astype(h.dtype)
    k_decay = jnp.exp(log_gamma[:, None] * (CHUNK - 1.0 - idx))[None, :, :, None].astype(h.dtype)
    chunk_decay = jnp.exp(log_gamma * CHUNK)[None, :, None, None].astype(h.dtype)

    def step(state, xs):
        qi, ki, vi = xs
        scores = jnp.einsum('bhnd,bhmd->bhnm', qi, ki) * intra
        inner = jnp.einsum('bhnm,bhmv->bhnv', scores, vi)
        cross = jnp.einsum('bhnd,bhdv->bhnv', qi * q_decay, state)
        new_state = state * chunk_decay + jnp.einsum('bhmd,bhmv->bhdv', ki * k_decay, vi)
        return new_state, inner + cross

    state0 = jnp.zeros((B, RET_HEADS, RET_DK, RET_DV), h.dtype)
    _, out = lax.scan(step, state0, (qc, kc, vc))
    out = out.transpose(1, 0, 3, 2, 4).reshape(B, S, RET_HEADS, RET_DV)
    out = _rmsnorm(out, gn_gain).reshape(B, S, RET_V_W)
    return (jax.nn.silu(g) * out) @ w_out


def _mla(h, w_in, q_a_gain, kv_a_gain, w_uq, w_ukv, q_gain, k_gain, w_out):
    B, S, _ = h.shape
    proj = h @ w_in
    c_q, c_kv, k_r = jnp.split(proj, [MLA_Q_RANK, MLA_Q_RANK + MLA_KV_RANK], axis=-1)
    q = (_rmsnorm(c_q, q_a_gain) @ w_uq).reshape(B, S, MLA_HEADS, MLA_QKD)
    kv = (_rmsnorm(c_kv, kv_a_gain) @ w_ukv).reshape(B, S, MLA_HEADS, MLA_NOPE + MLA_VD)
    k_nope, v = jnp.split(kv, [MLA_NOPE], axis=-1)
    k = jnp.concatenate([k_nope, jnp.broadcast_to(k_r[:, :, None, :], (B, S, MLA_HEADS, MLA_ROPE))], axis=-1)
    q = _rmsnorm(q, q_gain)
    k = _rmsnorm(k, k_gain)
    cos, sin = _rope_tables(S, MLA_ROPE, h.dtype)
    q = jnp.concatenate([q[..., :MLA_NOPE], _rope(q[..., MLA_NOPE:], cos, sin)], axis=-1)
    k = jnp.concatenate([k[..., :MLA_NOPE], _rope(k[..., MLA_NOPE:], cos, sin)], axis=-1)
    q = q.transpose(0, 2, 1, 3)
    k = k.transpose(0, 2, 1, 3)
    v = v.transpose(0, 2, 1, 3)
    scale = MLA_QKD ** -0.5
    outs = []
    for blk in range(S // Q_BLOCK):
        q0 = blk * Q_BLOCK
        kend = q0 + Q_BLOCK
        qb = q[:, :, q0:kend]
        kb = k[:, :, :kend]
        vb = v[:, :, :kend]
        s = jnp.einsum('bhqd,bhkd->bhqk', qb, kb).astype(jnp.float32) * scale
        q_chunk = (q0 + jnp.arange(Q_BLOCK)) // CHUNK
        k_chunk = jnp.arange(kend) // CHUNK
        s = jnp.where(k_chunk[None, :] <= q_chunk[:, None], s, -1e30)
        pr = jax.nn.softmax(s, axis=-1).astype(vb.dtype)
        outs.append(jnp.einsum('bhqk,bhkd->bhqd', pr, vb))
    o = jnp.concatenate(outs, axis=2).transpose(0, 2, 1, 3).reshape(B, S, MLA_HEADS * MLA_VD)
    return o @ w_out


def _dense(k, shape, fan_in):
    return jax.random.normal(k, shape, jnp.float32) * (fan_in ** -0.5)


def _gain(k, shape):
    return 1.0 + 0.05 * jax.random.normal(k, shape, jnp.float32)


def _fwd_setup_inputs(seed: int = 0) -> dict:
    key = jax.random.key(seed)
    ks = jax.random.split(key, 20)
    return {
        "x": jax.random.normal(ks[0], (BATCH, SEQ, D_MODEL), jnp.float32),
        "p": jax.random.normal(ks[1], (DEPTH, BATCH, SEQ, PLE_DIM), jnp.float32),
        "mix_norm": _gain(ks[2], (DEPTH, D_MODEL)),
        "ret_w_in": _dense(ks[3], (N_RET, D_MODEL, RET_IN), D_MODEL),
        "ret_gn": _gain(ks[4], (N_RET, RET_HEADS, RET_DV)),
        "ret_w_out": _dense(ks[5], (N_RET, RET_V_W, D_MODEL), RET_V_W),
        "mla_w_in": _dense(ks[6], (N_MLA, D_MODEL, MLA_IN), D_MODEL),
        "mla_q_a_norm": _gain(ks[7], (N_MLA, MLA_Q_RANK)),
        "mla_kv_a_norm": _gain(ks[8], (N_MLA, MLA_KV_RANK)),
        "mla_w_uq": _dense(ks[9], (N_MLA, MLA_Q_RANK, MLA_HEADS * MLA_QKD), MLA_Q_RANK),
        "mla_w_ukv": _dense(ks[10], (N_MLA, MLA_KV_RANK, MLA_HEADS * (MLA_NOPE + MLA_VD)), MLA_KV_RANK),
        "mla_q_norm": _gain(ks[11], (N_MLA, MLA_QKD)),
        "mla_k_norm": _gain(ks[12], (N_MLA, MLA_QKD)),
        "mla_w_out": _dense(ks[13], (N_MLA, MLA_HEADS * MLA_VD, D_MODEL), MLA_HEADS * MLA_VD),
        "mlp_norm": _gain(ks[14], (DEPTH, D_MODEL)),
        "mlp_w1": _dense(ks[15], (DEPTH, D_MODEL, D_FF), D_MODEL),
        "mlp_w2": _dense(ks[16], (DEPTH, D_FF, D_MODEL), D_FF),
        "ple_norm": _gain(ks[17], (DEPTH, D_MODEL)),
        "ple_gate_w": _dense(ks[18], (DEPTH, D_MODEL, D_MODEL), D_MODEL),
        "ple_proj_w": _dense(ks[19], (DEPTH, PLE_DIM, D_MODEL), PLE_DIM),
    }


def _fwd_reference(x, p, mix_norm, ret_w_in, ret_gn, ret_w_out, mla_w_in, mla_q_a_norm, mla_kv_a_norm,
              mla_w_uq, mla_w_ukv, mla_q_norm, mla_k_norm, mla_w_out, mlp_norm, mlp_w1, mlp_w2,
              ple_norm, ple_gate_w, ple_proj_w):
    h = x
    for i in range(DEPTH):
        j = i // N_MIXERS
        hn = _rmsnorm(h, mix_norm[i])
        if i % N_MIXERS == 0:
            mixed = _retention(hn, ret_w_in[j], ret_gn[j], ret_w_out[j])
        else:
            mixed = _mla(hn, mla_w_in[j], mla_q_a_norm[j], mla_kv_a_norm[j], mla_w_uq[j], mla_w_ukv[j],
                         mla_q_norm[j], mla_k_norm[j], mla_w_out[j])
        h = h + mixed
        hn = _rmsnorm(h, mlp_norm[i])
        h = h + jnp.square(jax.nn.relu(hn @ mlp_w1[i])) @ mlp_w2[i]
        gate = jax.nn.sigmoid(_rmsnorm(h, ple_norm[i]) @ ple_gate_w[i])
        h = h + gate * (p[i] @ ple_proj_w[i])
    return h


import jax as _jax
import jax.numpy as _jnp

TWIN_FORMAT = 'train_step'
FWD_PARAMS = ['x', 'p', 'mix_norm', 'ret_w_in', 'ret_gn', 'ret_w_out', 'mla_w_in', 'mla_q_a_norm', 'mla_kv_a_norm', 'mla_w_uq', 'mla_w_ukv', 'mla_q_norm', 'mla_k_norm', 'mla_w_out', 'mlp_norm', 'mlp_w1', 'mlp_w2', 'ple_norm', 'ple_gate_w', 'ple_proj_w']
TWIN_WEIGHTS = ['mix_norm', 'ret_w_in', 'ret_gn', 'ret_w_out', 'mla_w_in', 'mla_q_a_norm', 'mla_kv_a_norm', 'mla_w_uq', 'mla_w_ukv', 'mla_q_norm', 'mla_k_norm', 'mla_w_out', 'mlp_norm', 'mlp_w1', 'mlp_w2', 'ple_norm', 'ple_gate_w', 'ple_proj_w']
TWIN_DIFF_INPUT = 'x'
TWIN_INPUTS = ['x', 'p', 'mix_norm', 'ret_w_in', 'ret_gn', 'ret_w_out', 'mla_w_in', 'mla_q_a_norm', 'mla_kv_a_norm', 'mla_w_uq', 'mla_w_ukv', 'mla_q_norm', 'mla_k_norm', 'mla_w_out', 'mlp_norm', 'mlp_w1', 'mlp_w2', 'ple_norm', 'ple_gate_w', 'ple_proj_w', 'loss_target', 'm_mix_norm', 'm_ret_w_in', 'm_ret_gn', 'm_ret_w_out', 'm_mla_w_in', 'm_mla_q_a_norm', 'm_mla_kv_a_norm', 'm_mla_w_uq', 'm_mla_w_ukv', 'm_mla_q_norm', 'm_mla_k_norm', 'm_mla_w_out', 'm_mlp_norm', 'm_mlp_w1', 'm_mlp_w2', 'm_ple_norm', 'm_ple_gate_w', 'm_ple_proj_w', 'v_mix_norm', 'v_ret_w_in', 'v_ret_gn', 'v_ret_w_out', 'v_mla_w_in', 'v_mla_q_a_norm', 'v_mla_kv_a_norm', 'v_mla_w_uq', 'v_mla_w_ukv', 'v_mla_q_norm', 'v_mla_k_norm', 'v_mla_w_out', 'v_mlp_norm', 'v_mlp_w1', 'v_mlp_w2', 'v_ple_norm', 'v_ple_gate_w', 'v_ple_proj_w']
TWIN_OUTPUTS = ['loss', 'grad_x', 'grad_mix_norm', 'grad_ret_w_in', 'grad_ret_gn', 'grad_ret_w_out', 'grad_mla_w_in', 'grad_mla_q_a_norm', 'grad_mla_kv_a_norm', 'grad_mla_w_uq', 'grad_mla_w_ukv', 'grad_mla_q_norm', 'grad_mla_k_norm', 'grad_mla_w_out', 'grad_mlp_norm', 'grad_mlp_w1', 'grad_mlp_w2', 'grad_ple_norm', 'grad_ple_gate_w', 'grad_ple_proj_w', 'delta_mix_norm', 'delta_ret_w_in', 'delta_ret_gn', 'delta_ret_w_out', 'delta_mla_w_in', 'delta_mla_q_a_norm', 'delta_mla_kv_a_norm', 'delta_mla_w_uq', 'delta_mla_w_ukv', 'delta_mla_q_norm', 'delta_mla_k_norm', 'delta_mla_w_out', 'delta_mlp_norm', 'delta_mlp_w1', 'delta_mlp_w2', 'delta_ple_norm', 'delta_ple_gate_w', 'delta_ple_proj_w', 'new_m_mix_norm', 'new_m_ret_w_in', 'new_m_ret_gn', 'new_m_ret_w_out', 'new_m_mla_w_in', 'new_m_mla_q_a_norm', 'new_m_mla_kv_a_norm', 'new_m_mla_w_uq', 'new_m_mla_w_ukv', 'new_m_mla_q_norm', 'new_m_mla_k_norm', 'new_m_mla_w_out', 'new_m_mlp_norm', 'new_m_mlp_w1', 'new_m_mlp_w2', 'new_m_ple_norm', 'new_m_ple_gate_w', 'new_m_ple_proj_w', 'new_v_mix_norm', 'new_v_ret_w_in', 'new_v_ret_gn', 'new_v_ret_w_out', 'new_v_mla_w_in', 'new_v_mla_q_a_norm', 'new_v_mla_kv_a_norm', 'new_v_mla_w_uq', 'new_v_mla_w_ukv', 'new_v_mla_q_norm', 'new_v_mla_k_norm', 'new_v_mla_w_out', 'new_v_mlp_norm', 'new_v_mlp_w1', 'new_v_mlp_w2', 'new_v_ple_norm', 'new_v_ple_gate_w', 'new_v_ple_proj_w']
TWIN_LEAF_KINDS = {'loss': 'loss', 'grad_x': 'grad_x', 'grad_mix_norm': 'grad_w', 'grad_ret_w_in': 'grad_w', 'grad_ret_gn': 'grad_w', 'grad_ret_w_out': 'grad_w', 'grad_mla_w_in': 'grad_w', 'grad_mla_q_a_norm': 'grad_w', 'grad_mla_kv_a_norm': 'grad_w', 'grad_mla_w_uq': 'grad_w', 'grad_mla_w_ukv': 'grad_w', 'grad_mla_q_norm': 'grad_w', 'grad_mla_k_norm': 'grad_w', 'grad_mla_w_out': 'grad_w', 'grad_mlp_norm': 'grad_w', 'grad_mlp_w1': 'grad_w', 'grad_mlp_w2': 'grad_w', 'grad_ple_norm': 'grad_w', 'grad_ple_gate_w': 'grad_w', 'grad_ple_proj_w': 'grad_w', 'delta_mix_norm': 'delta_w', 'delta_ret_w_in': 'delta_w', 'delta_ret_gn': 'delta_w', 'delta_ret_w_out': 'delta_w', 'delta_mla_w_in': 'delta_w', 'delta_mla_q_a_norm': 'delta_w', 'delta_mla_kv_a_norm': 'delta_w', 'delta_mla_w_uq': 'delta_w', 'delta_mla_w_ukv': 'delta_w', 'delta_mla_q_norm': 'delta_w', 'delta_mla_k_norm': 'delta_w', 'delta_mla_w_out': 'delta_w', 'delta_mlp_norm': 'delta_w', 'delta_mlp_w1': 'delta_w', 'delta_mlp_w2': 'delta_w', 'delta_ple_norm': 'delta_w', 'delta_ple_gate_w': 'delta_w', 'delta_ple_proj_w': 'delta_w', 'new_m_mix_norm': 'new_m', 'new_m_ret_w_in': 'new_m', 'new_m_ret_gn': 'new_m', 'new_m_ret_w_out': 'new_m', 'new_m_mla_w_in': 'new_m', 'new_m_mla_q_a_norm': 'new_m', 'new_m_mla_kv_a_norm': 'new_m', 'new_m_mla_w_uq': 'new_m', 'new_m_mla_w_ukv': 'new_m', 'new_m_mla_q_norm': 'new_m', 'new_m_mla_k_norm': 'new_m', 'new_m_mla_w_out': 'new_m', 'new_m_mlp_norm': 'new_m', 'new_m_mlp_w1': 'new_m', 'new_m_mlp_w2': 'new_m', 'new_m_ple_norm': 'new_m', 'new_m_ple_gate_w': 'new_m', 'new_m_ple_proj_w': 'new_m', 'new_v_mix_norm': 'new_v', 'new_v_ret_w_in': 'new_v', 'new_v_ret_gn': 'new_v', 'new_v_ret_w_out': 'new_v', 'new_v_mla_w_in': 'new_v', 'new_v_mla_q_a_norm': 'new_v', 'new_v_mla_kv_a_norm': 'new_v', 'new_v_mla_w_uq': 'new_v', 'new_v_mla_w_ukv': 'new_v', 'new_v_mla_q_norm': 'new_v', 'new_v_mla_k_norm': 'new_v', 'new_v_mla_w_out': 'new_v', 'new_v_mlp_norm': 'new_v', 'new_v_mlp_w1': 'new_v', 'new_v_mlp_w2': 'new_v', 'new_v_ple_norm': 'new_v', 'new_v_ple_gate_w': 'new_v', 'new_v_ple_proj_w': 'new_v'}


def _forward(args):
    return _fwd_reference(*[args[k] for k in FWD_PARAMS])


def _output_shape():
    out = _jax.eval_shape(lambda: _forward(_fwd_setup_inputs(0)))
    return out.shape, out.dtype

N_MICROBATCH = 1
ADAM_LR = 0.001
ADAM_B1 = 0.9
ADAM_B2 = 0.999
ADAM_EPS = 1e-08
ADAM_WD = 0.01
ADAM_STEP = 10
PER_EXAMPLE_BATCH_AXIS = {'x': 0, 'p': 1, 'loss_target': 0}
SHARED_INPUTS = []
_WEIGHT_DTYPES = {'mix_norm': _jnp.float32, 'ret_w_in': _jnp.float32, 'ret_gn': _jnp.float32, 'ret_w_out': _jnp.float32, 'mla_w_in': _jnp.float32, 'mla_q_a_norm': _jnp.float32, 'mla_kv_a_norm': _jnp.float32, 'mla_w_uq': _jnp.float32, 'mla_w_ukv': _jnp.float32, 'mla_q_norm': _jnp.float32, 'mla_k_norm': _jnp.float32, 'mla_w_out': _jnp.float32, 'mlp_norm': _jnp.float32, 'mlp_w1': _jnp.float32, 'mlp_w2': _jnp.float32, 'ple_norm': _jnp.float32, 'ple_gate_w': _jnp.float32, 'ple_proj_w': _jnp.float32}
MOMENT_SCALE = {'mix_norm': 1.144277e+01, 'ret_w_in': 5.816688e-01, 'ret_gn': 5.214144e+00, 'ret_w_out': 7.713739e-01, 'mla_w_in': 1.368744e+01, 'mla_q_a_norm': 6.927924e-01, 'mla_kv_a_norm': 2.733757e+01, 'mla_w_uq': 3.605616e-01, 'mla_w_ukv': 6.930558e+00, 'mla_q_norm': 1.117003e+00, 'mla_k_norm': 1.107318e+00, 'mla_w_out': 9.427353e+00, 'mlp_norm': 9.860116e+01, 'mlp_w1': 4.077198e+00, 'mlp_w2': 1.871608e+01, 'ple_norm': 1.401976e+00, 'ple_gate_w': 9.950256e-01, 'ple_proj_w': 5.021345e-01}


def _to_microbatches(a, axis):
    t = _jnp.moveaxis(a, axis, 0)
    t = t.reshape((N_MICROBATCH, t.shape[0] // N_MICROBATCH) + t.shape[1:])
    return _jnp.moveaxis(t, 1, axis + 1)


def setup_inputs(seed: int = 0) -> dict:
    inp = _fwd_setup_inputs(seed)
    key = _jax.random.fold_in(_jax.random.key(seed), 7919)
    shape, _ = _output_shape()
    out = dict(inp)
    out["loss_target"] = _jax.random.normal(_jax.random.fold_in(key, 0), shape, _jnp.float32)
    for i, name in enumerate(TWIN_WEIGHTS):
        w = inp[name].astype(_jnp.float32)
        if MOMENT_SCALE is None:
            s = _jnp.sqrt(_jnp.mean(_jnp.square(w)) + 1e-30)
        else:
            s = MOMENT_SCALE[name]
        km, kv = _jax.random.split(_jax.random.fold_in(key, i + 1))
        out[name] = w
        out["m_" + name] = s * _jax.random.normal(km, w.shape, _jnp.float32)
        out["v_" + name] = (s * s) * _jax.random.uniform(kv, w.shape, _jnp.float32, 0.5, 1.5)
    if N_MICROBATCH > 1:
        for name, axis in PER_EXAMPLE_BATCH_AXIS.items():
            out[name] = _to_microbatches(out[name], axis)
    return {'x': out['x'], 'p': out['p'], 'mix_norm': out['mix_norm'], 'ret_w_in': out['ret_w_in'], 'ret_gn': out['ret_gn'], 'ret_w_out': out['ret_w_out'], 'mla_w_in': out['mla_w_in'], 'mla_q_a_norm': out['mla_q_a_norm'], 'mla_kv_a_norm': out['mla_kv_a_norm'], 'mla_w_uq': out['mla_w_uq'], 'mla_w_ukv': out['mla_w_ukv'], 'mla_q_norm': out['mla_q_norm'], 'mla_k_norm': out['mla_k_norm'], 'mla_w_out': out['mla_w_out'], 'mlp_norm': out['mlp_norm'], 'mlp_w1': out['mlp_w1'], 'mlp_w2': out['mlp_w2'], 'ple_norm': out['ple_norm'], 'ple_gate_w': out['ple_gate_w'], 'ple_proj_w': out['ple_proj_w'], 'loss_target': out['loss_target'], 'm_mix_norm': out['m_mix_norm'], 'm_ret_w_in': out['m_ret_w_in'], 'm_ret_gn': out['m_ret_gn'], 'm_ret_w_out': out['m_ret_w_out'], 'm_mla_w_in': out['m_mla_w_in'], 'm_mla_q_a_norm': out['m_mla_q_a_norm'], 'm_mla_kv_a_norm': out['m_mla_kv_a_norm'], 'm_mla_w_uq': out['m_mla_w_uq'], 'm_mla_w_ukv': out['m_mla_w_ukv'], 'm_mla_q_norm': out['m_mla_q_norm'], 'm_mla_k_norm': out['m_mla_k_norm'], 'm_mla_w_out': out['m_mla_w_out'], 'm_mlp_norm': out['m_mlp_norm'], 'm_mlp_w1': out['m_mlp_w1'], 'm_mlp_w2': out['m_mlp_w2'], 'm_ple_norm': out['m_ple_norm'], 'm_ple_gate_w': out['m_ple_gate_w'], 'm_ple_proj_w': out['m_ple_proj_w'], 'v_mix_norm': out['v_mix_norm'], 'v_ret_w_in': out['v_ret_w_in'], 'v_ret_gn': out['v_ret_gn'], 'v_ret_w_out': out['v_ret_w_out'], 'v_mla_w_in': out['v_mla_w_in'], 'v_mla_q_a_norm': out['v_mla_q_a_norm'], 'v_mla_kv_a_norm': out['v_mla_kv_a_norm'], 'v_mla_w_uq': out['v_mla_w_uq'], 'v_mla_w_ukv': out['v_mla_w_ukv'], 'v_mla_q_norm': out['v_mla_q_norm'], 'v_mla_k_norm': out['v_mla_k_norm'], 'v_mla_w_out': out['v_mla_w_out'], 'v_mlp_norm': out['v_mlp_norm'], 'v_mlp_w1': out['v_mlp_w1'], 'v_mlp_w2': out['v_mlp_w2'], 'v_ple_norm': out['v_ple_norm'], 'v_ple_gate_w': out['v_ple_gate_w'], 'v_ple_proj_w': out['v_ple_proj_w']}


def _loss(weights, diff, rest, loss_target):
    with _jax.named_scope("forward"):
        args = {**rest, TWIN_DIFF_INPUT: diff, **{k: w.astype(_WEIGHT_DTYPES[k]) for k, w in weights.items()}}
        y = _forward(args)
    with _jax.named_scope("loss_head"):
        err = _jnp.square(y.astype(_jnp.float32) - loss_target)
        return 0.5 * _jnp.sum(_jnp.mean(err, axis=-1)) if err.ndim else 0.5 * err


def _adamw(w, g, m, v):
    m = ADAM_B1 * m + (1.0 - ADAM_B1) * g
    v = ADAM_B2 * v + (1.0 - ADAM_B2) * _jnp.square(g)
    m_hat = m / (1.0 - ADAM_B1 ** ADAM_STEP)
    v_hat = v / (1.0 - ADAM_B2 ** ADAM_STEP)
    delta = -ADAM_LR * (m_hat / (_jnp.sqrt(v_hat) + ADAM_EPS) + ADAM_WD * w)
    return delta, m, v


def reference(x, p, mix_norm, ret_w_in, ret_gn, ret_w_out, mla_w_in, mla_q_a_norm, mla_kv_a_norm, mla_w_uq, mla_w_ukv, mla_q_norm, mla_k_norm, mla_w_out, mlp_norm, mlp_w1, mlp_w2, ple_norm, ple_gate_w, ple_proj_w, loss_target, m_mix_norm, m_ret_w_in, m_ret_gn, m_ret_w_out, m_mla_w_in, m_mla_q_a_norm, m_mla_kv_a_norm, m_mla_w_uq, m_mla_w_ukv, m_mla_q_norm, m_mla_k_norm, m_mla_w_out, m_mlp_norm, m_mlp_w1, m_mlp_w2, m_ple_norm, m_ple_gate_w, m_ple_proj_w, v_mix_norm, v_ret_w_in, v_ret_gn, v_ret_w_out, v_mla_w_in, v_mla_q_a_norm, v_mla_kv_a_norm, v_mla_w_uq, v_mla_w_ukv, v_mla_q_norm, v_mla_k_norm, v_mla_w_out, v_mlp_norm, v_mlp_w1, v_mlp_w2, v_ple_norm, v_ple_gate_w, v_ple_proj_w):
    given = dict(x=x, p=p, mix_norm=mix_norm, ret_w_in=ret_w_in, ret_gn=ret_gn, ret_w_out=ret_w_out, mla_w_in=mla_w_in, mla_q_a_norm=mla_q_a_norm, mla_kv_a_norm=mla_kv_a_norm, mla_w_uq=mla_w_uq, mla_w_ukv=mla_w_ukv, mla_q_norm=mla_q_norm, mla_k_norm=mla_k_norm, mla_w_out=mla_w_out, mlp_norm=mlp_norm, mlp_w1=mlp_w1, mlp_w2=mlp_w2, ple_norm=ple_norm, ple_gate_w=ple_gate_w, ple_proj_w=ple_proj_w, loss_target=loss_target, m_mix_norm=m_mix_norm, m_ret_w_in=m_ret_w_in, m_ret_gn=m_ret_gn, m_ret_w_out=m_ret_w_out, m_mla_w_in=m_mla_w_in, m_mla_q_a_norm=m_mla_q_a_norm, m_mla_kv_a_norm=m_mla_kv_a_norm, m_mla_w_uq=m_mla_w_uq, m_mla_w_ukv=m_mla_w_ukv, m_mla_q_norm=m_mla_q_norm, m_mla_k_norm=m_mla_k_norm, m_mla_w_out=m_mla_w_out, m_mlp_norm=m_mlp_norm, m_mlp_w1=m_mlp_w1, m_mlp_w2=m_mlp_w2, m_ple_norm=m_ple_norm, m_ple_gate_w=m_ple_gate_w, m_ple_proj_w=m_ple_proj_w, v_mix_norm=v_mix_norm, v_ret_w_in=v_ret_w_in, v_ret_gn=v_ret_gn, v_ret_w_out=v_ret_w_out, v_mla_w_in=v_mla_w_in, v_mla_q_a_norm=v_mla_q_a_norm, v_mla_kv_a_norm=v_mla_kv_a_norm, v_mla_w_uq=v_mla_w_uq, v_mla_w_ukv=v_mla_w_ukv, v_mla_q_norm=v_mla_q_norm, v_mla_k_norm=v_mla_k_norm, v_mla_w_out=v_mla_w_out, v_mlp_norm=v_mlp_norm, v_mlp_w1=v_mlp_w1, v_mlp_w2=v_mlp_w2, v_ple_norm=v_ple_norm, v_ple_gate_w=v_ple_gate_w, v_ple_proj_w=v_ple_proj_w)
    weights = {n: given[n] for n in TWIN_WEIGHTS}
    shared = {n: given[n] for n in SHARED_INPUTS}
    per_example = {n: given[n] for n in ['x', 'p']}
    grad_fn = _jax.value_and_grad(_loss, argnums=(0, 1))

    def one_microbatch(ex, loss_target):
        ex = dict(ex)
        diff = ex.pop(TWIN_DIFF_INPUT)
        return grad_fn(weights, diff, {**shared, **ex}, loss_target)

    if N_MICROBATCH == 1:
        loss, (grad_w, grad_x) = one_microbatch(per_example, given["loss_target"])
    else:
        def body(carry, xs):
            loss_sum, grad_sum = carry
            l_k, (gw_k, gx_k) = one_microbatch(xs[0], xs[1])
            with _jax.named_scope("update"):
                return (loss_sum + l_k, _jax.tree.map(_jnp.add, grad_sum, gw_k)), gx_k

        init = (_jnp.zeros((), _jnp.float32), _jax.tree.map(_jnp.zeros_like, weights))
        (loss, grad_w), grad_x = _jax.lax.scan(body, init, (per_example, given["loss_target"]))
    with _jax.named_scope("update"):
        delta_w, new_m, new_v = {}, {}, {}
        for n in TWIN_WEIGHTS:
            delta_w[n], new_m[n], new_v[n] = _adamw(weights[n], grad_w[n], given["m_" + n], given["v_" + n])
    return (loss, grad_x, *[grad_w[n] for n in TWIN_WEIGHTS], *[delta_w[n] for n in TWIN_WEIGHTS],
            *[new_m[n] for n in TWIN_WEIGHTS], *[new_v[n] for n in TWIN_WEIGHTS])
```

```python
import functools

import jax
import jax.numpy as jnp
from jax import lax
from jax.experimental import pallas as pl
from jax.experimental.pallas import tpu as pltpu

F32 = jnp.float32
BF16 = jnp.bfloat16

EPS = 1e-6
D_MODEL = 1024
CHUNK = 64
ROPE_THETA = 10000.0
RET_HEADS = 4
RET_DK = 256
RET_DV = 512
MLA_HEADS = 8
MLA_NOPE = 128
MLA_ROPE = 64
MLA_QKD = 192
MLA_VD = 128
MLA_HP = 256
MLA_Q_RANK = 384
MLA_KV_RANK = 256
MLA_IN = 704
MLA_IN_PAD = 768
D_FF = 4096
PLE_DIM = 256
N_CHIPS = 4

ADAM_LR = 0.001
ADAM_B1 = 0.9
ADAM_B2 = 0.999
ADAM_EPS = 1e-08
ADAM_WD = 0.01
ADAM_STEP = 10

VMEM_LIMIT = 56 * 1024 * 1024
PACK_W = 1024
NEG = -1e30


def _cparams(sem=None):
    return pltpu.CompilerParams(dimension_semantics=sem, vmem_limit_bytes=VMEM_LIMIT)


def _pick(dim, pref):
    if dim <= pref:
        return dim
    t = pref
    while dim % t:
        t //= 2
    return t


def _mm(a, b, *, name, ta=False, tb=False, bblk=False, oblk=None, outs=None, extras=(), epilogue=None,
        tm=1024, tn=512, tk=1024):
    if ta:
        K, M = a.shape
    else:
        M, K = a.shape
    if bblk:
        if tb:
            _, N, Kq = b.shape
            assert Kq * N_CHIPS == K
        else:
            _, Kb, Nq_b = b.shape
            N = Nq_b * N_CHIPS
            assert Kb == K
    else:
        N = b.shape[0] if tb else b.shape[1]
    tm = _pick(M, tm)
    tn = _pick(N if not (bblk and not tb) else b.shape[2], tn)
    if oblk is not None:
        tn = _pick(oblk, tn)
    tk = _pick(K if not (bblk and tb) else b.shape[2], tk)
    nk = K // tk
    grid = (M // tm, N // tn, nk)

    if ta:
        a_spec = pl.BlockSpec((tk, tm), lambda i, j, k: (k, i))
    else:
        a_spec = pl.BlockSpec((tm, tk), lambda i, j, k: (i, k))
    if bblk and tb:
        kpb = b.shape[2] // tk
        b_spec = pl.BlockSpec((None, tn, tk), lambda i, j, k: (k // kpb, j, k % kpb))
    elif bblk:
        npb = b.shape[2] // tn
        b_spec = pl.BlockSpec((None, tk, tn), lambda i, j, k: (j // npb, k, j % npb))
    elif tb:
        b_spec = pl.BlockSpec((tn, tk), lambda i, j, k: (j, k))
    else:
        b_spec = pl.BlockSpec((tk, tn), lambda i, j, k: (k, j))
    e_specs = [pl.BlockSpec((tm, tn), lambda i, j, k: (i, j)) for _ in extras]
    if outs is None:
        outs = [F32]
    if oblk is not None:
        opb = oblk // tn
        o_specs = [pl.BlockSpec((None, tm, tn), lambda i, j, k: (j // opb, i, j % opb)) for _ in outs]
        o_shapes = [jax.ShapeDtypeStruct((N_CHIPS, M, oblk), dt) for dt in outs]
    else:
        o_specs = [pl.BlockSpec((tm, tn), lambda i, j, k: (i, j)) for _ in outs]
        o_shapes = [jax.ShapeDtypeStruct((M, N), dt) for dt in outs]
    n_e, n_o = len(extras), len(outs)
    if ta:
        dims = (((0,), (0,)), ((), ()))
    elif tb:
        dims = (((1,), (1,)), ((), ()))
    else:
        dims = (((1,), (0,)), ((), ()))

    def body(a_ref, b_ref, *rest):
        e_refs, o_refs, acc = rest[:n_e], rest[n_e:n_e + n_o], rest[n_e + n_o]
        k = pl.program_id(2)
        part = lax.dot_general(a_ref[...].astype(BF16), b_ref[...].astype(BF16), dims,
                               preferred_element_type=F32)

        @pl.when(k == 0)
        def _():
            acc[...] = part

        @pl.when(k > 0)
        def _():
            acc[...] += part

        @pl.when(k == nk - 1)
        def _():
            res = acc[...]
            vals = (res,) if epilogue is None else epilogue(res, *[e[...] for e in e_refs])
            for o, v in zip(o_refs, vals):
                o[...] = v.astype(o.dtype)

    res = pl.pallas_call(
        body, name=name, grid=grid,
        in_specs=[a_spec, b_spec, *e_specs], out_specs=o_specs, out_shape=o_shapes,
        scratch_shapes=[pltpu.VMEM((tm, tn), F32)],
        compiler_params=_cparams(("parallel", "parallel", "arbitrary")),
    )(a, b, *extras)
    return res[0] if n_o == 1 else res


def _rows(fn, rows, fulls, outs, accs=(), *, name, tile=256):
    first = rows[0][0] if isinstance(rows[0], tuple) else rows[0]
    T = first.shape[0]
    tile = _pick(T, tile)
    in_specs, args = [], []
    for r in rows:
        if isinstance(r, tuple):
            arr, w, cb = r
            in_specs.append(pl.BlockSpec((tile, w), lambda i, cb=cb: (i, cb)))
        else:
            arr = r
            in_specs.append(pl.BlockSpec((tile, arr.shape[1]), lambda i: (i, 0)))
        args.append(arr)
    for f in fulls:
        in_specs.append(pl.BlockSpec(f.shape, lambda i, nd=f.ndim: (0,) * nd))
        args.append(f)
    out_specs = [pl.BlockSpec((tile, w), lambda i: (i, 0)) for w, _ in outs]
    out_specs += [pl.BlockSpec(s, lambda i: (0, 0)) for s, _ in accs]
    out_shape = [jax.ShapeDtypeStruct((T, w), dt) for w, dt in outs]
    out_shape += [jax.ShapeDtypeStruct(s, dt) for s, dt in accs]
    n_in, n_out = len(args), len(outs)

    def body(*refs):
        vals = fn(*[r[...] for r in refs[:n_in]])
        o_refs = refs[n_in:]
        for o, v in zip(o_refs[:n_out], vals[:n_out]):
            o[...] = v.astype(o.dtype)
        first_step = pl.program_id(0) == 0
        for o, v in zip(o_refs[n_out:], vals[n_out:]):
            @pl.when(first_step)
            def _(o=o, v=v):
                o[...] = v.astype(o.dtype)

            @pl.when(jnp.logical_not(first_step))
            def _(o=o, v=v):
                o[...] += v.astype(o.dtype)

    res = pl.pallas_call(
        body, name=name, grid=(T // tile,), in_specs=in_specs, out_specs=out_specs, out_shape=out_shape,
        compiler_params=_cparams(("arbitrary",)),
    )(*args)
    return res


def _rms(x, g):
    r = lax.rsqrt(jnp.mean(x * x, axis=-1, keepdims=True) + EPS)
    return (x * r) * g


def _rms_bwd(x, dy, g, n=None):
    n = x.shape[-1] if n is None else n
    r = lax.rsqrt(jnp.sum(x * x, axis=-1, keepdims=True) / n + EPS)
    xh = x * r
    dxh = dy * g
    dx = r * (dxh - xh * (jnp.sum(dxh * xh, axis=-1, keepdims=True) / n))
    return dx, dy * xh


def _colsum(v):
    return jnp.sum(v, axis=0, keepdims=True)


def _sigmoid(x):
    return 1.0 / (1.0 + jnp.exp(-x))


def _widen(v, width):
    reps = width // v.shape[1]
    return v if reps == 1 else jnp.concatenate([v] * reps, axis=-1)


def _norm_fwd(h, gain, name):
    return _rows(lambda x, g: (_rms(x, g),), [h], [gain], [(h.shape[1], BF16)], name=name)[0]


def _norm_bwd(h, dhn, gain, dres, name):
    def fn(x, dy, dr, g):
        dx, dg = _rms_bwd(x, dy, g)
        return dr + dx, _colsum(dg)
    d = h.shape[1]
    return _rows(fn, [h, dhn, dres], [gain], [(d, F32)], [((1, d), F32)], name=name)


def _ret_tables(T):
    inv = 1.0 / (ROPE_THETA ** (jnp.arange(0, RET_DK, 2, dtype=F32) / RET_DK))
    ang = jnp.arange(T, dtype=F32)[:, None] * inv[None, :]
    log_gamma = jnp.log(1.0 - 2.0 ** (-5.0 - jnp.arange(RET_HEADS, dtype=F32)))
    idx = jnp.arange(CHUNK, dtype=F32)
    intra = jnp.exp(log_gamma[:, None, None] * jnp.abs(idx[:, None] - idx[None, :]))
    qd = jnp.exp(log_gamma[:, None] * (idx + 1.0))[:, :, None]
    kd = jnp.exp(log_gamma[:, None] * (CHUNK - 1.0 - idx))[:, :, None]
    cd = jnp.exp(log_gamma * CHUNK)[:, None, None]
    return jnp.cos(ang), jnp.sin(ang), intra, qd, kd, cd


def _rope_half(x, c, s):
    x1, x2 = x[:, :RET_DK // 2], x[:, RET_DK // 2:]
    return jnp.concatenate([x1 * c - x2 * s, x2 * c + x1 * s], axis=-1)


def _rope_half_bwd(d, c, s):
    d1, d2 = d[:, :RET_DK // 2], d[:, RET_DK // 2:]
    return jnp.concatenate([d1 * c + d2 * s, d2 * c - d1 * s], axis=-1)


def _dot(a, b):
    return lax.dot_general(a, b, (((1,), (0,)), ((), ())), preferred_element_type=F32)


def _dot_nt(a, b):
    return lax.dot_general(a, b, (((1,), (1,)), ((), ())), preferred_element_type=F32)


def _dot_tn(a, b):
    return lax.dot_general(a, b, (((0,), (0,)), ((), ())), preferred_element_type=F32)


def _ret_specs(T, tb, rev):
    nj = T // tb
    jj = (lambda j: nj - 1 - j) if rev else (lambda j: j)
    kq = RET_HEADS
    vq = 2 * RET_HEADS * RET_DK // RET_DV
    return dict(
        q=pl.BlockSpec((tb, RET_DK), lambda h, j: (jj(j), h)),
        k=pl.BlockSpec((tb, RET_DK), lambda h, j: (jj(j), kq + h)),
        v=pl.BlockSpec((tb, RET_DV), lambda h, j: (jj(j), vq + h)),
        tab=pl.BlockSpec((tb, RET_DK // 2), lambda h, j: (jj(j), 0)),
        intra=pl.BlockSpec((None, CHUNK, CHUNK), lambda h, j: (h, 0, 0)),
        dec=pl.BlockSpec((None, CHUNK, 1), lambda h, j: (h, 0, 0)),
        cd=pl.BlockSpec((None, 1, 1), lambda h, j: (h, 0, 0)),
        o=pl.BlockSpec((tb, RET_DV), lambda h, j: (jj(j), h)),
        s=pl.BlockSpec((None, tb // CHUNK, RET_DK, RET_DV), lambda h, j: (h, jj(j), 0, 0)),
    )


def _ret_fwd(proj, tabs, name):
    T = proj.shape[0]
    cos, sin, intra, qd, kd, cd = tabs
    tb = _pick(T, 512)
    cps = tb // CHUNK
    sp = _ret_specs(T, tb, False)
    scale = RET_DK ** -0.5

    def body(q_ref, k_ref, v_ref, cos_ref, sin_ref, intra_ref, qd_ref, kd_ref, cd_ref, o_ref, s_ref, state):
        @pl.when(pl.program_id(1) == 0)
        def _():
            state[...] = jnp.zeros_like(state)

        for c in range(cps):
            rows = pl.ds(c * CHUNK, CHUNK)
            co, si = cos_ref[rows, :], sin_ref[rows, :]
            q = _rope_half(q_ref[rows, :], co, si)
            k = _rope_half(k_ref[rows, :], co, si) * scale
            vb = v_ref[rows, :].astype(BF16)
            st = state[...]
            sb = st.astype(BF16)
            s_ref[c] = sb
            sc = _dot_nt(q.astype(BF16), k.astype(BF16)) * intra_ref[...]
            inner = _dot(sc.astype(BF16), vb)
            cross = _dot((q * qd_ref[...]).astype(BF16), sb)
            o_ref[rows, :] = inner + cross
            state[...] = st * cd_ref[...] + _dot_tn((k * kd_ref[...]).astype(BF16), vb)

    return pl.pallas_call(
        body, name=name, grid=(RET_HEADS, T // tb),
        in_specs=[sp["q"], sp["k"], sp["v"], sp["tab"], sp["tab"], sp["intra"], sp["dec"], sp["dec"], sp["cd"]],
        out_specs=[sp["o"], sp["s"]],
        out_shape=[jax.ShapeDtypeStruct((T, RET_HEADS * RET_DV), F32),
                   jax.ShapeDtypeStruct((RET_HEADS, T // CHUNK, RET_DK, RET_DV), BF16)],
        scratch_shapes=[pltpu.VMEM((RET_DK, RET_DV), F32)],
        compiler_params=_cparams(("arbitrary", "arbitrary")),
    )(proj, proj, proj, cos, sin, intra, qd, kd, cd)


def _ret_bwd(proj, states, dout, tabs, name):
    T = proj.shape[0]
    cos, sin, intra, qd, kd, cd = tabs
    tb = _pick(T, 512)
    cps = tb // CHUNK
    sp = _ret_specs(T, tb, True)
    scale = RET_DK ** -0.5

    def body(q_ref, k_ref, v_ref, cos_ref, sin_ref, intra_ref, qd_ref, kd_ref, cd_ref, s_ref, do_ref,
             dq_ref, dk_ref, dv_ref, dstate):
        @pl.when(pl.program_id(1) == 0)
        def _():
            dstate[...] = jnp.zeros_like(dstate)

        for c in reversed(range(cps)):
            rows = pl.ds(c * CHUNK, CHUNK)
            co, si = cos_ref[rows, :], sin_ref[rows, :]
            q = _rope_half(q_ref[rows, :], co, si)
            k = _rope_half(k_ref[rows, :], co, si) * scale
            qb, kb = q.astype(BF16), k.astype(BF16)
            vb = v_ref[rows, :].astype(BF16)
            dob = do_ref[rows, :].astype(BF16)
            sb = s_ref[c]
            ia = intra_ref[...]
            pb = (_dot_nt(qb, kb) * ia).astype(BF16)
            dsn = dstate[...]
            dsb = dsn.astype(BF16)
            kdk = (k * kd_ref[...]).astype(BF16)
            qdq = (q * qd_ref[...]).astype(BF16)
            dv = _dot_tn(pb, dob) + _dot(kdk, dsb)
            dpb = (_dot_nt(dob, vb) * ia).astype(BF16)
            dq = _dot(dpb, kb) + _dot_nt(dob, sb) * qd_ref[...]
            dk = _dot_tn(dpb, qb) + _dot_nt(vb, dsb) * kd_ref[...]
            dstate[...] = dsn * cd_ref[...] + _dot_tn(qdq, dob)
            dq_ref[rows, :] = _rope_half_bwd(dq, co, si).astype(BF16)
            dk_ref[rows, :] = _rope_half_bwd(dk * scale, co, si).astype(BF16)
            dv_ref[rows, :] = dv.astype(BF16)

    return pl.pallas_call(
        body, name=name, grid=(RET_HEADS, T // tb),
        in_specs=[sp["q"], sp["k"], sp["v"], sp["tab"], sp["tab"], sp["intra"], sp["dec"], sp["dec"], sp["cd"],
                  sp["s"], sp["o"]],
        out_specs=[sp["q"], sp["q"], sp["o"]],
        out_shape=[jax.ShapeDtypeStruct((T, RET_HEADS * RET_DK), BF16),
                   jax.ShapeDtypeStruct((T, RET_HEADS * RET_DK), BF16),
                   jax.ShapeDtypeStruct((T, RET_HEADS * RET_DV), BF16)],
        scratch_shapes=[pltpu.VMEM((RET_DK, RET_DV), F32)],
        compiler_params=_cparams(("arbitrary", "arbitrary")),
    )(proj, proj, proj, cos, sin, intra, qd, kd, cd, states, dout)


def _ret_gate(out, proj, gn, name):
    def fn(o, g, *gains):
        parts = [_rms(o[:, h * RET_DV:(h + 1) * RET_DV], gains[h]) for h in range(RET_HEADS)]
        return (g * _sigmoid(g) * jnp.concatenate(parts, axis=-1),)
    w = RET_HEADS * RET_DV
    return _rows(fn, [out, (proj, w, 2)], [gn[h:h + 1] for h in range(RET_HEADS)], [(w, BF16)], name=name)[0]


def _ret_gate_bwd(out, proj, gn, dy, name):
    def fn(o, g, d, *gains):
        sg = _sigmoid(g)
        silu = g * sg
        dsilu = sg * (1.0 + g * (1.0 - sg))
        dos, dgs = [], []
        row = lax.broadcasted_iota(jnp.int32, (RET_HEADS, RET_DV), 0)
        dgn = jnp.zeros((RET_HEADS, RET_DV), F32)
        for h in range(RET_HEADS):
            sl = slice(h * RET_DV, (h + 1) * RET_DV)
            oh = o[:, sl]
            dgs.append(d[:, sl] * _rms(oh, gains[h]) * dsilu[:, sl])
            dx, dg = _rms_bwd(oh, d[:, sl] * silu[:, sl], gains[h])
            dos.append(dx)
            dgn = dgn + jnp.where(row == h, _colsum(dg), 0.0)
        return jnp.concatenate(dos, axis=-1), jnp.concatenate(dgs, axis=-1), dgn
    w = RET_HEADS * RET_DV
    return _rows(fn, [out, (proj, w, 2), dy], [gn[h:h + 1] for h in range(RET_HEADS)], [(w, BF16), (w, BF16)],
                 [((RET_HEADS, RET_DV), F32)], name=name, tile=128)


def _mla_tables(T):
    inv = 1.0 / (ROPE_THETA ** (jnp.arange(0, MLA_ROPE, 2, dtype=F32) / MLA_ROPE))
    ang = jnp.arange(T, dtype=F32)[:, None] * inv[None, :]
    c, s = jnp.cos(ang), jnp.sin(ang)
    z32, z64 = jnp.zeros((T, 32), F32), jnp.zeros((T, 64), F32)
    cos_t = jnp.concatenate([c, c, z64], axis=1)
    sin_a = jnp.concatenate([-s, z32, z64], axis=1)
    sin_b = jnp.concatenate([z32, s, z64], axis=1)
    return cos_t, sin_a, sin_b


def _rope_blk(x, ct, sa, sb):
    return x * ct + pltpu.roll(x, 96, 1) * sa + pltpu.roll(x, 32, 1) * sb


def _rope_blk_bwd(d, ct, sa, sb):
    return d * ct + pltpu.roll(d * sa, 32, 1) + pltpu.roll(d * sb, 96, 1)


def _head_norm(x, gain):
    r = lax.rsqrt(jnp.sum(x * x, axis=-1, keepdims=True) / MLA_QKD + EPS)
    return (x * r) * gain


def _mla_prep(q, kv, proj, gq, gk, tabs, name):
    def fn(qv, kvv, kr, ct, sa, sb, gqv, gkv):
        qs, ks, vs = [], [], []
        for h in range(MLA_HEADS):
            b = h * MLA_HP
            y = _head_norm(qv[:, b:b + MLA_HP], gqv)
            qs += [y[:, :128], _rope_blk(y[:, 128:], ct, sa, sb)]
            y = _head_norm(jnp.concatenate([kvv[:, b:b + 128], kr], axis=-1), gkv)
            ks += [y[:, :128], _rope_blk(y[:, 128:], ct, sa, sb)]
            vs.append(kvv[:, b + 128:b + 256])
        return jnp.concatenate(qs, axis=-1), jnp.concatenate(ks, axis=-1), jnp.concatenate(vs, axis=-1)
    w = MLA_HEADS * MLA_HP
    return _rows(fn, [q, kv, (proj, 128, 5), *tabs], [gq, gk],
                 [(w, BF16), (w, BF16), (MLA_HEADS * MLA_VD, BF16)], name=name, tile=128)


def _mla_prep_bwd(q, kv, proj, gq, gk, tabs, dqf, dkf, dvf, name):
    def fn(qv, kvv, kr, ct, sa, sb, dqv, dkv, dvv, gqv, gkv):
        dqs, dkvs = [], []
        dkr = jnp.zeros_like(kr)
        dgq = jnp.zeros((1, MLA_HP), F32)
        dgk = jnp.zeros((1, MLA_HP), F32)
        for h in range(MLA_HEADS):
            b = h * MLA_HP
            dy = jnp.concatenate([dqv[:, b:b + 128], _rope_blk_bwd(dqv[:, b + 128:b + 256], ct, sa, sb)], axis=-1)
            dx, dg = _rms_bwd(qv[:, b:b + MLA_HP], dy, gqv, MLA_QKD)
            dqs.append(dx)
            dgq = dgq + _colsum(dg)
            dy = jnp.concatenate([dkv[:, b:b + 128], _rope_blk_bwd(dkv[:, b + 128:b + 256], ct, sa, sb)], axis=-1)
            dx, dg = _rms_bwd(jnp.concatenate([kvv[:, b:b + 128], kr], axis=-1), dy, gkv, MLA_QKD)
            dkvs += [dx[:, :128], dvv[:, h * MLA_VD:(h + 1) * MLA_VD]]
            dkr = dkr + dx[:, 128:]
            dgk = dgk + _colsum(dg)
        return jnp.concatenate(dqs, axis=-1), jnp.concatenate(dkvs, axis=-1), dkr, dgq, dgk
    w = MLA_HEADS * MLA_HP
    return _rows(fn, [q, kv, (proj, 128, 5), *tabs, dqf, dkf, dvf], [gq, gk],
                 [(w, BF16), (w, BF16), (128, F32)], [((1, MLA_HP), F32), ((1, MLA_HP), F32)], name=name, tile=128)


def _chunk_mask(qi, ki, tq, tk):
    shift = CHUNK.bit_length() - 1
    rq = lax.shift_right_arithmetic(qi * tq + lax.broadcasted_iota(jnp.int32, (tq, tk), 0), shift)
    ck = lax.shift_right_arithmetic(ki * tk + lax.broadcasted_iota(jnp.int32, (tq, tk), 1), shift)
    return ck <= rq


def _flash_fwd(qf, kf, vf, name):
    T = qf.shape[0]
    t = _pick(T, 256)
    n = T // t
    scale = MLA_QKD ** -0.5

    def body(q_ref, k_ref, v_ref, o_ref, lse_ref, m_s, l_s, acc):
        qi, ki = pl.program_id(1), pl.program_id(2)

        @pl.when(ki == 0)
        def _():
            m_s[...] = jnp.full_like(m_s, NEG)
            l_s[...] = jnp.zeros_like(l_s)
            acc[...] = jnp.zeros_like(acc)

        @pl.when(ki <= qi)
        def _():
            s = _dot_nt(q_ref[...], k_ref[...]) * scale
            s = jnp.where(_chunk_mask(qi, ki, t, t), s, NEG)
            m_prev = m_s[...]
            m_new = jnp.maximum(m_prev, jnp.max(s, axis=-1, keepdims=True))
            alpha = jnp.exp(m_prev - m_new)
            p = jnp.exp(s - _widen(m_new, t))
            l_s[...] = alpha * l_s[...] + jnp.sum(p, axis=-1, keepdims=True)
            acc[...] = acc[...] * alpha + _dot(p.astype(BF16), v_ref[...])
            m_s[...] = m_new

        @pl.when(ki == qi)
        def _():
            o_ref[...] = acc[...] / l_s[...]
            lse_ref[...] = m_s[...] + jnp.log(l_s[...])

    kmap = lambda h, i, j: (jnp.minimum(j, i), h)
    return pl.pallas_call(
        body, name=name, grid=(MLA_HEADS, n, n),
        in_specs=[pl.BlockSpec((t, MLA_HP), lambda h, i, j: (i, h)),
                  pl.BlockSpec((t, MLA_HP), kmap), pl.BlockSpec((t, MLA_VD), kmap)],
        out_specs=[pl.BlockSpec((t, MLA_VD), lambda h, i, j: (i, h)),
                   pl.BlockSpec((t, MLA_VD), lambda h, i, j: (i, h))],
        out_shape=[jax.ShapeDtypeStruct((T, MLA_HEADS * MLA_VD), F32),
                   jax.ShapeDtypeStruct((T, MLA_HEADS * MLA_VD), F32)],
        scratch_shapes=[pltpu.VMEM((t, MLA_VD), F32), pltpu.VMEM((t, MLA_VD), F32), pltpu.VMEM((t, MLA_VD), F32)],
        compiler_params=_cparams(("parallel", "parallel", "arbitrary")),
    )(qf, kf, vf)


def _flash_p_ds(q_ref, k_ref, v_ref, o_ref, do_ref, lse_ref, qi, ki, t, scale):
    s = _dot_nt(q_ref[...], k_ref[...]) * scale
    s = jnp.where(_chunk_mask(qi, ki, t, t), s, NEG)
    p = jnp.exp(s - _widen(lse_ref[...], t))
    do = do_ref[...]
    dob = do.astype(BF16)
    delta = jnp.sum(do * o_ref[...], axis=-1, keepdims=True)
    dp = _dot_nt(dob, v_ref[...])
    ds = p * (dp - delta) * scale
    return p.astype(BF16), ds.astype(BF16), dob


def _flash_bwd_q(qf, kf, vf, o, do, lse, name):
    T = qf.shape[0]
    t = _pick(T, 256)
    n = T // t
    scale = MLA_QKD ** -0.5

    def body(q_ref, k_ref, v_ref, o_ref, do_ref, lse_ref, dq_ref, acc):
        qi, ki = pl.program_id(1), pl.program_id(2)

        @pl.when(ki == 0)
        def _():
            acc[...] = jnp.zeros_like(acc)

        @pl.when(ki <= qi)
        def _():
            _, ds, _ = _flash_p_ds(q_ref, k_ref, v_ref, o_ref, do_ref, lse_ref, qi, ki, t, scale)
            acc[...] += _dot(ds, k_ref[...])

        @pl.when(ki == qi)
        def _():
            dq_ref[...] = acc[...]

    qmap = lambda h, i, j: (i, h)
    kmap = lambda h, i, j: (jnp.minimum(j, i), h)
    return pl.pallas_call(
        body, name=name, grid=(MLA_HEADS, n, n),
        in_specs=[pl.BlockSpec((t, MLA_HP), qmap), pl.BlockSpec((t, MLA_HP), kmap), pl.BlockSpec((t, MLA_VD), kmap),
                  pl.BlockSpec((t, MLA_VD), qmap), pl.BlockSpec((t, MLA_VD), qmap), pl.BlockSpec((t, MLA_VD), qmap)],
        out_specs=pl.BlockSpec((t, MLA_HP), qmap),
        out_shape=jax.ShapeDtypeStruct((T, MLA_HEADS * MLA_HP), F32),
        scratch_shapes=[pltpu.VMEM((t, MLA_HP), F32)],
        compiler_params=_cparams(("parallel", "parallel", "arbitrary")),
    )(qf, kf, vf, o, do, lse)


def _flash_bwd_kv(qf, kf, vf, o, do, lse, name):
    T = qf.shape[0]
    t = _pick(T, 256)
    n = T // t
    scale = MLA_QKD ** -0.5

    def body(q_ref, k_ref, v_ref, o_ref, do_ref, lse_ref, dk_ref, dv_ref, dk_acc, dv_acc):
        ki, qi = pl.program_id(1), pl.program_id(2)

        @pl.when(qi == 0)
        def _():
            dk_acc[...] = jnp.zeros_like(dk_acc)
            dv_acc[...] = jnp.zeros_like(dv_acc)

        @pl.when(qi >= ki)
        def _():
            p, ds, dob = _flash_p_ds(q_ref, k_ref, v_ref, o_ref, do_ref, lse_ref, qi, ki, t, scale)
            dk_acc[...] += _dot_tn(ds, q_ref[...])
            dv_acc[...] += _dot_tn(p, dob)

        @pl.when(qi == n - 1)
        def _():
            dk_ref[...] = dk_acc[...]
            dv_ref[...] = dv_acc[...]

    qmap = lambda h, j, i: (jnp.maximum(i, j), h)
    kmap = lambda h, j, i: (j, h)
    return pl.pallas_call(
        body, name=name, grid=(MLA_HEADS, n, n),
        in_specs=[pl.BlockSpec((t, MLA_HP), qmap), pl.BlockSpec((t, MLA_HP), kmap), pl.BlockSpec((t, MLA_VD), kmap),
                  pl.BlockSpec((t, MLA_VD), qmap), pl.BlockSpec((t, MLA_VD), qmap), pl.BlockSpec((t, MLA_VD), qmap)],
        out_specs=[pl.BlockSpec((t, MLA_HP), kmap), pl.BlockSpec((t, MLA_VD), kmap)],
        out_shape=[jax.ShapeDtypeStruct((T, MLA_HEADS * MLA_HP), F32),
                   jax.ShapeDtypeStruct((T, MLA_HEADS * MLA_VD), F32)],
        scratch_shapes=[pltpu.VMEM((t, MLA_HP), F32), pltpu.VMEM((t, MLA_VD), F32)],
        compiler_params=_cparams(("parallel", "parallel", "arbitrary")),
    )(qf, kf, vf, o, do, lse)


MESH = pl.DeviceIdType.MESH
ANY = pl.BlockSpec(memory_space=pl.ANY)
_CHIP_FLIPS = ((1, 0), (0, 1), (1, 1))


def _place():
    return lax.axis_index("x"), lax.axis_index("y"), lax.axis_index("c")


def _other_chip(x, y, k):
    fx, fy = _CHIP_FLIPS[k]
    return ((1 - x) if fx else x), ((1 - y) if fy else y)


def _remote(src, dst, send_sems, recv_sems, k, to):
    return pltpu.make_async_remote_copy(src_ref=src, dst_ref=dst, send_sem=send_sems.at[k], recv_sem=recv_sems.at[k],
                                        device_id=to, device_id_type=MESH)


def _gather_weights(wsh):
    R, W = wsh.shape
    H = R // 2

    def body(w_ref, out_ref, send_sems, recv_sems, local_sem):
        x, y, c = _place()
        j = 2 * x + y
        sibling = (x, y, 1 - c)
        half = pl.ds(pl.multiple_of(c * H, 16), H)
        other = pl.ds(pl.multiple_of((1 - c) * H, 16), H)
        mine = pltpu.make_async_copy(w_ref, out_ref.at[j], local_sem)
        mine.start()
        sent = []
        for k in range(3):
            px, py = _other_chip(x, y, k)
            cp = _remote(w_ref.at[half], out_ref.at[j, half], send_sems, recv_sems, k, (px, py, c))
            cp.start()
            sent.append(cp)
        for k in range(3):
            px, py = _other_chip(x, y, k)
            blk = out_ref.at[2 * px + py, half]
            _remote(blk, blk, send_sems, recv_sems, k, (px, py, c)).wait_recv()
            cp = _remote(blk, blk, send_sems, recv_sems, 3 + k, sibling)
            cp.start()
            sent.append(cp)
        for k in range(3):
            px, py = _other_chip(x, y, k)
            blk = out_ref.at[2 * px + py, other]
            _remote(blk, blk, send_sems, recv_sems, 3 + k, sibling).wait_recv()
        for cp in sent:
            cp.wait_send()
        mine.wait()

    return pl.pallas_call(
        body, name="gather_weights", in_specs=[ANY], out_specs=ANY,
        out_shape=jax.ShapeDtypeStruct((N_CHIPS, R, W), wsh.dtype),
        scratch_shapes=[pltpu.SemaphoreType.DMA((6,)), pltpu.SemaphoreType.DMA((6,)), pltpu.SemaphoreType.DMA],
    )(wsh)


def _swap_halves(g):
    _, R, W = g.shape
    H = R // 2

    def body(g_ref, recv_ref, own_ref, send_sems, recv_sems, local_sem):
        x, y, c = _place()
        half = pl.ds(pl.multiple_of(c * H, 16), H)
        other = pl.ds(pl.multiple_of((1 - c) * H, 16), H)
        mine = pltpu.make_async_copy(g_ref.at[:, half], own_ref, local_sem)
        mine.start()
        cp = _remote(g_ref.at[:, other], recv_ref, send_sems, recv_sems, 0, (x, y, 1 - c))
        cp.start()
        cp.wait()
        mine.wait()

    out = jax.ShapeDtypeStruct((N_CHIPS, H, W), g.dtype)
    return pl.pallas_call(
        body, name="grad_swap_halves", in_specs=[ANY], out_specs=[ANY, ANY], out_shape=[out, out],
        scratch_shapes=[pltpu.SemaphoreType.DMA((1,)), pltpu.SemaphoreType.DMA((1,)), pltpu.SemaphoreType.DMA],
    )(g)


def _scatter_chips(a16, a32):
    _, H, W = a16.shape

    def body(a16_ref, a32_ref, recv_ref, own_ref, send_sems, recv_sems, local_sem):
        x, y, c = _place()
        j = 2 * x + y
        mine = pltpu.make_async_copy(a32_ref.at[j], own_ref, local_sem)
        mine.start()
        sent = []
        for k in range(3):
            px, py = _other_chip(x, y, k)
            pj = 2 * px + py
            cp = _remote(a16_ref.at[pj], recv_ref.at[(j - pj + 4) % 4 - 1], send_sems, recv_sems, k, (px, py, c))
            cp.start()
            sent.append(cp)
        for k in range(3):
            px, py = _other_chip(x, y, k)
            pj = 2 * px + py
            slot = recv_ref.at[(pj - j + 4) % 4 - 1]
            _remote(slot, slot, send_sems, recv_sems, k, (px, py, c)).wait_recv()
        for cp in sent:
            cp.wait_send()
        mine.wait()

    return pl.pallas_call(
        body, name="grad_scatter_chips", in_specs=[ANY, ANY], out_specs=[ANY, ANY],
        out_shape=[jax.ShapeDtypeStruct((3, H, W), a16.dtype), jax.ShapeDtypeStruct((H, W), a32.dtype)],
        scratch_shapes=[pltpu.SemaphoreType.DMA((3,)), pltpu.SemaphoreType.DMA((3,)), pltpu.SemaphoreType.DMA],
    )(a16, a32)


def _share_halves(r):
    H, W = r.shape

    def body(r_ref, out_ref, send_sems, recv_sems, local_sem):
        x, y, c = _place()
        half = pl.ds(pl.multiple_of(c * H, 8), H)
        mine = pltpu.make_async_copy(r_ref, out_ref.at[half], local_sem)
        mine.start()
        cp = _remote(r_ref, out_ref.at[half], send_sems, recv_sems, 0, (x, y, 1 - c))
        cp.start()
        cp.wait()
        mine.wait()

    return pl.pallas_call(
        body, name="grad_share_halves", in_specs=[ANY], out_specs=ANY,
        out_shape=jax.ShapeDtypeStruct((2 * H, W), r.dtype),
        scratch_shapes=[pltpu.SemaphoreType.DMA((1,)), pltpu.SemaphoreType.DMA((1,)), pltpu.SemaphoreType.DMA],
    )(r)


def _allsum_small(v, name):
    R, W = v.shape
    n_dev = 8
    vm = pl.BlockSpec(memory_space=pltpu.VMEM)

    def body(v_ref, out_ref, buf, send_sems, recv_sems):
        x, y, c = _place()
        me = 4 * x + 2 * y + c
        buf[me] = v_ref[...]
        sent = []
        for k in range(1, n_dev):
            peer = ((1 - x) if k & 4 else x, (1 - y) if k & 2 else y, (1 - c) if k & 1 else c)
            cp = _remote(v_ref, buf.at[me], send_sems, recv_sems, k - 1, peer)
            cp.start()
            sent.append(cp)
        for cp in sent:
            cp.wait_recv()
        for cp in sent:
            cp.wait_send()
        acc = buf[0]
        for q in range(1, n_dev):
            acc = acc + buf[q]
        out_ref[...] = acc

    return pl.pallas_call(
        body, name=name, in_specs=[vm], out_specs=vm, out_shape=jax.ShapeDtypeStruct((R, W), v.dtype),
        scratch_shapes=[pltpu.VMEM((n_dev, R, W), v.dtype), pltpu.SemaphoreType.DMA((n_dev - 1,)),
                        pltpu.SemaphoreType.DMA((n_dev - 1,))],
    )(v)


def _reduce_grads(g16):
    _, R, W = g16.shape
    H = R // 2
    recv, own = _swap_halves(g16)
    a32, a16 = _rows(lambda a, b: (a.astype(F32) + b.astype(F32),) * 2,
                     [own.reshape(N_CHIPS * H, W), recv.reshape(N_CHIPS * H, W)], [],
                     [(W, F32), (W, BF16)], name="grad_pair_sum", tile=288)
    got, mine = _scatter_chips(a16.reshape(N_CHIPS, H, W), a32.reshape(N_CHIPS, H, W))
    red = _rows(lambda m, a, b, c: (((m + a.astype(F32)) + b.astype(F32)) + c.astype(F32),),
                [mine, got[0], got[1], got[2]], [], [(W, F32)], name="grad_chip_sum", tile=288)[0]
    return _share_halves(red)


def _adamw(w, g, m, v, name):
    shape = w.shape
    cols = shape[-1]

    def fn(wv, gv, mv, vv):
        m2 = ADAM_B1 * mv + (1.0 - ADAM_B1) * gv
        v2 = ADAM_B2 * vv + (1.0 - ADAM_B2) * jnp.square(gv)
        m_hat = m2 / (1.0 - ADAM_B1 ** ADAM_STEP)
        v_hat = v2 / (1.0 - ADAM_B2 ** ADAM_STEP)
        return -ADAM_LR * (m_hat / (jnp.sqrt(v_hat) + ADAM_EPS) + ADAM_WD * wv), m2, v2

    w2, m2, v2 = (t.reshape(-1, cols) for t in (w, m, v))
    rows = w2.shape[0]
    tile = 256 if rows % 8 == 0 else rows
    res = _rows(fn, [w2, g.reshape(rows, cols), m2, v2], [], [(cols, F32)] * 3, name=name, tile=tile)
    return tuple(t.reshape(shape) for t in res)


def _add_res(acc, r):
    return (r + acc,)


def _tail_fwd(h1, p16, W, i, tag):
    hn2 = _norm_fwd(h1, W["mlp_norm"][i:i + 1], f"{tag}_mlp_norm")
    z, a = _mm(hn2, W["mlp_w1"][i], bblk=True, outs=[F32, BF16], name=f"{tag}_mlp_w1",
               epilogue=lambda acc: (acc, jnp.square(jnp.maximum(acc, 0.0))))
    h2 = _mm(a, W["mlp_w2"][i], extras=[h1], epilogue=_add_res, name=f"{tag}_mlp_w2")
    hn3 = _norm_fwd(h2, W["ple_norm"][i:i + 1], f"{tag}_ple_norm")
    gl = _mm(hn3, W["ple_gate_w"][i], name=f"{tag}_ple_gate")
    h3, pp = _mm(p16[i], W["ple_proj_w"][i], bblk=True, extras=[gl, h2], outs=[F32, F32], name=f"{tag}_ple_proj",
                 epilogue=lambda acc, g, h: (h + _sigmoid(g) * acc, acc))
    return h3, (h1, hn2, z, a, h2, hn3, gl, pp)


def _tail_bwd(dh3, saved, p16, W, i, tag):
    h1, hn2, z, a, h2, hn3, gl, pp = saved

    def gate_bwd(d, g, ppv):
        gate = _sigmoid(g)
        return d * gate, d * ppv * gate * (1.0 - gate)

    dpp, dgl = _rows(gate_bwd, [dh3, gl, pp], [], [(D_MODEL, BF16), (D_MODEL, BF16)], name=f"{tag}_ple_gate_bwd")
    d_proj = _mm(p16[i], dpp, ta=True, oblk=PLE_DIM, name=f"{tag}_d_ple_proj")
    d_gate = _mm(hn3, dgl, ta=True, name=f"{tag}_d_ple_gate")
    dhn3 = _mm(dgl, W["ple_gate_w"][i], tb=True, name=f"{tag}_ple_gate_dx")
    dh2, d_ple_norm = _norm_bwd(h2, dhn3, W["ple_norm"][i:i + 1], dh3, f"{tag}_ple_norm_bwd")
    d_w2 = _mm(a, dh2, ta=True, name=f"{tag}_d_mlp_w2")
    dz = _mm(dh2, W["mlp_w2"][i], tb=True, extras=[z], outs=[BF16], name=f"{tag}_mlp_w2_dx",
             epilogue=lambda acc, zv: (acc * (2.0 * jnp.maximum(zv, 0.0)),))
    d_w1 = _mm(hn2, dz, ta=True, oblk=D_MODEL, name=f"{tag}_d_mlp_w1")
    dhn2 = _mm(dz, W["mlp_w1"][i], tb=True, bblk=True, name=f"{tag}_mlp_w1_dx")
    dh1, d_mlp_norm = _norm_bwd(h1, dhn2, W["mlp_norm"][i:i + 1], dh2, f"{tag}_mlp_norm_bwd")
    return dh1, dict(mlp_norm=d_mlp_norm, mlp_w1=d_w1, mlp_w2=d_w2, ple_norm=d_ple_norm,
                     ple_gate_w=d_gate, ple_proj_w=d_proj)


def _ret_layer_fwd(h0, W, tabs):
    hn = _norm_fwd(h0, W["mix_norm"][0:1], "ret_mix_norm")
    proj = _mm(hn, W["ret_w_in"], bblk=True, name="ret_w_in")
    out, states = _ret_fwd(proj, tabs, "ret_scan")
    y = _ret_gate(out, proj, W["ret_gn"], "ret_gate")
    h1 = _mm(y, W["ret_w_out"], extras=[h0], epilogue=_add_res, name="ret_w_out")
    return h1, (h0, hn, proj, out, states, y)


def _ret_layer_bwd(dh1, saved, W, tabs):
    h0, hn, proj, out, states, y = saved
    d_w_out = _mm(y, dh1, ta=True, name="d_ret_w_out")
    dy = _mm(dh1, W["ret_w_out"], tb=True, name="ret_w_out_dx")
    dout, dg, d_gn = _ret_gate_bwd(out, proj, W["ret_gn"], dy, "ret_gate_bwd")
    dq, dk, dv = _ret_bwd(proj, states, dout, tabs, "ret_scan_bwd")
    dproj = jnp.concatenate([dq, dk, dv, dg], axis=1)
    d_w_in = _mm(hn, dproj, ta=True, oblk=proj.shape[1] // N_CHIPS, name="d_ret_w_in")
    dhn = _mm(dproj, W["ret_w_in"], tb=True, bblk=True, name="ret_w_in_dx")
    dh0, d_mix = _norm_bwd(h0, dhn, W["mix_norm"][0:1], dh1, "ret_mix_norm_bwd")
    return dh0, dict(mix_norm=d_mix, ret_w_in=d_w_in, ret_gn=d_gn, ret_w_out=d_w_out)


def _mla_layer_fwd(h0, W, tabs):
    hn = _norm_fwd(h0, W["mix_norm"][1:2], "mla_mix_norm")
    proj = _mm(hn, W["mla_w_in"], name="mla_w_in")

    def low_rank_norm(pv, gq, gkv):
        return _rms(pv[:, :MLA_Q_RANK], gq), _rms(pv[:, MLA_Q_RANK:MLA_Q_RANK + MLA_KV_RANK], gkv)

    cqn, ckvn = _rows(low_rank_norm, [proj], [W["mla_q_a_norm"], W["mla_kv_a_norm"]],
                      [(MLA_Q_RANK, BF16), (MLA_KV_RANK, BF16)], name="mla_low_rank_norm")
    q = _mm(cqn, W["mla_w_uq"], name="mla_w_uq")
    kv = _mm(ckvn, W["mla_w_ukv"], bblk=True, name="mla_w_ukv")
    qf, kf, vf = _mla_prep(q, kv, proj, W["mla_q_norm"], W["mla_k_norm"], tabs, "mla_prep")
    o, lse = _flash_fwd(qf, kf, vf, "mla_flash")
    h1 = _mm(o, W["mla_w_out"], extras=[h0], epilogue=_add_res, name="mla_w_out")
    return h1, (h0, hn, proj, cqn, ckvn, q, kv, qf, kf, vf, o, lse)


def _mla_layer_bwd(dh1, saved, W, tabs):
    h0, hn, proj, cqn, ckvn, q, kv, qf, kf, vf, o, lse = saved
    d_w_out = _mm(o, dh1, ta=True, name="d_mla_w_out")
    do = _mm(dh1, W["mla_w_out"], tb=True, name="mla_w_out_dx")
    dqf = _flash_bwd_q(qf, kf, vf, o, do, lse, "mla_flash_bwd_q")
    dkf, dvf = _flash_bwd_kv(qf, kf, vf, o, do, lse, "mla_flash_bwd_kv")
    dq, dkv, dkr, d_gq, d_gk = _mla_prep_bwd(q, kv, proj, W["mla_q_norm"], W["mla_k_norm"], tabs, dqf, dkf, dvf,
                                             "mla_prep_bwd")
    d_w_uq = _mm(cqn, dq, ta=True, name="d_mla_w_uq")
    dcqn = _mm(dq, W["mla_w_uq"], tb=True, name="mla_w_uq_dx")
    d_w_ukv = _mm(ckvn, dkv, ta=True, oblk=kv.shape[1] // N_CHIPS, name="d_mla_w_ukv")
    dckvn = _mm(dkv, W["mla_w_ukv"], tb=True, bblk=True, name="mla_w_ukv_dx")

    def low_rank_bwd(pv, dcq, dckv, dkr_v, gq, gkv):
        dxq, dgq = _rms_bwd(pv[:, :MLA_Q_RANK], dcq, gq)
        dxkv, dgkv = _rms_bwd(pv[:, MLA_Q_RANK:MLA_Q_RANK + MLA_KV_RANK], dckv, gkv)
        return jnp.concatenate([dxq, dxkv, dkr_v], axis=-1), _colsum(dgq), _colsum(dgkv)

    dproj, d_gqa, d_gkva = _rows(low_rank_bwd, [proj, dcqn, dckvn, dkr], [W["mla_q_a_norm"], W["mla_kv_a_norm"]],
                                 [(MLA_IN_PAD, BF16)], [((1, MLA_Q_RANK), F32), ((1, MLA_KV_RANK), F32)],
                                 name="mla_low_rank_norm_bwd")
    d_w_in = _mm(hn, dproj, ta=True, name="d_mla_w_in")
    dhn = _mm(dproj, W["mla_w_in"], tb=True, name="mla_w_in_dx")
    dh0, d_mix = _norm_bwd(h0, dhn, W["mix_norm"][1:2], dh1, "mla_mix_norm_bwd")
    return dh0, dict(mix_norm=d_mix, mla_w_in=d_w_in, mla_q_a_norm=d_gqa, mla_kv_a_norm=d_gkva, mla_w_uq=d_w_uq,
                     mla_w_ukv=d_w_ukv, mla_q_norm=d_gq, mla_k_norm=d_gk, mla_w_out=d_w_out)


def _local_step(x, p16, target, W):
    T = x.shape[0]
    ret_tabs, mla_tabs = _ret_tables(T), _mla_tables(T)
    h1, s_ret = _ret_layer_fwd(x, W, ret_tabs)
    h3, s_tail0 = _tail_fwd(h1, p16, W, 0, "l0")
    h4, s_mla = _mla_layer_fwd(h3, W, mla_tabs)
    y, s_tail1 = _tail_fwd(h4, p16, W, 1, "l1")

    def loss_head(yv, tv):
        e = yv - tv
        return e * (1.0 / D_MODEL), jnp.full((1, 128), 0.5 / D_MODEL, F32) * jnp.sum(e * e)

    dy, loss = _rows(loss_head, [y, target], [], [(D_MODEL, F32)], [((1, 128), F32)], name="loss_head")
    dh4, g_tail1 = _tail_bwd(dy, s_tail1, p16, W, 1, "l1")
    dh3, g_mla = _mla_layer_bwd(dh4, s_mla, W, mla_tabs)
    dh1, g_tail0 = _tail_bwd(dh3, s_tail0, p16, W, 0, "l0")
    dx, g_ret = _ret_layer_bwd(dh1, s_ret, W, ret_tabs)
    return loss, dx, g_ret, g_tail0, g_mla, g_tail1


_BIG = ("ret_w_in", "ret_w_out", "mla_w_in", "mla_w_uq", "mla_w_ukv", "mla_w_out", "mlp_w1", "mlp_w2",
        "ple_gate_w", "ple_proj_w")
_SMALL = ("mix_norm", "ret_gn", "mla_q_a_norm", "mla_kv_a_norm", "mla_q_norm", "mla_k_norm", "mlp_norm", "ple_norm")
_ORDER = ("mix_norm", "ret_w_in", "ret_gn", "ret_w_out", "mla_w_in", "mla_q_a_norm", "mla_kv_a_norm", "mla_w_uq",
          "mla_w_ukv", "mla_q_norm", "mla_k_norm", "mla_w_out", "mlp_norm", "mlp_w1", "mlp_w2", "ple_norm",
          "ple_gate_w", "ple_proj_w")
SMALL_ROWS = 16


def _pack_rows(shards):
    return [shards[n].size // PACK_W for n in _BIG]


def _split_rows(buf, shards):
    out, off = {}, 0
    for n, r in zip(_BIG, _pack_rows(shards)):
        out[n] = buf[..., off:off + r, :]
        off += r
    return out


def _full_weights(gath, shards, small):
    b = _split_rows(gath, shards)
    n = N_CHIPS
    W = {}
    W["ret_w_in"] = b["ret_w_in"].reshape(n, D_MODEL, -1)
    W["ret_w_out"] = b["ret_w_out"].reshape(-1, D_MODEL)
    W["mla_w_in"] = jnp.pad(b["mla_w_in"].reshape(D_MODEL, MLA_IN), ((0, 0), (0, MLA_IN_PAD - MLA_IN)))
    uq = b["mla_w_uq"].reshape(n, MLA_Q_RANK, MLA_HEADS // n, MLA_QKD).transpose(1, 0, 2, 3)
    W["mla_w_uq"] = jnp.pad(uq, ((0, 0), (0, 0), (0, 0), (0, MLA_HP - MLA_QKD))).reshape(MLA_Q_RANK, MLA_HEADS * MLA_HP)
    W["mla_w_ukv"] = b["mla_w_ukv"].reshape(n, MLA_KV_RANK, -1)
    W["mla_w_out"] = b["mla_w_out"].reshape(-1, D_MODEL)
    w1 = b["mlp_w1"].reshape(n, 2, D_MODEL, -1)
    W["mlp_w1"] = [w1[:, i] for i in range(2)]
    w2 = b["mlp_w2"].reshape(n, 2, -1, D_MODEL)
    W["mlp_w2"] = [w2[:, i].reshape(-1, D_MODEL) for i in range(2)]
    wg = b["ple_gate_w"].reshape(n, 2, -1, D_MODEL)
    W["ple_gate_w"] = [wg[:, i].reshape(-1, D_MODEL) for i in range(2)]
    wp = b["ple_proj_w"].reshape(n, 2, PLE_DIM, -1)
    W["ple_proj_w"] = [wp[:, i] for i in range(2)]
    W["ret_gn"] = small[0:2].reshape(RET_HEADS, RET_DV)
    W["mla_q_a_norm"] = small[2:3, :MLA_Q_RANK]
    W["mla_kv_a_norm"] = small[3:4, :MLA_KV_RANK]
    return W


def _pack_grads(g_ret, g_tail0, g_mla, g_tail1):
    n = N_CHIPS

    def both(name):
        return jnp.stack([g_tail0[name].reshape(n, -1, PACK_W), g_tail1[name].reshape(n, -1, PACK_W)], axis=1)

    uq = g_mla["mla_w_uq"].reshape(MLA_Q_RANK, n, MLA_HEADS // n, MLA_HP)[..., :MLA_QKD].transpose(1, 0, 2, 3)
    parts = dict(
        ret_w_in=g_ret["ret_w_in"], ret_w_out=g_ret["ret_w_out"], mla_w_in=g_mla["mla_w_in"][:, :MLA_IN],
        mla_w_uq=uq, mla_w_ukv=g_mla["mla_w_ukv"], mla_w_out=g_mla["mla_w_out"], mlp_w1=both("mlp_w1"),
        mlp_w2=both("mlp_w2"), ple_gate_w=both("ple_gate_w"), ple_proj_w=both("ple_proj_w"))
    return jnp.concatenate([parts[k].astype(BF16).reshape(n, -1, PACK_W) for k in _BIG], axis=1)


def _pad_row(v):
    v = v.reshape(1, -1)
    return jnp.pad(v, ((0, 0), (0, PACK_W - v.shape[1])))


def kernel(x, p, mix_norm, ret_w_in, ret_gn, ret_w_out, mla_w_in, mla_q_a_norm, mla_kv_a_norm, mla_w_uq, mla_w_ukv, mla_q_norm, mla_k_norm, mla_w_out, mlp_norm, mlp_w1, mlp_w2, ple_norm, ple_gate_w, ple_proj_w, loss_target, m_mix_norm, m_ret_w_in, m_ret_gn, m_ret_w_out, m_mla_w_in, m_mla_q_a_norm, m_mla_kv_a_norm, m_mla_w_uq, m_mla_w_ukv, m_mla_q_norm, m_mla_k_norm, m_mla_w_out, m_mlp_norm, m_mlp_w1, m_mlp_w2, m_ple_norm, m_ple_gate_w, m_ple_proj_w, v_mix_norm, v_ret_w_in, v_ret_gn, v_ret_w_out, v_mla_w_in, v_mla_q_a_norm, v_mla_kv_a_norm, v_mla_w_uq, v_mla_w_ukv, v_mla_q_norm, v_mla_k_norm, v_mla_w_out, v_mlp_norm, v_mlp_w1, v_mlp_w2, v_ple_norm, v_ple_gate_w, v_ple_proj_w):
    w = dict(mix_norm=mix_norm, ret_w_in=ret_w_in, ret_gn=ret_gn, ret_w_out=ret_w_out, mla_w_in=mla_w_in,
             mla_q_a_norm=mla_q_a_norm, mla_kv_a_norm=mla_kv_a_norm, mla_w_uq=mla_w_uq, mla_w_ukv=mla_w_ukv,
             mla_q_norm=mla_q_norm, mla_k_norm=mla_k_norm, mla_w_out=mla_w_out, mlp_norm=mlp_norm, mlp_w1=mlp_w1,
             mlp_w2=mlp_w2, ple_norm=ple_norm, ple_gate_w=ple_gate_w, ple_proj_w=ple_proj_w)
    m = dict(mix_norm=m_mix_norm, ret_w_in=m_ret_w_in, ret_gn=m_ret_gn, ret_w_out=m_ret_w_out, mla_w_in=m_mla_w_in,
             mla_q_a_norm=m_mla_q_a_norm, mla_kv_a_norm=m_mla_kv_a_norm, mla_w_uq=m_mla_w_uq, mla_w_ukv=m_mla_w_ukv,
             mla_q_norm=m_mla_q_norm, mla_k_norm=m_mla_k_norm, mla_w_out=m_mla_w_out, mlp_norm=m_mlp_norm,
             mlp_w1=m_mlp_w1, mlp_w2=m_mlp_w2, ple_norm=m_ple_norm, ple_gate_w=m_ple_gate_w, ple_proj_w=m_ple_proj_w)
    v = dict(mix_norm=v_mix_norm, ret_w_in=v_ret_w_in, ret_gn=v_ret_gn, ret_w_out=v_ret_w_out, mla_w_in=v_mla_w_in,
             mla_q_a_norm=v_mla_q_a_norm, mla_kv_a_norm=v_mla_kv_a_norm, mla_w_uq=v_mla_w_uq, mla_w_ukv=v_mla_w_ukv,
             mla_q_norm=v_mla_q_norm, mla_k_norm=v_mla_k_norm, mla_w_out=v_mla_w_out, mlp_norm=v_mlp_norm,
             mlp_w1=v_mlp_w1, mlp_w2=v_mlp_w2, ple_norm=v_ple_norm, ple_gate_w=v_ple_gate_w, ple_proj_w=v_ple_proj_w)
    xi, yi, ci = _place()
    chip = 2 * xi + yi
    n = N_CHIPS

    wsh = jnp.concatenate([w[k].astype(BF16).reshape(-1, PACK_W) for k in _BIG], axis=0)
    gath = _gather_weights(wsh)
    on = (jnp.arange(n) == chip) & (ci == 0)
    gn_rows = jnp.where(on[None, :, None], ret_gn[0][:, None, :], 0.0).reshape(2, PACK_W)
    qa_row = _pad_row(jnp.where(on[:, None], mla_q_a_norm, 0.0))
    kva_row = _pad_row(jnp.where(on[:, None], mla_kv_a_norm, 0.0))
    small_in = jnp.concatenate([gn_rows, qa_row, kva_row, jnp.zeros((4, PACK_W), F32)], axis=0)
    small = _allsum_small(small_in, "gather_gains")
    W = _full_weights(gath, {k: w[k] for k in _BIG}, small)
    W["mix_norm"], W["mlp_norm"], W["ple_norm"] = mix_norm, mlp_norm, ple_norm
    W["mla_q_norm"] = jnp.pad(mla_q_norm, ((0, 0), (0, MLA_HP - MLA_QKD)))
    W["mla_k_norm"] = jnp.pad(mla_k_norm, ((0, 0), (0, MLA_HP - MLA_QKD)))

    loss, dx, g_ret, g_tail0, g_mla, g_tail1 = _local_step(x[0], p[:, 0].astype(BF16), loss_target[0], W)

    red = _reduce_grads(_pack_grads(g_ret, g_tail0, g_mla, g_tail1))
    g_big = _split_rows(red, {k: w[k] for k in _BIG})
    small_g = jnp.concatenate([
        g_ret["mix_norm"], g_mla["mix_norm"], g_tail0["mlp_norm"], g_tail1["mlp_norm"], g_tail0["ple_norm"],
        g_tail1["ple_norm"], g_ret["ret_gn"].reshape(2, PACK_W), _pad_row(g_mla["mla_q_a_norm"]),
        _pad_row(g_mla["mla_kv_a_norm"]), _pad_row(g_mla["mla_q_norm"][:, :MLA_QKD]),
        _pad_row(g_mla["mla_k_norm"][:, :MLA_QKD]), _pad_row(loss[:, :1]), jnp.zeros((3, PACK_W), F32)], axis=0)
    tot = _allsum_small(small_g, "sum_small_grads")
    gn_all = tot[6:8].reshape(RET_HEADS, n, -1)
    g_small = dict(
        mix_norm=tot[0:2], mlp_norm=tot[2:4], ple_norm=tot[4:6],
        ret_gn=lax.dynamic_index_in_dim(gn_all, chip, axis=1, keepdims=False),
        mla_q_a_norm=lax.dynamic_index_in_dim(tot[8, :MLA_Q_RANK].reshape(n, -1), chip, axis=0, keepdims=True),
        mla_kv_a_norm=lax.dynamic_index_in_dim(tot[9, :MLA_KV_RANK].reshape(n, -1), chip, axis=0, keepdims=True),
        mla_q_norm=tot[10:11, :MLA_QKD], mla_k_norm=tot[11:12, :MLA_QKD])
    loss_out = tot[12, 0]

    grads, deltas, new_m, new_v = [], [], [], []
    for k in _ORDER:
        g = g_big[k] if k in g_big else g_small[k]
        g = g.reshape(w[k].shape)
        d, m2, v2 = _adamw(w[k], g, m[k], v[k], f"adamw_{k}")
        grads.append(g)
        deltas.append(d)
        new_m.append(m2)
        new_v.append(v2)
    return (loss_out, dx[None], *grads, *deltas, *new_m, *new_v)
```

```python
import functools

import jax
import jax.numpy as jnp
from jax import lax
from jax.experimental import pallas as pl
from jax.experimental.pallas import tpu as pltpu

F32 = jnp.float32
BF16 = jnp.bfloat16

EPS = 1e-6
D_MODEL = 1024
CHUNK = 64
ROPE_THETA = 10000.0
RET_HEADS = 4
RET_DK = 256
RET_DV = 512
MLA_HEADS = 8
MLA_NOPE = 128
MLA_ROPE = 64
MLA_QKD = 192
MLA_VD = 128
MLA_HP = 256
MLA_Q_RANK = 384
MLA_KV_RANK = 256
MLA_IN = 704
MLA_IN_PAD = 768
D_FF = 4096
PLE_DIM = 256
N_CHIPS = 4

ADAM_LR = 0.001
ADAM_B1 = 0.9
ADAM_B2 = 0.999
ADAM_EPS = 1e-08
ADAM_WD = 0.01
ADAM_STEP = 10

VMEM_LIMIT = 56 * 1024 * 1024
PACK_W = 1024
NEG = -1e30
FLASH_T = 512
COMM_CHUNKS = 9


def _cparams(sem=None):
    return pltpu.CompilerParams(dimension_semantics=sem, vmem_limit_bytes=VMEM_LIMIT)


def _pick(dim, pref):
    if dim <= pref:
        return dim
    t = pref
    while dim % t:
        t //= 2
    return t


def _mm(a, b, *, name, ta=False, tb=False, bblk=False, oblk=None, outs=None, extras=(), epilogue=None,
        tm=1024, tn=512, tk=1024):
    if ta:
        K, M = a.shape
    else:
        M, K = a.shape
    if bblk:
        if tb:
            _, N, Kq = b.shape
            assert Kq * N_CHIPS == K
        else:
            _, Kb, Nq_b = b.shape
            N = Nq_b * N_CHIPS
            assert Kb == K
    else:
        N = b.shape[0] if tb else b.shape[1]
    tm = _pick(M, tm)
    tn = _pick(N if not (bblk and not tb) else b.shape[2], tn)
    if oblk is not None:
        tn = _pick(oblk, tn)
    tk = _pick(K if not (bblk and tb) else b.shape[2], tk)
    nk = K // tk
    grid = (M // tm, N // tn, nk)

    if ta:
        a_spec = pl.BlockSpec((tk, tm), lambda i, j, k: (k, i))
    else:
        a_spec = pl.BlockSpec((tm, tk), lambda i, j, k: (i, k))
    if bblk and tb:
        kpb = b.shape[2] // tk
        b_spec = pl.BlockSpec((None, tn, tk), lambda i, j, k: (k // kpb, j, k % kpb))
    elif bblk:
        npb = b.shape[2] // tn
        b_spec = pl.BlockSpec((None, tk, tn), lambda i, j, k: (j // npb, k, j % npb))
    elif tb:
        b_spec = pl.BlockSpec((tn, tk), lambda i, j, k: (j, k))
    else:
        b_spec = pl.BlockSpec((tk, tn), lambda i, j, k: (k, j))
    e_specs = [pl.BlockSpec((tm, tn), lambda i, j, k: (i, j)) for _ in extras]
    if outs is None:
        outs = [F32]
    if oblk is not None:
        opb = oblk // tn
        o_specs = [pl.BlockSpec((None, tm, tn), lambda i, j, k: (j // opb, i, j % opb)) for _ in outs]
        o_shapes = [jax.ShapeDtypeStruct((N_CHIPS, M, oblk), dt) for dt in outs]
    else:
        o_specs = [pl.BlockSpec((tm, tn), lambda i, j, k: (i, j)) for _ in outs]
        o_shapes = [jax.ShapeDtypeStruct((M, N), dt) for dt in outs]
    n_e, n_o = len(extras), len(outs)
    if ta:
        dims = (((0,), (0,)), ((), ()))
    elif tb:
        dims = (((1,), (1,)), ((), ()))
    else:
        dims = (((1,), (0,)), ((), ()))

    def body(a_ref, b_ref, *rest):
        e_refs, o_refs, acc = rest[:n_e], rest[n_e:n_e + n_o], rest[n_e + n_o]
        k = pl.program_id(2)
        part = lax.dot_general(a_ref[...].astype(BF16), b_ref[...].astype(BF16), dims,
                               preferred_element_type=F32)

        @pl.when(k == 0)
        def _():
            acc[...] = part

        @pl.when(k > 0)
        def _():
            acc[...] += part

        @pl.when(k == nk - 1)
        def _():
            res = acc[...]
            vals = (res,) if epilogue is None else epilogue(res, *[e[...] for e in e_refs])
            for o, v in zip(o_refs, vals):
                o[...] = v.astype(o.dtype)

    res = pl.pallas_call(
        body, name=name, grid=grid,
        in_specs=[a_spec, b_spec, *e_specs], out_specs=o_specs, out_shape=o_shapes,
        scratch_shapes=[pltpu.VMEM((tm, tn), F32)],
        compiler_params=_cparams(("parallel", "parallel", "arbitrary")),
    )(a, b, *extras)
    return res[0] if n_o == 1 else res


def _rows(fn, rows, fulls, outs, accs=(), *, name, tile=256):
    first = rows[0][0] if isinstance(rows[0], tuple) else rows[0]
    T = first.shape[0]
    tile = _pick(T, tile)
    in_specs, args = [], []
    for r in rows:
        if isinstance(r, tuple):
            arr, w, cb = r
            in_specs.append(pl.BlockSpec((tile, w), lambda i, cb=cb: (i, cb)))
        else:
            arr = r
            in_specs.append(pl.BlockSpec((tile, arr.shape[1]), lambda i: (i, 0)))
        args.append(arr)
    for f in fulls:
        in_specs.append(pl.BlockSpec(f.shape, lambda i, nd=f.ndim: (0,) * nd))
        args.append(f)
    out_specs = [pl.BlockSpec((tile, w), lambda i: (i, 0)) for w, _ in outs]
    out_specs += [pl.BlockSpec(s, lambda i: (0, 0)) for s, _ in accs]
    out_shape = [jax.ShapeDtypeStruct((T, w), dt) for w, dt in outs]
    out_shape += [jax.ShapeDtypeStruct(s, dt) for s, dt in accs]
    n_in, n_out = len(args), len(outs)

    def body(*refs):
        vals = fn(*[r[...] for r in refs[:n_in]])
        o_refs = refs[n_in:]
        for o, v in zip(o_refs[:n_out], vals[:n_out]):
            o[...] = v.astype(o.dtype)
        first_step = pl.program_id(0) == 0
        for o, v in zip(o_refs[n_out:], vals[n_out:]):
            @pl.when(first_step)
            def _(o=o, v=v):
                o[...] = v.astype(o.dtype)

            @pl.when(jnp.logical_not(first_step))
            def _(o=o, v=v):
                o[...] += v.astype(o.dtype)

    res = pl.pallas_call(
        body, name=name, grid=(T // tile,), in_specs=in_specs, out_specs=out_specs, out_shape=out_shape,
        compiler_params=_cparams(("arbitrary",)),
    )(*args)
    return res


def _rms(x, g):
    r = lax.rsqrt(jnp.mean(x * x, axis=-1, keepdims=True) + EPS)
    return (x * r) * g


def _rms_bwd(x, dy, g, n=None):
    n = x.shape[-1] if n is None else n
    r = lax.rsqrt(jnp.sum(x * x, axis=-1, keepdims=True) / n + EPS)
    xh = x * r
    dxh = dy * g
    dx = r * (dxh - xh * (jnp.sum(dxh * xh, axis=-1, keepdims=True) / n))
    return dx, dy * xh


def _colsum(v):
    return jnp.sum(v, axis=0, keepdims=True)


def _sigmoid(x):
    return 1.0 / (1.0 + jnp.exp(-x))


def _widen(v, width):
    reps = width // v.shape[1]
    return v if reps == 1 else jnp.concatenate([v] * reps, axis=-1)


def _norm_fwd(h, gain, name):
    return _rows(lambda x, g: (_rms(x, g),), [h], [gain], [(h.shape[1], BF16)], name=name)[0]


def _norm_bwd(h, dhn, gain, dres, name):
    def fn(x, dy, dr, g):
        dx, dg = _rms_bwd(x, dy, g)
        return dr + dx, _colsum(dg)
    d = h.shape[1]
    return _rows(fn, [h, dhn, dres], [gain], [(d, F32)], [((1, d), F32)], name=name)


def _ret_tables(T):
    inv = 1.0 / (ROPE_THETA ** (jnp.arange(0, RET_DK, 2, dtype=F32) / RET_DK))
    ang = jnp.arange(T, dtype=F32)[:, None] * inv[None, :]
    log_gamma = jnp.log(1.0 - 2.0 ** (-5.0 - jnp.arange(RET_HEADS, dtype=F32)))
    idx = jnp.arange(CHUNK, dtype=F32)
    intra = jnp.exp(log_gamma[:, None, None] * jnp.abs(idx[:, None] - idx[None, :]))
    qd = jnp.exp(log_gamma[:, None] * (idx + 1.0))[:, :, None]
    kd = jnp.exp(log_gamma[:, None] * (CHUNK - 1.0 - idx))[:, :, None]
    cd = jnp.exp(log_gamma * CHUNK)[:, None, None]
    return jnp.cos(ang), jnp.sin(ang), intra, qd, kd, cd


def _rope_half(x, c, s):
    x1, x2 = x[:, :RET_DK // 2], x[:, RET_DK // 2:]
    return jnp.concatenate([x1 * c - x2 * s, x2 * c + x1 * s], axis=-1)


def _rope_half_bwd(d, c, s):
    d1, d2 = d[:, :RET_DK // 2], d[:, RET_DK // 2:]
    return jnp.concatenate([d1 * c + d2 * s, d2 * c - d1 * s], axis=-1)


def _dot(a, b):
    return lax.dot_general(a, b, (((1,), (0,)), ((), ())), preferred_element_type=F32)


def _dot_nt(a, b):
    return lax.dot_general(a, b, (((1,), (1,)), ((), ())), preferred_element_type=F32)


def _dot_tn(a, b):
    return lax.dot_general(a, b, (((0,), (0,)), ((), ())), preferred_element_type=F32)


def _ret_specs(T, tb, rev):
    nj = T // tb
    jj = (lambda j: nj - 1 - j) if rev else (lambda j: j)
    kq = RET_HEADS
    vq = 2 * RET_HEADS * RET_DK // RET_DV
    return dict(
        q=pl.BlockSpec((tb, RET_DK), lambda h, j: (jj(j), h)),
        k=pl.BlockSpec((tb, RET_DK), lambda h, j: (jj(j), kq + h)),
        v=pl.BlockSpec((tb, RET_DV), lambda h, j: (jj(j), vq + h)),
        tab=pl.BlockSpec((tb, RET_DK // 2), lambda h, j: (jj(j), 0)),
        intra=pl.BlockSpec((None, CHUNK, CHUNK), lambda h, j: (h, 0, 0)),
        dec=pl.BlockSpec((None, CHUNK, 1), lambda h, j: (h, 0, 0)),
        cd=pl.BlockSpec((None, 1, 1), lambda h, j: (h, 0, 0)),
        o=pl.BlockSpec((tb, RET_DV), lambda h, j: (jj(j), h)),
        s=pl.BlockSpec((None, tb // CHUNK, RET_DK, RET_DV), lambda h, j: (h, jj(j), 0, 0)),
    )


def _ret_fwd(proj, tabs, name):
    T = proj.shape[0]
    cos, sin, intra, qd, kd, cd = tabs
    tb = _pick(T, 512)
    cps = tb // CHUNK
    sp = _ret_specs(T, tb, False)
    scale = RET_DK ** -0.5

    def body(q_ref, k_ref, v_ref, cos_ref, sin_ref, intra_ref, qd_ref, kd_ref, cd_ref, o_ref, s_ref, state):
        @pl.when(pl.program_id(1) == 0)
        def _():
            state[...] = jnp.zeros_like(state)

        for c in range(cps):
            rows = pl.ds(c * CHUNK, CHUNK)
            co, si = cos_ref[rows, :], sin_ref[rows, :]
            q = _rope_half(q_ref[rows, :], co, si)
            k = _rope_half(k_ref[rows, :], co, si) * scale
            vb = v_ref[rows, :].astype(BF16)
            st = state[...]
            sb = st.astype(BF16)
            s_ref[c] = sb
            sc = _dot_nt(q.astype(BF16), k.astype(BF16)) * intra_ref[...]
            inner = _dot(sc.astype(BF16), vb)
            cross = _dot((q * qd_ref[...]).astype(BF16), sb)
            o_ref[rows, :] = inner + cross
            state[...] = st * cd_ref[...] + _dot_tn((k * kd_ref[...]).astype(BF16), vb)

    return pl.pallas_call(
        body, name=name, grid=(RET_HEADS, T // tb),
        in_specs=[sp["q"], sp["k"], sp["v"], sp["tab"], sp["tab"], sp["intra"], sp["dec"], sp["dec"], sp["cd"]],
        out_specs=[sp["o"], sp["s"]],
        out_shape=[jax.ShapeDtypeStruct((T, RET_HEADS * RET_DV), F32),
                   jax.ShapeDtypeStruct((RET_HEADS, T // CHUNK, RET_DK, RET_DV), BF16)],
        scratch_shapes=[pltpu.VMEM((RET_DK, RET_DV), F32)],
        compiler_params=_cparams(("arbitrary", "arbitrary")),
    )(proj, proj, proj, cos, sin, intra, qd, kd, cd)


def _ret_bwd(proj, states, dout, tabs, name):
    T = proj.shape[0]
    cos, sin, intra, qd, kd, cd = tabs
    tb = _pick(T, 512)
    cps = tb // CHUNK
    sp = _ret_specs(T, tb, True)
    scale = RET_DK ** -0.5

    def body(q_ref, k_ref, v_ref, cos_ref, sin_ref, intra_ref, qd_ref, kd_ref, cd_ref, s_ref, do_ref,
             dq_ref, dk_ref, dv_ref, dstate):
        @pl.when(pl.program_id(1) == 0)
        def _():
            dstate[...] = jnp.zeros_like(dstate)

        for c in reversed(range(cps)):
            rows = pl.ds(c * CHUNK, CHUNK)
            co, si = cos_ref[rows, :], sin_ref[rows, :]
            q = _rope_half(q_ref[rows, :], co, si)
            k = _rope_half(k_ref[rows, :], co, si) * scale
            qb, kb = q.astype(BF16), k.astype(BF16)
            vb = v_ref[rows, :].astype(BF16)
            dob = do_ref[rows, :].astype(BF16)
            sb = s_ref[c]
            ia = intra_ref[...]
            pb = (_dot_nt(qb, kb) * ia).astype(BF16)
            dsn = dstate[...]
            dsb = dsn.astype(BF16)
            kdk = (k * kd_ref[...]).astype(BF16)
            qdq = (q * qd_ref[...]).astype(BF16)
            dv = _dot_tn(pb, dob) + _dot(kdk, dsb)
            dpb = (_dot_nt(dob, vb) * ia).astype(BF16)
            dq = _dot(dpb, kb) + _dot_nt(dob, sb) * qd_ref[...]
            dk = _dot_tn(dpb, qb) + _dot_nt(vb, dsb) * kd_ref[...]
            dstate[...] = dsn * cd_ref[...] + _dot_tn(qdq, dob)
            dq_ref[rows, :] = _rope_half_bwd(dq, co, si).astype(BF16)
            dk_ref[rows, :] = _rope_half_bwd(dk * scale, co, si).astype(BF16)
            dv_ref[rows, :] = dv.astype(BF16)

    return pl.pallas_call(
        body, name=name, grid=(RET_HEADS, T // tb),
        in_specs=[sp["q"], sp["k"], sp["v"], sp["tab"], sp["tab"], sp["intra"], sp["dec"], sp["dec"], sp["cd"],
                  sp["s"], sp["o"]],
        out_specs=[sp["q"], sp["q"], sp["o"]],
        out_shape=[jax.ShapeDtypeStruct((T, RET_HEADS * RET_DK), BF16),
                   jax.ShapeDtypeStruct((T, RET_HEADS * RET_DK), BF16),
                   jax.ShapeDtypeStruct((T, RET_HEADS * RET_DV), BF16)],
        scratch_shapes=[pltpu.VMEM((RET_DK, RET_DV), F32)],
        compiler_params=_cparams(("arbitrary", "arbitrary")),
    )(proj, proj, proj, cos, sin, intra, qd, kd, cd, states, dout)


def _ret_gate(out, proj, gn, name):
    def fn(o, g, *gains):
        parts = [_rms(o[:, h * RET_DV:(h + 1) * RET_DV], gains[h]) for h in range(RET_HEADS)]
        return (g * _sigmoid(g) * jnp.concatenate(parts, axis=-1),)
    w = RET_HEADS * RET_DV
    return _rows(fn, [out, (proj, w, 2)], [gn[h:h + 1] for h in range(RET_HEADS)], [(w, BF16)], name=name)[0]


def _ret_gate_bwd(out, proj, gn, dy, name):
    def fn(o, g, d, *gains):
        sg = _sigmoid(g)
        silu = g * sg
        dsilu = sg * (1.0 + g * (1.0 - sg))
        dos, dgs = [], []
        row = lax.broadcasted_iota(jnp.int32, (RET_HEADS, RET_DV), 0)
        dgn = jnp.zeros((RET_HEADS, RET_DV), F32)
        for h in range(RET_HEADS):
            sl = slice(h * RET_DV, (h + 1) * RET_DV)
            oh = o[:, sl]
            dgs.append(d[:, sl] * _rms(oh, gains[h]) * dsilu[:, sl])
            dx, dg = _rms_bwd(oh, d[:, sl] * silu[:, sl], gains[h])
            dos.append(dx)
            dgn = dgn + jnp.where(row == h, _colsum(dg), 0.0)
        return jnp.concatenate(dos, axis=-1), jnp.concatenate(dgs, axis=-1), dgn
    w = RET_HEADS * RET_DV
    return _rows(fn, [out, (proj, w, 2), dy], [gn[h:h + 1] for h in range(RET_HEADS)], [(w, BF16), (w, BF16)],
                 [((RET_HEADS, RET_DV), F32)], name=name, tile=128)


def _mla_tables(T):
    inv = 1.0 / (ROPE_THETA ** (jnp.arange(0, MLA_ROPE, 2, dtype=F32) / MLA_ROPE))
    ang = jnp.arange(T, dtype=F32)[:, None] * inv[None, :]
    c, s = jnp.cos(ang), jnp.sin(ang)
    z32, z64 = jnp.zeros((T, 32), F32), jnp.zeros((T, 64), F32)
    cos_t = jnp.concatenate([c, c, z64], axis=1)
    sin_a = jnp.concatenate([-s, z32, z64], axis=1)
    sin_b = jnp.concatenate([z32, s, z64], axis=1)
    return cos_t, sin_a, sin_b


def _rope_blk(x, ct, sa, sb):
    return x * ct + pltpu.roll(x, 96, 1) * sa + pltpu.roll(x, 32, 1) * sb


def _rope_blk_bwd(d, ct, sa, sb):
    return d * ct + pltpu.roll(d * sa, 32, 1) + pltpu.roll(d * sb, 96, 1)


def _head_norm(x, gain):
    r = lax.rsqrt(jnp.sum(x * x, axis=-1, keepdims=True) / MLA_QKD + EPS)
    return (x * r) * gain


def _mla_prep(q, kv, proj, gq, gk, tabs, name):
    def fn(qv, kvv, kr, ct, sa, sb, gqv, gkv):
        qs, ks, vs = [], [], []
        for h in range(MLA_HEADS):
            b = h * MLA_HP
            y = _head_norm(qv[:, b:b + MLA_HP], gqv)
            qs += [y[:, :128], _rope_blk(y[:, 128:], ct, sa, sb)]
            y = _head_norm(jnp.concatenate([kvv[:, b:b + 128], kr], axis=-1), gkv)
            ks += [y[:, :128], _rope_blk(y[:, 128:], ct, sa, sb)]
            vs.append(kvv[:, b + 128:b + 256])
        return jnp.concatenate(qs, axis=-1), jnp.concatenate(ks, axis=-1), jnp.concatenate(vs, axis=-1)
    w = MLA_HEADS * MLA_HP
    return _rows(fn, [q, kv, (proj, 128, 5), *tabs], [gq, gk],
                 [(w, BF16), (w, BF16), (MLA_HEADS * MLA_VD, BF16)], name=name, tile=128)


def _mla_prep_bwd(q, kv, proj, gq, gk, tabs, dqf, dkf, dvf, name):
    def fn(qv, kvv, kr, ct, sa, sb, dqv, dkv, dvv, gqv, gkv):
        dqs, dkvs = [], []
        dkr = jnp.zeros_like(kr)
        dgq = jnp.zeros((1, MLA_HP), F32)
        dgk = jnp.zeros((1, MLA_HP), F32)
        for h in range(MLA_HEADS):
            b = h * MLA_HP
            dy = jnp.concatenate([dqv[:, b:b + 128], _rope_blk_bwd(dqv[:, b + 128:b + 256], ct, sa, sb)], axis=-1)
            dx, dg = _rms_bwd(qv[:, b:b + MLA_HP], dy, gqv, MLA_QKD)
            dqs.append(dx)
            dgq = dgq + _colsum(dg)
            dy = jnp.concatenate([dkv[:, b:b + 128], _rope_blk_bwd(dkv[:, b + 128:b + 256], ct, sa, sb)], axis=-1)
            dx, dg = _rms_bwd(jnp.concatenate([kvv[:, b:b + 128], kr], axis=-1), dy, gkv, MLA_QKD)
            dkvs += [dx[:, :128], dvv[:, h * MLA_VD:(h + 1) * MLA_VD]]
            dkr = dkr + dx[:, 128:]
            dgk = dgk + _colsum(dg)
        return jnp.concatenate(dqs, axis=-1), jnp.concatenate(dkvs, axis=-1), dkr, dgq, dgk
    w = MLA_HEADS * MLA_HP
    return _rows(fn, [q, kv, (proj, 128, 5), *tabs, dqf, dkf, dvf], [gq, gk],
                 [(w, BF16), (w, BF16), (128, F32)], [((1, MLA_HP), F32), ((1, MLA_HP), F32)], name=name, tile=128)


def _chunk_mask(qi, ki, tq, tk):
    shift = CHUNK.bit_length() - 1
    rq = lax.shift_right_arithmetic(qi * tq + lax.broadcasted_iota(jnp.int32, (tq, tk), 0), shift)
    ck = lax.shift_right_arithmetic(ki * tk + lax.broadcasted_iota(jnp.int32, (tq, tk), 1), shift)
    return ck <= rq


def _flash_fwd(qf, kf, vf, name):
    T = qf.shape[0]
    t = _pick(T, FLASH_T)
    n = T // t
    scale = MLA_QKD ** -0.5

    def body(q_ref, k_ref, v_ref, o_ref, lse_ref, m_s, l_s, acc):
        qi = pl.program_id(1)
        q = q_ref[...]
        m_s[...] = jnp.full_like(m_s, NEG)
        l_s[...] = jnp.zeros_like(l_s)
        acc[...] = jnp.zeros_like(acc)

        def step(kb, masked):
            rows = pl.ds(pl.multiple_of(kb * t, t), t)
            s = _dot_nt(q, k_ref[rows, :]) * scale
            if masked:
                s = jnp.where(_chunk_mask(0, 0, t, t), s, NEG)
            m_prev = m_s[...]
            m_new = jnp.maximum(m_prev, jnp.max(s, axis=-1, keepdims=True))
            alpha = jnp.exp(m_prev - m_new)
            p = jnp.exp(s - _widen(m_new, t))
            l_s[...] = alpha * l_s[...] + sum(p[:, i * 128:(i + 1) * 128] for i in range(t // 128))
            acc[...] = acc[...] * alpha + _dot(p.astype(BF16), v_ref[rows, :])
            m_s[...] = m_new

        @pl.loop(0, qi)
        def _(kb):
            step(kb, False)

        step(qi, True)
        l = jnp.sum(l_s[...], axis=-1, keepdims=True)
        o_ref[...] = acc[...] / l
        lse_ref[...] = m_s[...] + jnp.log(l)

    qmap = lambda h, i: (i, h)
    kmap = lambda h, i: (0, h)
    return pl.pallas_call(
        body, name=name, grid=(MLA_HEADS, n),
        in_specs=[pl.BlockSpec((t, MLA_HP), qmap), pl.BlockSpec((T, MLA_HP), kmap), pl.BlockSpec((T, MLA_VD), kmap)],
        out_specs=[pl.BlockSpec((t, MLA_VD), qmap), pl.BlockSpec((t, MLA_VD), qmap)],
        out_shape=[jax.ShapeDtypeStruct((T, MLA_HEADS * MLA_VD), F32),
                   jax.ShapeDtypeStruct((T, MLA_HEADS * MLA_VD), F32)],
        scratch_shapes=[pltpu.VMEM((t, MLA_VD), F32), pltpu.VMEM((t, MLA_VD), F32), pltpu.VMEM((t, MLA_VD), F32)],
        compiler_params=_cparams(("parallel", "arbitrary")),
    )(qf, kf, vf)


def _flash_delta(o, do, name):
    def fn(ov, dv):
        parts = []
        for h in range(MLA_HEADS):
            sl = slice(h * MLA_VD, (h + 1) * MLA_VD)
            d = jnp.sum(dv[:, sl] * ov[:, sl], axis=-1, keepdims=True)
            parts.append(jnp.broadcast_to(d, (d.shape[0], MLA_VD)))
        return jnp.concatenate(parts, axis=-1), dv
    w = MLA_HEADS * MLA_VD
    return _rows(fn, [o, do], [], [(w, F32), (w, BF16)], name=name)


def _flash_bwd(qf, kf, vf, do16, lse, delta, name):
    T = qf.shape[0]
    t = _pick(T, FLASH_T)
    n = T // t
    scale = MLA_QKD ** -0.5

    def body(q_ref, k_ref, v_ref, do_ref, lse_ref, dl_ref, dq_ref, dk_ref, dv_ref):
        kb = pl.program_id(1)

        @pl.when(kb == 0)
        def _():
            dq_ref[...] = jnp.zeros_like(dq_ref)

        dk_ref[...] = jnp.zeros_like(dk_ref)
        dv_ref[...] = jnp.zeros_like(dv_ref)
        k, v = k_ref[...], v_ref[...]

        def step(qb, masked):
            rows = pl.ds(pl.multiple_of(qb * t, t), t)
            q, dob = q_ref[rows, :], do_ref[rows, :]
            s = _dot_nt(q, k) * scale
            if masked:
                s = jnp.where(_chunk_mask(0, 0, t, t), s, NEG)
            p = jnp.exp(s - _widen(lse_ref[rows, :], t))
            ds = (p * (_dot_nt(dob, v) - _widen(dl_ref[rows, :], t)) * scale).astype(BF16)
            dv_ref[...] += _dot_tn(p.astype(BF16), dob)
            dk_ref[...] += _dot_tn(ds, q)
            dq_ref[rows, :] += _dot(ds, k)

        step(kb, True)

        @pl.loop(kb + 1, n)
        def _(qb):
            step(qb, False)

    qmap = lambda h, j: (0, h)
    kmap = lambda h, j: (j, h)
    return pl.pallas_call(
        body, name=name, grid=(MLA_HEADS, n),
        in_specs=[pl.BlockSpec((T, MLA_HP), qmap), pl.BlockSpec((t, MLA_HP), kmap), pl.BlockSpec((t, MLA_VD), kmap),
                  pl.BlockSpec((T, MLA_VD), qmap), pl.BlockSpec((T, MLA_VD), qmap), pl.BlockSpec((T, MLA_VD), qmap)],
        out_specs=[pl.BlockSpec((T, MLA_HP), qmap), pl.BlockSpec((t, MLA_HP), kmap), pl.BlockSpec((t, MLA_VD), kmap)],
        out_shape=[jax.ShapeDtypeStruct((T, MLA_HEADS * MLA_HP), F32),
                   jax.ShapeDtypeStruct((T, MLA_HEADS * MLA_HP), F32),
                   jax.ShapeDtypeStruct((T, MLA_HEADS * MLA_VD), F32)],
        compiler_params=_cparams(("arbitrary", "arbitrary")),
    )(qf, kf, vf, do16, lse, delta)


MESH = pl.DeviceIdType.MESH
ANY = pl.BlockSpec(memory_space=pl.ANY)
_CHIP_FLIPS = ((1, 0), (0, 1), (1, 1))


def _place():
    return lax.axis_index("x"), lax.axis_index("y"), lax.axis_index("c")


def _other_chip(x, y, k):
    fx, fy = _CHIP_FLIPS[k]
    return ((1 - x) if fx else x), ((1 - y) if fy else y)


def _row_chunks(rows):
    cr = rows // COMM_CHUNKS
    assert cr * COMM_CHUNKS == rows and cr % 16 == 0, rows
    return [(i * cr, cr) for i in range(COMM_CHUNKS)]


def _rows_at(start, size):
    return pl.ds(pl.multiple_of(start, 16), size)


def _remote(src, dst, send_sems, recv_sems, k, to):
    return pltpu.make_async_remote_copy(src_ref=src, dst_ref=dst, send_sem=send_sems.at[k], recv_sem=recv_sems.at[k],
                                        device_id=to, device_id_type=MESH)


def _gather_weights(wsh):
    R, W = wsh.shape
    H = R // 2

    chunks = _row_chunks(H)
    n = len(chunks)

    def body(w_ref, out_ref, send_sems, recv_sems, local_sem):
        x, y, c = _place()
        j = 2 * x + y
        sibling = (x, y, 1 - c)
        chips = [_other_chip(x, y, k) for k in range(3)]
        mine = pltpu.make_async_copy(w_ref, out_ref.at[j], local_sem)
        mine.start()
        sent = []
        for i, (off, cr) in enumerate(chunks):
            r = _rows_at(c * H + off, cr)
            for k, (px, py) in enumerate(chips):
                cp = _remote(w_ref.at[r], out_ref.at[j, r], send_sems, recv_sems, k * n + i, (px, py, c))
                cp.start()
                sent.append(cp)
        for i, (off, cr) in enumerate(chunks):
            r = _rows_at(c * H + off, cr)
            for k, (px, py) in enumerate(chips):
                blk = out_ref.at[2 * px + py, r]
                _remote(blk, blk, send_sems, recv_sems, k * n + i, (px, py, c)).wait_recv()
                cp = _remote(blk, blk, send_sems, recv_sems, (3 + k) * n + i, sibling)
                cp.start()
                sent.append(cp)
        for i, (off, cr) in enumerate(chunks):
            r = _rows_at((1 - c) * H + off, cr)
            for k, (px, py) in enumerate(chips):
                blk = out_ref.at[2 * px + py, r]
                _remote(blk, blk, send_sems, recv_sems, (3 + k) * n + i, sibling).wait_recv()
        for cp in sent:
            cp.wait_send()
        mine.wait()

    return pl.pallas_call(
        body, name="gather_weights", in_specs=[ANY], out_specs=ANY,
        out_shape=jax.ShapeDtypeStruct((N_CHIPS, R, W), wsh.dtype),
        scratch_shapes=[pltpu.SemaphoreType.DMA((6 * n,)), pltpu.SemaphoreType.DMA((6 * n,)), pltpu.SemaphoreType.DMA],
    )(wsh)


def _swap_halves(g):
    _, R, W = g.shape
    H = R // 2

    chunks = _row_chunks(H)
    n = len(chunks)

    def body(g_ref, recv_ref, own_ref, send_sems, recv_sems, local_sem):
        x, y, c = _place()
        mine = pltpu.make_async_copy(g_ref.at[:, _rows_at(c * H, H)], own_ref, local_sem)
        mine.start()
        sent = []
        for jj in range(N_CHIPS):
            for i, (off, cr) in enumerate(chunks):
                cp = _remote(g_ref.at[jj, _rows_at((1 - c) * H + off, cr)], recv_ref.at[jj, pl.ds(off, cr)],
                             send_sems, recv_sems, jj * n + i, (x, y, 1 - c))
                cp.start()
                sent.append(cp)
        for cp in sent:
            cp.wait()
        mine.wait()

    out = jax.ShapeDtypeStruct((N_CHIPS, H, W), g.dtype)
    return pl.pallas_call(
        body, name="grad_swap_halves", in_specs=[ANY], out_specs=[ANY, ANY], out_shape=[out, out],
        scratch_shapes=[pltpu.SemaphoreType.DMA((N_CHIPS * n,)), pltpu.SemaphoreType.DMA((N_CHIPS * n,)),
                        pltpu.SemaphoreType.DMA],
    )(g)


def _scatter_chips(a16, a32):
    _, H, W = a16.shape
    chunks = _row_chunks(H)
    n = len(chunks)

    def body(a16_ref, a32_ref, recv_ref, own_ref, send_sems, recv_sems, local_sem):
        x, y, c = _place()
        j = 2 * x + y
        mine = pltpu.make_async_copy(a32_ref.at[j], own_ref, local_sem)
        mine.start()
        sent = []
        for i, (off, cr) in enumerate(chunks):
            r = pl.ds(off, cr)
            for k in range(3):
                px, py = _other_chip(x, y, k)
                pj = 2 * px + py
                cp = _remote(a16_ref.at[pj, r], recv_ref.at[(j - pj + 4) % 4 - 1, r], send_sems, recv_sems,
                             k * n + i, (px, py, c))
                cp.start()
                sent.append(cp)
        for i, (off, cr) in enumerate(chunks):
            r = pl.ds(off, cr)
            for k in range(3):
                px, py = _other_chip(x, y, k)
                pj = 2 * px + py
                slot = recv_ref.at[(pj - j + 4) % 4 - 1, r]
                _remote(slot, slot, send_sems, recv_sems, k * n + i, (px, py, c)).wait_recv()
        for cp in sent:
            cp.wait_send()
        mine.wait()

    return pl.pallas_call(
        body, name="grad_scatter_chips", in_specs=[ANY, ANY], out_specs=[ANY, ANY],
        out_shape=[jax.ShapeDtypeStruct((3, H, W), a16.dtype), jax.ShapeDtypeStruct((H, W), a32.dtype)],
        scratch_shapes=[pltpu.SemaphoreType.DMA((3 * n,)), pltpu.SemaphoreType.DMA((3 * n,)), pltpu.SemaphoreType.DMA],
    )(a16, a32)


def _share_halves(r):
    H, W = r.shape
    chunks = _row_chunks(H)
    n = len(chunks)

    def body(r_ref, out_ref, send_sems, recv_sems, local_sem):
        x, y, c = _place()
        mine = pltpu.make_async_copy(r_ref, out_ref.at[_rows_at(c * H, H)], local_sem)
        mine.start()
        sent = []
        for i, (off, cr) in enumerate(chunks):
            cp = _remote(r_ref.at[pl.ds(off, cr)], out_ref.at[_rows_at(c * H + off, cr)], send_sems, recv_sems, i,
                         (x, y, 1 - c))
            cp.start()
            sent.append(cp)
        for cp in sent:
            cp.wait()
        mine.wait()

    return pl.pallas_call(
        body, name="grad_share_halves", in_specs=[ANY], out_specs=ANY,
        out_shape=jax.ShapeDtypeStruct((2 * H, W), r.dtype),
        scratch_shapes=[pltpu.SemaphoreType.DMA((n,)), pltpu.SemaphoreType.DMA((n,)), pltpu.SemaphoreType.DMA],
    )(r)


def _allsum_small(v, name):
    R, W = v.shape
    n_dev = 8
    vm = pl.BlockSpec(memory_space=pltpu.VMEM)

    def body(v_ref, out_ref, buf, send_sems, recv_sems):
        x, y, c = _place()
        me = 4 * x + 2 * y + c
        buf[me] = v_ref[...]
        sent = []
        for k in range(1, n_dev):
            peer = ((1 - x) if k & 4 else x, (1 - y) if k & 2 else y, (1 - c) if k & 1 else c)
            cp = _remote(v_ref, buf.at[me], send_sems, recv_sems, k - 1, peer)
            cp.start()
            sent.append(cp)
        for cp in sent:
            cp.wait_recv()
        for cp in sent:
            cp.wait_send()
        acc = buf[0]
        for q in range(1, n_dev):
            acc = acc + buf[q]
        out_ref[...] = acc

    return pl.pallas_call(
        body, name=name, in_specs=[vm], out_specs=vm, out_shape=jax.ShapeDtypeStruct((R, W), v.dtype),
        scratch_shapes=[pltpu.VMEM((n_dev, R, W), v.dtype), pltpu.SemaphoreType.DMA((n_dev - 1,)),
                        pltpu.SemaphoreType.DMA((n_dev - 1,))],
    )(v)


def _reduce_grads(g16):
    _, R, W = g16.shape
    H = R // 2
    recv, own = _swap_halves(g16)
    a32, a16 = _rows(lambda a, b: (a.astype(F32) + b.astype(F32),) * 2,
                     [own.reshape(N_CHIPS * H, W), recv.reshape(N_CHIPS * H, W)], [],
                     [(W, F32), (W, BF16)], name="grad_pair_sum", tile=288)
    got, mine = _scatter_chips(a16.reshape(N_CHIPS, H, W), a32.reshape(N_CHIPS, H, W))
    red = _rows(lambda m, a, b, c: (((m + a.astype(F32)) + b.astype(F32)) + c.astype(F32),),
                [mine, got[0], got[1], got[2]], [], [(W, F32)], name="grad_chip_sum", tile=288)[0]
    return _share_halves(red)


def _adamw(w, g, m, v, name):
    shape = w.shape
    cols = shape[-1]

    def fn(wv, gv, mv, vv):
        m2 = ADAM_B1 * mv + (1.0 - ADAM_B1) * gv
        v2 = ADAM_B2 * vv + (1.0 - ADAM_B2) * jnp.square(gv)
        m_hat = m2 / (1.0 - ADAM_B1 ** ADAM_STEP)
        v_hat = v2 / (1.0 - ADAM_B2 ** ADAM_STEP)
        return -ADAM_LR * (m_hat / (jnp.sqrt(v_hat) + ADAM_EPS) + ADAM_WD * wv), m2, v2

    w2, m2, v2 = (t.reshape(-1, cols) for t in (w, m, v))
    rows = w2.shape[0]
    tile = 256 if rows % 8 == 0 else rows
    res = _rows(fn, [w2, g.reshape(rows, cols), m2, v2], [], [(cols, F32)] * 3, name=name, tile=tile)
    return tuple(t.reshape(shape) for t in res)


def _add_res(acc, r):
    return (r + acc,)


def _tail_fwd(h1, p16, W, i, tag):
    hn2 = _norm_fwd(h1, W["mlp_norm"][i:i + 1], f"{tag}_mlp_norm")
    z, a = _mm(hn2, W["mlp_w1"][i], bblk=True, outs=[F32, BF16], name=f"{tag}_mlp_w1",
               epilogue=lambda acc: (acc, jnp.square(jnp.maximum(acc, 0.0))))
    h2 = _mm(a, W["mlp_w2"][i], extras=[h1], epilogue=_add_res, name=f"{tag}_mlp_w2")
    hn3 = _norm_fwd(h2, W["ple_norm"][i:i + 1], f"{tag}_ple_norm")
    gl = _mm(hn3, W["ple_gate_w"][i], name=f"{tag}_ple_gate")
    h3, pp = _mm(p16[i], W["ple_proj_w"][i], bblk=True, extras=[gl, h2], outs=[F32, F32], name=f"{tag}_ple_proj",
                 epilogue=lambda acc, g, h: (h + _sigmoid(g) * acc, acc))
    return h3, (h1, hn2, z, a, h2, hn3, gl, pp)


def _tail_bwd(dh3, saved, p16, W, i, tag):
    h1, hn2, z, a, h2, hn3, gl, pp = saved

    def gate_bwd(d, g, ppv):
        gate = _sigmoid(g)
        return d * gate, d * ppv * gate * (1.0 - gate)

    dpp, dgl = _rows(gate_bwd, [dh3, gl, pp], [], [(D_MODEL, BF16), (D_MODEL, BF16)], name=f"{tag}_ple_gate_bwd")
    d_proj = _mm(p16[i], dpp, ta=True, oblk=PLE_DIM, name=f"{tag}_d_ple_proj")
    d_gate = _mm(hn3, dgl, ta=True, name=f"{tag}_d_ple_gate")
    dhn3 = _mm(dgl, W["ple_gate_w"][i], tb=True, name=f"{tag}_ple_gate_dx")
    dh2, d_ple_norm = _norm_bwd(h2, dhn3, W["ple_norm"][i:i + 1], dh3, f"{tag}_ple_norm_bwd")
    d_w2 = _mm(a, dh2, ta=True, name=f"{tag}_d_mlp_w2")
    dz = _mm(dh2, W["mlp_w2"][i], tb=True, extras=[z], outs=[BF16], name=f"{tag}_mlp_w2_dx",
             epilogue=lambda acc, zv: (acc * (2.0 * jnp.maximum(zv, 0.0)),))
    d_w1 = _mm(hn2, dz, ta=True, oblk=D_MODEL, name=f"{tag}_d_mlp_w1")
    dhn2 = _mm(dz, W["mlp_w1"][i], tb=True, bblk=True, name=f"{tag}_mlp_w1_dx")
    dh1, d_mlp_norm = _norm_bwd(h1, dhn2, W["mlp_norm"][i:i + 1], dh2, f"{tag}_mlp_norm_bwd")
    return dh1, dict(mlp_norm=d_mlp_norm, mlp_w1=d_w1, mlp_w2=d_w2, ple_norm=d_ple_norm,
                     ple_gate_w=d_gate, ple_proj_w=d_proj)


def _ret_layer_fwd(h0, W, tabs):
    hn = _norm_fwd(h0, W["mix_norm"][0:1], "ret_mix_norm")
    proj = _mm(hn, W["ret_w_in"], bblk=True, name="ret_w_in")
    out, states = _ret_fwd(proj, tabs, "ret_scan")
    y = _ret_gate(out, proj, W["ret_gn"], "ret_gate")
    h1 = _mm(y, W["ret_w_out"], extras=[h0], epilogue=_add_res, name="ret_w_out")
    return h1, (h0, hn, proj, out, states, y)


def _ret_layer_bwd(dh1, saved, W, tabs):
    h0, hn, proj, out, states, y = saved
    d_w_out = _mm(y, dh1, ta=True, name="d_ret_w_out")
    dy = _mm(dh1, W["ret_w_out"], tb=True, name="ret_w_out_dx")
    dout, dg, d_gn = _ret_gate_bwd(out, proj, W["ret_gn"], dy, "ret_gate_bwd")
    dq, dk, dv = _ret_bwd(proj, states, dout, tabs, "ret_scan_bwd")
    dproj = jnp.concatenate([dq, dk, dv, dg], axis=1)
    d_w_in = _mm(hn, dproj, ta=True, oblk=proj.shape[1] // N_CHIPS, name="d_ret_w_in")
    dhn = _mm(dproj, W["ret_w_in"], tb=True, bblk=True, name="ret_w_in_dx")
    dh0, d_mix = _norm_bwd(h0, dhn, W["mix_norm"][0:1], dh1, "ret_mix_norm_bwd")
    return dh0, dict(mix_norm=d_mix, ret_w_in=d_w_in, ret_gn=d_gn, ret_w_out=d_w_out)


def _mla_layer_fwd(h0, W, tabs):
    hn = _norm_fwd(h0, W["mix_norm"][1:2], "mla_mix_norm")
    proj = _mm(hn, W["mla_w_in"], name="mla_w_in")

    def low_rank_norm(pv, gq, gkv):
        return _rms(pv[:, :MLA_Q_RANK], gq), _rms(pv[:, MLA_Q_RANK:MLA_Q_RANK + MLA_KV_RANK], gkv)

    cqn, ckvn = _rows(low_rank_norm, [proj], [W["mla_q_a_norm"], W["mla_kv_a_norm"]],
                      [(MLA_Q_RANK, BF16), (MLA_KV_RANK, BF16)], name="mla_low_rank_norm")
    q = _mm(cqn, W["mla_w_uq"], name="mla_w_uq")
    kv = _mm(ckvn, W["mla_w_ukv"], bblk=True, name="mla_w_ukv")
    qf, kf, vf = _mla_prep(q, kv, proj, W["mla_q_norm"], W["mla_k_norm"], tabs, "mla_prep")
    o, lse = _flash_fwd(qf, kf, vf, "mla_flash")
    h1 = _mm(o, W["mla_w_out"], extras=[h0], epilogue=_add_res, name="mla_w_out")
    return h1, (h0, hn, proj, cqn, ckvn, q, kv, qf, kf, vf, o, lse)


def _mla_layer_bwd(dh1, saved, W, tabs):
    h0, hn, proj, cqn, ckvn, q, kv, qf, kf, vf, o, lse = saved
    d_w_out = _mm(o, dh1, ta=True, name="d_mla_w_out")
    do = _mm(dh1, W["mla_w_out"], tb=True, name="mla_w_out_dx")
    delta, do16 = _flash_delta(o, do, "mla_flash_delta")
    dqf, dkf, dvf = _flash_bwd(qf, kf, vf, do16, lse, delta, "mla_flash_bwd")
    dq, dkv, dkr, d_gq, d_gk = _mla_prep_bwd(q, kv, proj, W["mla_q_norm"], W["mla_k_norm"], tabs, dqf, dkf, dvf,
                                             "mla_prep_bwd")
    d_w_uq = _mm(cqn, dq, ta=True, name="d_mla_w_uq")
    dcqn = _mm(dq, W["mla_w_uq"], tb=True, name="mla_w_uq_dx")
    d_w_ukv = _mm(ckvn, dkv, ta=True, oblk=kv.shape[1] // N_CHIPS, name="d_mla_w_ukv")
    dckvn = _mm(dkv, W["mla_w_ukv"], tb=True, bblk=True, name="mla_w_ukv_dx")

    def low_rank_bwd(pv, dcq, dckv, dkr_v, gq, gkv):
        dxq, dgq = _rms_bwd(pv[:, :MLA_Q_RANK], dcq, gq)
        dxkv, dgkv = _rms_bwd(pv[:, MLA_Q_RANK:MLA_Q_RANK + MLA_KV_RANK], dckv, gkv)
        return jnp.concatenate([dxq, dxkv, dkr_v], axis=-1), _colsum(dgq), _colsum(dgkv)

    dproj, d_gqa, d_gkva = _rows(low_rank_bwd, [proj, dcqn, dckvn, dkr], [W["mla_q_a_norm"], W["mla_kv_a_norm"]],
                                 [(MLA_IN_PAD, BF16)], [((1, MLA_Q_RANK), F32), ((1, MLA_KV_RANK), F32)],
                                 name="mla_low_rank_norm_bwd")
    d_w_in = _mm(hn, dproj, ta=True, name="d_mla_w_in")
    dhn = _mm(dproj, W["mla_w_in"], tb=True, name="mla_w_in_dx")
    dh0, d_mix = _norm_bwd(h0, dhn, W["mix_norm"][1:2], dh1, "mla_mix_norm_bwd")
    return dh0, dict(mix_norm=d_mix, mla_w_in=d_w_in, mla_q_a_norm=d_gqa, mla_kv_a_norm=d_gkva, mla_w_uq=d_w_uq,
                     mla_w_ukv=d_w_ukv, mla_q_norm=d_gq, mla_k_norm=d_gk, mla_w_out=d_w_out)


def _local_step(x, p16, target, W):
    T = x.shape[0]
    ret_tabs, mla_tabs = _ret_tables(T), _mla_tables(T)
    h1, s_ret = _ret_layer_fwd(x, W, ret_tabs)
    h3, s_tail0 = _tail_fwd(h1, p16, W, 0, "l0")
    h4, s_mla = _mla_layer_fwd(h3, W, mla_tabs)
    y, s_tail1 = _tail_fwd(h4, p16, W, 1, "l1")

    def loss_head(yv, tv):
        e = yv - tv
        return e * (1.0 / D_MODEL), jnp.full((1, 128), 0.5 / D_MODEL, F32) * jnp.sum(e * e)

    dy, loss = _rows(loss_head, [y, target], [], [(D_MODEL, F32)], [((1, 128), F32)], name="loss_head")
    dh4, g_tail1 = _tail_bwd(dy, s_tail1, p16, W, 1, "l1")
    dh3, g_mla = _mla_layer_bwd(dh4, s_mla, W, mla_tabs)
    dh1, g_tail0 = _tail_bwd(dh3, s_tail0, p16, W, 0, "l0")
    dx, g_ret = _ret_layer_bwd(dh1, s_ret, W, ret_tabs)
    return loss, dx, g_ret, g_tail0, g_mla, g_tail1


_BIG = ("ret_w_in", "ret_w_out", "mla_w_in", "mla_w_uq", "mla_w_ukv", "mla_w_out", "mlp_w1", "mlp_w2",
        "ple_gate_w", "ple_proj_w")
_SMALL = ("mix_norm", "ret_gn", "mla_q_a_norm", "mla_kv_a_norm", "mla_q_norm", "mla_k_norm", "mlp_norm", "ple_norm")
_ORDER = ("mix_norm", "ret_w_in", "ret_gn", "ret_w_out", "mla_w_in", "mla_q_a_norm", "mla_kv_a_norm", "mla_w_uq",
          "mla_w_ukv", "mla_q_norm", "mla_k_norm", "mla_w_out", "mlp_norm", "mlp_w1", "mlp_w2", "ple_norm",
          "ple_gate_w", "ple_proj_w")
SMALL_ROWS = 16


def _pack_rows(shards):
    return [shards[n].size // PACK_W for n in _BIG]


def _split_rows(buf, shards):
    out, off = {}, 0
    for n, r in zip(_BIG, _pack_rows(shards)):
        out[n] = buf[..., off:off + r, :]
        off += r
    return out


def _full_weights(gath, shards, small):
    b = _split_rows(gath, shards)
    n = N_CHIPS
    W = {}
    W["ret_w_in"] = b["ret_w_in"].reshape(n, D_MODEL, -1)
    W["ret_w_out"] = b["ret_w_out"].reshape(-1, D_MODEL)
    W["mla_w_in"] = jnp.pad(b["mla_w_in"].reshape(D_MODEL, MLA_IN), ((0, 0), (0, MLA_IN_PAD - MLA_IN)))
    uq = b["mla_w_uq"].reshape(n, MLA_Q_RANK, MLA_HEADS // n, MLA_QKD).transpose(1, 0, 2, 3)
    W["mla_w_uq"] = jnp.pad(uq, ((0, 0), (0, 0), (0, 0), (0, MLA_HP - MLA_QKD))).reshape(MLA_Q_RANK, MLA_HEADS * MLA_HP)
    W["mla_w_ukv"] = b["mla_w_ukv"].reshape(n, MLA_KV_RANK, -1)
    W["mla_w_out"] = b["mla_w_out"].reshape(-1, D_MODEL)
    w1 = b["mlp_w1"].reshape(n, 2, D_MODEL, -1)
    W["mlp_w1"] = [w1[:, i] for i in range(2)]
    w2 = b["mlp_w2"].reshape(n, 2, -1, D_MODEL)
    W["mlp_w2"] = [w2[:, i].reshape(-1, D_MODEL) for i in range(2)]
    wg = b["ple_gate_w"].reshape(n, 2, -1, D_MODEL)
    W["ple_gate_w"] = [wg[:, i].reshape(-1, D_MODEL) for i in range(2)]
    wp = b["ple_proj_w"].reshape(n, 2, PLE_DIM, -1)
    W["ple_proj_w"] = [wp[:, i] for i in range(2)]
    W["ret_gn"] = small[0:2].reshape(RET_HEADS, RET_DV)
    W["mla_q_a_norm"] = small[2:3, :MLA_Q_RANK]
    W["mla_kv_a_norm"] = small[3:4, :MLA_KV_RANK]
    return W


def _pack_grads(g_ret, g_tail0, g_mla, g_tail1):
    n = N_CHIPS

    def both(name):
        return jnp.stack([g_tail0[name].reshape(n, -1, PACK_W), g_tail1[name].reshape(n, -1, PACK_W)], axis=1)

    uq = g_mla["mla_w_uq"].reshape(MLA_Q_RANK, n, MLA_HEADS // n, MLA_HP)[..., :MLA_QKD].transpose(1, 0, 2, 3)
    parts = dict(
        ret_w_in=g_ret["ret_w_in"], ret_w_out=g_ret["ret_w_out"], mla_w_in=g_mla["mla_w_in"][:, :MLA_IN],
        mla_w_uq=uq, mla_w_ukv=g_mla["mla_w_ukv"], mla_w_out=g_mla["mla_w_out"], mlp_w1=both("mlp_w1"),
        mlp_w2=both("mlp_w2"), ple_gate_w=both("ple_gate_w"), ple_proj_w=both("ple_proj_w"))
    return jnp.concatenate([parts[k].astype(BF16).reshape(n, -1, PACK_W) for k in _BIG], axis=1)


def _pad_row(v):
    v = v.reshape(1, -1)
    return jnp.pad(v, ((0, 0), (0, PACK_W - v.shape[1])))


def kernel(x, p, mix_norm, ret_w_in, ret_gn, ret_w_out, mla_w_in, mla_q_a_norm, mla_kv_a_norm, mla_w_uq, mla_w_ukv, mla_q_norm, mla_k_norm, mla_w_out, mlp_norm, mlp_w1, mlp_w2, ple_norm, ple_gate_w, ple_proj_w, loss_target, m_mix_norm, m_ret_w_in, m_ret_gn, m_ret_w_out, m_mla_w_in, m_mla_q_a_norm, m_mla_kv_a_norm, m_mla_w_uq, m_mla_w_ukv, m_mla_q_norm, m_mla_k_norm, m_mla_w_out, m_mlp_norm, m_mlp_w1, m_mlp_w2, m_ple_norm, m_ple_gate_w, m_ple_proj_w, v_mix_norm, v_ret_w_in, v_ret_gn, v_ret_w_out, v_mla_w_in, v_mla_q_a_norm, v_mla_kv_a_norm, v_mla_w_uq, v_mla_w_ukv, v_mla_q_norm, v_mla_k_norm, v_mla_w_out, v_mlp_norm, v_mlp_w1, v_mlp_w2, v_ple_norm, v_ple_gate_w, v_ple_proj_w):
    w = dict(mix_norm=mix_norm, ret_w_in=ret_w_in, ret_gn=ret_gn, ret_w_out=ret_w_out, mla_w_in=mla_w_in,
             mla_q_a_norm=mla_q_a_norm, mla_kv_a_norm=mla_kv_a_norm, mla_w_uq=mla_w_uq, mla_w_ukv=mla_w_ukv,
             mla_q_norm=mla_q_norm, mla_k_norm=mla_k_norm, mla_w_out=mla_w_out, mlp_norm=mlp_norm, mlp_w1=mlp_w1,
             mlp_w2=mlp_w2, ple_norm=ple_norm, ple_gate_w=ple_gate_w, ple_proj_w=ple_proj_w)
    m = dict(mix_norm=m_mix_norm, ret_w_in=m_ret_w_in, ret_gn=m_ret_gn, ret_w_out=m_ret_w_out, mla_w_in=m_mla_w_in,
             mla_q_a_norm=m_mla_q_a_norm, mla_kv_a_norm=m_mla_kv_a_norm, mla_w_uq=m_mla_w_uq, mla_w_ukv=m_mla_w_ukv,
             mla_q_norm=m_mla_q_norm, mla_k_norm=m_mla_k_norm, mla_w_out=m_mla_w_out, mlp_norm=m_mlp_norm,
             mlp_w1=m_mlp_w1, mlp_w2=m_mlp_w2, ple_norm=m_ple_norm, ple_gate_w=m_ple_gate_w, ple_proj_w=m_ple_proj_w)
    v = dict(mix_norm=v_mix_norm, ret_w_in=v_ret_w_in, ret_gn=v_ret_gn, ret_w_out=v_ret_w_out, mla_w_in=v_mla_w_in,
             mla_q_a_norm=v_mla_q_a_norm, mla_kv_a_norm=v_mla_kv_a_norm, mla_w_uq=v_mla_w_uq, mla_w_ukv=v_mla_w_ukv,
             mla_q_norm=v_mla_q_norm, mla_k_norm=v_mla_k_norm, mla_w_out=v_mla_w_out, mlp_norm=v_mlp_norm,
             mlp_w1=v_mlp_w1, mlp_w2=v_mlp_w2, ple_norm=v_ple_norm, ple_gate_w=v_ple_gate_w, ple_proj_w=v_ple_proj_w)
    xi, yi, ci = _place()
    chip = 2 * xi + yi
    n = N_CHIPS

    wsh = jnp.concatenate([w[k].astype(BF16).reshape(-1, PACK_W) for k in _BIG], axis=0)
    gath = _gather_weights(wsh)
    on = (jnp.arange(n) == chip) & (ci == 0)
    gn_rows = jnp.where(on[None, :, None], ret_gn[0][:, None, :], 0.0).reshape(2, PACK_W)
    qa_row = _pad_row(jnp.where(on[:, None], mla_q_a_norm, 0.0))
    kva_row = _pad_row(jnp.where(on[:, None], mla_kv_a_norm, 0.0))
    small_in = jnp.concatenate([gn_rows, qa_row, kva_row, jnp.zeros((4, PACK_W), F32)], axis=0)
    small = _allsum_small(small_in, "gather_gains")
    W = _full_weights(gath, {k: w[k] for k in _BIG}, small)
    W["mix_norm"], W["mlp_norm"], W["ple_norm"] = mix_norm, mlp_norm, ple_norm
    W["mla_q_norm"] = jnp.pad(mla_q_norm, ((0, 0), (0, MLA_HP - MLA_QKD)))
    W["mla_k_norm"] = jnp.pad(mla_k_norm, ((0, 0), (0, MLA_HP - MLA_QKD)))

    loss, dx, g_ret, g_tail0, g_mla, g_tail1 = _local_step(x[0], p[:, 0].astype(BF16), loss_target[0], W)

    red = _reduce_grads(_pack_grads(g_ret, g_tail0, g_mla, g_tail1))
    g_big = _split_rows(red, {k: w[k] for k in _BIG})
    small_g = jnp.concatenate([
        g_ret["mix_norm"], g_mla["mix_norm"], g_tail0["mlp_norm"], g_tail1["mlp_norm"], g_tail0["ple_norm"],
        g_tail1["ple_norm"], g_ret["ret_gn"].reshape(2, PACK_W), _pad_row(g_mla["mla_q_a_norm"]),
        _pad_row(g_mla["mla_kv_a_norm"]), _pad_row(g_mla["mla_q_norm"][:, :MLA_QKD]),
        _pad_row(g_mla["mla_k_norm"][:, :MLA_QKD]), _pad_row(loss[:, :1]), jnp.zeros((3, PACK_W), F32)], axis=0)
    tot = _allsum_small(small_g, "sum_small_grads")
    gn_all = tot[6:8].reshape(RET_HEADS, n, -1)
    g_small = dict(
        mix_norm=tot[0:2], mlp_norm=tot[2:4], ple_norm=tot[4:6],
        ret_gn=lax.dynamic_index_in_dim(gn_all, chip, axis=1, keepdims=False),
        mla_q_a_norm=lax.dynamic_index_in_dim(tot[8, :MLA_Q_RANK].reshape(n, -1), chip, axis=0, keepdims=True),
        mla_kv_a_norm=lax.dynamic_index_in_dim(tot[9, :MLA_KV_RANK].reshape(n, -1), chip, axis=0, keepdims=True),
        mla_q_norm=tot[10:11, :MLA_QKD], mla_k_norm=tot[11:12, :MLA_QKD])
    loss_out = tot[12, 0]

    grads, deltas, new_m, new_v = [], [], [], []
    for k in _ORDER:
        g = g_big[k] if k in g_big else g_small[k]
        g = g.reshape(w[k].shape)
        d, m2, v2 = _adamw(w[k], g, m[k], v[k], f"adamw_{k}")
        grads.append(g)
        deltas.append(d)
        new_m.append(m2)
        new_v.append(v2)
    return (loss_out, dx[None], *grads, *deltas, *new_m, *new_v)
```

```python
import functools

import jax
import jax.numpy as jnp
from jax import lax
from jax.experimental import pallas as pl
from jax.experimental.pallas import tpu as pltpu

F32 = jnp.float32
BF16 = jnp.bfloat16

EPS = 1e-6
D_MODEL = 1024
CHUNK = 64
ROPE_THETA = 10000.0
RET_HEADS = 4
RET_DK = 256
RET_DV = 512
MLA_HEADS = 8
MLA_NOPE = 128
MLA_ROPE = 64
MLA_QKD = 192
MLA_VD = 128
MLA_HP = 256
MLA_Q_RANK = 384
MLA_KV_RANK = 256
MLA_IN = 704
MLA_IN_PAD = 768
D_FF = 4096
PLE_DIM = 256
N_CHIPS = 4

ADAM_LR = 0.001
ADAM_B1 = 0.9
ADAM_B2 = 0.999
ADAM_EPS = 1e-08
ADAM_WD = 0.01
ADAM_STEP = 10

VMEM_LIMIT = 56 * 1024 * 1024
PACK_W = 1024
NEG = -1e30
FLASH_T = 512
COMM_CHUNKS = 9


def _cparams(sem=None):
    return pltpu.CompilerParams(dimension_semantics=sem, vmem_limit_bytes=VMEM_LIMIT)


def _pick(dim, pref):
    if dim <= pref:
        return dim
    t = pref
    while dim % t:
        t //= 2
    return t


def _mm(a, b, *, name, ta=False, tb=False, bblk=False, oblk=None, outs=None, extras=(), epilogue=None,
        tm=1024, tn=512, tk=1024):
    if ta:
        K, M = a.shape
    else:
        M, K = a.shape
    if bblk:
        if tb:
            _, N, Kq = b.shape
            assert Kq * N_CHIPS == K
        else:
            _, Kb, Nq_b = b.shape
            N = Nq_b * N_CHIPS
            assert Kb == K
    else:
        N = b.shape[0] if tb else b.shape[1]
    tm = _pick(M, tm)
    tn = _pick(N if not (bblk and not tb) else b.shape[2], tn)
    if oblk is not None:
        tn = _pick(oblk, tn)
    tk = _pick(K if not (bblk and tb) else b.shape[2], tk)
    nk = K // tk
    grid = (M // tm, N // tn, nk)

    if ta:
        a_spec = pl.BlockSpec((tk, tm), lambda i, j, k: (k, i))
    else:
        a_spec = pl.BlockSpec((tm, tk), lambda i, j, k: (i, k))
    if bblk and tb:
        kpb = b.shape[2] // tk
        b_spec = pl.BlockSpec((None, tn, tk), lambda i, j, k: (k // kpb, j, k % kpb))
    elif bblk:
        npb = b.shape[2] // tn
        b_spec = pl.BlockSpec((None, tk, tn), lambda i, j, k: (j // npb, k, j % npb))
    elif tb:
        b_spec = pl.BlockSpec((tn, tk), lambda i, j, k: (j, k))
    else:
        b_spec = pl.BlockSpec((tk, tn), lambda i, j, k: (k, j))
    e_specs = [pl.BlockSpec((tm, tn), lambda i, j, k: (i, j)) for _ in extras]
    if outs is None:
        outs = [F32]
    if oblk is not None:
        opb = oblk // tn
        o_specs = [pl.BlockSpec((None, tm, tn), lambda i, j, k: (j // opb, i, j % opb)) for _ in outs]
        o_shapes = [jax.ShapeDtypeStruct((N_CHIPS, M, oblk), dt) for dt in outs]
    else:
        o_specs = [pl.BlockSpec((tm, tn), lambda i, j, k: (i, j)) for _ in outs]
        o_shapes = [jax.ShapeDtypeStruct((M, N), dt) for dt in outs]
    n_e, n_o = len(extras), len(outs)
    if ta:
        dims = (((0,), (0,)), ((), ()))
    elif tb:
        dims = (((1,), (1,)), ((), ()))
    else:
        dims = (((1,), (0,)), ((), ()))

    def body(a_ref, b_ref, *rest):
        e_refs, o_refs, acc = rest[:n_e], rest[n_e:n_e + n_o], rest[n_e + n_o]
        k = pl.program_id(2)
        part = lax.dot_general(a_ref[...].astype(BF16), b_ref[...].astype(BF16), dims,
                               preferred_element_type=F32)

        @pl.when(k == 0)
        def _():
            acc[...] = part

        @pl.when(k > 0)
        def _():
            acc[...] += part

        @pl.when(k == nk - 1)
        def _():
            res = acc[...]
            vals = (res,) if epilogue is None else epilogue(res, *[e[...] for e in e_refs])
            for o, v in zip(o_refs, vals):
                o[...] = v.astype(o.dtype)

    res = pl.pallas_call(
        body, name=name, grid=grid,
        in_specs=[a_spec, b_spec, *e_specs], out_specs=o_specs, out_shape=o_shapes,
        scratch_shapes=[pltpu.VMEM((tm, tn), F32)],
        compiler_params=_cparams(("parallel", "parallel", "arbitrary")),
    )(a, b, *extras)
    return res[0] if n_o == 1 else res


def _rows(fn, rows, fulls, outs, accs=(), *, name, tile=256):
    first = rows[0][0] if isinstance(rows[0], tuple) else rows[0]
    T = first.shape[0]
    tile = _pick(T, tile)
    in_specs, args = [], []
    for r in rows:
        if isinstance(r, tuple):
            arr, w, cb = r
            in_specs.append(pl.BlockSpec((tile, w), lambda i, cb=cb: (i, cb)))
        else:
            arr = r
            in_specs.append(pl.BlockSpec((tile, arr.shape[1]), lambda i: (i, 0)))
        args.append(arr)
    for f in fulls:
        in_specs.append(pl.BlockSpec(f.shape, lambda i, nd=f.ndim: (0,) * nd))
        args.append(f)
    out_specs = [pl.BlockSpec((tile, w), lambda i: (i, 0)) for w, _ in outs]
    out_specs += [pl.BlockSpec(s, lambda i: (0, 0)) for s, _ in accs]
    out_shape = [jax.ShapeDtypeStruct((T, w), dt) for w, dt in outs]
    out_shape += [jax.ShapeDtypeStruct(s, dt) for s, dt in accs]
    n_in, n_out = len(args), len(outs)

    def body(*refs):
        vals = fn(*[r[...] for r in refs[:n_in]])
        o_refs = refs[n_in:]
        for o, v in zip(o_refs[:n_out], vals[:n_out]):
            o[...] = v.astype(o.dtype)
        first_step = pl.program_id(0) == 0
        for o, v in zip(o_refs[n_out:], vals[n_out:]):
            @pl.when(first_step)
            def _(o=o, v=v):
                o[...] = v.astype(o.dtype)

            @pl.when(jnp.logical_not(first_step))
            def _(o=o, v=v):
                o[...] += v.astype(o.dtype)

    res = pl.pallas_call(
        body, name=name, grid=(T // tile,), in_specs=in_specs, out_specs=out_specs, out_shape=out_shape,
        compiler_params=_cparams(("arbitrary",)),
    )(*args)
    return res


def _rms(x, g):
    r = lax.rsqrt(jnp.mean(x * x, axis=-1, keepdims=True) + EPS)
    return (x * r) * g


def _rms_bwd(x, dy, g, n=None):
    n = x.shape[-1] if n is None else n
    r = lax.rsqrt(jnp.sum(x * x, axis=-1, keepdims=True) / n + EPS)
    xh = x * r
    dxh = dy * g
    dx = r * (dxh - xh * (jnp.sum(dxh * xh, axis=-1, keepdims=True) / n))
    return dx, dy * xh


def _colsum(v):
    return jnp.sum(v, axis=0, keepdims=True)


def _sigmoid(x):
    return 1.0 / (1.0 + jnp.exp(-x))


def _widen(v, width):
    reps = width // v.shape[1]
    return v if reps == 1 else jnp.concatenate([v] * reps, axis=-1)


def _norm_fwd(h, gain, name):
    return _rows(lambda x, g: (_rms(x, g),), [h], [gain], [(h.shape[1], BF16)], name=name)[0]


def _norm_bwd(h, dhn, gain, dres, name):
    def fn(x, dy, dr, g):
        dx, dg = _rms_bwd(x, dy, g)
        return dr + dx, _colsum(dg)
    d = h.shape[1]
    return _rows(fn, [h, dhn, dres], [gain], [(d, F32)], [((1, d), F32)], name=name)


def _ret_tables(T):
    inv = 1.0 / (ROPE_THETA ** (jnp.arange(0, RET_DK, 2, dtype=F32) / RET_DK))
    ang = jnp.arange(T, dtype=F32)[:, None] * inv[None, :]
    log_gamma = jnp.log(1.0 - 2.0 ** (-5.0 - jnp.arange(RET_HEADS, dtype=F32)))
    idx = jnp.arange(CHUNK, dtype=F32)
    intra = jnp.exp(log_gamma[:, None, None] * jnp.abs(idx[:, None] - idx[None, :]))
    qd = jnp.exp(log_gamma[:, None] * (idx + 1.0))[:, :, None]
    kd = jnp.exp(log_gamma[:, None] * (CHUNK - 1.0 - idx))[:, :, None]
    cd = jnp.exp(log_gamma * CHUNK)[:, None, None]
    return jnp.cos(ang), jnp.sin(ang), intra, qd, kd, cd


def _rope_half(x, c, s):
    x1, x2 = x[:, :RET_DK // 2], x[:, RET_DK // 2:]
    return jnp.concatenate([x1 * c - x2 * s, x2 * c + x1 * s], axis=-1)


def _rope_half_bwd(d, c, s):
    d1, d2 = d[:, :RET_DK // 2], d[:, RET_DK // 2:]
    return jnp.concatenate([d1 * c + d2 * s, d2 * c - d1 * s], axis=-1)


def _dot(a, b):
    return lax.dot_general(a, b, (((1,), (0,)), ((), ())), preferred_element_type=F32)


def _dot_nt(a, b):
    return lax.dot_general(a, b, (((1,), (1,)), ((), ())), preferred_element_type=F32)


def _dot_tn(a, b):
    return lax.dot_general(a, b, (((0,), (0,)), ((), ())), preferred_element_type=F32)


def _ret_specs(T, tb, rev):
    nj = T // tb
    jj = (lambda j: nj - 1 - j) if rev else (lambda j: j)
    kq = RET_HEADS
    vq = 2 * RET_HEADS * RET_DK // RET_DV
    return dict(
        q=pl.BlockSpec((tb, RET_DK), lambda h, j: (jj(j), h)),
        k=pl.BlockSpec((tb, RET_DK), lambda h, j: (jj(j), kq + h)),
        v=pl.BlockSpec((tb, RET_DV), lambda h, j: (jj(j), vq + h)),
        tab=pl.BlockSpec((tb, RET_DK // 2), lambda h, j: (jj(j), 0)),
        intra=pl.BlockSpec((None, CHUNK, CHUNK), lambda h, j: (h, 0, 0)),
        dec=pl.BlockSpec((None, CHUNK, 1), lambda h, j: (h, 0, 0)),
        cd=pl.BlockSpec((None, 1, 1), lambda h, j: (h, 0, 0)),
        o=pl.BlockSpec((tb, RET_DV), lambda h, j: (jj(j), h)),
        s=pl.BlockSpec((None, tb // CHUNK, RET_DK, RET_DV), lambda h, j: (h, jj(j), 0, 0)),
    )


def _ret_fwd(proj, tabs, name):
    T = proj.shape[0]
    cos, sin, intra, qd, kd, cd = tabs
    tb = _pick(T, 512)
    cps = tb // CHUNK
    sp = _ret_specs(T, tb, False)
    scale = RET_DK ** -0.5

    def body(q_ref, k_ref, v_ref, cos_ref, sin_ref, intra_ref, qd_ref, kd_ref, cd_ref, o_ref, s_ref, state):
        @pl.when(pl.program_id(1) == 0)
        def _():
            state[...] = jnp.zeros_like(state)

        for c in range(cps):
            rows = pl.ds(c * CHUNK, CHUNK)
            co, si = cos_ref[rows, :], sin_ref[rows, :]
            q = _rope_half(q_ref[rows, :], co, si)
            k = _rope_half(k_ref[rows, :], co, si) * scale
            vb = v_ref[rows, :].astype(BF16)
            st = state[...]
            sb = st.astype(BF16)
            s_ref[c] = sb
            sc = _dot_nt(q.astype(BF16), k.astype(BF16)) * intra_ref[...]
            inner = _dot(sc.astype(BF16), vb)
            cross = _dot((q * qd_ref[...]).astype(BF16), sb)
            o_ref[rows, :] = inner + cross
            state[...] = st * cd_ref[...] + _dot_tn((k * kd_ref[...]).astype(BF16), vb)

    return pl.pallas_call(
        body, name=name, grid=(RET_HEADS, T // tb),
        in_specs=[sp["q"], sp["k"], sp["v"], sp["tab"], sp["tab"], sp["intra"], sp["dec"], sp["dec"], sp["cd"]],
        out_specs=[sp["o"], sp["s"]],
        out_shape=[jax.ShapeDtypeStruct((T, RET_HEADS * RET_DV), F32),
                   jax.ShapeDtypeStruct((RET_HEADS, T // CHUNK, RET_DK, RET_DV), BF16)],
        scratch_shapes=[pltpu.VMEM((RET_DK, RET_DV), F32)],
        compiler_params=_cparams(("arbitrary", "arbitrary")),
    )(proj, proj, proj, cos, sin, intra, qd, kd, cd)


def _ret_bwd(proj, states, dout, tabs, name):
    T = proj.shape[0]
    cos, sin, intra, qd, kd, cd = tabs
    tb = _pick(T, 512)
    cps = tb // CHUNK
    sp = _ret_specs(T, tb, True)
    scale = RET_DK ** -0.5

    def body(q_ref, k_ref, v_ref, cos_ref, sin_ref, intra_ref, qd_ref, kd_ref, cd_ref, s_ref, do_ref,
             dq_ref, dk_ref, dv_ref, dstate):
        @pl.when(pl.program_id(1) == 0)
        def _():
            dstate[...] = jnp.zeros_like(dstate)

        for c in reversed(range(cps)):
            rows = pl.ds(c * CHUNK, CHUNK)
            co, si = cos_ref[rows, :], sin_ref[rows, :]
            q = _rope_half(q_ref[rows, :], co, si)
            k = _rope_half(k_ref[rows, :], co, si) * scale
            qb, kb = q.astype(BF16), k.astype(BF16)
            vb = v_ref[rows, :].astype(BF16)
            dob = do_ref[rows, :].astype(BF16)
            sb = s_ref[c]
            ia = intra_ref[...]
            pb = (_dot_nt(qb, kb) * ia).astype(BF16)
            dsn = dstate[...]
            dsb = dsn.astype(BF16)
            kdk = (k * kd_ref[...]).astype(BF16)
            qdq = (q * qd_ref[...]).astype(BF16)
            dv = _dot_tn(pb, dob) + _dot(kdk, dsb)
            dpb = (_dot_nt(dob, vb) * ia).astype(BF16)
            dq = _dot(dpb, kb) + _dot_nt(dob, sb) * qd_ref[...]
            dk = _dot_tn(dpb, qb) + _dot_nt(vb, dsb) * kd_ref[...]
            dstate[...] = dsn * cd_ref[...] + _dot_tn(qdq, dob)
            dq_ref[rows, :] = _rope_half_bwd(dq, co, si).astype(BF16)
            dk_ref[rows, :] = _rope_half_bwd(dk * scale, co, si).astype(BF16)
            dv_ref[rows, :] = dv.astype(BF16)

    return pl.pallas_call(
        body, name=name, grid=(RET_HEADS, T // tb),
        in_specs=[sp["q"], sp["k"], sp["v"], sp["tab"], sp["tab"], sp["intra"], sp["dec"], sp["dec"], sp["cd"],
                  sp["s"], sp["o"]],
        out_specs=[sp["q"], sp["q"], sp["o"]],
        out_shape=[jax.ShapeDtypeStruct((T, RET_HEADS * RET_DK), BF16),
                   jax.ShapeDtypeStruct((T, RET_HEADS * RET_DK), BF16),
                   jax.ShapeDtypeStruct((T, RET_HEADS * RET_DV), BF16)],
        scratch_shapes=[pltpu.VMEM((RET_DK, RET_DV), F32)],
        compiler_params=_cparams(("arbitrary", "arbitrary")),
    )(proj, proj, proj, cos, sin, intra, qd, kd, cd, states, dout)


def _ret_gate(out, proj, gn, name):
    def fn(o, g, *gains):
        parts = [_rms(o[:, h * RET_DV:(h + 1) * RET_DV], gains[h]) for h in range(RET_HEADS)]
        return (g * _sigmoid(g) * jnp.concatenate(parts, axis=-1),)
    w = RET_HEADS * RET_DV
    return _rows(fn, [out, (proj, w, 2)], [gn[h:h + 1] for h in range(RET_HEADS)], [(w, BF16)], name=name)[0]


def _ret_gate_bwd(out, proj, gn, dy, name):
    def fn(o, g, d, *gains):
        sg = _sigmoid(g)
        silu = g * sg
        dsilu = sg * (1.0 + g * (1.0 - sg))
        dos, dgs = [], []
        row = lax.broadcasted_iota(jnp.int32, (RET_HEADS, RET_DV), 0)
        dgn = jnp.zeros((RET_HEADS, RET_DV), F32)
        for h in range(RET_HEADS):
            sl = slice(h * RET_DV, (h + 1) * RET_DV)
            oh = o[:, sl]
            dgs.append(d[:, sl] * _rms(oh, gains[h]) * dsilu[:, sl])
            dx, dg = _rms_bwd(oh, d[:, sl] * silu[:, sl], gains[h])
            dos.append(dx)
            dgn = dgn + jnp.where(row == h, _colsum(dg), 0.0)
        return jnp.concatenate(dos, axis=-1), jnp.concatenate(dgs, axis=-1), dgn
    w = RET_HEADS * RET_DV
    return _rows(fn, [out, (proj, w, 2), dy], [gn[h:h + 1] for h in range(RET_HEADS)], [(w, BF16), (w, BF16)],
                 [((RET_HEADS, RET_DV), F32)], name=name, tile=128)


def _mla_tables(T):
    inv = 1.0 / (ROPE_THETA ** (jnp.arange(0, MLA_ROPE, 2, dtype=F32) / MLA_ROPE))
    ang = jnp.arange(T, dtype=F32)[:, None] * inv[None, :]
    c, s = jnp.cos(ang), jnp.sin(ang)
    z32, z64 = jnp.zeros((T, 32), F32), jnp.zeros((T, 64), F32)
    cos_t = jnp.concatenate([c, c, z64], axis=1)
    sin_a = jnp.concatenate([-s, z32, z64], axis=1)
    sin_b = jnp.concatenate([z32, s, z64], axis=1)
    return cos_t, sin_a, sin_b


def _rope_blk(x, ct, sa, sb):
    return x * ct + pltpu.roll(x, 96, 1) * sa + pltpu.roll(x, 32, 1) * sb


def _rope_blk_bwd(d, ct, sa, sb):
    return d * ct + pltpu.roll(d * sa, 32, 1) + pltpu.roll(d * sb, 96, 1)


def _head_norm(x, gain):
    r = lax.rsqrt(jnp.sum(x * x, axis=-1, keepdims=True) / MLA_QKD + EPS)
    return (x * r) * gain


def _mla_prep(q, kv, proj, gq, gk, tabs, name):
    def fn(qv, kvv, kr, ct, sa, sb, gqv, gkv):
        qs, ks, vs = [], [], []
        for h in range(MLA_HEADS):
            b = h * MLA_HP
            y = _head_norm(qv[:, b:b + MLA_HP], gqv)
            qs += [y[:, :128], _rope_blk(y[:, 128:], ct, sa, sb)]
            y = _head_norm(jnp.concatenate([kvv[:, b:b + 128], kr], axis=-1), gkv)
            ks += [y[:, :128], _rope_blk(y[:, 128:], ct, sa, sb)]
            vs.append(kvv[:, b + 128:b + 256])
        return jnp.concatenate(qs, axis=-1), jnp.concatenate(ks, axis=-1), jnp.concatenate(vs, axis=-1)
    w = MLA_HEADS * MLA_HP
    return _rows(fn, [q, kv, (proj, 128, 5), *tabs], [gq, gk],
                 [(w, BF16), (w, BF16), (MLA_HEADS * MLA_VD, BF16)], name=name, tile=128)


def _mla_prep_bwd(q, kv, proj, gq, gk, tabs, dqf, dkf, dvf, name):
    def fn(qv, kvv, kr, ct, sa, sb, dqv, dkv, dvv, gqv, gkv):
        dqs, dkvs = [], []
        dkr = jnp.zeros_like(kr)
        dgq = jnp.zeros((1, MLA_HP), F32)
        dgk = jnp.zeros((1, MLA_HP), F32)
        for h in range(MLA_HEADS):
            b = h * MLA_HP
            dy = jnp.concatenate([dqv[:, b:b + 128], _rope_blk_bwd(dqv[:, b + 128:b + 256], ct, sa, sb)], axis=-1)
            dx, dg = _rms_bwd(qv[:, b:b + MLA_HP], dy, gqv, MLA_QKD)
            dqs.append(dx)
            dgq = dgq + _colsum(dg)
            dy = jnp.concatenate([dkv[:, b:b + 128], _rope_blk_bwd(dkv[:, b + 128:b + 256], ct, sa, sb)], axis=-1)
            dx, dg = _rms_bwd(jnp.concatenate([kvv[:, b:b + 128], kr], axis=-1), dy, gkv, MLA_QKD)
            dkvs += [dx[:, :128], dvv[:, h * MLA_VD:(h + 1) * MLA_VD]]
            dkr = dkr + dx[:, 128:]
            dgk = dgk + _colsum(dg)
        return jnp.concatenate(dqs, axis=-1), jnp.concatenate(dkvs, axis=-1), dkr, dgq, dgk
    w = MLA_HEADS * MLA_HP
    return _rows(fn, [q, kv, (proj, 128, 5), *tabs, dqf, dkf, dvf], [gq, gk],
                 [(w, BF16), (w, BF16), (128, F32)], [((1, MLA_HP), F32), ((1, MLA_HP), F32)], name=name, tile=128)


def _chunk_mask(qi, ki, tq, tk):
    shift = CHUNK.bit_length() - 1
    rq = lax.shift_right_arithmetic(qi * tq + lax.broadcasted_iota(jnp.int32, (tq, tk), 0), shift)
    ck = lax.shift_right_arithmetic(ki * tk + lax.broadcasted_iota(jnp.int32, (tq, tk), 1), shift)
    return ck <= rq


def _flash_fwd(qf, kf, vf, name):
    T = qf.shape[0]
    t = _pick(T, FLASH_T)
    n = T // t
    scale = MLA_QKD ** -0.5

    def body(q_ref, k_ref, v_ref, o_ref, lse_ref, m_s, l_s, acc):
        qi = pl.program_id(1)
        q = q_ref[...]
        m_s[...] = jnp.full_like(m_s, NEG)
        l_s[...] = jnp.zeros_like(l_s)
        acc[...] = jnp.zeros_like(acc)

        def step(kb, masked):
            rows = pl.ds(pl.multiple_of(kb * t, t), t)
            s = _dot_nt(q, k_ref[rows, :]) * scale
            if masked:
                s = jnp.where(_chunk_mask(0, 0, t, t), s, NEG)
            m_prev = m_s[...]
            m_new = jnp.maximum(m_prev, jnp.max(s, axis=-1, keepdims=True))
            alpha = jnp.exp(m_prev - m_new)
            p = jnp.exp(s - _widen(m_new, t))
            l_s[...] = alpha * l_s[...] + sum(p[:, i * 128:(i + 1) * 128] for i in range(t // 128))
            acc[...] = acc[...] * alpha + _dot(p.astype(BF16), v_ref[rows, :])
            m_s[...] = m_new

        @pl.loop(0, qi)
        def _(kb):
            step(kb, False)

        step(qi, True)
        l = jnp.sum(l_s[...], axis=-1, keepdims=True)
        o_ref[...] = acc[...] / l
        lse_ref[...] = m_s[...] + jnp.log(l)

    qmap = lambda h, i: (i, h)
    kmap = lambda h, i: (0, h)
    return pl.pallas_call(
        body, name=name, grid=(MLA_HEADS, n),
        in_specs=[pl.BlockSpec((t, MLA_HP), qmap), pl.BlockSpec((T, MLA_HP), kmap), pl.BlockSpec((T, MLA_VD), kmap)],
        out_specs=[pl.BlockSpec((t, MLA_VD), qmap), pl.BlockSpec((t, MLA_VD), qmap)],
        out_shape=[jax.ShapeDtypeStruct((T, MLA_HEADS * MLA_VD), F32),
                   jax.ShapeDtypeStruct((T, MLA_HEADS * MLA_VD), F32)],
        scratch_shapes=[pltpu.VMEM((t, MLA_VD), F32), pltpu.VMEM((t, MLA_VD), F32), pltpu.VMEM((t, MLA_VD), F32)],
        compiler_params=_cparams(("parallel", "arbitrary")),
    )(qf, kf, vf)


def _flash_delta(o, do, name):
    def fn(ov, dv):
        parts = []
        for h in range(MLA_HEADS):
            sl = slice(h * MLA_VD, (h + 1) * MLA_VD)
            d = jnp.sum(dv[:, sl] * ov[:, sl], axis=-1, keepdims=True)
            parts.append(jnp.broadcast_to(d, (d.shape[0], MLA_VD)))
        return jnp.concatenate(parts, axis=-1), dv
    w = MLA_HEADS * MLA_VD
    return _rows(fn, [o, do], [], [(w, F32), (w, BF16)], name=name)


def _flash_bwd(qf, kf, vf, do16, lse, delta, name):
    T = qf.shape[0]
    t = _pick(T, FLASH_T)
    n = T // t
    scale = MLA_QKD ** -0.5

    def body(q_ref, k_ref, v_ref, do_ref, lse_ref, dl_ref, dq_ref, dk_ref, dv_ref):
        kb = pl.program_id(1)

        @pl.when(kb == 0)
        def _():
            dq_ref[...] = jnp.zeros_like(dq_ref)

        dk_ref[...] = jnp.zeros_like(dk_ref)
        dv_ref[...] = jnp.zeros_like(dv_ref)
        k, v = k_ref[...], v_ref[...]

        def step(qb, masked):
            rows = pl.ds(pl.multiple_of(qb * t, t), t)
            q, dob = q_ref[rows, :], do_ref[rows, :]
            s = _dot_nt(q, k) * scale
            if masked:
                s = jnp.where(_chunk_mask(0, 0, t, t), s, NEG)
            p = jnp.exp(s - _widen(lse_ref[rows, :], t))
            ds = (p * (_dot_nt(dob, v) - _widen(dl_ref[rows, :], t)) * scale).astype(BF16)
            dv_ref[...] += _dot_tn(p.astype(BF16), dob)
            dk_ref[...] += _dot_tn(ds, q)
            dq_ref[rows, :] += _dot(ds, k)

        step(kb, True)

        @pl.loop(kb + 1, n)
        def _(qb):
            step(qb, False)

    qmap = lambda h, j: (0, h)
    kmap = lambda h, j: (j, h)
    return pl.pallas_call(
        body, name=name, grid=(MLA_HEADS, n),
        in_specs=[pl.BlockSpec((T, MLA_HP), qmap), pl.BlockSpec((t, MLA_HP), kmap), pl.BlockSpec((t, MLA_VD), kmap),
                  pl.BlockSpec((T, MLA_VD), qmap), pl.BlockSpec((T, MLA_VD), qmap), pl.BlockSpec((T, MLA_VD), qmap)],
        out_specs=[pl.BlockSpec((T, MLA_HP), qmap), pl.BlockSpec((t, MLA_HP), kmap), pl.BlockSpec((t, MLA_VD), kmap)],
        out_shape=[jax.ShapeDtypeStruct((T, MLA_HEADS * MLA_HP), F32),
                   jax.ShapeDtypeStruct((T, MLA_HEADS * MLA_HP), F32),
                   jax.ShapeDtypeStruct((T, MLA_HEADS * MLA_VD), F32)],
        compiler_params=_cparams(("arbitrary", "arbitrary")),
    )(qf, kf, vf, do16, lse, delta)


MESH = pl.DeviceIdType.MESH
ANY = pl.BlockSpec(memory_space=pl.ANY)
_CHIP_FLIPS = ((1, 0), (0, 1), (1, 1))


def _place():
    return lax.axis_index("x"), lax.axis_index("y"), lax.axis_index("c")


def _other_chip(x, y, k):
    fx, fy = _CHIP_FLIPS[k]
    return ((1 - x) if fx else x), ((1 - y) if fy else y)


def _row_chunks(rows):
    cr = rows // COMM_CHUNKS
    assert cr * COMM_CHUNKS == rows and cr % 16 == 0, rows
    return [(i * cr, cr) for i in range(COMM_CHUNKS)]


def _rows_at(start, size):
    return pl.ds(pl.multiple_of(start, 16), size)


def _local_copies(pairs, sems):
    cps = [pltpu.make_async_copy(s, d, sems.at[i]) for i, (s, d) in enumerate(pairs)]
    for cp in cps:
        cp.start()
    return cps


def _remote(src, dst, send_sems, recv_sems, k, to):
    return pltpu.make_async_remote_copy(src_ref=src, dst_ref=dst, send_sem=send_sems.at[k], recv_sem=recv_sems.at[k],
                                        device_id=to, device_id_type=MESH)


def _gather_weights(wsh):
    R, W = wsh.shape
    H = R // 2

    chunks = _row_chunks(H)
    n = len(chunks)

    def body(w_ref, out_ref, send_sems, recv_sems, local_sem):
        x, y, c = _place()
        j = 2 * x + y
        sibling = (x, y, 1 - c)
        chips = [_other_chip(x, y, k) for k in range(3)]
        mine = _local_copies([(w_ref.at[pl.ds(h * H + off, cr)], out_ref.at[j, pl.ds(h * H + off, cr)])
                              for h in range(2) for off, cr in chunks], local_sem)
        sent = []
        for i, (off, cr) in enumerate(chunks):
            r = _rows_at(c * H + off, cr)
            for k, (px, py) in enumerate(chips):
                cp = _remote(w_ref.at[r], out_ref.at[j, r], send_sems, recv_sems, k * n + i, (px, py, c))
                cp.start()
                sent.append(cp)
        for i, (off, cr) in enumerate(chunks):
            r = _rows_at(c * H + off, cr)
            for k, (px, py) in enumerate(chips):
                blk = out_ref.at[2 * px + py, r]
                _remote(blk, blk, send_sems, recv_sems, k * n + i, (px, py, c)).wait_recv()
                cp = _remote(blk, blk, send_sems, recv_sems, (3 + k) * n + i, sibling)
                cp.start()
                sent.append(cp)
        for i, (off, cr) in enumerate(chunks):
            r = _rows_at((1 - c) * H + off, cr)
            for k, (px, py) in enumerate(chips):
                blk = out_ref.at[2 * px + py, r]
                _remote(blk, blk, send_sems, recv_sems, (3 + k) * n + i, sibling).wait_recv()
        for cp in sent:
            cp.wait_send()
        for cp in mine:
            cp.wait()

    return pl.pallas_call(
        body, name="gather_weights", in_specs=[ANY], out_specs=ANY,
        out_shape=jax.ShapeDtypeStruct((N_CHIPS, R, W), wsh.dtype),
        scratch_shapes=[pltpu.SemaphoreType.DMA((6 * n,)), pltpu.SemaphoreType.DMA((6 * n,)),
                        pltpu.SemaphoreType.DMA((2 * n,))],
    )(wsh)


def _swap_halves(g):
    _, R, W = g.shape
    H = R // 2

    chunks = _row_chunks(H)
    n = len(chunks)

    def body(g_ref, recv_ref, own_ref, send_sems, recv_sems, local_sem):
        x, y, c = _place()
        mine = _local_copies([(g_ref.at[jj, _rows_at(c * H + off, cr)], own_ref.at[jj, pl.ds(off, cr)])
                              for jj in range(N_CHIPS) for off, cr in chunks], local_sem)
        sent = []
        for jj in range(N_CHIPS):
            for i, (off, cr) in enumerate(chunks):
                cp = _remote(g_ref.at[jj, _rows_at((1 - c) * H + off, cr)], recv_ref.at[jj, pl.ds(off, cr)],
                             send_sems, recv_sems, jj * n + i, (x, y, 1 - c))
                cp.start()
                sent.append(cp)
        for cp in sent + mine:
            cp.wait()

    out = jax.ShapeDtypeStruct((N_CHIPS, H, W), g.dtype)
    return pl.pallas_call(
        body, name="grad_swap_halves", in_specs=[ANY], out_specs=[ANY, ANY], out_shape=[out, out],
        scratch_shapes=[pltpu.SemaphoreType.DMA((N_CHIPS * n,)), pltpu.SemaphoreType.DMA((N_CHIPS * n,)),
                        pltpu.SemaphoreType.DMA((N_CHIPS * n,))],
    )(g)


def _scatter_chips(a16, a32):
    _, H, W = a16.shape
    chunks = _row_chunks(H)
    n = len(chunks)

    def body(a16_ref, a32_ref, recv_ref, own_ref, send_sems, recv_sems, local_sem):
        x, y, c = _place()
        j = 2 * x + y
        mine = _local_copies([(a32_ref.at[j, pl.ds(off, cr)], own_ref.at[pl.ds(off, cr)]) for off, cr in chunks],
                             local_sem)
        sent = []
        for i, (off, cr) in enumerate(chunks):
            r = pl.ds(off, cr)
            for k in range(3):
                px, py = _other_chip(x, y, k)
                pj = 2 * px + py
                cp = _remote(a16_ref.at[pj, r], recv_ref.at[(j - pj + 4) % 4 - 1, r], send_sems, recv_sems,
                             k * n + i, (px, py, c))
                cp.start()
                sent.append(cp)
        for i, (off, cr) in enumerate(chunks):
            r = pl.ds(off, cr)
            for k in range(3):
                px, py = _other_chip(x, y, k)
                pj = 2 * px + py
                slot = recv_ref.at[(pj - j + 4) % 4 - 1, r]
                _remote(slot, slot, send_sems, recv_sems, k * n + i, (px, py, c)).wait_recv()
        for cp in sent:
            cp.wait_send()
        for cp in mine:
            cp.wait()

    return pl.pallas_call(
        body, name="grad_scatter_chips", in_specs=[ANY, ANY], out_specs=[ANY, ANY],
        out_shape=[jax.ShapeDtypeStruct((3, H, W), a16.dtype), jax.ShapeDtypeStruct((H, W), a32.dtype)],
        scratch_shapes=[pltpu.SemaphoreType.DMA((3 * n,)), pltpu.SemaphoreType.DMA((3 * n,)),
                        pltpu.SemaphoreType.DMA((n,))],
    )(a16, a32)


def _share_halves(r):
    H, W = r.shape
    chunks = _row_chunks(H)
    n = len(chunks)

    def body(r_ref, out_ref, send_sems, recv_sems, local_sem):
        x, y, c = _place()
        mine = _local_copies([(r_ref.at[pl.ds(off, cr)], out_ref.at[_rows_at(c * H + off, cr)]) for off, cr in chunks],
                             local_sem)
        sent = []
        for i, (off, cr) in enumerate(chunks):
            cp = _remote(r_ref.at[pl.ds(off, cr)], out_ref.at[_rows_at(c * H + off, cr)], send_sems, recv_sems, i,
                         (x, y, 1 - c))
            cp.start()
            sent.append(cp)
        for cp in sent + mine:
            cp.wait()

    return pl.pallas_call(
        body, name="grad_share_halves", in_specs=[ANY], out_specs=ANY,
        out_shape=jax.ShapeDtypeStruct((2 * H, W), r.dtype),
        scratch_shapes=[pltpu.SemaphoreType.DMA((n,)), pltpu.SemaphoreType.DMA((n,)), pltpu.SemaphoreType.DMA((n,))],
    )(r)


def _allsum_small(v, name):
    R, W = v.shape
    n_dev = 8
    vm = pl.BlockSpec(memory_space=pltpu.VMEM)

    def body(v_ref, out_ref, buf, send_sems, recv_sems):
        x, y, c = _place()
        me = 4 * x + 2 * y + c
        buf[me] = v_ref[...]
        sent = []
        for k in range(1, n_dev):
            peer = ((1 - x) if k & 4 else x, (1 - y) if k & 2 else y, (1 - c) if k & 1 else c)
            cp = _remote(v_ref, buf.at[me], send_sems, recv_sems, k - 1, peer)
            cp.start()
            sent.append(cp)
        for cp in sent:
            cp.wait_recv()
        for cp in sent:
            cp.wait_send()
        acc = buf[0]
        for q in range(1, n_dev):
            acc = acc + buf[q]
        out_ref[...] = acc

    return pl.pallas_call(
        body, name=name, in_specs=[vm], out_specs=vm, out_shape=jax.ShapeDtypeStruct((R, W), v.dtype),
        scratch_shapes=[pltpu.VMEM((n_dev, R, W), v.dtype), pltpu.SemaphoreType.DMA((n_dev - 1,)),
                        pltpu.SemaphoreType.DMA((n_dev - 1,))],
    )(v)


def _reduce_grads(g16):
    _, R, W = g16.shape
    H = R // 2
    recv, own = _swap_halves(g16)
    a32, a16 = _rows(lambda a, b: (a.astype(F32) + b.astype(F32),) * 2,
                     [own.reshape(N_CHIPS * H, W), recv.reshape(N_CHIPS * H, W)], [],
                     [(W, F32), (W, BF16)], name="grad_pair_sum", tile=288)
    got, mine = _scatter_chips(a16.reshape(N_CHIPS, H, W), a32.reshape(N_CHIPS, H, W))
    red = _rows(lambda m, a, b, c: (((m + a.astype(F32)) + b.astype(F32)) + c.astype(F32),),
                [mine, got[0], got[1], got[2]], [], [(W, F32)], name="grad_chip_sum", tile=288)[0]
    return _share_halves(red)


def _adamw(w, g, m, v, name):
    shape = w.shape
    cols = shape[-1]

    def fn(wv, gv, mv, vv):
        m2 = ADAM_B1 * mv + (1.0 - ADAM_B1) * gv
        v2 = ADAM_B2 * vv + (1.0 - ADAM_B2) * jnp.square(gv)
        m_hat = m2 / (1.0 - ADAM_B1 ** ADAM_STEP)
        v_hat = v2 / (1.0 - ADAM_B2 ** ADAM_STEP)
        return -ADAM_LR * (m_hat / (jnp.sqrt(v_hat) + ADAM_EPS) + ADAM_WD * wv), m2, v2

    w2, m2, v2 = (t.reshape(-1, cols) for t in (w, m, v))
    rows = w2.shape[0]
    tile = 256 if rows % 8 == 0 else rows
    res = _rows(fn, [w2, g.reshape(rows, cols), m2, v2], [], [(cols, F32)] * 3, name=name, tile=tile)
    return tuple(t.reshape(shape) for t in res)


def _add_res(acc, r):
    return (r + acc,)


def _tail_fwd(h1, p16, W, i, tag):
    hn2 = _norm_fwd(h1, W["mlp_norm"][i:i + 1], f"{tag}_mlp_norm")
    z, a = _mm(hn2, W["mlp_w1"][i], bblk=True, outs=[F32, BF16], name=f"{tag}_mlp_w1",
               epilogue=lambda acc: (acc, jnp.square(jnp.maximum(acc, 0.0))))
    h2 = _mm(a, W["mlp_w2"][i], extras=[h1], epilogue=_add_res, name=f"{tag}_mlp_w2")
    hn3 = _norm_fwd(h2, W["ple_norm"][i:i + 1], f"{tag}_ple_norm")
    gl = _mm(hn3, W["ple_gate_w"][i], name=f"{tag}_ple_gate")
    h3, pp = _mm(p16[i], W["ple_proj_w"][i], bblk=True, extras=[gl, h2], outs=[F32, F32], name=f"{tag}_ple_proj",
                 epilogue=lambda acc, g, h: (h + _sigmoid(g) * acc, acc))
    return h3, (h1, hn2, z, a, h2, hn3, gl, pp)


def _tail_bwd(dh3, saved, p16, W, i, tag):
    h1, hn2, z, a, h2, hn3, gl, pp = saved

    def gate_bwd(d, g, ppv):
        gate = _sigmoid(g)
        return d * gate, d * ppv * gate * (1.0 - gate)

    dpp, dgl = _rows(gate_bwd, [dh3, gl, pp], [], [(D_MODEL, BF16), (D_MODEL, BF16)], name=f"{tag}_ple_gate_bwd")
    d_proj = _mm(p16[i], dpp, ta=True, oblk=PLE_DIM, name=f"{tag}_d_ple_proj")
    d_gate = _mm(hn3, dgl, ta=True, name=f"{tag}_d_ple_gate")
    dhn3 = _mm(dgl, W["ple_gate_w"][i], tb=True, name=f"{tag}_ple_gate_dx")
    dh2, d_ple_norm = _norm_bwd(h2, dhn3, W["ple_norm"][i:i + 1], dh3, f"{tag}_ple_norm_bwd")
    d_w2 = _mm(a, dh2, ta=True, name=f"{tag}_d_mlp_w2")
    dz = _mm(dh2, W["mlp_w2"][i], tb=True, extras=[z], outs=[BF16], name=f"{tag}_mlp_w2_dx",
             epilogue=lambda acc, zv: (acc * (2.0 * jnp.maximum(zv, 0.0)),))
    d_w1 = _mm(hn2, dz, ta=True, oblk=D_MODEL, name=f"{tag}_d_mlp_w1")
    dhn2 = _mm(dz, W["mlp_w1"][i], tb=True, bblk=True, name=f"{tag}_mlp_w1_dx")
    dh1, d_mlp_norm = _norm_bwd(h1, dhn2, W["mlp_norm"][i:i + 1], dh2, f"{tag}_mlp_norm_bwd")
    return dh1, dict(mlp_norm=d_mlp_norm, mlp_w1=d_w1, mlp_w2=d_w2, ple_norm=d_ple_norm,
                     ple_gate_w=d_gate, ple_proj_w=d_proj)


def _ret_layer_fwd(h0, W, tabs):
    hn = _norm_fwd(h0, W["mix_norm"][0:1], "ret_mix_norm")
    proj = _mm(hn, W["ret_w_in"], bblk=True, name="ret_w_in")
    out, states = _ret_fwd(proj, tabs, "ret_scan")
    y = _ret_gate(out, proj, W["ret_gn"], "ret_gate")
    h1 = _mm(y, W["ret_w_out"], extras=[h0], epilogue=_add_res, name="ret_w_out")
    return h1, (h0, hn, proj, out, states, y)


def _ret_layer_bwd(dh1, saved, W, tabs):
    h0, hn, proj, out, states, y = saved
    d_w_out = _mm(y, dh1, ta=True, name="d_ret_w_out")
    dy = _mm(dh1, W["ret_w_out"], tb=True, name="ret_w_out_dx")
    dout, dg, d_gn = _ret_gate_bwd(out, proj, W["ret_gn"], dy, "ret_gate_bwd")
    dq, dk, dv = _ret_bwd(proj, states, dout, tabs, "ret_scan_bwd")
    dproj = jnp.concatenate([dq, dk, dv, dg], axis=1)
    d_w_in = _mm(hn, dproj, ta=True, oblk=proj.shape[1] // N_CHIPS, name="d_ret_w_in")
    dhn = _mm(dproj, W["ret_w_in"], tb=True, bblk=True, name="ret_w_in_dx")
    dh0, d_mix = _norm_bwd(h0, dhn, W["mix_norm"][0:1], dh1, "ret_mix_norm_bwd")
    return dh0, dict(mix_norm=d_mix, ret_w_in=d_w_in, ret_gn=d_gn, ret_w_out=d_w_out)


def _mla_layer_fwd(h0, W, tabs):
    hn = _norm_fwd(h0, W["mix_norm"][1:2], "mla_mix_norm")
    proj = _mm(hn, W["mla_w_in"], name="mla_w_in")

    def low_rank_norm(pv, gq, gkv):
        return _rms(pv[:, :MLA_Q_RANK], gq), _rms(pv[:, MLA_Q_RANK:MLA_Q_RANK + MLA_KV_RANK], gkv)

    cqn, ckvn = _rows(low_rank_norm, [proj], [W["mla_q_a_norm"], W["mla_kv_a_norm"]],
                      [(MLA_Q_RANK, BF16), (MLA_KV_RANK, BF16)], name="mla_low_rank_norm")
    q = _mm(cqn, W["mla_w_uq"], name="mla_w_uq")
    kv = _mm(ckvn, W["mla_w_ukv"], bblk=True, name="mla_w_ukv")
    qf, kf, vf = _mla_prep(q, kv, proj, W["mla_q_norm"], W["mla_k_norm"], tabs, "mla_prep")
    o, lse = _flash_fwd(qf, kf, vf, "mla_flash")
    h1 = _mm(o, W["mla_w_out"], extras=[h0], epilogue=_add_res, name="mla_w_out")
    return h1, (h0, hn, proj, cqn, ckvn, q, kv, qf, kf, vf, o, lse)


def _mla_layer_bwd(dh1, saved, W, tabs):
    h0, hn, proj, cqn, ckvn, q, kv, qf, kf, vf, o, lse = saved
    d_w_out = _mm(o, dh1, ta=True, name="d_mla_w_out")
    do = _mm(dh1, W["mla_w_out"], tb=True, name="mla_w_out_dx")
    delta, do16 = _flash_delta(o, do, "mla_flash_delta")
    dqf, dkf, dvf = _flash_bwd(qf, kf, vf, do16, lse, delta, "mla_flash_bwd")
    dq, dkv, dkr, d_gq, d_gk = _mla_prep_bwd(q, kv, proj, W["mla_q_norm"], W["mla_k_norm"], tabs, dqf, dkf, dvf,
                                             "mla_prep_bwd")
    d_w_uq = _mm(cqn, dq, ta=True, name="d_mla_w_uq")
    dcqn = _mm(dq, W["mla_w_uq"], tb=True, name="mla_w_uq_dx")
    d_w_ukv = _mm(ckvn, dkv, ta=True, oblk=kv.shape[1] // N_CHIPS, name="d_mla_w_ukv")
    dckvn = _mm(dkv, W["mla_w_ukv"], tb=True, bblk=True, name="mla_w_ukv_dx")

    def low_rank_bwd(pv, dcq, dckv, dkr_v, gq, gkv):
        dxq, dgq = _rms_bwd(pv[:, :MLA_Q_RANK], dcq, gq)
        dxkv, dgkv = _rms_bwd(pv[:, MLA_Q_RANK:MLA_Q_RANK + MLA_KV_RANK], dckv, gkv)
        return jnp.concatenate([dxq, dxkv, dkr_v], axis=-1), _colsum(dgq), _colsum(dgkv)

    dproj, d_gqa, d_gkva = _rows(low_rank_bwd, [proj, dcqn, dckvn, dkr], [W["mla_q_a_norm"], W["mla_kv_a_norm"]],
                                 [(MLA_IN_PAD, BF16)], [((1, MLA_Q_RANK), F32), ((1, MLA_KV_RANK), F32)],
                                 name="mla_low_rank_norm_bwd")
    d_w_in = _mm(hn, dproj, ta=True, name="d_mla_w_in")
    dhn = _mm(dproj, W["mla_w_in"], tb=True, name="mla_w_in_dx")
    dh0, d_mix = _norm_bwd(h0, dhn, W["mix_norm"][1:2], dh1, "mla_mix_norm_bwd")
    return dh0, dict(mix_norm=d_mix, mla_w_in=d_w_in, mla_q_a_norm=d_gqa, mla_kv_a_norm=d_gkva, mla_w_uq=d_w_uq,
                     mla_w_ukv=d_w_ukv, mla_q_norm=d_gq, mla_k_norm=d_gk, mla_w_out=d_w_out)


def _local_step(x, p16, target, W):
    T = x.shape[0]
    ret_tabs, mla_tabs = _ret_tables(T), _mla_tables(T)
    h1, s_ret = _ret_layer_fwd(x, W, ret_tabs)
    h3, s_tail0 = _tail_fwd(h1, p16, W, 0, "l0")
    h4, s_mla = _mla_layer_fwd(h3, W, mla_tabs)
    y, s_tail1 = _tail_fwd(h4, p16, W, 1, "l1")

    def loss_head(yv, tv):
        e = yv - tv
        return e * (1.0 / D_MODEL), jnp.full((1, 128), 0.5 / D_MODEL, F32) * jnp.sum(e * e)

    dy, loss = _rows(loss_head, [y, target], [], [(D_MODEL, F32)], [((1, 128), F32)], name="loss_head")
    dh4, g_tail1 = _tail_bwd(dy, s_tail1, p16, W, 1, "l1")
    dh3, g_mla = _mla_layer_bwd(dh4, s_mla, W, mla_tabs)
    dh1, g_tail0 = _tail_bwd(dh3, s_tail0, p16, W, 0, "l0")
    dx, g_ret = _ret_layer_bwd(dh1, s_ret, W, ret_tabs)
    return loss, dx, g_ret, g_tail0, g_mla, g_tail1


_BIG = ("ret_w_in", "ret_w_out", "mla_w_in", "mla_w_uq", "mla_w_ukv", "mla_w_out", "mlp_w1", "mlp_w2",
        "ple_gate_w", "ple_proj_w")
_SMALL = ("mix_norm", "ret_gn", "mla_q_a_norm", "mla_kv_a_norm", "mla_q_norm", "mla_k_norm", "mlp_norm", "ple_norm")
_ORDER = ("mix_norm", "ret_w_in", "ret_gn", "ret_w_out", "mla_w_in", "mla_q_a_norm", "mla_kv_a_norm", "mla_w_uq",
          "mla_w_ukv", "mla_q_norm", "mla_k_norm", "mla_w_out", "mlp_norm", "mlp_w1", "mlp_w2", "ple_norm",
          "ple_gate_w", "ple_proj_w")
SMALL_ROWS = 16


def _pack_rows(shards):
    return [shards[n].size // PACK_W for n in _BIG]


def _split_rows(buf, shards):
    out, off = {}, 0
    for n, r in zip(_BIG, _pack_rows(shards)):
        out[n] = buf[..., off:off + r, :]
        off += r
    return out


def _full_weights(gath, shards, small):
    b = _split_rows(gath, shards)
    n = N_CHIPS
    W = {}
    W["ret_w_in"] = b["ret_w_in"].reshape(n, D_MODEL, -1)
    W["ret_w_out"] = b["ret_w_out"].reshape(-1, D_MODEL)
    W["mla_w_in"] = jnp.pad(b["mla_w_in"].reshape(D_MODEL, MLA_IN), ((0, 0), (0, MLA_IN_PAD - MLA_IN)))
    uq = b["mla_w_uq"].reshape(n, MLA_Q_RANK, MLA_HEADS // n, MLA_QKD).transpose(1, 0, 2, 3)
    W["mla_w_uq"] = jnp.pad(uq, ((0, 0), (0, 0), (0, 0), (0, MLA_HP - MLA_QKD))).reshape(MLA_Q_RANK, MLA_HEADS * MLA_HP)
    W["mla_w_ukv"] = b["mla_w_ukv"].reshape(n, MLA_KV_RANK, -1)
    W["mla_w_out"] = b["mla_w_out"].reshape(-1, D_MODEL)
    w1 = b["mlp_w1"].reshape(n, 2, D_MODEL, -1)
    W["mlp_w1"] = [w1[:, i] for i in range(2)]
    w2 = b["mlp_w2"].reshape(n, 2, -1, D_MODEL)
    W["mlp_w2"] = [w2[:, i].reshape(-1, D_MODEL) for i in range(2)]
    wg = b["ple_gate_w"].reshape(n, 2, -1, D_MODEL)
    W["ple_gate_w"] = [wg[:, i].reshape(-1, D_MODEL) for i in range(2)]
    wp = b["ple_proj_w"].reshape(n, 2, PLE_DIM, -1)
    W["ple_proj_w"] = [wp[:, i] for i in range(2)]
    W["ret_gn"] = small[0:2].reshape(RET_HEADS, RET_DV)
    W["mla_q_a_norm"] = small[2:3, :MLA_Q_RANK]
    W["mla_kv_a_norm"] = small[3:4, :MLA_KV_RANK]
    return W


def _pack_grads(g_ret, g_tail0, g_mla, g_tail1):
    n = N_CHIPS

    def both(name):
        return jnp.stack([g_tail0[name].reshape(n, -1, PACK_W), g_tail1[name].reshape(n, -1, PACK_W)], axis=1)

    uq = g_mla["mla_w_uq"].reshape(MLA_Q_RANK, n, MLA_HEADS // n, MLA_HP)[..., :MLA_QKD].transpose(1, 0, 2, 3)
    parts = dict(
        ret_w_in=g_ret["ret_w_in"], ret_w_out=g_ret["ret_w_out"], mla_w_in=g_mla["mla_w_in"][:, :MLA_IN],
        mla_w_uq=uq, mla_w_ukv=g_mla["mla_w_ukv"], mla_w_out=g_mla["mla_w_out"], mlp_w1=both("mlp_w1"),
        mlp_w2=both("mlp_w2"), ple_gate_w=both("ple_gate_w"), ple_proj_w=both("ple_proj_w"))
    return jnp.concatenate([parts[k].astype(BF16).reshape(n, -1, PACK_W) for k in _BIG], axis=1)


def _pad_row(v):
    v = v.reshape(1, -1)
    return jnp.pad(v, ((0, 0), (0, PACK_W - v.shape[1])))


def kernel(x, p, mix_norm, ret_w_in, ret_gn, ret_w_out, mla_w_in, mla_q_a_norm, mla_kv_a_norm, mla_w_uq, mla_w_ukv, mla_q_norm, mla_k_norm, mla_w_out, mlp_norm, mlp_w1, mlp_w2, ple_norm, ple_gate_w, ple_proj_w, loss_target, m_mix_norm, m_ret_w_in, m_ret_gn, m_ret_w_out, m_mla_w_in, m_mla_q_a_norm, m_mla_kv_a_norm, m_mla_w_uq, m_mla_w_ukv, m_mla_q_norm, m_mla_k_norm, m_mla_w_out, m_mlp_norm, m_mlp_w1, m_mlp_w2, m_ple_norm, m_ple_gate_w, m_ple_proj_w, v_mix_norm, v_ret_w_in, v_ret_gn, v_ret_w_out, v_mla_w_in, v_mla_q_a_norm, v_mla_kv_a_norm, v_mla_w_uq, v_mla_w_ukv, v_mla_q_norm, v_mla_k_norm, v_mla_w_out, v_mlp_norm, v_mlp_w1, v_mlp_w2, v_ple_norm, v_ple_gate_w, v_ple_proj_w):
    w = dict(mix_norm=mix_norm, ret_w_in=ret_w_in, ret_gn=ret_gn, ret_w_out=ret_w_out, mla_w_in=mla_w_in,
             mla_q_a_norm=mla_q_a_norm, mla_kv_a_norm=mla_kv_a_norm, mla_w_uq=mla_w_uq, mla_w_ukv=mla_w_ukv,
             mla_q_norm=mla_q_norm, mla_k_norm=mla_k_norm, mla_w_out=mla_w_out, mlp_norm=mlp_norm, mlp_w1=mlp_w1,
             mlp_w2=mlp_w2, ple_norm=ple_norm, ple_gate_w=ple_gate_w, ple_proj_w=ple_proj_w)
    m = dict(mix_norm=m_mix_norm, ret_w_in=m_ret_w_in, ret_gn=m_ret_gn, ret_w_out=m_ret_w_out, mla_w_in=m_mla_w_in,
             mla_q_a_norm=m_mla_q_a_norm, mla_kv_a_norm=m_mla_kv_a_norm, mla_w_uq=m_mla_w_uq, mla_w_ukv=m_mla_w_ukv,
             mla_q_norm=m_mla_q_norm, mla_k_norm=m_mla_k_norm, mla_w_out=m_mla_w_out, mlp_norm=m_mlp_norm,
             mlp_w1=m_mlp_w1, mlp_w2=m_mlp_w2, ple_norm=m_ple_norm, ple_gate_w=m_ple_gate_w, ple_proj_w=m_ple_proj_w)
    v = dict(mix_norm=v_mix_norm, ret_w_in=v_ret_w_in, ret_gn=v_ret_gn, ret_w_out=v_ret_w_out, mla_w_in=v_mla_w_in,
             mla_q_a_norm=v_mla_q_a_norm, mla_kv_a_norm=v_mla_kv_a_norm, mla_w_uq=v_mla_w_uq, mla_w_ukv=v_mla_w_ukv,
             mla_q_norm=v_mla_q_norm, mla_k_norm=v_mla_k_norm, mla_w_out=v_mla_w_out, mlp_norm=v_mlp_norm,
             mlp_w1=v_mlp_w1, mlp_w2=v_mlp_w2, ple_norm=v_ple_norm, ple_gate_w=v_ple_gate_w, ple_proj_w=v_ple_proj_w)
    xi, yi, ci = _place()
    chip = 2 * xi + yi
    n = N_CHIPS

    wsh = jnp.concatenate([w[k].astype(BF16).reshape(-1, PACK_W) for k in _BIG], axis=0)
    gath = _gather_weights(wsh)
    on = (jnp.arange(n) == chip) & (ci == 0)
    gn_rows = jnp.where(on[None, :, None], ret_gn[0][:, None, :], 0.0).reshape(2, PACK_W)
    qa_row = _pad_row(jnp.where(on[:, None], mla_q_a_norm, 0.0))
    kva_row = _pad_row(jnp.where(on[:, None], mla_kv_a_norm, 0.0))
    small_in = jnp.concatenate([gn_rows, qa_row, kva_row, jnp.zeros((4, PACK_W), F32)], axis=0)
    small = _allsum_small(small_in, "gather_gains")
    W = _full_weights(gath, {k: w[k] for k in _BIG}, small)
    W["mix_norm"], W["mlp_norm"], W["ple_norm"] = mix_norm, mlp_norm, ple_norm
    W["mla_q_norm"] = jnp.pad(mla_q_norm, ((0, 0), (0, MLA_HP - MLA_QKD)))
    W["mla_k_norm"] = jnp.pad(mla_k_norm, ((0, 0), (0, MLA_HP - MLA_QKD)))

    loss, dx, g_ret, g_tail0, g_mla, g_tail1 = _local_step(x[0], p[:, 0].astype(BF16), loss_target[0], W)

    red = _reduce_grads(_pack_grads(g_ret, g_tail0, g_mla, g_tail1))
    g_big = _split_rows(red, {k: w[k] for k in _BIG})
    small_g = jnp.concatenate([
        g_ret["mix_norm"], g_mla["mix_norm"], g_tail0["mlp_norm"], g_tail1["mlp_norm"], g_tail0["ple_norm"],
        g_tail1["ple_norm"], g_ret["ret_gn"].reshape(2, PACK_W), _pad_row(g_mla["mla_q_a_norm"]),
        _pad_row(g_mla["mla_kv_a_norm"]), _pad_row(g_mla["mla_q_norm"][:, :MLA_QKD]),
        _pad_row(g_mla["mla_k_norm"][:, :MLA_QKD]), _pad_row(loss[:, :1]), jnp.zeros((3, PACK_W), F32)], axis=0)
    tot = _allsum_small(small_g, "sum_small_grads")
    gn_all = tot[6:8].reshape(RET_HEADS, n, -1)
    g_small = dict(
        mix_norm=tot[0:2], mlp_norm=tot[2:4], ple_norm=tot[4:6],
        ret_gn=lax.dynamic_index_in_dim(gn_all, chip, axis=1, keepdims=False),
        mla_q_a_norm=lax.dynamic_index_in_dim(tot[8, :MLA_Q_RANK].reshape(n, -1), chip, axis=0, keepdims=True),
        mla_kv_a_norm=lax.dynamic_index_in_dim(tot[9, :MLA_KV_RANK].reshape(n, -1), chip, axis=0, keepdims=True),
        mla_q_norm=tot[10:11, :MLA_QKD], mla_k_norm=tot[11:12, :MLA_QKD])
    loss_out = tot[12, 0]

    grads, deltas, new_m, new_v = [], [], [], []
    for k in _ORDER:
        g = g_big[k] if k in g_big else g_small[k]
        g = g.reshape(w[k].shape)
        d, m2, v2 = _adamw(w[k], g, m[k], v[k], f"adamw_{k}")
        grads.append(g)
        deltas.append(d)
        new_m.append(m2)
        new_v.append(v2)
    return (loss_out, dx[None], *grads, *deltas, *new_m, *new_v)
```

```python
import functools

import jax
import jax.numpy as jnp
from jax import lax
from jax.experimental import pallas as pl
from jax.experimental.pallas import tpu as pltpu

F32 = jnp.float32
BF16 = jnp.bfloat16

EPS = 1e-6
D_MODEL = 1024
CHUNK = 64
ROPE_THETA = 10000.0
RET_HEADS = 4
RET_DK = 256
RET_DV = 512
MLA_HEADS = 8
MLA_NOPE = 128
MLA_ROPE = 64
MLA_QKD = 192
MLA_VD = 128
MLA_HP = 256
MLA_Q_RANK = 384
MLA_KV_RANK = 256
MLA_IN = 704
MLA_IN_PAD = 768
D_FF = 4096
PLE_DIM = 256
N_CHIPS = 4

ADAM_LR = 0.001
ADAM_B1 = 0.9
ADAM_B2 = 0.999
ADAM_EPS = 1e-08
ADAM_WD = 0.01
ADAM_STEP = 10

VMEM_LIMIT = 56 * 1024 * 1024
PACK_W = 1024
NEG = -1e30
FLASH_T = 512
COMM_CHUNKS = 9


def _cparams(sem=None):
    return pltpu.CompilerParams(dimension_semantics=sem, vmem_limit_bytes=VMEM_LIMIT)


def _pick(dim, pref):
    if dim <= pref:
        return dim
    t = pref
    while dim % t:
        t //= 2
    return t


def _mm(a, b, *, name, ta=False, tb=False, bblk=False, oblk=None, outs=None, extras=(), epilogue=None,
        tm=1024, tn=512, tk=1024):
    if ta:
        K, M = a.shape
    else:
        M, K = a.shape
    if bblk:
        if tb:
            _, N, Kq = b.shape
            assert Kq * N_CHIPS == K
        else:
            _, Kb, Nq_b = b.shape
            N = Nq_b * N_CHIPS
            assert Kb == K
    else:
        N = b.shape[0] if tb else b.shape[1]
    tm = _pick(M, tm)
    tn = _pick(N if not (bblk and not tb) else b.shape[2], tn)
    if oblk is not None:
        tn = _pick(oblk, tn)
    tk = _pick(K if not (bblk and tb) else b.shape[2], tk)
    nk = K // tk
    grid = (M // tm, N // tn, nk)

    if ta:
        a_spec = pl.BlockSpec((tk, tm), lambda i, j, k: (k, i))
    else:
        a_spec = pl.BlockSpec((tm, tk), lambda i, j, k: (i, k))
    if bblk and tb:
        kpb = b.shape[2] // tk
        b_spec = pl.BlockSpec((None, tn, tk), lambda i, j, k: (k // kpb, j, k % kpb))
    elif bblk:
        npb = b.shape[2] // tn
        b_spec = pl.BlockSpec((None, tk, tn), lambda i, j, k: (j // npb, k, j % npb))
    elif tb:
        b_spec = pl.BlockSpec((tn, tk), lambda i, j, k: (j, k))
    else:
        b_spec = pl.BlockSpec((tk, tn), lambda i, j, k: (k, j))
    e_specs = [pl.BlockSpec((tm, tn), lambda i, j, k: (i, j)) for _ in extras]
    if outs is None:
        outs = [F32]
    if oblk is not None:
        opb = oblk // tn
        o_specs = [pl.BlockSpec((None, tm, tn), lambda i, j, k: (j // opb, i, j % opb)) for _ in outs]
        o_shapes = [jax.ShapeDtypeStruct((N_CHIPS, M, oblk), dt) for dt in outs]
    else:
        o_specs = [pl.BlockSpec((tm, tn), lambda i, j, k: (i, j)) for _ in outs]
        o_shapes = [jax.ShapeDtypeStruct((M, N), dt) for dt in outs]
    n_e, n_o = len(extras), len(outs)
    if ta:
        dims = (((0,), (0,)), ((), ()))
    elif tb:
        dims = (((1,), (1,)), ((), ()))
    else:
        dims = (((1,), (0,)), ((), ()))

    def body(a_ref, b_ref, *rest):
        e_refs, o_refs, acc = rest[:n_e], rest[n_e:n_e + n_o], rest[n_e + n_o]
        k = pl.program_id(2)
        part = lax.dot_general(a_ref[...].astype(BF16), b_ref[...].astype(BF16), dims,
                               preferred_element_type=F32)

        @pl.when(k == 0)
        def _():
            acc[...] = part

        @pl.when(k > 0)
        def _():
            acc[...] += part

        @pl.when(k == nk - 1)
        def _():
            res = acc[...]
            vals = (res,) if epilogue is None else epilogue(res, *[e[...] for e in e_refs])
            for o, v in zip(o_refs, vals):
                o[...] = v.astype(o.dtype)

    res = pl.pallas_call(
        body, name=name, grid=grid,
        in_specs=[a_spec, b_spec, *e_specs], out_specs=o_specs, out_shape=o_shapes,
        scratch_shapes=[pltpu.VMEM((tm, tn), F32)],
        compiler_params=_cparams(("parallel", "parallel", "arbitrary")),
    )(a, b, *extras)
    return res[0] if n_o == 1 else res


def _rows(fn, rows, fulls, outs, accs=(), *, name, tile=256):
    first = rows[0][0] if isinstance(rows[0], tuple) else rows[0]
    T = first.shape[0]
    tile = _pick(T, tile)
    in_specs, args = [], []
    for r in rows:
        if isinstance(r, tuple):
            arr, w, cb = r
            in_specs.append(pl.BlockSpec((tile, w), lambda i, cb=cb: (i, cb)))
        else:
            arr = r
            in_specs.append(pl.BlockSpec((tile, arr.shape[1]), lambda i: (i, 0)))
        args.append(arr)
    for f in fulls:
        in_specs.append(pl.BlockSpec(f.shape, lambda i, nd=f.ndim: (0,) * nd))
        args.append(f)
    out_specs = [pl.BlockSpec((tile, w), lambda i: (i, 0)) for w, _ in outs]
    out_specs += [pl.BlockSpec(s, lambda i: (0, 0)) for s, _ in accs]
    out_shape = [jax.ShapeDtypeStruct((T, w), dt) for w, dt in outs]
    out_shape += [jax.ShapeDtypeStruct(s, dt) for s, dt in accs]
    n_in, n_out = len(args), len(outs)

    def body(*refs):
        vals = fn(*[r[...] for r in refs[:n_in]])
        o_refs = refs[n_in:]
        for o, v in zip(o_refs[:n_out], vals[:n_out]):
            o[...] = v.astype(o.dtype)
        first_step = pl.program_id(0) == 0
        for o, v in zip(o_refs[n_out:], vals[n_out:]):
            @pl.when(first_step)
            def _(o=o, v=v):
                o[...] = v.astype(o.dtype)

            @pl.when(jnp.logical_not(first_step))
            def _(o=o, v=v):
                o[...] += v.astype(o.dtype)

    res = pl.pallas_call(
        body, name=name, grid=(T // tile,), in_specs=in_specs, out_specs=out_specs, out_shape=out_shape,
        compiler_params=_cparams(("arbitrary",)),
    )(*args)
    return res


def _rms(x, g):
    r = lax.rsqrt(jnp.mean(x * x, axis=-1, keepdims=True) + EPS)
    return (x * r) * g


def _rms_bwd(x, dy, g, n=None):
    n = x.shape[-1] if n is None else n
    r = lax.rsqrt(jnp.sum(x * x, axis=-1, keepdims=True) / n + EPS)
    xh = x * r
    dxh = dy * g
    dx = r * (dxh - xh * (jnp.sum(dxh * xh, axis=-1, keepdims=True) / n))
    return dx, dy * xh


def _colsum(v):
    return jnp.sum(v, axis=0, keepdims=True)


def _sigmoid(x):
    return 1.0 / (1.0 + jnp.exp(-x))


def _widen(v, width):
    reps = width // v.shape[1]
    return v if reps == 1 else jnp.concatenate([v] * reps, axis=-1)


def _norm_fwd(h, gain, name):
    return _rows(lambda x, g: (_rms(x, g),), [h], [gain], [(h.shape[1], BF16)], name=name)[0]


def _norm_bwd(h, dhn, gain, dres, name):
    def fn(x, dy, dr, g):
        dx, dg = _rms_bwd(x, dy, g)
        return dr + dx, _colsum(dg)
    d = h.shape[1]
    return _rows(fn, [h, dhn, dres], [gain], [(d, F32)], [((1, d), F32)], name=name)


def _ret_tables(T):
    inv = 1.0 / (ROPE_THETA ** (jnp.arange(0, RET_DK, 2, dtype=F32) / RET_DK))
    ang = jnp.arange(T, dtype=F32)[:, None] * inv[None, :]
    log_gamma = jnp.log(1.0 - 2.0 ** (-5.0 - jnp.arange(RET_HEADS, dtype=F32)))
    idx = jnp.arange(CHUNK, dtype=F32)
    intra = jnp.exp(log_gamma[:, None, None] * jnp.abs(idx[:, None] - idx[None, :]))
    qd = jnp.exp(log_gamma[:, None] * (idx + 1.0))[:, :, None]
    kd = jnp.exp(log_gamma[:, None] * (CHUNK - 1.0 - idx))[:, :, None]
    cd = jnp.exp(log_gamma * CHUNK)[:, None, None]
    return jnp.cos(ang), jnp.sin(ang), intra, qd, kd, cd


def _rope_half(x, c, s):
    x1, x2 = x[:, :RET_DK // 2], x[:, RET_DK // 2:]
    return jnp.concatenate([x1 * c - x2 * s, x2 * c + x1 * s], axis=-1)


def _rope_half_bwd(d, c, s):
    d1, d2 = d[:, :RET_DK // 2], d[:, RET_DK // 2:]
    return jnp.concatenate([d1 * c + d2 * s, d2 * c - d1 * s], axis=-1)


def _dot(a, b):
    return lax.dot_general(a, b, (((1,), (0,)), ((), ())), preferred_element_type=F32)


def _dot_nt(a, b):
    return lax.dot_general(a, b, (((1,), (1,)), ((), ())), preferred_element_type=F32)


def _dot_tn(a, b):
    return lax.dot_general(a, b, (((0,), (0,)), ((), ())), preferred_element_type=F32)


def _ret_specs(T, tb, rev):
    nj = T // tb
    jj = (lambda j: nj - 1 - j) if rev else (lambda j: j)
    kq = RET_HEADS
    vq = 2 * RET_HEADS * RET_DK // RET_DV
    return dict(
        q=pl.BlockSpec((tb, RET_DK), lambda h, j: (jj(j), h)),
        k=pl.BlockSpec((tb, RET_DK), lambda h, j: (jj(j), kq + h)),
        v=pl.BlockSpec((tb, RET_DV), lambda h, j: (jj(j), vq + h)),
        tab=pl.BlockSpec((tb, RET_DK // 2), lambda h, j: (jj(j), 0)),
        intra=pl.BlockSpec((None, CHUNK, CHUNK), lambda h, j: (h, 0, 0)),
        dec=pl.BlockSpec((None, CHUNK, 1), lambda h, j: (h, 0, 0)),
        cd=pl.BlockSpec((None, 1, 1), lambda h, j: (h, 0, 0)),
        o=pl.BlockSpec((tb, RET_DV), lambda h, j: (jj(j), h)),
        s=pl.BlockSpec((None, tb // CHUNK, RET_DK, RET_DV), lambda h, j: (h, jj(j), 0, 0)),
    )


def _ret_fwd(proj, tabs, name):
    T = proj.shape[0]
    cos, sin, intra, qd, kd, cd = tabs
    tb = _pick(T, 512)
    cps = tb // CHUNK
    sp = _ret_specs(T, tb, False)
    scale = RET_DK ** -0.5

    def body(q_ref, k_ref, v_ref, cos_ref, sin_ref, intra_ref, qd_ref, kd_ref, cd_ref, o_ref, s_ref, state):
        @pl.when(pl.program_id(1) == 0)
        def _():
            state[...] = jnp.zeros_like(state)

        for c in range(cps):
            rows = pl.ds(c * CHUNK, CHUNK)
            co, si = cos_ref[rows, :], sin_ref[rows, :]
            q = _rope_half(q_ref[rows, :], co, si)
            k = _rope_half(k_ref[rows, :], co, si) * scale
            vb = v_ref[rows, :].astype(BF16)
            st = state[...]
            sb = st.astype(BF16)
            s_ref[c] = sb
            sc = _dot_nt(q.astype(BF16), k.astype(BF16)) * intra_ref[...]
            inner = _dot(sc.astype(BF16), vb)
            cross = _dot((q * qd_ref[...]).astype(BF16), sb)
            o_ref[rows, :] = inner + cross
            state[...] = st * cd_ref[...] + _dot_tn((k * kd_ref[...]).astype(BF16), vb)

    return pl.pallas_call(
        body, name=name, grid=(RET_HEADS, T // tb),
        in_specs=[sp["q"], sp["k"], sp["v"], sp["tab"], sp["tab"], sp["intra"], sp["dec"], sp["dec"], sp["cd"]],
        out_specs=[sp["o"], sp["s"]],
        out_shape=[jax.ShapeDtypeStruct((T, RET_HEADS * RET_DV), F32),
                   jax.ShapeDtypeStruct((RET_HEADS, T // CHUNK, RET_DK, RET_DV), BF16)],
        scratch_shapes=[pltpu.VMEM((RET_DK, RET_DV), F32)],
        compiler_params=_cparams(("arbitrary", "arbitrary")),
    )(proj, proj, proj, cos, sin, intra, qd, kd, cd)


def _ret_bwd(proj, states, dout, tabs, name):
    T = proj.shape[0]
    cos, sin, intra, qd, kd, cd = tabs
    tb = _pick(T, 512)
    cps = tb // CHUNK
    sp = _ret_specs(T, tb, True)
    scale = RET_DK ** -0.5

    def body(q_ref, k_ref, v_ref, cos_ref, sin_ref, intra_ref, qd_ref, kd_ref, cd_ref, s_ref, do_ref,
             dq_ref, dk_ref, dv_ref, dstate):
        @pl.when(pl.program_id(1) == 0)
        def _():
            dstate[...] = jnp.zeros_like(dstate)

        for c in reversed(range(cps)):
            rows = pl.ds(c * CHUNK, CHUNK)
            co, si = cos_ref[rows, :], sin_ref[rows, :]
            q = _rope_half(q_ref[rows, :], co, si)
            k = _rope_half(k_ref[rows, :], co, si) * scale
            qb, kb = q.astype(BF16), k.astype(BF16)
            vb = v_ref[rows, :].astype(BF16)
            dob = do_ref[rows, :].astype(BF16)
            sb = s_ref[c]
            ia = intra_ref[...]
            pb = (_dot_nt(qb, kb) * ia).astype(BF16)
            dsn = dstate[...]
            dsb = dsn.astype(BF16)
            kdk = (k * kd_ref[...]).astype(BF16)
            qdq = (q * qd_ref[...]).astype(BF16)
            dv = _dot_tn(pb, dob) + _dot(kdk, dsb)
            dpb = (_dot_nt(dob, vb) * ia).astype(BF16)
            dq = _dot(dpb, kb) + _dot_nt(dob, sb) * qd_ref[...]
            dk = _dot_tn(dpb, qb) + _dot_nt(vb, dsb) * kd_ref[...]
            dstate[...] = dsn * cd_ref[...] + _dot_tn(qdq, dob)
            dq_ref[rows, :] = _rope_half_bwd(dq, co, si).astype(BF16)
            dk_ref[rows, :] = _rope_half_bwd(dk * scale, co, si).astype(BF16)
            dv_ref[rows, :] = dv.astype(BF16)

    return pl.pallas_call(
        body, name=name, grid=(RET_HEADS, T // tb),
        in_specs=[sp["q"], sp["k"], sp["v"], sp["tab"], sp["tab"], sp["intra"], sp["dec"], sp["dec"], sp["cd"],
                  sp["s"], sp["o"]],
        out_specs=[sp["q"], sp["q"], sp["o"]],
        out_shape=[jax.ShapeDtypeStruct((T, RET_HEADS * RET_DK), BF16),
                   jax.ShapeDtypeStruct((T, RET_HEADS * RET_DK), BF16),
                   jax.ShapeDtypeStruct((T, RET_HEADS * RET_DV), BF16)],
        scratch_shapes=[pltpu.VMEM((RET_DK, RET_DV), F32)],
        compiler_params=_cparams(("arbitrary", "arbitrary")),
    )(proj, proj, proj, cos, sin, intra, qd, kd, cd, states, dout)


def _ret_gate(out, proj, gn, name):
    def fn(o, g, *gains):
        parts = [_rms(o[:, h * RET_DV:(h + 1) * RET_DV], gains[h]) for h in range(RET_HEADS)]
        return (g * _sigmoid(g) * jnp.concatenate(parts, axis=-1),)
    w = RET_HEADS * RET_DV
    return _rows(fn, [out, (proj, w, 2)], [gn[h:h + 1] for h in range(RET_HEADS)], [(w, BF16)], name=name)[0]


def _ret_gate_bwd(out, proj, gn, dy, name):
    def fn(o, g, d, *gains):
        sg = _sigmoid(g)
        silu = g * sg
        dsilu = sg * (1.0 + g * (1.0 - sg))
        dos, dgs = [], []
        row = lax.broadcasted_iota(jnp.int32, (RET_HEADS, RET_DV), 0)
        dgn = jnp.zeros((RET_HEADS, RET_DV), F32)
        for h in range(RET_HEADS):
            sl = slice(h * RET_DV, (h + 1) * RET_DV)
            oh = o[:, sl]
            dgs.append(d[:, sl] * _rms(oh, gains[h]) * dsilu[:, sl])
            dx, dg = _rms_bwd(oh, d[:, sl] * silu[:, sl], gains[h])
            dos.append(dx)
            dgn = dgn + jnp.where(row == h, _colsum(dg), 0.0)
        return jnp.concatenate(dos, axis=-1), jnp.concatenate(dgs, axis=-1), dgn
    w = RET_HEADS * RET_DV
    return _rows(fn, [out, (proj, w, 2), dy], [gn[h:h + 1] for h in range(RET_HEADS)], [(w, BF16), (w, BF16)],
                 [((RET_HEADS, RET_DV), F32)], name=name, tile=128)


def _mla_tables(T):
    inv = 1.0 / (ROPE_THETA ** (jnp.arange(0, MLA_ROPE, 2, dtype=F32) / MLA_ROPE))
    ang = jnp.arange(T, dtype=F32)[:, None] * inv[None, :]
    c, s = jnp.cos(ang), jnp.sin(ang)
    z32, z64 = jnp.zeros((T, 32), F32), jnp.zeros((T, 64), F32)
    cos_t = jnp.concatenate([c, c, z64], axis=1)
    sin_a = jnp.concatenate([-s, z32, z64], axis=1)
    sin_b = jnp.concatenate([z32, s, z64], axis=1)
    return cos_t, sin_a, sin_b


def _rope_blk(x, ct, sa, sb):
    return x * ct + pltpu.roll(x, 96, 1) * sa + pltpu.roll(x, 32, 1) * sb


def _rope_blk_bwd(d, ct, sa, sb):
    return d * ct + pltpu.roll(d * sa, 32, 1) + pltpu.roll(d * sb, 96, 1)


def _head_norm(x, gain):
    r = lax.rsqrt(jnp.sum(x * x, axis=-1, keepdims=True) / MLA_QKD + EPS)
    return (x * r) * gain


def _mla_prep(q, kv, proj, gq, gk, tabs, name):
    def fn(qv, kvv, kr, ct, sa, sb, gqv, gkv):
        qs, ks, vs = [], [], []
        for h in range(MLA_HEADS):
            b = h * MLA_HP
            y = _head_norm(qv[:, b:b + MLA_HP], gqv)
            qs += [y[:, :128], _rope_blk(y[:, 128:], ct, sa, sb)]
            y = _head_norm(jnp.concatenate([kvv[:, b:b + 128], kr], axis=-1), gkv)
            ks += [y[:, :128], _rope_blk(y[:, 128:], ct, sa, sb)]
            vs.append(kvv[:, b + 128:b + 256])
        return jnp.concatenate(qs, axis=-1), jnp.concatenate(ks, axis=-1), jnp.concatenate(vs, axis=-1)
    w = MLA_HEADS * MLA_HP
    return _rows(fn, [q, kv, (proj, 128, 5), *tabs], [gq, gk],
                 [(w, BF16), (w, BF16), (MLA_HEADS * MLA_VD, BF16)], name=name, tile=128)


def _mla_prep_bwd(q, kv, proj, gq, gk, tabs, dqf, dkf, dvf, name):
    def fn(qv, kvv, kr, ct, sa, sb, dqv, dkv, dvv, gqv, gkv):
        dqs, dkvs = [], []
        dkr = jnp.zeros_like(kr)
        dgq = jnp.zeros((1, MLA_HP), F32)
        dgk = jnp.zeros((1, MLA_HP), F32)
        for h in range(MLA_HEADS):
            b = h * MLA_HP
            dy = jnp.concatenate([dqv[:, b:b + 128], _rope_blk_bwd(dqv[:, b + 128:b + 256], ct, sa, sb)], axis=-1)
            dx, dg = _rms_bwd(qv[:, b:b + MLA_HP], dy, gqv, MLA_QKD)
            dqs.append(dx)
            dgq = dgq + _colsum(dg)
            dy = jnp.concatenate([dkv[:, b:b + 128], _rope_blk_bwd(dkv[:, b + 128:b + 256], ct, sa, sb)], axis=-1)
            dx, dg = _rms_bwd(jnp.concatenate([kvv[:, b:b + 128], kr], axis=-1), dy, gkv, MLA_QKD)
            dkvs += [dx[:, :128], dvv[:, h * MLA_VD:(h + 1) * MLA_VD]]
            dkr = dkr + dx[:, 128:]
            dgk = dgk + _colsum(dg)
        return jnp.concatenate(dqs, axis=-1), jnp.concatenate(dkvs, axis=-1), dkr, dgq, dgk
    w = MLA_HEADS * MLA_HP
    return _rows(fn, [q, kv, (proj, 128, 5), *tabs, dqf, dkf, dvf], [gq, gk],
                 [(w, BF16), (w, BF16), (128, F32)], [((1, MLA_HP), F32), ((1, MLA_HP), F32)], name=name, tile=128)


def _chunk_mask(qi, ki, tq, tk):
    shift = CHUNK.bit_length() - 1
    rq = lax.shift_right_arithmetic(qi * tq + lax.broadcasted_iota(jnp.int32, (tq, tk), 0), shift)
    ck = lax.shift_right_arithmetic(ki * tk + lax.broadcasted_iota(jnp.int32, (tq, tk), 1), shift)
    return ck <= rq


def _flash_fwd(qf, kf, vf, name):
    T = qf.shape[0]
    t = _pick(T, FLASH_T)
    n = T // t
    scale = MLA_QKD ** -0.5

    def body(q_ref, k_ref, v_ref, o_ref, lse_ref, m_s, l_s, acc):
        qi = pl.program_id(1)
        q = q_ref[...]
        m_s[...] = jnp.full_like(m_s, NEG)
        l_s[...] = jnp.zeros_like(l_s)
        acc[...] = jnp.zeros_like(acc)

        def step(kb, masked):
            rows = pl.ds(pl.multiple_of(kb * t, t), t)
            s = _dot_nt(q, k_ref[rows, :]) * scale
            if masked:
                s = jnp.where(_chunk_mask(0, 0, t, t), s, NEG)
            m_prev = m_s[...]
            m_new = jnp.maximum(m_prev, jnp.max(s, axis=-1, keepdims=True))
            alpha = jnp.exp(m_prev - m_new)
            p = jnp.exp(s - _widen(m_new, t))
            l_s[...] = alpha * l_s[...] + sum(p[:, i * 128:(i + 1) * 128] for i in range(t // 128))
            acc[...] = acc[...] * alpha + _dot(p.astype(BF16), v_ref[rows, :])
            m_s[...] = m_new

        @pl.loop(0, qi)
        def _(kb):
            step(kb, False)

        step(qi, True)
        l = jnp.sum(l_s[...], axis=-1, keepdims=True)
        o_ref[...] = acc[...] / l
        lse_ref[...] = m_s[...] + jnp.log(l)

    qmap = lambda h, i: (i, h)
    kmap = lambda h, i: (0, h)
    return pl.pallas_call(
        body, name=name, grid=(MLA_HEADS, n),
        in_specs=[pl.BlockSpec((t, MLA_HP), qmap), pl.BlockSpec((T, MLA_HP), kmap), pl.BlockSpec((T, MLA_VD), kmap)],
        out_specs=[pl.BlockSpec((t, MLA_VD), qmap), pl.BlockSpec((t, MLA_VD), qmap)],
        out_shape=[jax.ShapeDtypeStruct((T, MLA_HEADS * MLA_VD), F32),
                   jax.ShapeDtypeStruct((T, MLA_HEADS * MLA_VD), F32)],
        scratch_shapes=[pltpu.VMEM((t, MLA_VD), F32), pltpu.VMEM((t, MLA_VD), F32), pltpu.VMEM((t, MLA_VD), F32)],
        compiler_params=_cparams(("parallel", "arbitrary")),
    )(qf, kf, vf)


def _flash_delta(o, do, name):
    def fn(ov, dv):
        parts = []
        for h in range(MLA_HEADS):
            sl = slice(h * MLA_VD, (h + 1) * MLA_VD)
            d = jnp.sum(dv[:, sl] * ov[:, sl], axis=-1, keepdims=True)
            parts.append(jnp.broadcast_to(d, (d.shape[0], MLA_VD)))
        return jnp.concatenate(parts, axis=-1), dv
    w = MLA_HEADS * MLA_VD
    return _rows(fn, [o, do], [], [(w, F32), (w, BF16)], name=name)


def _flash_bwd(qf, kf, vf, do16, lse, delta, name):
    T = qf.shape[0]
    t = _pick(T, FLASH_T)
    n = T // t
    scale = MLA_QKD ** -0.5

    def body(q_ref, k_ref, v_ref, do_ref, lse_ref, dl_ref, dq_ref, dk_ref, dv_ref):
        kb = pl.program_id(1)

        @pl.when(kb == 0)
        def _():
            dq_ref[...] = jnp.zeros_like(dq_ref)

        dk_ref[...] = jnp.zeros_like(dk_ref)
        dv_ref[...] = jnp.zeros_like(dv_ref)
        k, v = k_ref[...], v_ref[...]

        def step(qb, masked):
            rows = pl.ds(pl.multiple_of(qb * t, t), t)
            q, dob = q_ref[rows, :], do_ref[rows, :]
            s = _dot_nt(q, k) * scale
            if masked:
                s = jnp.where(_chunk_mask(0, 0, t, t), s, NEG)
            p = jnp.exp(s - _widen(lse_ref[rows, :], t))
            ds = (p * (_dot_nt(dob, v) - _widen(dl_ref[rows, :], t)) * scale).astype(BF16)
            dv_ref[...] += _dot_tn(p.astype(BF16), dob)
            dk_ref[...] += _dot_tn(ds, q)
            dq_ref[rows, :] += _dot(ds, k)

        step(kb, True)

        @pl.loop(kb + 1, n)
        def _(qb):
            step(qb, False)

    qmap = lambda h, j: (0, h)
    kmap = lambda h, j: (j, h)
    return pl.pallas_call(
        body, name=name, grid=(MLA_HEADS, n),
        in_specs=[pl.BlockSpec((T, MLA_HP), qmap), pl.BlockSpec((t, MLA_HP), kmap), pl.BlockSpec((t, MLA_VD), kmap),
                  pl.BlockSpec((T, MLA_VD), qmap), pl.BlockSpec((T, MLA_VD), qmap), pl.BlockSpec((T, MLA_VD), qmap)],
        out_specs=[pl.BlockSpec((T, MLA_HP), qmap), pl.BlockSpec((t, MLA_HP), kmap), pl.BlockSpec((t, MLA_VD), kmap)],
        out_shape=[jax.ShapeDtypeStruct((T, MLA_HEADS * MLA_HP), F32),
                   jax.ShapeDtypeStruct((T, MLA_HEADS * MLA_HP), F32),
                   jax.ShapeDtypeStruct((T, MLA_HEADS * MLA_VD), F32)],
        compiler_params=_cparams(("arbitrary", "arbitrary")),
    )(qf, kf, vf, do16, lse, delta)


MESH = pl.DeviceIdType.MESH
ANY = pl.BlockSpec(memory_space=pl.ANY)
_CHIP_FLIPS = ((1, 0), (0, 1), (1, 1))


def _place():
    return lax.axis_index("x"), lax.axis_index("y"), lax.axis_index("c")


def _other_chip(x, y, k):
    fx, fy = _CHIP_FLIPS[k]
    return ((1 - x) if fx else x), ((1 - y) if fy else y)


def _row_chunks(rows):
    cr = rows // COMM_CHUNKS
    assert cr * COMM_CHUNKS == rows and cr % 16 == 0, rows
    return [(i * cr, cr) for i in range(COMM_CHUNKS)]


def _rows_at(start, size):
    return pl.ds(pl.multiple_of(start, 16), size)


def _local_copies(pairs, sems):
    cps = [pltpu.make_async_copy(s, d, sems.at[i]) for i, (s, d) in enumerate(pairs)]
    for cp in cps:
        cp.start()
    return cps


def _remote(src, dst, send_sems, recv_sems, k, to):
    return pltpu.make_async_remote_copy(src_ref=src, dst_ref=dst, send_sem=send_sems.at[k], recv_sem=recv_sems.at[k],
                                        device_id=to, device_id_type=MESH)


def _index(i):
    return jnp.reshape(i, (1,)).astype(jnp.int32)


def _place_own(wsh, chip):
    R, W = wsh.shape
    tile = _pick(R, 288)

    def body(j_ref, w_ref, out_ref):
        out_ref[...] = w_ref[...]

    return pl.pallas_call(
        body, name="place_own_weights",
        grid_spec=pltpu.PrefetchScalarGridSpec(
            num_scalar_prefetch=1, grid=(R // tile,),
            in_specs=[pl.BlockSpec((tile, W), lambda i, j: (i, 0))],
            out_specs=pl.BlockSpec((None, tile, W), lambda i, j: (j[0], i, 0))),
        out_shape=jax.ShapeDtypeStruct((N_CHIPS, R, W), wsh.dtype),
        compiler_params=_cparams(("arbitrary",)),
    )(_index(chip), wsh)


def _gather_weights(wsh, full):
    R, W = wsh.shape
    H = R // 2

    chunks = _row_chunks(H)
    n = len(chunks)

    def body(w_ref, _full_ref, out_ref, send_sems, recv_sems):
        x, y, c = _place()
        j = 2 * x + y
        sibling = (x, y, 1 - c)
        chips = [_other_chip(x, y, k) for k in range(3)]
        sent = []
        for i, (off, cr) in enumerate(chunks):
            r = _rows_at(c * H + off, cr)
            for k, (px, py) in enumerate(chips):
                cp = _remote(w_ref.at[r], out_ref.at[j, r], send_sems, recv_sems, k * n + i, (px, py, c))
                cp.start()
                sent.append(cp)
        for i, (off, cr) in enumerate(chunks):
            r = _rows_at(c * H + off, cr)
            for k, (px, py) in enumerate(chips):
                blk = out_ref.at[2 * px + py, r]
                _remote(blk, blk, send_sems, recv_sems, k * n + i, (px, py, c)).wait_recv()
                cp = _remote(blk, blk, send_sems, recv_sems, (3 + k) * n + i, sibling)
                cp.start()
                sent.append(cp)
        for i, (off, cr) in enumerate(chunks):
            r = _rows_at((1 - c) * H + off, cr)
            for k, (px, py) in enumerate(chips):
                blk = out_ref.at[2 * px + py, r]
                _remote(blk, blk, send_sems, recv_sems, (3 + k) * n + i, sibling).wait_recv()
        for cp in sent:
            cp.wait_send()

    return pl.pallas_call(
        body, name="gather_weights", in_specs=[ANY, ANY], out_specs=ANY,
        out_shape=jax.ShapeDtypeStruct((N_CHIPS, R, W), wsh.dtype), input_output_aliases={1: 0},
        scratch_shapes=[pltpu.SemaphoreType.DMA((6 * n,)), pltpu.SemaphoreType.DMA((6 * n,))],
    )(wsh, full)


def _swap_halves(g):
    _, R, W = g.shape
    H = R // 2

    chunks = _row_chunks(H)
    n = len(chunks)

    def body(g_ref, recv_ref, send_sems, recv_sems):
        x, y, c = _place()
        sent = []
        for jj in range(N_CHIPS):
            for i, (off, cr) in enumerate(chunks):
                cp = _remote(g_ref.at[jj, _rows_at((1 - c) * H + off, cr)], recv_ref.at[jj, pl.ds(off, cr)],
                             send_sems, recv_sems, jj * n + i, (x, y, 1 - c))
                cp.start()
                sent.append(cp)
        for cp in sent:
            cp.wait()

    return pl.pallas_call(
        body, name="grad_swap_halves", in_specs=[ANY], out_specs=ANY,
        out_shape=jax.ShapeDtypeStruct((N_CHIPS, H, W), g.dtype),
        scratch_shapes=[pltpu.SemaphoreType.DMA((N_CHIPS * n,)), pltpu.SemaphoreType.DMA((N_CHIPS * n,))],
    )(g)


def _pair_sum(g, recv, core):
    _, H, W = recv.shape
    tile = _pick(H, 288)

    def body(c_ref, own_ref, recv_ref, a32_ref, a16_ref):
        s = own_ref[...].astype(F32) + recv_ref[...].astype(F32)
        a32_ref[...] = s
        a16_ref[...] = s.astype(BF16)

    blk = pl.BlockSpec((None, tile, W), lambda jj, i, c: (jj, i, 0))
    return pl.pallas_call(
        body, name="grad_pair_sum",
        grid_spec=pltpu.PrefetchScalarGridSpec(
            num_scalar_prefetch=1, grid=(N_CHIPS, H // tile),
            in_specs=[pl.BlockSpec((None, None, tile, W), lambda jj, i, c: (jj, c[0], i, 0)), blk],
            out_specs=[blk, blk]),
        out_shape=[jax.ShapeDtypeStruct((N_CHIPS, H, W), F32), jax.ShapeDtypeStruct((N_CHIPS, H, W), BF16)],
        compiler_params=_cparams(("arbitrary", "arbitrary")),
    )(_index(core), g.reshape(N_CHIPS, 2, H, W), recv)


def _chip_sum(a32, got, chip, core):
    _, H, W = a32.shape
    tile = _pick(H, 288)

    def body(s_ref, own_ref, g0_ref, g1_ref, g2_ref, out_ref):
        out_ref[...] = ((own_ref[...] + g0_ref[...].astype(F32)) + g1_ref[...].astype(F32)) + g2_ref[...].astype(F32)

    def got_spec(k):
        return pl.BlockSpec((None, tile, W), lambda i, s, k=k: (k, i, 0))

    return pl.pallas_call(
        body, name="grad_chip_sum",
        grid_spec=pltpu.PrefetchScalarGridSpec(
            num_scalar_prefetch=1, grid=(H // tile,),
            in_specs=[pl.BlockSpec((None, tile, W), lambda i, s: (s[0], i, 0)), got_spec(0), got_spec(1), got_spec(2)],
            out_specs=pl.BlockSpec((None, tile, W), lambda i, s: (s[1], i, 0))),
        out_shape=jax.ShapeDtypeStruct((2, H, W), F32),
        compiler_params=_cparams(("arbitrary",)),
    )(jnp.concatenate([_index(chip), _index(core)]), a32, got, got, got)


def _scatter_chips(a16):
    _, H, W = a16.shape
    chunks = _row_chunks(H)
    n = len(chunks)

    def body(a16_ref, recv_ref, send_sems, recv_sems):
        x, y, c = _place()
        j = 2 * x + y
        sent = []
        for i, (off, cr) in enumerate(chunks):
            r = pl.ds(off, cr)
            for k in range(3):
                px, py = _other_chip(x, y, k)
                pj = 2 * px + py
                cp = _remote(a16_ref.at[pj, r], recv_ref.at[(j - pj + 4) % 4 - 1, r], send_sems, recv_sems,
                             k * n + i, (px, py, c))
                cp.start()
                sent.append(cp)
        for i, (off, cr) in enumerate(chunks):
            r = pl.ds(off, cr)
            for k in range(3):
                px, py = _other_chip(x, y, k)
                pj = 2 * px + py
                slot = recv_ref.at[(pj - j + 4) % 4 - 1, r]
                _remote(slot, slot, send_sems, recv_sems, k * n + i, (px, py, c)).wait_recv()
        for cp in sent:
            cp.wait_send()

    return pl.pallas_call(
        body, name="grad_scatter_chips", in_specs=[ANY], out_specs=ANY,
        out_shape=jax.ShapeDtypeStruct((3, H, W), a16.dtype),
        scratch_shapes=[pltpu.SemaphoreType.DMA((3 * n,)), pltpu.SemaphoreType.DMA((3 * n,))],
    )(a16)


def _share_halves(red):
    _, H, W = red.shape
    chunks = _row_chunks(H)
    n = len(chunks)

    def body(_in_ref, out_ref, send_sems, recv_sems):
        x, y, c = _place()
        sent = []
        for i, (off, cr) in enumerate(chunks):
            blk = out_ref.at[c, pl.ds(off, cr)]
            cp = _remote(blk, blk, send_sems, recv_sems, i, (x, y, 1 - c))
            cp.start()
            sent.append(cp)
        for cp in sent:
            cp.wait()

    return pl.pallas_call(
        body, name="grad_share_halves", in_specs=[ANY], out_specs=ANY,
        out_shape=jax.ShapeDtypeStruct(red.shape, red.dtype), input_output_aliases={0: 0},
        scratch_shapes=[pltpu.SemaphoreType.DMA((n,)), pltpu.SemaphoreType.DMA((n,))],
    )(red)


def _allsum_small(v, name):
    R, W = v.shape
    n_dev = 8
    vm = pl.BlockSpec(memory_space=pltpu.VMEM)

    def body(v_ref, out_ref, buf, send_sems, recv_sems):
        x, y, c = _place()
        me = 4 * x + 2 * y + c
        buf[me] = v_ref[...]
        sent = []
        for k in range(1, n_dev):
            peer = ((1 - x) if k & 4 else x, (1 - y) if k & 2 else y, (1 - c) if k & 1 else c)
            cp = _remote(v_ref, buf.at[me], send_sems, recv_sems, k - 1, peer)
            cp.start()
            sent.append(cp)
        for cp in sent:
            cp.wait_recv()
        for cp in sent:
            cp.wait_send()
        acc = buf[0]
        for q in range(1, n_dev):
            acc = acc + buf[q]
        out_ref[...] = acc

    return pl.pallas_call(
        body, name=name, in_specs=[vm], out_specs=vm, out_shape=jax.ShapeDtypeStruct((R, W), v.dtype),
        scratch_shapes=[pltpu.VMEM((n_dev, R, W), v.dtype), pltpu.SemaphoreType.DMA((n_dev - 1,)),
                        pltpu.SemaphoreType.DMA((n_dev - 1,))],
    )(v)


def _reduce_grads(g16, chip, core):
    _, R, W = g16.shape
    a32, a16 = _pair_sum(g16, _swap_halves(g16), core)
    red = _chip_sum(a32, _scatter_chips(a16), chip, core)
    return _share_halves(red).reshape(R, W)


def _adamw(w, g, m, v, name):
    shape = w.shape
    cols = shape[-1]

    def fn(wv, gv, mv, vv):
        m2 = ADAM_B1 * mv + (1.0 - ADAM_B1) * gv
        v2 = ADAM_B2 * vv + (1.0 - ADAM_B2) * jnp.square(gv)
        m_hat = m2 / (1.0 - ADAM_B1 ** ADAM_STEP)
        v_hat = v2 / (1.0 - ADAM_B2 ** ADAM_STEP)
        return -ADAM_LR * (m_hat / (jnp.sqrt(v_hat) + ADAM_EPS) + ADAM_WD * wv), m2, v2

    w2, m2, v2 = (t.reshape(-1, cols) for t in (w, m, v))
    rows = w2.shape[0]
    tile = 256 if rows % 8 == 0 else rows
    res = _rows(fn, [w2, g.reshape(rows, cols), m2, v2], [], [(cols, F32)] * 3, name=name, tile=tile)
    return tuple(t.reshape(shape) for t in res)


def _add_res(acc, r):
    return (r + acc,)


def _tail_fwd(h1, p16, W, i, tag):
    hn2 = _norm_fwd(h1, W["mlp_norm"][i:i + 1], f"{tag}_mlp_norm")
    z, a = _mm(hn2, W["mlp_w1"][i], bblk=True, outs=[F32, BF16], name=f"{tag}_mlp_w1",
               epilogue=lambda acc: (acc, jnp.square(jnp.maximum(acc, 0.0))))
    h2 = _mm(a, W["mlp_w2"][i], extras=[h1], epilogue=_add_res, name=f"{tag}_mlp_w2")
    hn3 = _norm_fwd(h2, W["ple_norm"][i:i + 1], f"{tag}_ple_norm")
    gl = _mm(hn3, W["ple_gate_w"][i], name=f"{tag}_ple_gate")
    h3, pp = _mm(p16[i], W["ple_proj_w"][i], bblk=True, extras=[gl, h2], outs=[F32, F32], name=f"{tag}_ple_proj",
                 epilogue=lambda acc, g, h: (h + _sigmoid(g) * acc, acc))
    return h3, (h1, hn2, z, a, h2, hn3, gl, pp)


def _tail_bwd(dh3, saved, p16, W, i, tag):
    h1, hn2, z, a, h2, hn3, gl, pp = saved

    def gate_bwd(d, g, ppv):
        gate = _sigmoid(g)
        return d * gate, d * ppv * gate * (1.0 - gate)

    dpp, dgl = _rows(gate_bwd, [dh3, gl, pp], [], [(D_MODEL, BF16), (D_MODEL, BF16)], name=f"{tag}_ple_gate_bwd")
    d_proj = _mm(p16[i], dpp, ta=True, oblk=PLE_DIM, name=f"{tag}_d_ple_proj")
    d_gate = _mm(hn3, dgl, ta=True, name=f"{tag}_d_ple_gate")
    dhn3 = _mm(dgl, W["ple_gate_w"][i], tb=True, name=f"{tag}_ple_gate_dx")
    dh2, d_ple_norm = _norm_bwd(h2, dhn3, W["ple_norm"][i:i + 1], dh3, f"{tag}_ple_norm_bwd")
    d_w2 = _mm(a, dh2, ta=True, name=f"{tag}_d_mlp_w2")
    dz = _mm(dh2, W["mlp_w2"][i], tb=True, extras=[z], outs=[BF16], name=f"{tag}_mlp_w2_dx",
             epilogue=lambda acc, zv: (acc * (2.0 * jnp.maximum(zv, 0.0)),))
    d_w1 = _mm(hn2, dz, ta=True, oblk=D_MODEL, name=f"{tag}_d_mlp_w1")
    dhn2 = _mm(dz, W["mlp_w1"][i], tb=True, bblk=True, name=f"{tag}_mlp_w1_dx")
    dh1, d_mlp_norm = _norm_bwd(h1, dhn2, W["mlp_norm"][i:i + 1], dh2, f"{tag}_mlp_norm_bwd")
    return dh1, dict(mlp_norm=d_mlp_norm, mlp_w1=d_w1, mlp_w2=d_w2, ple_norm=d_ple_norm,
                     ple_gate_w=d_gate, ple_proj_w=d_proj)


def _ret_layer_fwd(h0, W, tabs):
    hn = _norm_fwd(h0, W["mix_norm"][0:1], "ret_mix_norm")
    proj = _mm(hn, W["ret_w_in"], bblk=True, name="ret_w_in")
    out, states = _ret_fwd(proj, tabs, "ret_scan")
    y = _ret_gate(out, proj, W["ret_gn"], "ret_gate")
    h1 = _mm(y, W["ret_w_out"], extras=[h0], epilogue=_add_res, name="ret_w_out")
    return h1, (h0, hn, proj, out, states, y)


def _ret_layer_bwd(dh1, saved, W, tabs):
    h0, hn, proj, out, states, y = saved
    d_w_out = _mm(y, dh1, ta=True, name="d_ret_w_out")
    dy = _mm(dh1, W["ret_w_out"], tb=True, name="ret_w_out_dx")
    dout, dg, d_gn = _ret_gate_bwd(out, proj, W["ret_gn"], dy, "ret_gate_bwd")
    dq, dk, dv = _ret_bwd(proj, states, dout, tabs, "ret_scan_bwd")
    dproj = jnp.concatenate([dq, dk, dv, dg], axis=1)
    d_w_in = _mm(hn, dproj, ta=True, oblk=proj.shape[1] // N_CHIPS, name="d_ret_w_in")
    dhn = _mm(dproj, W["ret_w_in"], tb=True, bblk=True, name="ret_w_in_dx")
    dh0, d_mix = _norm_bwd(h0, dhn, W["mix_norm"][0:1], dh1, "ret_mix_norm_bwd")
    return dh0, dict(mix_norm=d_mix, ret_w_in=d_w_in, ret_gn=d_gn, ret_w_out=d_w_out)


def _mla_layer_fwd(h0, W, tabs):
    hn = _norm_fwd(h0, W["mix_norm"][1:2], "mla_mix_norm")
    proj = _mm(hn, W["mla_w_in"], name="mla_w_in")

    def low_rank_norm(pv, gq, gkv):
        return _rms(pv[:, :MLA_Q_RANK], gq), _rms(pv[:, MLA_Q_RANK:MLA_Q_RANK + MLA_KV_RANK], gkv)

    cqn, ckvn = _rows(low_rank_norm, [proj], [W["mla_q_a_norm"], W["mla_kv_a_norm"]],
                      [(MLA_Q_RANK, BF16), (MLA_KV_RANK, BF16)], name="mla_low_rank_norm")
    q = _mm(cqn, W["mla_w_uq"], name="mla_w_uq")
    kv = _mm(ckvn, W["mla_w_ukv"], bblk=True, name="mla_w_ukv")
    qf, kf, vf = _mla_prep(q, kv, proj, W["mla_q_norm"], W["mla_k_norm"], tabs, "mla_prep")
    o, lse = _flash_fwd(qf, kf, vf, "mla_flash")
    h1 = _mm(o, W["mla_w_out"], extras=[h0], epilogue=_add_res, name="mla_w_out")
    return h1, (h0, hn, proj, cqn, ckvn, q, kv, qf, kf, vf, o, lse)


def _mla_layer_bwd(dh1, saved, W, tabs):
    h0, hn, proj, cqn, ckvn, q, kv, qf, kf, vf, o, lse = saved
    d_w_out = _mm(o, dh1, ta=True, name="d_mla_w_out")
    do = _mm(dh1, W["mla_w_out"], tb=True, name="mla_w_out_dx")
    delta, do16 = _flash_delta(o, do, "mla_flash_delta")
    dqf, dkf, dvf = _flash_bwd(qf, kf, vf, do16, lse, delta, "mla_flash_bwd")
    dq, dkv, dkr, d_gq, d_gk = _mla_prep_bwd(q, kv, proj, W["mla_q_norm"], W["mla_k_norm"], tabs, dqf, dkf, dvf,
                                             "mla_prep_bwd")
    d_w_uq = _mm(cqn, dq, ta=True, name="d_mla_w_uq")
    dcqn = _mm(dq, W["mla_w_uq"], tb=True, name="mla_w_uq_dx")
    d_w_ukv = _mm(ckvn, dkv, ta=True, oblk=kv.shape[1] // N_CHIPS, name="d_mla_w_ukv")
    dckvn = _mm(dkv, W["mla_w_ukv"], tb=True, bblk=True, name="mla_w_ukv_dx")

    def low_rank_bwd(pv, dcq, dckv, dkr_v, gq, gkv):
        dxq, dgq = _rms_bwd(pv[:, :MLA_Q_RANK], dcq, gq)
        dxkv, dgkv = _rms_bwd(pv[:, MLA_Q_RANK:MLA_Q_RANK + MLA_KV_RANK], dckv, gkv)
        return jnp.concatenate([dxq, dxkv, dkr_v], axis=-1), _colsum(dgq), _colsum(dgkv)

    dproj, d_gqa, d_gkva = _rows(low_rank_bwd, [proj, dcqn, dckvn, dkr], [W["mla_q_a_norm"], W["mla_kv_a_norm"]],
                                 [(MLA_IN_PAD, BF16)], [((1, MLA_Q_RANK), F32), ((1, MLA_KV_RANK), F32)],
                                 name="mla_low_rank_norm_bwd")
    d_w_in = _mm(hn, dproj, ta=True, name="d_mla_w_in")
    dhn = _mm(dproj, W["mla_w_in"], tb=True, name="mla_w_in_dx")
    dh0, d_mix = _norm_bwd(h0, dhn, W["mix_norm"][1:2], dh1, "mla_mix_norm_bwd")
    return dh0, dict(mix_norm=d_mix, mla_w_in=d_w_in, mla_q_a_norm=d_gqa, mla_kv_a_norm=d_gkva, mla_w_uq=d_w_uq,
                     mla_w_ukv=d_w_ukv, mla_q_norm=d_gq, mla_k_norm=d_gk, mla_w_out=d_w_out)


def _local_step(x, p16, target, W):
    T = x.shape[0]
    ret_tabs, mla_tabs = _ret_tables(T), _mla_tables(T)
    h1, s_ret = _ret_layer_fwd(x, W, ret_tabs)
    h3, s_tail0 = _tail_fwd(h1, p16, W, 0, "l0")
    h4, s_mla = _mla_layer_fwd(h3, W, mla_tabs)
    y, s_tail1 = _tail_fwd(h4, p16, W, 1, "l1")

    def loss_head(yv, tv):
        e = yv - tv
        return e * (1.0 / D_MODEL), jnp.full((1, 128), 0.5 / D_MODEL, F32) * jnp.sum(e * e)

    dy, loss = _rows(loss_head, [y, target], [], [(D_MODEL, F32)], [((1, 128), F32)], name="loss_head")
    dh4, g_tail1 = _tail_bwd(dy, s_tail1, p16, W, 1, "l1")
    dh3, g_mla = _mla_layer_bwd(dh4, s_mla, W, mla_tabs)
    dh1, g_tail0 = _tail_bwd(dh3, s_tail0, p16, W, 0, "l0")
    dx, g_ret = _ret_layer_bwd(dh1, s_ret, W, ret_tabs)
    return loss, dx, g_ret, g_tail0, g_mla, g_tail1


_BIG = ("ret_w_in", "ret_w_out", "mla_w_in", "mla_w_uq", "mla_w_ukv", "mla_w_out", "mlp_w1", "mlp_w2",
        "ple_gate_w", "ple_proj_w")
_SMALL = ("mix_norm", "ret_gn", "mla_q_a_norm", "mla_kv_a_norm", "mla_q_norm", "mla_k_norm", "mlp_norm", "ple_norm")
_ORDER = ("mix_norm", "ret_w_in", "ret_gn", "ret_w_out", "mla_w_in", "mla_q_a_norm", "mla_kv_a_norm", "mla_w_uq",
          "mla_w_ukv", "mla_q_norm", "mla_k_norm", "mla_w_out", "mlp_norm", "mlp_w1", "mlp_w2", "ple_norm",
          "ple_gate_w", "ple_proj_w")
SMALL_ROWS = 16


def _pack_rows(shards):
    return [shards[n].size // PACK_W for n in _BIG]


def _split_rows(buf, shards):
    out, off = {}, 0
    for n, r in zip(_BIG, _pack_rows(shards)):
        out[n] = buf[..., off:off + r, :]
        off += r
    return out


def _full_weights(gath, shards, small):
    b = _split_rows(gath, shards)
    n = N_CHIPS
    W = {}
    W["ret_w_in"] = b["ret_w_in"].reshape(n, D_MODEL, -1)
    W["ret_w_out"] = b["ret_w_out"].reshape(-1, D_MODEL)
    W["mla_w_in"] = jnp.pad(b["mla_w_in"].reshape(D_MODEL, MLA_IN), ((0, 0), (0, MLA_IN_PAD - MLA_IN)))
    uq = b["mla_w_uq"].reshape(n, MLA_Q_RANK, MLA_HEADS // n, MLA_QKD).transpose(1, 0, 2, 3)
    W["mla_w_uq"] = jnp.pad(uq, ((0, 0), (0, 0), (0, 0), (0, MLA_HP - MLA_QKD))).reshape(MLA_Q_RANK, MLA_HEADS * MLA_HP)
    W["mla_w_ukv"] = b["mla_w_ukv"].reshape(n, MLA_KV_RANK, -1)
    W["mla_w_out"] = b["mla_w_out"].reshape(-1, D_MODEL)
    w1 = b["mlp_w1"].reshape(n, 2, D_MODEL, -1)
    W["mlp_w1"] = [w1[:, i] for i in range(2)]
    w2 = b["mlp_w2"].reshape(n, 2, -1, D_MODEL)
    W["mlp_w2"] = [w2[:, i].reshape(-1, D_MODEL) for i in range(2)]
    wg = b["ple_gate_w"].reshape(n, 2, -1, D_MODEL)
    W["ple_gate_w"] = [wg[:, i].reshape(-1, D_MODEL) for i in range(2)]
    wp = b["ple_proj_w"].reshape(n, 2, PLE_DIM, -1)
    W["ple_proj_w"] = [wp[:, i] for i in range(2)]
    W["ret_gn"] = small[0:2].reshape(RET_HEADS, RET_DV)
    W["mla_q_a_norm"] = small[2:3, :MLA_Q_RANK]
    W["mla_kv_a_norm"] = small[3:4, :MLA_KV_RANK]
    return W


def _pack_grads(g_ret, g_tail0, g_mla, g_tail1):
    n = N_CHIPS

    def both(name):
        return jnp.stack([g_tail0[name].reshape(n, -1, PACK_W), g_tail1[name].reshape(n, -1, PACK_W)], axis=1)

    uq = g_mla["mla_w_uq"].reshape(MLA_Q_RANK, n, MLA_HEADS // n, MLA_HP)[..., :MLA_QKD].transpose(1, 0, 2, 3)
    parts = dict(
        ret_w_in=g_ret["ret_w_in"], ret_w_out=g_ret["ret_w_out"], mla_w_in=g_mla["mla_w_in"][:, :MLA_IN],
        mla_w_uq=uq, mla_w_ukv=g_mla["mla_w_ukv"], mla_w_out=g_mla["mla_w_out"], mlp_w1=both("mlp_w1"),
        mlp_w2=both("mlp_w2"), ple_gate_w=both("ple_gate_w"), ple_proj_w=both("ple_proj_w"))
    return jnp.concatenate([parts[k].astype(BF16).reshape(n, -1, PACK_W) for k in _BIG], axis=1)


def _pad_row(v):
    v = v.reshape(1, -1)
    return jnp.pad(v, ((0, 0), (0, PACK_W - v.shape[1])))


def kernel(x, p, mix_norm, ret_w_in, ret_gn, ret_w_out, mla_w_in, mla_q_a_norm, mla_kv_a_norm, mla_w_uq, mla_w_ukv, mla_q_norm, mla_k_norm, mla_w_out, mlp_norm, mlp_w1, mlp_w2, ple_norm, ple_gate_w, ple_proj_w, loss_target, m_mix_norm, m_ret_w_in, m_ret_gn, m_ret_w_out, m_mla_w_in, m_mla_q_a_norm, m_mla_kv_a_norm, m_mla_w_uq, m_mla_w_ukv, m_mla_q_norm, m_mla_k_norm, m_mla_w_out, m_mlp_norm, m_mlp_w1, m_mlp_w2, m_ple_norm, m_ple_gate_w, m_ple_proj_w, v_mix_norm, v_ret_w_in, v_ret_gn, v_ret_w_out, v_mla_w_in, v_mla_q_a_norm, v_mla_kv_a_norm, v_mla_w_uq, v_mla_w_ukv, v_mla_q_norm, v_mla_k_norm, v_mla_w_out, v_mlp_norm, v_mlp_w1, v_mlp_w2, v_ple_norm, v_ple_gate_w, v_ple_proj_w):
    w = dict(mix_norm=mix_norm, ret_w_in=ret_w_in, ret_gn=ret_gn, ret_w_out=ret_w_out, mla_w_in=mla_w_in,
             mla_q_a_norm=mla_q_a_norm, mla_kv_a_norm=mla_kv_a_norm, mla_w_uq=mla_w_uq, mla_w_ukv=mla_w_ukv,
             mla_q_norm=mla_q_norm, mla_k_norm=mla_k_norm, mla_w_out=mla_w_out, mlp_norm=mlp_norm, mlp_w1=mlp_w1,
             mlp_w2=mlp_w2, ple_norm=ple_norm, ple_gate_w=ple_gate_w, ple_proj_w=ple_proj_w)
    m = dict(mix_norm=m_mix_norm, ret_w_in=m_ret_w_in, ret_gn=m_ret_gn, ret_w_out=m_ret_w_out, mla_w_in=m_mla_w_in,
             mla_q_a_norm=m_mla_q_a_norm, mla_kv_a_norm=m_mla_kv_a_norm, mla_w_uq=m_mla_w_uq, mla_w_ukv=m_mla_w_ukv,
             mla_q_norm=m_mla_q_norm, mla_k_norm=m_mla_k_norm, mla_w_out=m_mla_w_out, mlp_norm=m_mlp_norm,
             mlp_w1=m_mlp_w1, mlp_w2=m_mlp_w2, ple_norm=m_ple_norm, ple_gate_w=m_ple_gate_w, ple_proj_w=m_ple_proj_w)
    v = dict(mix_norm=v_mix_norm, ret_w_in=v_ret_w_in, ret_gn=v_ret_gn, ret_w_out=v_ret_w_out, mla_w_in=v_mla_w_in,
             mla_q_a_norm=v_mla_q_a_norm, mla_kv_a_norm=v_mla_kv_a_norm, mla_w_uq=v_mla_w_uq, mla_w_ukv=v_mla_w_ukv,
             mla_q_norm=v_mla_q_norm, mla_k_norm=v_mla_k_norm, mla_w_out=v_mla_w_out, mlp_norm=v_mlp_norm,
             mlp_w1=v_mlp_w1, mlp_w2=v_mlp_w2, ple_norm=v_ple_norm, ple_gate_w=v_ple_gate_w, ple_proj_w=v_ple_proj_w)
    xi, yi, ci = _place()
    chip = 2 * xi + yi
    n = N_CHIPS

    wsh = jnp.concatenate([w[k].astype(BF16).reshape(-1, PACK_W) for k in _BIG], axis=0)
    gath = _gather_weights(wsh, _place_own(wsh, chip))
    on = (jnp.arange(n) == chip) & (ci == 0)
    gn_rows = jnp.where(on[None, :, None], ret_gn[0][:, None, :], 0.0).reshape(2, PACK_W)
    qa_row = _pad_row(jnp.where(on[:, None], mla_q_a_norm, 0.0))
    kva_row = _pad_row(jnp.where(on[:, None], mla_kv_a_norm, 0.0))
    small_in = jnp.concatenate([gn_rows, qa_row, kva_row, jnp.zeros((4, PACK_W), F32)], axis=0)
    small = _allsum_small(small_in, "gather_gains")
    W = _full_weights(gath, {k: w[k] for k in _BIG}, small)
    W["mix_norm"], W["mlp_norm"], W["ple_norm"] = mix_norm, mlp_norm, ple_norm
    W["mla_q_norm"] = jnp.pad(mla_q_norm, ((0, 0), (0, MLA_HP - MLA_QKD)))
    W["mla_k_norm"] = jnp.pad(mla_k_norm, ((0, 0), (0, MLA_HP - MLA_QKD)))

    loss, dx, g_ret, g_tail0, g_mla, g_tail1 = _local_step(x[0], p[:, 0].astype(BF16), loss_target[0], W)

    red = _reduce_grads(_pack_grads(g_ret, g_tail0, g_mla, g_tail1), chip, ci)
    g_big = _split_rows(red, {k: w[k] for k in _BIG})
    small_g = jnp.concatenate([
        g_ret["mix_norm"], g_mla["mix_norm"], g_tail0["mlp_norm"], g_tail1["mlp_norm"], g_tail0["ple_norm"],
        g_tail1["ple_norm"], g_ret["ret_gn"].reshape(2, PACK_W), _pad_row(g_mla["mla_q_a_norm"]),
        _pad_row(g_mla["mla_kv_a_norm"]), _pad_row(g_mla["mla_q_norm"][:, :MLA_QKD]),
        _pad_row(g_mla["mla_k_norm"][:, :MLA_QKD]), _pad_row(loss[:, :1]), jnp.zeros((3, PACK_W), F32)], axis=0)
    tot = _allsum_small(small_g, "sum_small_grads")
    gn_all = tot[6:8].reshape(RET_HEADS, n, -1)
    g_small = dict(
        mix_norm=tot[0:2], mlp_norm=tot[2:4], ple_norm=tot[4:6],
        ret_gn=lax.dynamic_index_in_dim(gn_all, chip, axis=1, keepdims=False),
        mla_q_a_norm=lax.dynamic_index_in_dim(tot[8, :MLA_Q_RANK].reshape(n, -1), chip, axis=0, keepdims=True),
        mla_kv_a_norm=lax.dynamic_index_in_dim(tot[9, :MLA_KV_RANK].reshape(n, -1), chip, axis=0, keepdims=True),
        mla_q_norm=tot[10:11, :MLA_QKD], mla_k_norm=tot[11:12, :MLA_QKD])
    loss_out = tot[12, 0]

    grads, deltas, new_m, new_v = [], [], [], []
    for k in _ORDER:
        g = g_big[k] if k in g_big else g_small[k]
        g = g.reshape(w[k].shape)
        d, m2, v2 = _adamw(w[k], g, m[k], v[k], f"adamw_{k}")
        grads.append(g)
        deltas.append(d)
        new_m.append(m2)
        new_v.append(v2)
    return (loss_out, dx[None], *grads, *deltas, *new_m, *new_v)
```

```python
import functools

import jax
import jax.numpy as jnp
from jax import lax
from jax.experimental import pallas as pl
from jax.experimental.pallas import tpu as pltpu

F32 = jnp.float32
BF16 = jnp.bfloat16

EPS = 1e-6
D_MODEL = 1024
CHUNK = 64
ROPE_THETA = 10000.0
RET_HEADS = 4
RET_DK = 256
RET_DV = 512
MLA_HEADS = 8
MLA_NOPE = 128
MLA_ROPE = 64
MLA_QKD = 192
MLA_VD = 128
MLA_HP = 256
MLA_Q_RANK = 384
MLA_KV_RANK = 256
MLA_IN = 704
MLA_IN_PAD = 768
D_FF = 4096
PLE_DIM = 256
N_CHIPS = 4

ADAM_LR = 0.001
ADAM_B1 = 0.9
ADAM_B2 = 0.999
ADAM_EPS = 1e-08
ADAM_WD = 0.01
ADAM_STEP = 10

VMEM_LIMIT = 56 * 1024 * 1024
PACK_W = 1024
NEG = -1e30
FLASH_T = 512


def _cparams(sem=None):
    return pltpu.CompilerParams(dimension_semantics=sem, vmem_limit_bytes=VMEM_LIMIT)


def _pick(dim, pref):
    if dim <= pref:
        return dim
    t = pref
    while dim % t:
        t //= 2
    return t


def _mm(a, b, *, name, ta=False, tb=False, bblk=False, outs=None, extras=(), epilogue=None, dw=None,
        tm=1024, tn=512):
    if ta:
        K, M = a.shape
    else:
        M, K = a.shape
    if bblk and tb:
        nb, N, Kq = b.shape
        assert nb * Kq == K
    elif bblk:
        nb, Kb, Nq = b.shape
        N = nb * Nq
        assert Kb == K
    else:
        N = b.shape[0] if tb else b.shape[1]
    tn = _pick(Nq if (bblk and not tb) else N, tn)
    if dw is not None and dw[0] == "cols":
        tn = _pick(N // N_CHIPS, tn)
    tm = _pick(M // N_CHIPS if (dw is not None and dw[0] == "rows") else M, tm)
    grid = (M // tm, N // tn)

    a_spec = pl.BlockSpec((K, tm), lambda i, j: (0, i)) if ta else pl.BlockSpec((tm, K), lambda i, j: (i, 0))
    if bblk and tb:
        b_spec = pl.BlockSpec((nb, tn, Kq), lambda i, j: (0, j, 0))
    elif bblk:
        npb = Nq // tn
        b_spec = pl.BlockSpec((None, K, tn), lambda i, j: (j // npb, 0, j % npb))
    elif tb:
        b_spec = pl.BlockSpec((tn, K), lambda i, j: (j, 0))
    else:
        b_spec = pl.BlockSpec((K, tn), lambda i, j: (0, j))
    in_specs = [a_spec, b_spec] + [pl.BlockSpec((tm, tn), lambda i, j: (i, j)) for _ in extras]
    args = [a, b, *extras]
    aliases = {}
    if outs is None:
        outs = [F32]
    if dw is None:
        o_specs = [pl.BlockSpec((tm, tn), lambda i, j: (i, j)) for _ in outs]
        o_shapes = [jax.ShapeDtypeStruct((M, N), dt) for dt in outs]
    else:
        kind, layers, layer, into = dw
        if kind == "cols":
            per = (N // N_CHIPS) // tn
            o_specs = [pl.BlockSpec((None, None, tm, tn), lambda i, j: (j // per, layer, i, j % per))]
            o_shapes = [jax.ShapeDtypeStruct((N_CHIPS, layers, M, N // N_CHIPS), outs[0])]
        else:
            per = (M // N_CHIPS) // tm
            o_specs = [pl.BlockSpec((None, None, tm, tn), lambda i, j: (i // per, layer, i % per, j))]
            o_shapes = [jax.ShapeDtypeStruct((N_CHIPS, layers, M // N_CHIPS, N), outs[0])]
        if into is not None:
            aliases = {len(args): 0}
            in_specs.append(pl.BlockSpec(memory_space=pl.ANY))
            args.append(into)
    n_e, n_o = len(extras), len(outs)

    def body(a_ref, b_ref, *rest):
        e_refs, o_refs = rest[:n_e], rest[len(rest) - n_o:]
        av = a_ref[...].astype(BF16)
        if bblk and tb:
            acc = _dot_nt(av[:, :Kq], b_ref[0].astype(BF16))
            for s in range(1, nb):
                acc = acc + _dot_nt(av[:, s * Kq:(s + 1) * Kq], b_ref[s].astype(BF16))
        elif ta:
            acc = _dot_tn(av, b_ref[...].astype(BF16))
        elif tb:
            acc = _dot_nt(av, b_ref[...].astype(BF16))
        else:
            acc = _dot(av, b_ref[...].astype(BF16))
        vals = (acc,) if epilogue is None else epilogue(acc, *[e[...] for e in e_refs])
        for o, v in zip(o_refs, vals):
            o[...] = v.astype(o.dtype)

    res = pl.pallas_call(
        body, name=name, grid=grid, in_specs=in_specs, out_specs=o_specs, out_shape=o_shapes,
        input_output_aliases=aliases, compiler_params=_cparams(("parallel", "arbitrary")),
    )(*args)
    return res[0] if n_o == 1 else res


def _rows(fn, rows, fulls, outs, accs=(), *, name, tile=256):
    first = rows[0][0] if isinstance(rows[0], tuple) else rows[0]
    T = first.shape[0]
    tile = _pick(T, tile)
    in_specs, args = [], []
    for r in rows:
        if isinstance(r, tuple):
            arr, w, cb = r
            in_specs.append(pl.BlockSpec((tile, w), lambda i, cb=cb: (i, cb)))
        else:
            arr = r
            in_specs.append(pl.BlockSpec((tile, arr.shape[1]), lambda i: (i, 0)))
        args.append(arr)
    for f in fulls:
        in_specs.append(pl.BlockSpec(f.shape, lambda i, nd=f.ndim: (0,) * nd))
        args.append(f)
    out_specs = [pl.BlockSpec((tile, w), lambda i: (i, 0)) for w, _ in outs]
    out_specs += [pl.BlockSpec(s, lambda i: (0, 0)) for s, _ in accs]
    out_shape = [jax.ShapeDtypeStruct((T, w), dt) for w, dt in outs]
    out_shape += [jax.ShapeDtypeStruct(s, dt) for s, dt in accs]
    n_in, n_out = len(args), len(outs)

    def body(*refs):
        vals = fn(*[r[...] for r in refs[:n_in]])
        o_refs = refs[n_in:]
        for o, v in zip(o_refs[:n_out], vals[:n_out]):
            o[...] = v.astype(o.dtype)
        first_step = pl.program_id(0) == 0
        for o, v in zip(o_refs[n_out:], vals[n_out:]):
            @pl.when(first_step)
            def _(o=o, v=v):
                o[...] = v.astype(o.dtype)

            @pl.when(jnp.logical_not(first_step))
            def _(o=o, v=v):
                o[...] += v.astype(o.dtype)

    res = pl.pallas_call(
        body, name=name, grid=(T // tile,), in_specs=in_specs, out_specs=out_specs, out_shape=out_shape,
        compiler_params=_cparams(("arbitrary",)),
    )(*args)
    return res


def _rms(x, g):
    r = lax.rsqrt(jnp.mean(x * x, axis=-1, keepdims=True) + EPS)
    return (x * r) * g


def _rms_bwd(x, dy, g, n=None):
    n = x.shape[-1] if n is None else n
    r = lax.rsqrt(jnp.sum(x * x, axis=-1, keepdims=True) / n + EPS)
    xh = x * r
    dxh = dy * g
    dx = r * (dxh - xh * (jnp.sum(dxh * xh, axis=-1, keepdims=True) / n))
    return dx, dy * xh


def _colsum(v):
    return jnp.sum(v, axis=0, keepdims=True)


def _sigmoid(x):
    return 1.0 / (1.0 + jnp.exp(-x))


def _widen(v, width):
    reps = width // v.shape[1]
    return v if reps == 1 else jnp.concatenate([v] * reps, axis=-1)


def _norm_fwd(h, gain, name):
    return _rows(lambda x, g: (_rms(x, g),), [h], [gain], [(h.shape[1], BF16)], name=name)[0]


def _norm_bwd(h, dhn, gain, dres, name):
    def fn(x, dy, dr, g):
        dx, dg = _rms_bwd(x, dy, g)
        return dr + dx, dr + dx, _colsum(dg)
    d = h.shape[1]
    return _rows(fn, [h, dhn, dres], [gain], [(d, F32), (d, BF16)], [((1, d), F32)], name=name)


def _ret_tables(T):
    inv = 1.0 / (ROPE_THETA ** (jnp.arange(0, RET_DK, 2, dtype=F32) / RET_DK))
    ang = jnp.arange(T, dtype=F32)[:, None] * inv[None, :]
    log_gamma = jnp.log(1.0 - 2.0 ** (-5.0 - jnp.arange(RET_HEADS, dtype=F32)))
    idx = jnp.arange(CHUNK, dtype=F32)
    intra = jnp.exp(log_gamma[:, None, None] * jnp.abs(idx[:, None] - idx[None, :]))
    qd = jnp.exp(log_gamma[:, None] * (idx + 1.0))[:, :, None]
    kd = jnp.exp(log_gamma[:, None] * (CHUNK - 1.0 - idx))[:, :, None]
    cd = jnp.exp(log_gamma * CHUNK)[:, None, None]
    return jnp.cos(ang), jnp.sin(ang), intra, qd, kd, cd


def _rope_half(x, c, s):
    x1, x2 = x[:, :RET_DK // 2], x[:, RET_DK // 2:]
    return jnp.concatenate([x1 * c - x2 * s, x2 * c + x1 * s], axis=-1)


def _rope_half_bwd(d, c, s):
    d1, d2 = d[:, :RET_DK // 2], d[:, RET_DK // 2:]
    return jnp.concatenate([d1 * c + d2 * s, d2 * c - d1 * s], axis=-1)


def _dot(a, b):
    return lax.dot_general(a, b, (((1,), (0,)), ((), ())), preferred_element_type=F32)


def _dot_nt(a, b):
    return lax.dot_general(a, b, (((1,), (1,)), ((), ())), preferred_element_type=F32)


def _dot_tn(a, b):
    return lax.dot_general(a, b, (((0,), (0,)), ((), ())), preferred_element_type=F32)


def _ret_specs(T, tb, rev):
    nj = T // tb
    jj = (lambda j: nj - 1 - j) if rev else (lambda j: j)
    kq = RET_HEADS
    vq = 2 * RET_HEADS * RET_DK // RET_DV
    return dict(
        q=pl.BlockSpec((tb, RET_DK), lambda h, j: (jj(j), h)),
        k=pl.BlockSpec((tb, RET_DK), lambda h, j: (jj(j), kq + h)),
        v=pl.BlockSpec((tb, RET_DV), lambda h, j: (jj(j), vq + h)),
        tab=pl.BlockSpec((tb, RET_DK // 2), lambda h, j: (jj(j), 0)),
        intra=pl.BlockSpec((None, CHUNK, CHUNK), lambda h, j: (h, 0, 0)),
        dec=pl.BlockSpec((None, CHUNK, 1), lambda h, j: (h, 0, 0)),
        cd=pl.BlockSpec((None, 1, 1), lambda h, j: (h, 0, 0)),
        o=pl.BlockSpec((tb, RET_DV), lambda h, j: (jj(j), h)),
        s=pl.BlockSpec((None, tb // CHUNK, RET_DK, RET_DV), lambda h, j: (h, jj(j), 0, 0)),
    )


def _ret_fwd(proj, tabs, name):
    T = proj.shape[0]
    cos, sin, intra, qd, kd, cd = tabs
    tb = _pick(T, 512)
    cps = tb // CHUNK
    sp = _ret_specs(T, tb, False)
    scale = RET_DK ** -0.5

    def body(q_ref, k_ref, v_ref, cos_ref, sin_ref, intra_ref, qd_ref, kd_ref, cd_ref, o_ref, s_ref, state):
        @pl.when(pl.program_id(1) == 0)
        def _():
            state[...] = jnp.zeros_like(state)

        for c in range(cps):
            rows = pl.ds(c * CHUNK, CHUNK)
            co, si = cos_ref[rows, :], sin_ref[rows, :]
            q = _rope_half(q_ref[rows, :], co, si)
            k = _rope_half(k_ref[rows, :], co, si) * scale
            vb = v_ref[rows, :].astype(BF16)
            st = state[...]
            sb = st.astype(BF16)
            s_ref[c] = sb
            sc = _dot_nt(q.astype(BF16), k.astype(BF16)) * intra_ref[...]
            inner = _dot(sc.astype(BF16), vb)
            cross = _dot((q * qd_ref[...]).astype(BF16), sb)
            o_ref[rows, :] = inner + cross
            state[...] = st * cd_ref[...] + _dot_tn((k * kd_ref[...]).astype(BF16), vb)

    return pl.pallas_call(
        body, name=name, grid=(RET_HEADS, T // tb),
        in_specs=[sp["q"], sp["k"], sp["v"], sp["tab"], sp["tab"], sp["intra"], sp["dec"], sp["dec"], sp["cd"]],
        out_specs=[sp["o"], sp["s"]],
        out_shape=[jax.ShapeDtypeStruct((T, RET_HEADS * RET_DV), F32),
                   jax.ShapeDtypeStruct((RET_HEADS, T // CHUNK, RET_DK, RET_DV), BF16)],
        scratch_shapes=[pltpu.VMEM((RET_DK, RET_DV), F32)],
        compiler_params=_cparams(("arbitrary", "arbitrary")),
    )(proj, proj, proj, cos, sin, intra, qd, kd, cd)


def _ret_bwd(proj, states, dout, tabs, name):
    T = proj.shape[0]
    cos, sin, intra, qd, kd, cd = tabs
    tb = _pick(T, 512)
    cps = tb // CHUNK
    sp = _ret_specs(T, tb, True)
    scale = RET_DK ** -0.5

    def body(q_ref, k_ref, v_ref, cos_ref, sin_ref, intra_ref, qd_ref, kd_ref, cd_ref, s_ref, do_ref,
             dq_ref, dk_ref, dv_ref, dstate):
        @pl.when(pl.program_id(1) == 0)
        def _():
            dstate[...] = jnp.zeros_like(dstate)

        for c in reversed(range(cps)):
            rows = pl.ds(c * CHUNK, CHUNK)
            co, si = cos_ref[rows, :], sin_ref[rows, :]
            q = _rope_half(q_ref[rows, :], co, si)
            k = _rope_half(k_ref[rows, :], co, si) * scale
            qb, kb = q.astype(BF16), k.astype(BF16)
            vb = v_ref[rows, :].astype(BF16)
            dob = do_ref[rows, :].astype(BF16)
            sb = s_ref[c]
            ia = intra_ref[...]
            pb = (_dot_nt(qb, kb) * ia).astype(BF16)
            dsn = dstate[...]
            dsb = dsn.astype(BF16)
            kdk = (k * kd_ref[...]).astype(BF16)
            qdq = (q * qd_ref[...]).astype(BF16)
            dv = _dot_tn(pb, dob) + _dot(kdk, dsb)
            dpb = (_dot_nt(dob, vb) * ia).astype(BF16)
            dq = _dot(dpb, kb) + _dot_nt(dob, sb) * qd_ref[...]
            dk = _dot_tn(dpb, qb) + _dot_nt(vb, dsb) * kd_ref[...]
            dstate[...] = dsn * cd_ref[...] + _dot_tn(qdq, dob)
            dq_ref[rows, :] = _rope_half_bwd(dq, co, si).astype(BF16)
            dk_ref[rows, :] = _rope_half_bwd(dk * scale, co, si).astype(BF16)
            dv_ref[rows, :] = dv.astype(BF16)

    return pl.pallas_call(
        body, name=name, grid=(RET_HEADS, T // tb),
        in_specs=[sp["q"], sp["k"], sp["v"], sp["tab"], sp["tab"], sp["intra"], sp["dec"], sp["dec"], sp["cd"],
                  sp["s"], sp["o"]],
        out_specs=[sp["q"], sp["q"], sp["o"]],
        out_shape=[jax.ShapeDtypeStruct((T, RET_HEADS * RET_DK), BF16),
                   jax.ShapeDtypeStruct((T, RET_HEADS * RET_DK), BF16),
                   jax.ShapeDtypeStruct((T, RET_HEADS * RET_DV), BF16)],
        scratch_shapes=[pltpu.VMEM((RET_DK, RET_DV), F32)],
        compiler_params=_cparams(("arbitrary", "arbitrary")),
    )(proj, proj, proj, cos, sin, intra, qd, kd, cd, states, dout)


def _ret_gate(out, proj, gn, name):
    def fn(o, g, *gains):
        parts = [_rms(o[:, h * RET_DV:(h + 1) * RET_DV], gains[h]) for h in range(RET_HEADS)]
        return (g * _sigmoid(g) * jnp.concatenate(parts, axis=-1),)
    w = RET_HEADS * RET_DV
    return _rows(fn, [out, (proj, w, 2)], [gn[h:h + 1] for h in range(RET_HEADS)], [(w, BF16)], name=name)[0]


def _ret_gate_bwd(out, proj, gn, dy, name):
    def fn(o, g, d, *gains):
        sg = _sigmoid(g)
        silu = g * sg
        dsilu = sg * (1.0 + g * (1.0 - sg))
        dos, dgs = [], []
        row = lax.broadcasted_iota(jnp.int32, (RET_HEADS, RET_DV), 0)
        dgn = jnp.zeros((RET_HEADS, RET_DV), F32)
        for h in range(RET_HEADS):
            sl = slice(h * RET_DV, (h + 1) * RET_DV)
            oh = o[:, sl]
            dgs.append(d[:, sl] * _rms(oh, gains[h]) * dsilu[:, sl])
            dx, dg = _rms_bwd(oh, d[:, sl] * silu[:, sl], gains[h])
            dos.append(dx)
            dgn = dgn + jnp.where(row == h, _colsum(dg), 0.0)
        return jnp.concatenate(dos, axis=-1), jnp.concatenate(dgs, axis=-1), dgn
    w = RET_HEADS * RET_DV
    return _rows(fn, [out, (proj, w, 2), dy], [gn[h:h + 1] for h in range(RET_HEADS)], [(w, BF16), (w, BF16)],
                 [((RET_HEADS, RET_DV), F32)], name=name, tile=128)


def _mla_tables(T):
    inv = 1.0 / (ROPE_THETA ** (jnp.arange(0, MLA_ROPE, 2, dtype=F32) / MLA_ROPE))
    ang = jnp.arange(T, dtype=F32)[:, None] * inv[None, :]
    c, s = jnp.cos(ang), jnp.sin(ang)
    z32, z64 = jnp.zeros((T, 32), F32), jnp.zeros((T, 64), F32)
    cos_t = jnp.concatenate([c, c, z64], axis=1)
    sin_a = jnp.concatenate([-s, z32, z64], axis=1)
    sin_b = jnp.concatenate([z32, s, z64], axis=1)
    return cos_t, sin_a, sin_b


def _rope_blk(x, ct, sa, sb):
    return x * ct + pltpu.roll(x, 96, 1) * sa + pltpu.roll(x, 32, 1) * sb


def _rope_blk_bwd(d, ct, sa, sb):
    return d * ct + pltpu.roll(d * sa, 32, 1) + pltpu.roll(d * sb, 96, 1)


def _head_norm(x, gain):
    r = lax.rsqrt(jnp.sum(x * x, axis=-1, keepdims=True) / MLA_QKD + EPS)
    return (x * r) * gain


def _mla_prep(q, kv, proj, gq, gk, tabs, name):
    def fn(qv, kvv, kr, ct, sa, sb, gqv, gkv):
        qs, ks, vs = [], [], []
        for h in range(MLA_HEADS):
            b = h * MLA_HP
            y = _head_norm(qv[:, b:b + MLA_HP], gqv)
            qs += [y[:, :128], _rope_blk(y[:, 128:], ct, sa, sb)]
            y = _head_norm(jnp.concatenate([kvv[:, b:b + 128], kr], axis=-1), gkv)
            ks += [y[:, :128], _rope_blk(y[:, 128:], ct, sa, sb)]
            vs.append(kvv[:, b + 128:b + 256])
        return jnp.concatenate(qs, axis=-1), jnp.concatenate(ks, axis=-1), jnp.concatenate(vs, axis=-1)
    w = MLA_HEADS * MLA_HP
    return _rows(fn, [q, kv, (proj, 128, 5), *tabs], [gq, gk],
                 [(w, BF16), (w, BF16), (MLA_HEADS * MLA_VD, BF16)], name=name, tile=128)


def _mla_prep_bwd(q, kv, proj, gq, gk, tabs, dqf, dkf, dvf, name):
    def fn(qv, kvv, kr, ct, sa, sb, dqv, dkv, dvv, gqv, gkv):
        dqs, dkvs = [], []
        dkr = jnp.zeros_like(kr)
        dgq = jnp.zeros((1, MLA_HP), F32)
        dgk = jnp.zeros((1, MLA_HP), F32)
        for h in range(MLA_HEADS):
            b = h * MLA_HP
            dy = jnp.concatenate([dqv[:, b:b + 128], _rope_blk_bwd(dqv[:, b + 128:b + 256], ct, sa, sb)], axis=-1)
            dx, dg = _rms_bwd(qv[:, b:b + MLA_HP], dy, gqv, MLA_QKD)
            dqs.append(dx)
            dgq = dgq + _colsum(dg)
            dy = jnp.concatenate([dkv[:, b:b + 128], _rope_blk_bwd(dkv[:, b + 128:b + 256], ct, sa, sb)], axis=-1)
            dx, dg = _rms_bwd(jnp.concatenate([kvv[:, b:b + 128], kr], axis=-1), dy, gkv, MLA_QKD)
            dkvs += [dx[:, :128], dvv[:, h * MLA_VD:(h + 1) * MLA_VD]]
            dkr = dkr + dx[:, 128:]
            dgk = dgk + _colsum(dg)
        return jnp.concatenate(dqs, axis=-1), jnp.concatenate(dkvs, axis=-1), dkr, dgq, dgk
    w = MLA_HEADS * MLA_HP
    return _rows(fn, [q, kv, (proj, 128, 5), *tabs, dqf, dkf, dvf], [gq, gk],
                 [(w, BF16), (w, BF16), (128, F32)], [((1, MLA_HP), F32), ((1, MLA_HP), F32)], name=name, tile=128)


def _chunk_mask(qi, ki, tq, tk):
    shift = CHUNK.bit_length() - 1
    rq = lax.shift_right_arithmetic(qi * tq + lax.broadcasted_iota(jnp.int32, (tq, tk), 0), shift)
    ck = lax.shift_right_arithmetic(ki * tk + lax.broadcasted_iota(jnp.int32, (tq, tk), 1), shift)
    return ck <= rq


def _flash_fwd(qf, kf, vf, name):
    T = qf.shape[0]
    t = _pick(T, FLASH_T)
    n = T // t
    scale = MLA_QKD ** -0.5

    def body(q_ref, k_ref, v_ref, o_ref, lse_ref, m_s, l_s, acc):
        qi = pl.program_id(1)
        q = q_ref[...]
        m_s[...] = jnp.full_like(m_s, NEG)
        l_s[...] = jnp.zeros_like(l_s)
        acc[...] = jnp.zeros_like(acc)

        def step(kb, masked):
            rows = pl.ds(pl.multiple_of(kb * t, t), t)
            s = _dot_nt(q, k_ref[rows, :]) * scale
            if masked:
                s = jnp.where(_chunk_mask(0, 0, t, t), s, NEG)
            m_prev = m_s[...]
            m_new = jnp.maximum(m_prev, jnp.max(s, axis=-1, keepdims=True))
            alpha = jnp.exp(m_prev - m_new)
            p = jnp.exp(s - _widen(m_new, t))
            l_s[...] = alpha * l_s[...] + sum(p[:, i * 128:(i + 1) * 128] for i in range(t // 128))
            acc[...] = acc[...] * alpha + _dot(p.astype(BF16), v_ref[rows, :])
            m_s[...] = m_new

        @pl.loop(0, qi)
        def _(kb):
            step(kb, False)

        step(qi, True)
        l = jnp.sum(l_s[...], axis=-1, keepdims=True)
        o_ref[...] = acc[...] / l
        lse_ref[...] = m_s[...] + jnp.log(l)

    qmap = lambda h, i: (i, h)
    kmap = lambda h, i: (0, h)
    return pl.pallas_call(
        body, name=name, grid=(MLA_HEADS, n),
        in_specs=[pl.BlockSpec((t, MLA_HP), qmap), pl.BlockSpec((T, MLA_HP), kmap), pl.BlockSpec((T, MLA_VD), kmap)],
        out_specs=[pl.BlockSpec((t, MLA_VD), qmap), pl.BlockSpec((t, MLA_VD), qmap)],
        out_shape=[jax.ShapeDtypeStruct((T, MLA_HEADS * MLA_VD), F32),
                   jax.ShapeDtypeStruct((T, MLA_HEADS * MLA_VD), F32)],
        scratch_shapes=[pltpu.VMEM((t, MLA_VD), F32), pltpu.VMEM((t, MLA_VD), F32), pltpu.VMEM((t, MLA_VD), F32)],
        compiler_params=_cparams(("parallel", "arbitrary")),
    )(qf, kf, vf)


def _flash_delta(o, do, name):
    def fn(ov, dv):
        parts = []
        for h in range(MLA_HEADS):
            sl = slice(h * MLA_VD, (h + 1) * MLA_VD)
            d = jnp.sum(dv[:, sl] * ov[:, sl], axis=-1, keepdims=True)
            parts.append(jnp.broadcast_to(d, (d.shape[0], MLA_VD)))
        return jnp.concatenate(parts, axis=-1), dv
    w = MLA_HEADS * MLA_VD
    return _rows(fn, [o, do], [], [(w, F32), (w, BF16)], name=name)


def _flash_bwd(qf, kf, vf, do16, lse, delta, name):
    T = qf.shape[0]
    t = _pick(T, FLASH_T)
    n = T // t
    scale = MLA_QKD ** -0.5

    def body(q_ref, k_ref, v_ref, do_ref, lse_ref, dl_ref, dq_ref, dk_ref, dv_ref):
        kb = pl.program_id(1)

        @pl.when(kb == 0)
        def _():
            dq_ref[...] = jnp.zeros_like(dq_ref)

        dk_ref[...] = jnp.zeros_like(dk_ref)
        dv_ref[...] = jnp.zeros_like(dv_ref)
        k, v = k_ref[...], v_ref[...]

        def step(qb, masked):
            rows = pl.ds(pl.multiple_of(qb * t, t), t)
            q, dob = q_ref[rows, :], do_ref[rows, :]
            s = _dot_nt(q, k) * scale
            if masked:
                s = jnp.where(_chunk_mask(0, 0, t, t), s, NEG)
            p = jnp.exp(s - _widen(lse_ref[rows, :], t))
            ds = (p * (_dot_nt(dob, v) - _widen(dl_ref[rows, :], t)) * scale).astype(BF16)
            dv_ref[...] += _dot_tn(p.astype(BF16), dob)
            dk_ref[...] += _dot_tn(ds, q)
            dq_ref[rows, :] += _dot(ds, k)

        step(kb, True)

        @pl.loop(kb + 1, n)
        def _(qb):
            step(qb, False)

    qmap = lambda h, j: (0, h)
    kmap = lambda h, j: (j, h)
    return pl.pallas_call(
        body, name=name, grid=(MLA_HEADS, n),
        in_specs=[pl.BlockSpec((T, MLA_HP), qmap), pl.BlockSpec((t, MLA_HP), kmap), pl.BlockSpec((t, MLA_VD), kmap),
                  pl.BlockSpec((T, MLA_VD), qmap), pl.BlockSpec((T, MLA_VD), qmap), pl.BlockSpec((T, MLA_VD), qmap)],
        out_specs=[pl.BlockSpec((T, MLA_HP), qmap), pl.BlockSpec((t, MLA_HP), kmap), pl.BlockSpec((t, MLA_VD), kmap)],
        out_shape=[jax.ShapeDtypeStruct((T, MLA_HEADS * MLA_HP), F32),
                   jax.ShapeDtypeStruct((T, MLA_HEADS * MLA_HP), F32),
                   jax.ShapeDtypeStruct((T, MLA_HEADS * MLA_VD), F32)],
        compiler_params=_cparams(("arbitrary", "arbitrary")),
    )(qf, kf, vf, do16, lse, delta)


MESH = pl.DeviceIdType.MESH
ANY = pl.BlockSpec(memory_space=pl.ANY)
_CHIP_FLIPS = ((1, 0), (0, 1), (1, 1))


def _place():
    return lax.axis_index("x"), lax.axis_index("y"), lax.axis_index("c")


def _other_chip(x, y, k):
    fx, fy = _CHIP_FLIPS[k]
    return ((1 - x) if fx else x), ((1 - y) if fy else y)


def _remote(src, dst, send_sems, recv_sems, k, to):
    return pltpu.make_async_remote_copy(src_ref=src, dst_ref=dst, send_sem=send_sems.at[k], recv_sem=recv_sems.at[k],
                                        device_id=to, device_id_type=MESH)


def _index(*vals):
    return jnp.stack(vals).astype(jnp.int32)


def _half(c, rows):
    return pl.ds(pl.multiple_of(c * rows, 16), rows)


def _gather_weights(parts):
    n_w = len(parts)

    def body(*refs):
        ins, outs = refs[:n_w], refs[n_w:2 * n_w]
        send_sems, recv_sems, local_sems = refs[2 * n_w:]
        x, y, c = _place()
        j = 2 * x + y
        sibling = (x, y, 1 - c)
        chips = [_other_chip(x, y, k) for k in range(3)]
        pending = []
        for w in range(n_w):
            own = pltpu.make_async_copy(ins[w], outs[w].at[j], local_sems.at[w])
            own.start()
            pending.append(own)
        sent = []
        for w in range(n_w):
            r = _half(c, parts[w].shape[0] // 2)
            for k, (px, py) in enumerate(chips):
                cp = _remote(ins[w].at[r], outs[w].at[j, r], send_sems, recv_sems, 6 * w + k, (px, py, c))
                cp.start()
                sent.append(cp)
        for w in range(n_w):
            r = _half(c, parts[w].shape[0] // 2)
            for k, (px, py) in enumerate(chips):
                blk = outs[w].at[2 * px + py, r]
                _remote(blk, blk, send_sems, recv_sems, 6 * w + k, (px, py, c)).wait_recv()
                cp = _remote(blk, blk, send_sems, recv_sems, 6 * w + 3 + k, sibling)
                cp.start()
                sent.append(cp)
        for w in range(n_w):
            r = _half(1 - c, parts[w].shape[0] // 2)
            for k, (px, py) in enumerate(chips):
                blk = outs[w].at[2 * px + py, r]
                _remote(blk, blk, send_sems, recv_sems, 6 * w + 3 + k, sibling).wait_recv()
        for cp in sent:
            cp.wait_send()
        for cp in pending:
            cp.wait()

    return pl.pallas_call(
        body, name="gather_weights", in_specs=[pl.BlockSpec(memory_space=pltpu.VMEM)] * n_w, out_specs=[ANY] * n_w,
        out_shape=[jax.ShapeDtypeStruct((N_CHIPS, *p.shape), p.dtype) for p in parts],
        scratch_shapes=[pltpu.SemaphoreType.DMA((6 * n_w,)), pltpu.SemaphoreType.DMA((6 * n_w,)),
                        pltpu.SemaphoreType.DMA((n_w,))],
        compiler_params=pltpu.CompilerParams(vmem_limit_bytes=VMEM_LIMIT),
    )(*parts)


def _swap_halves(gs):
    n_w = len(gs)

    def body(*refs):
        g_refs, recv_refs = refs[:n_w], refs[n_w:2 * n_w]
        send_sems, recv_sems = refs[2 * n_w:]
        x, y, c = _place()
        sent = []
        for w in range(n_w):
            for jj in range(N_CHIPS):
                cp = _remote(g_refs[w].at[jj, 1 - c], recv_refs[w].at[jj], send_sems, recv_sems, N_CHIPS * w + jj,
                             (x, y, 1 - c))
                cp.start()
                sent.append(cp)
        for cp in sent:
            cp.wait()

    return pl.pallas_call(
        body, name="grad_swap_halves", in_specs=[ANY] * n_w, out_specs=[ANY] * n_w,
        out_shape=[jax.ShapeDtypeStruct((N_CHIPS, *g.shape[2:]), g.dtype) for g in gs],
        scratch_shapes=[pltpu.SemaphoreType.DMA((N_CHIPS * n_w,)), pltpu.SemaphoreType.DMA((N_CHIPS * n_w,))],
    )(*gs)


def _pair_sum(g, recv, core, name):
    _, H, C = recv.shape
    tile = _pick(H, 256)

    def body(c_ref, own_ref, recv_ref, out_ref):
        out_ref[...] = (own_ref[...].astype(F32) + recv_ref[...].astype(F32)).astype(BF16)

    blk = pl.BlockSpec((None, tile, C), lambda jj, i, c: (jj, i, 0))
    return pl.pallas_call(
        body, name=name,
        grid_spec=pltpu.PrefetchScalarGridSpec(
            num_scalar_prefetch=1, grid=(N_CHIPS, H // tile),
            in_specs=[pl.BlockSpec((None, None, tile, C), lambda jj, i, c: (jj, c[0], i, 0)), blk],
            out_specs=blk),
        out_shape=jax.ShapeDtypeStruct((N_CHIPS, H, C), BF16),
        compiler_params=_cparams(("arbitrary", "arbitrary")),
    )(_index(core), g, recv)


def _chip_sum(g, recv, got, chip, core, name):
    _, H, C = recv.shape
    tile = _pick(H, 256)

    def body(s_ref, own_ref, recv_ref, g0_ref, g1_ref, g2_ref, out_ref):
        pair = own_ref[...].astype(F32) + recv_ref[...].astype(F32)
        out_ref[...] = ((pair + g0_ref[...].astype(F32)) + g1_ref[...].astype(F32)) + g2_ref[...].astype(F32)

    def got_spec(k):
        return pl.BlockSpec((None, tile, C), lambda i, s, k=k: (k, i, 0))

    return pl.pallas_call(
        body, name=name,
        grid_spec=pltpu.PrefetchScalarGridSpec(
            num_scalar_prefetch=1, grid=(H // tile,),
            in_specs=[pl.BlockSpec((None, None, tile, C), lambda i, s: (s[0], s[1], i, 0)),
                      pl.BlockSpec((None, tile, C), lambda i, s: (s[0], i, 0)), got_spec(0), got_spec(1), got_spec(2)],
            out_specs=pl.BlockSpec((None, tile, C), lambda i, s: (s[1], i, 0))),
        out_shape=jax.ShapeDtypeStruct((2, H, C), F32),
        compiler_params=_cparams(("arbitrary",)),
    )(_index(chip, core), g, recv, got, got, got)


def _scatter_chips(sums):
    n_w = len(sums)

    def body(*refs):
        a_refs, got_refs = refs[:n_w], refs[n_w:2 * n_w]
        send_sems, recv_sems = refs[2 * n_w:]
        x, y, c = _place()
        j = 2 * x + y
        sent = []
        for w in range(n_w):
            for k in range(3):
                px, py = _other_chip(x, y, k)
                pj = 2 * px + py
                cp = _remote(a_refs[w].at[pj], got_refs[w].at[(j - pj + 4) % 4 - 1], send_sems, recv_sems, 3 * w + k,
                             (px, py, c))
                cp.start()
                sent.append(cp)
        for w in range(n_w):
            for k in range(3):
                px, py = _other_chip(x, y, k)
                slot = got_refs[w].at[(2 * px + py - j + 4) % 4 - 1]
                _remote(slot, slot, send_sems, recv_sems, 3 * w + k, (px, py, c)).wait_recv()
        for cp in sent:
            cp.wait_send()

    return pl.pallas_call(
        body, name="grad_scatter_chips", in_specs=[ANY] * n_w, out_specs=[ANY] * n_w,
        out_shape=[jax.ShapeDtypeStruct((3, *a.shape[1:]), a.dtype) for a in sums],
        scratch_shapes=[pltpu.SemaphoreType.DMA((3 * n_w,)), pltpu.SemaphoreType.DMA((3 * n_w,))],
    )(*sums)


def _share_halves(reds):
    n_w = len(reds)

    def body(*refs):
        out_refs = refs[n_w:2 * n_w]
        send_sems, recv_sems = refs[2 * n_w:]
        x, y, c = _place()
        sent = []
        for w in range(n_w):
            blk = out_refs[w].at[c]
            cp = _remote(blk, blk, send_sems, recv_sems, w, (x, y, 1 - c))
            cp.start()
            sent.append(cp)
        for cp in sent:
            cp.wait()

    return pl.pallas_call(
        body, name="grad_share_halves", in_specs=[ANY] * n_w, out_specs=[ANY] * n_w,
        out_shape=[jax.ShapeDtypeStruct(r.shape, r.dtype) for r in reds],
        input_output_aliases={w: w for w in range(n_w)},
        scratch_shapes=[pltpu.SemaphoreType.DMA((n_w,)), pltpu.SemaphoreType.DMA((n_w,))],
    )(*reds)


def _allsum_small(v, name):
    R, W = v.shape
    n_dev = 8
    vm = pl.BlockSpec(memory_space=pltpu.VMEM)

    def body(v_ref, out_ref, buf, send_sems, recv_sems):
        x, y, c = _place()
        me = 4 * x + 2 * y + c
        buf[me] = v_ref[...]
        sent = []
        for k in range(1, n_dev):
            peer = ((1 - x) if k & 4 else x, (1 - y) if k & 2 else y, (1 - c) if k & 1 else c)
            cp = _remote(v_ref, buf.at[me], send_sems, recv_sems, k - 1, peer)
            cp.start()
            sent.append(cp)
        for cp in sent:
            cp.wait_recv()
        for cp in sent:
            cp.wait_send()
        acc = buf[0]
        for q in range(1, n_dev):
            acc = acc + buf[q]
        out_ref[...] = acc

    return pl.pallas_call(
        body, name=name, in_specs=[vm], out_specs=vm, out_shape=jax.ShapeDtypeStruct((R, W), v.dtype),
        scratch_shapes=[pltpu.VMEM((n_dev, R, W), v.dtype), pltpu.SemaphoreType.DMA((n_dev - 1,)),
                        pltpu.SemaphoreType.DMA((n_dev - 1,))],
    )(v)


def _reduce_grads(grads, chip, core):
    names = list(grads)
    gs = [grads[k].reshape(N_CHIPS, 2, -1, grads[k].shape[-1]) for k in names]
    recvs = _swap_halves(gs)
    sums = [_pair_sum(g, r, core, f"pair_sum_{k}") for k, g, r in zip(names, gs, recvs)]
    gots = _scatter_chips(sums)
    reds = [_chip_sum(g, r, t, chip, core, f"chip_sum_{k}") for k, g, r, t in zip(names, gs, recvs, gots)]
    return dict(zip(names, _share_halves(reds)))


def _adamw(w, g, m, v, name):
    shape = w.shape
    cols = shape[-1]

    def fn(wv, gv, mv, vv):
        m2 = ADAM_B1 * mv + (1.0 - ADAM_B1) * gv
        v2 = ADAM_B2 * vv + (1.0 - ADAM_B2) * jnp.square(gv)
        m_hat = m2 / (1.0 - ADAM_B1 ** ADAM_STEP)
        v_hat = v2 / (1.0 - ADAM_B2 ** ADAM_STEP)
        return -ADAM_LR * (m_hat / (jnp.sqrt(v_hat) + ADAM_EPS) + ADAM_WD * wv), m2, v2

    w2, m2, v2 = (t.reshape(-1, cols) for t in (w, m, v))
    rows = w2.shape[0]
    tile = 256 if rows % 8 == 0 else rows
    res = _rows(fn, [w2, g.reshape(rows, cols), m2, v2], [], [(cols, F32)] * 3, name=name, tile=tile)
    return tuple(t.reshape(shape) for t in res)


def _add_res(acc, r):
    return (r + acc,)


def _tail_fwd(h1, p16, W, i, tag):
    hn2 = _norm_fwd(h1, W["mlp_norm"][i:i + 1], f"{tag}_mlp_norm")
    a = _mm(hn2, W["mlp_w1"][i], bblk=True, outs=[BF16], name=f"{tag}_mlp_w1",
            epilogue=lambda acc: (jnp.square(jnp.maximum(acc, 0.0)),))
    h2 = _mm(a, W["mlp_w2"][i], extras=[h1], epilogue=_add_res, name=f"{tag}_mlp_w2")
    hn3 = _norm_fwd(h2, W["ple_norm"][i:i + 1], f"{tag}_ple_norm")
    gl = _mm(hn3, W["ple_gate_w"][i], name=f"{tag}_ple_gate")
    h3, pp = _mm(p16[i], W["ple_proj_w"][i], bblk=True, extras=[gl, h2], outs=[F32, F32], name=f"{tag}_ple_proj",
                 epilogue=lambda acc, g, h: (h + _sigmoid(g) * acc, acc))
    return h3, (h1, hn2, a, h2, hn3, gl, pp)


def _tail_bwd(dh3, saved, p16, W, i, tag, into):
    h1, hn2, a, h2, hn3, gl, pp = saved
    into = into or {}

    def gate_bwd(d, g, ppv):
        gate = _sigmoid(g)
        return d * gate, d * ppv * gate * (1.0 - gate)

    def dw(kind, name):
        return (kind, 2, i, into.get(name))

    dpp, dgl = _rows(gate_bwd, [dh3, gl, pp], [], [(D_MODEL, BF16), (D_MODEL, BF16)], name=f"{tag}_ple_gate_bwd")
    d_proj = _mm(p16[i], dpp, ta=True, outs=[BF16], dw=dw("cols", "ple_proj_w"), name=f"{tag}_d_ple_proj")
    d_gate = _mm(hn3, dgl, ta=True, outs=[BF16], dw=dw("rows", "ple_gate_w"), name=f"{tag}_d_ple_gate")
    dhn3 = _mm(dgl, W["ple_gate_w"][i], tb=True, name=f"{tag}_ple_gate_dx")
    dh2, dh2_16, d_ple_norm = _norm_bwd(h2, dhn3, W["ple_norm"][i:i + 1], dh3, f"{tag}_ple_norm_bwd")
    d_w2 = _mm(a, dh2_16, ta=True, outs=[BF16], dw=dw("rows", "mlp_w2"), name=f"{tag}_d_mlp_w2")
    dz = _mm(dh2_16, W["mlp_w2"][i], tb=True, extras=[a], outs=[BF16], name=f"{tag}_mlp_w2_dx",
             epilogue=lambda acc, av: (acc * (2.0 * jnp.sqrt(av.astype(F32))),))
    d_w1 = _mm(hn2, dz, ta=True, outs=[BF16], dw=dw("cols", "mlp_w1"), name=f"{tag}_d_mlp_w1")
    dhn2 = _mm(dz, W["mlp_w1"][i], tb=True, bblk=True, name=f"{tag}_mlp_w1_dx")
    dh1, dh1_16, d_mlp_norm = _norm_bwd(h1, dhn2, W["mlp_norm"][i:i + 1], dh2, f"{tag}_mlp_norm_bwd")
    return (dh1, dh1_16, dict(mlp_w1=d_w1, mlp_w2=d_w2, ple_gate_w=d_gate, ple_proj_w=d_proj),
            dict(mlp_norm=d_mlp_norm, ple_norm=d_ple_norm))


def _ret_layer_fwd(h0, W, tabs):
    hn = _norm_fwd(h0, W["mix_norm"][0:1], "ret_mix_norm")
    proj = _mm(hn, W["ret_w_in"], bblk=True, name="ret_w_in")
    out, states = _ret_fwd(proj, tabs, "ret_scan")
    y = _ret_gate(out, proj, W["ret_gn"], "ret_gate")
    h1 = _mm(y, W["ret_w_out"], extras=[h0], epilogue=_add_res, name="ret_w_out")
    return h1, (h0, hn, proj, out, states, y)


def _ret_layer_bwd(dh1, dh1_16, saved, W, tabs):
    h0, hn, proj, out, states, y = saved
    d_w_out = _mm(y, dh1_16, ta=True, outs=[BF16], dw=("rows", 1, 0, None), name="d_ret_w_out")
    dy = _mm(dh1_16, W["ret_w_out"], tb=True, name="ret_w_out_dx")
    dout, dg, d_gn = _ret_gate_bwd(out, proj, W["ret_gn"], dy, "ret_gate_bwd")
    dq, dk, dv = _ret_bwd(proj, states, dout, tabs, "ret_scan_bwd")
    dproj = jnp.concatenate([dq, dk, dv, dg], axis=1)
    d_w_in = _mm(hn, dproj, ta=True, outs=[BF16], dw=("cols", 1, 0, None), name="d_ret_w_in")
    dhn = _mm(dproj, W["ret_w_in"], tb=True, bblk=True, tn=256, name="ret_w_in_dx")
    dh0, _, d_mix = _norm_bwd(h0, dhn, W["mix_norm"][0:1], dh1, "ret_mix_norm_bwd")
    return dh0, dict(ret_w_in=d_w_in, ret_w_out=d_w_out), dict(mix_norm=d_mix, ret_gn=d_gn)


def _mla_layer_fwd(h0, W, tabs):
    hn = _norm_fwd(h0, W["mix_norm"][1:2], "mla_mix_norm")
    proj = _mm(hn, W["mla_w_in"], name="mla_w_in")

    def low_rank_norm(pv, gq, gkv):
        return _rms(pv[:, :MLA_Q_RANK], gq), _rms(pv[:, MLA_Q_RANK:MLA_Q_RANK + MLA_KV_RANK], gkv)

    cqn, ckvn = _rows(low_rank_norm, [proj], [W["mla_q_a_norm"], W["mla_kv_a_norm"]],
                      [(MLA_Q_RANK, BF16), (MLA_KV_RANK, BF16)], name="mla_low_rank_norm")
    q = _mm(cqn, W["mla_w_uq"], bblk=True, name="mla_w_uq")
    kv = _mm(ckvn, W["mla_w_ukv"], bblk=True, name="mla_w_ukv")
    qf, kf, vf = _mla_prep(q, kv, proj, W["mla_q_norm"], W["mla_k_norm"], tabs, "mla_prep")
    o, lse = _flash_fwd(qf, kf, vf, "mla_flash")
    h1 = _mm(o, W["mla_w_out"], extras=[h0], epilogue=_add_res, name="mla_w_out")
    return h1, (h0, hn, proj, cqn, ckvn, q, kv, qf, kf, vf, o, lse)


def _mla_layer_bwd(dh1, dh1_16, saved, W, tabs):
    h0, hn, proj, cqn, ckvn, q, kv, qf, kf, vf, o, lse = saved
    d_w_out = _mm(o, dh1_16, ta=True, outs=[BF16], dw=("rows", 1, 0, None), name="d_mla_w_out")
    do = _mm(dh1_16, W["mla_w_out"], tb=True, name="mla_w_out_dx")
    delta, do16 = _flash_delta(o, do, "mla_flash_delta")
    dqf, dkf, dvf = _flash_bwd(qf, kf, vf, do16, lse, delta, "mla_flash_bwd")
    dq, dkv, dkr, d_gq, d_gk = _mla_prep_bwd(q, kv, proj, W["mla_q_norm"], W["mla_k_norm"], tabs, dqf, dkf, dvf,
                                             "mla_prep_bwd")
    d_w_uq = _mm(cqn, dq, ta=True, outs=[BF16], dw=("cols", 1, 0, None), name="d_mla_w_uq")
    dcqn = _mm(dq, W["mla_w_uq"], tb=True, bblk=True, name="mla_w_uq_dx")
    d_w_ukv = _mm(ckvn, dkv, ta=True, outs=[BF16], dw=("cols", 1, 0, None), name="d_mla_w_ukv")
    dckvn = _mm(dkv, W["mla_w_ukv"], tb=True, bblk=True, name="mla_w_ukv_dx")

    def low_rank_bwd(pv, dcq, dckv, dkr_v, gq, gkv):
        dxq, dgq = _rms_bwd(pv[:, :MLA_Q_RANK], dcq, gq)
        dxkv, dgkv = _rms_bwd(pv[:, MLA_Q_RANK:MLA_Q_RANK + MLA_KV_RANK], dckv, gkv)
        return jnp.concatenate([dxq, dxkv, dkr_v], axis=-1), _colsum(dgq), _colsum(dgkv)

    dproj, d_gqa, d_gkva = _rows(low_rank_bwd, [proj, dcqn, dckvn, dkr], [W["mla_q_a_norm"], W["mla_kv_a_norm"]],
                                 [(MLA_IN_PAD, BF16)], [((1, MLA_Q_RANK), F32), ((1, MLA_KV_RANK), F32)],
                                 name="mla_low_rank_norm_bwd")
    d_w_in = _mm(hn, dproj, ta=True, outs=[BF16], dw=("rows", 1, 0, None), name="d_mla_w_in")
    dhn = _mm(dproj, W["mla_w_in"], tb=True, name="mla_w_in_dx")
    dh0, dh0_16, d_mix = _norm_bwd(h0, dhn, W["mix_norm"][1:2], dh1, "mla_mix_norm_bwd")
    return (dh0, dh0_16, dict(mla_w_in=d_w_in, mla_w_uq=d_w_uq, mla_w_ukv=d_w_ukv, mla_w_out=d_w_out),
            dict(mix_norm=d_mix, mla_q_a_norm=d_gqa, mla_kv_a_norm=d_gkva, mla_q_norm=d_gq, mla_k_norm=d_gk))


def _local_step(x, p16, target, W):
    T = x.shape[0]
    ret_tabs, mla_tabs = _ret_tables(T), _mla_tables(T)
    h1, s_ret = _ret_layer_fwd(x, W, ret_tabs)
    h3, s_tail0 = _tail_fwd(h1, p16, W, 0, "l0")
    h4, s_mla = _mla_layer_fwd(h3, W, mla_tabs)
    y, s_tail1 = _tail_fwd(h4, p16, W, 1, "l1")

    def loss_head(yv, tv):
        e = yv - tv
        return e * (1.0 / D_MODEL), jnp.full((1, 128), 0.5 / D_MODEL, F32) * jnp.sum(e * e)

    dy, loss = _rows(loss_head, [y, target], [], [(D_MODEL, F32)], [((1, 128), F32)], name="loss_head")
    dh4, dh4_16, g_tail, s_tail1 = _tail_bwd(dy, s_tail1, p16, W, 1, "l1", None)
    dh3, _, g_mla, s_mla = _mla_layer_bwd(dh4, dh4_16, s_mla, W, mla_tabs)
    dh1, dh1_16, g_tail, s_tail0 = _tail_bwd(dh3, s_tail0, p16, W, 0, "l0", g_tail)
    dx, g_ret, s_ret = _ret_layer_bwd(dh1, dh1_16, s_ret, W, ret_tabs)
    small = dict(
        mix_norm=jnp.concatenate([s_ret["mix_norm"], s_mla["mix_norm"]], axis=0),
        mlp_norm=jnp.concatenate([s_tail0["mlp_norm"], s_tail1["mlp_norm"]], axis=0),
        ple_norm=jnp.concatenate([s_tail0["ple_norm"], s_tail1["ple_norm"]], axis=0),
        ret_gn=s_ret["ret_gn"], mla_q_a_norm=s_mla["mla_q_a_norm"], mla_kv_a_norm=s_mla["mla_kv_a_norm"],
        mla_q_norm=s_mla["mla_q_norm"], mla_k_norm=s_mla["mla_k_norm"])
    return loss, dx, {**g_ret, **g_mla, **g_tail}, small


_ORDER = ("mix_norm", "ret_w_in", "ret_gn", "ret_w_out", "mla_w_in", "mla_q_a_norm", "mla_kv_a_norm", "mla_w_uq",
          "mla_w_ukv", "mla_q_norm", "mla_k_norm", "mla_w_out", "mlp_norm", "mlp_w1", "mlp_w2", "ple_norm",
          "ple_gate_w", "ple_proj_w")
_TWO_LAYER = ("mlp_w1", "mlp_w2", "ple_gate_w", "ple_proj_w")
HEADS_PER_CHIP = MLA_HEADS // N_CHIPS


def _travel_parts(w):
    uq = jnp.pad(w["mla_w_uq"][0].reshape(MLA_Q_RANK, HEADS_PER_CHIP, MLA_QKD), ((0, 0), (0, 0), (0, MLA_HP - MLA_QKD)))
    parts = {"ret_w_in": w["ret_w_in"][0], "ret_w_out": w["ret_w_out"][0]}
    for k in _TWO_LAYER:
        parts[k + "_0"] = w[k][0]
    parts["mla_w_in"] = jnp.pad(w["mla_w_in"][0], ((0, 0), (0, MLA_IN_PAD - MLA_IN)))
    parts["mla_w_uq"] = uq.reshape(MLA_Q_RANK, HEADS_PER_CHIP * MLA_HP)
    parts["mla_w_ukv"] = w["mla_w_ukv"][0]
    parts["mla_w_out"] = w["mla_w_out"][0]
    for k in _TWO_LAYER:
        parts[k + "_1"] = w[k][1]
    return {k: v.astype(BF16) for k, v in parts.items()}


def _full_weights(full, small):
    rows = lambda a: a.reshape(-1, a.shape[-1])
    W = {k: full[k] for k in ("ret_w_in", "mla_w_uq", "mla_w_ukv")}
    for k in ("ret_w_out", "mla_w_in", "mla_w_out"):
        W[k] = rows(full[k])
    W["mlp_w1"] = [full["mlp_w1_0"], full["mlp_w1_1"]]
    W["ple_proj_w"] = [full["ple_proj_w_0"], full["ple_proj_w_1"]]
    W["mlp_w2"] = [rows(full["mlp_w2_0"]), rows(full["mlp_w2_1"])]
    W["ple_gate_w"] = [rows(full["ple_gate_w_0"]), rows(full["ple_gate_w_1"])]
    W["ret_gn"] = small[0:2].reshape(RET_HEADS, RET_DV)
    W["mla_q_a_norm"] = small[2:3, :MLA_Q_RANK]
    W["mla_kv_a_norm"] = small[3:4, :MLA_KV_RANK]
    return W


def _shard_grad(name, red, shape):
    if name == "mla_w_in":
        red = red.reshape(-1, MLA_IN_PAD)[:, :MLA_IN]
    elif name == "mla_w_uq":
        red = red.reshape(MLA_Q_RANK, HEADS_PER_CHIP, MLA_HP)[:, :, :MLA_QKD]
    return red.reshape(shape)


def _pad_row(v):
    v = v.reshape(1, -1)
    return jnp.pad(v, ((0, 0), (0, PACK_W - v.shape[1])))


def kernel(x, p, mix_norm, ret_w_in, ret_gn, ret_w_out, mla_w_in, mla_q_a_norm, mla_kv_a_norm, mla_w_uq, mla_w_ukv, mla_q_norm, mla_k_norm, mla_w_out, mlp_norm, mlp_w1, mlp_w2, ple_norm, ple_gate_w, ple_proj_w, loss_target, m_mix_norm, m_ret_w_in, m_ret_gn, m_ret_w_out, m_mla_w_in, m_mla_q_a_norm, m_mla_kv_a_norm, m_mla_w_uq, m_mla_w_ukv, m_mla_q_norm, m_mla_k_norm, m_mla_w_out, m_mlp_norm, m_mlp_w1, m_mlp_w2, m_ple_norm, m_ple_gate_w, m_ple_proj_w, v_mix_norm, v_ret_w_in, v_ret_gn, v_ret_w_out, v_mla_w_in, v_mla_q_a_norm, v_mla_kv_a_norm, v_mla_w_uq, v_mla_w_ukv, v_mla_q_norm, v_mla_k_norm, v_mla_w_out, v_mlp_norm, v_mlp_w1, v_mlp_w2, v_ple_norm, v_ple_gate_w, v_ple_proj_w):
    w = dict(mix_norm=mix_norm, ret_w_in=ret_w_in, ret_gn=ret_gn, ret_w_out=ret_w_out, mla_w_in=mla_w_in,
             mla_q_a_norm=mla_q_a_norm, mla_kv_a_norm=mla_kv_a_norm, mla_w_uq=mla_w_uq, mla_w_ukv=mla_w_ukv,
             mla_q_norm=mla_q_norm, mla_k_norm=mla_k_norm, mla_w_out=mla_w_out, mlp_norm=mlp_norm, mlp_w1=mlp_w1,
             mlp_w2=mlp_w2, ple_norm=ple_norm, ple_gate_w=ple_gate_w, ple_proj_w=ple_proj_w)
    m = dict(mix_norm=m_mix_norm, ret_w_in=m_ret_w_in, ret_gn=m_ret_gn, ret_w_out=m_ret_w_out, mla_w_in=m_mla_w_in,
             mla_q_a_norm=m_mla_q_a_norm, mla_kv_a_norm=m_mla_kv_a_norm, mla_w_uq=m_mla_w_uq, mla_w_ukv=m_mla_w_ukv,
             mla_q_norm=m_mla_q_norm, mla_k_norm=m_mla_k_norm, mla_w_out=m_mla_w_out, mlp_norm=m_mlp_norm,
             mlp_w1=m_mlp_w1, mlp_w2=m_mlp_w2, ple_norm=m_ple_norm, ple_gate_w=m_ple_gate_w, ple_proj_w=m_ple_proj_w)
    v = dict(mix_norm=v_mix_norm, ret_w_in=v_ret_w_in, ret_gn=v_ret_gn, ret_w_out=v_ret_w_out, mla_w_in=v_mla_w_in,
             mla_q_a_norm=v_mla_q_a_norm, mla_kv_a_norm=v_mla_kv_a_norm, mla_w_uq=v_mla_w_uq, mla_w_ukv=v_mla_w_ukv,
             mla_q_norm=v_mla_q_norm, mla_k_norm=v_mla_k_norm, mla_w_out=v_mla_w_out, mlp_norm=v_mlp_norm,
             mlp_w1=v_mlp_w1, mlp_w2=v_mlp_w2, ple_norm=v_ple_norm, ple_gate_w=v_ple_gate_w, ple_proj_w=v_ple_proj_w)
    xi, yi, ci = _place()
    chip = 2 * xi + yi
    n = N_CHIPS

    parts = _travel_parts(w)
    full = dict(zip(parts, _gather_weights(list(parts.values()))))
    on = (jnp.arange(n) == chip) & (ci == 0)
    gn_rows = jnp.where(on[None, :, None], ret_gn[0][:, None, :], 0.0).reshape(2, PACK_W)
    qa_row = _pad_row(jnp.where(on[:, None], mla_q_a_norm, 0.0))
    kva_row = _pad_row(jnp.where(on[:, None], mla_kv_a_norm, 0.0))
    small_in = jnp.concatenate([gn_rows, qa_row, kva_row, jnp.zeros((4, PACK_W), F32)], axis=0)
    small = _allsum_small(small_in, "gather_gains")
    W = _full_weights(full, small)
    W["mix_norm"], W["mlp_norm"], W["ple_norm"] = mix_norm, mlp_norm, ple_norm
    W["mla_q_norm"] = jnp.pad(mla_q_norm, ((0, 0), (0, MLA_HP - MLA_QKD)))
    W["mla_k_norm"] = jnp.pad(mla_k_norm, ((0, 0), (0, MLA_HP - MLA_QKD)))

    loss, dx, g_big, gs = _local_step(x[0], p[:, 0].astype(BF16), loss_target[0], W)

    red = _reduce_grads(g_big, chip, ci)
    g_big = {k: _shard_grad(k, red[k], w[k].shape) for k in red}
    small_g = jnp.concatenate([
        gs["mix_norm"], gs["mlp_norm"], gs["ple_norm"], gs["ret_gn"].reshape(2, PACK_W), _pad_row(gs["mla_q_a_norm"]),
        _pad_row(gs["mla_kv_a_norm"]), _pad_row(gs["mla_q_norm"][:, :MLA_QKD]), _pad_row(gs["mla_k_norm"][:, :MLA_QKD]),
        _pad_row(loss[:, :1]), jnp.zeros((3, PACK_W), F32)], axis=0)
    tot = _allsum_small(small_g, "sum_small_grads")
    gn_all = tot[6:8].reshape(RET_HEADS, n, -1)
    g_small = dict(
        mix_norm=tot[0:2], mlp_norm=tot[2:4], ple_norm=tot[4:6],
        ret_gn=lax.dynamic_index_in_dim(gn_all, chip, axis=1, keepdims=False),
        mla_q_a_norm=lax.dynamic_index_in_dim(tot[8, :MLA_Q_RANK].reshape(n, -1), chip, axis=0, keepdims=True),
        mla_kv_a_norm=lax.dynamic_index_in_dim(tot[9, :MLA_KV_RANK].reshape(n, -1), chip, axis=0, keepdims=True),
        mla_q_norm=tot[10:11, :MLA_QKD], mla_k_norm=tot[11:12, :MLA_QKD])
    loss_out = tot[12, 0]

    grads, deltas, new_m, new_v = [], [], [], []
    for k in _ORDER:
        g = g_big[k] if k in g_big else g_small[k]
        g = g.reshape(w[k].shape)
        d, m2, v2 = _adamw(w[k], g, m[k], v[k], f"adamw_{k}")
        grads.append(g)
        deltas.append(d)
        new_m.append(m2)
        new_v.append(v2)
    return (loss_out, dx[None], *grads, *deltas, *new_m, *new_v)
```

```python
import functools

import jax
import jax.numpy as jnp
from jax import lax
from jax.experimental import pallas as pl
from jax.experimental.pallas import tpu as pltpu

F32 = jnp.float32
BF16 = jnp.bfloat16

EPS = 1e-6
D_MODEL = 1024
CHUNK = 64
ROPE_THETA = 10000.0
RET_HEADS = 4
RET_DK = 256
RET_DV = 512
RET_GROUP = 1
MLA_HEADS = 8
MLA_NOPE = 128
MLA_ROPE = 64
MLA_QKD = 192
MLA_VD = 128
MLA_HP = 256
MLA_Q_RANK = 384
MLA_KV_RANK = 256
MLA_IN = 704
MLA_IN_PAD = 768
D_FF = 4096
PLE_DIM = 256
N_CHIPS = 4

ADAM_LR = 0.001
ADAM_B1 = 0.9
ADAM_B2 = 0.999
ADAM_EPS = 1e-08
ADAM_WD = 0.01
ADAM_STEP = 10

VMEM_LIMIT = 56 * 1024 * 1024
PACK_W = 1024
NEG = -1e30
FLASH_T = 512
FLASH_HEADS = 2


def _cparams(sem=None):
    return pltpu.CompilerParams(dimension_semantics=sem, vmem_limit_bytes=VMEM_LIMIT)


def _pick(dim, pref):
    if dim <= pref:
        return dim
    t = pref
    while dim % t:
        t //= 2
    return t


def _mm(a, b, *, name, ta=False, tb=False, bblk=False, outs=None, extras=(), epilogue=None, dw=None,
        tm=1024, tn=512, after=()):
    if ta:
        K, M = a.shape
    else:
        M, K = a.shape
    if bblk and tb:
        nb, N, Kq = b.shape
        assert nb * Kq == K
    elif bblk:
        nb, Kb, Nq = b.shape
        N = nb * Nq
        assert Kb == K
    else:
        N = b.shape[0] if tb else b.shape[1]
    tn = _pick(Nq if (bblk and not tb) else N, tn)
    if dw is not None and dw[0] == "cols":
        tn = _pick(N // N_CHIPS, tn)
    tm = _pick(M // N_CHIPS if (dw is not None and dw[0] == "rows") else M, tm)
    grid = (M // tm, N // tn)

    a_spec = pl.BlockSpec((K, tm), lambda i, j: (0, i)) if ta else pl.BlockSpec((tm, K), lambda i, j: (i, 0))
    if bblk and tb:
        b_spec = pl.BlockSpec((nb, tn, Kq), lambda i, j: (0, j, 0))
    elif bblk:
        npb = Nq // tn
        b_spec = pl.BlockSpec((None, K, tn), lambda i, j: (j // npb, 0, j % npb))
    elif tb:
        b_spec = pl.BlockSpec((tn, K), lambda i, j: (j, 0))
    else:
        b_spec = pl.BlockSpec((K, tn), lambda i, j: (0, j))
    in_specs = [a_spec, b_spec] + [pl.BlockSpec((tm, tn), lambda i, j: (i, j)) for _ in extras]
    args = [a, b, *extras]
    aliases = {}
    if outs is None:
        outs = [F32]
    if dw is None:
        o_specs = [pl.BlockSpec((tm, tn), lambda i, j: (i, j)) for _ in outs]
        o_shapes = [jax.ShapeDtypeStruct((M, N), dt) for dt in outs]
    else:
        kind, layers, layer, into = dw
        if kind == "cols":
            per = (N // N_CHIPS) // tn
            o_specs = [pl.BlockSpec((None, None, tm, tn), lambda i, j: (j // per, layer, i, j % per))]
            o_shapes = [jax.ShapeDtypeStruct((N_CHIPS, layers, M, N // N_CHIPS), outs[0])]
        else:
            per = (M // N_CHIPS) // tm
            o_specs = [pl.BlockSpec((None, None, tm, tn), lambda i, j: (i // per, layer, i % per, j))]
            o_shapes = [jax.ShapeDtypeStruct((N_CHIPS, layers, M // N_CHIPS, N), outs[0])]
        if into is not None:
            aliases = {len(args): 0}
            in_specs.append(pl.BlockSpec(memory_space=pl.ANY))
            args.append(into)
    for t in after:
        in_specs.append(pl.BlockSpec(memory_space=pl.ANY))
        args.append(t)
    n_e, n_o = len(extras), len(outs)

    def body(a_ref, b_ref, *rest):
        e_refs, o_refs = rest[:n_e], rest[len(rest) - n_o:]
        av = a_ref[...].astype(BF16)
        if bblk and tb:
            acc = _dot_nt(av[:, :Kq], b_ref[0].astype(BF16))
            for s in range(1, nb):
                acc = acc + _dot_nt(av[:, s * Kq:(s + 1) * Kq], b_ref[s].astype(BF16))
        elif ta:
            acc = _dot_tn(av, b_ref[...].astype(BF16))
        elif tb:
            acc = _dot_nt(av, b_ref[...].astype(BF16))
        else:
            acc = _dot(av, b_ref[...].astype(BF16))
        vals = (acc,) if epilogue is None else epilogue(acc, *[e[...] for e in e_refs])
        for o, v in zip(o_refs, vals):
            o[...] = v.astype(o.dtype)

    res = pl.pallas_call(
        body, name=name, grid=grid, in_specs=in_specs, out_specs=o_specs, out_shape=o_shapes,
        input_output_aliases=aliases, compiler_params=_cparams(("parallel", "arbitrary")),
    )(*args)
    return res[0] if n_o == 1 else res


def _rows(fn, rows, fulls, outs, accs=(), *, name, tile=256, after=()):
    first = rows[0][0] if isinstance(rows[0], tuple) else rows[0]
    T = first.shape[0]
    tile = _pick(T, tile)
    in_specs, args = [], []
    for r in rows:
        if isinstance(r, tuple):
            arr, w, cb = r
            in_specs.append(pl.BlockSpec((tile, w), lambda i, cb=cb: (i, cb)))
        else:
            arr = r
            in_specs.append(pl.BlockSpec((tile, arr.shape[1]), lambda i: (i, 0)))
        args.append(arr)
    for f in fulls:
        in_specs.append(pl.BlockSpec(f.shape, lambda i, nd=f.ndim: (0,) * nd))
        args.append(f)
    out_specs = [pl.BlockSpec((tile, w), lambda i: (i, 0)) for w, _ in outs]
    out_specs += [pl.BlockSpec(s, lambda i: (0, 0)) for s, _ in accs]
    out_shape = [jax.ShapeDtypeStruct((T, w), dt) for w, dt in outs]
    out_shape += [jax.ShapeDtypeStruct(s, dt) for s, dt in accs]
    n_in, n_out = len(args), len(outs)
    for t in after:
        in_specs.append(pl.BlockSpec(memory_space=pl.ANY))
        args.append(t)

    def body(*refs):
        vals = fn(*[r[...] for r in refs[:n_in]])
        o_refs = refs[len(args):]
        for o, v in zip(o_refs[:n_out], vals[:n_out]):
            o[...] = v.astype(o.dtype)
        first_step = pl.program_id(0) == 0
        for o, v in zip(o_refs[n_out:], vals[n_out:]):
            @pl.when(first_step)
            def _(o=o, v=v):
                o[...] = v.astype(o.dtype)

            @pl.when(jnp.logical_not(first_step))
            def _(o=o, v=v):
                o[...] += v.astype(o.dtype)

    res = pl.pallas_call(
        body, name=name, grid=(T // tile,), in_specs=in_specs, out_specs=out_specs, out_shape=out_shape,
        compiler_params=_cparams(("arbitrary",)),
    )(*args)
    return res


def _rms(x, g):
    r = lax.rsqrt(jnp.mean(x * x, axis=-1, keepdims=True) + EPS)
    return (x * r) * g


def _rms_bwd(x, dy, g, n=None):
    n = x.shape[-1] if n is None else n
    r = lax.rsqrt(jnp.sum(x * x, axis=-1, keepdims=True) / n + EPS)
    xh = x * r
    dxh = dy * g
    dx = r * (dxh - xh * (jnp.sum(dxh * xh, axis=-1, keepdims=True) / n))
    return dx, dy * xh


def _colsum(v):
    return jnp.sum(v, axis=0, keepdims=True)


def _sigmoid(x):
    return 1.0 / (1.0 + jnp.exp(-x))


def _widen(v, width):
    reps = width // v.shape[1]
    return v if reps == 1 else jnp.concatenate([v] * reps, axis=-1)


def _norm_fwd(h, gain, name):
    return _rows(lambda x, g: (_rms(x, g),), [h], [gain], [(h.shape[1], BF16)], name=name)[0]


def _norm_bwd(h, dhn, gain, dres, name):
    def fn(x, dy, dr, g):
        dx, dg = _rms_bwd(x, dy, g)
        return dr + dx, dr + dx, _colsum(dg)
    d = h.shape[1]
    return _rows(fn, [h, dhn, dres], [gain], [(d, F32), (d, BF16)], [((1, d), F32)], name=name)


def _ret_tables(T):
    inv = 1.0 / (ROPE_THETA ** (jnp.arange(0, RET_DK, 2, dtype=F32) / RET_DK))
    ang = jnp.arange(T, dtype=F32)[:, None] * inv[None, :]
    log_gamma = jnp.log(1.0 - 2.0 ** (-5.0 - jnp.arange(RET_HEADS, dtype=F32)))
    idx = jnp.arange(CHUNK, dtype=F32)
    intra = jnp.exp(log_gamma[:, None, None] * jnp.abs(idx[:, None] - idx[None, :]))
    qd = jnp.exp(log_gamma[:, None] * (idx + 1.0))[:, :, None]
    kd = jnp.exp(log_gamma[:, None] * (CHUNK - 1.0 - idx))[:, :, None]
    cd = jnp.exp(log_gamma * CHUNK)[:, None, None]
    return jnp.cos(ang), jnp.sin(ang), intra, qd, kd, cd


def _rope_half(x, c, s):
    x1, x2 = x[:, :RET_DK // 2], x[:, RET_DK // 2:]
    return jnp.concatenate([x1 * c - x2 * s, x2 * c + x1 * s], axis=-1)


def _rope_half_bwd(d, c, s):
    d1, d2 = d[:, :RET_DK // 2], d[:, RET_DK // 2:]
    return jnp.concatenate([d1 * c + d2 * s, d2 * c - d1 * s], axis=-1)


def _dot(a, b):
    return lax.dot_general(a, b, (((1,), (0,)), ((), ())), preferred_element_type=F32)


def _dot_nt(a, b):
    return lax.dot_general(a, b, (((1,), (1,)), ((), ())), preferred_element_type=F32)


def _dot_tn(a, b):
    return lax.dot_general(a, b, (((0,), (0,)), ((), ())), preferred_element_type=F32)


def _ret_specs(T, tb, rev):
    nj = T // tb
    jj = (lambda j: nj - 1 - j) if rev else (lambda j: j)
    g = RET_GROUP
    kq = RET_HEADS // g
    vq = 2 * RET_HEADS * RET_DK // (g * RET_DV)
    return dict(
        q=pl.BlockSpec((tb, g * RET_DK), lambda h, j: (jj(j), h)),
        k=pl.BlockSpec((tb, g * RET_DK), lambda h, j: (jj(j), kq + h)),
        v=pl.BlockSpec((tb, g * RET_DV), lambda h, j: (jj(j), vq + h)),
        tab=pl.BlockSpec((tb, RET_DK // 2), lambda h, j: (jj(j), 0)),
        intra=pl.BlockSpec((g, CHUNK, CHUNK), lambda h, j: (h, 0, 0)),
        dec=pl.BlockSpec((g, CHUNK, 1), lambda h, j: (h, 0, 0)),
        cd=pl.BlockSpec((g, 1, 1), lambda h, j: (h, 0, 0)),
        o=pl.BlockSpec((tb, g * RET_DV), lambda h, j: (jj(j), h)),
        s=pl.BlockSpec((g, tb // CHUNK, RET_DK, RET_DV), lambda h, j: (h, jj(j), 0, 0)),
    )


def _ret_fwd(proj, tabs, name):
    T = proj.shape[0]
    cos, sin, intra, qd, kd, cd = tabs
    tb = _pick(T, 512)
    cps = tb // CHUNK
    sp = _ret_specs(T, tb, False)
    scale = RET_DK ** -0.5

    def body(q_ref, k_ref, v_ref, cos_ref, sin_ref, intra_ref, qd_ref, kd_ref, cd_ref, o_ref, s_ref, state):
        @pl.when(pl.program_id(1) == 0)
        def _():
            state[...] = jnp.zeros_like(state)

        for c in range(cps):
            rows = pl.ds(c * CHUNK, CHUNK)
            co, si = cos_ref[rows, :], sin_ref[rows, :]
            for h in range(RET_GROUP):
                hk, hv = slice(h * RET_DK, (h + 1) * RET_DK), slice(h * RET_DV, (h + 1) * RET_DV)
                q = _rope_half(q_ref[rows, hk], co, si)
                k = _rope_half(k_ref[rows, hk], co, si) * scale
                vb = v_ref[rows, hv].astype(BF16)
                st = state[h]
                sb = st.astype(BF16)
                s_ref[h, c] = sb
                sc = _dot_nt(q.astype(BF16), k.astype(BF16)) * intra_ref[h]
                inner = _dot(sc.astype(BF16), vb)
                cross = _dot((q * qd_ref[h]).astype(BF16), sb)
                o_ref[rows, hv] = inner + cross
                state[h] = st * cd_ref[h] + _dot_tn((k * kd_ref[h]).astype(BF16), vb)

    return pl.pallas_call(
        body, name=name, grid=(RET_HEADS // RET_GROUP, T // tb),
        in_specs=[sp["q"], sp["k"], sp["v"], sp["tab"], sp["tab"], sp["intra"], sp["dec"], sp["dec"], sp["cd"]],
        out_specs=[sp["o"], sp["s"]],
        out_shape=[jax.ShapeDtypeStruct((T, RET_HEADS * RET_DV), F32),
                   jax.ShapeDtypeStruct((RET_HEADS, T // CHUNK, RET_DK, RET_DV), BF16)],
        scratch_shapes=[pltpu.VMEM((RET_GROUP, RET_DK, RET_DV), F32)],
        compiler_params=_cparams(("arbitrary", "arbitrary")),
    )(proj, proj, proj, cos, sin, intra, qd, kd, cd)


def _ret_bwd(proj, states, dout, tabs, name):
    T = proj.shape[0]
    cos, sin, intra, qd, kd, cd = tabs
    tb = _pick(T, 512)
    cps = tb // CHUNK
    sp = _ret_specs(T, tb, True)
    scale = RET_DK ** -0.5

    def body(q_ref, k_ref, v_ref, cos_ref, sin_ref, intra_ref, qd_ref, kd_ref, cd_ref, s_ref, do_ref,
             dq_ref, dk_ref, dv_ref, dstate):
        @pl.when(pl.program_id(1) == 0)
        def _():
            dstate[...] = jnp.zeros_like(dstate)

        for c in reversed(range(cps)):
            rows = pl.ds(c * CHUNK, CHUNK)
            co, si = cos_ref[rows, :], sin_ref[rows, :]
            for h in range(RET_GROUP):
                hk, hv = slice(h * RET_DK, (h + 1) * RET_DK), slice(h * RET_DV, (h + 1) * RET_DV)
                q = _rope_half(q_ref[rows, hk], co, si)
                k = _rope_half(k_ref[rows, hk], co, si) * scale
                qb, kb = q.astype(BF16), k.astype(BF16)
                vb = v_ref[rows, hv].astype(BF16)
                dob = do_ref[rows, hv].astype(BF16)
                sb = s_ref[h, c]
                ia = intra_ref[h]
                pb = (_dot_nt(qb, kb) * ia).astype(BF16)
                dsn = dstate[h]
                dsb = dsn.astype(BF16)
                kdk = (k * kd_ref[h]).astype(BF16)
                qdq = (q * qd_ref[h]).astype(BF16)
                dv = _dot_tn(pb, dob) + _dot(kdk, dsb)
                dpb = (_dot_nt(dob, vb) * ia).astype(BF16)
                dq = _dot(dpb, kb) + _dot_nt(dob, sb) * qd_ref[h]
                dk = _dot_tn(dpb, qb) + _dot_nt(vb, dsb) * kd_ref[h]
                dstate[h] = dsn * cd_ref[h] + _dot_tn(qdq, dob)
                dq_ref[rows, hk] = _rope_half_bwd(dq, co, si).astype(BF16)
                dk_ref[rows, hk] = _rope_half_bwd(dk * scale, co, si).astype(BF16)
                dv_ref[rows, hv] = dv.astype(BF16)

    return pl.pallas_call(
        body, name=name, grid=(RET_HEADS // RET_GROUP, T // tb),
        in_specs=[sp["q"], sp["k"], sp["v"], sp["tab"], sp["tab"], sp["intra"], sp["dec"], sp["dec"], sp["cd"],
                  sp["s"], sp["o"]],
        out_specs=[sp["q"], sp["q"], sp["o"]],
        out_shape=[jax.ShapeDtypeStruct((T, RET_HEADS * RET_DK), BF16),
                   jax.ShapeDtypeStruct((T, RET_HEADS * RET_DK), BF16),
                   jax.ShapeDtypeStruct((T, RET_HEADS * RET_DV), BF16)],
        scratch_shapes=[pltpu.VMEM((RET_GROUP, RET_DK, RET_DV), F32)],
        compiler_params=_cparams(("arbitrary", "arbitrary")),
    )(proj, proj, proj, cos, sin, intra, qd, kd, cd, states, dout)


def _ret_gate(out, proj, gn, name):
    def fn(o, g, *gains):
        parts = [_rms(o[:, h * RET_DV:(h + 1) * RET_DV], gains[h]) for h in range(RET_HEADS)]
        return (g * _sigmoid(g) * jnp.concatenate(parts, axis=-1),)
    w = RET_HEADS * RET_DV
    return _rows(fn, [out, (proj, w, 2)], [gn[h:h + 1] for h in range(RET_HEADS)], [(w, BF16)], name=name)[0]


def _ret_gate_bwd(out, proj, gn, dy, name):
    def fn(o, g, d, *gains):
        sg = _sigmoid(g)
        silu = g * sg
        dsilu = sg * (1.0 + g * (1.0 - sg))
        dos, dgs = [], []
        row = lax.broadcasted_iota(jnp.int32, (RET_HEADS, RET_DV), 0)
        dgn = jnp.zeros((RET_HEADS, RET_DV), F32)
        for h in range(RET_HEADS):
            sl = slice(h * RET_DV, (h + 1) * RET_DV)
            oh = o[:, sl]
            dgs.append(d[:, sl] * _rms(oh, gains[h]) * dsilu[:, sl])
            dx, dg = _rms_bwd(oh, d[:, sl] * silu[:, sl], gains[h])
            dos.append(dx)
            dgn = dgn + jnp.where(row == h, _colsum(dg), 0.0)
        return jnp.concatenate(dos, axis=-1), jnp.concatenate(dgs, axis=-1), dgn
    w = RET_HEADS * RET_DV
    return _rows(fn, [out, (proj, w, 2), dy], [gn[h:h + 1] for h in range(RET_HEADS)], [(w, BF16), (w, BF16)],
                 [((RET_HEADS, RET_DV), F32)], name=name, tile=128)


def _mla_tables(T):
    inv = 1.0 / (ROPE_THETA ** (jnp.arange(0, MLA_ROPE, 2, dtype=F32) / MLA_ROPE))
    ang = jnp.arange(T, dtype=F32)[:, None] * inv[None, :]
    c, s = jnp.cos(ang), jnp.sin(ang)
    z32, z64 = jnp.zeros((T, 32), F32), jnp.zeros((T, 64), F32)
    cos_t = jnp.concatenate([c, c, z64], axis=1)
    sin_a = jnp.concatenate([-s, z32, z64], axis=1)
    sin_b = jnp.concatenate([z32, s, z64], axis=1)
    return cos_t, sin_a, sin_b


def _rope_blk(x, ct, sa, sb):
    return x * ct + pltpu.roll(x, 96, 1) * sa + pltpu.roll(x, 32, 1) * sb


def _rope_blk_bwd(d, ct, sa, sb):
    return d * ct + pltpu.roll(d * sa, 32, 1) + pltpu.roll(d * sb, 96, 1)


def _head_norm(x, gain):
    r = lax.rsqrt(jnp.sum(x * x, axis=-1, keepdims=True) / MLA_QKD + EPS)
    return (x * r) * gain


def _mla_prep(q, kv, proj, gq, gk, tabs, name):
    def fn(qv, kvv, kr, ct, sa, sb, gqv, gkv):
        qs, ks, vs = [], [], []
        for h in range(MLA_HEADS):
            b = h * MLA_HP
            y = _head_norm(qv[:, b:b + MLA_HP], gqv)
            qs += [y[:, :128], _rope_blk(y[:, 128:], ct, sa, sb)]
            y = _head_norm(jnp.concatenate([kvv[:, b:b + 128], kr], axis=-1), gkv)
            ks += [y[:, :128], _rope_blk(y[:, 128:], ct, sa, sb)]
            vs.append(kvv[:, b + 128:b + 256])
        return jnp.concatenate(qs, axis=-1), jnp.concatenate(ks, axis=-1), jnp.concatenate(vs, axis=-1)
    w = MLA_HEADS * MLA_HP
    return _rows(fn, [q, kv, (proj, 128, 5), *tabs], [gq, gk],
                 [(w, BF16), (w, BF16), (MLA_HEADS * MLA_VD, BF16)], name=name, tile=128)


def _mla_prep_bwd(q, kv, proj, gq, gk, tabs, dqf, dkf, dvf, name):
    def fn(qv, kvv, kr, ct, sa, sb, dqv, dkv, dvv, gqv, gkv):
        dqs, dkvs = [], []
        dkr = jnp.zeros_like(kr)
        dgq = jnp.zeros((1, MLA_HP), F32)
        dgk = jnp.zeros((1, MLA_HP), F32)
        for h in range(MLA_HEADS):
            b = h * MLA_HP
            dy = jnp.concatenate([dqv[:, b:b + 128], _rope_blk_bwd(dqv[:, b + 128:b + 256], ct, sa, sb)], axis=-1)
            dx, dg = _rms_bwd(qv[:, b:b + MLA_HP], dy, gqv, MLA_QKD)
            dqs.append(dx)
            dgq = dgq + _colsum(dg)
            dy = jnp.concatenate([dkv[:, b:b + 128], _rope_blk_bwd(dkv[:, b + 128:b + 256], ct, sa, sb)], axis=-1)
            dx, dg = _rms_bwd(jnp.concatenate([kvv[:, b:b + 128], kr], axis=-1), dy, gkv, MLA_QKD)
            dkvs += [dx[:, :128], dvv[:, h * MLA_VD:(h + 1) * MLA_VD]]
            dkr = dkr + dx[:, 128:]
            dgk = dgk + _colsum(dg)
        return jnp.concatenate(dqs, axis=-1), jnp.concatenate(dkvs, axis=-1), dkr, dgq, dgk
    w = MLA_HEADS * MLA_HP
    return _rows(fn, [q, kv, (proj, 128, 5), *tabs, dqf, dkf, dvf], [gq, gk],
                 [(w, BF16), (w, BF16), (128, F32)], [((1, MLA_HP), F32), ((1, MLA_HP), F32)], name=name, tile=128)


def _chunk_mask(qi, ki, tq, tk):
    shift = CHUNK.bit_length() - 1
    rq = lax.shift_right_arithmetic(qi * tq + lax.broadcasted_iota(jnp.int32, (tq, tk), 0), shift)
    ck = lax.shift_right_arithmetic(ki * tk + lax.broadcasted_iota(jnp.int32, (tq, tk), 1), shift)
    return ck <= rq


def _flash_fwd(qf, kf, vf, name):
    T = qf.shape[0]
    t = _pick(T, FLASH_T)
    n = T // t
    scale = MLA_QKD ** -0.5

    g = FLASH_HEADS

    def body(q_ref, k_ref, v_ref, o_ref, lse_ref, m_s, l_s, acc):
        qi = pl.program_id(1)
        m_s[...] = jnp.full_like(m_s, NEG)
        l_s[...] = jnp.zeros_like(l_s)
        acc[...] = jnp.zeros_like(acc)

        def step(kb, masked):
            rows = pl.ds(pl.multiple_of(kb * t, t), t)
            for h in range(g):
                hq, hv = slice(h * MLA_HP, (h + 1) * MLA_HP), slice(h * MLA_VD, (h + 1) * MLA_VD)
                s = _dot_nt(q_ref[:, hq], k_ref[rows, hq]) * scale
                if masked:
                    s = jnp.where(_chunk_mask(0, 0, t, t), s, NEG)
                m_prev = m_s[:, hv]
                m_new = jnp.maximum(m_prev, jnp.max(s, axis=-1, keepdims=True))
                alpha = jnp.exp(m_prev - m_new)
                p = jnp.exp(s - _widen(m_new, t))
                l_s[:, hv] = alpha * l_s[:, hv] + sum(p[:, i * 128:(i + 1) * 128] for i in range(t // 128))
                acc[:, hv] = acc[:, hv] * alpha + _dot(p.astype(BF16), v_ref[rows, hv])
                m_s[:, hv] = m_new

        @pl.loop(0, qi)
        def _(kb):
            step(kb, False)

        step(qi, True)
        for h in range(g):
            hv = slice(h * MLA_VD, (h + 1) * MLA_VD)
            l = jnp.sum(l_s[:, hv], axis=-1, keepdims=True)
            o_ref[:, hv] = acc[:, hv] / l
            lse_ref[:, hv] = m_s[:, hv] + jnp.log(l)

    qmap = lambda h, i: (i, h)
    kmap = lambda h, i: (0, h)
    vec = pltpu.VMEM((t, g * MLA_VD), F32)
    return pl.pallas_call(
        body, name=name, grid=(MLA_HEADS // g, n),
        in_specs=[pl.BlockSpec((t, g * MLA_HP), qmap), pl.BlockSpec((T, g * MLA_HP), kmap),
                  pl.BlockSpec((T, g * MLA_VD), kmap)],
        out_specs=[pl.BlockSpec((t, g * MLA_VD), qmap), pl.BlockSpec((t, g * MLA_VD), qmap)],
        out_shape=[jax.ShapeDtypeStruct((T, MLA_HEADS * MLA_VD), F32),
                   jax.ShapeDtypeStruct((T, MLA_HEADS * MLA_VD), F32)],
        scratch_shapes=[vec, vec, vec],
        compiler_params=_cparams(("parallel", "arbitrary")),
    )(qf, kf, vf)


def _flash_delta(o, do, name):
    def fn(ov, dv):
        parts = []
        for h in range(MLA_HEADS):
            sl = slice(h * MLA_VD, (h + 1) * MLA_VD)
            d = jnp.sum(dv[:, sl] * ov[:, sl], axis=-1, keepdims=True)
            parts.append(jnp.broadcast_to(d, (d.shape[0], MLA_VD)))
        return jnp.concatenate(parts, axis=-1), dv
    w = MLA_HEADS * MLA_VD
    return _rows(fn, [o, do], [], [(w, F32), (w, BF16)], name=name)


def _flash_bwd(qf, kf, vf, do16, lse, delta, name):
    T = qf.shape[0]
    t = _pick(T, FLASH_T)
    n = T // t
    scale = MLA_QKD ** -0.5

    def body(q_ref, k_ref, v_ref, do_ref, lse_ref, dl_ref, dq_ref, dk_ref, dv_ref):
        kb = pl.program_id(1)

        @pl.when(kb == 0)
        def _():
            dq_ref[...] = jnp.zeros_like(dq_ref)

        dk_ref[...] = jnp.zeros_like(dk_ref)
        dv_ref[...] = jnp.zeros_like(dv_ref)
        k, v = k_ref[...], v_ref[...]

        def step(qb, masked):
            rows = pl.ds(pl.multiple_of(qb * t, t), t)
            q, dob = q_ref[rows, :], do_ref[rows, :]
            s = _dot_nt(q, k) * scale
            if masked:
                s = jnp.where(_chunk_mask(0, 0, t, t), s, NEG)
            p = jnp.exp(s - _widen(lse_ref[rows, :], t))
            ds = (p * (_dot_nt(dob, v) - _widen(dl_ref[rows, :], t)) * scale).astype(BF16)
            dv_ref[...] += _dot_tn(p.astype(BF16), dob)
            dk_ref[...] += _dot_tn(ds, q)
            dq_ref[rows, :] += _dot(ds, k)

        step(kb, True)

        @pl.loop(kb + 1, n)
        def _(qb):
            step(qb, False)

    qmap = lambda h, j: (0, h)
    kmap = lambda h, j: (j, h)
    return pl.pallas_call(
        body, name=name, grid=(MLA_HEADS, n),
        in_specs=[pl.BlockSpec((T, MLA_HP), qmap), pl.BlockSpec((t, MLA_HP), kmap), pl.BlockSpec((t, MLA_VD), kmap),
                  pl.BlockSpec((T, MLA_VD), qmap), pl.BlockSpec((T, MLA_VD), qmap), pl.BlockSpec((T, MLA_VD), qmap)],
        out_specs=[pl.BlockSpec((T, MLA_HP), qmap), pl.BlockSpec((t, MLA_HP), kmap), pl.BlockSpec((t, MLA_VD), kmap)],
        out_shape=[jax.ShapeDtypeStruct((T, MLA_HEADS * MLA_HP), F32),
                   jax.ShapeDtypeStruct((T, MLA_HEADS * MLA_HP), F32),
                   jax.ShapeDtypeStruct((T, MLA_HEADS * MLA_VD), F32)],
        compiler_params=_cparams(("arbitrary", "arbitrary")),
    )(qf, kf, vf, do16, lse, delta)


MESH = pl.DeviceIdType.MESH
ANY = pl.BlockSpec(memory_space=pl.ANY)
_CHIP_FLIPS = ((1, 0), (0, 1), (1, 1))


def _place():
    return lax.axis_index("x"), lax.axis_index("y"), lax.axis_index("c")


def _other_chip(x, y, k):
    fx, fy = _CHIP_FLIPS[k]
    return ((1 - x) if fx else x), ((1 - y) if fy else y)


def _remote(src, dst, send_sems, recv_sems, k, to):
    return pltpu.make_async_remote_copy(src_ref=src, dst_ref=dst, send_sem=send_sems.at[k], recv_sem=recv_sems.at[k],
                                        device_id=to, device_id_type=MESH)


def _index(*vals):
    return jnp.stack(vals).astype(jnp.int32)


def _half(c, rows):
    return pl.ds(pl.multiple_of(c * rows, 16), rows)


def _gather_weights(parts, name, landed=None):
    n_w = len(parts)
    n_in = n_w if landed is None else 2 * n_w

    def body(*refs):
        ins, outs = refs[:n_w], refs[n_in:n_in + n_w]
        send_sems, recv_sems, local_sems = refs[n_in + n_w:]
        x, y, c = _place()
        j = 2 * x + y
        sibling = (x, y, 1 - c)
        chips = [_other_chip(x, y, k) for k in range(3)]
        pending = []
        for w in range(n_w):
            own = pltpu.make_async_copy(ins[w], outs[w].at[j], local_sems.at[w])
            own.start()
            pending.append(own)
        sent = []
        for w in range(n_w):
            if landed is not None:
                break
            r = _half(c, parts[w].shape[0] // 2)
            for k, (px, py) in enumerate(chips):
                cp = _remote(ins[w].at[r], outs[w].at[j, r], send_sems, recv_sems, 6 * w + k, (px, py, c))
                cp.start()
                sent.append(cp)
        for w in range(n_w):
            r = _half(c, parts[w].shape[0] // 2)
            for k, (px, py) in enumerate(chips):
                blk = outs[w].at[2 * px + py, r]
                if landed is None:
                    _remote(blk, blk, send_sems, recv_sems, 6 * w + k, (px, py, c)).wait_recv()
                cp = _remote(blk, blk, send_sems, recv_sems, 6 * w + 3 + k, sibling)
                cp.start()
                sent.append(cp)
        for w in range(n_w):
            r = _half(1 - c, parts[w].shape[0] // 2)
            for k, (px, py) in enumerate(chips):
                blk = outs[w].at[2 * px + py, r]
                _remote(blk, blk, send_sems, recv_sems, 6 * w + 3 + k, sibling).wait_recv()
        for cp in sent:
            cp.wait_send()
        for cp in pending:
            cp.wait()

    return pl.pallas_call(
        body, name=name, in_specs=[pl.BlockSpec(memory_space=pltpu.VMEM)] * n_w + [ANY] * (n_in - n_w),
        out_specs=[ANY] * n_w,
        out_shape=[jax.ShapeDtypeStruct((N_CHIPS, *p.shape), p.dtype) for p in parts],
        input_output_aliases={} if landed is None else {n_w + w: w for w in range(n_w)},
        scratch_shapes=[pltpu.SemaphoreType.DMA((6 * n_w,)), pltpu.SemaphoreType.DMA((6 * n_w,)),
                        pltpu.SemaphoreType.DMA((n_w,))],
        compiler_params=pltpu.CompilerParams(vmem_limit_bytes=VMEM_LIMIT),
    )(*parts, *(landed or []))


def _swap_halves(gs, name):
    n_w = len(gs)

    def body(*refs):
        g_refs, recv_refs = refs[:n_w], refs[n_w:2 * n_w]
        send_sems, recv_sems = refs[2 * n_w:]
        x, y, c = _place()
        sent = []
        for w in range(n_w):
            for jj in range(N_CHIPS):
                cp = _remote(g_refs[w].at[jj, 1 - c], recv_refs[w].at[jj], send_sems, recv_sems, N_CHIPS * w + jj,
                             (x, y, 1 - c))
                cp.start()
                sent.append(cp)
        for cp in sent:
            cp.wait()

    return pl.pallas_call(
        body, name=name, in_specs=[ANY] * n_w, out_specs=[ANY] * n_w,
        out_shape=[jax.ShapeDtypeStruct((N_CHIPS, *g.shape[2:]), g.dtype) for g in gs],
        scratch_shapes=[pltpu.SemaphoreType.DMA((N_CHIPS * n_w,)), pltpu.SemaphoreType.DMA((N_CHIPS * n_w,))],
    )(*gs)


def _pair_sum(g, recv, core, name):
    _, H, C = recv.shape
    tile = _pick(H, 256)

    def body(c_ref, own_ref, recv_ref, out_ref):
        out_ref[...] = (own_ref[...].astype(F32) + recv_ref[...].astype(F32)).astype(BF16)

    blk = pl.BlockSpec((None, tile, C), lambda jj, i, c: (jj, i, 0))
    return pl.pallas_call(
        body, name=name,
        grid_spec=pltpu.PrefetchScalarGridSpec(
            num_scalar_prefetch=1, grid=(N_CHIPS, H // tile),
            in_specs=[pl.BlockSpec((None, None, tile, C), lambda jj, i, c: (jj, c[0], i, 0)), blk],
            out_specs=blk),
        out_shape=jax.ShapeDtypeStruct((N_CHIPS, H, C), BF16),
        compiler_params=_cparams(("arbitrary", "arbitrary")),
    )(_index(core), g, recv)


def _chip_sum(g, recv, got, chip, core, name):
    _, H, C = recv.shape
    tile = _pick(H, 256)

    def body(s_ref, own_ref, recv_ref, g0_ref, g1_ref, g2_ref, out_ref):
        pair = own_ref[...].astype(F32) + recv_ref[...].astype(F32)
        out_ref[...] = ((pair + g0_ref[...].astype(F32)) + g1_ref[...].astype(F32)) + g2_ref[...].astype(F32)

    def got_spec(k):
        return pl.BlockSpec((None, tile, C), lambda i, s, k=k: (k, i, 0))

    return pl.pallas_call(
        body, name=name,
        grid_spec=pltpu.PrefetchScalarGridSpec(
            num_scalar_prefetch=1, grid=(H // tile,),
            in_specs=[pl.BlockSpec((None, None, tile, C), lambda i, s: (s[0], s[1], i, 0)),
                      pl.BlockSpec((None, tile, C), lambda i, s: (s[0], i, 0)), got_spec(0), got_spec(1), got_spec(2)],
            out_specs=pl.BlockSpec((None, tile, C), lambda i, s: (s[1], i, 0))),
        out_shape=jax.ShapeDtypeStruct((2, H, C), F32),
        compiler_params=_cparams(("arbitrary",)),
    )(_index(chip, core), g, recv, got, got, got)


def _scatter_chips(sums, name):
    n_w = len(sums)

    def body(*refs):
        a_refs, got_refs = refs[:n_w], refs[n_w:2 * n_w]
        send_sems, recv_sems = refs[2 * n_w:]
        x, y, c = _place()
        j = 2 * x + y
        sent = []
        for w in range(n_w):
            for k in range(3):
                px, py = _other_chip(x, y, k)
                pj = 2 * px + py
                cp = _remote(a_refs[w].at[pj], got_refs[w].at[(j - pj + 4) % 4 - 1], send_sems, recv_sems, 3 * w + k,
                             (px, py, c))
                cp.start()
                sent.append(cp)
        for w in range(n_w):
            for k in range(3):
                px, py = _other_chip(x, y, k)
                slot = got_refs[w].at[(2 * px + py - j + 4) % 4 - 1]
                _remote(slot, slot, send_sems, recv_sems, 3 * w + k, (px, py, c)).wait_recv()
        for cp in sent:
            cp.wait_send()

    return pl.pallas_call(
        body, name=name, in_specs=[ANY] * n_w, out_specs=[ANY] * n_w,
        out_shape=[jax.ShapeDtypeStruct((3, *a.shape[1:]), a.dtype) for a in sums],
        scratch_shapes=[pltpu.SemaphoreType.DMA((3 * n_w,)), pltpu.SemaphoreType.DMA((3 * n_w,))],
    )(*sums)


def _share_halves(reds):
    n_w = len(reds)

    def body(*refs):
        out_refs = refs[n_w:2 * n_w]
        send_sems, recv_sems = refs[2 * n_w:]
        x, y, c = _place()
        sent = []
        for w in range(n_w):
            blk = out_refs[w].at[c]
            cp = _remote(blk, blk, send_sems, recv_sems, w, (x, y, 1 - c))
            cp.start()
            sent.append(cp)
        for cp in sent:
            cp.wait()

    return pl.pallas_call(
        body, name="grad_share_halves", in_specs=[ANY] * n_w, out_specs=[ANY] * n_w,
        out_shape=[jax.ShapeDtypeStruct(r.shape, r.dtype) for r in reds],
        input_output_aliases={w: w for w in range(n_w)},
        scratch_shapes=[pltpu.SemaphoreType.DMA((n_w,)), pltpu.SemaphoreType.DMA((n_w,))],
    )(*reds)


def _allsum_small(v, name):
    R, W = v.shape
    n_dev = 8
    vm = pl.BlockSpec(memory_space=pltpu.VMEM)

    def body(v_ref, out_ref, buf, send_sems, recv_sems):
        x, y, c = _place()
        me = 4 * x + 2 * y + c
        buf[me] = v_ref[...]
        sent = []
        for k in range(1, n_dev):
            peer = ((1 - x) if k & 4 else x, (1 - y) if k & 2 else y, (1 - c) if k & 1 else c)
            cp = _remote(v_ref, buf.at[me], send_sems, recv_sems, k - 1, peer)
            cp.start()
            sent.append(cp)
        for cp in sent:
            cp.wait_recv()
        for cp in sent:
            cp.wait_send()
        acc = buf[0]
        for q in range(1, n_dev):
            acc = acc + buf[q]
        out_ref[...] = acc

    return pl.pallas_call(
        body, name=name, in_specs=[vm], out_specs=vm, out_shape=jax.ShapeDtypeStruct((R, W), v.dtype),
        scratch_shapes=[pltpu.VMEM((n_dev, R, W), v.dtype), pltpu.SemaphoreType.DMA((n_dev - 1,)),
                        pltpu.SemaphoreType.DMA((n_dev - 1,))],
    )(v)


HBM = pl.BlockSpec(memory_space=pltpu.HBM)
SEM = pl.BlockSpec(memory_space=pltpu.SEMAPHORE)
_DATAFLOW = pltpu.SideEffectType.DATAFLOW_SIDE_EFFECTING


def _split_start(name, srcs, land_shapes, n_copies, copies, after=()):
    ns, nl = len(srcs), len(land_shapes)
    lands = [lax.empty(s.shape, s.dtype) for s in land_shapes]

    def body(*refs):
        outs = refs[ns + nl + len(after):]
        for cp in copies(refs[:ns], refs[ns:ns + nl], outs[0], outs[1]):
            cp.start()
        outs[-1][...] = jnp.zeros_like(outs[-1])

    sems = pltpu.SemaphoreType.DMA((n_copies,))
    res = pl.pallas_call(
        body, name=name, in_specs=[HBM] * (ns + nl) + [ANY] * len(after),
        out_specs=(SEM, SEM, *[HBM] * (ns + nl), pl.BlockSpec(memory_space=pltpu.VMEM)),
        out_shape=(sems, sems, *[pltpu.HBM(a.shape, a.dtype) for a in srcs],
                   *[pltpu.HBM(s.shape, s.dtype) for s in land_shapes], jax.ShapeDtypeStruct((8, 128), F32)),
        input_output_aliases={i: 2 + i for i in range(ns + nl)},
        compiler_params=pltpu.CompilerParams(has_side_effects=_DATAFLOW),
    )(*[pltpu.with_memory_space_constraint(a, pltpu.HBM) for a in [*srcs, *lands]], *after)
    return res[0], res[1], list(res[2:2 + ns]), list(res[2 + ns:2 + ns + nl]), res[-1]


def _split_wait(name, send_sems, recv_sems, srcs, lands, copies, after=()):
    ns, nl = len(srcs), len(lands)

    def body(*refs):
        for cp in copies(refs[:ns], refs[ns:ns + nl], refs[ns + nl], refs[ns + nl + 1]):
            cp.wait_send()
            cp.wait_recv()

    res = pl.pallas_call(
        body, name=name, in_specs=[HBM] * (ns + nl) + [SEM, SEM] + [ANY] * len(after), out_specs=[HBM] * (ns + nl),
        out_shape=[pltpu.HBM(a.shape, a.dtype) for a in [*srcs, *lands]],
        input_output_aliases={i: i for i in range(ns + nl)},
        compiler_params=pltpu.CompilerParams(has_side_effects=_DATAFLOW),
    )(*srcs, *lands, send_sems, recv_sems, *after)
    return list(res[ns:])


def _gather_copies(rows):
    def copies(src_refs, land_refs, send_sems, recv_sems):
        x, y, c = _place()
        j = 2 * x + y
        out = []
        for w in range(len(src_refs)):
            r = _half(c, rows[w] // 2)
            for k in range(3):
                px, py = _other_chip(x, y, k)
                out.append(_remote(src_refs[w].at[r], land_refs[w].at[j, r], send_sems, recv_sems, 3 * w + k, (px, py, c)))
        return out
    return copies


def _scatter_copies(src_refs, land_refs, send_sems, recv_sems):
    x, y, c = _place()
    j = 2 * x + y
    out = []
    for w in range(len(src_refs)):
        for k in range(3):
            px, py = _other_chip(x, y, k)
            pj = 2 * px + py
            out.append(_remote(src_refs[w].at[pj], land_refs[w].at[(j - pj + 4) % 4 - 1], send_sems, recv_sems, 3 * w + k,
                               (px, py, c)))
    return out


def _reduce_begin(grads, core, tag):
    names = list(grads)
    gs = [grads[k].reshape(N_CHIPS, 2, -1, grads[k].shape[-1]) for k in names]
    recvs = _swap_halves(gs, f"grad_swap_halves_{tag}")
    sums = [_pair_sum(g, r, core, f"pair_sum_{k}") for k, g, r in zip(names, gs, recvs)]
    return names, gs, recvs, sums


def _reduce_end(begun, gots, chip, core):
    names, gs, recvs, _ = begun
    return {k: _chip_sum(g, r, t, chip, core, f"chip_sum_{k}") for k, g, r, t in zip(names, gs, recvs, gots)}


def _got_shapes(sums):
    return [jax.ShapeDtypeStruct((3, *a.shape[1:]), a.dtype) for a in sums]


def _adamw(w, g, m, v, name, layers=1, layer=0, into=None):
    shape = w.shape
    cols = shape[-1]
    w3, m3, v3 = (t.reshape(layers, -1, cols) for t in (w, m, v))
    rows = w3.shape[1]
    tile = _pick(rows, 256) if rows % 8 == 0 else rows
    n_in = 4 + (0 if into is None else 4)

    def body(*refs):
        wv, gv, mv, vv = (r[...] for r in refs[:4])
        g_ref, d_ref, m_ref, v_ref = refs[n_in:]
        m2 = ADAM_B1 * mv + (1.0 - ADAM_B1) * gv
        v2 = ADAM_B2 * vv + (1.0 - ADAM_B2) * jnp.square(gv)
        m_hat = m2 / (1.0 - ADAM_B1 ** ADAM_STEP)
        v_hat = v2 / (1.0 - ADAM_B2 ** ADAM_STEP)
        g_ref[...] = gv
        d_ref[...] = -ADAM_LR * (m_hat / (jnp.sqrt(v_hat) + ADAM_EPS) + ADAM_WD * wv)
        m_ref[...] = m2
        v_ref[...] = v2

    lay = pl.BlockSpec((None, tile, cols), lambda i: (layer, i, 0))
    out = jax.ShapeDtypeStruct((layers, rows, cols), F32)
    res = pl.pallas_call(
        body, name=name, grid=(rows // tile,),
        in_specs=[lay, pl.BlockSpec((tile, cols), lambda i: (i, 0)), lay, lay] + [ANY] * (n_in - 4),
        out_specs=[lay] * 4, out_shape=[out] * 4,
        input_output_aliases={} if into is None else {4 + k: k for k in range(4)},
        compiler_params=_cparams(("arbitrary",)),
    )(w3, g.reshape(rows, cols), m3, v3, *([] if into is None else [t.reshape(layers, rows, cols) for t in into]))
    return tuple(t.reshape(shape) for t in res)


def _add_res(acc, r):
    return (r + acc,)


def _tail_fwd(h1, p16, W, i, tag):
    hn2 = _norm_fwd(h1, W["mlp_norm"][i:i + 1], f"{tag}_mlp_norm")
    a = _mm(hn2, W["mlp_w1"][i], bblk=True, outs=[BF16], name=f"{tag}_mlp_w1",
            epilogue=lambda acc: (jnp.square(jnp.maximum(acc, 0.0)),))
    h2 = _mm(a, W["mlp_w2"][i], extras=[h1], epilogue=_add_res, name=f"{tag}_mlp_w2")
    hn3 = _norm_fwd(h2, W["ple_norm"][i:i + 1], f"{tag}_ple_norm")
    gl = _mm(hn3, W["ple_gate_w"][i], name=f"{tag}_ple_gate")
    h3, pp = _mm(p16[i], W["ple_proj_w"][i], bblk=True, extras=[gl, h2], outs=[F32, F32], name=f"{tag}_ple_proj",
                 epilogue=lambda acc, g, h: (h + _sigmoid(g) * acc, acc))
    return h3, (h1, hn2, a, h2, hn3, gl, pp)


def _tail_bwd(dh3, saved, p16, W, i, tag, after=()):
    h1, hn2, a, h2, hn3, gl, pp = saved

    def gate_bwd(d, g, ppv):
        gate = _sigmoid(g)
        return d * gate, d * ppv * gate * (1.0 - gate)

    def dw(kind, name):
        return (kind, 1, 0, None)

    dpp, dgl = _rows(gate_bwd, [dh3, gl, pp], [], [(D_MODEL, BF16), (D_MODEL, BF16)], name=f"{tag}_ple_gate_bwd",
                     after=after)
    d_proj = _mm(p16[i], dpp, ta=True, outs=[BF16], dw=dw("cols", "ple_proj_w"), name=f"{tag}_d_ple_proj")
    d_gate = _mm(hn3, dgl, ta=True, outs=[BF16], dw=dw("rows", "ple_gate_w"), name=f"{tag}_d_ple_gate")
    dhn3 = _mm(dgl, W["ple_gate_w"][i], tb=True, name=f"{tag}_ple_gate_dx")
    dh2, dh2_16, d_ple_norm = _norm_bwd(h2, dhn3, W["ple_norm"][i:i + 1], dh3, f"{tag}_ple_norm_bwd")
    d_w2 = _mm(a, dh2_16, ta=True, outs=[BF16], dw=dw("rows", "mlp_w2"), name=f"{tag}_d_mlp_w2")
    dz = _mm(dh2_16, W["mlp_w2"][i], tb=True, extras=[a], outs=[BF16], name=f"{tag}_mlp_w2_dx",
             epilogue=lambda acc, av: (acc * (2.0 * jnp.sqrt(av.astype(F32))),))
    d_w1 = _mm(hn2, dz, ta=True, outs=[BF16], dw=dw("cols", "mlp_w1"), name=f"{tag}_d_mlp_w1")
    dhn2 = _mm(dz, W["mlp_w1"][i], tb=True, bblk=True, name=f"{tag}_mlp_w1_dx")
    dh1, dh1_16, d_mlp_norm = _norm_bwd(h1, dhn2, W["mlp_norm"][i:i + 1], dh2, f"{tag}_mlp_norm_bwd")
    big = {f"mlp_w1_{i}": d_w1, f"mlp_w2_{i}": d_w2, f"ple_gate_w_{i}": d_gate, f"ple_proj_w_{i}": d_proj}
    return dh1, dh1_16, big, dict(mlp_norm=d_mlp_norm, ple_norm=d_ple_norm)


def _ret_layer_fwd(h0, W, tabs, after=()):
    hn = _rows(lambda x, g: (_rms(x, g),), [h0], [W["mix_norm"][0:1]], [(D_MODEL, BF16)], name="ret_mix_norm",
               after=after)[0]
    proj = _mm(hn, W["ret_w_in"], bblk=True, name="ret_w_in")
    out, states = _ret_fwd(proj, tabs, "ret_scan")
    y = _ret_gate(out, proj, W["ret_gn"], "ret_gate")
    h1 = _mm(y, W["ret_w_out"], extras=[h0], epilogue=_add_res, name="ret_w_out")
    return h1, (h0, hn, proj, out, states, y)


def _ret_layer_bwd(dh1, dh1_16, saved, W, tabs, after=()):
    h0, hn, proj, out, states, y = saved
    d_w_out = _mm(y, dh1_16, ta=True, outs=[BF16], dw=("rows", 1, 0, None), name="d_ret_w_out", after=after)
    dy = _mm(dh1_16, W["ret_w_out"], tb=True, name="ret_w_out_dx", after=after)
    dout, dg, d_gn = _ret_gate_bwd(out, proj, W["ret_gn"], dy, "ret_gate_bwd")
    dq, dk, dv = _ret_bwd(proj, states, dout, tabs, "ret_scan_bwd")
    dproj = jnp.concatenate([dq, dk, dv, dg], axis=1)
    d_w_in = _mm(hn, dproj, ta=True, outs=[BF16], dw=("cols", 1, 0, None), name="d_ret_w_in")
    dhn = _mm(dproj, W["ret_w_in"], tb=True, bblk=True, tn=256, name="ret_w_in_dx")
    dh0, _, d_mix = _norm_bwd(h0, dhn, W["mix_norm"][0:1], dh1, "ret_mix_norm_bwd")
    return dh0, dict(ret_w_in=d_w_in, ret_w_out=d_w_out), dict(mix_norm=d_mix, ret_gn=d_gn)


def _mla_layer_fwd(h0, W, tabs):
    hn = _norm_fwd(h0, W["mix_norm"][1:2], "mla_mix_norm")
    proj = _mm(hn, W["mla_w_in"], name="mla_w_in")

    def low_rank_norm(pv, gq, gkv):
        return _rms(pv[:, :MLA_Q_RANK], gq), _rms(pv[:, MLA_Q_RANK:MLA_Q_RANK + MLA_KV_RANK], gkv)

    cqn, ckvn = _rows(low_rank_norm, [proj], [W["mla_q_a_norm"], W["mla_kv_a_norm"]],
                      [(MLA_Q_RANK, BF16), (MLA_KV_RANK, BF16)], name="mla_low_rank_norm")
    q = _mm(cqn, W["mla_w_uq"], bblk=True, name="mla_w_uq")
    kv = _mm(ckvn, W["mla_w_ukv"], bblk=True, name="mla_w_ukv")
    qf, kf, vf = _mla_prep(q, kv, proj, W["mla_q_norm"], W["mla_k_norm"], tabs, "mla_prep")
    o, lse = _flash_fwd(qf, kf, vf, "mla_flash")
    h1 = _mm(o, W["mla_w_out"], extras=[h0], epilogue=_add_res, name="mla_w_out")
    return h1, (h0, hn, proj, cqn, ckvn, q, kv, qf, kf, vf, o, lse)


def _mla_layer_bwd(dh1, dh1_16, saved, W, tabs):
    h0, hn, proj, cqn, ckvn, q, kv, qf, kf, vf, o, lse = saved
    d_w_out = _mm(o, dh1_16, ta=True, outs=[BF16], dw=("rows", 1, 0, None), name="d_mla_w_out")
    do = _mm(dh1_16, W["mla_w_out"], tb=True, name="mla_w_out_dx")
    delta, do16 = _flash_delta(o, do, "mla_flash_delta")
    dqf, dkf, dvf = _flash_bwd(qf, kf, vf, do16, lse, delta, "mla_flash_bwd")
    dq, dkv, dkr, d_gq, d_gk = _mla_prep_bwd(q, kv, proj, W["mla_q_norm"], W["mla_k_norm"], tabs, dqf, dkf, dvf,
                                             "mla_prep_bwd")
    d_w_uq = _mm(cqn, dq, ta=True, outs=[BF16], dw=("cols", 1, 0, None), name="d_mla_w_uq")
    dcqn = _mm(dq, W["mla_w_uq"], tb=True, bblk=True, name="mla_w_uq_dx")
    d_w_ukv = _mm(ckvn, dkv, ta=True, outs=[BF16], dw=("cols", 1, 0, None), name="d_mla_w_ukv")
    dckvn = _mm(dkv, W["mla_w_ukv"], tb=True, bblk=True, name="mla_w_ukv_dx")

    def low_rank_bwd(pv, dcq, dckv, dkr_v, gq, gkv):
        dxq, dgq = _rms_bwd(pv[:, :MLA_Q_RANK], dcq, gq)
        dxkv, dgkv = _rms_bwd(pv[:, MLA_Q_RANK:MLA_Q_RANK + MLA_KV_RANK], dckv, gkv)
        return jnp.concatenate([dxq, dxkv, dkr_v], axis=-1), _colsum(dgq), _colsum(dgkv)

    dproj, d_gqa, d_gkva = _rows(low_rank_bwd, [proj, dcqn, dckvn, dkr], [W["mla_q_a_norm"], W["mla_kv_a_norm"]],
                                 [(MLA_IN_PAD, BF16)], [((1, MLA_Q_RANK), F32), ((1, MLA_KV_RANK), F32)],
                                 name="mla_low_rank_norm_bwd")
    d_w_in = _mm(hn, dproj, ta=True, outs=[BF16], dw=("rows", 1, 0, None), name="d_mla_w_in")
    dhn = _mm(dproj, W["mla_w_in"], tb=True, name="mla_w_in_dx")
    dh0, dh0_16, d_mix = _norm_bwd(h0, dhn, W["mix_norm"][1:2], dh1, "mla_mix_norm_bwd")
    return (dh0, dh0_16, dict(mla_w_in=d_w_in, mla_w_uq=d_w_uq, mla_w_ukv=d_w_ukv, mla_w_out=d_w_out),
            dict(mix_norm=d_mix, mla_q_a_norm=d_gqa, mla_kv_a_norm=d_gkva, mla_q_norm=d_gq, mla_k_norm=d_gk))


def _local_step(x, p16, target, W):
    T = x.shape[0]
    ret_tabs, mla_tabs = _ret_tables(T), _mla_tables(T)
    h1, s_ret = _ret_layer_fwd(x, W, ret_tabs)
    h3, s_tail0 = _tail_fwd(h1, p16, W, 0, "l0")
    h4, s_mla = _mla_layer_fwd(h3, W, mla_tabs)
    y, s_tail1 = _tail_fwd(h4, p16, W, 1, "l1")
    dy, loss = _loss_head(y, target)
    dh4, dh4_16, g_t1, n_t1 = _tail_bwd(dy, s_tail1, p16, W, 1, "l1")
    dh3, _, g_mla, n_mla = _mla_layer_bwd(dh4, dh4_16, s_mla, W, mla_tabs)
    dh1, dh1_16, g_t0, n_t0 = _tail_bwd(dh3, s_tail0, p16, W, 0, "l0")
    dx, g_ret, n_ret = _ret_layer_bwd(dh1, dh1_16, s_ret, W, ret_tabs)
    return loss, dx, {**g_ret, **g_t0, **g_mla, **g_t1}, _small_grads(n_ret, n_t0, n_mla, n_t1)


def _loss_head(y, target):
    def fn(yv, tv):
        e = yv - tv
        return e * (1.0 / D_MODEL), jnp.full((1, 128), 0.5 / D_MODEL, F32) * jnp.sum(e * e)
    return _rows(fn, [y, target], [], [(D_MODEL, F32)], [((1, 128), F32)], name="loss_head")


def _small_grads(n_ret, n_t0, n_mla, n_t1):
    return dict(
        mix_norm=jnp.concatenate([n_ret["mix_norm"], n_mla["mix_norm"]], axis=0),
        mlp_norm=jnp.concatenate([n_t0["mlp_norm"], n_t1["mlp_norm"]], axis=0),
        ple_norm=jnp.concatenate([n_t0["ple_norm"], n_t1["ple_norm"]], axis=0),
        ret_gn=n_ret["ret_gn"], mla_q_a_norm=n_mla["mla_q_a_norm"], mla_kv_a_norm=n_mla["mla_kv_a_norm"],
        mla_q_norm=n_mla["mla_q_norm"], mla_k_norm=n_mla["mla_k_norm"])


_ORDER = ("mix_norm", "ret_w_in", "ret_gn", "ret_w_out", "mla_w_in", "mla_q_a_norm", "mla_kv_a_norm", "mla_w_uq",
          "mla_w_ukv", "mla_q_norm", "mla_k_norm", "mla_w_out", "mlp_norm", "mlp_w1", "mlp_w2", "ple_norm",
          "ple_gate_w", "ple_proj_w")
_TWO_LAYER = ("mlp_w1", "mlp_w2", "ple_gate_w", "ple_proj_w")
HEADS_PER_CHIP = MLA_HEADS // N_CHIPS


def _travel_parts(w):
    uq = jnp.pad(w["mla_w_uq"][0].reshape(MLA_Q_RANK, HEADS_PER_CHIP, MLA_QKD), ((0, 0), (0, 0), (0, MLA_HP - MLA_QKD)))
    parts = {"ret_w_in": w["ret_w_in"][0], "ret_w_out": w["ret_w_out"][0]}
    for k in _TWO_LAYER:
        parts[k + "_0"] = w[k][0]
    parts["mla_w_in"] = jnp.pad(w["mla_w_in"][0], ((0, 0), (0, MLA_IN_PAD - MLA_IN)))
    parts["mla_w_uq"] = uq.reshape(MLA_Q_RANK, HEADS_PER_CHIP * MLA_HP)
    parts["mla_w_ukv"] = w["mla_w_ukv"][0]
    parts["mla_w_out"] = w["mla_w_out"][0]
    for k in _TWO_LAYER:
        parts[k + "_1"] = w[k][1]
    return {k: v.astype(BF16) for k, v in parts.items()}


def _full_weights(full, small):
    rows = lambda a: a.reshape(-1, a.shape[-1])
    W = {k: full[k] for k in ("ret_w_in", "mla_w_uq", "mla_w_ukv")}
    for k in ("ret_w_out", "mla_w_in", "mla_w_out"):
        W[k] = rows(full[k])
    W["mlp_w1"] = [full["mlp_w1_0"], full["mlp_w1_1"]]
    W["ple_proj_w"] = [full["ple_proj_w_0"], full["ple_proj_w_1"]]
    W["mlp_w2"] = [rows(full["mlp_w2_0"]), rows(full["mlp_w2_1"])]
    W["ple_gate_w"] = [rows(full["ple_gate_w_0"]), rows(full["ple_gate_w_1"])]
    W["ret_gn"] = small[0:2].reshape(RET_HEADS, RET_DV)
    W["mla_q_a_norm"] = small[2:3, :MLA_Q_RANK]
    W["mla_kv_a_norm"] = small[3:4, :MLA_KV_RANK]
    return W


def _shard_grad(name, red, shape):
    if name == "mla_w_in":
        red = red.reshape(-1, MLA_IN_PAD)[:, :MLA_IN]
    elif name == "mla_w_uq":
        red = red.reshape(MLA_Q_RANK, HEADS_PER_CHIP, MLA_HP)[:, :, :MLA_QKD]
    return red.reshape(shape)


def _pad_row(v):
    v = v.reshape(1, -1)
    return jnp.pad(v, ((0, 0), (0, PACK_W - v.shape[1])))


def kernel(x, p, mix_norm, ret_w_in, ret_gn, ret_w_out, mla_w_in, mla_q_a_norm, mla_kv_a_norm, mla_w_uq, mla_w_ukv, mla_q_norm, mla_k_norm, mla_w_out, mlp_norm, mlp_w1, mlp_w2, ple_norm, ple_gate_w, ple_proj_w, loss_target, m_mix_norm, m_ret_w_in, m_ret_gn, m_ret_w_out, m_mla_w_in, m_mla_q_a_norm, m_mla_kv_a_norm, m_mla_w_uq, m_mla_w_ukv, m_mla_q_norm, m_mla_k_norm, m_mla_w_out, m_mlp_norm, m_mlp_w1, m_mlp_w2, m_ple_norm, m_ple_gate_w, m_ple_proj_w, v_mix_norm, v_ret_w_in, v_ret_gn, v_ret_w_out, v_mla_w_in, v_mla_q_a_norm, v_mla_kv_a_norm, v_mla_w_uq, v_mla_w_ukv, v_mla_q_norm, v_mla_k_norm, v_mla_w_out, v_mlp_norm, v_mlp_w1, v_mlp_w2, v_ple_norm, v_ple_gate_w, v_ple_proj_w):
    w = dict(mix_norm=mix_norm, ret_w_in=ret_w_in, ret_gn=ret_gn, ret_w_out=ret_w_out, mla_w_in=mla_w_in,
             mla_q_a_norm=mla_q_a_norm, mla_kv_a_norm=mla_kv_a_norm, mla_w_uq=mla_w_uq, mla_w_ukv=mla_w_ukv,
             mla_q_norm=mla_q_norm, mla_k_norm=mla_k_norm, mla_w_out=mla_w_out, mlp_norm=mlp_norm, mlp_w1=mlp_w1,
             mlp_w2=mlp_w2, ple_norm=ple_norm, ple_gate_w=ple_gate_w, ple_proj_w=ple_proj_w)
    m = dict(mix_norm=m_mix_norm, ret_w_in=m_ret_w_in, ret_gn=m_ret_gn, ret_w_out=m_ret_w_out, mla_w_in=m_mla_w_in,
             mla_q_a_norm=m_mla_q_a_norm, mla_kv_a_norm=m_mla_kv_a_norm, mla_w_uq=m_mla_w_uq, mla_w_ukv=m_mla_w_ukv,
             mla_q_norm=m_mla_q_norm, mla_k_norm=m_mla_k_norm, mla_w_out=m_mla_w_out, mlp_norm=m_mlp_norm,
             mlp_w1=m_mlp_w1, mlp_w2=m_mlp_w2, ple_norm=m_ple_norm, ple_gate_w=m_ple_gate_w, ple_proj_w=m_ple_proj_w)
    v = dict(mix_norm=v_mix_norm, ret_w_in=v_ret_w_in, ret_gn=v_ret_gn, ret_w_out=v_ret_w_out, mla_w_in=v_mla_w_in,
             mla_q_a_norm=v_mla_q_a_norm, mla_kv_a_norm=v_mla_kv_a_norm, mla_w_uq=v_mla_w_uq, mla_w_ukv=v_mla_w_ukv,
             mla_q_norm=v_mla_q_norm, mla_k_norm=v_mla_k_norm, mla_w_out=v_mla_w_out, mlp_norm=v_mlp_norm,
             mlp_w1=v_mlp_w1, mlp_w2=v_mlp_w2, ple_norm=v_ple_norm, ple_gate_w=v_ple_gate_w, ple_proj_w=v_ple_proj_w)
    xi, yi, ci = _place()
    chip = 2 * xi + yi
    n = N_CHIPS

    parts = _travel_parts(w)
    first = ("ret_w_in", "ret_w_out")
    later = [k for k in parts if k not in first]
    full = dict(zip(first, _gather_weights([parts[k] for k in first], "gather_first")))
    later_copies = _gather_copies([parts[k].shape[0] for k in later])
    g_send, g_recv, later_src, later_land, g_token = _split_start(
        "gather_later_start", [parts[k] for k in later],
        [jax.ShapeDtypeStruct((n, *parts[k].shape), BF16) for k in later], 3 * len(later), later_copies,
        after=[full["ret_w_in"]])
    on = (jnp.arange(n) == chip) & (ci == 0)
    gn_rows = jnp.where(on[None, :, None], ret_gn[0][:, None, :], 0.0).reshape(2, PACK_W)
    qa_row = _pad_row(jnp.where(on[:, None], mla_q_a_norm, 0.0))
    kva_row = _pad_row(jnp.where(on[:, None], mla_kv_a_norm, 0.0))
    small_in = jnp.concatenate([gn_rows, qa_row, kva_row, jnp.zeros((4, PACK_W), F32)], axis=0)
    small = _allsum_small(small_in, "gather_gains")
    W = dict(mix_norm=mix_norm, mlp_norm=mlp_norm, ple_norm=ple_norm,
             mla_q_norm=jnp.pad(mla_q_norm, ((0, 0), (0, MLA_HP - MLA_QKD))),
             mla_k_norm=jnp.pad(mla_k_norm, ((0, 0), (0, MLA_HP - MLA_QKD))),
             ret_w_in=full["ret_w_in"], ret_w_out=full["ret_w_out"].reshape(-1, D_MODEL),
             ret_gn=small[0:2].reshape(RET_HEADS, RET_DV))
    x0, p16, target = x[0], p[:, 0].astype(BF16), loss_target[0]
    T = x0.shape[0]
    ret_tabs, mla_tabs = _ret_tables(T), _mla_tables(T)

    h1, s_ret = _ret_layer_fwd(x0, W, ret_tabs, after=[g_token])
    landed = _split_wait("gather_later_wait", g_send, g_recv, later_src, later_land, later_copies, after=[h1])
    full.update(zip(later, _gather_weights([parts[k] for k in later], "gather_later_finish", landed=landed)))
    W.update(_full_weights(full, small))
    h3, s_tail0 = _tail_fwd(h1, p16, W, 0, "l0")
    h4, s_mla = _mla_layer_fwd(h3, W, mla_tabs)
    y, s_tail1 = _tail_fwd(h4, p16, W, 1, "l1")
    dy, loss = _loss_head(y, target)

    dh4, dh4_16, g_t1, n_t1 = _tail_bwd(dy, s_tail1, p16, W, 1, "l1")
    dh3, _, g_mla, n_mla = _mla_layer_bwd(dh4, dh4_16, s_mla, W, mla_tabs)
    beg_a = _reduce_begin({**g_mla, **g_t1}, ci, "a")
    a_send, a_recv, a_src, a_land, a_token = _split_start(
        "scatter_a_start", beg_a[3], _got_shapes(beg_a[3]), 3 * len(beg_a[3]), _scatter_copies)
    dh1, dh1_16, g_t0, n_t0 = _tail_bwd(dh3, s_tail0, p16, W, 0, "l0", after=[a_token])
    beg_b = _reduce_begin(g_t0, ci, "b")
    b_send, b_recv, b_src, b_land, b_token = _split_start(
        "scatter_b_start", beg_b[3], _got_shapes(beg_b[3]), 3 * len(beg_b[3]), _scatter_copies)
    dx, g_ret, n_ret = _ret_layer_bwd(dh1, dh1_16, s_ret, W, ret_tabs, after=[b_token])
    got_a = _split_wait("scatter_a_wait", a_send, a_recv, a_src, a_land, _scatter_copies, after=[dx])
    got_b = _split_wait("scatter_b_wait", b_send, b_recv, b_src, b_land, _scatter_copies, after=[dx])
    beg_c = _reduce_begin(g_ret, ci, "c")
    got_c = _scatter_chips(beg_c[3], "grad_scatter_chips_c")
    red = {**_reduce_end(beg_a, got_a, chip, ci), **_reduce_end(beg_b, got_b, chip, ci),
           **_reduce_end(beg_c, got_c, chip, ci)}
    red = dict(zip(red, _share_halves(list(red.values()))))
    gs = _small_grads(n_ret, n_t0, n_mla, n_t1)
    small_g = jnp.concatenate([
        gs["mix_norm"], gs["mlp_norm"], gs["ple_norm"], gs["ret_gn"].reshape(2, PACK_W), _pad_row(gs["mla_q_a_norm"]),
        _pad_row(gs["mla_kv_a_norm"]), _pad_row(gs["mla_q_norm"][:, :MLA_QKD]), _pad_row(gs["mla_k_norm"][:, :MLA_QKD]),
        _pad_row(loss[:, :1]), jnp.zeros((3, PACK_W), F32)], axis=0)
    tot = _allsum_small(small_g, "sum_small_grads")
    gn_all = tot[6:8].reshape(RET_HEADS, n, -1)
    g_small = dict(
        mix_norm=tot[0:2], mlp_norm=tot[2:4], ple_norm=tot[4:6],
        ret_gn=lax.dynamic_index_in_dim(gn_all, chip, axis=1, keepdims=False),
        mla_q_a_norm=lax.dynamic_index_in_dim(tot[8, :MLA_Q_RANK].reshape(n, -1), chip, axis=0, keepdims=True),
        mla_kv_a_norm=lax.dynamic_index_in_dim(tot[9, :MLA_KV_RANK].reshape(n, -1), chip, axis=0, keepdims=True),
        mla_q_norm=tot[10:11, :MLA_QKD], mla_k_norm=tot[11:12, :MLA_QKD])
    loss_out = tot[12, 0]

    outs = []
    for k in _ORDER:
        if k in _TWO_LAYER:
            res = None
            for i in (1, 0):
                res = _adamw(w[k], red[f"{k}_{i}"], m[k], v[k], f"adamw_{k}_{i}", layers=2, layer=i, into=res)
        elif k in red:
            res = _adamw(w[k], _shard_grad(k, red[k], w[k].shape), m[k], v[k], f"adamw_{k}")
        else:
            res = _adamw(w[k], g_small[k], m[k], v[k], f"adamw_{k}")
        outs.append(res)
    return (loss_out, dx[None], *[o[0] for o in outs], *[o[1] for o in outs], *[o[2] for o in outs],
            *[o[3] for o in outs])
```

```python
import functools

import jax
import jax.numpy as jnp
from jax import lax
from jax.experimental import pallas as pl
from jax.experimental.pallas import tpu as pltpu

F32 = jnp.float32
BF16 = jnp.bfloat16

EPS = 1e-6
D_MODEL = 1024
CHUNK = 64
ROPE_THETA = 10000.0
RET_HEADS = 4
RET_DK = 256
RET_DV = 512
RET_GROUP = 1
MLA_HEADS = 8
MLA_NOPE = 128
MLA_ROPE = 64
MLA_QKD = 192
MLA_VD = 128
MLA_HP = 256
MLA_Q_RANK = 384
MLA_KV_RANK = 256
MLA_IN = 704
MLA_IN_PAD = 768
D_FF = 4096
PLE_DIM = 256
N_CHIPS = 4

ADAM_LR = 0.001
ADAM_B1 = 0.9
ADAM_B2 = 0.999
ADAM_EPS = 1e-08
ADAM_WD = 0.01
ADAM_STEP = 10

VMEM_LIMIT = 56 * 1024 * 1024
PACK_W = 1024
NEG = -1e30
FLASH_T = 512
FLASH_HEADS = 2
MM_SUB_ROWS = 256


def _cparams(sem=None):
    return pltpu.CompilerParams(dimension_semantics=sem, vmem_limit_bytes=VMEM_LIMIT)


def _pick(dim, pref):
    if dim <= pref:
        return dim
    t = pref
    while dim % t:
        t //= 2
    return t


def _mm(a, b, *, name, ta=False, tb=False, bblk=False, outs=None, extras=(), epilogue=None, dw=None,
        tm=1024, tn=512, after=()):
    if ta:
        K, M = a.shape
    else:
        M, K = a.shape
    if bblk and tb:
        nb, N, Kq = b.shape
        assert nb * Kq == K
    elif bblk:
        nb, Kb, Nq = b.shape
        N = nb * Nq
        assert Kb == K
    else:
        N = b.shape[0] if tb else b.shape[1]
    tn = _pick(Nq if (bblk and not tb) else N, tn)
    if dw is not None and dw[0] == "cols":
        tn = _pick(N // N_CHIPS, tn)
    tm = _pick(M // N_CHIPS if (dw is not None and dw[0] == "rows") else M, tm)
    grid = (M // tm, N // tn)

    a_spec = pl.BlockSpec((K, tm), lambda i, j: (0, i)) if ta else pl.BlockSpec((tm, K), lambda i, j: (i, 0))
    if bblk and tb:
        b_spec = pl.BlockSpec((nb, tn, Kq), lambda i, j: (0, j, 0))
    elif bblk:
        npb = Nq // tn
        b_spec = pl.BlockSpec((None, K, tn), lambda i, j: (j // npb, 0, j % npb))
    elif tb:
        b_spec = pl.BlockSpec((tn, K), lambda i, j: (j, 0))
    else:
        b_spec = pl.BlockSpec((K, tn), lambda i, j: (0, j))
    in_specs = [a_spec, b_spec] + [pl.BlockSpec((tm, tn), lambda i, j: (i, j)) for _ in extras]
    args = [a, b, *extras]
    aliases = {}
    if outs is None:
        outs = [F32]
    if dw is None:
        o_specs = [pl.BlockSpec((tm, tn), lambda i, j: (i, j)) for _ in outs]
        o_shapes = [jax.ShapeDtypeStruct((M, N), dt) for dt in outs]
    else:
        kind, layers, layer, into = dw
        if kind == "cols":
            per = (N // N_CHIPS) // tn
            o_specs = [pl.BlockSpec((None, None, tm, tn), lambda i, j: (j // per, layer, i, j % per))]
            o_shapes = [jax.ShapeDtypeStruct((N_CHIPS, layers, M, N // N_CHIPS), outs[0])]
        else:
            per = (M // N_CHIPS) // tm
            o_specs = [pl.BlockSpec((None, None, tm, tn), lambda i, j: (i // per, layer, i % per, j))]
            o_shapes = [jax.ShapeDtypeStruct((N_CHIPS, layers, M // N_CHIPS, N), outs[0])]
        if into is not None:
            aliases = {len(args): 0}
            in_specs.append(pl.BlockSpec(memory_space=pl.ANY))
            args.append(into)
    for t in after:
        in_specs.append(pl.BlockSpec(memory_space=pl.ANY))
        args.append(t)
    n_e, n_o = len(extras), len(outs)

    sub = _pick(tm, MM_SUB_ROWS)

    def body(a_ref, b_ref, *rest):
        e_refs, o_refs = rest[:n_e], rest[len(rest) - n_o:]
        for r0 in range(0, tm, sub):
            rows = slice(r0, r0 + sub)
            av = (a_ref[:, rows] if ta else a_ref[rows, :]).astype(BF16)
            if bblk and tb:
                acc = _dot_nt(av[:, :Kq], b_ref[0].astype(BF16))
                for s in range(1, nb):
                    acc = acc + _dot_nt(av[:, s * Kq:(s + 1) * Kq], b_ref[s].astype(BF16))
            elif ta:
                acc = _dot_tn(av, b_ref[...].astype(BF16))
            elif tb:
                acc = _dot_nt(av, b_ref[...].astype(BF16))
            else:
                acc = _dot(av, b_ref[...].astype(BF16))
            vals = (acc,) if epilogue is None else epilogue(acc, *[e[rows, :] for e in e_refs])
            for o, v in zip(o_refs, vals):
                o[rows, :] = v.astype(o.dtype)

    res = pl.pallas_call(
        body, name=name, grid=grid, in_specs=in_specs, out_specs=o_specs, out_shape=o_shapes,
        input_output_aliases=aliases, compiler_params=_cparams(("parallel", "arbitrary")),
    )(*args)
    return res[0] if n_o == 1 else res


def _rows(fn, rows, fulls, outs, accs=(), *, name, tile=512, after=()):
    first = rows[0][0] if isinstance(rows[0], tuple) else rows[0]
    T = first.shape[0]
    tile = _pick(T, tile)
    in_specs, args = [], []
    for r in rows:
        if isinstance(r, tuple):
            arr, w, cb = r
            in_specs.append(pl.BlockSpec((tile, w), lambda i, cb=cb: (i, cb)))
        else:
            arr = r
            in_specs.append(pl.BlockSpec((tile, arr.shape[1]), lambda i: (i, 0)))
        args.append(arr)
    for f in fulls:
        in_specs.append(pl.BlockSpec(f.shape, lambda i, nd=f.ndim: (0,) * nd))
        args.append(f)
    out_specs = [pl.BlockSpec((tile, w), lambda i: (i, 0)) for w, _ in outs]
    out_specs += [pl.BlockSpec(s, lambda i: (0, 0)) for s, _ in accs]
    out_shape = [jax.ShapeDtypeStruct((T, w), dt) for w, dt in outs]
    out_shape += [jax.ShapeDtypeStruct(s, dt) for s, dt in accs]
    n_in, n_out = len(args), len(outs)
    for t in after:
        in_specs.append(pl.BlockSpec(memory_space=pl.ANY))
        args.append(t)

    def body(*refs):
        vals = fn(*[r[...] for r in refs[:n_in]])
        o_refs = refs[len(args):]
        for o, v in zip(o_refs[:n_out], vals[:n_out]):
            o[...] = v.astype(o.dtype)
        first_step = pl.program_id(0) == 0
        for o, v in zip(o_refs[n_out:], vals[n_out:]):
            @pl.when(first_step)
            def _(o=o, v=v):
                o[...] = v.astype(o.dtype)

            @pl.when(jnp.logical_not(first_step))
            def _(o=o, v=v):
                o[...] += v.astype(o.dtype)

    res = pl.pallas_call(
        body, name=name, grid=(T // tile,), in_specs=in_specs, out_specs=out_specs, out_shape=out_shape,
        compiler_params=_cparams(("arbitrary",)),
    )(*args)
    return res


def _rms(x, g):
    r = lax.rsqrt(jnp.mean(x * x, axis=-1, keepdims=True) + EPS)
    return (x * r) * g


def _rms_bwd(x, dy, g, n=None):
    n = x.shape[-1] if n is None else n
    r = lax.rsqrt(jnp.sum(x * x, axis=-1, keepdims=True) / n + EPS)
    xh = x * r
    dxh = dy * g
    dx = r * (dxh - xh * (jnp.sum(dxh * xh, axis=-1, keepdims=True) / n))
    return dx, dy * xh


def _colsum(v):
    return jnp.sum(v, axis=0, keepdims=True)


def _sigmoid(x):
    return 1.0 / (1.0 + jnp.exp(-x))


def _widen(v, width):
    reps = width // v.shape[1]
    return v if reps == 1 else jnp.concatenate([v] * reps, axis=-1)


def _norm_fwd(h, gain, name):
    return _rows(lambda x, g: (_rms(x, g),), [h], [gain], [(h.shape[1], BF16)], name=name)[0]


def _norm_bwd(h, dhn, gain, dres, name):
    def fn(x, dy, dr, g):
        dx, dg = _rms_bwd(x, dy, g)
        return dr + dx, dr + dx, _colsum(dg)
    d = h.shape[1]
    return _rows(fn, [h, dhn, dres], [gain], [(d, F32), (d, BF16)], [((1, d), F32)], name=name)


def _ret_tables(T):
    inv = 1.0 / (ROPE_THETA ** (jnp.arange(0, RET_DK, 2, dtype=F32) / RET_DK))
    ang = jnp.arange(T, dtype=F32)[:, None] * inv[None, :]
    log_gamma = jnp.log(1.0 - 2.0 ** (-5.0 - jnp.arange(RET_HEADS, dtype=F32)))
    idx = jnp.arange(CHUNK, dtype=F32)
    intra = jnp.exp(log_gamma[:, None, None] * jnp.abs(idx[:, None] - idx[None, :]))
    qd = jnp.exp(log_gamma[:, None] * (idx + 1.0))[:, :, None]
    kd = jnp.exp(log_gamma[:, None] * (CHUNK - 1.0 - idx))[:, :, None]
    cd = jnp.exp(log_gamma * CHUNK)[:, None, None]
    return jnp.cos(ang), jnp.sin(ang), intra, qd, kd, cd


def _rope_half(x, c, s):
    x1, x2 = x[:, :RET_DK // 2], x[:, RET_DK // 2:]
    return jnp.concatenate([x1 * c - x2 * s, x2 * c + x1 * s], axis=-1)


def _rope_half_bwd(d, c, s):
    d1, d2 = d[:, :RET_DK // 2], d[:, RET_DK // 2:]
    return jnp.concatenate([d1 * c + d2 * s, d2 * c - d1 * s], axis=-1)


def _dot(a, b):
    return lax.dot_general(a, b, (((1,), (0,)), ((), ())), preferred_element_type=F32)


def _dot_nt(a, b):
    return lax.dot_general(a, b, (((1,), (1,)), ((), ())), preferred_element_type=F32)


def _dot_tn(a, b):
    return lax.dot_general(a, b, (((0,), (0,)), ((), ())), preferred_element_type=F32)


def _ret_specs(T, tb, rev):
    nj = T // tb
    jj = (lambda j: nj - 1 - j) if rev else (lambda j: j)
    g = RET_GROUP
    kq = RET_HEADS // g
    vq = 2 * RET_HEADS * RET_DK // (g * RET_DV)
    return dict(
        q=pl.BlockSpec((tb, g * RET_DK), lambda h, j: (jj(j), h)),
        k=pl.BlockSpec((tb, g * RET_DK), lambda h, j: (jj(j), kq + h)),
        v=pl.BlockSpec((tb, g * RET_DV), lambda h, j: (jj(j), vq + h)),
        tab=pl.BlockSpec((tb, RET_DK // 2), lambda h, j: (jj(j), 0)),
        intra=pl.BlockSpec((g, CHUNK, CHUNK), lambda h, j: (h, 0, 0)),
        dec=pl.BlockSpec((g, CHUNK, 1), lambda h, j: (h, 0, 0)),
        cd=pl.BlockSpec((g, 1, 1), lambda h, j: (h, 0, 0)),
        o=pl.BlockSpec((tb, g * RET_DV), lambda h, j: (jj(j), h)),
        s=pl.BlockSpec((g, tb // CHUNK, RET_DK, RET_DV), lambda h, j: (h, jj(j), 0, 0)),
    )


def _ret_fwd(proj, tabs, name):
    T = proj.shape[0]
    cos, sin, intra, qd, kd, cd = tabs
    tb = _pick(T, 512)
    cps = tb // CHUNK
    sp = _ret_specs(T, tb, False)
    scale = RET_DK ** -0.5

    def body(q_ref, k_ref, v_ref, cos_ref, sin_ref, intra_ref, qd_ref, kd_ref, cd_ref, o_ref, s_ref, state):
        @pl.when(pl.program_id(1) == 0)
        def _():
            state[...] = jnp.zeros_like(state)

        for c in range(cps):
            rows = pl.ds(c * CHUNK, CHUNK)
            co, si = cos_ref[rows, :], sin_ref[rows, :]
            for h in range(RET_GROUP):
                hk, hv = slice(h * RET_DK, (h + 1) * RET_DK), slice(h * RET_DV, (h + 1) * RET_DV)
                q = _rope_half(q_ref[rows, hk].astype(F32), co, si)
                k = _rope_half(k_ref[rows, hk].astype(F32), co, si) * scale
                vb = v_ref[rows, hv].astype(BF16)
                st = state[h]
                sb = st.astype(BF16)
                s_ref[h, c] = sb
                sc = _dot_nt(q.astype(BF16), k.astype(BF16)) * intra_ref[h]
                inner = _dot(sc.astype(BF16), vb)
                cross = _dot((q * qd_ref[h]).astype(BF16), sb)
                o_ref[rows, hv] = inner + cross
                state[h] = st * cd_ref[h] + _dot_tn((k * kd_ref[h]).astype(BF16), vb)

    return pl.pallas_call(
        body, name=name, grid=(RET_HEADS // RET_GROUP, T // tb),
        in_specs=[sp["q"], sp["k"], sp["v"], sp["tab"], sp["tab"], sp["intra"], sp["dec"], sp["dec"], sp["cd"]],
        out_specs=[sp["o"], sp["s"]],
        out_shape=[jax.ShapeDtypeStruct((T, RET_HEADS * RET_DV), F32),
                   jax.ShapeDtypeStruct((RET_HEADS, T // CHUNK, RET_DK, RET_DV), BF16)],
        scratch_shapes=[pltpu.VMEM((RET_GROUP, RET_DK, RET_DV), F32)],
        compiler_params=_cparams(("arbitrary", "arbitrary")),
    )(proj, proj, proj, cos, sin, intra, qd, kd, cd)


def _ret_bwd(proj, states, dout, tabs, name):
    T = proj.shape[0]
    cos, sin, intra, qd, kd, cd = tabs
    tb = _pick(T, 512)
    cps = tb // CHUNK
    sp = _ret_specs(T, tb, True)
    scale = RET_DK ** -0.5

    def body(q_ref, k_ref, v_ref, cos_ref, sin_ref, intra_ref, qd_ref, kd_ref, cd_ref, s_ref, do_ref,
             dq_ref, dk_ref, dv_ref, dstate):
        @pl.when(pl.program_id(1) == 0)
        def _():
            dstate[...] = jnp.zeros_like(dstate)

        for c in reversed(range(cps)):
            rows = pl.ds(c * CHUNK, CHUNK)
            co, si = cos_ref[rows, :], sin_ref[rows, :]
            for h in range(RET_GROUP):
                hk, hv = slice(h * RET_DK, (h + 1) * RET_DK), slice(h * RET_DV, (h + 1) * RET_DV)
                q = _rope_half(q_ref[rows, hk].astype(F32), co, si)
                k = _rope_half(k_ref[rows, hk].astype(F32), co, si) * scale
                qb, kb = q.astype(BF16), k.astype(BF16)
                vb = v_ref[rows, hv].astype(BF16)
                dob = do_ref[rows, hv].astype(BF16)
                sb = s_ref[h, c]
                ia = intra_ref[h]
                pb = (_dot_nt(qb, kb) * ia).astype(BF16)
                dsn = dstate[h]
                dsb = dsn.astype(BF16)
                kdk = (k * kd_ref[h]).astype(BF16)
                qdq = (q * qd_ref[h]).astype(BF16)
                dv = _dot_tn(pb, dob) + _dot(kdk, dsb)
                dpb = (_dot_nt(dob, vb) * ia).astype(BF16)
                dq = _dot(dpb, kb) + _dot_nt(dob, sb) * qd_ref[h]
                dk = _dot_tn(dpb, qb) + _dot_nt(vb, dsb) * kd_ref[h]
                dstate[h] = dsn * cd_ref[h] + _dot_tn(qdq, dob)
                dq_ref[rows, hk] = _rope_half_bwd(dq, co, si).astype(BF16)
                dk_ref[rows, hk] = _rope_half_bwd(dk * scale, co, si).astype(BF16)
                dv_ref[rows, hv] = dv.astype(BF16)

    return pl.pallas_call(
        body, name=name, grid=(RET_HEADS // RET_GROUP, T // tb),
        in_specs=[sp["q"], sp["k"], sp["v"], sp["tab"], sp["tab"], sp["intra"], sp["dec"], sp["dec"], sp["cd"],
                  sp["s"], sp["o"]],
        out_specs=[sp["q"], sp["q"], sp["o"]],
        out_shape=[jax.ShapeDtypeStruct((T, RET_HEADS * RET_DK), BF16),
                   jax.ShapeDtypeStruct((T, RET_HEADS * RET_DK), BF16),
                   jax.ShapeDtypeStruct((T, RET_HEADS * RET_DV), BF16)],
        scratch_shapes=[pltpu.VMEM((RET_GROUP, RET_DK, RET_DV), F32)],
        compiler_params=_cparams(("arbitrary", "arbitrary")),
    )(proj, proj, proj, cos, sin, intra, qd, kd, cd, states, dout)


def _ret_gate(out, proj, gn, name):
    def fn(o, g, *gains):
        g = g.astype(F32)
        parts = [_rms(o[:, h * RET_DV:(h + 1) * RET_DV], gains[h]) for h in range(RET_HEADS)]
        return (g * _sigmoid(g) * jnp.concatenate(parts, axis=-1),)
    w = RET_HEADS * RET_DV
    return _rows(fn, [out, (proj, w, 2)], [gn[h:h + 1] for h in range(RET_HEADS)], [(w, BF16)], name=name)[0]


def _ret_gate_bwd(out, proj, gn, dy, name):
    def fn(o, g, d, *gains):
        g = g.astype(F32)
        sg = _sigmoid(g)
        silu = g * sg
        dsilu = sg * (1.0 + g * (1.0 - sg))
        dos, dgs = [], []
        row = lax.broadcasted_iota(jnp.int32, (RET_HEADS, RET_DV), 0)
        dgn = jnp.zeros((RET_HEADS, RET_DV), F32)
        for h in range(RET_HEADS):
            sl = slice(h * RET_DV, (h + 1) * RET_DV)
            oh = o[:, sl]
            dgs.append(d[:, sl] * _rms(oh, gains[h]) * dsilu[:, sl])
            dx, dg = _rms_bwd(oh, d[:, sl] * silu[:, sl], gains[h])
            dos.append(dx)
            dgn = dgn + jnp.where(row == h, _colsum(dg), 0.0)
        return jnp.concatenate(dos, axis=-1), jnp.concatenate(dgs, axis=-1), dgn
    w = RET_HEADS * RET_DV
    return _rows(fn, [out, (proj, w, 2), dy], [gn[h:h + 1] for h in range(RET_HEADS)], [(w, BF16), (w, BF16)],
                 [((RET_HEADS, RET_DV), F32)], name=name, tile=128)


def _mla_tables(T):
    inv = 1.0 / (ROPE_THETA ** (jnp.arange(0, MLA_ROPE, 2, dtype=F32) / MLA_ROPE))
    ang = jnp.arange(T, dtype=F32)[:, None] * inv[None, :]
    c, s = jnp.cos(ang), jnp.sin(ang)
    z32, z64 = jnp.zeros((T, 32), F32), jnp.zeros((T, 64), F32)
    cos_t = jnp.concatenate([c, c, z64], axis=1)
    sin_a = jnp.concatenate([-s, z32, z64], axis=1)
    sin_b = jnp.concatenate([z32, s, z64], axis=1)
    return cos_t, sin_a, sin_b


def _rope_blk(x, ct, sa, sb):
    return x * ct + pltpu.roll(x, 96, 1) * sa + pltpu.roll(x, 32, 1) * sb


def _rope_blk_bwd(d, ct, sa, sb):
    return d * ct + pltpu.roll(d * sa, 32, 1) + pltpu.roll(d * sb, 96, 1)


def _head_norm(x, gain):
    r = lax.rsqrt(jnp.sum(x * x, axis=-1, keepdims=True) / MLA_QKD + EPS)
    return (x * r) * gain


def _mla_prep(q, kv, proj, gq, gk, tabs, name):
    def fn(qv, kvv, kr, ct, sa, sb, gqv, gkv):
        qs, ks, vs = [], [], []
        for h in range(MLA_HEADS):
            b = h * MLA_HP
            y = _head_norm(qv[:, b:b + MLA_HP], gqv)
            qs += [y[:, :128], _rope_blk(y[:, 128:], ct, sa, sb)]
            y = _head_norm(jnp.concatenate([kvv[:, b:b + 128], kr], axis=-1), gkv)
            ks += [y[:, :128], _rope_blk(y[:, 128:], ct, sa, sb)]
            vs.append(kvv[:, b + 128:b + 256])
        return jnp.concatenate(qs, axis=-1), jnp.concatenate(ks, axis=-1), jnp.concatenate(vs, axis=-1)
    w = MLA_HEADS * MLA_HP
    return _rows(fn, [q, kv, (proj, 128, 5), *tabs], [gq, gk],
                 [(w, BF16), (w, BF16), (MLA_HEADS * MLA_VD, BF16)], name=name, tile=128)


def _mla_prep_bwd(q, kv, proj, gq, gk, tabs, dqf, dkf, dvf, name):
    def fn(qv, kvv, kr, ct, sa, sb, dqv, dkv, dvv, gqv, gkv):
        dqs, dkvs = [], []
        dkr = jnp.zeros_like(kr)
        dgq = jnp.zeros((1, MLA_HP), F32)
        dgk = jnp.zeros((1, MLA_HP), F32)
        for h in range(MLA_HEADS):
            b = h * MLA_HP
            dy = jnp.concatenate([dqv[:, b:b + 128], _rope_blk_bwd(dqv[:, b + 128:b + 256], ct, sa, sb)], axis=-1)
            dx, dg = _rms_bwd(qv[:, b:b + MLA_HP], dy, gqv, MLA_QKD)
            dqs.append(dx)
            dgq = dgq + _colsum(dg)
            dy = jnp.concatenate([dkv[:, b:b + 128], _rope_blk_bwd(dkv[:, b + 128:b + 256], ct, sa, sb)], axis=-1)
            dx, dg = _rms_bwd(jnp.concatenate([kvv[:, b:b + 128], kr], axis=-1), dy, gkv, MLA_QKD)
            dkvs += [dx[:, :128], dvv[:, h * MLA_VD:(h + 1) * MLA_VD]]
            dkr = dkr + dx[:, 128:]
            dgk = dgk + _colsum(dg)
        return jnp.concatenate(dqs, axis=-1), jnp.concatenate(dkvs, axis=-1), dkr, dgq, dgk
    w = MLA_HEADS * MLA_HP
    return _rows(fn, [q, kv, (proj, 128, 5), *tabs, dqf, dkf, dvf], [gq, gk],
                 [(w, BF16), (w, BF16), (128, F32)], [((1, MLA_HP), F32), ((1, MLA_HP), F32)], name=name, tile=128)


def _chunk_mask(qi, ki, tq, tk):
    shift = CHUNK.bit_length() - 1
    rq = lax.shift_right_arithmetic(qi * tq + lax.broadcasted_iota(jnp.int32, (tq, tk), 0), shift)
    ck = lax.shift_right_arithmetic(ki * tk + lax.broadcasted_iota(jnp.int32, (tq, tk), 1), shift)
    return ck <= rq


def _flash_fwd(qf, kf, vf, name):
    T = qf.shape[0]
    t = _pick(T, FLASH_T)
    n = T // t
    scale = MLA_QKD ** -0.5

    g = FLASH_HEADS

    def body(q_ref, k_ref, v_ref, o_ref, lse_ref, m_s, l_s, acc):
        qi = pl.program_id(1)
        m_s[...] = jnp.full_like(m_s, NEG)
        l_s[...] = jnp.zeros_like(l_s)
        acc[...] = jnp.zeros_like(acc)

        def step(kb, masked):
            rows = pl.ds(pl.multiple_of(kb * t, t), t)
            for h in range(g):
                hq, hv = slice(h * MLA_HP, (h + 1) * MLA_HP), slice(h * MLA_VD, (h + 1) * MLA_VD)
                s = _dot_nt(q_ref[:, hq], k_ref[rows, hq]) * scale
                if masked:
                    s = jnp.where(_chunk_mask(0, 0, t, t), s, NEG)
                m_prev = m_s[:, hv]
                m_new = jnp.maximum(m_prev, jnp.max(s, axis=-1, keepdims=True))
                alpha = jnp.exp(m_prev - m_new)
                p = jnp.exp(s - _widen(m_new, t))
                l_s[:, hv] = alpha * l_s[:, hv] + sum(p[:, i * 128:(i + 1) * 128] for i in range(t // 128))
                acc[:, hv] = acc[:, hv] * alpha + _dot(p.astype(BF16), v_ref[rows, hv])
                m_s[:, hv] = m_new

        @pl.loop(0, qi)
        def _(kb):
            step(kb, False)

        step(qi, True)
        for h in range(g):
            hv = slice(h * MLA_VD, (h + 1) * MLA_VD)
            l = jnp.sum(l_s[:, hv], axis=-1, keepdims=True)
            o_ref[:, hv] = acc[:, hv] / l
            lse_ref[:, hv] = m_s[:, hv] + jnp.log(l)

    qmap = lambda h, i: (i, h)
    kmap = lambda h, i: (0, h)
    vec = pltpu.VMEM((t, g * MLA_VD), F32)
    return pl.pallas_call(
        body, name=name, grid=(MLA_HEADS // g, n),
        in_specs=[pl.BlockSpec((t, g * MLA_HP), qmap), pl.BlockSpec((T, g * MLA_HP), kmap),
                  pl.BlockSpec((T, g * MLA_VD), kmap)],
        out_specs=[pl.BlockSpec((t, g * MLA_VD), qmap), pl.BlockSpec((t, g * MLA_VD), qmap)],
        out_shape=[jax.ShapeDtypeStruct((T, MLA_HEADS * MLA_VD), F32),
                   jax.ShapeDtypeStruct((T, MLA_HEADS * MLA_VD), F32)],
        scratch_shapes=[vec, vec, vec],
        compiler_params=_cparams(("parallel", "arbitrary")),
    )(qf, kf, vf)


def _flash_delta(o, do, name):
    def fn(ov, dv):
        parts = []
        for h in range(MLA_HEADS):
            sl = slice(h * MLA_VD, (h + 1) * MLA_VD)
            d = jnp.sum(dv[:, sl] * ov[:, sl], axis=-1, keepdims=True)
            parts.append(jnp.broadcast_to(d, (d.shape[0], MLA_VD)))
        return jnp.concatenate(parts, axis=-1), dv
    w = MLA_HEADS * MLA_VD
    return _rows(fn, [o, do], [], [(w, F32), (w, BF16)], name=name)


def _flash_bwd(qf, kf, vf, do16, lse, delta, name):
    T = qf.shape[0]
    t = _pick(T, FLASH_T)
    n = T // t
    scale = MLA_QKD ** -0.5

    def body(q_ref, k_ref, v_ref, do_ref, lse_ref, dl_ref, dq_ref, dk_ref, dv_ref):
        kb = pl.program_id(1)

        @pl.when(kb == 0)
        def _():
            dq_ref[...] = jnp.zeros_like(dq_ref)

        dk_ref[...] = jnp.zeros_like(dk_ref)
        dv_ref[...] = jnp.zeros_like(dv_ref)
        k, v = k_ref[...], v_ref[...]

        def step(qb, masked):
            rows = pl.ds(pl.multiple_of(qb * t, t), t)
            q, dob = q_ref[rows, :], do_ref[rows, :]
            s = _dot_nt(q, k) * scale
            if masked:
                s = jnp.where(_chunk_mask(0, 0, t, t), s, NEG)
            p = jnp.exp(s - _widen(lse_ref[rows, :], t))
            ds = (p * (_dot_nt(dob, v) - _widen(dl_ref[rows, :], t)) * scale).astype(BF16)
            dv_ref[...] += _dot_tn(p.astype(BF16), dob)
            dk_ref[...] += _dot_tn(ds, q)
            dq_ref[rows, :] += _dot(ds, k)

        step(kb, True)

        @pl.loop(kb + 1, n)
        def _(qb):
            step(qb, False)

    qmap = lambda h, j: (0, h)
    kmap = lambda h, j: (j, h)
    return pl.pallas_call(
        body, name=name, grid=(MLA_HEADS, n),
        in_specs=[pl.BlockSpec((T, MLA_HP), qmap), pl.BlockSpec((t, MLA_HP), kmap), pl.BlockSpec((t, MLA_VD), kmap),
                  pl.BlockSpec((T, MLA_VD), qmap), pl.BlockSpec((T, MLA_VD), qmap), pl.BlockSpec((T, MLA_VD), qmap)],
        out_specs=[pl.BlockSpec((T, MLA_HP), qmap), pl.BlockSpec((t, MLA_HP), kmap), pl.BlockSpec((t, MLA_VD), kmap)],
        out_shape=[jax.ShapeDtypeStruct((T, MLA_HEADS * MLA_HP), F32),
                   jax.ShapeDtypeStruct((T, MLA_HEADS * MLA_HP), F32),
                   jax.ShapeDtypeStruct((T, MLA_HEADS * MLA_VD), F32)],
        compiler_params=_cparams(("arbitrary", "arbitrary")),
    )(qf, kf, vf, do16, lse, delta)


MESH = pl.DeviceIdType.MESH
ANY = pl.BlockSpec(memory_space=pl.ANY)
_CHIP_FLIPS = ((1, 0), (0, 1), (1, 1))


def _place():
    return lax.axis_index("x"), lax.axis_index("y"), lax.axis_index("c")


def _other_chip(x, y, k):
    fx, fy = _CHIP_FLIPS[k]
    return ((1 - x) if fx else x), ((1 - y) if fy else y)


def _remote(src, dst, send_sems, recv_sems, k, to):
    return pltpu.make_async_remote_copy(src_ref=src, dst_ref=dst, send_sem=send_sems.at[k], recv_sem=recv_sems.at[k],
                                        device_id=to, device_id_type=MESH)


def _index(*vals):
    return jnp.stack(vals).astype(jnp.int32)


def _half(c, rows):
    return pl.ds(pl.multiple_of(c * rows, 16), rows)


def _gather_weights(parts, name, landed=None):
    n_w = len(parts)
    n_in = n_w if landed is None else 2 * n_w

    def body(*refs):
        ins, outs = refs[:n_w], refs[n_in:n_in + n_w]
        send_sems, recv_sems, local_sems = refs[n_in + n_w:]
        x, y, c = _place()
        j = 2 * x + y
        sibling = (x, y, 1 - c)
        chips = [_other_chip(x, y, k) for k in range(3)]
        pending = []
        for w in range(n_w):
            own = pltpu.make_async_copy(ins[w], outs[w].at[j], local_sems.at[w])
            own.start()
            pending.append(own)
        sent = []
        for w in range(n_w):
            if landed is not None:
                break
            r = _half(c, parts[w].shape[0] // 2)
            for k, (px, py) in enumerate(chips):
                cp = _remote(ins[w].at[r], outs[w].at[j, r], send_sems, recv_sems, 6 * w + k, (px, py, c))
                cp.start()
                sent.append(cp)
        for w in range(n_w):
            r = _half(c, parts[w].shape[0] // 2)
            for k, (px, py) in enumerate(chips):
                blk = outs[w].at[2 * px + py, r]
                if landed is None:
                    _remote(blk, blk, send_sems, recv_sems, 6 * w + k, (px, py, c)).wait_recv()
                cp = _remote(blk, blk, send_sems, recv_sems, 6 * w + 3 + k, sibling)
                cp.start()
                sent.append(cp)
        for w in range(n_w):
            r = _half(1 - c, parts[w].shape[0] // 2)
            for k, (px, py) in enumerate(chips):
                blk = outs[w].at[2 * px + py, r]
                _remote(blk, blk, send_sems, recv_sems, 6 * w + 3 + k, sibling).wait_recv()
        for cp in sent:
            cp.wait_send()
        for cp in pending:
            cp.wait()

    return pl.pallas_call(
        body, name=name, in_specs=[pl.BlockSpec(memory_space=pltpu.VMEM)] * n_w + [ANY] * (n_in - n_w),
        out_specs=[ANY] * n_w,
        out_shape=[jax.ShapeDtypeStruct((N_CHIPS, *p.shape), p.dtype) for p in parts],
        input_output_aliases={} if landed is None else {n_w + w: w for w in range(n_w)},
        scratch_shapes=[pltpu.SemaphoreType.DMA((6 * n_w,)), pltpu.SemaphoreType.DMA((6 * n_w,)),
                        pltpu.SemaphoreType.DMA((n_w,))],
        compiler_params=pltpu.CompilerParams(vmem_limit_bytes=VMEM_LIMIT),
    )(*parts, *(landed or []))


def _swap_halves(gs, name):
    n_w = len(gs)

    def body(*refs):
        g_refs, recv_refs = refs[:n_w], refs[n_w:2 * n_w]
        send_sems, recv_sems = refs[2 * n_w:]
        x, y, c = _place()
        sent = []
        for w in range(n_w):
            for jj in range(N_CHIPS):
                cp = _remote(g_refs[w].at[jj, 1 - c], recv_refs[w].at[jj], send_sems, recv_sems, N_CHIPS * w + jj,
                             (x, y, 1 - c))
                cp.start()
                sent.append(cp)
        for cp in sent:
            cp.wait()

    return pl.pallas_call(
        body, name=name, in_specs=[ANY] * n_w, out_specs=[ANY] * n_w,
        out_shape=[jax.ShapeDtypeStruct((N_CHIPS, *g.shape[2:]), g.dtype) for g in gs],
        scratch_shapes=[pltpu.SemaphoreType.DMA((N_CHIPS * n_w,)), pltpu.SemaphoreType.DMA((N_CHIPS * n_w,))],
    )(*gs)


def _pair_sum(g, recv, core, name):
    _, H, C = recv.shape
    tile = _pick(H, 256)

    def body(c_ref, own_ref, recv_ref, out_ref):
        out_ref[...] = (own_ref[...].astype(F32) + recv_ref[...].astype(F32)).astype(BF16)

    blk = pl.BlockSpec((None, tile, C), lambda jj, i, c: (jj, i, 0))
    return pl.pallas_call(
        body, name=name,
        grid_spec=pltpu.PrefetchScalarGridSpec(
            num_scalar_prefetch=1, grid=(N_CHIPS, H // tile),
            in_specs=[pl.BlockSpec((None, None, tile, C), lambda jj, i, c: (jj, c[0], i, 0)), blk],
            out_specs=blk),
        out_shape=jax.ShapeDtypeStruct((N_CHIPS, H, C), BF16),
        compiler_params=_cparams(("arbitrary", "arbitrary")),
    )(_index(core), g, recv)


def _chip_sum(g, recv, got, chip, core, name):
    _, H, C = recv.shape
    tile = _pick(H, 256)

    def body(s_ref, own_ref, recv_ref, g0_ref, g1_ref, g2_ref, out_ref):
        pair = own_ref[...].astype(F32) + recv_ref[...].astype(F32)
        out_ref[...] = ((pair + g0_ref[...].astype(F32)) + g1_ref[...].astype(F32)) + g2_ref[...].astype(F32)

    def got_spec(k):
        return pl.BlockSpec((None, tile, C), lambda i, s, k=k: (k, i, 0))

    return pl.pallas_call(
        body, name=name,
        grid_spec=pltpu.PrefetchScalarGridSpec(
            num_scalar_prefetch=1, grid=(H // tile,),
            in_specs=[pl.BlockSpec((None, None, tile, C), lambda i, s: (s[0], s[1], i, 0)),
                      pl.BlockSpec((None, tile, C), lambda i, s: (s[0], i, 0)), got_spec(0), got_spec(1), got_spec(2)],
            out_specs=pl.BlockSpec((None, tile, C), lambda i, s: (s[1], i, 0))),
        out_shape=jax.ShapeDtypeStruct((2, H, C), F32),
        compiler_params=_cparams(("arbitrary",)),
    )(_index(chip, core), g, recv, got, got, got)


def _scatter_chips(sums, name):
    n_w = len(sums)

    def body(*refs):
        a_refs, got_refs = refs[:n_w], refs[n_w:2 * n_w]
        send_sems, recv_sems = refs[2 * n_w:]
        x, y, c = _place()
        j = 2 * x + y
        sent = []
        for w in range(n_w):
            for k in range(3):
                px, py = _other_chip(x, y, k)
                pj = 2 * px + py
                cp = _remote(a_refs[w].at[pj], got_refs[w].at[(j - pj + 4) % 4 - 1], send_sems, recv_sems, 3 * w + k,
                             (px, py, c))
                cp.start()
                sent.append(cp)
        for w in range(n_w):
            for k in range(3):
                px, py = _other_chip(x, y, k)
                slot = got_refs[w].at[(2 * px + py - j + 4) % 4 - 1]
                _remote(slot, slot, send_sems, recv_sems, 3 * w + k, (px, py, c)).wait_recv()
        for cp in sent:
            cp.wait_send()

    return pl.pallas_call(
        body, name=name, in_specs=[ANY] * n_w, out_specs=[ANY] * n_w,
        out_shape=[jax.ShapeDtypeStruct((3, *a.shape[1:]), a.dtype) for a in sums],
        scratch_shapes=[pltpu.SemaphoreType.DMA((3 * n_w,)), pltpu.SemaphoreType.DMA((3 * n_w,))],
    )(*sums)


def _share_halves(reds):
    n_w = len(reds)

    def body(*refs):
        out_refs = refs[n_w:2 * n_w]
        send_sems, recv_sems = refs[2 * n_w:]
        x, y, c = _place()
        sent = []
        for w in range(n_w):
            blk = out_refs[w].at[c]
            cp = _remote(blk, blk, send_sems, recv_sems, w, (x, y, 1 - c))
            cp.start()
            sent.append(cp)
        for cp in sent:
            cp.wait()

    return pl.pallas_call(
        body, name="grad_share_halves", in_specs=[ANY] * n_w, out_specs=[ANY] * n_w,
        out_shape=[jax.ShapeDtypeStruct(r.shape, r.dtype) for r in reds],
        input_output_aliases={w: w for w in range(n_w)},
        scratch_shapes=[pltpu.SemaphoreType.DMA((n_w,)), pltpu.SemaphoreType.DMA((n_w,))],
    )(*reds)


def _allsum_small(v, name):
    R, W = v.shape
    n_dev = 8
    vm = pl.BlockSpec(memory_space=pltpu.VMEM)

    def body(v_ref, out_ref, buf, send_sems, recv_sems):
        x, y, c = _place()
        me = 4 * x + 2 * y + c
        buf[me] = v_ref[...]
        sent = []
        for k in range(1, n_dev):
            peer = ((1 - x) if k & 4 else x, (1 - y) if k & 2 else y, (1 - c) if k & 1 else c)
            cp = _remote(v_ref, buf.at[me], send_sems, recv_sems, k - 1, peer)
            cp.start()
            sent.append(cp)
        for cp in sent:
            cp.wait_recv()
        for cp in sent:
            cp.wait_send()
        acc = buf[0]
        for q in range(1, n_dev):
            acc = acc + buf[q]
        out_ref[...] = acc

    return pl.pallas_call(
        body, name=name, in_specs=[vm], out_specs=vm, out_shape=jax.ShapeDtypeStruct((R, W), v.dtype),
        scratch_shapes=[pltpu.VMEM((n_dev, R, W), v.dtype), pltpu.SemaphoreType.DMA((n_dev - 1,)),
                        pltpu.SemaphoreType.DMA((n_dev - 1,))],
    )(v)


HBM = pl.BlockSpec(memory_space=pltpu.HBM)
SEM = pl.BlockSpec(memory_space=pltpu.SEMAPHORE)
_DATAFLOW = pltpu.SideEffectType.DATAFLOW_SIDE_EFFECTING


def _split_start(name, srcs, land_shapes, n_copies, copies, after=()):
    ns, nl = len(srcs), len(land_shapes)
    lands = [lax.empty(s.shape, s.dtype) for s in land_shapes]

    def body(*refs):
        outs = refs[ns + nl + len(after):]
        for cp in copies(refs[:ns], refs[ns:ns + nl], outs[0], outs[1]):
            cp.start()
        outs[-1][...] = jnp.zeros_like(outs[-1])

    sems = pltpu.SemaphoreType.DMA((n_copies,))
    res = pl.pallas_call(
        body, name=name, in_specs=[HBM] * (ns + nl) + [ANY] * len(after),
        out_specs=(SEM, SEM, *[HBM] * (ns + nl), pl.BlockSpec(memory_space=pltpu.VMEM)),
        out_shape=(sems, sems, *[pltpu.HBM(a.shape, a.dtype) for a in srcs],
                   *[pltpu.HBM(s.shape, s.dtype) for s in land_shapes], jax.ShapeDtypeStruct((8, 128), F32)),
        input_output_aliases={i: 2 + i for i in range(ns + nl)},
        compiler_params=pltpu.CompilerParams(has_side_effects=_DATAFLOW),
    )(*[pltpu.with_memory_space_constraint(a, pltpu.HBM) for a in [*srcs, *lands]], *after)
    return res[0], res[1], list(res[2:2 + ns]), list(res[2 + ns:2 + ns + nl]), res[-1]


def _split_wait(name, send_sems, recv_sems, srcs, lands, copies, after=()):
    ns, nl = len(srcs), len(lands)

    def body(*refs):
        for cp in copies(refs[:ns], refs[ns:ns + nl], refs[ns + nl], refs[ns + nl + 1]):
            cp.wait_send()
            cp.wait_recv()

    res = pl.pallas_call(
        body, name=name, in_specs=[HBM] * (ns + nl) + [SEM, SEM] + [ANY] * len(after), out_specs=[HBM] * (ns + nl),
        out_shape=[pltpu.HBM(a.shape, a.dtype) for a in [*srcs, *lands]],
        input_output_aliases={i: i for i in range(ns + nl)},
        compiler_params=pltpu.CompilerParams(has_side_effects=_DATAFLOW),
    )(*srcs, *lands, send_sems, recv_sems, *after)
    return list(res[ns:])


def _gather_copies(rows):
    def copies(src_refs, land_refs, send_sems, recv_sems):
        x, y, c = _place()
        j = 2 * x + y
        out = []
        for w in range(len(src_refs)):
            r = _half(c, rows[w] // 2)
            for k in range(3):
                px, py = _other_chip(x, y, k)
                out.append(_remote(src_refs[w].at[r], land_refs[w].at[j, r], send_sems, recv_sems, 3 * w + k, (px, py, c)))
        return out
    return copies


def _scatter_copies(src_refs, land_refs, send_sems, recv_sems):
    x, y, c = _place()
    j = 2 * x + y
    out = []
    for w in range(len(src_refs)):
        for k in range(3):
            px, py = _other_chip(x, y, k)
            pj = 2 * px + py
            out.append(_remote(src_refs[w].at[pj], land_refs[w].at[(j - pj + 4) % 4 - 1], send_sems, recv_sems, 3 * w + k,
                               (px, py, c)))
    return out


def _reduce_begin(grads, core, tag):
    names = list(grads)
    gs = [grads[k].reshape(N_CHIPS, 2, -1, grads[k].shape[-1]) for k in names]
    recvs = _swap_halves(gs, f"grad_swap_halves_{tag}")
    sums = [_pair_sum(g, r, core, f"pair_sum_{k}") for k, g, r in zip(names, gs, recvs)]
    return names, gs, recvs, sums


def _reduce_end(begun, gots, chip, core):
    names, gs, recvs, _ = begun
    return {k: _chip_sum(g, r, t, chip, core, f"chip_sum_{k}") for k, g, r, t in zip(names, gs, recvs, gots)}


def _got_shapes(sums):
    return [jax.ShapeDtypeStruct((3, *a.shape[1:]), a.dtype) for a in sums]


def _adamw(w, g, m, v, name, layers=1, layer=0, into=None):
    shape = w.shape
    cols = shape[-1]
    w3, m3, v3 = (t.reshape(layers, -1, cols) for t in (w, m, v))
    rows = w3.shape[1]
    tile = _pick(rows, 256) if rows % 8 == 0 else rows
    n_in = 4 + (0 if into is None else 4)

    def body(*refs):
        wv, gv, mv, vv = (r[...] for r in refs[:4])
        g_ref, d_ref, m_ref, v_ref = refs[n_in:]
        m2 = ADAM_B1 * mv + (1.0 - ADAM_B1) * gv
        v2 = ADAM_B2 * vv + (1.0 - ADAM_B2) * jnp.square(gv)
        m_hat = m2 / (1.0 - ADAM_B1 ** ADAM_STEP)
        v_hat = v2 / (1.0 - ADAM_B2 ** ADAM_STEP)
        g_ref[...] = gv
        d_ref[...] = -ADAM_LR * (m_hat / (jnp.sqrt(v_hat) + ADAM_EPS) + ADAM_WD * wv)
        m_ref[...] = m2
        v_ref[...] = v2

    lay = pl.BlockSpec((None, tile, cols), lambda i: (layer, i, 0))
    out = jax.ShapeDtypeStruct((layers, rows, cols), F32)
    res = pl.pallas_call(
        body, name=name, grid=(rows // tile,),
        in_specs=[lay, pl.BlockSpec((tile, cols), lambda i: (i, 0)), lay, lay] + [ANY] * (n_in - 4),
        out_specs=[lay] * 4, out_shape=[out] * 4,
        input_output_aliases={} if into is None else {4 + k: k for k in range(4)},
        compiler_params=_cparams(("arbitrary",)),
    )(w3, g.reshape(rows, cols), m3, v3, *([] if into is None else [t.reshape(layers, rows, cols) for t in into]))
    return tuple(t.reshape(shape) for t in res)


def _add_res(acc, r):
    return (r + acc,)


def _tail_fwd(h1, p16, W, i, tag):
    hn2 = _norm_fwd(h1, W["mlp_norm"][i:i + 1], f"{tag}_mlp_norm")
    a = _mm(hn2, W["mlp_w1"][i], bblk=True, outs=[BF16], name=f"{tag}_mlp_w1",
            epilogue=lambda acc: (jnp.square(jnp.maximum(acc, 0.0)),))
    h2 = _mm(a, W["mlp_w2"][i], extras=[h1], epilogue=_add_res, name=f"{tag}_mlp_w2")
    hn3 = _norm_fwd(h2, W["ple_norm"][i:i + 1], f"{tag}_ple_norm")
    gl = _mm(hn3, W["ple_gate_w"][i], name=f"{tag}_ple_gate")
    h3, pp = _mm(p16[i], W["ple_proj_w"][i], bblk=True, extras=[gl, h2], outs=[F32, F32], name=f"{tag}_ple_proj",
                 epilogue=lambda acc, g, h: (h + _sigmoid(g) * acc, acc))
    return h3, (h1, hn2, a, h2, hn3, gl, pp)


def _tail_bwd(dh3, saved, p16, W, i, tag, after=()):
    h1, hn2, a, h2, hn3, gl, pp = saved

    def gate_bwd(d, g, ppv):
        gate = _sigmoid(g)
        return d * gate, d * ppv * gate * (1.0 - gate)

    def dw(kind, name):
        return (kind, 1, 0, None)

    dpp, dgl = _rows(gate_bwd, [dh3, gl, pp], [], [(D_MODEL, BF16), (D_MODEL, BF16)], name=f"{tag}_ple_gate_bwd",
                     after=after)
    d_proj = _mm(p16[i], dpp, ta=True, outs=[BF16], dw=dw("cols", "ple_proj_w"), name=f"{tag}_d_ple_proj")
    d_gate = _mm(hn3, dgl, ta=True, outs=[BF16], dw=dw("rows", "ple_gate_w"), name=f"{tag}_d_ple_gate")
    dhn3 = _mm(dgl, W["ple_gate_w"][i], tb=True, name=f"{tag}_ple_gate_dx")
    dh2, dh2_16, d_ple_norm = _norm_bwd(h2, dhn3, W["ple_norm"][i:i + 1], dh3, f"{tag}_ple_norm_bwd")
    d_w2 = _mm(a, dh2_16, ta=True, outs=[BF16], dw=dw("rows", "mlp_w2"), name=f"{tag}_d_mlp_w2")
    dz = _mm(dh2_16, W["mlp_w2"][i], tb=True, extras=[a], outs=[BF16], name=f"{tag}_mlp_w2_dx",
             epilogue=lambda acc, av: (acc * (2.0 * jnp.sqrt(av.astype(F32))),))
    d_w1 = _mm(hn2, dz, ta=True, outs=[BF16], dw=dw("cols", "mlp_w1"), name=f"{tag}_d_mlp_w1")
    dhn2 = _mm(dz, W["mlp_w1"][i], tb=True, bblk=True, name=f"{tag}_mlp_w1_dx")
    dh1, dh1_16, d_mlp_norm = _norm_bwd(h1, dhn2, W["mlp_norm"][i:i + 1], dh2, f"{tag}_mlp_norm_bwd")
    big = {f"mlp_w1_{i}": d_w1, f"mlp_w2_{i}": d_w2, f"ple_gate_w_{i}": d_gate, f"ple_proj_w_{i}": d_proj}
    return dh1, dh1_16, big, dict(mlp_norm=d_mlp_norm, ple_norm=d_ple_norm)


def _ret_layer_fwd(h0, W, tabs, after=()):
    hn = _rows(lambda x, g: (_rms(x, g),), [h0], [W["mix_norm"][0:1]], [(D_MODEL, BF16)], name="ret_mix_norm",
               after=after)[0]
    proj = _mm(hn, W["ret_w_in"], bblk=True, outs=[BF16], name="ret_w_in")
    out, states = _ret_fwd(proj, tabs, "ret_scan")
    y = _ret_gate(out, proj, W["ret_gn"], "ret_gate")
    h1 = _mm(y, W["ret_w_out"], extras=[h0], epilogue=_add_res, name="ret_w_out")
    return h1, (h0, hn, proj, out, states, y)


def _ret_layer_bwd(dh1, dh1_16, saved, W, tabs, after=(), on_grads=None):
    h0, hn, proj, out, states, y = saved
    d_w_out = _mm(y, dh1_16, ta=True, outs=[BF16], dw=("rows", 1, 0, None), name="d_ret_w_out", after=after)
    dy = _mm(dh1_16, W["ret_w_out"], tb=True, name="ret_w_out_dx", after=after)
    dout, dg, d_gn = _ret_gate_bwd(out, proj, W["ret_gn"], dy, "ret_gate_bwd")
    dq, dk, dv = _ret_bwd(proj, states, dout, tabs, "ret_scan_bwd")
    dproj = jnp.concatenate([dq, dk, dv, dg], axis=1)
    d_w_in = _mm(hn, dproj, ta=True, outs=[BF16], dw=("cols", 1, 0, None), name="d_ret_w_in")
    big = dict(ret_w_in=d_w_in, ret_w_out=d_w_out)
    later = () if on_grads is None else on_grads(big)
    dhn = _mm(dproj, W["ret_w_in"], tb=True, bblk=True, tn=256, name="ret_w_in_dx", after=later)
    dh0, _, d_mix = _norm_bwd(h0, dhn, W["mix_norm"][0:1], dh1, "ret_mix_norm_bwd")
    return dh0, big, dict(mix_norm=d_mix, ret_gn=d_gn)


def _mla_layer_fwd(h0, W, tabs):
    hn = _norm_fwd(h0, W["mix_norm"][1:2], "mla_mix_norm")
    proj = _mm(hn, W["mla_w_in"], name="mla_w_in")

    def low_rank_norm(pv, gq, gkv):
        return _rms(pv[:, :MLA_Q_RANK], gq), _rms(pv[:, MLA_Q_RANK:MLA_Q_RANK + MLA_KV_RANK], gkv)

    cqn, ckvn = _rows(low_rank_norm, [proj], [W["mla_q_a_norm"], W["mla_kv_a_norm"]],
                      [(MLA_Q_RANK, BF16), (MLA_KV_RANK, BF16)], name="mla_low_rank_norm")
    q = _mm(cqn, W["mla_w_uq"], bblk=True, name="mla_w_uq")
    kv = _mm(ckvn, W["mla_w_ukv"], bblk=True, name="mla_w_ukv")
    qf, kf, vf = _mla_prep(q, kv, proj, W["mla_q_norm"], W["mla_k_norm"], tabs, "mla_prep")
    o, lse = _flash_fwd(qf, kf, vf, "mla_flash")
    h1 = _mm(o, W["mla_w_out"], extras=[h0], epilogue=_add_res, name="mla_w_out")
    return h1, (h0, hn, proj, cqn, ckvn, q, kv, qf, kf, vf, o, lse)


def _mla_layer_bwd(dh1, dh1_16, saved, W, tabs):
    h0, hn, proj, cqn, ckvn, q, kv, qf, kf, vf, o, lse = saved
    d_w_out = _mm(o, dh1_16, ta=True, outs=[BF16], dw=("rows", 1, 0, None), name="d_mla_w_out")
    do = _mm(dh1_16, W["mla_w_out"], tb=True, name="mla_w_out_dx")
    delta, do16 = _flash_delta(o, do, "mla_flash_delta")
    dqf, dkf, dvf = _flash_bwd(qf, kf, vf, do16, lse, delta, "mla_flash_bwd")
    dq, dkv, dkr, d_gq, d_gk = _mla_prep_bwd(q, kv, proj, W["mla_q_norm"], W["mla_k_norm"], tabs, dqf, dkf, dvf,
                                             "mla_prep_bwd")
    d_w_uq = _mm(cqn, dq, ta=True, outs=[BF16], dw=("cols", 1, 0, None), name="d_mla_w_uq")
    dcqn = _mm(dq, W["mla_w_uq"], tb=True, bblk=True, name="mla_w_uq_dx")
    d_w_ukv = _mm(ckvn, dkv, ta=True, outs=[BF16], dw=("cols", 1, 0, None), name="d_mla_w_ukv")
    dckvn = _mm(dkv, W["mla_w_ukv"], tb=True, bblk=True, name="mla_w_ukv_dx")

    def low_rank_bwd(pv, dcq, dckv, dkr_v, gq, gkv):
        dxq, dgq = _rms_bwd(pv[:, :MLA_Q_RANK], dcq, gq)
        dxkv, dgkv = _rms_bwd(pv[:, MLA_Q_RANK:MLA_Q_RANK + MLA_KV_RANK], dckv, gkv)
        return jnp.concatenate([dxq, dxkv, dkr_v], axis=-1), _colsum(dgq), _colsum(dgkv)

    dproj, d_gqa, d_gkva = _rows(low_rank_bwd, [proj, dcqn, dckvn, dkr], [W["mla_q_a_norm"], W["mla_kv_a_norm"]],
                                 [(MLA_IN_PAD, BF16)], [((1, MLA_Q_RANK), F32), ((1, MLA_KV_RANK), F32)],
                                 name="mla_low_rank_norm_bwd")
    d_w_in = _mm(hn, dproj, ta=True, outs=[BF16], dw=("rows", 1, 0, None), name="d_mla_w_in")
    dhn = _mm(dproj, W["mla_w_in"], tb=True, name="mla_w_in_dx")
    dh0, dh0_16, d_mix = _norm_bwd(h0, dhn, W["mix_norm"][1:2], dh1, "mla_mix_norm_bwd")
    return (dh0, dh0_16, dict(mla_w_in=d_w_in, mla_w_uq=d_w_uq, mla_w_ukv=d_w_ukv, mla_w_out=d_w_out),
            dict(mix_norm=d_mix, mla_q_a_norm=d_gqa, mla_kv_a_norm=d_gkva, mla_q_norm=d_gq, mla_k_norm=d_gk))


def _local_step(x, p16, target, W):
    T = x.shape[0]
    ret_tabs, mla_tabs = _ret_tables(T), _mla_tables(T)
    h1, s_ret = _ret_layer_fwd(x, W, ret_tabs)
    h3, s_tail0 = _tail_fwd(h1, p16, W, 0, "l0")
    h4, s_mla = _mla_layer_fwd(h3, W, mla_tabs)
    y, s_tail1 = _tail_fwd(h4, p16, W, 1, "l1")
    dy, loss = _loss_head(y, target)
    dh4, dh4_16, g_t1, n_t1 = _tail_bwd(dy, s_tail1, p16, W, 1, "l1")
    dh3, _, g_mla, n_mla = _mla_layer_bwd(dh4, dh4_16, s_mla, W, mla_tabs)
    dh1, dh1_16, g_t0, n_t0 = _tail_bwd(dh3, s_tail0, p16, W, 0, "l0")
    dx, g_ret, n_ret = _ret_layer_bwd(dh1, dh1_16, s_ret, W, ret_tabs)
    return loss, dx, {**g_ret, **g_t0, **g_mla, **g_t1}, _small_grads(n_ret, n_t0, n_mla, n_t1)


def _loss_head(y, target):
    def fn(yv, tv):
        e = yv - tv
        return e * (1.0 / D_MODEL), jnp.full((1, 128), 0.5 / D_MODEL, F32) * jnp.sum(e * e)
    return _rows(fn, [y, target], [], [(D_MODEL, F32)], [((1, 128), F32)], name="loss_head")


def _small_grads(n_ret, n_t0, n_mla, n_t1):
    return dict(
        mix_norm=jnp.concatenate([n_ret["mix_norm"], n_mla["mix_norm"]], axis=0),
        mlp_norm=jnp.concatenate([n_t0["mlp_norm"], n_t1["mlp_norm"]], axis=0),
        ple_norm=jnp.concatenate([n_t0["ple_norm"], n_t1["ple_norm"]], axis=0),
        ret_gn=n_ret["ret_gn"], mla_q_a_norm=n_mla["mla_q_a_norm"], mla_kv_a_norm=n_mla["mla_kv_a_norm"],
        mla_q_norm=n_mla["mla_q_norm"], mla_k_norm=n_mla["mla_k_norm"])


_ORDER = ("mix_norm", "ret_w_in", "ret_gn", "ret_w_out", "mla_w_in", "mla_q_a_norm", "mla_kv_a_norm", "mla_w_uq",
          "mla_w_ukv", "mla_q_norm", "mla_k_norm", "mla_w_out", "mlp_norm", "mlp_w1", "mlp_w2", "ple_norm",
          "ple_gate_w", "ple_proj_w")
_TWO_LAYER = ("mlp_w1", "mlp_w2", "ple_gate_w", "ple_proj_w")
HEADS_PER_CHIP = MLA_HEADS // N_CHIPS
GAIN_ROWS = 32


def _travel_parts(w):
    uq = jnp.pad(w["mla_w_uq"][0].reshape(MLA_Q_RANK, HEADS_PER_CHIP, MLA_QKD), ((0, 0), (0, 0), (0, MLA_HP - MLA_QKD)))
    parts = {"ret_w_in": w["ret_w_in"][0], "ret_w_out": w["ret_w_out"][0]}
    for k in _TWO_LAYER:
        parts[k + "_0"] = w[k][0]
    parts["mla_w_in"] = jnp.pad(w["mla_w_in"][0], ((0, 0), (0, MLA_IN_PAD - MLA_IN)))
    parts["mla_w_uq"] = uq.reshape(MLA_Q_RANK, HEADS_PER_CHIP * MLA_HP)
    parts["mla_w_ukv"] = w["mla_w_ukv"][0]
    parts["mla_w_out"] = w["mla_w_out"][0]
    for k in _TWO_LAYER:
        parts[k + "_1"] = w[k][1]
    gains = jnp.concatenate([_pad_row(w["ret_gn"]), _pad_row(w["mla_q_a_norm"]), _pad_row(w["mla_kv_a_norm"]),
                             jnp.zeros((GAIN_ROWS - 3, PACK_W), F32)], axis=0)
    return {"gains": gains, **{k: v.astype(BF16) for k, v in parts.items()}}


def _full_weights(full):
    rows = lambda a: a.reshape(-1, a.shape[-1])
    W = {k: full[k] for k in ("ret_w_in", "mla_w_uq", "mla_w_ukv")}
    for k in ("ret_w_out", "mla_w_in", "mla_w_out"):
        W[k] = rows(full[k])
    W["mlp_w1"] = [full["mlp_w1_0"], full["mlp_w1_1"]]
    W["ple_proj_w"] = [full["ple_proj_w_0"], full["ple_proj_w_1"]]
    W["mlp_w2"] = [rows(full["mlp_w2_0"]), rows(full["mlp_w2_1"])]
    W["ple_gate_w"] = [rows(full["ple_gate_w_0"]), rows(full["ple_gate_w_1"])]
    return W


def _shard_grad(name, red, shape):
    if name == "mla_w_in":
        red = red.reshape(-1, MLA_IN_PAD)[:, :MLA_IN]
    elif name == "mla_w_uq":
        red = red.reshape(MLA_Q_RANK, HEADS_PER_CHIP, MLA_HP)[:, :, :MLA_QKD]
    return red.reshape(shape)


def _pad_row(v):
    v = v.reshape(1, -1)
    return jnp.pad(v, ((0, 0), (0, PACK_W - v.shape[1])))


def kernel(x, p, mix_norm, ret_w_in, ret_gn, ret_w_out, mla_w_in, mla_q_a_norm, mla_kv_a_norm, mla_w_uq, mla_w_ukv, mla_q_norm, mla_k_norm, mla_w_out, mlp_norm, mlp_w1, mlp_w2, ple_norm, ple_gate_w, ple_proj_w, loss_target, m_mix_norm, m_ret_w_in, m_ret_gn, m_ret_w_out, m_mla_w_in, m_mla_q_a_norm, m_mla_kv_a_norm, m_mla_w_uq, m_mla_w_ukv, m_mla_q_norm, m_mla_k_norm, m_mla_w_out, m_mlp_norm, m_mlp_w1, m_mlp_w2, m_ple_norm, m_ple_gate_w, m_ple_proj_w, v_mix_norm, v_ret_w_in, v_ret_gn, v_ret_w_out, v_mla_w_in, v_mla_q_a_norm, v_mla_kv_a_norm, v_mla_w_uq, v_mla_w_ukv, v_mla_q_norm, v_mla_k_norm, v_mla_w_out, v_mlp_norm, v_mlp_w1, v_mlp_w2, v_ple_norm, v_ple_gate_w, v_ple_proj_w):
    w = dict(mix_norm=mix_norm, ret_w_in=ret_w_in, ret_gn=ret_gn, ret_w_out=ret_w_out, mla_w_in=mla_w_in,
             mla_q_a_norm=mla_q_a_norm, mla_kv_a_norm=mla_kv_a_norm, mla_w_uq=mla_w_uq, mla_w_ukv=mla_w_ukv,
             mla_q_norm=mla_q_norm, mla_k_norm=mla_k_norm, mla_w_out=mla_w_out, mlp_norm=mlp_norm, mlp_w1=mlp_w1,
             mlp_w2=mlp_w2, ple_norm=ple_norm, ple_gate_w=ple_gate_w, ple_proj_w=ple_proj_w)
    m = dict(mix_norm=m_mix_norm, ret_w_in=m_ret_w_in, ret_gn=m_ret_gn, ret_w_out=m_ret_w_out, mla_w_in=m_mla_w_in,
             mla_q_a_norm=m_mla_q_a_norm, mla_kv_a_norm=m_mla_kv_a_norm, mla_w_uq=m_mla_w_uq, mla_w_ukv=m_mla_w_ukv,
             mla_q_norm=m_mla_q_norm, mla_k_norm=m_mla_k_norm, mla_w_out=m_mla_w_out, mlp_norm=m_mlp_norm,
             mlp_w1=m_mlp_w1, mlp_w2=m_mlp_w2, ple_norm=m_ple_norm, ple_gate_w=m_ple_gate_w, ple_proj_w=m_ple_proj_w)
    v = dict(mix_norm=v_mix_norm, ret_w_in=v_ret_w_in, ret_gn=v_ret_gn, ret_w_out=v_ret_w_out, mla_w_in=v_mla_w_in,
             mla_q_a_norm=v_mla_q_a_norm, mla_kv_a_norm=v_mla_kv_a_norm, mla_w_uq=v_mla_w_uq, mla_w_ukv=v_mla_w_ukv,
             mla_q_norm=v_mla_q_norm, mla_k_norm=v_mla_k_norm, mla_w_out=v_mla_w_out, mlp_norm=v_mlp_norm,
             mlp_w1=v_mlp_w1, mlp_w2=v_mlp_w2, ple_norm=v_ple_norm, ple_gate_w=v_ple_gate_w, ple_proj_w=v_ple_proj_w)
    xi, yi, ci = _place()
    chip = 2 * xi + yi
    n = N_CHIPS

    parts = _travel_parts(w)
    first = ("gains", "ret_w_in", "ret_w_out")
    later = [k for k in parts if k not in first]
    full = dict(zip(first, _gather_weights([parts[k] for k in first], "gather_first")))
    later_copies = _gather_copies([parts[k].shape[0] for k in later])
    g_send, g_recv, later_src, later_land, g_token = _split_start(
        "gather_later_start", [parts[k] for k in later],
        [jax.ShapeDtypeStruct((n, *parts[k].shape), BF16) for k in later], 3 * len(later), later_copies,
        after=[full["ret_w_in"]])
    gains = full["gains"]
    W = dict(mix_norm=mix_norm, mlp_norm=mlp_norm, ple_norm=ple_norm,
             mla_q_norm=jnp.pad(mla_q_norm, ((0, 0), (0, MLA_HP - MLA_QKD))),
             mla_k_norm=jnp.pad(mla_k_norm, ((0, 0), (0, MLA_HP - MLA_QKD))),
             ret_w_in=full["ret_w_in"], ret_w_out=full["ret_w_out"].reshape(-1, D_MODEL),
             ret_gn=gains[:, 0, :RET_HEADS * 128].reshape(n, RET_HEADS, 128).transpose(1, 0, 2).reshape(RET_HEADS, RET_DV),
             mla_q_a_norm=gains[:, 1, :MLA_Q_RANK // n].reshape(1, MLA_Q_RANK),
             mla_kv_a_norm=gains[:, 2, :MLA_KV_RANK // n].reshape(1, MLA_KV_RANK))
    x0, p16, target = x[0], p[:, 0].astype(BF16), loss_target[0]
    T = x0.shape[0]
    ret_tabs, mla_tabs = _ret_tables(T), _mla_tables(T)

    h1, s_ret = _ret_layer_fwd(x0, W, ret_tabs, after=[g_token])
    landed = _split_wait("gather_later_wait", g_send, g_recv, later_src, later_land, later_copies, after=[h1])
    full.update(zip(later, _gather_weights([parts[k] for k in later], "gather_later_finish", landed=landed)))
    W.update(_full_weights(full))
    h3, s_tail0 = _tail_fwd(h1, p16, W, 0, "l0")
    h4, s_mla = _mla_layer_fwd(h3, W, mla_tabs)
    y, s_tail1 = _tail_fwd(h4, p16, W, 1, "l1")
    dy, loss = _loss_head(y, target)

    dh4, dh4_16, g_t1, n_t1 = _tail_bwd(dy, s_tail1, p16, W, 1, "l1")
    dh3, _, g_mla, n_mla = _mla_layer_bwd(dh4, dh4_16, s_mla, W, mla_tabs)
    beg_a = _reduce_begin({**g_mla, **g_t1}, ci, "a")
    a_send, a_recv, a_src, a_land, a_token = _split_start(
        "scatter_a_start", beg_a[3], _got_shapes(beg_a[3]), 3 * len(beg_a[3]), _scatter_copies)
    dh1, dh1_16, g_t0, n_t0 = _tail_bwd(dh3, s_tail0, p16, W, 0, "l0", after=[a_token])
    beg_b = _reduce_begin(g_t0, ci, "b")
    b_send, b_recv, b_src, b_land, b_token = _split_start(
        "scatter_b_start", beg_b[3], _got_shapes(beg_b[3]), 3 * len(beg_b[3]), _scatter_copies)
    stage_c = {}

    def start_c(g_ret):
        beg = _reduce_begin(g_ret, ci, "c")
        stage_c["beg"] = beg
        stage_c["st"] = _split_start("scatter_c_start", beg[3], _got_shapes(beg[3]), 3 * len(beg[3]), _scatter_copies)
        return [stage_c["st"][4]]

    dx, _, n_ret = _ret_layer_bwd(dh1, dh1_16, s_ret, W, ret_tabs, after=[b_token], on_grads=start_c)
    got_a = _split_wait("scatter_a_wait", a_send, a_recv, a_src, a_land, _scatter_copies, after=[dx])
    got_b = _split_wait("scatter_b_wait", b_send, b_recv, b_src, b_land, _scatter_copies, after=[dx])
    got_c = _split_wait("scatter_c_wait", *stage_c["st"][:4], _scatter_copies, after=[dx])
    red = {**_reduce_end(beg_a, got_a, chip, ci), **_reduce_end(beg_b, got_b, chip, ci),
           **_reduce_end(stage_c["beg"], got_c, chip, ci)}
    red = dict(zip(red, _share_halves(list(red.values()))))
    gs = _small_grads(n_ret, n_t0, n_mla, n_t1)
    small_g = jnp.concatenate([
        gs["mix_norm"], gs["mlp_norm"], gs["ple_norm"], gs["ret_gn"].reshape(2, PACK_W), _pad_row(gs["mla_q_a_norm"]),
        _pad_row(gs["mla_kv_a_norm"]), _pad_row(gs["mla_q_norm"][:, :MLA_QKD]), _pad_row(gs["mla_k_norm"][:, :MLA_QKD]),
        _pad_row(loss[:, :1]), jnp.zeros((3, PACK_W), F32)], axis=0)
    tot = _allsum_small(small_g, "sum_small_grads")
    gn_all = tot[6:8].reshape(RET_HEADS, n, -1)
    g_small = dict(
        mix_norm=tot[0:2], mlp_norm=tot[2:4], ple_norm=tot[4:6],
        ret_gn=lax.dynamic_index_in_dim(gn_all, chip, axis=1, keepdims=False),
        mla_q_a_norm=lax.dynamic_index_in_dim(tot[8, :MLA_Q_RANK].reshape(n, -1), chip, axis=0, keepdims=True),
        mla_kv_a_norm=lax.dynamic_index_in_dim(tot[9, :MLA_KV_RANK].reshape(n, -1), chip, axis=0, keepdims=True),
        mla_q_norm=tot[10:11, :MLA_QKD], mla_k_norm=tot[11:12, :MLA_QKD])
    loss_out = tot[12, 0]

    outs = []
    for k in _ORDER:
        if k in _TWO_LAYER:
            res = None
            for i in (1, 0):
                res = _adamw(w[k], red[f"{k}_{i}"], m[k], v[k], f"adamw_{k}_{i}", layers=2, layer=i, into=res)
        elif k in red:
            res = _adamw(w[k], _shard_grad(k, red[k], w[k].shape), m[k], v[k], f"adamw_{k}")
        else:
            res = _adamw(w[k], g_small[k], m[k], v[k], f"adamw_{k}")
        outs.append(res)
    return (loss_out, dx[None], *[o[0] for o in outs], *[o[1] for o in outs], *[o[2] for o in outs],
            *[o[3] for o in outs])
```

```python
import functools

import jax
import jax.numpy as jnp
import numpy as np
from jax import lax
from jax.experimental import pallas as pl
from jax.experimental.pallas import tpu as pltpu

F32 = jnp.float32
BF16 = jnp.bfloat16

EPS = 1e-6
D_MODEL = 1024
CHUNK = 64
ROPE_THETA = 10000.0
RET_HEADS = 4
RET_DK = 256
RET_DV = 512
RET_GROUP = 1
MLA_HEADS = 8
MLA_NOPE = 128
MLA_ROPE = 64
MLA_QKD = 192
MLA_VD = 128
MLA_HP = 256
MLA_Q_RANK = 384
MLA_KV_RANK = 256
MLA_IN = 704
MLA_IN_PAD = 768
D_FF = 4096
PLE_DIM = 256
N_CHIPS = 4

ADAM_LR = 0.001
ADAM_B1 = 0.9
ADAM_B2 = 0.999
ADAM_EPS = 1e-08
ADAM_WD = 0.01
ADAM_STEP = 10

VMEM_LIMIT = 56 * 1024 * 1024
PACK_W = 1024
NEG = -1e30
FLASH_T = 512
FLASH_HEADS = 2
MM_SUB_ROWS = 256


def _cparams(sem=None):
    return pltpu.CompilerParams(dimension_semantics=sem, vmem_limit_bytes=VMEM_LIMIT)


def _pick(dim, pref):
    if dim <= pref:
        return dim
    t = pref
    while dim % t:
        t //= 2
    return t


def _mm(a, b, *, name, ta=False, tb=False, bblk=False, outs=None, extras=(), epilogue=None, dw=None,
        tm=1024, tn=512, after=()):
    if ta:
        K, M = a.shape
    else:
        M, K = a.shape
    if bblk and tb:
        nb, N, Kq = b.shape
        assert nb * Kq == K
    elif bblk:
        nb, Kb, Nq = b.shape
        N = nb * Nq
        assert Kb == K
    else:
        N = b.shape[0] if tb else b.shape[1]
    tn = _pick(Nq if (bblk and not tb) else N, tn)
    if dw is not None and dw[0] == "cols":
        tn = _pick(N // N_CHIPS, tn)
    tm = _pick(M // N_CHIPS if (dw is not None and dw[0] == "rows") else M, tm)
    grid = (M // tm, N // tn)

    a_spec = pl.BlockSpec((K, tm), lambda i, j: (0, i)) if ta else pl.BlockSpec((tm, K), lambda i, j: (i, 0))
    if bblk and tb:
        b_spec = pl.BlockSpec((nb, tn, Kq), lambda i, j: (0, j, 0))
    elif bblk:
        npb = Nq // tn
        b_spec = pl.BlockSpec((None, K, tn), lambda i, j: (j // npb, 0, j % npb))
    elif tb:
        b_spec = pl.BlockSpec((tn, K), lambda i, j: (j, 0))
    else:
        b_spec = pl.BlockSpec((K, tn), lambda i, j: (0, j))
    in_specs = [a_spec, b_spec] + [pl.BlockSpec((tm, tn), lambda i, j: (i, j)) for _ in extras]
    args = [a, b, *extras]
    aliases = {}
    if outs is None:
        outs = [F32]
    if dw is None:
        o_specs = [pl.BlockSpec((tm, tn), lambda i, j: (i, j)) for _ in outs]
        o_shapes = [jax.ShapeDtypeStruct((M, N), dt) for dt in outs]
    else:
        kind, layers, layer, into = dw
        if kind == "cols":
            per = (N // N_CHIPS) // tn
            o_specs = [pl.BlockSpec((None, None, tm, tn), lambda i, j: (j // per, layer, i, j % per))]
            o_shapes = [jax.ShapeDtypeStruct((N_CHIPS, layers, M, N // N_CHIPS), outs[0])]
        else:
            per = (M // N_CHIPS) // tm
            o_specs = [pl.BlockSpec((None, None, tm, tn), lambda i, j: (i // per, layer, i % per, j))]
            o_shapes = [jax.ShapeDtypeStruct((N_CHIPS, layers, M // N_CHIPS, N), outs[0])]
        if into is not None:
            aliases = {len(args): 0}
            in_specs.append(pl.BlockSpec(memory_space=pl.ANY))
            args.append(into)
    for t in after:
        in_specs.append(pl.BlockSpec(memory_space=pl.ANY))
        args.append(t)
    n_e, n_o = len(extras), len(outs)

    sub = _pick(tm, MM_SUB_ROWS)

    def body(a_ref, b_ref, *rest):
        e_refs, o_refs = rest[:n_e], rest[len(rest) - n_o:]
        for r0 in range(0, tm, sub):
            rows = slice(r0, r0 + sub)
            av = (a_ref[:, rows] if ta else a_ref[rows, :]).astype(BF16)
            if bblk and tb:
                acc = _dot_nt(av[:, :Kq], b_ref[0].astype(BF16))
                for s in range(1, nb):
                    acc = acc + _dot_nt(av[:, s * Kq:(s + 1) * Kq], b_ref[s].astype(BF16))
            elif ta:
                acc = _dot_tn(av, b_ref[...].astype(BF16))
            elif tb:
                acc = _dot_nt(av, b_ref[...].astype(BF16))
            else:
                acc = _dot(av, b_ref[...].astype(BF16))
            vals = (acc,) if epilogue is None else epilogue(acc, *[e[rows, :] for e in e_refs])
            for o, v in zip(o_refs, vals):
                o[rows, :] = v.astype(o.dtype)

    res = pl.pallas_call(
        body, name=name, grid=grid, in_specs=in_specs, out_specs=o_specs, out_shape=o_shapes,
        input_output_aliases=aliases, compiler_params=_cparams(("parallel", "arbitrary")),
    )(*args)
    return res[0] if n_o == 1 else res


def _mm_rows(a, b, *, name, epilogue, outs, tb=False, bblk=False, extras=(), fulls=(), accs=(), tm=512, after=()):
    M, K = a.shape
    tm = _pick(M, tm)
    sub = _pick(tm, MM_SUB_ROWS)
    nb = b.shape[0] if bblk else 1
    n_e, n_f, n_o, n_a = len(extras), len(fulls), len(outs), len(accs)
    n_in = 2 + n_e + n_f + len(after)

    def whole(t):
        return pl.BlockSpec(t.shape, lambda i, nd=t.ndim: (0,) * nd)

    in_specs = [pl.BlockSpec((tm, K), lambda i: (i, 0)), whole(b)]
    in_specs += [pl.BlockSpec((tm, e.shape[1]), lambda i: (i, 0)) for e in extras] + [whole(f) for f in fulls]
    in_specs += [pl.BlockSpec(memory_space=pl.ANY) for _ in after]
    out_specs = [pl.BlockSpec((tm, w), lambda i: (i, 0)) for w, _ in outs] + [pl.BlockSpec(s, lambda i: (0, 0)) for s, _ in accs]
    out_shape = [jax.ShapeDtypeStruct((M, w), dt) for w, dt in outs] + [jax.ShapeDtypeStruct(s, dt) for s, dt in accs]

    def body(a_ref, b_ref, *rest):
        e_refs, f_refs = rest[:n_e], rest[n_e:n_e + n_f]
        o_refs, acc_refs = rest[n_in - 2:n_in - 2 + n_o], rest[n_in - 2 + n_o:]
        fv = [f[...] for f in f_refs]
        totals = None
        for r0 in range(0, tm, sub):
            rows = slice(r0, r0 + sub)
            av = a_ref[rows, :].astype(BF16)
            if bblk and tb:
                kq = K // nb
                acc = _dot_nt(av[:, :kq], b_ref[0])
                for s in range(1, nb):
                    acc = acc + _dot_nt(av[:, s * kq:(s + 1) * kq], b_ref[s])
            elif bblk:
                acc = jnp.concatenate([_dot(av, b_ref[s]) for s in range(nb)], axis=-1)
            elif tb:
                acc = _dot_nt(av, b_ref[...])
            else:
                acc = _dot(av, b_ref[...])
            vals = epilogue(acc, *[e[rows, :] for e in e_refs], *fv)
            for o, v in zip(o_refs, vals[:n_o]):
                o[rows, :] = v.astype(o.dtype)
            part = vals[n_o:]
            totals = part if totals is None else [t + p for t, p in zip(totals, part)]
        first_step = pl.program_id(0) == 0
        for o, v in zip(acc_refs, totals):
            @pl.when(first_step)
            def _(o=o, v=v):
                o[...] = v.astype(o.dtype)

            @pl.when(jnp.logical_not(first_step))
            def _(o=o, v=v):
                o[...] += v.astype(o.dtype)

    return pl.pallas_call(
        body, name=name, grid=(M // tm,), in_specs=in_specs, out_specs=out_specs, out_shape=out_shape,
        compiler_params=_cparams(("arbitrary",)),
    )(a, b, *extras, *fulls, *after)


def _rows(fn, rows, fulls, outs, accs=(), *, name, tile=512, after=()):
    first = rows[0][0] if isinstance(rows[0], tuple) else rows[0]
    T = first.shape[0]
    tile = _pick(T, tile)
    in_specs, args = [], []
    for r in rows:
        if isinstance(r, tuple):
            arr, w, cb = r
            in_specs.append(pl.BlockSpec((tile, w), lambda i, cb=cb: (i, cb)))
        else:
            arr = r
            in_specs.append(pl.BlockSpec((tile, arr.shape[1]), lambda i: (i, 0)))
        args.append(arr)
    for f in fulls:
        in_specs.append(pl.BlockSpec(f.shape, lambda i, nd=f.ndim: (0,) * nd))
        args.append(f)
    out_specs = [pl.BlockSpec((tile, w), lambda i: (i, 0)) for w, _ in outs]
    out_specs += [pl.BlockSpec(s, lambda i: (0, 0)) for s, _ in accs]
    out_shape = [jax.ShapeDtypeStruct((T, w), dt) for w, dt in outs]
    out_shape += [jax.ShapeDtypeStruct(s, dt) for s, dt in accs]
    n_in, n_out = len(args), len(outs)
    for t in after:
        in_specs.append(pl.BlockSpec(memory_space=pl.ANY))
        args.append(t)

    def body(*refs):
        vals = fn(*[r[...] for r in refs[:n_in]])
        o_refs = refs[len(args):]
        for o, v in zip(o_refs[:n_out], vals[:n_out]):
            o[...] = v.astype(o.dtype)
        first_step = pl.program_id(0) == 0
        for o, v in zip(o_refs[n_out:], vals[n_out:]):
            @pl.when(first_step)
            def _(o=o, v=v):
                o[...] = v.astype(o.dtype)

            @pl.when(jnp.logical_not(first_step))
            def _(o=o, v=v):
                o[...] += v.astype(o.dtype)

    res = pl.pallas_call(
        body, name=name, grid=(T // tile,), in_specs=in_specs, out_specs=out_specs, out_shape=out_shape,
        compiler_params=_cparams(("arbitrary",)),
    )(*args)
    return res


def _rms(x, g):
    r = lax.rsqrt(jnp.mean(x * x, axis=-1, keepdims=True) + EPS)
    return (x * r) * g


def _rms_bwd(x, dy, g, n=None):
    n = x.shape[-1] if n is None else n
    r = lax.rsqrt(jnp.sum(x * x, axis=-1, keepdims=True) / n + EPS)
    xh = x * r
    dxh = dy * g
    dx = r * (dxh - xh * (jnp.sum(dxh * xh, axis=-1, keepdims=True) / n))
    return dx, dy * xh


def _colsum(v):
    return jnp.sum(v, axis=0, keepdims=True)


def _sigmoid(x):
    return 1.0 / (1.0 + jnp.exp(-x))


def _widen(v, width):
    reps = width // v.shape[1]
    return v if reps == 1 else jnp.concatenate([v] * reps, axis=-1)


def _norm_fwd(h, gain, name):
    return _rows(lambda x, g: (_rms(x, g),), [h], [gain], [(h.shape[1], BF16)], name=name)[0]


def _norm_bwd(h, dhn, gain, dres, name):
    def fn(x, dy, dr, g):
        dx, dg = _rms_bwd(x, dy, g)
        return dr + dx, dr + dx, _colsum(dg)
    d = h.shape[1]
    return _rows(fn, [h, dhn, dres], [gain], [(d, F32), (d, BF16)], [((1, d), F32)], name=name)


def _rope_angles(T, dim):
    inv = (1.0 / (np.float32(ROPE_THETA) ** (np.arange(0, dim, 2, dtype=np.float32) / np.float32(dim)))).astype(np.float32)
    return np.arange(T, dtype=np.float32)[:, None] * inv[None, :]


def _ret_tables(T):
    ang = _rope_angles(T, RET_DK)
    log_gamma = np.log(np.float32(1.0) - np.float32(2.0) ** (-5.0 - np.arange(RET_HEADS, dtype=np.float32)))
    idx = np.arange(CHUNK, dtype=np.float32)
    intra = np.exp(log_gamma[:, None, None] * np.abs(idx[:, None] - idx[None, :]))
    qd = np.exp(log_gamma[:, None] * (idx + 1.0))[:, :, None]
    kd = np.exp(log_gamma[:, None] * (CHUNK - 1.0 - idx))[:, :, None]
    cd = np.exp(log_gamma * CHUNK)[:, None, None]
    return tuple(jnp.asarray(t, F32) for t in (np.cos(ang), np.sin(ang), intra, qd, kd, cd))


def _rope_half(x, c, s):
    x1, x2 = x[:, :RET_DK // 2], x[:, RET_DK // 2:]
    return jnp.concatenate([x1 * c - x2 * s, x2 * c + x1 * s], axis=-1)


def _rope_half_bwd(d, c, s):
    d1, d2 = d[:, :RET_DK // 2], d[:, RET_DK // 2:]
    return jnp.concatenate([d1 * c + d2 * s, d2 * c - d1 * s], axis=-1)


def _dot(a, b):
    return lax.dot_general(a, b, (((1,), (0,)), ((), ())), preferred_element_type=F32)


def _dot_nt(a, b):
    return lax.dot_general(a, b, (((1,), (1,)), ((), ())), preferred_element_type=F32)


def _dot_tn(a, b):
    return lax.dot_general(a, b, (((0,), (0,)), ((), ())), preferred_element_type=F32)


def _ret_specs(T, tb, rev):
    nj = T // tb
    jj = (lambda j: nj - 1 - j) if rev else (lambda j: j)
    g = RET_GROUP
    kq = RET_HEADS // g
    vq = 2 * RET_HEADS * RET_DK // (g * RET_DV)
    return dict(
        q=pl.BlockSpec((tb, g * RET_DK), lambda h, j: (jj(j), h)),
        k=pl.BlockSpec((tb, g * RET_DK), lambda h, j: (jj(j), kq + h)),
        v=pl.BlockSpec((tb, g * RET_DV), lambda h, j: (jj(j), vq + h)),
        tab=pl.BlockSpec((tb, RET_DK // 2), lambda h, j: (jj(j), 0)),
        intra=pl.BlockSpec((g, CHUNK, CHUNK), lambda h, j: (h, 0, 0)),
        dec=pl.BlockSpec((g, CHUNK, 1), lambda h, j: (h, 0, 0)),
        cd=pl.BlockSpec((g, 1, 1), lambda h, j: (h, 0, 0)),
        o=pl.BlockSpec((tb, g * RET_DV), lambda h, j: (jj(j), h)),
        s=pl.BlockSpec((g, tb // CHUNK, RET_DK, RET_DV), lambda h, j: (h, jj(j), 0, 0)),
    )


def _ret_fwd(proj, tabs, name):
    T = proj.shape[0]
    cos, sin, intra, qd, kd, cd = tabs
    tb = _pick(T, 512)
    cps = tb // CHUNK
    sp = _ret_specs(T, tb, False)
    scale = RET_DK ** -0.5

    def body(q_ref, k_ref, v_ref, cos_ref, sin_ref, intra_ref, qd_ref, kd_ref, cd_ref, o_ref, s_ref, state):
        @pl.when(pl.program_id(1) == 0)
        def _():
            state[...] = jnp.zeros_like(state)

        for c in range(cps):
            rows = pl.ds(c * CHUNK, CHUNK)
            co, si = cos_ref[rows, :], sin_ref[rows, :]
            for h in range(RET_GROUP):
                hk, hv = slice(h * RET_DK, (h + 1) * RET_DK), slice(h * RET_DV, (h + 1) * RET_DV)
                q = _rope_half(q_ref[rows, hk].astype(F32), co, si)
                k = _rope_half(k_ref[rows, hk].astype(F32), co, si) * scale
                vb = v_ref[rows, hv].astype(BF16)
                st = state[h]
                sb = st.astype(BF16)
                s_ref[h, c] = sb
                sc = _dot_nt(q.astype(BF16), k.astype(BF16)) * intra_ref[h]
                inner = _dot(sc.astype(BF16), vb)
                cross = _dot((q * qd_ref[h]).astype(BF16), sb)
                o_ref[rows, hv] = inner + cross
                state[h] = st * cd_ref[h] + _dot_tn((k * kd_ref[h]).astype(BF16), vb)

    return pl.pallas_call(
        body, name=name, grid=(RET_HEADS // RET_GROUP, T // tb),
        in_specs=[sp["q"], sp["k"], sp["v"], sp["tab"], sp["tab"], sp["intra"], sp["dec"], sp["dec"], sp["cd"]],
        out_specs=[sp["o"], sp["s"]],
        out_shape=[jax.ShapeDtypeStruct((T, RET_HEADS * RET_DV), F32),
                   jax.ShapeDtypeStruct((RET_HEADS, T // CHUNK, RET_DK, RET_DV), BF16)],
        scratch_shapes=[pltpu.VMEM((RET_GROUP, RET_DK, RET_DV), F32)],
        compiler_params=_cparams(("arbitrary", "arbitrary")),
    )(proj, proj, proj, cos, sin, intra, qd, kd, cd)


def _ret_bwd(proj, states, dout, tabs, name):
    T = proj.shape[0]
    cos, sin, intra, qd, kd, cd = tabs
    tb = _pick(T, 512)
    cps = tb // CHUNK
    sp = _ret_specs(T, tb, True)
    scale = RET_DK ** -0.5

    def body(q_ref, k_ref, v_ref, cos_ref, sin_ref, intra_ref, qd_ref, kd_ref, cd_ref, s_ref, do_ref,
             dq_ref, dk_ref, dv_ref, dstate):
        @pl.when(pl.program_id(1) == 0)
        def _():
            dstate[...] = jnp.zeros_like(dstate)

        for c in reversed(range(cps)):
            rows = pl.ds(c * CHUNK, CHUNK)
            co, si = cos_ref[rows, :], sin_ref[rows, :]
            for h in range(RET_GROUP):
                hk, hv = slice(h * RET_DK, (h + 1) * RET_DK), slice(h * RET_DV, (h + 1) * RET_DV)
                q = _rope_half(q_ref[rows, hk].astype(F32), co, si)
                k = _rope_half(k_ref[rows, hk].astype(F32), co, si) * scale
                qb, kb = q.astype(BF16), k.astype(BF16)
                vb = v_ref[rows, hv].astype(BF16)
                dob = do_ref[rows, hv].astype(BF16)
                sb = s_ref[h, c]
                ia = intra_ref[h]
                pb = (_dot_nt(qb, kb) * ia).astype(BF16)
                dsn = dstate[h]
                dsb = dsn.astype(BF16)
                kdk = (k * kd_ref[h]).astype(BF16)
                qdq = (q * qd_ref[h]).astype(BF16)
                dv = _dot_tn(pb, dob) + _dot(kdk, dsb)
                dpb = (_dot_nt(dob, vb) * ia).astype(BF16)
                dq = _dot(dpb, kb) + _dot_nt(dob, sb) * qd_ref[h]
                dk = _dot_tn(dpb, qb) + _dot_nt(vb, dsb) * kd_ref[h]
                dstate[h] = dsn * cd_ref[h] + _dot_tn(qdq, dob)
                dq_ref[rows, hk] = _rope_half_bwd(dq, co, si).astype(BF16)
                dk_ref[rows, hk] = _rope_half_bwd(dk * scale, co, si).astype(BF16)
                dv_ref[rows, hv] = dv.astype(BF16)

    return pl.pallas_call(
        body, name=name, grid=(RET_HEADS // RET_GROUP, T // tb),
        in_specs=[sp["q"], sp["k"], sp["v"], sp["tab"], sp["tab"], sp["intra"], sp["dec"], sp["dec"], sp["cd"],
                  sp["s"], sp["o"]],
        out_specs=[sp["q"], sp["q"], sp["o"]],
        out_shape=[jax.ShapeDtypeStruct((T, RET_HEADS * RET_DK), BF16),
                   jax.ShapeDtypeStruct((T, RET_HEADS * RET_DK), BF16),
                   jax.ShapeDtypeStruct((T, RET_HEADS * RET_DV), BF16)],
        scratch_shapes=[pltpu.VMEM((RET_GROUP, RET_DK, RET_DV), F32)],
        compiler_params=_cparams(("arbitrary", "arbitrary")),
    )(proj, proj, proj, cos, sin, intra, qd, kd, cd, states, dout)


def _ret_gate(out, proj, gn, name):
    def fn(o, g, *gains):
        g = g.astype(F32)
        parts = [_rms(o[:, h * RET_DV:(h + 1) * RET_DV], gains[h]) for h in range(RET_HEADS)]
        return (g * _sigmoid(g) * jnp.concatenate(parts, axis=-1),)
    w = RET_HEADS * RET_DV
    return _rows(fn, [out, (proj, w, 2)], [gn[h:h + 1] for h in range(RET_HEADS)], [(w, BF16)], name=name)[0]


def _ret_gate_bwd(out, proj, gn, dy, name):
    def fn(o, g, d, *gains):
        g = g.astype(F32)
        sg = _sigmoid(g)
        silu = g * sg
        dsilu = sg * (1.0 + g * (1.0 - sg))
        dos, dgs = [], []
        row = lax.broadcasted_iota(jnp.int32, (RET_HEADS, RET_DV), 0)
        dgn = jnp.zeros((RET_HEADS, RET_DV), F32)
        for h in range(RET_HEADS):
            sl = slice(h * RET_DV, (h + 1) * RET_DV)
            oh = o[:, sl]
            dgs.append(d[:, sl] * _rms(oh, gains[h]) * dsilu[:, sl])
            dx, dg = _rms_bwd(oh, d[:, sl] * silu[:, sl], gains[h])
            dos.append(dx)
            dgn = dgn + jnp.where(row == h, _colsum(dg), 0.0)
        return jnp.concatenate(dos, axis=-1), jnp.concatenate(dgs, axis=-1), dgn
    w = RET_HEADS * RET_DV
    return _rows(fn, [out, (proj, w, 2), dy], [gn[h:h + 1] for h in range(RET_HEADS)], [(w, BF16), (w, BF16)],
                 [((RET_HEADS, RET_DV), F32)], name=name, tile=128)


def _mla_tables(T):
    ang = _rope_angles(T, MLA_ROPE)
    c, s = np.cos(ang), np.sin(ang)
    z32, z64 = np.zeros((T, 32), np.float32), np.zeros((T, 64), np.float32)
    cos_t = np.concatenate([c, c, z64], axis=1)
    sin_a = np.concatenate([-s, z32, z64], axis=1)
    sin_b = np.concatenate([z32, s, z64], axis=1)
    return tuple(jnp.asarray(t, F32) for t in (cos_t, sin_a, sin_b))


def _rope_blk(x, ct, sa, sb):
    return x * ct + pltpu.roll(x, 96, 1) * sa + pltpu.roll(x, 32, 1) * sb


def _rope_blk_bwd(d, ct, sa, sb):
    return d * ct + pltpu.roll(d * sa, 32, 1) + pltpu.roll(d * sb, 96, 1)


def _head_norm(x, gain):
    r = lax.rsqrt(jnp.sum(x * x, axis=-1, keepdims=True) / MLA_QKD + EPS)
    return (x * r) * gain


def _mla_prep(q, kv, proj, gq, gk, tabs, name):
    def fn(qv, kvv, kr, ct, sa, sb, gqv, gkv):
        qs, ks, vs = [], [], []
        for h in range(MLA_HEADS):
            b = h * MLA_HP
            y = _head_norm(qv[:, b:b + MLA_HP], gqv)
            qs += [y[:, :128], _rope_blk(y[:, 128:], ct, sa, sb)]
            y = _head_norm(jnp.concatenate([kvv[:, b:b + 128], kr], axis=-1), gkv)
            ks += [y[:, :128], _rope_blk(y[:, 128:], ct, sa, sb)]
            vs.append(kvv[:, b + 128:b + 256])
        return jnp.concatenate(qs, axis=-1), jnp.concatenate(ks, axis=-1), jnp.concatenate(vs, axis=-1)
    w = MLA_HEADS * MLA_HP
    return _rows(fn, [q, kv, (proj, 128, 5), *tabs], [gq, gk],
                 [(w, BF16), (w, BF16), (MLA_HEADS * MLA_VD, BF16)], name=name, tile=128)


def _mla_prep_bwd(q, kv, proj, gq, gk, tabs, dqf, dkf, dvf, name):
    def fn(qv, kvv, kr, ct, sa, sb, dqv, dkv, dvv, gqv, gkv):
        dqs, dkvs = [], []
        dkr = jnp.zeros_like(kr)
        dgq = jnp.zeros((1, MLA_HP), F32)
        dgk = jnp.zeros((1, MLA_HP), F32)
        for h in range(MLA_HEADS):
            b = h * MLA_HP
            dy = jnp.concatenate([dqv[:, b:b + 128], _rope_blk_bwd(dqv[:, b + 128:b + 256], ct, sa, sb)], axis=-1)
            dx, dg = _rms_bwd(qv[:, b:b + MLA_HP], dy, gqv, MLA_QKD)
            dqs.append(dx)
            dgq = dgq + _colsum(dg)
            dy = jnp.concatenate([dkv[:, b:b + 128], _rope_blk_bwd(dkv[:, b + 128:b + 256], ct, sa, sb)], axis=-1)
            dx, dg = _rms_bwd(jnp.concatenate([kvv[:, b:b + 128], kr], axis=-1), dy, gkv, MLA_QKD)
            dkvs += [dx[:, :128], dvv[:, h * MLA_VD:(h + 1) * MLA_VD]]
            dkr = dkr + dx[:, 128:]
            dgk = dgk + _colsum(dg)
        return jnp.concatenate(dqs, axis=-1), jnp.concatenate(dkvs, axis=-1), dkr, dgq, dgk
    w = MLA_HEADS * MLA_HP
    return _rows(fn, [q, kv, (proj, 128, 5), *tabs, dqf, dkf, dvf], [gq, gk],
                 [(w, BF16), (w, BF16), (128, F32)], [((1, MLA_HP), F32), ((1, MLA_HP), F32)], name=name, tile=128)


def _chunk_mask(qi, ki, tq, tk):
    shift = CHUNK.bit_length() - 1
    rq = lax.shift_right_arithmetic(qi * tq + lax.broadcasted_iota(jnp.int32, (tq, tk), 0), shift)
    ck = lax.shift_right_arithmetic(ki * tk + lax.broadcasted_iota(jnp.int32, (tq, tk), 1), shift)
    return ck <= rq


def _flash_fwd(qf, kf, vf, name):
    T = qf.shape[0]
    t = _pick(T, FLASH_T)
    n = T // t
    scale = MLA_QKD ** -0.5

    g = FLASH_HEADS

    def body(q_ref, k_ref, v_ref, o_ref, lse_ref, m_s, l_s, acc):
        qi = pl.program_id(1)
        m_s[...] = jnp.full_like(m_s, NEG)
        l_s[...] = jnp.zeros_like(l_s)
        acc[...] = jnp.zeros_like(acc)

        def step(kb, masked):
            rows = pl.ds(pl.multiple_of(kb * t, t), t)
            for h in range(g):
                hq, hv = slice(h * MLA_HP, (h + 1) * MLA_HP), slice(h * MLA_VD, (h + 1) * MLA_VD)
                s = _dot_nt(q_ref[:, hq], k_ref[rows, hq]) * scale
                if masked:
                    s = jnp.where(_chunk_mask(0, 0, t, t), s, NEG)
                m_prev = m_s[:, hv]
                m_new = jnp.maximum(m_prev, jnp.max(s, axis=-1, keepdims=True))
                alpha = jnp.exp(m_prev - m_new)
                p = jnp.exp(s - _widen(m_new, t))
                l_s[:, hv] = alpha * l_s[:, hv] + sum(p[:, i * 128:(i + 1) * 128] for i in range(t // 128))
                acc[:, hv] = acc[:, hv] * alpha + _dot(p.astype(BF16), v_ref[rows, hv])
                m_s[:, hv] = m_new

        @pl.loop(0, qi)
        def _(kb):
            step(kb, False)

        step(qi, True)
        for h in range(g):
            hv = slice(h * MLA_VD, (h + 1) * MLA_VD)
            l = jnp.sum(l_s[:, hv], axis=-1, keepdims=True)
            o_ref[:, hv] = acc[:, hv] / l
            lse_ref[:, hv] = m_s[:, hv] + jnp.log(l)

    qmap = lambda h, i: (i, h)
    kmap = lambda h, i: (0, h)
    vec = pltpu.VMEM((t, g * MLA_VD), F32)
    return pl.pallas_call(
        body, name=name, grid=(MLA_HEADS // g, n),
        in_specs=[pl.BlockSpec((t, g * MLA_HP), qmap), pl.BlockSpec((T, g * MLA_HP), kmap),
                  pl.BlockSpec((T, g * MLA_VD), kmap)],
        out_specs=[pl.BlockSpec((t, g * MLA_VD), qmap), pl.BlockSpec((t, g * MLA_VD), qmap)],
        out_shape=[jax.ShapeDtypeStruct((T, MLA_HEADS * MLA_VD), F32),
                   jax.ShapeDtypeStruct((T, MLA_HEADS * MLA_VD), F32)],
        scratch_shapes=[vec, vec, vec],
        compiler_params=_cparams(("parallel", "arbitrary")),
    )(qf, kf, vf)


def _flash_delta(o, do, name):
    def fn(ov, dv):
        parts = []
        for h in range(MLA_HEADS):
            sl = slice(h * MLA_VD, (h + 1) * MLA_VD)
            d = jnp.sum(dv[:, sl] * ov[:, sl], axis=-1, keepdims=True)
            parts.append(jnp.broadcast_to(d, (d.shape[0], MLA_VD)))
        return jnp.concatenate(parts, axis=-1), dv
    w = MLA_HEADS * MLA_VD
    return _rows(fn, [o, do], [], [(w, F32), (w, BF16)], name=name)


def _flash_bwd(qf, kf, vf, do16, lse, delta, name):
    T = qf.shape[0]
    t = _pick(T, FLASH_T)
    n = T // t
    scale = MLA_QKD ** -0.5

    def body(q_ref, k_ref, v_ref, do_ref, lse_ref, dl_ref, dq_ref, dk_ref, dv_ref):
        kb = pl.program_id(1)

        @pl.when(kb == 0)
        def _():
            dq_ref[...] = jnp.zeros_like(dq_ref)

        dk_ref[...] = jnp.zeros_like(dk_ref)
        dv_ref[...] = jnp.zeros_like(dv_ref)
        k, v = k_ref[...], v_ref[...]

        def step(qb, masked):
            rows = pl.ds(pl.multiple_of(qb * t, t), t)
            q, dob = q_ref[rows, :], do_ref[rows, :]
            s = _dot_nt(q, k) * scale
            if masked:
                s = jnp.where(_chunk_mask(0, 0, t, t), s, NEG)
            p = jnp.exp(s - _widen(lse_ref[rows, :], t))
            ds = (p * (_dot_nt(dob, v) - _widen(dl_ref[rows, :], t)) * scale).astype(BF16)
            dv_ref[...] += _dot_tn(p.astype(BF16), dob)
            dk_ref[...] += _dot_tn(ds, q)
            dq_ref[rows, :] += _dot(ds, k)

        step(kb, True)

        @pl.loop(kb + 1, n)
        def _(qb):
            step(qb, False)

    qmap = lambda h, j: (0, h)
    kmap = lambda h, j: (j, h)
    return pl.pallas_call(
        body, name=name, grid=(MLA_HEADS, n),
        in_specs=[pl.BlockSpec((T, MLA_HP), qmap), pl.BlockSpec((t, MLA_HP), kmap), pl.BlockSpec((t, MLA_VD), kmap),
                  pl.BlockSpec((T, MLA_VD), qmap), pl.BlockSpec((T, MLA_VD), qmap), pl.BlockSpec((T, MLA_VD), qmap)],
        out_specs=[pl.BlockSpec((T, MLA_HP), qmap), pl.BlockSpec((t, MLA_HP), kmap), pl.BlockSpec((t, MLA_VD), kmap)],
        out_shape=[jax.ShapeDtypeStruct((T, MLA_HEADS * MLA_HP), F32),
                   jax.ShapeDtypeStruct((T, MLA_HEADS * MLA_HP), F32),
                   jax.ShapeDtypeStruct((T, MLA_HEADS * MLA_VD), F32)],
        compiler_params=_cparams(("arbitrary", "arbitrary")),
    )(qf, kf, vf, do16, lse, delta)


MESH = pl.DeviceIdType.MESH
ANY = pl.BlockSpec(memory_space=pl.ANY)
_CHIP_FLIPS = ((1, 0), (0, 1), (1, 1))


def _place():
    return lax.axis_index("x"), lax.axis_index("y"), lax.axis_index("c")


def _other_chip(x, y, k):
    fx, fy = _CHIP_FLIPS[k]
    return ((1 - x) if fx else x), ((1 - y) if fy else y)


def _remote(src, dst, send_sems, recv_sems, k, to):
    return pltpu.make_async_remote_copy(src_ref=src, dst_ref=dst, send_sem=send_sems.at[k], recv_sem=recv_sems.at[k],
                                        device_id=to, device_id_type=MESH)


def _index(*vals):
    return jnp.stack(vals).astype(jnp.int32)


def _half(c, rows):
    return pl.ds(pl.multiple_of(c * rows, 16), rows)


def _gather_weights(parts, name, landed=None):
    n_w = len(parts)
    n_in = n_w if landed is None else 2 * n_w

    def body(*refs):
        ins, outs = refs[:n_w], refs[n_in:n_in + n_w]
        send_sems, recv_sems, local_sems = refs[n_in + n_w:]
        x, y, c = _place()
        j = 2 * x + y
        sibling = (x, y, 1 - c)
        chips = [_other_chip(x, y, k) for k in range(3)]
        pending = []
        for w in range(n_w):
            own = pltpu.make_async_copy(ins[w], outs[w].at[j], local_sems.at[w])
            own.start()
            pending.append(own)
        sent = []
        for w in range(n_w):
            if landed is not None:
                break
            r = _half(c, parts[w].shape[0] // 2)
            for k, (px, py) in enumerate(chips):
                cp = _remote(ins[w].at[r], outs[w].at[j, r], send_sems, recv_sems, 6 * w + k, (px, py, c))
                cp.start()
                sent.append(cp)
        for w in range(n_w):
            r = _half(c, parts[w].shape[0] // 2)
            for k, (px, py) in enumerate(chips):
                blk = outs[w].at[2 * px + py, r]
                if landed is None:
                    _remote(blk, blk, send_sems, recv_sems, 6 * w + k, (px, py, c)).wait_recv()
                cp = _remote(blk, blk, send_sems, recv_sems, 6 * w + 3 + k, sibling)
                cp.start()
                sent.append(cp)
        for w in range(n_w):
            r = _half(1 - c, parts[w].shape[0] // 2)
            for k, (px, py) in enumerate(chips):
                blk = outs[w].at[2 * px + py, r]
                _remote(blk, blk, send_sems, recv_sems, 6 * w + 3 + k, sibling).wait_recv()
        for cp in sent:
            cp.wait_send()
        for cp in pending:
            cp.wait()

    return pl.pallas_call(
        body, name=name, in_specs=[pl.BlockSpec(memory_space=pltpu.VMEM)] * n_w + [ANY] * (n_in - n_w),
        out_specs=[ANY] * n_w,
        out_shape=[jax.ShapeDtypeStruct((N_CHIPS, *p.shape), p.dtype) for p in parts],
        input_output_aliases={} if landed is None else {n_w + w: w for w in range(n_w)},
        scratch_shapes=[pltpu.SemaphoreType.DMA((6 * n_w,)), pltpu.SemaphoreType.DMA((6 * n_w,)),
                        pltpu.SemaphoreType.DMA((n_w,))],
        compiler_params=pltpu.CompilerParams(vmem_limit_bytes=VMEM_LIMIT),
    )(*parts, *(landed or []))


def _swap_halves(gs, name):
    n_w = len(gs)

    def body(*refs):
        g_refs, recv_refs = refs[:n_w], refs[n_w:2 * n_w]
        send_sems, recv_sems = refs[2 * n_w:]
        x, y, c = _place()
        sent = []
        for w in range(n_w):
            for jj in range(N_CHIPS):
                cp = _remote(g_refs[w].at[jj, 1 - c], recv_refs[w].at[jj], send_sems, recv_sems, N_CHIPS * w + jj,
                             (x, y, 1 - c))
                cp.start()
                sent.append(cp)
        for cp in sent:
            cp.wait()

    return pl.pallas_call(
        body, name=name, in_specs=[ANY] * n_w, out_specs=[ANY] * n_w,
        out_shape=[jax.ShapeDtypeStruct((N_CHIPS, *g.shape[2:]), g.dtype) for g in gs],
        scratch_shapes=[pltpu.SemaphoreType.DMA((N_CHIPS * n_w,)), pltpu.SemaphoreType.DMA((N_CHIPS * n_w,))],
    )(*gs)


def _pair_sum(g, recv, core, name):
    _, H, C = recv.shape
    tile = _pick(H, 256)

    def body(c_ref, own_ref, recv_ref, out_ref):
        out_ref[...] = (own_ref[...].astype(F32) + recv_ref[...].astype(F32)).astype(BF16)

    blk = pl.BlockSpec((None, tile, C), lambda jj, i, c: (jj, i, 0))
    return pl.pallas_call(
        body, name=name,
        grid_spec=pltpu.PrefetchScalarGridSpec(
            num_scalar_prefetch=1, grid=(N_CHIPS, H // tile),
            in_specs=[pl.BlockSpec((None, None, tile, C), lambda jj, i, c: (jj, c[0], i, 0)), blk],
            out_specs=blk),
        out_shape=jax.ShapeDtypeStruct((N_CHIPS, H, C), BF16),
        compiler_params=_cparams(("arbitrary", "arbitrary")),
    )(_index(core), g, recv)


def _chip_sum(g, recv, got, chip, core, name):
    _, H, C = recv.shape
    tile = _pick(H, 256)

    def body(s_ref, own_ref, recv_ref, g0_ref, g1_ref, g2_ref, out_ref):
        pair = own_ref[...].astype(F32) + recv_ref[...].astype(F32)
        out_ref[...] = ((pair + g0_ref[...].astype(F32)) + g1_ref[...].astype(F32)) + g2_ref[...].astype(F32)

    def got_spec(k):
        return pl.BlockSpec((None, tile, C), lambda i, s, k=k: (k, i, 0))

    return pl.pallas_call(
        body, name=name,
        grid_spec=pltpu.PrefetchScalarGridSpec(
            num_scalar_prefetch=1, grid=(H // tile,),
            in_specs=[pl.BlockSpec((None, None, tile, C), lambda i, s: (s[0], s[1], i, 0)),
                      pl.BlockSpec((None, tile, C), lambda i, s: (s[0], i, 0)), got_spec(0), got_spec(1), got_spec(2)],
            out_specs=pl.BlockSpec((None, tile, C), lambda i, s: (s[1], i, 0))),
        out_shape=jax.ShapeDtypeStruct((2, H, C), F32),
        compiler_params=_cparams(("arbitrary",)),
    )(_index(chip, core), g, recv, got, got, got)


def _scatter_chips(sums, name):
    n_w = len(sums)

    def body(*refs):
        a_refs, got_refs = refs[:n_w], refs[n_w:2 * n_w]
        send_sems, recv_sems = refs[2 * n_w:]
        x, y, c = _place()
        j = 2 * x + y
        sent = []
        for w in range(n_w):
            for k in range(3):
                px, py = _other_chip(x, y, k)
                pj = 2 * px + py
                cp = _remote(a_refs[w].at[pj], got_refs[w].at[(j - pj + 4) % 4 - 1], send_sems, recv_sems, 3 * w + k,
                             (px, py, c))
                cp.start()
                sent.append(cp)
        for w in range(n_w):
            for k in range(3):
                px, py = _other_chip(x, y, k)
                slot = got_refs[w].at[(2 * px + py - j + 4) % 4 - 1]
                _remote(slot, slot, send_sems, recv_sems, 3 * w + k, (px, py, c)).wait_recv()
        for cp in sent:
            cp.wait_send()

    return pl.pallas_call(
        body, name=name, in_specs=[ANY] * n_w, out_specs=[ANY] * n_w,
        out_shape=[jax.ShapeDtypeStruct((3, *a.shape[1:]), a.dtype) for a in sums],
        scratch_shapes=[pltpu.SemaphoreType.DMA((3 * n_w,)), pltpu.SemaphoreType.DMA((3 * n_w,))],
    )(*sums)


def _share_halves(reds):
    n_w = len(reds)

    def body(*refs):
        out_refs = refs[n_w:2 * n_w]
        send_sems, recv_sems = refs[2 * n_w:]
        x, y, c = _place()
        sent = []
        for w in range(n_w):
            blk = out_refs[w].at[c]
            cp = _remote(blk, blk, send_sems, recv_sems, w, (x, y, 1 - c))
            cp.start()
            sent.append(cp)
        for cp in sent:
            cp.wait()

    return pl.pallas_call(
        body, name="grad_share_halves", in_specs=[ANY] * n_w, out_specs=[ANY] * n_w,
        out_shape=[jax.ShapeDtypeStruct(r.shape, r.dtype) for r in reds],
        input_output_aliases={w: w for w in range(n_w)},
        scratch_shapes=[pltpu.SemaphoreType.DMA((n_w,)), pltpu.SemaphoreType.DMA((n_w,))],
    )(*reds)


def _allsum_small(v, name):
    R, W = v.shape
    n_dev = 8
    vm = pl.BlockSpec(memory_space=pltpu.VMEM)

    def body(v_ref, out_ref, buf, send_sems, recv_sems):
        x, y, c = _place()
        me = 4 * x + 2 * y + c
        buf[me] = v_ref[...]
        sent = []
        for k in range(1, n_dev):
            peer = ((1 - x) if k & 4 else x, (1 - y) if k & 2 else y, (1 - c) if k & 1 else c)
            cp = _remote(v_ref, buf.at[me], send_sems, recv_sems, k - 1, peer)
            cp.start()
            sent.append(cp)
        for cp in sent:
            cp.wait_recv()
        for cp in sent:
            cp.wait_send()
        acc = buf[0]
        for q in range(1, n_dev):
            acc = acc + buf[q]
        out_ref[...] = acc

    return pl.pallas_call(
        body, name=name, in_specs=[vm], out_specs=vm, out_shape=jax.ShapeDtypeStruct((R, W), v.dtype),
        scratch_shapes=[pltpu.VMEM((n_dev, R, W), v.dtype), pltpu.SemaphoreType.DMA((n_dev - 1,)),
                        pltpu.SemaphoreType.DMA((n_dev - 1,))],
    )(v)


HBM = pl.BlockSpec(memory_space=pltpu.HBM)
SEM = pl.BlockSpec(memory_space=pltpu.SEMAPHORE)
_DATAFLOW = pltpu.SideEffectType.DATAFLOW_SIDE_EFFECTING


def _split_start(name, srcs, land_shapes, n_copies, copies, after=()):
    ns, nl = len(srcs), len(land_shapes)
    lands = [lax.empty(s.shape, s.dtype) for s in land_shapes]

    def body(*refs):
        outs = refs[ns + nl + len(after):]
        for cp in copies(refs[:ns], refs[ns:ns + nl], outs[0], outs[1]):
            cp.start()
        outs[-1][...] = jnp.zeros_like(outs[-1])

    sems = pltpu.SemaphoreType.DMA((n_copies,))
    res = pl.pallas_call(
        body, name=name, in_specs=[HBM] * (ns + nl) + [ANY] * len(after),
        out_specs=(SEM, SEM, *[HBM] * (ns + nl), pl.BlockSpec(memory_space=pltpu.VMEM)),
        out_shape=(sems, sems, *[pltpu.HBM(a.shape, a.dtype) for a in srcs],
                   *[pltpu.HBM(s.shape, s.dtype) for s in land_shapes], jax.ShapeDtypeStruct((8, 128), F32)),
        input_output_aliases={i: 2 + i for i in range(ns + nl)},
        compiler_params=pltpu.CompilerParams(has_side_effects=_DATAFLOW),
    )(*[pltpu.with_memory_space_constraint(a, pltpu.HBM) for a in [*srcs, *lands]], *after)
    return res[0], res[1], list(res[2:2 + ns]), list(res[2 + ns:2 + ns + nl]), res[-1]


def _split_wait(name, send_sems, recv_sems, srcs, lands, copies, after=()):
    ns, nl = len(srcs), len(lands)

    def body(*refs):
        for cp in copies(refs[:ns], refs[ns:ns + nl], refs[ns + nl], refs[ns + nl + 1]):
            cp.wait_send()
            cp.wait_recv()

    res = pl.pallas_call(
        body, name=name, in_specs=[HBM] * (ns + nl) + [SEM, SEM] + [ANY] * len(after), out_specs=[HBM] * (ns + nl),
        out_shape=[pltpu.HBM(a.shape, a.dtype) for a in [*srcs, *lands]],
        input_output_aliases={i: i for i in range(ns + nl)},
        compiler_params=pltpu.CompilerParams(has_side_effects=_DATAFLOW),
    )(*srcs, *lands, send_sems, recv_sems, *after)
    return list(res[ns:])


def _gather_copies(rows):
    def copies(src_refs, land_refs, send_sems, recv_sems):
        x, y, c = _place()
        j = 2 * x + y
        out = []
        for w in range(len(src_refs)):
            r = _half(c, rows[w] // 2)
            for k in range(3):
                px, py = _other_chip(x, y, k)
                out.append(_remote(src_refs[w].at[r], land_refs[w].at[j, r], send_sems, recv_sems, 3 * w + k, (px, py, c)))
        return out
    return copies


def _scatter_copies(src_refs, land_refs, send_sems, recv_sems):
    x, y, c = _place()
    j = 2 * x + y
    out = []
    for w in range(len(src_refs)):
        for k in range(3):
            px, py = _other_chip(x, y, k)
            pj = 2 * px + py
            out.append(_remote(src_refs[w].at[pj], land_refs[w].at[(j - pj + 4) % 4 - 1], send_sems, recv_sems, 3 * w + k,
                               (px, py, c)))
    return out


def _reduce_begin(grads, core, tag):
    names = list(grads)
    gs = [grads[k].reshape(N_CHIPS, 2, -1, grads[k].shape[-1]) for k in names]
    recvs = _swap_halves(gs, f"grad_swap_halves_{tag}")
    sums = [_pair_sum(g, r, core, f"pair_sum_{k}") for k, g, r in zip(names, gs, recvs)]
    return names, gs, recvs, sums


def _reduce_end(begun, gots, chip, core):
    names, gs, recvs, _ = begun
    return {k: _chip_sum(g, r, t, chip, core, f"chip_sum_{k}") for k, g, r, t in zip(names, gs, recvs, gots)}


def _got_shapes(sums):
    return [jax.ShapeDtypeStruct((3, *a.shape[1:]), a.dtype) for a in sums]


def _adamw(w, g, m, v, name, layers=1, layer=0, into=None):
    shape = w.shape
    cols = shape[-1]
    w3, m3, v3 = (t.reshape(layers, -1, cols) for t in (w, m, v))
    rows = w3.shape[1]
    tile = _pick(rows, 256) if rows % 8 == 0 else rows
    n_in = 4 + (0 if into is None else 4)

    def body(*refs):
        wv, gv, mv, vv = (r[...] for r in refs[:4])
        g_ref, d_ref, m_ref, v_ref = refs[n_in:]
        m2 = ADAM_B1 * mv + (1.0 - ADAM_B1) * gv
        v2 = ADAM_B2 * vv + (1.0 - ADAM_B2) * jnp.square(gv)
        m_hat = m2 / (1.0 - ADAM_B1 ** ADAM_STEP)
        v_hat = v2 / (1.0 - ADAM_B2 ** ADAM_STEP)
        g_ref[...] = gv
        d_ref[...] = -ADAM_LR * (m_hat / (jnp.sqrt(v_hat) + ADAM_EPS) + ADAM_WD * wv)
        m_ref[...] = m2
        v_ref[...] = v2

    lay = pl.BlockSpec((None, tile, cols), lambda i: (layer, i, 0))
    out = jax.ShapeDtypeStruct((layers, rows, cols), F32)
    res = pl.pallas_call(
        body, name=name, grid=(rows // tile,),
        in_specs=[lay, pl.BlockSpec((tile, cols), lambda i: (i, 0)), lay, lay] + [ANY] * (n_in - 4),
        out_specs=[lay] * 4, out_shape=[out] * 4,
        input_output_aliases={} if into is None else {4 + k: k for k in range(4)},
        compiler_params=_cparams(("arbitrary",)),
    )(w3, g.reshape(rows, cols), m3, v3, *([] if into is None else [t.reshape(layers, rows, cols) for t in into]))
    return tuple(t.reshape(shape) for t in res)


ROW_F32, ROW_BF16 = (D_MODEL, F32), (D_MODEL, BF16)


def _res_norm(acc, h, gain):
    hh = h + acc
    return hh, _rms(hh, gain)


def _dx_norm_bwd(d, w, h, dres, gain, name, **kw):
    def epilogue(acc, hv, dr, g):
        dx, dg = _rms_bwd(hv, acc, g)
        return dr + dx, dr + dx, _colsum(dg)
    return _mm_rows(d, w, tb=True, extras=[h, dres], fulls=[gain], outs=[ROW_F32, ROW_BF16], accs=[((1, D_MODEL), F32)],
                    epilogue=epilogue, name=name, **kw)


def _tail_fwd(h1, hn2, p16, W, i, tag, next_gain=None, target=None):
    a = _mm(hn2, W["mlp_w1"][i], bblk=True, outs=[BF16], name=f"{tag}_mlp_w1",
            epilogue=lambda acc: (jnp.square(jnp.maximum(acc, 0.0)),))
    h2, hn3 = _mm_rows(a, W["mlp_w2"][i], extras=[h1], fulls=[W["ple_norm"][i:i + 1]], outs=[ROW_F32, ROW_BF16],
                       epilogue=_res_norm, name=f"{tag}_mlp_w2")
    gl = _mm(hn3, W["ple_gate_w"][i], name=f"{tag}_ple_gate")
    if target is None:
        def gated(acc, g, h, gain):
            hh = h + _sigmoid(g) * acc
            return hh, acc, _rms(hh, gain)
        h3, pp, hn = _mm_rows(p16[i], W["ple_proj_w"][i], bblk=True, extras=[gl, h2], fulls=[next_gain],
                              outs=[ROW_F32, ROW_F32, ROW_BF16], epilogue=gated, name=f"{tag}_ple_proj")
        return h3, hn, (h1, hn2, a, h2, hn3, gl, pp)

    def gated_loss(acc, g, h, t):
        e = h + _sigmoid(g) * acc - t
        return acc, e * (1.0 / D_MODEL), jnp.full((1, 128), 0.5 / D_MODEL, F32) * jnp.sum(e * e)
    pp, dy, loss = _mm_rows(p16[i], W["ple_proj_w"][i], bblk=True, extras=[gl, h2, target], outs=[ROW_F32, ROW_F32],
                            accs=[((1, 128), F32)], epilogue=gated_loss, name=f"{tag}_ple_proj")
    return dy, loss, (h1, hn2, a, h2, hn3, gl, pp)


def _tail_bwd(dh3, saved, p16, W, i, tag, after=()):
    h1, hn2, a, h2, hn3, gl, pp = saved

    def gate_bwd(d, g, ppv):
        gate = _sigmoid(g)
        return d * gate, d * ppv * gate * (1.0 - gate)

    def dw(kind, name):
        return (kind, 1, 0, None)

    dpp, dgl = _rows(gate_bwd, [dh3, gl, pp], [], [(D_MODEL, BF16), (D_MODEL, BF16)], name=f"{tag}_ple_gate_bwd",
                     after=after)
    d_proj = _mm(p16[i], dpp, ta=True, outs=[BF16], dw=dw("cols", "ple_proj_w"), name=f"{tag}_d_ple_proj")
    d_gate = _mm(hn3, dgl, ta=True, outs=[BF16], dw=dw("rows", "ple_gate_w"), name=f"{tag}_d_ple_gate")
    dh2, dh2_16, d_ple_norm = _dx_norm_bwd(dgl, W["ple_gate_w"][i], h2, dh3, W["ple_norm"][i:i + 1],
                                           f"{tag}_ple_gate_dx")
    d_w2 = _mm(a, dh2_16, ta=True, outs=[BF16], dw=dw("rows", "mlp_w2"), name=f"{tag}_d_mlp_w2")
    dz = _mm(dh2_16, W["mlp_w2"][i], tb=True, extras=[a], outs=[BF16], name=f"{tag}_mlp_w2_dx",
             epilogue=lambda acc, av: (acc * (2.0 * jnp.sqrt(av.astype(F32))),))
    d_w1 = _mm(hn2, dz, ta=True, outs=[BF16], dw=dw("cols", "mlp_w1"), name=f"{tag}_d_mlp_w1")
    dh1, dh1_16, d_mlp_norm = _dx_norm_bwd(dz, W["mlp_w1"][i], h1, dh2, W["mlp_norm"][i:i + 1], f"{tag}_mlp_w1_dx",
                                           bblk=True)
    big = {f"mlp_w1_{i}": d_w1, f"mlp_w2_{i}": d_w2, f"ple_gate_w_{i}": d_gate, f"ple_proj_w_{i}": d_proj}
    return dh1, dh1_16, big, dict(mlp_norm=d_mlp_norm, ple_norm=d_ple_norm)


def _ret_layer_fwd(h0, W, tabs, after=()):
    hn = _rows(lambda x, g: (_rms(x, g),), [h0], [W["mix_norm"][0:1]], [(D_MODEL, BF16)], name="ret_mix_norm",
               after=after)[0]
    proj = _mm(hn, W["ret_w_in"], bblk=True, outs=[BF16], name="ret_w_in")
    out, states = _ret_fwd(proj, tabs, "ret_scan")
    y = _ret_gate(out, proj, W["ret_gn"], "ret_gate")
    h1, hn2 = _mm_rows(y, W["ret_w_out"], extras=[h0], fulls=[W["mlp_norm"][0:1]], outs=[ROW_F32, ROW_BF16],
                       epilogue=_res_norm, name="ret_w_out")
    return h1, hn2, (h0, hn, proj, out, states, y)


def _ret_layer_bwd(dh1, dh1_16, saved, W, tabs, after=(), on_grads=None):
    h0, hn, proj, out, states, y = saved
    d_w_out = _mm(y, dh1_16, ta=True, outs=[BF16], dw=("rows", 1, 0, None), name="d_ret_w_out", after=after)
    dy = _mm(dh1_16, W["ret_w_out"], tb=True, name="ret_w_out_dx", after=after)
    dout, dg, d_gn = _ret_gate_bwd(out, proj, W["ret_gn"], dy, "ret_gate_bwd")
    dq, dk, dv = _ret_bwd(proj, states, dout, tabs, "ret_scan_bwd")
    dproj = jnp.concatenate([dq, dk, dv, dg], axis=1)
    d_w_in = _mm(hn, dproj, ta=True, outs=[BF16], dw=("cols", 1, 0, None), name="d_ret_w_in")
    big = dict(ret_w_in=d_w_in, ret_w_out=d_w_out)
    later = () if on_grads is None else on_grads(big)
    dh0, _, d_mix = _dx_norm_bwd(dproj, W["ret_w_in"], h0, dh1, W["mix_norm"][0:1], "ret_w_in_dx", bblk=True, tm=256,
                                 after=later)
    return dh0, big, dict(mix_norm=d_mix, ret_gn=d_gn)


def _mla_layer_fwd(h0, hn, W, tabs):
    proj = _mm(hn, W["mla_w_in"], name="mla_w_in")

    def low_rank_norm(pv, gq, gkv):
        return _rms(pv[:, :MLA_Q_RANK], gq), _rms(pv[:, MLA_Q_RANK:MLA_Q_RANK + MLA_KV_RANK], gkv)

    cqn, ckvn = _rows(low_rank_norm, [proj], [W["mla_q_a_norm"], W["mla_kv_a_norm"]],
                      [(MLA_Q_RANK, BF16), (MLA_KV_RANK, BF16)], name="mla_low_rank_norm")
    q = _mm(cqn, W["mla_w_uq"], bblk=True, name="mla_w_uq")
    kv = _mm(ckvn, W["mla_w_ukv"], bblk=True, name="mla_w_ukv")
    qf, kf, vf = _mla_prep(q, kv, proj, W["mla_q_norm"], W["mla_k_norm"], tabs, "mla_prep")
    o, lse = _flash_fwd(qf, kf, vf, "mla_flash")
    h1, hn2 = _mm_rows(o, W["mla_w_out"], extras=[h0], fulls=[W["mlp_norm"][1:2]], outs=[ROW_F32, ROW_BF16],
                       epilogue=_res_norm, name="mla_w_out")
    return h1, hn2, (h0, hn, proj, cqn, ckvn, q, kv, qf, kf, vf, o, lse)


def _mla_layer_bwd(dh1, dh1_16, saved, W, tabs):
    h0, hn, proj, cqn, ckvn, q, kv, qf, kf, vf, o, lse = saved
    d_w_out = _mm(o, dh1_16, ta=True, outs=[BF16], dw=("rows", 1, 0, None), name="d_mla_w_out")
    do = _mm(dh1_16, W["mla_w_out"], tb=True, name="mla_w_out_dx")
    delta, do16 = _flash_delta(o, do, "mla_flash_delta")
    dqf, dkf, dvf = _flash_bwd(qf, kf, vf, do16, lse, delta, "mla_flash_bwd")
    dq, dkv, dkr, d_gq, d_gk = _mla_prep_bwd(q, kv, proj, W["mla_q_norm"], W["mla_k_norm"], tabs, dqf, dkf, dvf,
                                             "mla_prep_bwd")
    d_w_uq = _mm(cqn, dq, ta=True, outs=[BF16], dw=("cols", 1, 0, None), name="d_mla_w_uq")
    dcqn = _mm(dq, W["mla_w_uq"], tb=True, bblk=True, name="mla_w_uq_dx")
    d_w_ukv = _mm(ckvn, dkv, ta=True, outs=[BF16], dw=("cols", 1, 0, None), name="d_mla_w_ukv")
    dckvn = _mm(dkv, W["mla_w_ukv"], tb=True, bblk=True, name="mla_w_ukv_dx")

    def low_rank_bwd(pv, dcq, dckv, dkr_v, gq, gkv):
        dxq, dgq = _rms_bwd(pv[:, :MLA_Q_RANK], dcq, gq)
        dxkv, dgkv = _rms_bwd(pv[:, MLA_Q_RANK:MLA_Q_RANK + MLA_KV_RANK], dckv, gkv)
        return jnp.concatenate([dxq, dxkv, dkr_v], axis=-1), _colsum(dgq), _colsum(dgkv)

    dproj, d_gqa, d_gkva = _rows(low_rank_bwd, [proj, dcqn, dckvn, dkr], [W["mla_q_a_norm"], W["mla_kv_a_norm"]],
                                 [(MLA_IN_PAD, BF16)], [((1, MLA_Q_RANK), F32), ((1, MLA_KV_RANK), F32)],
                                 name="mla_low_rank_norm_bwd")
    d_w_in = _mm(hn, dproj, ta=True, outs=[BF16], dw=("rows", 1, 0, None), name="d_mla_w_in")
    dh0, dh0_16, d_mix = _dx_norm_bwd(dproj, W["mla_w_in"], h0, dh1, W["mix_norm"][1:2], "mla_w_in_dx")
    return (dh0, dh0_16, dict(mla_w_in=d_w_in, mla_w_uq=d_w_uq, mla_w_ukv=d_w_ukv, mla_w_out=d_w_out),
            dict(mix_norm=d_mix, mla_q_a_norm=d_gqa, mla_kv_a_norm=d_gkva, mla_q_norm=d_gq, mla_k_norm=d_gk))


def _local_step(x, p16, target, W):
    T = x.shape[0]
    ret_tabs, mla_tabs = _ret_tables(T), _mla_tables(T)
    h1, hn, s_ret = _ret_layer_fwd(x, W, ret_tabs)
    h3, hn, s_tail0 = _tail_fwd(h1, hn, p16, W, 0, "l0", next_gain=W["mix_norm"][1:2])
    h4, hn, s_mla = _mla_layer_fwd(h3, hn, W, mla_tabs)
    dy, loss, s_tail1 = _tail_fwd(h4, hn, p16, W, 1, "l1", target=target)
    dh4, dh4_16, g_t1, n_t1 = _tail_bwd(dy, s_tail1, p16, W, 1, "l1")
    dh3, _, g_mla, n_mla = _mla_layer_bwd(dh4, dh4_16, s_mla, W, mla_tabs)
    dh1, dh1_16, g_t0, n_t0 = _tail_bwd(dh3, s_tail0, p16, W, 0, "l0")
    dx, g_ret, n_ret = _ret_layer_bwd(dh1, dh1_16, s_ret, W, ret_tabs)
    return loss, dx, {**g_ret, **g_t0, **g_mla, **g_t1}, _small_grads(n_ret, n_t0, n_mla, n_t1)


def _loss_head(y, target):
    def fn(yv, tv):
        e = yv - tv
        return e * (1.0 / D_MODEL), jnp.full((1, 128), 0.5 / D_MODEL, F32) * jnp.sum(e * e)
    return _rows(fn, [y, target], [], [(D_MODEL, F32)], [((1, 128), F32)], name="loss_head")


def _small_grads(n_ret, n_t0, n_mla, n_t1):
    return dict(
        mix_norm=jnp.concatenate([n_ret["mix_norm"], n_mla["mix_norm"]], axis=0),
        mlp_norm=jnp.concatenate([n_t0["mlp_norm"], n_t1["mlp_norm"]], axis=0),
        ple_norm=jnp.concatenate([n_t0["ple_norm"], n_t1["ple_norm"]], axis=0),
        ret_gn=n_ret["ret_gn"], mla_q_a_norm=n_mla["mla_q_a_norm"], mla_kv_a_norm=n_mla["mla_kv_a_norm"],
        mla_q_norm=n_mla["mla_q_norm"], mla_k_norm=n_mla["mla_k_norm"])


_ORDER = ("mix_norm", "ret_w_in", "ret_gn", "ret_w_out", "mla_w_in", "mla_q_a_norm", "mla_kv_a_norm", "mla_w_uq",
          "mla_w_ukv", "mla_q_norm", "mla_k_norm", "mla_w_out", "mlp_norm", "mlp_w1", "mlp_w2", "ple_norm",
          "ple_gate_w", "ple_proj_w")
_TWO_LAYER = ("mlp_w1", "mlp_w2", "ple_gate_w", "ple_proj_w")
HEADS_PER_CHIP = MLA_HEADS // N_CHIPS
GAIN_ROWS = 32


def _travel_parts(w):
    uq = jnp.pad(w["mla_w_uq"][0].reshape(MLA_Q_RANK, HEADS_PER_CHIP, MLA_QKD), ((0, 0), (0, 0), (0, MLA_HP - MLA_QKD)))
    parts = {"ret_w_in": w["ret_w_in"][0], "ret_w_out": w["ret_w_out"][0]}
    for k in _TWO_LAYER:
        parts[k + "_0"] = w[k][0]
    parts["mla_w_in"] = jnp.pad(w["mla_w_in"][0], ((0, 0), (0, MLA_IN_PAD - MLA_IN)))
    parts["mla_w_uq"] = uq.reshape(MLA_Q_RANK, HEADS_PER_CHIP * MLA_HP)
    parts["mla_w_ukv"] = w["mla_w_ukv"][0]
    parts["mla_w_out"] = w["mla_w_out"][0]
    for k in _TWO_LAYER:
        parts[k + "_1"] = w[k][1]
    gains = jnp.concatenate([_pad_row(w["ret_gn"]), _pad_row(w["mla_q_a_norm"]), _pad_row(w["mla_kv_a_norm"]),
                             jnp.zeros((GAIN_ROWS - 3, PACK_W), F32)], axis=0)
    return {"gains": gains, **{k: v.astype(BF16) for k, v in parts.items()}}


def _full_weights(full):
    rows = lambda a: a.reshape(-1, a.shape[-1])
    W = {k: full[k] for k in ("ret_w_in", "mla_w_uq", "mla_w_ukv")}
    for k in ("ret_w_out", "mla_w_in", "mla_w_out"):
        W[k] = rows(full[k])
    W["mlp_w1"] = [full["mlp_w1_0"], full["mlp_w1_1"]]
    W["ple_proj_w"] = [full["ple_proj_w_0"], full["ple_proj_w_1"]]
    W["mlp_w2"] = [rows(full["mlp_w2_0"]), rows(full["mlp_w2_1"])]
    W["ple_gate_w"] = [rows(full["ple_gate_w_0"]), rows(full["ple_gate_w_1"])]
    return W


def _shard_grad(name, red, shape):
    if name == "mla_w_in":
        red = red.reshape(-1, MLA_IN_PAD)[:, :MLA_IN]
    elif name == "mla_w_uq":
        red = red.reshape(MLA_Q_RANK, HEADS_PER_CHIP, MLA_HP)[:, :, :MLA_QKD]
    return red.reshape(shape)


def _pad_row(v):
    v = v.reshape(1, -1)
    return jnp.pad(v, ((0, 0), (0, PACK_W - v.shape[1])))


def kernel(x, p, mix_norm, ret_w_in, ret_gn, ret_w_out, mla_w_in, mla_q_a_norm, mla_kv_a_norm, mla_w_uq, mla_w_ukv, mla_q_norm, mla_k_norm, mla_w_out, mlp_norm, mlp_w1, mlp_w2, ple_norm, ple_gate_w, ple_proj_w, loss_target, m_mix_norm, m_ret_w_in, m_ret_gn, m_ret_w_out, m_mla_w_in, m_mla_q_a_norm, m_mla_kv_a_norm, m_mla_w_uq, m_mla_w_ukv, m_mla_q_norm, m_mla_k_norm, m_mla_w_out, m_mlp_norm, m_mlp_w1, m_mlp_w2, m_ple_norm, m_ple_gate_w, m_ple_proj_w, v_mix_norm, v_ret_w_in, v_ret_gn, v_ret_w_out, v_mla_w_in, v_mla_q_a_norm, v_mla_kv_a_norm, v_mla_w_uq, v_mla_w_ukv, v_mla_q_norm, v_mla_k_norm, v_mla_w_out, v_mlp_norm, v_mlp_w1, v_mlp_w2, v_ple_norm, v_ple_gate_w, v_ple_proj_w):
    w = dict(mix_norm=mix_norm, ret_w_in=ret_w_in, ret_gn=ret_gn, ret_w_out=ret_w_out, mla_w_in=mla_w_in,
             mla_q_a_norm=mla_q_a_norm, mla_kv_a_norm=mla_kv_a_norm, mla_w_uq=mla_w_uq, mla_w_ukv=mla_w_ukv,
             mla_q_norm=mla_q_norm, mla_k_norm=mla_k_norm, mla_w_out=mla_w_out, mlp_norm=mlp_norm, mlp_w1=mlp_w1,
             mlp_w2=mlp_w2, ple_norm=ple_norm, ple_gate_w=ple_gate_w, ple_proj_w=ple_proj_w)
    m = dict(mix_norm=m_mix_norm, ret_w_in=m_ret_w_in, ret_gn=m_ret_gn, ret_w_out=m_ret_w_out, mla_w_in=m_mla_w_in,
             mla_q_a_norm=m_mla_q_a_norm, mla_kv_a_norm=m_mla_kv_a_norm, mla_w_uq=m_mla_w_uq, mla_w_ukv=m_mla_w_ukv,
             mla_q_norm=m_mla_q_norm, mla_k_norm=m_mla_k_norm, mla_w_out=m_mla_w_out, mlp_norm=m_mlp_norm,
             mlp_w1=m_mlp_w1, mlp_w2=m_mlp_w2, ple_norm=m_ple_norm, ple_gate_w=m_ple_gate_w, ple_proj_w=m_ple_proj_w)
    v = dict(mix_norm=v_mix_norm, ret_w_in=v_ret_w_in, ret_gn=v_ret_gn, ret_w_out=v_ret_w_out, mla_w_in=v_mla_w_in,
             mla_q_a_norm=v_mla_q_a_norm, mla_kv_a_norm=v_mla_kv_a_norm, mla_w_uq=v_mla_w_uq, mla_w_ukv=v_mla_w_ukv,
             mla_q_norm=v_mla_q_norm, mla_k_norm=v_mla_k_norm, mla_w_out=v_mla_w_out, mlp_norm=v_mlp_norm,
             mlp_w1=v_mlp_w1, mlp_w2=v_mlp_w2, ple_norm=v_ple_norm, ple_gate_w=v_ple_gate_w, ple_proj_w=v_ple_proj_w)
    xi, yi, ci = _place()
    chip = 2 * xi + yi
    n = N_CHIPS

    parts = _travel_parts(w)
    first = ("gains", "ret_w_in", "ret_w_out")
    later = [k for k in parts if k not in first]
    full = dict(zip(first, _gather_weights([parts[k] for k in first], "gather_first")))
    later_copies = _gather_copies([parts[k].shape[0] for k in later])
    g_send, g_recv, later_src, later_land, g_token = _split_start(
        "gather_later_start", [parts[k] for k in later],
        [jax.ShapeDtypeStruct((n, *parts[k].shape), BF16) for k in later], 3 * len(later), later_copies,
        after=[full["ret_w_in"]])
    gains = full["gains"]
    W = dict(mix_norm=mix_norm, mlp_norm=mlp_norm, ple_norm=ple_norm,
             mla_q_norm=jnp.pad(mla_q_norm, ((0, 0), (0, MLA_HP - MLA_QKD))),
             mla_k_norm=jnp.pad(mla_k_norm, ((0, 0), (0, MLA_HP - MLA_QKD))),
             ret_w_in=full["ret_w_in"], ret_w_out=full["ret_w_out"].reshape(-1, D_MODEL),
             ret_gn=gains[:, 0, :RET_HEADS * 128].reshape(n, RET_HEADS, 128).transpose(1, 0, 2).reshape(RET_HEADS, RET_DV),
             mla_q_a_norm=gains[:, 1, :MLA_Q_RANK // n].reshape(1, MLA_Q_RANK),
             mla_kv_a_norm=gains[:, 2, :MLA_KV_RANK // n].reshape(1, MLA_KV_RANK))
    x0, p16, target = x[0], p[:, 0].astype(BF16), loss_target[0]
    T = x0.shape[0]
    ret_tabs, mla_tabs = _ret_tables(T), _mla_tables(T)

    h1, hn, s_ret = _ret_layer_fwd(x0, W, ret_tabs, after=[g_token])
    landed = _split_wait("gather_later_wait", g_send, g_recv, later_src, later_land, later_copies, after=[h1])
    full.update(zip(later, _gather_weights([parts[k] for k in later], "gather_later_finish", landed=landed)))
    W.update(_full_weights(full))
    h3, hn, s_tail0 = _tail_fwd(h1, hn, p16, W, 0, "l0", next_gain=W["mix_norm"][1:2])
    h4, hn, s_mla = _mla_layer_fwd(h3, hn, W, mla_tabs)
    dy, loss, s_tail1 = _tail_fwd(h4, hn, p16, W, 1, "l1", target=target)

    dh4, dh4_16, g_t1, n_t1 = _tail_bwd(dy, s_tail1, p16, W, 1, "l1")
    dh3, _, g_mla, n_mla = _mla_layer_bwd(dh4, dh4_16, s_mla, W, mla_tabs)
    beg_a = _reduce_begin({**g_mla, **g_t1}, ci, "a")
    a_send, a_recv, a_src, a_land, a_token = _split_start(
        "scatter_a_start", beg_a[3], _got_shapes(beg_a[3]), 3 * len(beg_a[3]), _scatter_copies)
    dh1, dh1_16, g_t0, n_t0 = _tail_bwd(dh3, s_tail0, p16, W, 0, "l0", after=[a_token])
    beg_b = _reduce_begin(g_t0, ci, "b")
    b_send, b_recv, b_src, b_land, b_token = _split_start(
        "scatter_b_start", beg_b[3], _got_shapes(beg_b[3]), 3 * len(beg_b[3]), _scatter_copies)
    stage_c = {}

    def start_c(g_ret):
        beg = _reduce_begin(g_ret, ci, "c")
        stage_c["beg"] = beg
        stage_c["st"] = _split_start("scatter_c_start", beg[3], _got_shapes(beg[3]), 3 * len(beg[3]), _scatter_copies)
        return [stage_c["st"][4]]

    dx, _, n_ret = _ret_layer_bwd(dh1, dh1_16, s_ret, W, ret_tabs, after=[b_token], on_grads=start_c)
    got_a = _split_wait("scatter_a_wait", a_send, a_recv, a_src, a_land, _scatter_copies, after=[dx])
    got_b = _split_wait("scatter_b_wait", b_send, b_recv, b_src, b_land, _scatter_copies, after=[dx])
    got_c = _split_wait("scatter_c_wait", *stage_c["st"][:4], _scatter_copies, after=[dx])
    red = {**_reduce_end(beg_a, got_a, chip, ci), **_reduce_end(beg_b, got_b, chip, ci),
           **_reduce_end(stage_c["beg"], got_c, chip, ci)}
    red = dict(zip(red, _share_halves(list(red.values()))))
    gs = _small_grads(n_ret, n_t0, n_mla, n_t1)
    small_g = jnp.concatenate([
        gs["mix_norm"], gs["mlp_norm"], gs["ple_norm"], gs["ret_gn"].reshape(2, PACK_W), _pad_row(gs["mla_q_a_norm"]),
        _pad_row(gs["mla_kv_a_norm"]), _pad_row(gs["mla_q_norm"][:, :MLA_QKD]), _pad_row(gs["mla_k_norm"][:, :MLA_QKD]),
        _pad_row(loss[:, :1]), jnp.zeros((3, PACK_W), F32)], axis=0)
    tot = _allsum_small(small_g, "sum_small_grads")
    gn_all = tot[6:8].reshape(RET_HEADS, n, -1)
    g_small = dict(
        mix_norm=tot[0:2], mlp_norm=tot[2:4], ple_norm=tot[4:6],
        ret_gn=lax.dynamic_index_in_dim(gn_all, chip, axis=1, keepdims=False),
        mla_q_a_norm=lax.dynamic_index_in_dim(tot[8, :MLA_Q_RANK].reshape(n, -1), chip, axis=0, keepdims=True),
        mla_kv_a_norm=lax.dynamic_index_in_dim(tot[9, :MLA_KV_RANK].reshape(n, -1), chip, axis=0, keepdims=True),
        mla_q_norm=tot[10:11, :MLA_QKD], mla_k_norm=tot[11:12, :MLA_QKD])
    loss_out = tot[12, 0]

    outs = []
    for k in _ORDER:
        if k in _TWO_LAYER:
            res = None
            for i in (1, 0):
                res = _adamw(w[k], red[f"{k}_{i}"], m[k], v[k], f"adamw_{k}_{i}", layers=2, layer=i, into=res)
        elif k in red:
            res = _adamw(w[k], _shard_grad(k, red[k], w[k].shape), m[k], v[k], f"adamw_{k}")
        else:
            res = _adamw(w[k], g_small[k], m[k], v[k], f"adamw_{k}")
        outs.append(res)
    return (loss_out, dx[None], *[o[0] for o in outs], *[o[1] for o in outs], *[o[2] for o in outs],
            *[o[3] for o in outs])
```

```python
import functools

import jax
import jax.numpy as jnp
import numpy as np
from jax import lax
from jax.experimental import pallas as pl
from jax.experimental.pallas import tpu as pltpu

F32 = jnp.float32
BF16 = jnp.bfloat16

EPS = 1e-6
D_MODEL = 1024
CHUNK = 64
ROPE_THETA = 10000.0
RET_HEADS = 4
RET_DK = 256
RET_DV = 512
RET_GROUP = 1
MLA_HEADS = 8
MLA_NOPE = 128
MLA_ROPE = 64
MLA_QKD = 192
MLA_VD = 128
MLA_HP = 256
MLA_Q_RANK = 384
MLA_KV_RANK = 256
MLA_IN = 704
MLA_IN_PAD = 768
D_FF = 4096
PLE_DIM = 256
N_CHIPS = 4

ADAM_LR = 0.001
ADAM_B1 = 0.9
ADAM_B2 = 0.999
ADAM_EPS = 1e-08
ADAM_WD = 0.01
ADAM_STEP = 10

VMEM_LIMIT = 56 * 1024 * 1024
PACK_W = 1024
NEG = -1e30
LOG2E = 1.4426950408889634
FLASH_T = 512
FLASH_HEADS = 2
MM_SUB_ROWS = 256


def _cparams(sem=None):
    return pltpu.CompilerParams(dimension_semantics=sem, vmem_limit_bytes=VMEM_LIMIT)


def _pick(dim, pref):
    if dim <= pref:
        return dim
    t = pref
    while dim % t:
        t //= 2
    return t


def _mm(a, b, *, name, ta=False, tb=False, bblk=False, outs=None, extras=(), epilogue=None, dw=None,
        tm=1024, tn=512, after=()):
    if ta:
        K, M = a.shape
    else:
        M, K = a.shape
    if bblk and tb:
        nb, N, Kq = b.shape
        assert nb * Kq == K
    elif bblk:
        nb, Kb, Nq = b.shape
        N = nb * Nq
        assert Kb == K
    else:
        N = b.shape[0] if tb else b.shape[1]
    tn = _pick(Nq if (bblk and not tb) else N, tn)
    if dw is not None and dw[0] == "cols":
        tn = _pick(N // N_CHIPS, tn)
    tm = _pick(M // N_CHIPS if (dw is not None and dw[0] == "rows") else M, tm)
    grid = (M // tm, N // tn)

    a_spec = pl.BlockSpec((K, tm), lambda i, j: (0, i)) if ta else pl.BlockSpec((tm, K), lambda i, j: (i, 0))
    if bblk and tb:
        b_spec = pl.BlockSpec((nb, tn, Kq), lambda i, j: (0, j, 0))
    elif bblk:
        npb = Nq // tn
        b_spec = pl.BlockSpec((None, K, tn), lambda i, j: (j // npb, 0, j % npb))
    elif tb:
        b_spec = pl.BlockSpec((tn, K), lambda i, j: (j, 0))
    else:
        b_spec = pl.BlockSpec((K, tn), lambda i, j: (0, j))
    in_specs = [a_spec, b_spec] + [pl.BlockSpec((tm, tn), lambda i, j: (i, j)) for _ in extras]
    args = [a, b, *extras]
    aliases = {}
    if outs is None:
        outs = [F32]
    if dw is None:
        o_specs = [pl.BlockSpec((tm, tn), lambda i, j: (i, j)) for _ in outs]
        o_shapes = [jax.ShapeDtypeStruct((M, N), dt) for dt in outs]
    else:
        kind, layers, layer, into = dw
        if kind == "cols":
            per = (N // N_CHIPS) // tn
            o_specs = [pl.BlockSpec((None, None, tm, tn), lambda i, j: (j // per, layer, i, j % per))]
            o_shapes = [jax.ShapeDtypeStruct((N_CHIPS, layers, M, N // N_CHIPS), outs[0])]
        else:
            per = (M // N_CHIPS) // tm
            o_specs = [pl.BlockSpec((None, None, tm, tn), lambda i, j: (i // per, layer, i % per, j))]
            o_shapes = [jax.ShapeDtypeStruct((N_CHIPS, layers, M // N_CHIPS, N), outs[0])]
        if into is not None:
            aliases = {len(args): 0}
            in_specs.append(pl.BlockSpec(memory_space=pl.ANY))
            args.append(into)
    for t in after:
        in_specs.append(pl.BlockSpec(memory_space=pl.ANY))
        args.append(t)
    n_e, n_o = len(extras), len(outs)

    sub = _pick(tm, MM_SUB_ROWS)

    def body(a_ref, b_ref, *rest):
        e_refs, o_refs = rest[:n_e], rest[len(rest) - n_o:]
        for r0 in range(0, tm, sub):
            rows = slice(r0, r0 + sub)
            av = (a_ref[:, rows] if ta else a_ref[rows, :]).astype(BF16)
            if bblk and tb:
                acc = _dot_nt(av[:, :Kq], b_ref[0].astype(BF16))
                for s in range(1, nb):
                    acc = acc + _dot_nt(av[:, s * Kq:(s + 1) * Kq], b_ref[s].astype(BF16))
            elif ta:
                acc = _dot_tn(av, b_ref[...].astype(BF16))
            elif tb:
                acc = _dot_nt(av, b_ref[...].astype(BF16))
            else:
                acc = _dot(av, b_ref[...].astype(BF16))
            vals = (acc,) if epilogue is None else epilogue(acc, *[e[rows, :] for e in e_refs])
            for o, v in zip(o_refs, vals):
                o[rows, :] = v.astype(o.dtype)

    res = pl.pallas_call(
        body, name=name, grid=grid, in_specs=in_specs, out_specs=o_specs, out_shape=o_shapes,
        input_output_aliases=aliases, compiler_params=_cparams(("parallel", "arbitrary")),
    )(*args)
    return res[0] if n_o == 1 else res


def _mm_rows(a, b, *, name, epilogue, outs, tb=False, bblk=False, extras=(), fulls=(), accs=(), tm=512, after=()):
    M, K = a.shape
    tm = _pick(M, tm)
    sub = _pick(tm, MM_SUB_ROWS)
    nb = b.shape[0] if bblk else 1
    n_e, n_f, n_o, n_a = len(extras), len(fulls), len(outs), len(accs)
    n_in = 2 + n_e + n_f + len(after)

    def whole(t):
        return pl.BlockSpec(t.shape, lambda i, nd=t.ndim: (0,) * nd)

    in_specs = [pl.BlockSpec((tm, K), lambda i: (i, 0)), whole(b)]
    in_specs += [pl.BlockSpec((tm, e.shape[1]), lambda i: (i, 0)) for e in extras] + [whole(f) for f in fulls]
    in_specs += [pl.BlockSpec(memory_space=pl.ANY) for _ in after]
    out_specs = [pl.BlockSpec((tm, w), lambda i: (i, 0)) for w, _ in outs] + [pl.BlockSpec(s, lambda i: (0, 0)) for s, _ in accs]
    out_shape = [jax.ShapeDtypeStruct((M, w), dt) for w, dt in outs] + [jax.ShapeDtypeStruct(s, dt) for s, dt in accs]

    def body(a_ref, b_ref, *rest):
        e_refs, f_refs = rest[:n_e], rest[n_e:n_e + n_f]
        o_refs, acc_refs = rest[n_in - 2:n_in - 2 + n_o], rest[n_in - 2 + n_o:]
        fv = [f[...] for f in f_refs]
        totals = None
        for r0 in range(0, tm, sub):
            rows = slice(r0, r0 + sub)
            av = a_ref[rows, :].astype(BF16)
            if bblk and tb:
                kq = K // nb
                acc = _dot_nt(av[:, :kq], b_ref[0])
                for s in range(1, nb):
                    acc = acc + _dot_nt(av[:, s * kq:(s + 1) * kq], b_ref[s])
            elif bblk:
                acc = jnp.concatenate([_dot(av, b_ref[s]) for s in range(nb)], axis=-1)
            elif tb:
                acc = _dot_nt(av, b_ref[...])
            else:
                acc = _dot(av, b_ref[...])
            vals = epilogue(acc, *[e[rows, :] for e in e_refs], *fv)
            for o, v in zip(o_refs, vals[:n_o]):
                o[rows, :] = v.astype(o.dtype)
            part = vals[n_o:]
            totals = part if totals is None else [t + p for t, p in zip(totals, part)]
        first_step = pl.program_id(0) == 0
        for o, v in zip(acc_refs, totals):
            @pl.when(first_step)
            def _(o=o, v=v):
                o[...] = v.astype(o.dtype)

            @pl.when(jnp.logical_not(first_step))
            def _(o=o, v=v):
                o[...] += v.astype(o.dtype)

    return pl.pallas_call(
        body, name=name, grid=(M // tm,), in_specs=in_specs, out_specs=out_specs, out_shape=out_shape,
        compiler_params=_cparams(("arbitrary",)),
    )(a, b, *extras, *fulls, *after)


def _rows(fn, rows, fulls, outs, accs=(), *, name, tile=512, after=()):
    first = rows[0][0] if isinstance(rows[0], tuple) else rows[0]
    T = first.shape[0]
    tile = _pick(T, tile)
    in_specs, args = [], []
    for r in rows:
        if isinstance(r, tuple):
            arr, w, cb = r
            in_specs.append(pl.BlockSpec((tile, w), lambda i, cb=cb: (i, cb)))
        else:
            arr = r
            in_specs.append(pl.BlockSpec((tile, arr.shape[1]), lambda i: (i, 0)))
        args.append(arr)
    for f in fulls:
        in_specs.append(pl.BlockSpec(f.shape, lambda i, nd=f.ndim: (0,) * nd))
        args.append(f)
    outs = [o if len(o) == 4 else (*o, o[0], 0) for o in outs]
    out_specs = [pl.BlockSpec((tile, w), lambda i, cb=cb: (i, cb)) for w, _, _, cb in outs]
    out_specs += [pl.BlockSpec(s, lambda i: (0, 0)) for s, _ in accs]
    out_shape = [jax.ShapeDtypeStruct((T, tw), dt) for _, dt, tw, _ in outs]
    out_shape += [jax.ShapeDtypeStruct(s, dt) for s, dt in accs]
    n_in, n_out = len(args), len(outs)
    for t in after:
        in_specs.append(pl.BlockSpec(memory_space=pl.ANY))
        args.append(t)

    def body(*refs):
        vals = fn(*[r[...] for r in refs[:n_in]])
        o_refs = refs[len(args):]
        for o, v in zip(o_refs[:n_out], vals[:n_out]):
            o[...] = v.astype(o.dtype)
        first_step = pl.program_id(0) == 0
        for o, v in zip(o_refs[n_out:], vals[n_out:]):
            @pl.when(first_step)
            def _(o=o, v=v):
                o[...] = v.astype(o.dtype)

            @pl.when(jnp.logical_not(first_step))
            def _(o=o, v=v):
                o[...] += v.astype(o.dtype)

    res = pl.pallas_call(
        body, name=name, grid=(T // tile,), in_specs=in_specs, out_specs=out_specs, out_shape=out_shape,
        compiler_params=_cparams(("arbitrary",)),
    )(*args)
    return res


def _rms(x, g):
    r = lax.rsqrt(jnp.mean(x * x, axis=-1, keepdims=True) + EPS)
    return (x * r) * g


def _rms_bwd(x, dy, g, n=None):
    n = x.shape[-1] if n is None else n
    r = lax.rsqrt(jnp.sum(x * x, axis=-1, keepdims=True) / n + EPS)
    xh = x * r
    dxh = dy * g
    dx = r * (dxh - xh * (jnp.sum(dxh * xh, axis=-1, keepdims=True) / n))
    return dx, dy * xh


def _colsum(v):
    return jnp.sum(v, axis=0, keepdims=True)


def _sigmoid(x):
    return 1.0 / (1.0 + jnp.exp(-x))


def _widen(v, width):
    reps = width // v.shape[1]
    return v if reps == 1 else jnp.concatenate([v] * reps, axis=-1)


def _norm_fwd(h, gain, name):
    return _rows(lambda x, g: (_rms(x, g),), [h], [gain], [(h.shape[1], BF16)], name=name)[0]


def _norm_bwd(h, dhn, gain, dres, name):
    def fn(x, dy, dr, g):
        dx, dg = _rms_bwd(x, dy, g)
        return dr + dx, dr + dx, _colsum(dg)
    d = h.shape[1]
    return _rows(fn, [h, dhn, dres], [gain], [(d, F32), (d, BF16)], [((1, d), F32)], name=name)


def _rope_angles(T, dim):
    inv = (1.0 / (np.float32(ROPE_THETA) ** (np.arange(0, dim, 2, dtype=np.float32) / np.float32(dim)))).astype(np.float32)
    return np.arange(T, dtype=np.float32)[:, None] * inv[None, :]


def _ret_tables(T):
    ang = _rope_angles(T, RET_DK)
    log_gamma = np.log(np.float32(1.0) - np.float32(2.0) ** (-5.0 - np.arange(RET_HEADS, dtype=np.float32)))
    idx = np.arange(CHUNK, dtype=np.float32)
    intra = np.exp(log_gamma[:, None, None] * np.abs(idx[:, None] - idx[None, :]))
    qd = np.exp(log_gamma[:, None] * (idx + 1.0))[:, :, None]
    kd = np.exp(log_gamma[:, None] * (CHUNK - 1.0 - idx))[:, :, None]
    cd = np.exp(log_gamma * CHUNK)[:, None, None]
    return tuple(jnp.asarray(t, F32) for t in (np.cos(ang), np.sin(ang), intra, qd, kd, cd))


def _rope_half(x, c, s):
    x1, x2 = x[:, :RET_DK // 2], x[:, RET_DK // 2:]
    return jnp.concatenate([x1 * c - x2 * s, x2 * c + x1 * s], axis=-1)


def _rope_half_bwd(d, c, s):
    d1, d2 = d[:, :RET_DK // 2], d[:, RET_DK // 2:]
    return jnp.concatenate([d1 * c + d2 * s, d2 * c - d1 * s], axis=-1)


def _dot(a, b):
    return lax.dot_general(a, b, (((1,), (0,)), ((), ())), preferred_element_type=F32)


def _dot_nt(a, b):
    return lax.dot_general(a, b, (((1,), (1,)), ((), ())), preferred_element_type=F32)


def _dot_tn(a, b):
    return lax.dot_general(a, b, (((0,), (0,)), ((), ())), preferred_element_type=F32)


def _ret_specs(T, tb, rev):
    nj = T // tb
    jj = (lambda j: nj - 1 - j) if rev else (lambda j: j)
    g = RET_GROUP
    kq = RET_HEADS // g
    vq = 2 * RET_HEADS * RET_DK // (g * RET_DV)
    return dict(
        q=pl.BlockSpec((tb, g * RET_DK), lambda h, j: (jj(j), h)),
        k=pl.BlockSpec((tb, g * RET_DK), lambda h, j: (jj(j), kq + h)),
        v=pl.BlockSpec((tb, g * RET_DV), lambda h, j: (jj(j), vq + h)),
        tab=pl.BlockSpec((tb, RET_DK // 2), lambda h, j: (jj(j), 0)),
        intra=pl.BlockSpec((g, CHUNK, CHUNK), lambda h, j: (h, 0, 0)),
        dec=pl.BlockSpec((g, CHUNK, 1), lambda h, j: (h, 0, 0)),
        cd=pl.BlockSpec((g, 1, 1), lambda h, j: (h, 0, 0)),
        o=pl.BlockSpec((tb, g * RET_DV), lambda h, j: (jj(j), h)),
        s=pl.BlockSpec((g, tb // CHUNK, RET_DK, RET_DV), lambda h, j: (h, jj(j), 0, 0)),
    )


def _ret_fwd(proj, tabs, name):
    T = proj.shape[0]
    cos, sin, intra, qd, kd, cd = tabs
    tb = _pick(T, 512)
    cps = tb // CHUNK
    sp = _ret_specs(T, tb, False)
    scale = RET_DK ** -0.5

    def body(q_ref, k_ref, v_ref, cos_ref, sin_ref, intra_ref, qd_ref, kd_ref, cd_ref, o_ref, s_ref, state):
        @pl.when(pl.program_id(1) == 0)
        def _():
            state[...] = jnp.zeros_like(state)

        for c in range(cps):
            rows = pl.ds(c * CHUNK, CHUNK)
            co, si = cos_ref[rows, :], sin_ref[rows, :]
            for h in range(RET_GROUP):
                hk, hv = slice(h * RET_DK, (h + 1) * RET_DK), slice(h * RET_DV, (h + 1) * RET_DV)
                q = _rope_half(q_ref[rows, hk].astype(F32), co, si)
                k = _rope_half(k_ref[rows, hk].astype(F32), co, si) * scale
                vb = v_ref[rows, hv].astype(BF16)
                st = state[h]
                sb = st.astype(BF16)
                s_ref[h, c] = sb
                sc = _dot_nt(q.astype(BF16), k.astype(BF16)) * intra_ref[h]
                inner = _dot(sc.astype(BF16), vb)
                cross = _dot((q * qd_ref[h]).astype(BF16), sb)
                o_ref[rows, hv] = inner + cross
                state[h] = st * cd_ref[h] + _dot_tn((k * kd_ref[h]).astype(BF16), vb)

    return pl.pallas_call(
        body, name=name, grid=(RET_HEADS // RET_GROUP, T // tb),
        in_specs=[sp["q"], sp["k"], sp["v"], sp["tab"], sp["tab"], sp["intra"], sp["dec"], sp["dec"], sp["cd"]],
        out_specs=[sp["o"], sp["s"]],
        out_shape=[jax.ShapeDtypeStruct((T, RET_HEADS * RET_DV), F32),
                   jax.ShapeDtypeStruct((RET_HEADS, T // CHUNK, RET_DK, RET_DV), BF16)],
        scratch_shapes=[pltpu.VMEM((RET_GROUP, RET_DK, RET_DV), F32)],
        compiler_params=_cparams(("arbitrary", "arbitrary")),
    )(proj, proj, proj, cos, sin, intra, qd, kd, cd)


def _ret_bwd(proj, states, dout, dproj, tabs, name):
    assert RET_GROUP == 1
    T = proj.shape[0]
    cos, sin, intra, qd, kd, cd = tabs
    tb = _pick(T, 512)
    cps = tb // CHUNK
    nj = T // tb
    sp = _ret_specs(T, tb, True)
    scale = RET_DK ** -0.5
    k0, v0 = RET_HEADS * RET_DK, 2 * RET_HEADS * RET_DK

    def body(q_ref, k_ref, v_ref, cos_ref, sin_ref, intra_ref, qd_ref, kd_ref, cd_ref, s_ref, do_ref, _dproj_in,
             out_ref, dq_s, dk_s, dv_s, sems, dstate):
        head, j = pl.program_id(0), pl.program_id(1)
        step = head * nj + j
        slot = step % 2
        dq_ref, dk_ref, dv_ref = dq_s.at[slot], dk_s.at[slot], dv_s.at[slot]

        @pl.when(j == 0)
        def _():
            dstate[...] = jnp.zeros_like(dstate)

        for c in reversed(range(cps)):
            rows = pl.ds(c * CHUNK, CHUNK)
            co, si = cos_ref[rows, :], sin_ref[rows, :]
            for h in range(RET_GROUP):
                hk, hv = slice(h * RET_DK, (h + 1) * RET_DK), slice(h * RET_DV, (h + 1) * RET_DV)
                q = _rope_half(q_ref[rows, hk].astype(F32), co, si)
                k = _rope_half(k_ref[rows, hk].astype(F32), co, si) * scale
                qb, kb = q.astype(BF16), k.astype(BF16)
                vb = v_ref[rows, hv].astype(BF16)
                dob = do_ref[rows, hv].astype(BF16)
                sb = s_ref[h, c]
                ia = intra_ref[h]
                pb = (_dot_nt(qb, kb) * ia).astype(BF16)
                dsn = dstate[h]
                dsb = dsn.astype(BF16)
                kdk = (k * kd_ref[h]).astype(BF16)
                qdq = (q * qd_ref[h]).astype(BF16)
                dv = _dot_tn(pb, dob) + _dot(kdk, dsb)
                dpb = (_dot_nt(dob, vb) * ia).astype(BF16)
                dq = _dot(dpb, kb) + _dot_nt(dob, sb) * qd_ref[h]
                dk = _dot_tn(dpb, qb) + _dot_nt(vb, dsb) * kd_ref[h]
                dstate[h] = dsn * cd_ref[h] + _dot_tn(qdq, dob)
                dq_ref[rows, hk] = _rope_half_bwd(dq, co, si).astype(BF16)
                dk_ref[rows, hk] = _rope_half_bwd(dk * scale, co, si).astype(BF16)
                dv_ref[rows, hv] = dv.astype(BF16)

        def copies(sl):
            r = pl.ds(pl.multiple_of((nj - 1 - j) * tb, tb), tb)
            cols = lambda first, w: pl.ds(pl.multiple_of(first + head * w, 128), w)
            return [pltpu.make_async_copy(dq_s.at[sl], out_ref.at[r, cols(0, RET_DK)], sems.at[sl, 0]),
                    pltpu.make_async_copy(dk_s.at[sl], out_ref.at[r, cols(k0, RET_DK)], sems.at[sl, 1]),
                    pltpu.make_async_copy(dv_s.at[sl], out_ref.at[r, cols(v0, RET_DV)], sems.at[sl, 2])]

        @pl.when(step > 0)
        def _():
            for cp in copies(1 - slot):
                cp.wait()

        for cp in copies(slot):
            cp.start()

        @pl.when(step == RET_HEADS * nj - 1)
        def _():
            for cp in copies(slot):
                cp.wait()

    return pl.pallas_call(
        body, name=name, grid=(RET_HEADS, nj),
        in_specs=[sp["q"], sp["k"], sp["v"], sp["tab"], sp["tab"], sp["intra"], sp["dec"], sp["dec"], sp["cd"],
                  sp["s"], sp["o"], pl.BlockSpec(memory_space=pl.ANY)],
        out_specs=pl.BlockSpec(memory_space=pl.ANY), out_shape=jax.ShapeDtypeStruct(dproj.shape, dproj.dtype),
        input_output_aliases={11: 0},
        scratch_shapes=[pltpu.VMEM((2, tb, RET_DK), BF16), pltpu.VMEM((2, tb, RET_DK), BF16),
                        pltpu.VMEM((2, tb, RET_DV), BF16), pltpu.SemaphoreType.DMA((2, 3)),
                        pltpu.VMEM((RET_GROUP, RET_DK, RET_DV), F32)],
        compiler_params=_cparams(("arbitrary", "arbitrary")),
    )(proj, proj, proj, cos, sin, intra, qd, kd, cd, states, dout, dproj)


def _ret_gate(out, proj, gn, name):
    def fn(o, g, *gains):
        g = g.astype(F32)
        parts = [_rms(o[:, h * RET_DV:(h + 1) * RET_DV], gains[h]) for h in range(RET_HEADS)]
        return (g * _sigmoid(g) * jnp.concatenate(parts, axis=-1),)
    w = RET_HEADS * RET_DV
    return _rows(fn, [out, (proj, w, 2)], [gn[h:h + 1] for h in range(RET_HEADS)], [(w, BF16)], name=name)[0]


def _ret_gate_bwd(out, proj, gn, dy, name):
    def fn(o, g, d, *gains):
        g = g.astype(F32)
        sg = _sigmoid(g)
        silu = g * sg
        dsilu = sg * (1.0 + g * (1.0 - sg))
        dos, dgs = [], []
        row = lax.broadcasted_iota(jnp.int32, (RET_HEADS, RET_DV), 0)
        dgn = jnp.zeros((RET_HEADS, RET_DV), F32)
        for h in range(RET_HEADS):
            sl = slice(h * RET_DV, (h + 1) * RET_DV)
            oh = o[:, sl]
            dgs.append(d[:, sl] * _rms(oh, gains[h]) * dsilu[:, sl])
            dx, dg = _rms_bwd(oh, d[:, sl] * silu[:, sl], gains[h])
            dos.append(dx)
            dgn = dgn + jnp.where(row == h, _colsum(dg), 0.0)
        return jnp.concatenate(dos, axis=-1), jnp.concatenate(dgs, axis=-1), dgn
    w = RET_HEADS * RET_DV
    return _rows(fn, [out, (proj, w, 2), dy], [gn[h:h + 1] for h in range(RET_HEADS)],
                 [(w, BF16), (w, BF16, proj.shape[1], 2)], [((RET_HEADS, RET_DV), F32)], name=name, tile=128)


def _mla_tables(T):
    ang = _rope_angles(T, MLA_ROPE)
    c, s = np.cos(ang), np.sin(ang)
    z32, z64 = np.zeros((T, 32), np.float32), np.zeros((T, 64), np.float32)
    cos_t = np.concatenate([c, c, z64], axis=1)
    sin_a = np.concatenate([-s, z32, z64], axis=1)
    sin_b = np.concatenate([z32, s, z64], axis=1)
    return tuple(jnp.asarray(t, F32) for t in (cos_t, sin_a, sin_b))


def _rope_blk(x, ct, sa, sb):
    return x * ct + pltpu.roll(x, 96, 1) * sa + pltpu.roll(x, 32, 1) * sb


def _rope_blk_bwd(d, ct, sa, sb):
    return d * ct + pltpu.roll(d * sa, 32, 1) + pltpu.roll(d * sb, 96, 1)


def _head_norm(x, gain):
    r = lax.rsqrt(jnp.sum(x * x, axis=-1, keepdims=True) / MLA_QKD + EPS)
    return (x * r) * gain


def _mla_prep(q, kv, proj, gq, gk, tabs, name):
    def fn(qv, kvv, kr, ct, sa, sb, gqv, gkv):
        qs, ks, vs = [], [], []
        for h in range(MLA_HEADS):
            b = h * MLA_HP
            y = _head_norm(qv[:, b:b + MLA_HP], gqv)
            qs += [y[:, :128], _rope_blk(y[:, 128:], ct, sa, sb)]
            y = _head_norm(jnp.concatenate([kvv[:, b:b + 128], kr], axis=-1), gkv)
            ks += [y[:, :128], _rope_blk(y[:, 128:], ct, sa, sb)]
            vs.append(kvv[:, b + 128:b + 256])
        return jnp.concatenate(qs, axis=-1), jnp.concatenate(ks, axis=-1), jnp.concatenate(vs, axis=-1)
    w = MLA_HEADS * MLA_HP
    return _rows(fn, [q, kv, (proj, 128, 5), *tabs], [gq, gk],
                 [(w, BF16), (w, BF16), (MLA_HEADS * MLA_VD, BF16)], name=name, tile=128)


def _mla_prep_bwd(q, kv, proj, gq, gk, tabs, dqf, dkf, dvf, name):
    def fn(qv, kvv, kr, ct, sa, sb, dqv, dkv, dvv, gqv, gkv):
        dqs, dkvs = [], []
        dkr = jnp.zeros_like(kr)
        dgq = jnp.zeros((1, MLA_HP), F32)
        dgk = jnp.zeros((1, MLA_HP), F32)
        for h in range(MLA_HEADS):
            b = h * MLA_HP
            dy = jnp.concatenate([dqv[:, b:b + 128], _rope_blk_bwd(dqv[:, b + 128:b + 256], ct, sa, sb)], axis=-1)
            dx, dg = _rms_bwd(qv[:, b:b + MLA_HP], dy, gqv, MLA_QKD)
            dqs.append(dx)
            dgq = dgq + _colsum(dg)
            dy = jnp.concatenate([dkv[:, b:b + 128], _rope_blk_bwd(dkv[:, b + 128:b + 256], ct, sa, sb)], axis=-1)
            dx, dg = _rms_bwd(jnp.concatenate([kvv[:, b:b + 128], kr], axis=-1), dy, gkv, MLA_QKD)
            dkvs += [dx[:, :128], dvv[:, h * MLA_VD:(h + 1) * MLA_VD]]
            dkr = dkr + dx[:, 128:]
            dgk = dgk + _colsum(dg)
        return jnp.concatenate(dqs, axis=-1), jnp.concatenate(dkvs, axis=-1), dkr, dgq, dgk
    w = MLA_HEADS * MLA_HP
    return _rows(fn, [q, kv, (proj, 128, 5), *tabs, dqf, dkf, dvf], [gq, gk],
                 [(w, BF16), (w, BF16), (128, F32)], [((1, MLA_HP), F32), ((1, MLA_HP), F32)], name=name, tile=128)


def _chunk_mask(qi, ki, tq, tk):
    shift = CHUNK.bit_length() - 1
    rq = lax.shift_right_arithmetic(qi * tq + lax.broadcasted_iota(jnp.int32, (tq, tk), 0), shift)
    ck = lax.shift_right_arithmetic(ki * tk + lax.broadcasted_iota(jnp.int32, (tq, tk), 1), shift)
    return ck <= rq


def _flash_fwd(qf, kf, vf, name):
    T = qf.shape[0]
    t = _pick(T, FLASH_T)
    n = T // t
    scale = MLA_QKD ** -0.5

    g = FLASH_HEADS

    def body(q_ref, k_ref, v_ref, o_ref, lse_ref, m_s, l_s, acc):
        qi = pl.program_id(1)
        m_s[...] = jnp.full_like(m_s, NEG)
        l_s[...] = jnp.zeros_like(l_s)
        acc[...] = jnp.zeros_like(acc)

        def step(kb, masked):
            rows = pl.ds(pl.multiple_of(kb * t, t), t)
            for h in range(g):
                hq, hv = slice(h * MLA_HP, (h + 1) * MLA_HP), slice(h * MLA_VD, (h + 1) * MLA_VD)
                s = _dot_nt(q_ref[:, hq], k_ref[rows, hq])
                if masked:
                    s = jnp.where(_chunk_mask(0, 0, t, t), s, NEG)
                m_prev = m_s[:, hv]
                m_new = jnp.maximum(m_prev, jnp.max(s, axis=-1, keepdims=True))
                alpha = jnp.exp2(m_prev - m_new)
                p = jnp.exp2(s - _widen(m_new, t))
                l_s[:, hv] = alpha * l_s[:, hv] + sum(p[:, i * 128:(i + 1) * 128] for i in range(t // 128))
                acc[:, hv] = acc[:, hv] * alpha + _dot(p.astype(BF16), v_ref[rows, hv])
                m_s[:, hv] = m_new

        @pl.loop(0, qi)
        def _(kb):
            step(kb, False)

        step(qi, True)
        for h in range(g):
            hv = slice(h * MLA_VD, (h + 1) * MLA_VD)
            l = jnp.sum(l_s[:, hv], axis=-1, keepdims=True)
            o_ref[:, hv] = acc[:, hv] / l
            lse_ref[:, hv] = m_s[:, hv] + jnp.log2(l)

    qmap = lambda h, i: (i, h)
    kmap = lambda h, i: (0, h)
    vec = pltpu.VMEM((t, g * MLA_VD), F32)
    return pl.pallas_call(
        body, name=name, grid=(MLA_HEADS // g, n),
        in_specs=[pl.BlockSpec((t, g * MLA_HP), qmap), pl.BlockSpec((T, g * MLA_HP), kmap),
                  pl.BlockSpec((T, g * MLA_VD), kmap)],
        out_specs=[pl.BlockSpec((t, g * MLA_VD), qmap), pl.BlockSpec((t, g * MLA_VD), qmap)],
        out_shape=[jax.ShapeDtypeStruct((T, MLA_HEADS * MLA_VD), F32),
                   jax.ShapeDtypeStruct((T, MLA_HEADS * MLA_VD), F32)],
        scratch_shapes=[vec, vec, vec],
        compiler_params=_cparams(("parallel", "arbitrary")),
    )(qf, kf, vf)


def _flash_delta(o, do, name):
    def fn(ov, dv):
        parts = []
        for h in range(MLA_HEADS):
            sl = slice(h * MLA_VD, (h + 1) * MLA_VD)
            d = jnp.sum(dv[:, sl] * ov[:, sl], axis=-1, keepdims=True)
            parts.append(jnp.broadcast_to(d, (d.shape[0], MLA_VD)))
        return jnp.concatenate(parts, axis=-1), dv
    w = MLA_HEADS * MLA_VD
    return _rows(fn, [o, do], [], [(w, F32), (w, BF16)], name=name)


def _flash_bwd(qf, kf, vf, do16, lse, delta, name):
    T = qf.shape[0]
    t = _pick(T, FLASH_T)
    n = T // t
    scale = MLA_QKD ** -0.5

    def body(q_ref, k_ref, v_ref, do_ref, lse_ref, dl_ref, dq_ref, dk_ref, dv_ref):
        kb = pl.program_id(1)

        @pl.when(kb == 0)
        def _():
            dq_ref[...] = jnp.zeros_like(dq_ref)

        dk_ref[...] = jnp.zeros_like(dk_ref)
        dv_ref[...] = jnp.zeros_like(dv_ref)
        k, v = k_ref[...], v_ref[...]

        def step(qb, masked):
            rows = pl.ds(pl.multiple_of(qb * t, t), t)
            q, dob = q_ref[rows, :], do_ref[rows, :]
            s = _dot_nt(q, k)
            if masked:
                s = jnp.where(_chunk_mask(0, 0, t, t), s, NEG)
            p = jnp.exp2(s - _widen(lse_ref[rows, :], t))
            ds = (p * (_dot_nt(dob, v) - _widen(dl_ref[rows, :], t))).astype(BF16)
            dv_ref[...] += _dot_tn(p.astype(BF16), dob)
            dk_ref[...] += _dot_tn(ds, q)
            dq_ref[rows, :] += _dot(ds, k)

        step(kb, True)

        @pl.loop(kb + 1, n)
        def _(qb):
            step(qb, False)

        dk_ref[...] = dk_ref[...] * (1.0 / LOG2E)

        @pl.when(kb == n - 1)
        def _():
            dq_ref[...] = dq_ref[...] * scale

    qmap = lambda h, j: (0, h)
    kmap = lambda h, j: (j, h)
    return pl.pallas_call(
        body, name=name, grid=(MLA_HEADS, n),
        in_specs=[pl.BlockSpec((T, MLA_HP), qmap), pl.BlockSpec((t, MLA_HP), kmap), pl.BlockSpec((t, MLA_VD), kmap),
                  pl.BlockSpec((T, MLA_VD), qmap), pl.BlockSpec((T, MLA_VD), qmap), pl.BlockSpec((T, MLA_VD), qmap)],
        out_specs=[pl.BlockSpec((T, MLA_HP), qmap), pl.BlockSpec((t, MLA_HP), kmap), pl.BlockSpec((t, MLA_VD), kmap)],
        out_shape=[jax.ShapeDtypeStruct((T, MLA_HEADS * MLA_HP), F32),
                   jax.ShapeDtypeStruct((T, MLA_HEADS * MLA_HP), F32),
                   jax.ShapeDtypeStruct((T, MLA_HEADS * MLA_VD), F32)],
        compiler_params=_cparams(("arbitrary", "arbitrary")),
    )(qf, kf, vf, do16, lse, delta)


MESH = pl.DeviceIdType.MESH
ANY = pl.BlockSpec(memory_space=pl.ANY)
_CHIP_FLIPS = ((1, 0), (0, 1), (1, 1))


def _place():
    return lax.axis_index("x"), lax.axis_index("y"), lax.axis_index("c")


def _other_chip(x, y, k):
    fx, fy = _CHIP_FLIPS[k]
    return ((1 - x) if fx else x), ((1 - y) if fy else y)


def _remote(src, dst, send_sems, recv_sems, k, to):
    return pltpu.make_async_remote_copy(src_ref=src, dst_ref=dst, send_sem=send_sems.at[k], recv_sem=recv_sems.at[k],
                                        device_id=to, device_id_type=MESH)


def _index(*vals):
    return jnp.stack(vals).astype(jnp.int32)


def _half(c, rows):
    return pl.ds(pl.multiple_of(c * rows, 16), rows)


def _gather_weights(parts, name, landed=None):
    n_w = len(parts)
    n_in = n_w if landed is None else 2 * n_w

    def body(*refs):
        ins, outs = refs[:n_w], refs[n_in:n_in + n_w]
        send_sems, recv_sems, local_sems = refs[n_in + n_w:]
        x, y, c = _place()
        j = 2 * x + y
        sibling = (x, y, 1 - c)
        chips = [_other_chip(x, y, k) for k in range(3)]
        pending = []
        for w in range(n_w):
            own = pltpu.make_async_copy(ins[w], outs[w].at[j], local_sems.at[w])
            own.start()
            pending.append(own)
        sent = []
        for w in range(n_w):
            if landed is not None:
                break
            r = _half(c, parts[w].shape[0] // 2)
            for k, (px, py) in enumerate(chips):
                cp = _remote(ins[w].at[r], outs[w].at[j, r], send_sems, recv_sems, 6 * w + k, (px, py, c))
                cp.start()
                sent.append(cp)
        for w in range(n_w):
            r = _half(c, parts[w].shape[0] // 2)
            for k, (px, py) in enumerate(chips):
                blk = outs[w].at[2 * px + py, r]
                if landed is None:
                    _remote(blk, blk, send_sems, recv_sems, 6 * w + k, (px, py, c)).wait_recv()
                cp = _remote(blk, blk, send_sems, recv_sems, 6 * w + 3 + k, sibling)
                cp.start()
                sent.append(cp)
        for w in range(n_w):
            r = _half(1 - c, parts[w].shape[0] // 2)
            for k, (px, py) in enumerate(chips):
                blk = outs[w].at[2 * px + py, r]
                _remote(blk, blk, send_sems, recv_sems, 6 * w + 3 + k, sibling).wait_recv()
        for cp in sent:
            cp.wait_send()
        for cp in pending:
            cp.wait()

    return pl.pallas_call(
        body, name=name, in_specs=[pl.BlockSpec(memory_space=pltpu.VMEM)] * n_w + [ANY] * (n_in - n_w),
        out_specs=[ANY] * n_w,
        out_shape=[jax.ShapeDtypeStruct((N_CHIPS, *p.shape), p.dtype) for p in parts],
        input_output_aliases={} if landed is None else {n_w + w: w for w in range(n_w)},
        scratch_shapes=[pltpu.SemaphoreType.DMA((6 * n_w,)), pltpu.SemaphoreType.DMA((6 * n_w,)),
                        pltpu.SemaphoreType.DMA((n_w,))],
        compiler_params=pltpu.CompilerParams(vmem_limit_bytes=VMEM_LIMIT),
    )(*parts, *(landed or []))


def _swap_halves(gs, name):
    n_w = len(gs)

    def body(*refs):
        g_refs, recv_refs = refs[:n_w], refs[n_w:2 * n_w]
        send_sems, recv_sems = refs[2 * n_w:]
        x, y, c = _place()
        sent = []
        for w in range(n_w):
            for jj in range(N_CHIPS):
                cp = _remote(g_refs[w].at[jj, 1 - c], recv_refs[w].at[jj], send_sems, recv_sems, N_CHIPS * w + jj,
                             (x, y, 1 - c))
                cp.start()
                sent.append(cp)
        for cp in sent:
            cp.wait()

    return pl.pallas_call(
        body, name=name, in_specs=[ANY] * n_w, out_specs=[ANY] * n_w,
        out_shape=[jax.ShapeDtypeStruct((N_CHIPS, *g.shape[2:]), g.dtype) for g in gs],
        scratch_shapes=[pltpu.SemaphoreType.DMA((N_CHIPS * n_w,)), pltpu.SemaphoreType.DMA((N_CHIPS * n_w,))],
    )(*gs)


def _pair_sum(g, recv, core, name):
    _, H, C = recv.shape
    tile = _pick(H, 256)

    def body(c_ref, own_ref, recv_ref, out_ref):
        out_ref[...] = (own_ref[...].astype(F32) + recv_ref[...].astype(F32)).astype(BF16)

    blk = pl.BlockSpec((None, tile, C), lambda jj, i, c: (jj, i, 0))
    return pl.pallas_call(
        body, name=name,
        grid_spec=pltpu.PrefetchScalarGridSpec(
            num_scalar_prefetch=1, grid=(N_CHIPS, H // tile),
            in_specs=[pl.BlockSpec((None, None, tile, C), lambda jj, i, c: (jj, c[0], i, 0)), blk],
            out_specs=blk),
        out_shape=jax.ShapeDtypeStruct((N_CHIPS, H, C), BF16),
        compiler_params=_cparams(("arbitrary", "arbitrary")),
    )(_index(core), g, recv)


def _chip_sum(g, recv, got, chip, core, name):
    _, H, C = recv.shape
    tile = _pick(H, 256)

    def body(s_ref, own_ref, recv_ref, g0_ref, g1_ref, g2_ref, out_ref):
        pair = own_ref[...].astype(F32) + recv_ref[...].astype(F32)
        out_ref[...] = ((pair + g0_ref[...].astype(F32)) + g1_ref[...].astype(F32)) + g2_ref[...].astype(F32)

    def got_spec(k):
        return pl.BlockSpec((None, tile, C), lambda i, s, k=k: (k, i, 0))

    return pl.pallas_call(
        body, name=name,
        grid_spec=pltpu.PrefetchScalarGridSpec(
            num_scalar_prefetch=1, grid=(H // tile,),
            in_specs=[pl.BlockSpec((None, None, tile, C), lambda i, s: (s[0], s[1], i, 0)),
                      pl.BlockSpec((None, tile, C), lambda i, s: (s[0], i, 0)), got_spec(0), got_spec(1), got_spec(2)],
            out_specs=pl.BlockSpec((None, tile, C), lambda i, s: (s[1], i, 0))),
        out_shape=jax.ShapeDtypeStruct((2, H, C), F32),
        compiler_params=_cparams(("arbitrary",)),
    )(_index(chip, core), g, recv, got, got, got)


def _scatter_chips(sums, name):
    n_w = len(sums)

    def body(*refs):
        a_refs, got_refs = refs[:n_w], refs[n_w:2 * n_w]
        send_sems, recv_sems = refs[2 * n_w:]
        x, y, c = _place()
        j = 2 * x + y
        sent = []
        for w in range(n_w):
            for k in range(3):
                px, py = _other_chip(x, y, k)
                pj = 2 * px + py
                cp = _remote(a_refs[w].at[pj], got_refs[w].at[(j - pj + 4) % 4 - 1], send_sems, recv_sems, 3 * w + k,
                             (px, py, c))
                cp.start()
                sent.append(cp)
        for w in range(n_w):
            for k in range(3):
                px, py = _other_chip(x, y, k)
                slot = got_refs[w].at[(2 * px + py - j + 4) % 4 - 1]
                _remote(slot, slot, send_sems, recv_sems, 3 * w + k, (px, py, c)).wait_recv()
        for cp in sent:
            cp.wait_send()

    return pl.pallas_call(
        body, name=name, in_specs=[ANY] * n_w, out_specs=[ANY] * n_w,
        out_shape=[jax.ShapeDtypeStruct((3, *a.shape[1:]), a.dtype) for a in sums],
        scratch_shapes=[pltpu.SemaphoreType.DMA((3 * n_w,)), pltpu.SemaphoreType.DMA((3 * n_w,))],
    )(*sums)


def _share_halves(reds):
    n_w = len(reds)

    def body(*refs):
        out_refs = refs[n_w:2 * n_w]
        send_sems, recv_sems = refs[2 * n_w:]
        x, y, c = _place()
        sent = []
        for w in range(n_w):
            blk = out_refs[w].at[c]
            cp = _remote(blk, blk, send_sems, recv_sems, w, (x, y, 1 - c))
            cp.start()
            sent.append(cp)
        for cp in sent:
            cp.wait()

    return pl.pallas_call(
        body, name="grad_share_halves", in_specs=[ANY] * n_w, out_specs=[ANY] * n_w,
        out_shape=[jax.ShapeDtypeStruct(r.shape, r.dtype) for r in reds],
        input_output_aliases={w: w for w in range(n_w)},
        scratch_shapes=[pltpu.SemaphoreType.DMA((n_w,)), pltpu.SemaphoreType.DMA((n_w,))],
    )(*reds)


def _allsum_small(v, name):
    R, W = v.shape
    n_dev = 8
    vm = pl.BlockSpec(memory_space=pltpu.VMEM)

    def body(v_ref, out_ref, buf, send_sems, recv_sems):
        x, y, c = _place()
        me = 4 * x + 2 * y + c
        buf[me] = v_ref[...]
        sent = []
        for k in range(1, n_dev):
            peer = ((1 - x) if k & 4 else x, (1 - y) if k & 2 else y, (1 - c) if k & 1 else c)
            cp = _remote(v_ref, buf.at[me], send_sems, recv_sems, k - 1, peer)
            cp.start()
            sent.append(cp)
        for cp in sent:
            cp.wait_recv()
        for cp in sent:
            cp.wait_send()
        acc = buf[0]
        for q in range(1, n_dev):
            acc = acc + buf[q]
        out_ref[...] = acc

    return pl.pallas_call(
        body, name=name, in_specs=[vm], out_specs=vm, out_shape=jax.ShapeDtypeStruct((R, W), v.dtype),
        scratch_shapes=[pltpu.VMEM((n_dev, R, W), v.dtype), pltpu.SemaphoreType.DMA((n_dev - 1,)),
                        pltpu.SemaphoreType.DMA((n_dev - 1,))],
    )(v)


HBM = pl.BlockSpec(memory_space=pltpu.HBM)
SEM = pl.BlockSpec(memory_space=pltpu.SEMAPHORE)
_DATAFLOW = pltpu.SideEffectType.DATAFLOW_SIDE_EFFECTING


def _split_start(name, srcs, land_shapes, n_copies, copies, after=()):
    ns, nl = len(srcs), len(land_shapes)
    lands = [lax.empty(s.shape, s.dtype) for s in land_shapes]

    def body(*refs):
        outs = refs[ns + nl + len(after):]
        for cp in copies(refs[:ns], refs[ns:ns + nl], outs[0], outs[1]):
            cp.start()
        outs[-1][...] = jnp.zeros_like(outs[-1])

    sems = pltpu.SemaphoreType.DMA((n_copies,))
    res = pl.pallas_call(
        body, name=name, in_specs=[HBM] * (ns + nl) + [ANY] * len(after),
        out_specs=(SEM, SEM, *[HBM] * (ns + nl), pl.BlockSpec(memory_space=pltpu.VMEM)),
        out_shape=(sems, sems, *[pltpu.HBM(a.shape, a.dtype) for a in srcs],
                   *[pltpu.HBM(s.shape, s.dtype) for s in land_shapes], jax.ShapeDtypeStruct((8, 128), F32)),
        input_output_aliases={i: 2 + i for i in range(ns + nl)},
        compiler_params=pltpu.CompilerParams(has_side_effects=_DATAFLOW),
    )(*[pltpu.with_memory_space_constraint(a, pltpu.HBM) for a in [*srcs, *lands]], *after)
    return res[0], res[1], list(res[2:2 + ns]), list(res[2 + ns:2 + ns + nl]), res[-1]


def _split_wait(name, send_sems, recv_sems, srcs, lands, copies, after=()):
    ns, nl = len(srcs), len(lands)

    def body(*refs):
        for cp in copies(refs[:ns], refs[ns:ns + nl], refs[ns + nl], refs[ns + nl + 1]):
            cp.wait_send()
            cp.wait_recv()

    res = pl.pallas_call(
        body, name=name, in_specs=[HBM] * (ns + nl) + [SEM, SEM] + [ANY] * len(after), out_specs=[HBM] * (ns + nl),
        out_shape=[pltpu.HBM(a.shape, a.dtype) for a in [*srcs, *lands]],
        input_output_aliases={i: i for i in range(ns + nl)},
        compiler_params=pltpu.CompilerParams(has_side_effects=_DATAFLOW),
    )(*srcs, *lands, send_sems, recv_sems, *after)
    return list(res[ns:])


def _gather_copies(rows):
    def copies(src_refs, land_refs, send_sems, recv_sems):
        x, y, c = _place()
        j = 2 * x + y
        out = []
        for w in range(len(src_refs)):
            r = _half(c, rows[w] // 2)
            for k in range(3):
                px, py = _other_chip(x, y, k)
                out.append(_remote(src_refs[w].at[r], land_refs[w].at[j, r], send_sems, recv_sems, 3 * w + k, (px, py, c)))
        return out
    return copies


def _scatter_copies(src_refs, land_refs, send_sems, recv_sems):
    x, y, c = _place()
    j = 2 * x + y
    out = []
    for w in range(len(src_refs)):
        for k in range(3):
            px, py = _other_chip(x, y, k)
            pj = 2 * px + py
            out.append(_remote(src_refs[w].at[pj], land_refs[w].at[(j - pj + 4) % 4 - 1], send_sems, recv_sems, 3 * w + k,
                               (px, py, c)))
    return out


def _reduce_begin(grads, core, tag):
    names = list(grads)
    gs = [grads[k].reshape(N_CHIPS, 2, -1, grads[k].shape[-1]) for k in names]
    recvs = _swap_halves(gs, f"grad_swap_halves_{tag}")
    sums = [_pair_sum(g, r, core, f"pair_sum_{k}") for k, g, r in zip(names, gs, recvs)]
    return names, gs, recvs, sums


def _reduce_end(begun, gots, chip, core):
    names, gs, recvs, _ = begun
    return {k: _chip_sum(g, r, t, chip, core, f"chip_sum_{k}") for k, g, r, t in zip(names, gs, recvs, gots)}


def _got_shapes(sums):
    return [jax.ShapeDtypeStruct((3, *a.shape[1:]), a.dtype) for a in sums]


def _adamw(w, g, m, v, name, layers=1, layer=0, into=None):
    shape = w.shape
    cols = shape[-1]
    w3, m3, v3 = (t.reshape(layers, -1, cols) for t in (w, m, v))
    rows = w3.shape[1]
    tile = _pick(rows, 256) if rows % 8 == 0 else rows
    n_in = 4 + (0 if into is None else 4)

    def body(*refs):
        wv, gv, mv, vv = (r[...] for r in refs[:4])
        g_ref, d_ref, m_ref, v_ref = refs[n_in:]
        m2 = ADAM_B1 * mv + (1.0 - ADAM_B1) * gv
        v2 = ADAM_B2 * vv + (1.0 - ADAM_B2) * jnp.square(gv)
        m_hat = m2 / (1.0 - ADAM_B1 ** ADAM_STEP)
        v_hat = v2 / (1.0 - ADAM_B2 ** ADAM_STEP)
        g_ref[...] = gv
        d_ref[...] = -ADAM_LR * (m_hat / (jnp.sqrt(v_hat) + ADAM_EPS) + ADAM_WD * wv)
        m_ref[...] = m2
        v_ref[...] = v2

    lay = pl.BlockSpec((None, tile, cols), lambda i: (layer, i, 0))
    out = jax.ShapeDtypeStruct((layers, rows, cols), F32)
    res = pl.pallas_call(
        body, name=name, grid=(rows // tile,),
        in_specs=[lay, pl.BlockSpec((tile, cols), lambda i: (i, 0)), lay, lay] + [ANY] * (n_in - 4),
        out_specs=[lay] * 4, out_shape=[out] * 4,
        input_output_aliases={} if into is None else {4 + k: k for k in range(4)},
        compiler_params=_cparams(("arbitrary",)),
    )(w3, g.reshape(rows, cols), m3, v3, *([] if into is None else [t.reshape(layers, rows, cols) for t in into]))
    return tuple(t.reshape(shape) for t in res)


ROW_F32, ROW_BF16 = (D_MODEL, F32), (D_MODEL, BF16)


def _res_norm(acc, h, gain):
    hh = h + acc
    return hh, _rms(hh, gain)


def _dx_norm_bwd(d, w, h, dres, gain, name, **kw):
    def epilogue(acc, hv, dr, g):
        dx, dg = _rms_bwd(hv, acc, g)
        return dr + dx, dr + dx, _colsum(dg)
    return _mm_rows(d, w, tb=True, extras=[h, dres], fulls=[gain], outs=[ROW_F32, ROW_BF16], accs=[((1, D_MODEL), F32)],
                    epilogue=epilogue, name=name, **kw)


def _tail_fwd(h1, hn2, p16, W, i, tag, next_gain=None, target=None):
    a = _mm(hn2, W["mlp_w1"][i], bblk=True, outs=[BF16], name=f"{tag}_mlp_w1",
            epilogue=lambda acc: (jnp.square(jnp.maximum(acc, 0.0)),))
    h2, hn3 = _mm_rows(a, W["mlp_w2"][i], extras=[h1], fulls=[W["ple_norm"][i:i + 1]], outs=[ROW_F32, ROW_BF16],
                       epilogue=_res_norm, name=f"{tag}_mlp_w2")
    gl = _mm(hn3, W["ple_gate_w"][i], name=f"{tag}_ple_gate")
    if target is None:
        def gated(acc, g, h, gain):
            hh = h + _sigmoid(g) * acc
            return hh, acc, _rms(hh, gain)
        h3, pp, hn = _mm_rows(p16[i], W["ple_proj_w"][i], bblk=True, extras=[gl, h2], fulls=[next_gain],
                              outs=[ROW_F32, ROW_F32, ROW_BF16], epilogue=gated, name=f"{tag}_ple_proj")
        return h3, hn, (h1, hn2, a, h2, hn3, gl, pp)

    def gated_loss(acc, g, h, t):
        e = h + _sigmoid(g) * acc - t
        return acc, e * (1.0 / D_MODEL), jnp.full((1, 128), 0.5 / D_MODEL, F32) * jnp.sum(e * e)
    pp, dy, loss = _mm_rows(p16[i], W["ple_proj_w"][i], bblk=True, extras=[gl, h2, target], outs=[ROW_F32, ROW_F32],
                            accs=[((1, 128), F32)], epilogue=gated_loss, name=f"{tag}_ple_proj")
    return dy, loss, (h1, hn2, a, h2, hn3, gl, pp)


def _tail_bwd(dh3, saved, p16, W, i, tag, after=()):
    h1, hn2, a, h2, hn3, gl, pp = saved

    def gate_bwd(d, g, ppv):
        gate = _sigmoid(g)
        return d * gate, d * ppv * gate * (1.0 - gate)

    def dw(kind, name):
        return (kind, 1, 0, None)

    dpp, dgl = _rows(gate_bwd, [dh3, gl, pp], [], [(D_MODEL, BF16), (D_MODEL, BF16)], name=f"{tag}_ple_gate_bwd",
                     after=after)
    d_proj = _mm(p16[i], dpp, ta=True, outs=[BF16], dw=dw("cols", "ple_proj_w"), name=f"{tag}_d_ple_proj")
    d_gate = _mm(hn3, dgl, ta=True, outs=[BF16], dw=dw("rows", "ple_gate_w"), name=f"{tag}_d_ple_gate")
    dh2, dh2_16, d_ple_norm = _dx_norm_bwd(dgl, W["ple_gate_w"][i], h2, dh3, W["ple_norm"][i:i + 1],
                                           f"{tag}_ple_gate_dx")
    d_w2 = _mm(a, dh2_16, ta=True, outs=[BF16], dw=dw("rows", "mlp_w2"), name=f"{tag}_d_mlp_w2")
    dz = _mm(dh2_16, W["mlp_w2"][i], tb=True, extras=[a], outs=[BF16], name=f"{tag}_mlp_w2_dx",
             epilogue=lambda acc, av: (acc * (2.0 * jnp.sqrt(av.astype(F32))),))
    d_w1 = _mm(hn2, dz, ta=True, outs=[BF16], dw=dw("cols", "mlp_w1"), name=f"{tag}_d_mlp_w1")
    dh1, dh1_16, d_mlp_norm = _dx_norm_bwd(dz, W["mlp_w1"][i], h1, dh2, W["mlp_norm"][i:i + 1], f"{tag}_mlp_w1_dx",
                                           bblk=True)
    big = {f"mlp_w1_{i}": d_w1, f"mlp_w2_{i}": d_w2, f"ple_gate_w_{i}": d_gate, f"ple_proj_w_{i}": d_proj}
    return dh1, dh1_16, big, dict(mlp_norm=d_mlp_norm, ple_norm=d_ple_norm)


def _ret_layer_fwd(h0, W, tabs, after=()):
    hn = _rows(lambda x, g: (_rms(x, g),), [h0], [W["mix_norm"][0:1]], [(D_MODEL, BF16)], name="ret_mix_norm",
               after=after)[0]
    proj = _mm(hn, W["ret_w_in"], bblk=True, outs=[BF16], name="ret_w_in")
    out, states = _ret_fwd(proj, tabs, "ret_scan")
    y = _ret_gate(out, proj, W["ret_gn"], "ret_gate")
    h1, hn2 = _mm_rows(y, W["ret_w_out"], extras=[h0], fulls=[W["mlp_norm"][0:1]], outs=[ROW_F32, ROW_BF16],
                       epilogue=_res_norm, name="ret_w_out")
    return h1, hn2, (h0, hn, proj, out, states, y)


def _ret_layer_bwd(dh1, dh1_16, saved, W, tabs, after=(), on_grads=None):
    h0, hn, proj, out, states, y = saved
    d_w_out = _mm(y, dh1_16, ta=True, outs=[BF16], dw=("rows", 1, 0, None), name="d_ret_w_out", after=after)
    dy = _mm(dh1_16, W["ret_w_out"], tb=True, name="ret_w_out_dx", after=after)
    dout, dproj, d_gn = _ret_gate_bwd(out, proj, W["ret_gn"], dy, "ret_gate_bwd")
    dproj = _ret_bwd(proj, states, dout, dproj, tabs, "ret_scan_bwd")
    d_w_in = _mm(hn, dproj, ta=True, outs=[BF16], dw=("cols", 1, 0, None), name="d_ret_w_in")
    big = dict(ret_w_in=d_w_in, ret_w_out=d_w_out)
    later = () if on_grads is None else on_grads(big)
    dh0, _, d_mix = _dx_norm_bwd(dproj, W["ret_w_in"], h0, dh1, W["mix_norm"][0:1], "ret_w_in_dx", bblk=True, tm=256,
                                 after=later)
    return dh0, big, dict(mix_norm=d_mix, ret_gn=d_gn)


def _mla_layer_fwd(h0, hn, W, tabs):
    proj = _mm(hn, W["mla_w_in"], name="mla_w_in")

    def low_rank_norm(pv, gq, gkv):
        return _rms(pv[:, :MLA_Q_RANK], gq), _rms(pv[:, MLA_Q_RANK:MLA_Q_RANK + MLA_KV_RANK], gkv)

    cqn, ckvn = _rows(low_rank_norm, [proj], [W["mla_q_a_norm"], W["mla_kv_a_norm"]],
                      [(MLA_Q_RANK, BF16), (MLA_KV_RANK, BF16)], name="mla_low_rank_norm")
    q = _mm(cqn, W["mla_w_uq"], bblk=True, name="mla_w_uq")
    kv = _mm(ckvn, W["mla_w_ukv"], bblk=True, name="mla_w_ukv")
    qf, kf, vf = _mla_prep(q, kv, proj, W["mla_q_norm"] * (MLA_QKD ** -0.5 * LOG2E), W["mla_k_norm"], tabs, "mla_prep")
    o, lse = _flash_fwd(qf, kf, vf, "mla_flash")
    h1, hn2 = _mm_rows(o, W["mla_w_out"], extras=[h0], fulls=[W["mlp_norm"][1:2]], outs=[ROW_F32, ROW_BF16],
                       epilogue=_res_norm, name="mla_w_out")
    return h1, hn2, (h0, hn, proj, cqn, ckvn, q, kv, qf, kf, vf, o, lse)


def _mla_layer_bwd(dh1, dh1_16, saved, W, tabs):
    h0, hn, proj, cqn, ckvn, q, kv, qf, kf, vf, o, lse = saved
    d_w_out = _mm(o, dh1_16, ta=True, outs=[BF16], dw=("rows", 1, 0, None), name="d_mla_w_out")
    do = _mm(dh1_16, W["mla_w_out"], tb=True, name="mla_w_out_dx")
    delta, do16 = _flash_delta(o, do, "mla_flash_delta")
    dqf, dkf, dvf = _flash_bwd(qf, kf, vf, do16, lse, delta, "mla_flash_bwd")
    dq, dkv, dkr, d_gq, d_gk = _mla_prep_bwd(q, kv, proj, W["mla_q_norm"], W["mla_k_norm"], tabs, dqf, dkf, dvf,
                                             "mla_prep_bwd")
    d_w_uq = _mm(cqn, dq, ta=True, outs=[BF16], dw=("cols", 1, 0, None), name="d_mla_w_uq")
    dcqn = _mm(dq, W["mla_w_uq"], tb=True, bblk=True, name="mla_w_uq_dx")
    d_w_ukv = _mm(ckvn, dkv, ta=True, outs=[BF16], dw=("cols", 1, 0, None), name="d_mla_w_ukv")
    dckvn = _mm(dkv, W["mla_w_ukv"], tb=True, bblk=True, name="mla_w_ukv_dx")

    def low_rank_bwd(pv, dcq, dckv, dkr_v, gq, gkv):
        dxq, dgq = _rms_bwd(pv[:, :MLA_Q_RANK], dcq, gq)
        dxkv, dgkv = _rms_bwd(pv[:, MLA_Q_RANK:MLA_Q_RANK + MLA_KV_RANK], dckv, gkv)
        return jnp.concatenate([dxq, dxkv, dkr_v], axis=-1), _colsum(dgq), _colsum(dgkv)

    dproj, d_gqa, d_gkva = _rows(low_rank_bwd, [proj, dcqn, dckvn, dkr], [W["mla_q_a_norm"], W["mla_kv_a_norm"]],
                                 [(MLA_IN_PAD, BF16)], [((1, MLA_Q_RANK), F32), ((1, MLA_KV_RANK), F32)],
                                 name="mla_low_rank_norm_bwd")
    d_w_in = _mm(hn, dproj, ta=True, outs=[BF16], dw=("rows", 1, 0, None), name="d_mla_w_in")
    dh0, dh0_16, d_mix = _dx_norm_bwd(dproj, W["mla_w_in"], h0, dh1, W["mix_norm"][1:2], "mla_w_in_dx")
    return (dh0, dh0_16, dict(mla_w_in=d_w_in, mla_w_uq=d_w_uq, mla_w_ukv=d_w_ukv, mla_w_out=d_w_out),
            dict(mix_norm=d_mix, mla_q_a_norm=d_gqa, mla_kv_a_norm=d_gkva, mla_q_norm=d_gq, mla_k_norm=d_gk))


def _local_step(x, p16, target, W):
    T = x.shape[0]
    ret_tabs, mla_tabs = _ret_tables(T), _mla_tables(T)
    h1, hn, s_ret = _ret_layer_fwd(x, W, ret_tabs)
    h3, hn, s_tail0 = _tail_fwd(h1, hn, p16, W, 0, "l0", next_gain=W["mix_norm"][1:2])
    h4, hn, s_mla = _mla_layer_fwd(h3, hn, W, mla_tabs)
    dy, loss, s_tail1 = _tail_fwd(h4, hn, p16, W, 1, "l1", target=target)
    dh4, dh4_16, g_t1, n_t1 = _tail_bwd(dy, s_tail1, p16, W, 1, "l1")
    dh3, _, g_mla, n_mla = _mla_layer_bwd(dh4, dh4_16, s_mla, W, mla_tabs)
    dh1, dh1_16, g_t0, n_t0 = _tail_bwd(dh3, s_tail0, p16, W, 0, "l0")
    dx, g_ret, n_ret = _ret_layer_bwd(dh1, dh1_16, s_ret, W, ret_tabs)
    return loss, dx, {**g_ret, **g_t0, **g_mla, **g_t1}, _small_grads(n_ret, n_t0, n_mla, n_t1)


def _loss_head(y, target):
    def fn(yv, tv):
        e = yv - tv
        return e * (1.0 / D_MODEL), jnp.full((1, 128), 0.5 / D_MODEL, F32) * jnp.sum(e * e)
    return _rows(fn, [y, target], [], [(D_MODEL, F32)], [((1, 128), F32)], name="loss_head")


def _small_grads(n_ret, n_t0, n_mla, n_t1):
    return dict(
        mix_norm=jnp.concatenate([n_ret["mix_norm"], n_mla["mix_norm"]], axis=0),
        mlp_norm=jnp.concatenate([n_t0["mlp_norm"], n_t1["mlp_norm"]], axis=0),
        ple_norm=jnp.concatenate([n_t0["ple_norm"], n_t1["ple_norm"]], axis=0),
        ret_gn=n_ret["ret_gn"], mla_q_a_norm=n_mla["mla_q_a_norm"], mla_kv_a_norm=n_mla["mla_kv_a_norm"],
        mla_q_norm=n_mla["mla_q_norm"], mla_k_norm=n_mla["mla_k_norm"])


_ORDER = ("mix_norm", "ret_w_in", "ret_gn", "ret_w_out", "mla_w_in", "mla_q_a_norm", "mla_kv_a_norm", "mla_w_uq",
          "mla_w_ukv", "mla_q_norm", "mla_k_norm", "mla_w_out", "mlp_norm", "mlp_w1", "mlp_w2", "ple_norm",
          "ple_gate_w", "ple_proj_w")
_TWO_LAYER = ("mlp_w1", "mlp_w2", "ple_gate_w", "ple_proj_w")
HEADS_PER_CHIP = MLA_HEADS // N_CHIPS
GAIN_ROWS = 32


def _travel_parts(w):
    uq = jnp.pad(w["mla_w_uq"][0].reshape(MLA_Q_RANK, HEADS_PER_CHIP, MLA_QKD), ((0, 0), (0, 0), (0, MLA_HP - MLA_QKD)))
    parts = {"ret_w_in": w["ret_w_in"][0], "ret_w_out": w["ret_w_out"][0]}
    for k in _TWO_LAYER:
        parts[k + "_0"] = w[k][0]
    parts["mla_w_in"] = jnp.pad(w["mla_w_in"][0], ((0, 0), (0, MLA_IN_PAD - MLA_IN)))
    parts["mla_w_uq"] = uq.reshape(MLA_Q_RANK, HEADS_PER_CHIP * MLA_HP)
    parts["mla_w_ukv"] = w["mla_w_ukv"][0]
    parts["mla_w_out"] = w["mla_w_out"][0]
    for k in _TWO_LAYER:
        parts[k + "_1"] = w[k][1]
    gains = jnp.concatenate([_pad_row(w["ret_gn"]), _pad_row(w["mla_q_a_norm"]), _pad_row(w["mla_kv_a_norm"]),
                             jnp.zeros((GAIN_ROWS - 3, PACK_W), F32)], axis=0)
    return {"gains": gains, **{k: v.astype(BF16) for k, v in parts.items()}}


def _full_weights(full):
    rows = lambda a: a.reshape(-1, a.shape[-1])
    W = {k: full[k] for k in ("ret_w_in", "mla_w_uq", "mla_w_ukv")}
    for k in ("ret_w_out", "mla_w_in", "mla_w_out"):
        W[k] = rows(full[k])
    W["mlp_w1"] = [full["mlp_w1_0"], full["mlp_w1_1"]]
    W["ple_proj_w"] = [full["ple_proj_w_0"], full["ple_proj_w_1"]]
    W["mlp_w2"] = [rows(full["mlp_w2_0"]), rows(full["mlp_w2_1"])]
    W["ple_gate_w"] = [rows(full["ple_gate_w_0"]), rows(full["ple_gate_w_1"])]
    return W


def _shard_grad(name, red, shape):
    if name == "mla_w_in":
        red = red.reshape(-1, MLA_IN_PAD)[:, :MLA_IN]
    elif name == "mla_w_uq":
        red = red.reshape(MLA_Q_RANK, HEADS_PER_CHIP, MLA_HP)[:, :, :MLA_QKD]
    return red.reshape(shape)


def _pad_row(v):
    v = v.reshape(1, -1)
    return jnp.pad(v, ((0, 0), (0, PACK_W - v.shape[1])))


def kernel(x, p, mix_norm, ret_w_in, ret_gn, ret_w_out, mla_w_in, mla_q_a_norm, mla_kv_a_norm, mla_w_uq, mla_w_ukv, mla_q_norm, mla_k_norm, mla_w_out, mlp_norm, mlp_w1, mlp_w2, ple_norm, ple_gate_w, ple_proj_w, loss_target, m_mix_norm, m_ret_w_in, m_ret_gn, m_ret_w_out, m_mla_w_in, m_mla_q_a_norm, m_mla_kv_a_norm, m_mla_w_uq, m_mla_w_ukv, m_mla_q_norm, m_mla_k_norm, m_mla_w_out, m_mlp_norm, m_mlp_w1, m_mlp_w2, m_ple_norm, m_ple_gate_w, m_ple_proj_w, v_mix_norm, v_ret_w_in, v_ret_gn, v_ret_w_out, v_mla_w_in, v_mla_q_a_norm, v_mla_kv_a_norm, v_mla_w_uq, v_mla_w_ukv, v_mla_q_norm, v_mla_k_norm, v_mla_w_out, v_mlp_norm, v_mlp_w1, v_mlp_w2, v_ple_norm, v_ple_gate_w, v_ple_proj_w):
    w = dict(mix_norm=mix_norm, ret_w_in=ret_w_in, ret_gn=ret_gn, ret_w_out=ret_w_out, mla_w_in=mla_w_in,
             mla_q_a_norm=mla_q_a_norm, mla_kv_a_norm=mla_kv_a_norm, mla_w_uq=mla_w_uq, mla_w_ukv=mla_w_ukv,
             mla_q_norm=mla_q_norm, mla_k_norm=mla_k_norm, mla_w_out=mla_w_out, mlp_norm=mlp_norm, mlp_w1=mlp_w1,
             mlp_w2=mlp_w2, ple_norm=ple_norm, ple_gate_w=ple_gate_w, ple_proj_w=ple_proj_w)
    m = dict(mix_norm=m_mix_norm, ret_w_in=m_ret_w_in, ret_gn=m_ret_gn, ret_w_out=m_ret_w_out, mla_w_in=m_mla_w_in,
             mla_q_a_norm=m_mla_q_a_norm, mla_kv_a_norm=m_mla_kv_a_norm, mla_w_uq=m_mla_w_uq, mla_w_ukv=m_mla_w_ukv,
             mla_q_norm=m_mla_q_norm, mla_k_norm=m_mla_k_norm, mla_w_out=m_mla_w_out, mlp_norm=m_mlp_norm,
             mlp_w1=m_mlp_w1, mlp_w2=m_mlp_w2, ple_norm=m_ple_norm, ple_gate_w=m_ple_gate_w, ple_proj_w=m_ple_proj_w)
    v = dict(mix_norm=v_mix_norm, ret_w_in=v_ret_w_in, ret_gn=v_ret_gn, ret_w_out=v_ret_w_out, mla_w_in=v_mla_w_in,
             mla_q_a_norm=v_mla_q_a_norm, mla_kv_a_norm=v_mla_kv_a_norm, mla_w_uq=v_mla_w_uq, mla_w_ukv=v_mla_w_ukv,
             mla_q_norm=v_mla_q_norm, mla_k_norm=v_mla_k_norm, mla_w_out=v_mla_w_out, mlp_norm=v_mlp_norm,
             mlp_w1=v_mlp_w1, mlp_w2=v_mlp_w2, ple_norm=v_ple_norm, ple_gate_w=v_ple_gate_w, ple_proj_w=v_ple_proj_w)
    xi, yi, ci = _place()
    chip = 2 * xi + yi
    n = N_CHIPS

    parts = _travel_parts(w)
    first = ("gains", "ret_w_in", "ret_w_out")
    later = [k for k in parts if k not in first]
    full = dict(zip(first, _gather_weights([parts[k] for k in first], "gather_first")))
    later_copies = _gather_copies([parts[k].shape[0] for k in later])
    g_send, g_recv, later_src, later_land, g_token = _split_start(
        "gather_later_start", [parts[k] for k in later],
        [jax.ShapeDtypeStruct((n, *parts[k].shape), BF16) for k in later], 3 * len(later), later_copies,
        after=[full["ret_w_in"]])
    gains = full["gains"]
    W = dict(mix_norm=mix_norm, mlp_norm=mlp_norm, ple_norm=ple_norm,
             mla_q_norm=jnp.pad(mla_q_norm, ((0, 0), (0, MLA_HP - MLA_QKD))),
             mla_k_norm=jnp.pad(mla_k_norm, ((0, 0), (0, MLA_HP - MLA_QKD))),
             ret_w_in=full["ret_w_in"], ret_w_out=full["ret_w_out"].reshape(-1, D_MODEL),
             ret_gn=gains[:, 0, :RET_HEADS * 128].reshape(n, RET_HEADS, 128).transpose(1, 0, 2).reshape(RET_HEADS, RET_DV),
             mla_q_a_norm=gains[:, 1, :MLA_Q_RANK // n].reshape(1, MLA_Q_RANK),
             mla_kv_a_norm=gains[:, 2, :MLA_KV_RANK // n].reshape(1, MLA_KV_RANK))
    x0, p16, target = x[0], p[:, 0].astype(BF16), loss_target[0]
    T = x0.shape[0]
    ret_tabs, mla_tabs = _ret_tables(T), _mla_tables(T)

    h1, hn, s_ret = _ret_layer_fwd(x0, W, ret_tabs, after=[g_token])
    landed = _split_wait("gather_later_wait", g_send, g_recv, later_src, later_land, later_copies, after=[h1])
    full.update(zip(later, _gather_weights([parts[k] for k in later], "gather_later_finish", landed=landed)))
    W.update(_full_weights(full))
    h3, hn, s_tail0 = _tail_fwd(h1, hn, p16, W, 0, "l0", next_gain=W["mix_norm"][1:2])
    h4, hn, s_mla = _mla_layer_fwd(h3, hn, W, mla_tabs)
    dy, loss, s_tail1 = _tail_fwd(h4, hn, p16, W, 1, "l1", target=target)

    dh4, dh4_16, g_t1, n_t1 = _tail_bwd(dy, s_tail1, p16, W, 1, "l1")
    dh3, _, g_mla, n_mla = _mla_layer_bwd(dh4, dh4_16, s_mla, W, mla_tabs)
    beg_a = _reduce_begin({**g_mla, **g_t1}, ci, "a")
    a_send, a_recv, a_src, a_land, a_token = _split_start(
        "scatter_a_start", beg_a[3], _got_shapes(beg_a[3]), 3 * len(beg_a[3]), _scatter_copies)
    dh1, dh1_16, g_t0, n_t0 = _tail_bwd(dh3, s_tail0, p16, W, 0, "l0", after=[a_token])
    beg_b = _reduce_begin(g_t0, ci, "b")
    b_send, b_recv, b_src, b_land, b_token = _split_start(
        "scatter_b_start", beg_b[3], _got_shapes(beg_b[3]), 3 * len(beg_b[3]), _scatter_copies)
    stage_c = {}

    def start_c(g_ret):
        beg = _reduce_begin(g_ret, ci, "c")
        stage_c["beg"] = beg
        stage_c["st"] = _split_start("scatter_c_start", beg[3], _got_shapes(beg[3]), 3 * len(beg[3]), _scatter_copies)
        return [stage_c["st"][4]]

    dx, _, n_ret = _ret_layer_bwd(dh1, dh1_16, s_ret, W, ret_tabs, after=[b_token], on_grads=start_c)
    got_a = _split_wait("scatter_a_wait", a_send, a_recv, a_src, a_land, _scatter_copies, after=[dx])
    got_b = _split_wait("scatter_b_wait", b_send, b_recv, b_src, b_land, _scatter_copies, after=[dx])
    got_c = _split_wait("scatter_c_wait", *stage_c["st"][:4], _scatter_copies, after=[dx])
    red = {**_reduce_end(beg_a, got_a, chip, ci), **_reduce_end(beg_b, got_b, chip, ci),
           **_reduce_end(stage_c["beg"], got_c, chip, ci)}
    red = dict(zip(red, _share_halves(list(red.values()))))
    gs = _small_grads(n_ret, n_t0, n_mla, n_t1)
    small_g = jnp.concatenate([
        gs["mix_norm"], gs["mlp_norm"], gs["ple_norm"], gs["ret_gn"].reshape(2, PACK_W), _pad_row(gs["mla_q_a_norm"]),
        _pad_row(gs["mla_kv_a_norm"]), _pad_row(gs["mla_q_norm"][:, :MLA_QKD]), _pad_row(gs["mla_k_norm"][:, :MLA_QKD]),
        _pad_row(loss[:, :1]), jnp.zeros((3, PACK_W), F32)], axis=0)
    tot = _allsum_small(small_g, "sum_small_grads")
    gn_all = tot[6:8].reshape(RET_HEADS, n, -1)
    g_small = dict(
        mix_norm=tot[0:2], mlp_norm=tot[2:4], ple_norm=tot[4:6],
        ret_gn=lax.dynamic_index_in_dim(gn_all, chip, axis=1, keepdims=False),
        mla_q_a_norm=lax.dynamic_index_in_dim(tot[8, :MLA_Q_RANK].reshape(n, -1), chip, axis=0, keepdims=True),
        mla_kv_a_norm=lax.dynamic_index_in_dim(tot[9, :MLA_KV_RANK].reshape(n, -1), chip, axis=0, keepdims=True),
        mla_q_norm=tot[10:11, :MLA_QKD], mla_k_norm=tot[11:12, :MLA_QKD])
    loss_out = tot[12, 0]

    outs = []
    for k in _ORDER:
        if k in _TWO_LAYER:
            res = None
            for i in (1, 0):
                res = _adamw(w[k], red[f"{k}_{i}"], m[k], v[k], f"adamw_{k}_{i}", layers=2, layer=i, into=res)
        elif k in red:
            res = _adamw(w[k], _shard_grad(k, red[k], w[k].shape), m[k], v[k], f"adamw_{k}")
        else:
            res = _adamw(w[k], g_small[k], m[k], v[k], f"adamw_{k}")
        outs.append(res)
    return (loss_out, dx[None], *[o[0] for o in outs], *[o[1] for o in outs], *[o[2] for o in outs],
            *[o[3] for o in outs])
```

```python
import functools

import jax
import jax.numpy as jnp
import numpy as np
from jax import lax
from jax.experimental import pallas as pl
from jax.experimental.pallas import tpu as pltpu

F32 = jnp.float32
BF16 = jnp.bfloat16

EPS = 1e-6
D_MODEL = 1024
CHUNK = 64
ROPE_THETA = 10000.0
RET_HEADS = 4
RET_DK = 256
RET_DV = 512
RET_GROUP = 1
RET_BLOCK = 256
MLA_HEADS = 8
MLA_NOPE = 128
MLA_ROPE = 64
MLA_QKD = 192
MLA_VD = 128
MLA_HP = 256
MLA_Q_RANK = 384
MLA_KV_RANK = 256
MLA_IN = 704
MLA_IN_PAD = 768
D_FF = 4096
PLE_DIM = 256
N_CHIPS = 4

ADAM_LR = 0.001
ADAM_B1 = 0.9
ADAM_B2 = 0.999
ADAM_EPS = 1e-08
ADAM_WD = 0.01
ADAM_STEP = 10

VMEM_LIMIT = 56 * 1024 * 1024
PACK_W = 1024
NEG = -1e30
LOG2E = 1.4426950408889634
FLASH_T = 512
FLASH_HEADS = 2
MM_SUB_ROWS = 256


def _cparams(sem=None):
    return pltpu.CompilerParams(dimension_semantics=sem, vmem_limit_bytes=VMEM_LIMIT)


def _pick(dim, pref):
    if dim <= pref:
        return dim
    t = pref
    while dim % t:
        t //= 2
    return t


def _mm(a, b, *, name, ta=False, tb=False, bblk=False, outs=None, extras=(), epilogue=None, dw=None,
        tm=1024, tn=512, after=()):
    if ta:
        K, M = a.shape
    else:
        M, K = a.shape
    if bblk and tb:
        nb, N, Kq = b.shape
        assert nb * Kq == K
    elif bblk:
        nb, Kb, Nq = b.shape
        N = nb * Nq
        assert Kb == K
    else:
        N = b.shape[0] if tb else b.shape[1]
    tn = _pick(Nq if (bblk and not tb) else N, tn)
    if dw is not None and dw[0] == "cols":
        tn = _pick(N // N_CHIPS, tn)
    tm = _pick(M // N_CHIPS if (dw is not None and dw[0] == "rows") else M, tm)
    grid = (M // tm, N // tn)

    a_spec = pl.BlockSpec((K, tm), lambda i, j: (0, i)) if ta else pl.BlockSpec((tm, K), lambda i, j: (i, 0))
    if bblk and tb:
        b_spec = pl.BlockSpec((nb, tn, Kq), lambda i, j: (0, j, 0))
    elif bblk:
        npb = Nq // tn
        b_spec = pl.BlockSpec((None, K, tn), lambda i, j: (j // npb, 0, j % npb))
    elif tb:
        b_spec = pl.BlockSpec((tn, K), lambda i, j: (j, 0))
    else:
        b_spec = pl.BlockSpec((K, tn), lambda i, j: (0, j))
    in_specs = [a_spec, b_spec] + [pl.BlockSpec((tm, tn), lambda i, j: (i, j)) for _ in extras]
    args = [a, b, *extras]
    aliases = {}
    if outs is None:
        outs = [F32]
    if dw is None:
        o_specs = [pl.BlockSpec((tm, tn), lambda i, j: (i, j)) for _ in outs]
        o_shapes = [jax.ShapeDtypeStruct((M, N), dt) for dt in outs]
    else:
        kind, layers, layer, into = dw
        if kind == "cols":
            per = (N // N_CHIPS) // tn
            o_specs = [pl.BlockSpec((None, None, tm, tn), lambda i, j: (j // per, layer, i, j % per))]
            o_shapes = [jax.ShapeDtypeStruct((N_CHIPS, layers, M, N // N_CHIPS), outs[0])]
        else:
            per = (M // N_CHIPS) // tm
            o_specs = [pl.BlockSpec((None, None, tm, tn), lambda i, j: (i // per, layer, i % per, j))]
            o_shapes = [jax.ShapeDtypeStruct((N_CHIPS, layers, M // N_CHIPS, N), outs[0])]
        if into is not None:
            aliases = {len(args): 0}
            in_specs.append(pl.BlockSpec(memory_space=pl.ANY))
            args.append(into)
    for t in after:
        in_specs.append(pl.BlockSpec(memory_space=pl.ANY))
        args.append(t)
    n_e, n_o = len(extras), len(outs)

    sub = _pick(tm, MM_SUB_ROWS)

    def body(a_ref, b_ref, *rest):
        e_refs, o_refs = rest[:n_e], rest[len(rest) - n_o:]
        for r0 in range(0, tm, sub):
            rows = slice(r0, r0 + sub)
            av = (a_ref[:, rows] if ta else a_ref[rows, :]).astype(BF16)
            if bblk and tb:
                acc = _dot_nt(av[:, :Kq], b_ref[0].astype(BF16))
                for s in range(1, nb):
                    acc = acc + _dot_nt(av[:, s * Kq:(s + 1) * Kq], b_ref[s].astype(BF16))
            elif ta:
                acc = _dot_tn(av, b_ref[...].astype(BF16))
            elif tb:
                acc = _dot_nt(av, b_ref[...].astype(BF16))
            else:
                acc = _dot(av, b_ref[...].astype(BF16))
            vals = (acc,) if epilogue is None else epilogue(acc, *[e[rows, :] for e in e_refs])
            for o, v in zip(o_refs, vals):
                o[rows, :] = v.astype(o.dtype)

    res = pl.pallas_call(
        body, name=name, grid=grid, in_specs=in_specs, out_specs=o_specs, out_shape=o_shapes,
        input_output_aliases=aliases, compiler_params=_cparams(("parallel", "arbitrary")),
    )(*args)
    return res[0] if n_o == 1 else res


def _mm_rows(a, b, *, name, epilogue, outs, tb=False, bblk=False, extras=(), fulls=(), accs=(), tm=512, after=()):
    M, K = a.shape
    tm = _pick(M, tm)
    sub = _pick(tm, MM_SUB_ROWS)
    nb = b.shape[0] if bblk else 1
    n_e, n_f, n_o, n_a = len(extras), len(fulls), len(outs), len(accs)
    n_in = 2 + n_e + n_f + len(after)

    def whole(t):
        return pl.BlockSpec(t.shape, lambda i, nd=t.ndim: (0,) * nd)

    in_specs = [pl.BlockSpec((tm, K), lambda i: (i, 0)), whole(b)]
    in_specs += [pl.BlockSpec((tm, e.shape[1]), lambda i: (i, 0)) for e in extras] + [whole(f) for f in fulls]
    in_specs += [pl.BlockSpec(memory_space=pl.ANY) for _ in after]
    out_specs = [pl.BlockSpec((tm, w), lambda i: (i, 0)) for w, _ in outs] + [pl.BlockSpec(s, lambda i: (0, 0)) for s, _ in accs]
    out_shape = [jax.ShapeDtypeStruct((M, w), dt) for w, dt in outs] + [jax.ShapeDtypeStruct(s, dt) for s, dt in accs]

    def body(a_ref, b_ref, *rest):
        e_refs, f_refs = rest[:n_e], rest[n_e:n_e + n_f]
        o_refs, acc_refs = rest[n_in - 2:n_in - 2 + n_o], rest[n_in - 2 + n_o:]
        fv = [f[...] for f in f_refs]
        totals = None
        for r0 in range(0, tm, sub):
            rows = slice(r0, r0 + sub)
            av = a_ref[rows, :].astype(BF16)
            if bblk and tb:
                kq = K // nb
                acc = _dot_nt(av[:, :kq], b_ref[0])
                for s in range(1, nb):
                    acc = acc + _dot_nt(av[:, s * kq:(s + 1) * kq], b_ref[s])
            elif bblk:
                acc = jnp.concatenate([_dot(av, b_ref[s]) for s in range(nb)], axis=-1)
            elif tb:
                acc = _dot_nt(av, b_ref[...])
            else:
                acc = _dot(av, b_ref[...])
            vals = epilogue(acc, *[e[rows, :] for e in e_refs], *fv)
            for o, v in zip(o_refs, vals[:n_o]):
                o[rows, :] = v.astype(o.dtype)
            part = vals[n_o:]
            totals = part if totals is None else [t + p for t, p in zip(totals, part)]
        first_step = pl.program_id(0) == 0
        for o, v in zip(acc_refs, totals):
            @pl.when(first_step)
            def _(o=o, v=v):
                o[...] = v.astype(o.dtype)

            @pl.when(jnp.logical_not(first_step))
            def _(o=o, v=v):
                o[...] += v.astype(o.dtype)

    return pl.pallas_call(
        body, name=name, grid=(M // tm,), in_specs=in_specs, out_specs=out_specs, out_shape=out_shape,
        compiler_params=_cparams(("arbitrary",)),
    )(a, b, *extras, *fulls, *after)


def _rows(fn, rows, fulls, outs, accs=(), *, name, tile=512, after=()):
    first = rows[0][0] if isinstance(rows[0], tuple) else rows[0]
    T = first.shape[0]
    tile = _pick(T, tile)
    in_specs, args = [], []
    for r in rows:
        if isinstance(r, tuple):
            arr, w, cb = r
            in_specs.append(pl.BlockSpec((tile, w), lambda i, cb=cb: (i, cb)))
        else:
            arr = r
            in_specs.append(pl.BlockSpec((tile, arr.shape[1]), lambda i: (i, 0)))
        args.append(arr)
    for f in fulls:
        in_specs.append(pl.BlockSpec(f.shape, lambda i, nd=f.ndim: (0,) * nd))
        args.append(f)
    outs = [o if len(o) == 4 else (*o, o[0], 0) for o in outs]
    out_specs = [pl.BlockSpec((tile, w), lambda i, cb=cb: (i, cb)) for w, _, _, cb in outs]
    out_specs += [pl.BlockSpec(s, lambda i: (0, 0)) for s, _ in accs]
    out_shape = [jax.ShapeDtypeStruct((T, tw), dt) for _, dt, tw, _ in outs]
    out_shape += [jax.ShapeDtypeStruct(s, dt) for s, dt in accs]
    n_in, n_out = len(args), len(outs)
    for t in after:
        in_specs.append(pl.BlockSpec(memory_space=pl.ANY))
        args.append(t)

    def body(*refs):
        vals = fn(*[r[...] for r in refs[:n_in]])
        o_refs = refs[len(args):]
        for o, v in zip(o_refs[:n_out], vals[:n_out]):
            o[...] = v.astype(o.dtype)
        first_step = pl.program_id(0) == 0
        for o, v in zip(o_refs[n_out:], vals[n_out:]):
            @pl.when(first_step)
            def _(o=o, v=v):
                o[...] = v.astype(o.dtype)

            @pl.when(jnp.logical_not(first_step))
            def _(o=o, v=v):
                o[...] += v.astype(o.dtype)

    res = pl.pallas_call(
        body, name=name, grid=(T // tile,), in_specs=in_specs, out_specs=out_specs, out_shape=out_shape,
        compiler_params=_cparams(("arbitrary",)),
    )(*args)
    return res


def _rms(x, g):
    r = lax.rsqrt(jnp.mean(x * x, axis=-1, keepdims=True) + EPS)
    return (x * r) * g


def _rms_bwd(x, dy, g, n=None):
    n = x.shape[-1] if n is None else n
    r = lax.rsqrt(jnp.sum(x * x, axis=-1, keepdims=True) / n + EPS)
    xh = x * r
    dxh = dy * g
    dx = r * (dxh - xh * (jnp.sum(dxh * xh, axis=-1, keepdims=True) / n))
    return dx, dy * xh


def _colsum(v):
    return jnp.sum(v, axis=0, keepdims=True)


def _sigmoid(x):
    return 1.0 / (1.0 + jnp.exp(-x))


def _widen(v, width):
    reps = width // v.shape[1]
    return v if reps == 1 else jnp.concatenate([v] * reps, axis=-1)


def _norm_fwd(h, gain, name):
    return _rows(lambda x, g: (_rms(x, g),), [h], [gain], [(h.shape[1], BF16)], name=name)[0]


def _norm_bwd(h, dhn, gain, dres, name):
    def fn(x, dy, dr, g):
        dx, dg = _rms_bwd(x, dy, g)
        return dr + dx, dr + dx, _colsum(dg)
    d = h.shape[1]
    return _rows(fn, [h, dhn, dres], [gain], [(d, F32), (d, BF16)], [((1, d), F32)], name=name)


def _rope_angles(T, dim):
    inv = (1.0 / (np.float32(ROPE_THETA) ** (np.arange(0, dim, 2, dtype=np.float32) / np.float32(dim)))).astype(np.float32)
    return np.arange(T, dtype=np.float32)[:, None] * inv[None, :]


def _ret_tables(T):
    ang = _rope_angles(T, RET_DK)
    log_gamma = np.log(np.float32(1.0) - np.float32(2.0) ** (-5.0 - np.arange(RET_HEADS, dtype=np.float32)))
    idx = np.arange(RET_BLOCK, dtype=np.float32)
    chunk = np.arange(RET_BLOCK) // CHUNK
    dist = idx[:, None] - idx[None, :]
    seen = np.where(chunk[:, None] == chunk[None, :], np.abs(dist), np.where(chunk[:, None] > chunk[None, :], dist, np.inf))
    intra = np.exp(log_gamma[:, None, None] * seen[None].astype(np.float32))
    qd = np.exp(log_gamma[:, None] * (idx + 1.0))[:, :, None]
    kd = np.exp(log_gamma[:, None] * (RET_BLOCK - 1.0 - idx))[:, :, None]
    cd = np.exp(log_gamma * RET_BLOCK)[:, None, None]
    return tuple(jnp.asarray(t, F32) for t in (np.cos(ang), np.sin(ang), intra, qd, kd, cd))


def _rope_half(x, c, s):
    x1, x2 = x[:, :RET_DK // 2], x[:, RET_DK // 2:]
    return jnp.concatenate([x1 * c - x2 * s, x2 * c + x1 * s], axis=-1)


def _rope_half_bwd(d, c, s):
    d1, d2 = d[:, :RET_DK // 2], d[:, RET_DK // 2:]
    return jnp.concatenate([d1 * c + d2 * s, d2 * c - d1 * s], axis=-1)


def _dot(a, b):
    return lax.dot_general(a, b, (((1,), (0,)), ((), ())), preferred_element_type=F32)


def _dot_nt(a, b):
    return lax.dot_general(a, b, (((1,), (1,)), ((), ())), preferred_element_type=F32)


def _dot_tn(a, b):
    return lax.dot_general(a, b, (((0,), (0,)), ((), ())), preferred_element_type=F32)


def _ret_specs(T, tb, rev):
    nj = T // tb
    jj = (lambda j: nj - 1 - j) if rev else (lambda j: j)
    g = RET_GROUP
    kq = RET_HEADS // g
    vq = 2 * RET_HEADS * RET_DK // (g * RET_DV)
    return dict(
        q=pl.BlockSpec((tb, g * RET_DK), lambda h, j: (jj(j), h)),
        k=pl.BlockSpec((tb, g * RET_DK), lambda h, j: (jj(j), kq + h)),
        v=pl.BlockSpec((tb, g * RET_DV), lambda h, j: (jj(j), vq + h)),
        tab=pl.BlockSpec((tb, RET_DK // 2), lambda h, j: (jj(j), 0)),
        intra=pl.BlockSpec((g, RET_BLOCK, RET_BLOCK), lambda h, j: (h, 0, 0)),
        dec=pl.BlockSpec((g, RET_BLOCK, 1), lambda h, j: (h, 0, 0)),
        cd=pl.BlockSpec((g, 1, 1), lambda h, j: (h, 0, 0)),
        o=pl.BlockSpec((tb, g * RET_DV), lambda h, j: (jj(j), h)),
        s=pl.BlockSpec((g, tb // RET_BLOCK, RET_DK, RET_DV), lambda h, j: (h, jj(j), 0, 0)),
    )


def _ret_fwd(proj, tabs, name):
    T = proj.shape[0]
    cos, sin, intra, qd, kd, cd = tabs
    tb = _pick(T, 512)
    cps = tb // RET_BLOCK
    sp = _ret_specs(T, tb, False)
    scale = RET_DK ** -0.5

    def body(q_ref, k_ref, v_ref, cos_ref, sin_ref, intra_ref, qd_ref, kd_ref, cd_ref, o_ref, s_ref, state):
        @pl.when(pl.program_id(1) == 0)
        def _():
            state[...] = jnp.zeros_like(state)

        for c in range(cps):
            rows = pl.ds(c * RET_BLOCK, RET_BLOCK)
            co, si = cos_ref[rows, :], sin_ref[rows, :]
            for h in range(RET_GROUP):
                hk, hv = slice(h * RET_DK, (h + 1) * RET_DK), slice(h * RET_DV, (h + 1) * RET_DV)
                q = _rope_half(q_ref[rows, hk].astype(F32), co, si)
                k = _rope_half(k_ref[rows, hk].astype(F32), co, si) * scale
                vb = v_ref[rows, hv].astype(BF16)
                st = state[h]
                sb = st.astype(BF16)
                s_ref[h, c] = sb
                sc = _dot_nt(q.astype(BF16), k.astype(BF16)) * intra_ref[h]
                inner = _dot(sc.astype(BF16), vb)
                cross = _dot((q * qd_ref[h]).astype(BF16), sb)
                o_ref[rows, hv] = inner + cross
                state[h] = st * cd_ref[h] + _dot_tn((k * kd_ref[h]).astype(BF16), vb)

    return pl.pallas_call(
        body, name=name, grid=(RET_HEADS // RET_GROUP, T // tb),
        in_specs=[sp["q"], sp["k"], sp["v"], sp["tab"], sp["tab"], sp["intra"], sp["dec"], sp["dec"], sp["cd"]],
        out_specs=[sp["o"], sp["s"]],
        out_shape=[jax.ShapeDtypeStruct((T, RET_HEADS * RET_DV), F32),
                   jax.ShapeDtypeStruct((RET_HEADS, T // RET_BLOCK, RET_DK, RET_DV), BF16)],
        scratch_shapes=[pltpu.VMEM((RET_GROUP, RET_DK, RET_DV), F32)],
        compiler_params=_cparams(("arbitrary", "arbitrary")),
    )(proj, proj, proj, cos, sin, intra, qd, kd, cd)


def _ret_bwd(proj, states, dout, dproj, tabs, name):
    assert RET_GROUP == 1
    T = proj.shape[0]
    cos, sin, intra, qd, kd, cd = tabs
    tb = _pick(T, 512)
    cps = tb // RET_BLOCK
    nj = T // tb
    sp = _ret_specs(T, tb, True)
    scale = RET_DK ** -0.5
    k0, v0 = RET_HEADS * RET_DK, 2 * RET_HEADS * RET_DK

    def body(q_ref, k_ref, v_ref, cos_ref, sin_ref, intra_ref, qd_ref, kd_ref, cd_ref, s_ref, do_ref, _dproj_in,
             out_ref, dq_s, dk_s, dv_s, sems, dstate):
        head, j = pl.program_id(0), pl.program_id(1)
        step = head * nj + j
        slot = step % 2
        dq_ref, dk_ref, dv_ref = dq_s.at[slot], dk_s.at[slot], dv_s.at[slot]

        @pl.when(j == 0)
        def _():
            dstate[...] = jnp.zeros_like(dstate)

        for c in reversed(range(cps)):
            rows = pl.ds(c * RET_BLOCK, RET_BLOCK)
            co, si = cos_ref[rows, :], sin_ref[rows, :]
            for h in range(RET_GROUP):
                hk, hv = slice(h * RET_DK, (h + 1) * RET_DK), slice(h * RET_DV, (h + 1) * RET_DV)
                q = _rope_half(q_ref[rows, hk].astype(F32), co, si)
                k = _rope_half(k_ref[rows, hk].astype(F32), co, si) * scale
                qb, kb = q.astype(BF16), k.astype(BF16)
                vb = v_ref[rows, hv].astype(BF16)
                dob = do_ref[rows, hv].astype(BF16)
                sb = s_ref[h, c]
                ia = intra_ref[h]
                pb = (_dot_nt(qb, kb) * ia).astype(BF16)
                dsn = dstate[h]
                dsb = dsn.astype(BF16)
                kdk = (k * kd_ref[h]).astype(BF16)
                qdq = (q * qd_ref[h]).astype(BF16)
                dv = _dot_tn(pb, dob) + _dot(kdk, dsb)
                dpb = (_dot_nt(dob, vb) * ia).astype(BF16)
                dq = _dot(dpb, kb) + _dot_nt(dob, sb) * qd_ref[h]
                dk = _dot_tn(dpb, qb) + _dot_nt(vb, dsb) * kd_ref[h]
                dstate[h] = dsn * cd_ref[h] + _dot_tn(qdq, dob)
                dq_ref[rows, hk] = _rope_half_bwd(dq, co, si).astype(BF16)
                dk_ref[rows, hk] = _rope_half_bwd(dk * scale, co, si).astype(BF16)
                dv_ref[rows, hv] = dv.astype(BF16)

        def copies(sl):
            r = pl.ds(pl.multiple_of((nj - 1 - j) * tb, tb), tb)
            cols = lambda first, w: pl.ds(pl.multiple_of(first + head * w, 128), w)
            return [pltpu.make_async_copy(dq_s.at[sl], out_ref.at[r, cols(0, RET_DK)], sems.at[sl, 0]),
                    pltpu.make_async_copy(dk_s.at[sl], out_ref.at[r, cols(k0, RET_DK)], sems.at[sl, 1]),
                    pltpu.make_async_copy(dv_s.at[sl], out_ref.at[r, cols(v0, RET_DV)], sems.at[sl, 2])]

        @pl.when(step > 0)
        def _():
            for cp in copies(1 - slot):
                cp.wait()

        for cp in copies(slot):
            cp.start()

        @pl.when(step == RET_HEADS * nj - 1)
        def _():
            for cp in copies(slot):
                cp.wait()

    return pl.pallas_call(
        body, name=name, grid=(RET_HEADS, nj),
        in_specs=[sp["q"], sp["k"], sp["v"], sp["tab"], sp["tab"], sp["intra"], sp["dec"], sp["dec"], sp["cd"],
                  sp["s"], sp["o"], pl.BlockSpec(memory_space=pl.ANY)],
        out_specs=pl.BlockSpec(memory_space=pl.ANY), out_shape=jax.ShapeDtypeStruct(dproj.shape, dproj.dtype),
        input_output_aliases={11: 0},
        scratch_shapes=[pltpu.VMEM((2, tb, RET_DK), BF16), pltpu.VMEM((2, tb, RET_DK), BF16),
                        pltpu.VMEM((2, tb, RET_DV), BF16), pltpu.SemaphoreType.DMA((2, 3)),
                        pltpu.VMEM((RET_GROUP, RET_DK, RET_DV), F32)],
        compiler_params=_cparams(("arbitrary", "arbitrary")),
    )(proj, proj, proj, cos, sin, intra, qd, kd, cd, states, dout, dproj)


def _ret_gate(out, proj, gn, name):
    def fn(o, g, *gains):
        g = g.astype(F32)
        parts = [_rms(o[:, h * RET_DV:(h + 1) * RET_DV], gains[h]) for h in range(RET_HEADS)]
        return (g * _sigmoid(g) * jnp.concatenate(parts, axis=-1),)
    w = RET_HEADS * RET_DV
    return _rows(fn, [out, (proj, w, 2)], [gn[h:h + 1] for h in range(RET_HEADS)], [(w, BF16)], name=name)[0]


def _ret_gate_bwd(out, proj, gn, dy, name):
    def fn(o, g, d, *gains):
        g = g.astype(F32)
        sg = _sigmoid(g)
        silu = g * sg
        dsilu = sg * (1.0 + g * (1.0 - sg))
        dos, dgs = [], []
        row = lax.broadcasted_iota(jnp.int32, (RET_HEADS, RET_DV), 0)
        dgn = jnp.zeros((RET_HEADS, RET_DV), F32)
        for h in range(RET_HEADS):
            sl = slice(h * RET_DV, (h + 1) * RET_DV)
            oh = o[:, sl]
            dgs.append(d[:, sl] * _rms(oh, gains[h]) * dsilu[:, sl])
            dx, dg = _rms_bwd(oh, d[:, sl] * silu[:, sl], gains[h])
            dos.append(dx)
            dgn = dgn + jnp.where(row == h, _colsum(dg), 0.0)
        return jnp.concatenate(dos, axis=-1), jnp.concatenate(dgs, axis=-1), dgn
    w = RET_HEADS * RET_DV
    return _rows(fn, [out, (proj, w, 2), dy], [gn[h:h + 1] for h in range(RET_HEADS)],
                 [(w, BF16), (w, BF16, proj.shape[1], 2)], [((RET_HEADS, RET_DV), F32)], name=name, tile=128)


def _mla_tables(T):
    ang = _rope_angles(T, MLA_ROPE)
    c, s = np.cos(ang), np.sin(ang)
    z32, z64 = np.zeros((T, 32), np.float32), np.zeros((T, 64), np.float32)
    cos_t = np.concatenate([c, c, z64], axis=1)
    sin_a = np.concatenate([-s, z32, z64], axis=1)
    sin_b = np.concatenate([z32, s, z64], axis=1)
    return tuple(jnp.asarray(t, F32) for t in (cos_t, sin_a, sin_b))


def _rope_blk(x, ct, sa, sb):
    return x * ct + pltpu.roll(x, 96, 1) * sa + pltpu.roll(x, 32, 1) * sb


def _rope_blk_bwd(d, ct, sa, sb):
    return d * ct + pltpu.roll(d * sa, 32, 1) + pltpu.roll(d * sb, 96, 1)


def _head_norm(x, gain):
    r = lax.rsqrt(jnp.sum(x * x, axis=-1, keepdims=True) / MLA_QKD + EPS)
    return (x * r) * gain


def _mla_prep(q, kv, proj, gq, gk, tabs, name):
    def fn(qv, kvv, kr, ct, sa, sb, gqv, gkv):
        qs, ks, vs = [], [], []
        for h in range(MLA_HEADS):
            b = h * MLA_HP
            y = _head_norm(qv[:, b:b + MLA_HP], gqv)
            qs += [y[:, :128], _rope_blk(y[:, 128:], ct, sa, sb)]
            y = _head_norm(jnp.concatenate([kvv[:, b:b + 128], kr], axis=-1), gkv)
            ks += [y[:, :128], _rope_blk(y[:, 128:], ct, sa, sb)]
            vs.append(kvv[:, b + 128:b + 256])
        return jnp.concatenate(qs, axis=-1), jnp.concatenate(ks, axis=-1), jnp.concatenate(vs, axis=-1)
    w = MLA_HEADS * MLA_HP
    return _rows(fn, [q, kv, (proj, 128, 5), *tabs], [gq, gk],
                 [(w, BF16), (w, BF16), (MLA_HEADS * MLA_VD, BF16)], name=name, tile=128)


def _mla_prep_bwd(q, kv, proj, gq, gk, tabs, dqf, dkf, dvf, name):
    def fn(qv, kvv, kr, ct, sa, sb, dqv, dkv, dvv, gqv, gkv):
        dqs, dkvs = [], []
        dkr = jnp.zeros_like(kr)
        dgq = jnp.zeros((1, MLA_HP), F32)
        dgk = jnp.zeros((1, MLA_HP), F32)
        for h in range(MLA_HEADS):
            b = h * MLA_HP
            dy = jnp.concatenate([dqv[:, b:b + 128], _rope_blk_bwd(dqv[:, b + 128:b + 256], ct, sa, sb)], axis=-1)
            dx, dg = _rms_bwd(qv[:, b:b + MLA_HP], dy, gqv, MLA_QKD)
            dqs.append(dx)
            dgq = dgq + _colsum(dg)
            dy = jnp.concatenate([dkv[:, b:b + 128], _rope_blk_bwd(dkv[:, b + 128:b + 256], ct, sa, sb)], axis=-1)
            dx, dg = _rms_bwd(jnp.concatenate([kvv[:, b:b + 128], kr], axis=-1), dy, gkv, MLA_QKD)
            dkvs += [dx[:, :128], dvv[:, h * MLA_VD:(h + 1) * MLA_VD]]
            dkr = dkr + dx[:, 128:]
            dgk = dgk + _colsum(dg)
        return jnp.concatenate(dqs, axis=-1), jnp.concatenate(dkvs, axis=-1), dkr, dgq, dgk
    w = MLA_HEADS * MLA_HP
    return _rows(fn, [q, kv, (proj, 128, 5), *tabs, dqf, dkf, dvf], [gq, gk],
                 [(w, BF16), (w, BF16), (128, F32)], [((1, MLA_HP), F32), ((1, MLA_HP), F32)], name=name, tile=128)


def _chunk_mask(qi, ki, tq, tk):
    shift = CHUNK.bit_length() - 1
    rq = lax.shift_right_arithmetic(qi * tq + lax.broadcasted_iota(jnp.int32, (tq, tk), 0), shift)
    ck = lax.shift_right_arithmetic(ki * tk + lax.broadcasted_iota(jnp.int32, (tq, tk), 1), shift)
    return ck <= rq


def _flash_fwd(qf, kf, vf, name):
    T = qf.shape[0]
    t = _pick(T, FLASH_T)
    n = T // t
    scale = MLA_QKD ** -0.5

    g = FLASH_HEADS

    def body(q_ref, k_ref, v_ref, o_ref, lse_ref, m_s, l_s, acc):
        qi = pl.program_id(1)
        m_s[...] = jnp.full_like(m_s, NEG)
        l_s[...] = jnp.zeros_like(l_s)
        acc[...] = jnp.zeros_like(acc)

        def step(kb, masked):
            rows = pl.ds(pl.multiple_of(kb * t, t), t)
            for h in range(g):
                hq, hv = slice(h * MLA_HP, (h + 1) * MLA_HP), slice(h * MLA_VD, (h + 1) * MLA_VD)
                s = _dot_nt(q_ref[:, hq], k_ref[rows, hq])
                if masked:
                    s = jnp.where(_chunk_mask(0, 0, t, t), s, NEG)
                m_prev = m_s[:, hv]
                m_new = jnp.maximum(m_prev, jnp.max(s, axis=-1, keepdims=True))
                alpha = jnp.exp2(m_prev - m_new)
                p = jnp.exp2(s - _widen(m_new, t))
                l_s[:, hv] = alpha * l_s[:, hv] + sum(p[:, i * 128:(i + 1) * 128] for i in range(t // 128))
                acc[:, hv] = acc[:, hv] * alpha + _dot(p.astype(BF16), v_ref[rows, hv])
                m_s[:, hv] = m_new

        @pl.loop(0, qi)
        def _(kb):
            step(kb, False)

        step(qi, True)
        for h in range(g):
            hv = slice(h * MLA_VD, (h + 1) * MLA_VD)
            l = jnp.sum(l_s[:, hv], axis=-1, keepdims=True)
            o_ref[:, hv] = acc[:, hv] / l
            lse_ref[:, hv] = m_s[:, hv] + jnp.log2(l)

    qmap = lambda h, i: (i, h)
    kmap = lambda h, i: (0, h)
    vec = pltpu.VMEM((t, g * MLA_VD), F32)
    return pl.pallas_call(
        body, name=name, grid=(MLA_HEADS // g, n),
        in_specs=[pl.BlockSpec((t, g * MLA_HP), qmap), pl.BlockSpec((T, g * MLA_HP), kmap),
                  pl.BlockSpec((T, g * MLA_VD), kmap)],
        out_specs=[pl.BlockSpec((t, g * MLA_VD), qmap), pl.BlockSpec((t, g * MLA_VD), qmap)],
        out_shape=[jax.ShapeDtypeStruct((T, MLA_HEADS * MLA_VD), F32),
                   jax.ShapeDtypeStruct((T, MLA_HEADS * MLA_VD), F32)],
        scratch_shapes=[vec, vec, vec],
        compiler_params=_cparams(("parallel", "arbitrary")),
    )(qf, kf, vf)


def _flash_delta(o, do, name):
    def fn(ov, dv):
        parts = []
        for h in range(MLA_HEADS):
            sl = slice(h * MLA_VD, (h + 1) * MLA_VD)
            d = jnp.sum(dv[:, sl] * ov[:, sl], axis=-1, keepdims=True)
            parts.append(jnp.broadcast_to(d, (d.shape[0], MLA_VD)))
        return jnp.concatenate(parts, axis=-1), dv
    w = MLA_HEADS * MLA_VD
    return _rows(fn, [o, do], [], [(w, F32), (w, BF16)], name=name)


def _flash_bwd(qf, kf, vf, do16, lse, delta, name):
    T = qf.shape[0]
    t = _pick(T, FLASH_T)
    n = T // t
    scale = MLA_QKD ** -0.5

    def body(q_ref, k_ref, v_ref, do_ref, lse_ref, dl_ref, dq_ref, dk_ref, dv_ref):
        kb = pl.program_id(1)

        @pl.when(kb == 0)
        def _():
            dq_ref[...] = jnp.zeros_like(dq_ref)

        dk_ref[...] = jnp.zeros_like(dk_ref)
        dv_ref[...] = jnp.zeros_like(dv_ref)
        k, v = k_ref[...], v_ref[...]

        def step(qb, masked):
            rows = pl.ds(pl.multiple_of(qb * t, t), t)
            q, dob = q_ref[rows, :], do_ref[rows, :]
            s = _dot_nt(q, k)
            if masked:
                s = jnp.where(_chunk_mask(0, 0, t, t), s, NEG)
            p = jnp.exp2(s - _widen(lse_ref[rows, :], t))
            ds = (p * (_dot_nt(dob, v) - _widen(dl_ref[rows, :], t))).astype(BF16)
            dv_ref[...] += _dot_tn(p.astype(BF16), dob)
            dk_ref[...] += _dot_tn(ds, q)
            dq_ref[rows, :] += _dot(ds, k)

        step(kb, True)

        @pl.loop(kb + 1, n)
        def _(qb):
            step(qb, False)

        dk_ref[...] = dk_ref[...] * (1.0 / LOG2E)

        @pl.when(kb == n - 1)
        def _():
            dq_ref[...] = dq_ref[...] * scale

    qmap = lambda h, j: (0, h)
    kmap = lambda h, j: (j, h)
    return pl.pallas_call(
        body, name=name, grid=(MLA_HEADS, n),
        in_specs=[pl.BlockSpec((T, MLA_HP), qmap), pl.BlockSpec((t, MLA_HP), kmap), pl.BlockSpec((t, MLA_VD), kmap),
                  pl.BlockSpec((T, MLA_VD), qmap), pl.BlockSpec((T, MLA_VD), qmap), pl.BlockSpec((T, MLA_VD), qmap)],
        out_specs=[pl.BlockSpec((T, MLA_HP), qmap), pl.BlockSpec((t, MLA_HP), kmap), pl.BlockSpec((t, MLA_VD), kmap)],
        out_shape=[jax.ShapeDtypeStruct((T, MLA_HEADS * MLA_HP), F32),
                   jax.ShapeDtypeStruct((T, MLA_HEADS * MLA_HP), F32),
                   jax.ShapeDtypeStruct((T, MLA_HEADS * MLA_VD), F32)],
        compiler_params=_cparams(("arbitrary", "arbitrary")),
    )(qf, kf, vf, do16, lse, delta)


MESH = pl.DeviceIdType.MESH
ANY = pl.BlockSpec(memory_space=pl.ANY)
_CHIP_FLIPS = ((1, 0), (0, 1), (1, 1))


def _place():
    return lax.axis_index("x"), lax.axis_index("y"), lax.axis_index("c")


def _other_chip(x, y, k):
    fx, fy = _CHIP_FLIPS[k]
    return ((1 - x) if fx else x), ((1 - y) if fy else y)


def _remote(src, dst, send_sems, recv_sems, k, to):
    return pltpu.make_async_remote_copy(src_ref=src, dst_ref=dst, send_sem=send_sems.at[k], recv_sem=recv_sems.at[k],
                                        device_id=to, device_id_type=MESH)


def _index(*vals):
    return jnp.stack(vals).astype(jnp.int32)


def _half(c, rows):
    return pl.ds(pl.multiple_of(c * rows, 16), rows)


def _gather_weights(parts, name, landed=None):
    n_w = len(parts)
    n_in = n_w if landed is None else 2 * n_w

    def body(*refs):
        ins, outs = refs[:n_w], refs[n_in:n_in + n_w]
        send_sems, recv_sems, local_sems = refs[n_in + n_w:]
        x, y, c = _place()
        j = 2 * x + y
        sibling = (x, y, 1 - c)
        chips = [_other_chip(x, y, k) for k in range(3)]
        pending = []
        for w in range(n_w):
            own = pltpu.make_async_copy(ins[w], outs[w].at[j], local_sems.at[w])
            own.start()
            pending.append(own)
        sent = []
        for w in range(n_w):
            if landed is not None:
                break
            r = _half(c, parts[w].shape[0] // 2)
            for k, (px, py) in enumerate(chips):
                cp = _remote(ins[w].at[r], outs[w].at[j, r], send_sems, recv_sems, 6 * w + k, (px, py, c))
                cp.start()
                sent.append(cp)
        for w in range(n_w):
            r = _half(c, parts[w].shape[0] // 2)
            for k, (px, py) in enumerate(chips):
                blk = outs[w].at[2 * px + py, r]
                if landed is None:
                    _remote(blk, blk, send_sems, recv_sems, 6 * w + k, (px, py, c)).wait_recv()
                cp = _remote(blk, blk, send_sems, recv_sems, 6 * w + 3 + k, sibling)
                cp.start()
                sent.append(cp)
        for w in range(n_w):
            r = _half(1 - c, parts[w].shape[0] // 2)
            for k, (px, py) in enumerate(chips):
                blk = outs[w].at[2 * px + py, r]
                _remote(blk, blk, send_sems, recv_sems, 6 * w + 3 + k, sibling).wait_recv()
        for cp in sent:
            cp.wait_send()
        for cp in pending:
            cp.wait()

    return pl.pallas_call(
        body, name=name, in_specs=[pl.BlockSpec(memory_space=pltpu.VMEM)] * n_w + [ANY] * (n_in - n_w),
        out_specs=[ANY] * n_w,
        out_shape=[jax.ShapeDtypeStruct((N_CHIPS, *p.shape), p.dtype) for p in parts],
        input_output_aliases={} if landed is None else {n_w + w: w for w in range(n_w)},
        scratch_shapes=[pltpu.SemaphoreType.DMA((6 * n_w,)), pltpu.SemaphoreType.DMA((6 * n_w,)),
                        pltpu.SemaphoreType.DMA((n_w,))],
        compiler_params=pltpu.CompilerParams(vmem_limit_bytes=VMEM_LIMIT),
    )(*parts, *(landed or []))


def _swap_halves(gs, name):
    n_w = len(gs)

    def body(*refs):
        g_refs, recv_refs = refs[:n_w], refs[n_w:2 * n_w]
        send_sems, recv_sems = refs[2 * n_w:]
        x, y, c = _place()
        sent = []
        for w in range(n_w):
            for jj in range(N_CHIPS):
                cp = _remote(g_refs[w].at[jj, 1 - c], recv_refs[w].at[jj], send_sems, recv_sems, N_CHIPS * w + jj,
                             (x, y, 1 - c))
                cp.start()
                sent.append(cp)
        for cp in sent:
            cp.wait()

    return pl.pallas_call(
        body, name=name, in_specs=[ANY] * n_w, out_specs=[ANY] * n_w,
        out_shape=[jax.ShapeDtypeStruct((N_CHIPS, *g.shape[2:]), g.dtype) for g in gs],
        scratch_shapes=[pltpu.SemaphoreType.DMA((N_CHIPS * n_w,)), pltpu.SemaphoreType.DMA((N_CHIPS * n_w,))],
    )(*gs)


def _pair_sum(g, recv, core, name):
    _, H, C = recv.shape
    tile = _pick(H, 256)

    def body(c_ref, own_ref, recv_ref, out_ref):
        out_ref[...] = (own_ref[...].astype(F32) + recv_ref[...].astype(F32)).astype(BF16)

    blk = pl.BlockSpec((None, tile, C), lambda jj, i, c: (jj, i, 0))
    return pl.pallas_call(
        body, name=name,
        grid_spec=pltpu.PrefetchScalarGridSpec(
            num_scalar_prefetch=1, grid=(N_CHIPS, H // tile),
            in_specs=[pl.BlockSpec((None, None, tile, C), lambda jj, i, c: (jj, c[0], i, 0)), blk],
            out_specs=blk),
        out_shape=jax.ShapeDtypeStruct((N_CHIPS, H, C), BF16),
        compiler_params=_cparams(("arbitrary", "arbitrary")),
    )(_index(core), g, recv)


def _chip_sum(g, recv, got, chip, core, name):
    _, H, C = recv.shape
    tile = _pick(H, 256)

    def body(s_ref, own_ref, recv_ref, g0_ref, g1_ref, g2_ref, out_ref):
        pair = own_ref[...].astype(F32) + recv_ref[...].astype(F32)
        out_ref[...] = ((pair + g0_ref[...].astype(F32)) + g1_ref[...].astype(F32)) + g2_ref[...].astype(F32)

    def got_spec(k):
        return pl.BlockSpec((None, tile, C), lambda i, s, k=k: (k, i, 0))

    return pl.pallas_call(
        body, name=name,
        grid_spec=pltpu.PrefetchScalarGridSpec(
            num_scalar_prefetch=1, grid=(H // tile,),
            in_specs=[pl.BlockSpec((None, None, tile, C), lambda i, s: (s[0], s[1], i, 0)),
                      pl.BlockSpec((None, tile, C), lambda i, s: (s[0], i, 0)), got_spec(0), got_spec(1), got_spec(2)],
            out_specs=pl.BlockSpec((None, tile, C), lambda i, s: (s[1], i, 0))),
        out_shape=jax.ShapeDtypeStruct((2, H, C), F32),
        compiler_params=_cparams(("arbitrary",)),
    )(_index(chip, core), g, recv, got, got, got)


def _scatter_chips(sums, name):
    n_w = len(sums)

    def body(*refs):
        a_refs, got_refs = refs[:n_w], refs[n_w:2 * n_w]
        send_sems, recv_sems = refs[2 * n_w:]
        x, y, c = _place()
        j = 2 * x + y
        sent = []
        for w in range(n_w):
            for k in range(3):
                px, py = _other_chip(x, y, k)
                pj = 2 * px + py
                cp = _remote(a_refs[w].at[pj], got_refs[w].at[(j - pj + 4) % 4 - 1], send_sems, recv_sems, 3 * w + k,
                             (px, py, c))
                cp.start()
                sent.append(cp)
        for w in range(n_w):
            for k in range(3):
                px, py = _other_chip(x, y, k)
                slot = got_refs[w].at[(2 * px + py - j + 4) % 4 - 1]
                _remote(slot, slot, send_sems, recv_sems, 3 * w + k, (px, py, c)).wait_recv()
        for cp in sent:
            cp.wait_send()

    return pl.pallas_call(
        body, name=name, in_specs=[ANY] * n_w, out_specs=[ANY] * n_w,
        out_shape=[jax.ShapeDtypeStruct((3, *a.shape[1:]), a.dtype) for a in sums],
        scratch_shapes=[pltpu.SemaphoreType.DMA((3 * n_w,)), pltpu.SemaphoreType.DMA((3 * n_w,))],
    )(*sums)


def _share_halves(reds):
    n_w = len(reds)

    def body(*refs):
        out_refs = refs[n_w:2 * n_w]
        send_sems, recv_sems = refs[2 * n_w:]
        x, y, c = _place()
        sent = []
        for w in range(n_w):
            blk = out_refs[w].at[c]
            cp = _remote(blk, blk, send_sems, recv_sems, w, (x, y, 1 - c))
            cp.start()
            sent.append(cp)
        for cp in sent:
            cp.wait()

    return pl.pallas_call(
        body, name="grad_share_halves", in_specs=[ANY] * n_w, out_specs=[ANY] * n_w,
        out_shape=[jax.ShapeDtypeStruct(r.shape, r.dtype) for r in reds],
        input_output_aliases={w: w for w in range(n_w)},
        scratch_shapes=[pltpu.SemaphoreType.DMA((n_w,)), pltpu.SemaphoreType.DMA((n_w,))],
    )(*reds)


def _allsum_small(v, name):
    R, W = v.shape
    n_dev = 8
    vm = pl.BlockSpec(memory_space=pltpu.VMEM)

    def body(v_ref, out_ref, buf, send_sems, recv_sems):
        x, y, c = _place()
        me = 4 * x + 2 * y + c
        buf[me] = v_ref[...]
        sent = []
        for k in range(1, n_dev):
            peer = ((1 - x) if k & 4 else x, (1 - y) if k & 2 else y, (1 - c) if k & 1 else c)
            cp = _remote(v_ref, buf.at[me], send_sems, recv_sems, k - 1, peer)
            cp.start()
            sent.append(cp)
        for cp in sent:
            cp.wait_recv()
        for cp in sent:
            cp.wait_send()
        acc = buf[0]
        for q in range(1, n_dev):
            acc = acc + buf[q]
        out_ref[...] = acc

    return pl.pallas_call(
        body, name=name, in_specs=[vm], out_specs=vm, out_shape=jax.ShapeDtypeStruct((R, W), v.dtype),
        scratch_shapes=[pltpu.VMEM((n_dev, R, W), v.dtype), pltpu.SemaphoreType.DMA((n_dev - 1,)),
                        pltpu.SemaphoreType.DMA((n_dev - 1,))],
    )(v)


HBM = pl.BlockSpec(memory_space=pltpu.HBM)
SEM = pl.BlockSpec(memory_space=pltpu.SEMAPHORE)
_DATAFLOW = pltpu.SideEffectType.DATAFLOW_SIDE_EFFECTING


def _split_start(name, srcs, land_shapes, n_copies, copies, after=()):
    ns, nl = len(srcs), len(land_shapes)
    lands = [lax.empty(s.shape, s.dtype) for s in land_shapes]

    def body(*refs):
        outs = refs[ns + nl + len(after):]
        for cp in copies(refs[:ns], refs[ns:ns + nl], outs[0], outs[1]):
            cp.start()
        outs[-1][...] = jnp.zeros_like(outs[-1])

    sems = pltpu.SemaphoreType.DMA((n_copies,))
    res = pl.pallas_call(
        body, name=name, in_specs=[HBM] * (ns + nl) + [ANY] * len(after),
        out_specs=(SEM, SEM, *[HBM] * (ns + nl), pl.BlockSpec(memory_space=pltpu.VMEM)),
        out_shape=(sems, sems, *[pltpu.HBM(a.shape, a.dtype) for a in srcs],
                   *[pltpu.HBM(s.shape, s.dtype) for s in land_shapes], jax.ShapeDtypeStruct((8, 128), F32)),
        input_output_aliases={i: 2 + i for i in range(ns + nl)},
        compiler_params=pltpu.CompilerParams(has_side_effects=_DATAFLOW),
    )(*[pltpu.with_memory_space_constraint(a, pltpu.HBM) for a in [*srcs, *lands]], *after)
    return res[0], res[1], list(res[2:2 + ns]), list(res[2 + ns:2 + ns + nl]), res[-1]


def _split_wait(name, send_sems, recv_sems, srcs, lands, copies, after=()):
    ns, nl = len(srcs), len(lands)

    def body(*refs):
        for cp in copies(refs[:ns], refs[ns:ns + nl], refs[ns + nl], refs[ns + nl + 1]):
            cp.wait_send()
            cp.wait_recv()

    res = pl.pallas_call(
        body, name=name, in_specs=[HBM] * (ns + nl) + [SEM, SEM] + [ANY] * len(after), out_specs=[HBM] * (ns + nl),
        out_shape=[pltpu.HBM(a.shape, a.dtype) for a in [*srcs, *lands]],
        input_output_aliases={i: i for i in range(ns + nl)},
        compiler_params=pltpu.CompilerParams(has_side_effects=_DATAFLOW),
    )(*srcs, *lands, send_sems, recv_sems, *after)
    return list(res[ns:])


def _gather_copies(rows):
    def copies(src_refs, land_refs, send_sems, recv_sems):
        x, y, c = _place()
        j = 2 * x + y
        out = []
        for w in range(len(src_refs)):
            r = _half(c, rows[w] // 2)
            for k in range(3):
                px, py = _other_chip(x, y, k)
                out.append(_remote(src_refs[w].at[r], land_refs[w].at[j, r], send_sems, recv_sems, 3 * w + k, (px, py, c)))
        return out
    return copies


def _scatter_copies(src_refs, land_refs, send_sems, recv_sems):
    x, y, c = _place()
    j = 2 * x + y
    out = []
    for w in range(len(src_refs)):
        for k in range(3):
            px, py = _other_chip(x, y, k)
            pj = 2 * px + py
            out.append(_remote(src_refs[w].at[pj], land_refs[w].at[(j - pj + 4) % 4 - 1], send_sems, recv_sems, 3 * w + k,
                               (px, py, c)))
    return out


def _reduce_begin(grads, core, tag):
    names = list(grads)
    gs = [grads[k].reshape(N_CHIPS, 2, -1, grads[k].shape[-1]) for k in names]
    recvs = _swap_halves(gs, f"grad_swap_halves_{tag}")
    sums = [_pair_sum(g, r, core, f"pair_sum_{k}") for k, g, r in zip(names, gs, recvs)]
    return names, gs, recvs, sums


def _reduce_end(begun, gots, chip, core):
    names, gs, recvs, _ = begun
    return {k: _chip_sum(g, r, t, chip, core, f"chip_sum_{k}") for k, g, r, t in zip(names, gs, recvs, gots)}


def _got_shapes(sums):
    return [jax.ShapeDtypeStruct((3, *a.shape[1:]), a.dtype) for a in sums]


def _adamw(w, g, m, v, name, layers=1, layer=0, into=None):
    shape = w.shape
    cols = shape[-1]
    w3, m3, v3 = (t.reshape(layers, -1, cols) for t in (w, m, v))
    rows = w3.shape[1]
    tile = _pick(rows, 256) if rows % 8 == 0 else rows
    n_in = 4 + (0 if into is None else 4)

    def body(*refs):
        wv, gv, mv, vv = (r[...] for r in refs[:4])
        g_ref, d_ref, m_ref, v_ref = refs[n_in:]
        m2 = ADAM_B1 * mv + (1.0 - ADAM_B1) * gv
        v2 = ADAM_B2 * vv + (1.0 - ADAM_B2) * jnp.square(gv)
        m_hat = m2 / (1.0 - ADAM_B1 ** ADAM_STEP)
        v_hat = v2 / (1.0 - ADAM_B2 ** ADAM_STEP)
        g_ref[...] = gv
        d_ref[...] = -ADAM_LR * (m_hat / (jnp.sqrt(v_hat) + ADAM_EPS) + ADAM_WD * wv)
        m_ref[...] = m2
        v_ref[...] = v2

    lay = pl.BlockSpec((None, tile, cols), lambda i: (layer, i, 0))
    out = jax.ShapeDtypeStruct((layers, rows, cols), F32)
    res = pl.pallas_call(
        body, name=name, grid=(rows // tile,),
        in_specs=[lay, pl.BlockSpec((tile, cols), lambda i: (i, 0)), lay, lay] + [ANY] * (n_in - 4),
        out_specs=[lay] * 4, out_shape=[out] * 4,
        input_output_aliases={} if into is None else {4 + k: k for k in range(4)},
        compiler_params=_cparams(("arbitrary",)),
    )(w3, g.reshape(rows, cols), m3, v3, *([] if into is None else [t.reshape(layers, rows, cols) for t in into]))
    return tuple(t.reshape(shape) for t in res)


ROW_F32, ROW_BF16 = (D_MODEL, F32), (D_MODEL, BF16)


def _res_norm(acc, h, gain):
    hh = h + acc
    return hh, _rms(hh, gain)


def _dx_norm_bwd(d, w, h, dres, gain, name, **kw):
    def epilogue(acc, hv, dr, g):
        dx, dg = _rms_bwd(hv, acc, g)
        return dr + dx, dr + dx, _colsum(dg)
    return _mm_rows(d, w, tb=True, extras=[h, dres], fulls=[gain], outs=[ROW_F32, ROW_BF16], accs=[((1, D_MODEL), F32)],
                    epilogue=epilogue, name=name, **kw)


def _tail_fwd(h1, hn2, p16, W, i, tag, next_gain=None, target=None):
    a = _mm(hn2, W["mlp_w1"][i], bblk=True, outs=[BF16], name=f"{tag}_mlp_w1",
            epilogue=lambda acc: (jnp.square(jnp.maximum(acc, 0.0)),))
    h2, hn3 = _mm_rows(a, W["mlp_w2"][i], extras=[h1], fulls=[W["ple_norm"][i:i + 1]], outs=[ROW_F32, ROW_BF16],
                       epilogue=_res_norm, name=f"{tag}_mlp_w2")
    gl = _mm(hn3, W["ple_gate_w"][i], name=f"{tag}_ple_gate")
    if target is None:
        def gated(acc, g, h, gain):
            hh = h + _sigmoid(g) * acc
            return hh, acc, _rms(hh, gain)
        h3, pp, hn = _mm_rows(p16[i], W["ple_proj_w"][i], bblk=True, extras=[gl, h2], fulls=[next_gain],
                              outs=[ROW_F32, ROW_F32, ROW_BF16], epilogue=gated, name=f"{tag}_ple_proj")
        return h3, hn, (h1, hn2, a, h2, hn3, gl, pp)

    def gated_loss(acc, g, h, t):
        e = h + _sigmoid(g) * acc - t
        return acc, e * (1.0 / D_MODEL), jnp.full((1, 128), 0.5 / D_MODEL, F32) * jnp.sum(e * e)
    pp, dy, loss = _mm_rows(p16[i], W["ple_proj_w"][i], bblk=True, extras=[gl, h2, target], outs=[ROW_F32, ROW_F32],
                            accs=[((1, 128), F32)], epilogue=gated_loss, name=f"{tag}_ple_proj")
    return dy, loss, (h1, hn2, a, h2, hn3, gl, pp)


def _tail_bwd(dh3, saved, p16, W, i, tag, after=()):
    h1, hn2, a, h2, hn3, gl, pp = saved

    def gate_bwd(d, g, ppv):
        gate = _sigmoid(g)
        return d * gate, d * ppv * gate * (1.0 - gate)

    def dw(kind, name):
        return (kind, 1, 0, None)

    dpp, dgl = _rows(gate_bwd, [dh3, gl, pp], [], [(D_MODEL, BF16), (D_MODEL, BF16)], name=f"{tag}_ple_gate_bwd",
                     after=after)
    d_proj = _mm(p16[i], dpp, ta=True, outs=[BF16], dw=dw("cols", "ple_proj_w"), name=f"{tag}_d_ple_proj")
    d_gate = _mm(hn3, dgl, ta=True, outs=[BF16], dw=dw("rows", "ple_gate_w"), name=f"{tag}_d_ple_gate")
    dh2, dh2_16, d_ple_norm = _dx_norm_bwd(dgl, W["ple_gate_w"][i], h2, dh3, W["ple_norm"][i:i + 1],
                                           f"{tag}_ple_gate_dx")
    d_w2 = _mm(a, dh2_16, ta=True, outs=[BF16], dw=dw("rows", "mlp_w2"), name=f"{tag}_d_mlp_w2")
    dz = _mm(dh2_16, W["mlp_w2"][i], tb=True, extras=[a], outs=[BF16], name=f"{tag}_mlp_w2_dx",
             epilogue=lambda acc, av: (acc * (2.0 * jnp.sqrt(av.astype(F32))),))
    d_w1 = _mm(hn2, dz, ta=True, outs=[BF16], dw=dw("cols", "mlp_w1"), name=f"{tag}_d_mlp_w1")
    dh1, dh1_16, d_mlp_norm = _dx_norm_bwd(dz, W["mlp_w1"][i], h1, dh2, W["mlp_norm"][i:i + 1], f"{tag}_mlp_w1_dx",
                                           bblk=True)
    big = {f"mlp_w1_{i}": d_w1, f"mlp_w2_{i}": d_w2, f"ple_gate_w_{i}": d_gate, f"ple_proj_w_{i}": d_proj}
    return dh1, dh1_16, big, dict(mlp_norm=d_mlp_norm, ple_norm=d_ple_norm)


def _ret_layer_fwd(h0, W, tabs, after=()):
    hn = _rows(lambda x, g: (_rms(x, g),), [h0], [W["mix_norm"][0:1]], [(D_MODEL, BF16)], name="ret_mix_norm",
               after=after)[0]
    proj = _mm(hn, W["ret_w_in"], bblk=True, outs=[BF16], name="ret_w_in")
    out, states = _ret_fwd(proj, tabs, "ret_scan")
    y = _ret_gate(out, proj, W["ret_gn"], "ret_gate")
    h1, hn2 = _mm_rows(y, W["ret_w_out"], extras=[h0], fulls=[W["mlp_norm"][0:1]], outs=[ROW_F32, ROW_BF16],
                       epilogue=_res_norm, name="ret_w_out")
    return h1, hn2, (h0, hn, proj, out, states, y)


def _ret_layer_bwd(dh1, dh1_16, saved, W, tabs, after=(), on_grads=None):
    h0, hn, proj, out, states, y = saved
    d_w_out = _mm(y, dh1_16, ta=True, outs=[BF16], dw=("rows", 1, 0, None), name="d_ret_w_out", after=after)
    dy = _mm(dh1_16, W["ret_w_out"], tb=True, name="ret_w_out_dx", after=after)
    dout, dproj, d_gn = _ret_gate_bwd(out, proj, W["ret_gn"], dy, "ret_gate_bwd")
    dproj = _ret_bwd(proj, states, dout, dproj, tabs, "ret_scan_bwd")
    d_w_in = _mm(hn, dproj, ta=True, outs=[BF16], dw=("cols", 1, 0, None), name="d_ret_w_in")
    big = dict(ret_w_in=d_w_in, ret_w_out=d_w_out)
    later = () if on_grads is None else on_grads(big)
    dh0, _, d_mix = _dx_norm_bwd(dproj, W["ret_w_in"], h0, dh1, W["mix_norm"][0:1], "ret_w_in_dx", bblk=True, tm=256,
                                 after=later)
    return dh0, big, dict(mix_norm=d_mix, ret_gn=d_gn)


def _mla_layer_fwd(h0, hn, W, tabs):
    proj = _mm(hn, W["mla_w_in"], name="mla_w_in")

    def low_rank_norm(pv, gq, gkv):
        return _rms(pv[:, :MLA_Q_RANK], gq), _rms(pv[:, MLA_Q_RANK:MLA_Q_RANK + MLA_KV_RANK], gkv)

    cqn, ckvn = _rows(low_rank_norm, [proj], [W["mla_q_a_norm"], W["mla_kv_a_norm"]],
                      [(MLA_Q_RANK, BF16), (MLA_KV_RANK, BF16)], name="mla_low_rank_norm")
    q = _mm(cqn, W["mla_w_uq"], bblk=True, name="mla_w_uq")
    kv = _mm(ckvn, W["mla_w_ukv"], bblk=True, name="mla_w_ukv")
    qf, kf, vf = _mla_prep(q, kv, proj, W["mla_q_norm"] * (MLA_QKD ** -0.5 * LOG2E), W["mla_k_norm"], tabs, "mla_prep")
    o, lse = _flash_fwd(qf, kf, vf, "mla_flash")
    h1, hn2 = _mm_rows(o, W["mla_w_out"], extras=[h0], fulls=[W["mlp_norm"][1:2]], outs=[ROW_F32, ROW_BF16],
                       epilogue=_res_norm, name="mla_w_out")
    return h1, hn2, (h0, hn, proj, cqn, ckvn, q, kv, qf, kf, vf, o, lse)


def _mla_layer_bwd(dh1, dh1_16, saved, W, tabs):
    h0, hn, proj, cqn, ckvn, q, kv, qf, kf, vf, o, lse = saved
    d_w_out = _mm(o, dh1_16, ta=True, outs=[BF16], dw=("rows", 1, 0, None), name="d_mla_w_out")
    do = _mm(dh1_16, W["mla_w_out"], tb=True, name="mla_w_out_dx")
    delta, do16 = _flash_delta(o, do, "mla_flash_delta")
    dqf, dkf, dvf = _flash_bwd(qf, kf, vf, do16, lse, delta, "mla_flash_bwd")
    dq, dkv, dkr, d_gq, d_gk = _mla_prep_bwd(q, kv, proj, W["mla_q_norm"], W["mla_k_norm"], tabs, dqf, dkf, dvf,
                                             "mla_prep_bwd")
    d_w_uq = _mm(cqn, dq, ta=True, outs=[BF16], dw=("cols", 1, 0, None), name="d_mla_w_uq")
    dcqn = _mm(dq, W["mla_w_uq"], tb=True, bblk=True, name="mla_w_uq_dx")
    d_w_ukv = _mm(ckvn, dkv, ta=True, outs=[BF16], dw=("cols", 1, 0, None), name="d_mla_w_ukv")
    dckvn = _mm(dkv, W["mla_w_ukv"], tb=True, bblk=True, name="mla_w_ukv_dx")

    def low_rank_bwd(pv, dcq, dckv, dkr_v, gq, gkv):
        dxq, dgq = _rms_bwd(pv[:, :MLA_Q_RANK], dcq, gq)
        dxkv, dgkv = _rms_bwd(pv[:, MLA_Q_RANK:MLA_Q_RANK + MLA_KV_RANK], dckv, gkv)
        return jnp.concatenate([dxq, dxkv, dkr_v], axis=-1), _colsum(dgq), _colsum(dgkv)

    dproj, d_gqa, d_gkva = _rows(low_rank_bwd, [proj, dcqn, dckvn, dkr], [W["mla_q_a_norm"], W["mla_kv_a_norm"]],
                                 [(MLA_IN_PAD, BF16)], [((1, MLA_Q_RANK), F32), ((1, MLA_KV_RANK), F32)],
                                 name="mla_low_rank_norm_bwd")
    d_w_in = _mm(hn, dproj, ta=True, outs=[BF16], dw=("rows", 1, 0, None), name="d_mla_w_in")
    dh0, dh0_16, d_mix = _dx_norm_bwd(dproj, W["mla_w_in"], h0, dh1, W["mix_norm"][1:2], "mla_w_in_dx")
    return (dh0, dh0_16, dict(mla_w_in=d_w_in, mla_w_uq=d_w_uq, mla_w_ukv=d_w_ukv, mla_w_out=d_w_out),
            dict(mix_norm=d_mix, mla_q_a_norm=d_gqa, mla_kv_a_norm=d_gkva, mla_q_norm=d_gq, mla_k_norm=d_gk))


def _local_step(x, p16, target, W):
    T = x.shape[0]
    ret_tabs, mla_tabs = _ret_tables(T), _mla_tables(T)
    h1, hn, s_ret = _ret_layer_fwd(x, W, ret_tabs)
    h3, hn, s_tail0 = _tail_fwd(h1, hn, p16, W, 0, "l0", next_gain=W["mix_norm"][1:2])
    h4, hn, s_mla = _mla_layer_fwd(h3, hn, W, mla_tabs)
    dy, loss, s_tail1 = _tail_fwd(h4, hn, p16, W, 1, "l1", target=target)
    dh4, dh4_16, g_t1, n_t1 = _tail_bwd(dy, s_tail1, p16, W, 1, "l1")
    dh3, _, g_mla, n_mla = _mla_layer_bwd(dh4, dh4_16, s_mla, W, mla_tabs)
    dh1, dh1_16, g_t0, n_t0 = _tail_bwd(dh3, s_tail0, p16, W, 0, "l0")
    dx, g_ret, n_ret = _ret_layer_bwd(dh1, dh1_16, s_ret, W, ret_tabs)
    return loss, dx, {**g_ret, **g_t0, **g_mla, **g_t1}, _small_grads(n_ret, n_t0, n_mla, n_t1)


def _loss_head(y, target):
    def fn(yv, tv):
        e = yv - tv
        return e * (1.0 / D_MODEL), jnp.full((1, 128), 0.5 / D_MODEL, F32) * jnp.sum(e * e)
    return _rows(fn, [y, target], [], [(D_MODEL, F32)], [((1, 128), F32)], name="loss_head")


def _small_grads(n_ret, n_t0, n_mla, n_t1):
    return dict(
        mix_norm=jnp.concatenate([n_ret["mix_norm"], n_mla["mix_norm"]], axis=0),
        mlp_norm=jnp.concatenate([n_t0["mlp_norm"], n_t1["mlp_norm"]], axis=0),
        ple_norm=jnp.concatenate([n_t0["ple_norm"], n_t1["ple_norm"]], axis=0),
        ret_gn=n_ret["ret_gn"], mla_q_a_norm=n_mla["mla_q_a_norm"], mla_kv_a_norm=n_mla["mla_kv_a_norm"],
        mla_q_norm=n_mla["mla_q_norm"], mla_k_norm=n_mla["mla_k_norm"])


_ORDER = ("mix_norm", "ret_w_in", "ret_gn", "ret_w_out", "mla_w_in", "mla_q_a_norm", "mla_kv_a_norm", "mla_w_uq",
          "mla_w_ukv", "mla_q_norm", "mla_k_norm", "mla_w_out", "mlp_norm", "mlp_w1", "mlp_w2", "ple_norm",
          "ple_gate_w", "ple_proj_w")
_TWO_LAYER = ("mlp_w1", "mlp_w2", "ple_gate_w", "ple_proj_w")
HEADS_PER_CHIP = MLA_HEADS // N_CHIPS
GAIN_ROWS = 32


def _travel_parts(w):
    uq = jnp.pad(w["mla_w_uq"][0].reshape(MLA_Q_RANK, HEADS_PER_CHIP, MLA_QKD), ((0, 0), (0, 0), (0, MLA_HP - MLA_QKD)))
    parts = {"ret_w_in": w["ret_w_in"][0], "ret_w_out": w["ret_w_out"][0]}
    for k in _TWO_LAYER:
        parts[k + "_0"] = w[k][0]
    parts["mla_w_in"] = jnp.pad(w["mla_w_in"][0], ((0, 0), (0, MLA_IN_PAD - MLA_IN)))
    parts["mla_w_uq"] = uq.reshape(MLA_Q_RANK, HEADS_PER_CHIP * MLA_HP)
    parts["mla_w_ukv"] = w["mla_w_ukv"][0]
    parts["mla_w_out"] = w["mla_w_out"][0]
    for k in _TWO_LAYER:
        parts[k + "_1"] = w[k][1]
    gains = jnp.concatenate([_pad_row(w["ret_gn"]), _pad_row(w["mla_q_a_norm"]), _pad_row(w["mla_kv_a_norm"]),
                             jnp.zeros((GAIN_ROWS - 3, PACK_W), F32)], axis=0)
    return {"gains": gains, **{k: v.astype(BF16) for k, v in parts.items()}}


def _full_weights(full):
    rows = lambda a: a.reshape(-1, a.shape[-1])
    W = {k: full[k] for k in ("ret_w_in", "mla_w_uq", "mla_w_ukv")}
    for k in ("ret_w_out", "mla_w_in", "mla_w_out"):
        W[k] = rows(full[k])
    W["mlp_w1"] = [full["mlp_w1_0"], full["mlp_w1_1"]]
    W["ple_proj_w"] = [full["ple_proj_w_0"], full["ple_proj_w_1"]]
    W["mlp_w2"] = [rows(full["mlp_w2_0"]), rows(full["mlp_w2_1"])]
    W["ple_gate_w"] = [rows(full["ple_gate_w_0"]), rows(full["ple_gate_w_1"])]
    return W


def _shard_grad(name, red, shape):
    if name == "mla_w_in":
        red = red.reshape(-1, MLA_IN_PAD)[:, :MLA_IN]
    elif name == "mla_w_uq":
        red = red.reshape(MLA_Q_RANK, HEADS_PER_CHIP, MLA_HP)[:, :, :MLA_QKD]
    return red.reshape(shape)


def _pad_row(v):
    v = v.reshape(1, -1)
    return jnp.pad(v, ((0, 0), (0, PACK_W - v.shape[1])))


def kernel(x, p, mix_norm, ret_w_in, ret_gn, ret_w_out, mla_w_in, mla_q_a_norm, mla_kv_a_norm, mla_w_uq, mla_w_ukv, mla_q_norm, mla_k_norm, mla_w_out, mlp_norm, mlp_w1, mlp_w2, ple_norm, ple_gate_w, ple_proj_w, loss_target, m_mix_norm, m_ret_w_in, m_ret_gn, m_ret_w_out, m_mla_w_in, m_mla_q_a_norm, m_mla_kv_a_norm, m_mla_w_uq, m_mla_w_ukv, m_mla_q_norm, m_mla_k_norm, m_mla_w_out, m_mlp_norm, m_mlp_w1, m_mlp_w2, m_ple_norm, m_ple_gate_w, m_ple_proj_w, v_mix_norm, v_ret_w_in, v_ret_gn, v_ret_w_out, v_mla_w_in, v_mla_q_a_norm, v_mla_kv_a_norm, v_mla_w_uq, v_mla_w_ukv, v_mla_q_norm, v_mla_k_norm, v_mla_w_out, v_mlp_norm, v_mlp_w1, v_mlp_w2, v_ple_norm, v_ple_gate_w, v_ple_proj_w):
    w = dict(mix_norm=mix_norm, ret_w_in=ret_w_in, ret_gn=ret_gn, ret_w_out=ret_w_out, mla_w_in=mla_w_in,
             mla_q_a_norm=mla_q_a_norm, mla_kv_a_norm=mla_kv_a_norm, mla_w_uq=mla_w_uq, mla_w_ukv=mla_w_ukv,
             mla_q_norm=mla_q_norm, mla_k_norm=mla_k_norm, mla_w_out=mla_w_out, mlp_norm=mlp_norm, mlp_w1=mlp_w1,
             mlp_w2=mlp_w2, ple_norm=ple_norm, ple_gate_w=ple_gate_w, ple_proj_w=ple_proj_w)
    m = dict(mix_norm=m_mix_norm, ret_w_in=m_ret_w_in, ret_gn=m_ret_gn, ret_w_out=m_ret_w_out, mla_w_in=m_mla_w_in,
             mla_q_a_norm=m_mla_q_a_norm, mla_kv_a_norm=m_mla_kv_a_norm, mla_w_uq=m_mla_w_uq, mla_w_ukv=m_mla_w_ukv,
             mla_q_norm=m_mla_q_norm, mla_k_norm=m_mla_k_norm, mla_w_out=m_mla_w_out, mlp_norm=m_mlp_norm,
             mlp_w1=m_mlp_w1, mlp_w2=m_mlp_w2, ple_norm=m_ple_norm, ple_gate_w=m_ple_gate_w, ple_proj_w=m_ple_proj_w)
    v = dict(mix_norm=v_mix_norm, ret_w_in=v_ret_w_in, ret_gn=v_ret_gn, ret_w_out=v_ret_w_out, mla_w_in=v_mla_w_in,
             mla_q_a_norm=v_mla_q_a_norm, mla_kv_a_norm=v_mla_kv_a_norm, mla_w_uq=v_mla_w_uq, mla_w_ukv=v_mla_w_ukv,
             mla_q_norm=v_mla_q_norm, mla_k_norm=v_mla_k_norm, mla_w_out=v_mla_w_out, mlp_norm=v_mlp_norm,
             mlp_w1=v_mlp_w1, mlp_w2=v_mlp_w2, ple_norm=v_ple_norm, ple_gate_w=v_ple_gate_w, ple_proj_w=v_ple_proj_w)
    xi, yi, ci = _place()
    chip = 2 * xi + yi
    n = N_CHIPS

    parts = _travel_parts(w)
    first = ("gains", "ret_w_in", "ret_w_out")
    later = [k for k in parts if k not in first]
    full = dict(zip(first, _gather_weights([parts[k] for k in first], "gather_first")))
    later_copies = _gather_copies([parts[k].shape[0] for k in later])
    g_send, g_recv, later_src, later_land, g_token = _split_start(
        "gather_later_start", [parts[k] for k in later],
        [jax.ShapeDtypeStruct((n, *parts[k].shape), BF16) for k in later], 3 * len(later), later_copies,
        after=[full["ret_w_in"]])
    gains = full["gains"]
    W = dict(mix_norm=mix_norm, mlp_norm=mlp_norm, ple_norm=ple_norm,
             mla_q_norm=jnp.pad(mla_q_norm, ((0, 0), (0, MLA_HP - MLA_QKD))),
             mla_k_norm=jnp.pad(mla_k_norm, ((0, 0), (0, MLA_HP - MLA_QKD))),
             ret_w_in=full["ret_w_in"], ret_w_out=full["ret_w_out"].reshape(-1, D_MODEL),
             ret_gn=gains[:, 0, :RET_HEADS * 128].reshape(n, RET_HEADS, 128).transpose(1, 0, 2).reshape(RET_HEADS, RET_DV),
             mla_q_a_norm=gains[:, 1, :MLA_Q_RANK // n].reshape(1, MLA_Q_RANK),
             mla_kv_a_norm=gains[:, 2, :MLA_KV_RANK // n].reshape(1, MLA_KV_RANK))
    x0, p16, target = x[0], p[:, 0].astype(BF16), loss_target[0]
    T = x0.shape[0]
    ret_tabs, mla_tabs = _ret_tables(T), _mla_tables(T)

    h1, hn, s_ret = _ret_layer_fwd(x0, W, ret_tabs, after=[g_token])
    landed = _split_wait("gather_later_wait", g_send, g_recv, later_src, later_land, later_copies, after=[h1])
    full.update(zip(later, _gather_weights([parts[k] for k in later], "gather_later_finish", landed=landed)))
    W.update(_full_weights(full))
    h3, hn, s_tail0 = _tail_fwd(h1, hn, p16, W, 0, "l0", next_gain=W["mix_norm"][1:2])
    h4, hn, s_mla = _mla_layer_fwd(h3, hn, W, mla_tabs)
    dy, loss, s_tail1 = _tail_fwd(h4, hn, p16, W, 1, "l1", target=target)

    dh4, dh4_16, g_t1, n_t1 = _tail_bwd(dy, s_tail1, p16, W, 1, "l1")
    dh3, _, g_mla, n_mla = _mla_layer_bwd(dh4, dh4_16, s_mla, W, mla_tabs)
    beg_a = _reduce_begin({**g_mla, **g_t1}, ci, "a")
    a_send, a_recv, a_src, a_land, a_token = _split_start(
        "scatter_a_start", beg_a[3], _got_shapes(beg_a[3]), 3 * len(beg_a[3]), _scatter_copies)
    dh1, dh1_16, g_t0, n_t0 = _tail_bwd(dh3, s_tail0, p16, W, 0, "l0", after=[a_token])
    beg_b = _reduce_begin(g_t0, ci, "b")
    b_send, b_recv, b_src, b_land, b_token = _split_start(
        "scatter_b_start", beg_b[3], _got_shapes(beg_b[3]), 3 * len(beg_b[3]), _scatter_copies)
    stage_c = {}

    def start_c(g_ret):
        beg = _reduce_begin(g_ret, ci, "c")
        stage_c["beg"] = beg
        stage_c["st"] = _split_start("scatter_c_start", beg[3], _got_shapes(beg[3]), 3 * len(beg[3]), _scatter_copies)
        return [stage_c["st"][4]]

    dx, _, n_ret = _ret_layer_bwd(dh1, dh1_16, s_ret, W, ret_tabs, after=[b_token], on_grads=start_c)
    got_a = _split_wait("scatter_a_wait", a_send, a_recv, a_src, a_land, _scatter_copies, after=[dx])
    got_b = _split_wait("scatter_b_wait", b_send, b_recv, b_src, b_land, _scatter_copies, after=[dx])
    got_c = _split_wait("scatter_c_wait", *stage_c["st"][:4], _scatter_copies, after=[dx])
    red = {**_reduce_end(beg_a, got_a, chip, ci), **_reduce_end(beg_b, got_b, chip, ci),
           **_reduce_end(stage_c["beg"], got_c, chip, ci)}
    red = dict(zip(red, _share_halves(list(red.values()))))
    gs = _small_grads(n_ret, n_t0, n_mla, n_t1)
    small_g = jnp.concatenate([
        gs["mix_norm"], gs["mlp_norm"], gs["ple_norm"], gs["ret_gn"].reshape(2, PACK_W), _pad_row(gs["mla_q_a_norm"]),
        _pad_row(gs["mla_kv_a_norm"]), _pad_row(gs["mla_q_norm"][:, :MLA_QKD]), _pad_row(gs["mla_k_norm"][:, :MLA_QKD]),
        _pad_row(loss[:, :1]), jnp.zeros((3, PACK_W), F32)], axis=0)
    tot = _allsum_small(small_g, "sum_small_grads")
    gn_all = tot[6:8].reshape(RET_HEADS, n, -1)
    g_small = dict(
        mix_norm=tot[0:2], mlp_norm=tot[2:4], ple_norm=tot[4:6],
        ret_gn=lax.dynamic_index_in_dim(gn_all, chip, axis=1, keepdims=False),
        mla_q_a_norm=lax.dynamic_index_in_dim(tot[8, :MLA_Q_RANK].reshape(n, -1), chip, axis=0, keepdims=True),
        mla_kv_a_norm=lax.dynamic_index_in_dim(tot[9, :MLA_KV_RANK].reshape(n, -1), chip, axis=0, keepdims=True),
        mla_q_norm=tot[10:11, :MLA_QKD], mla_k_norm=tot[11:12, :MLA_QKD])
    loss_out = tot[12, 0]

    outs = []
    for k in _ORDER:
        if k in _TWO_LAYER:
            res = None
            for i in (1, 0):
                res = _adamw(w[k], red[f"{k}_{i}"], m[k], v[k], f"adamw_{k}_{i}", layers=2, layer=i, into=res)
        elif k in red:
            res = _adamw(w[k], _shard_grad(k, red[k], w[k].shape), m[k], v[k], f"adamw_{k}")
        else:
            res = _adamw(w[k], g_small[k], m[k], v[k], f"adamw_{k}")
        outs.append(res)
    return (loss_out, dx[None], *[o[0] for o in outs], *[o[1] for o in outs], *[o[2] for o in outs],
            *[o[3] for o in outs])
```

```python
import functools

import jax
import jax.numpy as jnp
import numpy as np
from jax import lax
from jax.experimental import pallas as pl
from jax.experimental.pallas import tpu as pltpu

F32 = jnp.float32
BF16 = jnp.bfloat16

EPS = 1e-6
D_MODEL = 1024
CHUNK = 64
ROPE_THETA = 10000.0
RET_HEADS = 4
RET_DK = 256
RET_DV = 512
RET_GROUP = 1
RET_BLOCK = 256
MLA_HEADS = 8
MLA_NOPE = 128
MLA_ROPE = 64
MLA_QKD = 192
MLA_VD = 128
MLA_HP = 256
MLA_Q_RANK = 384
MLA_KV_RANK = 256
MLA_IN = 704
MLA_IN_PAD = 768
D_FF = 4096
PLE_DIM = 256
N_CHIPS = 4

ADAM_LR = 0.001
ADAM_B1 = 0.9
ADAM_B2 = 0.999
ADAM_EPS = 1e-08
ADAM_WD = 0.01
ADAM_STEP = 10

VMEM_LIMIT = 56 * 1024 * 1024
PACK_W = 1024
NEG = -1e30
LOG2E = 1.4426950408889634
FLASH_T = 512
FLASH_HEADS = 2
MM_SUB_ROWS = 256


def _cparams(sem=None):
    return pltpu.CompilerParams(dimension_semantics=sem, vmem_limit_bytes=VMEM_LIMIT)


def _pick(dim, pref):
    if dim <= pref:
        return dim
    t = pref
    while dim % t:
        t //= 2
    return t


def _mm(a, b, *, name, ta=False, tb=False, bblk=False, outs=None, extras=(), epilogue=None, dw=None,
        tm=1024, tn=512, after=()):
    if ta:
        K, M = a.shape
    else:
        M, K = a.shape
    if bblk and tb:
        nb, N, Kq = b.shape
        assert nb * Kq == K
    elif bblk:
        nb, Kb, Nq = b.shape
        N = nb * Nq
        assert Kb == K
    else:
        N = b.shape[0] if tb else b.shape[1]
    tn = _pick(Nq if (bblk and not tb) else N, tn)
    if dw is not None and dw[0] == "cols":
        tn = _pick(N // N_CHIPS, tn)
    tm = _pick(M // N_CHIPS if (dw is not None and dw[0] == "rows") else M, tm)
    grid = (M // tm, N // tn)

    a_spec = pl.BlockSpec((K, tm), lambda i, j: (0, i)) if ta else pl.BlockSpec((tm, K), lambda i, j: (i, 0))
    if bblk and tb:
        b_spec = pl.BlockSpec((nb, tn, Kq), lambda i, j: (0, j, 0))
    elif bblk:
        npb = Nq // tn
        b_spec = pl.BlockSpec((None, K, tn), lambda i, j: (j // npb, 0, j % npb))
    elif tb:
        b_spec = pl.BlockSpec((tn, K), lambda i, j: (j, 0))
    else:
        b_spec = pl.BlockSpec((K, tn), lambda i, j: (0, j))
    in_specs = [a_spec, b_spec] + [pl.BlockSpec((tm, tn), lambda i, j: (i, j)) for _ in extras]
    args = [a, b, *extras]
    aliases = {}
    if outs is None:
        outs = [F32]
    if dw is None:
        o_specs = [pl.BlockSpec((tm, tn), lambda i, j: (i, j)) for _ in outs]
        o_shapes = [jax.ShapeDtypeStruct((M, N), dt) for dt in outs]
    else:
        kind, layers, layer, into = dw
        if kind == "cols":
            per = (N // N_CHIPS) // tn
            o_specs = [pl.BlockSpec((None, None, tm, tn), lambda i, j: (j // per, layer, i, j % per))]
            o_shapes = [jax.ShapeDtypeStruct((N_CHIPS, layers, M, N // N_CHIPS), outs[0])]
        else:
            per = (M // N_CHIPS) // tm
            o_specs = [pl.BlockSpec((None, None, tm, tn), lambda i, j: (i // per, layer, i % per, j))]
            o_shapes = [jax.ShapeDtypeStruct((N_CHIPS, layers, M // N_CHIPS, N), outs[0])]
        if into is not None:
            aliases = {len(args): 0}
            in_specs.append(pl.BlockSpec(memory_space=pl.ANY))
            args.append(into)
    for t in after:
        in_specs.append(pl.BlockSpec(memory_space=pl.ANY))
        args.append(t)
    n_e, n_o = len(extras), len(outs)

    sub = _pick(tm, MM_SUB_ROWS)

    def body(a_ref, b_ref, *rest):
        e_refs, o_refs = rest[:n_e], rest[len(rest) - n_o:]
        for r0 in range(0, tm, sub):
            rows = slice(r0, r0 + sub)
            av = (a_ref[:, rows] if ta else a_ref[rows, :]).astype(BF16)
            if bblk and tb:
                acc = _dot_nt(av[:, :Kq], b_ref[0].astype(BF16))
                for s in range(1, nb):
                    acc = acc + _dot_nt(av[:, s * Kq:(s + 1) * Kq], b_ref[s].astype(BF16))
            elif ta:
                acc = _dot_tn(av, b_ref[...].astype(BF16))
            elif tb:
                acc = _dot_nt(av, b_ref[...].astype(BF16))
            else:
                acc = _dot(av, b_ref[...].astype(BF16))
            vals = (acc,) if epilogue is None else epilogue(acc, *[e[rows, :] for e in e_refs])
            for o, v in zip(o_refs, vals):
                o[rows, :] = v.astype(o.dtype)

    res = pl.pallas_call(
        body, name=name, grid=grid, in_specs=in_specs, out_specs=o_specs, out_shape=o_shapes,
        input_output_aliases=aliases, compiler_params=_cparams(("parallel", "arbitrary")),
    )(*args)
    return res[0] if n_o == 1 else res


def _mm_rows(a, b, *, name, epilogue, outs, tb=False, bblk=False, extras=(), fulls=(), accs=(), tm=512, after=()):
    M, K = a.shape
    tm = _pick(M, tm)
    sub = _pick(tm, MM_SUB_ROWS)
    nb = b.shape[0] if bblk else 1
    n_e, n_f, n_o, n_a = len(extras), len(fulls), len(outs), len(accs)
    n_in = 2 + n_e + n_f + len(after)

    def whole(t):
        return pl.BlockSpec(t.shape, lambda i, nd=t.ndim: (0,) * nd)

    in_specs = [pl.BlockSpec((tm, K), lambda i: (i, 0)), whole(b)]
    in_specs += [pl.BlockSpec((tm, e.shape[1]), lambda i: (i, 0)) for e in extras] + [whole(f) for f in fulls]
    in_specs += [pl.BlockSpec(memory_space=pl.ANY) for _ in after]
    out_specs = [pl.BlockSpec((tm, w), lambda i: (i, 0)) for w, _ in outs] + [pl.BlockSpec(s, lambda i: (0, 0)) for s, _ in accs]
    out_shape = [jax.ShapeDtypeStruct((M, w), dt) for w, dt in outs] + [jax.ShapeDtypeStruct(s, dt) for s, dt in accs]

    def body(a_ref, b_ref, *rest):
        e_refs, f_refs = rest[:n_e], rest[n_e:n_e + n_f]
        o_refs, acc_refs = rest[n_in - 2:n_in - 2 + n_o], rest[n_in - 2 + n_o:]
        fv = [f[...] for f in f_refs]
        totals = None
        for r0 in range(0, tm, sub):
            rows = slice(r0, r0 + sub)
            av = a_ref[rows, :].astype(BF16)
            if bblk and tb:
                kq = K // nb
                acc = _dot_nt(av[:, :kq], b_ref[0])
                for s in range(1, nb):
                    acc = acc + _dot_nt(av[:, s * kq:(s + 1) * kq], b_ref[s])
            elif bblk:
                acc = jnp.concatenate([_dot(av, b_ref[s]) for s in range(nb)], axis=-1)
            elif tb:
                acc = _dot_nt(av, b_ref[...])
            else:
                acc = _dot(av, b_ref[...])
            vals = epilogue(acc, *[e[rows, :] for e in e_refs], *fv)
            for o, v in zip(o_refs, vals[:n_o]):
                o[rows, :] = v.astype(o.dtype)
            part = vals[n_o:]
            totals = part if totals is None else [t + p for t, p in zip(totals, part)]
        first_step = pl.program_id(0) == 0
        for o, v in zip(acc_refs, totals):
            @pl.when(first_step)
            def _(o=o, v=v):
                o[...] = v.astype(o.dtype)

            @pl.when(jnp.logical_not(first_step))
            def _(o=o, v=v):
                o[...] += v.astype(o.dtype)

    return pl.pallas_call(
        body, name=name, grid=(M // tm,), in_specs=in_specs, out_specs=out_specs, out_shape=out_shape,
        compiler_params=_cparams(("arbitrary",)),
    )(a, b, *extras, *fulls, *after)


def _rows(fn, rows, fulls, outs, accs=(), *, name, tile=512, after=()):
    first = rows[0][0] if isinstance(rows[0], tuple) else rows[0]
    T = first.shape[0]
    tile = _pick(T, tile)
    in_specs, args = [], []
    for r in rows:
        if isinstance(r, tuple):
            arr, w, cb = r
            in_specs.append(pl.BlockSpec((tile, w), lambda i, cb=cb: (i, cb)))
        else:
            arr = r
            in_specs.append(pl.BlockSpec((tile, arr.shape[1]), lambda i: (i, 0)))
        args.append(arr)
    for f in fulls:
        in_specs.append(pl.BlockSpec(f.shape, lambda i, nd=f.ndim: (0,) * nd))
        args.append(f)
    outs = [o if len(o) == 4 else (*o, o[0], 0) for o in outs]
    out_specs = [pl.BlockSpec((tile, w), lambda i, cb=cb: (i, cb)) for w, _, _, cb in outs]
    out_specs += [pl.BlockSpec(s, lambda i: (0, 0)) for s, _ in accs]
    out_shape = [jax.ShapeDtypeStruct((T, tw), dt) for _, dt, tw, _ in outs]
    out_shape += [jax.ShapeDtypeStruct(s, dt) for s, dt in accs]
    n_in, n_out = len(args), len(outs)
    for t in after:
        in_specs.append(pl.BlockSpec(memory_space=pl.ANY))
        args.append(t)

    def body(*refs):
        vals = fn(*[r[...] for r in refs[:n_in]])
        o_refs = refs[len(args):]
        for o, v in zip(o_refs[:n_out], vals[:n_out]):
            o[...] = v.astype(o.dtype)
        first_step = pl.program_id(0) == 0
        for o, v in zip(o_refs[n_out:], vals[n_out:]):
            @pl.when(first_step)
            def _(o=o, v=v):
                o[...] = v.astype(o.dtype)

            @pl.when(jnp.logical_not(first_step))
            def _(o=o, v=v):
                o[...] += v.astype(o.dtype)

    res = pl.pallas_call(
        body, name=name, grid=(T // tile,), in_specs=in_specs, out_specs=out_specs, out_shape=out_shape,
        compiler_params=_cparams(("arbitrary",)),
    )(*args)
    return res


def _rms(x, g):
    r = lax.rsqrt(jnp.mean(x * x, axis=-1, keepdims=True) + EPS)
    return (x * r) * g


def _rms_bwd(x, dy, g, n=None):
    n = x.shape[-1] if n is None else n
    r = lax.rsqrt(jnp.sum(x * x, axis=-1, keepdims=True) / n + EPS)
    xh = x * r
    dxh = dy * g
    dx = r * (dxh - xh * (jnp.sum(dxh * xh, axis=-1, keepdims=True) / n))
    return dx, dy * xh


def _colsum(v):
    return jnp.sum(v, axis=0, keepdims=True)


def _sigmoid(x):
    return 1.0 / (1.0 + jnp.exp(-x))


def _widen(v, width):
    reps = width // v.shape[1]
    return v if reps == 1 else jnp.concatenate([v] * reps, axis=-1)


def _norm_fwd(h, gain, name):
    return _rows(lambda x, g: (_rms(x, g),), [h], [gain], [(h.shape[1], BF16)], name=name)[0]


def _norm_bwd(h, dhn, gain, dres, name):
    def fn(x, dy, dr, g):
        dx, dg = _rms_bwd(x, dy, g)
        return dr + dx, dr + dx, _colsum(dg)
    d = h.shape[1]
    return _rows(fn, [h, dhn, dres], [gain], [(d, F32), (d, BF16)], [((1, d), F32)], name=name)


def _rope_angles(T, dim):
    inv = (1.0 / (np.float32(ROPE_THETA) ** (np.arange(0, dim, 2, dtype=np.float32) / np.float32(dim)))).astype(np.float32)
    return np.arange(T, dtype=np.float32)[:, None] * inv[None, :]


def _ret_tables(T):
    ang = _rope_angles(T, RET_DK)
    log_gamma = np.log(np.float32(1.0) - np.float32(2.0) ** (-5.0 - np.arange(RET_HEADS, dtype=np.float32)))
    idx = np.arange(RET_BLOCK, dtype=np.float32)
    chunk = np.arange(RET_BLOCK) // CHUNK
    dist = idx[:, None] - idx[None, :]
    seen = np.where(chunk[:, None] == chunk[None, :], np.abs(dist), np.where(chunk[:, None] > chunk[None, :], dist, np.inf))
    intra = np.exp(log_gamma[:, None, None] * seen[None].astype(np.float32))
    qd = np.exp(log_gamma[:, None] * (idx + 1.0))[:, :, None]
    kd = np.exp(log_gamma[:, None] * (RET_BLOCK - 1.0 - idx))[:, :, None]
    cd = np.exp(log_gamma * RET_BLOCK)[:, None, None]
    return tuple(jnp.asarray(t, F32) for t in (np.cos(ang), np.sin(ang), intra, qd, kd, cd))


def _rope_half(x, c, s):
    x1, x2 = x[:, :RET_DK // 2], x[:, RET_DK // 2:]
    return jnp.concatenate([x1 * c - x2 * s, x2 * c + x1 * s], axis=-1)


def _rope_half_bwd(d, c, s):
    d1, d2 = d[:, :RET_DK // 2], d[:, RET_DK // 2:]
    return jnp.concatenate([d1 * c + d2 * s, d2 * c - d1 * s], axis=-1)


def _dot(a, b):
    return lax.dot_general(a, b, (((1,), (0,)), ((), ())), preferred_element_type=F32)


def _dot_nt(a, b):
    return lax.dot_general(a, b, (((1,), (1,)), ((), ())), preferred_element_type=F32)


def _dot_tn(a, b):
    return lax.dot_general(a, b, (((0,), (0,)), ((), ())), preferred_element_type=F32)


def _ret_specs(T, tb, rev):
    nj = T // tb
    jj = (lambda j: nj - 1 - j) if rev else (lambda j: j)
    g = RET_GROUP
    kq = RET_HEADS // g
    vq = 2 * RET_HEADS * RET_DK // (g * RET_DV)
    return dict(
        q=pl.BlockSpec((tb, g * RET_DK), lambda h, j: (jj(j), h)),
        k=pl.BlockSpec((tb, g * RET_DK), lambda h, j: (jj(j), kq + h)),
        v=pl.BlockSpec((tb, g * RET_DV), lambda h, j: (jj(j), vq + h)),
        tab=pl.BlockSpec((tb, RET_DK // 2), lambda h, j: (jj(j), 0)),
        intra=pl.BlockSpec((g, RET_BLOCK, RET_BLOCK), lambda h, j: (h, 0, 0)),
        dec=pl.BlockSpec((g, RET_BLOCK, 1), lambda h, j: (h, 0, 0)),
        cd=pl.BlockSpec((g, 1, 1), lambda h, j: (h, 0, 0)),
        o=pl.BlockSpec((tb, g * RET_DV), lambda h, j: (jj(j), h)),
        s=pl.BlockSpec((g, tb // RET_BLOCK, RET_DK, RET_DV), lambda h, j: (h, jj(j), 0, 0)),
    )


def _ret_fwd(proj, tabs, name):
    T = proj.shape[0]
    cos, sin, intra, qd, kd, cd = tabs
    tb = _pick(T, 512)
    cps = tb // RET_BLOCK
    sp = _ret_specs(T, tb, False)
    scale = RET_DK ** -0.5

    def body(q_ref, k_ref, v_ref, cos_ref, sin_ref, intra_ref, qd_ref, kd_ref, cd_ref, o_ref, s_ref, state):
        @pl.when(pl.program_id(1) == 0)
        def _():
            state[...] = jnp.zeros_like(state)

        for c in range(cps):
            rows = pl.ds(c * RET_BLOCK, RET_BLOCK)
            co, si = cos_ref[rows, :], sin_ref[rows, :]
            for h in range(RET_GROUP):
                hk, hv = slice(h * RET_DK, (h + 1) * RET_DK), slice(h * RET_DV, (h + 1) * RET_DV)
                q = _rope_half(q_ref[rows, hk].astype(F32), co, si)
                k = _rope_half(k_ref[rows, hk].astype(F32), co, si) * scale
                vb = v_ref[rows, hv].astype(BF16)
                st = state[h]
                sb = st.astype(BF16)
                s_ref[h, c] = sb
                sc = _dot_nt(q.astype(BF16), k.astype(BF16)) * intra_ref[h]
                inner = _dot(sc.astype(BF16), vb)
                cross = _dot((q * qd_ref[h]).astype(BF16), sb)
                o_ref[rows, hv] = inner + cross
                state[h] = st * cd_ref[h] + _dot_tn((k * kd_ref[h]).astype(BF16), vb)

    return pl.pallas_call(
        body, name=name, grid=(RET_HEADS // RET_GROUP, T // tb),
        in_specs=[sp["q"], sp["k"], sp["v"], sp["tab"], sp["tab"], sp["intra"], sp["dec"], sp["dec"], sp["cd"]],
        out_specs=[sp["o"], sp["s"]],
        out_shape=[jax.ShapeDtypeStruct((T, RET_HEADS * RET_DV), F32),
                   jax.ShapeDtypeStruct((RET_HEADS, T // RET_BLOCK, RET_DK, RET_DV), BF16)],
        scratch_shapes=[pltpu.VMEM((RET_GROUP, RET_DK, RET_DV), F32)],
        compiler_params=_cparams(("arbitrary", "arbitrary")),
    )(proj, proj, proj, cos, sin, intra, qd, kd, cd)


def _ret_bwd(proj, states, dout, dproj, tabs, name):
    assert RET_GROUP == 1
    T = proj.shape[0]
    cos, sin, intra, qd, kd, cd = tabs
    tb = _pick(T, 512)
    cps = tb // RET_BLOCK
    nj = T // tb
    sp = _ret_specs(T, tb, True)
    scale = RET_DK ** -0.5
    k0, v0 = RET_HEADS * RET_DK, 2 * RET_HEADS * RET_DK

    def body(q_ref, k_ref, v_ref, cos_ref, sin_ref, intra_ref, qd_ref, kd_ref, cd_ref, s_ref, do_ref, _dproj_in,
             out_ref, dq_s, dk_s, dv_s, sems, dstate):
        head, j = pl.program_id(0), pl.program_id(1)
        step = head * nj + j
        slot = step % 2
        dq_ref, dk_ref, dv_ref = dq_s.at[slot], dk_s.at[slot], dv_s.at[slot]

        @pl.when(j == 0)
        def _():
            dstate[...] = jnp.zeros_like(dstate)

        for c in reversed(range(cps)):
            rows = pl.ds(c * RET_BLOCK, RET_BLOCK)
            co, si = cos_ref[rows, :], sin_ref[rows, :]
            for h in range(RET_GROUP):
                hk, hv = slice(h * RET_DK, (h + 1) * RET_DK), slice(h * RET_DV, (h + 1) * RET_DV)
                q = _rope_half(q_ref[rows, hk].astype(F32), co, si)
                k = _rope_half(k_ref[rows, hk].astype(F32), co, si) * scale
                qb, kb = q.astype(BF16), k.astype(BF16)
                vb = v_ref[rows, hv].astype(BF16)
                dob = do_ref[rows, hv].astype(BF16)
                sb = s_ref[h, c]
                ia = intra_ref[h]
                pb = (_dot_nt(qb, kb) * ia).astype(BF16)
                dsn = dstate[h]
                dsb = dsn.astype(BF16)
                kdk = (k * kd_ref[h]).astype(BF16)
                qdq = (q * qd_ref[h]).astype(BF16)
                dv = _dot_tn(pb, dob) + _dot(kdk, dsb)
                dpb = (_dot_nt(dob, vb) * ia).astype(BF16)
                dq = _dot(dpb, kb) + _dot_nt(dob, sb) * qd_ref[h]
                dk = _dot_tn(dpb, qb) + _dot_nt(vb, dsb) * kd_ref[h]
                dstate[h] = dsn * cd_ref[h] + _dot_tn(qdq, dob)
                dq_ref[rows, hk] = _rope_half_bwd(dq, co, si).astype(BF16)
                dk_ref[rows, hk] = _rope_half_bwd(dk * scale, co, si).astype(BF16)
                dv_ref[rows, hv] = dv.astype(BF16)

        def copies(sl):
            r = pl.ds(pl.multiple_of((nj - 1 - j) * tb, tb), tb)
            cols = lambda first, w: pl.ds(pl.multiple_of(first + head * w, 128), w)
            return [pltpu.make_async_copy(dq_s.at[sl], out_ref.at[r, cols(0, RET_DK)], sems.at[sl, 0]),
                    pltpu.make_async_copy(dk_s.at[sl], out_ref.at[r, cols(k0, RET_DK)], sems.at[sl, 1]),
                    pltpu.make_async_copy(dv_s.at[sl], out_ref.at[r, cols(v0, RET_DV)], sems.at[sl, 2])]

        @pl.when(step > 0)
        def _():
            for cp in copies(1 - slot):
                cp.wait()

        for cp in copies(slot):
            cp.start()

        @pl.when(step == RET_HEADS * nj - 1)
        def _():
            for cp in copies(slot):
                cp.wait()

    return pl.pallas_call(
        body, name=name, grid=(RET_HEADS, nj),
        in_specs=[sp["q"], sp["k"], sp["v"], sp["tab"], sp["tab"], sp["intra"], sp["dec"], sp["dec"], sp["cd"],
                  sp["s"], sp["o"], pl.BlockSpec(memory_space=pl.ANY)],
        out_specs=pl.BlockSpec(memory_space=pl.ANY), out_shape=jax.ShapeDtypeStruct(dproj.shape, dproj.dtype),
        input_output_aliases={11: 0},
        scratch_shapes=[pltpu.VMEM((2, tb, RET_DK), BF16), pltpu.VMEM((2, tb, RET_DK), BF16),
                        pltpu.VMEM((2, tb, RET_DV), BF16), pltpu.SemaphoreType.DMA((2, 3)),
                        pltpu.VMEM((RET_GROUP, RET_DK, RET_DV), F32)],
        compiler_params=_cparams(("arbitrary", "arbitrary")),
    )(proj, proj, proj, cos, sin, intra, qd, kd, cd, states, dout, dproj)


def _ret_gate(out, proj, gn, name):
    def fn(o, g, *gains):
        g = g.astype(F32)
        parts = [_rms(o[:, h * RET_DV:(h + 1) * RET_DV], gains[h]) for h in range(RET_HEADS)]
        return (g * _sigmoid(g) * jnp.concatenate(parts, axis=-1),)
    w = RET_HEADS * RET_DV
    return _rows(fn, [out, (proj, w, 2)], [gn[h:h + 1] for h in range(RET_HEADS)], [(w, BF16)], name=name)[0]


def _ret_gate_bwd(out, proj, gn, dy, name):
    def fn(o, g, d, *gains):
        g = g.astype(F32)
        sg = _sigmoid(g)
        silu = g * sg
        dsilu = sg * (1.0 + g * (1.0 - sg))
        dos, dgs = [], []
        row = lax.broadcasted_iota(jnp.int32, (RET_HEADS, RET_DV), 0)
        dgn = jnp.zeros((RET_HEADS, RET_DV), F32)
        for h in range(RET_HEADS):
            sl = slice(h * RET_DV, (h + 1) * RET_DV)
            oh = o[:, sl]
            dgs.append(d[:, sl] * _rms(oh, gains[h]) * dsilu[:, sl])
            dx, dg = _rms_bwd(oh, d[:, sl] * silu[:, sl], gains[h])
            dos.append(dx)
            dgn = dgn + jnp.where(row == h, _colsum(dg), 0.0)
        return jnp.concatenate(dos, axis=-1), jnp.concatenate(dgs, axis=-1), dgn
    w = RET_HEADS * RET_DV
    return _rows(fn, [out, (proj, w, 2), dy], [gn[h:h + 1] for h in range(RET_HEADS)],
                 [(w, BF16), (w, BF16, proj.shape[1], 2)], [((RET_HEADS, RET_DV), F32)], name=name, tile=128)


def _mla_tables(T):
    ang = _rope_angles(T, MLA_ROPE)
    c, s = np.cos(ang), np.sin(ang)
    z32, z64 = np.zeros((T, 32), np.float32), np.zeros((T, 64), np.float32)
    cos_t = np.concatenate([c, c, z64], axis=1)
    sin_a = np.concatenate([-s, z32, z64], axis=1)
    sin_b = np.concatenate([z32, s, z64], axis=1)
    return tuple(jnp.asarray(t, F32) for t in (cos_t, sin_a, sin_b))


def _rope_blk(x, ct, sa, sb):
    return x * ct + pltpu.roll(x, 96, 1) * sa + pltpu.roll(x, 32, 1) * sb


def _rope_blk_bwd(d, ct, sa, sb):
    return d * ct + pltpu.roll(d * sa, 32, 1) + pltpu.roll(d * sb, 96, 1)


def _head_norm(x, gain):
    r = lax.rsqrt(jnp.sum(x * x, axis=-1, keepdims=True) / MLA_QKD + EPS)
    return (x * r) * gain


def _mla_prep(q, kv, proj, gq, gk, tabs, name):
    def fn(qv, kvv, kr, ct, sa, sb, gqv, gkv):
        qv, kvv = qv.astype(F32), kvv.astype(F32)
        qs, ks, vs = [], [], []
        for h in range(MLA_HEADS):
            b = h * MLA_HP
            y = _head_norm(qv[:, b:b + MLA_HP], gqv)
            qs += [y[:, :128], _rope_blk(y[:, 128:], ct, sa, sb)]
            y = _head_norm(jnp.concatenate([kvv[:, b:b + 128], kr], axis=-1), gkv)
            ks += [y[:, :128], _rope_blk(y[:, 128:], ct, sa, sb)]
            vs.append(kvv[:, b + 128:b + 256])
        return jnp.concatenate(qs, axis=-1), jnp.concatenate(ks, axis=-1), jnp.concatenate(vs, axis=-1)
    w = MLA_HEADS * MLA_HP
    return _rows(fn, [q, kv, (proj, 128, 5), *tabs], [gq, gk],
                 [(w, BF16), (w, BF16), (MLA_HEADS * MLA_VD, BF16)], name=name, tile=128)


def _mla_prep_bwd(q, kv, proj, gq, gk, tabs, dqf, dkf, dvf, name):
    def fn(qv, kvv, kr, ct, sa, sb, dqv, dkv, dvv, gqv, gkv):
        qv, kvv, dqv, dkv = (t.astype(F32) for t in (qv, kvv, dqv, dkv))
        dqs, dkvs = [], []
        dkr = jnp.zeros_like(kr)
        dgq = jnp.zeros((1, MLA_HP), F32)
        dgk = jnp.zeros((1, MLA_HP), F32)
        for h in range(MLA_HEADS):
            b = h * MLA_HP
            dy = jnp.concatenate([dqv[:, b:b + 128], _rope_blk_bwd(dqv[:, b + 128:b + 256], ct, sa, sb)], axis=-1)
            dx, dg = _rms_bwd(qv[:, b:b + MLA_HP], dy, gqv, MLA_QKD)
            dqs.append(dx)
            dgq = dgq + _colsum(dg)
            dy = jnp.concatenate([dkv[:, b:b + 128], _rope_blk_bwd(dkv[:, b + 128:b + 256], ct, sa, sb)], axis=-1)
            dx, dg = _rms_bwd(jnp.concatenate([kvv[:, b:b + 128], kr], axis=-1), dy, gkv, MLA_QKD)
            dkvs += [dx[:, :128], dvv[:, h * MLA_VD:(h + 1) * MLA_VD]]
            dkr = dkr + dx[:, 128:]
            dgk = dgk + _colsum(dg)
        return jnp.concatenate(dqs, axis=-1), jnp.concatenate(dkvs, axis=-1), dkr, dgq, dgk
    w = MLA_HEADS * MLA_HP
    return _rows(fn, [q, kv, (proj, 128, 5), *tabs, dqf, dkf, dvf], [gq, gk],
                 [(w, BF16), (w, BF16), (128, F32)], [((1, MLA_HP), F32), ((1, MLA_HP), F32)], name=name, tile=128)


def _chunk_mask(qi, ki, tq, tk):
    shift = CHUNK.bit_length() - 1
    rq = lax.shift_right_arithmetic(qi * tq + lax.broadcasted_iota(jnp.int32, (tq, tk), 0), shift)
    ck = lax.shift_right_arithmetic(ki * tk + lax.broadcasted_iota(jnp.int32, (tq, tk), 1), shift)
    return ck <= rq


def _flash_fwd(qf, kf, vf, name):
    T = qf.shape[0]
    t = _pick(T, FLASH_T)
    n = T // t
    scale = MLA_QKD ** -0.5

    g = FLASH_HEADS

    def body(q_ref, k_ref, v_ref, o_ref, lse_ref, m_s, l_s, acc):
        qi = pl.program_id(1)
        m_s[...] = jnp.full_like(m_s, NEG)
        l_s[...] = jnp.zeros_like(l_s)
        acc[...] = jnp.zeros_like(acc)

        def step(kb, masked):
            rows = pl.ds(pl.multiple_of(kb * t, t), t)
            for h in range(g):
                hq, hv = slice(h * MLA_HP, (h + 1) * MLA_HP), slice(h * MLA_VD, (h + 1) * MLA_VD)
                s = _dot_nt(q_ref[:, hq], k_ref[rows, hq])
                if masked:
                    s = jnp.where(_chunk_mask(0, 0, t, t), s, NEG)
                m_prev = m_s[:, hv]
                m_new = jnp.maximum(m_prev, jnp.max(s, axis=-1, keepdims=True))
                alpha = jnp.exp2(m_prev - m_new)
                p = jnp.exp2(s - _widen(m_new, t))
                l_s[:, hv] = alpha * l_s[:, hv] + sum(p[:, i * 128:(i + 1) * 128] for i in range(t // 128))
                acc[:, hv] = acc[:, hv] * alpha + _dot(p.astype(BF16), v_ref[rows, hv])
                m_s[:, hv] = m_new

        @pl.loop(0, qi)
        def _(kb):
            step(kb, False)

        step(qi, True)
        for h in range(g):
            hv = slice(h * MLA_VD, (h + 1) * MLA_VD)
            l = jnp.sum(l_s[:, hv], axis=-1, keepdims=True)
            o_ref[:, hv] = acc[:, hv] / l
            lse_ref[:, hv] = m_s[:, hv] + jnp.log2(l)

    qmap = lambda h, i: (i, h)
    kmap = lambda h, i: (0, h)
    vec = pltpu.VMEM((t, g * MLA_VD), F32)
    return pl.pallas_call(
        body, name=name, grid=(MLA_HEADS // g, n),
        in_specs=[pl.BlockSpec((t, g * MLA_HP), qmap), pl.BlockSpec((T, g * MLA_HP), kmap),
                  pl.BlockSpec((T, g * MLA_VD), kmap)],
        out_specs=[pl.BlockSpec((t, g * MLA_VD), qmap), pl.BlockSpec((t, g * MLA_VD), qmap)],
        out_shape=[jax.ShapeDtypeStruct((T, MLA_HEADS * MLA_VD), F32),
                   jax.ShapeDtypeStruct((T, MLA_HEADS * MLA_VD), F32)],
        scratch_shapes=[vec, vec, vec],
        compiler_params=_cparams(("parallel", "arbitrary")),
    )(qf, kf, vf)


def _flash_delta(o, do, name):
    def fn(ov, dv):
        parts = []
        for h in range(MLA_HEADS):
            sl = slice(h * MLA_VD, (h + 1) * MLA_VD)
            d = jnp.sum(dv[:, sl] * ov[:, sl], axis=-1, keepdims=True)
            parts.append(jnp.broadcast_to(d, (d.shape[0], MLA_VD)))
        return jnp.concatenate(parts, axis=-1), dv
    w = MLA_HEADS * MLA_VD
    return _rows(fn, [o, do], [], [(w, F32), (w, BF16)], name=name)


def _flash_bwd(qf, kf, vf, do16, lse, delta, name):
    T = qf.shape[0]
    t = _pick(T, FLASH_T)
    n = T // t
    scale = MLA_QKD ** -0.5

    def body(q_ref, k_ref, v_ref, do_ref, lse_ref, dl_ref, dq_out, dk_out, dv_out, dq_ref, dk_ref, dv_ref):
        kb = pl.program_id(1)

        @pl.when(kb == 0)
        def _():
            dq_ref[...] = jnp.zeros_like(dq_ref)

        dk_ref[...] = jnp.zeros_like(dk_ref)
        dv_ref[...] = jnp.zeros_like(dv_ref)
        k, v = k_ref[...], v_ref[...]

        def step(qb, masked):
            rows = pl.ds(pl.multiple_of(qb * t, t), t)
            q, dob = q_ref[rows, :], do_ref[rows, :]
            s = _dot_nt(q, k)
            if masked:
                s = jnp.where(_chunk_mask(0, 0, t, t), s, NEG)
            p = jnp.exp2(s - _widen(lse_ref[rows, :], t))
            ds = (p * (_dot_nt(dob, v) - _widen(dl_ref[rows, :], t))).astype(BF16)
            dv_ref[...] += _dot_tn(p.astype(BF16), dob)
            dk_ref[...] += _dot_tn(ds, q)
            dq_ref[rows, :] += _dot(ds, k)

        step(kb, True)

        @pl.loop(kb + 1, n)
        def _(qb):
            step(qb, False)

        dk_out[...] = (dk_ref[...] * (1.0 / LOG2E)).astype(BF16)
        dv_out[...] = dv_ref[...].astype(BF16)

        @pl.when(kb == n - 1)
        def _():
            dq_out[...] = (dq_ref[...] * scale).astype(BF16)

    qmap = lambda h, j: (0, h)
    kmap = lambda h, j: (j, h)
    return pl.pallas_call(
        body, name=name, grid=(MLA_HEADS, n),
        in_specs=[pl.BlockSpec((T, MLA_HP), qmap), pl.BlockSpec((t, MLA_HP), kmap), pl.BlockSpec((t, MLA_VD), kmap),
                  pl.BlockSpec((T, MLA_VD), qmap), pl.BlockSpec((T, MLA_VD), qmap), pl.BlockSpec((T, MLA_VD), qmap)],
        out_specs=[pl.BlockSpec((T, MLA_HP), qmap), pl.BlockSpec((t, MLA_HP), kmap), pl.BlockSpec((t, MLA_VD), kmap)],
        out_shape=[jax.ShapeDtypeStruct((T, MLA_HEADS * MLA_HP), BF16),
                   jax.ShapeDtypeStruct((T, MLA_HEADS * MLA_HP), BF16),
                   jax.ShapeDtypeStruct((T, MLA_HEADS * MLA_VD), BF16)],
        scratch_shapes=[pltpu.VMEM((T, MLA_HP), F32), pltpu.VMEM((t, MLA_HP), F32), pltpu.VMEM((t, MLA_VD), F32)],
        compiler_params=_cparams(("arbitrary", "arbitrary")),
    )(qf, kf, vf, do16, lse, delta)


MESH = pl.DeviceIdType.MESH
ANY = pl.BlockSpec(memory_space=pl.ANY)
_CHIP_FLIPS = ((1, 0), (0, 1), (1, 1))


def _place():
    return lax.axis_index("x"), lax.axis_index("y"), lax.axis_index("c")


def _other_chip(x, y, k):
    fx, fy = _CHIP_FLIPS[k]
    return ((1 - x) if fx else x), ((1 - y) if fy else y)


def _remote(src, dst, send_sems, recv_sems, k, to):
    return pltpu.make_async_remote_copy(src_ref=src, dst_ref=dst, send_sem=send_sems.at[k], recv_sem=recv_sems.at[k],
                                        device_id=to, device_id_type=MESH)


def _index(*vals):
    return jnp.stack(vals).astype(jnp.int32)


def _half(c, rows):
    return pl.ds(pl.multiple_of(c * rows, 16), rows)


def _gather_weights(parts, name, landed=None):
    n_w = len(parts)
    n_in = n_w if landed is None else 2 * n_w

    def body(*refs):
        ins, outs = refs[:n_w], refs[n_in:n_in + n_w]
        send_sems, recv_sems, local_sems = refs[n_in + n_w:]
        x, y, c = _place()
        j = 2 * x + y
        sibling = (x, y, 1 - c)
        chips = [_other_chip(x, y, k) for k in range(3)]
        pending = []
        for w in range(n_w):
            own = pltpu.make_async_copy(ins[w], outs[w].at[j], local_sems.at[w])
            own.start()
            pending.append(own)
        sent = []
        for w in range(n_w):
            if landed is not None:
                break
            r = _half(c, parts[w].shape[0] // 2)
            for k, (px, py) in enumerate(chips):
                cp = _remote(ins[w].at[r], outs[w].at[j, r], send_sems, recv_sems, 6 * w + k, (px, py, c))
                cp.start()
                sent.append(cp)
        for w in range(n_w):
            r = _half(c, parts[w].shape[0] // 2)
            for k, (px, py) in enumerate(chips):
                blk = outs[w].at[2 * px + py, r]
                if landed is None:
                    _remote(blk, blk, send_sems, recv_sems, 6 * w + k, (px, py, c)).wait_recv()
                cp = _remote(blk, blk, send_sems, recv_sems, 6 * w + 3 + k, sibling)
                cp.start()
                sent.append(cp)
        for w in range(n_w):
            r = _half(1 - c, parts[w].shape[0] // 2)
            for k, (px, py) in enumerate(chips):
                blk = outs[w].at[2 * px + py, r]
                _remote(blk, blk, send_sems, recv_sems, 6 * w + 3 + k, sibling).wait_recv()
        for cp in sent:
            cp.wait_send()
        for cp in pending:
            cp.wait()

    return pl.pallas_call(
        body, name=name, in_specs=[pl.BlockSpec(memory_space=pltpu.VMEM)] * n_w + [ANY] * (n_in - n_w),
        out_specs=[ANY] * n_w,
        out_shape=[jax.ShapeDtypeStruct((N_CHIPS, *p.shape), p.dtype) for p in parts],
        input_output_aliases={} if landed is None else {n_w + w: w for w in range(n_w)},
        scratch_shapes=[pltpu.SemaphoreType.DMA((6 * n_w,)), pltpu.SemaphoreType.DMA((6 * n_w,)),
                        pltpu.SemaphoreType.DMA((n_w,))],
        compiler_params=pltpu.CompilerParams(vmem_limit_bytes=VMEM_LIMIT),
    )(*parts, *(landed or []))


def _swap_halves(gs, name):
    n_w = len(gs)

    def body(*refs):
        g_refs, recv_refs = refs[:n_w], refs[n_w:2 * n_w]
        send_sems, recv_sems = refs[2 * n_w:]
        x, y, c = _place()
        sent = []
        for w in range(n_w):
            for jj in range(N_CHIPS):
                cp = _remote(g_refs[w].at[jj, 1 - c], recv_refs[w].at[jj], send_sems, recv_sems, N_CHIPS * w + jj,
                             (x, y, 1 - c))
                cp.start()
                sent.append(cp)
        for cp in sent:
            cp.wait()

    return pl.pallas_call(
        body, name=name, in_specs=[ANY] * n_w, out_specs=[ANY] * n_w,
        out_shape=[jax.ShapeDtypeStruct((N_CHIPS, *g.shape[2:]), g.dtype) for g in gs],
        scratch_shapes=[pltpu.SemaphoreType.DMA((N_CHIPS * n_w,)), pltpu.SemaphoreType.DMA((N_CHIPS * n_w,))],
    )(*gs)


def _pair_sum(g, recv, core, name):
    _, H, C = recv.shape
    tile = _pick(H, 256)

    def body(c_ref, own_ref, recv_ref, out_ref):
        out_ref[...] = (own_ref[...].astype(F32) + recv_ref[...].astype(F32)).astype(BF16)

    blk = pl.BlockSpec((None, tile, C), lambda jj, i, c: (jj, i, 0))
    return pl.pallas_call(
        body, name=name,
        grid_spec=pltpu.PrefetchScalarGridSpec(
            num_scalar_prefetch=1, grid=(N_CHIPS, H // tile),
            in_specs=[pl.BlockSpec((None, None, tile, C), lambda jj, i, c: (jj, c[0], i, 0)), blk],
            out_specs=blk),
        out_shape=jax.ShapeDtypeStruct((N_CHIPS, H, C), BF16),
        compiler_params=_cparams(("arbitrary", "arbitrary")),
    )(_index(core), g, recv)


def _chip_sum(g, recv, got, chip, core, name):
    _, H, C = recv.shape
    tile = _pick(H, 256)

    def body(s_ref, own_ref, recv_ref, g0_ref, g1_ref, g2_ref, out_ref):
        pair = own_ref[...].astype(F32) + recv_ref[...].astype(F32)
        out_ref[...] = ((pair + g0_ref[...].astype(F32)) + g1_ref[...].astype(F32)) + g2_ref[...].astype(F32)

    def got_spec(k):
        return pl.BlockSpec((None, tile, C), lambda i, s, k=k: (k, i, 0))

    return pl.pallas_call(
        body, name=name,
        grid_spec=pltpu.PrefetchScalarGridSpec(
            num_scalar_prefetch=1, grid=(H // tile,),
            in_specs=[pl.BlockSpec((None, None, tile, C), lambda i, s: (s[0], s[1], i, 0)),
                      pl.BlockSpec((None, tile, C), lambda i, s: (s[0], i, 0)), got_spec(0), got_spec(1), got_spec(2)],
            out_specs=pl.BlockSpec((None, tile, C), lambda i, s: (s[1], i, 0))),
        out_shape=jax.ShapeDtypeStruct((2, H, C), F32),
        compiler_params=_cparams(("arbitrary",)),
    )(_index(chip, core), g, recv, got, got, got)


def _scatter_chips(sums, name):
    n_w = len(sums)

    def body(*refs):
        a_refs, got_refs = refs[:n_w], refs[n_w:2 * n_w]
        send_sems, recv_sems = refs[2 * n_w:]
        x, y, c = _place()
        j = 2 * x + y
        sent = []
        for w in range(n_w):
            for k in range(3):
                px, py = _other_chip(x, y, k)
                pj = 2 * px + py
                cp = _remote(a_refs[w].at[pj], got_refs[w].at[(j - pj + 4) % 4 - 1], send_sems, recv_sems, 3 * w + k,
                             (px, py, c))
                cp.start()
                sent.append(cp)
        for w in range(n_w):
            for k in range(3):
                px, py = _other_chip(x, y, k)
                slot = got_refs[w].at[(2 * px + py - j + 4) % 4 - 1]
                _remote(slot, slot, send_sems, recv_sems, 3 * w + k, (px, py, c)).wait_recv()
        for cp in sent:
            cp.wait_send()

    return pl.pallas_call(
        body, name=name, in_specs=[ANY] * n_w, out_specs=[ANY] * n_w,
        out_shape=[jax.ShapeDtypeStruct((3, *a.shape[1:]), a.dtype) for a in sums],
        scratch_shapes=[pltpu.SemaphoreType.DMA((3 * n_w,)), pltpu.SemaphoreType.DMA((3 * n_w,))],
    )(*sums)


def _share_halves(reds):
    n_w = len(reds)

    def body(*refs):
        out_refs = refs[n_w:2 * n_w]
        send_sems, recv_sems = refs[2 * n_w:]
        x, y, c = _place()
        sent = []
        for w in range(n_w):
            blk = out_refs[w].at[c]
            cp = _remote(blk, blk, send_sems, recv_sems, w, (x, y, 1 - c))
            cp.start()
            sent.append(cp)
        for cp in sent:
            cp.wait()

    return pl.pallas_call(
        body, name="grad_share_halves", in_specs=[ANY] * n_w, out_specs=[ANY] * n_w,
        out_shape=[jax.ShapeDtypeStruct(r.shape, r.dtype) for r in reds],
        input_output_aliases={w: w for w in range(n_w)},
        scratch_shapes=[pltpu.SemaphoreType.DMA((n_w,)), pltpu.SemaphoreType.DMA((n_w,))],
    )(*reds)


def _allsum_small(v, name):
    R, W = v.shape
    n_dev = 8
    vm = pl.BlockSpec(memory_space=pltpu.VMEM)

    def body(v_ref, out_ref, buf, send_sems, recv_sems):
        x, y, c = _place()
        me = 4 * x + 2 * y + c
        buf[me] = v_ref[...]
        sent = []
        for k in range(1, n_dev):
            peer = ((1 - x) if k & 4 else x, (1 - y) if k & 2 else y, (1 - c) if k & 1 else c)
            cp = _remote(v_ref, buf.at[me], send_sems, recv_sems, k - 1, peer)
            cp.start()
            sent.append(cp)
        for cp in sent:
            cp.wait_recv()
        for cp in sent:
            cp.wait_send()
        acc = buf[0]
        for q in range(1, n_dev):
            acc = acc + buf[q]
        out_ref[...] = acc

    return pl.pallas_call(
        body, name=name, in_specs=[vm], out_specs=vm, out_shape=jax.ShapeDtypeStruct((R, W), v.dtype),
        scratch_shapes=[pltpu.VMEM((n_dev, R, W), v.dtype), pltpu.SemaphoreType.DMA((n_dev - 1,)),
                        pltpu.SemaphoreType.DMA((n_dev - 1,))],
    )(v)


HBM = pl.BlockSpec(memory_space=pltpu.HBM)
SEM = pl.BlockSpec(memory_space=pltpu.SEMAPHORE)
_DATAFLOW = pltpu.SideEffectType.DATAFLOW_SIDE_EFFECTING


def _split_start(name, srcs, land_shapes, n_copies, copies, after=()):
    ns, nl = len(srcs), len(land_shapes)
    lands = [lax.empty(s.shape, s.dtype) for s in land_shapes]

    def body(*refs):
        outs = refs[ns + nl + len(after):]
        for cp in copies(refs[:ns], refs[ns:ns + nl], outs[0], outs[1]):
            cp.start()
        outs[-1][...] = jnp.zeros_like(outs[-1])

    sems = pltpu.SemaphoreType.DMA((n_copies,))
    res = pl.pallas_call(
        body, name=name, in_specs=[HBM] * (ns + nl) + [ANY] * len(after),
        out_specs=(SEM, SEM, *[HBM] * (ns + nl), pl.BlockSpec(memory_space=pltpu.VMEM)),
        out_shape=(sems, sems, *[pltpu.HBM(a.shape, a.dtype) for a in srcs],
                   *[pltpu.HBM(s.shape, s.dtype) for s in land_shapes], jax.ShapeDtypeStruct((8, 128), F32)),
        input_output_aliases={i: 2 + i for i in range(ns + nl)},
        compiler_params=pltpu.CompilerParams(has_side_effects=_DATAFLOW),
    )(*[pltpu.with_memory_space_constraint(a, pltpu.HBM) for a in [*srcs, *lands]], *after)
    return res[0], res[1], list(res[2:2 + ns]), list(res[2 + ns:2 + ns + nl]), res[-1]


def _split_wait(name, send_sems, recv_sems, srcs, lands, copies, after=()):
    ns, nl = len(srcs), len(lands)

    def body(*refs):
        for cp in copies(refs[:ns], refs[ns:ns + nl], refs[ns + nl], refs[ns + nl + 1]):
            cp.wait_send()
            cp.wait_recv()

    res = pl.pallas_call(
        body, name=name, in_specs=[HBM] * (ns + nl) + [SEM, SEM] + [ANY] * len(after), out_specs=[HBM] * (ns + nl),
        out_shape=[pltpu.HBM(a.shape, a.dtype) for a in [*srcs, *lands]],
        input_output_aliases={i: i for i in range(ns + nl)},
        compiler_params=pltpu.CompilerParams(has_side_effects=_DATAFLOW),
    )(*srcs, *lands, send_sems, recv_sems, *after)
    return list(res[ns:])


def _gather_copies(rows):
    def copies(src_refs, land_refs, send_sems, recv_sems):
        x, y, c = _place()
        j = 2 * x + y
        out = []
        for w in range(len(src_refs)):
            r = _half(c, rows[w] // 2)
            for k in range(3):
                px, py = _other_chip(x, y, k)
                out.append(_remote(src_refs[w].at[r], land_refs[w].at[j, r], send_sems, recv_sems, 3 * w + k, (px, py, c)))
        return out
    return copies


def _scatter_copies(src_refs, land_refs, send_sems, recv_sems):
    x, y, c = _place()
    j = 2 * x + y
    out = []
    for w in range(len(src_refs)):
        for k in range(3):
            px, py = _other_chip(x, y, k)
            pj = 2 * px + py
            out.append(_remote(src_refs[w].at[pj], land_refs[w].at[(j - pj + 4) % 4 - 1], send_sems, recv_sems, 3 * w + k,
                               (px, py, c)))
    return out


def _reduce_begin(grads, core, tag):
    names = list(grads)
    gs = [grads[k].reshape(N_CHIPS, 2, -1, grads[k].shape[-1]) for k in names]
    recvs = _swap_halves(gs, f"grad_swap_halves_{tag}")
    sums = [_pair_sum(g, r, core, f"pair_sum_{k}") for k, g, r in zip(names, gs, recvs)]
    return names, gs, recvs, sums


def _reduce_end(begun, gots, chip, core):
    names, gs, recvs, _ = begun
    return {k: _chip_sum(g, r, t, chip, core, f"chip_sum_{k}") for k, g, r, t in zip(names, gs, recvs, gots)}


def _got_shapes(sums):
    return [jax.ShapeDtypeStruct((3, *a.shape[1:]), a.dtype) for a in sums]


def _adamw(w, g, m, v, name, layers=1, layer=0, into=None):
    shape = w.shape
    cols = shape[-1]
    w3, m3, v3 = (t.reshape(layers, -1, cols) for t in (w, m, v))
    rows = w3.shape[1]
    tile = _pick(rows, 256) if rows % 8 == 0 else rows
    n_in = 4 + (0 if into is None else 4)

    def body(*refs):
        wv, gv, mv, vv = (r[...] for r in refs[:4])
        g_ref, d_ref, m_ref, v_ref = refs[n_in:]
        m2 = ADAM_B1 * mv + (1.0 - ADAM_B1) * gv
        v2 = ADAM_B2 * vv + (1.0 - ADAM_B2) * jnp.square(gv)
        m_hat = m2 / (1.0 - ADAM_B1 ** ADAM_STEP)
        v_hat = v2 / (1.0 - ADAM_B2 ** ADAM_STEP)
        g_ref[...] = gv
        d_ref[...] = -ADAM_LR * (m_hat / (jnp.sqrt(v_hat) + ADAM_EPS) + ADAM_WD * wv)
        m_ref[...] = m2
        v_ref[...] = v2

    lay = pl.BlockSpec((None, tile, cols), lambda i: (layer, i, 0))
    out = jax.ShapeDtypeStruct((layers, rows, cols), F32)
    res = pl.pallas_call(
        body, name=name, grid=(rows // tile,),
        in_specs=[lay, pl.BlockSpec((tile, cols), lambda i: (i, 0)), lay, lay] + [ANY] * (n_in - 4),
        out_specs=[lay] * 4, out_shape=[out] * 4,
        input_output_aliases={} if into is None else {4 + k: k for k in range(4)},
        compiler_params=_cparams(("arbitrary",)),
    )(w3, g.reshape(rows, cols), m3, v3, *([] if into is None else [t.reshape(layers, rows, cols) for t in into]))
    return tuple(t.reshape(shape) for t in res)


ROW_F32, ROW_BF16 = (D_MODEL, F32), (D_MODEL, BF16)


def _res_norm(acc, h, gain):
    hh = h + acc
    return hh, _rms(hh, gain)


def _dx_norm_bwd(d, w, h, dres, gain, name, **kw):
    def epilogue(acc, hv, dr, g):
        dx, dg = _rms_bwd(hv, acc, g)
        return dr + dx, dr + dx, _colsum(dg)
    return _mm_rows(d, w, tb=True, extras=[h, dres], fulls=[gain], outs=[ROW_F32, ROW_BF16], accs=[((1, D_MODEL), F32)],
                    epilogue=epilogue, name=name, **kw)


def _tail_fwd(h1, hn2, p16, W, i, tag, next_gain=None, target=None):
    a = _mm(hn2, W["mlp_w1"][i], bblk=True, outs=[BF16], name=f"{tag}_mlp_w1",
            epilogue=lambda acc: (jnp.square(jnp.maximum(acc, 0.0)),))
    h2, hn3 = _mm_rows(a, W["mlp_w2"][i], extras=[h1], fulls=[W["ple_norm"][i:i + 1]], outs=[ROW_F32, ROW_BF16],
                       epilogue=_res_norm, name=f"{tag}_mlp_w2")
    gl = _mm(hn3, W["ple_gate_w"][i], name=f"{tag}_ple_gate")
    if target is None:
        def gated(acc, g, h, gain):
            hh = h + _sigmoid(g) * acc
            return hh, acc, _rms(hh, gain)
        h3, pp, hn = _mm_rows(p16[i], W["ple_proj_w"][i], bblk=True, extras=[gl, h2], fulls=[next_gain],
                              outs=[ROW_F32, ROW_BF16, ROW_BF16], epilogue=gated, name=f"{tag}_ple_proj")
        return h3, hn, (h1, hn2, a, h2, hn3, gl, pp)

    def gated_loss(acc, g, h, t):
        e = h + _sigmoid(g) * acc - t
        return acc, e * (1.0 / D_MODEL), jnp.full((1, 128), 0.5 / D_MODEL, F32) * jnp.sum(e * e)
    pp, dy, loss = _mm_rows(p16[i], W["ple_proj_w"][i], bblk=True, extras=[gl, h2, target], outs=[ROW_BF16, ROW_F32],
                            accs=[((1, 128), F32)], epilogue=gated_loss, name=f"{tag}_ple_proj")
    return dy, loss, (h1, hn2, a, h2, hn3, gl, pp)


def _tail_bwd(dh3, saved, p16, W, i, tag, after=()):
    h1, hn2, a, h2, hn3, gl, pp = saved

    def gate_bwd(d, g, ppv):
        gate = _sigmoid(g)
        return d * gate, d * ppv * gate * (1.0 - gate)

    def dw(kind, name):
        return (kind, 1, 0, None)

    dpp, dgl = _rows(gate_bwd, [dh3, gl, pp], [], [(D_MODEL, BF16), (D_MODEL, BF16)], name=f"{tag}_ple_gate_bwd",
                     after=after)
    d_proj = _mm(p16[i], dpp, ta=True, outs=[BF16], dw=dw("cols", "ple_proj_w"), name=f"{tag}_d_ple_proj")
    d_gate = _mm(hn3, dgl, ta=True, outs=[BF16], dw=dw("rows", "ple_gate_w"), name=f"{tag}_d_ple_gate")
    dh2, dh2_16, d_ple_norm = _dx_norm_bwd(dgl, W["ple_gate_w"][i], h2, dh3, W["ple_norm"][i:i + 1],
                                           f"{tag}_ple_gate_dx")
    d_w2 = _mm(a, dh2_16, ta=True, outs=[BF16], dw=dw("rows", "mlp_w2"), name=f"{tag}_d_mlp_w2")
    dz = _mm(dh2_16, W["mlp_w2"][i], tb=True, extras=[a], outs=[BF16], name=f"{tag}_mlp_w2_dx",
             epilogue=lambda acc, av: (acc * (2.0 * jnp.sqrt(av.astype(F32))),))
    d_w1 = _mm(hn2, dz, ta=True, outs=[BF16], dw=dw("cols", "mlp_w1"), name=f"{tag}_d_mlp_w1")
    dh1, dh1_16, d_mlp_norm = _dx_norm_bwd(dz, W["mlp_w1"][i], h1, dh2, W["mlp_norm"][i:i + 1], f"{tag}_mlp_w1_dx",
                                           bblk=True)
    big = {f"mlp_w1_{i}": d_w1, f"mlp_w2_{i}": d_w2, f"ple_gate_w_{i}": d_gate, f"ple_proj_w_{i}": d_proj}
    return dh1, dh1_16, big, dict(mlp_norm=d_mlp_norm, ple_norm=d_ple_norm)


def _ret_layer_fwd(h0, W, tabs, after=()):
    hn = _rows(lambda x, g: (_rms(x, g),), [h0], [W["mix_norm"][0:1]], [(D_MODEL, BF16)], name="ret_mix_norm",
               after=after)[0]
    proj = _mm(hn, W["ret_w_in"], bblk=True, outs=[BF16], name="ret_w_in")
    out, states = _ret_fwd(proj, tabs, "ret_scan")
    y = _ret_gate(out, proj, W["ret_gn"], "ret_gate")
    h1, hn2 = _mm_rows(y, W["ret_w_out"], extras=[h0], fulls=[W["mlp_norm"][0:1]], outs=[ROW_F32, ROW_BF16],
                       epilogue=_res_norm, name="ret_w_out")
    return h1, hn2, (h0, hn, proj, out, states, y)


def _ret_layer_bwd(dh1, dh1_16, saved, W, tabs, after=(), on_grads=None):
    h0, hn, proj, out, states, y = saved
    d_w_out = _mm(y, dh1_16, ta=True, outs=[BF16], dw=("rows", 1, 0, None), name="d_ret_w_out", after=after)
    dy = _mm(dh1_16, W["ret_w_out"], tb=True, name="ret_w_out_dx", after=after)
    dout, dproj, d_gn = _ret_gate_bwd(out, proj, W["ret_gn"], dy, "ret_gate_bwd")
    dproj = _ret_bwd(proj, states, dout, dproj, tabs, "ret_scan_bwd")
    d_w_in = _mm(hn, dproj, ta=True, outs=[BF16], dw=("cols", 1, 0, None), name="d_ret_w_in")
    big = dict(ret_w_in=d_w_in, ret_w_out=d_w_out)
    later = () if on_grads is None else on_grads(big)
    dh0, _, d_mix = _dx_norm_bwd(dproj, W["ret_w_in"], h0, dh1, W["mix_norm"][0:1], "ret_w_in_dx", bblk=True, tm=256,
                                 after=later)
    return dh0, big, dict(mix_norm=d_mix, ret_gn=d_gn)


def _mla_layer_fwd(h0, hn, W, tabs):
    proj = _mm(hn, W["mla_w_in"], name="mla_w_in")

    def low_rank_norm(pv, gq, gkv):
        return _rms(pv[:, :MLA_Q_RANK], gq), _rms(pv[:, MLA_Q_RANK:MLA_Q_RANK + MLA_KV_RANK], gkv)

    cqn, ckvn = _rows(low_rank_norm, [proj], [W["mla_q_a_norm"], W["mla_kv_a_norm"]],
                      [(MLA_Q_RANK, BF16), (MLA_KV_RANK, BF16)], name="mla_low_rank_norm")
    q = _mm(cqn, W["mla_w_uq"], bblk=True, outs=[BF16], name="mla_w_uq")
    kv = _mm(ckvn, W["mla_w_ukv"], bblk=True, outs=[BF16], name="mla_w_ukv")
    qf, kf, vf = _mla_prep(q, kv, proj, W["mla_q_norm"] * (MLA_QKD ** -0.5 * LOG2E), W["mla_k_norm"], tabs, "mla_prep")
    o, lse = _flash_fwd(qf, kf, vf, "mla_flash")
    h1, hn2 = _mm_rows(o, W["mla_w_out"], extras=[h0], fulls=[W["mlp_norm"][1:2]], outs=[ROW_F32, ROW_BF16],
                       epilogue=_res_norm, name="mla_w_out")
    return h1, hn2, (h0, hn, proj, cqn, ckvn, q, kv, qf, kf, vf, o, lse)


def _mla_layer_bwd(dh1, dh1_16, saved, W, tabs):
    h0, hn, proj, cqn, ckvn, q, kv, qf, kf, vf, o, lse = saved
    d_w_out = _mm(o, dh1_16, ta=True, outs=[BF16], dw=("rows", 1, 0, None), name="d_mla_w_out")
    def with_delta(acc, ov):
        parts = []
        for h in range(MLA_HEADS):
            sl = slice(h * MLA_VD, (h + 1) * MLA_VD)
            d = jnp.sum(acc[:, sl] * ov[:, sl], axis=-1, keepdims=True)
            parts.append(jnp.broadcast_to(d, (d.shape[0], MLA_VD)))
        return jnp.concatenate(parts, axis=-1), acc

    delta, do16 = _mm_rows(dh1_16, W["mla_w_out"], tb=True, extras=[o], outs=[ROW_F32, ROW_BF16], epilogue=with_delta,
                           name="mla_w_out_dx")
    dqf, dkf, dvf = _flash_bwd(qf, kf, vf, do16, lse, delta, "mla_flash_bwd")
    dq, dkv, dkr, d_gq, d_gk = _mla_prep_bwd(q, kv, proj, W["mla_q_norm"], W["mla_k_norm"], tabs, dqf, dkf, dvf,
                                             "mla_prep_bwd")
    d_w_uq = _mm(cqn, dq, ta=True, outs=[BF16], dw=("cols", 1, 0, None), name="d_mla_w_uq")
    dcqn = _mm(dq, W["mla_w_uq"], tb=True, bblk=True, name="mla_w_uq_dx")
    d_w_ukv = _mm(ckvn, dkv, ta=True, outs=[BF16], dw=("cols", 1, 0, None), name="d_mla_w_ukv")
    dckvn = _mm(dkv, W["mla_w_ukv"], tb=True, bblk=True, name="mla_w_ukv_dx")

    def low_rank_bwd(pv, dcq, dckv, dkr_v, gq, gkv):
        dxq, dgq = _rms_bwd(pv[:, :MLA_Q_RANK], dcq, gq)
        dxkv, dgkv = _rms_bwd(pv[:, MLA_Q_RANK:MLA_Q_RANK + MLA_KV_RANK], dckv, gkv)
        return jnp.concatenate([dxq, dxkv, dkr_v], axis=-1), _colsum(dgq), _colsum(dgkv)

    dproj, d_gqa, d_gkva = _rows(low_rank_bwd, [proj, dcqn, dckvn, dkr], [W["mla_q_a_norm"], W["mla_kv_a_norm"]],
                                 [(MLA_IN_PAD, BF16)], [((1, MLA_Q_RANK), F32), ((1, MLA_KV_RANK), F32)],
                                 name="mla_low_rank_norm_bwd")
    d_w_in = _mm(hn, dproj, ta=True, outs=[BF16], dw=("rows", 1, 0, None), name="d_mla_w_in")
    dh0, dh0_16, d_mix = _dx_norm_bwd(dproj, W["mla_w_in"], h0, dh1, W["mix_norm"][1:2], "mla_w_in_dx")
    return (dh0, dh0_16, dict(mla_w_in=d_w_in, mla_w_uq=d_w_uq, mla_w_ukv=d_w_ukv, mla_w_out=d_w_out),
            dict(mix_norm=d_mix, mla_q_a_norm=d_gqa, mla_kv_a_norm=d_gkva, mla_q_norm=d_gq, mla_k_norm=d_gk))


def _local_step(x, p16, target, W):
    T = x.shape[0]
    ret_tabs, mla_tabs = _ret_tables(T), _mla_tables(T)
    h1, hn, s_ret = _ret_layer_fwd(x, W, ret_tabs)
    h3, hn, s_tail0 = _tail_fwd(h1, hn, p16, W, 0, "l0", next_gain=W["mix_norm"][1:2])
    h4, hn, s_mla = _mla_layer_fwd(h3, hn, W, mla_tabs)
    dy, loss, s_tail1 = _tail_fwd(h4, hn, p16, W, 1, "l1", target=target)
    dh4, dh4_16, g_t1, n_t1 = _tail_bwd(dy, s_tail1, p16, W, 1, "l1")
    dh3, _, g_mla, n_mla = _mla_layer_bwd(dh4, dh4_16, s_mla, W, mla_tabs)
    dh1, dh1_16, g_t0, n_t0 = _tail_bwd(dh3, s_tail0, p16, W, 0, "l0")
    dx, g_ret, n_ret = _ret_layer_bwd(dh1, dh1_16, s_ret, W, ret_tabs)
    return loss, dx, {**g_ret, **g_t0, **g_mla, **g_t1}, _small_grads(n_ret, n_t0, n_mla, n_t1)


def _loss_head(y, target):
    def fn(yv, tv):
        e = yv - tv
        return e * (1.0 / D_MODEL), jnp.full((1, 128), 0.5 / D_MODEL, F32) * jnp.sum(e * e)
    return _rows(fn, [y, target], [], [(D_MODEL, F32)], [((1, 128), F32)], name="loss_head")


def _small_grads(n_ret, n_t0, n_mla, n_t1):
    return dict(
        mix_norm=jnp.concatenate([n_ret["mix_norm"], n_mla["mix_norm"]], axis=0),
        mlp_norm=jnp.concatenate([n_t0["mlp_norm"], n_t1["mlp_norm"]], axis=0),
        ple_norm=jnp.concatenate([n_t0["ple_norm"], n_t1["ple_norm"]], axis=0),
        ret_gn=n_ret["ret_gn"], mla_q_a_norm=n_mla["mla_q_a_norm"], mla_kv_a_norm=n_mla["mla_kv_a_norm"],
        mla_q_norm=n_mla["mla_q_norm"], mla_k_norm=n_mla["mla_k_norm"])


_ORDER = ("mix_norm", "ret_w_in", "ret_gn", "ret_w_out", "mla_w_in", "mla_q_a_norm", "mla_kv_a_norm", "mla_w_uq",
          "mla_w_ukv", "mla_q_norm", "mla_k_norm", "mla_w_out", "mlp_norm", "mlp_w1", "mlp_w2", "ple_norm",
          "ple_gate_w", "ple_proj_w")
_TWO_LAYER = ("mlp_w1", "mlp_w2", "ple_gate_w", "ple_proj_w")
HEADS_PER_CHIP = MLA_HEADS // N_CHIPS
GAIN_ROWS = 32


def _travel_parts(w):
    uq = jnp.pad(w["mla_w_uq"][0].reshape(MLA_Q_RANK, HEADS_PER_CHIP, MLA_QKD), ((0, 0), (0, 0), (0, MLA_HP - MLA_QKD)))
    parts = {"ret_w_in": w["ret_w_in"][0], "ret_w_out": w["ret_w_out"][0]}
    for k in _TWO_LAYER:
        parts[k + "_0"] = w[k][0]
    parts["mla_w_in"] = jnp.pad(w["mla_w_in"][0], ((0, 0), (0, MLA_IN_PAD - MLA_IN)))
    parts["mla_w_uq"] = uq.reshape(MLA_Q_RANK, HEADS_PER_CHIP * MLA_HP)
    parts["mla_w_ukv"] = w["mla_w_ukv"][0]
    parts["mla_w_out"] = w["mla_w_out"][0]
    for k in _TWO_LAYER:
        parts[k + "_1"] = w[k][1]
    gains = jnp.concatenate([_pad_row(w["ret_gn"]), _pad_row(w["mla_q_a_norm"]), _pad_row(w["mla_kv_a_norm"]),
                             jnp.zeros((GAIN_ROWS - 3, PACK_W), F32)], axis=0)
    return {"gains": gains, **{k: v.astype(BF16) for k, v in parts.items()}}


def _full_weights(full):
    rows = lambda a: a.reshape(-1, a.shape[-1])
    W = {k: full[k] for k in ("ret_w_in", "mla_w_uq", "mla_w_ukv")}
    for k in ("ret_w_out", "mla_w_in", "mla_w_out"):
        W[k] = rows(full[k])
    W["mlp_w1"] = [full["mlp_w1_0"], full["mlp_w1_1"]]
    W["ple_proj_w"] = [full["ple_proj_w_0"], full["ple_proj_w_1"]]
    W["mlp_w2"] = [rows(full["mlp_w2_0"]), rows(full["mlp_w2_1"])]
    W["ple_gate_w"] = [rows(full["ple_gate_w_0"]), rows(full["ple_gate_w_1"])]
    return W


def _shard_grad(name, red, shape):
    if name == "mla_w_in":
        red = red.reshape(-1, MLA_IN_PAD)[:, :MLA_IN]
    elif name == "mla_w_uq":
        red = red.reshape(MLA_Q_RANK, HEADS_PER_CHIP, MLA_HP)[:, :, :MLA_QKD]
    return red.reshape(shape)


def _pad_row(v):
    v = v.reshape(1, -1)
    return jnp.pad(v, ((0, 0), (0, PACK_W - v.shape[1])))


def kernel(x, p, mix_norm, ret_w_in, ret_gn, ret_w_out, mla_w_in, mla_q_a_norm, mla_kv_a_norm, mla_w_uq, mla_w_ukv, mla_q_norm, mla_k_norm, mla_w_out, mlp_norm, mlp_w1, mlp_w2, ple_norm, ple_gate_w, ple_proj_w, loss_target, m_mix_norm, m_ret_w_in, m_ret_gn, m_ret_w_out, m_mla_w_in, m_mla_q_a_norm, m_mla_kv_a_norm, m_mla_w_uq, m_mla_w_ukv, m_mla_q_norm, m_mla_k_norm, m_mla_w_out, m_mlp_norm, m_mlp_w1, m_mlp_w2, m_ple_norm, m_ple_gate_w, m_ple_proj_w, v_mix_norm, v_ret_w_in, v_ret_gn, v_ret_w_out, v_mla_w_in, v_mla_q_a_norm, v_mla_kv_a_norm, v_mla_w_uq, v_mla_w_ukv, v_mla_q_norm, v_mla_k_norm, v_mla_w_out, v_mlp_norm, v_mlp_w1, v_mlp_w2, v_ple_norm, v_ple_gate_w, v_ple_proj_w):
    w = dict(mix_norm=mix_norm, ret_w_in=ret_w_in, ret_gn=ret_gn, ret_w_out=ret_w_out, mla_w_in=mla_w_in,
             mla_q_a_norm=mla_q_a_norm, mla_kv_a_norm=mla_kv_a_norm, mla_w_uq=mla_w_uq, mla_w_ukv=mla_w_ukv,
             mla_q_norm=mla_q_norm, mla_k_norm=mla_k_norm, mla_w_out=mla_w_out, mlp_norm=mlp_norm, mlp_w1=mlp_w1,
             mlp_w2=mlp_w2, ple_norm=ple_norm, ple_gate_w=ple_gate_w, ple_proj_w=ple_proj_w)
    m = dict(mix_norm=m_mix_norm, ret_w_in=m_ret_w_in, ret_gn=m_ret_gn, ret_w_out=m_ret_w_out, mla_w_in=m_mla_w_in,
             mla_q_a_norm=m_mla_q_a_norm, mla_kv_a_norm=m_mla_kv_a_norm, mla_w_uq=m_mla_w_uq, mla_w_ukv=m_mla_w_ukv,
             mla_q_norm=m_mla_q_norm, mla_k_norm=m_mla_k_norm, mla_w_out=m_mla_w_out, mlp_norm=m_mlp_norm,
             mlp_w1=m_mlp_w1, mlp_w2=m_mlp_w2, ple_norm=m_ple_norm, ple_gate_w=m_ple_gate_w, ple_proj_w=m_ple_proj_w)
    v = dict(mix_norm=v_mix_norm, ret_w_in=v_ret_w_in, ret_gn=v_ret_gn, ret_w_out=v_ret_w_out, mla_w_in=v_mla_w_in,
             mla_q_a_norm=v_mla_q_a_norm, mla_kv_a_norm=v_mla_kv_a_norm, mla_w_uq=v_mla_w_uq, mla_w_ukv=v_mla_w_ukv,
             mla_q_norm=v_mla_q_norm, mla_k_norm=v_mla_k_norm, mla_w_out=v_mla_w_out, mlp_norm=v_mlp_norm,
             mlp_w1=v_mlp_w1, mlp_w2=v_mlp_w2, ple_norm=v_ple_norm, ple_gate_w=v_ple_gate_w, ple_proj_w=v_ple_proj_w)
    xi, yi, ci = _place()
    chip = 2 * xi + yi
    n = N_CHIPS

    parts = _travel_parts(w)
    first = ("gains", "ret_w_in", "ret_w_out")
    later = [k for k in parts if k not in first]
    full = dict(zip(first, _gather_weights([parts[k] for k in first], "gather_first")))
    later_copies = _gather_copies([parts[k].shape[0] for k in later])
    g_send, g_recv, later_src, later_land, g_token = _split_start(
        "gather_later_start", [parts[k] for k in later],
        [jax.ShapeDtypeStruct((n, *parts[k].shape), BF16) for k in later], 3 * len(later), later_copies,
        after=[full["ret_w_in"]])
    gains = full["gains"]
    W = dict(mix_norm=mix_norm, mlp_norm=mlp_norm, ple_norm=ple_norm,
             mla_q_norm=jnp.pad(mla_q_norm, ((0, 0), (0, MLA_HP - MLA_QKD))),
             mla_k_norm=jnp.pad(mla_k_norm, ((0, 0), (0, MLA_HP - MLA_QKD))),
             ret_w_in=full["ret_w_in"], ret_w_out=full["ret_w_out"].reshape(-1, D_MODEL),
             ret_gn=gains[:, 0, :RET_HEADS * 128].reshape(n, RET_HEADS, 128).transpose(1, 0, 2).reshape(RET_HEADS, RET_DV),
             mla_q_a_norm=gains[:, 1, :MLA_Q_RANK // n].reshape(1, MLA_Q_RANK),
             mla_kv_a_norm=gains[:, 2, :MLA_KV_RANK // n].reshape(1, MLA_KV_RANK))
    x0, p16, target = x[0], p[:, 0].astype(BF16), loss_target[0]
    T = x0.shape[0]
    ret_tabs, mla_tabs = _ret_tables(T), _mla_tables(T)

    h1, hn, s_ret = _ret_layer_fwd(x0, W, ret_tabs, after=[g_token])
    landed = _split_wait("gather_later_wait", g_send, g_recv, later_src, later_land, later_copies, after=[h1])
    full.update(zip(later, _gather_weights([parts[k] for k in later], "gather_later_finish", landed=landed)))
    W.update(_full_weights(full))
    h3, hn, s_tail0 = _tail_fwd(h1, hn, p16, W, 0, "l0", next_gain=W["mix_norm"][1:2])
    h4, hn, s_mla = _mla_layer_fwd(h3, hn, W, mla_tabs)
    dy, loss, s_tail1 = _tail_fwd(h4, hn, p16, W, 1, "l1", target=target)

    dh4, dh4_16, g_t1, n_t1 = _tail_bwd(dy, s_tail1, p16, W, 1, "l1")
    dh3, _, g_mla, n_mla = _mla_layer_bwd(dh4, dh4_16, s_mla, W, mla_tabs)
    beg_a = _reduce_begin({**g_mla, **g_t1}, ci, "a")
    a_send, a_recv, a_src, a_land, a_token = _split_start(
        "scatter_a_start", beg_a[3], _got_shapes(beg_a[3]), 3 * len(beg_a[3]), _scatter_copies)
    dh1, dh1_16, g_t0, n_t0 = _tail_bwd(dh3, s_tail0, p16, W, 0, "l0", after=[a_token])
    beg_b = _reduce_begin(g_t0, ci, "b")
    b_send, b_recv, b_src, b_land, b_token = _split_start(
        "scatter_b_start", beg_b[3], _got_shapes(beg_b[3]), 3 * len(beg_b[3]), _scatter_copies)
    stage_c = {}

    def start_c(g_ret):
        beg = _reduce_begin(g_ret, ci, "c")
        stage_c["beg"] = beg
        stage_c["st"] = _split_start("scatter_c_start", beg[3], _got_shapes(beg[3]), 3 * len(beg[3]), _scatter_copies)
        return [stage_c["st"][4]]

    dx, _, n_ret = _ret_layer_bwd(dh1, dh1_16, s_ret, W, ret_tabs, after=[b_token], on_grads=start_c)
    got_a = _split_wait("scatter_a_wait", a_send, a_recv, a_src, a_land, _scatter_copies, after=[dx])
    got_b = _split_wait("scatter_b_wait", b_send, b_recv, b_src, b_land, _scatter_copies, after=[dx])
    got_c = _split_wait("scatter_c_wait", *stage_c["st"][:4], _scatter_copies, after=[dx])
    red = {**_reduce_end(beg_a, got_a, chip, ci), **_reduce_end(beg_b, got_b, chip, ci),
           **_reduce_end(stage_c["beg"], got_c, chip, ci)}
    red = dict(zip(red, _share_halves(list(red.values()))))
    gs = _small_grads(n_ret, n_t0, n_mla, n_t1)
    small_g = jnp.concatenate([
        gs["mix_norm"], gs["mlp_norm"], gs["ple_norm"], gs["ret_gn"].reshape(2, PACK_W), _pad_row(gs["mla_q_a_norm"]),
        _pad_row(gs["mla_kv_a_norm"]), _pad_row(gs["mla_q_norm"][:, :MLA_QKD]), _pad_row(gs["mla_k_norm"][:, :MLA_QKD]),
        _pad_row(loss[:, :1]), jnp.zeros((3, PACK_W), F32)], axis=0)
    tot = _allsum_small(small_g, "sum_small_grads")
    gn_all = tot[6:8].reshape(RET_HEADS, n, -1)
    g_small = dict(
        mix_norm=tot[0:2], mlp_norm=tot[2:4], ple_norm=tot[4:6],
        ret_gn=lax.dynamic_index_in_dim(gn_all, chip, axis=1, keepdims=False),
        mla_q_a_norm=lax.dynamic_index_in_dim(tot[8, :MLA_Q_RANK].reshape(n, -1), chip, axis=0, keepdims=True),
        mla_kv_a_norm=lax.dynamic_index_in_dim(tot[9, :MLA_KV_RANK].reshape(n, -1), chip, axis=0, keepdims=True),
        mla_q_norm=tot[10:11, :MLA_QKD], mla_k_norm=tot[11:12, :MLA_QKD])
    loss_out = tot[12, 0]

    outs = []
    for k in _ORDER:
        if k in _TWO_LAYER:
            res = None
            for i in (1, 0):
                res = _adamw(w[k], red[f"{k}_{i}"], m[k], v[k], f"adamw_{k}_{i}", layers=2, layer=i, into=res)
        elif k in red:
            res = _adamw(w[k], _shard_grad(k, red[k], w[k].shape), m[k], v[k], f"adamw_{k}")
        else:
            res = _adamw(w[k], g_small[k], m[k], v[k], f"adamw_{k}")
        outs.append(res)
    return (loss_out, dx[None], *[o[0] for o in outs], *[o[1] for o in outs], *[o[2] for o in outs],
            *[o[3] for o in outs])
```

```python
import functools

import jax
import jax.numpy as jnp
import numpy as np
from jax import lax
from jax.experimental import pallas as pl
from jax.experimental.pallas import tpu as pltpu

F32 = jnp.float32
BF16 = jnp.bfloat16

EPS = 1e-6
D_MODEL = 1024
CHUNK = 64
ROPE_THETA = 10000.0
RET_HEADS = 4
RET_DK = 256
RET_DV = 512
RET_GROUP = 1
RET_BLOCK = 256
MLA_HEADS = 8
MLA_NOPE = 128
MLA_ROPE = 64
MLA_QKD = 192
MLA_VD = 128
MLA_HP = 256
MLA_Q_RANK = 384
MLA_KV_RANK = 256
MLA_IN = 704
MLA_IN_PAD = 768
D_FF = 4096
PLE_DIM = 256
N_CHIPS = 4

ADAM_LR = 0.001
ADAM_B1 = 0.9
ADAM_B2 = 0.999
ADAM_EPS = 1e-08
ADAM_WD = 0.01
ADAM_STEP = 10

VMEM_LIMIT = 56 * 1024 * 1024
PACK_W = 1024
NEG = -1e30
LOG2E = 1.4426950408889634
FLASH_T = 512
FLASH_HEADS = 2
MM_SUB_ROWS = 256


def _cparams(sem=None):
    return pltpu.CompilerParams(dimension_semantics=sem, vmem_limit_bytes=VMEM_LIMIT)


def _pick(dim, pref):
    if dim <= pref:
        return dim
    t = pref
    while dim % t:
        t //= 2
    return t


def _mm(a, b, *, name, ta=False, tb=False, bblk=False, outs=None, extras=(), epilogue=None, dw=None,
        tm=1024, tn=512, after=()):
    if ta:
        K, M = a.shape
    else:
        M, K = a.shape
    if bblk and tb:
        nb, N, Kq = b.shape
        assert nb * Kq == K
    elif bblk:
        nb, Kb, Nq = b.shape
        N = nb * Nq
        assert Kb == K
    else:
        N = b.shape[0] if tb else b.shape[1]
    tn = _pick(Nq if (bblk and not tb) else N, tn)
    if dw is not None and dw[0] == "cols":
        tn = _pick(N // N_CHIPS, tn)
    tm = _pick(M // N_CHIPS if (dw is not None and dw[0] == "rows") else M, tm)
    grid = (M // tm, N // tn)

    a_spec = pl.BlockSpec((K, tm), lambda i, j: (0, i)) if ta else pl.BlockSpec((tm, K), lambda i, j: (i, 0))
    if bblk and tb:
        b_spec = pl.BlockSpec((nb, tn, Kq), lambda i, j: (0, j, 0))
    elif bblk:
        npb = Nq // tn
        b_spec = pl.BlockSpec((None, K, tn), lambda i, j: (j // npb, 0, j % npb))
    elif tb:
        b_spec = pl.BlockSpec((tn, K), lambda i, j: (j, 0))
    else:
        b_spec = pl.BlockSpec((K, tn), lambda i, j: (0, j))
    in_specs = [a_spec, b_spec] + [pl.BlockSpec((tm, tn), lambda i, j: (i, j)) for _ in extras]
    args = [a, b, *extras]
    aliases = {}
    if outs is None:
        outs = [F32]
    if dw is None:
        o_specs = [pl.BlockSpec((tm, tn), lambda i, j: (i, j)) for _ in outs]
        o_shapes = [jax.ShapeDtypeStruct((M, N), dt) for dt in outs]
    else:
        kind, layers, layer, into = dw
        if kind == "cols":
            per = (N // N_CHIPS) // tn
            o_specs = [pl.BlockSpec((None, None, tm, tn), lambda i, j: (j // per, layer, i, j % per))]
            o_shapes = [jax.ShapeDtypeStruct((N_CHIPS, layers, M, N // N_CHIPS), outs[0])]
        else:
            per = (M // N_CHIPS) // tm
            o_specs = [pl.BlockSpec((None, None, tm, tn), lambda i, j: (i // per, layer, i % per, j))]
            o_shapes = [jax.ShapeDtypeStruct((N_CHIPS, layers, M // N_CHIPS, N), outs[0])]
        if into is not None:
            aliases = {len(args): 0}
            in_specs.append(pl.BlockSpec(memory_space=pl.ANY))
            args.append(into)
    for t in after:
        in_specs.append(pl.BlockSpec(memory_space=pl.ANY))
        args.append(t)
    n_e, n_o = len(extras), len(outs)

    sub = _pick(tm, MM_SUB_ROWS)

    def body(a_ref, b_ref, *rest):
        e_refs, o_refs = rest[:n_e], rest[len(rest) - n_o:]
        for r0 in range(0, tm, sub):
            rows = slice(r0, r0 + sub)
            av = (a_ref[:, rows] if ta else a_ref[rows, :]).astype(BF16)
            if bblk and tb:
                acc = _dot_nt(av[:, :Kq], b_ref[0].astype(BF16))
                for s in range(1, nb):
                    acc = acc + _dot_nt(av[:, s * Kq:(s + 1) * Kq], b_ref[s].astype(BF16))
            elif ta:
                acc = _dot_tn(av, b_ref[...].astype(BF16))
            elif tb:
                acc = _dot_nt(av, b_ref[...].astype(BF16))
            else:
                acc = _dot(av, b_ref[...].astype(BF16))
            vals = (acc,) if epilogue is None else epilogue(acc, *[e[rows, :] for e in e_refs])
            for o, v in zip(o_refs, vals):
                o[rows, :] = v.astype(o.dtype)

    res = pl.pallas_call(
        body, name=name, grid=grid, in_specs=in_specs, out_specs=o_specs, out_shape=o_shapes,
        input_output_aliases=aliases, compiler_params=_cparams(("parallel", "arbitrary")),
    )(*args)
    return res[0] if n_o == 1 else res


def _mm_rows(a, b, *, name, epilogue, outs, tb=False, bblk=False, extras=(), fulls=(), accs=(), tm=512, after=()):
    M, K = a.shape
    tm = _pick(M, tm)
    sub = _pick(tm, MM_SUB_ROWS)
    nb = b.shape[0] if bblk else 1
    n_e, n_f, n_o, n_a = len(extras), len(fulls), len(outs), len(accs)
    n_in = 2 + n_e + n_f + len(after)

    def whole(t):
        return pl.BlockSpec(t.shape, lambda i, nd=t.ndim: (0,) * nd)

    in_specs = [pl.BlockSpec((tm, K), lambda i: (i, 0)), whole(b)]
    in_specs += [pl.BlockSpec((tm, e.shape[1]), lambda i: (i, 0)) for e in extras] + [whole(f) for f in fulls]
    in_specs += [pl.BlockSpec(memory_space=pl.ANY) for _ in after]
    out_specs = [pl.BlockSpec((tm, w), lambda i: (i, 0)) for w, _ in outs] + [pl.BlockSpec(s, lambda i: (0, 0)) for s, _ in accs]
    out_shape = [jax.ShapeDtypeStruct((M, w), dt) for w, dt in outs] + [jax.ShapeDtypeStruct(s, dt) for s, dt in accs]

    def body(a_ref, b_ref, *rest):
        e_refs, f_refs = rest[:n_e], rest[n_e:n_e + n_f]
        o_refs, acc_refs = rest[n_in - 2:n_in - 2 + n_o], rest[n_in - 2 + n_o:]
        fv = [f[...] for f in f_refs]
        totals = None
        for r0 in range(0, tm, sub):
            rows = slice(r0, r0 + sub)
            av = a_ref[rows, :].astype(BF16)
            if bblk and tb:
                kq = K // nb
                acc = _dot_nt(av[:, :kq], b_ref[0])
                for s in range(1, nb):
                    acc = acc + _dot_nt(av[:, s * kq:(s + 1) * kq], b_ref[s])
            elif bblk:
                acc = jnp.concatenate([_dot(av, b_ref[s]) for s in range(nb)], axis=-1)
            elif tb:
                acc = _dot_nt(av, b_ref[...])
            else:
                acc = _dot(av, b_ref[...])
            vals = epilogue(acc, *[e[rows, :] for e in e_refs], *fv)
            for o, v in zip(o_refs, vals[:n_o]):
                o[rows, :] = v.astype(o.dtype)
            part = vals[n_o:]
            totals = part if totals is None else [t + p for t, p in zip(totals, part)]
        first_step = pl.program_id(0) == 0
        for o, v in zip(acc_refs, totals):
            @pl.when(first_step)
            def _(o=o, v=v):
                o[...] = v.astype(o.dtype)

            @pl.when(jnp.logical_not(first_step))
            def _(o=o, v=v):
                o[...] += v.astype(o.dtype)

    return pl.pallas_call(
        body, name=name, grid=(M // tm,), in_specs=in_specs, out_specs=out_specs, out_shape=out_shape,
        compiler_params=_cparams(("arbitrary",)),
    )(a, b, *extras, *fulls, *after)


def _rows(fn, rows, fulls, outs, accs=(), *, name, tile=512, after=()):
    first = rows[0][0] if isinstance(rows[0], tuple) else rows[0]
    T = first.shape[0]
    tile = _pick(T, tile)
    in_specs, args = [], []
    for r in rows:
        if isinstance(r, tuple):
            arr, w, cb = r
            in_specs.append(pl.BlockSpec((tile, w), lambda i, cb=cb: (i, cb)))
        else:
            arr = r
            in_specs.append(pl.BlockSpec((tile, arr.shape[1]), lambda i: (i, 0)))
        args.append(arr)
    for f in fulls:
        in_specs.append(pl.BlockSpec(f.shape, lambda i, nd=f.ndim: (0,) * nd))
        args.append(f)
    outs = [o if len(o) == 4 else (*o, o[0], 0) for o in outs]
    out_specs = [pl.BlockSpec((tile, w), lambda i, cb=cb: (i, cb)) for w, _, _, cb in outs]
    out_specs += [pl.BlockSpec(s, lambda i: (0, 0)) for s, _ in accs]
    out_shape = [jax.ShapeDtypeStruct((T, tw), dt) for _, dt, tw, _ in outs]
    out_shape += [jax.ShapeDtypeStruct(s, dt) for s, dt in accs]
    n_in, n_out = len(args), len(outs)
    for t in after:
        in_specs.append(pl.BlockSpec(memory_space=pl.ANY))
        args.append(t)

    def body(*refs):
        vals = fn(*[r[...] for r in refs[:n_in]])
        o_refs = refs[len(args):]
        for o, v in zip(o_refs[:n_out], vals[:n_out]):
            o[...] = v.astype(o.dtype)
        first_step = pl.program_id(0) == 0
        for o, v in zip(o_refs[n_out:], vals[n_out:]):
            @pl.when(first_step)
            def _(o=o, v=v):
                o[...] = v.astype(o.dtype)

            @pl.when(jnp.logical_not(first_step))
            def _(o=o, v=v):
                o[...] += v.astype(o.dtype)

    res = pl.pallas_call(
        body, name=name, grid=(T // tile,), in_specs=in_specs, out_specs=out_specs, out_shape=out_shape,
        compiler_params=_cparams(("arbitrary",)),
    )(*args)
    return res


def _rowsum(v, mxu):
    if not mxu:
        return jnp.sum(v, axis=-1, keepdims=True)
    ones = jnp.ones((v.shape[1], v.shape[1]), BF16)
    hi = v.astype(BF16)
    lo = (v - hi.astype(F32)).astype(BF16)
    return _dot(hi, ones) + _dot(lo, ones)


def _rms(x, g, mxu=False):
    r = lax.rsqrt(_rowsum(x * x, mxu) / x.shape[-1] + EPS)
    return (x * r) * g


def _rms_bwd(x, dy, g, n=None, mxu=False):
    n = x.shape[-1] if n is None else n
    r = lax.rsqrt(_rowsum(x * x, mxu) / n + EPS)
    xh = x * r
    dxh = dy * g
    dx = r * (dxh - xh * (_rowsum(dxh * xh, mxu) / n))
    return dx, dy * xh


def _colsum(v):
    return jnp.sum(v, axis=0, keepdims=True)


def _sigmoid(x):
    return 1.0 / (1.0 + jnp.exp(-x))


def _widen(v, width):
    reps = width // v.shape[1]
    return v if reps == 1 else jnp.concatenate([v] * reps, axis=-1)


def _norm_fwd(h, gain, name):
    return _rows(lambda x, g: (_rms(x, g),), [h], [gain], [(h.shape[1], BF16)], name=name)[0]


def _norm_bwd(h, dhn, gain, dres, name):
    def fn(x, dy, dr, g):
        dx, dg = _rms_bwd(x, dy, g)
        return dr + dx, dr + dx, _colsum(dg)
    d = h.shape[1]
    return _rows(fn, [h, dhn, dres], [gain], [(d, F32), (d, BF16)], [((1, d), F32)], name=name)


def _rope_angles(T, dim):
    inv = (1.0 / (np.float32(ROPE_THETA) ** (np.arange(0, dim, 2, dtype=np.float32) / np.float32(dim)))).astype(np.float32)
    return np.arange(T, dtype=np.float32)[:, None] * inv[None, :]


def _ret_tables(T):
    ang = _rope_angles(T, RET_DK)
    log_gamma = np.log(np.float32(1.0) - np.float32(2.0) ** (-5.0 - np.arange(RET_HEADS, dtype=np.float32)))
    idx = np.arange(RET_BLOCK, dtype=np.float32)
    chunk = np.arange(RET_BLOCK) // CHUNK
    dist = idx[:, None] - idx[None, :]
    seen = np.where(chunk[:, None] == chunk[None, :], np.abs(dist), np.where(chunk[:, None] > chunk[None, :], dist, np.inf))
    intra = np.exp(log_gamma[:, None, None] * seen[None].astype(np.float32))
    qd = np.exp(log_gamma[:, None] * (idx + 1.0))[:, :, None]
    kd = np.exp(log_gamma[:, None] * (RET_BLOCK - 1.0 - idx))[:, :, None]
    cd = np.exp(log_gamma * RET_BLOCK)[:, None, None]
    return tuple(jnp.asarray(t, F32) for t in (np.cos(ang), np.sin(ang), intra, qd, kd, cd))


def _rope_half(x, c, s):
    x1, x2 = x[:, :RET_DK // 2], x[:, RET_DK // 2:]
    return jnp.concatenate([x1 * c - x2 * s, x2 * c + x1 * s], axis=-1)


def _rope_half_bwd(d, c, s):
    d1, d2 = d[:, :RET_DK // 2], d[:, RET_DK // 2:]
    return jnp.concatenate([d1 * c + d2 * s, d2 * c - d1 * s], axis=-1)


def _dot(a, b):
    return lax.dot_general(a, b, (((1,), (0,)), ((), ())), preferred_element_type=F32)


def _dot_nt(a, b):
    return lax.dot_general(a, b, (((1,), (1,)), ((), ())), preferred_element_type=F32)


def _dot_tn(a, b):
    return lax.dot_general(a, b, (((0,), (0,)), ((), ())), preferred_element_type=F32)


def _ret_specs(T, tb, rev):
    nj = T // tb
    jj = (lambda j: nj - 1 - j) if rev else (lambda j: j)
    g = RET_GROUP
    kq = RET_HEADS // g
    vq = 2 * RET_HEADS * RET_DK // (g * RET_DV)
    return dict(
        q=pl.BlockSpec((tb, g * RET_DK), lambda h, j: (jj(j), h)),
        k=pl.BlockSpec((tb, g * RET_DK), lambda h, j: (jj(j), kq + h)),
        v=pl.BlockSpec((tb, g * RET_DV), lambda h, j: (jj(j), vq + h)),
        tab=pl.BlockSpec((tb, RET_DK // 2), lambda h, j: (jj(j), 0)),
        intra=pl.BlockSpec((g, RET_BLOCK, RET_BLOCK), lambda h, j: (h, 0, 0)),
        dec=pl.BlockSpec((g, RET_BLOCK, 1), lambda h, j: (h, 0, 0)),
        cd=pl.BlockSpec((g, 1, 1), lambda h, j: (h, 0, 0)),
        o=pl.BlockSpec((tb, g * RET_DV), lambda h, j: (jj(j), h)),
        s=pl.BlockSpec((g, tb // RET_BLOCK, RET_DK, RET_DV), lambda h, j: (h, jj(j), 0, 0)),
    )


def _ret_fwd(proj, tabs, name):
    T = proj.shape[0]
    cos, sin, intra, qd, kd, cd = tabs
    tb = _pick(T, 512)
    cps = tb // RET_BLOCK
    sp = _ret_specs(T, tb, False)
    scale = RET_DK ** -0.5

    def body(q_ref, k_ref, v_ref, cos_ref, sin_ref, intra_ref, qd_ref, kd_ref, cd_ref, o_ref, s_ref, state):
        @pl.when(pl.program_id(1) == 0)
        def _():
            state[...] = jnp.zeros_like(state)

        for c in range(cps):
            rows = pl.ds(c * RET_BLOCK, RET_BLOCK)
            co, si = cos_ref[rows, :], sin_ref[rows, :]
            for h in range(RET_GROUP):
                hk, hv = slice(h * RET_DK, (h + 1) * RET_DK), slice(h * RET_DV, (h + 1) * RET_DV)
                q = _rope_half(q_ref[rows, hk].astype(F32), co, si)
                k = _rope_half(k_ref[rows, hk].astype(F32), co, si) * scale
                vb = v_ref[rows, hv].astype(BF16)
                st = state[h]
                sb = st.astype(BF16)
                s_ref[h, c] = sb
                sc = _dot_nt(q.astype(BF16), k.astype(BF16)) * intra_ref[h]
                inner = _dot(sc.astype(BF16), vb)
                cross = _dot((q * qd_ref[h]).astype(BF16), sb)
                o_ref[rows, hv] = inner + cross
                state[h] = st * cd_ref[h] + _dot_tn((k * kd_ref[h]).astype(BF16), vb)

    return pl.pallas_call(
        body, name=name, grid=(RET_HEADS // RET_GROUP, T // tb),
        in_specs=[sp["q"], sp["k"], sp["v"], sp["tab"], sp["tab"], sp["intra"], sp["dec"], sp["dec"], sp["cd"]],
        out_specs=[sp["o"], sp["s"]],
        out_shape=[jax.ShapeDtypeStruct((T, RET_HEADS * RET_DV), F32),
                   jax.ShapeDtypeStruct((RET_HEADS, T // RET_BLOCK, RET_DK, RET_DV), BF16)],
        scratch_shapes=[pltpu.VMEM((RET_GROUP, RET_DK, RET_DV), F32)],
        compiler_params=_cparams(("arbitrary", "arbitrary")),
    )(proj, proj, proj, cos, sin, intra, qd, kd, cd)


def _ret_bwd(proj, states, dout, dproj, tabs, name):
    assert RET_GROUP == 1
    T = proj.shape[0]
    cos, sin, intra, qd, kd, cd = tabs
    tb = _pick(T, 512)
    cps = tb // RET_BLOCK
    nj = T // tb
    sp = _ret_specs(T, tb, True)
    scale = RET_DK ** -0.5
    k0, v0 = RET_HEADS * RET_DK, 2 * RET_HEADS * RET_DK

    def body(q_ref, k_ref, v_ref, cos_ref, sin_ref, intra_ref, qd_ref, kd_ref, cd_ref, s_ref, do_ref, _dproj_in,
             out_ref, dq_s, dk_s, dv_s, sems, dstate):
        head, j = pl.program_id(0), pl.program_id(1)
        step = head * nj + j
        slot = step % 2
        dq_ref, dk_ref, dv_ref = dq_s.at[slot], dk_s.at[slot], dv_s.at[slot]

        @pl.when(j == 0)
        def _():
            dstate[...] = jnp.zeros_like(dstate)

        for c in reversed(range(cps)):
            rows = pl.ds(c * RET_BLOCK, RET_BLOCK)
            co, si = cos_ref[rows, :], sin_ref[rows, :]
            for h in range(RET_GROUP):
                hk, hv = slice(h * RET_DK, (h + 1) * RET_DK), slice(h * RET_DV, (h + 1) * RET_DV)
                q = _rope_half(q_ref[rows, hk].astype(F32), co, si)
                k = _rope_half(k_ref[rows, hk].astype(F32), co, si) * scale
                qb, kb = q.astype(BF16), k.astype(BF16)
                vb = v_ref[rows, hv].astype(BF16)
                dob = do_ref[rows, hv].astype(BF16)
                sb = s_ref[h, c]
                ia = intra_ref[h]
                pb = (_dot_nt(qb, kb) * ia).astype(BF16)
                dsn = dstate[h]
                dsb = dsn.astype(BF16)
                kdk = (k * kd_ref[h]).astype(BF16)
                qdq = (q * qd_ref[h]).astype(BF16)
                dv = _dot_tn(pb, dob) + _dot(kdk, dsb)
                dpb = (_dot_nt(dob, vb) * ia).astype(BF16)
                dq = _dot(dpb, kb) + _dot_nt(dob, sb) * qd_ref[h]
                dk = _dot_tn(dpb, qb) + _dot_nt(vb, dsb) * kd_ref[h]
                dstate[h] = dsn * cd_ref[h] + _dot_tn(qdq, dob)
                dq_ref[rows, hk] = _rope_half_bwd(dq, co, si).astype(BF16)
                dk_ref[rows, hk] = _rope_half_bwd(dk * scale, co, si).astype(BF16)
                dv_ref[rows, hv] = dv.astype(BF16)

        def copies(sl):
            r = pl.ds(pl.multiple_of((nj - 1 - j) * tb, tb), tb)
            cols = lambda first, w: pl.ds(pl.multiple_of(first + head * w, 128), w)
            return [pltpu.make_async_copy(dq_s.at[sl], out_ref.at[r, cols(0, RET_DK)], sems.at[sl, 0]),
                    pltpu.make_async_copy(dk_s.at[sl], out_ref.at[r, cols(k0, RET_DK)], sems.at[sl, 1]),
                    pltpu.make_async_copy(dv_s.at[sl], out_ref.at[r, cols(v0, RET_DV)], sems.at[sl, 2])]

        @pl.when(step > 0)
        def _():
            for cp in copies(1 - slot):
                cp.wait()

        for cp in copies(slot):
            cp.start()

        @pl.when(step == RET_HEADS * nj - 1)
        def _():
            for cp in copies(slot):
                cp.wait()

    return pl.pallas_call(
        body, name=name, grid=(RET_HEADS, nj),
        in_specs=[sp["q"], sp["k"], sp["v"], sp["tab"], sp["tab"], sp["intra"], sp["dec"], sp["dec"], sp["cd"],
                  sp["s"], sp["o"], pl.BlockSpec(memory_space=pl.ANY)],
        out_specs=pl.BlockSpec(memory_space=pl.ANY), out_shape=jax.ShapeDtypeStruct(dproj.shape, dproj.dtype),
        input_output_aliases={11: 0},
        scratch_shapes=[pltpu.VMEM((2, tb, RET_DK), BF16), pltpu.VMEM((2, tb, RET_DK), BF16),
                        pltpu.VMEM((2, tb, RET_DV), BF16), pltpu.SemaphoreType.DMA((2, 3)),
                        pltpu.VMEM((RET_GROUP, RET_DK, RET_DV), F32)],
        compiler_params=_cparams(("arbitrary", "arbitrary")),
    )(proj, proj, proj, cos, sin, intra, qd, kd, cd, states, dout, dproj)


def _ret_gate(out, proj, gn, name):
    def fn(o, g, *gains):
        g = g.astype(F32)
        parts = [_rms(o[:, h * RET_DV:(h + 1) * RET_DV], gains[h], mxu=True) for h in range(RET_HEADS)]
        return (g * _sigmoid(g) * jnp.concatenate(parts, axis=-1),)
    w = RET_HEADS * RET_DV
    return _rows(fn, [out, (proj, w, 2)], [gn[h:h + 1] for h in range(RET_HEADS)], [(w, BF16)], name=name)[0]


def _ret_gate_bwd(out, proj, gn, dy, name):
    def fn(o, g, d, *gains):
        g = g.astype(F32)
        sg = _sigmoid(g)
        silu = g * sg
        dsilu = sg * (1.0 + g * (1.0 - sg))
        dos, dgs = [], []
        row = lax.broadcasted_iota(jnp.int32, (RET_HEADS, RET_DV), 0)
        dgn = jnp.zeros((RET_HEADS, RET_DV), F32)
        for h in range(RET_HEADS):
            sl = slice(h * RET_DV, (h + 1) * RET_DV)
            oh = o[:, sl]
            dgs.append(d[:, sl] * _rms(oh, gains[h], mxu=True) * dsilu[:, sl])
            dx, dg = _rms_bwd(oh, d[:, sl] * silu[:, sl], gains[h], mxu=True)
            dos.append(dx)
            dgn = dgn + jnp.where(row == h, _colsum(dg), 0.0)
        return jnp.concatenate(dos, axis=-1), jnp.concatenate(dgs, axis=-1), dgn
    w = RET_HEADS * RET_DV
    return _rows(fn, [out, (proj, w, 2), dy], [gn[h:h + 1] for h in range(RET_HEADS)],
                 [(w, BF16), (w, BF16, proj.shape[1], 2)], [((RET_HEADS, RET_DV), F32)], name=name, tile=128)


def _mla_tables(T):
    ang = _rope_angles(T, MLA_ROPE)
    c, s = np.cos(ang), np.sin(ang)
    z32, z64 = np.zeros((T, 32), np.float32), np.zeros((T, 64), np.float32)
    cos_t = np.concatenate([c, c, z64], axis=1)
    sin_a = np.concatenate([-s, z32, z64], axis=1)
    sin_b = np.concatenate([z32, s, z64], axis=1)
    return tuple(jnp.asarray(t, F32) for t in (cos_t, sin_a, sin_b))


def _rope_blk(x, ct, sa, sb):
    return x * ct + pltpu.roll(x, 96, 1) * sa + pltpu.roll(x, 32, 1) * sb


def _rope_blk_bwd(d, ct, sa, sb):
    return d * ct + pltpu.roll(d * sa, 32, 1) + pltpu.roll(d * sb, 96, 1)


def _head_norm(x, gain):
    r = lax.rsqrt(_rowsum(x * x, True) / MLA_QKD + EPS)
    return (x * r) * gain


def _mla_prep(q, kv, proj, gq, gk, tabs, name):
    def fn(qv, kvv, kr, ct, sa, sb, gqv, gkv):
        qv, kvv = qv.astype(F32), kvv.astype(F32)
        qs, ks, vs = [], [], []
        for h in range(MLA_HEADS):
            b = h * MLA_HP
            y = _head_norm(qv[:, b:b + MLA_HP], gqv)
            qs += [y[:, :128], _rope_blk(y[:, 128:], ct, sa, sb)]
            y = _head_norm(jnp.concatenate([kvv[:, b:b + 128], kr], axis=-1), gkv)
            ks += [y[:, :128], _rope_blk(y[:, 128:], ct, sa, sb)]
            vs.append(kvv[:, b + 128:b + 256])
        return jnp.concatenate(qs, axis=-1), jnp.concatenate(ks, axis=-1), jnp.concatenate(vs, axis=-1)
    w = MLA_HEADS * MLA_HP
    return _rows(fn, [q, kv, (proj, 128, 5), *tabs], [gq, gk],
                 [(w, BF16), (w, BF16), (MLA_HEADS * MLA_VD, BF16)], name=name, tile=128)


def _mla_prep_bwd(q, kv, proj, gq, gk, tabs, dqf, dkf, dvf, name):
    def fn(qv, kvv, kr, ct, sa, sb, dqv, dkv, dvv, gqv, gkv):
        qv, kvv, dqv, dkv = (t.astype(F32) for t in (qv, kvv, dqv, dkv))
        dqs, dkvs = [], []
        dkr = jnp.zeros_like(kr)
        dgq = jnp.zeros((1, MLA_HP), F32)
        dgk = jnp.zeros((1, MLA_HP), F32)
        for h in range(MLA_HEADS):
            b = h * MLA_HP
            dy = jnp.concatenate([dqv[:, b:b + 128], _rope_blk_bwd(dqv[:, b + 128:b + 256], ct, sa, sb)], axis=-1)
            dx, dg = _rms_bwd(qv[:, b:b + MLA_HP], dy, gqv, MLA_QKD, mxu=True)
            dqs.append(dx)
            dgq = dgq + _colsum(dg)
            dy = jnp.concatenate([dkv[:, b:b + 128], _rope_blk_bwd(dkv[:, b + 128:b + 256], ct, sa, sb)], axis=-1)
            dx, dg = _rms_bwd(jnp.concatenate([kvv[:, b:b + 128], kr], axis=-1), dy, gkv, MLA_QKD, mxu=True)
            dkvs += [dx[:, :128], dvv[:, h * MLA_VD:(h + 1) * MLA_VD]]
            dkr = dkr + dx[:, 128:]
            dgk = dgk + _colsum(dg)
        return jnp.concatenate(dqs, axis=-1), jnp.concatenate(dkvs, axis=-1), dkr, dgq, dgk
    w = MLA_HEADS * MLA_HP
    return _rows(fn, [q, kv, (proj, 128, 5), *tabs, dqf, dkf, dvf], [gq, gk],
                 [(w, BF16), (w, BF16), (128, F32)], [((1, MLA_HP), F32), ((1, MLA_HP), F32)], name=name, tile=128)


def _chunk_mask(qi, ki, tq, tk):
    shift = CHUNK.bit_length() - 1
    rq = lax.shift_right_arithmetic(qi * tq + lax.broadcasted_iota(jnp.int32, (tq, tk), 0), shift)
    ck = lax.shift_right_arithmetic(ki * tk + lax.broadcasted_iota(jnp.int32, (tq, tk), 1), shift)
    return ck <= rq


def _flash_fwd(qf, kf, vf, name):
    T = qf.shape[0]
    t = _pick(T, FLASH_T)
    n = T // t
    scale = MLA_QKD ** -0.5

    g = FLASH_HEADS

    def body(q_ref, k_ref, v_ref, o_ref, lse_ref, m_s, l_s, acc):
        qi = pl.program_id(1)
        m_s[...] = jnp.full_like(m_s, NEG)
        l_s[...] = jnp.zeros_like(l_s)
        acc[...] = jnp.zeros_like(acc)

        def step(kb, masked):
            rows = pl.ds(pl.multiple_of(kb * t, t), t)
            for h in range(g):
                hq, hv = slice(h * MLA_HP, (h + 1) * MLA_HP), slice(h * MLA_VD, (h + 1) * MLA_VD)
                s = _dot_nt(q_ref[:, hq], k_ref[rows, hq])
                if masked:
                    s = jnp.where(_chunk_mask(0, 0, t, t), s, NEG)
                m_prev = m_s[:, hv]
                m_new = jnp.maximum(m_prev, jnp.max(s, axis=-1, keepdims=True))
                alpha = jnp.exp2(m_prev - m_new)
                p = jnp.exp2(s - _widen(m_new, t))
                l_s[:, hv] = alpha * l_s[:, hv] + sum(p[:, i * 128:(i + 1) * 128] for i in range(t // 128))
                acc[:, hv] = acc[:, hv] * alpha + _dot(p.astype(BF16), v_ref[rows, hv])
                m_s[:, hv] = m_new

        @pl.loop(0, qi)
        def _(kb):
            step(kb, False)

        step(qi, True)
        for h in range(g):
            hv = slice(h * MLA_VD, (h + 1) * MLA_VD)
            l = jnp.sum(l_s[:, hv], axis=-1, keepdims=True)
            o_ref[:, hv] = acc[:, hv] / l
            lse_ref[:, hv] = m_s[:, hv] + jnp.log2(l)

    qmap = lambda h, i: (i, h)
    kmap = lambda h, i: (0, h)
    vec = pltpu.VMEM((t, g * MLA_VD), F32)
    return pl.pallas_call(
        body, name=name, grid=(MLA_HEADS // g, n),
        in_specs=[pl.BlockSpec((t, g * MLA_HP), qmap), pl.BlockSpec((T, g * MLA_HP), kmap),
                  pl.BlockSpec((T, g * MLA_VD), kmap)],
        out_specs=[pl.BlockSpec((t, g * MLA_VD), qmap), pl.BlockSpec((t, g * MLA_VD), qmap)],
        out_shape=[jax.ShapeDtypeStruct((T, MLA_HEADS * MLA_VD), F32),
                   jax.ShapeDtypeStruct((T, MLA_HEADS * MLA_VD), F32)],
        scratch_shapes=[vec, vec, vec],
        compiler_params=_cparams(("parallel", "arbitrary")),
    )(qf, kf, vf)


def _flash_delta(o, do, name):
    def fn(ov, dv):
        parts = []
        for h in range(MLA_HEADS):
            sl = slice(h * MLA_VD, (h + 1) * MLA_VD)
            d = jnp.sum(dv[:, sl] * ov[:, sl], axis=-1, keepdims=True)
            parts.append(jnp.broadcast_to(d, (d.shape[0], MLA_VD)))
        return jnp.concatenate(parts, axis=-1), dv
    w = MLA_HEADS * MLA_VD
    return _rows(fn, [o, do], [], [(w, F32), (w, BF16)], name=name)


def _flash_bwd(qf, kf, vf, do16, lse, delta, name):
    T = qf.shape[0]
    t = _pick(T, FLASH_T)
    n = T // t
    scale = MLA_QKD ** -0.5

    def body(q_ref, k_ref, v_ref, do_ref, lse_ref, dl_ref, dq_out, dk_out, dv_out, dq_ref, dk_ref, dv_ref):
        kb = pl.program_id(1)

        @pl.when(kb == 0)
        def _():
            dq_ref[...] = jnp.zeros_like(dq_ref)

        dk_ref[...] = jnp.zeros_like(dk_ref)
        dv_ref[...] = jnp.zeros_like(dv_ref)
        k, v = k_ref[...], v_ref[...]

        def step(qb, masked):
            rows = pl.ds(pl.multiple_of(qb * t, t), t)
            q, dob = q_ref[rows, :], do_ref[rows, :]
            s = _dot_nt(q, k)
            if masked:
                s = jnp.where(_chunk_mask(0, 0, t, t), s, NEG)
            p = jnp.exp2(s - _widen(lse_ref[rows, :], t))
            ds = (p * (_dot_nt(dob, v) - _widen(dl_ref[rows, :], t))).astype(BF16)
            dv_ref[...] += _dot_tn(p.astype(BF16), dob)
            dk_ref[...] += _dot_tn(ds, q)
            dq_ref[rows, :] += _dot(ds, k)

        step(kb, True)

        @pl.loop(kb + 1, n)
        def _(qb):
            step(qb, False)

        dk_out[...] = (dk_ref[...] * (1.0 / LOG2E)).astype(BF16)
        dv_out[...] = dv_ref[...].astype(BF16)

        @pl.when(kb == n - 1)
        def _():
            dq_out[...] = (dq_ref[...] * scale).astype(BF16)

    qmap = lambda h, j: (0, h)
    kmap = lambda h, j: (j, h)
    return pl.pallas_call(
        body, name=name, grid=(MLA_HEADS, n),
        in_specs=[pl.BlockSpec((T, MLA_HP), qmap), pl.BlockSpec((t, MLA_HP), kmap), pl.BlockSpec((t, MLA_VD), kmap),
                  pl.BlockSpec((T, MLA_VD), qmap), pl.BlockSpec((T, MLA_VD), qmap), pl.BlockSpec((T, MLA_VD), qmap)],
        out_specs=[pl.BlockSpec((T, MLA_HP), qmap), pl.BlockSpec((t, MLA_HP), kmap), pl.BlockSpec((t, MLA_VD), kmap)],
        out_shape=[jax.ShapeDtypeStruct((T, MLA_HEADS * MLA_HP), BF16),
                   jax.ShapeDtypeStruct((T, MLA_HEADS * MLA_HP), BF16),
                   jax.ShapeDtypeStruct((T, MLA_HEADS * MLA_VD), BF16)],
        scratch_shapes=[pltpu.VMEM((T, MLA_HP), F32), pltpu.VMEM((t, MLA_HP), F32), pltpu.VMEM((t, MLA_VD), F32)],
        compiler_params=_cparams(("arbitrary", "arbitrary")),
    )(qf, kf, vf, do16, lse, delta)


MESH = pl.DeviceIdType.MESH
ANY = pl.BlockSpec(memory_space=pl.ANY)
_CHIP_FLIPS = ((1, 0), (0, 1), (1, 1))


def _place():
    return lax.axis_index("x"), lax.axis_index("y"), lax.axis_index("c")


def _other_chip(x, y, k):
    fx, fy = _CHIP_FLIPS[k]
    return ((1 - x) if fx else x), ((1 - y) if fy else y)


def _remote(src, dst, send_sems, recv_sems, k, to):
    return pltpu.make_async_remote_copy(src_ref=src, dst_ref=dst, send_sem=send_sems.at[k], recv_sem=recv_sems.at[k],
                                        device_id=to, device_id_type=MESH)


def _index(*vals):
    return jnp.stack(vals).astype(jnp.int32)


def _half(c, rows):
    return pl.ds(pl.multiple_of(c * rows, 16), rows)


def _gather_weights(parts, name, landed=None):
    n_w = len(parts)
    n_in = n_w if landed is None else 2 * n_w

    def body(*refs):
        ins, outs = refs[:n_w], refs[n_in:n_in + n_w]
        send_sems, recv_sems, local_sems = refs[n_in + n_w:]
        x, y, c = _place()
        j = 2 * x + y
        sibling = (x, y, 1 - c)
        chips = [_other_chip(x, y, k) for k in range(3)]
        pending = []
        for w in range(n_w):
            own = pltpu.make_async_copy(ins[w], outs[w].at[j], local_sems.at[w])
            own.start()
            pending.append(own)
        sent = []
        for w in range(n_w):
            if landed is not None:
                break
            r = _half(c, parts[w].shape[0] // 2)
            for k, (px, py) in enumerate(chips):
                cp = _remote(ins[w].at[r], outs[w].at[j, r], send_sems, recv_sems, 6 * w + k, (px, py, c))
                cp.start()
                sent.append(cp)
        for w in range(n_w):
            r = _half(c, parts[w].shape[0] // 2)
            for k, (px, py) in enumerate(chips):
                blk = outs[w].at[2 * px + py, r]
                if landed is None:
                    _remote(blk, blk, send_sems, recv_sems, 6 * w + k, (px, py, c)).wait_recv()
                cp = _remote(blk, blk, send_sems, recv_sems, 6 * w + 3 + k, sibling)
                cp.start()
                sent.append(cp)
        for w in range(n_w):
            r = _half(1 - c, parts[w].shape[0] // 2)
            for k, (px, py) in enumerate(chips):
                blk = outs[w].at[2 * px + py, r]
                _remote(blk, blk, send_sems, recv_sems, 6 * w + 3 + k, sibling).wait_recv()
        for cp in sent:
            cp.wait_send()
        for cp in pending:
            cp.wait()

    return pl.pallas_call(
        body, name=name, in_specs=[pl.BlockSpec(memory_space=pltpu.VMEM)] * n_w + [ANY] * (n_in - n_w),
        out_specs=[ANY] * n_w,
        out_shape=[jax.ShapeDtypeStruct((N_CHIPS, *p.shape), p.dtype) for p in parts],
        input_output_aliases={} if landed is None else {n_w + w: w for w in range(n_w)},
        scratch_shapes=[pltpu.SemaphoreType.DMA((6 * n_w,)), pltpu.SemaphoreType.DMA((6 * n_w,)),
                        pltpu.SemaphoreType.DMA((n_w,))],
        compiler_params=pltpu.CompilerParams(vmem_limit_bytes=VMEM_LIMIT),
    )(*parts, *(landed or []))


def _swap_halves(gs, name):
    n_w = len(gs)

    def body(*refs):
        g_refs, recv_refs = refs[:n_w], refs[n_w:2 * n_w]
        send_sems, recv_sems = refs[2 * n_w:]
        x, y, c = _place()
        sent = []
        for w in range(n_w):
            for jj in range(N_CHIPS):
                cp = _remote(g_refs[w].at[jj, 1 - c], recv_refs[w].at[jj], send_sems, recv_sems, N_CHIPS * w + jj,
                             (x, y, 1 - c))
                cp.start()
                sent.append(cp)
        for cp in sent:
            cp.wait()

    return pl.pallas_call(
        body, name=name, in_specs=[ANY] * n_w, out_specs=[ANY] * n_w,
        out_shape=[jax.ShapeDtypeStruct((N_CHIPS, *g.shape[2:]), g.dtype) for g in gs],
        scratch_shapes=[pltpu.SemaphoreType.DMA((N_CHIPS * n_w,)), pltpu.SemaphoreType.DMA((N_CHIPS * n_w,))],
    )(*gs)


def _pair_sum(g, recv, core, name):
    _, H, C = recv.shape
    tile = _pick(H, 256)

    def body(c_ref, own_ref, recv_ref, out_ref):
        out_ref[...] = (own_ref[...].astype(F32) + recv_ref[...].astype(F32)).astype(BF16)

    blk = pl.BlockSpec((None, tile, C), lambda jj, i, c: (jj, i, 0))
    return pl.pallas_call(
        body, name=name,
        grid_spec=pltpu.PrefetchScalarGridSpec(
            num_scalar_prefetch=1, grid=(N_CHIPS, H // tile),
            in_specs=[pl.BlockSpec((None, None, tile, C), lambda jj, i, c: (jj, c[0], i, 0)), blk],
            out_specs=blk),
        out_shape=jax.ShapeDtypeStruct((N_CHIPS, H, C), BF16),
        compiler_params=_cparams(("arbitrary", "arbitrary")),
    )(_index(core), g, recv)


def _chip_sum(g, recv, got, chip, core, name):
    _, H, C = recv.shape
    tile = _pick(H, 256)

    def body(s_ref, own_ref, recv_ref, g0_ref, g1_ref, g2_ref, out_ref):
        pair = own_ref[...].astype(F32) + recv_ref[...].astype(F32)
        out_ref[...] = ((pair + g0_ref[...].astype(F32)) + g1_ref[...].astype(F32)) + g2_ref[...].astype(F32)

    def got_spec(k):
        return pl.BlockSpec((None, tile, C), lambda i, s, k=k: (k, i, 0))

    return pl.pallas_call(
        body, name=name,
        grid_spec=pltpu.PrefetchScalarGridSpec(
            num_scalar_prefetch=1, grid=(H // tile,),
            in_specs=[pl.BlockSpec((None, None, tile, C), lambda i, s: (s[0], s[1], i, 0)),
                      pl.BlockSpec((None, tile, C), lambda i, s: (s[0], i, 0)), got_spec(0), got_spec(1), got_spec(2)],
            out_specs=pl.BlockSpec((None, tile, C), lambda i, s: (s[1], i, 0))),
        out_shape=jax.ShapeDtypeStruct((2, H, C), F32),
        compiler_params=_cparams(("arbitrary",)),
    )(_index(chip, core), g, recv, got, got, got)


def _scatter_chips(sums, name):
    n_w = len(sums)

    def body(*refs):
        a_refs, got_refs = refs[:n_w], refs[n_w:2 * n_w]
        send_sems, recv_sems = refs[2 * n_w:]
        x, y, c = _place()
        j = 2 * x + y
        sent = []
        for w in range(n_w):
            for k in range(3):
                px, py = _other_chip(x, y, k)
                pj = 2 * px + py
                cp = _remote(a_refs[w].at[pj], got_refs[w].at[(j - pj + 4) % 4 - 1], send_sems, recv_sems, 3 * w + k,
                             (px, py, c))
                cp.start()
                sent.append(cp)
        for w in range(n_w):
            for k in range(3):
                px, py = _other_chip(x, y, k)
                slot = got_refs[w].at[(2 * px + py - j + 4) % 4 - 1]
                _remote(slot, slot, send_sems, recv_sems, 3 * w + k, (px, py, c)).wait_recv()
        for cp in sent:
            cp.wait_send()

    return pl.pallas_call(
        body, name=name, in_specs=[ANY] * n_w, out_specs=[ANY] * n_w,
        out_shape=[jax.ShapeDtypeStruct((3, *a.shape[1:]), a.dtype) for a in sums],
        scratch_shapes=[pltpu.SemaphoreType.DMA((3 * n_w,)), pltpu.SemaphoreType.DMA((3 * n_w,))],
    )(*sums)


def _share_halves(reds):
    n_w = len(reds)

    def body(*refs):
        out_refs = refs[n_w:2 * n_w]
        send_sems, recv_sems = refs[2 * n_w:]
        x, y, c = _place()
        sent = []
        for w in range(n_w):
            blk = out_refs[w].at[c]
            cp = _remote(blk, blk, send_sems, recv_sems, w, (x, y, 1 - c))
            cp.start()
            sent.append(cp)
        for cp in sent:
            cp.wait()

    return pl.pallas_call(
        body, name="grad_share_halves", in_specs=[ANY] * n_w, out_specs=[ANY] * n_w,
        out_shape=[jax.ShapeDtypeStruct(r.shape, r.dtype) for r in reds],
        input_output_aliases={w: w for w in range(n_w)},
        scratch_shapes=[pltpu.SemaphoreType.DMA((n_w,)), pltpu.SemaphoreType.DMA((n_w,))],
    )(*reds)


def _allsum_small(v, name):
    R, W = v.shape
    n_dev = 8
    vm = pl.BlockSpec(memory_space=pltpu.VMEM)

    def body(v_ref, out_ref, buf, send_sems, recv_sems):
        x, y, c = _place()
        me = 4 * x + 2 * y + c
        buf[me] = v_ref[...]
        sent = []
        for k in range(1, n_dev):
            peer = ((1 - x) if k & 4 else x, (1 - y) if k & 2 else y, (1 - c) if k & 1 else c)
            cp = _remote(v_ref, buf.at[me], send_sems, recv_sems, k - 1, peer)
            cp.start()
            sent.append(cp)
        for cp in sent:
            cp.wait_recv()
        for cp in sent:
            cp.wait_send()
        acc = buf[0]
        for q in range(1, n_dev):
            acc = acc + buf[q]
        out_ref[...] = acc

    return pl.pallas_call(
        body, name=name, in_specs=[vm], out_specs=vm, out_shape=jax.ShapeDtypeStruct((R, W), v.dtype),
        scratch_shapes=[pltpu.VMEM((n_dev, R, W), v.dtype), pltpu.SemaphoreType.DMA((n_dev - 1,)),
                        pltpu.SemaphoreType.DMA((n_dev - 1,))],
    )(v)


HBM = pl.BlockSpec(memory_space=pltpu.HBM)
SEM = pl.BlockSpec(memory_space=pltpu.SEMAPHORE)
_DATAFLOW = pltpu.SideEffectType.DATAFLOW_SIDE_EFFECTING


def _split_start(name, srcs, land_shapes, n_copies, copies, after=()):
    ns, nl = len(srcs), len(land_shapes)
    lands = [lax.empty(s.shape, s.dtype) for s in land_shapes]

    def body(*refs):
        outs = refs[ns + nl + len(after):]
        for cp in copies(refs[:ns], refs[ns:ns + nl], outs[0], outs[1]):
            cp.start()
        outs[-1][...] = jnp.zeros_like(outs[-1])

    sems = pltpu.SemaphoreType.DMA((n_copies,))
    res = pl.pallas_call(
        body, name=name, in_specs=[HBM] * (ns + nl) + [ANY] * len(after),
        out_specs=(SEM, SEM, *[HBM] * (ns + nl), pl.BlockSpec(memory_space=pltpu.VMEM)),
        out_shape=(sems, sems, *[pltpu.HBM(a.shape, a.dtype) for a in srcs],
                   *[pltpu.HBM(s.shape, s.dtype) for s in land_shapes], jax.ShapeDtypeStruct((8, 128), F32)),
        input_output_aliases={i: 2 + i for i in range(ns + nl)},
        compiler_params=pltpu.CompilerParams(has_side_effects=_DATAFLOW),
    )(*[pltpu.with_memory_space_constraint(a, pltpu.HBM) for a in [*srcs, *lands]], *after)
    return res[0], res[1], list(res[2:2 + ns]), list(res[2 + ns:2 + ns + nl]), res[-1]


def _split_wait(name, send_sems, recv_sems, srcs, lands, copies, after=()):
    ns, nl = len(srcs), len(lands)

    def body(*refs):
        for cp in copies(refs[:ns], refs[ns:ns + nl], refs[ns + nl], refs[ns + nl + 1]):
            cp.wait_send()
            cp.wait_recv()

    res = pl.pallas_call(
        body, name=name, in_specs=[HBM] * (ns + nl) + [SEM, SEM] + [ANY] * len(after), out_specs=[HBM] * (ns + nl),
        out_shape=[pltpu.HBM(a.shape, a.dtype) for a in [*srcs, *lands]],
        input_output_aliases={i: i for i in range(ns + nl)},
        compiler_params=pltpu.CompilerParams(has_side_effects=_DATAFLOW),
    )(*srcs, *lands, send_sems, recv_sems, *after)
    return list(res[ns:])


def _gather_copies(rows):
    def copies(src_refs, land_refs, send_sems, recv_sems):
        x, y, c = _place()
        j = 2 * x + y
        out = []
        for w in range(len(src_refs)):
            r = _half(c, rows[w] // 2)
            for k in range(3):
                px, py = _other_chip(x, y, k)
                out.append(_remote(src_refs[w].at[r], land_refs[w].at[j, r], send_sems, recv_sems, 3 * w + k, (px, py, c)))
        return out
    return copies


def _scatter_copies(src_refs, land_refs, send_sems, recv_sems):
    x, y, c = _place()
    j = 2 * x + y
    out = []
    for w in range(len(src_refs)):
        for k in range(3):
            px, py = _other_chip(x, y, k)
            pj = 2 * px + py
            out.append(_remote(src_refs[w].at[pj], land_refs[w].at[(j - pj + 4) % 4 - 1], send_sems, recv_sems, 3 * w + k,
                               (px, py, c)))
    return out


def _reduce_begin(grads, core, tag):
    names = list(grads)
    gs = [grads[k].reshape(N_CHIPS, 2, -1, grads[k].shape[-1]) for k in names]
    recvs = _swap_halves(gs, f"grad_swap_halves_{tag}")
    sums = [_pair_sum(g, r, core, f"pair_sum_{k}") for k, g, r in zip(names, gs, recvs)]
    return names, gs, recvs, sums


def _reduce_end(begun, gots, chip, core):
    names, gs, recvs, _ = begun
    return {k: _chip_sum(g, r, t, chip, core, f"chip_sum_{k}") for k, g, r, t in zip(names, gs, recvs, gots)}


def _got_shapes(sums):
    return [jax.ShapeDtypeStruct((3, *a.shape[1:]), a.dtype) for a in sums]


def _adamw(w, g, m, v, name, layers=1, layer=0, into=None):
    shape = w.shape
    cols = shape[-1]
    w3, m3, v3 = (t.reshape(layers, -1, cols) for t in (w, m, v))
    rows = w3.shape[1]
    tile = _pick(rows, 256) if rows % 8 == 0 else rows
    n_in = 4 + (0 if into is None else 4)

    def body(*refs):
        wv, gv, mv, vv = (r[...] for r in refs[:4])
        g_ref, d_ref, m_ref, v_ref = refs[n_in:]
        m2 = ADAM_B1 * mv + (1.0 - ADAM_B1) * gv
        v2 = ADAM_B2 * vv + (1.0 - ADAM_B2) * jnp.square(gv)
        m_hat = m2 / (1.0 - ADAM_B1 ** ADAM_STEP)
        v_hat = v2 / (1.0 - ADAM_B2 ** ADAM_STEP)
        g_ref[...] = gv
        d_ref[...] = -ADAM_LR * (m_hat / (jnp.sqrt(v_hat) + ADAM_EPS) + ADAM_WD * wv)
        m_ref[...] = m2
        v_ref[...] = v2

    lay = pl.BlockSpec((None, tile, cols), lambda i: (layer, i, 0))
    out = jax.ShapeDtypeStruct((layers, rows, cols), F32)
    res = pl.pallas_call(
        body, name=name, grid=(rows // tile,),
        in_specs=[lay, pl.BlockSpec((tile, cols), lambda i: (i, 0)), lay, lay] + [ANY] * (n_in - 4),
        out_specs=[lay] * 4, out_shape=[out] * 4,
        input_output_aliases={} if into is None else {4 + k: k for k in range(4)},
        compiler_params=_cparams(("arbitrary",)),
    )(w3, g.reshape(rows, cols), m3, v3, *([] if into is None else [t.reshape(layers, rows, cols) for t in into]))
    return tuple(t.reshape(shape) for t in res)


ROW_F32, ROW_BF16 = (D_MODEL, F32), (D_MODEL, BF16)


def _res_norm(acc, h, gain):
    hh = h + acc
    return hh, _rms(hh, gain)


def _dx_norm_bwd(d, w, h, dres, gain, name, **kw):
    def epilogue(acc, hv, dr, g):
        dx, dg = _rms_bwd(hv, acc, g)
        return dr + dx, dr + dx, _colsum(dg)
    return _mm_rows(d, w, tb=True, extras=[h, dres], fulls=[gain], outs=[ROW_F32, ROW_BF16], accs=[((1, D_MODEL), F32)],
                    epilogue=epilogue, name=name, **kw)


def _tail_fwd(h1, hn2, p16, W, i, tag, next_gain=None, target=None):
    a = _mm(hn2, W["mlp_w1"][i], bblk=True, outs=[BF16], name=f"{tag}_mlp_w1",
            epilogue=lambda acc: (jnp.square(jnp.maximum(acc, 0.0)),))
    h2, hn3 = _mm_rows(a, W["mlp_w2"][i], extras=[h1], fulls=[W["ple_norm"][i:i + 1]], outs=[ROW_F32, ROW_BF16],
                       epilogue=_res_norm, name=f"{tag}_mlp_w2")
    gl = _mm(hn3, W["ple_gate_w"][i], name=f"{tag}_ple_gate")
    if target is None:
        def gated(acc, g, h, gain):
            hh = h + _sigmoid(g) * acc
            return hh, acc, _rms(hh, gain)
        h3, pp, hn = _mm_rows(p16[i], W["ple_proj_w"][i], bblk=True, extras=[gl, h2], fulls=[next_gain],
                              outs=[ROW_F32, ROW_BF16, ROW_BF16], epilogue=gated, name=f"{tag}_ple_proj")
        return h3, hn, (h1, hn2, a, h2, hn3, gl, pp)

    def gated_loss(acc, g, h, t):
        e = h + _sigmoid(g) * acc - t
        return acc, e * (1.0 / D_MODEL), jnp.full((1, 128), 0.5 / D_MODEL, F32) * jnp.sum(e * e)
    pp, dy, loss = _mm_rows(p16[i], W["ple_proj_w"][i], bblk=True, extras=[gl, h2, target], outs=[ROW_BF16, ROW_F32],
                            accs=[((1, 128), F32)], epilogue=gated_loss, name=f"{tag}_ple_proj")
    return dy, loss, (h1, hn2, a, h2, hn3, gl, pp)


def _tail_bwd(dh3, saved, p16, W, i, tag, after=()):
    h1, hn2, a, h2, hn3, gl, pp = saved

    def gate_bwd(d, g, ppv):
        gate = _sigmoid(g)
        return d * gate, d * ppv * gate * (1.0 - gate)

    def dw(kind, name):
        return (kind, 1, 0, None)

    dpp, dgl = _rows(gate_bwd, [dh3, gl, pp], [], [(D_MODEL, BF16), (D_MODEL, BF16)], name=f"{tag}_ple_gate_bwd",
                     after=after)
    d_proj = _mm(p16[i], dpp, ta=True, outs=[BF16], dw=dw("cols", "ple_proj_w"), name=f"{tag}_d_ple_proj")
    d_gate = _mm(hn3, dgl, ta=True, outs=[BF16], dw=dw("rows", "ple_gate_w"), name=f"{tag}_d_ple_gate")
    dh2, dh2_16, d_ple_norm = _dx_norm_bwd(dgl, W["ple_gate_w"][i], h2, dh3, W["ple_norm"][i:i + 1],
                                           f"{tag}_ple_gate_dx")
    d_w2 = _mm(a, dh2_16, ta=True, outs=[BF16], dw=dw("rows", "mlp_w2"), name=f"{tag}_d_mlp_w2")
    dz = _mm(dh2_16, W["mlp_w2"][i], tb=True, extras=[a], outs=[BF16], name=f"{tag}_mlp_w2_dx",
             epilogue=lambda acc, av: (acc * (2.0 * jnp.sqrt(av.astype(F32))),))
    d_w1 = _mm(hn2, dz, ta=True, outs=[BF16], dw=dw("cols", "mlp_w1"), name=f"{tag}_d_mlp_w1")
    dh1, dh1_16, d_mlp_norm = _dx_norm_bwd(dz, W["mlp_w1"][i], h1, dh2, W["mlp_norm"][i:i + 1], f"{tag}_mlp_w1_dx",
                                           bblk=True)
    big = {f"mlp_w1_{i}": d_w1, f"mlp_w2_{i}": d_w2, f"ple_gate_w_{i}": d_gate, f"ple_proj_w_{i}": d_proj}
    return dh1, dh1_16, big, dict(mlp_norm=d_mlp_norm, ple_norm=d_ple_norm)


def _ret_layer_fwd(h0, W, tabs, after=()):
    hn = _rows(lambda x, g: (_rms(x, g),), [h0], [W["mix_norm"][0:1]], [(D_MODEL, BF16)], name="ret_mix_norm",
               after=after)[0]
    proj = _mm(hn, W["ret_w_in"], bblk=True, outs=[BF16], name="ret_w_in")
    out, states = _ret_fwd(proj, tabs, "ret_scan")
    y = _ret_gate(out, proj, W["ret_gn"], "ret_gate")
    h1, hn2 = _mm_rows(y, W["ret_w_out"], extras=[h0], fulls=[W["mlp_norm"][0:1]], outs=[ROW_F32, ROW_BF16],
                       epilogue=_res_norm, name="ret_w_out")
    return h1, hn2, (h0, hn, proj, out, states, y)


def _ret_layer_bwd(dh1, dh1_16, saved, W, tabs, after=(), on_grads=None):
    h0, hn, proj, out, states, y = saved
    d_w_out = _mm(y, dh1_16, ta=True, outs=[BF16], dw=("rows", 1, 0, None), name="d_ret_w_out", after=after)
    dy = _mm(dh1_16, W["ret_w_out"], tb=True, name="ret_w_out_dx", after=after)
    dout, dproj, d_gn = _ret_gate_bwd(out, proj, W["ret_gn"], dy, "ret_gate_bwd")
    dproj = _ret_bwd(proj, states, dout, dproj, tabs, "ret_scan_bwd")
    d_w_in = _mm(hn, dproj, ta=True, outs=[BF16], dw=("cols", 1, 0, None), name="d_ret_w_in")
    big = dict(ret_w_in=d_w_in, ret_w_out=d_w_out)
    later = () if on_grads is None else on_grads(big)
    dh0, _, d_mix = _dx_norm_bwd(dproj, W["ret_w_in"], h0, dh1, W["mix_norm"][0:1], "ret_w_in_dx", bblk=True, tm=256,
                                 after=later)
    return dh0, big, dict(mix_norm=d_mix, ret_gn=d_gn)


def _mla_layer_fwd(h0, hn, W, tabs):
    proj = _mm(hn, W["mla_w_in"], name="mla_w_in")

    def low_rank_norm(pv, gq, gkv):
        return _rms(pv[:, :MLA_Q_RANK], gq), _rms(pv[:, MLA_Q_RANK:MLA_Q_RANK + MLA_KV_RANK], gkv)

    cqn, ckvn = _rows(low_rank_norm, [proj], [W["mla_q_a_norm"], W["mla_kv_a_norm"]],
                      [(MLA_Q_RANK, BF16), (MLA_KV_RANK, BF16)], name="mla_low_rank_norm")
    q = _mm(cqn, W["mla_w_uq"], bblk=True, outs=[BF16], name="mla_w_uq")
    kv = _mm(ckvn, W["mla_w_ukv"], bblk=True, outs=[BF16], name="mla_w_ukv")
    qf, kf, vf = _mla_prep(q, kv, proj, W["mla_q_norm"] * (MLA_QKD ** -0.5 * LOG2E), W["mla_k_norm"], tabs, "mla_prep")
    o, lse = _flash_fwd(qf, kf, vf, "mla_flash")
    h1, hn2 = _mm_rows(o, W["mla_w_out"], extras=[h0], fulls=[W["mlp_norm"][1:2]], outs=[ROW_F32, ROW_BF16],
                       epilogue=_res_norm, name="mla_w_out")
    return h1, hn2, (h0, hn, proj, cqn, ckvn, q, kv, qf, kf, vf, o, lse)


def _mla_layer_bwd(dh1, dh1_16, saved, W, tabs):
    h0, hn, proj, cqn, ckvn, q, kv, qf, kf, vf, o, lse = saved
    d_w_out = _mm(o, dh1_16, ta=True, outs=[BF16], dw=("rows", 1, 0, None), name="d_mla_w_out")
    def with_delta(acc, ov):
        parts = []
        for h in range(MLA_HEADS):
            sl = slice(h * MLA_VD, (h + 1) * MLA_VD)
            d = jnp.sum(acc[:, sl] * ov[:, sl], axis=-1, keepdims=True)
            parts.append(jnp.broadcast_to(d, (d.shape[0], MLA_VD)))
        return jnp.concatenate(parts, axis=-1), acc

    delta, do16 = _mm_rows(dh1_16, W["mla_w_out"], tb=True, extras=[o], outs=[ROW_F32, ROW_BF16], epilogue=with_delta,
                           name="mla_w_out_dx")
    dqf, dkf, dvf = _flash_bwd(qf, kf, vf, do16, lse, delta, "mla_flash_bwd")
    dq, dkv, dkr, d_gq, d_gk = _mla_prep_bwd(q, kv, proj, W["mla_q_norm"], W["mla_k_norm"], tabs, dqf, dkf, dvf,
                                             "mla_prep_bwd")
    d_w_uq = _mm(cqn, dq, ta=True, outs=[BF16], dw=("cols", 1, 0, None), name="d_mla_w_uq")
    dcqn = _mm(dq, W["mla_w_uq"], tb=True, bblk=True, name="mla_w_uq_dx")
    d_w_ukv = _mm(ckvn, dkv, ta=True, outs=[BF16], dw=("cols", 1, 0, None), name="d_mla_w_ukv")
    dckvn = _mm(dkv, W["mla_w_ukv"], tb=True, bblk=True, name="mla_w_ukv_dx")

    def low_rank_bwd(pv, dcq, dckv, dkr_v, gq, gkv):
        dxq, dgq = _rms_bwd(pv[:, :MLA_Q_RANK], dcq, gq)
        dxkv, dgkv = _rms_bwd(pv[:, MLA_Q_RANK:MLA_Q_RANK + MLA_KV_RANK], dckv, gkv)
        return jnp.concatenate([dxq, dxkv, dkr_v], axis=-1), _colsum(dgq), _colsum(dgkv)

    dproj, d_gqa, d_gkva = _rows(low_rank_bwd, [proj, dcqn, dckvn, dkr], [W["mla_q_a_norm"], W["mla_kv_a_norm"]],
                                 [(MLA_IN_PAD, BF16)], [((1, MLA_Q_RANK), F32), ((1, MLA_KV_RANK), F32)],
                                 name="mla_low_rank_norm_bwd")
    d_w_in = _mm(hn, dproj, ta=True, outs=[BF16], dw=("rows", 1, 0, None), name="d_mla_w_in")
    dh0, dh0_16, d_mix = _dx_norm_bwd(dproj, W["mla_w_in"], h0, dh1, W["mix_norm"][1:2], "mla_w_in_dx")
    return (dh0, dh0_16, dict(mla_w_in=d_w_in, mla_w_uq=d_w_uq, mla_w_ukv=d_w_ukv, mla_w_out=d_w_out),
            dict(mix_norm=d_mix, mla_q_a_norm=d_gqa, mla_kv_a_norm=d_gkva, mla_q_norm=d_gq, mla_k_norm=d_gk))


def _local_step(x, p16, target, W):
    T = x.shape[0]
    ret_tabs, mla_tabs = _ret_tables(T), _mla_tables(T)
    h1, hn, s_ret = _ret_layer_fwd(x, W, ret_tabs)
    h3, hn, s_tail0 = _tail_fwd(h1, hn, p16, W, 0, "l0", next_gain=W["mix_norm"][1:2])
    h4, hn, s_mla = _mla_layer_fwd(h3, hn, W, mla_tabs)
    dy, loss, s_tail1 = _tail_fwd(h4, hn, p16, W, 1, "l1", target=target)
    dh4, dh4_16, g_t1, n_t1 = _tail_bwd(dy, s_tail1, p16, W, 1, "l1")
    dh3, _, g_mla, n_mla = _mla_layer_bwd(dh4, dh4_16, s_mla, W, mla_tabs)
    dh1, dh1_16, g_t0, n_t0 = _tail_bwd(dh3, s_tail0, p16, W, 0, "l0")
    dx, g_ret, n_ret = _ret_layer_bwd(dh1, dh1_16, s_ret, W, ret_tabs)
    return loss, dx, {**g_ret, **g_t0, **g_mla, **g_t1}, _small_grads(n_ret, n_t0, n_mla, n_t1)


def _loss_head(y, target):
    def fn(yv, tv):
        e = yv - tv
        return e * (1.0 / D_MODEL), jnp.full((1, 128), 0.5 / D_MODEL, F32) * jnp.sum(e * e)
    return _rows(fn, [y, target], [], [(D_MODEL, F32)], [((1, 128), F32)], name="loss_head")


def _small_grads(n_ret, n_t0, n_mla, n_t1):
    return dict(
        mix_norm=jnp.concatenate([n_ret["mix_norm"], n_mla["mix_norm"]], axis=0),
        mlp_norm=jnp.concatenate([n_t0["mlp_norm"], n_t1["mlp_norm"]], axis=0),
        ple_norm=jnp.concatenate([n_t0["ple_norm"], n_t1["ple_norm"]], axis=0),
        ret_gn=n_ret["ret_gn"], mla_q_a_norm=n_mla["mla_q_a_norm"], mla_kv_a_norm=n_mla["mla_kv_a_norm"],
        mla_q_norm=n_mla["mla_q_norm"], mla_k_norm=n_mla["mla_k_norm"])


_ORDER = ("mix_norm", "ret_w_in", "ret_gn", "ret_w_out", "mla_w_in", "mla_q_a_norm", "mla_kv_a_norm", "mla_w_uq",
          "mla_w_ukv", "mla_q_norm", "mla_k_norm", "mla_w_out", "mlp_norm", "mlp_w1", "mlp_w2", "ple_norm",
          "ple_gate_w", "ple_proj_w")
_TWO_LAYER = ("mlp_w1", "mlp_w2", "ple_gate_w", "ple_proj_w")
HEADS_PER_CHIP = MLA_HEADS // N_CHIPS
GAIN_ROWS = 32


def _travel_parts(w):
    uq = jnp.pad(w["mla_w_uq"][0].reshape(MLA_Q_RANK, HEADS_PER_CHIP, MLA_QKD), ((0, 0), (0, 0), (0, MLA_HP - MLA_QKD)))
    parts = {"ret_w_in": w["ret_w_in"][0], "ret_w_out": w["ret_w_out"][0]}
    for k in _TWO_LAYER:
        parts[k + "_0"] = w[k][0]
    parts["mla_w_in"] = jnp.pad(w["mla_w_in"][0], ((0, 0), (0, MLA_IN_PAD - MLA_IN)))
    parts["mla_w_uq"] = uq.reshape(MLA_Q_RANK, HEADS_PER_CHIP * MLA_HP)
    parts["mla_w_ukv"] = w["mla_w_ukv"][0]
    parts["mla_w_out"] = w["mla_w_out"][0]
    for k in _TWO_LAYER:
        parts[k + "_1"] = w[k][1]
    gains = jnp.concatenate([_pad_row(w["ret_gn"]), _pad_row(w["mla_q_a_norm"]), _pad_row(w["mla_kv_a_norm"]),
                             jnp.zeros((GAIN_ROWS - 3, PACK_W), F32)], axis=0)
    return {"gains": gains, **{k: v.astype(BF16) for k, v in parts.items()}}


def _full_weights(full):
    rows = lambda a: a.reshape(-1, a.shape[-1])
    W = {k: full[k] for k in ("ret_w_in", "mla_w_uq", "mla_w_ukv")}
    for k in ("ret_w_out", "mla_w_in", "mla_w_out"):
        W[k] = rows(full[k])
    W["mlp_w1"] = [full["mlp_w1_0"], full["mlp_w1_1"]]
    W["ple_proj_w"] = [full["ple_proj_w_0"], full["ple_proj_w_1"]]
    W["mlp_w2"] = [rows(full["mlp_w2_0"]), rows(full["mlp_w2_1"])]
    W["ple_gate_w"] = [rows(full["ple_gate_w_0"]), rows(full["ple_gate_w_1"])]
    return W


def _shard_grad(name, red, shape):
    if name == "mla_w_in":
        red = red.reshape(-1, MLA_IN_PAD)[:, :MLA_IN]
    elif name == "mla_w_uq":
        red = red.reshape(MLA_Q_RANK, HEADS_PER_CHIP, MLA_HP)[:, :, :MLA_QKD]
    return red.reshape(shape)


def _pad_row(v):
    v = v.reshape(1, -1)
    return jnp.pad(v, ((0, 0), (0, PACK_W - v.shape[1])))


def kernel(x, p, mix_norm, ret_w_in, ret_gn, ret_w_out, mla_w_in, mla_q_a_norm, mla_kv_a_norm, mla_w_uq, mla_w_ukv, mla_q_norm, mla_k_norm, mla_w_out, mlp_norm, mlp_w1, mlp_w2, ple_norm, ple_gate_w, ple_proj_w, loss_target, m_mix_norm, m_ret_w_in, m_ret_gn, m_ret_w_out, m_mla_w_in, m_mla_q_a_norm, m_mla_kv_a_norm, m_mla_w_uq, m_mla_w_ukv, m_mla_q_norm, m_mla_k_norm, m_mla_w_out, m_mlp_norm, m_mlp_w1, m_mlp_w2, m_ple_norm, m_ple_gate_w, m_ple_proj_w, v_mix_norm, v_ret_w_in, v_ret_gn, v_ret_w_out, v_mla_w_in, v_mla_q_a_norm, v_mla_kv_a_norm, v_mla_w_uq, v_mla_w_ukv, v_mla_q_norm, v_mla_k_norm, v_mla_w_out, v_mlp_norm, v_mlp_w1, v_mlp_w2, v_ple_norm, v_ple_gate_w, v_ple_proj_w):
    w = dict(mix_norm=mix_norm, ret_w_in=ret_w_in, ret_gn=ret_gn, ret_w_out=ret_w_out, mla_w_in=mla_w_in,
             mla_q_a_norm=mla_q_a_norm, mla_kv_a_norm=mla_kv_a_norm, mla_w_uq=mla_w_uq, mla_w_ukv=mla_w_ukv,
             mla_q_norm=mla_q_norm, mla_k_norm=mla_k_norm, mla_w_out=mla_w_out, mlp_norm=mlp_norm, mlp_w1=mlp_w1,
             mlp_w2=mlp_w2, ple_norm=ple_norm, ple_gate_w=ple_gate_w, ple_proj_w=ple_proj_w)
    m = dict(mix_norm=m_mix_norm, ret_w_in=m_ret_w_in, ret_gn=m_ret_gn, ret_w_out=m_ret_w_out, mla_w_in=m_mla_w_in,
             mla_q_a_norm=m_mla_q_a_norm, mla_kv_a_norm=m_mla_kv_a_norm, mla_w_uq=m_mla_w_uq, mla_w_ukv=m_mla_w_ukv,
             mla_q_norm=m_mla_q_norm, mla_k_norm=m_mla_k_norm, mla_w_out=m_mla_w_out, mlp_norm=m_mlp_norm,
             mlp_w1=m_mlp_w1, mlp_w2=m_mlp_w2, ple_norm=m_ple_norm, ple_gate_w=m_ple_gate_w, ple_proj_w=m_ple_proj_w)
    v = dict(mix_norm=v_mix_norm, ret_w_in=v_ret_w_in, ret_gn=v_ret_gn, ret_w_out=v_ret_w_out, mla_w_in=v_mla_w_in,
             mla_q_a_norm=v_mla_q_a_norm, mla_kv_a_norm=v_mla_kv_a_norm, mla_w_uq=v_mla_w_uq, mla_w_ukv=v_mla_w_ukv,
             mla_q_norm=v_mla_q_norm, mla_k_norm=v_mla_k_norm, mla_w_out=v_mla_w_out, mlp_norm=v_mlp_norm,
             mlp_w1=v_mlp_w1, mlp_w2=v_mlp_w2, ple_norm=v_ple_norm, ple_gate_w=v_ple_gate_w, ple_proj_w=v_ple_proj_w)
    xi, yi, ci = _place()
    chip = 2 * xi + yi
    n = N_CHIPS

    parts = _travel_parts(w)
    first = ("gains", "ret_w_in", "ret_w_out")
    later = [k for k in parts if k not in first]
    full = dict(zip(first, _gather_weights([parts[k] for k in first], "gather_first")))
    later_copies = _gather_copies([parts[k].shape[0] for k in later])
    g_send, g_recv, later_src, later_land, g_token = _split_start(
        "gather_later_start", [parts[k] for k in later],
        [jax.ShapeDtypeStruct((n, *parts[k].shape), BF16) for k in later], 3 * len(later), later_copies,
        after=[full["ret_w_in"]])
    gains = full["gains"]
    W = dict(mix_norm=mix_norm, mlp_norm=mlp_norm, ple_norm=ple_norm,
             mla_q_norm=jnp.pad(mla_q_norm, ((0, 0), (0, MLA_HP - MLA_QKD))),
             mla_k_norm=jnp.pad(mla_k_norm, ((0, 0), (0, MLA_HP - MLA_QKD))),
             ret_w_in=full["ret_w_in"], ret_w_out=full["ret_w_out"].reshape(-1, D_MODEL),
             ret_gn=gains[:, 0, :RET_HEADS * 128].reshape(n, RET_HEADS, 128).transpose(1, 0, 2).reshape(RET_HEADS, RET_DV),
             mla_q_a_norm=gains[:, 1, :MLA_Q_RANK // n].reshape(1, MLA_Q_RANK),
             mla_kv_a_norm=gains[:, 2, :MLA_KV_RANK // n].reshape(1, MLA_KV_RANK))
    x0, p16, target = x[0], p[:, 0].astype(BF16), loss_target[0]
    T = x0.shape[0]
    ret_tabs, mla_tabs = _ret_tables(T), _mla_tables(T)

    h1, hn, s_ret = _ret_layer_fwd(x0, W, ret_tabs, after=[g_token])
    landed = _split_wait("gather_later_wait", g_send, g_recv, later_src, later_land, later_copies, after=[h1])
    full.update(zip(later, _gather_weights([parts[k] for k in later], "gather_later_finish", landed=landed)))
    W.update(_full_weights(full))
    h3, hn, s_tail0 = _tail_fwd(h1, hn, p16, W, 0, "l0", next_gain=W["mix_norm"][1:2])
    h4, hn, s_mla = _mla_layer_fwd(h3, hn, W, mla_tabs)
    dy, loss, s_tail1 = _tail_fwd(h4, hn, p16, W, 1, "l1", target=target)

    dh4, dh4_16, g_t1, n_t1 = _tail_bwd(dy, s_tail1, p16, W, 1, "l1")
    dh3, _, g_mla, n_mla = _mla_layer_bwd(dh4, dh4_16, s_mla, W, mla_tabs)
    beg_a = _reduce_begin({**g_mla, **g_t1}, ci, "a")
    a_send, a_recv, a_src, a_land, a_token = _split_start(
        "scatter_a_start", beg_a[3], _got_shapes(beg_a[3]), 3 * len(beg_a[3]), _scatter_copies)
    dh1, dh1_16, g_t0, n_t0 = _tail_bwd(dh3, s_tail0, p16, W, 0, "l0", after=[a_token])
    beg_b = _reduce_begin(g_t0, ci, "b")
    b_send, b_recv, b_src, b_land, b_token = _split_start(
        "scatter_b_start", beg_b[3], _got_shapes(beg_b[3]), 3 * len(beg_b[3]), _scatter_copies)
    stage_c = {}

    def start_c(g_ret):
        beg = _reduce_begin(g_ret, ci, "c")
        stage_c["beg"] = beg
        stage_c["st"] = _split_start("scatter_c_start", beg[3], _got_shapes(beg[3]), 3 * len(beg[3]), _scatter_copies)
        return [stage_c["st"][4]]

    dx, _, n_ret = _ret_layer_bwd(dh1, dh1_16, s_ret, W, ret_tabs, after=[b_token], on_grads=start_c)
    got_a = _split_wait("scatter_a_wait", a_send, a_recv, a_src, a_land, _scatter_copies, after=[dx])
    got_b = _split_wait("scatter_b_wait", b_send, b_recv, b_src, b_land, _scatter_copies, after=[dx])
    got_c = _split_wait("scatter_c_wait", *stage_c["st"][:4], _scatter_copies, after=[dx])
    red = {**_reduce_end(beg_a, got_a, chip, ci), **_reduce_end(beg_b, got_b, chip, ci),
           **_reduce_end(stage_c["beg"], got_c, chip, ci)}
    red = dict(zip(red, _share_halves(list(red.values()))))
    gs = _small_grads(n_ret, n_t0, n_mla, n_t1)
    small_g = jnp.concatenate([
        gs["mix_norm"], gs["mlp_norm"], gs["ple_norm"], gs["ret_gn"].reshape(2, PACK_W), _pad_row(gs["mla_q_a_norm"]),
        _pad_row(gs["mla_kv_a_norm"]), _pad_row(gs["mla_q_norm"][:, :MLA_QKD]), _pad_row(gs["mla_k_norm"][:, :MLA_QKD]),
        _pad_row(loss[:, :1]), jnp.zeros((3, PACK_W), F32)], axis=0)
    tot = _allsum_small(small_g, "sum_small_grads")
    gn_all = tot[6:8].reshape(RET_HEADS, n, -1)
    g_small = dict(
        mix_norm=tot[0:2], mlp_norm=tot[2:4], ple_norm=tot[4:6],
        ret_gn=lax.dynamic_index_in_dim(gn_all, chip, axis=1, keepdims=False),
        mla_q_a_norm=lax.dynamic_index_in_dim(tot[8, :MLA_Q_RANK].reshape(n, -1), chip, axis=0, keepdims=True),
        mla_kv_a_norm=lax.dynamic_index_in_dim(tot[9, :MLA_KV_RANK].reshape(n, -1), chip, axis=0, keepdims=True),
        mla_q_norm=tot[10:11, :MLA_QKD], mla_k_norm=tot[11:12, :MLA_QKD])
    loss_out = tot[12, 0]

    outs = []
    for k in _ORDER:
        if k in _TWO_LAYER:
            res = None
            for i in (1, 0):
                res = _adamw(w[k], red[f"{k}_{i}"], m[k], v[k], f"adamw_{k}_{i}", layers=2, layer=i, into=res)
        elif k in red:
            res = _adamw(w[k], _shard_grad(k, red[k], w[k].shape), m[k], v[k], f"adamw_{k}")
        else:
            res = _adamw(w[k], g_small[k], m[k], v[k], f"adamw_{k}")
        outs.append(res)
    return (loss_out, dx[None], *[o[0] for o in outs], *[o[1] for o in outs], *[o[2] for o in outs],
            *[o[3] for o in outs])
```

```python
import functools

import jax
import jax.numpy as jnp
import numpy as np
from jax import lax
from jax.experimental import pallas as pl
from jax.experimental.pallas import tpu as pltpu

F32 = jnp.float32
BF16 = jnp.bfloat16

EPS = 1e-6
D_MODEL = 1024
CHUNK = 64
ROPE_THETA = 10000.0
RET_HEADS = 4
RET_DK = 256
RET_DV = 512
RET_GROUP = 1
RET_BLOCK = 256
MLA_HEADS = 8
MLA_NOPE = 128
MLA_ROPE = 64
MLA_QKD = 192
MLA_VD = 128
MLA_HP = 256
MLA_Q_RANK = 384
MLA_KV_RANK = 256
MLA_IN = 704
MLA_IN_PAD = 768
D_FF = 4096
PLE_DIM = 256
N_CHIPS = 4

ADAM_LR = 0.001
ADAM_B1 = 0.9
ADAM_B2 = 0.999
ADAM_EPS = 1e-08
ADAM_WD = 0.01
ADAM_STEP = 10

VMEM_LIMIT = 56 * 1024 * 1024
PACK_W = 1024
NEG = -1e30
LOG2E = 1.4426950408889634
FLASH_T = 512
FLASH_HEADS = 2
MM_SUB_ROWS = 256


def _cparams(sem=None):
    return pltpu.CompilerParams(dimension_semantics=sem, vmem_limit_bytes=VMEM_LIMIT)


def _pick(dim, pref):
    if dim <= pref:
        return dim
    t = pref
    while dim % t:
        t //= 2
    return t


def _mm(a, b, *, name, ta=False, tb=False, bblk=False, outs=None, extras=(), epilogue=None, dw=None,
        tm=1024, tn=512, after=()):
    if ta:
        K, M = a.shape
    else:
        M, K = a.shape
    if bblk and tb:
        nb, N, Kq = b.shape
        assert nb * Kq == K
    elif bblk:
        nb, Kb, Nq = b.shape
        N = nb * Nq
        assert Kb == K
    else:
        N = b.shape[0] if tb else b.shape[1]
    tn = _pick(Nq if (bblk and not tb) else N, tn)
    if dw is not None and dw[0] == "cols":
        tn = _pick(N // N_CHIPS, tn)
    tm = _pick(M // N_CHIPS if (dw is not None and dw[0] == "rows") else M, tm)
    grid = (M // tm, N // tn)

    a_spec = pl.BlockSpec((K, tm), lambda i, j: (0, i)) if ta else pl.BlockSpec((tm, K), lambda i, j: (i, 0))
    if bblk and tb:
        b_spec = pl.BlockSpec((nb, tn, Kq), lambda i, j: (0, j, 0))
    elif bblk:
        npb = Nq // tn
        b_spec = pl.BlockSpec((None, K, tn), lambda i, j: (j // npb, 0, j % npb))
    elif tb:
        b_spec = pl.BlockSpec((tn, K), lambda i, j: (j, 0))
    else:
        b_spec = pl.BlockSpec((K, tn), lambda i, j: (0, j))
    in_specs = [a_spec, b_spec] + [pl.BlockSpec((tm, tn), lambda i, j: (i, j)) for _ in extras]
    args = [a, b, *extras]
    aliases = {}
    if outs is None:
        outs = [F32]
    if dw is None:
        o_specs = [pl.BlockSpec((tm, tn), lambda i, j: (i, j)) for _ in outs]
        o_shapes = [jax.ShapeDtypeStruct((M, N), dt) for dt in outs]
    else:
        kind, layers, layer, into = dw
        if kind == "cols":
            per = (N // N_CHIPS) // tn
            o_specs = [pl.BlockSpec((None, None, tm, tn), lambda i, j: (j // per, layer, i, j % per))]
            o_shapes = [jax.ShapeDtypeStruct((N_CHIPS, layers, M, N // N_CHIPS), outs[0])]
        else:
            per = (M // N_CHIPS) // tm
            o_specs = [pl.BlockSpec((None, None, tm, tn), lambda i, j: (i // per, layer, i % per, j))]
            o_shapes = [jax.ShapeDtypeStruct((N_CHIPS, layers, M // N_CHIPS, N), outs[0])]
        if into is not None:
            aliases = {len(args): 0}
            in_specs.append(pl.BlockSpec(memory_space=pl.ANY))
            args.append(into)
    for t in after:
        in_specs.append(pl.BlockSpec(memory_space=pl.ANY))
        args.append(t)
    n_e, n_o = len(extras), len(outs)

    sub = _pick(tm, MM_SUB_ROWS)

    def body(a_ref, b_ref, *rest):
        e_refs, o_refs = rest[:n_e], rest[len(rest) - n_o:]
        for r0 in range(0, tm, sub):
            rows = slice(r0, r0 + sub)
            av = (a_ref[:, rows] if ta else a_ref[rows, :]).astype(BF16)
            if bblk and tb:
                acc = _dot_nt(av[:, :Kq], b_ref[0].astype(BF16))
                for s in range(1, nb):
                    acc = acc + _dot_nt(av[:, s * Kq:(s + 1) * Kq], b_ref[s].astype(BF16))
            elif ta:
                acc = _dot_tn(av, b_ref[...].astype(BF16))
            elif tb:
                acc = _dot_nt(av, b_ref[...].astype(BF16))
            else:
                acc = _dot(av, b_ref[...].astype(BF16))
            vals = (acc,) if epilogue is None else epilogue(acc, *[e[rows, :] for e in e_refs])
            for o, v in zip(o_refs, vals):
                o[rows, :] = v.astype(o.dtype)

    res = pl.pallas_call(
        body, name=name, grid=grid, in_specs=in_specs, out_specs=o_specs, out_shape=o_shapes,
        input_output_aliases=aliases, compiler_params=_cparams(("parallel", "arbitrary")),
    )(*args)
    return res[0] if n_o == 1 else res


def _mm_rows(a, b, *, name, epilogue, outs, tb=False, bblk=False, extras=(), fulls=(), accs=(), tm=512, after=()):
    M, K = a.shape
    tm = _pick(M, tm)
    sub = _pick(tm, MM_SUB_ROWS)
    nb = b.shape[0] if bblk else 1
    n_e, n_f, n_o, n_a = len(extras), len(fulls), len(outs), len(accs)
    n_in = 2 + n_e + n_f + len(after)

    def whole(t):
        return pl.BlockSpec(t.shape, lambda i, nd=t.ndim: (0,) * nd)

    in_specs = [pl.BlockSpec((tm, K), lambda i: (i, 0)), whole(b)]
    in_specs += [pl.BlockSpec((tm, e.shape[1]), lambda i: (i, 0)) for e in extras] + [whole(f) for f in fulls]
    in_specs += [pl.BlockSpec(memory_space=pl.ANY) for _ in after]
    out_specs = [pl.BlockSpec((tm, w), lambda i: (i, 0)) for w, _ in outs] + [pl.BlockSpec(s, lambda i: (0, 0)) for s, _ in accs]
    out_shape = [jax.ShapeDtypeStruct((M, w), dt) for w, dt in outs] + [jax.ShapeDtypeStruct(s, dt) for s, dt in accs]

    def body(a_ref, b_ref, *rest):
        e_refs, f_refs = rest[:n_e], rest[n_e:n_e + n_f]
        o_refs, acc_refs = rest[n_in - 2:n_in - 2 + n_o], rest[n_in - 2 + n_o:]
        fv = [f[...] for f in f_refs]
        totals = None
        for r0 in range(0, tm, sub):
            rows = slice(r0, r0 + sub)
            av = a_ref[rows, :].astype(BF16)
            if bblk and tb:
                kq = K // nb
                acc = _dot_nt(av[:, :kq], b_ref[0])
                for s in range(1, nb):
                    acc = acc + _dot_nt(av[:, s * kq:(s + 1) * kq], b_ref[s])
            elif bblk:
                acc = jnp.concatenate([_dot(av, b_ref[s]) for s in range(nb)], axis=-1)
            elif tb:
                acc = _dot_nt(av, b_ref[...])
            else:
                acc = _dot(av, b_ref[...])
            vals = epilogue(acc, *[e[rows, :] for e in e_refs], *fv)
            for o, v in zip(o_refs, vals[:n_o]):
                o[rows, :] = v.astype(o.dtype)
            part = vals[n_o:]
            totals = part if totals is None else [t + p for t, p in zip(totals, part)]
        first_step = pl.program_id(0) == 0
        for o, v in zip(acc_refs, totals):
            @pl.when(first_step)
            def _(o=o, v=v):
                o[...] = v.astype(o.dtype)

            @pl.when(jnp.logical_not(first_step))
            def _(o=o, v=v):
                o[...] += v.astype(o.dtype)

    return pl.pallas_call(
        body, name=name, grid=(M // tm,), in_specs=in_specs, out_specs=out_specs, out_shape=out_shape,
        compiler_params=_cparams(("arbitrary",)),
    )(a, b, *extras, *fulls, *after)


def _rows(fn, rows, fulls, outs, accs=(), *, name, tile=512, after=()):
    first = rows[0][0] if isinstance(rows[0], tuple) else rows[0]
    T = first.shape[0]
    tile = _pick(T, tile)
    in_specs, args = [], []
    for r in rows:
        if isinstance(r, tuple):
            arr, w, cb = r
            in_specs.append(pl.BlockSpec((tile, w), lambda i, cb=cb: (i, cb)))
        else:
            arr = r
            in_specs.append(pl.BlockSpec((tile, arr.shape[1]), lambda i: (i, 0)))
        args.append(arr)
    for f in fulls:
        in_specs.append(pl.BlockSpec(f.shape, lambda i, nd=f.ndim: (0,) * nd))
        args.append(f)
    outs = [o if len(o) == 4 else (*o, o[0], 0) for o in outs]
    out_specs = [pl.BlockSpec((tile, w), lambda i, cb=cb: (i, cb)) for w, _, _, cb in outs]
    out_specs += [pl.BlockSpec(s, lambda i: (0, 0)) for s, _ in accs]
    out_shape = [jax.ShapeDtypeStruct((T, tw), dt) for _, dt, tw, _ in outs]
    out_shape += [jax.ShapeDtypeStruct(s, dt) for s, dt in accs]
    n_in, n_out = len(args), len(outs)
    for t in after:
        in_specs.append(pl.BlockSpec(memory_space=pl.ANY))
        args.append(t)

    def body(*refs):
        vals = fn(*[r[...] for r in refs[:n_in]])
        o_refs = refs[len(args):]
        for o, v in zip(o_refs[:n_out], vals[:n_out]):
            o[...] = v.astype(o.dtype)
        first_step = pl.program_id(0) == 0
        for o, v in zip(o_refs[n_out:], vals[n_out:]):
            @pl.when(first_step)
            def _(o=o, v=v):
                o[...] = v.astype(o.dtype)

            @pl.when(jnp.logical_not(first_step))
            def _(o=o, v=v):
                o[...] += v.astype(o.dtype)

    res = pl.pallas_call(
        body, name=name, grid=(T // tile,), in_specs=in_specs, out_specs=out_specs, out_shape=out_shape,
        compiler_params=_cparams(("arbitrary",)),
    )(*args)
    return res


def _rowsum(v, mxu):
    if not mxu:
        return jnp.sum(v, axis=-1, keepdims=True)
    ones = jnp.ones((v.shape[1], v.shape[1]), BF16)
    hi = v.astype(BF16)
    lo = (v - hi.astype(F32)).astype(BF16)
    return _dot(hi, ones) + _dot(lo, ones)


def _rms(x, g, mxu=False):
    r = lax.rsqrt(_rowsum(x * x, mxu) / x.shape[-1] + EPS)
    return (x * r) * g


def _rms_bwd(x, dy, g, n=None, mxu=False):
    n = x.shape[-1] if n is None else n
    r = lax.rsqrt(_rowsum(x * x, mxu) / n + EPS)
    xh = x * r
    dxh = dy * g
    dx = r * (dxh - xh * (_rowsum(dxh * xh, mxu) / n))
    return dx, dy * xh


def _colsum(v):
    return jnp.sum(v, axis=0, keepdims=True)


def _sigmoid(x):
    return 1.0 / (1.0 + jnp.exp(-x))


def _widen(v, width):
    reps = width // v.shape[1]
    return v if reps == 1 else jnp.concatenate([v] * reps, axis=-1)


def _norm_fwd(h, gain, name):
    return _rows(lambda x, g: (_rms(x, g),), [h], [gain], [(h.shape[1], BF16)], name=name)[0]


def _norm_bwd(h, dhn, gain, dres, name):
    def fn(x, dy, dr, g):
        dx, dg = _rms_bwd(x, dy, g)
        return dr + dx, dr + dx, _colsum(dg)
    d = h.shape[1]
    return _rows(fn, [h, dhn, dres], [gain], [(d, F32), (d, BF16)], [((1, d), F32)], name=name)


def _rope_angles(T, dim):
    inv = (1.0 / (np.float32(ROPE_THETA) ** (np.arange(0, dim, 2, dtype=np.float32) / np.float32(dim)))).astype(np.float32)
    return np.arange(T, dtype=np.float32)[:, None] * inv[None, :]


def _ret_tables(T):
    ang = _rope_angles(T, RET_DK)
    log_gamma = np.log(np.float32(1.0) - np.float32(2.0) ** (-5.0 - np.arange(RET_HEADS, dtype=np.float32)))
    idx = np.arange(RET_BLOCK, dtype=np.float32)
    chunk = np.arange(RET_BLOCK) // CHUNK
    dist = idx[:, None] - idx[None, :]
    seen = np.where(chunk[:, None] == chunk[None, :], np.abs(dist), np.where(chunk[:, None] > chunk[None, :], dist, np.inf))
    intra = np.exp(log_gamma[:, None, None] * seen[None].astype(np.float32))
    qd = np.exp(log_gamma[:, None] * (idx + 1.0))[:, :, None]
    kd = np.exp(log_gamma[:, None] * (RET_BLOCK - 1.0 - idx))[:, :, None]
    cd = np.exp(log_gamma * RET_BLOCK)[:, None, None]
    return tuple(jnp.asarray(t, F32) for t in (np.cos(ang), np.sin(ang), intra, qd, kd, cd))


def _rope_half(x, c, s):
    x1, x2 = x[:, :RET_DK // 2], x[:, RET_DK // 2:]
    return jnp.concatenate([x1 * c - x2 * s, x2 * c + x1 * s], axis=-1)


def _rope_half_bwd(d, c, s):
    d1, d2 = d[:, :RET_DK // 2], d[:, RET_DK // 2:]
    return jnp.concatenate([d1 * c + d2 * s, d2 * c - d1 * s], axis=-1)


def _dot(a, b):
    return lax.dot_general(a, b, (((1,), (0,)), ((), ())), preferred_element_type=F32)


def _dot_nt(a, b):
    return lax.dot_general(a, b, (((1,), (1,)), ((), ())), preferred_element_type=F32)


def _dot_tn(a, b):
    return lax.dot_general(a, b, (((0,), (0,)), ((), ())), preferred_element_type=F32)


def _ret_specs(T, tb, rev):
    nj = T // tb
    jj = (lambda j: nj - 1 - j) if rev else (lambda j: j)
    g = RET_GROUP
    kq = RET_HEADS // g
    vq = 2 * RET_HEADS * RET_DK // (g * RET_DV)
    return dict(
        q=pl.BlockSpec((tb, g * RET_DK), lambda h, j: (jj(j), h)),
        k=pl.BlockSpec((tb, g * RET_DK), lambda h, j: (jj(j), kq + h)),
        v=pl.BlockSpec((tb, g * RET_DV), lambda h, j: (jj(j), vq + h)),
        tab=pl.BlockSpec((tb, RET_DK // 2), lambda h, j: (jj(j), 0)),
        intra=pl.BlockSpec((g, RET_BLOCK, RET_BLOCK), lambda h, j: (h, 0, 0)),
        dec=pl.BlockSpec((g, RET_BLOCK, 1), lambda h, j: (h, 0, 0)),
        cd=pl.BlockSpec((g, 1, 1), lambda h, j: (h, 0, 0)),
        o=pl.BlockSpec((tb, g * RET_DV), lambda h, j: (jj(j), h)),
        s=pl.BlockSpec((g, tb // RET_BLOCK, RET_DK, RET_DV), lambda h, j: (h, jj(j), 0, 0)),
    )


def _ret_fwd(proj, tabs, name):
    T = proj.shape[0]
    cos, sin, intra, qd, kd, cd = tabs
    tb = _pick(T, 512)
    cps = tb // RET_BLOCK
    sp = _ret_specs(T, tb, False)
    scale = RET_DK ** -0.5

    def body(q_ref, k_ref, v_ref, cos_ref, sin_ref, intra_ref, qd_ref, kd_ref, cd_ref, o_ref, s_ref, state):
        @pl.when(pl.program_id(1) == 0)
        def _():
            state[...] = jnp.zeros_like(state)

        for c in range(cps):
            rows = pl.ds(c * RET_BLOCK, RET_BLOCK)
            co, si = cos_ref[rows, :], sin_ref[rows, :]
            for h in range(RET_GROUP):
                hk, hv = slice(h * RET_DK, (h + 1) * RET_DK), slice(h * RET_DV, (h + 1) * RET_DV)
                q = _rope_half(q_ref[rows, hk].astype(F32), co, si)
                k = _rope_half(k_ref[rows, hk].astype(F32), co, si) * scale
                vb = v_ref[rows, hv].astype(BF16)
                st = state[h]
                sb = st.astype(BF16)
                s_ref[h, c] = sb
                sc = _dot_nt(q.astype(BF16), k.astype(BF16)) * intra_ref[h]
                inner = _dot(sc.astype(BF16), vb)
                cross = _dot((q * qd_ref[h]).astype(BF16), sb)
                o_ref[rows, hv] = inner + cross
                state[h] = st * cd_ref[h] + _dot_tn((k * kd_ref[h]).astype(BF16), vb)

    return pl.pallas_call(
        body, name=name, grid=(RET_HEADS // RET_GROUP, T // tb),
        in_specs=[sp["q"], sp["k"], sp["v"], sp["tab"], sp["tab"], sp["intra"], sp["dec"], sp["dec"], sp["cd"]],
        out_specs=[sp["o"], sp["s"]],
        out_shape=[jax.ShapeDtypeStruct((T, RET_HEADS * RET_DV), F32),
                   jax.ShapeDtypeStruct((RET_HEADS, T // RET_BLOCK, RET_DK, RET_DV), BF16)],
        scratch_shapes=[pltpu.VMEM((RET_GROUP, RET_DK, RET_DV), F32)],
        compiler_params=_cparams(("arbitrary", "arbitrary")),
    )(proj, proj, proj, cos, sin, intra, qd, kd, cd)


def _ret_bwd(proj, states, dout, dproj, tabs, name):
    assert RET_GROUP == 1
    T = proj.shape[0]
    cos, sin, intra, qd, kd, cd = tabs
    tb = _pick(T, 512)
    cps = tb // RET_BLOCK
    nj = T // tb
    sp = _ret_specs(T, tb, True)
    scale = RET_DK ** -0.5
    k0, v0 = RET_HEADS * RET_DK, 2 * RET_HEADS * RET_DK

    def body(q_ref, k_ref, v_ref, cos_ref, sin_ref, intra_ref, qd_ref, kd_ref, cd_ref, s_ref, do_ref, _dproj_in,
             out_ref, dq_s, dk_s, dv_s, sems, dstate):
        head, j = pl.program_id(0), pl.program_id(1)
        step = head * nj + j
        slot = step % 2
        dq_ref, dk_ref, dv_ref = dq_s.at[slot], dk_s.at[slot], dv_s.at[slot]

        @pl.when(j == 0)
        def _():
            dstate[...] = jnp.zeros_like(dstate)

        for c in reversed(range(cps)):
            rows = pl.ds(c * RET_BLOCK, RET_BLOCK)
            co, si = cos_ref[rows, :], sin_ref[rows, :]
            for h in range(RET_GROUP):
                hk, hv = slice(h * RET_DK, (h + 1) * RET_DK), slice(h * RET_DV, (h + 1) * RET_DV)
                q = _rope_half(q_ref[rows, hk].astype(F32), co, si)
                k = _rope_half(k_ref[rows, hk].astype(F32), co, si) * scale
                qb, kb = q.astype(BF16), k.astype(BF16)
                vb = v_ref[rows, hv].astype(BF16)
                dob = do_ref[rows, hv].astype(BF16)
                sb = s_ref[h, c]
                ia = intra_ref[h]
                pb = (_dot_nt(qb, kb) * ia).astype(BF16)
                dsn = dstate[h]
                dsb = dsn.astype(BF16)
                kdk = (k * kd_ref[h]).astype(BF16)
                qdq = (q * qd_ref[h]).astype(BF16)
                dv = _dot_tn(pb, dob) + _dot(kdk, dsb)
                dpb = (_dot_nt(dob, vb) * ia).astype(BF16)
                dq = _dot(dpb, kb) + _dot_nt(dob, sb) * qd_ref[h]
                dk = _dot_tn(dpb, qb) + _dot_nt(vb, dsb) * kd_ref[h]
                dstate[h] = dsn * cd_ref[h] + _dot_tn(qdq, dob)
                dq_ref[rows, hk] = _rope_half_bwd(dq, co, si).astype(BF16)
                dk_ref[rows, hk] = _rope_half_bwd(dk * scale, co, si).astype(BF16)
                dv_ref[rows, hv] = dv.astype(BF16)

        def copies(sl):
            r = pl.ds(pl.multiple_of((nj - 1 - j) * tb, tb), tb)
            cols = lambda first, w: pl.ds(pl.multiple_of(first + head * w, 128), w)
            return [pltpu.make_async_copy(dq_s.at[sl], out_ref.at[r, cols(0, RET_DK)], sems.at[sl, 0]),
                    pltpu.make_async_copy(dk_s.at[sl], out_ref.at[r, cols(k0, RET_DK)], sems.at[sl, 1]),
                    pltpu.make_async_copy(dv_s.at[sl], out_ref.at[r, cols(v0, RET_DV)], sems.at[sl, 2])]

        @pl.when(step > 0)
        def _():
            for cp in copies(1 - slot):
                cp.wait()

        for cp in copies(slot):
            cp.start()

        @pl.when(step == RET_HEADS * nj - 1)
        def _():
            for cp in copies(slot):
                cp.wait()

    return pl.pallas_call(
        body, name=name, grid=(RET_HEADS, nj),
        in_specs=[sp["q"], sp["k"], sp["v"], sp["tab"], sp["tab"], sp["intra"], sp["dec"], sp["dec"], sp["cd"],
                  sp["s"], sp["o"], pl.BlockSpec(memory_space=pl.ANY)],
        out_specs=pl.BlockSpec(memory_space=pl.ANY), out_shape=jax.ShapeDtypeStruct(dproj.shape, dproj.dtype),
        input_output_aliases={11: 0},
        scratch_shapes=[pltpu.VMEM((2, tb, RET_DK), BF16), pltpu.VMEM((2, tb, RET_DK), BF16),
                        pltpu.VMEM((2, tb, RET_DV), BF16), pltpu.SemaphoreType.DMA((2, 3)),
                        pltpu.VMEM((RET_GROUP, RET_DK, RET_DV), F32)],
        compiler_params=_cparams(("arbitrary", "arbitrary")),
    )(proj, proj, proj, cos, sin, intra, qd, kd, cd, states, dout, dproj)


def _ret_gate(out, proj, gn, name):
    def fn(o, g, *gains):
        g = g.astype(F32)
        parts = [_rms(o[:, h * RET_DV:(h + 1) * RET_DV], gains[h], mxu=True) for h in range(RET_HEADS)]
        return (g * _sigmoid(g) * jnp.concatenate(parts, axis=-1),)
    w = RET_HEADS * RET_DV
    return _rows(fn, [out, (proj, w, 2)], [gn[h:h + 1] for h in range(RET_HEADS)], [(w, BF16)], name=name)[0]


def _ret_gate_bwd(out, proj, gn, dy, name):
    def fn(o, g, d, *gains):
        g = g.astype(F32)
        sg = _sigmoid(g)
        silu = g * sg
        dsilu = sg * (1.0 + g * (1.0 - sg))
        dos, dgs = [], []
        row = lax.broadcasted_iota(jnp.int32, (RET_HEADS, RET_DV), 0)
        dgn = jnp.zeros((RET_HEADS, RET_DV), F32)
        for h in range(RET_HEADS):
            sl = slice(h * RET_DV, (h + 1) * RET_DV)
            oh = o[:, sl]
            dgs.append(d[:, sl] * _rms(oh, gains[h], mxu=True) * dsilu[:, sl])
            dx, dg = _rms_bwd(oh, d[:, sl] * silu[:, sl], gains[h], mxu=True)
            dos.append(dx)
            dgn = dgn + jnp.where(row == h, _colsum(dg), 0.0)
        return jnp.concatenate(dos, axis=-1), jnp.concatenate(dgs, axis=-1), dgn
    w = RET_HEADS * RET_DV
    return _rows(fn, [out, (proj, w, 2), dy], [gn[h:h + 1] for h in range(RET_HEADS)],
                 [(w, BF16), (w, BF16, proj.shape[1], 2)], [((RET_HEADS, RET_DV), F32)], name=name, tile=128)


def _mla_tables(T):
    ang = _rope_angles(T, MLA_ROPE)
    c, s = np.cos(ang), np.sin(ang)
    z32, z64 = np.zeros((T, 32), np.float32), np.zeros((T, 64), np.float32)
    cos_t = np.concatenate([c, c, z64], axis=1)
    sin_a = np.concatenate([-s, z32, z64], axis=1)
    sin_b = np.concatenate([z32, s, z64], axis=1)
    return tuple(jnp.asarray(t, F32) for t in (cos_t, sin_a, sin_b))


def _rope_blk(x, ct, sa, sb):
    return x * ct + pltpu.roll(x, 96, 1) * sa + pltpu.roll(x, 32, 1) * sb


def _rope_blk_bwd(d, ct, sa, sb):
    return d * ct + pltpu.roll(d * sa, 32, 1) + pltpu.roll(d * sb, 96, 1)


def _head_norm(x, gain):
    r = lax.rsqrt(_rowsum(x * x, True) / MLA_QKD + EPS)
    return (x * r) * gain


def _mla_prep(q, kv, proj, gq, gk, tabs, name):
    def fn(qv, kvv, kr, ct, sa, sb, gqv, gkv):
        qv, kvv = qv.astype(F32), kvv.astype(F32)
        qs, ks, vs = [], [], []
        for h in range(MLA_HEADS):
            b = h * MLA_HP
            y = _head_norm(qv[:, b:b + MLA_HP], gqv)
            qs += [y[:, :128], _rope_blk(y[:, 128:], ct, sa, sb)]
            y = _head_norm(jnp.concatenate([kvv[:, b:b + 128], kr], axis=-1), gkv)
            ks += [y[:, :128], _rope_blk(y[:, 128:], ct, sa, sb)]
            vs.append(kvv[:, b + 128:b + 256])
        return jnp.concatenate(qs, axis=-1), jnp.concatenate(ks, axis=-1), jnp.concatenate(vs, axis=-1)
    w = MLA_HEADS * MLA_HP
    return _rows(fn, [q, kv, (proj, 128, 5), *tabs], [gq, gk],
                 [(w, BF16), (w, BF16), (MLA_HEADS * MLA_VD, BF16)], name=name, tile=128)


def _mla_prep_bwd(q, kv, proj, gq, gk, tabs, dqf, dkf, dvf, name):
    def fn(qv, kvv, kr, ct, sa, sb, dqv, dkv, dvv, gqv, gkv):
        qv, kvv, dqv, dkv = (t.astype(F32) for t in (qv, kvv, dqv, dkv))
        dqs, dkvs = [], []
        dkr = jnp.zeros_like(kr)
        dgq = jnp.zeros((1, MLA_HP), F32)
        dgk = jnp.zeros((1, MLA_HP), F32)
        for h in range(MLA_HEADS):
            b = h * MLA_HP
            dy = jnp.concatenate([dqv[:, b:b + 128], _rope_blk_bwd(dqv[:, b + 128:b + 256], ct, sa, sb)], axis=-1)
            dx, dg = _rms_bwd(qv[:, b:b + MLA_HP], dy, gqv, MLA_QKD, mxu=True)
            dqs.append(dx)
            dgq = dgq + _colsum(dg)
            dy = jnp.concatenate([dkv[:, b:b + 128], _rope_blk_bwd(dkv[:, b + 128:b + 256], ct, sa, sb)], axis=-1)
            dx, dg = _rms_bwd(jnp.concatenate([kvv[:, b:b + 128], kr], axis=-1), dy, gkv, MLA_QKD, mxu=True)
            dkvs += [dx[:, :128], dvv[:, h * MLA_VD:(h + 1) * MLA_VD]]
            dkr = dkr + dx[:, 128:]
            dgk = dgk + _colsum(dg)
        return jnp.concatenate(dqs, axis=-1), jnp.concatenate(dkvs, axis=-1), dkr, dgq, dgk
    w = MLA_HEADS * MLA_HP
    return _rows(fn, [q, kv, (proj, 128, 5), *tabs, dqf, dkf, dvf], [gq, gk],
                 [(w, BF16), (w, BF16), (128, F32)], [((1, MLA_HP), F32), ((1, MLA_HP), F32)], name=name, tile=128)


def _chunk_mask(qi, ki, tq, tk):
    shift = CHUNK.bit_length() - 1
    rq = lax.shift_right_arithmetic(qi * tq + lax.broadcasted_iota(jnp.int32, (tq, tk), 0), shift)
    ck = lax.shift_right_arithmetic(ki * tk + lax.broadcasted_iota(jnp.int32, (tq, tk), 1), shift)
    return ck <= rq


def _flash_fwd(qf, kf, vf, name):
    T = qf.shape[0]
    t = _pick(T, FLASH_T)
    n = T // t
    scale = MLA_QKD ** -0.5

    g = FLASH_HEADS

    def body(q_ref, k_ref, v_ref, o_ref, lse_ref, m_s, l_s, acc):
        qi = pl.program_id(1)
        m_s[...] = jnp.full_like(m_s, NEG)
        l_s[...] = jnp.zeros_like(l_s)
        acc[...] = jnp.zeros_like(acc)

        def step(kb, masked):
            rows = pl.ds(pl.multiple_of(kb * t, t), t)
            for h in range(g):
                hq, hv = slice(h * MLA_HP, (h + 1) * MLA_HP), slice(h * MLA_VD, (h + 1) * MLA_VD)
                s = _dot_nt(q_ref[:, hq], k_ref[rows, hq])
                if masked:
                    s = jnp.where(_chunk_mask(0, 0, t, t), s, NEG)
                m_prev = m_s[:, hv]
                m_new = jnp.maximum(m_prev, jnp.max(s, axis=-1, keepdims=True))
                alpha = jnp.exp2(m_prev - m_new)
                p = jnp.exp2(s - _widen(m_new, t))
                l_s[:, hv] = alpha * l_s[:, hv] + sum(p[:, i * 128:(i + 1) * 128] for i in range(t // 128))
                acc[:, hv] = acc[:, hv] * alpha + _dot(p.astype(BF16), v_ref[rows, hv])
                m_s[:, hv] = m_new

        @pl.loop(0, qi)
        def _(kb):
            step(kb, False)

        step(qi, True)
        for h in range(g):
            hv = slice(h * MLA_VD, (h + 1) * MLA_VD)
            l = jnp.sum(l_s[:, hv], axis=-1, keepdims=True)
            o_ref[:, hv] = acc[:, hv] / l
            lse_ref[:, hv] = m_s[:, hv] + jnp.log2(l)

    qmap = lambda h, i: (i, h)
    kmap = lambda h, i: (0, h)
    vec = pltpu.VMEM((t, g * MLA_VD), F32)
    return pl.pallas_call(
        body, name=name, grid=(MLA_HEADS // g, n),
        in_specs=[pl.BlockSpec((t, g * MLA_HP), qmap), pl.BlockSpec((T, g * MLA_HP), kmap),
                  pl.BlockSpec((T, g * MLA_VD), kmap)],
        out_specs=[pl.BlockSpec((t, g * MLA_VD), qmap), pl.BlockSpec((t, g * MLA_VD), qmap)],
        out_shape=[jax.ShapeDtypeStruct((T, MLA_HEADS * MLA_VD), F32),
                   jax.ShapeDtypeStruct((T, MLA_HEADS * MLA_VD), F32)],
        scratch_shapes=[vec, vec, vec],
        compiler_params=_cparams(("parallel", "arbitrary")),
    )(qf, kf, vf)


def _flash_delta(o, do, name):
    def fn(ov, dv):
        parts = []
        for h in range(MLA_HEADS):
            sl = slice(h * MLA_VD, (h + 1) * MLA_VD)
            d = jnp.sum(dv[:, sl] * ov[:, sl], axis=-1, keepdims=True)
            parts.append(jnp.broadcast_to(d, (d.shape[0], MLA_VD)))
        return jnp.concatenate(parts, axis=-1), dv
    w = MLA_HEADS * MLA_VD
    return _rows(fn, [o, do], [], [(w, F32), (w, BF16)], name=name)


def _flash_bwd(qf, kf, vf, do16, lse, delta, name):
    T = qf.shape[0]
    t = _pick(T, FLASH_T)
    n = T // t
    scale = MLA_QKD ** -0.5

    def body(q_ref, k_ref, v_ref, do_ref, lse_ref, dl_ref, dq_out, dk_out, dv_out, dq_ref, dk_ref, dv_ref):
        kb = pl.program_id(1)

        @pl.when(kb == 0)
        def _():
            dq_ref[...] = jnp.zeros_like(dq_ref)

        dk_ref[...] = jnp.zeros_like(dk_ref)
        dv_ref[...] = jnp.zeros_like(dv_ref)
        k, v = k_ref[...], v_ref[...]

        def step(qb, masked):
            rows = pl.ds(pl.multiple_of(qb * t, t), t)
            q, dob = q_ref[rows, :], do_ref[rows, :]
            s = _dot_nt(q, k)
            if masked:
                s = jnp.where(_chunk_mask(0, 0, t, t), s, NEG)
            p = jnp.exp2(s - _widen(lse_ref[rows, :], t))
            ds = (p * (_dot_nt(dob, v) - _widen(dl_ref[rows, :], t))).astype(BF16)
            dv_ref[...] += _dot_tn(p.astype(BF16), dob)
            dk_ref[...] += _dot_tn(ds, q)
            dq_ref[rows, :] += _dot(ds, k)

        step(kb, True)

        @pl.loop(kb + 1, n)
        def _(qb):
            step(qb, False)

        dk_out[...] = (dk_ref[...] * (1.0 / LOG2E)).astype(BF16)
        dv_out[...] = dv_ref[...].astype(BF16)

        @pl.when(kb == n - 1)
        def _():
            dq_out[...] = (dq_ref[...] * scale).astype(BF16)

    qmap = lambda h, j: (0, h)
    kmap = lambda h, j: (j, h)
    return pl.pallas_call(
        body, name=name, grid=(MLA_HEADS, n),
        in_specs=[pl.BlockSpec((T, MLA_HP), qmap), pl.BlockSpec((t, MLA_HP), kmap), pl.BlockSpec((t, MLA_VD), kmap),
                  pl.BlockSpec((T, MLA_VD), qmap), pl.BlockSpec((T, MLA_VD), qmap), pl.BlockSpec((T, MLA_VD), qmap)],
        out_specs=[pl.BlockSpec((T, MLA_HP), qmap), pl.BlockSpec((t, MLA_HP), kmap), pl.BlockSpec((t, MLA_VD), kmap)],
        out_shape=[jax.ShapeDtypeStruct((T, MLA_HEADS * MLA_HP), BF16),
                   jax.ShapeDtypeStruct((T, MLA_HEADS * MLA_HP), BF16),
                   jax.ShapeDtypeStruct((T, MLA_HEADS * MLA_VD), BF16)],
        scratch_shapes=[pltpu.VMEM((T, MLA_HP), F32), pltpu.VMEM((t, MLA_HP), F32), pltpu.VMEM((t, MLA_VD), F32)],
        compiler_params=_cparams(("arbitrary", "arbitrary")),
    )(qf, kf, vf, do16, lse, delta)


MESH = pl.DeviceIdType.MESH
ANY = pl.BlockSpec(memory_space=pl.ANY)
_CHIP_FLIPS = ((1, 0), (0, 1), (1, 1))


def _place():
    return lax.axis_index("x"), lax.axis_index("y"), lax.axis_index("c")


def _other_chip(x, y, k):
    fx, fy = _CHIP_FLIPS[k]
    return ((1 - x) if fx else x), ((1 - y) if fy else y)


def _remote(src, dst, send_sems, recv_sems, k, to):
    return pltpu.make_async_remote_copy(src_ref=src, dst_ref=dst, send_sem=send_sems.at[k], recv_sem=recv_sems.at[k],
                                        device_id=to, device_id_type=MESH)


def _index(*vals):
    return jnp.stack(vals).astype(jnp.int32)


def _half(c, rows):
    return pl.ds(pl.multiple_of(c * rows, 16), rows)


def _gather_weights(parts, name, landed=None):
    n_w = len(parts)
    n_in = n_w if landed is None else 2 * n_w

    def body(*refs):
        ins, outs = refs[:n_w], refs[n_in:n_in + n_w]
        send_sems, recv_sems, local_sems = refs[n_in + n_w:]
        x, y, c = _place()
        j = 2 * x + y
        sibling = (x, y, 1 - c)
        chips = [_other_chip(x, y, k) for k in range(3)]
        pending = []
        for w in range(n_w):
            own = pltpu.make_async_copy(ins[w], outs[w].at[j], local_sems.at[w])
            own.start()
            pending.append(own)
        sent = []
        for w in range(n_w):
            if landed is not None:
                break
            r = _half(c, parts[w].shape[0] // 2)
            for k, (px, py) in enumerate(chips):
                cp = _remote(ins[w].at[r], outs[w].at[j, r], send_sems, recv_sems, 6 * w + k, (px, py, c))
                cp.start()
                sent.append(cp)
        for w in range(n_w):
            r = _half(c, parts[w].shape[0] // 2)
            for k, (px, py) in enumerate(chips):
                blk = outs[w].at[2 * px + py, r]
                if landed is None:
                    _remote(blk, blk, send_sems, recv_sems, 6 * w + k, (px, py, c)).wait_recv()
                cp = _remote(blk, blk, send_sems, recv_sems, 6 * w + 3 + k, sibling)
                cp.start()
                sent.append(cp)
        for w in range(n_w):
            r = _half(1 - c, parts[w].shape[0] // 2)
            for k, (px, py) in enumerate(chips):
                blk = outs[w].at[2 * px + py, r]
                _remote(blk, blk, send_sems, recv_sems, 6 * w + 3 + k, sibling).wait_recv()
        for cp in sent:
            cp.wait_send()
        for cp in pending:
            cp.wait()

    return pl.pallas_call(
        body, name=name, in_specs=[pl.BlockSpec(memory_space=pltpu.VMEM)] * n_w + [ANY] * (n_in - n_w),
        out_specs=[ANY] * n_w,
        out_shape=[jax.ShapeDtypeStruct((N_CHIPS, *p.shape), p.dtype) for p in parts],
        input_output_aliases={} if landed is None else {n_w + w: w for w in range(n_w)},
        scratch_shapes=[pltpu.SemaphoreType.DMA((6 * n_w,)), pltpu.SemaphoreType.DMA((6 * n_w,)),
                        pltpu.SemaphoreType.DMA((n_w,))],
        compiler_params=pltpu.CompilerParams(vmem_limit_bytes=VMEM_LIMIT),
    )(*parts, *(landed or []))


def _swap_halves(gs, name):
    n_w = len(gs)

    def body(*refs):
        g_refs, recv_refs = refs[:n_w], refs[n_w:2 * n_w]
        send_sems, recv_sems = refs[2 * n_w:]
        x, y, c = _place()
        sent = []
        for w in range(n_w):
            for jj in range(N_CHIPS):
                cp = _remote(g_refs[w].at[jj, 1 - c], recv_refs[w].at[jj], send_sems, recv_sems, N_CHIPS * w + jj,
                             (x, y, 1 - c))
                cp.start()
                sent.append(cp)
        for cp in sent:
            cp.wait()

    return pl.pallas_call(
        body, name=name, in_specs=[ANY] * n_w, out_specs=[ANY] * n_w,
        out_shape=[jax.ShapeDtypeStruct((N_CHIPS, *g.shape[2:]), g.dtype) for g in gs],
        scratch_shapes=[pltpu.SemaphoreType.DMA((N_CHIPS * n_w,)), pltpu.SemaphoreType.DMA((N_CHIPS * n_w,))],
    )(*gs)


def _pair_sum(g, recv, core, name):
    _, H, C = recv.shape
    tile = _pick(H, 256)

    def body(c_ref, own_ref, recv_ref, out_ref):
        out_ref[...] = (own_ref[...].astype(F32) + recv_ref[...].astype(F32)).astype(BF16)

    blk = pl.BlockSpec((None, tile, C), lambda jj, i, c: (jj, i, 0))
    return pl.pallas_call(
        body, name=name,
        grid_spec=pltpu.PrefetchScalarGridSpec(
            num_scalar_prefetch=1, grid=(N_CHIPS, H // tile),
            in_specs=[pl.BlockSpec((None, None, tile, C), lambda jj, i, c: (jj, c[0], i, 0)), blk],
            out_specs=blk),
        out_shape=jax.ShapeDtypeStruct((N_CHIPS, H, C), BF16),
        compiler_params=_cparams(("arbitrary", "arbitrary")),
    )(_index(core), g, recv)


def _chip_sum(g, recv, got, chip, core, name):
    _, H, C = recv.shape
    tile = _pick(H, 256)

    def body(s_ref, own_ref, recv_ref, g0_ref, g1_ref, g2_ref, out_ref):
        pair = own_ref[...].astype(F32) + recv_ref[...].astype(F32)
        out_ref[...] = ((pair + g0_ref[...].astype(F32)) + g1_ref[...].astype(F32)) + g2_ref[...].astype(F32)

    def got_spec(k):
        return pl.BlockSpec((None, tile, C), lambda i, s, k=k: (k, i, 0))

    return pl.pallas_call(
        body, name=name,
        grid_spec=pltpu.PrefetchScalarGridSpec(
            num_scalar_prefetch=1, grid=(H // tile,),
            in_specs=[pl.BlockSpec((None, None, tile, C), lambda i, s: (s[0], s[1], i, 0)),
                      pl.BlockSpec((None, tile, C), lambda i, s: (s[0], i, 0)), got_spec(0), got_spec(1), got_spec(2)],
            out_specs=pl.BlockSpec((None, tile, C), lambda i, s: (s[1], i, 0))),
        out_shape=jax.ShapeDtypeStruct((2, H, C), F32),
        compiler_params=_cparams(("arbitrary",)),
    )(_index(chip, core), g, recv, got, got, got)


def _scatter_chips(sums, name):
    n_w = len(sums)

    def body(*refs):
        a_refs, got_refs = refs[:n_w], refs[n_w:2 * n_w]
        send_sems, recv_sems = refs[2 * n_w:]
        x, y, c = _place()
        j = 2 * x + y
        sent = []
        for w in range(n_w):
            for k in range(3):
                px, py = _other_chip(x, y, k)
                pj = 2 * px + py
                cp = _remote(a_refs[w].at[pj], got_refs[w].at[(j - pj + 4) % 4 - 1], send_sems, recv_sems, 3 * w + k,
                             (px, py, c))
                cp.start()
                sent.append(cp)
        for w in range(n_w):
            for k in range(3):
                px, py = _other_chip(x, y, k)
                slot = got_refs[w].at[(2 * px + py - j + 4) % 4 - 1]
                _remote(slot, slot, send_sems, recv_sems, 3 * w + k, (px, py, c)).wait_recv()
        for cp in sent:
            cp.wait_send()

    return pl.pallas_call(
        body, name=name, in_specs=[ANY] * n_w, out_specs=[ANY] * n_w,
        out_shape=[jax.ShapeDtypeStruct((3, *a.shape[1:]), a.dtype) for a in sums],
        scratch_shapes=[pltpu.SemaphoreType.DMA((3 * n_w,)), pltpu.SemaphoreType.DMA((3 * n_w,))],
    )(*sums)


def _share_halves(reds):
    n_w = len(reds)

    def body(*refs):
        out_refs = refs[n_w:2 * n_w]
        send_sems, recv_sems = refs[2 * n_w:]
        x, y, c = _place()
        sent = []
        for w in range(n_w):
            blk = out_refs[w].at[c]
            cp = _remote(blk, blk, send_sems, recv_sems, w, (x, y, 1 - c))
            cp.start()
            sent.append(cp)
        for cp in sent:
            cp.wait()

    return pl.pallas_call(
        body, name="grad_share_halves", in_specs=[ANY] * n_w, out_specs=[ANY] * n_w,
        out_shape=[jax.ShapeDtypeStruct(r.shape, r.dtype) for r in reds],
        input_output_aliases={w: w for w in range(n_w)},
        scratch_shapes=[pltpu.SemaphoreType.DMA((n_w,)), pltpu.SemaphoreType.DMA((n_w,))],
    )(*reds)


def _allsum_small(v, name):
    R, W = v.shape
    n_dev = 8
    vm = pl.BlockSpec(memory_space=pltpu.VMEM)

    def body(v_ref, out_ref, buf, send_sems, recv_sems):
        x, y, c = _place()
        me = 4 * x + 2 * y + c
        buf[me] = v_ref[...]
        sent = []
        for k in range(1, n_dev):
            peer = ((1 - x) if k & 4 else x, (1 - y) if k & 2 else y, (1 - c) if k & 1 else c)
            cp = _remote(v_ref, buf.at[me], send_sems, recv_sems, k - 1, peer)
            cp.start()
            sent.append(cp)
        for cp in sent:
            cp.wait_recv()
        for cp in sent:
            cp.wait_send()
        acc = buf[0]
        for q in range(1, n_dev):
            acc = acc + buf[q]
        out_ref[...] = acc

    return pl.pallas_call(
        body, name=name, in_specs=[vm], out_specs=vm, out_shape=jax.ShapeDtypeStruct((R, W), v.dtype),
        scratch_shapes=[pltpu.VMEM((n_dev, R, W), v.dtype), pltpu.SemaphoreType.DMA((n_dev - 1,)),
                        pltpu.SemaphoreType.DMA((n_dev - 1,))],
    )(v)


HBM = pl.BlockSpec(memory_space=pltpu.HBM)
SEM = pl.BlockSpec(memory_space=pltpu.SEMAPHORE)
_DATAFLOW = pltpu.SideEffectType.DATAFLOW_SIDE_EFFECTING


def _split_start(name, srcs, land_shapes, n_copies, copies, after=()):
    ns, nl = len(srcs), len(land_shapes)
    lands = [lax.empty(s.shape, s.dtype) for s in land_shapes]

    def body(*refs):
        outs = refs[ns + nl + len(after):]
        for cp in copies(refs[:ns], refs[ns:ns + nl], outs[0], outs[1]):
            cp.start()
        outs[-1][...] = jnp.zeros_like(outs[-1])

    sems = pltpu.SemaphoreType.DMA((n_copies,))
    res = pl.pallas_call(
        body, name=name, in_specs=[HBM] * (ns + nl) + [ANY] * len(after),
        out_specs=(SEM, SEM, *[HBM] * (ns + nl), pl.BlockSpec(memory_space=pltpu.VMEM)),
        out_shape=(sems, sems, *[pltpu.HBM(a.shape, a.dtype) for a in srcs],
                   *[pltpu.HBM(s.shape, s.dtype) for s in land_shapes], jax.ShapeDtypeStruct((8, 128), F32)),
        input_output_aliases={i: 2 + i for i in range(ns + nl)},
        compiler_params=pltpu.CompilerParams(has_side_effects=_DATAFLOW),
    )(*[pltpu.with_memory_space_constraint(a, pltpu.HBM) for a in [*srcs, *lands]], *after)
    return res[0], res[1], list(res[2:2 + ns]), list(res[2 + ns:2 + ns + nl]), res[-1]


def _split_wait(name, send_sems, recv_sems, srcs, lands, copies, after=()):
    ns, nl = len(srcs), len(lands)

    def body(*refs):
        for cp in copies(refs[:ns], refs[ns:ns + nl], refs[ns + nl], refs[ns + nl + 1]):
            cp.wait_send()
            cp.wait_recv()

    res = pl.pallas_call(
        body, name=name, in_specs=[HBM] * (ns + nl) + [SEM, SEM] + [ANY] * len(after), out_specs=[HBM] * (ns + nl),
        out_shape=[pltpu.HBM(a.shape, a.dtype) for a in [*srcs, *lands]],
        input_output_aliases={i: i for i in range(ns + nl)},
        compiler_params=pltpu.CompilerParams(has_side_effects=_DATAFLOW),
    )(*srcs, *lands, send_sems, recv_sems, *after)
    return list(res[ns:])


def _gather_copies(rows):
    def copies(src_refs, land_refs, send_sems, recv_sems):
        x, y, c = _place()
        j = 2 * x + y
        out = []
        for w in range(len(src_refs)):
            r = _half(c, rows[w] // 2)
            for k in range(3):
                px, py = _other_chip(x, y, k)
                out.append(_remote(src_refs[w].at[r], land_refs[w].at[j, r], send_sems, recv_sems, 3 * w + k, (px, py, c)))
        return out
    return copies


def _scatter_copies(src_refs, land_refs, send_sems, recv_sems):
    x, y, c = _place()
    j = 2 * x + y
    out = []
    for w in range(len(src_refs)):
        for k in range(3):
            px, py = _other_chip(x, y, k)
            pj = 2 * px + py
            out.append(_remote(src_refs[w].at[pj], land_refs[w].at[(j - pj + 4) % 4 - 1], send_sems, recv_sems, 3 * w + k,
                               (px, py, c)))
    return out


def _reduce_begin(grads, core, tag):
    names = list(grads)
    gs = [grads[k].reshape(N_CHIPS, 2, -1, grads[k].shape[-1]) for k in names]
    recvs = _swap_halves(gs, f"grad_swap_halves_{tag}")
    sums = [_pair_sum(g, r, core, f"pair_sum_{k}") for k, g, r in zip(names, gs, recvs)]
    return names, gs, recvs, sums


def _reduce_end(begun, gots, chip, core):
    names, gs, recvs, _ = begun
    return {k: _chip_sum(g, r, t, chip, core, f"chip_sum_{k}") for k, g, r, t in zip(names, gs, recvs, gots)}


def _got_shapes(sums):
    return [jax.ShapeDtypeStruct((3, *a.shape[1:]), a.dtype) for a in sums]


def _adamw(w, g, m, v, name, layers=1, layer=0, into=None):
    shape = w.shape
    cols = shape[-1]
    w3, m3, v3 = (t.reshape(layers, -1, cols) for t in (w, m, v))
    rows = w3.shape[1]
    tile = _pick(rows, 256) if rows % 8 == 0 else rows
    n_in = 4 + (0 if into is None else 4)
    stack_g = layers > 1

    def body(*refs):
        wv, gv, mv, vv = (r[...] for r in refs[:4])
        d_ref, m_ref, v_ref = refs[len(refs) - 3:]
        m2 = ADAM_B1 * mv + (1.0 - ADAM_B1) * gv
        v2 = ADAM_B2 * vv + (1.0 - ADAM_B2) * jnp.square(gv)
        m_hat = m2 / (1.0 - ADAM_B1 ** ADAM_STEP)
        v_hat = v2 / (1.0 - ADAM_B2 ** ADAM_STEP)
        if stack_g:
            refs[n_in][...] = gv
        d_ref[...] = -ADAM_LR * (m_hat / (jnp.sqrt(v_hat) + ADAM_EPS) + ADAM_WD * wv)
        m_ref[...] = m2
        v_ref[...] = v2

    n_out = 4 if stack_g else 3
    lay = pl.BlockSpec((None, tile, cols), lambda i: (layer, i, 0))
    out = jax.ShapeDtypeStruct((layers, rows, cols), F32)
    res = pl.pallas_call(
        body, name=name, grid=(rows // tile,),
        in_specs=[lay, pl.BlockSpec((tile, cols), lambda i: (i, 0)), lay, lay] + [ANY] * (n_in - 4),
        out_specs=[lay] * n_out, out_shape=[out] * n_out,
        input_output_aliases={} if into is None else {4 + k: k for k in range(4)},
        compiler_params=_cparams(("arbitrary",)),
    )(w3, g.reshape(rows, cols), m3, v3, *([] if into is None else [t.reshape(layers, rows, cols) for t in into]))
    res = tuple(t.reshape(shape) for t in res)
    return res if stack_g else (g.reshape(shape), *res)


ROW_F32, ROW_BF16 = (D_MODEL, F32), (D_MODEL, BF16)


def _res_norm(acc, h, gain):
    hh = h + acc
    return hh, _rms(hh, gain)


def _dx_norm_bwd(d, w, h, dres, gain, name, **kw):
    def epilogue(acc, hv, dr, g):
        dx, dg = _rms_bwd(hv, acc, g)
        return dr + dx, dr + dx, _colsum(dg)
    return _mm_rows(d, w, tb=True, extras=[h, dres], fulls=[gain], outs=[ROW_F32, ROW_BF16], accs=[((1, D_MODEL), F32)],
                    epilogue=epilogue, name=name, **kw)


def _tail_fwd(h1, hn2, p16, W, i, tag, next_gain=None, target=None):
    a = _mm(hn2, W["mlp_w1"][i], bblk=True, outs=[BF16], name=f"{tag}_mlp_w1",
            epilogue=lambda acc: (jnp.square(jnp.maximum(acc, 0.0)),))
    h2, hn3 = _mm_rows(a, W["mlp_w2"][i], extras=[h1], fulls=[W["ple_norm"][i:i + 1]], outs=[ROW_F32, ROW_BF16],
                       epilogue=_res_norm, name=f"{tag}_mlp_w2")
    gl = _mm(hn3, W["ple_gate_w"][i], name=f"{tag}_ple_gate")
    if target is None:
        def gated(acc, g, h, gain):
            hh = h + _sigmoid(g) * acc
            return hh, acc, _rms(hh, gain)
        h3, pp, hn = _mm_rows(p16[i], W["ple_proj_w"][i], bblk=True, extras=[gl, h2], fulls=[next_gain],
                              outs=[ROW_F32, ROW_BF16, ROW_BF16], epilogue=gated, name=f"{tag}_ple_proj")
        return h3, hn, (h1, hn2, a, h2, hn3, gl, pp)

    def gated_loss(acc, g, h, t):
        e = h + _sigmoid(g) * acc - t
        return acc, e * (1.0 / D_MODEL), jnp.full((1, 128), 0.5 / D_MODEL, F32) * jnp.sum(e * e)
    pp, dy, loss = _mm_rows(p16[i], W["ple_proj_w"][i], bblk=True, extras=[gl, h2, target], outs=[ROW_BF16, ROW_F32],
                            accs=[((1, 128), F32)], epilogue=gated_loss, name=f"{tag}_ple_proj")
    return dy, loss, (h1, hn2, a, h2, hn3, gl, pp)


def _tail_bwd(dh3, saved, p16, W, i, tag, after=()):
    h1, hn2, a, h2, hn3, gl, pp = saved

    def gate_bwd(d, g, ppv):
        gate = _sigmoid(g)
        return d * gate, d * ppv * gate * (1.0 - gate)

    def dw(kind, name):
        return (kind, 1, 0, None)

    dpp, dgl = _rows(gate_bwd, [dh3, gl, pp], [], [(D_MODEL, BF16), (D_MODEL, BF16)], name=f"{tag}_ple_gate_bwd",
                     after=after)
    d_proj = _mm(p16[i], dpp, ta=True, outs=[BF16], dw=dw("cols", "ple_proj_w"), name=f"{tag}_d_ple_proj")
    d_gate = _mm(hn3, dgl, ta=True, outs=[BF16], dw=dw("rows", "ple_gate_w"), name=f"{tag}_d_ple_gate")
    dh2, dh2_16, d_ple_norm = _dx_norm_bwd(dgl, W["ple_gate_w"][i], h2, dh3, W["ple_norm"][i:i + 1],
                                           f"{tag}_ple_gate_dx")
    d_w2 = _mm(a, dh2_16, ta=True, outs=[BF16], dw=dw("rows", "mlp_w2"), name=f"{tag}_d_mlp_w2")
    dz = _mm(dh2_16, W["mlp_w2"][i], tb=True, extras=[a], outs=[BF16], name=f"{tag}_mlp_w2_dx",
             epilogue=lambda acc, av: (acc * (2.0 * jnp.sqrt(av.astype(F32))),))
    d_w1 = _mm(hn2, dz, ta=True, outs=[BF16], dw=dw("cols", "mlp_w1"), name=f"{tag}_d_mlp_w1")
    dh1, dh1_16, d_mlp_norm = _dx_norm_bwd(dz, W["mlp_w1"][i], h1, dh2, W["mlp_norm"][i:i + 1], f"{tag}_mlp_w1_dx",
                                           bblk=True)
    big = {f"mlp_w1_{i}": d_w1, f"mlp_w2_{i}": d_w2, f"ple_gate_w_{i}": d_gate, f"ple_proj_w_{i}": d_proj}
    return dh1, dh1_16, big, dict(mlp_norm=d_mlp_norm, ple_norm=d_ple_norm)


def _ret_layer_fwd(h0, W, tabs, after=()):
    hn = _rows(lambda x, g: (_rms(x, g),), [h0], [W["mix_norm"][0:1]], [(D_MODEL, BF16)], name="ret_mix_norm",
               after=after)[0]
    proj = _mm(hn, W["ret_w_in"], bblk=True, outs=[BF16], name="ret_w_in")
    out, states = _ret_fwd(proj, tabs, "ret_scan")
    y = _ret_gate(out, proj, W["ret_gn"], "ret_gate")
    h1, hn2 = _mm_rows(y, W["ret_w_out"], extras=[h0], fulls=[W["mlp_norm"][0:1]], outs=[ROW_F32, ROW_BF16],
                       epilogue=_res_norm, name="ret_w_out")
    return h1, hn2, (h0, hn, proj, out, states, y)


def _d_ret_w_out(dh1_16, saved):
    return _mm(saved[5], dh1_16, ta=True, outs=[BF16], dw=("rows", 1, 0, None), name="d_ret_w_out")


def _ret_layer_bwd(dh1, dh1_16, saved, W, tabs, after=(), on_grads=None, d_w_out=None):
    h0, hn, proj, out, states, y = saved
    d_w_out = _d_ret_w_out(dh1_16, saved) if d_w_out is None else d_w_out
    dy = _mm(dh1_16, W["ret_w_out"], tb=True, name="ret_w_out_dx", after=after)
    dout, dproj, d_gn = _ret_gate_bwd(out, proj, W["ret_gn"], dy, "ret_gate_bwd")
    dproj = _ret_bwd(proj, states, dout, dproj, tabs, "ret_scan_bwd")
    d_w_in = _mm(hn, dproj, ta=True, outs=[BF16], dw=("cols", 1, 0, None), name="d_ret_w_in")
    big = dict(ret_w_in=d_w_in, ret_w_out=d_w_out)
    later = () if on_grads is None else on_grads(big)
    dh0, _, d_mix = _dx_norm_bwd(dproj, W["ret_w_in"], h0, dh1, W["mix_norm"][0:1], "ret_w_in_dx", bblk=True, tm=256,
                                 after=later)
    return dh0, big, dict(mix_norm=d_mix, ret_gn=d_gn)


def _mla_layer_fwd(h0, hn, W, tabs):
    proj = _mm(hn, W["mla_w_in"], name="mla_w_in")

    def low_rank_norm(pv, gq, gkv):
        return _rms(pv[:, :MLA_Q_RANK], gq), _rms(pv[:, MLA_Q_RANK:MLA_Q_RANK + MLA_KV_RANK], gkv)

    cqn, ckvn = _rows(low_rank_norm, [proj], [W["mla_q_a_norm"], W["mla_kv_a_norm"]],
                      [(MLA_Q_RANK, BF16), (MLA_KV_RANK, BF16)], name="mla_low_rank_norm")
    q = _mm(cqn, W["mla_w_uq"], bblk=True, outs=[BF16], name="mla_w_uq")
    kv = _mm(ckvn, W["mla_w_ukv"], bblk=True, outs=[BF16], name="mla_w_ukv")
    qf, kf, vf = _mla_prep(q, kv, proj, W["mla_q_norm"] * (MLA_QKD ** -0.5 * LOG2E), W["mla_k_norm"], tabs, "mla_prep")
    o, lse = _flash_fwd(qf, kf, vf, "mla_flash")
    h1, hn2 = _mm_rows(o, W["mla_w_out"], extras=[h0], fulls=[W["mlp_norm"][1:2]], outs=[ROW_F32, ROW_BF16],
                       epilogue=_res_norm, name="mla_w_out")
    return h1, hn2, (h0, hn, proj, cqn, ckvn, q, kv, qf, kf, vf, o, lse)


def _mla_layer_bwd(dh1, dh1_16, saved, W, tabs):
    h0, hn, proj, cqn, ckvn, q, kv, qf, kf, vf, o, lse = saved
    d_w_out = _mm(o, dh1_16, ta=True, outs=[BF16], dw=("rows", 1, 0, None), name="d_mla_w_out")
    def with_delta(acc, ov):
        parts = []
        for h in range(MLA_HEADS):
            sl = slice(h * MLA_VD, (h + 1) * MLA_VD)
            d = jnp.sum(acc[:, sl] * ov[:, sl], axis=-1, keepdims=True)
            parts.append(jnp.broadcast_to(d, (d.shape[0], MLA_VD)))
        return jnp.concatenate(parts, axis=-1), acc

    delta, do16 = _mm_rows(dh1_16, W["mla_w_out"], tb=True, extras=[o], outs=[ROW_F32, ROW_BF16], epilogue=with_delta,
                           name="mla_w_out_dx")
    dqf, dkf, dvf = _flash_bwd(qf, kf, vf, do16, lse, delta, "mla_flash_bwd")
    dq, dkv, dkr, d_gq, d_gk = _mla_prep_bwd(q, kv, proj, W["mla_q_norm"], W["mla_k_norm"], tabs, dqf, dkf, dvf,
                                             "mla_prep_bwd")
    d_w_uq = _mm(cqn, dq, ta=True, outs=[BF16], dw=("cols", 1, 0, None), name="d_mla_w_uq")
    dcqn = _mm(dq, W["mla_w_uq"], tb=True, bblk=True, name="mla_w_uq_dx")
    d_w_ukv = _mm(ckvn, dkv, ta=True, outs=[BF16], dw=("cols", 1, 0, None), name="d_mla_w_ukv")
    dckvn = _mm(dkv, W["mla_w_ukv"], tb=True, bblk=True, name="mla_w_ukv_dx")

    def low_rank_bwd(pv, dcq, dckv, dkr_v, gq, gkv):
        dxq, dgq = _rms_bwd(pv[:, :MLA_Q_RANK], dcq, gq)
        dxkv, dgkv = _rms_bwd(pv[:, MLA_Q_RANK:MLA_Q_RANK + MLA_KV_RANK], dckv, gkv)
        return jnp.concatenate([dxq, dxkv, dkr_v], axis=-1), _colsum(dgq), _colsum(dgkv)

    dproj, d_gqa, d_gkva = _rows(low_rank_bwd, [proj, dcqn, dckvn, dkr], [W["mla_q_a_norm"], W["mla_kv_a_norm"]],
                                 [(MLA_IN_PAD, BF16)], [((1, MLA_Q_RANK), F32), ((1, MLA_KV_RANK), F32)],
                                 name="mla_low_rank_norm_bwd")
    d_w_in = _mm(hn, dproj, ta=True, outs=[BF16], dw=("rows", 1, 0, None), name="d_mla_w_in")
    dh0, dh0_16, d_mix = _dx_norm_bwd(dproj, W["mla_w_in"], h0, dh1, W["mix_norm"][1:2], "mla_w_in_dx")
    return (dh0, dh0_16, dict(mla_w_in=d_w_in, mla_w_uq=d_w_uq, mla_w_ukv=d_w_ukv, mla_w_out=d_w_out),
            dict(mix_norm=d_mix, mla_q_a_norm=d_gqa, mla_kv_a_norm=d_gkva, mla_q_norm=d_gq, mla_k_norm=d_gk))


def _local_step(x, p16, target, W):
    T = x.shape[0]
    ret_tabs, mla_tabs = _ret_tables(T), _mla_tables(T)
    h1, hn, s_ret = _ret_layer_fwd(x, W, ret_tabs)
    h3, hn, s_tail0 = _tail_fwd(h1, hn, p16, W, 0, "l0", next_gain=W["mix_norm"][1:2])
    h4, hn, s_mla = _mla_layer_fwd(h3, hn, W, mla_tabs)
    dy, loss, s_tail1 = _tail_fwd(h4, hn, p16, W, 1, "l1", target=target)
    dh4, dh4_16, g_t1, n_t1 = _tail_bwd(dy, s_tail1, p16, W, 1, "l1")
    dh3, _, g_mla, n_mla = _mla_layer_bwd(dh4, dh4_16, s_mla, W, mla_tabs)
    dh1, dh1_16, g_t0, n_t0 = _tail_bwd(dh3, s_tail0, p16, W, 0, "l0")
    dx, g_ret, n_ret = _ret_layer_bwd(dh1, dh1_16, s_ret, W, ret_tabs)
    return loss, dx, {**g_ret, **g_t0, **g_mla, **g_t1}, _small_grads(n_ret, n_t0, n_mla, n_t1)


def _loss_head(y, target):
    def fn(yv, tv):
        e = yv - tv
        return e * (1.0 / D_MODEL), jnp.full((1, 128), 0.5 / D_MODEL, F32) * jnp.sum(e * e)
    return _rows(fn, [y, target], [], [(D_MODEL, F32)], [((1, 128), F32)], name="loss_head")


def _small_grads(n_ret, n_t0, n_mla, n_t1):
    return dict(
        mix_norm=jnp.concatenate([n_ret["mix_norm"], n_mla["mix_norm"]], axis=0),
        mlp_norm=jnp.concatenate([n_t0["mlp_norm"], n_t1["mlp_norm"]], axis=0),
        ple_norm=jnp.concatenate([n_t0["ple_norm"], n_t1["ple_norm"]], axis=0),
        ret_gn=n_ret["ret_gn"], mla_q_a_norm=n_mla["mla_q_a_norm"], mla_kv_a_norm=n_mla["mla_kv_a_norm"],
        mla_q_norm=n_mla["mla_q_norm"], mla_k_norm=n_mla["mla_k_norm"])


_ORDER = ("mix_norm", "ret_w_in", "ret_gn", "ret_w_out", "mla_w_in", "mla_q_a_norm", "mla_kv_a_norm", "mla_w_uq",
          "mla_w_ukv", "mla_q_norm", "mla_k_norm", "mla_w_out", "mlp_norm", "mlp_w1", "mlp_w2", "ple_norm",
          "ple_gate_w", "ple_proj_w")
_TWO_LAYER = ("mlp_w1", "mlp_w2", "ple_gate_w", "ple_proj_w")
HEADS_PER_CHIP = MLA_HEADS // N_CHIPS
GAIN_ROWS = 32


def _travel_parts(w):
    uq = jnp.pad(w["mla_w_uq"][0].reshape(MLA_Q_RANK, HEADS_PER_CHIP, MLA_QKD), ((0, 0), (0, 0), (0, MLA_HP - MLA_QKD)))
    parts = {"ret_w_in": w["ret_w_in"][0], "ret_w_out": w["ret_w_out"][0]}
    for k in _TWO_LAYER:
        parts[k + "_0"] = w[k][0]
    parts["mla_w_in"] = jnp.pad(w["mla_w_in"][0], ((0, 0), (0, MLA_IN_PAD - MLA_IN)))
    parts["mla_w_uq"] = uq.reshape(MLA_Q_RANK, HEADS_PER_CHIP * MLA_HP)
    parts["mla_w_ukv"] = w["mla_w_ukv"][0]
    parts["mla_w_out"] = w["mla_w_out"][0]
    for k in _TWO_LAYER:
        parts[k + "_1"] = w[k][1]
    gains = jnp.concatenate([_pad_row(w["ret_gn"]), _pad_row(w["mla_q_a_norm"]), _pad_row(w["mla_kv_a_norm"]),
                             jnp.zeros((GAIN_ROWS - 3, PACK_W), F32)], axis=0)
    return {"gains": gains, **{k: v.astype(BF16) for k, v in parts.items()}}


def _full_weights(full):
    rows = lambda a: a.reshape(-1, a.shape[-1])
    W = {k: full[k] for k in ("ret_w_in", "mla_w_uq", "mla_w_ukv") if k in full}
    for k in ("ret_w_out", "mla_w_in", "mla_w_out"):
        if k in full:
            W[k] = rows(full[k])
    for k, by_rows in (("mlp_w1", False), ("ple_proj_w", False), ("mlp_w2", True), ("ple_gate_w", True)):
        layers = [full.get(f"{k}_{i}") for i in range(2)]
        W[k] = [rows(t) if (by_rows and t is not None) else t for t in layers]
    return W


def _shard_grad(name, red, shape):
    if name == "mla_w_in":
        red = red.reshape(-1, MLA_IN_PAD)[:, :MLA_IN]
    elif name == "mla_w_uq":
        red = red.reshape(MLA_Q_RANK, HEADS_PER_CHIP, MLA_HP)[:, :, :MLA_QKD]
    return red.reshape(shape)


def _pad_row(v):
    v = v.reshape(1, -1)
    return jnp.pad(v, ((0, 0), (0, PACK_W - v.shape[1])))


def kernel(x, p, mix_norm, ret_w_in, ret_gn, ret_w_out, mla_w_in, mla_q_a_norm, mla_kv_a_norm, mla_w_uq, mla_w_ukv, mla_q_norm, mla_k_norm, mla_w_out, mlp_norm, mlp_w1, mlp_w2, ple_norm, ple_gate_w, ple_proj_w, loss_target, m_mix_norm, m_ret_w_in, m_ret_gn, m_ret_w_out, m_mla_w_in, m_mla_q_a_norm, m_mla_kv_a_norm, m_mla_w_uq, m_mla_w_ukv, m_mla_q_norm, m_mla_k_norm, m_mla_w_out, m_mlp_norm, m_mlp_w1, m_mlp_w2, m_ple_norm, m_ple_gate_w, m_ple_proj_w, v_mix_norm, v_ret_w_in, v_ret_gn, v_ret_w_out, v_mla_w_in, v_mla_q_a_norm, v_mla_kv_a_norm, v_mla_w_uq, v_mla_w_ukv, v_mla_q_norm, v_mla_k_norm, v_mla_w_out, v_mlp_norm, v_mlp_w1, v_mlp_w2, v_ple_norm, v_ple_gate_w, v_ple_proj_w):
    w = dict(mix_norm=mix_norm, ret_w_in=ret_w_in, ret_gn=ret_gn, ret_w_out=ret_w_out, mla_w_in=mla_w_in,
             mla_q_a_norm=mla_q_a_norm, mla_kv_a_norm=mla_kv_a_norm, mla_w_uq=mla_w_uq, mla_w_ukv=mla_w_ukv,
             mla_q_norm=mla_q_norm, mla_k_norm=mla_k_norm, mla_w_out=mla_w_out, mlp_norm=mlp_norm, mlp_w1=mlp_w1,
             mlp_w2=mlp_w2, ple_norm=ple_norm, ple_gate_w=ple_gate_w, ple_proj_w=ple_proj_w)
    m = dict(mix_norm=m_mix_norm, ret_w_in=m_ret_w_in, ret_gn=m_ret_gn, ret_w_out=m_ret_w_out, mla_w_in=m_mla_w_in,
             mla_q_a_norm=m_mla_q_a_norm, mla_kv_a_norm=m_mla_kv_a_norm, mla_w_uq=m_mla_w_uq, mla_w_ukv=m_mla_w_ukv,
             mla_q_norm=m_mla_q_norm, mla_k_norm=m_mla_k_norm, mla_w_out=m_mla_w_out, mlp_norm=m_mlp_norm,
             mlp_w1=m_mlp_w1, mlp_w2=m_mlp_w2, ple_norm=m_ple_norm, ple_gate_w=m_ple_gate_w, ple_proj_w=m_ple_proj_w)
    v = dict(mix_norm=v_mix_norm, ret_w_in=v_ret_w_in, ret_gn=v_ret_gn, ret_w_out=v_ret_w_out, mla_w_in=v_mla_w_in,
             mla_q_a_norm=v_mla_q_a_norm, mla_kv_a_norm=v_mla_kv_a_norm, mla_w_uq=v_mla_w_uq, mla_w_ukv=v_mla_w_ukv,
             mla_q_norm=v_mla_q_norm, mla_k_norm=v_mla_k_norm, mla_w_out=v_mla_w_out, mlp_norm=v_mlp_norm,
             mlp_w1=v_mlp_w1, mlp_w2=v_mlp_w2, ple_norm=v_ple_norm, ple_gate_w=v_ple_gate_w, ple_proj_w=v_ple_proj_w)
    xi, yi, ci = _place()
    chip = 2 * xi + yi
    n = N_CHIPS

    parts = _travel_parts(w)
    first = ("gains", "ret_w_in", "ret_w_out")
    mid = [k + "_0" for k in _TWO_LAYER]
    last = [k for k in parts if k not in first and k not in mid]
    full = dict(zip(first, _gather_weights([parts[k] for k in first], "gather_first")))

    def gather_behind(names, tag, after):
        copies = _gather_copies([parts[k].shape[0] for k in names])
        started = _split_start(f"gather_{tag}_start", [parts[k] for k in names],
                               [jax.ShapeDtypeStruct((n, *parts[k].shape), BF16) for k in names], 3 * len(names),
                               copies, after=after)

        def arrive(after):
            landed = _split_wait(f"gather_{tag}_wait", *started[:4], copies, after=after)
            full.update(zip(names, _gather_weights([parts[k] for k in names], f"gather_{tag}_finish", landed=landed)))
            W.update(_full_weights(full))
        return started[4], arrive

    mid_token, mid_arrive = gather_behind(mid, "mid", [full["ret_w_in"]])
    g_token, last_arrive = gather_behind(last, "last", [mid_token])
    gains = full["gains"]
    W = dict(mix_norm=mix_norm, mlp_norm=mlp_norm, ple_norm=ple_norm,
             mla_q_norm=jnp.pad(mla_q_norm, ((0, 0), (0, MLA_HP - MLA_QKD))),
             mla_k_norm=jnp.pad(mla_k_norm, ((0, 0), (0, MLA_HP - MLA_QKD))),
             ret_w_in=full["ret_w_in"], ret_w_out=full["ret_w_out"].reshape(-1, D_MODEL),
             ret_gn=gains[:, 0, :RET_HEADS * 128].reshape(n, RET_HEADS, 128).transpose(1, 0, 2).reshape(RET_HEADS, RET_DV),
             mla_q_a_norm=gains[:, 1, :MLA_Q_RANK // n].reshape(1, MLA_Q_RANK),
             mla_kv_a_norm=gains[:, 2, :MLA_KV_RANK // n].reshape(1, MLA_KV_RANK))
    x0, p16, target = x[0], p[:, 0].astype(BF16), loss_target[0]
    T = x0.shape[0]
    ret_tabs, mla_tabs = _ret_tables(T), _mla_tables(T)

    h1, hn, s_ret = _ret_layer_fwd(x0, W, ret_tabs, after=[g_token])
    mid_arrive([h1])
    h3, hn, s_tail0 = _tail_fwd(h1, hn, p16, W, 0, "l0", next_gain=W["mix_norm"][1:2])
    last_arrive([h3])
    h4, hn, s_mla = _mla_layer_fwd(h3, hn, W, mla_tabs)
    dy, loss, s_tail1 = _tail_fwd(h4, hn, p16, W, 1, "l1", target=target)

    dh4, dh4_16, g_t1, n_t1 = _tail_bwd(dy, s_tail1, p16, W, 1, "l1")
    dh3, _, g_mla, n_mla = _mla_layer_bwd(dh4, dh4_16, s_mla, W, mla_tabs)
    beg_a = _reduce_begin({**g_mla, **g_t1}, ci, "a")
    a_send, a_recv, a_src, a_land, a_token = _split_start(
        "scatter_a_start", beg_a[3], _got_shapes(beg_a[3]), 3 * len(beg_a[3]), _scatter_copies)
    dh1, dh1_16, g_t0, n_t0 = _tail_bwd(dh3, s_tail0, p16, W, 0, "l0", after=[a_token])
    d_ret_w_out = _d_ret_w_out(dh1_16, s_ret)
    beg_b = _reduce_begin({**g_t0, "ret_w_out": d_ret_w_out}, ci, "b")
    b_send, b_recv, b_src, b_land, b_token = _split_start(
        "scatter_b_start", beg_b[3], _got_shapes(beg_b[3]), 3 * len(beg_b[3]), _scatter_copies)
    stage_c = {}

    def start_c(g_ret):
        beg = _reduce_begin({"ret_w_in": g_ret["ret_w_in"]}, ci, "c")
        stage_c["beg"] = beg
        stage_c["st"] = _split_start("scatter_c_start", beg[3], _got_shapes(beg[3]), 3 * len(beg[3]), _scatter_copies)
        return [stage_c["st"][4]]

    dx, _, n_ret = _ret_layer_bwd(dh1, dh1_16, s_ret, W, ret_tabs, after=[b_token], on_grads=start_c,
                                  d_w_out=d_ret_w_out)
    got_a = _split_wait("scatter_a_wait", a_send, a_recv, a_src, a_land, _scatter_copies, after=[dx])
    got_b = _split_wait("scatter_b_wait", b_send, b_recv, b_src, b_land, _scatter_copies, after=[dx])
    got_c = _split_wait("scatter_c_wait", *stage_c["st"][:4], _scatter_copies, after=[dx])
    red = {**_reduce_end(beg_a, got_a, chip, ci), **_reduce_end(beg_b, got_b, chip, ci),
           **_reduce_end(stage_c["beg"], got_c, chip, ci)}
    red = dict(zip(red, _share_halves(list(red.values()))))
    gs = _small_grads(n_ret, n_t0, n_mla, n_t1)
    small_g = jnp.concatenate([
        gs["mix_norm"], gs["mlp_norm"], gs["ple_norm"], gs["ret_gn"].reshape(2, PACK_W), _pad_row(gs["mla_q_a_norm"]),
        _pad_row(gs["mla_kv_a_norm"]), _pad_row(gs["mla_q_norm"][:, :MLA_QKD]), _pad_row(gs["mla_k_norm"][:, :MLA_QKD]),
        _pad_row(loss[:, :1]), jnp.zeros((3, PACK_W), F32)], axis=0)
    tot = _allsum_small(small_g, "sum_small_grads")
    gn_all = tot[6:8].reshape(RET_HEADS, n, -1)
    g_small = dict(
        mix_norm=tot[0:2], mlp_norm=tot[2:4], ple_norm=tot[4:6],
        ret_gn=lax.dynamic_index_in_dim(gn_all, chip, axis=1, keepdims=False),
        mla_q_a_norm=lax.dynamic_index_in_dim(tot[8, :MLA_Q_RANK].reshape(n, -1), chip, axis=0, keepdims=True),
        mla_kv_a_norm=lax.dynamic_index_in_dim(tot[9, :MLA_KV_RANK].reshape(n, -1), chip, axis=0, keepdims=True),
        mla_q_norm=tot[10:11, :MLA_QKD], mla_k_norm=tot[11:12, :MLA_QKD])
    loss_out = tot[12, 0]

    outs = []
    for k in _ORDER:
        if k in _TWO_LAYER:
            res = None
            for i in (1, 0):
                res = _adamw(w[k], red[f"{k}_{i}"], m[k], v[k], f"adamw_{k}_{i}", layers=2, layer=i, into=res)
        elif k in red:
            res = _adamw(w[k], _shard_grad(k, red[k], w[k].shape), m[k], v[k], f"adamw_{k}")
        else:
            res = _adamw(w[k], g_small[k], m[k], v[k], f"adamw_{k}")
        outs.append(res)
    return (loss_out, dx[None], *[o[0] for o in outs], *[o[1] for o in outs], *[o[2] for o in outs],
            *[o[3] for o in outs])
```

```python
import functools

import jax
import jax.numpy as jnp
import numpy as np
from jax import lax
from jax.experimental import pallas as pl
from jax.experimental.pallas import tpu as pltpu

F32 = jnp.float32
BF16 = jnp.bfloat16

EPS = 1e-6
D_MODEL = 1024
CHUNK = 64
ROPE_THETA = 10000.0
RET_HEADS = 4
RET_DK = 256
RET_DV = 512
RET_GROUP = 1
RET_BLOCK = 256
MLA_HEADS = 8
MLA_NOPE = 128
MLA_ROPE = 64
MLA_QKD = 192
MLA_VD = 128
MLA_HP = 256
MLA_Q_RANK = 384
MLA_KV_RANK = 256
MLA_IN = 704
MLA_IN_PAD = 768
D_FF = 4096
PLE_DIM = 256
N_CHIPS = 4

ADAM_LR = 0.001
ADAM_B1 = 0.9
ADAM_B2 = 0.999
ADAM_EPS = 1e-08
ADAM_WD = 0.01
ADAM_STEP = 10

VMEM_LIMIT = 56 * 1024 * 1024
PACK_W = 1024
NEG = -1e30
LOG2E = 1.4426950408889634
FLASH_T = 512
FLASH_HEADS = 2
MM_SUB_ROWS = 256


def _cparams(sem=None):
    return pltpu.CompilerParams(dimension_semantics=sem, vmem_limit_bytes=VMEM_LIMIT)


def _pick(dim, pref):
    if dim <= pref:
        return dim
    t = pref
    while dim % t:
        t //= 2
    return t


def _mm(a, b, *, name, ta=False, tb=False, bblk=False, outs=None, extras=(), epilogue=None, dw=None,
        tm=1024, tn=512, after=()):
    if ta:
        K, M = a.shape
    else:
        M, K = a.shape
    if bblk and tb:
        nb, N, Kq = b.shape
        assert nb * Kq == K
    elif bblk:
        nb, Kb, Nq = b.shape
        N = nb * Nq
        assert Kb == K
    else:
        N = b.shape[0] if tb else b.shape[1]
    tn = _pick(Nq if (bblk and not tb) else N, tn)
    if dw is not None and dw[0] == "cols":
        tn = _pick(N // N_CHIPS, tn)
    tm = _pick(M // N_CHIPS if (dw is not None and dw[0] == "rows") else M, tm)
    grid = (M // tm, N // tn)

    a_spec = pl.BlockSpec((K, tm), lambda i, j: (0, i)) if ta else pl.BlockSpec((tm, K), lambda i, j: (i, 0))
    if bblk and tb:
        b_spec = pl.BlockSpec((nb, tn, Kq), lambda i, j: (0, j, 0))
    elif bblk:
        npb = Nq // tn
        b_spec = pl.BlockSpec((None, K, tn), lambda i, j: (j // npb, 0, j % npb))
    elif tb:
        b_spec = pl.BlockSpec((tn, K), lambda i, j: (j, 0))
    else:
        b_spec = pl.BlockSpec((K, tn), lambda i, j: (0, j))
    in_specs = [a_spec, b_spec] + [pl.BlockSpec((tm, tn), lambda i, j: (i, j)) for _ in extras]
    args = [a, b, *extras]
    aliases = {}
    if outs is None:
        outs = [F32]
    if dw is None:
        o_specs = [pl.BlockSpec((tm, tn), lambda i, j: (i, j)) for _ in outs]
        o_shapes = [jax.ShapeDtypeStruct((M, N), dt) for dt in outs]
    else:
        kind, layers, layer, into = dw
        if kind == "cols":
            per = (N // N_CHIPS) // tn
            o_specs = [pl.BlockSpec((None, None, tm, tn), lambda i, j: (j // per, layer, i, j % per))]
            o_shapes = [jax.ShapeDtypeStruct((N_CHIPS, layers, M, N // N_CHIPS), outs[0])]
        else:
            per = (M // N_CHIPS) // tm
            o_specs = [pl.BlockSpec((None, None, tm, tn), lambda i, j: (i // per, layer, i % per, j))]
            o_shapes = [jax.ShapeDtypeStruct((N_CHIPS, layers, M // N_CHIPS, N), outs[0])]
        if into is not None:
            aliases = {len(args): 0}
            in_specs.append(pl.BlockSpec(memory_space=pl.ANY))
            args.append(into)
    for t in after:
        in_specs.append(pl.BlockSpec(memory_space=pl.ANY))
        args.append(t)
    n_e, n_o = len(extras), len(outs)

    sub = _pick(tm, MM_SUB_ROWS)

    def body(a_ref, b_ref, *rest):
        e_refs, o_refs = rest[:n_e], rest[len(rest) - n_o:]
        for r0 in range(0, tm, sub):
            rows = slice(r0, r0 + sub)
            av = (a_ref[:, rows] if ta else a_ref[rows, :]).astype(BF16)
            if bblk and tb:
                acc = _dot_nt(av[:, :Kq], b_ref[0].astype(BF16))
                for s in range(1, nb):
                    acc = acc + _dot_nt(av[:, s * Kq:(s + 1) * Kq], b_ref[s].astype(BF16))
            elif ta:
                acc = _dot_tn(av, b_ref[...].astype(BF16))
            elif tb:
                acc = _dot_nt(av, b_ref[...].astype(BF16))
            else:
                acc = _dot(av, b_ref[...].astype(BF16))
            vals = (acc,) if epilogue is None else epilogue(acc, *[e[rows, :] for e in e_refs])
            for o, v in zip(o_refs, vals):
                o[rows, :] = v.astype(o.dtype)

    res = pl.pallas_call(
        body, name=name, grid=grid, in_specs=in_specs, out_specs=o_specs, out_shape=o_shapes,
        input_output_aliases=aliases, compiler_params=_cparams(("parallel", "arbitrary")),
    )(*args)
    return res[0] if n_o == 1 else res


def _mm_rows(a, b, *, name, epilogue, outs, tb=False, bblk=False, extras=(), fulls=(), accs=(), tm=512, after=()):
    M, K = a.shape
    tm = _pick(M, tm)
    sub = _pick(tm, MM_SUB_ROWS)
    nb = b.shape[0] if bblk else 1
    n_e, n_f, n_o, n_a = len(extras), len(fulls), len(outs), len(accs)
    n_in = 2 + n_e + n_f + len(after)

    def whole(t):
        return pl.BlockSpec(t.shape, lambda i, nd=t.ndim: (0,) * nd)

    in_specs = [pl.BlockSpec((tm, K), lambda i: (i, 0)), whole(b)]
    in_specs += [pl.BlockSpec((tm, e.shape[1]), lambda i: (i, 0)) for e in extras] + [whole(f) for f in fulls]
    in_specs += [pl.BlockSpec(memory_space=pl.ANY) for _ in after]
    out_specs = [pl.BlockSpec((tm, w), lambda i: (i, 0)) for w, _ in outs] + [pl.BlockSpec(s, lambda i: (0, 0)) for s, _ in accs]
    out_shape = [jax.ShapeDtypeStruct((M, w), dt) for w, dt in outs] + [jax.ShapeDtypeStruct(s, dt) for s, dt in accs]

    def body(a_ref, b_ref, *rest):
        e_refs, f_refs = rest[:n_e], rest[n_e:n_e + n_f]
        o_refs, acc_refs = rest[n_in - 2:n_in - 2 + n_o], rest[n_in - 2 + n_o:]
        fv = [f[...] for f in f_refs]
        totals = None
        for r0 in range(0, tm, sub):
            rows = slice(r0, r0 + sub)
            av = a_ref[rows, :].astype(BF16)
            if bblk and tb:
                kq = K // nb
                acc = _dot_nt(av[:, :kq], b_ref[0])
                for s in range(1, nb):
                    acc = acc + _dot_nt(av[:, s * kq:(s + 1) * kq], b_ref[s])
            elif bblk:
                acc = jnp.concatenate([_dot(av, b_ref[s]) for s in range(nb)], axis=-1)
            elif tb:
                acc = _dot_nt(av, b_ref[...])
            else:
                acc = _dot(av, b_ref[...])
            vals = epilogue(acc, *[e[rows, :] for e in e_refs], *fv)
            for o, v in zip(o_refs, vals[:n_o]):
                o[rows, :] = v.astype(o.dtype)
            part = vals[n_o:]
            totals = part if totals is None else [t + p for t, p in zip(totals, part)]
        first_step = pl.program_id(0) == 0
        for o, v in zip(acc_refs, totals):
            @pl.when(first_step)
            def _(o=o, v=v):
                o[...] = v.astype(o.dtype)

            @pl.when(jnp.logical_not(first_step))
            def _(o=o, v=v):
                o[...] += v.astype(o.dtype)

    return pl.pallas_call(
        body, name=name, grid=(M // tm,), in_specs=in_specs, out_specs=out_specs, out_shape=out_shape,
        compiler_params=_cparams(("arbitrary",)),
    )(a, b, *extras, *fulls, *after)


def _rows(fn, rows, fulls, outs, accs=(), *, name, tile=512, after=()):
    first = rows[0][0] if isinstance(rows[0], tuple) else rows[0]
    T = first.shape[0]
    tile = _pick(T, tile)
    in_specs, args = [], []
    for r in rows:
        if isinstance(r, tuple):
            arr, w, cb = r
            in_specs.append(pl.BlockSpec((tile, w), lambda i, cb=cb: (i, cb)))
        else:
            arr = r
            in_specs.append(pl.BlockSpec((tile, arr.shape[1]), lambda i: (i, 0)))
        args.append(arr)
    for f in fulls:
        in_specs.append(pl.BlockSpec(f.shape, lambda i, nd=f.ndim: (0,) * nd))
        args.append(f)
    outs = [o if len(o) == 4 else (*o, o[0], 0) for o in outs]
    out_specs = [pl.BlockSpec((tile, w), lambda i, cb=cb: (i, cb)) for w, _, _, cb in outs]
    out_specs += [pl.BlockSpec(s, lambda i: (0, 0)) for s, _ in accs]
    out_shape = [jax.ShapeDtypeStruct((T, tw), dt) for _, dt, tw, _ in outs]
    out_shape += [jax.ShapeDtypeStruct(s, dt) for s, dt in accs]
    n_in, n_out = len(args), len(outs)
    for t in after:
        in_specs.append(pl.BlockSpec(memory_space=pl.ANY))
        args.append(t)

    def body(*refs):
        vals = fn(*[r[...] for r in refs[:n_in]])
        o_refs = refs[len(args):]
        for o, v in zip(o_refs[:n_out], vals[:n_out]):
            o[...] = v.astype(o.dtype)
        first_step = pl.program_id(0) == 0
        for o, v in zip(o_refs[n_out:], vals[n_out:]):
            @pl.when(first_step)
            def _(o=o, v=v):
                o[...] = v.astype(o.dtype)

            @pl.when(jnp.logical_not(first_step))
            def _(o=o, v=v):
                o[...] += v.astype(o.dtype)

    res = pl.pallas_call(
        body, name=name, grid=(T // tile,), in_specs=in_specs, out_specs=out_specs, out_shape=out_shape,
        compiler_params=_cparams(("arbitrary",)),
    )(*args)
    return res


def _rowsum(v, mxu):
    if not mxu:
        return jnp.sum(v, axis=-1, keepdims=True)
    ones = jnp.ones((v.shape[1], v.shape[1]), BF16)
    hi = v.astype(BF16)
    lo = (v - hi.astype(F32)).astype(BF16)
    return _dot(hi, ones) + _dot(lo, ones)


def _rms(x, g, mxu=False):
    r = lax.rsqrt(_rowsum(x * x, mxu) / x.shape[-1] + EPS)
    return (x * r) * g


def _rms_bwd(x, dy, g, n=None, mxu=False):
    n = x.shape[-1] if n is None else n
    r = lax.rsqrt(_rowsum(x * x, mxu) / n + EPS)
    xh = x * r
    dxh = dy * g
    dx = r * (dxh - xh * (_rowsum(dxh * xh, mxu) / n))
    return dx, dy * xh


def _colsum(v):
    return jnp.sum(v, axis=0, keepdims=True)


def _sigmoid(x):
    return 1.0 / (1.0 + jnp.exp(-x))


def _widen(v, width):
    reps = width // v.shape[1]
    return v if reps == 1 else jnp.concatenate([v] * reps, axis=-1)


def _rope_angles(T, dim):
    inv = (1.0 / (np.float32(ROPE_THETA) ** (np.arange(0, dim, 2, dtype=np.float32) / np.float32(dim)))).astype(np.float32)
    return np.arange(T, dtype=np.float32)[:, None] * inv[None, :]


def _ret_tables(T):
    ang = _rope_angles(T, RET_DK)
    log_gamma = np.log(np.float32(1.0) - np.float32(2.0) ** (-5.0 - np.arange(RET_HEADS, dtype=np.float32)))
    idx = np.arange(RET_BLOCK, dtype=np.float32)
    chunk = np.arange(RET_BLOCK) // CHUNK
    dist = idx[:, None] - idx[None, :]
    seen = np.where(chunk[:, None] == chunk[None, :], np.abs(dist), np.where(chunk[:, None] > chunk[None, :], dist, np.inf))
    intra = np.exp(log_gamma[:, None, None] * seen[None].astype(np.float32))
    qd = np.exp(log_gamma[:, None] * (idx + 1.0))[:, :, None]
    kd = np.exp(log_gamma[:, None] * (RET_BLOCK - 1.0 - idx))[:, :, None]
    cd = np.exp(log_gamma * RET_BLOCK)[:, None, None]
    return tuple(jnp.asarray(t, F32) for t in (np.cos(ang), np.sin(ang), intra, qd, kd, cd))


def _rope_half(x, c, s):
    x1, x2 = x[:, :RET_DK // 2], x[:, RET_DK // 2:]
    return jnp.concatenate([x1 * c - x2 * s, x2 * c + x1 * s], axis=-1)


def _rope_half_bwd(d, c, s):
    d1, d2 = d[:, :RET_DK // 2], d[:, RET_DK // 2:]
    return jnp.concatenate([d1 * c + d2 * s, d2 * c - d1 * s], axis=-1)


def _dot(a, b):
    return lax.dot_general(a, b, (((1,), (0,)), ((), ())), preferred_element_type=F32)


def _dot_nt(a, b):
    return lax.dot_general(a, b, (((1,), (1,)), ((), ())), preferred_element_type=F32)


def _dot_tn(a, b):
    return lax.dot_general(a, b, (((0,), (0,)), ((), ())), preferred_element_type=F32)


def _ret_specs(T, tb, rev):
    nj = T // tb
    jj = (lambda j: nj - 1 - j) if rev else (lambda j: j)
    g = RET_GROUP
    kq = RET_HEADS // g
    vq = 2 * RET_HEADS * RET_DK // (g * RET_DV)
    return dict(
        q=pl.BlockSpec((tb, g * RET_DK), lambda h, j: (jj(j), h)),
        k=pl.BlockSpec((tb, g * RET_DK), lambda h, j: (jj(j), kq + h)),
        v=pl.BlockSpec((tb, g * RET_DV), lambda h, j: (jj(j), vq + h)),
        tab=pl.BlockSpec((tb, RET_DK // 2), lambda h, j: (jj(j), 0)),
        intra=pl.BlockSpec((g, RET_BLOCK, RET_BLOCK), lambda h, j: (h, 0, 0)),
        dec=pl.BlockSpec((g, RET_BLOCK, 1), lambda h, j: (h, 0, 0)),
        cd=pl.BlockSpec((g, 1, 1), lambda h, j: (h, 0, 0)),
        o=pl.BlockSpec((tb, g * RET_DV), lambda h, j: (jj(j), h)),
        s=pl.BlockSpec((g, tb // RET_BLOCK, RET_DK, RET_DV), lambda h, j: (h, jj(j), 0, 0)),
    )


def _ret_fwd(proj, tabs, name):
    T = proj.shape[0]
    cos, sin, intra, qd, kd, cd = tabs
    tb = _pick(T, 512)
    cps = tb // RET_BLOCK
    sp = _ret_specs(T, tb, False)
    scale = RET_DK ** -0.5

    def body(q_ref, k_ref, v_ref, cos_ref, sin_ref, intra_ref, qd_ref, kd_ref, cd_ref, o_ref, s_ref, state):
        @pl.when(pl.program_id(1) == 0)
        def _():
            state[...] = jnp.zeros_like(state)

        for c in range(cps):
            rows = pl.ds(c * RET_BLOCK, RET_BLOCK)
            co, si = cos_ref[rows, :], sin_ref[rows, :]
            for h in range(RET_GROUP):
                hk, hv = slice(h * RET_DK, (h + 1) * RET_DK), slice(h * RET_DV, (h + 1) * RET_DV)
                q = _rope_half(q_ref[rows, hk].astype(F32), co, si)
                k = _rope_half(k_ref[rows, hk].astype(F32), co, si) * scale
                vb = v_ref[rows, hv].astype(BF16)
                st = state[h]
                sb = st.astype(BF16)
                s_ref[h, c] = sb
                sc = _dot_nt(q.astype(BF16), k.astype(BF16)) * intra_ref[h]
                inner = _dot(sc.astype(BF16), vb)
                cross = _dot((q * qd_ref[h]).astype(BF16), sb)
                o_ref[rows, hv] = inner + cross
                state[h] = st * cd_ref[h] + _dot_tn((k * kd_ref[h]).astype(BF16), vb)

    return pl.pallas_call(
        body, name=name, grid=(RET_HEADS // RET_GROUP, T // tb),
        in_specs=[sp["q"], sp["k"], sp["v"], sp["tab"], sp["tab"], sp["intra"], sp["dec"], sp["dec"], sp["cd"]],
        out_specs=[sp["o"], sp["s"]],
        out_shape=[jax.ShapeDtypeStruct((T, RET_HEADS * RET_DV), F32),
                   jax.ShapeDtypeStruct((RET_HEADS, T // RET_BLOCK, RET_DK, RET_DV), BF16)],
        scratch_shapes=[pltpu.VMEM((RET_GROUP, RET_DK, RET_DV), F32)],
        compiler_params=_cparams(("arbitrary", "arbitrary")),
    )(proj, proj, proj, cos, sin, intra, qd, kd, cd)


def _ret_bwd(proj, states, dout, dproj, tabs, name):
    assert RET_GROUP == 1
    T = proj.shape[0]
    cos, sin, intra, qd, kd, cd = tabs
    tb = _pick(T, 512)
    cps = tb // RET_BLOCK
    nj = T // tb
    sp = _ret_specs(T, tb, True)
    scale = RET_DK ** -0.5
    k0, v0 = RET_HEADS * RET_DK, 2 * RET_HEADS * RET_DK

    def body(q_ref, k_ref, v_ref, cos_ref, sin_ref, intra_ref, qd_ref, kd_ref, cd_ref, s_ref, do_ref, _dproj_in,
             out_ref, dq_s, dk_s, dv_s, sems, dstate):
        head, j = pl.program_id(0), pl.program_id(1)
        step = head * nj + j
        slot = step % 2
        dq_ref, dk_ref, dv_ref = dq_s.at[slot], dk_s.at[slot], dv_s.at[slot]

        @pl.when(j == 0)
        def _():
            dstate[...] = jnp.zeros_like(dstate)

        for c in reversed(range(cps)):
            rows = pl.ds(c * RET_BLOCK, RET_BLOCK)
            co, si = cos_ref[rows, :], sin_ref[rows, :]
            for h in range(RET_GROUP):
                hk, hv = slice(h * RET_DK, (h + 1) * RET_DK), slice(h * RET_DV, (h + 1) * RET_DV)
                q = _rope_half(q_ref[rows, hk].astype(F32), co, si)
                k = _rope_half(k_ref[rows, hk].astype(F32), co, si) * scale
                qb, kb = q.astype(BF16), k.astype(BF16)
                vb = v_ref[rows, hv].astype(BF16)
                dob = do_ref[rows, hv].astype(BF16)
                sb = s_ref[h, c]
                ia = intra_ref[h]
                pb = (_dot_nt(qb, kb) * ia).astype(BF16)
                dsn = dstate[h]
                dsb = dsn.astype(BF16)
                kdk = (k * kd_ref[h]).astype(BF16)
                qdq = (q * qd_ref[h]).astype(BF16)
                dv = _dot_tn(pb, dob) + _dot(kdk, dsb)
                dpb = (_dot_nt(dob, vb) * ia).astype(BF16)
                dq = _dot(dpb, kb) + _dot_nt(dob, sb) * qd_ref[h]
                dk = _dot_tn(dpb, qb) + _dot_nt(vb, dsb) * kd_ref[h]
                dstate[h] = dsn * cd_ref[h] + _dot_tn(qdq, dob)
                dq_ref[rows, hk] = _rope_half_bwd(dq, co, si).astype(BF16)
                dk_ref[rows, hk] = _rope_half_bwd(dk * scale, co, si).astype(BF16)
                dv_ref[rows, hv] = dv.astype(BF16)

        def copies(sl):
            r = pl.ds(pl.multiple_of((nj - 1 - j) * tb, tb), tb)
            cols = lambda first, w: pl.ds(pl.multiple_of(first + head * w, 128), w)
            return [pltpu.make_async_copy(dq_s.at[sl], out_ref.at[r, cols(0, RET_DK)], sems.at[sl, 0]),
                    pltpu.make_async_copy(dk_s.at[sl], out_ref.at[r, cols(k0, RET_DK)], sems.at[sl, 1]),
                    pltpu.make_async_copy(dv_s.at[sl], out_ref.at[r, cols(v0, RET_DV)], sems.at[sl, 2])]

        @pl.when(step > 0)
        def _():
            for cp in copies(1 - slot):
                cp.wait()

        for cp in copies(slot):
            cp.start()

        @pl.when(step == RET_HEADS * nj - 1)
        def _():
            for cp in copies(slot):
                cp.wait()

    return pl.pallas_call(
        body, name=name, grid=(RET_HEADS, nj),
        in_specs=[sp["q"], sp["k"], sp["v"], sp["tab"], sp["tab"], sp["intra"], sp["dec"], sp["dec"], sp["cd"],
                  sp["s"], sp["o"], pl.BlockSpec(memory_space=pl.ANY)],
        out_specs=pl.BlockSpec(memory_space=pl.ANY), out_shape=jax.ShapeDtypeStruct(dproj.shape, dproj.dtype),
        input_output_aliases={11: 0},
        scratch_shapes=[pltpu.VMEM((2, tb, RET_DK), BF16), pltpu.VMEM((2, tb, RET_DK), BF16),
                        pltpu.VMEM((2, tb, RET_DV), BF16), pltpu.SemaphoreType.DMA((2, 3)),
                        pltpu.VMEM((RET_GROUP, RET_DK, RET_DV), F32)],
        compiler_params=_cparams(("arbitrary", "arbitrary")),
    )(proj, proj, proj, cos, sin, intra, qd, kd, cd, states, dout, dproj)


def _ret_gate(out, proj, gn, name):
    def fn(o, g, *gains):
        g = g.astype(F32)
        parts = [_rms(o[:, h * RET_DV:(h + 1) * RET_DV], gains[h]) for h in range(RET_HEADS)]
        return (g * _sigmoid(g) * jnp.concatenate(parts, axis=-1),)
    w = RET_HEADS * RET_DV
    return _rows(fn, [out, (proj, w, 2)], [gn[h:h + 1] for h in range(RET_HEADS)], [(w, BF16)], name=name)[0]


def _ret_gate_bwd(out, proj, gn, dy, name):
    def fn(o, g, d, *gains):
        g = g.astype(F32)
        sg = _sigmoid(g)
        silu = g * sg
        dsilu = sg * (1.0 + g * (1.0 - sg))
        dos, dgs = [], []
        row = lax.broadcasted_iota(jnp.int32, (RET_HEADS, RET_DV), 0)
        dgn = jnp.zeros((RET_HEADS, RET_DV), F32)
        for h in range(RET_HEADS):
            sl = slice(h * RET_DV, (h + 1) * RET_DV)
            oh = o[:, sl]
            dgs.append(d[:, sl] * _rms(oh, gains[h]) * dsilu[:, sl])
            dx, dg = _rms_bwd(oh, d[:, sl] * silu[:, sl], gains[h])
            dos.append(dx)
            dgn = dgn + jnp.where(row == h, _colsum(dg), 0.0)
        return jnp.concatenate(dos, axis=-1), jnp.concatenate(dgs, axis=-1), dgn
    w = RET_HEADS * RET_DV
    return _rows(fn, [out, (proj, w, 2), dy], [gn[h:h + 1] for h in range(RET_HEADS)],
                 [(w, BF16), (w, BF16, proj.shape[1], 2)], [((RET_HEADS, RET_DV), F32)], name=name, tile=128)


def _mla_tables(T):
    ang = _rope_angles(T, MLA_ROPE)
    c, s = np.cos(ang), np.sin(ang)
    z32, z64 = np.zeros((T, 32), np.float32), np.zeros((T, 64), np.float32)
    cos_t = np.concatenate([c, c, z64], axis=1)
    sin_a = np.concatenate([-s, z32, z64], axis=1)
    sin_b = np.concatenate([z32, s, z64], axis=1)
    return tuple(jnp.asarray(t, F32) for t in (cos_t, sin_a, sin_b))


def _rope_blk(x, ct, sa, sb):
    return x * ct + pltpu.roll(x, 96, 1) * sa + pltpu.roll(x, 32, 1) * sb


def _rope_blk_bwd(d, ct, sa, sb):
    return d * ct + pltpu.roll(d * sa, 32, 1) + pltpu.roll(d * sb, 96, 1)


def _head_norm(x, gain):
    r = lax.rsqrt(_rowsum(x * x, True) / MLA_QKD + EPS)
    return (x * r) * gain


def _mla_prep(q, kv, proj, gq, gk, tabs, name):
    def fn(qv, kvv, kr, ct, sa, sb, gqv, gkv):
        qv, kvv = qv.astype(F32), kvv.astype(F32)
        qs, ks, vs = [], [], []
        for h in range(MLA_HEADS):
            b = h * MLA_HP
            y = _head_norm(qv[:, b:b + MLA_HP], gqv)
            qs += [y[:, :128], _rope_blk(y[:, 128:], ct, sa, sb)]
            y = _head_norm(jnp.concatenate([kvv[:, b:b + 128], kr], axis=-1), gkv)
            ks += [y[:, :128], _rope_blk(y[:, 128:], ct, sa, sb)]
            vs.append(kvv[:, b + 128:b + 256])
        return jnp.concatenate(qs, axis=-1), jnp.concatenate(ks, axis=-1), jnp.concatenate(vs, axis=-1)
    w = MLA_HEADS * MLA_HP
    return _rows(fn, [q, kv, (proj, 128, 5), *tabs], [gq, gk],
                 [(w, BF16), (w, BF16), (MLA_HEADS * MLA_VD, BF16)], name=name, tile=128)


def _mla_prep_bwd(q, kv, proj, gq, gk, tabs, dqf, dkf, dvf, name):
    def fn(qv, kvv, kr, ct, sa, sb, dqv, dkv, dvv, gqv, gkv):
        qv, kvv, dqv, dkv = (t.astype(F32) for t in (qv, kvv, dqv, dkv))
        dqs, dkvs = [], []
        dkr = jnp.zeros_like(kr)
        dgq = jnp.zeros((1, MLA_HP), F32)
        dgk = jnp.zeros((1, MLA_HP), F32)
        for h in range(MLA_HEADS):
            b = h * MLA_HP
            dy = jnp.concatenate([dqv[:, b:b + 128], _rope_blk_bwd(dqv[:, b + 128:b + 256], ct, sa, sb)], axis=-1)
            dx, dg = _rms_bwd(qv[:, b:b + MLA_HP], dy, gqv, MLA_QKD, mxu=True)
            dqs.append(dx)
            dgq = dgq + _colsum(dg)
            dy = jnp.concatenate([dkv[:, b:b + 128], _rope_blk_bwd(dkv[:, b + 128:b + 256], ct, sa, sb)], axis=-1)
            dx, dg = _rms_bwd(jnp.concatenate([kvv[:, b:b + 128], kr], axis=-1), dy, gkv, MLA_QKD, mxu=True)
            dkvs += [dx[:, :128], dvv[:, h * MLA_VD:(h + 1) * MLA_VD]]
            dkr = dkr + dx[:, 128:]
            dgk = dgk + _colsum(dg)
        return jnp.concatenate(dqs, axis=-1), jnp.concatenate(dkvs, axis=-1), dkr, dgq, dgk
    w = MLA_HEADS * MLA_HP
    return _rows(fn, [q, kv, (proj, 128, 5), *tabs, dqf, dkf, dvf], [gq, gk],
                 [(w, BF16), (w, BF16), (128, F32)], [((1, MLA_HP), F32), ((1, MLA_HP), F32)], name=name, tile=128)


def _chunk_mask(qi, ki, tq, tk):
    shift = CHUNK.bit_length() - 1
    rq = lax.shift_right_arithmetic(qi * tq + lax.broadcasted_iota(jnp.int32, (tq, tk), 0), shift)
    ck = lax.shift_right_arithmetic(ki * tk + lax.broadcasted_iota(jnp.int32, (tq, tk), 1), shift)
    return ck <= rq


def _flash_fwd(qf, kf, vf, name):
    T = qf.shape[0]
    t = _pick(T, FLASH_T)
    n = T // t
    scale = MLA_QKD ** -0.5

    g = FLASH_HEADS

    def body(q_ref, k_ref, v_ref, o_ref, lse_ref, m_s, l_s, acc):
        qi = pl.program_id(1)
        m_s[...] = jnp.full_like(m_s, NEG)
        l_s[...] = jnp.zeros_like(l_s)
        acc[...] = jnp.zeros_like(acc)

        def step(kb, masked):
            rows = pl.ds(pl.multiple_of(kb * t, t), t)
            for h in range(g):
                hq, hv = slice(h * MLA_HP, (h + 1) * MLA_HP), slice(h * MLA_VD, (h + 1) * MLA_VD)
                s = _dot_nt(q_ref[:, hq], k_ref[rows, hq])
                if masked:
                    s = jnp.where(_chunk_mask(0, 0, t, t), s, NEG)
                m_prev = m_s[:, hv]
                m_new = jnp.maximum(m_prev, jnp.max(s, axis=-1, keepdims=True))
                alpha = jnp.exp2(m_prev - m_new)
                p = jnp.exp2(s - _widen(m_new, t))
                l_s[:, hv] = alpha * l_s[:, hv] + sum(p[:, i * 128:(i + 1) * 128] for i in range(t // 128))
                acc[:, hv] = acc[:, hv] * alpha + _dot(p.astype(BF16), v_ref[rows, hv])
                m_s[:, hv] = m_new

        @pl.loop(0, qi)
        def _(kb):
            step(kb, False)

        step(qi, True)
        for h in range(g):
            hv = slice(h * MLA_VD, (h + 1) * MLA_VD)
            l = jnp.sum(l_s[:, hv], axis=-1, keepdims=True)
            o_ref[:, hv] = acc[:, hv] / l
            lse_ref[:, hv] = m_s[:, hv] + jnp.log2(l)

    qmap = lambda h, i: (i, h)
    kmap = lambda h, i: (0, h)
    vec = pltpu.VMEM((t, g * MLA_VD), F32)
    return pl.pallas_call(
        body, name=name, grid=(MLA_HEADS // g, n),
        in_specs=[pl.BlockSpec((t, g * MLA_HP), qmap), pl.BlockSpec((T, g * MLA_HP), kmap),
                  pl.BlockSpec((T, g * MLA_VD), kmap)],
        out_specs=[pl.BlockSpec((t, g * MLA_VD), qmap), pl.BlockSpec((t, g * MLA_VD), qmap)],
        out_shape=[jax.ShapeDtypeStruct((T, MLA_HEADS * MLA_VD), F32),
                   jax.ShapeDtypeStruct((T, MLA_HEADS * MLA_VD), F32)],
        scratch_shapes=[vec, vec, vec],
        compiler_params=_cparams(("parallel", "arbitrary")),
    )(qf, kf, vf)


def _flash_bwd(qf, kf, vf, do16, lse, delta, name):
    T = qf.shape[0]
    t = _pick(T, FLASH_T)
    n = T // t
    scale = MLA_QKD ** -0.5

    def body(q_ref, k_ref, v_ref, do_ref, lse_ref, dl_ref, dq_out, dk_out, dv_out, dq_ref, dk_ref, dv_ref):
        kb = pl.program_id(1)

        @pl.when(kb == 0)
        def _():
            dq_ref[...] = jnp.zeros_like(dq_ref)

        dk_ref[...] = jnp.zeros_like(dk_ref)
        dv_ref[...] = jnp.zeros_like(dv_ref)
        k, v = k_ref[...], v_ref[...]

        def step(qb, masked):
            rows = pl.ds(pl.multiple_of(qb * t, t), t)
            q, dob = q_ref[rows, :], do_ref[rows, :]
            s = _dot_nt(q, k)
            if masked:
                s = jnp.where(_chunk_mask(0, 0, t, t), s, NEG)
            p = jnp.exp2(s - _widen(lse_ref[rows, :], t))
            ds = (p * (_dot_nt(dob, v) - _widen(dl_ref[rows, :], t))).astype(BF16)
            dv_ref[...] += _dot_tn(p.astype(BF16), dob)
            dk_ref[...] += _dot_tn(ds, q)
            dq_ref[rows, :] += _dot(ds, k)

        step(kb, True)

        @pl.loop(kb + 1, n)
        def _(qb):
            step(qb, False)

        dk_out[...] = (dk_ref[...] * (1.0 / LOG2E)).astype(BF16)
        dv_out[...] = dv_ref[...].astype(BF16)

        @pl.when(kb == n - 1)
        def _():
            dq_out[...] = (dq_ref[...] * scale).astype(BF16)

    qmap = lambda h, j: (0, h)
    kmap = lambda h, j: (j, h)
    return pl.pallas_call(
        body, name=name, grid=(MLA_HEADS, n),
        in_specs=[pl.BlockSpec((T, MLA_HP), qmap), pl.BlockSpec((t, MLA_HP), kmap), pl.BlockSpec((t, MLA_VD), kmap),
                  pl.BlockSpec((T, MLA_VD), qmap), pl.BlockSpec((T, MLA_VD), qmap), pl.BlockSpec((T, MLA_VD), qmap)],
        out_specs=[pl.BlockSpec((T, MLA_HP), qmap), pl.BlockSpec((t, MLA_HP), kmap), pl.BlockSpec((t, MLA_VD), kmap)],
        out_shape=[jax.ShapeDtypeStruct((T, MLA_HEADS * MLA_HP), BF16),
                   jax.ShapeDtypeStruct((T, MLA_HEADS * MLA_HP), BF16),
                   jax.ShapeDtypeStruct((T, MLA_HEADS * MLA_VD), BF16)],
        scratch_shapes=[pltpu.VMEM((T, MLA_HP), F32), pltpu.VMEM((t, MLA_HP), F32), pltpu.VMEM((t, MLA_VD), F32)],
        compiler_params=_cparams(("arbitrary", "arbitrary")),
    )(qf, kf, vf, do16, lse, delta)


MESH = pl.DeviceIdType.MESH
ANY = pl.BlockSpec(memory_space=pl.ANY)
_CHIP_FLIPS = ((1, 0), (0, 1), (1, 1))


def _place():
    return lax.axis_index("x"), lax.axis_index("y"), lax.axis_index("c")


def _other_chip(x, y, k):
    fx, fy = _CHIP_FLIPS[k]
    return ((1 - x) if fx else x), ((1 - y) if fy else y)


def _remote(src, dst, send_sems, recv_sems, k, to):
    return pltpu.make_async_remote_copy(src_ref=src, dst_ref=dst, send_sem=send_sems.at[k], recv_sem=recv_sems.at[k],
                                        device_id=to, device_id_type=MESH)


def _index(*vals):
    return jnp.stack(vals).astype(jnp.int32)


def _half(c, rows):
    return pl.ds(pl.multiple_of(c * rows, 16), rows)


def _gather_weights(parts, name, landed=None):
    n_w = len(parts)
    n_in = n_w if landed is None else 2 * n_w

    def body(*refs):
        ins, outs = refs[:n_w], refs[n_in:n_in + n_w]
        send_sems, recv_sems, local_sems = refs[n_in + n_w:]
        x, y, c = _place()
        j = 2 * x + y
        sibling = (x, y, 1 - c)
        chips = [_other_chip(x, y, k) for k in range(3)]
        pending = []
        for w in range(n_w):
            own = pltpu.make_async_copy(ins[w], outs[w].at[j], local_sems.at[w])
            own.start()
            pending.append(own)
        sent = []
        for w in range(n_w):
            if landed is not None:
                break
            r = _half(c, parts[w].shape[0] // 2)
            for k, (px, py) in enumerate(chips):
                cp = _remote(ins[w].at[r], outs[w].at[j, r], send_sems, recv_sems, 6 * w + k, (px, py, c))
                cp.start()
                sent.append(cp)
        for w in range(n_w):
            r = _half(c, parts[w].shape[0] // 2)
            for k, (px, py) in enumerate(chips):
                blk = outs[w].at[2 * px + py, r]
                if landed is None:
                    _remote(blk, blk, send_sems, recv_sems, 6 * w + k, (px, py, c)).wait_recv()
                cp = _remote(blk, blk, send_sems, recv_sems, 6 * w + 3 + k, sibling)
                cp.start()
                sent.append(cp)
        for w in range(n_w):
            r = _half(1 - c, parts[w].shape[0] // 2)
            for k, (px, py) in enumerate(chips):
                blk = outs[w].at[2 * px + py, r]
                _remote(blk, blk, send_sems, recv_sems, 6 * w + 3 + k, sibling).wait_recv()
        for cp in sent:
            cp.wait_send()
        for cp in pending:
            cp.wait()

    return pl.pallas_call(
        body, name=name, in_specs=[pl.BlockSpec(memory_space=pltpu.VMEM)] * n_w + [ANY] * (n_in - n_w),
        out_specs=[ANY] * n_w,
        out_shape=[jax.ShapeDtypeStruct((N_CHIPS, *p.shape), p.dtype) for p in parts],
        input_output_aliases={} if landed is None else {n_w + w: w for w in range(n_w)},
        scratch_shapes=[pltpu.SemaphoreType.DMA((6 * n_w,)), pltpu.SemaphoreType.DMA((6 * n_w,)),
                        pltpu.SemaphoreType.DMA((n_w,))],
        compiler_params=pltpu.CompilerParams(vmem_limit_bytes=VMEM_LIMIT),
    )(*parts, *(landed or []))


def _swap_halves(gs, name):
    n_w = len(gs)

    def body(*refs):
        g_refs, recv_refs = refs[:n_w], refs[n_w:2 * n_w]
        send_sems, recv_sems = refs[2 * n_w:]
        x, y, c = _place()
        sent = []
        for w in range(n_w):
            for jj in range(N_CHIPS):
                cp = _remote(g_refs[w].at[jj, 1 - c], recv_refs[w].at[jj], send_sems, recv_sems, N_CHIPS * w + jj,
                             (x, y, 1 - c))
                cp.start()
                sent.append(cp)
        for cp in sent:
            cp.wait()

    return pl.pallas_call(
        body, name=name, in_specs=[ANY] * n_w, out_specs=[ANY] * n_w,
        out_shape=[jax.ShapeDtypeStruct((N_CHIPS, *g.shape[2:]), g.dtype) for g in gs],
        scratch_shapes=[pltpu.SemaphoreType.DMA((N_CHIPS * n_w,)), pltpu.SemaphoreType.DMA((N_CHIPS * n_w,))],
    )(*gs)


def _pair_sum(g, recv, core, name):
    _, H, C = recv.shape
    tile = _pick(H, 256)

    def body(c_ref, own_ref, recv_ref, out_ref):
        out_ref[...] = (own_ref[...].astype(F32) + recv_ref[...].astype(F32)).astype(BF16)

    blk = pl.BlockSpec((None, tile, C), lambda jj, i, c: (jj, i, 0))
    return pl.pallas_call(
        body, name=name,
        grid_spec=pltpu.PrefetchScalarGridSpec(
            num_scalar_prefetch=1, grid=(N_CHIPS, H // tile),
            in_specs=[pl.BlockSpec((None, None, tile, C), lambda jj, i, c: (jj, c[0], i, 0)), blk],
            out_specs=blk),
        out_shape=jax.ShapeDtypeStruct((N_CHIPS, H, C), BF16),
        compiler_params=_cparams(("arbitrary", "arbitrary")),
    )(_index(core), g, recv)


def _chip_sum(g, recv, got, chip, core, name):
    _, H, C = recv.shape
    tile = _pick(H, 256)

    def body(s_ref, own_ref, recv_ref, g0_ref, g1_ref, g2_ref, out_ref):
        pair = own_ref[...].astype(F32) + recv_ref[...].astype(F32)
        out_ref[...] = ((pair + g0_ref[...].astype(F32)) + g1_ref[...].astype(F32)) + g2_ref[...].astype(F32)

    def got_spec(k):
        return pl.BlockSpec((None, tile, C), lambda i, s, k=k: (k, i, 0))

    return pl.pallas_call(
        body, name=name,
        grid_spec=pltpu.PrefetchScalarGridSpec(
            num_scalar_prefetch=1, grid=(H // tile,),
            in_specs=[pl.BlockSpec((None, None, tile, C), lambda i, s: (s[0], s[1], i, 0)),
                      pl.BlockSpec((None, tile, C), lambda i, s: (s[0], i, 0)), got_spec(0), got_spec(1), got_spec(2)],
            out_specs=pl.BlockSpec((None, tile, C), lambda i, s: (s[1], i, 0))),
        out_shape=jax.ShapeDtypeStruct((2, H, C), F32),
        compiler_params=_cparams(("arbitrary",)),
    )(_index(chip, core), g, recv, got, got, got)


def _share_halves(reds):
    n_w = len(reds)

    def body(*refs):
        out_refs = refs[n_w:2 * n_w]
        send_sems, recv_sems = refs[2 * n_w:]
        x, y, c = _place()
        sent = []
        for w in range(n_w):
            blk = out_refs[w].at[c]
            cp = _remote(blk, blk, send_sems, recv_sems, w, (x, y, 1 - c))
            cp.start()
            sent.append(cp)
        for cp in sent:
            cp.wait()

    return pl.pallas_call(
        body, name="grad_share_halves", in_specs=[ANY] * n_w, out_specs=[ANY] * n_w,
        out_shape=[jax.ShapeDtypeStruct(r.shape, r.dtype) for r in reds],
        input_output_aliases={w: w for w in range(n_w)},
        scratch_shapes=[pltpu.SemaphoreType.DMA((n_w,)), pltpu.SemaphoreType.DMA((n_w,))],
    )(*reds)


def _allsum_small(v, name):
    R, W = v.shape
    n_dev = 8
    vm = pl.BlockSpec(memory_space=pltpu.VMEM)

    def body(v_ref, out_ref, buf, send_sems, recv_sems):
        x, y, c = _place()
        me = 4 * x + 2 * y + c
        buf[me] = v_ref[...]
        sent = []
        for k in range(1, n_dev):
            peer = ((1 - x) if k & 4 else x, (1 - y) if k & 2 else y, (1 - c) if k & 1 else c)
            cp = _remote(v_ref, buf.at[me], send_sems, recv_sems, k - 1, peer)
            cp.start()
            sent.append(cp)
        for cp in sent:
            cp.wait_recv()
        for cp in sent:
            cp.wait_send()
        acc = buf[0]
        for q in range(1, n_dev):
            acc = acc + buf[q]
        out_ref[...] = acc

    return pl.pallas_call(
        body, name=name, in_specs=[vm], out_specs=vm, out_shape=jax.ShapeDtypeStruct((R, W), v.dtype),
        scratch_shapes=[pltpu.VMEM((n_dev, R, W), v.dtype), pltpu.SemaphoreType.DMA((n_dev - 1,)),
                        pltpu.SemaphoreType.DMA((n_dev - 1,))],
    )(v)


HBM = pl.BlockSpec(memory_space=pltpu.HBM)
SEM = pl.BlockSpec(memory_space=pltpu.SEMAPHORE)
_DATAFLOW = pltpu.SideEffectType.DATAFLOW_SIDE_EFFECTING


def _split_start(name, srcs, land_shapes, n_copies, copies, after=()):
    ns, nl = len(srcs), len(land_shapes)
    lands = [lax.empty(s.shape, s.dtype) for s in land_shapes]

    def body(*refs):
        outs = refs[ns + nl + len(after):]
        for cp in copies(refs[:ns], refs[ns:ns + nl], outs[0], outs[1]):
            cp.start()
        outs[-1][...] = jnp.zeros_like(outs[-1])

    sems = pltpu.SemaphoreType.DMA((n_copies,))
    res = pl.pallas_call(
        body, name=name, in_specs=[HBM] * (ns + nl) + [ANY] * len(after),
        out_specs=(SEM, SEM, *[HBM] * (ns + nl), pl.BlockSpec(memory_space=pltpu.VMEM)),
        out_shape=(sems, sems, *[pltpu.HBM(a.shape, a.dtype) for a in srcs],
                   *[pltpu.HBM(s.shape, s.dtype) for s in land_shapes], jax.ShapeDtypeStruct((8, 128), F32)),
        input_output_aliases={i: 2 + i for i in range(ns + nl)},
        compiler_params=pltpu.CompilerParams(has_side_effects=_DATAFLOW),
    )(*[pltpu.with_memory_space_constraint(a, pltpu.HBM) for a in [*srcs, *lands]], *after)
    return res[0], res[1], list(res[2:2 + ns]), list(res[2 + ns:2 + ns + nl]), res[-1]


def _split_wait(name, send_sems, recv_sems, srcs, lands, copies, after=()):
    ns, nl = len(srcs), len(lands)

    def body(*refs):
        for cp in copies(refs[:ns], refs[ns:ns + nl], refs[ns + nl], refs[ns + nl + 1]):
            cp.wait_send()
            cp.wait_recv()

    res = pl.pallas_call(
        body, name=name, in_specs=[HBM] * (ns + nl) + [SEM, SEM] + [ANY] * len(after), out_specs=[HBM] * (ns + nl),
        out_shape=[pltpu.HBM(a.shape, a.dtype) for a in [*srcs, *lands]],
        input_output_aliases={i: i for i in range(ns + nl)},
        compiler_params=pltpu.CompilerParams(has_side_effects=_DATAFLOW),
    )(*srcs, *lands, send_sems, recv_sems, *after)
    return list(res[ns:])


def _gather_copies(rows):
    def copies(src_refs, land_refs, send_sems, recv_sems):
        x, y, c = _place()
        j = 2 * x + y
        out = []
        for w in range(len(src_refs)):
            r = _half(c, rows[w] // 2)
            for k in range(3):
                px, py = _other_chip(x, y, k)
                out.append(_remote(src_refs[w].at[r], land_refs[w].at[j, r], send_sems, recv_sems, 3 * w + k, (px, py, c)))
        return out
    return copies


def _scatter_copies(src_refs, land_refs, send_sems, recv_sems):
    x, y, c = _place()
    j = 2 * x + y
    out = []
    for w in range(len(src_refs)):
        for k in range(3):
            px, py = _other_chip(x, y, k)
            pj = 2 * px + py
            out.append(_remote(src_refs[w].at[pj], land_refs[w].at[(j - pj + 4) % 4 - 1], send_sems, recv_sems, 3 * w + k,
                               (px, py, c)))
    return out


def _reduce_begin(grads, core, tag):
    names = list(grads)
    gs = [grads[k].reshape(N_CHIPS, 2, -1, grads[k].shape[-1]) for k in names]
    recvs = _swap_halves(gs, f"grad_swap_halves_{tag}")
    sums = [_pair_sum(g, r, core, f"pair_sum_{k}") for k, g, r in zip(names, gs, recvs)]
    return names, gs, recvs, sums


def _reduce_end(begun, gots, chip, core):
    names, gs, recvs, _ = begun
    return {k: _chip_sum(g, r, t, chip, core, f"chip_sum_{k}") for k, g, r, t in zip(names, gs, recvs, gots)}


def _got_shapes(sums):
    return [jax.ShapeDtypeStruct((3, *a.shape[1:]), a.dtype) for a in sums]


def _adamw(w, g, m, v, name, layers=1, layer=0, into=None):
    shape = w.shape
    cols = shape[-1]
    w3, m3, v3 = (t.reshape(layers, -1, cols) for t in (w, m, v))
    rows = w3.shape[1]
    tile = _pick(rows, 256) if rows % 8 == 0 else rows
    n_in = 4 + (0 if into is None else 4)
    stack_g = layers > 1

    def body(*refs):
        wv, gv, mv, vv = (r[...] for r in refs[:4])
        d_ref, m_ref, v_ref = refs[len(refs) - 3:]
        m2 = ADAM_B1 * mv + (1.0 - ADAM_B1) * gv
        v2 = ADAM_B2 * vv + (1.0 - ADAM_B2) * jnp.square(gv)
        m_hat = m2 / (1.0 - ADAM_B1 ** ADAM_STEP)
        v_hat = v2 / (1.0 - ADAM_B2 ** ADAM_STEP)
        if stack_g:
            refs[n_in][...] = gv
        d_ref[...] = -ADAM_LR * (m_hat / (jnp.sqrt(v_hat) + ADAM_EPS) + ADAM_WD * wv)
        m_ref[...] = m2
        v_ref[...] = v2

    n_out = 4 if stack_g else 3
    lay = pl.BlockSpec((None, tile, cols), lambda i: (layer, i, 0))
    out = jax.ShapeDtypeStruct((layers, rows, cols), F32)
    res = pl.pallas_call(
        body, name=name, grid=(rows // tile,),
        in_specs=[lay, pl.BlockSpec((tile, cols), lambda i: (i, 0)), lay, lay] + [ANY] * (n_in - 4),
        out_specs=[lay] * n_out, out_shape=[out] * n_out,
        input_output_aliases={} if into is None else {4 + k: k for k in range(4)},
        compiler_params=_cparams(("arbitrary",)),
    )(w3, g.reshape(rows, cols), m3, v3, *([] if into is None else [t.reshape(layers, rows, cols) for t in into]))
    res = tuple(t.reshape(shape) for t in res)
    return res if stack_g else (g.reshape(shape), *res)


ROW_F32, ROW_BF16 = (D_MODEL, F32), (D_MODEL, BF16)


def _res_norm(acc, h, gain):
    hh = h + acc
    return hh, _rms(hh, gain)


def _dx_norm_bwd(d, w, h, dres, gain, name, **kw):
    def epilogue(acc, hv, dr, g):
        dx, dg = _rms_bwd(hv, acc, g)
        return dr + dx, dr + dx, _colsum(dg)
    return _mm_rows(d, w, tb=True, extras=[h, dres], fulls=[gain], outs=[ROW_F32, ROW_BF16], accs=[((1, D_MODEL), F32)],
                    epilogue=epilogue, name=name, **kw)


def _tail_fwd(h1, hn2, p16, W, i, tag, next_gain=None, target=None):
    a = _mm(hn2, W["mlp_w1"][i], bblk=True, outs=[BF16], name=f"{tag}_mlp_w1",
            epilogue=lambda acc: (jnp.square(jnp.maximum(acc, 0.0)),))
    h2, hn3 = _mm_rows(a, W["mlp_w2"][i], extras=[h1], fulls=[W["ple_norm"][i:i + 1]], outs=[ROW_F32, ROW_BF16],
                       epilogue=_res_norm, name=f"{tag}_mlp_w2")
    def embed(acc, pv, h, wp):
        gate = _sigmoid(acc)
        ppv = jnp.concatenate([_dot(pv, wp[s]) for s in range(N_CHIPS)], axis=-1)
        return gate, ppv, h + gate * ppv

    if target is None:
        def gated(acc, pv, h, wp, gain):
            gate, ppv, hh = embed(acc, pv, h, wp)
            return hh, ppv, gate, _rms(hh, gain)
        h3, pp, gate, hn = _mm_rows(hn3, W["ple_gate_w"][i], extras=[p16[i], h2], fulls=[W["ple_proj_w"][i], next_gain],
                                    outs=[ROW_F32, ROW_BF16, ROW_BF16, ROW_BF16], epilogue=gated, name=f"{tag}_ple")
        return h3, hn, (h1, hn2, a, h2, hn3, gate, pp)

    def gated_loss(acc, pv, h, t, wp):
        gate, ppv, hh = embed(acc, pv, h, wp)
        e = hh - t
        return ppv, gate, e * (1.0 / D_MODEL), jnp.full((1, 128), 0.5 / D_MODEL, F32) * jnp.sum(e * e)
    pp, gate, dy, loss = _mm_rows(hn3, W["ple_gate_w"][i], extras=[p16[i], h2, target], fulls=[W["ple_proj_w"][i]],
                                  outs=[ROW_BF16, ROW_BF16, ROW_F32], accs=[((1, 128), F32)], epilogue=gated_loss,
                                  name=f"{tag}_ple")
    return dy, loss, (h1, hn2, a, h2, hn3, gate, pp)


def _tail_bwd(dh3, saved, p16, W, i, tag, after=()):
    h1, hn2, a, h2, hn3, gate, pp = saved

    def gate_bwd(d, g, ppv):
        g, ppv = g.astype(F32), ppv.astype(F32)
        return d * g, d * ppv * g * (1.0 - g)

    def dw(kind, name):
        return (kind, 1, 0, None)

    dpp, dgl = _rows(gate_bwd, [dh3, gate, pp], [], [(D_MODEL, BF16), (D_MODEL, BF16)], name=f"{tag}_ple_gate_bwd",
                     after=after)
    d_proj = _mm(p16[i], dpp, ta=True, outs=[BF16], dw=dw("cols", "ple_proj_w"), name=f"{tag}_d_ple_proj")
    d_gate = _mm(hn3, dgl, ta=True, outs=[BF16], dw=dw("rows", "ple_gate_w"), name=f"{tag}_d_ple_gate")
    dh2, dh2_16, d_ple_norm = _dx_norm_bwd(dgl, W["ple_gate_w"][i], h2, dh3, W["ple_norm"][i:i + 1],
                                           f"{tag}_ple_gate_dx")
    d_w2 = _mm(a, dh2_16, ta=True, outs=[BF16], dw=dw("rows", "mlp_w2"), name=f"{tag}_d_mlp_w2")
    dz = _mm(dh2_16, W["mlp_w2"][i], tb=True, extras=[a], outs=[BF16], name=f"{tag}_mlp_w2_dx",
             epilogue=lambda acc, av: (acc * (2.0 * jnp.sqrt(av.astype(F32))),))
    d_w1 = _mm(hn2, dz, ta=True, outs=[BF16], dw=dw("cols", "mlp_w1"), name=f"{tag}_d_mlp_w1")
    dh1, dh1_16, d_mlp_norm = _dx_norm_bwd(dz, W["mlp_w1"][i], h1, dh2, W["mlp_norm"][i:i + 1], f"{tag}_mlp_w1_dx",
                                           bblk=True)
    big = {f"mlp_w1_{i}": d_w1, f"mlp_w2_{i}": d_w2, f"ple_gate_w_{i}": d_gate, f"ple_proj_w_{i}": d_proj}
    return dh1, dh1_16, big, dict(mlp_norm=d_mlp_norm, ple_norm=d_ple_norm)


def _ret_layer_fwd(h0, W, tabs, after=()):
    hn = _rows(lambda x, g: (_rms(x, g),), [h0], [W["mix_norm"][0:1]], [(D_MODEL, BF16)], name="ret_mix_norm",
               after=after)[0]
    proj = _mm(hn, W["ret_w_in"], bblk=True, outs=[BF16], name="ret_w_in")
    out, states = _ret_fwd(proj, tabs, "ret_scan")
    y = _ret_gate(out, proj, W["ret_gn"], "ret_gate")
    h1, hn2 = _mm_rows(y, W["ret_w_out"], extras=[h0], fulls=[W["mlp_norm"][0:1]], outs=[ROW_F32, ROW_BF16],
                       epilogue=_res_norm, name="ret_w_out")
    return h1, hn2, (h0, hn, proj, out, states, y)


def _d_ret_w_out(dh1_16, saved):
    return _mm(saved[5], dh1_16, ta=True, outs=[BF16], dw=("rows", 1, 0, None), name="d_ret_w_out")


def _ret_layer_bwd(dh1, dh1_16, saved, W, tabs, after=(), on_grads=None, d_w_out=None):
    h0, hn, proj, out, states, y = saved
    d_w_out = _d_ret_w_out(dh1_16, saved) if d_w_out is None else d_w_out
    dy = _mm(dh1_16, W["ret_w_out"], tb=True, name="ret_w_out_dx", after=after)
    dout, dproj, d_gn = _ret_gate_bwd(out, proj, W["ret_gn"], dy, "ret_gate_bwd")
    dproj = _ret_bwd(proj, states, dout, dproj, tabs, "ret_scan_bwd")
    d_w_in = _mm(hn, dproj, ta=True, outs=[BF16], dw=("cols", 1, 0, None), name="d_ret_w_in")
    big = dict(ret_w_in=d_w_in, ret_w_out=d_w_out)
    later = () if on_grads is None else on_grads(big)
    dh0, _, d_mix = _dx_norm_bwd(dproj, W["ret_w_in"], h0, dh1, W["mix_norm"][0:1], "ret_w_in_dx", bblk=True, tm=256,
                                 after=later)
    return dh0, big, dict(mix_norm=d_mix, ret_gn=d_gn)


def _mla_layer_fwd(h0, hn, W, tabs):
    proj = _mm(hn, W["mla_w_in"], name="mla_w_in")

    def low_rank_norm(pv, gq, gkv):
        return _rms(pv[:, :MLA_Q_RANK], gq), _rms(pv[:, MLA_Q_RANK:MLA_Q_RANK + MLA_KV_RANK], gkv)

    cqn, ckvn = _rows(low_rank_norm, [proj], [W["mla_q_a_norm"], W["mla_kv_a_norm"]],
                      [(MLA_Q_RANK, BF16), (MLA_KV_RANK, BF16)], name="mla_low_rank_norm")
    q = _mm(cqn, W["mla_w_uq"], bblk=True, outs=[BF16], name="mla_w_uq")
    kv = _mm(ckvn, W["mla_w_ukv"], bblk=True, outs=[BF16], name="mla_w_ukv")
    qf, kf, vf = _mla_prep(q, kv, proj, W["mla_q_norm"] * (MLA_QKD ** -0.5 * LOG2E), W["mla_k_norm"], tabs, "mla_prep")
    o, lse = _flash_fwd(qf, kf, vf, "mla_flash")
    h1, hn2 = _mm_rows(o, W["mla_w_out"], extras=[h0], fulls=[W["mlp_norm"][1:2]], outs=[ROW_F32, ROW_BF16],
                       epilogue=_res_norm, name="mla_w_out")
    return h1, hn2, (h0, hn, proj, cqn, ckvn, q, kv, qf, kf, vf, o, lse)


def _mla_layer_bwd(dh1, dh1_16, saved, W, tabs):
    h0, hn, proj, cqn, ckvn, q, kv, qf, kf, vf, o, lse = saved
    d_w_out = _mm(o, dh1_16, ta=True, outs=[BF16], dw=("rows", 1, 0, None), name="d_mla_w_out")
    def with_delta(acc, ov):
        parts = []
        for h in range(MLA_HEADS):
            sl = slice(h * MLA_VD, (h + 1) * MLA_VD)
            d = jnp.sum(acc[:, sl] * ov[:, sl], axis=-1, keepdims=True)
            parts.append(jnp.broadcast_to(d, (d.shape[0], MLA_VD)))
        return jnp.concatenate(parts, axis=-1), acc

    delta, do16 = _mm_rows(dh1_16, W["mla_w_out"], tb=True, extras=[o], outs=[ROW_F32, ROW_BF16], epilogue=with_delta,
                           name="mla_w_out_dx")
    dqf, dkf, dvf = _flash_bwd(qf, kf, vf, do16, lse, delta, "mla_flash_bwd")
    dq, dkv, dkr, d_gq, d_gk = _mla_prep_bwd(q, kv, proj, W["mla_q_norm"], W["mla_k_norm"], tabs, dqf, dkf, dvf,
                                             "mla_prep_bwd")
    d_w_uq = _mm(cqn, dq, ta=True, outs=[BF16], dw=("cols", 1, 0, None), name="d_mla_w_uq")
    dcqn = _mm(dq, W["mla_w_uq"], tb=True, bblk=True, name="mla_w_uq_dx")
    d_w_ukv = _mm(ckvn, dkv, ta=True, outs=[BF16], dw=("cols", 1, 0, None), name="d_mla_w_ukv")
    dckvn = _mm(dkv, W["mla_w_ukv"], tb=True, bblk=True, name="mla_w_ukv_dx")

    def low_rank_bwd(pv, dcq, dckv, dkr_v, gq, gkv):
        dxq, dgq = _rms_bwd(pv[:, :MLA_Q_RANK], dcq, gq)
        dxkv, dgkv = _rms_bwd(pv[:, MLA_Q_RANK:MLA_Q_RANK + MLA_KV_RANK], dckv, gkv)
        return jnp.concatenate([dxq, dxkv, dkr_v], axis=-1), _colsum(dgq), _colsum(dgkv)

    dproj, d_gqa, d_gkva = _rows(low_rank_bwd, [proj, dcqn, dckvn, dkr], [W["mla_q_a_norm"], W["mla_kv_a_norm"]],
                                 [(MLA_IN_PAD, BF16)], [((1, MLA_Q_RANK), F32), ((1, MLA_KV_RANK), F32)],
                                 name="mla_low_rank_norm_bwd")
    d_w_in = _mm(hn, dproj, ta=True, outs=[BF16], dw=("rows", 1, 0, None), name="d_mla_w_in")
    dh0, dh0_16, d_mix = _dx_norm_bwd(dproj, W["mla_w_in"], h0, dh1, W["mix_norm"][1:2], "mla_w_in_dx")
    return (dh0, dh0_16, dict(mla_w_in=d_w_in, mla_w_uq=d_w_uq, mla_w_ukv=d_w_ukv, mla_w_out=d_w_out),
            dict(mix_norm=d_mix, mla_q_a_norm=d_gqa, mla_kv_a_norm=d_gkva, mla_q_norm=d_gq, mla_k_norm=d_gk))


def _local_step(x, p16, target, W):
    T = x.shape[0]
    ret_tabs, mla_tabs = _ret_tables(T), _mla_tables(T)
    h1, hn, s_ret = _ret_layer_fwd(x, W, ret_tabs)
    h3, hn, s_tail0 = _tail_fwd(h1, hn, p16, W, 0, "l0", next_gain=W["mix_norm"][1:2])
    h4, hn, s_mla = _mla_layer_fwd(h3, hn, W, mla_tabs)
    dy, loss, s_tail1 = _tail_fwd(h4, hn, p16, W, 1, "l1", target=target)
    dh4, dh4_16, g_t1, n_t1 = _tail_bwd(dy, s_tail1, p16, W, 1, "l1")
    dh3, _, g_mla, n_mla = _mla_layer_bwd(dh4, dh4_16, s_mla, W, mla_tabs)
    dh1, dh1_16, g_t0, n_t0 = _tail_bwd(dh3, s_tail0, p16, W, 0, "l0")
    dx, g_ret, n_ret = _ret_layer_bwd(dh1, dh1_16, s_ret, W, ret_tabs)
    return loss, dx, {**g_ret, **g_t0, **g_mla, **g_t1}, _small_grads(n_ret, n_t0, n_mla, n_t1)


def _small_grads(n_ret, n_t0, n_mla, n_t1):
    return dict(
        mix_norm=jnp.concatenate([n_ret["mix_norm"], n_mla["mix_norm"]], axis=0),
        mlp_norm=jnp.concatenate([n_t0["mlp_norm"], n_t1["mlp_norm"]], axis=0),
        ple_norm=jnp.concatenate([n_t0["ple_norm"], n_t1["ple_norm"]], axis=0),
        ret_gn=n_ret["ret_gn"], mla_q_a_norm=n_mla["mla_q_a_norm"], mla_kv_a_norm=n_mla["mla_kv_a_norm"],
        mla_q_norm=n_mla["mla_q_norm"], mla_k_norm=n_mla["mla_k_norm"])


_ORDER = ("mix_norm", "ret_w_in", "ret_gn", "ret_w_out", "mla_w_in", "mla_q_a_norm", "mla_kv_a_norm", "mla_w_uq",
          "mla_w_ukv", "mla_q_norm", "mla_k_norm", "mla_w_out", "mlp_norm", "mlp_w1", "mlp_w2", "ple_norm",
          "ple_gate_w", "ple_proj_w")
_TWO_LAYER = ("mlp_w1", "mlp_w2", "ple_gate_w", "ple_proj_w")
HEADS_PER_CHIP = MLA_HEADS // N_CHIPS
GAIN_ROWS = 32


def _travel_parts(w):
    uq = jnp.pad(w["mla_w_uq"][0].reshape(MLA_Q_RANK, HEADS_PER_CHIP, MLA_QKD), ((0, 0), (0, 0), (0, MLA_HP - MLA_QKD)))
    parts = {"ret_w_in": w["ret_w_in"][0], "ret_w_out": w["ret_w_out"][0]}
    for k in _TWO_LAYER:
        parts[k + "_0"] = w[k][0]
    parts["mla_w_in"] = jnp.pad(w["mla_w_in"][0], ((0, 0), (0, MLA_IN_PAD - MLA_IN)))
    parts["mla_w_uq"] = uq.reshape(MLA_Q_RANK, HEADS_PER_CHIP * MLA_HP)
    parts["mla_w_ukv"] = w["mla_w_ukv"][0]
    parts["mla_w_out"] = w["mla_w_out"][0]
    for k in _TWO_LAYER:
        parts[k + "_1"] = w[k][1]
    gains = jnp.concatenate([_pad_row(w["ret_gn"]), _pad_row(w["mla_q_a_norm"]), _pad_row(w["mla_kv_a_norm"]),
                             jnp.zeros((GAIN_ROWS - 3, PACK_W), F32)], axis=0)
    return {"gains": gains, **{k: v.astype(BF16) for k, v in parts.items()}}


def _full_weights(full):
    rows = lambda a: a.reshape(-1, a.shape[-1])
    W = {k: full[k] for k in ("ret_w_in", "mla_w_uq", "mla_w_ukv") if k in full}
    for k in ("ret_w_out", "mla_w_in", "mla_w_out"):
        if k in full:
            W[k] = rows(full[k])
    for k, by_rows in (("mlp_w1", False), ("ple_proj_w", False), ("mlp_w2", True), ("ple_gate_w", True)):
        layers = [full.get(f"{k}_{i}") for i in range(2)]
        W[k] = [rows(t) if (by_rows and t is not None) else t for t in layers]
    return W


def _shard_grad(name, red, shape):
    if name == "mla_w_in":
        red = red.reshape(-1, MLA_IN_PAD)[:, :MLA_IN]
    elif name == "mla_w_uq":
        red = red.reshape(MLA_Q_RANK, HEADS_PER_CHIP, MLA_HP)[:, :, :MLA_QKD]
    return red.reshape(shape)


def _pad_row(v):
    v = v.reshape(1, -1)
    return jnp.pad(v, ((0, 0), (0, PACK_W - v.shape[1])))


def kernel(x, p, mix_norm, ret_w_in, ret_gn, ret_w_out, mla_w_in, mla_q_a_norm, mla_kv_a_norm, mla_w_uq, mla_w_ukv, mla_q_norm, mla_k_norm, mla_w_out, mlp_norm, mlp_w1, mlp_w2, ple_norm, ple_gate_w, ple_proj_w, loss_target, m_mix_norm, m_ret_w_in, m_ret_gn, m_ret_w_out, m_mla_w_in, m_mla_q_a_norm, m_mla_kv_a_norm, m_mla_w_uq, m_mla_w_ukv, m_mla_q_norm, m_mla_k_norm, m_mla_w_out, m_mlp_norm, m_mlp_w1, m_mlp_w2, m_ple_norm, m_ple_gate_w, m_ple_proj_w, v_mix_norm, v_ret_w_in, v_ret_gn, v_ret_w_out, v_mla_w_in, v_mla_q_a_norm, v_mla_kv_a_norm, v_mla_w_uq, v_mla_w_ukv, v_mla_q_norm, v_mla_k_norm, v_mla_w_out, v_mlp_norm, v_mlp_w1, v_mlp_w2, v_ple_norm, v_ple_gate_w, v_ple_proj_w):
    w = dict(mix_norm=mix_norm, ret_w_in=ret_w_in, ret_gn=ret_gn, ret_w_out=ret_w_out, mla_w_in=mla_w_in,
             mla_q_a_norm=mla_q_a_norm, mla_kv_a_norm=mla_kv_a_norm, mla_w_uq=mla_w_uq, mla_w_ukv=mla_w_ukv,
             mla_q_norm=mla_q_norm, mla_k_norm=mla_k_norm, mla_w_out=mla_w_out, mlp_norm=mlp_norm, mlp_w1=mlp_w1,
             mlp_w2=mlp_w2, ple_norm=ple_norm, ple_gate_w=ple_gate_w, ple_proj_w=ple_proj_w)
    m = dict(mix_norm=m_mix_norm, ret_w_in=m_ret_w_in, ret_gn=m_ret_gn, ret_w_out=m_ret_w_out, mla_w_in=m_mla_w_in,
             mla_q_a_norm=m_mla_q_a_norm, mla_kv_a_norm=m_mla_kv_a_norm, mla_w_uq=m_mla_w_uq, mla_w_ukv=m_mla_w_ukv,
             mla_q_norm=m_mla_q_norm, mla_k_norm=m_mla_k_norm, mla_w_out=m_mla_w_out, mlp_norm=m_mlp_norm,
             mlp_w1=m_mlp_w1, mlp_w2=m_mlp_w2, ple_norm=m_ple_norm, ple_gate_w=m_ple_gate_w, ple_proj_w=m_ple_proj_w)
    v = dict(mix_norm=v_mix_norm, ret_w_in=v_ret_w_in, ret_gn=v_ret_gn, ret_w_out=v_ret_w_out, mla_w_in=v_mla_w_in,
             mla_q_a_norm=v_mla_q_a_norm, mla_kv_a_norm=v_mla_kv_a_norm, mla_w_uq=v_mla_w_uq, mla_w_ukv=v_mla_w_ukv,
             mla_q_norm=v_mla_q_norm, mla_k_norm=v_mla_k_norm, mla_w_out=v_mla_w_out, mlp_norm=v_mlp_norm,
             mlp_w1=v_mlp_w1, mlp_w2=v_mlp_w2, ple_norm=v_ple_norm, ple_gate_w=v_ple_gate_w, ple_proj_w=v_ple_proj_w)
    xi, yi, ci = _place()
    chip = 2 * xi + yi
    n = N_CHIPS

    parts = _travel_parts(w)
    first = ("gains", "ret_w_in", "ret_w_out")
    mid = [k + "_0" for k in _TWO_LAYER]
    last = [k for k in parts if k not in first and k not in mid]
    full = dict(zip(first, _gather_weights([parts[k] for k in first], "gather_first")))

    def gather_behind(names, tag, after):
        copies = _gather_copies([parts[k].shape[0] for k in names])
        started = _split_start(f"gather_{tag}_start", [parts[k] for k in names],
                               [jax.ShapeDtypeStruct((n, *parts[k].shape), BF16) for k in names], 3 * len(names),
                               copies, after=after)

        def arrive(after):
            landed = _split_wait(f"gather_{tag}_wait", *started[:4], copies, after=after)
            full.update(zip(names, _gather_weights([parts[k] for k in names], f"gather_{tag}_finish", landed=landed)))
            W.update(_full_weights(full))
        return started[4], arrive

    mid_token, mid_arrive = gather_behind(mid, "mid", [full["ret_w_in"]])
    g_token, last_arrive = gather_behind(last, "last", [mid_token])
    gains = full["gains"]
    W = dict(mix_norm=mix_norm, mlp_norm=mlp_norm, ple_norm=ple_norm,
             mla_q_norm=jnp.pad(mla_q_norm, ((0, 0), (0, MLA_HP - MLA_QKD))),
             mla_k_norm=jnp.pad(mla_k_norm, ((0, 0), (0, MLA_HP - MLA_QKD))),
             ret_w_in=full["ret_w_in"], ret_w_out=full["ret_w_out"].reshape(-1, D_MODEL),
             ret_gn=gains[:, 0, :RET_HEADS * 128].reshape(n, RET_HEADS, 128).transpose(1, 0, 2).reshape(RET_HEADS, RET_DV),
             mla_q_a_norm=gains[:, 1, :MLA_Q_RANK // n].reshape(1, MLA_Q_RANK),
             mla_kv_a_norm=gains[:, 2, :MLA_KV_RANK // n].reshape(1, MLA_KV_RANK))
    x0, p16, target = x[0], p[:, 0].astype(BF16), loss_target[0]
    T = x0.shape[0]
    ret_tabs, mla_tabs = _ret_tables(T), _mla_tables(T)

    h1, hn, s_ret = _ret_layer_fwd(x0, W, ret_tabs, after=[g_token])
    mid_arrive([h1])
    h3, hn, s_tail0 = _tail_fwd(h1, hn, p16, W, 0, "l0", next_gain=W["mix_norm"][1:2])
    last_arrive([h3])
    h4, hn, s_mla = _mla_layer_fwd(h3, hn, W, mla_tabs)
    dy, loss, s_tail1 = _tail_fwd(h4, hn, p16, W, 1, "l1", target=target)

    dh4, dh4_16, g_t1, n_t1 = _tail_bwd(dy, s_tail1, p16, W, 1, "l1")
    dh3, _, g_mla, n_mla = _mla_layer_bwd(dh4, dh4_16, s_mla, W, mla_tabs)
    beg_a = _reduce_begin({**g_mla, **g_t1}, ci, "a")
    a_send, a_recv, a_src, a_land, a_token = _split_start(
        "scatter_a_start", beg_a[3], _got_shapes(beg_a[3]), 3 * len(beg_a[3]), _scatter_copies)
    dh1, dh1_16, g_t0, n_t0 = _tail_bwd(dh3, s_tail0, p16, W, 0, "l0", after=[a_token])
    d_ret_w_out = _d_ret_w_out(dh1_16, s_ret)
    beg_b = _reduce_begin({**g_t0, "ret_w_out": d_ret_w_out}, ci, "b")
    b_send, b_recv, b_src, b_land, b_token = _split_start(
        "scatter_b_start", beg_b[3], _got_shapes(beg_b[3]), 3 * len(beg_b[3]), _scatter_copies)
    stage_c = {}

    def start_c(g_ret):
        beg = _reduce_begin({"ret_w_in": g_ret["ret_w_in"]}, ci, "c")
        stage_c["beg"] = beg
        stage_c["st"] = _split_start("scatter_c_start", beg[3], _got_shapes(beg[3]), 3 * len(beg[3]), _scatter_copies)
        return [stage_c["st"][4]]

    dx, _, n_ret = _ret_layer_bwd(dh1, dh1_16, s_ret, W, ret_tabs, after=[b_token], on_grads=start_c,
                                  d_w_out=d_ret_w_out)
    got_a = _split_wait("scatter_a_wait", a_send, a_recv, a_src, a_land, _scatter_copies, after=[dx])
    got_b = _split_wait("scatter_b_wait", b_send, b_recv, b_src, b_land, _scatter_copies, after=[dx])
    got_c = _split_wait("scatter_c_wait", *stage_c["st"][:4], _scatter_copies, after=[dx])
    red = {**_reduce_end(beg_a, got_a, chip, ci), **_reduce_end(beg_b, got_b, chip, ci),
           **_reduce_end(stage_c["beg"], got_c, chip, ci)}
    red = dict(zip(red, _share_halves(list(red.values()))))
    gs = _small_grads(n_ret, n_t0, n_mla, n_t1)
    small_g = jnp.concatenate([
        gs["mix_norm"], gs["mlp_norm"], gs["ple_norm"], gs["ret_gn"].reshape(2, PACK_W), _pad_row(gs["mla_q_a_norm"]),
        _pad_row(gs["mla_kv_a_norm"]), _pad_row(gs["mla_q_norm"][:, :MLA_QKD]), _pad_row(gs["mla_k_norm"][:, :MLA_QKD]),
        _pad_row(loss[:, :1]), jnp.zeros((3, PACK_W), F32)], axis=0)
    tot = _allsum_small(small_g, "sum_small_grads")
    gn_all = tot[6:8].reshape(RET_HEADS, n, -1)
    g_small = dict(
        mix_norm=tot[0:2], mlp_norm=tot[2:4], ple_norm=tot[4:6],
        ret_gn=lax.dynamic_index_in_dim(gn_all, chip, axis=1, keepdims=False),
        mla_q_a_norm=lax.dynamic_index_in_dim(tot[8, :MLA_Q_RANK].reshape(n, -1), chip, axis=0, keepdims=True),
        mla_kv_a_norm=lax.dynamic_index_in_dim(tot[9, :MLA_KV_RANK].reshape(n, -1), chip, axis=0, keepdims=True),
        mla_q_norm=tot[10:11, :MLA_QKD], mla_k_norm=tot[11:12, :MLA_QKD])
    loss_out = tot[12, 0]

    outs = []
    for k in _ORDER:
        if k in _TWO_LAYER:
            res = None
            for i in (1, 0):
                res = _adamw(w[k], red[f"{k}_{i}"], m[k], v[k], f"adamw_{k}_{i}", layers=2, layer=i, into=res)
        elif k in red:
            res = _adamw(w[k], _shard_grad(k, red[k], w[k].shape), m[k], v[k], f"adamw_{k}")
        else:
            res = _adamw(w[k], g_small[k], m[k], v[k], f"adamw_{k}")
        outs.append(res)
    return (loss_out, dx[None], *[o[0] for o in outs], *[o[1] for o in outs], *[o[2] for o in outs],
            *[o[3] for o in outs])
```

```python
import functools

import jax
import jax.numpy as jnp
import numpy as np
from jax import lax
from jax.experimental import pallas as pl
from jax.experimental.pallas import tpu as pltpu

F32 = jnp.float32
BF16 = jnp.bfloat16

EPS = 1e-6
D_MODEL = 1024
CHUNK = 64
ROPE_THETA = 10000.0
RET_HEADS = 4
RET_DK = 256
RET_DV = 512
RET_GROUP = 1
RET_BLOCK = 256
MLA_HEADS = 8
MLA_NOPE = 128
MLA_ROPE = 64
MLA_QKD = 192
MLA_VD = 128
MLA_HP = 256
MLA_Q_RANK = 384
MLA_KV_RANK = 256
MLA_IN = 704
MLA_IN_PAD = 768
D_FF = 4096
PLE_DIM = 256
N_CHIPS = 4

ADAM_LR = 0.001
ADAM_B1 = 0.9
ADAM_B2 = 0.999
ADAM_EPS = 1e-08
ADAM_WD = 0.01
ADAM_STEP = 10

VMEM_LIMIT = 56 * 1024 * 1024
PACK_W = 1024
NEG = -1e30
LOG2E = 1.4426950408889634
FLASH_T = 512
FLASH_HEADS = 2
MM_SUB_ROWS = 256


def _cparams(sem=None):
    return pltpu.CompilerParams(dimension_semantics=sem, vmem_limit_bytes=VMEM_LIMIT)


def _pick(dim, pref):
    if dim <= pref:
        return dim
    t = pref
    while dim % t:
        t //= 2
    return t


def _mm(a, b, *, name, ta=False, tb=False, bblk=False, outs=None, extras=(), epilogue=None, dw=None,
        tm=1024, tn=512, after=()):
    if ta:
        K, M = a.shape
    else:
        M, K = a.shape
    if bblk and tb:
        nb, N, Kq = b.shape
        assert nb * Kq == K
    elif bblk:
        nb, Kb, Nq = b.shape
        N = nb * Nq
        assert Kb == K
    else:
        N = b.shape[0] if tb else b.shape[1]
    tn = _pick(Nq if (bblk and not tb) else N, tn)
    if dw is not None and dw[0] == "cols":
        tn = _pick(N // N_CHIPS, tn)
    tm = _pick(M // N_CHIPS if (dw is not None and dw[0] == "rows") else M, tm)
    grid = (M // tm, N // tn)

    a_spec = pl.BlockSpec((K, tm), lambda i, j: (0, i)) if ta else pl.BlockSpec((tm, K), lambda i, j: (i, 0))
    if bblk and tb:
        b_spec = pl.BlockSpec((nb, tn, Kq), lambda i, j: (0, j, 0))
    elif bblk:
        npb = Nq // tn
        b_spec = pl.BlockSpec((None, K, tn), lambda i, j: (j // npb, 0, j % npb))
    elif tb:
        b_spec = pl.BlockSpec((tn, K), lambda i, j: (j, 0))
    else:
        b_spec = pl.BlockSpec((K, tn), lambda i, j: (0, j))
    in_specs = [a_spec, b_spec] + [pl.BlockSpec((tm, tn), lambda i, j: (i, j)) for _ in extras]
    args = [a, b, *extras]
    aliases = {}
    if outs is None:
        outs = [F32]
    if dw is None:
        o_specs = [pl.BlockSpec((tm, tn), lambda i, j: (i, j)) for _ in outs]
        o_shapes = [jax.ShapeDtypeStruct((M, N), dt) for dt in outs]
    else:
        kind, layers, layer, into = dw
        if kind == "cols":
            per = (N // N_CHIPS) // tn
            o_specs = [pl.BlockSpec((None, None, tm, tn), lambda i, j: (j // per, layer, i, j % per))]
            o_shapes = [jax.ShapeDtypeStruct((N_CHIPS, layers, M, N // N_CHIPS), outs[0])]
        else:
            per = (M // N_CHIPS) // tm
            o_specs = [pl.BlockSpec((None, None, tm, tn), lambda i, j: (i // per, layer, i % per, j))]
            o_shapes = [jax.ShapeDtypeStruct((N_CHIPS, layers, M // N_CHIPS, N), outs[0])]
        if into is not None:
            aliases = {len(args): 0}
            in_specs.append(pl.BlockSpec(memory_space=pl.ANY))
            args.append(into)
    for t in after:
        in_specs.append(pl.BlockSpec(memory_space=pl.ANY))
        args.append(t)
    n_e, n_o = len(extras), len(outs)

    sub = _pick(tm, MM_SUB_ROWS)

    def body(a_ref, b_ref, *rest):
        e_refs, o_refs = rest[:n_e], rest[len(rest) - n_o:]
        for r0 in range(0, tm, sub):
            rows = slice(r0, r0 + sub)
            av = (a_ref[:, rows] if ta else a_ref[rows, :]).astype(BF16)
            if bblk and tb:
                acc = _dot_nt(av[:, :Kq], b_ref[0].astype(BF16))
                for s in range(1, nb):
                    acc = acc + _dot_nt(av[:, s * Kq:(s + 1) * Kq], b_ref[s].astype(BF16))
            elif ta:
                acc = _dot_tn(av, b_ref[...].astype(BF16))
            elif tb:
                acc = _dot_nt(av, b_ref[...].astype(BF16))
            else:
                acc = _dot(av, b_ref[...].astype(BF16))
            vals = (acc,) if epilogue is None else epilogue(acc, *[e[rows, :] for e in e_refs])
            for o, v in zip(o_refs, vals):
                o[rows, :] = v.astype(o.dtype)

    res = pl.pallas_call(
        body, name=name, grid=grid, in_specs=in_specs, out_specs=o_specs, out_shape=o_shapes,
        input_output_aliases=aliases, compiler_params=_cparams(("parallel", "arbitrary")),
    )(*args)
    return res[0] if n_o == 1 else res


def _mm_rows(a, b, *, name, epilogue, outs, tb=False, bblk=False, extras=(), fulls=(), accs=(), tm=512, after=()):
    M, K = a.shape
    tm = _pick(M, tm)
    sub = _pick(tm, MM_SUB_ROWS)
    nb = b.shape[0] if bblk else 1
    n_e, n_f, n_o, n_a = len(extras), len(fulls), len(outs), len(accs)
    n_in = 2 + n_e + n_f + len(after)

    def whole(t):
        return pl.BlockSpec(t.shape, lambda i, nd=t.ndim: (0,) * nd)

    in_specs = [pl.BlockSpec((tm, K), lambda i: (i, 0)), whole(b)]
    in_specs += [pl.BlockSpec((tm, e.shape[1]), lambda i: (i, 0)) for e in extras] + [whole(f) for f in fulls]
    in_specs += [pl.BlockSpec(memory_space=pl.ANY) for _ in after]
    out_specs = [pl.BlockSpec((tm, w), lambda i: (i, 0)) for w, _ in outs] + [pl.BlockSpec(s, lambda i: (0, 0)) for s, _ in accs]
    out_shape = [jax.ShapeDtypeStruct((M, w), dt) for w, dt in outs] + [jax.ShapeDtypeStruct(s, dt) for s, dt in accs]

    def body(a_ref, b_ref, *rest):
        e_refs, f_refs = rest[:n_e], rest[n_e:n_e + n_f]
        o_refs, acc_refs = rest[n_in - 2:n_in - 2 + n_o], rest[n_in - 2 + n_o:]
        fv = [f[...] for f in f_refs]
        totals = None
        for r0 in range(0, tm, sub):
            rows = slice(r0, r0 + sub)
            av = a_ref[rows, :].astype(BF16)
            if bblk and tb:
                kq = K // nb
                acc = _dot_nt(av[:, :kq], b_ref[0])
                for s in range(1, nb):
                    acc = acc + _dot_nt(av[:, s * kq:(s + 1) * kq], b_ref[s])
            elif bblk:
                acc = jnp.concatenate([_dot(av, b_ref[s]) for s in range(nb)], axis=-1)
            elif tb:
                acc = _dot_nt(av, b_ref[...])
            else:
                acc = _dot(av, b_ref[...])
            vals = epilogue(acc, *[e[rows, :] for e in e_refs], *fv)
            for o, v in zip(o_refs, vals[:n_o]):
                o[rows, :] = v.astype(o.dtype)
            part = vals[n_o:]
            totals = part if totals is None else [t + p for t, p in zip(totals, part)]
        first_step = pl.program_id(0) == 0
        for o, v in zip(acc_refs, totals):
            @pl.when(first_step)
            def _(o=o, v=v):
                o[...] = v.astype(o.dtype)

            @pl.when(jnp.logical_not(first_step))
            def _(o=o, v=v):
                o[...] += v.astype(o.dtype)

    return pl.pallas_call(
        body, name=name, grid=(M // tm,), in_specs=in_specs, out_specs=out_specs, out_shape=out_shape,
        compiler_params=_cparams(("arbitrary",)),
    )(a, b, *extras, *fulls, *after)


def _rows(fn, rows, fulls, outs, accs=(), *, name, tile=512, after=()):
    first = rows[0][0] if isinstance(rows[0], tuple) else rows[0]
    T = first.shape[0]
    tile = _pick(T, tile)
    in_specs, args = [], []
    for r in rows:
        if isinstance(r, tuple):
            arr, w, cb = r
            in_specs.append(pl.BlockSpec((tile, w), lambda i, cb=cb: (i, cb)))
        else:
            arr = r
            in_specs.append(pl.BlockSpec((tile, arr.shape[1]), lambda i: (i, 0)))
        args.append(arr)
    for f in fulls:
        in_specs.append(pl.BlockSpec(f.shape, lambda i, nd=f.ndim: (0,) * nd))
        args.append(f)
    outs = [o if len(o) == 4 else (*o, o[0], 0) for o in outs]
    out_specs = [pl.BlockSpec((tile, w), lambda i, cb=cb: (i, cb)) for w, _, _, cb in outs]
    out_specs += [pl.BlockSpec(s, lambda i: (0, 0)) for s, _ in accs]
    out_shape = [jax.ShapeDtypeStruct((T, tw), dt) for _, dt, tw, _ in outs]
    out_shape += [jax.ShapeDtypeStruct(s, dt) for s, dt in accs]
    n_in, n_out = len(args), len(outs)
    for t in after:
        in_specs.append(pl.BlockSpec(memory_space=pl.ANY))
        args.append(t)

    def body(*refs):
        vals = fn(*[r[...] for r in refs[:n_in]])
        o_refs = refs[len(args):]
        for o, v in zip(o_refs[:n_out], vals[:n_out]):
            o[...] = v.astype(o.dtype)
        first_step = pl.program_id(0) == 0
        for o, v in zip(o_refs[n_out:], vals[n_out:]):
            @pl.when(first_step)
            def _(o=o, v=v):
                o[...] = v.astype(o.dtype)

            @pl.when(jnp.logical_not(first_step))
            def _(o=o, v=v):
                o[...] += v.astype(o.dtype)

    res = pl.pallas_call(
        body, name=name, grid=(T // tile,), in_specs=in_specs, out_specs=out_specs, out_shape=out_shape,
        compiler_params=_cparams(("arbitrary",)),
    )(*args)
    return res


def _rowsum(v, mxu):
    if not mxu:
        return jnp.sum(v, axis=-1, keepdims=True)
    ones = jnp.ones((v.shape[1], v.shape[1]), BF16)
    hi = v.astype(BF16)
    lo = (v - hi.astype(F32)).astype(BF16)
    return _dot(hi, ones) + _dot(lo, ones)


def _rms(x, g, mxu=False):
    r = lax.rsqrt(_rowsum(x * x, mxu) / x.shape[-1] + EPS)
    return (x * r) * g


def _rms_bwd(x, dy, g, n=None, mxu=False):
    n = x.shape[-1] if n is None else n
    r = lax.rsqrt(_rowsum(x * x, mxu) / n + EPS)
    xh = x * r
    dxh = dy * g
    dx = r * (dxh - xh * (_rowsum(dxh * xh, mxu) / n))
    return dx, dy * xh


def _colsum(v):
    return jnp.sum(v, axis=0, keepdims=True)


def _sigmoid(x):
    return 1.0 / (1.0 + jnp.exp(-x))


def _widen(v, width):
    reps = width // v.shape[1]
    return v if reps == 1 else jnp.concatenate([v] * reps, axis=-1)


def _rope_angles(T, dim):
    inv = (1.0 / (np.float32(ROPE_THETA) ** (np.arange(0, dim, 2, dtype=np.float32) / np.float32(dim)))).astype(np.float32)
    return np.arange(T, dtype=np.float32)[:, None] * inv[None, :]


def _ret_tables(T):
    ang = _rope_angles(T, RET_DK)
    log_gamma = np.log(np.float32(1.0) - np.float32(2.0) ** (-5.0 - np.arange(RET_HEADS, dtype=np.float32)))
    idx = np.arange(RET_BLOCK, dtype=np.float32)
    chunk = np.arange(RET_BLOCK) // CHUNK
    dist = idx[:, None] - idx[None, :]
    seen = np.where(chunk[:, None] == chunk[None, :], np.abs(dist), np.where(chunk[:, None] > chunk[None, :], dist, np.inf))
    intra = np.exp(log_gamma[:, None, None] * seen[None].astype(np.float32))
    qd = np.exp(log_gamma[:, None] * (idx + 1.0))[:, :, None]
    kd = np.exp(log_gamma[:, None] * (RET_BLOCK - 1.0 - idx))[:, :, None]
    cd = np.exp(log_gamma * RET_BLOCK)[:, None, None]
    return tuple(jnp.asarray(t, F32) for t in (np.cos(ang), np.sin(ang), intra, qd, kd, cd))


def _rope_half(x, c, s):
    x1, x2 = x[:, :RET_DK // 2], x[:, RET_DK // 2:]
    return jnp.concatenate([x1 * c - x2 * s, x2 * c + x1 * s], axis=-1)


def _rope_half_bwd(d, c, s):
    d1, d2 = d[:, :RET_DK // 2], d[:, RET_DK // 2:]
    return jnp.concatenate([d1 * c + d2 * s, d2 * c - d1 * s], axis=-1)


def _dot(a, b):
    return lax.dot_general(a, b, (((1,), (0,)), ((), ())), preferred_element_type=F32)


def _dot_nt(a, b):
    return lax.dot_general(a, b, (((1,), (1,)), ((), ())), preferred_element_type=F32)


def _dot_tn(a, b):
    return lax.dot_general(a, b, (((0,), (0,)), ((), ())), preferred_element_type=F32)


def _ret_specs(T, tb, rev):
    nj = T // tb
    jj = (lambda j: nj - 1 - j) if rev else (lambda j: j)
    g = RET_GROUP
    kq = RET_HEADS // g
    vq = 2 * RET_HEADS * RET_DK // (g * RET_DV)
    return dict(
        q=pl.BlockSpec((tb, g * RET_DK), lambda h, j: (jj(j), h)),
        k=pl.BlockSpec((tb, g * RET_DK), lambda h, j: (jj(j), kq + h)),
        v=pl.BlockSpec((tb, g * RET_DV), lambda h, j: (jj(j), vq + h)),
        tab=pl.BlockSpec((tb, RET_DK // 2), lambda h, j: (jj(j), 0)),
        intra=pl.BlockSpec((g, RET_BLOCK, RET_BLOCK), lambda h, j: (h, 0, 0)),
        dec=pl.BlockSpec((g, RET_BLOCK, 1), lambda h, j: (h, 0, 0)),
        cd=pl.BlockSpec((g, 1, 1), lambda h, j: (h, 0, 0)),
        o=pl.BlockSpec((tb, g * RET_DV), lambda h, j: (jj(j), h)),
        s=pl.BlockSpec((g, tb // RET_BLOCK, RET_DK, RET_DV), lambda h, j: (h, jj(j), 0, 0)),
    )


def _ret_fwd(proj, tabs, name):
    T = proj.shape[0]
    cos, sin, intra, qd, kd, cd = tabs
    tb = _pick(T, 512)
    cps = tb // RET_BLOCK
    sp = _ret_specs(T, tb, False)
    scale = RET_DK ** -0.5

    def body(q_ref, k_ref, v_ref, cos_ref, sin_ref, intra_ref, qd_ref, kd_ref, cd_ref, o_ref, s_ref, state):
        @pl.when(pl.program_id(1) == 0)
        def _():
            state[...] = jnp.zeros_like(state)

        for c in range(cps):
            rows = pl.ds(c * RET_BLOCK, RET_BLOCK)
            co, si = cos_ref[rows, :], sin_ref[rows, :]
            for h in range(RET_GROUP):
                hk, hv = slice(h * RET_DK, (h + 1) * RET_DK), slice(h * RET_DV, (h + 1) * RET_DV)
                q = _rope_half(q_ref[rows, hk].astype(F32), co, si)
                k = _rope_half(k_ref[rows, hk].astype(F32), co, si) * scale
                vb = v_ref[rows, hv].astype(BF16)
                st = state[h]
                sb = st.astype(BF16)
                s_ref[h, c] = sb
                sc = _dot_nt(q.astype(BF16), k.astype(BF16)) * intra_ref[h]
                inner = _dot(sc.astype(BF16), vb)
                cross = _dot((q * qd_ref[h]).astype(BF16), sb)
                o_ref[rows, hv] = inner + cross
                state[h] = st * cd_ref[h] + _dot_tn((k * kd_ref[h]).astype(BF16), vb)

    return pl.pallas_call(
        body, name=name, grid=(RET_HEADS // RET_GROUP, T // tb),
        in_specs=[sp["q"], sp["k"], sp["v"], sp["tab"], sp["tab"], sp["intra"], sp["dec"], sp["dec"], sp["cd"]],
        out_specs=[sp["o"], sp["s"]],
        out_shape=[jax.ShapeDtypeStruct((T, RET_HEADS * RET_DV), F32),
                   jax.ShapeDtypeStruct((RET_HEADS, T // RET_BLOCK, RET_DK, RET_DV), BF16)],
        scratch_shapes=[pltpu.VMEM((RET_GROUP, RET_DK, RET_DV), F32)],
        compiler_params=_cparams(("arbitrary", "arbitrary")),
    )(proj, proj, proj, cos, sin, intra, qd, kd, cd)


def _ret_bwd(proj, states, dout, dproj, tabs, name):
    assert RET_GROUP == 1
    T = proj.shape[0]
    cos, sin, intra, qd, kd, cd = tabs
    tb = _pick(T, 512)
    cps = tb // RET_BLOCK
    nj = T // tb
    sp = _ret_specs(T, tb, True)
    scale = RET_DK ** -0.5
    k0, v0 = RET_HEADS * RET_DK, 2 * RET_HEADS * RET_DK

    def body(q_ref, k_ref, v_ref, cos_ref, sin_ref, intra_ref, qd_ref, kd_ref, cd_ref, s_ref, do_ref, _dproj_in,
             out_ref, dq_s, dk_s, dv_s, sems, dstate):
        head, j = pl.program_id(0), pl.program_id(1)
        step = head * nj + j
        slot = step % 2
        dq_ref, dk_ref, dv_ref = dq_s.at[slot], dk_s.at[slot], dv_s.at[slot]

        @pl.when(j == 0)
        def _():
            dstate[...] = jnp.zeros_like(dstate)

        for c in reversed(range(cps)):
            rows = pl.ds(c * RET_BLOCK, RET_BLOCK)
            co, si = cos_ref[rows, :], sin_ref[rows, :]
            for h in range(RET_GROUP):
                hk, hv = slice(h * RET_DK, (h + 1) * RET_DK), slice(h * RET_DV, (h + 1) * RET_DV)
                q = _rope_half(q_ref[rows, hk].astype(F32), co, si)
                k = _rope_half(k_ref[rows, hk].astype(F32), co, si) * scale
                qb, kb = q.astype(BF16), k.astype(BF16)
                vb = v_ref[rows, hv].astype(BF16)
                dob = do_ref[rows, hv].astype(BF16)
                sb = s_ref[h, c]
                ia = intra_ref[h]
                pb = (_dot_nt(qb, kb) * ia).astype(BF16)
                dsn = dstate[h]
                dsb = dsn.astype(BF16)
                kdk = (k * kd_ref[h]).astype(BF16)
                qdq = (q * qd_ref[h]).astype(BF16)
                dv = _dot_tn(pb, dob) + _dot(kdk, dsb)
                dpb = (_dot_nt(dob, vb) * ia).astype(BF16)
                dq = _dot(dpb, kb) + _dot_nt(dob, sb) * qd_ref[h]
                dk = _dot_tn(dpb, qb) + _dot_nt(vb, dsb) * kd_ref[h]
                dstate[h] = dsn * cd_ref[h] + _dot_tn(qdq, dob)
                dq_ref[rows, hk] = _rope_half_bwd(dq, co, si).astype(BF16)
                dk_ref[rows, hk] = _rope_half_bwd(dk * scale, co, si).astype(BF16)
                dv_ref[rows, hv] = dv.astype(BF16)

        def copies(sl):
            r = pl.ds(pl.multiple_of((nj - 1 - j) * tb, tb), tb)
            cols = lambda first, w: pl.ds(pl.multiple_of(first + head * w, 128), w)
            return [pltpu.make_async_copy(dq_s.at[sl], out_ref.at[r, cols(0, RET_DK)], sems.at[sl, 0]),
                    pltpu.make_async_copy(dk_s.at[sl], out_ref.at[r, cols(k0, RET_DK)], sems.at[sl, 1]),
                    pltpu.make_async_copy(dv_s.at[sl], out_ref.at[r, cols(v0, RET_DV)], sems.at[sl, 2])]

        @pl.when(step > 0)
        def _():
            for cp in copies(1 - slot):
                cp.wait()

        for cp in copies(slot):
            cp.start()

        @pl.when(step == RET_HEADS * nj - 1)
        def _():
            for cp in copies(slot):
                cp.wait()

    return pl.pallas_call(
        body, name=name, grid=(RET_HEADS, nj),
        in_specs=[sp["q"], sp["k"], sp["v"], sp["tab"], sp["tab"], sp["intra"], sp["dec"], sp["dec"], sp["cd"],
                  sp["s"], sp["o"], pl.BlockSpec(memory_space=pl.ANY)],
        out_specs=pl.BlockSpec(memory_space=pl.ANY), out_shape=jax.ShapeDtypeStruct(dproj.shape, dproj.dtype),
        input_output_aliases={11: 0},
        scratch_shapes=[pltpu.VMEM((2, tb, RET_DK), BF16), pltpu.VMEM((2, tb, RET_DK), BF16),
                        pltpu.VMEM((2, tb, RET_DV), BF16), pltpu.SemaphoreType.DMA((2, 3)),
                        pltpu.VMEM((RET_GROUP, RET_DK, RET_DV), F32)],
        compiler_params=_cparams(("arbitrary", "arbitrary")),
    )(proj, proj, proj, cos, sin, intra, qd, kd, cd, states, dout, dproj)


def _ret_gate(out, proj, gn, name):
    def fn(o, g, *gains):
        g = g.astype(F32)
        parts = [_rms(o[:, h * RET_DV:(h + 1) * RET_DV], gains[h]) for h in range(RET_HEADS)]
        return (g * _sigmoid(g) * jnp.concatenate(parts, axis=-1),)
    w = RET_HEADS * RET_DV
    return _rows(fn, [out, (proj, w, 2)], [gn[h:h + 1] for h in range(RET_HEADS)], [(w, BF16)], name=name)[0]


def _ret_gate_bwd(out, proj, gn, dy, name):
    def fn(o, g, d, *gains):
        g = g.astype(F32)
        sg = _sigmoid(g)
        silu = g * sg
        dsilu = sg * (1.0 + g * (1.0 - sg))
        dos, dgs = [], []
        row = lax.broadcasted_iota(jnp.int32, (RET_HEADS, RET_DV), 0)
        dgn = jnp.zeros((RET_HEADS, RET_DV), F32)
        for h in range(RET_HEADS):
            sl = slice(h * RET_DV, (h + 1) * RET_DV)
            oh = o[:, sl]
            dgs.append(d[:, sl] * _rms(oh, gains[h]) * dsilu[:, sl])
            dx, dg = _rms_bwd(oh, d[:, sl] * silu[:, sl], gains[h])
            dos.append(dx)
            dgn = dgn + jnp.where(row == h, _colsum(dg), 0.0)
        return jnp.concatenate(dos, axis=-1), jnp.concatenate(dgs, axis=-1), dgn
    w = RET_HEADS * RET_DV
    return _rows(fn, [out, (proj, w, 2), dy], [gn[h:h + 1] for h in range(RET_HEADS)],
                 [(w, BF16), (w, BF16, proj.shape[1], 2)], [((RET_HEADS, RET_DV), F32)], name=name, tile=128)


def _mla_tables(T):
    ang = _rope_angles(T, MLA_ROPE)
    c, s = np.cos(ang), np.sin(ang)
    z32, z64 = np.zeros((T, 32), np.float32), np.zeros((T, 64), np.float32)
    cos_t = np.concatenate([c, c, z64], axis=1)
    sin_a = np.concatenate([-s, z32, z64], axis=1)
    sin_b = np.concatenate([z32, s, z64], axis=1)
    return tuple(jnp.asarray(t, F32) for t in (cos_t, sin_a, sin_b))


def _rope_blk(x, ct, sa, sb):
    return x * ct + pltpu.roll(x, 96, 1) * sa + pltpu.roll(x, 32, 1) * sb


def _rope_blk_bwd(d, ct, sa, sb):
    return d * ct + pltpu.roll(d * sa, 32, 1) + pltpu.roll(d * sb, 96, 1)


def _head_norm(x, gain):
    r = lax.rsqrt(_rowsum(x * x, True) / MLA_QKD + EPS)
    return (x * r) * gain


def _prep_heads(qv, kvv, kr, ct, sa, sb, gqv, gkv):
    qs, ks, vs = [], [], []
    for h in range(MLA_HEADS):
        b = h * MLA_HP
        y = _head_norm(qv[:, b:b + MLA_HP], gqv)
        qs += [y[:, :128], _rope_blk(y[:, 128:], ct, sa, sb)]
        y = _head_norm(jnp.concatenate([kvv[:, b:b + 128], kr], axis=-1), gkv)
        ks += [y[:, :128], _rope_blk(y[:, 128:], ct, sa, sb)]
        vs.append(kvv[:, b + 128:b + 256])
    return jnp.concatenate(qs, axis=-1), jnp.concatenate(ks, axis=-1), jnp.concatenate(vs, axis=-1)


def _mla_front(hn, W, tabs, name):
    wide = MLA_HEADS * MLA_HP
    gq = W["mla_q_norm"] * (MLA_QKD ** -0.5 * LOG2E)

    def epilogue(acc, ct, sa, sb, gqa, gkva, wuq, wukv, gqv, gkv):
        cqn = _rms(acc[:, :MLA_Q_RANK], gqa).astype(BF16)
        ckvn = _rms(acc[:, MLA_Q_RANK:MLA_Q_RANK + MLA_KV_RANK], gkva).astype(BF16)
        q = jnp.concatenate([_dot(cqn, wuq[s]) for s in range(N_CHIPS)], axis=-1).astype(BF16)
        kv = jnp.concatenate([_dot(ckvn, wukv[s]) for s in range(N_CHIPS)], axis=-1).astype(BF16)
        qf, kf, vf = _prep_heads(q.astype(F32), kv.astype(F32), acc[:, MLA_IN_PAD - 128:], ct, sa, sb, gqv, gkv)
        return acc, cqn, ckvn, q, kv, qf, kf, vf

    return _mm_rows(hn, W["mla_w_in"], extras=list(tabs),
                    fulls=[W["mla_q_a_norm"], W["mla_kv_a_norm"], W["mla_w_uq"], W["mla_w_ukv"], gq, W["mla_k_norm"]],
                    outs=[(MLA_IN_PAD, F32), (MLA_Q_RANK, BF16), (MLA_KV_RANK, BF16), (wide, BF16), (wide, BF16),
                          (wide, BF16), (wide, BF16), (MLA_HEADS * MLA_VD, BF16)],
                    epilogue=epilogue, name=name, tm=256)


def _mla_prep_bwd(q, kv, proj, gq, gk, tabs, dqf, dkf, dvf, name):
    def fn(qv, kvv, kr, ct, sa, sb, dqv, dkv, dvv, gqv, gkv):
        qv, kvv, dqv, dkv = (t.astype(F32) for t in (qv, kvv, dqv, dkv))
        dqs, dkvs = [], []
        dkr = jnp.zeros_like(kr)
        dgq = jnp.zeros((1, MLA_HP), F32)
        dgk = jnp.zeros((1, MLA_HP), F32)
        for h in range(MLA_HEADS):
            b = h * MLA_HP
            dy = jnp.concatenate([dqv[:, b:b + 128], _rope_blk_bwd(dqv[:, b + 128:b + 256], ct, sa, sb)], axis=-1)
            dx, dg = _rms_bwd(qv[:, b:b + MLA_HP], dy, gqv, MLA_QKD, mxu=True)
            dqs.append(dx)
            dgq = dgq + _colsum(dg)
            dy = jnp.concatenate([dkv[:, b:b + 128], _rope_blk_bwd(dkv[:, b + 128:b + 256], ct, sa, sb)], axis=-1)
            dx, dg = _rms_bwd(jnp.concatenate([kvv[:, b:b + 128], kr], axis=-1), dy, gkv, MLA_QKD, mxu=True)
            dkvs += [dx[:, :128], dvv[:, h * MLA_VD:(h + 1) * MLA_VD]]
            dkr = dkr + dx[:, 128:]
            dgk = dgk + _colsum(dg)
        return jnp.concatenate(dqs, axis=-1), jnp.concatenate(dkvs, axis=-1), dkr, dgq, dgk
    w = MLA_HEADS * MLA_HP
    return _rows(fn, [q, kv, (proj, 128, 5), *tabs, dqf, dkf, dvf], [gq, gk],
                 [(w, BF16), (w, BF16), (128, F32)], [((1, MLA_HP), F32), ((1, MLA_HP), F32)], name=name, tile=128)


def _chunk_mask(qi, ki, tq, tk):
    shift = CHUNK.bit_length() - 1
    rq = lax.shift_right_arithmetic(qi * tq + lax.broadcasted_iota(jnp.int32, (tq, tk), 0), shift)
    ck = lax.shift_right_arithmetic(ki * tk + lax.broadcasted_iota(jnp.int32, (tq, tk), 1), shift)
    return ck <= rq


def _flash_fwd(qf, kf, vf, name):
    T = qf.shape[0]
    t = _pick(T, FLASH_T)
    n = T // t
    scale = MLA_QKD ** -0.5

    g = FLASH_HEADS

    def body(q_ref, k_ref, v_ref, o_ref, lse_ref, m_s, l_s, acc):
        qi = pl.program_id(1)
        m_s[...] = jnp.full_like(m_s, NEG)
        l_s[...] = jnp.zeros_like(l_s)
        acc[...] = jnp.zeros_like(acc)

        def step(kb, masked):
            rows = pl.ds(pl.multiple_of(kb * t, t), t)
            for h in range(g):
                hq, hv = slice(h * MLA_HP, (h + 1) * MLA_HP), slice(h * MLA_VD, (h + 1) * MLA_VD)
                s = _dot_nt(q_ref[:, hq], k_ref[rows, hq])
                if masked:
                    s = jnp.where(_chunk_mask(0, 0, t, t), s, NEG)
                m_prev = m_s[:, hv]
                m_new = jnp.maximum(m_prev, jnp.max(s, axis=-1, keepdims=True))
                alpha = jnp.exp2(m_prev - m_new)
                p = jnp.exp2(s - _widen(m_new, t))
                l_s[:, hv] = alpha * l_s[:, hv] + sum(p[:, i * 128:(i + 1) * 128] for i in range(t // 128))
                acc[:, hv] = acc[:, hv] * alpha + _dot(p.astype(BF16), v_ref[rows, hv])
                m_s[:, hv] = m_new

        @pl.loop(0, qi)
        def _(kb):
            step(kb, False)

        step(qi, True)
        for h in range(g):
            hv = slice(h * MLA_VD, (h + 1) * MLA_VD)
            l = jnp.sum(l_s[:, hv], axis=-1, keepdims=True)
            o_ref[:, hv] = acc[:, hv] / l
            lse_ref[:, hv] = m_s[:, hv] + jnp.log2(l)

    qmap = lambda h, i: (i, h)
    kmap = lambda h, i: (0, h)
    vec = pltpu.VMEM((t, g * MLA_VD), F32)
    return pl.pallas_call(
        body, name=name, grid=(MLA_HEADS // g, n),
        in_specs=[pl.BlockSpec((t, g * MLA_HP), qmap), pl.BlockSpec((T, g * MLA_HP), kmap),
                  pl.BlockSpec((T, g * MLA_VD), kmap)],
        out_specs=[pl.BlockSpec((t, g * MLA_VD), qmap), pl.BlockSpec((t, g * MLA_VD), qmap)],
        out_shape=[jax.ShapeDtypeStruct((T, MLA_HEADS * MLA_VD), F32),
                   jax.ShapeDtypeStruct((T, MLA_HEADS * MLA_VD), F32)],
        scratch_shapes=[vec, vec, vec],
        compiler_params=_cparams(("parallel", "arbitrary")),
    )(qf, kf, vf)


def _flash_bwd(qf, kf, vf, do16, lse, delta, name):
    T = qf.shape[0]
    t = _pick(T, FLASH_T)
    n = T // t
    scale = MLA_QKD ** -0.5

    def body(q_ref, k_ref, v_ref, do_ref, lse_ref, dl_ref, dq_out, dk_out, dv_out, dq_ref, dk_ref, dv_ref):
        kb = pl.program_id(1)

        @pl.when(kb == 0)
        def _():
            dq_ref[...] = jnp.zeros_like(dq_ref)

        dk_ref[...] = jnp.zeros_like(dk_ref)
        dv_ref[...] = jnp.zeros_like(dv_ref)
        k, v = k_ref[...], v_ref[...]

        def step(qb, masked):
            rows = pl.ds(pl.multiple_of(qb * t, t), t)
            q, dob = q_ref[rows, :], do_ref[rows, :]
            s = _dot_nt(q, k)
            if masked:
                s = jnp.where(_chunk_mask(0, 0, t, t), s, NEG)
            p = jnp.exp2(s - _widen(lse_ref[rows, :], t))
            ds = (p * (_dot_nt(dob, v) - _widen(dl_ref[rows, :], t))).astype(BF16)
            dv_ref[...] += _dot_tn(p.astype(BF16), dob)
            dk_ref[...] += _dot_tn(ds, q)
            dq_ref[rows, :] += _dot(ds, k)

        step(kb, True)

        @pl.loop(kb + 1, n)
        def _(qb):
            step(qb, False)

        dk_out[...] = (dk_ref[...] * (1.0 / LOG2E)).astype(BF16)
        dv_out[...] = dv_ref[...].astype(BF16)

        @pl.when(kb == n - 1)
        def _():
            dq_out[...] = (dq_ref[...] * scale).astype(BF16)

    qmap = lambda h, j: (0, h)
    kmap = lambda h, j: (j, h)
    return pl.pallas_call(
        body, name=name, grid=(MLA_HEADS, n),
        in_specs=[pl.BlockSpec((T, MLA_HP), qmap), pl.BlockSpec((t, MLA_HP), kmap), pl.BlockSpec((t, MLA_VD), kmap),
                  pl.BlockSpec((T, MLA_VD), qmap), pl.BlockSpec((T, MLA_VD), qmap), pl.BlockSpec((T, MLA_VD), qmap)],
        out_specs=[pl.BlockSpec((T, MLA_HP), qmap), pl.BlockSpec((t, MLA_HP), kmap), pl.BlockSpec((t, MLA_VD), kmap)],
        out_shape=[jax.ShapeDtypeStruct((T, MLA_HEADS * MLA_HP), BF16),
                   jax.ShapeDtypeStruct((T, MLA_HEADS * MLA_HP), BF16),
                   jax.ShapeDtypeStruct((T, MLA_HEADS * MLA_VD), BF16)],
        scratch_shapes=[pltpu.VMEM((T, MLA_HP), F32), pltpu.VMEM((t, MLA_HP), F32), pltpu.VMEM((t, MLA_VD), F32)],
        compiler_params=_cparams(("arbitrary", "arbitrary")),
    )(qf, kf, vf, do16, lse, delta)


MESH = pl.DeviceIdType.MESH
ANY = pl.BlockSpec(memory_space=pl.ANY)
_CHIP_FLIPS = ((1, 0), (0, 1), (1, 1))


def _place():
    return lax.axis_index("x"), lax.axis_index("y"), lax.axis_index("c")


def _other_chip(x, y, k):
    fx, fy = _CHIP_FLIPS[k]
    return ((1 - x) if fx else x), ((1 - y) if fy else y)


def _remote(src, dst, send_sems, recv_sems, k, to):
    return pltpu.make_async_remote_copy(src_ref=src, dst_ref=dst, send_sem=send_sems.at[k], recv_sem=recv_sems.at[k],
                                        device_id=to, device_id_type=MESH)


def _index(*vals):
    return jnp.stack(vals).astype(jnp.int32)


def _half(c, rows):
    return pl.ds(pl.multiple_of(c * rows, 16), rows)


def _gather_weights(parts, name, landed=None):
    n_w = len(parts)
    n_in = n_w if landed is None else 2 * n_w

    def body(*refs):
        ins, outs = refs[:n_w], refs[n_in:n_in + n_w]
        send_sems, recv_sems, local_sems = refs[n_in + n_w:]
        x, y, c = _place()
        j = 2 * x + y
        sibling = (x, y, 1 - c)
        chips = [_other_chip(x, y, k) for k in range(3)]
        pending = []
        for w in range(n_w):
            own = pltpu.make_async_copy(ins[w], outs[w].at[j], local_sems.at[w])
            own.start()
            pending.append(own)
        sent = []
        for w in range(n_w):
            if landed is not None:
                break
            r = _half(c, parts[w].shape[0] // 2)
            for k, (px, py) in enumerate(chips):
                cp = _remote(ins[w].at[r], outs[w].at[j, r], send_sems, recv_sems, 6 * w + k, (px, py, c))
                cp.start()
                sent.append(cp)
        for w in range(n_w):
            r = _half(c, parts[w].shape[0] // 2)
            for k, (px, py) in enumerate(chips):
                blk = outs[w].at[2 * px + py, r]
                if landed is None:
                    _remote(blk, blk, send_sems, recv_sems, 6 * w + k, (px, py, c)).wait_recv()
                cp = _remote(blk, blk, send_sems, recv_sems, 6 * w + 3 + k, sibling)
                cp.start()
                sent.append(cp)
        for w in range(n_w):
            r = _half(1 - c, parts[w].shape[0] // 2)
            for k, (px, py) in enumerate(chips):
                blk = outs[w].at[2 * px + py, r]
                _remote(blk, blk, send_sems, recv_sems, 6 * w + 3 + k, sibling).wait_recv()
        for cp in sent:
            cp.wait_send()
        for cp in pending:
            cp.wait()

    return pl.pallas_call(
        body, name=name, in_specs=[pl.BlockSpec(memory_space=pltpu.VMEM)] * n_w + [ANY] * (n_in - n_w),
        out_specs=[ANY] * n_w,
        out_shape=[jax.ShapeDtypeStruct((N_CHIPS, *p.shape), p.dtype) for p in parts],
        input_output_aliases={} if landed is None else {n_w + w: w for w in range(n_w)},
        scratch_shapes=[pltpu.SemaphoreType.DMA((6 * n_w,)), pltpu.SemaphoreType.DMA((6 * n_w,)),
                        pltpu.SemaphoreType.DMA((n_w,))],
        compiler_params=pltpu.CompilerParams(vmem_limit_bytes=VMEM_LIMIT),
    )(*parts, *(landed or []))


def _swap_halves(gs, name):
    n_w = len(gs)

    def body(*refs):
        g_refs, recv_refs = refs[:n_w], refs[n_w:2 * n_w]
        send_sems, recv_sems = refs[2 * n_w:]
        x, y, c = _place()
        sent = []
        for w in range(n_w):
            for jj in range(N_CHIPS):
                cp = _remote(g_refs[w].at[jj, 1 - c], recv_refs[w].at[jj], send_sems, recv_sems, N_CHIPS * w + jj,
                             (x, y, 1 - c))
                cp.start()
                sent.append(cp)
        for cp in sent:
            cp.wait()

    return pl.pallas_call(
        body, name=name, in_specs=[ANY] * n_w, out_specs=[ANY] * n_w,
        out_shape=[jax.ShapeDtypeStruct((N_CHIPS, *g.shape[2:]), g.dtype) for g in gs],
        scratch_shapes=[pltpu.SemaphoreType.DMA((N_CHIPS * n_w,)), pltpu.SemaphoreType.DMA((N_CHIPS * n_w,))],
    )(*gs)


def _pair_sum(g, recv, core, name):
    _, H, C = recv.shape
    tile = _pick(H, 256)

    def body(c_ref, own_ref, recv_ref, out_ref):
        out_ref[...] = (own_ref[...].astype(F32) + recv_ref[...].astype(F32)).astype(BF16)

    blk = pl.BlockSpec((None, tile, C), lambda jj, i, c: (jj, i, 0))
    return pl.pallas_call(
        body, name=name,
        grid_spec=pltpu.PrefetchScalarGridSpec(
            num_scalar_prefetch=1, grid=(N_CHIPS, H // tile),
            in_specs=[pl.BlockSpec((None, None, tile, C), lambda jj, i, c: (jj, c[0], i, 0)), blk],
            out_specs=blk),
        out_shape=jax.ShapeDtypeStruct((N_CHIPS, H, C), BF16),
        compiler_params=_cparams(("arbitrary", "arbitrary")),
    )(_index(core), g, recv)


def _chip_sum(g, recv, got, chip, core, name):
    _, H, C = recv.shape
    tile = _pick(H, 256)

    def body(s_ref, own_ref, recv_ref, g0_ref, g1_ref, g2_ref, out_ref):
        pair = own_ref[...].astype(F32) + recv_ref[...].astype(F32)
        out_ref[...] = ((pair + g0_ref[...].astype(F32)) + g1_ref[...].astype(F32)) + g2_ref[...].astype(F32)

    def got_spec(k):
        return pl.BlockSpec((None, tile, C), lambda i, s, k=k: (k, i, 0))

    return pl.pallas_call(
        body, name=name,
        grid_spec=pltpu.PrefetchScalarGridSpec(
            num_scalar_prefetch=1, grid=(H // tile,),
            in_specs=[pl.BlockSpec((None, None, tile, C), lambda i, s: (s[0], s[1], i, 0)),
                      pl.BlockSpec((None, tile, C), lambda i, s: (s[0], i, 0)), got_spec(0), got_spec(1), got_spec(2)],
            out_specs=pl.BlockSpec((None, tile, C), lambda i, s: (s[1], i, 0))),
        out_shape=jax.ShapeDtypeStruct((2, H, C), F32),
        compiler_params=_cparams(("arbitrary",)),
    )(_index(chip, core), g, recv, got, got, got)


def _share_halves(reds):
    n_w = len(reds)

    def body(*refs):
        out_refs = refs[n_w:2 * n_w]
        send_sems, recv_sems = refs[2 * n_w:]
        x, y, c = _place()
        sent = []
        for w in range(n_w):
            blk = out_refs[w].at[c]
            cp = _remote(blk, blk, send_sems, recv_sems, w, (x, y, 1 - c))
            cp.start()
            sent.append(cp)
        for cp in sent:
            cp.wait()

    return pl.pallas_call(
        body, name="grad_share_halves", in_specs=[ANY] * n_w, out_specs=[ANY] * n_w,
        out_shape=[jax.ShapeDtypeStruct(r.shape, r.dtype) for r in reds],
        input_output_aliases={w: w for w in range(n_w)},
        scratch_shapes=[pltpu.SemaphoreType.DMA((n_w,)), pltpu.SemaphoreType.DMA((n_w,))],
    )(*reds)


def _allsum_small(v, name):
    R, W = v.shape
    n_dev = 8
    vm = pl.BlockSpec(memory_space=pltpu.VMEM)

    def body(v_ref, out_ref, buf, send_sems, recv_sems):
        x, y, c = _place()
        me = 4 * x + 2 * y + c
        buf[me] = v_ref[...]
        sent = []
        for k in range(1, n_dev):
            peer = ((1 - x) if k & 4 else x, (1 - y) if k & 2 else y, (1 - c) if k & 1 else c)
            cp = _remote(v_ref, buf.at[me], send_sems, recv_sems, k - 1, peer)
            cp.start()
            sent.append(cp)
        for cp in sent:
            cp.wait_recv()
        for cp in sent:
            cp.wait_send()
        acc = buf[0]
        for q in range(1, n_dev):
            acc = acc + buf[q]
        out_ref[...] = acc

    return pl.pallas_call(
        body, name=name, in_specs=[vm], out_specs=vm, out_shape=jax.ShapeDtypeStruct((R, W), v.dtype),
        scratch_shapes=[pltpu.VMEM((n_dev, R, W), v.dtype), pltpu.SemaphoreType.DMA((n_dev - 1,)),
                        pltpu.SemaphoreType.DMA((n_dev - 1,))],
    )(v)


HBM = pl.BlockSpec(memory_space=pltpu.HBM)
SEM = pl.BlockSpec(memory_space=pltpu.SEMAPHORE)
_DATAFLOW = pltpu.SideEffectType.DATAFLOW_SIDE_EFFECTING


def _split_start(name, srcs, land_shapes, n_copies, copies, after=()):
    ns, nl = len(srcs), len(land_shapes)
    lands = [lax.empty(s.shape, s.dtype) for s in land_shapes]

    def body(*refs):
        outs = refs[ns + nl + len(after):]
        for cp in copies(refs[:ns], refs[ns:ns + nl], outs[0], outs[1]):
            cp.start()
        outs[-1][...] = jnp.zeros_like(outs[-1])

    sems = pltpu.SemaphoreType.DMA((n_copies,))
    res = pl.pallas_call(
        body, name=name, in_specs=[HBM] * (ns + nl) + [ANY] * len(after),
        out_specs=(SEM, SEM, *[HBM] * (ns + nl), pl.BlockSpec(memory_space=pltpu.VMEM)),
        out_shape=(sems, sems, *[pltpu.HBM(a.shape, a.dtype) for a in srcs],
                   *[pltpu.HBM(s.shape, s.dtype) for s in land_shapes], jax.ShapeDtypeStruct((8, 128), F32)),
        input_output_aliases={i: 2 + i for i in range(ns + nl)},
        compiler_params=pltpu.CompilerParams(has_side_effects=_DATAFLOW),
    )(*[pltpu.with_memory_space_constraint(a, pltpu.HBM) for a in [*srcs, *lands]], *after)
    return res[0], res[1], list(res[2:2 + ns]), list(res[2 + ns:2 + ns + nl]), res[-1]


def _split_wait(name, send_sems, recv_sems, srcs, lands, copies, after=()):
    ns, nl = len(srcs), len(lands)

    def body(*refs):
        for cp in copies(refs[:ns], refs[ns:ns + nl], refs[ns + nl], refs[ns + nl + 1]):
            cp.wait_send()
            cp.wait_recv()

    res = pl.pallas_call(
        body, name=name, in_specs=[HBM] * (ns + nl) + [SEM, SEM] + [ANY] * len(after), out_specs=[HBM] * (ns + nl),
        out_shape=[pltpu.HBM(a.shape, a.dtype) for a in [*srcs, *lands]],
        input_output_aliases={i: i for i in range(ns + nl)},
        compiler_params=pltpu.CompilerParams(has_side_effects=_DATAFLOW),
    )(*srcs, *lands, send_sems, recv_sems, *after)
    return list(res[ns:])


def _gather_copies(rows):
    def copies(src_refs, land_refs, send_sems, recv_sems):
        x, y, c = _place()
        j = 2 * x + y
        out = []
        for w in range(len(src_refs)):
            r = _half(c, rows[w] // 2)
            for k in range(3):
                px, py = _other_chip(x, y, k)
                out.append(_remote(src_refs[w].at[r], land_refs[w].at[j, r], send_sems, recv_sems, 3 * w + k, (px, py, c)))
        return out
    return copies


def _scatter_copies(src_refs, land_refs, send_sems, recv_sems):
    x, y, c = _place()
    j = 2 * x + y
    out = []
    for w in range(len(src_refs)):
        for k in range(3):
            px, py = _other_chip(x, y, k)
            pj = 2 * px + py
            out.append(_remote(src_refs[w].at[pj], land_refs[w].at[(j - pj + 4) % 4 - 1], send_sems, recv_sems, 3 * w + k,
                               (px, py, c)))
    return out


def _reduce_begin(grads, core, tag):
    names = list(grads)
    gs = [grads[k].reshape(N_CHIPS, 2, -1, grads[k].shape[-1]) for k in names]
    recvs = _swap_halves(gs, f"grad_swap_halves_{tag}")
    sums = [_pair_sum(g, r, core, f"pair_sum_{k}") for k, g, r in zip(names, gs, recvs)]
    return names, gs, recvs, sums


def _reduce_end(begun, gots, chip, core):
    names, gs, recvs, _ = begun
    return {k: _chip_sum(g, r, t, chip, core, f"chip_sum_{k}") for k, g, r, t in zip(names, gs, recvs, gots)}


def _got_shapes(sums):
    return [jax.ShapeDtypeStruct((3, *a.shape[1:]), a.dtype) for a in sums]


def _adamw(w, g, m, v, name, layers=1, layer=0, into=None):
    shape = w.shape
    cols = shape[-1]
    w3, m3, v3 = (t.reshape(layers, -1, cols) for t in (w, m, v))
    rows = w3.shape[1]
    tile = _pick(rows, 256) if rows % 8 == 0 else rows
    n_in = 4 + (0 if into is None else 4)
    stack_g = layers > 1

    def body(*refs):
        wv, gv, mv, vv = (r[...] for r in refs[:4])
        d_ref, m_ref, v_ref = refs[len(refs) - 3:]
        m2 = ADAM_B1 * mv + (1.0 - ADAM_B1) * gv
        v2 = ADAM_B2 * vv + (1.0 - ADAM_B2) * jnp.square(gv)
        m_hat = m2 / (1.0 - ADAM_B1 ** ADAM_STEP)
        v_hat = v2 / (1.0 - ADAM_B2 ** ADAM_STEP)
        if stack_g:
            refs[n_in][...] = gv
        d_ref[...] = -ADAM_LR * (m_hat / (jnp.sqrt(v_hat) + ADAM_EPS) + ADAM_WD * wv)
        m_ref[...] = m2
        v_ref[...] = v2

    n_out = 4 if stack_g else 3
    lay = pl.BlockSpec((None, tile, cols), lambda i: (layer, i, 0))
    out = jax.ShapeDtypeStruct((layers, rows, cols), F32)
    res = pl.pallas_call(
        body, name=name, grid=(rows // tile,),
        in_specs=[lay, pl.BlockSpec((tile, cols), lambda i: (i, 0)), lay, lay] + [ANY] * (n_in - 4),
        out_specs=[lay] * n_out, out_shape=[out] * n_out,
        input_output_aliases={} if into is None else {4 + k: k for k in range(4)},
        compiler_params=_cparams(("arbitrary",)),
    )(w3, g.reshape(rows, cols), m3, v3, *([] if into is None else [t.reshape(layers, rows, cols) for t in into]))
    res = tuple(t.reshape(shape) for t in res)
    return res if stack_g else (g.reshape(shape), *res)


ROW_F32, ROW_BF16 = (D_MODEL, F32), (D_MODEL, BF16)


def _res_norm(acc, h, gain):
    hh = h + acc
    return hh, _rms(hh, gain)


def _dx_norm_bwd(d, w, h, dres, gain, name, **kw):
    def epilogue(acc, hv, dr, g):
        dx, dg = _rms_bwd(hv, acc, g)
        return dr + dx, dr + dx, _colsum(dg)
    return _mm_rows(d, w, tb=True, extras=[h, dres], fulls=[gain], outs=[ROW_F32, ROW_BF16], accs=[((1, D_MODEL), F32)],
                    epilogue=epilogue, name=name, **kw)


def _tail_fwd(h1, hn2, p16, W, i, tag, next_gain=None, target=None):
    a = _mm(hn2, W["mlp_w1"][i], bblk=True, outs=[BF16], name=f"{tag}_mlp_w1",
            epilogue=lambda acc: (jnp.square(jnp.maximum(acc, 0.0)),))
    h2, hn3 = _mm_rows(a, W["mlp_w2"][i], extras=[h1], fulls=[W["ple_norm"][i:i + 1]], outs=[ROW_F32, ROW_BF16],
                       epilogue=_res_norm, name=f"{tag}_mlp_w2")
    def embed(acc, pv, h, wp):
        gate = _sigmoid(acc)
        ppv = jnp.concatenate([_dot(pv, wp[s]) for s in range(N_CHIPS)], axis=-1)
        return gate, ppv, h + gate * ppv

    if target is None:
        def gated(acc, pv, h, wp, gain):
            gate, ppv, hh = embed(acc, pv, h, wp)
            return hh, ppv, gate, _rms(hh, gain)
        h3, pp, gate, hn = _mm_rows(hn3, W["ple_gate_w"][i], extras=[p16[i], h2], fulls=[W["ple_proj_w"][i], next_gain],
                                    outs=[ROW_F32, ROW_BF16, ROW_BF16, ROW_BF16], epilogue=gated, name=f"{tag}_ple")
        return h3, hn, (h1, hn2, a, h2, hn3, gate, pp)

    def gated_loss(acc, pv, h, t, wp):
        gate, ppv, hh = embed(acc, pv, h, wp)
        e = hh - t
        return ppv, gate, e * (1.0 / D_MODEL), jnp.full((1, 128), 0.5 / D_MODEL, F32) * jnp.sum(e * e)
    pp, gate, dy, loss = _mm_rows(hn3, W["ple_gate_w"][i], extras=[p16[i], h2, target], fulls=[W["ple_proj_w"][i]],
                                  outs=[ROW_BF16, ROW_BF16, ROW_F32], accs=[((1, 128), F32)], epilogue=gated_loss,
                                  name=f"{tag}_ple")
    return dy, loss, (h1, hn2, a, h2, hn3, gate, pp)


def _tail_bwd(dh3, saved, p16, W, i, tag, after=()):
    h1, hn2, a, h2, hn3, gate, pp = saved

    def gate_bwd(d, g, ppv):
        g, ppv = g.astype(F32), ppv.astype(F32)
        return d * g, d * ppv * g * (1.0 - g)

    def dw(kind, name):
        return (kind, 1, 0, None)

    dpp, dgl = _rows(gate_bwd, [dh3, gate, pp], [], [(D_MODEL, BF16), (D_MODEL, BF16)], name=f"{tag}_ple_gate_bwd",
                     after=after)
    d_proj = _mm(p16[i], dpp, ta=True, outs=[BF16], dw=dw("cols", "ple_proj_w"), name=f"{tag}_d_ple_proj")
    d_gate = _mm(hn3, dgl, ta=True, outs=[BF16], dw=dw("rows", "ple_gate_w"), name=f"{tag}_d_ple_gate")
    dh2, dh2_16, d_ple_norm = _dx_norm_bwd(dgl, W["ple_gate_w"][i], h2, dh3, W["ple_norm"][i:i + 1],
                                           f"{tag}_ple_gate_dx")
    d_w2 = _mm(a, dh2_16, ta=True, outs=[BF16], dw=dw("rows", "mlp_w2"), name=f"{tag}_d_mlp_w2")
    dz = _mm(dh2_16, W["mlp_w2"][i], tb=True, extras=[a], outs=[BF16], name=f"{tag}_mlp_w2_dx",
             epilogue=lambda acc, av: (acc * (2.0 * jnp.sqrt(av.astype(F32))),))
    d_w1 = _mm(hn2, dz, ta=True, outs=[BF16], dw=dw("cols", "mlp_w1"), name=f"{tag}_d_mlp_w1")
    dh1, dh1_16, d_mlp_norm = _dx_norm_bwd(dz, W["mlp_w1"][i], h1, dh2, W["mlp_norm"][i:i + 1], f"{tag}_mlp_w1_dx",
                                           bblk=True)
    big = {f"mlp_w1_{i}": d_w1, f"mlp_w2_{i}": d_w2, f"ple_gate_w_{i}": d_gate, f"ple_proj_w_{i}": d_proj}
    return dh1, dh1_16, big, dict(mlp_norm=d_mlp_norm, ple_norm=d_ple_norm)


def _ret_layer_fwd(h0, W, tabs, after=()):
    hn = _rows(lambda x, g: (_rms(x, g),), [h0], [W["mix_norm"][0:1]], [(D_MODEL, BF16)], name="ret_mix_norm",
               after=after)[0]
    proj = _mm(hn, W["ret_w_in"], bblk=True, outs=[BF16], name="ret_w_in")
    out, states = _ret_fwd(proj, tabs, "ret_scan")
    y = _ret_gate(out, proj, W["ret_gn"], "ret_gate")
    h1, hn2 = _mm_rows(y, W["ret_w_out"], extras=[h0], fulls=[W["mlp_norm"][0:1]], outs=[ROW_F32, ROW_BF16],
                       epilogue=_res_norm, name="ret_w_out")
    return h1, hn2, (h0, hn, proj, out, states, y)


def _d_ret_w_out(dh1_16, saved):
    return _mm(saved[5], dh1_16, ta=True, outs=[BF16], dw=("rows", 1, 0, None), name="d_ret_w_out")


def _ret_layer_bwd(dh1, dh1_16, saved, W, tabs, after=(), on_grads=None, d_w_out=None):
    h0, hn, proj, out, states, y = saved
    d_w_out = _d_ret_w_out(dh1_16, saved) if d_w_out is None else d_w_out
    dy = _mm(dh1_16, W["ret_w_out"], tb=True, name="ret_w_out_dx", after=after)
    dout, dproj, d_gn = _ret_gate_bwd(out, proj, W["ret_gn"], dy, "ret_gate_bwd")
    dproj = _ret_bwd(proj, states, dout, dproj, tabs, "ret_scan_bwd")
    d_w_in = _mm(hn, dproj, ta=True, outs=[BF16], dw=("cols", 1, 0, None), name="d_ret_w_in")
    big = dict(ret_w_in=d_w_in, ret_w_out=d_w_out)
    later = () if on_grads is None else on_grads(big)
    dh0, _, d_mix = _dx_norm_bwd(dproj, W["ret_w_in"], h0, dh1, W["mix_norm"][0:1], "ret_w_in_dx", bblk=True, tm=256,
                                 after=later)
    return dh0, big, dict(mix_norm=d_mix, ret_gn=d_gn)


def _mla_layer_fwd(h0, hn, W, tabs):
    proj, cqn, ckvn, q, kv, qf, kf, vf = _mla_front(hn, W, tabs, "mla_front")
    o, lse = _flash_fwd(qf, kf, vf, "mla_flash")
    h1, hn2 = _mm_rows(o, W["mla_w_out"], extras=[h0], fulls=[W["mlp_norm"][1:2]], outs=[ROW_F32, ROW_BF16],
                       epilogue=_res_norm, name="mla_w_out")
    return h1, hn2, (h0, hn, proj, cqn, ckvn, q, kv, qf, kf, vf, o, lse)


def _mla_layer_bwd(dh1, dh1_16, saved, W, tabs):
    h0, hn, proj, cqn, ckvn, q, kv, qf, kf, vf, o, lse = saved
    d_w_out = _mm(o, dh1_16, ta=True, outs=[BF16], dw=("rows", 1, 0, None), name="d_mla_w_out")
    def with_delta(acc, ov):
        parts = []
        for h in range(MLA_HEADS):
            sl = slice(h * MLA_VD, (h + 1) * MLA_VD)
            d = jnp.sum(acc[:, sl] * ov[:, sl], axis=-1, keepdims=True)
            parts.append(jnp.broadcast_to(d, (d.shape[0], MLA_VD)))
        return jnp.concatenate(parts, axis=-1), acc

    delta, do16 = _mm_rows(dh1_16, W["mla_w_out"], tb=True, extras=[o], outs=[ROW_F32, ROW_BF16], epilogue=with_delta,
                           name="mla_w_out_dx")
    dqf, dkf, dvf = _flash_bwd(qf, kf, vf, do16, lse, delta, "mla_flash_bwd")
    dq, dkv, dkr, d_gq, d_gk = _mla_prep_bwd(q, kv, proj, W["mla_q_norm"], W["mla_k_norm"], tabs, dqf, dkf, dvf,
                                             "mla_prep_bwd")
    d_w_uq = _mm(cqn, dq, ta=True, outs=[BF16], dw=("cols", 1, 0, None), name="d_mla_w_uq")
    dcqn = _mm(dq, W["mla_w_uq"], tb=True, bblk=True, name="mla_w_uq_dx")
    d_w_ukv = _mm(ckvn, dkv, ta=True, outs=[BF16], dw=("cols", 1, 0, None), name="d_mla_w_ukv")
    dckvn = _mm(dkv, W["mla_w_ukv"], tb=True, bblk=True, name="mla_w_ukv_dx")

    def low_rank_bwd(pv, dcq, dckv, dkr_v, gq, gkv):
        dxq, dgq = _rms_bwd(pv[:, :MLA_Q_RANK], dcq, gq)
        dxkv, dgkv = _rms_bwd(pv[:, MLA_Q_RANK:MLA_Q_RANK + MLA_KV_RANK], dckv, gkv)
        return jnp.concatenate([dxq, dxkv, dkr_v], axis=-1), _colsum(dgq), _colsum(dgkv)

    dproj, d_gqa, d_gkva = _rows(low_rank_bwd, [proj, dcqn, dckvn, dkr], [W["mla_q_a_norm"], W["mla_kv_a_norm"]],
                                 [(MLA_IN_PAD, BF16)], [((1, MLA_Q_RANK), F32), ((1, MLA_KV_RANK), F32)],
                                 name="mla_low_rank_norm_bwd")
    d_w_in = _mm(hn, dproj, ta=True, outs=[BF16], dw=("rows", 1, 0, None), name="d_mla_w_in")
    dh0, dh0_16, d_mix = _dx_norm_bwd(dproj, W["mla_w_in"], h0, dh1, W["mix_norm"][1:2], "mla_w_in_dx")
    return (dh0, dh0_16, dict(mla_w_in=d_w_in, mla_w_uq=d_w_uq, mla_w_ukv=d_w_ukv, mla_w_out=d_w_out),
            dict(mix_norm=d_mix, mla_q_a_norm=d_gqa, mla_kv_a_norm=d_gkva, mla_q_norm=d_gq, mla_k_norm=d_gk))


def _local_step(x, p16, target, W):
    T = x.shape[0]
    ret_tabs, mla_tabs = _ret_tables(T), _mla_tables(T)
    h1, hn, s_ret = _ret_layer_fwd(x, W, ret_tabs)
    h3, hn, s_tail0 = _tail_fwd(h1, hn, p16, W, 0, "l0", next_gain=W["mix_norm"][1:2])
    h4, hn, s_mla = _mla_layer_fwd(h3, hn, W, mla_tabs)
    dy, loss, s_tail1 = _tail_fwd(h4, hn, p16, W, 1, "l1", target=target)
    dh4, dh4_16, g_t1, n_t1 = _tail_bwd(dy, s_tail1, p16, W, 1, "l1")
    dh3, _, g_mla, n_mla = _mla_layer_bwd(dh4, dh4_16, s_mla, W, mla_tabs)
    dh1, dh1_16, g_t0, n_t0 = _tail_bwd(dh3, s_tail0, p16, W, 0, "l0")
    dx, g_ret, n_ret = _ret_layer_bwd(dh1, dh1_16, s_ret, W, ret_tabs)
    return loss, dx, {**g_ret, **g_t0, **g_mla, **g_t1}, _small_grads(n_ret, n_t0, n_mla, n_t1)


def _small_grads(n_ret, n_t0, n_mla, n_t1):
    return dict(
        mix_norm=jnp.concatenate([n_ret["mix_norm"], n_mla["mix_norm"]], axis=0),
        mlp_norm=jnp.concatenate([n_t0["mlp_norm"], n_t1["mlp_norm"]], axis=0),
        ple_norm=jnp.concatenate([n_t0["ple_norm"], n_t1["ple_norm"]], axis=0),
        ret_gn=n_ret["ret_gn"], mla_q_a_norm=n_mla["mla_q_a_norm"], mla_kv_a_norm=n_mla["mla_kv_a_norm"],
        mla_q_norm=n_mla["mla_q_norm"], mla_k_norm=n_mla["mla_k_norm"])


_ORDER = ("mix_norm", "ret_w_in", "ret_gn", "ret_w_out", "mla_w_in", "mla_q_a_norm", "mla_kv_a_norm", "mla_w_uq",
          "mla_w_ukv", "mla_q_norm", "mla_k_norm", "mla_w_out", "mlp_norm", "mlp_w1", "mlp_w2", "ple_norm",
          "ple_gate_w", "ple_proj_w")
_TWO_LAYER = ("mlp_w1", "mlp_w2", "ple_gate_w", "ple_proj_w")
HEADS_PER_CHIP = MLA_HEADS // N_CHIPS
GAIN_ROWS = 32


def _travel_parts(w):
    uq = jnp.pad(w["mla_w_uq"][0].reshape(MLA_Q_RANK, HEADS_PER_CHIP, MLA_QKD), ((0, 0), (0, 0), (0, MLA_HP - MLA_QKD)))
    parts = {"ret_w_in": w["ret_w_in"][0], "ret_w_out": w["ret_w_out"][0]}
    for k in _TWO_LAYER:
        parts[k + "_0"] = w[k][0]
    parts["mla_w_in"] = jnp.pad(w["mla_w_in"][0], ((0, 0), (0, MLA_IN_PAD - MLA_IN)))
    parts["mla_w_uq"] = uq.reshape(MLA_Q_RANK, HEADS_PER_CHIP * MLA_HP)
    parts["mla_w_ukv"] = w["mla_w_ukv"][0]
    parts["mla_w_out"] = w["mla_w_out"][0]
    for k in _TWO_LAYER:
        parts[k + "_1"] = w[k][1]
    gains = jnp.concatenate([_pad_row(w["ret_gn"]), _pad_row(w["mla_q_a_norm"]), _pad_row(w["mla_kv_a_norm"]),
                             jnp.zeros((GAIN_ROWS - 3, PACK_W), F32)], axis=0)
    return {"gains": gains, **{k: v.astype(BF16) for k, v in parts.items()}}


def _full_weights(full):
    rows = lambda a: a.reshape(-1, a.shape[-1])
    W = {k: full[k] for k in ("ret_w_in", "mla_w_uq", "mla_w_ukv") if k in full}
    for k in ("ret_w_out", "mla_w_in", "mla_w_out"):
        if k in full:
            W[k] = rows(full[k])
    for k, by_rows in (("mlp_w1", False), ("ple_proj_w", False), ("mlp_w2", True), ("ple_gate_w", True)):
        layers = [full.get(f"{k}_{i}") for i in range(2)]
        W[k] = [rows(t) if (by_rows and t is not None) else t for t in layers]
    return W


def _shard_grad(name, red, shape):
    if name == "mla_w_in":
        red = red.reshape(-1, MLA_IN_PAD)[:, :MLA_IN]
    elif name == "mla_w_uq":
        red = red.reshape(MLA_Q_RANK, HEADS_PER_CHIP, MLA_HP)[:, :, :MLA_QKD]
    return red.reshape(shape)


def _pad_row(v):
    v = v.reshape(1, -1)
    return jnp.pad(v, ((0, 0), (0, PACK_W - v.shape[1])))


def kernel(x, p, mix_norm, ret_w_in, ret_gn, ret_w_out, mla_w_in, mla_q_a_norm, mla_kv_a_norm, mla_w_uq, mla_w_ukv, mla_q_norm, mla_k_norm, mla_w_out, mlp_norm, mlp_w1, mlp_w2, ple_norm, ple_gate_w, ple_proj_w, loss_target, m_mix_norm, m_ret_w_in, m_ret_gn, m_ret_w_out, m_mla_w_in, m_mla_q_a_norm, m_mla_kv_a_norm, m_mla_w_uq, m_mla_w_ukv, m_mla_q_norm, m_mla_k_norm, m_mla_w_out, m_mlp_norm, m_mlp_w1, m_mlp_w2, m_ple_norm, m_ple_gate_w, m_ple_proj_w, v_mix_norm, v_ret_w_in, v_ret_gn, v_ret_w_out, v_mla_w_in, v_mla_q_a_norm, v_mla_kv_a_norm, v_mla_w_uq, v_mla_w_ukv, v_mla_q_norm, v_mla_k_norm, v_mla_w_out, v_mlp_norm, v_mlp_w1, v_mlp_w2, v_ple_norm, v_ple_gate_w, v_ple_proj_w):
    w = dict(mix_norm=mix_norm, ret_w_in=ret_w_in, ret_gn=ret_gn, ret_w_out=ret_w_out, mla_w_in=mla_w_in,
             mla_q_a_norm=mla_q_a_norm, mla_kv_a_norm=mla_kv_a_norm, mla_w_uq=mla_w_uq, mla_w_ukv=mla_w_ukv,
             mla_q_norm=mla_q_norm, mla_k_norm=mla_k_norm, mla_w_out=mla_w_out, mlp_norm=mlp_norm, mlp_w1=mlp_w1,
             mlp_w2=mlp_w2, ple_norm=ple_norm, ple_gate_w=ple_gate_w, ple_proj_w=ple_proj_w)
    m = dict(mix_norm=m_mix_norm, ret_w_in=m_ret_w_in, ret_gn=m_ret_gn, ret_w_out=m_ret_w_out, mla_w_in=m_mla_w_in,
             mla_q_a_norm=m_mla_q_a_norm, mla_kv_a_norm=m_mla_kv_a_norm, mla_w_uq=m_mla_w_uq, mla_w_ukv=m_mla_w_ukv,
             mla_q_norm=m_mla_q_norm, mla_k_norm=m_mla_k_norm, mla_w_out=m_mla_w_out, mlp_norm=m_mlp_norm,
             mlp_w1=m_mlp_w1, mlp_w2=m_mlp_w2, ple_norm=m_ple_norm, ple_gate_w=m_ple_gate_w, ple_proj_w=m_ple_proj_w)
    v = dict(mix_norm=v_mix_norm, ret_w_in=v_ret_w_in, ret_gn=v_ret_gn, ret_w_out=v_ret_w_out, mla_w_in=v_mla_w_in,
             mla_q_a_norm=v_mla_q_a_norm, mla_kv_a_norm=v_mla_kv_a_norm, mla_w_uq=v_mla_w_uq, mla_w_ukv=v_mla_w_ukv,
             mla_q_norm=v_mla_q_norm, mla_k_norm=v_mla_k_norm, mla_w_out=v_mla_w_out, mlp_norm=v_mlp_norm,
             mlp_w1=v_mlp_w1, mlp_w2=v_mlp_w2, ple_norm=v_ple_norm, ple_gate_w=v_ple_gate_w, ple_proj_w=v_ple_proj_w)
    xi, yi, ci = _place()
    chip = 2 * xi + yi
    n = N_CHIPS

    parts = _travel_parts(w)
    first = ("gains", "ret_w_in", "ret_w_out")
    mid = [k + "_0" for k in _TWO_LAYER]
    last = [k for k in parts if k not in first and k not in mid]
    full = dict(zip(first, _gather_weights([parts[k] for k in first], "gather_first")))

    def gather_behind(names, tag, after):
        copies = _gather_copies([parts[k].shape[0] for k in names])
        started = _split_start(f"gather_{tag}_start", [parts[k] for k in names],
                               [jax.ShapeDtypeStruct((n, *parts[k].shape), BF16) for k in names], 3 * len(names),
                               copies, after=after)

        def arrive(after):
            landed = _split_wait(f"gather_{tag}_wait", *started[:4], copies, after=after)
            full.update(zip(names, _gather_weights([parts[k] for k in names], f"gather_{tag}_finish", landed=landed)))
            W.update(_full_weights(full))
        return started[4], arrive

    mid_token, mid_arrive = gather_behind(mid, "mid", [full["ret_w_in"]])
    g_token, last_arrive = gather_behind(last, "last", [mid_token])
    gains = full["gains"]
    W = dict(mix_norm=mix_norm, mlp_norm=mlp_norm, ple_norm=ple_norm,
             mla_q_norm=jnp.pad(mla_q_norm, ((0, 0), (0, MLA_HP - MLA_QKD))),
             mla_k_norm=jnp.pad(mla_k_norm, ((0, 0), (0, MLA_HP - MLA_QKD))),
             ret_w_in=full["ret_w_in"], ret_w_out=full["ret_w_out"].reshape(-1, D_MODEL),
             ret_gn=gains[:, 0, :RET_HEADS * 128].reshape(n, RET_HEADS, 128).transpose(1, 0, 2).reshape(RET_HEADS, RET_DV),
             mla_q_a_norm=gains[:, 1, :MLA_Q_RANK // n].reshape(1, MLA_Q_RANK),
             mla_kv_a_norm=gains[:, 2, :MLA_KV_RANK // n].reshape(1, MLA_KV_RANK))
    x0, p16, target = x[0], p[:, 0].astype(BF16), loss_target[0]
    T = x0.shape[0]
    ret_tabs, mla_tabs = _ret_tables(T), _mla_tables(T)

    h1, hn, s_ret = _ret_layer_fwd(x0, W, ret_tabs, after=[g_token])
    mid_arrive([h1])
    h3, hn, s_tail0 = _tail_fwd(h1, hn, p16, W, 0, "l0", next_gain=W["mix_norm"][1:2])
    last_arrive([h3])
    h4, hn, s_mla = _mla_layer_fwd(h3, hn, W, mla_tabs)
    dy, loss, s_tail1 = _tail_fwd(h4, hn, p16, W, 1, "l1", target=target)

    dh4, dh4_16, g_t1, n_t1 = _tail_bwd(dy, s_tail1, p16, W, 1, "l1")
    dh3, _, g_mla, n_mla = _mla_layer_bwd(dh4, dh4_16, s_mla, W, mla_tabs)
    beg_a = _reduce_begin({**g_mla, **g_t1}, ci, "a")
    a_send, a_recv, a_src, a_land, a_token = _split_start(
        "scatter_a_start", beg_a[3], _got_shapes(beg_a[3]), 3 * len(beg_a[3]), _scatter_copies)
    dh1, dh1_16, g_t0, n_t0 = _tail_bwd(dh3, s_tail0, p16, W, 0, "l0", after=[a_token])
    d_ret_w_out = _d_ret_w_out(dh1_16, s_ret)
    beg_b = _reduce_begin({**g_t0, "ret_w_out": d_ret_w_out}, ci, "b")
    b_send, b_recv, b_src, b_land, b_token = _split_start(
        "scatter_b_start", beg_b[3], _got_shapes(beg_b[3]), 3 * len(beg_b[3]), _scatter_copies)
    stage_c = {}

    def start_c(g_ret):
        beg = _reduce_begin({"ret_w_in": g_ret["ret_w_in"]}, ci, "c")
        stage_c["beg"] = beg
        stage_c["st"] = _split_start("scatter_c_start", beg[3], _got_shapes(beg[3]), 3 * len(beg[3]), _scatter_copies)
        return [stage_c["st"][4]]

    dx, _, n_ret = _ret_layer_bwd(dh1, dh1_16, s_ret, W, ret_tabs, after=[b_token], on_grads=start_c,
                                  d_w_out=d_ret_w_out)
    got_a = _split_wait("scatter_a_wait", a_send, a_recv, a_src, a_land, _scatter_copies, after=[dx])
    got_b = _split_wait("scatter_b_wait", b_send, b_recv, b_src, b_land, _scatter_copies, after=[dx])
    got_c = _split_wait("scatter_c_wait", *stage_c["st"][:4], _scatter_copies, after=[dx])
    red = {**_reduce_end(beg_a, got_a, chip, ci), **_reduce_end(beg_b, got_b, chip, ci),
           **_reduce_end(stage_c["beg"], got_c, chip, ci)}
    red = dict(zip(red, _share_halves(list(red.values()))))
    gs = _small_grads(n_ret, n_t0, n_mla, n_t1)
    small_g = jnp.concatenate([
        gs["mix_norm"], gs["mlp_norm"], gs["ple_norm"], gs["ret_gn"].reshape(2, PACK_W), _pad_row(gs["mla_q_a_norm"]),
        _pad_row(gs["mla_kv_a_norm"]), _pad_row(gs["mla_q_norm"][:, :MLA_QKD]), _pad_row(gs["mla_k_norm"][:, :MLA_QKD]),
        _pad_row(loss[:, :1]), jnp.zeros((3, PACK_W), F32)], axis=0)
    tot = _allsum_small(small_g, "sum_small_grads")
    gn_all = tot[6:8].reshape(RET_HEADS, n, -1)
    g_small = dict(
        mix_norm=tot[0:2], mlp_norm=tot[2:4], ple_norm=tot[4:6],
        ret_gn=lax.dynamic_index_in_dim(gn_all, chip, axis=1, keepdims=False),
        mla_q_a_norm=lax.dynamic_index_in_dim(tot[8, :MLA_Q_RANK].reshape(n, -1), chip, axis=0, keepdims=True),
        mla_kv_a_norm=lax.dynamic_index_in_dim(tot[9, :MLA_KV_RANK].reshape(n, -1), chip, axis=0, keepdims=True),
        mla_q_norm=tot[10:11, :MLA_QKD], mla_k_norm=tot[11:12, :MLA_QKD])
    loss_out = tot[12, 0]

    outs = []
    for k in _ORDER:
        if k in _TWO_LAYER:
            res = None
            for i in (1, 0):
                res = _adamw(w[k], red[f"{k}_{i}"], m[k], v[k], f"adamw_{k}_{i}", layers=2, layer=i, into=res)
        elif k in red:
            res = _adamw(w[k], _shard_grad(k, red[k], w[k].shape), m[k], v[k], f"adamw_{k}")
        else:
            res = _adamw(w[k], g_small[k], m[k], v[k], f"adamw_{k}")
        outs.append(res)
    return (loss_out, dx[None], *[o[0] for o in outs], *[o[1] for o in outs], *[o[2] for o in outs],
            *[o[3] for o in outs])
```

```python
import functools

import jax
import jax.numpy as jnp
import numpy as np
from jax import lax
from jax.experimental import pallas as pl
from jax.experimental.pallas import tpu as pltpu

F32 = jnp.float32
BF16 = jnp.bfloat16

EPS = 1e-6
D_MODEL = 1024
CHUNK = 64
ROPE_THETA = 10000.0
RET_HEADS = 4
RET_DK = 256
RET_DV = 512
RET_GROUP = 1
RET_BLOCK = 256
MLA_HEADS = 8
MLA_NOPE = 128
MLA_ROPE = 64
MLA_QKD = 192
MLA_VD = 128
MLA_HP = 256
MLA_Q_RANK = 384
MLA_KV_RANK = 256
MLA_IN = 704
MLA_IN_PAD = 768
D_FF = 4096
PLE_DIM = 256
N_CHIPS = 4

ADAM_LR = 0.001
ADAM_B1 = 0.9
ADAM_B2 = 0.999
ADAM_EPS = 1e-08
ADAM_WD = 0.01
ADAM_STEP = 10

VMEM_LIMIT = 56 * 1024 * 1024
PACK_W = 1024
NEG = -1e30
LOG2E = 1.4426950408889634
FLASH_T = 512
FLASH_HEADS = 2
MM_SUB_ROWS = 256


def _cparams(sem=None):
    return pltpu.CompilerParams(dimension_semantics=sem, vmem_limit_bytes=VMEM_LIMIT)


def _pick(dim, pref):
    if dim <= pref:
        return dim
    t = pref
    while dim % t:
        t //= 2
    return t


def _mm(a, b, *, name, ta=False, tb=False, bblk=False, outs=None, extras=(), epilogue=None, dw=None,
        tm=1024, tn=512, after=()):
    if ta:
        K, M = a.shape
    else:
        M, K = a.shape
    if bblk and tb:
        nb, N, Kq = b.shape
        assert nb * Kq == K
    elif bblk:
        nb, Kb, Nq = b.shape
        N = nb * Nq
        assert Kb == K
    else:
        N = b.shape[0] if tb else b.shape[1]
    tn = _pick(Nq if (bblk and not tb) else N, tn)
    if dw is not None and dw[0] == "cols":
        tn = _pick(N // N_CHIPS, tn)
    tm = _pick(M // N_CHIPS if (dw is not None and dw[0] == "rows") else M, tm)
    grid = (M // tm, N // tn)

    a_spec = pl.BlockSpec((K, tm), lambda i, j: (0, i)) if ta else pl.BlockSpec((tm, K), lambda i, j: (i, 0))
    if bblk and tb:
        b_spec = pl.BlockSpec((nb, tn, Kq), lambda i, j: (0, j, 0))
    elif bblk:
        npb = Nq // tn
        b_spec = pl.BlockSpec((None, K, tn), lambda i, j: (j // npb, 0, j % npb))
    elif tb:
        b_spec = pl.BlockSpec((tn, K), lambda i, j: (j, 0))
    else:
        b_spec = pl.BlockSpec((K, tn), lambda i, j: (0, j))
    in_specs = [a_spec, b_spec] + [pl.BlockSpec((tm, tn), lambda i, j: (i, j)) for _ in extras]
    args = [a, b, *extras]
    aliases = {}
    if outs is None:
        outs = [F32]
    if dw is None:
        o_specs = [pl.BlockSpec((tm, tn), lambda i, j: (i, j)) for _ in outs]
        o_shapes = [jax.ShapeDtypeStruct((M, N), dt) for dt in outs]
    else:
        kind, layers, layer, into = dw
        if kind == "cols":
            per = (N // N_CHIPS) // tn
            o_specs = [pl.BlockSpec((None, None, tm, tn), lambda i, j: (j // per, layer, i, j % per))]
            o_shapes = [jax.ShapeDtypeStruct((N_CHIPS, layers, M, N // N_CHIPS), outs[0])]
        else:
            per = (M // N_CHIPS) // tm
            o_specs = [pl.BlockSpec((None, None, tm, tn), lambda i, j: (i // per, layer, i % per, j))]
            o_shapes = [jax.ShapeDtypeStruct((N_CHIPS, layers, M // N_CHIPS, N), outs[0])]
        if into is not None:
            aliases = {len(args): 0}
            in_specs.append(pl.BlockSpec(memory_space=pl.ANY))
            args.append(into)
    for t in after:
        in_specs.append(pl.BlockSpec(memory_space=pl.ANY))
        args.append(t)
    n_e, n_o = len(extras), len(outs)

    sub = _pick(tm, MM_SUB_ROWS)

    def body(a_ref, b_ref, *rest):
        e_refs, o_refs = rest[:n_e], rest[len(rest) - n_o:]
        for r0 in range(0, tm, sub):
            rows = slice(r0, r0 + sub)
            av = (a_ref[:, rows] if ta else a_ref[rows, :]).astype(BF16)
            if bblk and tb:
                acc = _dot_nt(av[:, :Kq], b_ref[0].astype(BF16))
                for s in range(1, nb):
                    acc = acc + _dot_nt(av[:, s * Kq:(s + 1) * Kq], b_ref[s].astype(BF16))
            elif ta:
                acc = _dot_tn(av, b_ref[...].astype(BF16))
            elif tb:
                acc = _dot_nt(av, b_ref[...].astype(BF16))
            else:
                acc = _dot(av, b_ref[...].astype(BF16))
            vals = (acc,) if epilogue is None else epilogue(acc, *[e[rows, :] for e in e_refs])
            for o, v in zip(o_refs, vals):
                o[rows, :] = v.astype(o.dtype)

    res = pl.pallas_call(
        body, name=name, grid=grid, in_specs=in_specs, out_specs=o_specs, out_shape=o_shapes,
        input_output_aliases=aliases, compiler_params=_cparams(("parallel", "arbitrary")),
    )(*args)
    return res[0] if n_o == 1 else res


def _mm_rows(a, b, *, name, epilogue, outs, tb=False, bblk=False, extras=(), fulls=(), accs=(), tm=512, after=()):
    M, K = a.shape
    tm = _pick(M, tm)
    sub = _pick(tm, MM_SUB_ROWS)
    nb = b.shape[0] if bblk else 1
    n_e, n_f, n_o, n_a = len(extras), len(fulls), len(outs), len(accs)
    n_in = 2 + n_e + n_f + len(after)

    def whole(t):
        return pl.BlockSpec(t.shape, lambda i, nd=t.ndim: (0,) * nd)

    in_specs = [pl.BlockSpec((tm, K), lambda i: (i, 0)), whole(b)]
    in_specs += [pl.BlockSpec((tm, e.shape[1]), lambda i: (i, 0)) for e in extras] + [whole(f) for f in fulls]
    in_specs += [pl.BlockSpec(memory_space=pl.ANY) for _ in after]
    out_specs = [pl.BlockSpec((tm, w), lambda i: (i, 0)) for w, _ in outs] + [pl.BlockSpec(s, lambda i: (0, 0)) for s, _ in accs]
    out_shape = [jax.ShapeDtypeStruct((M, w), dt) for w, dt in outs] + [jax.ShapeDtypeStruct(s, dt) for s, dt in accs]

    def body(a_ref, b_ref, *rest):
        e_refs, f_refs = rest[:n_e], rest[n_e:n_e + n_f]
        o_refs, acc_refs = rest[n_in - 2:n_in - 2 + n_o], rest[n_in - 2 + n_o:]
        fv = [f[...] for f in f_refs]
        totals = None
        for r0 in range(0, tm, sub):
            rows = slice(r0, r0 + sub)
            av = a_ref[rows, :].astype(BF16)
            if bblk and tb:
                kq = K // nb
                acc = _dot_nt(av[:, :kq], b_ref[0])
                for s in range(1, nb):
                    acc = acc + _dot_nt(av[:, s * kq:(s + 1) * kq], b_ref[s])
            elif bblk:
                acc = jnp.concatenate([_dot(av, b_ref[s]) for s in range(nb)], axis=-1)
            elif tb:
                acc = _dot_nt(av, b_ref[...])
            else:
                acc = _dot(av, b_ref[...])
            vals = epilogue(acc, *[e[rows, :] for e in e_refs], *fv)
            for o, v in zip(o_refs, vals[:n_o]):
                o[rows, :] = v.astype(o.dtype)
            part = vals[n_o:]
            totals = part if totals is None else [t + p for t, p in zip(totals, part)]
        first_step = pl.program_id(0) == 0
        for o, v in zip(acc_refs, totals):
            @pl.when(first_step)
            def _(o=o, v=v):
                o[...] = v.astype(o.dtype)

            @pl.when(jnp.logical_not(first_step))
            def _(o=o, v=v):
                o[...] += v.astype(o.dtype)

    return pl.pallas_call(
        body, name=name, grid=(M // tm,), in_specs=in_specs, out_specs=out_specs, out_shape=out_shape,
        compiler_params=_cparams(("arbitrary",)),
    )(a, b, *extras, *fulls, *after)


def _rows(fn, rows, fulls, outs, accs=(), *, name, tile=512, after=()):
    first = rows[0][0] if isinstance(rows[0], tuple) else rows[0]
    T = first.shape[0]
    tile = _pick(T, tile)
    in_specs, args = [], []
    for r in rows:
        if isinstance(r, tuple):
            arr, w, cb = r
            in_specs.append(pl.BlockSpec((tile, w), lambda i, cb=cb: (i, cb)))
        else:
            arr = r
            in_specs.append(pl.BlockSpec((tile, arr.shape[1]), lambda i: (i, 0)))
        args.append(arr)
    for f in fulls:
        in_specs.append(pl.BlockSpec(f.shape, lambda i, nd=f.ndim: (0,) * nd))
        args.append(f)
    outs = [o if len(o) == 4 else (*o, o[0], 0) for o in outs]
    out_specs = [pl.BlockSpec((tile, w), lambda i, cb=cb: (i, cb)) for w, _, _, cb in outs]
    out_specs += [pl.BlockSpec(s, lambda i: (0, 0)) for s, _ in accs]
    out_shape = [jax.ShapeDtypeStruct((T, tw), dt) for _, dt, tw, _ in outs]
    out_shape += [jax.ShapeDtypeStruct(s, dt) for s, dt in accs]
    n_in, n_out = len(args), len(outs)
    for t in after:
        in_specs.append(pl.BlockSpec(memory_space=pl.ANY))
        args.append(t)

    def body(*refs):
        vals = fn(*[r[...] for r in refs[:n_in]])
        o_refs = refs[len(args):]
        for o, v in zip(o_refs[:n_out], vals[:n_out]):
            o[...] = v.astype(o.dtype)
        first_step = pl.program_id(0) == 0
        for o, v in zip(o_refs[n_out:], vals[n_out:]):
            @pl.when(first_step)
            def _(o=o, v=v):
                o[...] = v.astype(o.dtype)

            @pl.when(jnp.logical_not(first_step))
            def _(o=o, v=v):
                o[...] += v.astype(o.dtype)

    res = pl.pallas_call(
        body, name=name, grid=(T // tile,), in_specs=in_specs, out_specs=out_specs, out_shape=out_shape,
        compiler_params=_cparams(("arbitrary",)),
    )(*args)
    return res


def _rowsum(v, mxu):
    if not mxu:
        return jnp.sum(v, axis=-1, keepdims=True)
    ones = jnp.ones((v.shape[1], v.shape[1]), BF16)
    hi = v.astype(BF16)
    lo = (v - hi.astype(F32)).astype(BF16)
    return _dot(hi, ones) + _dot(lo, ones)


def _rms(x, g, mxu=False):
    r = lax.rsqrt(_rowsum(x * x, mxu) / x.shape[-1] + EPS)
    return (x * r) * g


def _rms_bwd(x, dy, g, n=None, mxu=False):
    n = x.shape[-1] if n is None else n
    r = lax.rsqrt(_rowsum(x * x, mxu) / n + EPS)
    xh = x * r
    dxh = dy * g
    dx = r * (dxh - xh * (_rowsum(dxh * xh, mxu) / n))
    return dx, dy * xh


def _colsum(v):
    return jnp.sum(v, axis=0, keepdims=True)


def _sigmoid(x):
    return 1.0 / (1.0 + jnp.exp(-x))


def _widen(v, width):
    reps = width // v.shape[1]
    return v if reps == 1 else jnp.concatenate([v] * reps, axis=-1)


def _rope_angles(T, dim):
    inv = (1.0 / (np.float32(ROPE_THETA) ** (np.arange(0, dim, 2, dtype=np.float32) / np.float32(dim)))).astype(np.float32)
    return np.arange(T, dtype=np.float32)[:, None] * inv[None, :]


def _ret_tables(T):
    ang = _rope_angles(T, RET_DK)
    log_gamma = np.log(np.float32(1.0) - np.float32(2.0) ** (-5.0 - np.arange(RET_HEADS, dtype=np.float32)))
    idx = np.arange(RET_BLOCK, dtype=np.float32)
    chunk = np.arange(RET_BLOCK) // CHUNK
    dist = idx[:, None] - idx[None, :]
    seen = np.where(chunk[:, None] == chunk[None, :], np.abs(dist), np.where(chunk[:, None] > chunk[None, :], dist, np.inf))
    intra = np.exp(log_gamma[:, None, None] * seen[None].astype(np.float32))
    qd = np.exp(log_gamma[:, None] * (idx + 1.0))[:, :, None]
    kd = np.exp(log_gamma[:, None] * (RET_BLOCK - 1.0 - idx))[:, :, None]
    cd = np.exp(log_gamma * RET_BLOCK)[:, None, None]
    return tuple(jnp.asarray(t, F32) for t in (np.cos(ang), np.sin(ang), intra, qd, kd, cd))


def _rope_half(x, c, s):
    x1, x2 = x[:, :RET_DK // 2], x[:, RET_DK // 2:]
    return jnp.concatenate([x1 * c - x2 * s, x2 * c + x1 * s], axis=-1)


def _rope_half_bwd(d, c, s):
    d1, d2 = d[:, :RET_DK // 2], d[:, RET_DK // 2:]
    return jnp.concatenate([d1 * c + d2 * s, d2 * c - d1 * s], axis=-1)


def _dot(a, b):
    return lax.dot_general(a, b, (((1,), (0,)), ((), ())), preferred_element_type=F32)


def _dot_nt(a, b):
    return lax.dot_general(a, b, (((1,), (1,)), ((), ())), preferred_element_type=F32)


def _dot_tn(a, b):
    return lax.dot_general(a, b, (((0,), (0,)), ((), ())), preferred_element_type=F32)


def _ret_specs(T, tb, rev):
    nj = T // tb
    jj = (lambda j: nj - 1 - j) if rev else (lambda j: j)
    g = RET_GROUP
    kq = RET_HEADS // g
    vq = 2 * RET_HEADS * RET_DK // (g * RET_DV)
    return dict(
        q=pl.BlockSpec((tb, g * RET_DK), lambda h, j: (jj(j), h)),
        k=pl.BlockSpec((tb, g * RET_DK), lambda h, j: (jj(j), kq + h)),
        v=pl.BlockSpec((tb, g * RET_DV), lambda h, j: (jj(j), vq + h)),
        tab=pl.BlockSpec((tb, RET_DK // 2), lambda h, j: (jj(j), 0)),
        intra=pl.BlockSpec((g, RET_BLOCK, RET_BLOCK), lambda h, j: (h, 0, 0)),
        dec=pl.BlockSpec((g, RET_BLOCK, 1), lambda h, j: (h, 0, 0)),
        cd=pl.BlockSpec((g, 1, 1), lambda h, j: (h, 0, 0)),
        o=pl.BlockSpec((tb, g * RET_DV), lambda h, j: (jj(j), h)),
        s=pl.BlockSpec((g, tb // RET_BLOCK, RET_DK, RET_DV), lambda h, j: (h, jj(j), 0, 0)),
    )


def _ret_fwd(proj, tabs, name):
    T = proj.shape[0]
    cos, sin, intra, qd, kd, cd = tabs
    tb = _pick(T, 512)
    cps = tb // RET_BLOCK
    sp = _ret_specs(T, tb, False)
    scale = RET_DK ** -0.5

    def body(q_ref, k_ref, v_ref, cos_ref, sin_ref, intra_ref, qd_ref, kd_ref, cd_ref, o_ref, s_ref, state):
        @pl.when(pl.program_id(1) == 0)
        def _():
            state[...] = jnp.zeros_like(state)

        for c in range(cps):
            rows = pl.ds(c * RET_BLOCK, RET_BLOCK)
            co, si = cos_ref[rows, :], sin_ref[rows, :]
            for h in range(RET_GROUP):
                hk, hv = slice(h * RET_DK, (h + 1) * RET_DK), slice(h * RET_DV, (h + 1) * RET_DV)
                q = _rope_half(q_ref[rows, hk].astype(F32), co, si)
                k = _rope_half(k_ref[rows, hk].astype(F32), co, si) * scale
                vb = v_ref[rows, hv].astype(BF16)
                st = state[h]
                sb = st.astype(BF16)
                s_ref[h, c] = sb
                sc = _dot_nt(q.astype(BF16), k.astype(BF16)) * intra_ref[h]
                inner = _dot(sc.astype(BF16), vb)
                cross = _dot((q * qd_ref[h]).astype(BF16), sb)
                o_ref[rows, hv] = inner + cross
                state[h] = st * cd_ref[h] + _dot_tn((k * kd_ref[h]).astype(BF16), vb)

    return pl.pallas_call(
        body, name=name, grid=(RET_HEADS // RET_GROUP, T // tb),
        in_specs=[sp["q"], sp["k"], sp["v"], sp["tab"], sp["tab"], sp["intra"], sp["dec"], sp["dec"], sp["cd"]],
        out_specs=[sp["o"], sp["s"]],
        out_shape=[jax.ShapeDtypeStruct((T, RET_HEADS * RET_DV), F32),
                   jax.ShapeDtypeStruct((RET_HEADS, T // RET_BLOCK, RET_DK, RET_DV), BF16)],
        scratch_shapes=[pltpu.VMEM((RET_GROUP, RET_DK, RET_DV), F32)],
        compiler_params=_cparams(("arbitrary", "arbitrary")),
    )(proj, proj, proj, cos, sin, intra, qd, kd, cd)


def _ret_bwd(proj, states, dout, dproj, tabs, name):
    assert RET_GROUP == 1
    T = proj.shape[0]
    cos, sin, intra, qd, kd, cd = tabs
    tb = _pick(T, 512)
    cps = tb // RET_BLOCK
    nj = T // tb
    sp = _ret_specs(T, tb, True)
    scale = RET_DK ** -0.5
    k0, v0 = RET_HEADS * RET_DK, 2 * RET_HEADS * RET_DK

    def body(q_ref, k_ref, v_ref, cos_ref, sin_ref, intra_ref, qd_ref, kd_ref, cd_ref, s_ref, do_ref, _dproj_in,
             out_ref, dq_s, dk_s, dv_s, sems, dstate):
        head, j = pl.program_id(0), pl.program_id(1)
        step = head * nj + j
        slot = step % 2
        dq_ref, dk_ref, dv_ref = dq_s.at[slot], dk_s.at[slot], dv_s.at[slot]

        @pl.when(j == 0)
        def _():
            dstate[...] = jnp.zeros_like(dstate)

        for c in reversed(range(cps)):
            rows = pl.ds(c * RET_BLOCK, RET_BLOCK)
            co, si = cos_ref[rows, :], sin_ref[rows, :]
            for h in range(RET_GROUP):
                hk, hv = slice(h * RET_DK, (h + 1) * RET_DK), slice(h * RET_DV, (h + 1) * RET_DV)
                q = _rope_half(q_ref[rows, hk].astype(F32), co, si)
                k = _rope_half(k_ref[rows, hk].astype(F32), co, si) * scale
                qb, kb = q.astype(BF16), k.astype(BF16)
                vb = v_ref[rows, hv].astype(BF16)
                dob = do_ref[rows, hv].astype(BF16)
                sb = s_ref[h, c]
                ia = intra_ref[h]
                pb = (_dot_nt(qb, kb) * ia).astype(BF16)
                dsn = dstate[h]
                dsb = dsn.astype(BF16)
                kdk = (k * kd_ref[h]).astype(BF16)
                qdq = (q * qd_ref[h]).astype(BF16)
                dv = _dot_tn(pb, dob) + _dot(kdk, dsb)
                dpb = (_dot_nt(dob, vb) * ia).astype(BF16)
                dq = _dot(dpb, kb) + _dot_nt(dob, sb) * qd_ref[h]
                dk = _dot_tn(dpb, qb) + _dot_nt(vb, dsb) * kd_ref[h]
                dstate[h] = dsn * cd_ref[h] + _dot_tn(qdq, dob)
                dq_ref[rows, hk] = _rope_half_bwd(dq, co, si).astype(BF16)
                dk_ref[rows, hk] = _rope_half_bwd(dk * scale, co, si).astype(BF16)
                dv_ref[rows, hv] = dv.astype(BF16)

        def copies(sl):
            r = pl.ds(pl.multiple_of((nj - 1 - j) * tb, tb), tb)
            cols = lambda first, w: pl.ds(pl.multiple_of(first + head * w, 128), w)
            return [pltpu.make_async_copy(dq_s.at[sl], out_ref.at[r, cols(0, RET_DK)], sems.at[sl, 0]),
                    pltpu.make_async_copy(dk_s.at[sl], out_ref.at[r, cols(k0, RET_DK)], sems.at[sl, 1]),
                    pltpu.make_async_copy(dv_s.at[sl], out_ref.at[r, cols(v0, RET_DV)], sems.at[sl, 2])]

        @pl.when(step > 0)
        def _():
            for cp in copies(1 - slot):
                cp.wait()

        for cp in copies(slot):
            cp.start()

        @pl.when(step == RET_HEADS * nj - 1)
        def _():
            for cp in copies(slot):
                cp.wait()

    return pl.pallas_call(
        body, name=name, grid=(RET_HEADS, nj),
        in_specs=[sp["q"], sp["k"], sp["v"], sp["tab"], sp["tab"], sp["intra"], sp["dec"], sp["dec"], sp["cd"],
                  sp["s"], sp["o"], pl.BlockSpec(memory_space=pl.ANY)],
        out_specs=pl.BlockSpec(memory_space=pl.ANY), out_shape=jax.ShapeDtypeStruct(dproj.shape, dproj.dtype),
        input_output_aliases={11: 0},
        scratch_shapes=[pltpu.VMEM((2, tb, RET_DK), BF16), pltpu.VMEM((2, tb, RET_DK), BF16),
                        pltpu.VMEM((2, tb, RET_DV), BF16), pltpu.SemaphoreType.DMA((2, 3)),
                        pltpu.VMEM((RET_GROUP, RET_DK, RET_DV), F32)],
        compiler_params=_cparams(("arbitrary", "arbitrary")),
    )(proj, proj, proj, cos, sin, intra, qd, kd, cd, states, dout, dproj)


def _ret_gate(out, proj, gn, name):
    def fn(o, g, *gains):
        g = g.astype(F32)
        parts = [_rms(o[:, h * RET_DV:(h + 1) * RET_DV], gains[h]) for h in range(RET_HEADS)]
        return (g * _sigmoid(g) * jnp.concatenate(parts, axis=-1),)
    w = RET_HEADS * RET_DV
    return _rows(fn, [out, (proj, w, 2)], [gn[h:h + 1] for h in range(RET_HEADS)], [(w, BF16)], name=name)[0]


def _ret_gate_bwd(out, proj, gn, dy, name):
    def fn(o, g, d, *gains):
        g = g.astype(F32)
        sg = _sigmoid(g)
        silu = g * sg
        dsilu = sg * (1.0 + g * (1.0 - sg))
        dos, dgs = [], []
        row = lax.broadcasted_iota(jnp.int32, (RET_HEADS, RET_DV), 0)
        dgn = jnp.zeros((RET_HEADS, RET_DV), F32)
        for h in range(RET_HEADS):
            sl = slice(h * RET_DV, (h + 1) * RET_DV)
            oh = o[:, sl]
            dgs.append(d[:, sl] * _rms(oh, gains[h]) * dsilu[:, sl])
            dx, dg = _rms_bwd(oh, d[:, sl] * silu[:, sl], gains[h])
            dos.append(dx)
            dgn = dgn + jnp.where(row == h, _colsum(dg), 0.0)
        return jnp.concatenate(dos, axis=-1), jnp.concatenate(dgs, axis=-1), dgn
    w = RET_HEADS * RET_DV
    return _rows(fn, [out, (proj, w, 2), dy], [gn[h:h + 1] for h in range(RET_HEADS)],
                 [(w, BF16), (w, BF16, proj.shape[1], 2)], [((RET_HEADS, RET_DV), F32)], name=name, tile=128)


def _mla_tables(T):
    ang = _rope_angles(T, MLA_ROPE)
    c, s = np.cos(ang), np.sin(ang)
    z32, z64 = np.zeros((T, 32), np.float32), np.zeros((T, 64), np.float32)
    cos_t = np.concatenate([c, c, z64], axis=1)
    sin_a = np.concatenate([-s, z32, z64], axis=1)
    sin_b = np.concatenate([z32, s, z64], axis=1)
    return tuple(jnp.asarray(t, F32) for t in (cos_t, sin_a, sin_b))


def _rope_blk(x, ct, sa, sb):
    return x * ct + pltpu.roll(x, 96, 1) * sa + pltpu.roll(x, 32, 1) * sb


def _rope_blk_bwd(d, ct, sa, sb):
    return d * ct + pltpu.roll(d * sa, 32, 1) + pltpu.roll(d * sb, 96, 1)


def _head_norm(x, gain):
    r = lax.rsqrt(_rowsum(x * x, True) / MLA_QKD + EPS)
    return (x * r) * gain


def _prep_heads(qv, kvv, kr, ct, sa, sb, gqv, gkv):
    qs, ks, vs = [], [], []
    for h in range(MLA_HEADS):
        b = h * MLA_HP
        y = _head_norm(qv[:, b:b + MLA_HP], gqv)
        qs += [y[:, :128], _rope_blk(y[:, 128:], ct, sa, sb)]
        y = _head_norm(jnp.concatenate([kvv[:, b:b + 128], kr], axis=-1), gkv)
        ks += [y[:, :128], _rope_blk(y[:, 128:], ct, sa, sb)]
        vs.append(kvv[:, b + 128:b + 256])
    return jnp.concatenate(qs, axis=-1), jnp.concatenate(ks, axis=-1), jnp.concatenate(vs, axis=-1)


def _mla_front(hn, W, tabs, name):
    wide = MLA_HEADS * MLA_HP
    gq = W["mla_q_norm"] * (MLA_QKD ** -0.5 * LOG2E)

    def epilogue(acc, ct, sa, sb, gqa, gkva, wuq, wukv, gqv, gkv):
        cqn = _rms(acc[:, :MLA_Q_RANK], gqa).astype(BF16)
        ckvn = _rms(acc[:, MLA_Q_RANK:MLA_Q_RANK + MLA_KV_RANK], gkva).astype(BF16)
        q = jnp.concatenate([_dot(cqn, wuq[s]) for s in range(N_CHIPS)], axis=-1).astype(BF16)
        kv = jnp.concatenate([_dot(ckvn, wukv[s]) for s in range(N_CHIPS)], axis=-1).astype(BF16)
        qf, kf, vf = _prep_heads(q.astype(F32), kv.astype(F32), acc[:, MLA_IN_PAD - 128:], ct, sa, sb, gqv, gkv)
        return acc, cqn, ckvn, q, kv, qf, kf, vf

    return _mm_rows(hn, W["mla_w_in"], extras=list(tabs),
                    fulls=[W["mla_q_a_norm"], W["mla_kv_a_norm"], W["mla_w_uq"], W["mla_w_ukv"], gq, W["mla_k_norm"]],
                    outs=[(MLA_IN_PAD, F32), (MLA_Q_RANK, BF16), (MLA_KV_RANK, BF16), (wide, BF16), (wide, BF16),
                          (wide, BF16), (wide, BF16), (MLA_HEADS * MLA_VD, BF16)],
                    epilogue=epilogue, name=name, tm=256)


def _prep_heads_bwd(qv, kvv, kr, ct, sa, sb, dqv, dkv, dvv, gqv, gkv):
    dqs, dkvs = [], []
    dkr = jnp.zeros_like(kr)
    dgq = jnp.zeros((1, MLA_HP), F32)
    dgk = jnp.zeros((1, MLA_HP), F32)
    for h in range(MLA_HEADS):
        b = h * MLA_HP
        dy = jnp.concatenate([dqv[:, b:b + 128], _rope_blk_bwd(dqv[:, b + 128:b + 256], ct, sa, sb)], axis=-1)
        dx, dg = _rms_bwd(qv[:, b:b + MLA_HP], dy, gqv, MLA_QKD, mxu=True)
        dqs.append(dx)
        dgq = dgq + _colsum(dg)
        dy = jnp.concatenate([dkv[:, b:b + 128], _rope_blk_bwd(dkv[:, b + 128:b + 256], ct, sa, sb)], axis=-1)
        dx, dg = _rms_bwd(jnp.concatenate([kvv[:, b:b + 128], kr], axis=-1), dy, gkv, MLA_QKD, mxu=True)
        dkvs += [dx[:, :128], dvv[:, h * MLA_VD:(h + 1) * MLA_VD].astype(F32)]
        dkr = dkr + dx[:, 128:]
        dgk = dgk + _colsum(dg)
    return jnp.concatenate(dqs, axis=-1), jnp.concatenate(dkvs, axis=-1), dkr, dgq, dgk


def _mla_back(q, kv, proj, h0, dh1, dqf, dkf, dvf, W, tabs, name):
    def fn(qv, kvv, pv, hv, dr, ct, sa, sb, dqv, dkv, dvv, gqv, gkv, gqa, gkva, wuq, wukv, w_in, g_mix):
        qv, kvv, dqv, dkv = (t.astype(F32) for t in (qv, kvv, dqv, dkv))
        dq, dkvx, dkr, dgq, dgk = _prep_heads_bwd(qv, kvv, pv[:, MLA_IN_PAD - 128:], ct, sa, sb, dqv, dkv, dvv, gqv, gkv)
        dq, dkvx = dq.astype(BF16), dkvx.astype(BF16)
        nq = wuq.shape[2]
        dcq = sum(_dot_nt(dq[:, s * nq:(s + 1) * nq], wuq[s]) for s in range(N_CHIPS))
        dckv = sum(_dot_nt(dkvx[:, s * nq:(s + 1) * nq], wukv[s]) for s in range(N_CHIPS))
        dxq, dgqa = _rms_bwd(pv[:, :MLA_Q_RANK], dcq, gqa)
        dxkv, dgkva = _rms_bwd(pv[:, MLA_Q_RANK:MLA_Q_RANK + MLA_KV_RANK], dckv, gkva)
        dproj = jnp.concatenate([dxq, dxkv, dkr], axis=-1).astype(BF16)
        dx, dgm = _rms_bwd(hv, _dot_nt(dproj, w_in), g_mix)
        return (dq, dkvx, dproj, dr + dx, dr + dx, dgq, dgk, _colsum(dgqa), _colsum(dgkva), _colsum(dgm))

    wide = MLA_HEADS * MLA_HP
    return _rows(fn, [q, kv, proj, h0, dh1, *tabs, dqf, dkf, dvf],
                 [W["mla_q_norm"], W["mla_k_norm"], W["mla_q_a_norm"], W["mla_kv_a_norm"], W["mla_w_uq"], W["mla_w_ukv"],
                  W["mla_w_in"], W["mix_norm"][1:2]],
                 [(wide, BF16), (wide, BF16), (MLA_IN_PAD, BF16), ROW_F32, ROW_BF16],
                 [((1, MLA_HP), F32), ((1, MLA_HP), F32), ((1, MLA_Q_RANK), F32), ((1, MLA_KV_RANK), F32),
                  ((1, D_MODEL), F32)], name=name, tile=256)


def _chunk_mask(qi, ki, tq, tk):
    shift = CHUNK.bit_length() - 1
    rq = lax.shift_right_arithmetic(qi * tq + lax.broadcasted_iota(jnp.int32, (tq, tk), 0), shift)
    ck = lax.shift_right_arithmetic(ki * tk + lax.broadcasted_iota(jnp.int32, (tq, tk), 1), shift)
    return ck <= rq


def _flash_fwd(qf, kf, vf, name):
    T = qf.shape[0]
    t = _pick(T, FLASH_T)
    n = T // t
    scale = MLA_QKD ** -0.5

    g = FLASH_HEADS

    def body(q_ref, k_ref, v_ref, o_ref, lse_ref, m_s, l_s, acc):
        qi = pl.program_id(1)
        m_s[...] = jnp.full_like(m_s, NEG)
        l_s[...] = jnp.zeros_like(l_s)
        acc[...] = jnp.zeros_like(acc)

        def step(kb, masked):
            rows = pl.ds(pl.multiple_of(kb * t, t), t)
            for h in range(g):
                hq, hv = slice(h * MLA_HP, (h + 1) * MLA_HP), slice(h * MLA_VD, (h + 1) * MLA_VD)
                s = _dot_nt(q_ref[:, hq], k_ref[rows, hq])
                if masked:
                    s = jnp.where(_chunk_mask(0, 0, t, t), s, NEG)
                m_prev = m_s[:, hv]
                m_new = jnp.maximum(m_prev, jnp.max(s, axis=-1, keepdims=True))
                alpha = jnp.exp2(m_prev - m_new)
                p = jnp.exp2(s - _widen(m_new, t))
                l_s[:, hv] = alpha * l_s[:, hv] + sum(p[:, i * 128:(i + 1) * 128] for i in range(t // 128))
                acc[:, hv] = acc[:, hv] * alpha + _dot(p.astype(BF16), v_ref[rows, hv])
                m_s[:, hv] = m_new

        @pl.loop(0, qi)
        def _(kb):
            step(kb, False)

        step(qi, True)
        for h in range(g):
            hv = slice(h * MLA_VD, (h + 1) * MLA_VD)
            l = jnp.sum(l_s[:, hv], axis=-1, keepdims=True)
            o_ref[:, hv] = acc[:, hv] / l
            lse_ref[:, hv] = m_s[:, hv] + jnp.log2(l)

    qmap = lambda h, i: (i, h)
    kmap = lambda h, i: (0, h)
    vec = pltpu.VMEM((t, g * MLA_VD), F32)
    return pl.pallas_call(
        body, name=name, grid=(MLA_HEADS // g, n),
        in_specs=[pl.BlockSpec((t, g * MLA_HP), qmap), pl.BlockSpec((T, g * MLA_HP), kmap),
                  pl.BlockSpec((T, g * MLA_VD), kmap)],
        out_specs=[pl.BlockSpec((t, g * MLA_VD), qmap), pl.BlockSpec((t, g * MLA_VD), qmap)],
        out_shape=[jax.ShapeDtypeStruct((T, MLA_HEADS * MLA_VD), F32),
                   jax.ShapeDtypeStruct((T, MLA_HEADS * MLA_VD), F32)],
        scratch_shapes=[vec, vec, vec],
        compiler_params=_cparams(("parallel", "arbitrary")),
    )(qf, kf, vf)


def _flash_bwd(qf, kf, vf, do16, lse, delta, name):
    T = qf.shape[0]
    t = _pick(T, FLASH_T)
    n = T // t
    scale = MLA_QKD ** -0.5

    def body(q_ref, k_ref, v_ref, do_ref, lse_ref, dl_ref, dq_out, dk_out, dv_out, dq_ref, dk_ref, dv_ref):
        kb = pl.program_id(1)

        @pl.when(kb == 0)
        def _():
            dq_ref[...] = jnp.zeros_like(dq_ref)

        dk_ref[...] = jnp.zeros_like(dk_ref)
        dv_ref[...] = jnp.zeros_like(dv_ref)
        k, v = k_ref[...], v_ref[...]

        def step(qb, masked):
            rows = pl.ds(pl.multiple_of(qb * t, t), t)
            q, dob = q_ref[rows, :], do_ref[rows, :]
            s = _dot_nt(q, k)
            if masked:
                s = jnp.where(_chunk_mask(0, 0, t, t), s, NEG)
            p = jnp.exp2(s - _widen(lse_ref[rows, :], t))
            ds = (p * (_dot_nt(dob, v) - _widen(dl_ref[rows, :], t))).astype(BF16)
            dv_ref[...] += _dot_tn(p.astype(BF16), dob)
            dk_ref[...] += _dot_tn(ds, q)
            dq_ref[rows, :] += _dot(ds, k)

        step(kb, True)

        @pl.loop(kb + 1, n)
        def _(qb):
            step(qb, False)

        dk_out[...] = (dk_ref[...] * (1.0 / LOG2E)).astype(BF16)
        dv_out[...] = dv_ref[...].astype(BF16)

        @pl.when(kb == n - 1)
        def _():
            dq_out[...] = (dq_ref[...] * scale).astype(BF16)

    qmap = lambda h, j: (0, h)
    kmap = lambda h, j: (j, h)
    return pl.pallas_call(
        body, name=name, grid=(MLA_HEADS, n),
        in_specs=[pl.BlockSpec((T, MLA_HP), qmap), pl.BlockSpec((t, MLA_HP), kmap), pl.BlockSpec((t, MLA_VD), kmap),
                  pl.BlockSpec((T, MLA_VD), qmap), pl.BlockSpec((T, MLA_VD), qmap), pl.BlockSpec((T, MLA_VD), qmap)],
        out_specs=[pl.BlockSpec((T, MLA_HP), qmap), pl.BlockSpec((t, MLA_HP), kmap), pl.BlockSpec((t, MLA_VD), kmap)],
        out_shape=[jax.ShapeDtypeStruct((T, MLA_HEADS * MLA_HP), BF16),
                   jax.ShapeDtypeStruct((T, MLA_HEADS * MLA_HP), BF16),
                   jax.ShapeDtypeStruct((T, MLA_HEADS * MLA_VD), BF16)],
        scratch_shapes=[pltpu.VMEM((T, MLA_HP), F32), pltpu.VMEM((t, MLA_HP), F32), pltpu.VMEM((t, MLA_VD), F32)],
        compiler_params=_cparams(("arbitrary", "arbitrary")),
    )(qf, kf, vf, do16, lse, delta)


MESH = pl.DeviceIdType.MESH
ANY = pl.BlockSpec(memory_space=pl.ANY)
_CHIP_FLIPS = ((1, 0), (0, 1), (1, 1))


def _place():
    return lax.axis_index("x"), lax.axis_index("y"), lax.axis_index("c")


def _other_chip(x, y, k):
    fx, fy = _CHIP_FLIPS[k]
    return ((1 - x) if fx else x), ((1 - y) if fy else y)


def _remote(src, dst, send_sems, recv_sems, k, to):
    return pltpu.make_async_remote_copy(src_ref=src, dst_ref=dst, send_sem=send_sems.at[k], recv_sem=recv_sems.at[k],
                                        device_id=to, device_id_type=MESH)


def _index(*vals):
    return jnp.stack(vals).astype(jnp.int32)


def _half(c, rows):
    return pl.ds(pl.multiple_of(c * rows, 16), rows)


def _gather_weights(parts, name, landed=None):
    n_w = len(parts)
    n_in = n_w if landed is None else 2 * n_w

    def body(*refs):
        ins, outs = refs[:n_w], refs[n_in:n_in + n_w]
        send_sems, recv_sems, local_sems = refs[n_in + n_w:]
        x, y, c = _place()
        j = 2 * x + y
        sibling = (x, y, 1 - c)
        chips = [_other_chip(x, y, k) for k in range(3)]
        pending = []
        for w in range(n_w):
            own = pltpu.make_async_copy(ins[w], outs[w].at[j], local_sems.at[w])
            own.start()
            pending.append(own)
        sent = []
        for w in range(n_w):
            if landed is not None:
                break
            r = _half(c, parts[w].shape[0] // 2)
            for k, (px, py) in enumerate(chips):
                cp = _remote(ins[w].at[r], outs[w].at[j, r], send_sems, recv_sems, 6 * w + k, (px, py, c))
                cp.start()
                sent.append(cp)
        for w in range(n_w):
            r = _half(c, parts[w].shape[0] // 2)
            for k, (px, py) in enumerate(chips):
                blk = outs[w].at[2 * px + py, r]
                if landed is None:
                    _remote(blk, blk, send_sems, recv_sems, 6 * w + k, (px, py, c)).wait_recv()
                cp = _remote(blk, blk, send_sems, recv_sems, 6 * w + 3 + k, sibling)
                cp.start()
                sent.append(cp)
        for w in range(n_w):
            r = _half(1 - c, parts[w].shape[0] // 2)
            for k, (px, py) in enumerate(chips):
                blk = outs[w].at[2 * px + py, r]
                _remote(blk, blk, send_sems, recv_sems, 6 * w + 3 + k, sibling).wait_recv()
        for cp in sent:
            cp.wait_send()
        for cp in pending:
            cp.wait()

    return pl.pallas_call(
        body, name=name, in_specs=[pl.BlockSpec(memory_space=pltpu.VMEM)] * n_w + [ANY] * (n_in - n_w),
        out_specs=[ANY] * n_w,
        out_shape=[jax.ShapeDtypeStruct((N_CHIPS, *p.shape), p.dtype) for p in parts],
        input_output_aliases={} if landed is None else {n_w + w: w for w in range(n_w)},
        scratch_shapes=[pltpu.SemaphoreType.DMA((6 * n_w,)), pltpu.SemaphoreType.DMA((6 * n_w,)),
                        pltpu.SemaphoreType.DMA((n_w,))],
        compiler_params=pltpu.CompilerParams(vmem_limit_bytes=VMEM_LIMIT),
    )(*parts, *(landed or []))


def _swap_halves(gs, name):
    n_w = len(gs)

    def body(*refs):
        g_refs, recv_refs = refs[:n_w], refs[n_w:2 * n_w]
        send_sems, recv_sems = refs[2 * n_w:]
        x, y, c = _place()
        sent = []
        for w in range(n_w):
            for jj in range(N_CHIPS):
                cp = _remote(g_refs[w].at[jj, 1 - c], recv_refs[w].at[jj], send_sems, recv_sems, N_CHIPS * w + jj,
                             (x, y, 1 - c))
                cp.start()
                sent.append(cp)
        for cp in sent:
            cp.wait()

    return pl.pallas_call(
        body, name=name, in_specs=[ANY] * n_w, out_specs=[ANY] * n_w,
        out_shape=[jax.ShapeDtypeStruct((N_CHIPS, *g.shape[2:]), g.dtype) for g in gs],
        scratch_shapes=[pltpu.SemaphoreType.DMA((N_CHIPS * n_w,)), pltpu.SemaphoreType.DMA((N_CHIPS * n_w,))],
    )(*gs)


def _pair_sum(g, recv, core, name):
    _, H, C = recv.shape
    tile = _pick(H, 256)

    def body(c_ref, own_ref, recv_ref, out_ref):
        out_ref[...] = (own_ref[...].astype(F32) + recv_ref[...].astype(F32)).astype(BF16)

    blk = pl.BlockSpec((None, tile, C), lambda jj, i, c: (jj, i, 0))
    return pl.pallas_call(
        body, name=name,
        grid_spec=pltpu.PrefetchScalarGridSpec(
            num_scalar_prefetch=1, grid=(N_CHIPS, H // tile),
            in_specs=[pl.BlockSpec((None, None, tile, C), lambda jj, i, c: (jj, c[0], i, 0)), blk],
            out_specs=blk),
        out_shape=jax.ShapeDtypeStruct((N_CHIPS, H, C), BF16),
        compiler_params=_cparams(("arbitrary", "arbitrary")),
    )(_index(core), g, recv)


def _chip_sum(g, recv, got, chip, core, name):
    _, H, C = recv.shape
    tile = _pick(H, 256)

    def body(s_ref, own_ref, recv_ref, g0_ref, g1_ref, g2_ref, out_ref):
        pair = own_ref[...].astype(F32) + recv_ref[...].astype(F32)
        out_ref[...] = ((pair + g0_ref[...].astype(F32)) + g1_ref[...].astype(F32)) + g2_ref[...].astype(F32)

    def got_spec(k):
        return pl.BlockSpec((None, tile, C), lambda i, s, k=k: (k, i, 0))

    return pl.pallas_call(
        body, name=name,
        grid_spec=pltpu.PrefetchScalarGridSpec(
            num_scalar_prefetch=1, grid=(H // tile,),
            in_specs=[pl.BlockSpec((None, None, tile, C), lambda i, s: (s[0], s[1], i, 0)),
                      pl.BlockSpec((None, tile, C), lambda i, s: (s[0], i, 0)), got_spec(0), got_spec(1), got_spec(2)],
            out_specs=pl.BlockSpec((None, tile, C), lambda i, s: (s[1], i, 0))),
        out_shape=jax.ShapeDtypeStruct((2, H, C), F32),
        compiler_params=_cparams(("arbitrary",)),
    )(_index(chip, core), g, recv, got, got, got)


def _share_halves(reds):
    n_w = len(reds)

    def body(*refs):
        out_refs = refs[n_w:2 * n_w]
        send_sems, recv_sems = refs[2 * n_w:]
        x, y, c = _place()
        sent = []
        for w in range(n_w):
            blk = out_refs[w].at[c]
            cp = _remote(blk, blk, send_sems, recv_sems, w, (x, y, 1 - c))
            cp.start()
            sent.append(cp)
        for cp in sent:
            cp.wait()

    return pl.pallas_call(
        body, name="grad_share_halves", in_specs=[ANY] * n_w, out_specs=[ANY] * n_w,
        out_shape=[jax.ShapeDtypeStruct(r.shape, r.dtype) for r in reds],
        input_output_aliases={w: w for w in range(n_w)},
        scratch_shapes=[pltpu.SemaphoreType.DMA((n_w,)), pltpu.SemaphoreType.DMA((n_w,))],
    )(*reds)


def _allsum_small(v, name):
    R, W = v.shape
    n_dev = 8
    vm = pl.BlockSpec(memory_space=pltpu.VMEM)

    def body(v_ref, out_ref, buf, send_sems, recv_sems):
        x, y, c = _place()
        me = 4 * x + 2 * y + c
        buf[me] = v_ref[...]
        sent = []
        for k in range(1, n_dev):
            peer = ((1 - x) if k & 4 else x, (1 - y) if k & 2 else y, (1 - c) if k & 1 else c)
            cp = _remote(v_ref, buf.at[me], send_sems, recv_sems, k - 1, peer)
            cp.start()
            sent.append(cp)
        for cp in sent:
            cp.wait_recv()
        for cp in sent:
            cp.wait_send()
        acc = buf[0]
        for q in range(1, n_dev):
            acc = acc + buf[q]
        out_ref[...] = acc

    return pl.pallas_call(
        body, name=name, in_specs=[vm], out_specs=vm, out_shape=jax.ShapeDtypeStruct((R, W), v.dtype),
        scratch_shapes=[pltpu.VMEM((n_dev, R, W), v.dtype), pltpu.SemaphoreType.DMA((n_dev - 1,)),
                        pltpu.SemaphoreType.DMA((n_dev - 1,))],
    )(v)


HBM = pl.BlockSpec(memory_space=pltpu.HBM)
SEM = pl.BlockSpec(memory_space=pltpu.SEMAPHORE)
_DATAFLOW = pltpu.SideEffectType.DATAFLOW_SIDE_EFFECTING


def _split_start(name, srcs, land_shapes, n_copies, copies, after=()):
    ns, nl = len(srcs), len(land_shapes)
    lands = [lax.empty(s.shape, s.dtype) for s in land_shapes]

    def body(*refs):
        outs = refs[ns + nl + len(after):]
        for cp in copies(refs[:ns], refs[ns:ns + nl], outs[0], outs[1]):
            cp.start()
        outs[-1][...] = jnp.zeros_like(outs[-1])

    sems = pltpu.SemaphoreType.DMA((n_copies,))
    res = pl.pallas_call(
        body, name=name, in_specs=[HBM] * (ns + nl) + [ANY] * len(after),
        out_specs=(SEM, SEM, *[HBM] * (ns + nl), pl.BlockSpec(memory_space=pltpu.VMEM)),
        out_shape=(sems, sems, *[pltpu.HBM(a.shape, a.dtype) for a in srcs],
                   *[pltpu.HBM(s.shape, s.dtype) for s in land_shapes], jax.ShapeDtypeStruct((8, 128), F32)),
        input_output_aliases={i: 2 + i for i in range(ns + nl)},
        compiler_params=pltpu.CompilerParams(has_side_effects=_DATAFLOW),
    )(*[pltpu.with_memory_space_constraint(a, pltpu.HBM) for a in [*srcs, *lands]], *after)
    return res[0], res[1], list(res[2:2 + ns]), list(res[2 + ns:2 + ns + nl]), res[-1]


def _split_wait(name, send_sems, recv_sems, srcs, lands, copies, after=()):
    ns, nl = len(srcs), len(lands)

    def body(*refs):
        for cp in copies(refs[:ns], refs[ns:ns + nl], refs[ns + nl], refs[ns + nl + 1]):
            cp.wait_send()
            cp.wait_recv()

    res = pl.pallas_call(
        body, name=name, in_specs=[HBM] * (ns + nl) + [SEM, SEM] + [ANY] * len(after), out_specs=[HBM] * (ns + nl),
        out_shape=[pltpu.HBM(a.shape, a.dtype) for a in [*srcs, *lands]],
        input_output_aliases={i: i for i in range(ns + nl)},
        compiler_params=pltpu.CompilerParams(has_side_effects=_DATAFLOW),
    )(*srcs, *lands, send_sems, recv_sems, *after)
    return list(res[ns:])


def _gather_copies(rows):
    def copies(src_refs, land_refs, send_sems, recv_sems):
        x, y, c = _place()
        j = 2 * x + y
        out = []
        for w in range(len(src_refs)):
            r = _half(c, rows[w] // 2)
            for k in range(3):
                px, py = _other_chip(x, y, k)
                out.append(_remote(src_refs[w].at[r], land_refs[w].at[j, r], send_sems, recv_sems, 3 * w + k, (px, py, c)))
        return out
    return copies


def _scatter_copies(src_refs, land_refs, send_sems, recv_sems):
    x, y, c = _place()
    j = 2 * x + y
    out = []
    for w in range(len(src_refs)):
        for k in range(3):
            px, py = _other_chip(x, y, k)
            pj = 2 * px + py
            out.append(_remote(src_refs[w].at[pj], land_refs[w].at[(j - pj + 4) % 4 - 1], send_sems, recv_sems, 3 * w + k,
                               (px, py, c)))
    return out


def _reduce_begin(grads, core, tag):
    names = list(grads)
    gs = [grads[k].reshape(N_CHIPS, 2, -1, grads[k].shape[-1]) for k in names]
    recvs = _swap_halves(gs, f"grad_swap_halves_{tag}")
    sums = [_pair_sum(g, r, core, f"pair_sum_{k}") for k, g, r in zip(names, gs, recvs)]
    return names, gs, recvs, sums


def _reduce_end(begun, gots, chip, core):
    names, gs, recvs, _ = begun
    return {k: _chip_sum(g, r, t, chip, core, f"chip_sum_{k}") for k, g, r, t in zip(names, gs, recvs, gots)}


def _got_shapes(sums):
    return [jax.ShapeDtypeStruct((3, *a.shape[1:]), a.dtype) for a in sums]


def _adamw(w, g, m, v, name, layers=1, layer=0, into=None):
    shape = w.shape
    cols = shape[-1]
    w3, m3, v3 = (t.reshape(layers, -1, cols) for t in (w, m, v))
    rows = w3.shape[1]
    tile = _pick(rows, 256) if rows % 8 == 0 else rows
    n_in = 4 + (0 if into is None else 4)
    stack_g = layers > 1

    def body(*refs):
        wv, gv, mv, vv = (r[...] for r in refs[:4])
        d_ref, m_ref, v_ref = refs[len(refs) - 3:]
        m2 = ADAM_B1 * mv + (1.0 - ADAM_B1) * gv
        v2 = ADAM_B2 * vv + (1.0 - ADAM_B2) * jnp.square(gv)
        m_hat = m2 / (1.0 - ADAM_B1 ** ADAM_STEP)
        v_hat = v2 / (1.0 - ADAM_B2 ** ADAM_STEP)
        if stack_g:
            refs[n_in][...] = gv
        d_ref[...] = -ADAM_LR * (m_hat / (jnp.sqrt(v_hat) + ADAM_EPS) + ADAM_WD * wv)
        m_ref[...] = m2
        v_ref[...] = v2

    n_out = 4 if stack_g else 3
    lay = pl.BlockSpec((None, tile, cols), lambda i: (layer, i, 0))
    out = jax.ShapeDtypeStruct((layers, rows, cols), F32)
    res = pl.pallas_call(
        body, name=name, grid=(rows // tile,),
        in_specs=[lay, pl.BlockSpec((tile, cols), lambda i: (i, 0)), lay, lay] + [ANY] * (n_in - 4),
        out_specs=[lay] * n_out, out_shape=[out] * n_out,
        input_output_aliases={} if into is None else {4 + k: k for k in range(4)},
        compiler_params=_cparams(("arbitrary",)),
    )(w3, g.reshape(rows, cols), m3, v3, *([] if into is None else [t.reshape(layers, rows, cols) for t in into]))
    res = tuple(t.reshape(shape) for t in res)
    return res if stack_g else (g.reshape(shape), *res)


ROW_F32, ROW_BF16 = (D_MODEL, F32), (D_MODEL, BF16)


def _res_norm(acc, h, gain):
    hh = h + acc
    return hh, _rms(hh, gain)


def _dx_norm_bwd(d, w, h, dres, gain, name, **kw):
    def epilogue(acc, hv, dr, g):
        dx, dg = _rms_bwd(hv, acc, g)
        return dr + dx, dr + dx, _colsum(dg)
    return _mm_rows(d, w, tb=True, extras=[h, dres], fulls=[gain], outs=[ROW_F32, ROW_BF16], accs=[((1, D_MODEL), F32)],
                    epilogue=epilogue, name=name, **kw)


def _tail_fwd(h1, hn2, p16, W, i, tag, next_gain=None, target=None):
    a = _mm(hn2, W["mlp_w1"][i], bblk=True, outs=[BF16], name=f"{tag}_mlp_w1",
            epilogue=lambda acc: (jnp.square(jnp.maximum(acc, 0.0)),))
    h2, hn3 = _mm_rows(a, W["mlp_w2"][i], extras=[h1], fulls=[W["ple_norm"][i:i + 1]], outs=[ROW_F32, ROW_BF16],
                       epilogue=_res_norm, name=f"{tag}_mlp_w2")
    def embed(acc, pv, h, wp):
        gate = _sigmoid(acc)
        ppv = jnp.concatenate([_dot(pv, wp[s]) for s in range(N_CHIPS)], axis=-1)
        return gate, ppv, h + gate * ppv

    if target is None:
        def gated(acc, pv, h, wp, gain):
            gate, ppv, hh = embed(acc, pv, h, wp)
            return hh, ppv, gate, _rms(hh, gain)
        h3, pp, gate, hn = _mm_rows(hn3, W["ple_gate_w"][i], extras=[p16[i], h2], fulls=[W["ple_proj_w"][i], next_gain],
                                    outs=[ROW_F32, ROW_BF16, ROW_BF16, ROW_BF16], epilogue=gated, name=f"{tag}_ple")
        return h3, hn, (h1, hn2, a, h2, hn3, gate, pp)

    def gated_loss(acc, pv, h, t, wp):
        gate, ppv, hh = embed(acc, pv, h, wp)
        e = hh - t
        return ppv, gate, e * (1.0 / D_MODEL), jnp.full((1, 128), 0.5 / D_MODEL, F32) * jnp.sum(e * e)
    pp, gate, dy, loss = _mm_rows(hn3, W["ple_gate_w"][i], extras=[p16[i], h2, target], fulls=[W["ple_proj_w"][i]],
                                  outs=[ROW_BF16, ROW_BF16, ROW_F32], accs=[((1, 128), F32)], epilogue=gated_loss,
                                  name=f"{tag}_ple")
    return dy, loss, (h1, hn2, a, h2, hn3, gate, pp)


def _tail_bwd(dh3, saved, p16, W, i, tag, after=()):
    h1, hn2, a, h2, hn3, gate, pp = saved

    def gate_bwd(d, g, ppv):
        g, ppv = g.astype(F32), ppv.astype(F32)
        return d * g, d * ppv * g * (1.0 - g)

    def dw(kind, name):
        return (kind, 1, 0, None)

    dpp, dgl = _rows(gate_bwd, [dh3, gate, pp], [], [(D_MODEL, BF16), (D_MODEL, BF16)], name=f"{tag}_ple_gate_bwd",
                     after=after)
    d_proj = _mm(p16[i], dpp, ta=True, outs=[BF16], dw=dw("cols", "ple_proj_w"), name=f"{tag}_d_ple_proj")
    d_gate = _mm(hn3, dgl, ta=True, outs=[BF16], dw=dw("rows", "ple_gate_w"), name=f"{tag}_d_ple_gate")
    dh2, dh2_16, d_ple_norm = _dx_norm_bwd(dgl, W["ple_gate_w"][i], h2, dh3, W["ple_norm"][i:i + 1],
                                           f"{tag}_ple_gate_dx")
    d_w2 = _mm(a, dh2_16, ta=True, outs=[BF16], dw=dw("rows", "mlp_w2"), name=f"{tag}_d_mlp_w2")
    dz = _mm(dh2_16, W["mlp_w2"][i], tb=True, extras=[a], outs=[BF16], name=f"{tag}_mlp_w2_dx",
             epilogue=lambda acc, av: (acc * (2.0 * jnp.sqrt(av.astype(F32))),))
    d_w1 = _mm(hn2, dz, ta=True, outs=[BF16], dw=dw("cols", "mlp_w1"), name=f"{tag}_d_mlp_w1")
    dh1, dh1_16, d_mlp_norm = _dx_norm_bwd(dz, W["mlp_w1"][i], h1, dh2, W["mlp_norm"][i:i + 1], f"{tag}_mlp_w1_dx",
                                           bblk=True)
    big = {f"mlp_w1_{i}": d_w1, f"mlp_w2_{i}": d_w2, f"ple_gate_w_{i}": d_gate, f"ple_proj_w_{i}": d_proj}
    return dh1, dh1_16, big, dict(mlp_norm=d_mlp_norm, ple_norm=d_ple_norm)


def _ret_layer_fwd(h0, W, tabs, after=()):
    hn = _rows(lambda x, g: (_rms(x, g),), [h0], [W["mix_norm"][0:1]], [(D_MODEL, BF16)], name="ret_mix_norm",
               after=after)[0]
    proj = _mm(hn, W["ret_w_in"], bblk=True, outs=[BF16], name="ret_w_in")
    out, states = _ret_fwd(proj, tabs, "ret_scan")
    y = _ret_gate(out, proj, W["ret_gn"], "ret_gate")
    h1, hn2 = _mm_rows(y, W["ret_w_out"], extras=[h0], fulls=[W["mlp_norm"][0:1]], outs=[ROW_F32, ROW_BF16],
                       epilogue=_res_norm, name="ret_w_out")
    return h1, hn2, (h0, hn, proj, out, states, y)


def _d_ret_w_out(dh1_16, saved):
    return _mm(saved[5], dh1_16, ta=True, outs=[BF16], dw=("rows", 1, 0, None), name="d_ret_w_out")


def _ret_layer_bwd(dh1, dh1_16, saved, W, tabs, after=(), on_grads=None, d_w_out=None):
    h0, hn, proj, out, states, y = saved
    d_w_out = _d_ret_w_out(dh1_16, saved) if d_w_out is None else d_w_out
    dy = _mm(dh1_16, W["ret_w_out"], tb=True, name="ret_w_out_dx", after=after)
    dout, dproj, d_gn = _ret_gate_bwd(out, proj, W["ret_gn"], dy, "ret_gate_bwd")
    dproj = _ret_bwd(proj, states, dout, dproj, tabs, "ret_scan_bwd")
    d_w_in = _mm(hn, dproj, ta=True, outs=[BF16], dw=("cols", 1, 0, None), name="d_ret_w_in")
    big = dict(ret_w_in=d_w_in, ret_w_out=d_w_out)
    later = () if on_grads is None else on_grads(big)
    dh0, _, d_mix = _dx_norm_bwd(dproj, W["ret_w_in"], h0, dh1, W["mix_norm"][0:1], "ret_w_in_dx", bblk=True, tm=256,
                                 after=later)
    return dh0, big, dict(mix_norm=d_mix, ret_gn=d_gn)


def _mla_layer_fwd(h0, hn, W, tabs):
    proj, cqn, ckvn, q, kv, qf, kf, vf = _mla_front(hn, W, tabs, "mla_front")
    o, lse = _flash_fwd(qf, kf, vf, "mla_flash")
    h1, hn2 = _mm_rows(o, W["mla_w_out"], extras=[h0], fulls=[W["mlp_norm"][1:2]], outs=[ROW_F32, ROW_BF16],
                       epilogue=_res_norm, name="mla_w_out")
    return h1, hn2, (h0, hn, proj, cqn, ckvn, q, kv, qf, kf, vf, o, lse)


def _mla_layer_bwd(dh1, dh1_16, saved, W, tabs):
    h0, hn, proj, cqn, ckvn, q, kv, qf, kf, vf, o, lse = saved
    d_w_out = _mm(o, dh1_16, ta=True, outs=[BF16], dw=("rows", 1, 0, None), name="d_mla_w_out")
    def with_delta(acc, ov):
        parts = []
        for h in range(MLA_HEADS):
            sl = slice(h * MLA_VD, (h + 1) * MLA_VD)
            d = jnp.sum(acc[:, sl] * ov[:, sl], axis=-1, keepdims=True)
            parts.append(jnp.broadcast_to(d, (d.shape[0], MLA_VD)))
        return jnp.concatenate(parts, axis=-1), acc

    delta, do16 = _mm_rows(dh1_16, W["mla_w_out"], tb=True, extras=[o], outs=[ROW_F32, ROW_BF16], epilogue=with_delta,
                           name="mla_w_out_dx")
    dqf, dkf, dvf = _flash_bwd(qf, kf, vf, do16, lse, delta, "mla_flash_bwd")
    dq, dkv, dproj, dh0, dh0_16, d_gq, d_gk, d_gqa, d_gkva, d_mix = _mla_back(q, kv, proj, h0, dh1, dqf, dkf, dvf, W, tabs,
                                                                              "mla_back")
    d_w_uq = _mm(cqn, dq, ta=True, outs=[BF16], dw=("cols", 1, 0, None), name="d_mla_w_uq")
    d_w_ukv = _mm(ckvn, dkv, ta=True, outs=[BF16], dw=("cols", 1, 0, None), name="d_mla_w_ukv")
    d_w_in = _mm(hn, dproj, ta=True, outs=[BF16], dw=("rows", 1, 0, None), name="d_mla_w_in")
    return (dh0, dh0_16, dict(mla_w_in=d_w_in, mla_w_uq=d_w_uq, mla_w_ukv=d_w_ukv, mla_w_out=d_w_out),
            dict(mix_norm=d_mix, mla_q_a_norm=d_gqa, mla_kv_a_norm=d_gkva, mla_q_norm=d_gq, mla_k_norm=d_gk))


def _local_step(x, p16, target, W):
    T = x.shape[0]
    ret_tabs, mla_tabs = _ret_tables(T), _mla_tables(T)
    h1, hn, s_ret = _ret_layer_fwd(x, W, ret_tabs)
    h3, hn, s_tail0 = _tail_fwd(h1, hn, p16, W, 0, "l0", next_gain=W["mix_norm"][1:2])
    h4, hn, s_mla = _mla_layer_fwd(h3, hn, W, mla_tabs)
    dy, loss, s_tail1 = _tail_fwd(h4, hn, p16, W, 1, "l1", target=target)
    dh4, dh4_16, g_t1, n_t1 = _tail_bwd(dy, s_tail1, p16, W, 1, "l1")
    dh3, _, g_mla, n_mla = _mla_layer_bwd(dh4, dh4_16, s_mla, W, mla_tabs)
    dh1, dh1_16, g_t0, n_t0 = _tail_bwd(dh3, s_tail0, p16, W, 0, "l0")
    dx, g_ret, n_ret = _ret_layer_bwd(dh1, dh1_16, s_ret, W, ret_tabs)
    return loss, dx, {**g_ret, **g_t0, **g_mla, **g_t1}, _small_grads(n_ret, n_t0, n_mla, n_t1)


def _small_grads(n_ret, n_t0, n_mla, n_t1):
    return dict(
        mix_norm=jnp.concatenate([n_ret["mix_norm"], n_mla["mix_norm"]], axis=0),
        mlp_norm=jnp.concatenate([n_t0["mlp_norm"], n_t1["mlp_norm"]], axis=0),
        ple_norm=jnp.concatenate([n_t0["ple_norm"], n_t1["ple_norm"]], axis=0),
        ret_gn=n_ret["ret_gn"], mla_q_a_norm=n_mla["mla_q_a_norm"], mla_kv_a_norm=n_mla["mla_kv_a_norm"],
        mla_q_norm=n_mla["mla_q_norm"], mla_k_norm=n_mla["mla_k_norm"])


_ORDER = ("mix_norm", "ret_w_in", "ret_gn", "ret_w_out", "mla_w_in", "mla_q_a_norm", "mla_kv_a_norm", "mla_w_uq",
          "mla_w_ukv", "mla_q_norm", "mla_k_norm", "mla_w_out", "mlp_norm", "mlp_w1", "mlp_w2", "ple_norm",
          "ple_gate_w", "ple_proj_w")
_TWO_LAYER = ("mlp_w1", "mlp_w2", "ple_gate_w", "ple_proj_w")
HEADS_PER_CHIP = MLA_HEADS // N_CHIPS
GAIN_ROWS = 32


def _travel_parts(w):
    uq = jnp.pad(w["mla_w_uq"][0].reshape(MLA_Q_RANK, HEADS_PER_CHIP, MLA_QKD), ((0, 0), (0, 0), (0, MLA_HP - MLA_QKD)))
    parts = {"ret_w_in": w["ret_w_in"][0], "ret_w_out": w["ret_w_out"][0]}
    for k in _TWO_LAYER:
        parts[k + "_0"] = w[k][0]
    parts["mla_w_in"] = jnp.pad(w["mla_w_in"][0], ((0, 0), (0, MLA_IN_PAD - MLA_IN)))
    parts["mla_w_uq"] = uq.reshape(MLA_Q_RANK, HEADS_PER_CHIP * MLA_HP)
    parts["mla_w_ukv"] = w["mla_w_ukv"][0]
    parts["mla_w_out"] = w["mla_w_out"][0]
    for k in _TWO_LAYER:
        parts[k + "_1"] = w[k][1]
    gains = jnp.concatenate([_pad_row(w["ret_gn"]), _pad_row(w["mla_q_a_norm"]), _pad_row(w["mla_kv_a_norm"]),
                             jnp.zeros((GAIN_ROWS - 3, PACK_W), F32)], axis=0)
    return {"gains": gains, **{k: v.astype(BF16) for k, v in parts.items()}}


def _full_weights(full):
    rows = lambda a: a.reshape(-1, a.shape[-1])
    W = {k: full[k] for k in ("ret_w_in", "mla_w_uq", "mla_w_ukv") if k in full}
    for k in ("ret_w_out", "mla_w_in", "mla_w_out"):
        if k in full:
            W[k] = rows(full[k])
    for k, by_rows in (("mlp_w1", False), ("ple_proj_w", False), ("mlp_w2", True), ("ple_gate_w", True)):
        layers = [full.get(f"{k}_{i}") for i in range(2)]
        W[k] = [rows(t) if (by_rows and t is not None) else t for t in layers]
    return W


def _shard_grad(name, red, shape):
    if name == "mla_w_in":
        red = red.reshape(-1, MLA_IN_PAD)[:, :MLA_IN]
    elif name == "mla_w_uq":
        red = red.reshape(MLA_Q_RANK, HEADS_PER_CHIP, MLA_HP)[:, :, :MLA_QKD]
    return red.reshape(shape)


def _pad_row(v):
    v = v.reshape(1, -1)
    return jnp.pad(v, ((0, 0), (0, PACK_W - v.shape[1])))


def kernel(x, p, mix_norm, ret_w_in, ret_gn, ret_w_out, mla_w_in, mla_q_a_norm, mla_kv_a_norm, mla_w_uq, mla_w_ukv, mla_q_norm, mla_k_norm, mla_w_out, mlp_norm, mlp_w1, mlp_w2, ple_norm, ple_gate_w, ple_proj_w, loss_target, m_mix_norm, m_ret_w_in, m_ret_gn, m_ret_w_out, m_mla_w_in, m_mla_q_a_norm, m_mla_kv_a_norm, m_mla_w_uq, m_mla_w_ukv, m_mla_q_norm, m_mla_k_norm, m_mla_w_out, m_mlp_norm, m_mlp_w1, m_mlp_w2, m_ple_norm, m_ple_gate_w, m_ple_proj_w, v_mix_norm, v_ret_w_in, v_ret_gn, v_ret_w_out, v_mla_w_in, v_mla_q_a_norm, v_mla_kv_a_norm, v_mla_w_uq, v_mla_w_ukv, v_mla_q_norm, v_mla_k_norm, v_mla_w_out, v_mlp_norm, v_mlp_w1, v_mlp_w2, v_ple_norm, v_ple_gate_w, v_ple_proj_w):
    w = dict(mix_norm=mix_norm, ret_w_in=ret_w_in, ret_gn=ret_gn, ret_w_out=ret_w_out, mla_w_in=mla_w_in,
             mla_q_a_norm=mla_q_a_norm, mla_kv_a_norm=mla_kv_a_norm, mla_w_uq=mla_w_uq, mla_w_ukv=mla_w_ukv,
             mla_q_norm=mla_q_norm, mla_k_norm=mla_k_norm, mla_w_out=mla_w_out, mlp_norm=mlp_norm, mlp_w1=mlp_w1,
             mlp_w2=mlp_w2, ple_norm=ple_norm, ple_gate_w=ple_gate_w, ple_proj_w=ple_proj_w)
    m = dict(mix_norm=m_mix_norm, ret_w_in=m_ret_w_in, ret_gn=m_ret_gn, ret_w_out=m_ret_w_out, mla_w_in=m_mla_w_in,
             mla_q_a_norm=m_mla_q_a_norm, mla_kv_a_norm=m_mla_kv_a_norm, mla_w_uq=m_mla_w_uq, mla_w_ukv=m_mla_w_ukv,
             mla_q_norm=m_mla_q_norm, mla_k_norm=m_mla_k_norm, mla_w_out=m_mla_w_out, mlp_norm=m_mlp_norm,
             mlp_w1=m_mlp_w1, mlp_w2=m_mlp_w2, ple_norm=m_ple_norm, ple_gate_w=m_ple_gate_w, ple_proj_w=m_ple_proj_w)
    v = dict(mix_norm=v_mix_norm, ret_w_in=v_ret_w_in, ret_gn=v_ret_gn, ret_w_out=v_ret_w_out, mla_w_in=v_mla_w_in,
             mla_q_a_norm=v_mla_q_a_norm, mla_kv_a_norm=v_mla_kv_a_norm, mla_w_uq=v_mla_w_uq, mla_w_ukv=v_mla_w_ukv,
             mla_q_norm=v_mla_q_norm, mla_k_norm=v_mla_k_norm, mla_w_out=v_mla_w_out, mlp_norm=v_mlp_norm,
             mlp_w1=v_mlp_w1, mlp_w2=v_mlp_w2, ple_norm=v_ple_norm, ple_gate_w=v_ple_gate_w, ple_proj_w=v_ple_proj_w)
    xi, yi, ci = _place()
    chip = 2 * xi + yi
    n = N_CHIPS

    parts = _travel_parts(w)
    first = ("gains", "ret_w_in", "ret_w_out")
    mid = [k + "_0" for k in _TWO_LAYER]
    last = [k for k in parts if k not in first and k not in mid]
    full = dict(zip(first, _gather_weights([parts[k] for k in first], "gather_first")))

    def gather_behind(names, tag, after):
        copies = _gather_copies([parts[k].shape[0] for k in names])
        started = _split_start(f"gather_{tag}_start", [parts[k] for k in names],
                               [jax.ShapeDtypeStruct((n, *parts[k].shape), BF16) for k in names], 3 * len(names),
                               copies, after=after)

        def arrive(after):
            landed = _split_wait(f"gather_{tag}_wait", *started[:4], copies, after=after)
            full.update(zip(names, _gather_weights([parts[k] for k in names], f"gather_{tag}_finish", landed=landed)))
            W.update(_full_weights(full))
        return started[4], arrive

    mid_token, mid_arrive = gather_behind(mid, "mid", [full["ret_w_in"]])
    g_token, last_arrive = gather_behind(last, "last", [mid_token])
    gains = full["gains"]
    W = dict(mix_norm=mix_norm, mlp_norm=mlp_norm, ple_norm=ple_norm,
             mla_q_norm=jnp.pad(mla_q_norm, ((0, 0), (0, MLA_HP - MLA_QKD))),
             mla_k_norm=jnp.pad(mla_k_norm, ((0, 0), (0, MLA_HP - MLA_QKD))),
             ret_w_in=full["ret_w_in"], ret_w_out=full["ret_w_out"].reshape(-1, D_MODEL),
             ret_gn=gains[:, 0, :RET_HEADS * 128].reshape(n, RET_HEADS, 128).transpose(1, 0, 2).reshape(RET_HEADS, RET_DV),
             mla_q_a_norm=gains[:, 1, :MLA_Q_RANK // n].reshape(1, MLA_Q_RANK),
             mla_kv_a_norm=gains[:, 2, :MLA_KV_RANK // n].reshape(1, MLA_KV_RANK))
    x0, p16, target = x[0], p[:, 0].astype(BF16), loss_target[0]
    T = x0.shape[0]
    ret_tabs, mla_tabs = _ret_tables(T), _mla_tables(T)

    h1, hn, s_ret = _ret_layer_fwd(x0, W, ret_tabs, after=[g_token])
    mid_arrive([h1])
    h3, hn, s_tail0 = _tail_fwd(h1, hn, p16, W, 0, "l0", next_gain=W["mix_norm"][1:2])
    last_arrive([h3])
    h4, hn, s_mla = _mla_layer_fwd(h3, hn, W, mla_tabs)
    dy, loss, s_tail1 = _tail_fwd(h4, hn, p16, W, 1, "l1", target=target)

    dh4, dh4_16, g_t1, n_t1 = _tail_bwd(dy, s_tail1, p16, W, 1, "l1")
    dh3, _, g_mla, n_mla = _mla_layer_bwd(dh4, dh4_16, s_mla, W, mla_tabs)
    beg_a = _reduce_begin({**g_mla, **g_t1}, ci, "a")
    a_send, a_recv, a_src, a_land, a_token = _split_start(
        "scatter_a_start", beg_a[3], _got_shapes(beg_a[3]), 3 * len(beg_a[3]), _scatter_copies)
    dh1, dh1_16, g_t0, n_t0 = _tail_bwd(dh3, s_tail0, p16, W, 0, "l0", after=[a_token])
    d_ret_w_out = _d_ret_w_out(dh1_16, s_ret)
    beg_b = _reduce_begin({**g_t0, "ret_w_out": d_ret_w_out}, ci, "b")
    b_send, b_recv, b_src, b_land, b_token = _split_start(
        "scatter_b_start", beg_b[3], _got_shapes(beg_b[3]), 3 * len(beg_b[3]), _scatter_copies)
    stage_c = {}

    def start_c(g_ret):
        beg = _reduce_begin({"ret_w_in": g_ret["ret_w_in"]}, ci, "c")
        stage_c["beg"] = beg
        stage_c["st"] = _split_start("scatter_c_start", beg[3], _got_shapes(beg[3]), 3 * len(beg[3]), _scatter_copies)
        return [stage_c["st"][4]]

    dx, _, n_ret = _ret_layer_bwd(dh1, dh1_16, s_ret, W, ret_tabs, after=[b_token], on_grads=start_c,
                                  d_w_out=d_ret_w_out)
    got_a = _split_wait("scatter_a_wait", a_send, a_recv, a_src, a_land, _scatter_copies, after=[dx])
    got_b = _split_wait("scatter_b_wait", b_send, b_recv, b_src, b_land, _scatter_copies, after=[dx])
    got_c = _split_wait("scatter_c_wait", *stage_c["st"][:4], _scatter_copies, after=[dx])
    red = {**_reduce_end(beg_a, got_a, chip, ci), **_reduce_end(beg_b, got_b, chip, ci),
           **_reduce_end(stage_c["beg"], got_c, chip, ci)}
    red = dict(zip(red, _share_halves(list(red.values()))))
    gs = _small_grads(n_ret, n_t0, n_mla, n_t1)
    small_g = jnp.concatenate([
        gs["mix_norm"], gs["mlp_norm"], gs["ple_norm"], gs["ret_gn"].reshape(2, PACK_W), _pad_row(gs["mla_q_a_norm"]),
        _pad_row(gs["mla_kv_a_norm"]), _pad_row(gs["mla_q_norm"][:, :MLA_QKD]), _pad_row(gs["mla_k_norm"][:, :MLA_QKD]),
        _pad_row(loss[:, :1]), jnp.zeros((3, PACK_W), F32)], axis=0)
    tot = _allsum_small(small_g, "sum_small_grads")
    gn_all = tot[6:8].reshape(RET_HEADS, n, -1)
    g_small = dict(
        mix_norm=tot[0:2], mlp_norm=tot[2:4], ple_norm=tot[4:6],
        ret_gn=lax.dynamic_index_in_dim(gn_all, chip, axis=1, keepdims=False),
        mla_q_a_norm=lax.dynamic_index_in_dim(tot[8, :MLA_Q_RANK].reshape(n, -1), chip, axis=0, keepdims=True),
        mla_kv_a_norm=lax.dynamic_index_in_dim(tot[9, :MLA_KV_RANK].reshape(n, -1), chip, axis=0, keepdims=True),
        mla_q_norm=tot[10:11, :MLA_QKD], mla_k_norm=tot[11:12, :MLA_QKD])
    loss_out = tot[12, 0]

    outs = []
    for k in _ORDER:
        if k in _TWO_LAYER:
            res = None
            for i in (1, 0):
                res = _adamw(w[k], red[f"{k}_{i}"], m[k], v[k], f"adamw_{k}_{i}", layers=2, layer=i, into=res)
        elif k in red:
            res = _adamw(w[k], _shard_grad(k, red[k], w[k].shape), m[k], v[k], f"adamw_{k}")
        else:
            res = _adamw(w[k], g_small[k], m[k], v[k], f"adamw_{k}")
        outs.append(res)
    return (loss_out, dx[None], *[o[0] for o in outs], *[o[1] for o in outs], *[o[2] for o in outs],
            *[o[3] for o in outs])
```

```python
import functools

import jax
import jax.numpy as jnp
import numpy as np
from jax import lax
from jax.experimental import pallas as pl
from jax.experimental.pallas import tpu as pltpu

F32 = jnp.float32
BF16 = jnp.bfloat16

EPS = 1e-6
D_MODEL = 1024
CHUNK = 64
ROPE_THETA = 10000.0
RET_HEADS = 4
RET_DK = 256
RET_DV = 512
RET_GROUP = 1
RET_BLOCK = 256
MLA_HEADS = 8
MLA_NOPE = 128
MLA_ROPE = 64
MLA_QKD = 192
MLA_VD = 128
MLA_HP = 256
MLA_Q_RANK = 384
MLA_KV_RANK = 256
MLA_IN = 704
MLA_IN_PAD = 768
D_FF = 4096
PLE_DIM = 256
N_CHIPS = 4

ADAM_LR = 0.001
ADAM_B1 = 0.9
ADAM_B2 = 0.999
ADAM_EPS = 1e-08
ADAM_WD = 0.01
ADAM_STEP = 10

VMEM_LIMIT = 56 * 1024 * 1024
PACK_W = 1024
NEG = -1e30
LOG2E = 1.4426950408889634
FLASH_T = 512
FLASH_HEADS = 2
MM_SUB_ROWS = 256


def _cparams(sem=None):
    return pltpu.CompilerParams(dimension_semantics=sem, vmem_limit_bytes=VMEM_LIMIT)


def _pick(dim, pref):
    if dim <= pref:
        return dim
    t = pref
    while dim % t:
        t //= 2
    return t


def _mm(a, b, *, name, ta=False, tb=False, bblk=False, outs=None, extras=(), epilogue=None, dw=None,
        tm=1024, tn=512, after=()):
    if ta:
        K, M = a.shape
    else:
        M, K = a.shape
    if bblk and tb:
        nb, N, Kq = b.shape
        assert nb * Kq == K
    elif bblk:
        nb, Kb, Nq = b.shape
        N = nb * Nq
        assert Kb == K
    else:
        N = b.shape[0] if tb else b.shape[1]
    tn = _pick(Nq if (bblk and not tb) else N, tn)
    if dw is not None and dw[0] == "cols":
        tn = _pick(N // N_CHIPS, tn)
    tm = _pick(M // N_CHIPS if (dw is not None and dw[0] == "rows") else M, tm)
    grid = (M // tm, N // tn)

    a_spec = pl.BlockSpec((K, tm), lambda i, j: (0, i)) if ta else pl.BlockSpec((tm, K), lambda i, j: (i, 0))
    if bblk and tb:
        b_spec = pl.BlockSpec((nb, tn, Kq), lambda i, j: (0, j, 0))
    elif bblk:
        npb = Nq // tn
        b_spec = pl.BlockSpec((None, K, tn), lambda i, j: (j // npb, 0, j % npb))
    elif tb:
        b_spec = pl.BlockSpec((tn, K), lambda i, j: (j, 0))
    else:
        b_spec = pl.BlockSpec((K, tn), lambda i, j: (0, j))
    in_specs = [a_spec, b_spec] + [pl.BlockSpec((tm, tn), lambda i, j: (i, j)) for _ in extras]
    args = [a, b, *extras]
    aliases = {}
    if outs is None:
        outs = [F32]
    if dw is None:
        o_specs = [pl.BlockSpec((tm, tn), lambda i, j: (i, j)) for _ in outs]
        o_shapes = [jax.ShapeDtypeStruct((M, N), dt) for dt in outs]
    else:
        kind, layers, layer, into = dw
        if kind == "cols":
            per = (N // N_CHIPS) // tn
            o_specs = [pl.BlockSpec((None, None, tm, tn), lambda i, j: (j // per, layer, i, j % per))]
            o_shapes = [jax.ShapeDtypeStruct((N_CHIPS, layers, M, N // N_CHIPS), outs[0])]
        else:
            per = (M // N_CHIPS) // tm
            o_specs = [pl.BlockSpec((None, None, tm, tn), lambda i, j: (i // per, layer, i % per, j))]
            o_shapes = [jax.ShapeDtypeStruct((N_CHIPS, layers, M // N_CHIPS, N), outs[0])]
        if into is not None:
            aliases = {len(args): 0}
            in_specs.append(pl.BlockSpec(memory_space=pl.ANY))
            args.append(into)
    for t in after:
        in_specs.append(pl.BlockSpec(memory_space=pl.ANY))
        args.append(t)
    n_e, n_o = len(extras), len(outs)

    sub = _pick(tm, MM_SUB_ROWS)

    def body(a_ref, b_ref, *rest):
        e_refs, o_refs = rest[:n_e], rest[len(rest) - n_o:]
        for r0 in range(0, tm, sub):
            rows = slice(r0, r0 + sub)
            av = (a_ref[:, rows] if ta else a_ref[rows, :]).astype(BF16)
            if bblk and tb:
                acc = _dot_nt(av[:, :Kq], b_ref[0].astype(BF16))
                for s in range(1, nb):
                    acc = acc + _dot_nt(av[:, s * Kq:(s + 1) * Kq], b_ref[s].astype(BF16))
            elif ta:
                acc = _dot_tn(av, b_ref[...].astype(BF16))
            elif tb:
                acc = _dot_nt(av, b_ref[...].astype(BF16))
            else:
                acc = _dot(av, b_ref[...].astype(BF16))
            vals = (acc,) if epilogue is None else epilogue(acc, *[e[rows, :] for e in e_refs])
            for o, v in zip(o_refs, vals):
                o[rows, :] = v.astype(o.dtype)

    res = pl.pallas_call(
        body, name=name, grid=grid, in_specs=in_specs, out_specs=o_specs, out_shape=o_shapes,
        input_output_aliases=aliases, compiler_params=_cparams(("parallel", "arbitrary")),
    )(*args)
    return res[0] if n_o == 1 else res


def _mm_rows(a, b, *, name, epilogue, outs, tb=False, bblk=False, extras=(), fulls=(), accs=(), tm=512, after=()):
    M, K = a.shape
    tm = _pick(M, tm)
    sub = _pick(tm, MM_SUB_ROWS)
    nb = b.shape[0] if bblk else 1
    n_e, n_f, n_o, n_a = len(extras), len(fulls), len(outs), len(accs)
    n_in = 2 + n_e + n_f + len(after)

    def whole(t):
        return pl.BlockSpec(t.shape, lambda i, nd=t.ndim: (0,) * nd)

    in_specs = [pl.BlockSpec((tm, K), lambda i: (i, 0)), whole(b)]
    in_specs += [pl.BlockSpec((tm, e.shape[1]), lambda i: (i, 0)) for e in extras] + [whole(f) for f in fulls]
    in_specs += [pl.BlockSpec(memory_space=pl.ANY) for _ in after]
    out_specs = [pl.BlockSpec((tm, w), lambda i: (i, 0)) for w, _ in outs] + [pl.BlockSpec(s, lambda i: (0, 0)) for s, _ in accs]
    out_shape = [jax.ShapeDtypeStruct((M, w), dt) for w, dt in outs] + [jax.ShapeDtypeStruct(s, dt) for s, dt in accs]

    def body(a_ref, b_ref, *rest):
        e_refs, f_refs = rest[:n_e], rest[n_e:n_e + n_f]
        o_refs, acc_refs = rest[n_in - 2:n_in - 2 + n_o], rest[n_in - 2 + n_o:]
        fv = [f[...] for f in f_refs]
        totals = None
        for r0 in range(0, tm, sub):
            rows = slice(r0, r0 + sub)
            av = a_ref[rows, :].astype(BF16)
            if bblk and tb:
                kq = K // nb
                acc = _dot_nt(av[:, :kq], b_ref[0])
                for s in range(1, nb):
                    acc = acc + _dot_nt(av[:, s * kq:(s + 1) * kq], b_ref[s])
            elif bblk:
                acc = jnp.concatenate([_dot(av, b_ref[s]) for s in range(nb)], axis=-1)
            elif tb:
                acc = _dot_nt(av, b_ref[...])
            else:
                acc = _dot(av, b_ref[...])
            vals = epilogue(acc, *[e[rows, :] for e in e_refs], *fv)
            for o, v in zip(o_refs, vals[:n_o]):
                o[rows, :] = v.astype(o.dtype)
            part = vals[n_o:]
            totals = part if totals is None else [t + p for t, p in zip(totals, part)]
        first_step = pl.program_id(0) == 0
        for o, v in zip(acc_refs, totals):
            @pl.when(first_step)
            def _(o=o, v=v):
                o[...] = v.astype(o.dtype)

            @pl.when(jnp.logical_not(first_step))
            def _(o=o, v=v):
                o[...] += v.astype(o.dtype)

    return pl.pallas_call(
        body, name=name, grid=(M // tm,), in_specs=in_specs, out_specs=out_specs, out_shape=out_shape,
        compiler_params=_cparams(("arbitrary",)),
    )(a, b, *extras, *fulls, *after)


def _rows(fn, rows, fulls, outs, accs=(), *, name, tile=512, after=()):
    first = rows[0][0] if isinstance(rows[0], tuple) else rows[0]
    T = first.shape[0]
    tile = _pick(T, tile)
    in_specs, args = [], []
    for r in rows:
        if isinstance(r, tuple):
            arr, w, cb = r
            in_specs.append(pl.BlockSpec((tile, w), lambda i, cb=cb: (i, cb)))
        else:
            arr = r
            in_specs.append(pl.BlockSpec((tile, arr.shape[1]), lambda i: (i, 0)))
        args.append(arr)
    for f in fulls:
        in_specs.append(pl.BlockSpec(f.shape, lambda i, nd=f.ndim: (0,) * nd))
        args.append(f)
    outs = [o if len(o) == 4 else (*o, o[0], 0) for o in outs]
    out_specs = [pl.BlockSpec((tile, w), lambda i, cb=cb: (i, cb)) for w, _, _, cb in outs]
    out_specs += [pl.BlockSpec(s, lambda i: (0, 0)) for s, _ in accs]
    out_shape = [jax.ShapeDtypeStruct((T, tw), dt) for _, dt, tw, _ in outs]
    out_shape += [jax.ShapeDtypeStruct(s, dt) for s, dt in accs]
    n_in, n_out = len(args), len(outs)
    for t in after:
        in_specs.append(pl.BlockSpec(memory_space=pl.ANY))
        args.append(t)

    def body(*refs):
        vals = fn(*[r[...] for r in refs[:n_in]])
        o_refs = refs[len(args):]
        for o, v in zip(o_refs[:n_out], vals[:n_out]):
            o[...] = v.astype(o.dtype)
        first_step = pl.program_id(0) == 0
        for o, v in zip(o_refs[n_out:], vals[n_out:]):
            @pl.when(first_step)
            def _(o=o, v=v):
                o[...] = v.astype(o.dtype)

            @pl.when(jnp.logical_not(first_step))
            def _(o=o, v=v):
                o[...] += v.astype(o.dtype)

    res = pl.pallas_call(
        body, name=name, grid=(T // tile,), in_specs=in_specs, out_specs=out_specs, out_shape=out_shape,
        compiler_params=_cparams(("arbitrary",)),
    )(*args)
    return res


def _rowsum(v, mxu):
    if not mxu:
        return jnp.sum(v, axis=-1, keepdims=True)
    ones = jnp.ones((v.shape[1], v.shape[1]), BF16)
    hi = v.astype(BF16)
    lo = (v - hi.astype(F32)).astype(BF16)
    return _dot(hi, ones) + _dot(lo, ones)


def _rms(x, g, mxu=False):
    r = lax.rsqrt(_rowsum(x * x, mxu) / x.shape[-1] + EPS)
    return (x * r) * g


def _rms_bwd(x, dy, g, n=None, mxu=False):
    n = x.shape[-1] if n is None else n
    r = lax.rsqrt(_rowsum(x * x, mxu) / n + EPS)
    xh = x * r
    dxh = dy * g
    dx = r * (dxh - xh * (_rowsum(dxh * xh, mxu) / n))
    return dx, dy * xh


def _colsum(v):
    return jnp.sum(v, axis=0, keepdims=True)


def _sigmoid(x):
    return 1.0 / (1.0 + jnp.exp(-x))


def _widen(v, width):
    reps = width // v.shape[1]
    return v if reps == 1 else jnp.concatenate([v] * reps, axis=-1)


def _rope_angles(T, dim):
    inv = (1.0 / (np.float32(ROPE_THETA) ** (np.arange(0, dim, 2, dtype=np.float32) / np.float32(dim)))).astype(np.float32)
    return np.arange(T, dtype=np.float32)[:, None] * inv[None, :]


def _ret_tables(T):
    ang = _rope_angles(T, RET_DK)
    log_gamma = np.log(np.float32(1.0) - np.float32(2.0) ** (-5.0 - np.arange(RET_HEADS, dtype=np.float32)))
    idx = np.arange(RET_BLOCK, dtype=np.float32)
    chunk = np.arange(RET_BLOCK) // CHUNK
    dist = idx[:, None] - idx[None, :]
    seen = np.where(chunk[:, None] == chunk[None, :], np.abs(dist), np.where(chunk[:, None] > chunk[None, :], dist, np.inf))
    intra = np.exp(log_gamma[:, None, None] * seen[None].astype(np.float32))
    qd = np.exp(log_gamma[:, None] * (idx + 1.0))[:, :, None]
    kd = np.exp(log_gamma[:, None] * (RET_BLOCK - 1.0 - idx))[:, :, None]
    cd = np.exp(log_gamma * RET_BLOCK)[:, None, None]
    return tuple(jnp.asarray(t, F32) for t in (np.cos(ang), np.sin(ang), intra, qd, kd, cd))


def _rope_half(x, c, s):
    x1, x2 = x[:, :RET_DK // 2], x[:, RET_DK // 2:]
    return jnp.concatenate([x1 * c - x2 * s, x2 * c + x1 * s], axis=-1)


def _rope_half_bwd(d, c, s):
    d1, d2 = d[:, :RET_DK // 2], d[:, RET_DK // 2:]
    return jnp.concatenate([d1 * c + d2 * s, d2 * c - d1 * s], axis=-1)


def _dot(a, b):
    return lax.dot_general(a, b, (((1,), (0,)), ((), ())), preferred_element_type=F32)


def _dot_nt(a, b):
    return lax.dot_general(a, b, (((1,), (1,)), ((), ())), preferred_element_type=F32)


def _dot_tn(a, b):
    return lax.dot_general(a, b, (((0,), (0,)), ((), ())), preferred_element_type=F32)


def _ret_specs(T, tb, rev):
    nj = T // tb
    jj = (lambda j: nj - 1 - j) if rev else (lambda j: j)
    g = RET_GROUP
    kq = RET_HEADS // g
    vq = 2 * RET_HEADS * RET_DK // (g * RET_DV)
    return dict(
        q=pl.BlockSpec((tb, g * RET_DK), lambda h, j: (jj(j), h)),
        k=pl.BlockSpec((tb, g * RET_DK), lambda h, j: (jj(j), kq + h)),
        v=pl.BlockSpec((tb, g * RET_DV), lambda h, j: (jj(j), vq + h)),
        tab=pl.BlockSpec((tb, RET_DK // 2), lambda h, j: (jj(j), 0)),
        intra=pl.BlockSpec((g, RET_BLOCK, RET_BLOCK), lambda h, j: (h, 0, 0)),
        dec=pl.BlockSpec((g, RET_BLOCK, 1), lambda h, j: (h, 0, 0)),
        cd=pl.BlockSpec((g, 1, 1), lambda h, j: (h, 0, 0)),
        o=pl.BlockSpec((tb, g * RET_DV), lambda h, j: (jj(j), h)),
        s=pl.BlockSpec((g, tb // RET_BLOCK, RET_DK, RET_DV), lambda h, j: (h, jj(j), 0, 0)),
    )


def _ret_fwd(proj, tabs, name):
    T = proj.shape[0]
    cos, sin, intra, qd, kd, cd = tabs
    tb = _pick(T, 512)
    cps = tb // RET_BLOCK
    sp = _ret_specs(T, tb, False)
    scale = RET_DK ** -0.5

    def body(q_ref, k_ref, v_ref, cos_ref, sin_ref, intra_ref, qd_ref, kd_ref, cd_ref, o_ref, s_ref, state):
        @pl.when(pl.program_id(1) == 0)
        def _():
            state[...] = jnp.zeros_like(state)

        for c in range(cps):
            rows = pl.ds(c * RET_BLOCK, RET_BLOCK)
            co, si = cos_ref[rows, :], sin_ref[rows, :]
            for h in range(RET_GROUP):
                hk, hv = slice(h * RET_DK, (h + 1) * RET_DK), slice(h * RET_DV, (h + 1) * RET_DV)
                q = _rope_half(q_ref[rows, hk].astype(F32), co, si)
                k = _rope_half(k_ref[rows, hk].astype(F32), co, si) * scale
                vb = v_ref[rows, hv].astype(BF16)
                st = state[h]
                sb = st.astype(BF16)
                s_ref[h, c] = sb
                sc = _dot_nt(q.astype(BF16), k.astype(BF16)) * intra_ref[h]
                inner = _dot(sc.astype(BF16), vb)
                cross = _dot((q * qd_ref[h]).astype(BF16), sb)
                o_ref[rows, hv] = inner + cross
                state[h] = st * cd_ref[h] + _dot_tn((k * kd_ref[h]).astype(BF16), vb)

    return pl.pallas_call(
        body, name=name, grid=(RET_HEADS // RET_GROUP, T // tb),
        in_specs=[sp["q"], sp["k"], sp["v"], sp["tab"], sp["tab"], sp["intra"], sp["dec"], sp["dec"], sp["cd"]],
        out_specs=[sp["o"], sp["s"]],
        out_shape=[jax.ShapeDtypeStruct((T, RET_HEADS * RET_DV), F32),
                   jax.ShapeDtypeStruct((RET_HEADS, T // RET_BLOCK, RET_DK, RET_DV), BF16)],
        scratch_shapes=[pltpu.VMEM((RET_GROUP, RET_DK, RET_DV), F32)],
        compiler_params=_cparams(("arbitrary", "arbitrary")),
    )(proj, proj, proj, cos, sin, intra, qd, kd, cd)


def _ret_bwd(proj, states, dout, dproj, tabs, name):
    assert RET_GROUP == 1
    T = proj.shape[0]
    cos, sin, intra, qd, kd, cd = tabs
    tb = _pick(T, 512)
    cps = tb // RET_BLOCK
    nj = T // tb
    sp = _ret_specs(T, tb, True)
    scale = RET_DK ** -0.5
    k0, v0 = RET_HEADS * RET_DK, 2 * RET_HEADS * RET_DK

    def body(q_ref, k_ref, v_ref, cos_ref, sin_ref, intra_ref, qd_ref, kd_ref, cd_ref, s_ref, do_ref, _dproj_in,
             out_ref, dq_s, dk_s, dv_s, sems, dstate):
        head, j = pl.program_id(0), pl.program_id(1)
        step = head * nj + j
        slot = step % 2
        dq_ref, dk_ref, dv_ref = dq_s.at[slot], dk_s.at[slot], dv_s.at[slot]

        @pl.when(j == 0)
        def _():
            dstate[...] = jnp.zeros_like(dstate)

        for c in reversed(range(cps)):
            rows = pl.ds(c * RET_BLOCK, RET_BLOCK)
            co, si = cos_ref[rows, :], sin_ref[rows, :]
            for h in range(RET_GROUP):
                hk, hv = slice(h * RET_DK, (h + 1) * RET_DK), slice(h * RET_DV, (h + 1) * RET_DV)
                q = _rope_half(q_ref[rows, hk].astype(F32), co, si)
                k = _rope_half(k_ref[rows, hk].astype(F32), co, si) * scale
                qb, kb = q.astype(BF16), k.astype(BF16)
                vb = v_ref[rows, hv].astype(BF16)
                dob = do_ref[rows, hv].astype(BF16)
                sb = s_ref[h, c]
                ia = intra_ref[h]
                pb = (_dot_nt(qb, kb) * ia).astype(BF16)
                dsn = dstate[h]
                dsb = dsn.astype(BF16)
                kdk = (k * kd_ref[h]).astype(BF16)
                qdq = (q * qd_ref[h]).astype(BF16)
                dv = _dot_tn(pb, dob) + _dot(kdk, dsb)
                dpb = (_dot_nt(dob, vb) * ia).astype(BF16)
                dq = _dot(dpb, kb) + _dot_nt(dob, sb) * qd_ref[h]
                dk = _dot_tn(dpb, qb) + _dot_nt(vb, dsb) * kd_ref[h]
                dstate[h] = dsn * cd_ref[h] + _dot_tn(qdq, dob)
                dq_ref[rows, hk] = _rope_half_bwd(dq, co, si).astype(BF16)
                dk_ref[rows, hk] = _rope_half_bwd(dk * scale, co, si).astype(BF16)
                dv_ref[rows, hv] = dv.astype(BF16)

        def copies(sl):
            r = pl.ds(pl.multiple_of((nj - 1 - j) * tb, tb), tb)
            cols = lambda first, w: pl.ds(pl.multiple_of(first + head * w, 128), w)
            return [pltpu.make_async_copy(dq_s.at[sl], out_ref.at[r, cols(0, RET_DK)], sems.at[sl, 0]),
                    pltpu.make_async_copy(dk_s.at[sl], out_ref.at[r, cols(k0, RET_DK)], sems.at[sl, 1]),
                    pltpu.make_async_copy(dv_s.at[sl], out_ref.at[r, cols(v0, RET_DV)], sems.at[sl, 2])]

        @pl.when(step > 0)
        def _():
            for cp in copies(1 - slot):
                cp.wait()

        for cp in copies(slot):
            cp.start()

        @pl.when(step == RET_HEADS * nj - 1)
        def _():
            for cp in copies(slot):
                cp.wait()

    return pl.pallas_call(
        body, name=name, grid=(RET_HEADS, nj),
        in_specs=[sp["q"], sp["k"], sp["v"], sp["tab"], sp["tab"], sp["intra"], sp["dec"], sp["dec"], sp["cd"],
                  sp["s"], sp["o"], pl.BlockSpec(memory_space=pl.ANY)],
        out_specs=pl.BlockSpec(memory_space=pl.ANY), out_shape=jax.ShapeDtypeStruct(dproj.shape, dproj.dtype),
        input_output_aliases={11: 0},
        scratch_shapes=[pltpu.VMEM((2, tb, RET_DK), BF16), pltpu.VMEM((2, tb, RET_DK), BF16),
                        pltpu.VMEM((2, tb, RET_DV), BF16), pltpu.SemaphoreType.DMA((2, 3)),
                        pltpu.VMEM((RET_GROUP, RET_DK, RET_DV), F32)],
        compiler_params=_cparams(("arbitrary", "arbitrary")),
    )(proj, proj, proj, cos, sin, intra, qd, kd, cd, states, dout, dproj)


def _ret_gate(out, proj, gn, name):
    def fn(o, g, *gains):
        g = g.astype(F32)
        parts = [_rms(o[:, h * RET_DV:(h + 1) * RET_DV], gains[h]) for h in range(RET_HEADS)]
        return (g * _sigmoid(g) * jnp.concatenate(parts, axis=-1),)
    w = RET_HEADS * RET_DV
    return _rows(fn, [out, (proj, w, 2)], [gn[h:h + 1] for h in range(RET_HEADS)], [(w, BF16)], name=name)[0]


def _ret_gate_bwd(out, proj, gn, dy, name):
    def fn(o, g, d, *gains):
        g = g.astype(F32)
        sg = _sigmoid(g)
        silu = g * sg
        dsilu = sg * (1.0 + g * (1.0 - sg))
        dos, dgs = [], []
        row = lax.broadcasted_iota(jnp.int32, (RET_HEADS, RET_DV), 0)
        dgn = jnp.zeros((RET_HEADS, RET_DV), F32)
        for h in range(RET_HEADS):
            sl = slice(h * RET_DV, (h + 1) * RET_DV)
            oh = o[:, sl]
            dgs.append(d[:, sl] * _rms(oh, gains[h]) * dsilu[:, sl])
            dx, dg = _rms_bwd(oh, d[:, sl] * silu[:, sl], gains[h])
            dos.append(dx)
            dgn = dgn + jnp.where(row == h, _colsum(dg), 0.0)
        return jnp.concatenate(dos, axis=-1), jnp.concatenate(dgs, axis=-1), dgn
    w = RET_HEADS * RET_DV
    return _rows(fn, [out, (proj, w, 2), dy], [gn[h:h + 1] for h in range(RET_HEADS)],
                 [(w, BF16), (w, BF16, proj.shape[1], 2)], [((RET_HEADS, RET_DV), F32)], name=name, tile=128)


def _mla_tables(T):
    ang = _rope_angles(T, MLA_ROPE)
    c, s = np.cos(ang), np.sin(ang)
    z32, z64 = np.zeros((T, 32), np.float32), np.zeros((T, 64), np.float32)
    cos_t = np.concatenate([c, c, z64], axis=1)
    sin_a = np.concatenate([-s, z32, z64], axis=1)
    sin_b = np.concatenate([z32, s, z64], axis=1)
    return tuple(jnp.asarray(t, F32) for t in (cos_t, sin_a, sin_b))


def _rope_blk(x, ct, sa, sb):
    return x * ct + pltpu.roll(x, 96, 1) * sa + pltpu.roll(x, 32, 1) * sb


def _rope_blk_bwd(d, ct, sa, sb):
    return d * ct + pltpu.roll(d * sa, 32, 1) + pltpu.roll(d * sb, 96, 1)


def _head_norm(x, gain):
    r = lax.rsqrt(_rowsum(x * x, True) / MLA_QKD + EPS)
    return (x * r) * gain


def _prep_heads(qv, kvv, kr, ct, sa, sb, gqv, gkv):
    qs, ks, vs = [], [], []
    for h in range(MLA_HEADS):
        b = h * MLA_HP
        y = _head_norm(qv[:, b:b + MLA_HP], gqv)
        qs += [y[:, :128], _rope_blk(y[:, 128:], ct, sa, sb)]
        y = _head_norm(jnp.concatenate([kvv[:, b:b + 128], kr], axis=-1), gkv)
        ks += [y[:, :128], _rope_blk(y[:, 128:], ct, sa, sb)]
        vs.append(kvv[:, b + 128:b + 256])
    return jnp.concatenate(qs, axis=-1), jnp.concatenate(ks, axis=-1), jnp.concatenate(vs, axis=-1)


def _mla_front(hn, W, tabs, name):
    wide = MLA_HEADS * MLA_HP
    gq = W["mla_q_norm"] * (MLA_QKD ** -0.5 * LOG2E)

    def epilogue(acc, ct, sa, sb, gqa, gkva, wuq, wukv, gqv, gkv):
        cqn = _rms(acc[:, :MLA_Q_RANK], gqa).astype(BF16)
        ckvn = _rms(acc[:, MLA_Q_RANK:MLA_Q_RANK + MLA_KV_RANK], gkva).astype(BF16)
        q = jnp.concatenate([_dot(cqn, wuq[s]) for s in range(N_CHIPS)], axis=-1).astype(BF16)
        kv = jnp.concatenate([_dot(ckvn, wukv[s]) for s in range(N_CHIPS)], axis=-1).astype(BF16)
        qf, kf, vf = _prep_heads(q.astype(F32), kv.astype(F32), acc[:, MLA_IN_PAD - 128:], ct, sa, sb, gqv, gkv)
        return acc, cqn, ckvn, q, kv, qf, kf, vf

    return _mm_rows(hn, W["mla_w_in"], extras=list(tabs),
                    fulls=[W["mla_q_a_norm"], W["mla_kv_a_norm"], W["mla_w_uq"], W["mla_w_ukv"], gq, W["mla_k_norm"]],
                    outs=[(MLA_IN_PAD, F32), (MLA_Q_RANK, BF16), (MLA_KV_RANK, BF16), (wide, BF16), (wide, BF16),
                          (wide, BF16), (wide, BF16), (MLA_HEADS * MLA_VD, BF16)],
                    epilogue=epilogue, name=name, tm=256)


def _prep_heads_bwd(qv, kvv, kr, ct, sa, sb, dqv, dkv, dvv, gqv, gkv):
    dqs, dkvs = [], []
    dkr = jnp.zeros_like(kr)
    dgq = jnp.zeros((1, MLA_HP), F32)
    dgk = jnp.zeros((1, MLA_HP), F32)
    for h in range(MLA_HEADS):
        b = h * MLA_HP
        dy = jnp.concatenate([dqv[:, b:b + 128], _rope_blk_bwd(dqv[:, b + 128:b + 256], ct, sa, sb)], axis=-1)
        dx, dg = _rms_bwd(qv[:, b:b + MLA_HP], dy, gqv, MLA_QKD, mxu=True)
        dqs.append(dx)
        dgq = dgq + _colsum(dg)
        dy = jnp.concatenate([dkv[:, b:b + 128], _rope_blk_bwd(dkv[:, b + 128:b + 256], ct, sa, sb)], axis=-1)
        dx, dg = _rms_bwd(jnp.concatenate([kvv[:, b:b + 128], kr], axis=-1), dy, gkv, MLA_QKD, mxu=True)
        dkvs += [dx[:, :128], dvv[:, h * MLA_VD:(h + 1) * MLA_VD].astype(F32)]
        dkr = dkr + dx[:, 128:]
        dgk = dgk + _colsum(dg)
    return jnp.concatenate(dqs, axis=-1), jnp.concatenate(dkvs, axis=-1), dkr, dgq, dgk


def _mla_back(q, kv, proj, h0, dh1, dqf, dkf, dvf, W, tabs, name):
    def fn(qv, kvv, pv, hv, dr, ct, sa, sb, dqv, dkv, dvv, gqv, gkv, gqa, gkva, wuq, wukv, w_in, g_mix):
        qv, kvv, dqv, dkv = (t.astype(F32) for t in (qv, kvv, dqv, dkv))
        dq, dkvx, dkr, dgq, dgk = _prep_heads_bwd(qv, kvv, pv[:, MLA_IN_PAD - 128:], ct, sa, sb, dqv, dkv, dvv, gqv, gkv)
        dq, dkvx = dq.astype(BF16), dkvx.astype(BF16)
        nq = wuq.shape[2]
        dcq = sum(_dot_nt(dq[:, s * nq:(s + 1) * nq], wuq[s]) for s in range(N_CHIPS))
        dckv = sum(_dot_nt(dkvx[:, s * nq:(s + 1) * nq], wukv[s]) for s in range(N_CHIPS))
        dxq, dgqa = _rms_bwd(pv[:, :MLA_Q_RANK], dcq, gqa)
        dxkv, dgkva = _rms_bwd(pv[:, MLA_Q_RANK:MLA_Q_RANK + MLA_KV_RANK], dckv, gkva)
        dproj = jnp.concatenate([dxq, dxkv, dkr], axis=-1).astype(BF16)
        dx, dgm = _rms_bwd(hv, _dot_nt(dproj, w_in), g_mix)
        return (dq, dkvx, dproj, dr + dx, dr + dx, dgq, dgk, _colsum(dgqa), _colsum(dgkva), _colsum(dgm))

    wide = MLA_HEADS * MLA_HP
    return _rows(fn, [q, kv, proj, h0, dh1, *tabs, dqf, dkf, dvf],
                 [W["mla_q_norm"], W["mla_k_norm"], W["mla_q_a_norm"], W["mla_kv_a_norm"], W["mla_w_uq"], W["mla_w_ukv"],
                  W["mla_w_in"], W["mix_norm"][1:2]],
                 [(wide, BF16), (wide, BF16), (MLA_IN_PAD, BF16), ROW_F32, ROW_BF16],
                 [((1, MLA_HP), F32), ((1, MLA_HP), F32), ((1, MLA_Q_RANK), F32), ((1, MLA_KV_RANK), F32),
                  ((1, D_MODEL), F32)], name=name, tile=256)


def _chunk_mask(qi, ki, tq, tk):
    shift = CHUNK.bit_length() - 1
    rq = lax.shift_right_arithmetic(qi * tq + lax.broadcasted_iota(jnp.int32, (tq, tk), 0), shift)
    ck = lax.shift_right_arithmetic(ki * tk + lax.broadcasted_iota(jnp.int32, (tq, tk), 1), shift)
    return ck <= rq


def _flash_fwd(qf, kf, vf, name):
    T = qf.shape[0]
    t = _pick(T, FLASH_T)
    n = T // t
    scale = MLA_QKD ** -0.5

    g = FLASH_HEADS

    def body(q_ref, k_ref, v_ref, o_ref, lse_ref, m_s, l_s, acc):
        qi = pl.program_id(1)
        m_s[...] = jnp.full_like(m_s, NEG)
        l_s[...] = jnp.zeros_like(l_s)
        acc[...] = jnp.zeros_like(acc)

        def step(kb, masked):
            rows = pl.ds(pl.multiple_of(kb * t, t), t)
            for h in range(g):
                hq, hv = slice(h * MLA_HP, (h + 1) * MLA_HP), slice(h * MLA_VD, (h + 1) * MLA_VD)
                s = _dot_nt(q_ref[:, hq], k_ref[rows, hq])
                if masked:
                    s = jnp.where(_chunk_mask(0, 0, t, t), s, NEG)
                m_prev = m_s[:, hv]
                m_new = jnp.maximum(m_prev, jnp.max(s, axis=-1, keepdims=True))
                alpha = jnp.exp2(m_prev - m_new)
                p = jnp.exp2(s - _widen(m_new, t))
                l_s[:, hv] = alpha * l_s[:, hv] + sum(p[:, i * 128:(i + 1) * 128] for i in range(t // 128))
                acc[:, hv] = acc[:, hv] * alpha + _dot(p.astype(BF16), v_ref[rows, hv])
                m_s[:, hv] = m_new

        @pl.loop(0, qi)
        def _(kb):
            step(kb, False)

        step(qi, True)
        for h in range(g):
            hv = slice(h * MLA_VD, (h + 1) * MLA_VD)
            l = jnp.sum(l_s[:, hv], axis=-1, keepdims=True)
            o_ref[:, hv] = acc[:, hv] / l
            lse_ref[:, hv] = m_s[:, hv] + jnp.log2(l)

    qmap = lambda h, i: (i, h)
    kmap = lambda h, i: (0, h)
    vec = pltpu.VMEM((t, g * MLA_VD), F32)
    return pl.pallas_call(
        body, name=name, grid=(MLA_HEADS // g, n),
        in_specs=[pl.BlockSpec((t, g * MLA_HP), qmap), pl.BlockSpec((T, g * MLA_HP), kmap),
                  pl.BlockSpec((T, g * MLA_VD), kmap)],
        out_specs=[pl.BlockSpec((t, g * MLA_VD), qmap), pl.BlockSpec((t, g * MLA_VD), qmap)],
        out_shape=[jax.ShapeDtypeStruct((T, MLA_HEADS * MLA_VD), F32),
                   jax.ShapeDtypeStruct((T, MLA_HEADS * MLA_VD), F32)],
        scratch_shapes=[vec, vec, vec],
        compiler_params=_cparams(("parallel", "arbitrary")),
    )(qf, kf, vf)


def _flash_bwd(qf, kf, vf, do16, lse, delta, name):
    T = qf.shape[0]
    t = _pick(T, FLASH_T)
    n = T // t
    scale = MLA_QKD ** -0.5

    def body(q_ref, k_ref, v_ref, do_ref, lse_ref, dl_ref, dq_out, dk_out, dv_out, dq_ref, dk_ref, dv_ref):
        kb = pl.program_id(1)

        @pl.when(kb == 0)
        def _():
            dq_ref[...] = jnp.zeros_like(dq_ref)

        dk_ref[...] = jnp.zeros_like(dk_ref)
        dv_ref[...] = jnp.zeros_like(dv_ref)
        k, v = k_ref[...], v_ref[...]

        def step(qb, masked):
            rows = pl.ds(pl.multiple_of(qb * t, t), t)
            q, dob = q_ref[rows, :], do_ref[rows, :]
            s = _dot_nt(q, k)
            if masked:
                s = jnp.where(_chunk_mask(0, 0, t, t), s, NEG)
            p = jnp.exp2(s - _widen(lse_ref[rows, :], t))
            ds = (p * (_dot_nt(dob, v) - _widen(dl_ref[rows, :], t))).astype(BF16)
            dv_ref[...] += _dot_tn(p.astype(BF16), dob)
            dk_ref[...] += _dot_tn(ds, q)
            dq_ref[rows, :] += _dot(ds, k)

        step(kb, True)

        @pl.loop(kb + 1, n)
        def _(qb):
            step(qb, False)

        dk_out[...] = (dk_ref[...] * (1.0 / LOG2E)).astype(BF16)
        dv_out[...] = dv_ref[...].astype(BF16)

        @pl.when(kb == n - 1)
        def _():
            dq_out[...] = (dq_ref[...] * scale).astype(BF16)

    qmap = lambda h, j: (0, h)
    kmap = lambda h, j: (j, h)
    return pl.pallas_call(
        body, name=name, grid=(MLA_HEADS, n),
        in_specs=[pl.BlockSpec((T, MLA_HP), qmap), pl.BlockSpec((t, MLA_HP), kmap), pl.BlockSpec((t, MLA_VD), kmap),
                  pl.BlockSpec((T, MLA_VD), qmap), pl.BlockSpec((T, MLA_VD), qmap), pl.BlockSpec((T, MLA_VD), qmap)],
        out_specs=[pl.BlockSpec((T, MLA_HP), qmap), pl.BlockSpec((t, MLA_HP), kmap), pl.BlockSpec((t, MLA_VD), kmap)],
        out_shape=[jax.ShapeDtypeStruct((T, MLA_HEADS * MLA_HP), BF16),
                   jax.ShapeDtypeStruct((T, MLA_HEADS * MLA_HP), BF16),
                   jax.ShapeDtypeStruct((T, MLA_HEADS * MLA_VD), BF16)],
        scratch_shapes=[pltpu.VMEM((T, MLA_HP), F32), pltpu.VMEM((t, MLA_HP), F32), pltpu.VMEM((t, MLA_VD), F32)],
        compiler_params=_cparams(("arbitrary", "arbitrary")),
    )(qf, kf, vf, do16, lse, delta)


MESH = pl.DeviceIdType.MESH
ANY = pl.BlockSpec(memory_space=pl.ANY)
_CHIP_FLIPS = ((1, 0), (0, 1), (1, 1))


def _place():
    return lax.axis_index("x"), lax.axis_index("y"), lax.axis_index("c")


def _other_chip(x, y, k):
    fx, fy = _CHIP_FLIPS[k]
    return ((1 - x) if fx else x), ((1 - y) if fy else y)


def _remote(src, dst, send_sems, recv_sems, k, to):
    return pltpu.make_async_remote_copy(src_ref=src, dst_ref=dst, send_sem=send_sems.at[k], recv_sem=recv_sems.at[k],
                                        device_id=to, device_id_type=MESH)


def _index(*vals):
    return jnp.stack(vals).astype(jnp.int32)


def _half(c, rows):
    return pl.ds(pl.multiple_of(c * rows, 16), rows)


def _gather_weights(parts, name, landed=None):
    n_w = len(parts)
    n_in = n_w if landed is None else 2 * n_w

    def body(*refs):
        ins, outs = refs[:n_w], refs[n_in:n_in + n_w]
        send_sems, recv_sems, local_sems = refs[n_in + n_w:]
        x, y, c = _place()
        j = 2 * x + y
        sibling = (x, y, 1 - c)
        chips = [_other_chip(x, y, k) for k in range(3)]
        pending = []
        for w in range(n_w):
            own = pltpu.make_async_copy(ins[w], outs[w].at[j], local_sems.at[w])
            own.start()
            pending.append(own)
        sent = []
        for w in range(n_w):
            if landed is not None:
                break
            r = _half(c, parts[w].shape[0] // 2)
            for k, (px, py) in enumerate(chips):
                cp = _remote(ins[w].at[r], outs[w].at[j, r], send_sems, recv_sems, 6 * w + k, (px, py, c))
                cp.start()
                sent.append(cp)
        for w in range(n_w):
            r = _half(c, parts[w].shape[0] // 2)
            for k, (px, py) in enumerate(chips):
                blk = outs[w].at[2 * px + py, r]
                if landed is None:
                    _remote(blk, blk, send_sems, recv_sems, 6 * w + k, (px, py, c)).wait_recv()
                cp = _remote(blk, blk, send_sems, recv_sems, 6 * w + 3 + k, sibling)
                cp.start()
                sent.append(cp)
        for w in range(n_w):
            r = _half(1 - c, parts[w].shape[0] // 2)
            for k, (px, py) in enumerate(chips):
                blk = outs[w].at[2 * px + py, r]
                _remote(blk, blk, send_sems, recv_sems, 6 * w + 3 + k, sibling).wait_recv()
        for cp in sent:
            cp.wait_send()
        for cp in pending:
            cp.wait()

    return pl.pallas_call(
        body, name=name, in_specs=[pl.BlockSpec(memory_space=pltpu.VMEM)] * n_w + [ANY] * (n_in - n_w),
        out_specs=[ANY] * n_w,
        out_shape=[jax.ShapeDtypeStruct((N_CHIPS, *p.shape), p.dtype) for p in parts],
        input_output_aliases={} if landed is None else {n_w + w: w for w in range(n_w)},
        scratch_shapes=[pltpu.SemaphoreType.DMA((6 * n_w,)), pltpu.SemaphoreType.DMA((6 * n_w,)),
                        pltpu.SemaphoreType.DMA((n_w,))],
        compiler_params=pltpu.CompilerParams(vmem_limit_bytes=VMEM_LIMIT),
    )(*parts, *(landed or []))


def _swap_halves(gs, name):
    n_w = len(gs)

    def body(*refs):
        g_refs, recv_refs = refs[:n_w], refs[n_w:2 * n_w]
        send_sems, recv_sems = refs[2 * n_w:]
        x, y, c = _place()
        sent = []
        for w in range(n_w):
            for jj in range(N_CHIPS):
                cp = _remote(g_refs[w].at[jj, 1 - c], recv_refs[w].at[jj], send_sems, recv_sems, N_CHIPS * w + jj,
                             (x, y, 1 - c))
                cp.start()
                sent.append(cp)
        for cp in sent:
            cp.wait()

    return pl.pallas_call(
        body, name=name, in_specs=[ANY] * n_w, out_specs=[ANY] * n_w,
        out_shape=[jax.ShapeDtypeStruct((N_CHIPS, *g.shape[2:]), g.dtype) for g in gs],
        scratch_shapes=[pltpu.SemaphoreType.DMA((N_CHIPS * n_w,)), pltpu.SemaphoreType.DMA((N_CHIPS * n_w,))],
    )(*gs)


def _pair_sum(g, recv, core, name):
    _, H, C = recv.shape
    tile = _pick(H, 256)

    def body(c_ref, own_ref, recv_ref, out_ref):
        out_ref[...] = (own_ref[...].astype(F32) + recv_ref[...].astype(F32)).astype(BF16)

    blk = pl.BlockSpec((None, tile, C), lambda jj, i, c: (jj, i, 0))
    return pl.pallas_call(
        body, name=name,
        grid_spec=pltpu.PrefetchScalarGridSpec(
            num_scalar_prefetch=1, grid=(N_CHIPS, H // tile),
            in_specs=[pl.BlockSpec((None, None, tile, C), lambda jj, i, c: (jj, c[0], i, 0)), blk],
            out_specs=blk),
        out_shape=jax.ShapeDtypeStruct((N_CHIPS, H, C), BF16),
        compiler_params=_cparams(("arbitrary", "arbitrary")),
    )(_index(core), g, recv)


def _chip_sum(g, recv, got, chip, core, name):
    _, H, C = recv.shape
    tile = _pick(H, 256)

    def body(s_ref, own_ref, recv_ref, g0_ref, g1_ref, g2_ref, out_ref):
        pair = own_ref[...].astype(F32) + recv_ref[...].astype(F32)
        out_ref[...] = ((pair + g0_ref[...].astype(F32)) + g1_ref[...].astype(F32)) + g2_ref[...].astype(F32)

    def got_spec(k):
        return pl.BlockSpec((None, tile, C), lambda i, s, k=k: (k, i, 0))

    return pl.pallas_call(
        body, name=name,
        grid_spec=pltpu.PrefetchScalarGridSpec(
            num_scalar_prefetch=1, grid=(H // tile,),
            in_specs=[pl.BlockSpec((None, None, tile, C), lambda i, s: (s[0], s[1], i, 0)),
                      pl.BlockSpec((None, tile, C), lambda i, s: (s[0], i, 0)), got_spec(0), got_spec(1), got_spec(2)],
            out_specs=pl.BlockSpec((None, tile, C), lambda i, s: (s[1], i, 0))),
        out_shape=jax.ShapeDtypeStruct((2, H, C), F32),
        compiler_params=_cparams(("arbitrary",)),
    )(_index(chip, core), g, recv, got, got, got)


def _share_halves(reds):
    n_w = len(reds)

    def body(*refs):
        out_refs = refs[n_w:2 * n_w]
        send_sems, recv_sems = refs[2 * n_w:]
        x, y, c = _place()
        sent = []
        for w in range(n_w):
            blk = out_refs[w].at[c]
            cp = _remote(blk, blk, send_sems, recv_sems, w, (x, y, 1 - c))
            cp.start()
            sent.append(cp)
        for cp in sent:
            cp.wait()

    return pl.pallas_call(
        body, name="grad_share_halves", in_specs=[ANY] * n_w, out_specs=[ANY] * n_w,
        out_shape=[jax.ShapeDtypeStruct(r.shape, r.dtype) for r in reds],
        input_output_aliases={w: w for w in range(n_w)},
        scratch_shapes=[pltpu.SemaphoreType.DMA((n_w,)), pltpu.SemaphoreType.DMA((n_w,))],
    )(*reds)


def _allsum_small(v, name):
    R, W = v.shape
    n_dev = 8
    vm = pl.BlockSpec(memory_space=pltpu.VMEM)

    def body(v_ref, out_ref, buf, send_sems, recv_sems):
        x, y, c = _place()
        me = 4 * x + 2 * y + c
        buf[me] = v_ref[...]
        sent = []
        for k in range(1, n_dev):
            peer = ((1 - x) if k & 4 else x, (1 - y) if k & 2 else y, (1 - c) if k & 1 else c)
            cp = _remote(v_ref, buf.at[me], send_sems, recv_sems, k - 1, peer)
            cp.start()
            sent.append(cp)
        for cp in sent:
            cp.wait_recv()
        for cp in sent:
            cp.wait_send()
        acc = buf[0]
        for q in range(1, n_dev):
            acc = acc + buf[q]
        out_ref[...] = acc

    return pl.pallas_call(
        body, name=name, in_specs=[vm], out_specs=vm, out_shape=jax.ShapeDtypeStruct((R, W), v.dtype),
        scratch_shapes=[pltpu.VMEM((n_dev, R, W), v.dtype), pltpu.SemaphoreType.DMA((n_dev - 1,)),
                        pltpu.SemaphoreType.DMA((n_dev - 1,))],
    )(v)


HBM = pl.BlockSpec(memory_space=pltpu.HBM)
SEM = pl.BlockSpec(memory_space=pltpu.SEMAPHORE)
_DATAFLOW = pltpu.SideEffectType.DATAFLOW_SIDE_EFFECTING


def _split_start(name, srcs, land_shapes, n_copies, copies, after=()):
    ns, nl = len(srcs), len(land_shapes)
    lands = [lax.empty(s.shape, s.dtype) for s in land_shapes]

    def body(*refs):
        outs = refs[ns + nl + len(after):]
        for cp in copies(refs[:ns], refs[ns:ns + nl], outs[0], outs[1]):
            cp.start()
        outs[-1][...] = jnp.zeros_like(outs[-1])

    sems = pltpu.SemaphoreType.DMA((n_copies,))
    res = pl.pallas_call(
        body, name=name, in_specs=[HBM] * (ns + nl) + [ANY] * len(after),
        out_specs=(SEM, SEM, *[HBM] * (ns + nl), pl.BlockSpec(memory_space=pltpu.VMEM)),
        out_shape=(sems, sems, *[pltpu.HBM(a.shape, a.dtype) for a in srcs],
                   *[pltpu.HBM(s.shape, s.dtype) for s in land_shapes], jax.ShapeDtypeStruct((8, 128), F32)),
        input_output_aliases={i: 2 + i for i in range(ns + nl)},
        compiler_params=pltpu.CompilerParams(has_side_effects=_DATAFLOW),
    )(*[pltpu.with_memory_space_constraint(a, pltpu.HBM) for a in [*srcs, *lands]], *after)
    return res[0], res[1], list(res[2:2 + ns]), list(res[2 + ns:2 + ns + nl]), res[-1]


def _split_wait(name, send_sems, recv_sems, srcs, lands, copies, after=()):
    ns, nl = len(srcs), len(lands)

    def body(*refs):
        for cp in copies(refs[:ns], refs[ns:ns + nl], refs[ns + nl], refs[ns + nl + 1]):
            cp.wait_send()
            cp.wait_recv()

    res = pl.pallas_call(
        body, name=name, in_specs=[HBM] * (ns + nl) + [SEM, SEM] + [ANY] * len(after), out_specs=[HBM] * (ns + nl),
        out_shape=[pltpu.HBM(a.shape, a.dtype) for a in [*srcs, *lands]],
        input_output_aliases={i: i for i in range(ns + nl)},
        compiler_params=pltpu.CompilerParams(has_side_effects=_DATAFLOW),
    )(*srcs, *lands, send_sems, recv_sems, *after)
    return list(res[ns:])


def _gather_copies(rows):
    def copies(src_refs, land_refs, send_sems, recv_sems):
        x, y, c = _place()
        j = 2 * x + y
        out = []
        for w in range(len(src_refs)):
            r = _half(c, rows[w] // 2)
            for k in range(3):
                px, py = _other_chip(x, y, k)
                out.append(_remote(src_refs[w].at[r], land_refs[w].at[j, r], send_sems, recv_sems, 3 * w + k, (px, py, c)))
        return out
    return copies


def _scatter_copies(src_refs, land_refs, send_sems, recv_sems):
    x, y, c = _place()
    j = 2 * x + y
    out = []
    for w in range(len(src_refs)):
        for k in range(3):
            px, py = _other_chip(x, y, k)
            pj = 2 * px + py
            out.append(_remote(src_refs[w].at[pj], land_refs[w].at[(j - pj + 4) % 4 - 1], send_sems, recv_sems, 3 * w + k,
                               (px, py, c)))
    return out


def _reduce_begin(grads, core, tag):
    names = list(grads)
    gs = [grads[k].reshape(N_CHIPS, 2, -1, grads[k].shape[-1]) for k in names]
    recvs = _swap_halves(gs, f"grad_swap_halves_{tag}")
    sums = [_pair_sum(g, r, core, f"pair_sum_{k}") for k, g, r in zip(names, gs, recvs)]
    return names, gs, recvs, sums


def _reduce_end(begun, gots, chip, core):
    names, gs, recvs, _ = begun
    return {k: _chip_sum(g, r, t, chip, core, f"chip_sum_{k}") for k, g, r, t in zip(names, gs, recvs, gots)}


def _got_shapes(sums):
    return [jax.ShapeDtypeStruct((3, *a.shape[1:]), a.dtype) for a in sums]


def _adamw(w, g, m, v, name, layers=1, layer=0, into=None):
    shape = w.shape
    cols = shape[-1]
    w3, m3, v3 = (t.reshape(layers, -1, cols) for t in (w, m, v))
    rows = w3.shape[1]
    tile = _pick(rows, 256) if rows % 8 == 0 else rows
    n_in = 4 + (0 if into is None else 4)
    stack_g = layers > 1

    def body(*refs):
        wv, gv, mv, vv = (r[...] for r in refs[:4])
        d_ref, m_ref, v_ref = refs[len(refs) - 3:]
        m2 = ADAM_B1 * mv + (1.0 - ADAM_B1) * gv
        v2 = ADAM_B2 * vv + (1.0 - ADAM_B2) * jnp.square(gv)
        m_hat = m2 / (1.0 - ADAM_B1 ** ADAM_STEP)
        v_hat = v2 / (1.0 - ADAM_B2 ** ADAM_STEP)
        if stack_g:
            refs[n_in][...] = gv
        d_ref[...] = -ADAM_LR * (m_hat / (jnp.sqrt(v_hat) + ADAM_EPS) + ADAM_WD * wv)
        m_ref[...] = m2
        v_ref[...] = v2

    n_out = 4 if stack_g else 3
    lay = pl.BlockSpec((None, tile, cols), lambda i: (layer, i, 0))
    out = jax.ShapeDtypeStruct((layers, rows, cols), F32)
    res = pl.pallas_call(
        body, name=name, grid=(rows // tile,),
        in_specs=[lay, pl.BlockSpec((tile, cols), lambda i: (i, 0)), lay, lay] + [ANY] * (n_in - 4),
        out_specs=[lay] * n_out, out_shape=[out] * n_out,
        input_output_aliases={} if into is None else {4 + k: k for k in range(4)},
        compiler_params=_cparams(("arbitrary",)),
    )(w3, g.reshape(rows, cols), m3, v3, *([] if into is None else [t.reshape(layers, rows, cols) for t in into]))
    res = tuple(t.reshape(shape) for t in res)
    return res if stack_g else (g.reshape(shape), *res)


ROW_F32, ROW_BF16 = (D_MODEL, F32), (D_MODEL, BF16)


def _res_norm(acc, h, gain):
    hh = h + acc
    return hh, _rms(hh, gain)


def _dx_norm_bwd(d, w, h, dres, gain, name, **kw):
    def epilogue(acc, hv, dr, g):
        dx, dg = _rms_bwd(hv, acc, g)
        return dr + dx, dr + dx, _colsum(dg)
    return _mm_rows(d, w, tb=True, extras=[h, dres], fulls=[gain], outs=[ROW_F32, ROW_BF16], accs=[((1, D_MODEL), F32)],
                    epilogue=epilogue, name=name, **kw)


def _tail_fwd(h1, hn2, p16, W, i, tag, next_gain=None, target=None):
    a = _mm(hn2, W["mlp_w1"][i], bblk=True, outs=[BF16], name=f"{tag}_mlp_w1",
            epilogue=lambda acc: (jnp.square(jnp.maximum(acc, 0.0)),))
    h2, hn3 = _mm_rows(a, W["mlp_w2"][i], extras=[h1], fulls=[W["ple_norm"][i:i + 1]], outs=[ROW_F32, ROW_BF16],
                       epilogue=_res_norm, name=f"{tag}_mlp_w2")
    def embed(acc, pv, h, wp):
        gate = _sigmoid(acc)
        ppv = jnp.concatenate([_dot(pv, wp[s]) for s in range(N_CHIPS)], axis=-1)
        return gate, ppv, h + gate * ppv

    if target is None:
        def gated(acc, pv, h, wp, gain):
            gate, ppv, hh = embed(acc, pv, h, wp)
            return hh, ppv, gate, _rms(hh, gain)
        h3, pp, gate, hn = _mm_rows(hn3, W["ple_gate_w"][i], extras=[p16[i], h2], fulls=[W["ple_proj_w"][i], next_gain],
                                    outs=[ROW_F32, ROW_BF16, ROW_BF16, ROW_BF16], epilogue=gated, name=f"{tag}_ple")
        return h3, hn, (h1, hn2, a, h2, hn3, gate, pp)

    def gated_loss(acc, pv, h, t, wp):
        gate, ppv, hh = embed(acc, pv, h, wp)
        e = hh - t
        return ppv, gate, e * (1.0 / D_MODEL), jnp.full((1, 128), 0.5 / D_MODEL, F32) * jnp.sum(e * e)
    pp, gate, dy, loss = _mm_rows(hn3, W["ple_gate_w"][i], extras=[p16[i], h2, target], fulls=[W["ple_proj_w"][i]],
                                  outs=[ROW_BF16, ROW_BF16, ROW_F32], accs=[((1, 128), F32)], epilogue=gated_loss,
                                  name=f"{tag}_ple")
    return dy, loss, (h1, hn2, a, h2, hn3, gate, pp)


def _tail_bwd(dh3, saved, p16, W, i, tag, after=()):
    h1, hn2, a, h2, hn3, gate, pp = saved

    def embed_bwd(d, g, ppv, hv, wg, gain):
        g, ppv = g.astype(F32), ppv.astype(F32)
        dppv, dglv = (d * g).astype(BF16), (d * ppv * g * (1.0 - g)).astype(BF16)
        dx, dg = _rms_bwd(hv, _dot_nt(dglv, wg), gain)
        return dppv, dglv, d + dx, d + dx, _colsum(dg)

    def dw(kind, name):
        return (kind, 1, 0, None)

    dpp, dgl, dh2, dh2_16, d_ple_norm = _rows(
        embed_bwd, [dh3, gate, pp, h2], [W["ple_gate_w"][i], W["ple_norm"][i:i + 1]],
        [ROW_BF16, ROW_BF16, ROW_F32, ROW_BF16], [((1, D_MODEL), F32)], name=f"{tag}_ple_bwd", after=after)
    d_proj = _mm(p16[i], dpp, ta=True, outs=[BF16], dw=dw("cols", "ple_proj_w"), name=f"{tag}_d_ple_proj")
    d_gate = _mm(hn3, dgl, ta=True, outs=[BF16], dw=dw("rows", "ple_gate_w"), name=f"{tag}_d_ple_gate")
    d_w2 = _mm(a, dh2_16, ta=True, outs=[BF16], dw=dw("rows", "mlp_w2"), name=f"{tag}_d_mlp_w2")
    dz = _mm(dh2_16, W["mlp_w2"][i], tb=True, extras=[a], outs=[BF16], name=f"{tag}_mlp_w2_dx",
             epilogue=lambda acc, av: (acc * (2.0 * jnp.sqrt(av.astype(F32))),))
    d_w1 = _mm(hn2, dz, ta=True, outs=[BF16], dw=dw("cols", "mlp_w1"), name=f"{tag}_d_mlp_w1")
    dh1, dh1_16, d_mlp_norm = _dx_norm_bwd(dz, W["mlp_w1"][i], h1, dh2, W["mlp_norm"][i:i + 1], f"{tag}_mlp_w1_dx",
                                           bblk=True)
    big = {f"mlp_w1_{i}": d_w1, f"mlp_w2_{i}": d_w2, f"ple_gate_w_{i}": d_gate, f"ple_proj_w_{i}": d_proj}
    return dh1, dh1_16, big, dict(mlp_norm=d_mlp_norm, ple_norm=d_ple_norm)


def _ret_layer_fwd(h0, W, tabs, after=()):
    hn = _rows(lambda x, g: (_rms(x, g),), [h0], [W["mix_norm"][0:1]], [(D_MODEL, BF16)], name="ret_mix_norm",
               after=after)[0]
    proj = _mm(hn, W["ret_w_in"], bblk=True, outs=[BF16], name="ret_w_in")
    out, states = _ret_fwd(proj, tabs, "ret_scan")
    y = _ret_gate(out, proj, W["ret_gn"], "ret_gate")
    h1, hn2 = _mm_rows(y, W["ret_w_out"], extras=[h0], fulls=[W["mlp_norm"][0:1]], outs=[ROW_F32, ROW_BF16],
                       epilogue=_res_norm, name="ret_w_out")
    return h1, hn2, (h0, hn, proj, out, states, y)


def _d_ret_w_out(dh1_16, saved):
    return _mm(saved[5], dh1_16, ta=True, outs=[BF16], dw=("rows", 1, 0, None), name="d_ret_w_out")


def _ret_layer_bwd(dh1, dh1_16, saved, W, tabs, after=(), on_grads=None, d_w_out=None):
    h0, hn, proj, out, states, y = saved
    d_w_out = _d_ret_w_out(dh1_16, saved) if d_w_out is None else d_w_out
    dy = _mm(dh1_16, W["ret_w_out"], tb=True, name="ret_w_out_dx", after=after)
    dout, dproj, d_gn = _ret_gate_bwd(out, proj, W["ret_gn"], dy, "ret_gate_bwd")
    dproj = _ret_bwd(proj, states, dout, dproj, tabs, "ret_scan_bwd")
    d_w_in = _mm(hn, dproj, ta=True, outs=[BF16], dw=("cols", 1, 0, None), name="d_ret_w_in")
    big = dict(ret_w_in=d_w_in, ret_w_out=d_w_out)
    later = () if on_grads is None else on_grads(big)
    dh0, _, d_mix = _dx_norm_bwd(dproj, W["ret_w_in"], h0, dh1, W["mix_norm"][0:1], "ret_w_in_dx", bblk=True, tm=256,
                                 after=later)
    return dh0, big, dict(mix_norm=d_mix, ret_gn=d_gn)


def _mla_layer_fwd(h0, hn, W, tabs):
    proj, cqn, ckvn, q, kv, qf, kf, vf = _mla_front(hn, W, tabs, "mla_front")
    o, lse = _flash_fwd(qf, kf, vf, "mla_flash")
    h1, hn2 = _mm_rows(o, W["mla_w_out"], extras=[h0], fulls=[W["mlp_norm"][1:2]], outs=[ROW_F32, ROW_BF16],
                       epilogue=_res_norm, name="mla_w_out")
    return h1, hn2, (h0, hn, proj, cqn, ckvn, q, kv, qf, kf, vf, o, lse)


def _mla_layer_bwd(dh1, dh1_16, saved, W, tabs):
    h0, hn, proj, cqn, ckvn, q, kv, qf, kf, vf, o, lse = saved
    d_w_out = _mm(o, dh1_16, ta=True, outs=[BF16], dw=("rows", 1, 0, None), name="d_mla_w_out")
    def with_delta(acc, ov):
        parts = []
        for h in range(MLA_HEADS):
            sl = slice(h * MLA_VD, (h + 1) * MLA_VD)
            d = jnp.sum(acc[:, sl] * ov[:, sl], axis=-1, keepdims=True)
            parts.append(jnp.broadcast_to(d, (d.shape[0], MLA_VD)))
        return jnp.concatenate(parts, axis=-1), acc

    delta, do16 = _mm_rows(dh1_16, W["mla_w_out"], tb=True, extras=[o], outs=[ROW_F32, ROW_BF16], epilogue=with_delta,
                           name="mla_w_out_dx")
    dqf, dkf, dvf = _flash_bwd(qf, kf, vf, do16, lse, delta, "mla_flash_bwd")
    dq, dkv, dproj, dh0, dh0_16, d_gq, d_gk, d_gqa, d_gkva, d_mix = _mla_back(q, kv, proj, h0, dh1, dqf, dkf, dvf, W, tabs,
                                                                              "mla_back")
    d_w_uq = _mm(cqn, dq, ta=True, outs=[BF16], dw=("cols", 1, 0, None), name="d_mla_w_uq")
    d_w_ukv = _mm(ckvn, dkv, ta=True, outs=[BF16], dw=("cols", 1, 0, None), name="d_mla_w_ukv")
    d_w_in = _mm(hn, dproj, ta=True, outs=[BF16], dw=("rows", 1, 0, None), name="d_mla_w_in")
    return (dh0, dh0_16, dict(mla_w_in=d_w_in, mla_w_uq=d_w_uq, mla_w_ukv=d_w_ukv, mla_w_out=d_w_out),
            dict(mix_norm=d_mix, mla_q_a_norm=d_gqa, mla_kv_a_norm=d_gkva, mla_q_norm=d_gq, mla_k_norm=d_gk))


def _local_step(x, p16, target, W):
    T = x.shape[0]
    ret_tabs, mla_tabs = _ret_tables(T), _mla_tables(T)
    h1, hn, s_ret = _ret_layer_fwd(x, W, ret_tabs)
    h3, hn, s_tail0 = _tail_fwd(h1, hn, p16, W, 0, "l0", next_gain=W["mix_norm"][1:2])
    h4, hn, s_mla = _mla_layer_fwd(h3, hn, W, mla_tabs)
    dy, loss, s_tail1 = _tail_fwd(h4, hn, p16, W, 1, "l1", target=target)
    dh4, dh4_16, g_t1, n_t1 = _tail_bwd(dy, s_tail1, p16, W, 1, "l1")
    dh3, _, g_mla, n_mla = _mla_layer_bwd(dh4, dh4_16, s_mla, W, mla_tabs)
    dh1, dh1_16, g_t0, n_t0 = _tail_bwd(dh3, s_tail0, p16, W, 0, "l0")
    dx, g_ret, n_ret = _ret_layer_bwd(dh1, dh1_16, s_ret, W, ret_tabs)
    return loss, dx, {**g_ret, **g_t0, **g_mla, **g_t1}, _small_grads(n_ret, n_t0, n_mla, n_t1)


def _small_grads(n_ret, n_t0, n_mla, n_t1):
    return dict(
        mix_norm=jnp.concatenate([n_ret["mix_norm"], n_mla["mix_norm"]], axis=0),
        mlp_norm=jnp.concatenate([n_t0["mlp_norm"], n_t1["mlp_norm"]], axis=0),
        ple_norm=jnp.concatenate([n_t0["ple_norm"], n_t1["ple_norm"]], axis=0),
        ret_gn=n_ret["ret_gn"], mla_q_a_norm=n_mla["mla_q_a_norm"], mla_kv_a_norm=n_mla["mla_kv_a_norm"],
        mla_q_norm=n_mla["mla_q_norm"], mla_k_norm=n_mla["mla_k_norm"])


_ORDER = ("mix_norm", "ret_w_in", "ret_gn", "ret_w_out", "mla_w_in", "mla_q_a_norm", "mla_kv_a_norm", "mla_w_uq",
          "mla_w_ukv", "mla_q_norm", "mla_k_norm", "mla_w_out", "mlp_norm", "mlp_w1", "mlp_w2", "ple_norm",
          "ple_gate_w", "ple_proj_w")
_TWO_LAYER = ("mlp_w1", "mlp_w2", "ple_gate_w", "ple_proj_w")
HEADS_PER_CHIP = MLA_HEADS // N_CHIPS
GAIN_ROWS = 32


def _travel_parts(w):
    uq = jnp.pad(w["mla_w_uq"][0].reshape(MLA_Q_RANK, HEADS_PER_CHIP, MLA_QKD), ((0, 0), (0, 0), (0, MLA_HP - MLA_QKD)))
    parts = {"ret_w_in": w["ret_w_in"][0], "ret_w_out": w["ret_w_out"][0]}
    for k in _TWO_LAYER:
        parts[k + "_0"] = w[k][0]
    parts["mla_w_in"] = jnp.pad(w["mla_w_in"][0], ((0, 0), (0, MLA_IN_PAD - MLA_IN)))
    parts["mla_w_uq"] = uq.reshape(MLA_Q_RANK, HEADS_PER_CHIP * MLA_HP)
    parts["mla_w_ukv"] = w["mla_w_ukv"][0]
    parts["mla_w_out"] = w["mla_w_out"][0]
    for k in _TWO_LAYER:
        parts[k + "_1"] = w[k][1]
    gains = jnp.concatenate([_pad_row(w["ret_gn"]), _pad_row(w["mla_q_a_norm"]), _pad_row(w["mla_kv_a_norm"]),
                             jnp.zeros((GAIN_ROWS - 3, PACK_W), F32)], axis=0)
    return {"gains": gains, **{k: v.astype(BF16) for k, v in parts.items()}}


def _full_weights(full):
    rows = lambda a: a.reshape(-1, a.shape[-1])
    W = {k: full[k] for k in ("ret_w_in", "mla_w_uq", "mla_w_ukv") if k in full}
    for k in ("ret_w_out", "mla_w_in", "mla_w_out"):
        if k in full:
            W[k] = rows(full[k])
    for k, by_rows in (("mlp_w1", False), ("ple_proj_w", False), ("mlp_w2", True), ("ple_gate_w", True)):
        layers = [full.get(f"{k}_{i}") for i in range(2)]
        W[k] = [rows(t) if (by_rows and t is not None) else t for t in layers]
    return W


def _shard_grad(name, red, shape):
    if name == "mla_w_in":
        red = red.reshape(-1, MLA_IN_PAD)[:, :MLA_IN]
    elif name == "mla_w_uq":
        red = red.reshape(MLA_Q_RANK, HEADS_PER_CHIP, MLA_HP)[:, :, :MLA_QKD]
    return red.reshape(shape)


def _pad_row(v):
    v = v.reshape(1, -1)
    return jnp.pad(v, ((0, 0), (0, PACK_W - v.shape[1])))


def kernel(x, p, mix_norm, ret_w_in, ret_gn, ret_w_out, mla_w_in, mla_q_a_norm, mla_kv_a_norm, mla_w_uq, mla_w_ukv, mla_q_norm, mla_k_norm, mla_w_out, mlp_norm, mlp_w1, mlp_w2, ple_norm, ple_gate_w, ple_proj_w, loss_target, m_mix_norm, m_ret_w_in, m_ret_gn, m_ret_w_out, m_mla_w_in, m_mla_q_a_norm, m_mla_kv_a_norm, m_mla_w_uq, m_mla_w_ukv, m_mla_q_norm, m_mla_k_norm, m_mla_w_out, m_mlp_norm, m_mlp_w1, m_mlp_w2, m_ple_norm, m_ple_gate_w, m_ple_proj_w, v_mix_norm, v_ret_w_in, v_ret_gn, v_ret_w_out, v_mla_w_in, v_mla_q_a_norm, v_mla_kv_a_norm, v_mla_w_uq, v_mla_w_ukv, v_mla_q_norm, v_mla_k_norm, v_mla_w_out, v_mlp_norm, v_mlp_w1, v_mlp_w2, v_ple_norm, v_ple_gate_w, v_ple_proj_w):
    w = dict(mix_norm=mix_norm, ret_w_in=ret_w_in, ret_gn=ret_gn, ret_w_out=ret_w_out, mla_w_in=mla_w_in,
             mla_q_a_norm=mla_q_a_norm, mla_kv_a_norm=mla_kv_a_norm, mla_w_uq=mla_w_uq, mla_w_ukv=mla_w_ukv,
             mla_q_norm=mla_q_norm, mla_k_norm=mla_k_norm, mla_w_out=mla_w_out, mlp_norm=mlp_norm, mlp_w1=mlp_w1,
             mlp_w2=mlp_w2, ple_norm=ple_norm, ple_gate_w=ple_gate_w, ple_proj_w=ple_proj_w)
    m = dict(mix_norm=m_mix_norm, ret_w_in=m_ret_w_in, ret_gn=m_ret_gn, ret_w_out=m_ret_w_out, mla_w_in=m_mla_w_in,
             mla_q_a_norm=m_mla_q_a_norm, mla_kv_a_norm=m_mla_kv_a_norm, mla_w_uq=m_mla_w_uq, mla_w_ukv=m_mla_w_ukv,
             mla_q_norm=m_mla_q_norm, mla_k_norm=m_mla_k_norm, mla_w_out=m_mla_w_out, mlp_norm=m_mlp_norm,
             mlp_w1=m_mlp_w1, mlp_w2=m_mlp_w2, ple_norm=m_ple_norm, ple_gate_w=m_ple_gate_w, ple_proj_w=m_ple_proj_w)
    v = dict(mix_norm=v_mix_norm, ret_w_in=v_ret_w_in, ret_gn=v_ret_gn, ret_w_out=v_ret_w_out, mla_w_in=v_mla_w_in,
             mla_q_a_norm=v_mla_q_a_norm, mla_kv_a_norm=v_mla_kv_a_norm, mla_w_uq=v_mla_w_uq, mla_w_ukv=v_mla_w_ukv,
             mla_q_norm=v_mla_q_norm, mla_k_norm=v_mla_k_norm, mla_w_out=v_mla_w_out, mlp_norm=v_mlp_norm,
             mlp_w1=v_mlp_w1, mlp_w2=v_mlp_w2, ple_norm=v_ple_norm, ple_gate_w=v_ple_gate_w, ple_proj_w=v_ple_proj_w)
    xi, yi, ci = _place()
    chip = 2 * xi + yi
    n = N_CHIPS

    parts = _travel_parts(w)
    first = ("gains", "ret_w_in", "ret_w_out")
    mid = [k + "_0" for k in _TWO_LAYER]
    last = [k for k in parts if k not in first and k not in mid]
    full = dict(zip(first, _gather_weights([parts[k] for k in first], "gather_first")))

    def gather_behind(names, tag, after):
        copies = _gather_copies([parts[k].shape[0] for k in names])
        started = _split_start(f"gather_{tag}_start", [parts[k] for k in names],
                               [jax.ShapeDtypeStruct((n, *parts[k].shape), BF16) for k in names], 3 * len(names),
                               copies, after=after)

        def arrive(after):
            landed = _split_wait(f"gather_{tag}_wait", *started[:4], copies, after=after)
            full.update(zip(names, _gather_weights([parts[k] for k in names], f"gather_{tag}_finish", landed=landed)))
            W.update(_full_weights(full))
        return started[4], arrive

    mid_token, mid_arrive = gather_behind(mid, "mid", [full["ret_w_in"]])
    g_token, last_arrive = gather_behind(last, "last", [mid_token])
    gains = full["gains"]
    W = dict(mix_norm=mix_norm, mlp_norm=mlp_norm, ple_norm=ple_norm,
             mla_q_norm=jnp.pad(mla_q_norm, ((0, 0), (0, MLA_HP - MLA_QKD))),
             mla_k_norm=jnp.pad(mla_k_norm, ((0, 0), (0, MLA_HP - MLA_QKD))),
             ret_w_in=full["ret_w_in"], ret_w_out=full["ret_w_out"].reshape(-1, D_MODEL),
             ret_gn=gains[:, 0, :RET_HEADS * 128].reshape(n, RET_HEADS, 128).transpose(1, 0, 2).reshape(RET_HEADS, RET_DV),
             mla_q_a_norm=gains[:, 1, :MLA_Q_RANK // n].reshape(1, MLA_Q_RANK),
             mla_kv_a_norm=gains[:, 2, :MLA_KV_RANK // n].reshape(1, MLA_KV_RANK))
    x0, p16, target = x[0], p[:, 0].astype(BF16), loss_target[0]
    T = x0.shape[0]
    ret_tabs, mla_tabs = _ret_tables(T), _mla_tables(T)

    h1, hn, s_ret = _ret_layer_fwd(x0, W, ret_tabs, after=[g_token])
    mid_arrive([h1])
    h3, hn, s_tail0 = _tail_fwd(h1, hn, p16, W, 0, "l0", next_gain=W["mix_norm"][1:2])
    last_arrive([h3])
    h4, hn, s_mla = _mla_layer_fwd(h3, hn, W, mla_tabs)
    dy, loss, s_tail1 = _tail_fwd(h4, hn, p16, W, 1, "l1", target=target)

    dh4, dh4_16, g_t1, n_t1 = _tail_bwd(dy, s_tail1, p16, W, 1, "l1")
    dh3, _, g_mla, n_mla = _mla_layer_bwd(dh4, dh4_16, s_mla, W, mla_tabs)
    beg_a = _reduce_begin({**g_mla, **g_t1}, ci, "a")
    a_send, a_recv, a_src, a_land, a_token = _split_start(
        "scatter_a_start", beg_a[3], _got_shapes(beg_a[3]), 3 * len(beg_a[3]), _scatter_copies)
    dh1, dh1_16, g_t0, n_t0 = _tail_bwd(dh3, s_tail0, p16, W, 0, "l0", after=[a_token])
    d_ret_w_out = _d_ret_w_out(dh1_16, s_ret)
    beg_b = _reduce_begin({**g_t0, "ret_w_out": d_ret_w_out}, ci, "b")
    b_send, b_recv, b_src, b_land, b_token = _split_start(
        "scatter_b_start", beg_b[3], _got_shapes(beg_b[3]), 3 * len(beg_b[3]), _scatter_copies)
    stage_c = {}

    def start_c(g_ret):
        beg = _reduce_begin({"ret_w_in": g_ret["ret_w_in"]}, ci, "c")
        stage_c["beg"] = beg
        stage_c["st"] = _split_start("scatter_c_start", beg[3], _got_shapes(beg[3]), 3 * len(beg[3]), _scatter_copies)
        return [stage_c["st"][4]]

    dx, _, n_ret = _ret_layer_bwd(dh1, dh1_16, s_ret, W, ret_tabs, after=[b_token], on_grads=start_c,
                                  d_w_out=d_ret_w_out)
    got_a = _split_wait("scatter_a_wait", a_send, a_recv, a_src, a_land, _scatter_copies, after=[dx])
    got_b = _split_wait("scatter_b_wait", b_send, b_recv, b_src, b_land, _scatter_copies, after=[dx])
    got_c = _split_wait("scatter_c_wait", *stage_c["st"][:4], _scatter_copies, after=[dx])
    red = {**_reduce_end(beg_a, got_a, chip, ci), **_reduce_end(beg_b, got_b, chip, ci),
           **_reduce_end(stage_c["beg"], got_c, chip, ci)}
    red = dict(zip(red, _share_halves(list(red.values()))))
    gs = _small_grads(n_ret, n_t0, n_mla, n_t1)
    small_g = jnp.concatenate([
        gs["mix_norm"], gs["mlp_norm"], gs["ple_norm"], gs["ret_gn"].reshape(2, PACK_W), _pad_row(gs["mla_q_a_norm"]),
        _pad_row(gs["mla_kv_a_norm"]), _pad_row(gs["mla_q_norm"][:, :MLA_QKD]), _pad_row(gs["mla_k_norm"][:, :MLA_QKD]),
        _pad_row(loss[:, :1]), jnp.zeros((3, PACK_W), F32)], axis=0)
    tot = _allsum_small(small_g, "sum_small_grads")
    gn_all = tot[6:8].reshape(RET_HEADS, n, -1)
    g_small = dict(
        mix_norm=tot[0:2], mlp_norm=tot[2:4], ple_norm=tot[4:6],
        ret_gn=lax.dynamic_index_in_dim(gn_all, chip, axis=1, keepdims=False),
        mla_q_a_norm=lax.dynamic_index_in_dim(tot[8, :MLA_Q_RANK].reshape(n, -1), chip, axis=0, keepdims=True),
        mla_kv_a_norm=lax.dynamic_index_in_dim(tot[9, :MLA_KV_RANK].reshape(n, -1), chip, axis=0, keepdims=True),
        mla_q_norm=tot[10:11, :MLA_QKD], mla_k_norm=tot[11:12, :MLA_QKD])
    loss_out = tot[12, 0]

    outs = []
    for k in _ORDER:
        if k in _TWO_LAYER:
            res = None
            for i in (1, 0):
                res = _adamw(w[k], red[f"{k}_{i}"], m[k], v[k], f"adamw_{k}_{i}", layers=2, layer=i, into=res)
        elif k in red:
            res = _adamw(w[k], _shard_grad(k, red[k], w[k].shape), m[k], v[k], f"adamw_{k}")
        else:
            res = _adamw(w[k], g_small[k], m[k], v[k], f"adamw_{k}")
        outs.append(res)
    return (loss_out, dx[None], *[o[0] for o in outs], *[o[1] for o in outs], *[o[2] for o in outs],
            *[o[3] for o in outs])
```

```python
import jax
import jax.numpy as jnp
import numpy as np
from jax import lax
from jax.experimental import pallas as pl
from jax.experimental.pallas import tpu as pltpu

F32 = jnp.float32
BF16 = jnp.bfloat16

EPS = 1e-6
D_MODEL = 1024
CHUNK = 64
ROPE_THETA = 10000.0
RET_HEADS = 4
RET_DK = 256
RET_DV = 512
RET_GROUP = 1
RET_BLOCK = 256
MLA_HEADS = 8
MLA_ROPE = 64
MLA_QKD = 192
MLA_VD = 128
MLA_HP = 256
MLA_Q_RANK = 384
MLA_KV_RANK = 256
MLA_IN = 704
MLA_IN_PAD = 768
N_CHIPS = 4

ADAM_LR = 0.001
ADAM_B1 = 0.9
ADAM_B2 = 0.999
ADAM_EPS = 1e-08
ADAM_WD = 0.01
ADAM_STEP = 10

VMEM_LIMIT = 56 * 1024 * 1024
PACK_W = 1024
NEG = -1e30
LOG2E = 1.4426950408889634
FLASH_T = 512
FLASH_HEADS = 2
MM_SUB_ROWS = 256
SUM_ROWS = 512
ADAM_ROWS = 512


def _cparams(sem=None):
    return pltpu.CompilerParams(dimension_semantics=sem, vmem_limit_bytes=VMEM_LIMIT)


def _pick(dim, pref):
    if dim <= pref:
        return dim
    t = pref
    while dim % t:
        t //= 2
    return t


def _mm(a, b, *, name, ta=False, tb=False, bblk=False, outs=None, extras=(), epilogue=None, dw=None,
        tm=1024, tn=512, after=()):
    if ta:
        K, M = a.shape
    else:
        M, K = a.shape
    if bblk and tb:
        nb, N, Kq = b.shape
        assert nb * Kq == K
    elif bblk:
        nb, Kb, Nq = b.shape
        N = nb * Nq
        assert Kb == K
    else:
        N = b.shape[0] if tb else b.shape[1]
    tn = _pick(Nq if (bblk and not tb) else N, tn)
    if dw is not None and dw[0] == "cols":
        tn = _pick(N // N_CHIPS, tn)
    tm = _pick(M // N_CHIPS if (dw is not None and dw[0] == "rows") else M, tm)
    grid = (M // tm, N // tn)

    a_spec = pl.BlockSpec((K, tm), lambda i, j: (0, i)) if ta else pl.BlockSpec((tm, K), lambda i, j: (i, 0))
    if bblk and tb:
        b_spec = pl.BlockSpec((nb, tn, Kq), lambda i, j: (0, j, 0))
    elif bblk:
        npb = Nq // tn
        b_spec = pl.BlockSpec((None, K, tn), lambda i, j: (j // npb, 0, j % npb))
    elif tb:
        b_spec = pl.BlockSpec((tn, K), lambda i, j: (j, 0))
    else:
        b_spec = pl.BlockSpec((K, tn), lambda i, j: (0, j))
    in_specs = [a_spec, b_spec] + [pl.BlockSpec((tm, tn), lambda i, j: (i, j)) for _ in extras]
    args = [a, b, *extras]
    aliases = {}
    if outs is None:
        outs = [F32]
    if dw is None:
        o_specs = [pl.BlockSpec((tm, tn), lambda i, j: (i, j)) for _ in outs]
        o_shapes = [jax.ShapeDtypeStruct((M, N), dt) for dt in outs]
    else:
        kind, layers, layer, into = dw
        if kind == "cols":
            per = (N // N_CHIPS) // tn
            o_specs = [pl.BlockSpec((None, None, tm, tn), lambda i, j: (j // per, layer, i, j % per))]
            o_shapes = [jax.ShapeDtypeStruct((N_CHIPS, layers, M, N // N_CHIPS), outs[0])]
        else:
            per = (M // N_CHIPS) // tm
            o_specs = [pl.BlockSpec((None, None, tm, tn), lambda i, j: (i // per, layer, i % per, j))]
            o_shapes = [jax.ShapeDtypeStruct((N_CHIPS, layers, M // N_CHIPS, N), outs[0])]
        if into is not None:
            aliases = {len(args): 0}
            in_specs.append(pl.BlockSpec(memory_space=pl.ANY))
            args.append(into)
    for t in after:
        in_specs.append(pl.BlockSpec(memory_space=pl.ANY))
        args.append(t)
    n_e, n_o = len(extras), len(outs)

    sub = _pick(tm, MM_SUB_ROWS)

    def body(a_ref, b_ref, *rest):
        e_refs, o_refs = rest[:n_e], rest[len(rest) - n_o:]
        for r0 in range(0, tm, sub):
            rows = slice(r0, r0 + sub)
            av = (a_ref[:, rows] if ta else a_ref[rows, :]).astype(BF16)
            if bblk and tb:
                acc = _dot_nt(av[:, :Kq], b_ref[0].astype(BF16))
                for s in range(1, nb):
                    acc = acc + _dot_nt(av[:, s * Kq:(s + 1) * Kq], b_ref[s].astype(BF16))
            elif ta:
                acc = _dot_tn(av, b_ref[...].astype(BF16))
            elif tb:
                acc = _dot_nt(av, b_ref[...].astype(BF16))
            else:
                acc = _dot(av, b_ref[...].astype(BF16))
            vals = (acc,) if epilogue is None else epilogue(acc, *[e[rows, :] for e in e_refs])
            for o, v in zip(o_refs, vals):
                o[rows, :] = v.astype(o.dtype)

    res = pl.pallas_call(
        body, name=name, grid=grid, in_specs=in_specs, out_specs=o_specs, out_shape=o_shapes,
        input_output_aliases=aliases, compiler_params=_cparams(("parallel", "arbitrary")),
    )(*args)
    return res[0] if n_o == 1 else res


def _mm_rows(a, b, *, name, epilogue, outs, tb=False, bblk=False, extras=(), fulls=(), accs=(), tm=512, after=()):
    M, K = a.shape
    tm = _pick(M, tm)
    sub = _pick(tm, MM_SUB_ROWS)
    nb = b.shape[0] if bblk else 1
    n_e, n_f, n_o, n_a = len(extras), len(fulls), len(outs), len(accs)
    n_in = 2 + n_e + n_f + len(after)

    def whole(t):
        return pl.BlockSpec(t.shape, lambda i, nd=t.ndim: (0,) * nd)

    in_specs = [pl.BlockSpec((tm, K), lambda i: (i, 0)), whole(b)]
    in_specs += [pl.BlockSpec((tm, e.shape[1]), lambda i: (i, 0)) for e in extras] + [whole(f) for f in fulls]
    in_specs += [pl.BlockSpec(memory_space=pl.ANY) for _ in after]
    out_specs = [pl.BlockSpec((tm, w), lambda i: (i, 0)) for w, _ in outs] + [pl.BlockSpec(s, lambda i: (0, 0)) for s, _ in accs]
    out_shape = [jax.ShapeDtypeStruct((M, w), dt) for w, dt in outs] + [jax.ShapeDtypeStruct(s, dt) for s, dt in accs]

    def body(a_ref, b_ref, *rest):
        e_refs, f_refs = rest[:n_e], rest[n_e:n_e + n_f]
        o_refs, acc_refs = rest[n_in - 2:n_in - 2 + n_o], rest[n_in - 2 + n_o:]
        fv = [f[...] for f in f_refs]
        totals = None
        for r0 in range(0, tm, sub):
            rows = slice(r0, r0 + sub)
            av = a_ref[rows, :].astype(BF16)
            if bblk and tb:
                kq = K // nb
                acc = _dot_nt(av[:, :kq], b_ref[0])
                for s in range(1, nb):
                    acc = acc + _dot_nt(av[:, s * kq:(s + 1) * kq], b_ref[s])
            elif bblk:
                acc = jnp.concatenate([_dot(av, b_ref[s]) for s in range(nb)], axis=-1)
            elif tb:
                acc = _dot_nt(av, b_ref[...])
            else:
                acc = _dot(av, b_ref[...])
            vals = epilogue(acc, *[e[rows, :] for e in e_refs], *fv)
            for o, v in zip(o_refs, vals[:n_o]):
                o[rows, :] = v.astype(o.dtype)
            part = vals[n_o:]
            totals = part if totals is None else [t + p for t, p in zip(totals, part)]
        first_step = pl.program_id(0) == 0
        for o, v in zip(acc_refs, totals):
            @pl.when(first_step)
            def _(o=o, v=v):
                o[...] = v.astype(o.dtype)

            @pl.when(jnp.logical_not(first_step))
            def _(o=o, v=v):
                o[...] += v.astype(o.dtype)

    return pl.pallas_call(
        body, name=name, grid=(M // tm,), in_specs=in_specs, out_specs=out_specs, out_shape=out_shape,
        compiler_params=_cparams(("arbitrary",)),
    )(a, b, *extras, *fulls, *after)


def _rows(fn, rows, fulls, outs, accs=(), *, name, tile=512, after=()):
    first = rows[0][0] if isinstance(rows[0], tuple) else rows[0]
    T = first.shape[0]
    tile = _pick(T, tile)
    in_specs, args = [], []
    for r in rows:
        if isinstance(r, tuple):
            arr, w, cb = r
            in_specs.append(pl.BlockSpec((tile, w), lambda i, cb=cb: (i, cb)))
        else:
            arr = r
            in_specs.append(pl.BlockSpec((tile, arr.shape[1]), lambda i: (i, 0)))
        args.append(arr)
    for f in fulls:
        in_specs.append(pl.BlockSpec(f.shape, lambda i, nd=f.ndim: (0,) * nd))
        args.append(f)
    outs = [o if len(o) == 4 else (*o, o[0], 0) for o in outs]
    out_specs = [pl.BlockSpec((tile, w), lambda i, cb=cb: (i, cb)) for w, _, _, cb in outs]
    out_specs += [pl.BlockSpec(s, lambda i: (0, 0)) for s, _ in accs]
    out_shape = [jax.ShapeDtypeStruct((T, tw), dt) for _, dt, tw, _ in outs]
    out_shape += [jax.ShapeDtypeStruct(s, dt) for s, dt in accs]
    n_in, n_out = len(args), len(outs)
    for t in after:
        in_specs.append(pl.BlockSpec(memory_space=pl.ANY))
        args.append(t)

    def body(*refs):
        vals = fn(*[r[...] for r in refs[:n_in]])
        o_refs = refs[len(args):]
        for o, v in zip(o_refs[:n_out], vals[:n_out]):
            o[...] = v.astype(o.dtype)
        first_step = pl.program_id(0) == 0
        for o, v in zip(o_refs[n_out:], vals[n_out:]):
            @pl.when(first_step)
            def _(o=o, v=v):
                o[...] = v.astype(o.dtype)

            @pl.when(jnp.logical_not(first_step))
            def _(o=o, v=v):
                o[...] += v.astype(o.dtype)

    res = pl.pallas_call(
        body, name=name, grid=(T // tile,), in_specs=in_specs, out_specs=out_specs, out_shape=out_shape,
        compiler_params=_cparams(("arbitrary",)),
    )(*args)
    return res


def _rowsum(v, mxu):
    if not mxu:
        return jnp.sum(v, axis=-1, keepdims=True)
    ones = jnp.ones((v.shape[1], v.shape[1]), BF16)
    hi = v.astype(BF16)
    lo = (v - hi.astype(F32)).astype(BF16)
    return _dot(hi, ones) + _dot(lo, ones)


def _rms(x, g, mxu=False):
    r = lax.rsqrt(_rowsum(x * x, mxu) / x.shape[-1] + EPS)
    return (x * r) * g


def _rms_bwd(x, dy, g, n=None, mxu=False):
    n = x.shape[-1] if n is None else n
    r = lax.rsqrt(_rowsum(x * x, mxu) / n + EPS)
    xh = x * r
    dxh = dy * g
    dx = r * (dxh - xh * (_rowsum(dxh * xh, mxu) / n))
    return dx, dy * xh


def _colsum(v):
    return jnp.sum(v, axis=0, keepdims=True)


def _sigmoid(x):
    return 1.0 / (1.0 + jnp.exp(-x))


def _widen(v, width):
    reps = width // v.shape[1]
    return v if reps == 1 else jnp.concatenate([v] * reps, axis=-1)


def _rope_angles(T, dim):
    inv = (1.0 / (np.float32(ROPE_THETA) ** (np.arange(0, dim, 2, dtype=np.float32) / np.float32(dim)))).astype(np.float32)
    return np.arange(T, dtype=np.float32)[:, None] * inv[None, :]


def _ret_tables(T):
    ang = _rope_angles(T, RET_DK)
    log_gamma = np.log(np.float32(1.0) - np.float32(2.0) ** (-5.0 - np.arange(RET_HEADS, dtype=np.float32)))
    idx = np.arange(RET_BLOCK, dtype=np.float32)
    chunk = np.arange(RET_BLOCK) // CHUNK
    dist = idx[:, None] - idx[None, :]
    seen = np.where(chunk[:, None] == chunk[None, :], np.abs(dist), np.where(chunk[:, None] > chunk[None, :], dist, np.inf))
    intra = np.exp(log_gamma[:, None, None] * seen[None].astype(np.float32))
    qd = np.exp(log_gamma[:, None] * (idx + 1.0))[:, :, None]
    kd = np.exp(log_gamma[:, None] * (RET_BLOCK - 1.0 - idx))[:, :, None]
    cd = np.exp(log_gamma * RET_BLOCK)[:, None, None]
    return tuple(jnp.asarray(t, F32) for t in (np.cos(ang), np.sin(ang), intra, qd, kd, cd))


def _rope_half(x, c, s):
    x1, x2 = x[:, :RET_DK // 2], x[:, RET_DK // 2:]
    return jnp.concatenate([x1 * c - x2 * s, x2 * c + x1 * s], axis=-1)


def _rope_half_bwd(d, c, s):
    d1, d2 = d[:, :RET_DK // 2], d[:, RET_DK // 2:]
    return jnp.concatenate([d1 * c + d2 * s, d2 * c - d1 * s], axis=-1)


def _dot(a, b):
    return lax.dot_general(a, b, (((1,), (0,)), ((), ())), preferred_element_type=F32)


def _dot_nt(a, b):
    return lax.dot_general(a, b, (((1,), (1,)), ((), ())), preferred_element_type=F32)


def _dot_tn(a, b):
    return lax.dot_general(a, b, (((0,), (0,)), ((), ())), preferred_element_type=F32)


def _ret_specs(T, tb, rev):
    nj = T // tb
    jj = (lambda j: nj - 1 - j) if rev else (lambda j: j)
    g = RET_GROUP
    kq = RET_HEADS // g
    vq = 2 * RET_HEADS * RET_DK // (g * RET_DV)
    return dict(
        q=pl.BlockSpec((tb, g * RET_DK), lambda h, j: (jj(j), h)),
        k=pl.BlockSpec((tb, g * RET_DK), lambda h, j: (jj(j), kq + h)),
        v=pl.BlockSpec((tb, g * RET_DV), lambda h, j: (jj(j), vq + h)),
        tab=pl.BlockSpec((tb, RET_DK // 2), lambda h, j: (jj(j), 0)),
        intra=pl.BlockSpec((g, RET_BLOCK, RET_BLOCK), lambda h, j: (h, 0, 0)),
        dec=pl.BlockSpec((g, RET_BLOCK, 1), lambda h, j: (h, 0, 0)),
        cd=pl.BlockSpec((g, 1, 1), lambda h, j: (h, 0, 0)),
        o=pl.BlockSpec((tb, g * RET_DV), lambda h, j: (jj(j), h)),
        s=pl.BlockSpec((g, tb // RET_BLOCK, RET_DK, RET_DV), lambda h, j: (h, jj(j), 0, 0)),
    )


def _ret_fwd(proj, tabs, name):
    T = proj.shape[0]
    cos, sin, intra, qd, kd, cd = tabs
    tb = _pick(T, 512)
    cps = tb // RET_BLOCK
    sp = _ret_specs(T, tb, False)
    scale = RET_DK ** -0.5

    def body(q_ref, k_ref, v_ref, cos_ref, sin_ref, intra_ref, qd_ref, kd_ref, cd_ref, o_ref, s_ref, state):
        @pl.when(pl.program_id(1) == 0)
        def _():
            state[...] = jnp.zeros_like(state)

        for c in range(cps):
            rows = pl.ds(c * RET_BLOCK, RET_BLOCK)
            co, si = cos_ref[rows, :], sin_ref[rows, :]
            for h in range(RET_GROUP):
                hk, hv = slice(h * RET_DK, (h + 1) * RET_DK), slice(h * RET_DV, (h + 1) * RET_DV)
                q = _rope_half(q_ref[rows, hk].astype(F32), co, si)
                k = _rope_half(k_ref[rows, hk].astype(F32), co, si) * scale
                vb = v_ref[rows, hv].astype(BF16)
                st = state[h]
                sb = st.astype(BF16)
                s_ref[h, c] = sb
                sc = _dot_nt(q.astype(BF16), k.astype(BF16)) * intra_ref[h]
                inner = _dot(sc.astype(BF16), vb)
                cross = _dot((q * qd_ref[h]).astype(BF16), sb)
                o_ref[rows, hv] = inner + cross
                state[h] = st * cd_ref[h] + _dot_tn((k * kd_ref[h]).astype(BF16), vb)

    return pl.pallas_call(
        body, name=name, grid=(RET_HEADS // RET_GROUP, T // tb),
        in_specs=[sp["q"], sp["k"], sp["v"], sp["tab"], sp["tab"], sp["intra"], sp["dec"], sp["dec"], sp["cd"]],
        out_specs=[sp["o"], sp["s"]],
        out_shape=[jax.ShapeDtypeStruct((T, RET_HEADS * RET_DV), F32),
                   jax.ShapeDtypeStruct((RET_HEADS, T // RET_BLOCK, RET_DK, RET_DV), BF16)],
        scratch_shapes=[pltpu.VMEM((RET_GROUP, RET_DK, RET_DV), F32)],
        compiler_params=_cparams(("arbitrary", "arbitrary")),
    )(proj, proj, proj, cos, sin, intra, qd, kd, cd)


def _ret_bwd(proj, states, dout, dproj, tabs, name):
    assert RET_GROUP == 1
    T = proj.shape[0]
    cos, sin, intra, qd, kd, cd = tabs
    tb = _pick(T, 512)
    cps = tb // RET_BLOCK
    nj = T // tb
    sp = _ret_specs(T, tb, True)
    scale = RET_DK ** -0.5
    k0, v0 = RET_HEADS * RET_DK, 2 * RET_HEADS * RET_DK

    def body(q_ref, k_ref, v_ref, cos_ref, sin_ref, intra_ref, qd_ref, kd_ref, cd_ref, s_ref, do_ref, _dproj_in,
             out_ref, dq_s, dk_s, dv_s, sems, dstate):
        head, j = pl.program_id(0), pl.program_id(1)
        step = head * nj + j
        slot = step % 2
        dq_ref, dk_ref, dv_ref = dq_s.at[slot], dk_s.at[slot], dv_s.at[slot]

        @pl.when(j == 0)
        def _():
            dstate[...] = jnp.zeros_like(dstate)

        for c in reversed(range(cps)):
            rows = pl.ds(c * RET_BLOCK, RET_BLOCK)
            co, si = cos_ref[rows, :], sin_ref[rows, :]
            for h in range(RET_GROUP):
                hk, hv = slice(h * RET_DK, (h + 1) * RET_DK), slice(h * RET_DV, (h + 1) * RET_DV)
                q = _rope_half(q_ref[rows, hk].astype(F32), co, si)
                k = _rope_half(k_ref[rows, hk].astype(F32), co, si) * scale
                qb, kb = q.astype(BF16), k.astype(BF16)
                vb = v_ref[rows, hv].astype(BF16)
                dob = do_ref[rows, hv].astype(BF16)
                sb = s_ref[h, c]
                ia = intra_ref[h]
                pb = (_dot_nt(qb, kb) * ia).astype(BF16)
                dsn = dstate[h]
                dsb = dsn.astype(BF16)
                kdk = (k * kd_ref[h]).astype(BF16)
                qdq = (q * qd_ref[h]).astype(BF16)
                dv = _dot_tn(pb, dob) + _dot(kdk, dsb)
                dpb = (_dot_nt(dob, vb) * ia).astype(BF16)
                dq = _dot(dpb, kb) + _dot_nt(dob, sb) * qd_ref[h]
                dk = _dot_tn(dpb, qb) + _dot_nt(vb, dsb) * kd_ref[h]
                dstate[h] = dsn * cd_ref[h] + _dot_tn(qdq, dob)
                dq_ref[rows, hk] = _rope_half_bwd(dq, co, si).astype(BF16)
                dk_ref[rows, hk] = _rope_half_bwd(dk * scale, co, si).astype(BF16)
                dv_ref[rows, hv] = dv.astype(BF16)

        def copies(sl):
            r = pl.ds(pl.multiple_of((nj - 1 - j) * tb, tb), tb)
            cols = lambda first, w: pl.ds(pl.multiple_of(first + head * w, 128), w)
            return [pltpu.make_async_copy(dq_s.at[sl], out_ref.at[r, cols(0, RET_DK)], sems.at[sl, 0]),
                    pltpu.make_async_copy(dk_s.at[sl], out_ref.at[r, cols(k0, RET_DK)], sems.at[sl, 1]),
                    pltpu.make_async_copy(dv_s.at[sl], out_ref.at[r, cols(v0, RET_DV)], sems.at[sl, 2])]

        @pl.when(step > 0)
        def _():
            for cp in copies(1 - slot):
                cp.wait()

        for cp in copies(slot):
            cp.start()

        @pl.when(step == RET_HEADS * nj - 1)
        def _():
            for cp in copies(slot):
                cp.wait()

    return pl.pallas_call(
        body, name=name, grid=(RET_HEADS, nj),
        in_specs=[sp["q"], sp["k"], sp["v"], sp["tab"], sp["tab"], sp["intra"], sp["dec"], sp["dec"], sp["cd"],
                  sp["s"], sp["o"], pl.BlockSpec(memory_space=pl.ANY)],
        out_specs=pl.BlockSpec(memory_space=pl.ANY), out_shape=jax.ShapeDtypeStruct(dproj.shape, dproj.dtype),
        input_output_aliases={11: 0},
        scratch_shapes=[pltpu.VMEM((2, tb, RET_DK), BF16), pltpu.VMEM((2, tb, RET_DK), BF16),
                        pltpu.VMEM((2, tb, RET_DV), BF16), pltpu.SemaphoreType.DMA((2, 3)),
                        pltpu.VMEM((RET_GROUP, RET_DK, RET_DV), F32)],
        compiler_params=_cparams(("arbitrary", "arbitrary")),
    )(proj, proj, proj, cos, sin, intra, qd, kd, cd, states, dout, dproj)


def _ret_gate(out, proj, gn, name):
    def fn(o, g, *gains):
        g = g.astype(F32)
        parts = [_rms(o[:, h * RET_DV:(h + 1) * RET_DV], gains[h]) for h in range(RET_HEADS)]
        return (g * _sigmoid(g) * jnp.concatenate(parts, axis=-1),)
    w = RET_HEADS * RET_DV
    return _rows(fn, [out, (proj, w, 2)], [gn[h:h + 1] for h in range(RET_HEADS)], [(w, BF16)], name=name)[0]


def _ret_gate_bwd(out, proj, gn, dy, name):
    def fn(o, g, d, *gains):
        g = g.astype(F32)
        sg = _sigmoid(g)
        silu = g * sg
        dsilu = sg * (1.0 + g * (1.0 - sg))
        dos, dgs = [], []
        row = lax.broadcasted_iota(jnp.int32, (RET_HEADS, RET_DV), 0)
        dgn = jnp.zeros((RET_HEADS, RET_DV), F32)
        for h in range(RET_HEADS):
            sl = slice(h * RET_DV, (h + 1) * RET_DV)
            oh = o[:, sl]
            dgs.append(d[:, sl] * _rms(oh, gains[h]) * dsilu[:, sl])
            dx, dg = _rms_bwd(oh, d[:, sl] * silu[:, sl], gains[h])
            dos.append(dx)
            dgn = dgn + jnp.where(row == h, _colsum(dg), 0.0)
        return jnp.concatenate(dos, axis=-1), jnp.concatenate(dgs, axis=-1), dgn
    w = RET_HEADS * RET_DV
    return _rows(fn, [out, (proj, w, 2), dy], [gn[h:h + 1] for h in range(RET_HEADS)],
                 [(w, BF16), (w, BF16, proj.shape[1], 2)], [((RET_HEADS, RET_DV), F32)], name=name, tile=128)


def _mla_tables(T):
    ang = _rope_angles(T, MLA_ROPE)
    c, s = np.cos(ang), np.sin(ang)
    z32, z64 = np.zeros((T, 32), np.float32), np.zeros((T, 64), np.float32)
    cos_t = np.concatenate([c, c, z64], axis=1)
    sin_a = np.concatenate([-s, z32, z64], axis=1)
    sin_b = np.concatenate([z32, s, z64], axis=1)
    return tuple(jnp.asarray(t, F32) for t in (cos_t, sin_a, sin_b))


def _rope_blk(x, ct, sa, sb):
    return x * ct + pltpu.roll(x, 96, 1) * sa + pltpu.roll(x, 32, 1) * sb


def _rope_blk_bwd(d, ct, sa, sb):
    return d * ct + pltpu.roll(d * sa, 32, 1) + pltpu.roll(d * sb, 96, 1)


def _head_norm(x, gain):
    r = lax.rsqrt(_rowsum(x * x, True) / MLA_QKD + EPS)
    return (x * r) * gain


def _prep_heads(qv, kvv, kr, ct, sa, sb, gqv, gkv):
    qs, ks, vs = [], [], []
    for h in range(MLA_HEADS):
        b = h * MLA_HP
        y = _head_norm(qv[:, b:b + MLA_HP], gqv)
        qs += [y[:, :128], _rope_blk(y[:, 128:], ct, sa, sb)]
        y = _head_norm(jnp.concatenate([kvv[:, b:b + 128], kr], axis=-1), gkv)
        ks += [y[:, :128], _rope_blk(y[:, 128:], ct, sa, sb)]
        vs.append(kvv[:, b + 128:b + 256])
    return jnp.concatenate(qs, axis=-1), jnp.concatenate(ks, axis=-1), jnp.concatenate(vs, axis=-1)


def _mla_front(hn, W, tabs, name):
    wide = MLA_HEADS * MLA_HP
    gq = W["mla_q_norm"] * (MLA_QKD ** -0.5 * LOG2E)

    def epilogue(acc, ct, sa, sb, gqa, gkva, wuq, wukv, gqv, gkv):
        cqn = _rms(acc[:, :MLA_Q_RANK], gqa).astype(BF16)
        ckvn = _rms(acc[:, MLA_Q_RANK:MLA_Q_RANK + MLA_KV_RANK], gkva).astype(BF16)
        q = jnp.concatenate([_dot(cqn, wuq[s]) for s in range(N_CHIPS)], axis=-1).astype(BF16)
        kv = jnp.concatenate([_dot(ckvn, wukv[s]) for s in range(N_CHIPS)], axis=-1).astype(BF16)
        qf, kf, vf = _prep_heads(q.astype(F32), kv.astype(F32), acc[:, MLA_IN_PAD - 128:], ct, sa, sb, gqv, gkv)
        return acc, cqn, ckvn, q, kv, qf, kf, vf

    return _mm_rows(hn, W["mla_w_in"], extras=list(tabs),
                    fulls=[W["mla_q_a_norm"], W["mla_kv_a_norm"], W["mla_w_uq"], W["mla_w_ukv"], gq, W["mla_k_norm"]],
                    outs=[(MLA_IN_PAD, F32), (MLA_Q_RANK, BF16), (MLA_KV_RANK, BF16), (wide, BF16), (wide, BF16),
                          (wide, BF16), (wide, BF16), (MLA_HEADS * MLA_VD, BF16)],
                    epilogue=epilogue, name=name, tm=256)


def _prep_heads_bwd(qv, kvv, kr, ct, sa, sb, dqv, dkv, dvv, gqv, gkv):
    dqs, dkvs = [], []
    dkr = jnp.zeros_like(kr)
    dgq = jnp.zeros((1, MLA_HP), F32)
    dgk = jnp.zeros((1, MLA_HP), F32)
    for h in range(MLA_HEADS):
        b = h * MLA_HP
        dy = jnp.concatenate([dqv[:, b:b + 128], _rope_blk_bwd(dqv[:, b + 128:b + 256], ct, sa, sb)], axis=-1)
        dx, dg = _rms_bwd(qv[:, b:b + MLA_HP], dy, gqv, MLA_QKD, mxu=True)
        dqs.append(dx)
        dgq = dgq + _colsum(dg)
        dy = jnp.concatenate([dkv[:, b:b + 128], _rope_blk_bwd(dkv[:, b + 128:b + 256], ct, sa, sb)], axis=-1)
        dx, dg = _rms_bwd(jnp.concatenate([kvv[:, b:b + 128], kr], axis=-1), dy, gkv, MLA_QKD, mxu=True)
        dkvs += [dx[:, :128], dvv[:, h * MLA_VD:(h + 1) * MLA_VD].astype(F32)]
        dkr = dkr + dx[:, 128:]
        dgk = dgk + _colsum(dg)
    return jnp.concatenate(dqs, axis=-1), jnp.concatenate(dkvs, axis=-1), dkr, dgq, dgk


def _mla_back(q, kv, proj, h0, dh1, dqf, dkf, dvf, W, tabs, name):
    def fn(qv, kvv, pv, hv, dr, ct, sa, sb, dqv, dkv, dvv, gqv, gkv, gqa, gkva, wuq, wukv, w_in, g_mix):
        qv, kvv, dqv, dkv = (t.astype(F32) for t in (qv, kvv, dqv, dkv))
        dq, dkvx, dkr, dgq, dgk = _prep_heads_bwd(qv, kvv, pv[:, MLA_IN_PAD - 128:], ct, sa, sb, dqv, dkv, dvv, gqv, gkv)
        dq, dkvx = dq.astype(BF16), dkvx.astype(BF16)
        nq = wuq.shape[2]
        dcq = sum(_dot_nt(dq[:, s * nq:(s + 1) * nq], wuq[s]) for s in range(N_CHIPS))
        dckv = sum(_dot_nt(dkvx[:, s * nq:(s + 1) * nq], wukv[s]) for s in range(N_CHIPS))
        dxq, dgqa = _rms_bwd(pv[:, :MLA_Q_RANK], dcq, gqa)
        dxkv, dgkva = _rms_bwd(pv[:, MLA_Q_RANK:MLA_Q_RANK + MLA_KV_RANK], dckv, gkva)
        dproj = jnp.concatenate([dxq, dxkv, dkr], axis=-1).astype(BF16)
        dx, dgm = _rms_bwd(hv, _dot_nt(dproj, w_in), g_mix)
        return (dq, dkvx, dproj, dr + dx, dr + dx, dgq, dgk, _colsum(dgqa), _colsum(dgkva), _colsum(dgm))

    wide = MLA_HEADS * MLA_HP
    return _rows(fn, [q, kv, proj, h0, dh1, *tabs, dqf, dkf, dvf],
                 [W["mla_q_norm"], W["mla_k_norm"], W["mla_q_a_norm"], W["mla_kv_a_norm"], W["mla_w_uq"], W["mla_w_ukv"],
                  W["mla_w_in"], W["mix_norm"][1:2]],
                 [(wide, BF16), (wide, BF16), (MLA_IN_PAD, BF16), ROW_F32, ROW_BF16],
                 [((1, MLA_HP), F32), ((1, MLA_HP), F32), ((1, MLA_Q_RANK), F32), ((1, MLA_KV_RANK), F32),
                  ((1, D_MODEL), F32)], name=name, tile=256)


def _chunk_mask(qi, ki, tq, tk):
    shift = CHUNK.bit_length() - 1
    rq = lax.shift_right_arithmetic(qi * tq + lax.broadcasted_iota(jnp.int32, (tq, tk), 0), shift)
    ck = lax.shift_right_arithmetic(ki * tk + lax.broadcasted_iota(jnp.int32, (tq, tk), 1), shift)
    return ck <= rq


def _flash_fwd(qf, kf, vf, name):
    T = qf.shape[0]
    t = _pick(T, FLASH_T)
    n = T // t
    g = FLASH_HEADS

    def body(q_ref, k_ref, v_ref, o_ref, lse_ref, m_s, l_s, acc):
        qi = pl.program_id(1)
        m_s[...] = jnp.full_like(m_s, NEG)
        l_s[...] = jnp.zeros_like(l_s)
        acc[...] = jnp.zeros_like(acc)

        def step(kb, masked):
            rows = pl.ds(pl.multiple_of(kb * t, t), t)
            for h in range(g):
                hq, hv = slice(h * MLA_HP, (h + 1) * MLA_HP), slice(h * MLA_VD, (h + 1) * MLA_VD)
                s = _dot_nt(q_ref[:, hq], k_ref[rows, hq])
                if masked:
                    s = jnp.where(_chunk_mask(0, 0, t, t), s, NEG)
                m_prev = m_s[:, hv]
                m_new = jnp.maximum(m_prev, jnp.max(s, axis=-1, keepdims=True))
                alpha = jnp.exp2(m_prev - m_new)
                p = jnp.exp2(s - _widen(m_new, t))
                l_s[:, hv] = alpha * l_s[:, hv] + sum(p[:, i * 128:(i + 1) * 128] for i in range(t // 128))
                acc[:, hv] = acc[:, hv] * alpha + _dot(p.astype(BF16), v_ref[rows, hv])
                m_s[:, hv] = m_new

        @pl.loop(0, qi)
        def _(kb):
            step(kb, False)

        step(qi, True)
        for h in range(g):
            hv = slice(h * MLA_VD, (h + 1) * MLA_VD)
            l = jnp.sum(l_s[:, hv], axis=-1, keepdims=True)
            o_ref[:, hv] = acc[:, hv] / l
            lse_ref[:, hv] = m_s[:, hv] + jnp.log2(l)

    qmap = lambda h, i: (i, h)
    kmap = lambda h, i: (0, h)
    vec = pltpu.VMEM((t, g * MLA_VD), F32)
    return pl.pallas_call(
        body, name=name, grid=(MLA_HEADS // g, n),
        in_specs=[pl.BlockSpec((t, g * MLA_HP), qmap), pl.BlockSpec((T, g * MLA_HP), kmap),
                  pl.BlockSpec((T, g * MLA_VD), kmap)],
        out_specs=[pl.BlockSpec((t, g * MLA_VD), qmap), pl.BlockSpec((t, g * MLA_VD), qmap)],
        out_shape=[jax.ShapeDtypeStruct((T, MLA_HEADS * MLA_VD), F32),
                   jax.ShapeDtypeStruct((T, MLA_HEADS * MLA_VD), F32)],
        scratch_shapes=[vec, vec, vec],
        compiler_params=_cparams(("parallel", "arbitrary")),
    )(qf, kf, vf)


def _flash_bwd(qf, kf, vf, do16, lse, delta, name):
    T = qf.shape[0]
    t = _pick(T, FLASH_T)
    n = T // t
    scale = MLA_QKD ** -0.5

    def body(q_ref, k_ref, v_ref, do_ref, lse_ref, dl_ref, dq_out, dk_out, dv_out, dq_ref, dk_ref, dv_ref):
        kb = pl.program_id(1)

        @pl.when(kb == 0)
        def _():
            dq_ref[...] = jnp.zeros_like(dq_ref)

        dk_ref[...] = jnp.zeros_like(dk_ref)
        dv_ref[...] = jnp.zeros_like(dv_ref)
        k, v = k_ref[...], v_ref[...]

        def step(qb, masked):
            rows = pl.ds(pl.multiple_of(qb * t, t), t)
            q, dob = q_ref[rows, :], do_ref[rows, :]
            s = _dot_nt(q, k)
            if masked:
                s = jnp.where(_chunk_mask(0, 0, t, t), s, NEG)
            p = jnp.exp2(s - _widen(lse_ref[rows, :], t))
            ds = (p * (_dot_nt(dob, v) - _widen(dl_ref[rows, :], t))).astype(BF16)
            dv_ref[...] += _dot_tn(p.astype(BF16), dob)
            dk_ref[...] += _dot_tn(ds, q)
            dq_ref[rows, :] += _dot(ds, k)

        step(kb, True)

        @pl.loop(kb + 1, n)
        def _(qb):
            step(qb, False)

        dk_out[...] = (dk_ref[...] * (1.0 / LOG2E)).astype(BF16)
        dv_out[...] = dv_ref[...].astype(BF16)

        @pl.when(kb == n - 1)
        def _():
            dq_out[...] = (dq_ref[...] * scale).astype(BF16)

    qmap = lambda h, j: (0, h)
    kmap = lambda h, j: (j, h)
    return pl.pallas_call(
        body, name=name, grid=(MLA_HEADS, n),
        in_specs=[pl.BlockSpec((T, MLA_HP), qmap), pl.BlockSpec((t, MLA_HP), kmap), pl.BlockSpec((t, MLA_VD), kmap),
                  pl.BlockSpec((T, MLA_VD), qmap), pl.BlockSpec((T, MLA_VD), qmap), pl.BlockSpec((T, MLA_VD), qmap)],
        out_specs=[pl.BlockSpec((T, MLA_HP), qmap), pl.BlockSpec((t, MLA_HP), kmap), pl.BlockSpec((t, MLA_VD), kmap)],
        out_shape=[jax.ShapeDtypeStruct((T, MLA_HEADS * MLA_HP), BF16),
                   jax.ShapeDtypeStruct((T, MLA_HEADS * MLA_HP), BF16),
                   jax.ShapeDtypeStruct((T, MLA_HEADS * MLA_VD), BF16)],
        scratch_shapes=[pltpu.VMEM((T, MLA_HP), F32), pltpu.VMEM((t, MLA_HP), F32), pltpu.VMEM((t, MLA_VD), F32)],
        compiler_params=_cparams(("arbitrary", "arbitrary")),
    )(qf, kf, vf, do16, lse, delta)


MESH = pl.DeviceIdType.MESH
ANY = pl.BlockSpec(memory_space=pl.ANY)
_CHIP_FLIPS = ((1, 0), (0, 1), (1, 1))


def _place():
    return lax.axis_index("x"), lax.axis_index("y"), lax.axis_index("c")


def _other_chip(x, y, k):
    fx, fy = _CHIP_FLIPS[k]
    return ((1 - x) if fx else x), ((1 - y) if fy else y)


def _remote(src, dst, send_sems, recv_sems, k, to):
    return pltpu.make_async_remote_copy(src_ref=src, dst_ref=dst, send_sem=send_sems.at[k], recv_sem=recv_sems.at[k],
                                        device_id=to, device_id_type=MESH)


def _index(*vals):
    return jnp.stack(vals).astype(jnp.int32)


def _half(c, rows):
    return pl.ds(pl.multiple_of(c * rows, 16), rows)


def _gather_weights(parts, name, landed=None):
    n_w = len(parts)
    n_in = n_w if landed is None else 2 * n_w

    def body(*refs):
        ins, outs = refs[:n_w], refs[n_in:n_in + n_w]
        send_sems, recv_sems, local_sems = refs[n_in + n_w:]
        x, y, c = _place()
        j = 2 * x + y
        sibling = (x, y, 1 - c)
        chips = [_other_chip(x, y, k) for k in range(3)]
        pending = []
        for w in range(n_w):
            own = pltpu.make_async_copy(ins[w], outs[w].at[j], local_sems.at[w])
            own.start()
            pending.append(own)
        sent = []
        for w in range(n_w):
            if landed is not None:
                break
            r = _half(c, parts[w].shape[0] // 2)
            for k, (px, py) in enumerate(chips):
                cp = _remote(ins[w].at[r], outs[w].at[j, r], send_sems, recv_sems, 6 * w + k, (px, py, c))
                cp.start()
                sent.append(cp)
        for w in range(n_w):
            r = _half(c, parts[w].shape[0] // 2)
            for k, (px, py) in enumerate(chips):
                blk = outs[w].at[2 * px + py, r]
                if landed is None:
                    _remote(blk, blk, send_sems, recv_sems, 6 * w + k, (px, py, c)).wait_recv()
                cp = _remote(blk, blk, send_sems, recv_sems, 6 * w + 3 + k, sibling)
                cp.start()
                sent.append(cp)
        for w in range(n_w):
            r = _half(1 - c, parts[w].shape[0] // 2)
            for k, (px, py) in enumerate(chips):
                blk = outs[w].at[2 * px + py, r]
                _remote(blk, blk, send_sems, recv_sems, 6 * w + 3 + k, sibling).wait_recv()
        for cp in sent:
            cp.wait_send()
        for cp in pending:
            cp.wait()

    return pl.pallas_call(
        body, name=name, in_specs=[pl.BlockSpec(memory_space=pltpu.VMEM)] * n_w + [ANY] * (n_in - n_w),
        out_specs=[ANY] * n_w,
        out_shape=[jax.ShapeDtypeStruct((N_CHIPS, *p.shape), p.dtype) for p in parts],
        input_output_aliases={} if landed is None else {n_w + w: w for w in range(n_w)},
        scratch_shapes=[pltpu.SemaphoreType.DMA((6 * n_w,)), pltpu.SemaphoreType.DMA((6 * n_w,)),
                        pltpu.SemaphoreType.DMA((n_w,))],
        compiler_params=pltpu.CompilerParams(vmem_limit_bytes=VMEM_LIMIT),
    )(*parts, *(landed or []))


def _swap_halves(gs, name):
    n_w = len(gs)

    def body(*refs):
        g_refs, recv_refs = refs[:n_w], refs[n_w:2 * n_w]
        send_sems, recv_sems = refs[2 * n_w:]
        x, y, c = _place()
        sent = []
        for w in range(n_w):
            for jj in range(N_CHIPS):
                cp = _remote(g_refs[w].at[jj, 1 - c], recv_refs[w].at[jj], send_sems, recv_sems, N_CHIPS * w + jj,
                             (x, y, 1 - c))
                cp.start()
                sent.append(cp)
        for cp in sent:
            cp.wait()

    return pl.pallas_call(
        body, name=name, in_specs=[ANY] * n_w, out_specs=[ANY] * n_w,
        out_shape=[jax.ShapeDtypeStruct((N_CHIPS, *g.shape[2:]), g.dtype) for g in gs],
        scratch_shapes=[pltpu.SemaphoreType.DMA((N_CHIPS * n_w,)), pltpu.SemaphoreType.DMA((N_CHIPS * n_w,))],
    )(*gs)


def _pair_sum(g, recv, core, name):
    _, H, C = recv.shape
    tile = _pick(H, SUM_ROWS)

    def body(c_ref, own_ref, recv_ref, out_ref):
        out_ref[...] = (own_ref[...].astype(F32) + recv_ref[...].astype(F32)).astype(BF16)

    blk = pl.BlockSpec((None, tile, C), lambda jj, i, c: (jj, i, 0))
    return pl.pallas_call(
        body, name=name,
        grid_spec=pltpu.PrefetchScalarGridSpec(
            num_scalar_prefetch=1, grid=(N_CHIPS, H // tile),
            in_specs=[pl.BlockSpec((None, None, tile, C), lambda jj, i, c: (jj, c[0], i, 0)), blk],
            out_specs=blk),
        out_shape=jax.ShapeDtypeStruct((N_CHIPS, H, C), BF16),
        compiler_params=_cparams(("arbitrary", "arbitrary")),
    )(_index(core), g, recv)


def _chip_sum(g, recv, got, chip, core, name):
    _, H, C = recv.shape
    tile = _pick(H, SUM_ROWS)

    def body(s_ref, own_ref, recv_ref, g0_ref, g1_ref, g2_ref, out_ref):
        pair = own_ref[...].astype(F32) + recv_ref[...].astype(F32)
        out_ref[...] = ((pair + g0_ref[...].astype(F32)) + g1_ref[...].astype(F32)) + g2_ref[...].astype(F32)

    def got_spec(k):
        return pl.BlockSpec((None, tile, C), lambda i, s, k=k: (k, i, 0))

    return pl.pallas_call(
        body, name=name,
        grid_spec=pltpu.PrefetchScalarGridSpec(
            num_scalar_prefetch=1, grid=(H // tile,),
            in_specs=[pl.BlockSpec((None, None, tile, C), lambda i, s: (s[0], s[1], i, 0)),
                      pl.BlockSpec((None, tile, C), lambda i, s: (s[0], i, 0)), got_spec(0), got_spec(1), got_spec(2)],
            out_specs=pl.BlockSpec((None, tile, C), lambda i, s: (s[1], i, 0))),
        out_shape=jax.ShapeDtypeStruct((2, H, C), F32),
        compiler_params=_cparams(("arbitrary",)),
    )(_index(chip, core), g, recv, got, got, got)


def _share_halves(reds):
    n_w = len(reds)

    def body(*refs):
        out_refs = refs[n_w:2 * n_w]
        send_sems, recv_sems = refs[2 * n_w:]
        x, y, c = _place()
        sent = []
        for w in range(n_w):
            blk = out_refs[w].at[c]
            cp = _remote(blk, blk, send_sems, recv_sems, w, (x, y, 1 - c))
            cp.start()
            sent.append(cp)
        for cp in sent:
            cp.wait()

    return pl.pallas_call(
        body, name="grad_share_halves", in_specs=[ANY] * n_w, out_specs=[ANY] * n_w,
        out_shape=[jax.ShapeDtypeStruct(r.shape, r.dtype) for r in reds],
        input_output_aliases={w: w for w in range(n_w)},
        scratch_shapes=[pltpu.SemaphoreType.DMA((n_w,)), pltpu.SemaphoreType.DMA((n_w,))],
    )(*reds)


def _allsum_small(v, name):
    R, W = v.shape
    n_dev = 8
    vm = pl.BlockSpec(memory_space=pltpu.VMEM)

    def body(v_ref, out_ref, buf, send_sems, recv_sems):
        x, y, c = _place()
        me = 4 * x + 2 * y + c
        buf[me] = v_ref[...]
        sent = []
        for k in range(1, n_dev):
            peer = ((1 - x) if k & 4 else x, (1 - y) if k & 2 else y, (1 - c) if k & 1 else c)
            cp = _remote(v_ref, buf.at[me], send_sems, recv_sems, k - 1, peer)
            cp.start()
            sent.append(cp)
        for cp in sent:
            cp.wait_recv()
        for cp in sent:
            cp.wait_send()
        acc = buf[0]
        for q in range(1, n_dev):
            acc = acc + buf[q]
        out_ref[...] = acc

    return pl.pallas_call(
        body, name=name, in_specs=[vm], out_specs=vm, out_shape=jax.ShapeDtypeStruct((R, W), v.dtype),
        scratch_shapes=[pltpu.VMEM((n_dev, R, W), v.dtype), pltpu.SemaphoreType.DMA((n_dev - 1,)),
                        pltpu.SemaphoreType.DMA((n_dev - 1,))],
    )(v)


HBM = pl.BlockSpec(memory_space=pltpu.HBM)
SEM = pl.BlockSpec(memory_space=pltpu.SEMAPHORE)
_DATAFLOW = pltpu.SideEffectType.DATAFLOW_SIDE_EFFECTING


def _split_start(name, srcs, land_shapes, n_copies, copies, after=()):
    ns, nl = len(srcs), len(land_shapes)
    lands = [lax.empty(s.shape, s.dtype) for s in land_shapes]

    def body(*refs):
        outs = refs[ns + nl + len(after):]
        for cp in copies(refs[:ns], refs[ns:ns + nl], outs[0], outs[1]):
            cp.start()
        outs[-1][...] = jnp.zeros_like(outs[-1])

    sems = pltpu.SemaphoreType.DMA((n_copies,))
    res = pl.pallas_call(
        body, name=name, in_specs=[HBM] * (ns + nl) + [ANY] * len(after),
        out_specs=(SEM, SEM, *[HBM] * (ns + nl), pl.BlockSpec(memory_space=pltpu.VMEM)),
        out_shape=(sems, sems, *[pltpu.HBM(a.shape, a.dtype) for a in srcs],
                   *[pltpu.HBM(s.shape, s.dtype) for s in land_shapes], jax.ShapeDtypeStruct((8, 128), F32)),
        input_output_aliases={i: 2 + i for i in range(ns + nl)},
        compiler_params=pltpu.CompilerParams(has_side_effects=_DATAFLOW),
    )(*[pltpu.with_memory_space_constraint(a, pltpu.HBM) for a in [*srcs, *lands]], *after)
    return res[0], res[1], list(res[2:2 + ns]), list(res[2 + ns:2 + ns + nl]), res[-1]


def _split_wait(name, send_sems, recv_sems, srcs, lands, copies, after=()):
    ns, nl = len(srcs), len(lands)

    def body(*refs):
        for cp in copies(refs[:ns], refs[ns:ns + nl], refs[ns + nl], refs[ns + nl + 1]):
            cp.wait_send()
            cp.wait_recv()

    res = pl.pallas_call(
        body, name=name, in_specs=[HBM] * (ns + nl) + [SEM, SEM] + [ANY] * len(after), out_specs=[HBM] * (ns + nl),
        out_shape=[pltpu.HBM(a.shape, a.dtype) for a in [*srcs, *lands]],
        input_output_aliases={i: i for i in range(ns + nl)},
        compiler_params=pltpu.CompilerParams(has_side_effects=_DATAFLOW),
    )(*srcs, *lands, send_sems, recv_sems, *after)
    return list(res[ns:])


def _gather_copies(rows):
    def copies(src_refs, land_refs, send_sems, recv_sems):
        x, y, c = _place()
        j = 2 * x + y
        out = []
        for w in range(len(src_refs)):
            r = _half(c, rows[w] // 2)
            for k in range(3):
                px, py = _other_chip(x, y, k)
                out.append(_remote(src_refs[w].at[r], land_refs[w].at[j, r], send_sems, recv_sems, 3 * w + k, (px, py, c)))
        return out
    return copies


def _scatter_copies(src_refs, land_refs, send_sems, recv_sems):
    x, y, c = _place()
    j = 2 * x + y
    out = []
    for w in range(len(src_refs)):
        for k in range(3):
            px, py = _other_chip(x, y, k)
            pj = 2 * px + py
            out.append(_remote(src_refs[w].at[pj], land_refs[w].at[(j - pj + 4) % 4 - 1], send_sems, recv_sems, 3 * w + k,
                               (px, py, c)))
    return out


def _reduce_begin(grads, core, tag):
    names = list(grads)
    gs = [grads[k].reshape(N_CHIPS, 2, -1, grads[k].shape[-1]) for k in names]
    recvs = _swap_halves(gs, f"grad_swap_halves_{tag}")
    sums = [_pair_sum(g, r, core, f"pair_sum_{k}") for k, g, r in zip(names, gs, recvs)]
    return names, gs, recvs, sums


def _reduce_end(begun, gots, chip, core):
    names, gs, recvs, _ = begun
    return {k: _chip_sum(g, r, t, chip, core, f"chip_sum_{k}") for k, g, r, t in zip(names, gs, recvs, gots)}


def _got_shapes(sums):
    return [jax.ShapeDtypeStruct((3, *a.shape[1:]), a.dtype) for a in sums]


def _adamw(w, g, m, v, name, layers=1, layer=0, into=None):
    shape = w.shape
    cols = shape[-1]
    w3, m3, v3 = (t.reshape(layers, -1, cols) for t in (w, m, v))
    rows = w3.shape[1]
    tile = _pick(rows, ADAM_ROWS if cols <= 1024 else ADAM_ROWS // 2) if rows % 8 == 0 else rows
    n_in = 4 + (0 if into is None else 4)
    stack_g = layers > 1

    def body(*refs):
        wv, gv, mv, vv = (r[...] for r in refs[:4])
        d_ref, m_ref, v_ref = refs[len(refs) - 3:]
        m2 = ADAM_B1 * mv + (1.0 - ADAM_B1) * gv
        v2 = ADAM_B2 * vv + (1.0 - ADAM_B2) * jnp.square(gv)
        m_hat = m2 / (1.0 - ADAM_B1 ** ADAM_STEP)
        v_hat = v2 / (1.0 - ADAM_B2 ** ADAM_STEP)
        if stack_g:
            refs[n_in][...] = gv
        d_ref[...] = -ADAM_LR * (m_hat / (jnp.sqrt(v_hat) + ADAM_EPS) + ADAM_WD * wv)
        m_ref[...] = m2
        v_ref[...] = v2

    n_out = 4 if stack_g else 3
    lay = pl.BlockSpec((None, tile, cols), lambda i: (layer, i, 0))
    out = jax.ShapeDtypeStruct((layers, rows, cols), F32)
    res = pl.pallas_call(
        body, name=name, grid=(rows // tile,),
        in_specs=[lay, pl.BlockSpec((tile, cols), lambda i: (i, 0)), lay, lay] + [ANY] * (n_in - 4),
        out_specs=[lay] * n_out, out_shape=[out] * n_out,
        input_output_aliases={} if into is None else {4 + k: k for k in range(4)},
        compiler_params=_cparams(("arbitrary",)),
    )(w3, g.reshape(rows, cols), m3, v3, *([] if into is None else [t.reshape(layers, rows, cols) for t in into]))
    res = tuple(t.reshape(shape) for t in res)
    return res if stack_g else (g.reshape(shape), *res)


ROW_F32, ROW_BF16 = (D_MODEL, F32), (D_MODEL, BF16)


def _res_norm(acc, h, gain):
    hh = h + acc
    return hh, _rms(hh, gain)


def _dx_norm_bwd(d, w, h, dres, gain, name, **kw):
    def epilogue(acc, hv, dr, g):
        dx, dg = _rms_bwd(hv, acc, g)
        return dr + dx, dr + dx, _colsum(dg)
    return _mm_rows(d, w, tb=True, extras=[h, dres], fulls=[gain], outs=[ROW_F32, ROW_BF16], accs=[((1, D_MODEL), F32)],
                    epilogue=epilogue, name=name, **kw)


def _tail_fwd(h1, hn2, p16, W, i, tag, next_gain=None, target=None):
    a = _mm(hn2, W["mlp_w1"][i], bblk=True, outs=[BF16], name=f"{tag}_mlp_w1",
            epilogue=lambda acc: (jnp.square(jnp.maximum(acc, 0.0)),))
    h2, hn3 = _mm_rows(a, W["mlp_w2"][i], extras=[h1], fulls=[W["ple_norm"][i:i + 1]], outs=[ROW_F32, ROW_BF16],
                       epilogue=_res_norm, name=f"{tag}_mlp_w2")
    def embed(acc, pv, h, wp):
        gate = _sigmoid(acc)
        ppv = jnp.concatenate([_dot(pv, wp[s]) for s in range(N_CHIPS)], axis=-1)
        return gate, ppv, h + gate * ppv

    if target is None:
        def gated(acc, pv, h, wp, gain):
            gate, ppv, hh = embed(acc, pv, h, wp)
            return hh, ppv, gate, _rms(hh, gain)
        h3, pp, gate, hn = _mm_rows(hn3, W["ple_gate_w"][i], extras=[p16[i], h2], fulls=[W["ple_proj_w"][i], next_gain],
                                    outs=[ROW_F32, ROW_BF16, ROW_BF16, ROW_BF16], epilogue=gated, name=f"{tag}_ple")
        return h3, hn, (h1, hn2, a, h2, hn3, gate, pp)

    def gated_loss(acc, pv, h, t, wp):
        gate, ppv, hh = embed(acc, pv, h, wp)
        e = hh - t
        return ppv, gate, e * (1.0 / D_MODEL), jnp.full((1, 128), 0.5 / D_MODEL, F32) * jnp.sum(e * e)
    pp, gate, dy, loss = _mm_rows(hn3, W["ple_gate_w"][i], extras=[p16[i], h2, target], fulls=[W["ple_proj_w"][i]],
                                  outs=[ROW_BF16, ROW_BF16, ROW_F32], accs=[((1, 128), F32)], epilogue=gated_loss,
                                  name=f"{tag}_ple")
    return dy, loss, (h1, hn2, a, h2, hn3, gate, pp)


def _tail_bwd(dh3, saved, p16, W, i, tag, after=()):
    h1, hn2, a, h2, hn3, gate, pp = saved

    def embed_bwd(d, g, ppv, hv, wg, gain):
        g, ppv = g.astype(F32), ppv.astype(F32)
        dppv, dglv = (d * g).astype(BF16), (d * ppv * g * (1.0 - g)).astype(BF16)
        dx, dg = _rms_bwd(hv, _dot_nt(dglv, wg), gain)
        return dppv, dglv, d + dx, d + dx, _colsum(dg)

    def dw(kind, name):
        return (kind, 1, 0, None)

    dpp, dgl, dh2, dh2_16, d_ple_norm = _rows(
        embed_bwd, [dh3, gate, pp, h2], [W["ple_gate_w"][i], W["ple_norm"][i:i + 1]],
        [ROW_BF16, ROW_BF16, ROW_F32, ROW_BF16], [((1, D_MODEL), F32)], name=f"{tag}_ple_bwd", after=after)
    d_proj = _mm(p16[i], dpp, ta=True, outs=[BF16], dw=dw("cols", "ple_proj_w"), name=f"{tag}_d_ple_proj")
    d_gate = _mm(hn3, dgl, ta=True, outs=[BF16], dw=dw("rows", "ple_gate_w"), name=f"{tag}_d_ple_gate")
    d_w2 = _mm(a, dh2_16, ta=True, outs=[BF16], dw=dw("rows", "mlp_w2"), name=f"{tag}_d_mlp_w2")
    dz = _mm(dh2_16, W["mlp_w2"][i], tb=True, extras=[a], outs=[BF16], name=f"{tag}_mlp_w2_dx",
             epilogue=lambda acc, av: (acc * (2.0 * jnp.sqrt(av.astype(F32))),))
    d_w1 = _mm(hn2, dz, ta=True, outs=[BF16], dw=dw("cols", "mlp_w1"), name=f"{tag}_d_mlp_w1")
    dh1, dh1_16, d_mlp_norm = _dx_norm_bwd(dz, W["mlp_w1"][i], h1, dh2, W["mlp_norm"][i:i + 1], f"{tag}_mlp_w1_dx",
                                           bblk=True)
    big = {f"mlp_w1_{i}": d_w1, f"mlp_w2_{i}": d_w2, f"ple_gate_w_{i}": d_gate, f"ple_proj_w_{i}": d_proj}
    return dh1, dh1_16, big, dict(mlp_norm=d_mlp_norm, ple_norm=d_ple_norm)


def _ret_layer_fwd(h0, W, tabs, after=()):
    hn = _rows(lambda x, g: (_rms(x, g),), [h0], [W["mix_norm"][0:1]], [(D_MODEL, BF16)], name="ret_mix_norm",
               after=after)[0]
    proj = _mm(hn, W["ret_w_in"], bblk=True, outs=[BF16], name="ret_w_in")
    out, states = _ret_fwd(proj, tabs, "ret_scan")
    y = _ret_gate(out, proj, W["ret_gn"], "ret_gate")
    h1, hn2 = _mm_rows(y, W["ret_w_out"], extras=[h0], fulls=[W["mlp_norm"][0:1]], outs=[ROW_F32, ROW_BF16],
                       epilogue=_res_norm, name="ret_w_out")
    return h1, hn2, (h0, hn, proj, out, states, y)


def _d_ret_w_out(dh1_16, saved):
    return _mm(saved[5], dh1_16, ta=True, outs=[BF16], dw=("rows", 1, 0, None), name="d_ret_w_out")


def _ret_layer_bwd(dh1, dh1_16, saved, W, tabs, after=(), on_grads=None, d_w_out=None):
    h0, hn, proj, out, states, y = saved
    d_w_out = _d_ret_w_out(dh1_16, saved) if d_w_out is None else d_w_out
    dy = _mm(dh1_16, W["ret_w_out"], tb=True, name="ret_w_out_dx", after=after)
    dout, dproj, d_gn = _ret_gate_bwd(out, proj, W["ret_gn"], dy, "ret_gate_bwd")
    dproj = _ret_bwd(proj, states, dout, dproj, tabs, "ret_scan_bwd")
    d_w_in = _mm(hn, dproj, ta=True, outs=[BF16], dw=("cols", 1, 0, None), name="d_ret_w_in")
    big = dict(ret_w_in=d_w_in, ret_w_out=d_w_out)
    later = () if on_grads is None else on_grads(big)
    dh0, _, d_mix = _dx_norm_bwd(dproj, W["ret_w_in"], h0, dh1, W["mix_norm"][0:1], "ret_w_in_dx", bblk=True, tm=256,
                                 after=later)
    return dh0, big, dict(mix_norm=d_mix, ret_gn=d_gn)


def _mla_layer_fwd(h0, hn, W, tabs):
    proj, cqn, ckvn, q, kv, qf, kf, vf = _mla_front(hn, W, tabs, "mla_front")
    o, lse = _flash_fwd(qf, kf, vf, "mla_flash")
    h1, hn2 = _mm_rows(o, W["mla_w_out"], extras=[h0], fulls=[W["mlp_norm"][1:2]], outs=[ROW_F32, ROW_BF16],
                       epilogue=_res_norm, name="mla_w_out")
    return h1, hn2, (h0, hn, proj, cqn, ckvn, q, kv, qf, kf, vf, o, lse)


def _mla_layer_bwd(dh1, dh1_16, saved, W, tabs):
    h0, hn, proj, cqn, ckvn, q, kv, qf, kf, vf, o, lse = saved
    d_w_out = _mm(o, dh1_16, ta=True, outs=[BF16], dw=("rows", 1, 0, None), name="d_mla_w_out")
    def with_delta(acc, ov):
        parts = []
        for h in range(MLA_HEADS):
            sl = slice(h * MLA_VD, (h + 1) * MLA_VD)
            d = jnp.sum(acc[:, sl] * ov[:, sl], axis=-1, keepdims=True)
            parts.append(jnp.broadcast_to(d, (d.shape[0], MLA_VD)))
        return jnp.concatenate(parts, axis=-1), acc

    delta, do16 = _mm_rows(dh1_16, W["mla_w_out"], tb=True, extras=[o], outs=[ROW_F32, ROW_BF16], epilogue=with_delta,
                           name="mla_w_out_dx")
    dqf, dkf, dvf = _flash_bwd(qf, kf, vf, do16, lse, delta, "mla_flash_bwd")
    dq, dkv, dproj, dh0, dh0_16, d_gq, d_gk, d_gqa, d_gkva, d_mix = _mla_back(q, kv, proj, h0, dh1, dqf, dkf, dvf, W, tabs,
                                                                              "mla_back")
    d_w_uq = _mm(cqn, dq, ta=True, outs=[BF16], dw=("cols", 1, 0, None), name="d_mla_w_uq")
    d_w_ukv = _mm(ckvn, dkv, ta=True, outs=[BF16], dw=("cols", 1, 0, None), name="d_mla_w_ukv")
    d_w_in = _mm(hn, dproj, ta=True, outs=[BF16], dw=("rows", 1, 0, None), name="d_mla_w_in")
    return (dh0, dh0_16, dict(mla_w_in=d_w_in, mla_w_uq=d_w_uq, mla_w_ukv=d_w_ukv, mla_w_out=d_w_out),
            dict(mix_norm=d_mix, mla_q_a_norm=d_gqa, mla_kv_a_norm=d_gkva, mla_q_norm=d_gq, mla_k_norm=d_gk))


def _small_grads(n_ret, n_t0, n_mla, n_t1):
    return dict(
        mix_norm=jnp.concatenate([n_ret["mix_norm"], n_mla["mix_norm"]], axis=0),
        mlp_norm=jnp.concatenate([n_t0["mlp_norm"], n_t1["mlp_norm"]], axis=0),
        ple_norm=jnp.concatenate([n_t0["ple_norm"], n_t1["ple_norm"]], axis=0),
        ret_gn=n_ret["ret_gn"], mla_q_a_norm=n_mla["mla_q_a_norm"], mla_kv_a_norm=n_mla["mla_kv_a_norm"],
        mla_q_norm=n_mla["mla_q_norm"], mla_k_norm=n_mla["mla_k_norm"])


_ORDER = ("mix_norm", "ret_w_in", "ret_gn", "ret_w_out", "mla_w_in", "mla_q_a_norm", "mla_kv_a_norm", "mla_w_uq",
          "mla_w_ukv", "mla_q_norm", "mla_k_norm", "mla_w_out", "mlp_norm", "mlp_w1", "mlp_w2", "ple_norm",
          "ple_gate_w", "ple_proj_w")
_TWO_LAYER = ("mlp_w1", "mlp_w2", "ple_gate_w", "ple_proj_w")
HEADS_PER_CHIP = MLA_HEADS // N_CHIPS
GAIN_ROWS = 32


def _travel_parts(w):
    uq = jnp.pad(w["mla_w_uq"][0].reshape(MLA_Q_RANK, HEADS_PER_CHIP, MLA_QKD), ((0, 0), (0, 0), (0, MLA_HP - MLA_QKD)))
    parts = {"ret_w_in": w["ret_w_in"][0], "ret_w_out": w["ret_w_out"][0]}
    for k in _TWO_LAYER:
        parts[k + "_0"] = w[k][0]
    parts["mla_w_in"] = jnp.pad(w["mla_w_in"][0], ((0, 0), (0, MLA_IN_PAD - MLA_IN)))
    parts["mla_w_uq"] = uq.reshape(MLA_Q_RANK, HEADS_PER_CHIP * MLA_HP)
    parts["mla_w_ukv"] = w["mla_w_ukv"][0]
    parts["mla_w_out"] = w["mla_w_out"][0]
    for k in _TWO_LAYER:
        parts[k + "_1"] = w[k][1]
    gains = jnp.concatenate([_pad_row(w["ret_gn"]), _pad_row(w["mla_q_a_norm"]), _pad_row(w["mla_kv_a_norm"]),
                             jnp.zeros((GAIN_ROWS - 3, PACK_W), F32)], axis=0)
    return {"gains": gains, **{k: v.astype(BF16) for k, v in parts.items()}}


def _full_weights(full):
    rows = lambda a: a.reshape(-1, a.shape[-1])
    W = {k: full[k] for k in ("ret_w_in", "mla_w_uq", "mla_w_ukv") if k in full}
    for k in ("ret_w_out", "mla_w_in", "mla_w_out"):
        if k in full:
            W[k] = rows(full[k])
    for k, by_rows in (("mlp_w1", False), ("ple_proj_w", False), ("mlp_w2", True), ("ple_gate_w", True)):
        layers = [full.get(f"{k}_{i}") for i in range(2)]
        W[k] = [rows(t) if (by_rows and t is not None) else t for t in layers]
    return W


def _shard_grad(name, red, shape):
    if name == "mla_w_in":
        red = red.reshape(-1, MLA_IN_PAD)[:, :MLA_IN]
    elif name == "mla_w_uq":
        red = red.reshape(MLA_Q_RANK, HEADS_PER_CHIP, MLA_HP)[:, :, :MLA_QKD]
    return red.reshape(shape)


def _pad_row(v):
    v = v.reshape(1, -1)
    return jnp.pad(v, ((0, 0), (0, PACK_W - v.shape[1])))


def kernel(x, p, mix_norm, ret_w_in, ret_gn, ret_w_out, mla_w_in, mla_q_a_norm, mla_kv_a_norm, mla_w_uq, mla_w_ukv, mla_q_norm, mla_k_norm, mla_w_out, mlp_norm, mlp_w1, mlp_w2, ple_norm, ple_gate_w, ple_proj_w, loss_target, m_mix_norm, m_ret_w_in, m_ret_gn, m_ret_w_out, m_mla_w_in, m_mla_q_a_norm, m_mla_kv_a_norm, m_mla_w_uq, m_mla_w_ukv, m_mla_q_norm, m_mla_k_norm, m_mla_w_out, m_mlp_norm, m_mlp_w1, m_mlp_w2, m_ple_norm, m_ple_gate_w, m_ple_proj_w, v_mix_norm, v_ret_w_in, v_ret_gn, v_ret_w_out, v_mla_w_in, v_mla_q_a_norm, v_mla_kv_a_norm, v_mla_w_uq, v_mla_w_ukv, v_mla_q_norm, v_mla_k_norm, v_mla_w_out, v_mlp_norm, v_mlp_w1, v_mlp_w2, v_ple_norm, v_ple_gate_w, v_ple_proj_w):
    w = dict(mix_norm=mix_norm, ret_w_in=ret_w_in, ret_gn=ret_gn, ret_w_out=ret_w_out, mla_w_in=mla_w_in,
             mla_q_a_norm=mla_q_a_norm, mla_kv_a_norm=mla_kv_a_norm, mla_w_uq=mla_w_uq, mla_w_ukv=mla_w_ukv,
             mla_q_norm=mla_q_norm, mla_k_norm=mla_k_norm, mla_w_out=mla_w_out, mlp_norm=mlp_norm, mlp_w1=mlp_w1,
             mlp_w2=mlp_w2, ple_norm=ple_norm, ple_gate_w=ple_gate_w, ple_proj_w=ple_proj_w)
    m = dict(mix_norm=m_mix_norm, ret_w_in=m_ret_w_in, ret_gn=m_ret_gn, ret_w_out=m_ret_w_out, mla_w_in=m_mla_w_in,
             mla_q_a_norm=m_mla_q_a_norm, mla_kv_a_norm=m_mla_kv_a_norm, mla_w_uq=m_mla_w_uq, mla_w_ukv=m_mla_w_ukv,
             mla_q_norm=m_mla_q_norm, mla_k_norm=m_mla_k_norm, mla_w_out=m_mla_w_out, mlp_norm=m_mlp_norm,
             mlp_w1=m_mlp_w1, mlp_w2=m_mlp_w2, ple_norm=m_ple_norm, ple_gate_w=m_ple_gate_w, ple_proj_w=m_ple_proj_w)
    v = dict(mix_norm=v_mix_norm, ret_w_in=v_ret_w_in, ret_gn=v_ret_gn, ret_w_out=v_ret_w_out, mla_w_in=v_mla_w_in,
             mla_q_a_norm=v_mla_q_a_norm, mla_kv_a_norm=v_mla_kv_a_norm, mla_w_uq=v_mla_w_uq, mla_w_ukv=v_mla_w_ukv,
             mla_q_norm=v_mla_q_norm, mla_k_norm=v_mla_k_norm, mla_w_out=v_mla_w_out, mlp_norm=v_mlp_norm,
             mlp_w1=v_mlp_w1, mlp_w2=v_mlp_w2, ple_norm=v_ple_norm, ple_gate_w=v_ple_gate_w, ple_proj_w=v_ple_proj_w)
    xi, yi, ci = _place()
    chip = 2 * xi + yi
    n = N_CHIPS

    parts = _travel_parts(w)
    first = ("gains", "ret_w_in", "ret_w_out")
    mid = [k + "_0" for k in _TWO_LAYER]
    last = [k for k in parts if k not in first and k not in mid]
    full = dict(zip(first, _gather_weights([parts[k] for k in first], "gather_first")))

    def gather_behind(names, tag, after):
        copies = _gather_copies([parts[k].shape[0] for k in names])
        started = _split_start(f"gather_{tag}_start", [parts[k] for k in names],
                               [jax.ShapeDtypeStruct((n, *parts[k].shape), BF16) for k in names], 3 * len(names),
                               copies, after=after)

        def arrive(after):
            landed = _split_wait(f"gather_{tag}_wait", *started[:4], copies, after=after)
            full.update(zip(names, _gather_weights([parts[k] for k in names], f"gather_{tag}_finish", landed=landed)))
            W.update(_full_weights(full))
        return started[4], arrive

    mid_token, mid_arrive = gather_behind(mid, "mid", [full["ret_w_in"]])
    g_token, last_arrive = gather_behind(last, "last", [mid_token])
    gains = full["gains"]
    W = dict(mix_norm=mix_norm, mlp_norm=mlp_norm, ple_norm=ple_norm,
             mla_q_norm=jnp.pad(mla_q_norm, ((0, 0), (0, MLA_HP - MLA_QKD))),
             mla_k_norm=jnp.pad(mla_k_norm, ((0, 0), (0, MLA_HP - MLA_QKD))),
             ret_w_in=full["ret_w_in"], ret_w_out=full["ret_w_out"].reshape(-1, D_MODEL),
             ret_gn=gains[:, 0, :RET_HEADS * 128].reshape(n, RET_HEADS, 128).transpose(1, 0, 2).reshape(RET_HEADS, RET_DV),
             mla_q_a_norm=gains[:, 1, :MLA_Q_RANK // n].reshape(1, MLA_Q_RANK),
             mla_kv_a_norm=gains[:, 2, :MLA_KV_RANK // n].reshape(1, MLA_KV_RANK))
    x0, p16, target = x[0], p[:, 0].astype(BF16), loss_target[0]
    T = x0.shape[0]
    ret_tabs, mla_tabs = _ret_tables(T), _mla_tables(T)

    h1, hn, s_ret = _ret_layer_fwd(x0, W, ret_tabs, after=[g_token])
    mid_arrive([h1])
    h3, hn, s_tail0 = _tail_fwd(h1, hn, p16, W, 0, "l0", next_gain=W["mix_norm"][1:2])
    last_arrive([h3])
    h4, hn, s_mla = _mla_layer_fwd(h3, hn, W, mla_tabs)
    dy, loss, s_tail1 = _tail_fwd(h4, hn, p16, W, 1, "l1", target=target)

    dh4, dh4_16, g_t1, n_t1 = _tail_bwd(dy, s_tail1, p16, W, 1, "l1")
    dh3, _, g_mla, n_mla = _mla_layer_bwd(dh4, dh4_16, s_mla, W, mla_tabs)
    beg_a = _reduce_begin({**g_mla, **g_t1}, ci, "a")
    a_send, a_recv, a_src, a_land, a_token = _split_start(
        "scatter_a_start", beg_a[3], _got_shapes(beg_a[3]), 3 * len(beg_a[3]), _scatter_copies)
    dh1, dh1_16, g_t0, n_t0 = _tail_bwd(dh3, s_tail0, p16, W, 0, "l0", after=[a_token])
    d_ret_w_out = _d_ret_w_out(dh1_16, s_ret)
    beg_b = _reduce_begin({**g_t0, "ret_w_out": d_ret_w_out}, ci, "b")
    b_send, b_recv, b_src, b_land, b_token = _split_start(
        "scatter_b_start", beg_b[3], _got_shapes(beg_b[3]), 3 * len(beg_b[3]), _scatter_copies)
    stage_c = {}

    def start_c(g_ret):
        beg = _reduce_begin({"ret_w_in": g_ret["ret_w_in"]}, ci, "c")
        stage_c["beg"] = beg
        stage_c["st"] = _split_start("scatter_c_start", beg[3], _got_shapes(beg[3]), 3 * len(beg[3]), _scatter_copies)
        return [stage_c["st"][4]]

    dx, _, n_ret = _ret_layer_bwd(dh1, dh1_16, s_ret, W, ret_tabs, after=[b_token], on_grads=start_c,
                                  d_w_out=d_ret_w_out)
    got_a = _split_wait("scatter_a_wait", a_send, a_recv, a_src, a_land, _scatter_copies, after=[dx])
    got_b = _split_wait("scatter_b_wait", b_send, b_recv, b_src, b_land, _scatter_copies, after=[dx])
    got_c = _split_wait("scatter_c_wait", *stage_c["st"][:4], _scatter_copies, after=[dx])
    red = {**_reduce_end(beg_a, got_a, chip, ci), **_reduce_end(beg_b, got_b, chip, ci),
           **_reduce_end(stage_c["beg"], got_c, chip, ci)}
    red = dict(zip(red, _share_halves(list(red.values()))))
    gs = _small_grads(n_ret, n_t0, n_mla, n_t1)
    small_g = jnp.concatenate([
        gs["mix_norm"], gs["mlp_norm"], gs["ple_norm"], gs["ret_gn"].reshape(2, PACK_W), _pad_row(gs["mla_q_a_norm"]),
        _pad_row(gs["mla_kv_a_norm"]), _pad_row(gs["mla_q_norm"][:, :MLA_QKD]), _pad_row(gs["mla_k_norm"][:, :MLA_QKD]),
        _pad_row(loss[:, :1]), jnp.zeros((3, PACK_W), F32)], axis=0)
    tot = _allsum_small(small_g, "sum_small_grads")
    gn_all = tot[6:8].reshape(RET_HEADS, n, -1)
    g_small = dict(
        mix_norm=tot[0:2], mlp_norm=tot[2:4], ple_norm=tot[4:6],
        ret_gn=lax.dynamic_index_in_dim(gn_all, chip, axis=1, keepdims=False),
        mla_q_a_norm=lax.dynamic_index_in_dim(tot[8, :MLA_Q_RANK].reshape(n, -1), chip, axis=0, keepdims=True),
        mla_kv_a_norm=lax.dynamic_index_in_dim(tot[9, :MLA_KV_RANK].reshape(n, -1), chip, axis=0, keepdims=True),
        mla_q_norm=tot[10:11, :MLA_QKD], mla_k_norm=tot[11:12, :MLA_QKD])
    loss_out = tot[12, 0]

    outs = []
    for k in _ORDER:
        if k in _TWO_LAYER:
            res = None
            for i in (1, 0):
                res = _adamw(w[k], red[f"{k}_{i}"], m[k], v[k], f"adamw_{k}_{i}", layers=2, layer=i, into=res)
        elif k in red:
            res = _adamw(w[k], _shard_grad(k, red[k], w[k].shape), m[k], v[k], f"adamw_{k}")
        else:
            res = _adamw(w[k], g_small[k], m[k], v[k], f"adamw_{k}")
        outs.append(res)
    return (loss_out, dx[None], *[o[0] for o in outs], *[o[1] for o in outs], *[o[2] for o in outs],
            *[o[3] for o in outs])
```

```python
import jax
import jax.numpy as jnp
import numpy as np
from jax import lax
from jax.experimental import pallas as pl
from jax.experimental.pallas import tpu as pltpu

F32 = jnp.float32
BF16 = jnp.bfloat16

EPS = 1e-6
D_MODEL = 1024
CHUNK = 64
ROPE_THETA = 10000.0
RET_HEADS = 4
RET_DK = 256
RET_DV = 512
RET_GROUP = 1
RET_BLOCK = 256
MLA_HEADS = 8
MLA_ROPE = 64
MLA_QKD = 192
MLA_VD = 128
MLA_HP = 256
MLA_Q_RANK = 384
MLA_KV_RANK = 256
MLA_IN = 704
MLA_IN_PAD = 768
N_CHIPS = 4

ADAM_LR = 0.001
ADAM_B1 = 0.9
ADAM_B2 = 0.999
ADAM_EPS = 1e-08
ADAM_WD = 0.01
ADAM_STEP = 10

VMEM_LIMIT = 56 * 1024 * 1024
PACK_W = 1024
NEG = -1e30
LOG2E = 1.4426950408889634
FLASH_T = 512
FLASH_HEADS = 2
MM_SUB_ROWS = 256
SUM_ROWS = 512
ADAM_ROWS = 512


def _cparams(sem=None):
    return pltpu.CompilerParams(dimension_semantics=sem, vmem_limit_bytes=VMEM_LIMIT)


def _pick(dim, pref):
    if dim <= pref:
        return dim
    t = pref
    while dim % t:
        t //= 2
    return t


def _mm(a, b, *, name, ta=False, tb=False, bblk=False, outs=None, extras=(), epilogue=None, dw=None,
        tm=1024, tn=512, after=()):
    if ta:
        K, M = a.shape
    else:
        M, K = a.shape
    if bblk and tb:
        nb, N, Kq = b.shape
        assert nb * Kq == K
    elif bblk:
        nb, Kb, Nq = b.shape
        N = nb * Nq
        assert Kb == K
    else:
        N = b.shape[0] if tb else b.shape[1]
    tn = _pick(Nq if (bblk and not tb) else N, tn)
    if dw is not None and dw[0] == "cols":
        tn = _pick(N // N_CHIPS, tn)
    tm = _pick(M // N_CHIPS if (dw is not None and dw[0] == "rows") else M, tm)
    grid = (M // tm, N // tn)

    a_spec = pl.BlockSpec((K, tm), lambda i, j: (0, i)) if ta else pl.BlockSpec((tm, K), lambda i, j: (i, 0))
    if bblk and tb:
        b_spec = pl.BlockSpec((nb, tn, Kq), lambda i, j: (0, j, 0))
    elif bblk:
        npb = Nq // tn
        b_spec = pl.BlockSpec((None, K, tn), lambda i, j: (j // npb, 0, j % npb))
    elif tb:
        b_spec = pl.BlockSpec((tn, K), lambda i, j: (j, 0))
    else:
        b_spec = pl.BlockSpec((K, tn), lambda i, j: (0, j))
    in_specs = [a_spec, b_spec] + [pl.BlockSpec((tm, tn), lambda i, j: (i, j)) for _ in extras]
    args = [a, b, *extras]
    aliases = {}
    if outs is None:
        outs = [F32]
    if dw is None:
        o_specs = [pl.BlockSpec((tm, tn), lambda i, j: (i, j)) for _ in outs]
        o_shapes = [jax.ShapeDtypeStruct((M, N), dt) for dt in outs]
    else:
        kind, layers, layer, into = dw
        if kind == "cols":
            per = (N // N_CHIPS) // tn
            o_specs = [pl.BlockSpec((None, None, tm, tn), lambda i, j: (j // per, layer, i, j % per))]
            o_shapes = [jax.ShapeDtypeStruct((N_CHIPS, layers, M, N // N_CHIPS), outs[0])]
        else:
            per = (M // N_CHIPS) // tm
            o_specs = [pl.BlockSpec((None, None, tm, tn), lambda i, j: (i // per, layer, i % per, j))]
            o_shapes = [jax.ShapeDtypeStruct((N_CHIPS, layers, M // N_CHIPS, N), outs[0])]
        if into is not None:
            aliases = {len(args): 0}
            in_specs.append(pl.BlockSpec(memory_space=pl.ANY))
            args.append(into)
    for t in after:
        in_specs.append(pl.BlockSpec(memory_space=pl.ANY))
        args.append(t)
    n_e, n_o = len(extras), len(outs)

    sub = _pick(tm, MM_SUB_ROWS)

    def body(a_ref, b_ref, *rest):
        e_refs, o_refs = rest[:n_e], rest[len(rest) - n_o:]
        for r0 in range(0, tm, sub):
            rows = slice(r0, r0 + sub)
            av = (a_ref[:, rows] if ta else a_ref[rows, :]).astype(BF16)
            if bblk and tb:
                acc = _dot_nt(av[:, :Kq], b_ref[0].astype(BF16))
                for s in range(1, nb):
                    acc = acc + _dot_nt(av[:, s * Kq:(s + 1) * Kq], b_ref[s].astype(BF16))
            elif ta:
                acc = _dot_tn(av, b_ref[...].astype(BF16))
            elif tb:
                acc = _dot_nt(av, b_ref[...].astype(BF16))
            else:
                acc = _dot(av, b_ref[...].astype(BF16))
            vals = (acc,) if epilogue is None else epilogue(acc, *[e[rows, :] for e in e_refs])
            for o, v in zip(o_refs, vals):
                o[rows, :] = v.astype(o.dtype)

    res = pl.pallas_call(
        body, name=name, grid=grid, in_specs=in_specs, out_specs=o_specs, out_shape=o_shapes,
        input_output_aliases=aliases, compiler_params=_cparams(("parallel", "arbitrary")),
    )(*args)
    return res[0] if n_o == 1 else res


def _mm_rows(a, b, *, name, epilogue, outs, tb=False, bblk=False, extras=(), fulls=(), accs=(), tm=512, after=()):
    M, K = a.shape
    tm = _pick(M, tm)
    sub = _pick(tm, MM_SUB_ROWS)
    nb = b.shape[0] if bblk else 1
    n_e, n_f, n_o, n_a = len(extras), len(fulls), len(outs), len(accs)
    n_in = 2 + n_e + n_f + len(after)

    def whole(t):
        return pl.BlockSpec(t.shape, lambda i, nd=t.ndim: (0,) * nd)

    in_specs = [pl.BlockSpec((tm, K), lambda i: (i, 0)), whole(b)]
    in_specs += [pl.BlockSpec((tm, e.shape[1]), lambda i: (i, 0)) for e in extras] + [whole(f) for f in fulls]
    in_specs += [pl.BlockSpec(memory_space=pl.ANY) for _ in after]
    out_specs = [pl.BlockSpec((tm, w), lambda i: (i, 0)) for w, _ in outs] + [pl.BlockSpec(s, lambda i: (0, 0)) for s, _ in accs]
    out_shape = [jax.ShapeDtypeStruct((M, w), dt) for w, dt in outs] + [jax.ShapeDtypeStruct(s, dt) for s, dt in accs]

    def body(a_ref, b_ref, *rest):
        e_refs, f_refs = rest[:n_e], rest[n_e:n_e + n_f]
        o_refs, acc_refs = rest[n_in - 2:n_in - 2 + n_o], rest[n_in - 2 + n_o:]
        fv = [f[...] for f in f_refs]
        totals = None
        for r0 in range(0, tm, sub):
            rows = slice(r0, r0 + sub)
            av = a_ref[rows, :].astype(BF16)
            if bblk and tb:
                kq = K // nb
                acc = _dot_nt(av[:, :kq], b_ref[0])
                for s in range(1, nb):
                    acc = acc + _dot_nt(av[:, s * kq:(s + 1) * kq], b_ref[s])
            elif bblk:
                acc = jnp.concatenate([_dot(av, b_ref[s]) for s in range(nb)], axis=-1)
            elif tb:
                acc = _dot_nt(av, b_ref[...])
            else:
                acc = _dot(av, b_ref[...])
            vals = epilogue(acc, *[e[rows, :] for e in e_refs], *fv)
            for o, v in zip(o_refs, vals[:n_o]):
                o[rows, :] = v.astype(o.dtype)
            part = vals[n_o:]
            totals = part if totals is None else [t + p for t, p in zip(totals, part)]
        first_step = pl.program_id(0) == 0
        for o, v in zip(acc_refs, totals):
            @pl.when(first_step)
            def _(o=o, v=v):
                o[...] = v.astype(o.dtype)

            @pl.when(jnp.logical_not(first_step))
            def _(o=o, v=v):
                o[...] += v.astype(o.dtype)

    return pl.pallas_call(
        body, name=name, grid=(M // tm,), in_specs=in_specs, out_specs=out_specs, out_shape=out_shape,
        compiler_params=_cparams(("arbitrary",)),
    )(a, b, *extras, *fulls, *after)


def _rows(fn, rows, fulls, outs, accs=(), *, name, tile=512, after=()):
    first = rows[0][0] if isinstance(rows[0], tuple) else rows[0]
    T = first.shape[0]
    tile = _pick(T, tile)
    in_specs, args = [], []
    for r in rows:
        if isinstance(r, tuple):
            arr, w, cb = r
            in_specs.append(pl.BlockSpec((tile, w), lambda i, cb=cb: (i, cb)))
        else:
            arr = r
            in_specs.append(pl.BlockSpec((tile, arr.shape[1]), lambda i: (i, 0)))
        args.append(arr)
    for f in fulls:
        in_specs.append(pl.BlockSpec(f.shape, lambda i, nd=f.ndim: (0,) * nd))
        args.append(f)
    outs = [o if len(o) == 4 else (*o, o[0], 0) for o in outs]
    out_specs = [pl.BlockSpec((tile, w), lambda i, cb=cb: (i, cb)) for w, _, _, cb in outs]
    out_specs += [pl.BlockSpec(s, lambda i: (0, 0)) for s, _ in accs]
    out_shape = [jax.ShapeDtypeStruct((T, tw), dt) for _, dt, tw, _ in outs]
    out_shape += [jax.ShapeDtypeStruct(s, dt) for s, dt in accs]
    n_in, n_out = len(args), len(outs)
    for t in after:
        in_specs.append(pl.BlockSpec(memory_space=pl.ANY))
        args.append(t)

    def body(*refs):
        vals = fn(*[r[...] for r in refs[:n_in]])
        o_refs = refs[len(args):]
        for o, v in zip(o_refs[:n_out], vals[:n_out]):
            o[...] = v.astype(o.dtype)
        first_step = pl.program_id(0) == 0
        for o, v in zip(o_refs[n_out:], vals[n_out:]):
            @pl.when(first_step)
            def _(o=o, v=v):
                o[...] = v.astype(o.dtype)

            @pl.when(jnp.logical_not(first_step))
            def _(o=o, v=v):
                o[...] += v.astype(o.dtype)

    res = pl.pallas_call(
        body, name=name, grid=(T // tile,), in_specs=in_specs, out_specs=out_specs, out_shape=out_shape,
        compiler_params=_cparams(("arbitrary",)),
    )(*args)
    return res


def _rowsum(v, mxu):
    if not mxu:
        return jnp.sum(v, axis=-1, keepdims=True)
    ones = jnp.ones((v.shape[1], v.shape[1]), BF16)
    hi = v.astype(BF16)
    lo = (v - hi.astype(F32)).astype(BF16)
    return _dot(hi, ones) + _dot(lo, ones)


def _rms(x, g, mxu=False):
    r = lax.rsqrt(_rowsum(x * x, mxu) / x.shape[-1] + EPS)
    return (x * r) * g


def _rms_bwd(x, dy, g, n=None, mxu=False):
    n = x.shape[-1] if n is None else n
    r = lax.rsqrt(_rowsum(x * x, mxu) / n + EPS)
    xh = x * r
    dxh = dy * g
    dx = r * (dxh - xh * (_rowsum(dxh * xh, mxu) / n))
    return dx, dy * xh


def _colsum(v):
    return jnp.sum(v, axis=0, keepdims=True)


def _sigmoid(x):
    return 1.0 / (1.0 + jnp.exp(-x))


def _widen(v, width):
    reps = width // v.shape[1]
    return v if reps == 1 else jnp.concatenate([v] * reps, axis=-1)


def _rope_angles(T, dim):
    inv = (1.0 / (np.float32(ROPE_THETA) ** (np.arange(0, dim, 2, dtype=np.float32) / np.float32(dim)))).astype(np.float32)
    return np.arange(T, dtype=np.float32)[:, None] * inv[None, :]


def _ret_tables(T):
    ang = _rope_angles(T, RET_DK)
    log_gamma = np.log(np.float32(1.0) - np.float32(2.0) ** (-5.0 - np.arange(RET_HEADS, dtype=np.float32)))
    idx = np.arange(RET_BLOCK, dtype=np.float32)
    chunk = np.arange(RET_BLOCK) // CHUNK
    dist = idx[:, None] - idx[None, :]
    seen = np.where(chunk[:, None] == chunk[None, :], np.abs(dist), np.where(chunk[:, None] > chunk[None, :], dist, np.inf))
    intra = np.exp(log_gamma[:, None, None] * seen[None].astype(np.float32))
    qd = np.exp(log_gamma[:, None] * (idx + 1.0))[:, :, None]
    kd = np.exp(log_gamma[:, None] * (RET_BLOCK - 1.0 - idx))[:, :, None]
    cd = np.exp(log_gamma * RET_BLOCK)[:, None, None]
    return tuple(jnp.asarray(t, F32) for t in (np.cos(ang), np.sin(ang), intra, qd, kd, cd))


def _rope_half(x, c, s):
    x1, x2 = x[:, :RET_DK // 2], x[:, RET_DK // 2:]
    return jnp.concatenate([x1 * c - x2 * s, x2 * c + x1 * s], axis=-1)


def _rope_half_bwd(d, c, s):
    d1, d2 = d[:, :RET_DK // 2], d[:, RET_DK // 2:]
    return jnp.concatenate([d1 * c + d2 * s, d2 * c - d1 * s], axis=-1)


def _dot(a, b):
    return lax.dot_general(a, b, (((1,), (0,)), ((), ())), preferred_element_type=F32)


def _dot_nt(a, b):
    return lax.dot_general(a, b, (((1,), (1,)), ((), ())), preferred_element_type=F32)


def _dot_tn(a, b):
    return lax.dot_general(a, b, (((0,), (0,)), ((), ())), preferred_element_type=F32)


def _ret_specs(T, tb, rev):
    nj = T // tb
    jj = (lambda j: nj - 1 - j) if rev else (lambda j: j)
    g = RET_GROUP
    kq = RET_HEADS // g
    vq = 2 * RET_HEADS * RET_DK // (g * RET_DV)
    return dict(
        q=pl.BlockSpec((tb, g * RET_DK), lambda h, j: (jj(j), h)),
        k=pl.BlockSpec((tb, g * RET_DK), lambda h, j: (jj(j), kq + h)),
        v=pl.BlockSpec((tb, g * RET_DV), lambda h, j: (jj(j), vq + h)),
        tab=pl.BlockSpec((tb, RET_DK // 2), lambda h, j: (jj(j), 0)),
        intra=pl.BlockSpec((g, RET_BLOCK, RET_BLOCK), lambda h, j: (h, 0, 0)),
        dec=pl.BlockSpec((g, RET_BLOCK, 1), lambda h, j: (h, 0, 0)),
        cd=pl.BlockSpec((g, 1, 1), lambda h, j: (h, 0, 0)),
        o=pl.BlockSpec((tb, g * RET_DV), lambda h, j: (jj(j), h)),
        s=pl.BlockSpec((g, tb // RET_BLOCK, RET_DK, RET_DV), lambda h, j: (h, jj(j), 0, 0)),
    )


def _ret_fwd(proj, tabs, name):
    T = proj.shape[0]
    cos, sin, intra, qd, kd, cd = tabs
    tb = _pick(T, 512)
    cps = tb // RET_BLOCK
    sp = _ret_specs(T, tb, False)
    scale = RET_DK ** -0.5

    def body(q_ref, k_ref, v_ref, cos_ref, sin_ref, intra_ref, qd_ref, kd_ref, cd_ref, o_ref, s_ref, state):
        @pl.when(pl.program_id(1) == 0)
        def _():
            state[...] = jnp.zeros_like(state)

        for c in range(cps):
            rows = pl.ds(c * RET_BLOCK, RET_BLOCK)
            co, si = cos_ref[rows, :], sin_ref[rows, :]
            for h in range(RET_GROUP):
                hk, hv = slice(h * RET_DK, (h + 1) * RET_DK), slice(h * RET_DV, (h + 1) * RET_DV)
                q = _rope_half(q_ref[rows, hk].astype(F32), co, si)
                k = _rope_half(k_ref[rows, hk].astype(F32), co, si) * scale
                vb = v_ref[rows, hv].astype(BF16)
                st = state[h]
                sb = st.astype(BF16)
                s_ref[h, c] = sb
                sc = _dot_nt(q.astype(BF16), k.astype(BF16)) * intra_ref[h]
                inner = _dot(sc.astype(BF16), vb)
                cross = _dot((q * qd_ref[h]).astype(BF16), sb)
                o_ref[rows, hv] = inner + cross
                state[h] = st * cd_ref[h] + _dot_tn((k * kd_ref[h]).astype(BF16), vb)

    return pl.pallas_call(
        body, name=name, grid=(RET_HEADS // RET_GROUP, T // tb),
        in_specs=[sp["q"], sp["k"], sp["v"], sp["tab"], sp["tab"], sp["intra"], sp["dec"], sp["dec"], sp["cd"]],
        out_specs=[sp["o"], sp["s"]],
        out_shape=[jax.ShapeDtypeStruct((T, RET_HEADS * RET_DV), F32),
                   jax.ShapeDtypeStruct((RET_HEADS, T // RET_BLOCK, RET_DK, RET_DV), BF16)],
        scratch_shapes=[pltpu.VMEM((RET_GROUP, RET_DK, RET_DV), F32)],
        compiler_params=_cparams(("arbitrary", "arbitrary")),
    )(proj, proj, proj, cos, sin, intra, qd, kd, cd)


def _ret_bwd(proj, states, dout, dproj, tabs, name):
    assert RET_GROUP == 1
    T = proj.shape[0]
    cos, sin, intra, qd, kd, cd = tabs
    tb = _pick(T, 512)
    cps = tb // RET_BLOCK
    nj = T // tb
    sp = _ret_specs(T, tb, True)
    scale = RET_DK ** -0.5
    k0, v0 = RET_HEADS * RET_DK, 2 * RET_HEADS * RET_DK

    def body(q_ref, k_ref, v_ref, cos_ref, sin_ref, intra_ref, qd_ref, kd_ref, cd_ref, s_ref, do_ref, _dproj_in,
             out_ref, dq_s, dk_s, dv_s, sems, dstate):
        head, j = pl.program_id(0), pl.program_id(1)
        step = head * nj + j
        slot = step % 2
        dq_ref, dk_ref, dv_ref = dq_s.at[slot], dk_s.at[slot], dv_s.at[slot]

        @pl.when(j == 0)
        def _():
            dstate[...] = jnp.zeros_like(dstate)

        for c in reversed(range(cps)):
            rows = pl.ds(c * RET_BLOCK, RET_BLOCK)
            co, si = cos_ref[rows, :], sin_ref[rows, :]
            for h in range(RET_GROUP):
                hk, hv = slice(h * RET_DK, (h + 1) * RET_DK), slice(h * RET_DV, (h + 1) * RET_DV)
                q = _rope_half(q_ref[rows, hk].astype(F32), co, si)
                k = _rope_half(k_ref[rows, hk].astype(F32), co, si) * scale
                qb, kb = q.astype(BF16), k.astype(BF16)
                vb = v_ref[rows, hv].astype(BF16)
                dob = do_ref[rows, hv].astype(BF16)
                sb = s_ref[h, c]
                ia = intra_ref[h]
                pb = (_dot_nt(qb, kb) * ia).astype(BF16)
                dsn = dstate[h]
                dsb = dsn.astype(BF16)
                kdk = (k * kd_ref[h]).astype(BF16)
                qdq = (q * qd_ref[h]).astype(BF16)
                dv = _dot_tn(pb, dob) + _dot(kdk, dsb)
                dpb = (_dot_nt(dob, vb) * ia).astype(BF16)
                dq = _dot(dpb, kb) + _dot_nt(dob, sb) * qd_ref[h]
                dk = _dot_tn(dpb, qb) + _dot_nt(vb, dsb) * kd_ref[h]
                dstate[h] = dsn * cd_ref[h] + _dot_tn(qdq, dob)
                dq_ref[rows, hk] = _rope_half_bwd(dq, co, si).astype(BF16)
                dk_ref[rows, hk] = _rope_half_bwd(dk * scale, co, si).astype(BF16)
                dv_ref[rows, hv] = dv.astype(BF16)

        def copies(sl):
            r = pl.ds(pl.multiple_of((nj - 1 - j) * tb, tb), tb)
            cols = lambda first, w: pl.ds(pl.multiple_of(first + head * w, 128), w)
            return [pltpu.make_async_copy(dq_s.at[sl], out_ref.at[r, cols(0, RET_DK)], sems.at[sl, 0]),
                    pltpu.make_async_copy(dk_s.at[sl], out_ref.at[r, cols(k0, RET_DK)], sems.at[sl, 1]),
                    pltpu.make_async_copy(dv_s.at[sl], out_ref.at[r, cols(v0, RET_DV)], sems.at[sl, 2])]

        @pl.when(step > 0)
        def _():
            for cp in copies(1 - slot):
                cp.wait()

        for cp in copies(slot):
            cp.start()

        @pl.when(step == RET_HEADS * nj - 1)
        def _():
            for cp in copies(slot):
                cp.wait()

    return pl.pallas_call(
        body, name=name, grid=(RET_HEADS, nj),
        in_specs=[sp["q"], sp["k"], sp["v"], sp["tab"], sp["tab"], sp["intra"], sp["dec"], sp["dec"], sp["cd"],
                  sp["s"], sp["o"], pl.BlockSpec(memory_space=pl.ANY)],
        out_specs=pl.BlockSpec(memory_space=pl.ANY), out_shape=jax.ShapeDtypeStruct(dproj.shape, dproj.dtype),
        input_output_aliases={11: 0},
        scratch_shapes=[pltpu.VMEM((2, tb, RET_DK), BF16), pltpu.VMEM((2, tb, RET_DK), BF16),
                        pltpu.VMEM((2, tb, RET_DV), BF16), pltpu.SemaphoreType.DMA((2, 3)),
                        pltpu.VMEM((RET_GROUP, RET_DK, RET_DV), F32)],
        compiler_params=_cparams(("arbitrary", "arbitrary")),
    )(proj, proj, proj, cos, sin, intra, qd, kd, cd, states, dout, dproj)


def _ret_gate(out, proj, gn, name):
    def fn(o, g, *gains):
        g = g.astype(F32)
        parts = [_rms(o[:, h * RET_DV:(h + 1) * RET_DV], gains[h]) for h in range(RET_HEADS)]
        return (g * _sigmoid(g) * jnp.concatenate(parts, axis=-1),)
    w = RET_HEADS * RET_DV
    return _rows(fn, [out, (proj, w, 2)], [gn[h:h + 1] for h in range(RET_HEADS)], [(w, BF16)], name=name)[0]


def _ret_gate_bwd(out, proj, gn, dy, name):
    def fn(o, g, d, *gains):
        g = g.astype(F32)
        sg = _sigmoid(g)
        silu = g * sg
        dsilu = sg * (1.0 + g * (1.0 - sg))
        dos, dgs = [], []
        row = lax.broadcasted_iota(jnp.int32, (RET_HEADS, RET_DV), 0)
        dgn = jnp.zeros((RET_HEADS, RET_DV), F32)
        for h in range(RET_HEADS):
            sl = slice(h * RET_DV, (h + 1) * RET_DV)
            oh = o[:, sl]
            dgs.append(d[:, sl] * _rms(oh, gains[h]) * dsilu[:, sl])
            dx, dg = _rms_bwd(oh, d[:, sl] * silu[:, sl], gains[h])
            dos.append(dx)
            dgn = dgn + jnp.where(row == h, _colsum(dg), 0.0)
        return jnp.concatenate(dos, axis=-1), jnp.concatenate(dgs, axis=-1), dgn
    w = RET_HEADS * RET_DV
    return _rows(fn, [out, (proj, w, 2), dy], [gn[h:h + 1] for h in range(RET_HEADS)],
                 [(w, BF16), (w, BF16, proj.shape[1], 2)], [((RET_HEADS, RET_DV), F32)], name=name, tile=128)


def _mla_tables(T):
    ang = _rope_angles(T, MLA_ROPE)
    c, s = np.cos(ang), np.sin(ang)
    z32, z64 = np.zeros((T, 32), np.float32), np.zeros((T, 64), np.float32)
    cos_t = np.concatenate([c, c, z64], axis=1)
    sin_a = np.concatenate([-s, z32, z64], axis=1)
    sin_b = np.concatenate([z32, s, z64], axis=1)
    return tuple(jnp.asarray(t, F32) for t in (cos_t, sin_a, sin_b))


def _rope_blk(x, ct, sa, sb):
    return x * ct + pltpu.roll(x, 96, 1) * sa + pltpu.roll(x, 32, 1) * sb


def _rope_blk_bwd(d, ct, sa, sb):
    return d * ct + pltpu.roll(d * sa, 32, 1) + pltpu.roll(d * sb, 96, 1)


def _head_norm(x, gain):
    r = lax.rsqrt(_rowsum(x * x, True) / MLA_QKD + EPS)
    return (x * r) * gain


def _prep_heads(qv, kvv, kr, ct, sa, sb, gqv, gkv):
    qs, ks, vs = [], [], []
    for h in range(MLA_HEADS):
        b = h * MLA_HP
        y = _head_norm(qv[:, b:b + MLA_HP], gqv)
        qs += [y[:, :128], _rope_blk(y[:, 128:], ct, sa, sb)]
        y = _head_norm(jnp.concatenate([kvv[:, b:b + 128], kr], axis=-1), gkv)
        ks += [y[:, :128], _rope_blk(y[:, 128:], ct, sa, sb)]
        vs.append(kvv[:, b + 128:b + 256])
    return jnp.concatenate(qs, axis=-1), jnp.concatenate(ks, axis=-1), jnp.concatenate(vs, axis=-1)


def _mla_front(hn, W, tabs, name):
    wide = MLA_HEADS * MLA_HP
    gq = W["mla_q_norm"] * (MLA_QKD ** -0.5 * LOG2E)

    def epilogue(acc, ct, sa, sb, gqa, gkva, wuq, wukv, gqv, gkv):
        cqn = _rms(acc[:, :MLA_Q_RANK], gqa).astype(BF16)
        ckvn = _rms(acc[:, MLA_Q_RANK:MLA_Q_RANK + MLA_KV_RANK], gkva).astype(BF16)
        q = jnp.concatenate([_dot(cqn, wuq[s]) for s in range(N_CHIPS)], axis=-1).astype(BF16)
        kv = jnp.concatenate([_dot(ckvn, wukv[s]) for s in range(N_CHIPS)], axis=-1).astype(BF16)
        qf, kf, vf = _prep_heads(q.astype(F32), kv.astype(F32), acc[:, MLA_IN_PAD - 128:], ct, sa, sb, gqv, gkv)
        return acc, cqn, ckvn, q, kv, qf, kf, vf

    return _mm_rows(hn, W["mla_w_in"], extras=list(tabs),
                    fulls=[W["mla_q_a_norm"], W["mla_kv_a_norm"], W["mla_w_uq"], W["mla_w_ukv"], gq, W["mla_k_norm"]],
                    outs=[(MLA_IN_PAD, F32), (MLA_Q_RANK, BF16), (MLA_KV_RANK, BF16), (wide, BF16), (wide, BF16),
                          (wide, BF16), (wide, BF16), (MLA_HEADS * MLA_VD, BF16)],
                    epilogue=epilogue, name=name, tm=256)


def _prep_heads_bwd(qv, kvv, kr, ct, sa, sb, dqv, dkv, dvv, gqv, gkv):
    dqs, dkvs = [], []
    dkr = jnp.zeros_like(kr)
    dgq = jnp.zeros((1, MLA_HP), F32)
    dgk = jnp.zeros((1, MLA_HP), F32)
    for h in range(MLA_HEADS):
        b = h * MLA_HP
        dy = jnp.concatenate([dqv[:, b:b + 128], _rope_blk_bwd(dqv[:, b + 128:b + 256], ct, sa, sb)], axis=-1)
        dx, dg = _rms_bwd(qv[:, b:b + MLA_HP], dy, gqv, MLA_QKD, mxu=True)
        dqs.append(dx)
        dgq = dgq + _colsum(dg)
        dy = jnp.concatenate([dkv[:, b:b + 128], _rope_blk_bwd(dkv[:, b + 128:b + 256], ct, sa, sb)], axis=-1)
        dx, dg = _rms_bwd(jnp.concatenate([kvv[:, b:b + 128], kr], axis=-1), dy, gkv, MLA_QKD, mxu=True)
        dkvs += [dx[:, :128], dvv[:, h * MLA_VD:(h + 1) * MLA_VD].astype(F32)]
        dkr = dkr + dx[:, 128:]
        dgk = dgk + _colsum(dg)
    return jnp.concatenate(dqs, axis=-1), jnp.concatenate(dkvs, axis=-1), dkr, dgq, dgk


def _mla_back(q, kv, proj, h0, dh1, dqf, dkf, dvf, W, tabs, name):
    def fn(qv, kvv, pv, hv, dr, ct, sa, sb, dqv, dkv, dvv, gqv, gkv, gqa, gkva, wuq, wukv, w_in, g_mix):
        qv, kvv, dqv, dkv = (t.astype(F32) for t in (qv, kvv, dqv, dkv))
        dq, dkvx, dkr, dgq, dgk = _prep_heads_bwd(qv, kvv, pv[:, MLA_IN_PAD - 128:], ct, sa, sb, dqv, dkv, dvv, gqv, gkv)
        dq, dkvx = dq.astype(BF16), dkvx.astype(BF16)
        nq = wuq.shape[2]
        dcq = sum(_dot_nt(dq[:, s * nq:(s + 1) * nq], wuq[s]) for s in range(N_CHIPS))
        dckv = sum(_dot_nt(dkvx[:, s * nq:(s + 1) * nq], wukv[s]) for s in range(N_CHIPS))
        dxq, dgqa = _rms_bwd(pv[:, :MLA_Q_RANK], dcq, gqa)
        dxkv, dgkva = _rms_bwd(pv[:, MLA_Q_RANK:MLA_Q_RANK + MLA_KV_RANK], dckv, gkva)
        dproj = jnp.concatenate([dxq, dxkv, dkr], axis=-1).astype(BF16)
        dx, dgm = _rms_bwd(hv, _dot_nt(dproj, w_in), g_mix)
        return (dq, dkvx, dproj, dr + dx, dr + dx, dgq, dgk, _colsum(dgqa), _colsum(dgkva), _colsum(dgm))

    wide = MLA_HEADS * MLA_HP
    return _rows(fn, [q, kv, proj, h0, dh1, *tabs, dqf, dkf, dvf],
                 [W["mla_q_norm"], W["mla_k_norm"], W["mla_q_a_norm"], W["mla_kv_a_norm"], W["mla_w_uq"], W["mla_w_ukv"],
                  W["mla_w_in"], W["mix_norm"][1:2]],
                 [(wide, BF16), (wide, BF16), (MLA_IN_PAD, BF16), ROW_F32, ROW_BF16],
                 [((1, MLA_HP), F32), ((1, MLA_HP), F32), ((1, MLA_Q_RANK), F32), ((1, MLA_KV_RANK), F32),
                  ((1, D_MODEL), F32)], name=name, tile=256)


def _chunk_mask(qi, ki, tq, tk):
    shift = CHUNK.bit_length() - 1
    rq = lax.shift_right_arithmetic(qi * tq + lax.broadcasted_iota(jnp.int32, (tq, tk), 0), shift)
    ck = lax.shift_right_arithmetic(ki * tk + lax.broadcasted_iota(jnp.int32, (tq, tk), 1), shift)
    return ck <= rq


def _flash_fwd(qf, kf, vf, name):
    T = qf.shape[0]
    t = _pick(T, FLASH_T)
    n = T // t
    g = FLASH_HEADS

    def body(q_ref, k_ref, v_ref, o_ref, lse_ref, m_s, l_s, acc):
        qi = pl.program_id(1)
        m_s[...] = jnp.full_like(m_s, NEG)
        l_s[...] = jnp.zeros_like(l_s)
        acc[...] = jnp.zeros_like(acc)

        def step(kb, masked):
            rows = pl.ds(pl.multiple_of(kb * t, t), t)
            for h in range(g):
                hq, hv = slice(h * MLA_HP, (h + 1) * MLA_HP), slice(h * MLA_VD, (h + 1) * MLA_VD)
                s = _dot_nt(q_ref[:, hq], k_ref[rows, hq])
                if masked:
                    s = jnp.where(_chunk_mask(0, 0, t, t), s, NEG)
                m_prev = m_s[:, hv]
                m_new = jnp.maximum(m_prev, jnp.max(s, axis=-1, keepdims=True))
                alpha = jnp.exp2(m_prev - m_new)
                p = jnp.exp2(s - _widen(m_new, t))
                l_s[:, hv] = alpha * l_s[:, hv] + sum(p[:, i * 128:(i + 1) * 128] for i in range(t // 128))
                acc[:, hv] = acc[:, hv] * alpha + _dot(p.astype(BF16), v_ref[rows, hv])
                m_s[:, hv] = m_new

        @pl.loop(0, qi)
        def _(kb):
            step(kb, False)

        step(qi, True)
        for h in range(g):
            hv = slice(h * MLA_VD, (h + 1) * MLA_VD)
            l = jnp.sum(l_s[:, hv], axis=-1, keepdims=True)
            o_ref[:, hv] = acc[:, hv] / l
            lse_ref[:, hv] = m_s[:, hv] + jnp.log2(l)

    qmap = lambda h, i: (i, h)
    kmap = lambda h, i: (0, h)
    vec = pltpu.VMEM((t, g * MLA_VD), F32)
    return pl.pallas_call(
        body, name=name, grid=(MLA_HEADS // g, n),
        in_specs=[pl.BlockSpec((t, g * MLA_HP), qmap), pl.BlockSpec((T, g * MLA_HP), kmap),
                  pl.BlockSpec((T, g * MLA_VD), kmap)],
        out_specs=[pl.BlockSpec((t, g * MLA_VD), qmap), pl.BlockSpec((t, g * MLA_VD), qmap)],
        out_shape=[jax.ShapeDtypeStruct((T, MLA_HEADS * MLA_VD), F32),
                   jax.ShapeDtypeStruct((T, MLA_HEADS * MLA_VD), F32)],
        scratch_shapes=[vec, vec, vec],
        compiler_params=_cparams(("parallel", "arbitrary")),
    )(qf, kf, vf)


def _flash_bwd(qf, kf, vf, do16, lse, delta, name):
    T = qf.shape[0]
    t = _pick(T, FLASH_T)
    n = T // t
    scale = MLA_QKD ** -0.5

    def body(q_ref, k_ref, v_ref, do_ref, lse_ref, dl_ref, dq_out, dk_out, dv_out, dq_ref, dk_ref, dv_ref):
        kb = pl.program_id(1)

        @pl.when(kb == 0)
        def _():
            dq_ref[...] = jnp.zeros_like(dq_ref)

        dk_ref[...] = jnp.zeros_like(dk_ref)
        dv_ref[...] = jnp.zeros_like(dv_ref)
        k, v = k_ref[...], v_ref[...]

        def step(qb, masked):
            rows = pl.ds(pl.multiple_of(qb * t, t), t)
            q, dob = q_ref[rows, :], do_ref[rows, :]
            s = _dot_nt(q, k)
            if masked:
                s = jnp.where(_chunk_mask(0, 0, t, t), s, NEG)
            p = jnp.exp2(s - _widen(lse_ref[rows, :], t))
            ds = (p * (_dot_nt(dob, v) - _widen(dl_ref[rows, :], t))).astype(BF16)
            dv_ref[...] += _dot_tn(p.astype(BF16), dob)
            dk_ref[...] += _dot_tn(ds, q)
            dq_ref[rows, :] += _dot(ds, k)

        step(kb, True)

        @pl.loop(kb + 1, n)
        def _(qb):
            step(qb, False)

        dk_out[...] = (dk_ref[...] * (1.0 / LOG2E)).astype(BF16)
        dv_out[...] = dv_ref[...].astype(BF16)

        @pl.when(kb == n - 1)
        def _():
            dq_out[...] = (dq_ref[...] * scale).astype(BF16)

    qmap = lambda h, j: (0, h)
    kmap = lambda h, j: (j, h)
    return pl.pallas_call(
        body, name=name, grid=(MLA_HEADS, n),
        in_specs=[pl.BlockSpec((T, MLA_HP), qmap), pl.BlockSpec((t, MLA_HP), kmap), pl.BlockSpec((t, MLA_VD), kmap),
                  pl.BlockSpec((T, MLA_VD), qmap), pl.BlockSpec((T, MLA_VD), qmap), pl.BlockSpec((T, MLA_VD), qmap)],
        out_specs=[pl.BlockSpec((T, MLA_HP), qmap), pl.BlockSpec((t, MLA_HP), kmap), pl.BlockSpec((t, MLA_VD), kmap)],
        out_shape=[jax.ShapeDtypeStruct((T, MLA_HEADS * MLA_HP), BF16),
                   jax.ShapeDtypeStruct((T, MLA_HEADS * MLA_HP), BF16),
                   jax.ShapeDtypeStruct((T, MLA_HEADS * MLA_VD), BF16)],
        scratch_shapes=[pltpu.VMEM((T, MLA_HP), F32), pltpu.VMEM((t, MLA_HP), F32), pltpu.VMEM((t, MLA_VD), F32)],
        compiler_params=_cparams(("arbitrary", "arbitrary")),
    )(qf, kf, vf, do16, lse, delta)


MESH = pl.DeviceIdType.MESH
ANY = pl.BlockSpec(memory_space=pl.ANY)
_CHIP_FLIPS = ((1, 0), (0, 1), (1, 1))


def _place():
    return lax.axis_index("x"), lax.axis_index("y"), lax.axis_index("c")


def _other_chip(x, y, k):
    fx, fy = _CHIP_FLIPS[k]
    return ((1 - x) if fx else x), ((1 - y) if fy else y)


def _remote(src, dst, send_sems, recv_sems, k, to):
    return pltpu.make_async_remote_copy(src_ref=src, dst_ref=dst, send_sem=send_sems.at[k], recv_sem=recv_sems.at[k],
                                        device_id=to, device_id_type=MESH)


def _index(*vals):
    return jnp.stack(vals).astype(jnp.int32)


def _half(c, rows):
    return pl.ds(pl.multiple_of(c * rows, 16), rows)


def _gather_weights(parts, name, landed=None):
    n_w = len(parts)
    n_in = n_w if landed is None else 2 * n_w

    def body(*refs):
        ins, outs = refs[:n_w], refs[n_in:n_in + n_w]
        send_sems, recv_sems, local_sems = refs[n_in + n_w:]
        x, y, c = _place()
        j = 2 * x + y
        sibling = (x, y, 1 - c)
        chips = [_other_chip(x, y, k) for k in range(3)]
        pending = []
        for w in range(n_w):
            own = pltpu.make_async_copy(ins[w], outs[w].at[j], local_sems.at[w])
            own.start()
            pending.append(own)
        sent = []
        for w in range(n_w):
            if landed is not None:
                break
            r = _half(c, parts[w].shape[0] // 2)
            for k, (px, py) in enumerate(chips):
                cp = _remote(ins[w].at[r], outs[w].at[j, r], send_sems, recv_sems, 6 * w + k, (px, py, c))
                cp.start()
                sent.append(cp)
        for w in range(n_w):
            r = _half(c, parts[w].shape[0] // 2)
            for k, (px, py) in enumerate(chips):
                blk = outs[w].at[2 * px + py, r]
                if landed is None:
                    _remote(blk, blk, send_sems, recv_sems, 6 * w + k, (px, py, c)).wait_recv()
                cp = _remote(blk, blk, send_sems, recv_sems, 6 * w + 3 + k, sibling)
                cp.start()
                sent.append(cp)
        for w in range(n_w):
            r = _half(1 - c, parts[w].shape[0] // 2)
            for k, (px, py) in enumerate(chips):
                blk = outs[w].at[2 * px + py, r]
                _remote(blk, blk, send_sems, recv_sems, 6 * w + 3 + k, sibling).wait_recv()
        for cp in sent:
            cp.wait_send()
        for cp in pending:
            cp.wait()

    return pl.pallas_call(
        body, name=name, in_specs=[pl.BlockSpec(memory_space=pltpu.VMEM)] * n_w + [ANY] * (n_in - n_w),
        out_specs=[ANY] * n_w,
        out_shape=[jax.ShapeDtypeStruct((N_CHIPS, *p.shape), p.dtype) for p in parts],
        input_output_aliases={} if landed is None else {n_w + w: w for w in range(n_w)},
        scratch_shapes=[pltpu.SemaphoreType.DMA((6 * n_w,)), pltpu.SemaphoreType.DMA((6 * n_w,)),
                        pltpu.SemaphoreType.DMA((n_w,))],
        compiler_params=pltpu.CompilerParams(vmem_limit_bytes=VMEM_LIMIT),
    )(*parts, *(landed or []))


def _swap_halves(gs, name):
    n_w = len(gs)

    def body(*refs):
        g_refs, recv_refs = refs[:n_w], refs[n_w:2 * n_w]
        send_sems, recv_sems = refs[2 * n_w:]
        x, y, c = _place()
        sent = []
        for w in range(n_w):
            for jj in range(N_CHIPS):
                cp = _remote(g_refs[w].at[jj, 1 - c], recv_refs[w].at[jj], send_sems, recv_sems, N_CHIPS * w + jj,
                             (x, y, 1 - c))
                cp.start()
                sent.append(cp)
        for cp in sent:
            cp.wait()

    return pl.pallas_call(
        body, name=name, in_specs=[ANY] * n_w, out_specs=[ANY] * n_w,
        out_shape=[jax.ShapeDtypeStruct((N_CHIPS, *g.shape[2:]), g.dtype) for g in gs],
        scratch_shapes=[pltpu.SemaphoreType.DMA((N_CHIPS * n_w,)), pltpu.SemaphoreType.DMA((N_CHIPS * n_w,))],
    )(*gs)


def _pair_sum(g, recv, core, name):
    _, H, C = recv.shape
    tile = _pick(H, SUM_ROWS)

    def body(c_ref, own_ref, recv_ref, out_ref):
        out_ref[...] = (own_ref[...].astype(F32) + recv_ref[...].astype(F32)).astype(BF16)

    blk = pl.BlockSpec((None, tile, C), lambda jj, i, c: (jj, i, 0))
    return pl.pallas_call(
        body, name=name,
        grid_spec=pltpu.PrefetchScalarGridSpec(
            num_scalar_prefetch=1, grid=(N_CHIPS, H // tile),
            in_specs=[pl.BlockSpec((None, None, tile, C), lambda jj, i, c: (jj, c[0], i, 0)), blk],
            out_specs=blk),
        out_shape=jax.ShapeDtypeStruct((N_CHIPS, H, C), BF16),
        compiler_params=_cparams(("arbitrary", "arbitrary")),
    )(_index(core), g, recv)


def _chip_sum(g, recv, got, chip, core, name):
    _, H, C = recv.shape
    tile = _pick(H, SUM_ROWS)

    def body(s_ref, own_ref, recv_ref, g0_ref, g1_ref, g2_ref, out_ref):
        pair = own_ref[...].astype(F32) + recv_ref[...].astype(F32)
        out_ref[...] = ((pair + g0_ref[...].astype(F32)) + g1_ref[...].astype(F32)) + g2_ref[...].astype(F32)

    def got_spec(k):
        return pl.BlockSpec((None, tile, C), lambda i, s, k=k: (k, i, 0))

    return pl.pallas_call(
        body, name=name,
        grid_spec=pltpu.PrefetchScalarGridSpec(
            num_scalar_prefetch=1, grid=(H // tile,),
            in_specs=[pl.BlockSpec((None, None, tile, C), lambda i, s: (s[0], s[1], i, 0)),
                      pl.BlockSpec((None, tile, C), lambda i, s: (s[0], i, 0)), got_spec(0), got_spec(1), got_spec(2)],
            out_specs=pl.BlockSpec((None, tile, C), lambda i, s: (s[1], i, 0))),
        out_shape=jax.ShapeDtypeStruct((2, H, C), F32),
        compiler_params=_cparams(("arbitrary",)),
    )(_index(chip, core), g, recv, got, got, got)


def _share_halves(reds):
    n_w = len(reds)

    def body(*refs):
        out_refs = refs[n_w:2 * n_w]
        send_sems, recv_sems = refs[2 * n_w:]
        x, y, c = _place()
        sent = []
        for w in range(n_w):
            blk = out_refs[w].at[c]
            cp = _remote(blk, blk, send_sems, recv_sems, w, (x, y, 1 - c))
            cp.start()
            sent.append(cp)
        for cp in sent:
            cp.wait()

    return pl.pallas_call(
        body, name="grad_share_halves", in_specs=[ANY] * n_w, out_specs=[ANY] * n_w,
        out_shape=[jax.ShapeDtypeStruct(r.shape, r.dtype) for r in reds],
        input_output_aliases={w: w for w in range(n_w)},
        scratch_shapes=[pltpu.SemaphoreType.DMA((n_w,)), pltpu.SemaphoreType.DMA((n_w,))],
    )(*reds)


def _allsum_small(v, name):
    R, W = v.shape
    n_dev = 8
    vm = pl.BlockSpec(memory_space=pltpu.VMEM)

    def body(v_ref, out_ref, buf, send_sems, recv_sems):
        x, y, c = _place()
        me = 4 * x + 2 * y + c
        buf[me] = v_ref[...]
        sent = []
        for k in range(1, n_dev):
            peer = ((1 - x) if k & 4 else x, (1 - y) if k & 2 else y, (1 - c) if k & 1 else c)
            cp = _remote(v_ref, buf.at[me], send_sems, recv_sems, k - 1, peer)
            cp.start()
            sent.append(cp)
        for cp in sent:
            cp.wait_recv()
        for cp in sent:
            cp.wait_send()
        acc = buf[0]
        for q in range(1, n_dev):
            acc = acc + buf[q]
        out_ref[...] = acc

    return pl.pallas_call(
        body, name=name, in_specs=[vm], out_specs=vm, out_shape=jax.ShapeDtypeStruct((R, W), v.dtype),
        scratch_shapes=[pltpu.VMEM((n_dev, R, W), v.dtype), pltpu.SemaphoreType.DMA((n_dev - 1,)),
                        pltpu.SemaphoreType.DMA((n_dev - 1,))],
    )(v)


HBM = pl.BlockSpec(memory_space=pltpu.HBM)
SEM = pl.BlockSpec(memory_space=pltpu.SEMAPHORE)
_DATAFLOW = pltpu.SideEffectType.DATAFLOW_SIDE_EFFECTING


def _split_start(name, srcs, land_shapes, n_copies, copies, after=()):
    ns, nl = len(srcs), len(land_shapes)
    lands = [lax.empty(s.shape, s.dtype) for s in land_shapes]

    def body(*refs):
        outs = refs[ns + nl + len(after):]
        for cp in copies(refs[:ns], refs[ns:ns + nl], outs[0], outs[1]):
            cp.start()
        outs[-1][...] = jnp.zeros_like(outs[-1])

    sems = pltpu.SemaphoreType.DMA((n_copies,))
    res = pl.pallas_call(
        body, name=name, in_specs=[HBM] * (ns + nl) + [ANY] * len(after),
        out_specs=(SEM, SEM, *[HBM] * (ns + nl), pl.BlockSpec(memory_space=pltpu.VMEM)),
        out_shape=(sems, sems, *[pltpu.HBM(a.shape, a.dtype) for a in srcs],
                   *[pltpu.HBM(s.shape, s.dtype) for s in land_shapes], jax.ShapeDtypeStruct((8, 128), F32)),
        input_output_aliases={i: 2 + i for i in range(ns + nl)},
        compiler_params=pltpu.CompilerParams(has_side_effects=_DATAFLOW),
    )(*[pltpu.with_memory_space_constraint(a, pltpu.HBM) for a in [*srcs, *lands]], *after)
    return res[0], res[1], list(res[2:2 + ns]), list(res[2 + ns:2 + ns + nl]), res[-1]


def _split_wait(name, send_sems, recv_sems, srcs, lands, copies, after=()):
    ns, nl = len(srcs), len(lands)

    def body(*refs):
        for cp in copies(refs[:ns], refs[ns:ns + nl], refs[ns + nl], refs[ns + nl + 1]):
            cp.wait_send()
            cp.wait_recv()

    res = pl.pallas_call(
        body, name=name, in_specs=[HBM] * (ns + nl) + [SEM, SEM] + [ANY] * len(after), out_specs=[HBM] * (ns + nl),
        out_shape=[pltpu.HBM(a.shape, a.dtype) for a in [*srcs, *lands]],
        input_output_aliases={i: i for i in range(ns + nl)},
        compiler_params=pltpu.CompilerParams(has_side_effects=_DATAFLOW),
    )(*srcs, *lands, send_sems, recv_sems, *after)
    return list(res[ns:])


def _gather_copies(rows):
    def copies(src_refs, land_refs, send_sems, recv_sems):
        x, y, c = _place()
        j = 2 * x + y
        out = []
        for w in range(len(src_refs)):
            r = _half(c, rows[w] // 2)
            for k in range(3):
                px, py = _other_chip(x, y, k)
                out.append(_remote(src_refs[w].at[r], land_refs[w].at[j, r], send_sems, recv_sems, 3 * w + k, (px, py, c)))
        return out
    return copies


def _scatter_copies(src_refs, land_refs, send_sems, recv_sems):
    x, y, c = _place()
    j = 2 * x + y
    out = []
    for w in range(len(src_refs)):
        for k in range(3):
            px, py = _other_chip(x, y, k)
            pj = 2 * px + py
            out.append(_remote(src_refs[w].at[pj], land_refs[w].at[(j - pj + 4) % 4 - 1], send_sems, recv_sems, 3 * w + k,
                               (px, py, c)))
    return out


def _reduce_begin(grads, core, tag):
    names = list(grads)
    gs = [grads[k].reshape(N_CHIPS, 2, -1, grads[k].shape[-1]) for k in names]
    recvs = _swap_halves(gs, f"grad_swap_halves_{tag}")
    sums = [_pair_sum(g, r, core, f"pair_sum_{k}") for k, g, r in zip(names, gs, recvs)]
    return names, gs, recvs, sums


def _reduce_end(begun, gots, chip, core):
    names, gs, recvs, _ = begun
    return {k: _chip_sum(g, r, t, chip, core, f"chip_sum_{k}") for k, g, r, t in zip(names, gs, recvs, gots)}


def _got_shapes(sums):
    return [jax.ShapeDtypeStruct((3, *a.shape[1:]), a.dtype) for a in sums]


def _adamw(w, g, m, v, name, layers=1, layer=0, into=None):
    shape = w.shape
    cols = shape[-1]
    w3, m3, v3 = (t.reshape(layers, -1, cols) for t in (w, m, v))
    rows = w3.shape[1]
    tile = _pick(rows, ADAM_ROWS if cols <= 1024 else ADAM_ROWS // 2) if rows % 8 == 0 else rows
    n_in = 4 + (0 if into is None else 4)
    stack_g = layers > 1

    def body(*refs):
        wv, gv, mv, vv = (r[...] for r in refs[:4])
        d_ref, m_ref, v_ref = refs[len(refs) - 3:]
        m2 = ADAM_B1 * mv + (1.0 - ADAM_B1) * gv
        v2 = ADAM_B2 * vv + (1.0 - ADAM_B2) * jnp.square(gv)
        m_hat = m2 / (1.0 - ADAM_B1 ** ADAM_STEP)
        v_hat = v2 / (1.0 - ADAM_B2 ** ADAM_STEP)
        if stack_g:
            refs[n_in][...] = gv
        d_ref[...] = -ADAM_LR * (m_hat / (jnp.sqrt(v_hat) + ADAM_EPS) + ADAM_WD * wv)
        m_ref[...] = m2
        v_ref[...] = v2

    n_out = 4 if stack_g else 3
    lay = pl.BlockSpec((None, tile, cols), lambda i: (layer, i, 0))
    out = jax.ShapeDtypeStruct((layers, rows, cols), F32)
    res = pl.pallas_call(
        body, name=name, grid=(rows // tile,),
        in_specs=[lay, pl.BlockSpec((tile, cols), lambda i: (i, 0)), lay, lay] + [ANY] * (n_in - 4),
        out_specs=[lay] * n_out, out_shape=[out] * n_out,
        input_output_aliases={} if into is None else {4 + k: k for k in range(4)},
        compiler_params=_cparams(("arbitrary",)),
    )(w3, g.reshape(rows, cols), m3, v3, *([] if into is None else [t.reshape(layers, rows, cols) for t in into]))
    res = tuple(t.reshape(shape) for t in res)
    return res if stack_g else (g.reshape(shape), *res)


ROW_F32, ROW_BF16 = (D_MODEL, F32), (D_MODEL, BF16)


def _res_norm(acc, h, gain):
    hh = h + acc
    return hh, _rms(hh, gain)


def _dx_norm_bwd(d, w, h, dres, gain, name, **kw):
    def epilogue(acc, hv, dr, g):
        dx, dg = _rms_bwd(hv, acc, g)
        return dr + dx, dr + dx, _colsum(dg)
    return _mm_rows(d, w, tb=True, extras=[h, dres], fulls=[gain], outs=[ROW_F32, ROW_BF16], accs=[((1, D_MODEL), F32)],
                    epilogue=epilogue, name=name, **kw)


def _tail_fwd(h1, hn2, p16, W, i, tag, next_gain=None, target=None):
    a = _mm(hn2, W["mlp_w1"][i], bblk=True, outs=[BF16], name=f"{tag}_mlp_w1", tn=1024,
            epilogue=lambda acc: (jnp.square(jnp.maximum(acc, 0.0)),))
    h2, hn3 = _mm_rows(a, W["mlp_w2"][i], extras=[h1], fulls=[W["ple_norm"][i:i + 1]], outs=[ROW_F32, ROW_BF16],
                       epilogue=_res_norm, name=f"{tag}_mlp_w2")
    def embed(acc, pv, h, wp):
        gate = _sigmoid(acc)
        ppv = jnp.concatenate([_dot(pv, wp[s]) for s in range(N_CHIPS)], axis=-1)
        return gate, ppv, h + gate * ppv

    if target is None:
        def gated(acc, pv, h, wp, gain):
            gate, ppv, hh = embed(acc, pv, h, wp)
            return hh, ppv, gate, _rms(hh, gain)
        h3, pp, gate, hn = _mm_rows(hn3, W["ple_gate_w"][i], extras=[p16[i], h2], fulls=[W["ple_proj_w"][i], next_gain],
                                    outs=[ROW_F32, ROW_BF16, ROW_BF16, ROW_BF16], epilogue=gated, name=f"{tag}_ple")
        return h3, hn, (h1, hn2, a, h2, hn3, gate, pp)

    def gated_loss(acc, pv, h, t, wp):
        gate, ppv, hh = embed(acc, pv, h, wp)
        e = hh - t
        return ppv, gate, e * (1.0 / D_MODEL), jnp.full((1, 128), 0.5 / D_MODEL, F32) * jnp.sum(e * e)
    pp, gate, dy, loss = _mm_rows(hn3, W["ple_gate_w"][i], extras=[p16[i], h2, target], fulls=[W["ple_proj_w"][i]],
                                  outs=[ROW_BF16, ROW_BF16, ROW_F32], accs=[((1, 128), F32)], epilogue=gated_loss,
                                  name=f"{tag}_ple")
    return dy, loss, (h1, hn2, a, h2, hn3, gate, pp)


def _tail_bwd(dh3, saved, p16, W, i, tag, after=()):
    h1, hn2, a, h2, hn3, gate, pp = saved

    def embed_bwd(d, g, ppv, hv, wg, gain):
        g, ppv = g.astype(F32), ppv.astype(F32)
        dppv, dglv = (d * g).astype(BF16), (d * ppv * g * (1.0 - g)).astype(BF16)
        dx, dg = _rms_bwd(hv, _dot_nt(dglv, wg), gain)
        return dppv, dglv, d + dx, d + dx, _colsum(dg)

    def dw(kind, name):
        return (kind, 1, 0, None)

    dpp, dgl, dh2, dh2_16, d_ple_norm = _rows(
        embed_bwd, [dh3, gate, pp, h2], [W["ple_gate_w"][i], W["ple_norm"][i:i + 1]],
        [ROW_BF16, ROW_BF16, ROW_F32, ROW_BF16], [((1, D_MODEL), F32)], name=f"{tag}_ple_bwd", after=after)
    d_proj = _mm(p16[i], dpp, ta=True, outs=[BF16], dw=dw("cols", "ple_proj_w"), name=f"{tag}_d_ple_proj")
    d_gate = _mm(hn3, dgl, ta=True, outs=[BF16], dw=dw("rows", "ple_gate_w"), name=f"{tag}_d_ple_gate")
    d_w2 = _mm(a, dh2_16, ta=True, outs=[BF16], dw=dw("rows", "mlp_w2"), name=f"{tag}_d_mlp_w2", tn=1024)
    dz = _mm(dh2_16, W["mlp_w2"][i], tb=True, extras=[a], outs=[BF16], name=f"{tag}_mlp_w2_dx", tn=1024,
             epilogue=lambda acc, av: (acc * (2.0 * jnp.sqrt(av.astype(F32))),))
    d_w1 = _mm(hn2, dz, ta=True, outs=[BF16], dw=dw("cols", "mlp_w1"), name=f"{tag}_d_mlp_w1", tn=1024)
    dh1, dh1_16, d_mlp_norm = _dx_norm_bwd(dz, W["mlp_w1"][i], h1, dh2, W["mlp_norm"][i:i + 1], f"{tag}_mlp_w1_dx",
                                           bblk=True)
    big = {f"mlp_w1_{i}": d_w1, f"mlp_w2_{i}": d_w2, f"ple_gate_w_{i}": d_gate, f"ple_proj_w_{i}": d_proj}
    return dh1, dh1_16, big, dict(mlp_norm=d_mlp_norm, ple_norm=d_ple_norm)


def _ret_layer_fwd(h0, W, tabs, after=()):
    hn = _rows(lambda x, g: (_rms(x, g),), [h0], [W["mix_norm"][0:1]], [(D_MODEL, BF16)], name="ret_mix_norm",
               after=after)[0]
    proj = _mm(hn, W["ret_w_in"], bblk=True, outs=[BF16], name="ret_w_in", tn=768)
    out, states = _ret_fwd(proj, tabs, "ret_scan")
    y = _ret_gate(out, proj, W["ret_gn"], "ret_gate")
    h1, hn2 = _mm_rows(y, W["ret_w_out"], extras=[h0], fulls=[W["mlp_norm"][0:1]], outs=[ROW_F32, ROW_BF16],
                       epilogue=_res_norm, name="ret_w_out")
    return h1, hn2, (h0, hn, proj, out, states, y)


def _d_ret_w_out(dh1_16, saved):
    return _mm(saved[5], dh1_16, ta=True, outs=[BF16], dw=("rows", 1, 0, None), name="d_ret_w_out")


def _ret_layer_bwd(dh1, dh1_16, saved, W, tabs, after=(), on_grads=None, d_w_out=None):
    h0, hn, proj, out, states, y = saved
    d_w_out = _d_ret_w_out(dh1_16, saved) if d_w_out is None else d_w_out
    dy = _mm(dh1_16, W["ret_w_out"], tb=True, name="ret_w_out_dx", tn=1024, after=after)
    dout, dproj, d_gn = _ret_gate_bwd(out, proj, W["ret_gn"], dy, "ret_gate_bwd")
    dproj = _ret_bwd(proj, states, dout, dproj, tabs, "ret_scan_bwd")
    d_w_in = _mm(hn, dproj, ta=True, outs=[BF16], dw=("cols", 1, 0, None), name="d_ret_w_in", tn=768)
    big = dict(ret_w_in=d_w_in, ret_w_out=d_w_out)
    later = () if on_grads is None else on_grads(big)
    dh0, _, d_mix = _dx_norm_bwd(dproj, W["ret_w_in"], h0, dh1, W["mix_norm"][0:1], "ret_w_in_dx", bblk=True, tm=256,
                                 after=later)
    return dh0, big, dict(mix_norm=d_mix, ret_gn=d_gn)


def _mla_layer_fwd(h0, hn, W, tabs):
    proj, cqn, ckvn, q, kv, qf, kf, vf = _mla_front(hn, W, tabs, "mla_front")
    o, lse = _flash_fwd(qf, kf, vf, "mla_flash")
    h1, hn2 = _mm_rows(o, W["mla_w_out"], extras=[h0], fulls=[W["mlp_norm"][1:2]], outs=[ROW_F32, ROW_BF16],
                       epilogue=_res_norm, name="mla_w_out")
    return h1, hn2, (h0, hn, proj, cqn, ckvn, q, kv, qf, kf, vf, o, lse)


def _mla_layer_bwd(dh1, dh1_16, saved, W, tabs):
    h0, hn, proj, cqn, ckvn, q, kv, qf, kf, vf, o, lse = saved
    d_w_out = _mm(o, dh1_16, ta=True, outs=[BF16], dw=("rows", 1, 0, None), name="d_mla_w_out")
    def with_delta(acc, ov):
        parts = []
        for h in range(MLA_HEADS):
            sl = slice(h * MLA_VD, (h + 1) * MLA_VD)
            d = jnp.sum(acc[:, sl] * ov[:, sl], axis=-1, keepdims=True)
            parts.append(jnp.broadcast_to(d, (d.shape[0], MLA_VD)))
        return jnp.concatenate(parts, axis=-1), acc

    delta, do16 = _mm_rows(dh1_16, W["mla_w_out"], tb=True, extras=[o], outs=[ROW_F32, ROW_BF16], epilogue=with_delta,
                           name="mla_w_out_dx")
    dqf, dkf, dvf = _flash_bwd(qf, kf, vf, do16, lse, delta, "mla_flash_bwd")
    dq, dkv, dproj, dh0, dh0_16, d_gq, d_gk, d_gqa, d_gkva, d_mix = _mla_back(q, kv, proj, h0, dh1, dqf, dkf, dvf, W, tabs,
                                                                              "mla_back")
    d_w_uq = _mm(cqn, dq, ta=True, outs=[BF16], dw=("cols", 1, 0, None), name="d_mla_w_uq")
    d_w_ukv = _mm(ckvn, dkv, ta=True, outs=[BF16], dw=("cols", 1, 0, None), name="d_mla_w_ukv")
    d_w_in = _mm(hn, dproj, ta=True, outs=[BF16], dw=("rows", 1, 0, None), name="d_mla_w_in")
    return (dh0, dh0_16, dict(mla_w_in=d_w_in, mla_w_uq=d_w_uq, mla_w_ukv=d_w_ukv, mla_w_out=d_w_out),
            dict(mix_norm=d_mix, mla_q_a_norm=d_gqa, mla_kv_a_norm=d_gkva, mla_q_norm=d_gq, mla_k_norm=d_gk))


def _small_grads(n_ret, n_t0, n_mla, n_t1):
    return dict(
        mix_norm=jnp.concatenate([n_ret["mix_norm"], n_mla["mix_norm"]], axis=0),
        mlp_norm=jnp.concatenate([n_t0["mlp_norm"], n_t1["mlp_norm"]], axis=0),
        ple_norm=jnp.concatenate([n_t0["ple_norm"], n_t1["ple_norm"]], axis=0),
        ret_gn=n_ret["ret_gn"], mla_q_a_norm=n_mla["mla_q_a_norm"], mla_kv_a_norm=n_mla["mla_kv_a_norm"],
        mla_q_norm=n_mla["mla_q_norm"], mla_k_norm=n_mla["mla_k_norm"])


_ORDER = ("mix_norm", "ret_w_in", "ret_gn", "ret_w_out", "mla_w_in", "mla_q_a_norm", "mla_kv_a_norm", "mla_w_uq",
          "mla_w_ukv", "mla_q_norm", "mla_k_norm", "mla_w_out", "mlp_norm", "mlp_w1", "mlp_w2", "ple_norm",
          "ple_gate_w", "ple_proj_w")
_TWO_LAYER = ("mlp_w1", "mlp_w2", "ple_gate_w", "ple_proj_w")
HEADS_PER_CHIP = MLA_HEADS // N_CHIPS
GAIN_ROWS = 32


def _travel_parts(w):
    uq = jnp.pad(w["mla_w_uq"][0].reshape(MLA_Q_RANK, HEADS_PER_CHIP, MLA_QKD), ((0, 0), (0, 0), (0, MLA_HP - MLA_QKD)))
    parts = {"ret_w_in": w["ret_w_in"][0], "ret_w_out": w["ret_w_out"][0]}
    for k in _TWO_LAYER:
        parts[k + "_0"] = w[k][0]
    parts["mla_w_in"] = jnp.pad(w["mla_w_in"][0], ((0, 0), (0, MLA_IN_PAD - MLA_IN)))
    parts["mla_w_uq"] = uq.reshape(MLA_Q_RANK, HEADS_PER_CHIP * MLA_HP)
    parts["mla_w_ukv"] = w["mla_w_ukv"][0]
    parts["mla_w_out"] = w["mla_w_out"][0]
    for k in _TWO_LAYER:
        parts[k + "_1"] = w[k][1]
    gains = jnp.concatenate([_pad_row(w["ret_gn"]), _pad_row(w["mla_q_a_norm"]), _pad_row(w["mla_kv_a_norm"]),
                             jnp.zeros((GAIN_ROWS - 3, PACK_W), F32)], axis=0)
    return {"gains": gains, **{k: v.astype(BF16) for k, v in parts.items()}}


def _full_weights(full):
    rows = lambda a: a.reshape(-1, a.shape[-1])
    W = {k: full[k] for k in ("ret_w_in", "mla_w_uq", "mla_w_ukv") if k in full}
    for k in ("ret_w_out", "mla_w_in", "mla_w_out"):
        if k in full:
            W[k] = rows(full[k])
    for k, by_rows in (("mlp_w1", False), ("ple_proj_w", False), ("mlp_w2", True), ("ple_gate_w", True)):
        layers = [full.get(f"{k}_{i}") for i in range(2)]
        W[k] = [rows(t) if (by_rows and t is not None) else t for t in layers]
    return W


def _shard_grad(name, red, shape):
    if name == "mla_w_in":
        red = red.reshape(-1, MLA_IN_PAD)[:, :MLA_IN]
    elif name == "mla_w_uq":
        red = red.reshape(MLA_Q_RANK, HEADS_PER_CHIP, MLA_HP)[:, :, :MLA_QKD]
    return red.reshape(shape)


def _pad_row(v):
    v = v.reshape(1, -1)
    return jnp.pad(v, ((0, 0), (0, PACK_W - v.shape[1])))


def kernel(x, p, mix_norm, ret_w_in, ret_gn, ret_w_out, mla_w_in, mla_q_a_norm, mla_kv_a_norm, mla_w_uq, mla_w_ukv, mla_q_norm, mla_k_norm, mla_w_out, mlp_norm, mlp_w1, mlp_w2, ple_norm, ple_gate_w, ple_proj_w, loss_target, m_mix_norm, m_ret_w_in, m_ret_gn, m_ret_w_out, m_mla_w_in, m_mla_q_a_norm, m_mla_kv_a_norm, m_mla_w_uq, m_mla_w_ukv, m_mla_q_norm, m_mla_k_norm, m_mla_w_out, m_mlp_norm, m_mlp_w1, m_mlp_w2, m_ple_norm, m_ple_gate_w, m_ple_proj_w, v_mix_norm, v_ret_w_in, v_ret_gn, v_ret_w_out, v_mla_w_in, v_mla_q_a_norm, v_mla_kv_a_norm, v_mla_w_uq, v_mla_w_ukv, v_mla_q_norm, v_mla_k_norm, v_mla_w_out, v_mlp_norm, v_mlp_w1, v_mlp_w2, v_ple_norm, v_ple_gate_w, v_ple_proj_w):
    w = dict(mix_norm=mix_norm, ret_w_in=ret_w_in, ret_gn=ret_gn, ret_w_out=ret_w_out, mla_w_in=mla_w_in,
             mla_q_a_norm=mla_q_a_norm, mla_kv_a_norm=mla_kv_a_norm, mla_w_uq=mla_w_uq, mla_w_ukv=mla_w_ukv,
             mla_q_norm=mla_q_norm, mla_k_norm=mla_k_norm, mla_w_out=mla_w_out, mlp_norm=mlp_norm, mlp_w1=mlp_w1,
             mlp_w2=mlp_w2, ple_norm=ple_norm, ple_gate_w=ple_gate_w, ple_proj_w=ple_proj_w)
    m = dict(mix_norm=m_mix_norm, ret_w_in=m_ret_w_in, ret_gn=m_ret_gn, ret_w_out=m_ret_w_out, mla_w_in=m_mla_w_in,
             mla_q_a_norm=m_mla_q_a_norm, mla_kv_a_norm=m_mla_kv_a_norm, mla_w_uq=m_mla_w_uq, mla_w_ukv=m_mla_w_ukv,
             mla_q_norm=m_mla_q_norm, mla_k_norm=m_mla_k_norm, mla_w_out=m_mla_w_out, mlp_norm=m_mlp_norm,
             mlp_w1=m_mlp_w1, mlp_w2=m_mlp_w2, ple_norm=m_ple_norm, ple_gate_w=m_ple_gate_w, ple_proj_w=m_ple_proj_w)
    v = dict(mix_norm=v_mix_norm, ret_w_in=v_ret_w_in, ret_gn=v_ret_gn, ret_w_out=v_ret_w_out, mla_w_in=v_mla_w_in,
             mla_q_a_norm=v_mla_q_a_norm, mla_kv_a_norm=v_mla_kv_a_norm, mla_w_uq=v_mla_w_uq, mla_w_ukv=v_mla_w_ukv,
             mla_q_norm=v_mla_q_norm, mla_k_norm=v_mla_k_norm, mla_w_out=v_mla_w_out, mlp_norm=v_mlp_norm,
             mlp_w1=v_mlp_w1, mlp_w2=v_mlp_w2, ple_norm=v_ple_norm, ple_gate_w=v_ple_gate_w, ple_proj_w=v_ple_proj_w)
    xi, yi, ci = _place()
    chip = 2 * xi + yi
    n = N_CHIPS

    parts = _travel_parts(w)
    first = ("gains", "ret_w_in", "ret_w_out")
    mid = [k + "_0" for k in _TWO_LAYER]
    last = [k for k in parts if k not in first and k not in mid]
    full = dict(zip(first, _gather_weights([parts[k] for k in first], "gather_first")))

    def gather_behind(names, tag, after):
        copies = _gather_copies([parts[k].shape[0] for k in names])
        started = _split_start(f"gather_{tag}_start", [parts[k] for k in names],
                               [jax.ShapeDtypeStruct((n, *parts[k].shape), BF16) for k in names], 3 * len(names),
                               copies, after=after)

        def arrive(after):
            landed = _split_wait(f"gather_{tag}_wait", *started[:4], copies, after=after)
            full.update(zip(names, _gather_weights([parts[k] for k in names], f"gather_{tag}_finish", landed=landed)))
            W.update(_full_weights(full))
        return started[4], arrive

    mid_token, mid_arrive = gather_behind(mid, "mid", [full["ret_w_in"]])
    g_token, last_arrive = gather_behind(last, "last", [mid_token])
    gains = full["gains"]
    W = dict(mix_norm=mix_norm, mlp_norm=mlp_norm, ple_norm=ple_norm,
             mla_q_norm=jnp.pad(mla_q_norm, ((0, 0), (0, MLA_HP - MLA_QKD))),
             mla_k_norm=jnp.pad(mla_k_norm, ((0, 0), (0, MLA_HP - MLA_QKD))),
             ret_w_in=full["ret_w_in"], ret_w_out=full["ret_w_out"].reshape(-1, D_MODEL),
             ret_gn=gains[:, 0, :RET_HEADS * 128].reshape(n, RET_HEADS, 128).transpose(1, 0, 2).reshape(RET_HEADS, RET_DV),
             mla_q_a_norm=gains[:, 1, :MLA_Q_RANK // n].reshape(1, MLA_Q_RANK),
             mla_kv_a_norm=gains[:, 2, :MLA_KV_RANK // n].reshape(1, MLA_KV_RANK))
    x0, p16, target = x[0], p[:, 0].astype(BF16), loss_target[0]
    T = x0.shape[0]
    ret_tabs, mla_tabs = _ret_tables(T), _mla_tables(T)

    h1, hn, s_ret = _ret_layer_fwd(x0, W, ret_tabs, after=[g_token])
    mid_arrive([h1])
    h3, hn, s_tail0 = _tail_fwd(h1, hn, p16, W, 0, "l0", next_gain=W["mix_norm"][1:2])
    last_arrive([h3])
    h4, hn, s_mla = _mla_layer_fwd(h3, hn, W, mla_tabs)
    dy, loss, s_tail1 = _tail_fwd(h4, hn, p16, W, 1, "l1", target=target)

    dh4, dh4_16, g_t1, n_t1 = _tail_bwd(dy, s_tail1, p16, W, 1, "l1")
    dh3, _, g_mla, n_mla = _mla_layer_bwd(dh4, dh4_16, s_mla, W, mla_tabs)
    beg_a = _reduce_begin({**g_mla, **g_t1}, ci, "a")
    a_send, a_recv, a_src, a_land, a_token = _split_start(
        "scatter_a_start", beg_a[3], _got_shapes(beg_a[3]), 3 * len(beg_a[3]), _scatter_copies)
    dh1, dh1_16, g_t0, n_t0 = _tail_bwd(dh3, s_tail0, p16, W, 0, "l0", after=[a_token])
    d_ret_w_out = _d_ret_w_out(dh1_16, s_ret)
    beg_b = _reduce_begin({**g_t0, "ret_w_out": d_ret_w_out}, ci, "b")
    b_send, b_recv, b_src, b_land, b_token = _split_start(
        "scatter_b_start", beg_b[3], _got_shapes(beg_b[3]), 3 * len(beg_b[3]), _scatter_copies)
    stage_c = {}

    def start_c(g_ret):
        beg = _reduce_begin({"ret_w_in": g_ret["ret_w_in"]}, ci, "c")
        stage_c["beg"] = beg
        stage_c["st"] = _split_start("scatter_c_start", beg[3], _got_shapes(beg[3]), 3 * len(beg[3]), _scatter_copies)
        return [stage_c["st"][4]]

    dx, _, n_ret = _ret_layer_bwd(dh1, dh1_16, s_ret, W, ret_tabs, after=[b_token], on_grads=start_c,
                                  d_w_out=d_ret_w_out)
    got_a = _split_wait("scatter_a_wait", a_send, a_recv, a_src, a_land, _scatter_copies, after=[dx])
    got_b = _split_wait("scatter_b_wait", b_send, b_recv, b_src, b_land, _scatter_copies, after=[dx])
    got_c = _split_wait("scatter_c_wait", *stage_c["st"][:4], _scatter_copies, after=[dx])
    red = {**_reduce_end(beg_a, got_a, chip, ci), **_reduce_end(beg_b, got_b, chip, ci),
           **_reduce_end(stage_c["beg"], got_c, chip, ci)}
    red = dict(zip(red, _share_halves(list(red.values()))))
    gs = _small_grads(n_ret, n_t0, n_mla, n_t1)
    small_g = jnp.concatenate([
        gs["mix_norm"], gs["mlp_norm"], gs["ple_norm"], gs["ret_gn"].reshape(2, PACK_W), _pad_row(gs["mla_q_a_norm"]),
        _pad_row(gs["mla_kv_a_norm"]), _pad_row(gs["mla_q_norm"][:, :MLA_QKD]), _pad_row(gs["mla_k_norm"][:, :MLA_QKD]),
        _pad_row(loss[:, :1]), jnp.zeros((3, PACK_W), F32)], axis=0)
    tot = _allsum_small(small_g, "sum_small_grads")
    gn_all = tot[6:8].reshape(RET_HEADS, n, -1)
    g_small = dict(
        mix_norm=tot[0:2], mlp_norm=tot[2:4], ple_norm=tot[4:6],
        ret_gn=lax.dynamic_index_in_dim(gn_all, chip, axis=1, keepdims=False),
        mla_q_a_norm=lax.dynamic_index_in_dim(tot[8, :MLA_Q_RANK].reshape(n, -1), chip, axis=0, keepdims=True),
        mla_kv_a_norm=lax.dynamic_index_in_dim(tot[9, :MLA_KV_RANK].reshape(n, -1), chip, axis=0, keepdims=True),
        mla_q_norm=tot[10:11, :MLA_QKD], mla_k_norm=tot[11:12, :MLA_QKD])
    loss_out = tot[12, 0]

    outs = []
    for k in _ORDER:
        if k in _TWO_LAYER:
            res = None
            for i in (1, 0):
                res = _adamw(w[k], red[f"{k}_{i}"], m[k], v[k], f"adamw_{k}_{i}", layers=2, layer=i, into=res)
        elif k in red:
            res = _adamw(w[k], _shard_grad(k, red[k], w[k].shape), m[k], v[k], f"adamw_{k}")
        else:
            res = _adamw(w[k], g_small[k], m[k], v[k], f"adamw_{k}")
        outs.append(res)
    return (loss_out, dx[None], *[o[0] for o in outs], *[o[1] for o in outs], *[o[2] for o in outs],
            *[o[3] for o in outs])
```

```python
import jax
import jax.numpy as jnp
import numpy as np
from jax import lax
from jax.experimental import pallas as pl
from jax.experimental.pallas import tpu as pltpu

F32 = jnp.float32
BF16 = jnp.bfloat16

EPS = 1e-6
D_MODEL = 1024
CHUNK = 64
ROPE_THETA = 10000.0
RET_HEADS = 4
RET_DK = 256
RET_DV = 512
RET_GROUP = 1
RET_BLOCK = 256
RET_ROWS = 1024
MLA_HEADS = 8
MLA_ROPE = 64
MLA_QKD = 192
MLA_VD = 128
MLA_HP = 256
MLA_Q_RANK = 384
MLA_KV_RANK = 256
MLA_IN = 704
MLA_IN_PAD = 768
N_CHIPS = 4

ADAM_LR = 0.001
ADAM_B1 = 0.9
ADAM_B2 = 0.999
ADAM_EPS = 1e-08
ADAM_WD = 0.01
ADAM_STEP = 10

VMEM_LIMIT = 56 * 1024 * 1024
PACK_W = 1024
NEG = -1e30
LOG2E = 1.4426950408889634
FLASH_T = 512
FLASH_HEADS = 2
MM_SUB_ROWS = 256
SUM_ROWS = 512
ADAM_ROWS = 512


def _cparams(sem=None):
    return pltpu.CompilerParams(dimension_semantics=sem, vmem_limit_bytes=VMEM_LIMIT)


def _pick(dim, pref):
    if dim <= pref:
        return dim
    t = pref
    while dim % t:
        t //= 2
    return t


def _mm(a, b, *, name, ta=False, tb=False, bblk=False, outs=None, extras=(), epilogue=None, dw=None,
        tm=1024, tn=512, after=()):
    if ta:
        K, M = a.shape
    else:
        M, K = a.shape
    if bblk and tb:
        nb, N, Kq = b.shape
        assert nb * Kq == K
    elif bblk:
        nb, Kb, Nq = b.shape
        N = nb * Nq
        assert Kb == K
    else:
        N = b.shape[0] if tb else b.shape[1]
    tn = _pick(Nq if (bblk and not tb) else N, tn)
    if dw is not None and dw[0] == "cols":
        tn = _pick(N // N_CHIPS, tn)
    tm = _pick(M // N_CHIPS if (dw is not None and dw[0] == "rows") else M, tm)
    grid = (M // tm, N // tn)

    a_spec = pl.BlockSpec((K, tm), lambda i, j: (0, i)) if ta else pl.BlockSpec((tm, K), lambda i, j: (i, 0))
    if bblk and tb:
        b_spec = pl.BlockSpec((nb, tn, Kq), lambda i, j: (0, j, 0))
    elif bblk:
        npb = Nq // tn
        b_spec = pl.BlockSpec((None, K, tn), lambda i, j: (j // npb, 0, j % npb))
    elif tb:
        b_spec = pl.BlockSpec((tn, K), lambda i, j: (j, 0))
    else:
        b_spec = pl.BlockSpec((K, tn), lambda i, j: (0, j))
    in_specs = [a_spec, b_spec] + [pl.BlockSpec((tm, tn), lambda i, j: (i, j)) for _ in extras]
    args = [a, b, *extras]
    aliases = {}
    if outs is None:
        outs = [F32]
    if dw is None:
        o_specs = [pl.BlockSpec((tm, tn), lambda i, j: (i, j)) for _ in outs]
        o_shapes = [jax.ShapeDtypeStruct((M, N), dt) for dt in outs]
    else:
        kind, layers, layer, into = dw
        if kind == "cols":
            per = (N // N_CHIPS) // tn
            o_specs = [pl.BlockSpec((None, None, tm, tn), lambda i, j: (j // per, layer, i, j % per))]
            o_shapes = [jax.ShapeDtypeStruct((N_CHIPS, layers, M, N // N_CHIPS), outs[0])]
        else:
            per = (M // N_CHIPS) // tm
            o_specs = [pl.BlockSpec((None, None, tm, tn), lambda i, j: (i // per, layer, i % per, j))]
            o_shapes = [jax.ShapeDtypeStruct((N_CHIPS, layers, M // N_CHIPS, N), outs[0])]
        if into is not None:
            aliases = {len(args): 0}
            in_specs.append(pl.BlockSpec(memory_space=pl.ANY))
            args.append(into)
    for t in after:
        in_specs.append(pl.BlockSpec(memory_space=pl.ANY))
        args.append(t)
    n_e, n_o = len(extras), len(outs)

    sub = _pick(tm, MM_SUB_ROWS)

    def body(a_ref, b_ref, *rest):
        e_refs, o_refs = rest[:n_e], rest[len(rest) - n_o:]
        for r0 in range(0, tm, sub):
            rows = slice(r0, r0 + sub)
            av = (a_ref[:, rows] if ta else a_ref[rows, :]).astype(BF16)
            if bblk and tb:
                acc = _dot_nt(av[:, :Kq], b_ref[0].astype(BF16))
                for s in range(1, nb):
                    acc = acc + _dot_nt(av[:, s * Kq:(s + 1) * Kq], b_ref[s].astype(BF16))
            elif ta:
                acc = _dot_tn(av, b_ref[...].astype(BF16))
            elif tb:
                acc = _dot_nt(av, b_ref[...].astype(BF16))
            else:
                acc = _dot(av, b_ref[...].astype(BF16))
            vals = (acc,) if epilogue is None else epilogue(acc, *[e[rows, :] for e in e_refs])
            for o, v in zip(o_refs, vals):
                o[rows, :] = v.astype(o.dtype)

    res = pl.pallas_call(
        body, name=name, grid=grid, in_specs=in_specs, out_specs=o_specs, out_shape=o_shapes,
        input_output_aliases=aliases, compiler_params=_cparams(("parallel", "arbitrary")),
    )(*args)
    return res[0] if n_o == 1 else res


def _mm_rows(a, b, *, name, epilogue, outs, tb=False, bblk=False, extras=(), fulls=(), accs=(), tm=512, after=()):
    M, K = a.shape
    tm = _pick(M, tm)
    sub = _pick(tm, MM_SUB_ROWS)
    nb = b.shape[0] if bblk else 1
    n_e, n_f, n_o, n_a = len(extras), len(fulls), len(outs), len(accs)
    n_in = 2 + n_e + n_f + len(after)

    def whole(t):
        return pl.BlockSpec(t.shape, lambda i, nd=t.ndim: (0,) * nd)

    in_specs = [pl.BlockSpec((tm, K), lambda i: (i, 0)), whole(b)]
    in_specs += [pl.BlockSpec((tm, e.shape[1]), lambda i: (i, 0)) for e in extras] + [whole(f) for f in fulls]
    in_specs += [pl.BlockSpec(memory_space=pl.ANY) for _ in after]
    out_specs = [pl.BlockSpec((tm, w), lambda i: (i, 0)) for w, _ in outs] + [pl.BlockSpec(s, lambda i: (0, 0)) for s, _ in accs]
    out_shape = [jax.ShapeDtypeStruct((M, w), dt) for w, dt in outs] + [jax.ShapeDtypeStruct(s, dt) for s, dt in accs]

    def body(a_ref, b_ref, *rest):
        e_refs, f_refs = rest[:n_e], rest[n_e:n_e + n_f]
        o_refs, acc_refs = rest[n_in - 2:n_in - 2 + n_o], rest[n_in - 2 + n_o:]
        fv = [f[...] for f in f_refs]
        totals = None
        for r0 in range(0, tm, sub):
            rows = slice(r0, r0 + sub)
            av = a_ref[rows, :].astype(BF16)
            if bblk and tb:
                kq = K // nb
                acc = _dot_nt(av[:, :kq], b_ref[0])
                for s in range(1, nb):
                    acc = acc + _dot_nt(av[:, s * kq:(s + 1) * kq], b_ref[s])
            elif bblk:
                acc = jnp.concatenate([_dot(av, b_ref[s]) for s in range(nb)], axis=-1)
            elif tb:
                acc = _dot_nt(av, b_ref[...])
            else:
                acc = _dot(av, b_ref[...])
            vals = epilogue(acc, *[e[rows, :] for e in e_refs], *fv)
            for o, v in zip(o_refs, vals[:n_o]):
                o[rows, :] = v.astype(o.dtype)
            part = vals[n_o:]
            totals = part if totals is None else [t + p for t, p in zip(totals, part)]
        first_step = pl.program_id(0) == 0
        for o, v in zip(acc_refs, totals):
            @pl.when(first_step)
            def _(o=o, v=v):
                o[...] = v.astype(o.dtype)

            @pl.when(jnp.logical_not(first_step))
            def _(o=o, v=v):
                o[...] += v.astype(o.dtype)

    return pl.pallas_call(
        body, name=name, grid=(M // tm,), in_specs=in_specs, out_specs=out_specs, out_shape=out_shape,
        compiler_params=_cparams(("arbitrary",)),
    )(a, b, *extras, *fulls, *after)


def _rows(fn, rows, fulls, outs, accs=(), *, name, tile=512, after=()):
    first = rows[0][0] if isinstance(rows[0], tuple) else rows[0]
    T = first.shape[0]
    tile = _pick(T, tile)
    in_specs, args = [], []
    for r in rows:
        if isinstance(r, tuple):
            arr, w, cb = r
            in_specs.append(pl.BlockSpec((tile, w), lambda i, cb=cb: (i, cb)))
        else:
            arr = r
            in_specs.append(pl.BlockSpec((tile, arr.shape[1]), lambda i: (i, 0)))
        args.append(arr)
    for f in fulls:
        in_specs.append(pl.BlockSpec(f.shape, lambda i, nd=f.ndim: (0,) * nd))
        args.append(f)
    outs = [o if len(o) == 4 else (*o, o[0], 0) for o in outs]
    out_specs = [pl.BlockSpec((tile, w), lambda i, cb=cb: (i, cb)) for w, _, _, cb in outs]
    out_specs += [pl.BlockSpec(s, lambda i: (0, 0)) for s, _ in accs]
    out_shape = [jax.ShapeDtypeStruct((T, tw), dt) for _, dt, tw, _ in outs]
    out_shape += [jax.ShapeDtypeStruct(s, dt) for s, dt in accs]
    n_in, n_out = len(args), len(outs)
    for t in after:
        in_specs.append(pl.BlockSpec(memory_space=pl.ANY))
        args.append(t)

    def body(*refs):
        vals = fn(*[r[...] for r in refs[:n_in]])
        o_refs = refs[len(args):]
        for o, v in zip(o_refs[:n_out], vals[:n_out]):
            o[...] = v.astype(o.dtype)
        first_step = pl.program_id(0) == 0
        for o, v in zip(o_refs[n_out:], vals[n_out:]):
            @pl.when(first_step)
            def _(o=o, v=v):
                o[...] = v.astype(o.dtype)

            @pl.when(jnp.logical_not(first_step))
            def _(o=o, v=v):
                o[...] += v.astype(o.dtype)

    res = pl.pallas_call(
        body, name=name, grid=(T // tile,), in_specs=in_specs, out_specs=out_specs, out_shape=out_shape,
        compiler_params=_cparams(("arbitrary",)),
    )(*args)
    return res


def _rowsum(v, mxu):
    if not mxu:
        return jnp.sum(v, axis=-1, keepdims=True)
    ones = jnp.ones((v.shape[1], v.shape[1]), BF16)
    hi = v.astype(BF16)
    lo = (v - hi.astype(F32)).astype(BF16)
    return _dot(hi, ones) + _dot(lo, ones)


def _rms(x, g, mxu=False):
    r = lax.rsqrt(_rowsum(x * x, mxu) / x.shape[-1] + EPS)
    return (x * r) * g


def _rms_bwd(x, dy, g, n=None, mxu=False):
    n = x.shape[-1] if n is None else n
    r = lax.rsqrt(_rowsum(x * x, mxu) / n + EPS)
    xh = x * r
    dxh = dy * g
    dx = r * (dxh - xh * (_rowsum(dxh * xh, mxu) / n))
    return dx, dy * xh


def _colsum(v):
    return jnp.sum(v, axis=0, keepdims=True)


def _sigmoid(x):
    return 1.0 / (1.0 + jnp.exp(-x))


def _widen(v, width):
    reps = width // v.shape[1]
    return v if reps == 1 else jnp.concatenate([v] * reps, axis=-1)


def _rope_angles(T, dim):
    inv = (1.0 / (np.float32(ROPE_THETA) ** (np.arange(0, dim, 2, dtype=np.float32) / np.float32(dim)))).astype(np.float32)
    return np.arange(T, dtype=np.float32)[:, None] * inv[None, :]


def _ret_tables(T):
    ang = _rope_angles(T, RET_DK)
    log_gamma = np.log(np.float32(1.0) - np.float32(2.0) ** (-5.0 - np.arange(RET_HEADS, dtype=np.float32)))
    idx = np.arange(RET_BLOCK, dtype=np.float32)
    chunk = np.arange(RET_BLOCK) // CHUNK
    dist = idx[:, None] - idx[None, :]
    seen = np.where(chunk[:, None] == chunk[None, :], np.abs(dist), np.where(chunk[:, None] > chunk[None, :], dist, np.inf))
    intra = np.exp(log_gamma[:, None, None] * seen[None].astype(np.float32))
    qd = np.exp(log_gamma[:, None] * (idx + 1.0))[:, :, None]
    kd = np.exp(log_gamma[:, None] * (RET_BLOCK - 1.0 - idx))[:, :, None]
    cd = np.exp(log_gamma * RET_BLOCK)[:, None, None]
    return tuple(jnp.asarray(t, F32) for t in (np.cos(ang), np.sin(ang), intra, qd, kd, cd))


def _rope_half(x, c, s):
    x1, x2 = x[:, :RET_DK // 2], x[:, RET_DK // 2:]
    return jnp.concatenate([x1 * c - x2 * s, x2 * c + x1 * s], axis=-1)


def _rope_half_bwd(d, c, s):
    d1, d2 = d[:, :RET_DK // 2], d[:, RET_DK // 2:]
    return jnp.concatenate([d1 * c + d2 * s, d2 * c - d1 * s], axis=-1)


def _dot(a, b):
    return lax.dot_general(a, b, (((1,), (0,)), ((), ())), preferred_element_type=F32)


def _dot_nt(a, b):
    return lax.dot_general(a, b, (((1,), (1,)), ((), ())), preferred_element_type=F32)


def _dot_tn(a, b):
    return lax.dot_general(a, b, (((0,), (0,)), ((), ())), preferred_element_type=F32)


def _ret_specs(T, tb, rev):
    nj = T // tb
    jj = (lambda j: nj - 1 - j) if rev else (lambda j: j)
    g = RET_GROUP
    kq = RET_HEADS // g
    vq = 2 * RET_HEADS * RET_DK // (g * RET_DV)
    return dict(
        q=pl.BlockSpec((tb, g * RET_DK), lambda h, j: (jj(j), h)),
        k=pl.BlockSpec((tb, g * RET_DK), lambda h, j: (jj(j), kq + h)),
        v=pl.BlockSpec((tb, g * RET_DV), lambda h, j: (jj(j), vq + h)),
        tab=pl.BlockSpec((tb, RET_DK // 2), lambda h, j: (jj(j), 0)),
        intra=pl.BlockSpec((g, RET_BLOCK, RET_BLOCK), lambda h, j: (h, 0, 0)),
        dec=pl.BlockSpec((g, RET_BLOCK, 1), lambda h, j: (h, 0, 0)),
        cd=pl.BlockSpec((g, 1, 1), lambda h, j: (h, 0, 0)),
        o=pl.BlockSpec((tb, g * RET_DV), lambda h, j: (jj(j), h)),
        s=pl.BlockSpec((g, tb // RET_BLOCK, RET_DK, RET_DV), lambda h, j: (h, jj(j), 0, 0)),
    )


def _ret_fwd(proj, tabs, name):
    T = proj.shape[0]
    cos, sin, intra, qd, kd, cd = tabs
    tb = _pick(T, RET_ROWS)
    cps = tb // RET_BLOCK
    sp = _ret_specs(T, tb, False)
    scale = RET_DK ** -0.5

    def body(q_ref, k_ref, v_ref, cos_ref, sin_ref, intra_ref, qd_ref, kd_ref, cd_ref, o_ref, s_ref, state):
        @pl.when(pl.program_id(1) == 0)
        def _():
            state[...] = jnp.zeros_like(state)

        for c in range(cps):
            rows = pl.ds(c * RET_BLOCK, RET_BLOCK)
            co, si = cos_ref[rows, :], sin_ref[rows, :]
            for h in range(RET_GROUP):
                hk, hv = slice(h * RET_DK, (h + 1) * RET_DK), slice(h * RET_DV, (h + 1) * RET_DV)
                q = _rope_half(q_ref[rows, hk].astype(F32), co, si)
                k = _rope_half(k_ref[rows, hk].astype(F32), co, si) * scale
                vb = v_ref[rows, hv].astype(BF16)
                st = state[h]
                sb = st.astype(BF16)
                s_ref[h, c] = sb
                sc = _dot_nt(q.astype(BF16), k.astype(BF16)) * intra_ref[h]
                inner = _dot(sc.astype(BF16), vb)
                cross = _dot((q * qd_ref[h]).astype(BF16), sb)
                o_ref[rows, hv] = inner + cross
                state[h] = st * cd_ref[h] + _dot_tn((k * kd_ref[h]).astype(BF16), vb)

    return pl.pallas_call(
        body, name=name, grid=(RET_HEADS // RET_GROUP, T // tb),
        in_specs=[sp["q"], sp["k"], sp["v"], sp["tab"], sp["tab"], sp["intra"], sp["dec"], sp["dec"], sp["cd"]],
        out_specs=[sp["o"], sp["s"]],
        out_shape=[jax.ShapeDtypeStruct((T, RET_HEADS * RET_DV), F32),
                   jax.ShapeDtypeStruct((RET_HEADS, T // RET_BLOCK, RET_DK, RET_DV), BF16)],
        scratch_shapes=[pltpu.VMEM((RET_GROUP, RET_DK, RET_DV), F32)],
        compiler_params=_cparams(("arbitrary", "arbitrary")),
    )(proj, proj, proj, cos, sin, intra, qd, kd, cd)


def _ret_bwd(proj, states, dout, dproj, tabs, name):
    assert RET_GROUP == 1
    T = proj.shape[0]
    cos, sin, intra, qd, kd, cd = tabs
    tb = _pick(T, RET_ROWS)
    cps = tb // RET_BLOCK
    nj = T // tb
    sp = _ret_specs(T, tb, True)
    scale = RET_DK ** -0.5
    k0, v0 = RET_HEADS * RET_DK, 2 * RET_HEADS * RET_DK

    def body(q_ref, k_ref, v_ref, cos_ref, sin_ref, intra_ref, qd_ref, kd_ref, cd_ref, s_ref, do_ref, _dproj_in,
             out_ref, dq_s, dk_s, dv_s, sems, dstate):
        head, j = pl.program_id(0), pl.program_id(1)
        step = head * nj + j
        slot = step % 2
        dq_ref, dk_ref, dv_ref = dq_s.at[slot], dk_s.at[slot], dv_s.at[slot]

        @pl.when(j == 0)
        def _():
            dstate[...] = jnp.zeros_like(dstate)

        for c in reversed(range(cps)):
            rows = pl.ds(c * RET_BLOCK, RET_BLOCK)
            co, si = cos_ref[rows, :], sin_ref[rows, :]
            for h in range(RET_GROUP):
                hk, hv = slice(h * RET_DK, (h + 1) * RET_DK), slice(h * RET_DV, (h + 1) * RET_DV)
                q = _rope_half(q_ref[rows, hk].astype(F32), co, si)
                k = _rope_half(k_ref[rows, hk].astype(F32), co, si) * scale
                qb, kb = q.astype(BF16), k.astype(BF16)
                vb = v_ref[rows, hv].astype(BF16)
                dob = do_ref[rows, hv].astype(BF16)
                sb = s_ref[h, c]
                ia = intra_ref[h]
                pb = (_dot_nt(qb, kb) * ia).astype(BF16)
                dsn = dstate[h]
                dsb = dsn.astype(BF16)
                kdk = (k * kd_ref[h]).astype(BF16)
                qdq = (q * qd_ref[h]).astype(BF16)
                dv = _dot_tn(pb, dob) + _dot(kdk, dsb)
                dpb = (_dot_nt(dob, vb) * ia).astype(BF16)
                dq = _dot(dpb, kb) + _dot_nt(dob, sb) * qd_ref[h]
                dk = _dot_tn(dpb, qb) + _dot_nt(vb, dsb) * kd_ref[h]
                dstate[h] = dsn * cd_ref[h] + _dot_tn(qdq, dob)
                dq_ref[rows, hk] = _rope_half_bwd(dq, co, si).astype(BF16)
                dk_ref[rows, hk] = _rope_half_bwd(dk * scale, co, si).astype(BF16)
                dv_ref[rows, hv] = dv.astype(BF16)

        def copies(sl):
            r = pl.ds(pl.multiple_of((nj - 1 - j) * tb, tb), tb)
            cols = lambda first, w: pl.ds(pl.multiple_of(first + head * w, 128), w)
            return [pltpu.make_async_copy(dq_s.at[sl], out_ref.at[r, cols(0, RET_DK)], sems.at[sl, 0]),
                    pltpu.make_async_copy(dk_s.at[sl], out_ref.at[r, cols(k0, RET_DK)], sems.at[sl, 1]),
                    pltpu.make_async_copy(dv_s.at[sl], out_ref.at[r, cols(v0, RET_DV)], sems.at[sl, 2])]

        @pl.when(step > 0)
        def _():
            for cp in copies(1 - slot):
                cp.wait()

        for cp in copies(slot):
            cp.start()

        @pl.when(step == RET_HEADS * nj - 1)
        def _():
            for cp in copies(slot):
                cp.wait()

    return pl.pallas_call(
        body, name=name, grid=(RET_HEADS, nj),
        in_specs=[sp["q"], sp["k"], sp["v"], sp["tab"], sp["tab"], sp["intra"], sp["dec"], sp["dec"], sp["cd"],
                  sp["s"], sp["o"], pl.BlockSpec(memory_space=pl.ANY)],
        out_specs=pl.BlockSpec(memory_space=pl.ANY), out_shape=jax.ShapeDtypeStruct(dproj.shape, dproj.dtype),
        input_output_aliases={11: 0},
        scratch_shapes=[pltpu.VMEM((2, tb, RET_DK), BF16), pltpu.VMEM((2, tb, RET_DK), BF16),
                        pltpu.VMEM((2, tb, RET_DV), BF16), pltpu.SemaphoreType.DMA((2, 3)),
                        pltpu.VMEM((RET_GROUP, RET_DK, RET_DV), F32)],
        compiler_params=_cparams(("arbitrary", "arbitrary")),
    )(proj, proj, proj, cos, sin, intra, qd, kd, cd, states, dout, dproj)


def _ret_gate(out, proj, gn, name):
    def fn(o, g, *gains):
        g = g.astype(F32)
        parts = [_rms(o[:, h * RET_DV:(h + 1) * RET_DV], gains[h]) for h in range(RET_HEADS)]
        return (g * _sigmoid(g) * jnp.concatenate(parts, axis=-1),)
    w = RET_HEADS * RET_DV
    return _rows(fn, [out, (proj, w, 2)], [gn[h:h + 1] for h in range(RET_HEADS)], [(w, BF16)], name=name)[0]


def _ret_gate_bwd(out, proj, gn, dy, name):
    def fn(o, g, d, *gains):
        g = g.astype(F32)
        sg = _sigmoid(g)
        silu = g * sg
        dsilu = sg * (1.0 + g * (1.0 - sg))
        dos, dgs = [], []
        row = lax.broadcasted_iota(jnp.int32, (RET_HEADS, RET_DV), 0)
        dgn = jnp.zeros((RET_HEADS, RET_DV), F32)
        for h in range(RET_HEADS):
            sl = slice(h * RET_DV, (h + 1) * RET_DV)
            oh = o[:, sl]
            dgs.append(d[:, sl] * _rms(oh, gains[h]) * dsilu[:, sl])
            dx, dg = _rms_bwd(oh, d[:, sl] * silu[:, sl], gains[h])
            dos.append(dx)
            dgn = dgn + jnp.where(row == h, _colsum(dg), 0.0)
        return jnp.concatenate(dos, axis=-1), jnp.concatenate(dgs, axis=-1), dgn
    w = RET_HEADS * RET_DV
    return _rows(fn, [out, (proj, w, 2), dy], [gn[h:h + 1] for h in range(RET_HEADS)],
                 [(w, BF16), (w, BF16, proj.shape[1], 2)], [((RET_HEADS, RET_DV), F32)], name=name, tile=256)


def _mla_tables(T):
    ang = _rope_angles(T, MLA_ROPE)
    c, s = np.cos(ang), np.sin(ang)
    z32, z64 = np.zeros((T, 32), np.float32), np.zeros((T, 64), np.float32)
    cos_t = np.concatenate([c, c, z64], axis=1)
    sin_a = np.concatenate([-s, z32, z64], axis=1)
    sin_b = np.concatenate([z32, s, z64], axis=1)
    return tuple(jnp.asarray(t, F32) for t in (cos_t, sin_a, sin_b))


def _rope_blk(x, ct, sa, sb):
    return x * ct + pltpu.roll(x, 96, 1) * sa + pltpu.roll(x, 32, 1) * sb


def _rope_blk_bwd(d, ct, sa, sb):
    return d * ct + pltpu.roll(d * sa, 32, 1) + pltpu.roll(d * sb, 96, 1)


def _head_norm(x, gain):
    r = lax.rsqrt(_rowsum(x * x, True) / MLA_QKD + EPS)
    return (x * r) * gain


def _prep_heads(qv, kvv, kr, ct, sa, sb, gqv, gkv):
    qs, ks, vs = [], [], []
    for h in range(MLA_HEADS):
        b = h * MLA_HP
        y = _head_norm(qv[:, b:b + MLA_HP], gqv)
        qs += [y[:, :128], _rope_blk(y[:, 128:], ct, sa, sb)]
        y = _head_norm(jnp.concatenate([kvv[:, b:b + 128], kr], axis=-1), gkv)
        ks += [y[:, :128], _rope_blk(y[:, 128:], ct, sa, sb)]
        vs.append(kvv[:, b + 128:b + 256])
    return jnp.concatenate(qs, axis=-1), jnp.concatenate(ks, axis=-1), jnp.concatenate(vs, axis=-1)


def _mla_front(hn, W, tabs, name):
    wide = MLA_HEADS * MLA_HP
    gq = W["mla_q_norm"] * (MLA_QKD ** -0.5 * LOG2E)

    def epilogue(acc, ct, sa, sb, gqa, gkva, wuq, wukv, gqv, gkv):
        cqn = _rms(acc[:, :MLA_Q_RANK], gqa).astype(BF16)
        ckvn = _rms(acc[:, MLA_Q_RANK:MLA_Q_RANK + MLA_KV_RANK], gkva).astype(BF16)
        q = jnp.concatenate([_dot(cqn, wuq[s]) for s in range(N_CHIPS)], axis=-1).astype(BF16)
        kv = jnp.concatenate([_dot(ckvn, wukv[s]) for s in range(N_CHIPS)], axis=-1).astype(BF16)
        qf, kf, vf = _prep_heads(q.astype(F32), kv.astype(F32), acc[:, MLA_IN_PAD - 128:], ct, sa, sb, gqv, gkv)
        return acc, cqn, ckvn, q, kv, qf, kf, vf

    return _mm_rows(hn, W["mla_w_in"], extras=list(tabs),
                    fulls=[W["mla_q_a_norm"], W["mla_kv_a_norm"], W["mla_w_uq"], W["mla_w_ukv"], gq, W["mla_k_norm"]],
                    outs=[(MLA_IN_PAD, F32), (MLA_Q_RANK, BF16), (MLA_KV_RANK, BF16), (wide, BF16), (wide, BF16),
                          (wide, BF16), (wide, BF16), (MLA_HEADS * MLA_VD, BF16)],
                    epilogue=epilogue, name=name, tm=256)


def _prep_heads_bwd(qv, kvv, kr, ct, sa, sb, dqv, dkv, dvv, gqv, gkv):
    dqs, dkvs = [], []
    dkr = jnp.zeros_like(kr)
    dgq = jnp.zeros((1, MLA_HP), F32)
    dgk = jnp.zeros((1, MLA_HP), F32)
    for h in range(MLA_HEADS):
        b = h * MLA_HP
        dy = jnp.concatenate([dqv[:, b:b + 128], _rope_blk_bwd(dqv[:, b + 128:b + 256], ct, sa, sb)], axis=-1)
        dx, dg = _rms_bwd(qv[:, b:b + MLA_HP], dy, gqv, MLA_QKD, mxu=True)
        dqs.append(dx)
        dgq = dgq + _colsum(dg)
        dy = jnp.concatenate([dkv[:, b:b + 128], _rope_blk_bwd(dkv[:, b + 128:b + 256], ct, sa, sb)], axis=-1)
        dx, dg = _rms_bwd(jnp.concatenate([kvv[:, b:b + 128], kr], axis=-1), dy, gkv, MLA_QKD, mxu=True)
        dkvs += [dx[:, :128], dvv[:, h * MLA_VD:(h + 1) * MLA_VD].astype(F32)]
        dkr = dkr + dx[:, 128:]
        dgk = dgk + _colsum(dg)
    return jnp.concatenate(dqs, axis=-1), jnp.concatenate(dkvs, axis=-1), dkr, dgq, dgk


def _mla_back(q, kv, proj, h0, dh1, dqf, dkf, dvf, W, tabs, name):
    def fn(qv, kvv, pv, hv, dr, ct, sa, sb, dqv, dkv, dvv, gqv, gkv, gqa, gkva, wuq, wukv, w_in, g_mix):
        qv, kvv, dqv, dkv = (t.astype(F32) for t in (qv, kvv, dqv, dkv))
        dq, dkvx, dkr, dgq, dgk = _prep_heads_bwd(qv, kvv, pv[:, MLA_IN_PAD - 128:], ct, sa, sb, dqv, dkv, dvv, gqv, gkv)
        dq, dkvx = dq.astype(BF16), dkvx.astype(BF16)
        nq = wuq.shape[2]
        dcq = sum(_dot_nt(dq[:, s * nq:(s + 1) * nq], wuq[s]) for s in range(N_CHIPS))
        dckv = sum(_dot_nt(dkvx[:, s * nq:(s + 1) * nq], wukv[s]) for s in range(N_CHIPS))
        dxq, dgqa = _rms_bwd(pv[:, :MLA_Q_RANK], dcq, gqa)
        dxkv, dgkva = _rms_bwd(pv[:, MLA_Q_RANK:MLA_Q_RANK + MLA_KV_RANK], dckv, gkva)
        dproj = jnp.concatenate([dxq, dxkv, dkr], axis=-1).astype(BF16)
        dx, dgm = _rms_bwd(hv, _dot_nt(dproj, w_in), g_mix)
        return (dq, dkvx, dproj, dr + dx, dr + dx, dgq, dgk, _colsum(dgqa), _colsum(dgkva), _colsum(dgm))

    wide = MLA_HEADS * MLA_HP
    return _rows(fn, [q, kv, proj, h0, dh1, *tabs, dqf, dkf, dvf],
                 [W["mla_q_norm"], W["mla_k_norm"], W["mla_q_a_norm"], W["mla_kv_a_norm"], W["mla_w_uq"], W["mla_w_ukv"],
                  W["mla_w_in"], W["mix_norm"][1:2]],
                 [(wide, BF16), (wide, BF16), (MLA_IN_PAD, BF16), ROW_F32, ROW_BF16],
                 [((1, MLA_HP), F32), ((1, MLA_HP), F32), ((1, MLA_Q_RANK), F32), ((1, MLA_KV_RANK), F32),
                  ((1, D_MODEL), F32)], name=name, tile=256)


def _chunk_mask(qi, ki, tq, tk):
    shift = CHUNK.bit_length() - 1
    rq = lax.shift_right_arithmetic(qi * tq + lax.broadcasted_iota(jnp.int32, (tq, tk), 0), shift)
    ck = lax.shift_right_arithmetic(ki * tk + lax.broadcasted_iota(jnp.int32, (tq, tk), 1), shift)
    return ck <= rq


def _flash_fwd(qf, kf, vf, name):
    T = qf.shape[0]
    t = _pick(T, FLASH_T)
    n = T // t
    g = FLASH_HEADS

    def body(q_ref, k_ref, v_ref, o_ref, lse_ref, m_s, l_s, acc):
        qi = pl.program_id(1)
        m_s[...] = jnp.full_like(m_s, NEG)
        l_s[...] = jnp.zeros_like(l_s)
        acc[...] = jnp.zeros_like(acc)

        def step(kb, masked):
            rows = pl.ds(pl.multiple_of(kb * t, t), t)
            for h in range(g):
                hq, hv = slice(h * MLA_HP, (h + 1) * MLA_HP), slice(h * MLA_VD, (h + 1) * MLA_VD)
                s = _dot_nt(q_ref[:, hq], k_ref[rows, hq])
                if masked:
                    s = jnp.where(_chunk_mask(0, 0, t, t), s, NEG)
                m_prev = m_s[:, hv]
                m_new = jnp.maximum(m_prev, jnp.max(s, axis=-1, keepdims=True))
                alpha = jnp.exp2(m_prev - m_new)
                p = jnp.exp2(s - _widen(m_new, t))
                l_s[:, hv] = alpha * l_s[:, hv] + sum(p[:, i * 128:(i + 1) * 128] for i in range(t // 128))
                acc[:, hv] = acc[:, hv] * alpha + _dot(p.astype(BF16), v_ref[rows, hv])
                m_s[:, hv] = m_new

        @pl.loop(0, qi)
        def _(kb):
            step(kb, False)

        step(qi, True)
        for h in range(g):
            hv = slice(h * MLA_VD, (h + 1) * MLA_VD)
            l = jnp.sum(l_s[:, hv], axis=-1, keepdims=True)
            o_ref[:, hv] = acc[:, hv] / l
            lse_ref[:, hv] = m_s[:, hv] + jnp.log2(l)

    qmap = lambda h, i: (i, h)
    kmap = lambda h, i: (0, h)
    vec = pltpu.VMEM((t, g * MLA_VD), F32)
    return pl.pallas_call(
        body, name=name, grid=(MLA_HEADS // g, n),
        in_specs=[pl.BlockSpec((t, g * MLA_HP), qmap), pl.BlockSpec((T, g * MLA_HP), kmap),
                  pl.BlockSpec((T, g * MLA_VD), kmap)],
        out_specs=[pl.BlockSpec((t, g * MLA_VD), qmap), pl.BlockSpec((t, g * MLA_VD), qmap)],
        out_shape=[jax.ShapeDtypeStruct((T, MLA_HEADS * MLA_VD), F32),
                   jax.ShapeDtypeStruct((T, MLA_HEADS * MLA_VD), F32)],
        scratch_shapes=[vec, vec, vec],
        compiler_params=_cparams(("parallel", "arbitrary")),
    )(qf, kf, vf)


def _flash_bwd(qf, kf, vf, do16, lse, delta, name):
    T = qf.shape[0]
    t = _pick(T, FLASH_T)
    n = T // t
    scale = MLA_QKD ** -0.5

    def body(q_ref, k_ref, v_ref, do_ref, lse_ref, dl_ref, dq_out, dk_out, dv_out, dq_ref, dk_ref, dv_ref):
        kb = pl.program_id(1)

        @pl.when(kb == 0)
        def _():
            dq_ref[...] = jnp.zeros_like(dq_ref)

        dk_ref[...] = jnp.zeros_like(dk_ref)
        dv_ref[...] = jnp.zeros_like(dv_ref)
        k, v = k_ref[...], v_ref[...]

        def step(qb, masked):
            rows = pl.ds(pl.multiple_of(qb * t, t), t)
            q, dob = q_ref[rows, :], do_ref[rows, :]
            s = _dot_nt(q, k)
            if masked:
                s = jnp.where(_chunk_mask(0, 0, t, t), s, NEG)
            p = jnp.exp2(s - _widen(lse_ref[rows, :], t))
            ds = (p * (_dot_nt(dob, v) - _widen(dl_ref[rows, :], t))).astype(BF16)
            dv_ref[...] += _dot_tn(p.astype(BF16), dob)
            dk_ref[...] += _dot_tn(ds, q)
            dq_ref[rows, :] += _dot(ds, k)

        step(kb, True)

        @pl.loop(kb + 1, n)
        def _(qb):
            step(qb, False)

        dk_out[...] = (dk_ref[...] * (1.0 / LOG2E)).astype(BF16)
        dv_out[...] = dv_ref[...].astype(BF16)

        @pl.when(kb == n - 1)
        def _():
            dq_out[...] = (dq_ref[...] * scale).astype(BF16)

    qmap = lambda h, j: (0, h)
    kmap = lambda h, j: (j, h)
    return pl.pallas_call(
        body, name=name, grid=(MLA_HEADS, n),
        in_specs=[pl.BlockSpec((T, MLA_HP), qmap), pl.BlockSpec((t, MLA_HP), kmap), pl.BlockSpec((t, MLA_VD), kmap),
                  pl.BlockSpec((T, MLA_VD), qmap), pl.BlockSpec((T, MLA_VD), qmap), pl.BlockSpec((T, MLA_VD), qmap)],
        out_specs=[pl.BlockSpec((T, MLA_HP), qmap), pl.BlockSpec((t, MLA_HP), kmap), pl.BlockSpec((t, MLA_VD), kmap)],
        out_shape=[jax.ShapeDtypeStruct((T, MLA_HEADS * MLA_HP), BF16),
                   jax.ShapeDtypeStruct((T, MLA_HEADS * MLA_HP), BF16),
                   jax.ShapeDtypeStruct((T, MLA_HEADS * MLA_VD), BF16)],
        scratch_shapes=[pltpu.VMEM((T, MLA_HP), F32), pltpu.VMEM((t, MLA_HP), F32), pltpu.VMEM((t, MLA_VD), F32)],
        compiler_params=_cparams(("arbitrary", "arbitrary")),
    )(qf, kf, vf, do16, lse, delta)


MESH = pl.DeviceIdType.MESH
ANY = pl.BlockSpec(memory_space=pl.ANY)
_CHIP_FLIPS = ((1, 0), (0, 1), (1, 1))


def _place():
    return lax.axis_index("x"), lax.axis_index("y"), lax.axis_index("c")


def _other_chip(x, y, k):
    fx, fy = _CHIP_FLIPS[k]
    return ((1 - x) if fx else x), ((1 - y) if fy else y)


def _remote(src, dst, send_sems, recv_sems, k, to):
    return pltpu.make_async_remote_copy(src_ref=src, dst_ref=dst, send_sem=send_sems.at[k], recv_sem=recv_sems.at[k],
                                        device_id=to, device_id_type=MESH)


def _index(*vals):
    return jnp.stack(vals).astype(jnp.int32)


def _half(c, rows):
    return pl.ds(pl.multiple_of(c * rows, 16), rows)


def _gather_weights(parts, name, landed=None):
    n_w = len(parts)
    n_in = n_w if landed is None else 2 * n_w

    def body(*refs):
        ins, outs = refs[:n_w], refs[n_in:n_in + n_w]
        send_sems, recv_sems, local_sems = refs[n_in + n_w:]
        x, y, c = _place()
        j = 2 * x + y
        sibling = (x, y, 1 - c)
        chips = [_other_chip(x, y, k) for k in range(3)]
        pending = []
        for w in range(n_w):
            own = pltpu.make_async_copy(ins[w], outs[w].at[j], local_sems.at[w])
            own.start()
            pending.append(own)
        sent = []
        for w in range(n_w):
            if landed is not None:
                break
            r = _half(c, parts[w].shape[0] // 2)
            for k, (px, py) in enumerate(chips):
                cp = _remote(ins[w].at[r], outs[w].at[j, r], send_sems, recv_sems, 6 * w + k, (px, py, c))
                cp.start()
                sent.append(cp)
        for w in range(n_w):
            r = _half(c, parts[w].shape[0] // 2)
            for k, (px, py) in enumerate(chips):
                blk = outs[w].at[2 * px + py, r]
                if landed is None:
                    _remote(blk, blk, send_sems, recv_sems, 6 * w + k, (px, py, c)).wait_recv()
                cp = _remote(blk, blk, send_sems, recv_sems, 6 * w + 3 + k, sibling)
                cp.start()
                sent.append(cp)
        for w in range(n_w):
            r = _half(1 - c, parts[w].shape[0] // 2)
            for k, (px, py) in enumerate(chips):
                blk = outs[w].at[2 * px + py, r]
                _remote(blk, blk, send_sems, recv_sems, 6 * w + 3 + k, sibling).wait_recv()
        for cp in sent:
            cp.wait_send()
        for cp in pending:
            cp.wait()

    return pl.pallas_call(
        body, name=name, in_specs=[pl.BlockSpec(memory_space=pltpu.VMEM)] * n_w + [ANY] * (n_in - n_w),
        out_specs=[ANY] * n_w,
        out_shape=[jax.ShapeDtypeStruct((N_CHIPS, *p.shape), p.dtype) for p in parts],
        input_output_aliases={} if landed is None else {n_w + w: w for w in range(n_w)},
        scratch_shapes=[pltpu.SemaphoreType.DMA((6 * n_w,)), pltpu.SemaphoreType.DMA((6 * n_w,)),
                        pltpu.SemaphoreType.DMA((n_w,))],
        compiler_params=pltpu.CompilerParams(vmem_limit_bytes=VMEM_LIMIT),
    )(*parts, *(landed or []))


def _swap_halves(gs, name):
    n_w = len(gs)

    def body(*refs):
        g_refs, recv_refs = refs[:n_w], refs[n_w:2 * n_w]
        send_sems, recv_sems = refs[2 * n_w:]
        x, y, c = _place()
        sent = []
        for w in range(n_w):
            for jj in range(N_CHIPS):
                cp = _remote(g_refs[w].at[jj, 1 - c], recv_refs[w].at[jj], send_sems, recv_sems, N_CHIPS * w + jj,
                             (x, y, 1 - c))
                cp.start()
                sent.append(cp)
        for cp in sent:
            cp.wait()

    return pl.pallas_call(
        body, name=name, in_specs=[ANY] * n_w, out_specs=[ANY] * n_w,
        out_shape=[jax.ShapeDtypeStruct((N_CHIPS, *g.shape[2:]), g.dtype) for g in gs],
        scratch_shapes=[pltpu.SemaphoreType.DMA((N_CHIPS * n_w,)), pltpu.SemaphoreType.DMA((N_CHIPS * n_w,))],
    )(*gs)


def _pair_sum(g, recv, core, name):
    _, H, C = recv.shape
    tile = _pick(H, SUM_ROWS)

    def body(c_ref, own_ref, recv_ref, out_ref):
        out_ref[...] = (own_ref[...].astype(F32) + recv_ref[...].astype(F32)).astype(BF16)

    blk = pl.BlockSpec((None, tile, C), lambda jj, i, c: (jj, i, 0))
    return pl.pallas_call(
        body, name=name,
        grid_spec=pltpu.PrefetchScalarGridSpec(
            num_scalar_prefetch=1, grid=(N_CHIPS, H // tile),
            in_specs=[pl.BlockSpec((None, None, tile, C), lambda jj, i, c: (jj, c[0], i, 0)), blk],
            out_specs=blk),
        out_shape=jax.ShapeDtypeStruct((N_CHIPS, H, C), BF16),
        compiler_params=_cparams(("arbitrary", "arbitrary")),
    )(_index(core), g, recv)


def _chip_sum(g, recv, got, chip, core, name):
    _, H, C = recv.shape
    tile = _pick(H, SUM_ROWS)

    def body(s_ref, own_ref, recv_ref, g0_ref, g1_ref, g2_ref, out_ref):
        pair = own_ref[...].astype(F32) + recv_ref[...].astype(F32)
        out_ref[...] = ((pair + g0_ref[...].astype(F32)) + g1_ref[...].astype(F32)) + g2_ref[...].astype(F32)

    def got_spec(k):
        return pl.BlockSpec((None, tile, C), lambda i, s, k=k: (k, i, 0))

    return pl.pallas_call(
        body, name=name,
        grid_spec=pltpu.PrefetchScalarGridSpec(
            num_scalar_prefetch=1, grid=(H // tile,),
            in_specs=[pl.BlockSpec((None, None, tile, C), lambda i, s: (s[0], s[1], i, 0)),
                      pl.BlockSpec((None, tile, C), lambda i, s: (s[0], i, 0)), got_spec(0), got_spec(1), got_spec(2)],
            out_specs=pl.BlockSpec((None, tile, C), lambda i, s: (s[1], i, 0))),
        out_shape=jax.ShapeDtypeStruct((2, H, C), F32),
        compiler_params=_cparams(("arbitrary",)),
    )(_index(chip, core), g, recv, got, got, got)


def _share_halves(reds):
    n_w = len(reds)

    def body(*refs):
        out_refs = refs[n_w:2 * n_w]
        send_sems, recv_sems = refs[2 * n_w:]
        x, y, c = _place()
        sent = []
        for w in range(n_w):
            blk = out_refs[w].at[c]
            cp = _remote(blk, blk, send_sems, recv_sems, w, (x, y, 1 - c))
            cp.start()
            sent.append(cp)
        for cp in sent:
            cp.wait()

    return pl.pallas_call(
        body, name="grad_share_halves", in_specs=[ANY] * n_w, out_specs=[ANY] * n_w,
        out_shape=[jax.ShapeDtypeStruct(r.shape, r.dtype) for r in reds],
        input_output_aliases={w: w for w in range(n_w)},
        scratch_shapes=[pltpu.SemaphoreType.DMA((n_w,)), pltpu.SemaphoreType.DMA((n_w,))],
    )(*reds)


def _allsum_small(v, name):
    R, W = v.shape
    n_dev = 8
    vm = pl.BlockSpec(memory_space=pltpu.VMEM)

    def body(v_ref, out_ref, buf, send_sems, recv_sems):
        x, y, c = _place()
        me = 4 * x + 2 * y + c
        buf[me] = v_ref[...]
        sent = []
        for k in range(1, n_dev):
            peer = ((1 - x) if k & 4 else x, (1 - y) if k & 2 else y, (1 - c) if k & 1 else c)
            cp = _remote(v_ref, buf.at[me], send_sems, recv_sems, k - 1, peer)
            cp.start()
            sent.append(cp)
        for cp in sent:
            cp.wait_recv()
        for cp in sent:
            cp.wait_send()
        acc = buf[0]
        for q in range(1, n_dev):
            acc = acc + buf[q]
        out_ref[...] = acc

    return pl.pallas_call(
        body, name=name, in_specs=[vm], out_specs=vm, out_shape=jax.ShapeDtypeStruct((R, W), v.dtype),
        scratch_shapes=[pltpu.VMEM((n_dev, R, W), v.dtype), pltpu.SemaphoreType.DMA((n_dev - 1,)),
                        pltpu.SemaphoreType.DMA((n_dev - 1,))],
    )(v)


HBM = pl.BlockSpec(memory_space=pltpu.HBM)
SEM = pl.BlockSpec(memory_space=pltpu.SEMAPHORE)
_DATAFLOW = pltpu.SideEffectType.DATAFLOW_SIDE_EFFECTING


def _split_start(name, srcs, land_shapes, n_copies, copies, after=()):
    ns, nl = len(srcs), len(land_shapes)
    lands = [lax.empty(s.shape, s.dtype) for s in land_shapes]

    def body(*refs):
        outs = refs[ns + nl + len(after):]
        for cp in copies(refs[:ns], refs[ns:ns + nl], outs[0], outs[1]):
            cp.start()
        outs[-1][...] = jnp.zeros_like(outs[-1])

    sems = pltpu.SemaphoreType.DMA((n_copies,))
    res = pl.pallas_call(
        body, name=name, in_specs=[HBM] * (ns + nl) + [ANY] * len(after),
        out_specs=(SEM, SEM, *[HBM] * (ns + nl), pl.BlockSpec(memory_space=pltpu.VMEM)),
        out_shape=(sems, sems, *[pltpu.HBM(a.shape, a.dtype) for a in srcs],
                   *[pltpu.HBM(s.shape, s.dtype) for s in land_shapes], jax.ShapeDtypeStruct((8, 128), F32)),
        input_output_aliases={i: 2 + i for i in range(ns + nl)},
        compiler_params=pltpu.CompilerParams(has_side_effects=_DATAFLOW),
    )(*[pltpu.with_memory_space_constraint(a, pltpu.HBM) for a in [*srcs, *lands]], *after)
    return res[0], res[1], list(res[2:2 + ns]), list(res[2 + ns:2 + ns + nl]), res[-1]


def _split_wait(name, send_sems, recv_sems, srcs, lands, copies, after=()):
    ns, nl = len(srcs), len(lands)

    def body(*refs):
        for cp in copies(refs[:ns], refs[ns:ns + nl], refs[ns + nl], refs[ns + nl + 1]):
            cp.wait_send()
            cp.wait_recv()

    res = pl.pallas_call(
        body, name=name, in_specs=[HBM] * (ns + nl) + [SEM, SEM] + [ANY] * len(after), out_specs=[HBM] * (ns + nl),
        out_shape=[pltpu.HBM(a.shape, a.dtype) for a in [*srcs, *lands]],
        input_output_aliases={i: i for i in range(ns + nl)},
        compiler_params=pltpu.CompilerParams(has_side_effects=_DATAFLOW),
    )(*srcs, *lands, send_sems, recv_sems, *after)
    return list(res[ns:])


def _gather_copies(rows):
    def copies(src_refs, land_refs, send_sems, recv_sems):
        x, y, c = _place()
        j = 2 * x + y
        out = []
        for w in range(len(src_refs)):
            r = _half(c, rows[w] // 2)
            for k in range(3):
                px, py = _other_chip(x, y, k)
                out.append(_remote(src_refs[w].at[r], land_refs[w].at[j, r], send_sems, recv_sems, 3 * w + k, (px, py, c)))
        return out
    return copies


def _scatter_copies(src_refs, land_refs, send_sems, recv_sems):
    x, y, c = _place()
    j = 2 * x + y
    out = []
    for w in range(len(src_refs)):
        for k in range(3):
            px, py = _other_chip(x, y, k)
            pj = 2 * px + py
            out.append(_remote(src_refs[w].at[pj], land_refs[w].at[(j - pj + 4) % 4 - 1], send_sems, recv_sems, 3 * w + k,
                               (px, py, c)))
    return out


def _reduce_begin(grads, core, tag):
    names = list(grads)
    gs = [grads[k].reshape(N_CHIPS, 2, -1, grads[k].shape[-1]) for k in names]
    recvs = _swap_halves(gs, f"grad_swap_halves_{tag}")
    sums = [_pair_sum(g, r, core, f"pair_sum_{k}") for k, g, r in zip(names, gs, recvs)]
    return names, gs, recvs, sums


def _reduce_end(begun, gots, chip, core):
    names, gs, recvs, _ = begun
    return {k: _chip_sum(g, r, t, chip, core, f"chip_sum_{k}") for k, g, r, t in zip(names, gs, recvs, gots)}


def _got_shapes(sums):
    return [jax.ShapeDtypeStruct((3, *a.shape[1:]), a.dtype) for a in sums]


def _adamw(w, g, m, v, name, layers=1, layer=0, into=None):
    shape = w.shape
    cols = shape[-1]
    w3, m3, v3 = (t.reshape(layers, -1, cols) for t in (w, m, v))
    rows = w3.shape[1]
    tile = _pick(rows, ADAM_ROWS if cols <= 1024 else ADAM_ROWS // 2) if rows % 8 == 0 else rows
    n_in = 4 + (0 if into is None else 4)
    stack_g = layers > 1

    def body(*refs):
        wv, gv, mv, vv = (r[...] for r in refs[:4])
        d_ref, m_ref, v_ref = refs[len(refs) - 3:]
        m2 = ADAM_B1 * mv + (1.0 - ADAM_B1) * gv
        v2 = ADAM_B2 * vv + (1.0 - ADAM_B2) * jnp.square(gv)
        m_hat = m2 / (1.0 - ADAM_B1 ** ADAM_STEP)
        v_hat = v2 / (1.0 - ADAM_B2 ** ADAM_STEP)
        if stack_g:
            refs[n_in][...] = gv
        d_ref[...] = -ADAM_LR * (m_hat / (jnp.sqrt(v_hat) + ADAM_EPS) + ADAM_WD * wv)
        m_ref[...] = m2
        v_ref[...] = v2

    n_out = 4 if stack_g else 3
    lay = pl.BlockSpec((None, tile, cols), lambda i: (layer, i, 0))
    out = jax.ShapeDtypeStruct((layers, rows, cols), F32)
    res = pl.pallas_call(
        body, name=name, grid=(rows // tile,),
        in_specs=[lay, pl.BlockSpec((tile, cols), lambda i: (i, 0)), lay, lay] + [ANY] * (n_in - 4),
        out_specs=[lay] * n_out, out_shape=[out] * n_out,
        input_output_aliases={} if into is None else {4 + k: k for k in range(4)},
        compiler_params=_cparams(("arbitrary",)),
    )(w3, g.reshape(rows, cols), m3, v3, *([] if into is None else [t.reshape(layers, rows, cols) for t in into]))
    res = tuple(t.reshape(shape) for t in res)
    return res if stack_g else (g.reshape(shape), *res)


ROW_F32, ROW_BF16 = (D_MODEL, F32), (D_MODEL, BF16)


def _res_norm(acc, h, gain):
    hh = h + acc
    return hh, _rms(hh, gain)


def _dx_norm_bwd(d, w, h, dres, gain, name, **kw):
    def epilogue(acc, hv, dr, g):
        dx, dg = _rms_bwd(hv, acc, g)
        return dr + dx, dr + dx, _colsum(dg)
    return _mm_rows(d, w, tb=True, extras=[h, dres], fulls=[gain], outs=[ROW_F32, ROW_BF16], accs=[((1, D_MODEL), F32)],
                    epilogue=epilogue, name=name, **kw)


def _tail_fwd(h1, hn2, p16, W, i, tag, next_gain=None, target=None):
    a = _mm(hn2, W["mlp_w1"][i], bblk=True, outs=[BF16], name=f"{tag}_mlp_w1", tm=2048, tn=1024,
            epilogue=lambda acc: (jnp.square(jnp.maximum(acc, 0.0)),))
    h2, hn3 = _mm_rows(a, W["mlp_w2"][i], extras=[h1], fulls=[W["ple_norm"][i:i + 1]], outs=[ROW_F32, ROW_BF16],
                       epilogue=_res_norm, name=f"{tag}_mlp_w2")
    def embed(acc, pv, h, wp):
        gate = _sigmoid(acc)
        ppv = jnp.concatenate([_dot(pv, wp[s]) for s in range(N_CHIPS)], axis=-1)
        return gate, ppv, h + gate * ppv

    if target is None:
        def gated(acc, pv, h, wp, gain):
            gate, ppv, hh = embed(acc, pv, h, wp)
            return hh, ppv, gate, _rms(hh, gain)
        h3, pp, gate, hn = _mm_rows(hn3, W["ple_gate_w"][i], extras=[p16[i], h2], fulls=[W["ple_proj_w"][i], next_gain],
                                    outs=[ROW_F32, ROW_BF16, ROW_BF16, ROW_BF16], epilogue=gated, name=f"{tag}_ple")
        return h3, hn, (h1, hn2, a, h2, hn3, gate, pp)

    def gated_loss(acc, pv, h, t, wp):
        gate, ppv, hh = embed(acc, pv, h, wp)
        e = hh - t
        return ppv, gate, e * (1.0 / D_MODEL), jnp.full((1, 128), 0.5 / D_MODEL, F32) * jnp.sum(e * e)
    pp, gate, dy, loss = _mm_rows(hn3, W["ple_gate_w"][i], extras=[p16[i], h2, target], fulls=[W["ple_proj_w"][i]],
                                  outs=[ROW_BF16, ROW_BF16, ROW_F32], accs=[((1, 128), F32)], epilogue=gated_loss,
                                  name=f"{tag}_ple")
    return dy, loss, (h1, hn2, a, h2, hn3, gate, pp)


def _tail_bwd(dh3, saved, p16, W, i, tag, after=()):
    h1, hn2, a, h2, hn3, gate, pp = saved

    def embed_bwd(d, g, ppv, hv, wg, gain):
        g, ppv = g.astype(F32), ppv.astype(F32)
        dppv, dglv = (d * g).astype(BF16), (d * ppv * g * (1.0 - g)).astype(BF16)
        dx, dg = _rms_bwd(hv, _dot_nt(dglv, wg), gain)
        return dppv, dglv, d + dx, d + dx, _colsum(dg)

    def dw(kind, name):
        return (kind, 1, 0, None)

    dpp, dgl, dh2, dh2_16, d_ple_norm = _rows(
        embed_bwd, [dh3, gate, pp, h2], [W["ple_gate_w"][i], W["ple_norm"][i:i + 1]],
        [ROW_BF16, ROW_BF16, ROW_F32, ROW_BF16], [((1, D_MODEL), F32)], name=f"{tag}_ple_bwd", after=after)
    d_proj = _mm(p16[i], dpp, ta=True, outs=[BF16], dw=dw("cols", "ple_proj_w"), name=f"{tag}_d_ple_proj")
    d_gate = _mm(hn3, dgl, ta=True, outs=[BF16], dw=dw("rows", "ple_gate_w"), name=f"{tag}_d_ple_gate")
    d_w2 = _mm(a, dh2_16, ta=True, outs=[BF16], dw=dw("rows", "mlp_w2"), name=f"{tag}_d_mlp_w2", tn=1024)
    dz = _mm(dh2_16, W["mlp_w2"][i], tb=True, extras=[a], outs=[BF16], name=f"{tag}_mlp_w2_dx", tm=2048, tn=1024,
             epilogue=lambda acc, av: (acc * (2.0 * jnp.sqrt(av.astype(F32))),))
    d_w1 = _mm(hn2, dz, ta=True, outs=[BF16], dw=dw("cols", "mlp_w1"), name=f"{tag}_d_mlp_w1", tn=1024)
    dh1, dh1_16, d_mlp_norm = _dx_norm_bwd(dz, W["mlp_w1"][i], h1, dh2, W["mlp_norm"][i:i + 1], f"{tag}_mlp_w1_dx",
                                           bblk=True)
    big = {f"mlp_w1_{i}": d_w1, f"mlp_w2_{i}": d_w2, f"ple_gate_w_{i}": d_gate, f"ple_proj_w_{i}": d_proj}
    return dh1, dh1_16, big, dict(mlp_norm=d_mlp_norm, ple_norm=d_ple_norm)


def _ret_layer_fwd(h0, W, tabs, after=()):
    hn = _rows(lambda x, g: (_rms(x, g),), [h0], [W["mix_norm"][0:1]], [(D_MODEL, BF16)], name="ret_mix_norm",
               after=after)[0]
    proj = _mm(hn, W["ret_w_in"], bblk=True, outs=[BF16], name="ret_w_in", tm=2048, tn=768)
    out, states = _ret_fwd(proj, tabs, "ret_scan")
    y = _ret_gate(out, proj, W["ret_gn"], "ret_gate")
    h1, hn2 = _mm_rows(y, W["ret_w_out"], extras=[h0], fulls=[W["mlp_norm"][0:1]], outs=[ROW_F32, ROW_BF16],
                       epilogue=_res_norm, name="ret_w_out")
    return h1, hn2, (h0, hn, proj, out, states, y)


def _d_ret_w_out(dh1_16, saved):
    return _mm(saved[5], dh1_16, ta=True, outs=[BF16], dw=("rows", 1, 0, None), name="d_ret_w_out")


def _ret_layer_bwd(dh1, dh1_16, saved, W, tabs, after=(), on_grads=None, d_w_out=None):
    h0, hn, proj, out, states, y = saved
    d_w_out = _d_ret_w_out(dh1_16, saved) if d_w_out is None else d_w_out
    dy = _mm(dh1_16, W["ret_w_out"], tb=True, name="ret_w_out_dx", tn=1024, after=after)
    dout, dproj, d_gn = _ret_gate_bwd(out, proj, W["ret_gn"], dy, "ret_gate_bwd")
    dproj = _ret_bwd(proj, states, dout, dproj, tabs, "ret_scan_bwd")
    d_w_in = _mm(hn, dproj, ta=True, outs=[BF16], dw=("cols", 1, 0, None), name="d_ret_w_in", tn=768)
    big = dict(ret_w_in=d_w_in, ret_w_out=d_w_out)
    later = () if on_grads is None else on_grads(big)
    dh0, _, d_mix = _dx_norm_bwd(dproj, W["ret_w_in"], h0, dh1, W["mix_norm"][0:1], "ret_w_in_dx", bblk=True, tm=256,
                                 after=later)
    return dh0, big, dict(mix_norm=d_mix, ret_gn=d_gn)


def _mla_layer_fwd(h0, hn, W, tabs):
    proj, cqn, ckvn, q, kv, qf, kf, vf = _mla_front(hn, W, tabs, "mla_front")
    o, lse = _flash_fwd(qf, kf, vf, "mla_flash")
    h1, hn2 = _mm_rows(o, W["mla_w_out"], extras=[h0], fulls=[W["mlp_norm"][1:2]], outs=[ROW_F32, ROW_BF16],
                       epilogue=_res_norm, name="mla_w_out")
    return h1, hn2, (h0, hn, proj, cqn, ckvn, q, kv, qf, kf, vf, o, lse)


def _mla_layer_bwd(dh1, dh1_16, saved, W, tabs):
    h0, hn, proj, cqn, ckvn, q, kv, qf, kf, vf, o, lse = saved
    d_w_out = _mm(o, dh1_16, ta=True, outs=[BF16], dw=("rows", 1, 0, None), name="d_mla_w_out")
    def with_delta(acc, ov):
        parts = []
        for h in range(MLA_HEADS):
            sl = slice(h * MLA_VD, (h + 1) * MLA_VD)
            d = jnp.sum(acc[:, sl] * ov[:, sl], axis=-1, keepdims=True)
            parts.append(jnp.broadcast_to(d, (d.shape[0], MLA_VD)))
        return jnp.concatenate(parts, axis=-1), acc

    delta, do16 = _mm_rows(dh1_16, W["mla_w_out"], tb=True, extras=[o], outs=[ROW_F32, ROW_BF16], epilogue=with_delta,
                           name="mla_w_out_dx")
    dqf, dkf, dvf = _flash_bwd(qf, kf, vf, do16, lse, delta, "mla_flash_bwd")
    dq, dkv, dproj, dh0, dh0_16, d_gq, d_gk, d_gqa, d_gkva, d_mix = _mla_back(q, kv, proj, h0, dh1, dqf, dkf, dvf, W, tabs,
                                                                              "mla_back")
    d_w_uq = _mm(cqn, dq, ta=True, outs=[BF16], dw=("cols", 1, 0, None), name="d_mla_w_uq")
    d_w_ukv = _mm(ckvn, dkv, ta=True, outs=[BF16], dw=("cols", 1, 0, None), name="d_mla_w_ukv")
    d_w_in = _mm(hn, dproj, ta=True, outs=[BF16], dw=("rows", 1, 0, None), name="d_mla_w_in")
    return (dh0, dh0_16, dict(mla_w_in=d_w_in, mla_w_uq=d_w_uq, mla_w_ukv=d_w_ukv, mla_w_out=d_w_out),
            dict(mix_norm=d_mix, mla_q_a_norm=d_gqa, mla_kv_a_norm=d_gkva, mla_q_norm=d_gq, mla_k_norm=d_gk))


def _small_grads(n_ret, n_t0, n_mla, n_t1):
    return dict(
        mix_norm=jnp.concatenate([n_ret["mix_norm"], n_mla["mix_norm"]], axis=0),
        mlp_norm=jnp.concatenate([n_t0["mlp_norm"], n_t1["mlp_norm"]], axis=0),
        ple_norm=jnp.concatenate([n_t0["ple_norm"], n_t1["ple_norm"]], axis=0),
        ret_gn=n_ret["ret_gn"], mla_q_a_norm=n_mla["mla_q_a_norm"], mla_kv_a_norm=n_mla["mla_kv_a_norm"],
        mla_q_norm=n_mla["mla_q_norm"], mla_k_norm=n_mla["mla_k_norm"])


_ORDER = ("mix_norm", "ret_w_in", "ret_gn", "ret_w_out", "mla_w_in", "mla_q_a_norm", "mla_kv_a_norm", "mla_w_uq",
          "mla_w_ukv", "mla_q_norm", "mla_k_norm", "mla_w_out", "mlp_norm", "mlp_w1", "mlp_w2", "ple_norm",
          "ple_gate_w", "ple_proj_w")
_TWO_LAYER = ("mlp_w1", "mlp_w2", "ple_gate_w", "ple_proj_w")
HEADS_PER_CHIP = MLA_HEADS // N_CHIPS
GAIN_ROWS = 32


def _travel_parts(w):
    uq = jnp.pad(w["mla_w_uq"][0].reshape(MLA_Q_RANK, HEADS_PER_CHIP, MLA_QKD), ((0, 0), (0, 0), (0, MLA_HP - MLA_QKD)))
    parts = {"ret_w_in": w["ret_w_in"][0], "ret_w_out": w["ret_w_out"][0]}
    for k in _TWO_LAYER:
        parts[k + "_0"] = w[k][0]
    parts["mla_w_in"] = jnp.pad(w["mla_w_in"][0], ((0, 0), (0, MLA_IN_PAD - MLA_IN)))
    parts["mla_w_uq"] = uq.reshape(MLA_Q_RANK, HEADS_PER_CHIP * MLA_HP)
    parts["mla_w_ukv"] = w["mla_w_ukv"][0]
    parts["mla_w_out"] = w["mla_w_out"][0]
    for k in _TWO_LAYER:
        parts[k + "_1"] = w[k][1]
    gains = jnp.concatenate([_pad_row(w["ret_gn"]), _pad_row(w["mla_q_a_norm"]), _pad_row(w["mla_kv_a_norm"]),
                             jnp.zeros((GAIN_ROWS - 3, PACK_W), F32)], axis=0)
    return {"gains": gains, **{k: v.astype(BF16) for k, v in parts.items()}}


def _full_weights(full):
    rows = lambda a: a.reshape(-1, a.shape[-1])
    W = {k: full[k] for k in ("ret_w_in", "mla_w_uq", "mla_w_ukv") if k in full}
    for k in ("ret_w_out", "mla_w_in", "mla_w_out"):
        if k in full:
            W[k] = rows(full[k])
    for k, by_rows in (("mlp_w1", False), ("ple_proj_w", False), ("mlp_w2", True), ("ple_gate_w", True)):
        layers = [full.get(f"{k}_{i}") for i in range(2)]
        W[k] = [rows(t) if (by_rows and t is not None) else t for t in layers]
    return W


def _shard_grad(name, red, shape):
    if name == "mla_w_in":
        red = red.reshape(-1, MLA_IN_PAD)[:, :MLA_IN]
    elif name == "mla_w_uq":
        red = red.reshape(MLA_Q_RANK, HEADS_PER_CHIP, MLA_HP)[:, :, :MLA_QKD]
    return red.reshape(shape)


def _pad_row(v):
    v = v.reshape(1, -1)
    return jnp.pad(v, ((0, 0), (0, PACK_W - v.shape[1])))


def kernel(x, p, mix_norm, ret_w_in, ret_gn, ret_w_out, mla_w_in, mla_q_a_norm, mla_kv_a_norm, mla_w_uq, mla_w_ukv, mla_q_norm, mla_k_norm, mla_w_out, mlp_norm, mlp_w1, mlp_w2, ple_norm, ple_gate_w, ple_proj_w, loss_target, m_mix_norm, m_ret_w_in, m_ret_gn, m_ret_w_out, m_mla_w_in, m_mla_q_a_norm, m_mla_kv_a_norm, m_mla_w_uq, m_mla_w_ukv, m_mla_q_norm, m_mla_k_norm, m_mla_w_out, m_mlp_norm, m_mlp_w1, m_mlp_w2, m_ple_norm, m_ple_gate_w, m_ple_proj_w, v_mix_norm, v_ret_w_in, v_ret_gn, v_ret_w_out, v_mla_w_in, v_mla_q_a_norm, v_mla_kv_a_norm, v_mla_w_uq, v_mla_w_ukv, v_mla_q_norm, v_mla_k_norm, v_mla_w_out, v_mlp_norm, v_mlp_w1, v_mlp_w2, v_ple_norm, v_ple_gate_w, v_ple_proj_w):
    w = dict(mix_norm=mix_norm, ret_w_in=ret_w_in, ret_gn=ret_gn, ret_w_out=ret_w_out, mla_w_in=mla_w_in,
             mla_q_a_norm=mla_q_a_norm, mla_kv_a_norm=mla_kv_a_norm, mla_w_uq=mla_w_uq, mla_w_ukv=mla_w_ukv,
             mla_q_norm=mla_q_norm, mla_k_norm=mla_k_norm, mla_w_out=mla_w_out, mlp_norm=mlp_norm, mlp_w1=mlp_w1,
             mlp_w2=mlp_w2, ple_norm=ple_norm, ple_gate_w=ple_gate_w, ple_proj_w=ple_proj_w)
    m = dict(mix_norm=m_mix_norm, ret_w_in=m_ret_w_in, ret_gn=m_ret_gn, ret_w_out=m_ret_w_out, mla_w_in=m_mla_w_in,
             mla_q_a_norm=m_mla_q_a_norm, mla_kv_a_norm=m_mla_kv_a_norm, mla_w_uq=m_mla_w_uq, mla_w_ukv=m_mla_w_ukv,
             mla_q_norm=m_mla_q_norm, mla_k_norm=m_mla_k_norm, mla_w_out=m_mla_w_out, mlp_norm=m_mlp_norm,
             mlp_w1=m_mlp_w1, mlp_w2=m_mlp_w2, ple_norm=m_ple_norm, ple_gate_w=m_ple_gate_w, ple_proj_w=m_ple_proj_w)
    v = dict(mix_norm=v_mix_norm, ret_w_in=v_ret_w_in, ret_gn=v_ret_gn, ret_w_out=v_ret_w_out, mla_w_in=v_mla_w_in,
             mla_q_a_norm=v_mla_q_a_norm, mla_kv_a_norm=v_mla_kv_a_norm, mla_w_uq=v_mla_w_uq, mla_w_ukv=v_mla_w_ukv,
             mla_q_norm=v_mla_q_norm, mla_k_norm=v_mla_k_norm, mla_w_out=v_mla_w_out, mlp_norm=v_mlp_norm,
             mlp_w1=v_mlp_w1, mlp_w2=v_mlp_w2, ple_norm=v_ple_norm, ple_gate_w=v_ple_gate_w, ple_proj_w=v_ple_proj_w)
    xi, yi, ci = _place()
    chip = 2 * xi + yi
    n = N_CHIPS

    parts = _travel_parts(w)
    first = ("gains", "ret_w_in", "ret_w_out")
    mid = [k + "_0" for k in _TWO_LAYER]
    last = [k for k in parts if k not in first and k not in mid]
    full = dict(zip(first, _gather_weights([parts[k] for k in first], "gather_first")))

    def gather_behind(names, tag, after):
        copies = _gather_copies([parts[k].shape[0] for k in names])
        started = _split_start(f"gather_{tag}_start", [parts[k] for k in names],
                               [jax.ShapeDtypeStruct((n, *parts[k].shape), BF16) for k in names], 3 * len(names),
                               copies, after=after)

        def arrive(after):
            landed = _split_wait(f"gather_{tag}_wait", *started[:4], copies, after=after)
            full.update(zip(names, _gather_weights([parts[k] for k in names], f"gather_{tag}_finish", landed=landed)))
            W.update(_full_weights(full))
        return started[4], arrive

    mid_token, mid_arrive = gather_behind(mid, "mid", [full["ret_w_in"]])
    g_token, last_arrive = gather_behind(last, "last", [mid_token])
    gains = full["gains"]
    W = dict(mix_norm=mix_norm, mlp_norm=mlp_norm, ple_norm=ple_norm,
             mla_q_norm=jnp.pad(mla_q_norm, ((0, 0), (0, MLA_HP - MLA_QKD))),
             mla_k_norm=jnp.pad(mla_k_norm, ((0, 0), (0, MLA_HP - MLA_QKD))),
             ret_w_in=full["ret_w_in"], ret_w_out=full["ret_w_out"].reshape(-1, D_MODEL),
             ret_gn=gains[:, 0, :RET_HEADS * 128].reshape(n, RET_HEADS, 128).transpose(1, 0, 2).reshape(RET_HEADS, RET_DV),
             mla_q_a_norm=gains[:, 1, :MLA_Q_RANK // n].reshape(1, MLA_Q_RANK),
             mla_kv_a_norm=gains[:, 2, :MLA_KV_RANK // n].reshape(1, MLA_KV_RANK))
    x0, p16, target = x[0], p[:, 0].astype(BF16), loss_target[0]
    T = x0.shape[0]
    ret_tabs, mla_tabs = _ret_tables(T), _mla_tables(T)

    h1, hn, s_ret = _ret_layer_fwd(x0, W, ret_tabs, after=[g_token])
    mid_arrive([h1])
    h3, hn, s_tail0 = _tail_fwd(h1, hn, p16, W, 0, "l0", next_gain=W["mix_norm"][1:2])
    last_arrive([h3])
    h4, hn, s_mla = _mla_layer_fwd(h3, hn, W, mla_tabs)
    dy, loss, s_tail1 = _tail_fwd(h4, hn, p16, W, 1, "l1", target=target)

    dh4, dh4_16, g_t1, n_t1 = _tail_bwd(dy, s_tail1, p16, W, 1, "l1")
    dh3, _, g_mla, n_mla = _mla_layer_bwd(dh4, dh4_16, s_mla, W, mla_tabs)
    beg_a = _reduce_begin({**g_mla, **g_t1}, ci, "a")
    a_send, a_recv, a_src, a_land, a_token = _split_start(
        "scatter_a_start", beg_a[3], _got_shapes(beg_a[3]), 3 * len(beg_a[3]), _scatter_copies)
    dh1, dh1_16, g_t0, n_t0 = _tail_bwd(dh3, s_tail0, p16, W, 0, "l0", after=[a_token])
    d_ret_w_out = _d_ret_w_out(dh1_16, s_ret)
    beg_b = _reduce_begin({**g_t0, "ret_w_out": d_ret_w_out}, ci, "b")
    b_send, b_recv, b_src, b_land, b_token = _split_start(
        "scatter_b_start", beg_b[3], _got_shapes(beg_b[3]), 3 * len(beg_b[3]), _scatter_copies)
    stage_c = {}

    def start_c(g_ret):
        beg = _reduce_begin({"ret_w_in": g_ret["ret_w_in"]}, ci, "c")
        stage_c["beg"] = beg
        stage_c["st"] = _split_start("scatter_c_start", beg[3], _got_shapes(beg[3]), 3 * len(beg[3]), _scatter_copies)
        return [stage_c["st"][4]]

    dx, _, n_ret = _ret_layer_bwd(dh1, dh1_16, s_ret, W, ret_tabs, after=[b_token], on_grads=start_c,
                                  d_w_out=d_ret_w_out)
    got_a = _split_wait("scatter_a_wait", a_send, a_recv, a_src, a_land, _scatter_copies, after=[dx])
    got_b = _split_wait("scatter_b_wait", b_send, b_recv, b_src, b_land, _scatter_copies, after=[dx])
    got_c = _split_wait("scatter_c_wait", *stage_c["st"][:4], _scatter_copies, after=[dx])
    red = {**_reduce_end(beg_a, got_a, chip, ci), **_reduce_end(beg_b, got_b, chip, ci),
           **_reduce_end(stage_c["beg"], got_c, chip, ci)}
    red = dict(zip(red, _share_halves(list(red.values()))))
    gs = _small_grads(n_ret, n_t0, n_mla, n_t1)
    small_g = jnp.concatenate([
        gs["mix_norm"], gs["mlp_norm"], gs["ple_norm"], gs["ret_gn"].reshape(2, PACK_W), _pad_row(gs["mla_q_a_norm"]),
        _pad_row(gs["mla_kv_a_norm"]), _pad_row(gs["mla_q_norm"][:, :MLA_QKD]), _pad_row(gs["mla_k_norm"][:, :MLA_QKD]),
        _pad_row(loss[:, :1]), jnp.zeros((3, PACK_W), F32)], axis=0)
    tot = _allsum_small(small_g, "sum_small_grads")
    gn_all = tot[6:8].reshape(RET_HEADS, n, -1)
    g_small = dict(
        mix_norm=tot[0:2], mlp_norm=tot[2:4], ple_norm=tot[4:6],
        ret_gn=lax.dynamic_index_in_dim(gn_all, chip, axis=1, keepdims=False),
        mla_q_a_norm=lax.dynamic_index_in_dim(tot[8, :MLA_Q_RANK].reshape(n, -1), chip, axis=0, keepdims=True),
        mla_kv_a_norm=lax.dynamic_index_in_dim(tot[9, :MLA_KV_RANK].reshape(n, -1), chip, axis=0, keepdims=True),
        mla_q_norm=tot[10:11, :MLA_QKD], mla_k_norm=tot[11:12, :MLA_QKD])
    loss_out = tot[12, 0]

    outs = []
    for k in _ORDER:
        if k in _TWO_LAYER:
            res = None
            for i in (1, 0):
                res = _adamw(w[k], red[f"{k}_{i}"], m[k], v[k], f"adamw_{k}_{i}", layers=2, layer=i, into=res)
        elif k in red:
            res = _adamw(w[k], _shard_grad(k, red[k], w[k].shape), m[k], v[k], f"adamw_{k}")
        else:
            res = _adamw(w[k], g_small[k], m[k], v[k], f"adamw_{k}")
        outs.append(res)
    return (loss_out, dx[None], *[o[0] for o in outs], *[o[1] for o in outs], *[o[2] for o in outs],
            *[o[3] for o in outs])
```

```python
import jax
import jax.numpy as jnp
import numpy as np
from jax import lax
from jax.experimental import pallas as pl
from jax.experimental.pallas import tpu as pltpu

F32 = jnp.float32
BF16 = jnp.bfloat16

EPS = 1e-6
D_MODEL = 1024
CHUNK = 64
ROPE_THETA = 10000.0
RET_HEADS = 4
RET_DK = 256
RET_DV = 512
RET_GROUP = 1
RET_BLOCK = 256
RET_ROWS = 1024
MLA_HEADS = 8
MLA_ROPE = 64
MLA_QKD = 192
MLA_VD = 128
MLA_HP = 256
MLA_Q_RANK = 384
MLA_KV_RANK = 256
MLA_IN = 704
MLA_IN_PAD = 768
N_CHIPS = 4

ADAM_LR = 0.001
ADAM_B1 = 0.9
ADAM_B2 = 0.999
ADAM_EPS = 1e-08
ADAM_WD = 0.01
ADAM_STEP = 10

VMEM_LIMIT = 56 * 1024 * 1024
PACK_W = 1024
NEG = -1e30
LOG2E = 1.4426950408889634
FLASH_T = 512
FLASH_HEADS = 2
MM_SUB_ROWS = 256
SUM_ROWS = 512
ADAM_ROWS = 512


def _cparams(sem=None):
    return pltpu.CompilerParams(dimension_semantics=sem, vmem_limit_bytes=VMEM_LIMIT)


def _pick(dim, pref):
    if dim <= pref:
        return dim
    t = pref
    while dim % t:
        t //= 2
    return t


def _mm(a, b, *, name, ta=False, tb=False, bblk=False, outs=None, extras=(), epilogue=None, dw=None,
        tm=1024, tn=512, after=()):
    if ta:
        K, M = a.shape
    else:
        M, K = a.shape
    if bblk and tb:
        nb, N, Kq = b.shape
        assert nb * Kq == K
    elif bblk:
        nb, Kb, Nq = b.shape
        N = nb * Nq
        assert Kb == K
    else:
        N = b.shape[0] if tb else b.shape[1]
    tn = _pick(Nq if (bblk and not tb) else N, tn)
    if dw is not None and dw[0] == "cols":
        tn = _pick(N // N_CHIPS, tn)
    tm = _pick(M // N_CHIPS if (dw is not None and dw[0] == "rows") else M, tm)
    grid = (M // tm, N // tn)

    a_spec = pl.BlockSpec((K, tm), lambda i, j: (0, i)) if ta else pl.BlockSpec((tm, K), lambda i, j: (i, 0))
    if bblk and tb:
        b_spec = pl.BlockSpec((nb, tn, Kq), lambda i, j: (0, j, 0))
    elif bblk:
        npb = Nq // tn
        b_spec = pl.BlockSpec((None, K, tn), lambda i, j: (j // npb, 0, j % npb))
    elif tb:
        b_spec = pl.BlockSpec((tn, K), lambda i, j: (j, 0))
    else:
        b_spec = pl.BlockSpec((K, tn), lambda i, j: (0, j))
    in_specs = [a_spec, b_spec] + [pl.BlockSpec((tm, tn), lambda i, j: (i, j)) for _ in extras]
    args = [a, b, *extras]
    aliases = {}
    if outs is None:
        outs = [F32]
    if dw is None:
        o_specs = [pl.BlockSpec((tm, tn), lambda i, j: (i, j)) for _ in outs]
        o_shapes = [jax.ShapeDtypeStruct((M, N), dt) for dt in outs]
    else:
        kind, layers, layer, into = dw
        if kind == "cols":
            per = (N // N_CHIPS) // tn
            o_specs = [pl.BlockSpec((None, None, tm, tn), lambda i, j: (j // per, layer, i, j % per))]
            o_shapes = [jax.ShapeDtypeStruct((N_CHIPS, layers, M, N // N_CHIPS), outs[0])]
        else:
            per = (M // N_CHIPS) // tm
            o_specs = [pl.BlockSpec((None, None, tm, tn), lambda i, j: (i // per, layer, i % per, j))]
            o_shapes = [jax.ShapeDtypeStruct((N_CHIPS, layers, M // N_CHIPS, N), outs[0])]
        if into is not None:
            aliases = {len(args): 0}
            in_specs.append(pl.BlockSpec(memory_space=pl.ANY))
            args.append(into)
    for t in after:
        in_specs.append(pl.BlockSpec(memory_space=pl.ANY))
        args.append(t)
    n_e, n_o = len(extras), len(outs)

    sub = _pick(tm, MM_SUB_ROWS)

    def body(a_ref, b_ref, *rest):
        e_refs, o_refs = rest[:n_e], rest[len(rest) - n_o:]
        for r0 in range(0, tm, sub):
            rows = slice(r0, r0 + sub)
            av = (a_ref[:, rows] if ta else a_ref[rows, :]).astype(BF16)
            if bblk and tb:
                acc = _dot_nt(av[:, :Kq], b_ref[0].astype(BF16))
                for s in range(1, nb):
                    acc = acc + _dot_nt(av[:, s * Kq:(s + 1) * Kq], b_ref[s].astype(BF16))
            elif ta:
                acc = _dot_tn(av, b_ref[...].astype(BF16))
            elif tb:
                acc = _dot_nt(av, b_ref[...].astype(BF16))
            else:
                acc = _dot(av, b_ref[...].astype(BF16))
            vals = (acc,) if epilogue is None else epilogue(acc, *[e[rows, :] for e in e_refs])
            for o, v in zip(o_refs, vals):
                o[rows, :] = v.astype(o.dtype)

    res = pl.pallas_call(
        body, name=name, grid=grid, in_specs=in_specs, out_specs=o_specs, out_shape=o_shapes,
        input_output_aliases=aliases, compiler_params=_cparams(("parallel", "arbitrary")),
    )(*args)
    return res[0] if n_o == 1 else res


def _mm_rows(a, b, *, name, epilogue, outs, tb=False, bblk=False, extras=(), fulls=(), accs=(), tm=512, after=()):
    M, K = a.shape
    tm = _pick(M, tm)
    sub = _pick(tm, MM_SUB_ROWS)
    nb = b.shape[0] if bblk else 1
    n_e, n_f, n_o, n_a = len(extras), len(fulls), len(outs), len(accs)
    n_in = 2 + n_e + n_f + len(after)

    def whole(t):
        return pl.BlockSpec(t.shape, lambda i, nd=t.ndim: (0,) * nd)

    in_specs = [pl.BlockSpec((tm, K), lambda i: (i, 0)), whole(b)]
    in_specs += [pl.BlockSpec((tm, e.shape[1]), lambda i: (i, 0)) for e in extras] + [whole(f) for f in fulls]
    in_specs += [pl.BlockSpec(memory_space=pl.ANY) for _ in after]
    out_specs = [pl.BlockSpec((tm, w), lambda i: (i, 0)) for w, _ in outs] + [pl.BlockSpec(s, lambda i: (0, 0)) for s, _ in accs]
    out_shape = [jax.ShapeDtypeStruct((M, w), dt) for w, dt in outs] + [jax.ShapeDtypeStruct(s, dt) for s, dt in accs]

    def body(a_ref, b_ref, *rest):
        e_refs, f_refs = rest[:n_e], rest[n_e:n_e + n_f]
        o_refs, acc_refs = rest[n_in - 2:n_in - 2 + n_o], rest[n_in - 2 + n_o:]
        fv = [f[...] for f in f_refs]
        totals = None
        for r0 in range(0, tm, sub):
            rows = slice(r0, r0 + sub)
            av = a_ref[rows, :].astype(BF16)
            if bblk and tb:
                kq = K // nb
                acc = _dot_nt(av[:, :kq], b_ref[0])
                for s in range(1, nb):
                    acc = acc + _dot_nt(av[:, s * kq:(s + 1) * kq], b_ref[s])
            elif bblk:
                acc = jnp.concatenate([_dot(av, b_ref[s]) for s in range(nb)], axis=-1)
            elif tb:
                acc = _dot_nt(av, b_ref[...])
            else:
                acc = _dot(av, b_ref[...])
            vals = epilogue(acc, *[e[rows, :] for e in e_refs], *fv)
            for o, v in zip(o_refs, vals[:n_o]):
                o[rows, :] = v.astype(o.dtype)
            part = vals[n_o:]
            totals = part if totals is None else [t + p for t, p in zip(totals, part)]
        first_step = pl.program_id(0) == 0
        for o, v in zip(acc_refs, totals):
            @pl.when(first_step)
            def _(o=o, v=v):
                o[...] = v.astype(o.dtype)

            @pl.when(jnp.logical_not(first_step))
            def _(o=o, v=v):
                o[...] += v.astype(o.dtype)

    return pl.pallas_call(
        body, name=name, grid=(M // tm,), in_specs=in_specs, out_specs=out_specs, out_shape=out_shape,
        compiler_params=_cparams(("arbitrary",)),
    )(a, b, *extras, *fulls, *after)


def _rows(fn, rows, fulls, outs, accs=(), *, name, tile=512, after=()):
    first = rows[0][0] if isinstance(rows[0], tuple) else rows[0]
    T = first.shape[0]
    tile = _pick(T, tile)
    in_specs, args = [], []
    for r in rows:
        if isinstance(r, tuple):
            arr, w, cb = r
            in_specs.append(pl.BlockSpec((tile, w), lambda i, cb=cb: (i, cb)))
        else:
            arr = r
            in_specs.append(pl.BlockSpec((tile, arr.shape[1]), lambda i: (i, 0)))
        args.append(arr)
    for f in fulls:
        in_specs.append(pl.BlockSpec(f.shape, lambda i, nd=f.ndim: (0,) * nd))
        args.append(f)
    outs = [o if len(o) == 4 else (*o, o[0], 0) for o in outs]
    out_specs = [pl.BlockSpec((tile, w), lambda i, cb=cb: (i, cb)) for w, _, _, cb in outs]
    out_specs += [pl.BlockSpec(s, lambda i: (0, 0)) for s, _ in accs]
    out_shape = [jax.ShapeDtypeStruct((T, tw), dt) for _, dt, tw, _ in outs]
    out_shape += [jax.ShapeDtypeStruct(s, dt) for s, dt in accs]
    n_in, n_out = len(args), len(outs)
    for t in after:
        in_specs.append(pl.BlockSpec(memory_space=pl.ANY))
        args.append(t)

    def body(*refs):
        vals = fn(*[r[...] for r in refs[:n_in]])
        o_refs = refs[len(args):]
        for o, v in zip(o_refs[:n_out], vals[:n_out]):
            o[...] = v.astype(o.dtype)
        first_step = pl.program_id(0) == 0
        for o, v in zip(o_refs[n_out:], vals[n_out:]):
            @pl.when(first_step)
            def _(o=o, v=v):
                o[...] = v.astype(o.dtype)

            @pl.when(jnp.logical_not(first_step))
            def _(o=o, v=v):
                o[...] += v.astype(o.dtype)

    res = pl.pallas_call(
        body, name=name, grid=(T // tile,), in_specs=in_specs, out_specs=out_specs, out_shape=out_shape,
        compiler_params=_cparams(("arbitrary",)),
    )(*args)
    return res


def _rowsum(v, mxu):
    if not mxu:
        return jnp.sum(v, axis=-1, keepdims=True)
    ones = jnp.ones((v.shape[1], v.shape[1]), BF16)
    hi = v.astype(BF16)
    lo = (v - hi.astype(F32)).astype(BF16)
    return _dot(hi, ones) + _dot(lo, ones)


def _rms(x, g, mxu=False):
    r = lax.rsqrt(_rowsum(x * x, mxu) / x.shape[-1] + EPS)
    return (x * r) * g


def _rms_bwd(x, dy, g, n=None, mxu=False):
    n = x.shape[-1] if n is None else n
    r = lax.rsqrt(_rowsum(x * x, mxu) / n + EPS)
    xh = x * r
    dxh = dy * g
    dx = r * (dxh - xh * (_rowsum(dxh * xh, mxu) / n))
    return dx, dy * xh


def _colsum(v):
    return jnp.sum(v, axis=0, keepdims=True)


def _sigmoid(x):
    return 1.0 / (1.0 + jnp.exp(-x))


def _widen(v, width):
    reps = width // v.shape[1]
    return v if reps == 1 else jnp.concatenate([v] * reps, axis=-1)


def _rope_angles(T, dim):
    inv = (1.0 / (np.float32(ROPE_THETA) ** (np.arange(0, dim, 2, dtype=np.float32) / np.float32(dim)))).astype(np.float32)
    return np.arange(T, dtype=np.float32)[:, None] * inv[None, :]


def _ret_tables(T):
    ang = _rope_angles(T, RET_DK)
    log_gamma = np.log(np.float32(1.0) - np.float32(2.0) ** (-5.0 - np.arange(RET_HEADS, dtype=np.float32)))
    idx = np.arange(RET_BLOCK, dtype=np.float32)
    chunk = np.arange(RET_BLOCK) // CHUNK
    dist = idx[:, None] - idx[None, :]
    seen = np.where(chunk[:, None] == chunk[None, :], np.abs(dist), np.where(chunk[:, None] > chunk[None, :], dist, np.inf))
    intra = np.exp(log_gamma[:, None, None] * seen[None].astype(np.float32))
    qd = np.exp(log_gamma[:, None] * (idx + 1.0))[:, :, None]
    kd = np.exp(log_gamma[:, None] * (RET_BLOCK - 1.0 - idx))[:, :, None]
    cd = np.exp(log_gamma * RET_BLOCK)[:, None, None]
    return tuple(jnp.asarray(t, F32) for t in (np.cos(ang), np.sin(ang), intra, qd, kd, cd))


def _rope_half(x, c, s):
    x1, x2 = x[:, :RET_DK // 2], x[:, RET_DK // 2:]
    return jnp.concatenate([x1 * c - x2 * s, x2 * c + x1 * s], axis=-1)


def _rope_half_bwd(d, c, s):
    d1, d2 = d[:, :RET_DK // 2], d[:, RET_DK // 2:]
    return jnp.concatenate([d1 * c + d2 * s, d2 * c - d1 * s], axis=-1)


def _dot(a, b):
    return lax.dot_general(a, b, (((1,), (0,)), ((), ())), preferred_element_type=F32)


def _dot_nt(a, b):
    return lax.dot_general(a, b, (((1,), (1,)), ((), ())), preferred_element_type=F32)


def _dot_tn(a, b):
    return lax.dot_general(a, b, (((0,), (0,)), ((), ())), preferred_element_type=F32)


def _ret_specs(T, tb, rev):
    nj = T // tb
    jj = (lambda j: nj - 1 - j) if rev else (lambda j: j)
    g = RET_GROUP
    kq = RET_HEADS // g
    vq = 2 * RET_HEADS * RET_DK // (g * RET_DV)
    return dict(
        q=pl.BlockSpec((tb, g * RET_DK), lambda h, j: (jj(j), h)),
        k=pl.BlockSpec((tb, g * RET_DK), lambda h, j: (jj(j), kq + h)),
        v=pl.BlockSpec((tb, g * RET_DV), lambda h, j: (jj(j), vq + h)),
        tab=pl.BlockSpec((tb, RET_DK // 2), lambda h, j: (jj(j), 0)),
        intra=pl.BlockSpec((g, RET_BLOCK, RET_BLOCK), lambda h, j: (h, 0, 0)),
        dec=pl.BlockSpec((g, RET_BLOCK, 1), lambda h, j: (h, 0, 0)),
        cd=pl.BlockSpec((g, 1, 1), lambda h, j: (h, 0, 0)),
        o=pl.BlockSpec((tb, g * RET_DV), lambda h, j: (jj(j), h)),
        s=pl.BlockSpec((g, tb // RET_BLOCK, RET_DK, RET_DV), lambda h, j: (h, jj(j), 0, 0)),
    )


def _ret_fwd(proj, tabs, name):
    T = proj.shape[0]
    cos, sin, intra, qd, kd, cd = tabs
    tb = _pick(T, RET_ROWS)
    cps = tb // RET_BLOCK
    sp = _ret_specs(T, tb, False)
    scale = RET_DK ** -0.5

    def body(q_ref, k_ref, v_ref, cos_ref, sin_ref, intra_ref, qd_ref, kd_ref, cd_ref, o_ref, s_ref, state):
        @pl.when(pl.program_id(1) == 0)
        def _():
            state[...] = jnp.zeros_like(state)

        for c in range(cps):
            rows = pl.ds(c * RET_BLOCK, RET_BLOCK)
            co, si = cos_ref[rows, :], sin_ref[rows, :]
            for h in range(RET_GROUP):
                hk, hv = slice(h * RET_DK, (h + 1) * RET_DK), slice(h * RET_DV, (h + 1) * RET_DV)
                q = _rope_half(q_ref[rows, hk].astype(F32), co, si)
                k = _rope_half(k_ref[rows, hk].astype(F32), co, si) * scale
                vb = v_ref[rows, hv].astype(BF16)
                st = state[h]
                sb = st.astype(BF16)
                s_ref[h, c] = sb
                sc = _dot_nt(q.astype(BF16), k.astype(BF16)) * intra_ref[h]
                inner = _dot(sc.astype(BF16), vb)
                cross = _dot((q * qd_ref[h]).astype(BF16), sb)
                o_ref[rows, hv] = inner + cross
                state[h] = st * cd_ref[h] + _dot_tn((k * kd_ref[h]).astype(BF16), vb)

    return pl.pallas_call(
        body, name=name, grid=(RET_HEADS // RET_GROUP, T // tb),
        in_specs=[sp["q"], sp["k"], sp["v"], sp["tab"], sp["tab"], sp["intra"], sp["dec"], sp["dec"], sp["cd"]],
        out_specs=[sp["o"], sp["s"]],
        out_shape=[jax.ShapeDtypeStruct((T, RET_HEADS * RET_DV), F32),
                   jax.ShapeDtypeStruct((RET_HEADS, T // RET_BLOCK, RET_DK, RET_DV), BF16)],
        scratch_shapes=[pltpu.VMEM((RET_GROUP, RET_DK, RET_DV), F32)],
        compiler_params=_cparams(("arbitrary", "arbitrary")),
    )(proj, proj, proj, cos, sin, intra, qd, kd, cd)


def _ret_bwd(proj, states, dout, dproj, tabs, name):
    assert RET_GROUP == 1
    T = proj.shape[0]
    cos, sin, intra, qd, kd, cd = tabs
    tb = _pick(T, RET_ROWS)
    cps = tb // RET_BLOCK
    nj = T // tb
    sp = _ret_specs(T, tb, True)
    scale = RET_DK ** -0.5
    k0, v0 = RET_HEADS * RET_DK, 2 * RET_HEADS * RET_DK

    def body(q_ref, k_ref, v_ref, cos_ref, sin_ref, intra_ref, qd_ref, kd_ref, cd_ref, s_ref, do_ref, _dproj_in,
             out_ref, dq_s, dk_s, dv_s, sems, dstate):
        head, j = pl.program_id(0), pl.program_id(1)
        step = head * nj + j
        slot = step % 2
        dq_ref, dk_ref, dv_ref = dq_s.at[slot], dk_s.at[slot], dv_s.at[slot]

        @pl.when(j == 0)
        def _():
            dstate[...] = jnp.zeros_like(dstate)

        for c in reversed(range(cps)):
            rows = pl.ds(c * RET_BLOCK, RET_BLOCK)
            co, si = cos_ref[rows, :], sin_ref[rows, :]
            for h in range(RET_GROUP):
                hk, hv = slice(h * RET_DK, (h + 1) * RET_DK), slice(h * RET_DV, (h + 1) * RET_DV)
                q = _rope_half(q_ref[rows, hk].astype(F32), co, si)
                k = _rope_half(k_ref[rows, hk].astype(F32), co, si) * scale
                qb, kb = q.astype(BF16), k.astype(BF16)
                vb = v_ref[rows, hv].astype(BF16)
                dob = do_ref[rows, hv].astype(BF16)
                sb = s_ref[h, c]
                ia = intra_ref[h]
                pb = (_dot_nt(qb, kb) * ia).astype(BF16)
                dsn = dstate[h]
                dsb = dsn.astype(BF16)
                kdk = (k * kd_ref[h]).astype(BF16)
                qdq = (q * qd_ref[h]).astype(BF16)
                dv = _dot_tn(pb, dob) + _dot(kdk, dsb)
                dpb = (_dot_nt(dob, vb) * ia).astype(BF16)
                dq = _dot(dpb, kb) + _dot_nt(dob, sb) * qd_ref[h]
                dk = _dot_tn(dpb, qb) + _dot_nt(vb, dsb) * kd_ref[h]
                dstate[h] = dsn * cd_ref[h] + _dot_tn(qdq, dob)
                dq_ref[rows, hk] = _rope_half_bwd(dq, co, si).astype(BF16)
                dk_ref[rows, hk] = _rope_half_bwd(dk * scale, co, si).astype(BF16)
                dv_ref[rows, hv] = dv.astype(BF16)

        def copies(sl):
            r = pl.ds(pl.multiple_of((nj - 1 - j) * tb, tb), tb)
            cols = lambda first, w: pl.ds(pl.multiple_of(first + head * w, 128), w)
            return [pltpu.make_async_copy(dq_s.at[sl], out_ref.at[r, cols(0, RET_DK)], sems.at[sl, 0]),
                    pltpu.make_async_copy(dk_s.at[sl], out_ref.at[r, cols(k0, RET_DK)], sems.at[sl, 1]),
                    pltpu.make_async_copy(dv_s.at[sl], out_ref.at[r, cols(v0, RET_DV)], sems.at[sl, 2])]

        @pl.when(step > 0)
        def _():
            for cp in copies(1 - slot):
                cp.wait()

        for cp in copies(slot):
            cp.start()

        @pl.when(step == RET_HEADS * nj - 1)
        def _():
            for cp in copies(slot):
                cp.wait()

    return pl.pallas_call(
        body, name=name, grid=(RET_HEADS, nj),
        in_specs=[sp["q"], sp["k"], sp["v"], sp["tab"], sp["tab"], sp["intra"], sp["dec"], sp["dec"], sp["cd"],
                  sp["s"], sp["o"], pl.BlockSpec(memory_space=pl.ANY)],
        out_specs=pl.BlockSpec(memory_space=pl.ANY), out_shape=jax.ShapeDtypeStruct(dproj.shape, dproj.dtype),
        input_output_aliases={11: 0},
        scratch_shapes=[pltpu.VMEM((2, tb, RET_DK), BF16), pltpu.VMEM((2, tb, RET_DK), BF16),
                        pltpu.VMEM((2, tb, RET_DV), BF16), pltpu.SemaphoreType.DMA((2, 3)),
                        pltpu.VMEM((RET_GROUP, RET_DK, RET_DV), F32)],
        compiler_params=_cparams(("arbitrary", "arbitrary")),
    )(proj, proj, proj, cos, sin, intra, qd, kd, cd, states, dout, dproj)


def _ret_gate(out, proj, gn, name):
    def fn(o, g, *gains):
        g = g.astype(F32)
        parts = [_rms(o[:, h * RET_DV:(h + 1) * RET_DV], gains[h]) for h in range(RET_HEADS)]
        return (g * _sigmoid(g) * jnp.concatenate(parts, axis=-1),)
    w = RET_HEADS * RET_DV
    return _rows(fn, [out, (proj, w, 2)], [gn[h:h + 1] for h in range(RET_HEADS)], [(w, BF16)], name=name,
                 tile=1024)[0]


def _ret_gate_bwd(out, proj, gn, dy, name):
    def fn(o, g, d, *gains):
        g = g.astype(F32)
        sg = _sigmoid(g)
        silu = g * sg
        dsilu = sg * (1.0 + g * (1.0 - sg))
        dos, dgs = [], []
        row = lax.broadcasted_iota(jnp.int32, (RET_HEADS, RET_DV), 0)
        dgn = jnp.zeros((RET_HEADS, RET_DV), F32)
        for h in range(RET_HEADS):
            sl = slice(h * RET_DV, (h + 1) * RET_DV)
            oh = o[:, sl]
            dgs.append(d[:, sl] * _rms(oh, gains[h]) * dsilu[:, sl])
            dx, dg = _rms_bwd(oh, d[:, sl] * silu[:, sl], gains[h])
            dos.append(dx)
            dgn = dgn + jnp.where(row == h, _colsum(dg), 0.0)
        return jnp.concatenate(dos, axis=-1), jnp.concatenate(dgs, axis=-1), dgn
    w = RET_HEADS * RET_DV
    return _rows(fn, [out, (proj, w, 2), dy], [gn[h:h + 1] for h in range(RET_HEADS)],
                 [(w, BF16), (w, BF16, proj.shape[1], 2)], [((RET_HEADS, RET_DV), F32)], name=name, tile=256)


def _mla_tables(T):
    ang = _rope_angles(T, MLA_ROPE)
    c, s = np.cos(ang), np.sin(ang)
    z32, z64 = np.zeros((T, 32), np.float32), np.zeros((T, 64), np.float32)
    cos_t = np.concatenate([c, c, z64], axis=1)
    sin_a = np.concatenate([-s, z32, z64], axis=1)
    sin_b = np.concatenate([z32, s, z64], axis=1)
    return tuple(jnp.asarray(t, F32) for t in (cos_t, sin_a, sin_b))


def _rope_blk(x, ct, sa, sb):
    return x * ct + pltpu.roll(x, 96, 1) * sa + pltpu.roll(x, 32, 1) * sb


def _rope_blk_bwd(d, ct, sa, sb):
    return d * ct + pltpu.roll(d * sa, 32, 1) + pltpu.roll(d * sb, 96, 1)


def _head_norm(x, gain):
    r = lax.rsqrt(_rowsum(x * x, True) / MLA_QKD + EPS)
    return (x * r) * gain


def _prep_heads(qv, kvv, kr, ct, sa, sb, gqv, gkv):
    qs, ks, vs = [], [], []
    for h in range(MLA_HEADS):
        b = h * MLA_HP
        y = _head_norm(qv[:, b:b + MLA_HP], gqv)
        qs += [y[:, :128], _rope_blk(y[:, 128:], ct, sa, sb)]
        y = _head_norm(jnp.concatenate([kvv[:, b:b + 128], kr], axis=-1), gkv)
        ks += [y[:, :128], _rope_blk(y[:, 128:], ct, sa, sb)]
        vs.append(kvv[:, b + 128:b + 256])
    return jnp.concatenate(qs, axis=-1), jnp.concatenate(ks, axis=-1), jnp.concatenate(vs, axis=-1)


def _mla_front(hn, W, tabs, name):
    wide = MLA_HEADS * MLA_HP
    gq = W["mla_q_norm"] * (MLA_QKD ** -0.5 * LOG2E)

    def epilogue(acc, ct, sa, sb, gqa, gkva, wuq, wukv, gqv, gkv):
        cqn = _rms(acc[:, :MLA_Q_RANK], gqa).astype(BF16)
        ckvn = _rms(acc[:, MLA_Q_RANK:MLA_Q_RANK + MLA_KV_RANK], gkva).astype(BF16)
        q = jnp.concatenate([_dot(cqn, wuq[s]) for s in range(N_CHIPS)], axis=-1).astype(BF16)
        kv = jnp.concatenate([_dot(ckvn, wukv[s]) for s in range(N_CHIPS)], axis=-1).astype(BF16)
        qf, kf, vf = _prep_heads(q.astype(F32), kv.astype(F32), acc[:, MLA_IN_PAD - 128:], ct, sa, sb, gqv, gkv)
        return acc, cqn, ckvn, q, kv, qf, kf, vf

    return _mm_rows(hn, W["mla_w_in"], extras=list(tabs),
                    fulls=[W["mla_q_a_norm"], W["mla_kv_a_norm"], W["mla_w_uq"], W["mla_w_ukv"], gq, W["mla_k_norm"]],
                    outs=[(MLA_IN_PAD, F32), (MLA_Q_RANK, BF16), (MLA_KV_RANK, BF16), (wide, BF16), (wide, BF16),
                          (wide, BF16), (wide, BF16), (MLA_HEADS * MLA_VD, BF16)],
                    epilogue=epilogue, name=name, tm=256)


def _prep_heads_bwd(qv, kvv, kr, ct, sa, sb, dqv, dkv, dvv, gqv, gkv):
    dqs, dkvs = [], []
    dkr = jnp.zeros_like(kr)
    dgq = jnp.zeros((1, MLA_HP), F32)
    dgk = jnp.zeros((1, MLA_HP), F32)
    for h in range(MLA_HEADS):
        b = h * MLA_HP
        dy = jnp.concatenate([dqv[:, b:b + 128], _rope_blk_bwd(dqv[:, b + 128:b + 256], ct, sa, sb)], axis=-1)
        dx, dg = _rms_bwd(qv[:, b:b + MLA_HP], dy, gqv, MLA_QKD, mxu=True)
        dqs.append(dx)
        dgq = dgq + _colsum(dg)
        dy = jnp.concatenate([dkv[:, b:b + 128], _rope_blk_bwd(dkv[:, b + 128:b + 256], ct, sa, sb)], axis=-1)
        dx, dg = _rms_bwd(jnp.concatenate([kvv[:, b:b + 128], kr], axis=-1), dy, gkv, MLA_QKD, mxu=True)
        dkvs += [dx[:, :128], dvv[:, h * MLA_VD:(h + 1) * MLA_VD].astype(F32)]
        dkr = dkr + dx[:, 128:]
        dgk = dgk + _colsum(dg)
    return jnp.concatenate(dqs, axis=-1), jnp.concatenate(dkvs, axis=-1), dkr, dgq, dgk


def _mla_back(q, kv, proj, h0, dh1, dqf, dkf, dvf, W, tabs, name):
    def fn(qv, kvv, pv, hv, dr, ct, sa, sb, dqv, dkv, dvv, gqv, gkv, gqa, gkva, wuq, wukv, w_in, g_mix):
        qv, kvv, dqv, dkv = (t.astype(F32) for t in (qv, kvv, dqv, dkv))
        dq, dkvx, dkr, dgq, dgk = _prep_heads_bwd(qv, kvv, pv[:, MLA_IN_PAD - 128:], ct, sa, sb, dqv, dkv, dvv, gqv, gkv)
        dq, dkvx = dq.astype(BF16), dkvx.astype(BF16)
        nq = wuq.shape[2]
        dcq = sum(_dot_nt(dq[:, s * nq:(s + 1) * nq], wuq[s]) for s in range(N_CHIPS))
        dckv = sum(_dot_nt(dkvx[:, s * nq:(s + 1) * nq], wukv[s]) for s in range(N_CHIPS))
        dxq, dgqa = _rms_bwd(pv[:, :MLA_Q_RANK], dcq, gqa)
        dxkv, dgkva = _rms_bwd(pv[:, MLA_Q_RANK:MLA_Q_RANK + MLA_KV_RANK], dckv, gkva)
        dproj = jnp.concatenate([dxq, dxkv, dkr], axis=-1).astype(BF16)
        dx, dgm = _rms_bwd(hv, _dot_nt(dproj, w_in), g_mix)
        return (dq, dkvx, dproj, dr + dx, dr + dx, dgq, dgk, _colsum(dgqa), _colsum(dgkva), _colsum(dgm))

    wide = MLA_HEADS * MLA_HP
    return _rows(fn, [q, kv, proj, h0, dh1, *tabs, dqf, dkf, dvf],
                 [W["mla_q_norm"], W["mla_k_norm"], W["mla_q_a_norm"], W["mla_kv_a_norm"], W["mla_w_uq"], W["mla_w_ukv"],
                  W["mla_w_in"], W["mix_norm"][1:2]],
                 [(wide, BF16), (wide, BF16), (MLA_IN_PAD, BF16), ROW_F32, ROW_BF16],
                 [((1, MLA_HP), F32), ((1, MLA_HP), F32), ((1, MLA_Q_RANK), F32), ((1, MLA_KV_RANK), F32),
                  ((1, D_MODEL), F32)], name=name, tile=256)


def _chunk_mask(qi, ki, tq, tk):
    shift = CHUNK.bit_length() - 1
    rq = lax.shift_right_arithmetic(qi * tq + lax.broadcasted_iota(jnp.int32, (tq, tk), 0), shift)
    ck = lax.shift_right_arithmetic(ki * tk + lax.broadcasted_iota(jnp.int32, (tq, tk), 1), shift)
    return ck <= rq


def _flash_fwd(qf, kf, vf, name):
    T = qf.shape[0]
    t = _pick(T, FLASH_T)
    n = T // t
    g = FLASH_HEADS

    def body(q_ref, k_ref, v_ref, o_ref, lse_ref, m_s, l_s, acc):
        qi = pl.program_id(1)
        m_s[...] = jnp.full_like(m_s, NEG)
        l_s[...] = jnp.zeros_like(l_s)
        acc[...] = jnp.zeros_like(acc)

        def step(kb, masked):
            rows = pl.ds(pl.multiple_of(kb * t, t), t)
            for h in range(g):
                hq, hv = slice(h * MLA_HP, (h + 1) * MLA_HP), slice(h * MLA_VD, (h + 1) * MLA_VD)
                s = _dot_nt(q_ref[:, hq], k_ref[rows, hq])
                if masked:
                    s = jnp.where(_chunk_mask(0, 0, t, t), s, NEG)
                m_prev = m_s[:, hv]
                m_new = jnp.maximum(m_prev, jnp.max(s, axis=-1, keepdims=True))
                alpha = jnp.exp2(m_prev - m_new)
                p = jnp.exp2(s - _widen(m_new, t))
                l_s[:, hv] = alpha * l_s[:, hv] + sum(p[:, i * 128:(i + 1) * 128] for i in range(t // 128))
                acc[:, hv] = acc[:, hv] * alpha + _dot(p.astype(BF16), v_ref[rows, hv])
                m_s[:, hv] = m_new

        @pl.loop(0, qi)
        def _(kb):
            step(kb, False)

        step(qi, True)
        for h in range(g):
            hv = slice(h * MLA_VD, (h + 1) * MLA_VD)
            l = jnp.sum(l_s[:, hv], axis=-1, keepdims=True)
            o_ref[:, hv] = acc[:, hv] / l
            lse_ref[:, hv] = m_s[:, hv] + jnp.log2(l)

    qmap = lambda h, i: (i, h)
    kmap = lambda h, i: (0, h)
    vec = pltpu.VMEM((t, g * MLA_VD), F32)
    return pl.pallas_call(
        body, name=name, grid=(MLA_HEADS // g, n),
        in_specs=[pl.BlockSpec((t, g * MLA_HP), qmap), pl.BlockSpec((T, g * MLA_HP), kmap),
                  pl.BlockSpec((T, g * MLA_VD), kmap)],
        out_specs=[pl.BlockSpec((t, g * MLA_VD), qmap), pl.BlockSpec((t, g * MLA_VD), qmap)],
        out_shape=[jax.ShapeDtypeStruct((T, MLA_HEADS * MLA_VD), F32),
                   jax.ShapeDtypeStruct((T, MLA_HEADS * MLA_VD), F32)],
        scratch_shapes=[vec, vec, vec],
        compiler_params=_cparams(("parallel", "arbitrary")),
    )(qf, kf, vf)


def _flash_bwd(qf, kf, vf, do16, lse, delta, name):
    T = qf.shape[0]
    t = _pick(T, FLASH_T)
    n = T // t
    scale = MLA_QKD ** -0.5

    def body(q_ref, k_ref, v_ref, do_ref, lse_ref, dl_ref, dq_out, dk_out, dv_out, dq_ref, dk_ref, dv_ref):
        kb = pl.program_id(1)

        @pl.when(kb == 0)
        def _():
            dq_ref[...] = jnp.zeros_like(dq_ref)

        dk_ref[...] = jnp.zeros_like(dk_ref)
        dv_ref[...] = jnp.zeros_like(dv_ref)
        k, v = k_ref[...], v_ref[...]

        def step(qb, masked):
            rows = pl.ds(pl.multiple_of(qb * t, t), t)
            q, dob = q_ref[rows, :], do_ref[rows, :]
            s = _dot_nt(q, k)
            if masked:
                s = jnp.where(_chunk_mask(0, 0, t, t), s, NEG)
            p = jnp.exp2(s - _widen(lse_ref[rows, :], t))
            ds = (p * (_dot_nt(dob, v) - _widen(dl_ref[rows, :], t))).astype(BF16)
            dv_ref[...] += _dot_tn(p.astype(BF16), dob)
            dk_ref[...] += _dot_tn(ds, q)
            dq_ref[rows, :] += _dot(ds, k)

        step(kb, True)

        @pl.loop(kb + 1, n)
        def _(qb):
            step(qb, False)

        dk_out[...] = (dk_ref[...] * (1.0 / LOG2E)).astype(BF16)
        dv_out[...] = dv_ref[...].astype(BF16)

        @pl.when(kb == n - 1)
        def _():
            dq_out[...] = (dq_ref[...] * scale).astype(BF16)

    qmap = lambda h, j: (0, h)
    kmap = lambda h, j: (j, h)
    return pl.pallas_call(
        body, name=name, grid=(MLA_HEADS, n),
        in_specs=[pl.BlockSpec((T, MLA_HP), qmap), pl.BlockSpec((t, MLA_HP), kmap), pl.BlockSpec((t, MLA_VD), kmap),
                  pl.BlockSpec((T, MLA_VD), qmap), pl.BlockSpec((T, MLA_VD), qmap), pl.BlockSpec((T, MLA_VD), qmap)],
        out_specs=[pl.BlockSpec((T, MLA_HP), qmap), pl.BlockSpec((t, MLA_HP), kmap), pl.BlockSpec((t, MLA_VD), kmap)],
        out_shape=[jax.ShapeDtypeStruct((T, MLA_HEADS * MLA_HP), BF16),
                   jax.ShapeDtypeStruct((T, MLA_HEADS * MLA_HP), BF16),
                   jax.ShapeDtypeStruct((T, MLA_HEADS * MLA_VD), BF16)],
        scratch_shapes=[pltpu.VMEM((T, MLA_HP), F32), pltpu.VMEM((t, MLA_HP), F32), pltpu.VMEM((t, MLA_VD), F32)],
        compiler_params=_cparams(("arbitrary", "arbitrary")),
    )(qf, kf, vf, do16, lse, delta)


MESH = pl.DeviceIdType.MESH
ANY = pl.BlockSpec(memory_space=pl.ANY)
_CHIP_FLIPS = ((1, 0), (0, 1), (1, 1))


def _place():
    return lax.axis_index("x"), lax.axis_index("y"), lax.axis_index("c")


def _other_chip(x, y, k):
    fx, fy = _CHIP_FLIPS[k]
    return ((1 - x) if fx else x), ((1 - y) if fy else y)


def _remote(src, dst, send_sems, recv_sems, k, to):
    return pltpu.make_async_remote_copy(src_ref=src, dst_ref=dst, send_sem=send_sems.at[k], recv_sem=recv_sems.at[k],
                                        device_id=to, device_id_type=MESH)


def _index(*vals):
    return jnp.stack(vals).astype(jnp.int32)


def _half(c, rows):
    return pl.ds(pl.multiple_of(c * rows, 16), rows)


def _gather_weights(parts, name, landed=None):
    n_w = len(parts)
    n_in = n_w if landed is None else 2 * n_w

    def body(*refs):
        ins, outs = refs[:n_w], refs[n_in:n_in + n_w]
        send_sems, recv_sems, local_sems = refs[n_in + n_w:]
        x, y, c = _place()
        j = 2 * x + y
        sibling = (x, y, 1 - c)
        chips = [_other_chip(x, y, k) for k in range(3)]
        pending = []
        for w in range(n_w):
            own = pltpu.make_async_copy(ins[w], outs[w].at[j], local_sems.at[w])
            own.start()
            pending.append(own)
        sent = []
        for w in range(n_w):
            if landed is not None:
                break
            r = _half(c, parts[w].shape[0] // 2)
            for k, (px, py) in enumerate(chips):
                cp = _remote(ins[w].at[r], outs[w].at[j, r], send_sems, recv_sems, 6 * w + k, (px, py, c))
                cp.start()
                sent.append(cp)
        for w in range(n_w):
            r = _half(c, parts[w].shape[0] // 2)
            for k, (px, py) in enumerate(chips):
                blk = outs[w].at[2 * px + py, r]
                if landed is None:
                    _remote(blk, blk, send_sems, recv_sems, 6 * w + k, (px, py, c)).wait_recv()
                cp = _remote(blk, blk, send_sems, recv_sems, 6 * w + 3 + k, sibling)
                cp.start()
                sent.append(cp)
        for w in range(n_w):
            r = _half(1 - c, parts[w].shape[0] // 2)
            for k, (px, py) in enumerate(chips):
                blk = outs[w].at[2 * px + py, r]
                _remote(blk, blk, send_sems, recv_sems, 6 * w + 3 + k, sibling).wait_recv()
        for cp in sent:
            cp.wait_send()
        for cp in pending:
            cp.wait()

    return pl.pallas_call(
        body, name=name, in_specs=[pl.BlockSpec(memory_space=pltpu.VMEM)] * n_w + [ANY] * (n_in - n_w),
        out_specs=[ANY] * n_w,
        out_shape=[jax.ShapeDtypeStruct((N_CHIPS, *p.shape), p.dtype) for p in parts],
        input_output_aliases={} if landed is None else {n_w + w: w for w in range(n_w)},
        scratch_shapes=[pltpu.SemaphoreType.DMA((6 * n_w,)), pltpu.SemaphoreType.DMA((6 * n_w,)),
                        pltpu.SemaphoreType.DMA((n_w,))],
        compiler_params=pltpu.CompilerParams(vmem_limit_bytes=VMEM_LIMIT),
    )(*parts, *(landed or []))


def _swap_halves(gs, name):
    n_w = len(gs)

    def body(*refs):
        g_refs, recv_refs = refs[:n_w], refs[n_w:2 * n_w]
        send_sems, recv_sems = refs[2 * n_w:]
        x, y, c = _place()
        sent = []
        for w in range(n_w):
            for jj in range(N_CHIPS):
                cp = _remote(g_refs[w].at[jj, 1 - c], recv_refs[w].at[jj], send_sems, recv_sems, N_CHIPS * w + jj,
                             (x, y, 1 - c))
                cp.start()
                sent.append(cp)
        for cp in sent:
            cp.wait()

    return pl.pallas_call(
        body, name=name, in_specs=[ANY] * n_w, out_specs=[ANY] * n_w,
        out_shape=[jax.ShapeDtypeStruct((N_CHIPS, *g.shape[2:]), g.dtype) for g in gs],
        scratch_shapes=[pltpu.SemaphoreType.DMA((N_CHIPS * n_w,)), pltpu.SemaphoreType.DMA((N_CHIPS * n_w,))],
    )(*gs)


def _pair_sum(g, recv, core, name):
    _, H, C = recv.shape
    tile = _pick(H, SUM_ROWS)

    def body(c_ref, own_ref, recv_ref, out_ref):
        out_ref[...] = (own_ref[...].astype(F32) + recv_ref[...].astype(F32)).astype(BF16)

    blk = pl.BlockSpec((None, tile, C), lambda jj, i, c: (jj, i, 0))
    return pl.pallas_call(
        body, name=name,
        grid_spec=pltpu.PrefetchScalarGridSpec(
            num_scalar_prefetch=1, grid=(N_CHIPS, H // tile),
            in_specs=[pl.BlockSpec((None, None, tile, C), lambda jj, i, c: (jj, c[0], i, 0)), blk],
            out_specs=blk),
        out_shape=jax.ShapeDtypeStruct((N_CHIPS, H, C), BF16),
        compiler_params=_cparams(("arbitrary", "arbitrary")),
    )(_index(core), g, recv)


def _chip_sum(g, recv, got, chip, core, name):
    _, H, C = recv.shape
    tile = _pick(H, SUM_ROWS)

    def body(s_ref, own_ref, recv_ref, g0_ref, g1_ref, g2_ref, out_ref):
        pair = own_ref[...].astype(F32) + recv_ref[...].astype(F32)
        out_ref[...] = ((pair + g0_ref[...].astype(F32)) + g1_ref[...].astype(F32)) + g2_ref[...].astype(F32)

    def got_spec(k):
        return pl.BlockSpec((None, tile, C), lambda i, s, k=k: (k, i, 0))

    return pl.pallas_call(
        body, name=name,
        grid_spec=pltpu.PrefetchScalarGridSpec(
            num_scalar_prefetch=1, grid=(H // tile,),
            in_specs=[pl.BlockSpec((None, None, tile, C), lambda i, s: (s[0], s[1], i, 0)),
                      pl.BlockSpec((None, tile, C), lambda i, s: (s[0], i, 0)), got_spec(0), got_spec(1), got_spec(2)],
            out_specs=pl.BlockSpec((None, tile, C), lambda i, s: (s[1], i, 0))),
        out_shape=jax.ShapeDtypeStruct((2, H, C), F32),
        compiler_params=_cparams(("arbitrary",)),
    )(_index(chip, core), g, recv, got, got, got)


def _share_halves(reds):
    n_w = len(reds)

    def body(*refs):
        out_refs = refs[n_w:2 * n_w]
        send_sems, recv_sems = refs[2 * n_w:]
        x, y, c = _place()
        sent = []
        for w in range(n_w):
            blk = out_refs[w].at[c]
            cp = _remote(blk, blk, send_sems, recv_sems, w, (x, y, 1 - c))
            cp.start()
            sent.append(cp)
        for cp in sent:
            cp.wait()

    return pl.pallas_call(
        body, name="grad_share_halves", in_specs=[ANY] * n_w, out_specs=[ANY] * n_w,
        out_shape=[jax.ShapeDtypeStruct(r.shape, r.dtype) for r in reds],
        input_output_aliases={w: w for w in range(n_w)},
        scratch_shapes=[pltpu.SemaphoreType.DMA((n_w,)), pltpu.SemaphoreType.DMA((n_w,))],
    )(*reds)


def _allsum_small(v, name):
    R, W = v.shape
    n_dev = 8
    vm = pl.BlockSpec(memory_space=pltpu.VMEM)

    def body(v_ref, out_ref, buf, send_sems, recv_sems):
        x, y, c = _place()
        me = 4 * x + 2 * y + c
        buf[me] = v_ref[...]
        sent = []
        for k in range(1, n_dev):
            peer = ((1 - x) if k & 4 else x, (1 - y) if k & 2 else y, (1 - c) if k & 1 else c)
            cp = _remote(v_ref, buf.at[me], send_sems, recv_sems, k - 1, peer)
            cp.start()
            sent.append(cp)
        for cp in sent:
            cp.wait_recv()
        for cp in sent:
            cp.wait_send()
        acc = buf[0]
        for q in range(1, n_dev):
            acc = acc + buf[q]
        out_ref[...] = acc

    return pl.pallas_call(
        body, name=name, in_specs=[vm], out_specs=vm, out_shape=jax.ShapeDtypeStruct((R, W), v.dtype),
        scratch_shapes=[pltpu.VMEM((n_dev, R, W), v.dtype), pltpu.SemaphoreType.DMA((n_dev - 1,)),
                        pltpu.SemaphoreType.DMA((n_dev - 1,))],
    )(v)


HBM = pl.BlockSpec(memory_space=pltpu.HBM)
SEM = pl.BlockSpec(memory_space=pltpu.SEMAPHORE)
_DATAFLOW = pltpu.SideEffectType.DATAFLOW_SIDE_EFFECTING


def _split_start(name, srcs, land_shapes, n_copies, copies, after=()):
    ns, nl = len(srcs), len(land_shapes)
    lands = [lax.empty(s.shape, s.dtype) for s in land_shapes]

    def body(*refs):
        outs = refs[ns + nl + len(after):]
        for cp in copies(refs[:ns], refs[ns:ns + nl], outs[0], outs[1]):
            cp.start()
        outs[-1][...] = jnp.zeros_like(outs[-1])

    sems = pltpu.SemaphoreType.DMA((n_copies,))
    res = pl.pallas_call(
        body, name=name, in_specs=[HBM] * (ns + nl) + [ANY] * len(after),
        out_specs=(SEM, SEM, *[HBM] * (ns + nl), pl.BlockSpec(memory_space=pltpu.VMEM)),
        out_shape=(sems, sems, *[pltpu.HBM(a.shape, a.dtype) for a in srcs],
                   *[pltpu.HBM(s.shape, s.dtype) for s in land_shapes], jax.ShapeDtypeStruct((8, 128), F32)),
        input_output_aliases={i: 2 + i for i in range(ns + nl)},
        compiler_params=pltpu.CompilerParams(has_side_effects=_DATAFLOW),
    )(*[pltpu.with_memory_space_constraint(a, pltpu.HBM) for a in [*srcs, *lands]], *after)
    return res[0], res[1], list(res[2:2 + ns]), list(res[2 + ns:2 + ns + nl]), res[-1]


def _split_wait(name, send_sems, recv_sems, srcs, lands, copies, after=()):
    ns, nl = len(srcs), len(lands)

    def body(*refs):
        for cp in copies(refs[:ns], refs[ns:ns + nl], refs[ns + nl], refs[ns + nl + 1]):
            cp.wait_send()
            cp.wait_recv()

    res = pl.pallas_call(
        body, name=name, in_specs=[HBM] * (ns + nl) + [SEM, SEM] + [ANY] * len(after), out_specs=[HBM] * (ns + nl),
        out_shape=[pltpu.HBM(a.shape, a.dtype) for a in [*srcs, *lands]],
        input_output_aliases={i: i for i in range(ns + nl)},
        compiler_params=pltpu.CompilerParams(has_side_effects=_DATAFLOW),
    )(*srcs, *lands, send_sems, recv_sems, *after)
    return list(res[ns:])


def _gather_copies(rows):
    def copies(src_refs, land_refs, send_sems, recv_sems):
        x, y, c = _place()
        j = 2 * x + y
        out = []
        for w in range(len(src_refs)):
            r = _half(c, rows[w] // 2)
            for k in range(3):
                px, py = _other_chip(x, y, k)
                out.append(_remote(src_refs[w].at[r], land_refs[w].at[j, r], send_sems, recv_sems, 3 * w + k, (px, py, c)))
        return out
    return copies


def _scatter_copies(src_refs, land_refs, send_sems, recv_sems):
    x, y, c = _place()
    j = 2 * x + y
    out = []
    for w in range(len(src_refs)):
        for k in range(3):
            px, py = _other_chip(x, y, k)
            pj = 2 * px + py
            out.append(_remote(src_refs[w].at[pj], land_refs[w].at[(j - pj + 4) % 4 - 1], send_sems, recv_sems, 3 * w + k,
                               (px, py, c)))
    return out


def _reduce_begin(grads, core, tag):
    names = list(grads)
    gs = [grads[k].reshape(N_CHIPS, 2, -1, grads[k].shape[-1]) for k in names]
    recvs = _swap_halves(gs, f"grad_swap_halves_{tag}")
    sums = [_pair_sum(g, r, core, f"pair_sum_{k}") for k, g, r in zip(names, gs, recvs)]
    return names, gs, recvs, sums


def _reduce_end(begun, gots, chip, core):
    names, gs, recvs, _ = begun
    return {k: _chip_sum(g, r, t, chip, core, f"chip_sum_{k}") for k, g, r, t in zip(names, gs, recvs, gots)}


def _got_shapes(sums):
    return [jax.ShapeDtypeStruct((3, *a.shape[1:]), a.dtype) for a in sums]


def _adamw(w, g, m, v, name, layers=1, layer=0, into=None):
    shape = w.shape
    cols = shape[-1]
    w3, m3, v3 = (t.reshape(layers, -1, cols) for t in (w, m, v))
    rows = w3.shape[1]
    tile = _pick(rows, ADAM_ROWS if cols <= 1024 else ADAM_ROWS // 2) if rows % 8 == 0 else rows
    n_in = 4 + (0 if into is None else 4)
    stack_g = layers > 1

    def body(*refs):
        wv, gv, mv, vv = (r[...] for r in refs[:4])
        d_ref, m_ref, v_ref = refs[len(refs) - 3:]
        m2 = ADAM_B1 * mv + (1.0 - ADAM_B1) * gv
        v2 = ADAM_B2 * vv + (1.0 - ADAM_B2) * jnp.square(gv)
        m_hat = m2 / (1.0 - ADAM_B1 ** ADAM_STEP)
        v_hat = v2 / (1.0 - ADAM_B2 ** ADAM_STEP)
        if stack_g:
            refs[n_in][...] = gv
        d_ref[...] = -ADAM_LR * (m_hat / (jnp.sqrt(v_hat) + ADAM_EPS) + ADAM_WD * wv)
        m_ref[...] = m2
        v_ref[...] = v2

    n_out = 4 if stack_g else 3
    lay = pl.BlockSpec((None, tile, cols), lambda i: (layer, i, 0))
    out = jax.ShapeDtypeStruct((layers, rows, cols), F32)
    res = pl.pallas_call(
        body, name=name, grid=(rows // tile,),
        in_specs=[lay, pl.BlockSpec((tile, cols), lambda i: (i, 0)), lay, lay] + [ANY] * (n_in - 4),
        out_specs=[lay] * n_out, out_shape=[out] * n_out,
        input_output_aliases={} if into is None else {4 + k: k for k in range(4)},
        compiler_params=_cparams(("arbitrary",)),
    )(w3, g.reshape(rows, cols), m3, v3, *([] if into is None else [t.reshape(layers, rows, cols) for t in into]))
    res = tuple(t.reshape(shape) for t in res)
    return res if stack_g else (g.reshape(shape), *res)


ROW_F32, ROW_BF16 = (D_MODEL, F32), (D_MODEL, BF16)


def _res_norm(acc, h, gain):
    hh = h + acc
    return hh, _rms(hh, gain)


def _dx_norm_bwd(d, w, h, dres, gain, name, **kw):
    def epilogue(acc, hv, dr, g):
        dx, dg = _rms_bwd(hv, acc, g)
        return dr + dx, dr + dx, _colsum(dg)
    return _mm_rows(d, w, tb=True, extras=[h, dres], fulls=[gain], outs=[ROW_F32, ROW_BF16], accs=[((1, D_MODEL), F32)],
                    epilogue=epilogue, name=name, **kw)


def _tail_fwd(h1, hn2, p16, W, i, tag, next_gain=None, target=None):
    a = _mm(hn2, W["mlp_w1"][i], bblk=True, outs=[BF16], name=f"{tag}_mlp_w1", tm=2048, tn=1024,
            epilogue=lambda acc: (jnp.square(jnp.maximum(acc, 0.0)),))
    h2, hn3 = _mm_rows(a, W["mlp_w2"][i], extras=[h1], fulls=[W["ple_norm"][i:i + 1]], outs=[ROW_F32, ROW_BF16],
                       epilogue=_res_norm, name=f"{tag}_mlp_w2")
    def embed(acc, pv, h, wp):
        gate = _sigmoid(acc)
        ppv = jnp.concatenate([_dot(pv, wp[s]) for s in range(N_CHIPS)], axis=-1)
        return gate, ppv, h + gate * ppv

    if target is None:
        def gated(acc, pv, h, wp, gain):
            gate, ppv, hh = embed(acc, pv, h, wp)
            return hh, ppv, gate, _rms(hh, gain)
        h3, pp, gate, hn = _mm_rows(hn3, W["ple_gate_w"][i], extras=[p16[i], h2], fulls=[W["ple_proj_w"][i], next_gain],
                                    outs=[ROW_F32, ROW_BF16, ROW_BF16, ROW_BF16], epilogue=gated, name=f"{tag}_ple",
                                    tm=1024)
        return h3, hn, (h1, hn2, a, h2, hn3, gate, pp)

    def gated_loss(acc, pv, h, t, wp):
        gate, ppv, hh = embed(acc, pv, h, wp)
        e = hh - t
        return ppv, gate, e * (1.0 / D_MODEL), jnp.full((1, 128), 0.5 / D_MODEL, F32) * jnp.sum(e * e)
    pp, gate, dy, loss = _mm_rows(hn3, W["ple_gate_w"][i], extras=[p16[i], h2, target], fulls=[W["ple_proj_w"][i]],
                                  outs=[ROW_BF16, ROW_BF16, ROW_F32], accs=[((1, 128), F32)], epilogue=gated_loss,
                                  name=f"{tag}_ple", tm=1024)
    return dy, loss, (h1, hn2, a, h2, hn3, gate, pp)


def _tail_bwd(dh3, saved, p16, W, i, tag, after=()):
    h1, hn2, a, h2, hn3, gate, pp = saved

    def embed_bwd(d, g, ppv, hv, wg, gain):
        g, ppv = g.astype(F32), ppv.astype(F32)
        dppv, dglv = (d * g).astype(BF16), (d * ppv * g * (1.0 - g)).astype(BF16)
        dx, dg = _rms_bwd(hv, _dot_nt(dglv, wg), gain)
        return dppv, dglv, d + dx, d + dx, _colsum(dg)

    def dw(kind, name):
        return (kind, 1, 0, None)

    dpp, dgl, dh2, dh2_16, d_ple_norm = _rows(
        embed_bwd, [dh3, gate, pp, h2], [W["ple_gate_w"][i], W["ple_norm"][i:i + 1]],
        [ROW_BF16, ROW_BF16, ROW_F32, ROW_BF16], [((1, D_MODEL), F32)], name=f"{tag}_ple_bwd", after=after)
    d_proj = _mm(p16[i], dpp, ta=True, outs=[BF16], dw=dw("cols", "ple_proj_w"), name=f"{tag}_d_ple_proj")
    d_gate = _mm(hn3, dgl, ta=True, outs=[BF16], dw=dw("rows", "ple_gate_w"), name=f"{tag}_d_ple_gate")
    d_w2 = _mm(a, dh2_16, ta=True, outs=[BF16], dw=dw("rows", "mlp_w2"), name=f"{tag}_d_mlp_w2", tn=1024)
    dz = _mm(dh2_16, W["mlp_w2"][i], tb=True, extras=[a], outs=[BF16], name=f"{tag}_mlp_w2_dx", tm=2048, tn=1024,
             epilogue=lambda acc, av: (acc * (2.0 * jnp.sqrt(av.astype(F32))),))
    d_w1 = _mm(hn2, dz, ta=True, outs=[BF16], dw=dw("cols", "mlp_w1"), name=f"{tag}_d_mlp_w1", tn=1024)
    dh1, dh1_16, d_mlp_norm = _dx_norm_bwd(dz, W["mlp_w1"][i], h1, dh2, W["mlp_norm"][i:i + 1], f"{tag}_mlp_w1_dx",
                                           bblk=True)
    big = {f"mlp_w1_{i}": d_w1, f"mlp_w2_{i}": d_w2, f"ple_gate_w_{i}": d_gate, f"ple_proj_w_{i}": d_proj}
    return dh1, dh1_16, big, dict(mlp_norm=d_mlp_norm, ple_norm=d_ple_norm)


def _ret_layer_fwd(h0, W, tabs, after=()):
    hn = _rows(lambda x, g: (_rms(x, g),), [h0], [W["mix_norm"][0:1]], [(D_MODEL, BF16)], name="ret_mix_norm",
               tile=1024, after=after)[0]
    proj = _mm(hn, W["ret_w_in"], bblk=True, outs=[BF16], name="ret_w_in", tm=2048, tn=768)
    out, states = _ret_fwd(proj, tabs, "ret_scan")
    y = _ret_gate(out, proj, W["ret_gn"], "ret_gate")
    h1, hn2 = _mm_rows(y, W["ret_w_out"], extras=[h0], fulls=[W["mlp_norm"][0:1]], outs=[ROW_F32, ROW_BF16],
                       epilogue=_res_norm, name="ret_w_out", tm=1024)
    return h1, hn2, (h0, hn, proj, out, states, y)


def _d_ret_w_out(dh1_16, saved):
    return _mm(saved[5], dh1_16, ta=True, outs=[BF16], dw=("rows", 1, 0, None), name="d_ret_w_out")


def _ret_layer_bwd(dh1, dh1_16, saved, W, tabs, after=(), on_grads=None, d_w_out=None):
    h0, hn, proj, out, states, y = saved
    d_w_out = _d_ret_w_out(dh1_16, saved) if d_w_out is None else d_w_out
    dy = _mm(dh1_16, W["ret_w_out"], tb=True, name="ret_w_out_dx", tn=1024, after=after)
    dout, dproj, d_gn = _ret_gate_bwd(out, proj, W["ret_gn"], dy, "ret_gate_bwd")
    dproj = _ret_bwd(proj, states, dout, dproj, tabs, "ret_scan_bwd")
    d_w_in = _mm(hn, dproj, ta=True, outs=[BF16], dw=("cols", 1, 0, None), name="d_ret_w_in", tn=768)
    big = dict(ret_w_in=d_w_in, ret_w_out=d_w_out)
    later = () if on_grads is None else on_grads(big)
    dh0, _, d_mix = _dx_norm_bwd(dproj, W["ret_w_in"], h0, dh1, W["mix_norm"][0:1], "ret_w_in_dx", bblk=True, tm=256,
                                 after=later)
    return dh0, big, dict(mix_norm=d_mix, ret_gn=d_gn)


def _mla_layer_fwd(h0, hn, W, tabs):
    proj, cqn, ckvn, q, kv, qf, kf, vf = _mla_front(hn, W, tabs, "mla_front")
    o, lse = _flash_fwd(qf, kf, vf, "mla_flash")
    h1, hn2 = _mm_rows(o, W["mla_w_out"], extras=[h0], fulls=[W["mlp_norm"][1:2]], outs=[ROW_F32, ROW_BF16],
                       epilogue=_res_norm, name="mla_w_out", tm=1024)
    return h1, hn2, (h0, hn, proj, cqn, ckvn, q, kv, qf, kf, vf, o, lse)


def _mla_layer_bwd(dh1, dh1_16, saved, W, tabs):
    h0, hn, proj, cqn, ckvn, q, kv, qf, kf, vf, o, lse = saved
    d_w_out = _mm(o, dh1_16, ta=True, outs=[BF16], dw=("rows", 1, 0, None), name="d_mla_w_out")
    def with_delta(acc, ov):
        parts = []
        for h in range(MLA_HEADS):
            sl = slice(h * MLA_VD, (h + 1) * MLA_VD)
            d = jnp.sum(acc[:, sl] * ov[:, sl], axis=-1, keepdims=True)
            parts.append(jnp.broadcast_to(d, (d.shape[0], MLA_VD)))
        return jnp.concatenate(parts, axis=-1), acc

    delta, do16 = _mm_rows(dh1_16, W["mla_w_out"], tb=True, extras=[o], outs=[ROW_F32, ROW_BF16], epilogue=with_delta,
                           name="mla_w_out_dx", tm=1024)
    dqf, dkf, dvf = _flash_bwd(qf, kf, vf, do16, lse, delta, "mla_flash_bwd")
    dq, dkv, dproj, dh0, dh0_16, d_gq, d_gk, d_gqa, d_gkva, d_mix = _mla_back(q, kv, proj, h0, dh1, dqf, dkf, dvf, W, tabs,
                                                                              "mla_back")
    d_w_uq = _mm(cqn, dq, ta=True, outs=[BF16], dw=("cols", 1, 0, None), name="d_mla_w_uq")
    d_w_ukv = _mm(ckvn, dkv, ta=True, outs=[BF16], dw=("cols", 1, 0, None), name="d_mla_w_ukv")
    d_w_in = _mm(hn, dproj, ta=True, outs=[BF16], dw=("rows", 1, 0, None), name="d_mla_w_in")
    return (dh0, dh0_16, dict(mla_w_in=d_w_in, mla_w_uq=d_w_uq, mla_w_ukv=d_w_ukv, mla_w_out=d_w_out),
            dict(mix_norm=d_mix, mla_q_a_norm=d_gqa, mla_kv_a_norm=d_gkva, mla_q_norm=d_gq, mla_k_norm=d_gk))


def _small_grads(n_ret, n_t0, n_mla, n_t1):
    return dict(
        mix_norm=jnp.concatenate([n_ret["mix_norm"], n_mla["mix_norm"]], axis=0),
        mlp_norm=jnp.concatenate([n_t0["mlp_norm"], n_t1["mlp_norm"]], axis=0),
        ple_norm=jnp.concatenate([n_t0["ple_norm"], n_t1["ple_norm"]], axis=0),
        ret_gn=n_ret["ret_gn"], mla_q_a_norm=n_mla["mla_q_a_norm"], mla_kv_a_norm=n_mla["mla_kv_a_norm"],
        mla_q_norm=n_mla["mla_q_norm"], mla_k_norm=n_mla["mla_k_norm"])


_ORDER = ("mix_norm", "ret_w_in", "ret_gn", "ret_w_out", "mla_w_in", "mla_q_a_norm", "mla_kv_a_norm", "mla_w_uq",
          "mla_w_ukv", "mla_q_norm", "mla_k_norm", "mla_w_out", "mlp_norm", "mlp_w1", "mlp_w2", "ple_norm",
          "ple_gate_w", "ple_proj_w")
_TWO_LAYER = ("mlp_w1", "mlp_w2", "ple_gate_w", "ple_proj_w")
HEADS_PER_CHIP = MLA_HEADS // N_CHIPS
GAIN_ROWS = 32


def _travel_parts(w):
    uq = jnp.pad(w["mla_w_uq"][0].reshape(MLA_Q_RANK, HEADS_PER_CHIP, MLA_QKD), ((0, 0), (0, 0), (0, MLA_HP - MLA_QKD)))
    parts = {"ret_w_in": w["ret_w_in"][0], "ret_w_out": w["ret_w_out"][0]}
    for k in _TWO_LAYER:
        parts[k + "_0"] = w[k][0]
    parts["mla_w_in"] = jnp.pad(w["mla_w_in"][0], ((0, 0), (0, MLA_IN_PAD - MLA_IN)))
    parts["mla_w_uq"] = uq.reshape(MLA_Q_RANK, HEADS_PER_CHIP * MLA_HP)
    parts["mla_w_ukv"] = w["mla_w_ukv"][0]
    parts["mla_w_out"] = w["mla_w_out"][0]
    for k in _TWO_LAYER:
        parts[k + "_1"] = w[k][1]
    gains = jnp.concatenate([_pad_row(w["ret_gn"]), _pad_row(w["mla_q_a_norm"]), _pad_row(w["mla_kv_a_norm"]),
                             jnp.zeros((GAIN_ROWS - 3, PACK_W), F32)], axis=0)
    return {"gains": gains, **{k: v.astype(BF16) for k, v in parts.items()}}


def _full_weights(full):
    rows = lambda a: a.reshape(-1, a.shape[-1])
    W = {k: full[k] for k in ("ret_w_in", "mla_w_uq", "mla_w_ukv") if k in full}
    for k in ("ret_w_out", "mla_w_in", "mla_w_out"):
        if k in full:
            W[k] = rows(full[k])
    for k, by_rows in (("mlp_w1", False), ("ple_proj_w", False), ("mlp_w2", True), ("ple_gate_w", True)):
        layers = [full.get(f"{k}_{i}") for i in range(2)]
        W[k] = [rows(t) if (by_rows and t is not None) else t for t in layers]
    return W


def _shard_grad(name, red, shape):
    if name == "mla_w_in":
        red = red.reshape(-1, MLA_IN_PAD)[:, :MLA_IN]
    elif name == "mla_w_uq":
        red = red.reshape(MLA_Q_RANK, HEADS_PER_CHIP, MLA_HP)[:, :, :MLA_QKD]
    return red.reshape(shape)


def _pad_row(v):
    v = v.reshape(1, -1)
    return jnp.pad(v, ((0, 0), (0, PACK_W - v.shape[1])))


def kernel(x, p, mix_norm, ret_w_in, ret_gn, ret_w_out, mla_w_in, mla_q_a_norm, mla_kv_a_norm, mla_w_uq, mla_w_ukv, mla_q_norm, mla_k_norm, mla_w_out, mlp_norm, mlp_w1, mlp_w2, ple_norm, ple_gate_w, ple_proj_w, loss_target, m_mix_norm, m_ret_w_in, m_ret_gn, m_ret_w_out, m_mla_w_in, m_mla_q_a_norm, m_mla_kv_a_norm, m_mla_w_uq, m_mla_w_ukv, m_mla_q_norm, m_mla_k_norm, m_mla_w_out, m_mlp_norm, m_mlp_w1, m_mlp_w2, m_ple_norm, m_ple_gate_w, m_ple_proj_w, v_mix_norm, v_ret_w_in, v_ret_gn, v_ret_w_out, v_mla_w_in, v_mla_q_a_norm, v_mla_kv_a_norm, v_mla_w_uq, v_mla_w_ukv, v_mla_q_norm, v_mla_k_norm, v_mla_w_out, v_mlp_norm, v_mlp_w1, v_mlp_w2, v_ple_norm, v_ple_gate_w, v_ple_proj_w):
    w = dict(mix_norm=mix_norm, ret_w_in=ret_w_in, ret_gn=ret_gn, ret_w_out=ret_w_out, mla_w_in=mla_w_in,
             mla_q_a_norm=mla_q_a_norm, mla_kv_a_norm=mla_kv_a_norm, mla_w_uq=mla_w_uq, mla_w_ukv=mla_w_ukv,
             mla_q_norm=mla_q_norm, mla_k_norm=mla_k_norm, mla_w_out=mla_w_out, mlp_norm=mlp_norm, mlp_w1=mlp_w1,
             mlp_w2=mlp_w2, ple_norm=ple_norm, ple_gate_w=ple_gate_w, ple_proj_w=ple_proj_w)
    m = dict(mix_norm=m_mix_norm, ret_w_in=m_ret_w_in, ret_gn=m_ret_gn, ret_w_out=m_ret_w_out, mla_w_in=m_mla_w_in,
             mla_q_a_norm=m_mla_q_a_norm, mla_kv_a_norm=m_mla_kv_a_norm, mla_w_uq=m_mla_w_uq, mla_w_ukv=m_mla_w_ukv,
             mla_q_norm=m_mla_q_norm, mla_k_norm=m_mla_k_norm, mla_w_out=m_mla_w_out, mlp_norm=m_mlp_norm,
             mlp_w1=m_mlp_w1, mlp_w2=m_mlp_w2, ple_norm=m_ple_norm, ple_gate_w=m_ple_gate_w, ple_proj_w=m_ple_proj_w)
    v = dict(mix_norm=v_mix_norm, ret_w_in=v_ret_w_in, ret_gn=v_ret_gn, ret_w_out=v_ret_w_out, mla_w_in=v_mla_w_in,
             mla_q_a_norm=v_mla_q_a_norm, mla_kv_a_norm=v_mla_kv_a_norm, mla_w_uq=v_mla_w_uq, mla_w_ukv=v_mla_w_ukv,
             mla_q_norm=v_mla_q_norm, mla_k_norm=v_mla_k_norm, mla_w_out=v_mla_w_out, mlp_norm=v_mlp_norm,
             mlp_w1=v_mlp_w1, mlp_w2=v_mlp_w2, ple_norm=v_ple_norm, ple_gate_w=v_ple_gate_w, ple_proj_w=v_ple_proj_w)
    xi, yi, ci = _place()
    chip = 2 * xi + yi
    n = N_CHIPS

    parts = _travel_parts(w)
    first = ("gains", "ret_w_in", "ret_w_out")
    mid = [k + "_0" for k in _TWO_LAYER]
    last = [k for k in parts if k not in first and k not in mid]
    full = dict(zip(first, _gather_weights([parts[k] for k in first], "gather_first")))

    def gather_behind(names, tag, after):
        copies = _gather_copies([parts[k].shape[0] for k in names])
        started = _split_start(f"gather_{tag}_start", [parts[k] for k in names],
                               [jax.ShapeDtypeStruct((n, *parts[k].shape), BF16) for k in names], 3 * len(names),
                               copies, after=after)

        def arrive(after):
            landed = _split_wait(f"gather_{tag}_wait", *started[:4], copies, after=after)
            full.update(zip(names, _gather_weights([parts[k] for k in names], f"gather_{tag}_finish", landed=landed)))
            W.update(_full_weights(full))
        return started[4], arrive

    mid_token, mid_arrive = gather_behind(mid, "mid", [full["ret_w_in"]])
    g_token, last_arrive = gather_behind(last, "last", [mid_token])
    gains = full["gains"]
    W = dict(mix_norm=mix_norm, mlp_norm=mlp_norm, ple_norm=ple_norm,
             mla_q_norm=jnp.pad(mla_q_norm, ((0, 0), (0, MLA_HP - MLA_QKD))),
             mla_k_norm=jnp.pad(mla_k_norm, ((0, 0), (0, MLA_HP - MLA_QKD))),
             ret_w_in=full["ret_w_in"], ret_w_out=full["ret_w_out"].reshape(-1, D_MODEL),
             ret_gn=gains[:, 0, :RET_HEADS * 128].reshape(n, RET_HEADS, 128).transpose(1, 0, 2).reshape(RET_HEADS, RET_DV),
             mla_q_a_norm=gains[:, 1, :MLA_Q_RANK // n].reshape(1, MLA_Q_RANK),
             mla_kv_a_norm=gains[:, 2, :MLA_KV_RANK // n].reshape(1, MLA_KV_RANK))
    x0, p16, target = x[0], p[:, 0].astype(BF16), loss_target[0]
    T = x0.shape[0]
    ret_tabs, mla_tabs = _ret_tables(T), _mla_tables(T)

    h1, hn, s_ret = _ret_layer_fwd(x0, W, ret_tabs, after=[g_token])
    mid_arrive([h1])
    h3, hn, s_tail0 = _tail_fwd(h1, hn, p16, W, 0, "l0", next_gain=W["mix_norm"][1:2])
    last_arrive([h3])
    h4, hn, s_mla = _mla_layer_fwd(h3, hn, W, mla_tabs)
    dy, loss, s_tail1 = _tail_fwd(h4, hn, p16, W, 1, "l1", target=target)

    dh4, dh4_16, g_t1, n_t1 = _tail_bwd(dy, s_tail1, p16, W, 1, "l1")
    dh3, _, g_mla, n_mla = _mla_layer_bwd(dh4, dh4_16, s_mla, W, mla_tabs)
    beg_a = _reduce_begin({**g_mla, **g_t1}, ci, "a")
    a_send, a_recv, a_src, a_land, a_token = _split_start(
        "scatter_a_start", beg_a[3], _got_shapes(beg_a[3]), 3 * len(beg_a[3]), _scatter_copies)
    dh1, dh1_16, g_t0, n_t0 = _tail_bwd(dh3, s_tail0, p16, W, 0, "l0", after=[a_token])
    d_ret_w_out = _d_ret_w_out(dh1_16, s_ret)
    beg_b = _reduce_begin({**g_t0, "ret_w_out": d_ret_w_out}, ci, "b")
    b_send, b_recv, b_src, b_land, b_token = _split_start(
        "scatter_b_start", beg_b[3], _got_shapes(beg_b[3]), 3 * len(beg_b[3]), _scatter_copies)
    stage_c = {}

    def start_c(g_ret):
        beg = _reduce_begin({"ret_w_in": g_ret["ret_w_in"]}, ci, "c")
        stage_c["beg"] = beg
        stage_c["st"] = _split_start("scatter_c_start", beg[3], _got_shapes(beg[3]), 3 * len(beg[3]), _scatter_copies)
        return [stage_c["st"][4]]

    dx, _, n_ret = _ret_layer_bwd(dh1, dh1_16, s_ret, W, ret_tabs, after=[b_token], on_grads=start_c,
                                  d_w_out=d_ret_w_out)
    got_a = _split_wait("scatter_a_wait", a_send, a_recv, a_src, a_land, _scatter_copies, after=[dx])
    got_b = _split_wait("scatter_b_wait", b_send, b_recv, b_src, b_land, _scatter_copies, after=[dx])
    got_c = _split_wait("scatter_c_wait", *stage_c["st"][:4], _scatter_copies, after=[dx])
    red = {**_reduce_end(beg_a, got_a, chip, ci), **_reduce_end(beg_b, got_b, chip, ci),
           **_reduce_end(stage_c["beg"], got_c, chip, ci)}
    red = dict(zip(red, _share_halves(list(red.values()))))
    gs = _small_grads(n_ret, n_t0, n_mla, n_t1)
    small_g = jnp.concatenate([
        gs["mix_norm"], gs["mlp_norm"], gs["ple_norm"], gs["ret_gn"].reshape(2, PACK_W), _pad_row(gs["mla_q_a_norm"]),
        _pad_row(gs["mla_kv_a_norm"]), _pad_row(gs["mla_q_norm"][:, :MLA_QKD]), _pad_row(gs["mla_k_norm"][:, :MLA_QKD]),
        _pad_row(loss[:, :1]), jnp.zeros((3, PACK_W), F32)], axis=0)
    tot = _allsum_small(small_g, "sum_small_grads")
    gn_all = tot[6:8].reshape(RET_HEADS, n, -1)
    g_small = dict(
        mix_norm=tot[0:2], mlp_norm=tot[2:4], ple_norm=tot[4:6],
        ret_gn=lax.dynamic_index_in_dim(gn_all, chip, axis=1, keepdims=False),
        mla_q_a_norm=lax.dynamic_index_in_dim(tot[8, :MLA_Q_RANK].reshape(n, -1), chip, axis=0, keepdims=True),
        mla_kv_a_norm=lax.dynamic_index_in_dim(tot[9, :MLA_KV_RANK].reshape(n, -1), chip, axis=0, keepdims=True),
        mla_q_norm=tot[10:11, :MLA_QKD], mla_k_norm=tot[11:12, :MLA_QKD])
    loss_out = tot[12, 0]

    outs = []
    for k in _ORDER:
        if k in _TWO_LAYER:
            res = None
            for i in (1, 0):
                res = _adamw(w[k], red[f"{k}_{i}"], m[k], v[k], f"adamw_{k}_{i}", layers=2, layer=i, into=res)
        elif k in red:
            res = _adamw(w[k], _shard_grad(k, red[k], w[k].shape), m[k], v[k], f"adamw_{k}")
        else:
            res = _adamw(w[k], g_small[k], m[k], v[k], f"adamw_{k}")
        outs.append(res)
    return (loss_out, dx[None], *[o[0] for o in outs], *[o[1] for o in outs], *[o[2] for o in outs],
            *[o[3] for o in outs])
```

```python
import jax
import jax.numpy as jnp
import numpy as np
from jax import lax
from jax.experimental import pallas as pl
from jax.experimental.pallas import tpu as pltpu

F32 = jnp.float32
BF16 = jnp.bfloat16

EPS = 1e-6
D_MODEL = 1024
CHUNK = 64
ROPE_THETA = 10000.0
RET_HEADS = 4
RET_DK = 256
RET_DV = 512
RET_GROUP = 1
RET_BLOCK = 256
RET_ROWS = 1024
MLA_HEADS = 8
MLA_ROPE = 64
MLA_QKD = 192
MLA_VD = 128
MLA_HP = 256
MLA_Q_RANK = 384
MLA_KV_RANK = 256
MLA_IN = 704
MLA_IN_PAD = 768
N_CHIPS = 4

ADAM_LR = 0.001
ADAM_B1 = 0.9
ADAM_B2 = 0.999
ADAM_EPS = 1e-08
ADAM_WD = 0.01
ADAM_STEP = 10

VMEM_LIMIT = 56 * 1024 * 1024
PACK_W = 1024
NEG = -1e30
LOG2E = 1.4426950408889634
FLASH_T = 512
FLASH_HEADS = 2
MM_SUB_ROWS = 256
SUM_ROWS = 512
ADAM_ROWS = 512


def _cparams(sem=None):
    return pltpu.CompilerParams(dimension_semantics=sem, vmem_limit_bytes=VMEM_LIMIT)


def _pick(dim, pref):
    if dim <= pref:
        return dim
    t = pref
    while dim % t:
        t //= 2
    return t


def _mm(a, b, *, name, ta=False, tb=False, bblk=False, outs=None, extras=(), epilogue=None, dw=None,
        tm=1024, tn=512, after=()):
    if ta:
        K, M = a.shape
    else:
        M, K = a.shape
    if bblk and tb:
        nb, N, Kq = b.shape
        assert nb * Kq == K
    elif bblk:
        nb, Kb, Nq = b.shape
        N = nb * Nq
        assert Kb == K
    else:
        N = b.shape[0] if tb else b.shape[1]
    tn = _pick(Nq if (bblk and not tb) else N, tn)
    if dw is not None and dw[0] == "cols":
        tn = _pick(N // N_CHIPS, tn)
    tm = _pick(M // N_CHIPS if (dw is not None and dw[0] == "rows") else M, tm)
    grid = (M // tm, N // tn)

    a_spec = pl.BlockSpec((K, tm), lambda i, j: (0, i)) if ta else pl.BlockSpec((tm, K), lambda i, j: (i, 0))
    if bblk and tb:
        b_spec = pl.BlockSpec((nb, tn, Kq), lambda i, j: (0, j, 0))
    elif bblk:
        npb = Nq // tn
        b_spec = pl.BlockSpec((None, K, tn), lambda i, j: (j // npb, 0, j % npb))
    elif tb:
        b_spec = pl.BlockSpec((tn, K), lambda i, j: (j, 0))
    else:
        b_spec = pl.BlockSpec((K, tn), lambda i, j: (0, j))
    in_specs = [a_spec, b_spec] + [pl.BlockSpec((tm, tn), lambda i, j: (i, j)) for _ in extras]
    args = [a, b, *extras]
    aliases = {}
    if outs is None:
        outs = [F32]
    if dw is None:
        o_specs = [pl.BlockSpec((tm, tn), lambda i, j: (i, j)) for _ in outs]
        o_shapes = [jax.ShapeDtypeStruct((M, N), dt) for dt in outs]
    else:
        kind, layers, layer, into = dw
        if kind == "cols":
            per = (N // N_CHIPS) // tn
            o_specs = [pl.BlockSpec((None, None, tm, tn), lambda i, j: (j // per, layer, i, j % per))]
            o_shapes = [jax.ShapeDtypeStruct((N_CHIPS, layers, M, N // N_CHIPS), outs[0])]
        else:
            per = (M // N_CHIPS) // tm
            o_specs = [pl.BlockSpec((None, None, tm, tn), lambda i, j: (i // per, layer, i % per, j))]
            o_shapes = [jax.ShapeDtypeStruct((N_CHIPS, layers, M // N_CHIPS, N), outs[0])]
        if into is not None:
            aliases = {len(args): 0}
            in_specs.append(pl.BlockSpec(memory_space=pl.ANY))
            args.append(into)
    for t in after:
        in_specs.append(pl.BlockSpec(memory_space=pl.ANY))
        args.append(t)
    n_e, n_o = len(extras), len(outs)

    sub = _pick(tm, MM_SUB_ROWS)

    def body(a_ref, b_ref, *rest):
        e_refs, o_refs = rest[:n_e], rest[len(rest) - n_o:]
        for r0 in range(0, tm, sub):
            rows = slice(r0, r0 + sub)
            av = (a_ref[:, rows] if ta else a_ref[rows, :]).astype(BF16)
            if bblk and tb:
                acc = _dot_nt(av[:, :Kq], b_ref[0].astype(BF16))
                for s in range(1, nb):
                    acc = acc + _dot_nt(av[:, s * Kq:(s + 1) * Kq], b_ref[s].astype(BF16))
            elif ta:
                acc = _dot_tn(av, b_ref[...].astype(BF16))
            elif tb:
                acc = _dot_nt(av, b_ref[...].astype(BF16))
            else:
                acc = _dot(av, b_ref[...].astype(BF16))
            vals = (acc,) if epilogue is None else epilogue(acc, *[e[rows, :] for e in e_refs])
            for o, v in zip(o_refs, vals):
                o[rows, :] = v.astype(o.dtype)

    res = pl.pallas_call(
        body, name=name, grid=grid, in_specs=in_specs, out_specs=o_specs, out_shape=o_shapes,
        input_output_aliases=aliases, compiler_params=_cparams(("parallel", "arbitrary")),
    )(*args)
    return res[0] if n_o == 1 else res


def _mm_rows(a, b, *, name, epilogue, outs, tb=False, bblk=False, extras=(), fulls=(), accs=(), tm=512, after=()):
    M, K = a.shape
    tm = _pick(M, tm)
    sub = _pick(tm, MM_SUB_ROWS)
    nb = b.shape[0] if bblk else 1
    n_e, n_f, n_o, n_a = len(extras), len(fulls), len(outs), len(accs)
    n_in = 2 + n_e + n_f + len(after)

    def whole(t):
        return pl.BlockSpec(t.shape, lambda i, nd=t.ndim: (0,) * nd)

    in_specs = [pl.BlockSpec((tm, K), lambda i: (i, 0)), whole(b)]
    in_specs += [pl.BlockSpec((tm, e.shape[1]), lambda i: (i, 0)) for e in extras] + [whole(f) for f in fulls]
    in_specs += [pl.BlockSpec(memory_space=pl.ANY) for _ in after]
    out_specs = [pl.BlockSpec((tm, w), lambda i: (i, 0)) for w, _ in outs] + [pl.BlockSpec(s, lambda i: (0, 0)) for s, _ in accs]
    out_shape = [jax.ShapeDtypeStruct((M, w), dt) for w, dt in outs] + [jax.ShapeDtypeStruct(s, dt) for s, dt in accs]

    def body(a_ref, b_ref, *rest):
        e_refs, f_refs = rest[:n_e], rest[n_e:n_e + n_f]
        o_refs, acc_refs = rest[n_in - 2:n_in - 2 + n_o], rest[n_in - 2 + n_o:]
        fv = [f[...] for f in f_refs]
        totals = None
        for r0 in range(0, tm, sub):
            rows = slice(r0, r0 + sub)
            av = a_ref[rows, :].astype(BF16)
            if bblk and tb:
                kq = K // nb
                acc = _dot_nt(av[:, :kq], b_ref[0])
                for s in range(1, nb):
                    acc = acc + _dot_nt(av[:, s * kq:(s + 1) * kq], b_ref[s])
            elif bblk:
                acc = jnp.concatenate([_dot(av, b_ref[s]) for s in range(nb)], axis=-1)
            elif tb:
                acc = _dot_nt(av, b_ref[...])
            else:
                acc = _dot(av, b_ref[...])
            vals = epilogue(acc, *[e[rows, :] for e in e_refs], *fv)
            for o, v in zip(o_refs, vals[:n_o]):
                o[rows, :] = v.astype(o.dtype)
            part = vals[n_o:]
            totals = part if totals is None else [t + p for t, p in zip(totals, part)]
        first_step = pl.program_id(0) == 0
        for o, v in zip(acc_refs, totals):
            @pl.when(first_step)
            def _(o=o, v=v):
                o[...] = v.astype(o.dtype)

            @pl.when(jnp.logical_not(first_step))
            def _(o=o, v=v):
                o[...] += v.astype(o.dtype)

    return pl.pallas_call(
        body, name=name, grid=(M // tm,), in_specs=in_specs, out_specs=out_specs, out_shape=out_shape,
        compiler_params=_cparams(("arbitrary",)),
    )(a, b, *extras, *fulls, *after)


def _rows(fn, rows, fulls, outs, accs=(), *, name, tile=512, after=()):
    first = rows[0][0] if isinstance(rows[0], tuple) else rows[0]
    T = first.shape[0]
    tile = _pick(T, tile)
    in_specs, args = [], []
    for r in rows:
        if isinstance(r, tuple):
            arr, w, cb = r
            in_specs.append(pl.BlockSpec((tile, w), lambda i, cb=cb: (i, cb)))
        else:
            arr = r
            in_specs.append(pl.BlockSpec((tile, arr.shape[1]), lambda i: (i, 0)))
        args.append(arr)
    for f in fulls:
        in_specs.append(pl.BlockSpec(f.shape, lambda i, nd=f.ndim: (0,) * nd))
        args.append(f)
    outs = [o if len(o) == 4 else (*o, o[0], 0) for o in outs]
    out_specs = [pl.BlockSpec((tile, w), lambda i, cb=cb: (i, cb)) for w, _, _, cb in outs]
    out_specs += [pl.BlockSpec(s, lambda i: (0, 0)) for s, _ in accs]
    out_shape = [jax.ShapeDtypeStruct((T, tw), dt) for _, dt, tw, _ in outs]
    out_shape += [jax.ShapeDtypeStruct(s, dt) for s, dt in accs]
    n_in, n_out = len(args), len(outs)
    for t in after:
        in_specs.append(pl.BlockSpec(memory_space=pl.ANY))
        args.append(t)

    def body(*refs):
        vals = fn(*[r[...] for r in refs[:n_in]])
        o_refs = refs[len(args):]
        for o, v in zip(o_refs[:n_out], vals[:n_out]):
            o[...] = v.astype(o.dtype)
        first_step = pl.program_id(0) == 0
        for o, v in zip(o_refs[n_out:], vals[n_out:]):
            @pl.when(first_step)
            def _(o=o, v=v):
                o[...] = v.astype(o.dtype)

            @pl.when(jnp.logical_not(first_step))
            def _(o=o, v=v):
                o[...] += v.astype(o.dtype)

    res = pl.pallas_call(
        body, name=name, grid=(T // tile,), in_specs=in_specs, out_specs=out_specs, out_shape=out_shape,
        compiler_params=_cparams(("arbitrary",)),
    )(*args)
    return res


def _rowsum(v, mxu):
    if not mxu:
        return jnp.sum(v, axis=-1, keepdims=True)
    ones = jnp.ones((v.shape[1], v.shape[1]), BF16)
    hi = v.astype(BF16)
    lo = (v - hi.astype(F32)).astype(BF16)
    return _dot(hi, ones) + _dot(lo, ones)


def _rms(x, g, mxu=False):
    r = lax.rsqrt(_rowsum(x * x, mxu) / x.shape[-1] + EPS)
    return (x * r) * g


def _rms_bwd(x, dy, g, n=None, mxu=False):
    n = x.shape[-1] if n is None else n
    r = lax.rsqrt(_rowsum(x * x, mxu) / n + EPS)
    xh = x * r
    dxh = dy * g
    dx = r * (dxh - xh * (_rowsum(dxh * xh, mxu) / n))
    return dx, dy * xh


def _colsum(v):
    return jnp.sum(v, axis=0, keepdims=True)


def _sigmoid(x):
    return 1.0 / (1.0 + jnp.exp(-x))


def _widen(v, width):
    reps = width // v.shape[1]
    return v if reps == 1 else jnp.concatenate([v] * reps, axis=-1)


def _rope_angles(T, dim):
    inv = (1.0 / (np.float32(ROPE_THETA) ** (np.arange(0, dim, 2, dtype=np.float32) / np.float32(dim)))).astype(np.float32)
    return np.arange(T, dtype=np.float32)[:, None] * inv[None, :]


def _ret_tables(T):
    ang = _rope_angles(T, RET_DK)
    log_gamma = np.log(np.float32(1.0) - np.float32(2.0) ** (-5.0 - np.arange(RET_HEADS, dtype=np.float32)))
    idx = np.arange(RET_BLOCK, dtype=np.float32)
    chunk = np.arange(RET_BLOCK) // CHUNK
    dist = idx[:, None] - idx[None, :]
    seen = np.where(chunk[:, None] == chunk[None, :], np.abs(dist), np.where(chunk[:, None] > chunk[None, :], dist, np.inf))
    intra = np.exp(log_gamma[:, None, None] * seen[None].astype(np.float32))
    qd = np.exp(log_gamma[:, None] * (idx + 1.0))[:, :, None]
    kd = np.exp(log_gamma[:, None] * (RET_BLOCK - 1.0 - idx))[:, :, None]
    cd = np.exp(log_gamma * RET_BLOCK)[:, None, None]
    return tuple(jnp.asarray(t, F32) for t in (np.cos(ang), np.sin(ang), intra, qd, kd, cd))


def _rope_half(x, c, s):
    x1, x2 = x[:, :RET_DK // 2], x[:, RET_DK // 2:]
    return jnp.concatenate([x1 * c - x2 * s, x2 * c + x1 * s], axis=-1)


def _rope_half_bwd(d, c, s):
    d1, d2 = d[:, :RET_DK // 2], d[:, RET_DK // 2:]
    return jnp.concatenate([d1 * c + d2 * s, d2 * c - d1 * s], axis=-1)


def _dot(a, b):
    return lax.dot_general(a, b, (((1,), (0,)), ((), ())), preferred_element_type=F32)


def _dot_nt(a, b):
    return lax.dot_general(a, b, (((1,), (1,)), ((), ())), preferred_element_type=F32)


def _dot_tn(a, b):
    return lax.dot_general(a, b, (((0,), (0,)), ((), ())), preferred_element_type=F32)


def _ret_specs(T, tb, rev):
    nj = T // tb
    jj = (lambda j: nj - 1 - j) if rev else (lambda j: j)
    g = RET_GROUP
    kq = RET_HEADS // g
    vq = 2 * RET_HEADS * RET_DK // (g * RET_DV)
    return dict(
        q=pl.BlockSpec((tb, g * RET_DK), lambda h, j: (jj(j), h)),
        k=pl.BlockSpec((tb, g * RET_DK), lambda h, j: (jj(j), kq + h)),
        v=pl.BlockSpec((tb, g * RET_DV), lambda h, j: (jj(j), vq + h)),
        tab=pl.BlockSpec((tb, RET_DK // 2), lambda h, j: (jj(j), 0)),
        intra=pl.BlockSpec((g, RET_BLOCK, RET_BLOCK), lambda h, j: (h, 0, 0)),
        dec=pl.BlockSpec((g, RET_BLOCK, 1), lambda h, j: (h, 0, 0)),
        cd=pl.BlockSpec((g, 1, 1), lambda h, j: (h, 0, 0)),
        o=pl.BlockSpec((tb, g * RET_DV), lambda h, j: (jj(j), h)),
        s=pl.BlockSpec((g, tb // RET_BLOCK, RET_DK, RET_DV), lambda h, j: (h, jj(j), 0, 0)),
    )


def _ret_fwd(proj, tabs, name):
    T = proj.shape[0]
    cos, sin, intra, qd, kd, cd = tabs
    tb = _pick(T, RET_ROWS)
    cps = tb // RET_BLOCK
    sp = _ret_specs(T, tb, False)
    scale = RET_DK ** -0.5

    def body(q_ref, k_ref, v_ref, cos_ref, sin_ref, intra_ref, qd_ref, kd_ref, cd_ref, o_ref, s_ref, state):
        @pl.when(pl.program_id(1) == 0)
        def _():
            state[...] = jnp.zeros_like(state)

        for c in range(cps):
            rows = pl.ds(c * RET_BLOCK, RET_BLOCK)
            co, si = cos_ref[rows, :], sin_ref[rows, :]
            for h in range(RET_GROUP):
                hk, hv = slice(h * RET_DK, (h + 1) * RET_DK), slice(h * RET_DV, (h + 1) * RET_DV)
                q = _rope_half(q_ref[rows, hk].astype(F32), co, si)
                k = _rope_half(k_ref[rows, hk].astype(F32), co, si) * scale
                vb = v_ref[rows, hv].astype(BF16)
                st = state[h]
                sb = st.astype(BF16)
                s_ref[h, c] = sb
                sc = _dot_nt(q.astype(BF16), k.astype(BF16)) * intra_ref[h]
                inner = _dot(sc.astype(BF16), vb)
                cross = _dot((q * qd_ref[h]).astype(BF16), sb)
                o_ref[rows, hv] = inner + cross
                state[h] = st * cd_ref[h] + _dot_tn((k * kd_ref[h]).astype(BF16), vb)

    return pl.pallas_call(
        body, name=name, grid=(RET_HEADS // RET_GROUP, T // tb),
        in_specs=[sp["q"], sp["k"], sp["v"], sp["tab"], sp["tab"], sp["intra"], sp["dec"], sp["dec"], sp["cd"]],
        out_specs=[sp["o"], sp["s"]],
        out_shape=[jax.ShapeDtypeStruct((T, RET_HEADS * RET_DV), F32),
                   jax.ShapeDtypeStruct((RET_HEADS, T // RET_BLOCK, RET_DK, RET_DV), BF16)],
        scratch_shapes=[pltpu.VMEM((RET_GROUP, RET_DK, RET_DV), F32)],
        compiler_params=_cparams(("arbitrary", "arbitrary")),
    )(proj, proj, proj, cos, sin, intra, qd, kd, cd)


def _ret_bwd(proj, states, dout, dproj, tabs, name):
    assert RET_GROUP == 1
    T = proj.shape[0]
    cos, sin, intra, qd, kd, cd = tabs
    tb = _pick(T, RET_ROWS)
    cps = tb // RET_BLOCK
    nj = T // tb
    sp = _ret_specs(T, tb, True)
    scale = RET_DK ** -0.5
    k0, v0 = RET_HEADS * RET_DK, 2 * RET_HEADS * RET_DK

    def body(q_ref, k_ref, v_ref, cos_ref, sin_ref, intra_ref, qd_ref, kd_ref, cd_ref, s_ref, do_ref, _dproj_in,
             out_ref, dq_s, dk_s, dv_s, sems, dstate):
        head, j = pl.program_id(0), pl.program_id(1)
        step = head * nj + j
        slot = step % 2
        dq_ref, dk_ref, dv_ref = dq_s.at[slot], dk_s.at[slot], dv_s.at[slot]

        @pl.when(j == 0)
        def _():
            dstate[...] = jnp.zeros_like(dstate)

        for c in reversed(range(cps)):
            rows = pl.ds(c * RET_BLOCK, RET_BLOCK)
            co, si = cos_ref[rows, :], sin_ref[rows, :]
            for h in range(RET_GROUP):
                hk, hv = slice(h * RET_DK, (h + 1) * RET_DK), slice(h * RET_DV, (h + 1) * RET_DV)
                q = _rope_half(q_ref[rows, hk].astype(F32), co, si)
                k = _rope_half(k_ref[rows, hk].astype(F32), co, si) * scale
                qb, kb = q.astype(BF16), k.astype(BF16)
                vb = v_ref[rows, hv].astype(BF16)
                dob = do_ref[rows, hv].astype(BF16)
                sb = s_ref[h, c]
                ia = intra_ref[h]
                pb = (_dot_nt(qb, kb) * ia).astype(BF16)
                dsn = dstate[h]
                dsb = dsn.astype(BF16)
                kdk = (k * kd_ref[h]).astype(BF16)
                qdq = (q * qd_ref[h]).astype(BF16)
                dv = _dot_tn(pb, dob) + _dot(kdk, dsb)
                dpb = (_dot_nt(dob, vb) * ia).astype(BF16)
                dq = _dot(dpb, kb) + _dot_nt(dob, sb) * qd_ref[h]
                dk = _dot_tn(dpb, qb) + _dot_nt(vb, dsb) * kd_ref[h]
                dstate[h] = dsn * cd_ref[h] + _dot_tn(qdq, dob)
                dq_ref[rows, hk] = _rope_half_bwd(dq, co, si).astype(BF16)
                dk_ref[rows, hk] = _rope_half_bwd(dk * scale, co, si).astype(BF16)
                dv_ref[rows, hv] = dv.astype(BF16)

        def copies(sl):
            r = pl.ds(pl.multiple_of((nj - 1 - j) * tb, tb), tb)
            cols = lambda first, w: pl.ds(pl.multiple_of(first + head * w, 128), w)
            return [pltpu.make_async_copy(dq_s.at[sl], out_ref.at[r, cols(0, RET_DK)], sems.at[sl, 0]),
                    pltpu.make_async_copy(dk_s.at[sl], out_ref.at[r, cols(k0, RET_DK)], sems.at[sl, 1]),
                    pltpu.make_async_copy(dv_s.at[sl], out_ref.at[r, cols(v0, RET_DV)], sems.at[sl, 2])]

        @pl.when(step > 0)
        def _():
            for cp in copies(1 - slot):
                cp.wait()

        for cp in copies(slot):
            cp.start()

        @pl.when(step == RET_HEADS * nj - 1)
        def _():
            for cp in copies(slot):
                cp.wait()

    return pl.pallas_call(
        body, name=name, grid=(RET_HEADS, nj),
        in_specs=[sp["q"], sp["k"], sp["v"], sp["tab"], sp["tab"], sp["intra"], sp["dec"], sp["dec"], sp["cd"],
                  sp["s"], sp["o"], pl.BlockSpec(memory_space=pl.ANY)],
        out_specs=pl.BlockSpec(memory_space=pl.ANY), out_shape=jax.ShapeDtypeStruct(dproj.shape, dproj.dtype),
        input_output_aliases={11: 0},
        scratch_shapes=[pltpu.VMEM((2, tb, RET_DK), BF16), pltpu.VMEM((2, tb, RET_DK), BF16),
                        pltpu.VMEM((2, tb, RET_DV), BF16), pltpu.SemaphoreType.DMA((2, 3)),
                        pltpu.VMEM((RET_GROUP, RET_DK, RET_DV), F32)],
        compiler_params=_cparams(("arbitrary", "arbitrary")),
    )(proj, proj, proj, cos, sin, intra, qd, kd, cd, states, dout, dproj)


def _ret_gate(out, proj, gn, name):
    def fn(o, g, *gains):
        g = g.astype(F32)
        parts = [_rms(o[:, h * RET_DV:(h + 1) * RET_DV], gains[h]) for h in range(RET_HEADS)]
        return (g * _sigmoid(g) * jnp.concatenate(parts, axis=-1),)
    w = RET_HEADS * RET_DV
    return _rows(fn, [out, (proj, w, 2)], [gn[h:h + 1] for h in range(RET_HEADS)], [(w, BF16)], name=name)[0]


def _ret_gate_bwd(out, proj, gn, dy, name, after=()):
    def fn(o, g, d, *gains):
        g = g.astype(F32)
        sg = _sigmoid(g)
        silu = g * sg
        dsilu = sg * (1.0 + g * (1.0 - sg))
        dos, dgs = [], []
        row = lax.broadcasted_iota(jnp.int32, (RET_HEADS, RET_DV), 0)
        dgn = jnp.zeros((RET_HEADS, RET_DV), F32)
        for h in range(RET_HEADS):
            sl = slice(h * RET_DV, (h + 1) * RET_DV)
            oh = o[:, sl]
            dgs.append(d[:, sl] * _rms(oh, gains[h]) * dsilu[:, sl])
            dx, dg = _rms_bwd(oh, d[:, sl] * silu[:, sl], gains[h])
            dos.append(dx)
            dgn = dgn + jnp.where(row == h, _colsum(dg), 0.0)
        return jnp.concatenate(dos, axis=-1), jnp.concatenate(dgs, axis=-1), dgn
    w = RET_HEADS * RET_DV
    return _rows(fn, [out, (proj, w, 2), dy], [gn[h:h + 1] for h in range(RET_HEADS)],
                 [(w, BF16), (w, BF16, proj.shape[1], 2)], [((RET_HEADS, RET_DV), F32)], name=name, tile=256,
                 after=after)


def _mla_tables(T):
    ang = _rope_angles(T, MLA_ROPE)
    c, s = np.cos(ang), np.sin(ang)
    z32, z64 = np.zeros((T, 32), np.float32), np.zeros((T, 64), np.float32)
    cos_t = np.concatenate([c, c, z64], axis=1)
    sin_a = np.concatenate([-s, z32, z64], axis=1)
    sin_b = np.concatenate([z32, s, z64], axis=1)
    return tuple(jnp.asarray(t, F32) for t in (cos_t, sin_a, sin_b))


def _rope_blk(x, ct, sa, sb):
    return x * ct + pltpu.roll(x, 96, 1) * sa + pltpu.roll(x, 32, 1) * sb


def _rope_blk_bwd(d, ct, sa, sb):
    return d * ct + pltpu.roll(d * sa, 32, 1) + pltpu.roll(d * sb, 96, 1)


def _head_norm(x, gain):
    r = lax.rsqrt(_rowsum(x * x, True) / MLA_QKD + EPS)
    return (x * r) * gain


def _prep_heads(qv, kvv, kr, ct, sa, sb, gqv, gkv):
    qs, ks, vs = [], [], []
    for h in range(MLA_HEADS):
        b = h * MLA_HP
        y = _head_norm(qv[:, b:b + MLA_HP], gqv)
        qs += [y[:, :128], _rope_blk(y[:, 128:], ct, sa, sb)]
        y = _head_norm(jnp.concatenate([kvv[:, b:b + 128], kr], axis=-1), gkv)
        ks += [y[:, :128], _rope_blk(y[:, 128:], ct, sa, sb)]
        vs.append(kvv[:, b + 128:b + 256])
    return jnp.concatenate(qs, axis=-1), jnp.concatenate(ks, axis=-1), jnp.concatenate(vs, axis=-1)


def _mla_front(hn, W, tabs, name):
    wide = MLA_HEADS * MLA_HP
    gq = W["mla_q_norm"] * (MLA_QKD ** -0.5 * LOG2E)

    def epilogue(acc, ct, sa, sb, gqa, gkva, wuq, wukv, gqv, gkv):
        cqn = _rms(acc[:, :MLA_Q_RANK], gqa).astype(BF16)
        ckvn = _rms(acc[:, MLA_Q_RANK:MLA_Q_RANK + MLA_KV_RANK], gkva).astype(BF16)
        q = jnp.concatenate([_dot(cqn, wuq[s]) for s in range(N_CHIPS)], axis=-1).astype(BF16)
        kv = jnp.concatenate([_dot(ckvn, wukv[s]) for s in range(N_CHIPS)], axis=-1).astype(BF16)
        qf, kf, vf = _prep_heads(q.astype(F32), kv.astype(F32), acc[:, MLA_IN_PAD - 128:], ct, sa, sb, gqv, gkv)
        return acc, cqn, ckvn, q, kv, qf, kf, vf

    return _mm_rows(hn, W["mla_w_in"], extras=list(tabs),
                    fulls=[W["mla_q_a_norm"], W["mla_kv_a_norm"], W["mla_w_uq"], W["mla_w_ukv"], gq, W["mla_k_norm"]],
                    outs=[(MLA_IN_PAD, F32), (MLA_Q_RANK, BF16), (MLA_KV_RANK, BF16), (wide, BF16), (wide, BF16),
                          (wide, BF16), (wide, BF16), (MLA_HEADS * MLA_VD, BF16)],
                    epilogue=epilogue, name=name, tm=256)


def _prep_heads_bwd(qv, kvv, kr, ct, sa, sb, dqv, dkv, dvv, gqv, gkv):
    dqs, dkvs = [], []
    dkr = jnp.zeros_like(kr)
    dgq = jnp.zeros((1, MLA_HP), F32)
    dgk = jnp.zeros((1, MLA_HP), F32)
    for h in range(MLA_HEADS):
        b = h * MLA_HP
        dy = jnp.concatenate([dqv[:, b:b + 128], _rope_blk_bwd(dqv[:, b + 128:b + 256], ct, sa, sb)], axis=-1)
        dx, dg = _rms_bwd(qv[:, b:b + MLA_HP], dy, gqv, MLA_QKD, mxu=True)
        dqs.append(dx)
        dgq = dgq + _colsum(dg)
        dy = jnp.concatenate([dkv[:, b:b + 128], _rope_blk_bwd(dkv[:, b + 128:b + 256], ct, sa, sb)], axis=-1)
        dx, dg = _rms_bwd(jnp.concatenate([kvv[:, b:b + 128], kr], axis=-1), dy, gkv, MLA_QKD, mxu=True)
        dkvs += [dx[:, :128], dvv[:, h * MLA_VD:(h + 1) * MLA_VD].astype(F32)]
        dkr = dkr + dx[:, 128:]
        dgk = dgk + _colsum(dg)
    return jnp.concatenate(dqs, axis=-1), jnp.concatenate(dkvs, axis=-1), dkr, dgq, dgk


def _mla_back(q, kv, proj, h0, dh1, dqf, dkf, dvf, W, tabs, name):
    def fn(qv, kvv, pv, hv, dr, ct, sa, sb, dqv, dkv, dvv, gqv, gkv, gqa, gkva, wuq, wukv, w_in, g_mix):
        qv, kvv, dqv, dkv = (t.astype(F32) for t in (qv, kvv, dqv, dkv))
        dq, dkvx, dkr, dgq, dgk = _prep_heads_bwd(qv, kvv, pv[:, MLA_IN_PAD - 128:], ct, sa, sb, dqv, dkv, dvv, gqv, gkv)
        dq, dkvx = dq.astype(BF16), dkvx.astype(BF16)
        nq = wuq.shape[2]
        dcq = sum(_dot_nt(dq[:, s * nq:(s + 1) * nq], wuq[s]) for s in range(N_CHIPS))
        dckv = sum(_dot_nt(dkvx[:, s * nq:(s + 1) * nq], wukv[s]) for s in range(N_CHIPS))
        dxq, dgqa = _rms_bwd(pv[:, :MLA_Q_RANK], dcq, gqa)
        dxkv, dgkva = _rms_bwd(pv[:, MLA_Q_RANK:MLA_Q_RANK + MLA_KV_RANK], dckv, gkva)
        dproj = jnp.concatenate([dxq, dxkv, dkr], axis=-1).astype(BF16)
        dx, dgm = _rms_bwd(hv, _dot_nt(dproj, w_in), g_mix)
        return (dq, dkvx, dproj, dr + dx, dr + dx, dgq, dgk, _colsum(dgqa), _colsum(dgkva), _colsum(dgm))

    wide = MLA_HEADS * MLA_HP
    return _rows(fn, [q, kv, proj, h0, dh1, *tabs, dqf, dkf, dvf],
                 [W["mla_q_norm"], W["mla_k_norm"], W["mla_q_a_norm"], W["mla_kv_a_norm"], W["mla_w_uq"], W["mla_w_ukv"],
                  W["mla_w_in"], W["mix_norm"][1:2]],
                 [(wide, BF16), (wide, BF16), (MLA_IN_PAD, BF16), ROW_F32, ROW_BF16],
                 [((1, MLA_HP), F32), ((1, MLA_HP), F32), ((1, MLA_Q_RANK), F32), ((1, MLA_KV_RANK), F32),
                  ((1, D_MODEL), F32)], name=name, tile=256)


def _chunk_mask(qi, ki, tq, tk):
    shift = CHUNK.bit_length() - 1
    rq = lax.shift_right_arithmetic(qi * tq + lax.broadcasted_iota(jnp.int32, (tq, tk), 0), shift)
    ck = lax.shift_right_arithmetic(ki * tk + lax.broadcasted_iota(jnp.int32, (tq, tk), 1), shift)
    return ck <= rq


def _flash_fwd(qf, kf, vf, name):
    T = qf.shape[0]
    t = _pick(T, FLASH_T)
    n = T // t
    g = FLASH_HEADS

    def body(q_ref, k_ref, v_ref, o_ref, lse_ref, m_s, l_s, acc):
        qi = pl.program_id(1)
        m_s[...] = jnp.full_like(m_s, NEG)
        l_s[...] = jnp.zeros_like(l_s)
        acc[...] = jnp.zeros_like(acc)

        def step(kb, masked):
            rows = pl.ds(pl.multiple_of(kb * t, t), t)
            for h in range(g):
                hq, hv = slice(h * MLA_HP, (h + 1) * MLA_HP), slice(h * MLA_VD, (h + 1) * MLA_VD)
                s = _dot_nt(q_ref[:, hq], k_ref[rows, hq])
                if masked:
                    s = jnp.where(_chunk_mask(0, 0, t, t), s, NEG)
                m_prev = m_s[:, hv]
                m_new = jnp.maximum(m_prev, jnp.max(s, axis=-1, keepdims=True))
                alpha = jnp.exp2(m_prev - m_new)
                p = jnp.exp2(s - _widen(m_new, t))
                l_s[:, hv] = alpha * l_s[:, hv] + sum(p[:, i * 128:(i + 1) * 128] for i in range(t // 128))
                acc[:, hv] = acc[:, hv] * alpha + _dot(p.astype(BF16), v_ref[rows, hv])
                m_s[:, hv] = m_new

        @pl.loop(0, qi)
        def _(kb):
            step(kb, False)

        step(qi, True)
        for h in range(g):
            hv = slice(h * MLA_VD, (h + 1) * MLA_VD)
            l = jnp.sum(l_s[:, hv], axis=-1, keepdims=True)
            o_ref[:, hv] = acc[:, hv] / l
            lse_ref[:, hv] = m_s[:, hv] + jnp.log2(l)

    qmap = lambda h, i: (i, h)
    kmap = lambda h, i: (0, h)
    vec = pltpu.VMEM((t, g * MLA_VD), F32)
    return pl.pallas_call(
        body, name=name, grid=(MLA_HEADS // g, n),
        in_specs=[pl.BlockSpec((t, g * MLA_HP), qmap), pl.BlockSpec((T, g * MLA_HP), kmap),
                  pl.BlockSpec((T, g * MLA_VD), kmap)],
        out_specs=[pl.BlockSpec((t, g * MLA_VD), qmap), pl.BlockSpec((t, g * MLA_VD), qmap)],
        out_shape=[jax.ShapeDtypeStruct((T, MLA_HEADS * MLA_VD), F32),
                   jax.ShapeDtypeStruct((T, MLA_HEADS * MLA_VD), F32)],
        scratch_shapes=[vec, vec, vec],
        compiler_params=_cparams(("parallel", "arbitrary")),
    )(qf, kf, vf)


def _flash_bwd(qf, kf, vf, do16, lse, delta, name):
    T = qf.shape[0]
    t = _pick(T, FLASH_T)
    n = T // t
    scale = MLA_QKD ** -0.5

    def body(q_ref, k_ref, v_ref, do_ref, lse_ref, dl_ref, dq_out, dk_out, dv_out, dq_ref, dk_ref, dv_ref):
        kb = pl.program_id(1)

        @pl.when(kb == 0)
        def _():
            dq_ref[...] = jnp.zeros_like(dq_ref)

        dk_ref[...] = jnp.zeros_like(dk_ref)
        dv_ref[...] = jnp.zeros_like(dv_ref)
        k, v = k_ref[...], v_ref[...]

        def step(qb, masked):
            rows = pl.ds(pl.multiple_of(qb * t, t), t)
            q, dob = q_ref[rows, :], do_ref[rows, :]
            s = _dot_nt(q, k)
            if masked:
                s = jnp.where(_chunk_mask(0, 0, t, t), s, NEG)
            p = jnp.exp2(s - _widen(lse_ref[rows, :], t))
            ds = (p * (_dot_nt(dob, v) - _widen(dl_ref[rows, :], t))).astype(BF16)
            dv_ref[...] += _dot_tn(p.astype(BF16), dob)
            dk_ref[...] += _dot_tn(ds, q)
            dq_ref[rows, :] += _dot(ds, k)

        step(kb, True)

        @pl.loop(kb + 1, n)
        def _(qb):
            step(qb, False)

        dk_out[...] = (dk_ref[...] * (1.0 / LOG2E)).astype(BF16)
        dv_out[...] = dv_ref[...].astype(BF16)

        @pl.when(kb == n - 1)
        def _():
            dq_out[...] = (dq_ref[...] * scale).astype(BF16)

    qmap = lambda h, j: (0, h)
    kmap = lambda h, j: (j, h)
    return pl.pallas_call(
        body, name=name, grid=(MLA_HEADS, n),
        in_specs=[pl.BlockSpec((T, MLA_HP), qmap), pl.BlockSpec((t, MLA_HP), kmap), pl.BlockSpec((t, MLA_VD), kmap),
                  pl.BlockSpec((T, MLA_VD), qmap), pl.BlockSpec((T, MLA_VD), qmap), pl.BlockSpec((T, MLA_VD), qmap)],
        out_specs=[pl.BlockSpec((T, MLA_HP), qmap), pl.BlockSpec((t, MLA_HP), kmap), pl.BlockSpec((t, MLA_VD), kmap)],
        out_shape=[jax.ShapeDtypeStruct((T, MLA_HEADS * MLA_HP), BF16),
                   jax.ShapeDtypeStruct((T, MLA_HEADS * MLA_HP), BF16),
                   jax.ShapeDtypeStruct((T, MLA_HEADS * MLA_VD), BF16)],
        scratch_shapes=[pltpu.VMEM((T, MLA_HP), F32), pltpu.VMEM((t, MLA_HP), F32), pltpu.VMEM((t, MLA_VD), F32)],
        compiler_params=_cparams(("arbitrary", "arbitrary")),
    )(qf, kf, vf, do16, lse, delta)


MESH = pl.DeviceIdType.MESH
ANY = pl.BlockSpec(memory_space=pl.ANY)
_CHIP_FLIPS = ((1, 0), (0, 1), (1, 1))


def _place():
    return lax.axis_index("x"), lax.axis_index("y"), lax.axis_index("c")


def _other_chip(x, y, k):
    fx, fy = _CHIP_FLIPS[k]
    return ((1 - x) if fx else x), ((1 - y) if fy else y)


def _remote(src, dst, send_sems, recv_sems, k, to):
    return pltpu.make_async_remote_copy(src_ref=src, dst_ref=dst, send_sem=send_sems.at[k], recv_sem=recv_sems.at[k],
                                        device_id=to, device_id_type=MESH)


def _index(*vals):
    return jnp.stack(vals).astype(jnp.int32)


def _half(c, rows):
    return pl.ds(pl.multiple_of(c * rows, 16), rows)


def _gather_weights(parts, name, landed=None):
    n_w = len(parts)
    n_in = n_w if landed is None else 2 * n_w

    def body(*refs):
        ins, outs = refs[:n_w], refs[n_in:n_in + n_w]
        send_sems, recv_sems, local_sems = refs[n_in + n_w:]
        x, y, c = _place()
        j = 2 * x + y
        sibling = (x, y, 1 - c)
        chips = [_other_chip(x, y, k) for k in range(3)]
        pending = []
        for w in range(n_w):
            own = pltpu.make_async_copy(ins[w], outs[w].at[j], local_sems.at[w])
            own.start()
            pending.append(own)
        sent = []
        for w in range(n_w):
            if landed is not None:
                break
            r = _half(c, parts[w].shape[0] // 2)
            for k, (px, py) in enumerate(chips):
                cp = _remote(ins[w].at[r], outs[w].at[j, r], send_sems, recv_sems, 6 * w + k, (px, py, c))
                cp.start()
                sent.append(cp)
        for w in range(n_w):
            r = _half(c, parts[w].shape[0] // 2)
            for k, (px, py) in enumerate(chips):
                blk = outs[w].at[2 * px + py, r]
                if landed is None:
                    _remote(blk, blk, send_sems, recv_sems, 6 * w + k, (px, py, c)).wait_recv()
                cp = _remote(blk, blk, send_sems, recv_sems, 6 * w + 3 + k, sibling)
                cp.start()
                sent.append(cp)
        for w in range(n_w):
            r = _half(1 - c, parts[w].shape[0] // 2)
            for k, (px, py) in enumerate(chips):
                blk = outs[w].at[2 * px + py, r]
                _remote(blk, blk, send_sems, recv_sems, 6 * w + 3 + k, sibling).wait_recv()
        for cp in sent:
            cp.wait_send()
        for cp in pending:
            cp.wait()

    return pl.pallas_call(
        body, name=name, in_specs=[pl.BlockSpec(memory_space=pltpu.VMEM)] * n_w + [ANY] * (n_in - n_w),
        out_specs=[ANY] * n_w,
        out_shape=[jax.ShapeDtypeStruct((N_CHIPS, *p.shape), p.dtype) for p in parts],
        input_output_aliases={} if landed is None else {n_w + w: w for w in range(n_w)},
        scratch_shapes=[pltpu.SemaphoreType.DMA((6 * n_w,)), pltpu.SemaphoreType.DMA((6 * n_w,)),
                        pltpu.SemaphoreType.DMA((n_w,))],
        compiler_params=pltpu.CompilerParams(vmem_limit_bytes=VMEM_LIMIT),
    )(*parts, *(landed or []))


def _swap_halves(gs, name):
    n_w = len(gs)

    def body(*refs):
        g_refs, recv_refs = refs[:n_w], refs[n_w:2 * n_w]
        send_sems, recv_sems = refs[2 * n_w:]
        x, y, c = _place()
        sent = []
        for w in range(n_w):
            for jj in range(N_CHIPS):
                cp = _remote(g_refs[w].at[jj, 1 - c], recv_refs[w].at[jj], send_sems, recv_sems, N_CHIPS * w + jj,
                             (x, y, 1 - c))
                cp.start()
                sent.append(cp)
        for cp in sent:
            cp.wait()

    return pl.pallas_call(
        body, name=name, in_specs=[ANY] * n_w, out_specs=[ANY] * n_w,
        out_shape=[jax.ShapeDtypeStruct((N_CHIPS, *g.shape[2:]), g.dtype) for g in gs],
        scratch_shapes=[pltpu.SemaphoreType.DMA((N_CHIPS * n_w,)), pltpu.SemaphoreType.DMA((N_CHIPS * n_w,))],
    )(*gs)


def _pair_sum(g, recv, core, name):
    _, H, C = recv.shape
    tile = _pick(H, SUM_ROWS)

    def body(c_ref, own_ref, recv_ref, out_ref):
        out_ref[...] = (own_ref[...].astype(F32) + recv_ref[...].astype(F32)).astype(BF16)

    blk = pl.BlockSpec((None, tile, C), lambda jj, i, c: (jj, i, 0))
    return pl.pallas_call(
        body, name=name,
        grid_spec=pltpu.PrefetchScalarGridSpec(
            num_scalar_prefetch=1, grid=(N_CHIPS, H // tile),
            in_specs=[pl.BlockSpec((None, None, tile, C), lambda jj, i, c: (jj, c[0], i, 0)), blk],
            out_specs=blk),
        out_shape=jax.ShapeDtypeStruct((N_CHIPS, H, C), BF16),
        compiler_params=_cparams(("arbitrary", "arbitrary")),
    )(_index(core), g, recv)


def _chip_sum(g, recv, got, chip, core, name):
    _, H, C = recv.shape
    tile = _pick(H, SUM_ROWS)

    def body(s_ref, own_ref, recv_ref, g0_ref, g1_ref, g2_ref, out_ref):
        pair = own_ref[...].astype(F32) + recv_ref[...].astype(F32)
        out_ref[...] = ((pair + g0_ref[...].astype(F32)) + g1_ref[...].astype(F32)) + g2_ref[...].astype(F32)

    def got_spec(k):
        return pl.BlockSpec((None, tile, C), lambda i, s, k=k: (k, i, 0))

    return pl.pallas_call(
        body, name=name,
        grid_spec=pltpu.PrefetchScalarGridSpec(
            num_scalar_prefetch=1, grid=(H // tile,),
            in_specs=[pl.BlockSpec((None, None, tile, C), lambda i, s: (s[0], s[1], i, 0)),
                      pl.BlockSpec((None, tile, C), lambda i, s: (s[0], i, 0)), got_spec(0), got_spec(1), got_spec(2)],
            out_specs=pl.BlockSpec((None, tile, C), lambda i, s: (s[1], i, 0))),
        out_shape=jax.ShapeDtypeStruct((2, H, C), F32),
        compiler_params=_cparams(("arbitrary",)),
    )(_index(chip, core), g, recv, got, got, got)


def _share_halves(reds):
    n_w = len(reds)

    def body(*refs):
        out_refs = refs[n_w:2 * n_w]
        send_sems, recv_sems = refs[2 * n_w:]
        x, y, c = _place()
        sent = []
        for w in range(n_w):
            blk = out_refs[w].at[c]
            cp = _remote(blk, blk, send_sems, recv_sems, w, (x, y, 1 - c))
            cp.start()
            sent.append(cp)
        for cp in sent:
            cp.wait()

    return pl.pallas_call(
        body, name="grad_share_halves", in_specs=[ANY] * n_w, out_specs=[ANY] * n_w,
        out_shape=[jax.ShapeDtypeStruct(r.shape, r.dtype) for r in reds],
        input_output_aliases={w: w for w in range(n_w)},
        scratch_shapes=[pltpu.SemaphoreType.DMA((n_w,)), pltpu.SemaphoreType.DMA((n_w,))],
    )(*reds)


def _allsum_small(v, name):
    R, W = v.shape
    n_dev = 8
    vm = pl.BlockSpec(memory_space=pltpu.VMEM)

    def body(v_ref, out_ref, buf, send_sems, recv_sems):
        x, y, c = _place()
        me = 4 * x + 2 * y + c
        buf[me] = v_ref[...]
        sent = []
        for k in range(1, n_dev):
            peer = ((1 - x) if k & 4 else x, (1 - y) if k & 2 else y, (1 - c) if k & 1 else c)
            cp = _remote(v_ref, buf.at[me], send_sems, recv_sems, k - 1, peer)
            cp.start()
            sent.append(cp)
        for cp in sent:
            cp.wait_recv()
        for cp in sent:
            cp.wait_send()
        acc = buf[0]
        for q in range(1, n_dev):
            acc = acc + buf[q]
        out_ref[...] = acc

    return pl.pallas_call(
        body, name=name, in_specs=[vm], out_specs=vm, out_shape=jax.ShapeDtypeStruct((R, W), v.dtype),
        scratch_shapes=[pltpu.VMEM((n_dev, R, W), v.dtype), pltpu.SemaphoreType.DMA((n_dev - 1,)),
                        pltpu.SemaphoreType.DMA((n_dev - 1,))],
    )(v)


HBM = pl.BlockSpec(memory_space=pltpu.HBM)
SEM = pl.BlockSpec(memory_space=pltpu.SEMAPHORE)
_DATAFLOW = pltpu.SideEffectType.DATAFLOW_SIDE_EFFECTING


def _split_start(name, srcs, land_shapes, n_copies, copies, after=()):
    ns, nl = len(srcs), len(land_shapes)
    lands = [lax.empty(s.shape, s.dtype) for s in land_shapes]

    def body(*refs):
        outs = refs[ns + nl + len(after):]
        for cp in copies(refs[:ns], refs[ns:ns + nl], outs[0], outs[1]):
            cp.start()
        outs[-1][...] = jnp.zeros_like(outs[-1])

    sems = pltpu.SemaphoreType.DMA((n_copies,))
    res = pl.pallas_call(
        body, name=name, in_specs=[HBM] * (ns + nl) + [ANY] * len(after),
        out_specs=(SEM, SEM, *[HBM] * (ns + nl), pl.BlockSpec(memory_space=pltpu.VMEM)),
        out_shape=(sems, sems, *[pltpu.HBM(a.shape, a.dtype) for a in srcs],
                   *[pltpu.HBM(s.shape, s.dtype) for s in land_shapes], jax.ShapeDtypeStruct((8, 128), F32)),
        input_output_aliases={i: 2 + i for i in range(ns + nl)},
        compiler_params=pltpu.CompilerParams(has_side_effects=_DATAFLOW),
    )(*[pltpu.with_memory_space_constraint(a, pltpu.HBM) for a in [*srcs, *lands]], *after)
    return res[0], res[1], list(res[2:2 + ns]), list(res[2 + ns:2 + ns + nl]), res[-1]


def _split_wait(name, send_sems, recv_sems, srcs, lands, copies, after=()):
    ns, nl = len(srcs), len(lands)

    def body(*refs):
        for cp in copies(refs[:ns], refs[ns:ns + nl], refs[ns + nl], refs[ns + nl + 1]):
            cp.wait_send()
            cp.wait_recv()

    res = pl.pallas_call(
        body, name=name, in_specs=[HBM] * (ns + nl) + [SEM, SEM] + [ANY] * len(after), out_specs=[HBM] * (ns + nl),
        out_shape=[pltpu.HBM(a.shape, a.dtype) for a in [*srcs, *lands]],
        input_output_aliases={i: i for i in range(ns + nl)},
        compiler_params=pltpu.CompilerParams(has_side_effects=_DATAFLOW),
    )(*srcs, *lands, send_sems, recv_sems, *after)
    return list(res[:ns]), list(res[ns:])


def _gather_copies(rows):
    def copies(src_refs, land_refs, send_sems, recv_sems):
        x, y, c = _place()
        j = 2 * x + y
        out = []
        for w in range(len(src_refs)):
            r = _half(c, rows[w] // 2)
            for k in range(3):
                px, py = _other_chip(x, y, k)
                out.append(_remote(src_refs[w].at[r], land_refs[w].at[j, r], send_sems, recv_sems, 3 * w + k, (px, py, c)))
        return out
    return copies


def _scatter_copies(src_refs, land_refs, send_sems, recv_sems):
    x, y, c = _place()
    j = 2 * x + y
    out = []
    for w in range(len(src_refs)):
        for k in range(3):
            px, py = _other_chip(x, y, k)
            pj = 2 * px + py
            out.append(_remote(src_refs[w].at[pj], land_refs[w].at[(j - pj + 4) % 4 - 1], send_sems, recv_sems, 3 * w + k,
                               (px, py, c)))
    return out


def _halves(grads):
    names = list(grads)
    return names, [grads[k].reshape(N_CHIPS, 2, -1, grads[k].shape[-1]) for k in names]


def _reduce_begin(grads, core, tag):
    names, gs = _halves(grads)
    recvs = _swap_halves(gs, f"grad_swap_halves_{tag}")
    sums = [_pair_sum(g, r, core, f"pair_sum_{k}") for k, g, r in zip(names, gs, recvs)]
    return names, gs, recvs, sums


def _swap_copies(src_refs, land_refs, send_sems, recv_sems):
    x, y, c = _place()
    return [_remote(src_refs[w].at[jj, 1 - c], land_refs[w].at[jj], send_sems, recv_sems, N_CHIPS * w + jj, (x, y, 1 - c))
            for w in range(len(src_refs)) for jj in range(N_CHIPS)]


def _swap_begin(grads, tag):
    names, gs = _halves(grads)
    started = _split_start(f"swap_{tag}_start", gs, [jax.ShapeDtypeStruct((N_CHIPS, *g.shape[2:]), g.dtype) for g in gs],
                           N_CHIPS * len(gs), _swap_copies)
    return (names, started[:4]), started[4]


def _swap_end(begun, core, tag, after):
    names, started = begun
    gs, recvs = _split_wait(f"swap_{tag}_wait", *started, _swap_copies, after=after)
    sums = [_pair_sum(g, r, core, f"pair_sum_{k}") for k, g, r in zip(names, gs, recvs)]
    return names, gs, recvs, sums


def _reduce_end(begun, gots, chip, core):
    names, gs, recvs, _ = begun
    return {k: _chip_sum(g, r, t, chip, core, f"chip_sum_{k}") for k, g, r, t in zip(names, gs, recvs, gots)}


def _got_shapes(sums):
    return [jax.ShapeDtypeStruct((3, *a.shape[1:]), a.dtype) for a in sums]


def _adamw(w, g, m, v, name, layers=1, layer=0, into=None):
    shape = w.shape
    cols = shape[-1]
    w3, m3, v3 = (t.reshape(layers, -1, cols) for t in (w, m, v))
    rows = w3.shape[1]
    tile = _pick(rows, ADAM_ROWS if cols <= 1024 else ADAM_ROWS // 2) if rows % 8 == 0 else rows
    n_in = 4 + (0 if into is None else 4)
    stack_g = layers > 1

    def body(*refs):
        wv, gv, mv, vv = (r[...] for r in refs[:4])
        d_ref, m_ref, v_ref = refs[len(refs) - 3:]
        m2 = ADAM_B1 * mv + (1.0 - ADAM_B1) * gv
        v2 = ADAM_B2 * vv + (1.0 - ADAM_B2) * jnp.square(gv)
        m_hat = m2 / (1.0 - ADAM_B1 ** ADAM_STEP)
        v_hat = v2 / (1.0 - ADAM_B2 ** ADAM_STEP)
        if stack_g:
            refs[n_in][...] = gv
        d_ref[...] = -ADAM_LR * (m_hat / (jnp.sqrt(v_hat) + ADAM_EPS) + ADAM_WD * wv)
        m_ref[...] = m2
        v_ref[...] = v2

    n_out = 4 if stack_g else 3
    lay = pl.BlockSpec((None, tile, cols), lambda i: (layer, i, 0))
    out = jax.ShapeDtypeStruct((layers, rows, cols), F32)
    res = pl.pallas_call(
        body, name=name, grid=(rows // tile,),
        in_specs=[lay, pl.BlockSpec((tile, cols), lambda i: (i, 0)), lay, lay] + [ANY] * (n_in - 4),
        out_specs=[lay] * n_out, out_shape=[out] * n_out,
        input_output_aliases={} if into is None else {4 + k: k for k in range(4)},
        compiler_params=_cparams(("arbitrary",)),
    )(w3, g.reshape(rows, cols), m3, v3, *([] if into is None else [t.reshape(layers, rows, cols) for t in into]))
    res = tuple(t.reshape(shape) for t in res)
    return res if stack_g else (g.reshape(shape), *res)


ROW_F32, ROW_BF16 = (D_MODEL, F32), (D_MODEL, BF16)


def _res_norm(acc, h, gain):
    hh = h + acc
    return hh, _rms(hh, gain)


def _dx_norm_bwd(d, w, h, dres, gain, name, **kw):
    def epilogue(acc, hv, dr, g):
        dx, dg = _rms_bwd(hv, acc, g)
        return dr + dx, dr + dx, _colsum(dg)
    return _mm_rows(d, w, tb=True, extras=[h, dres], fulls=[gain], outs=[ROW_F32, ROW_BF16], accs=[((1, D_MODEL), F32)],
                    epilogue=epilogue, name=name, **kw)


def _tail_fwd(h1, hn2, p16, W, i, tag, next_gain=None, target=None):
    a = _mm(hn2, W["mlp_w1"][i], bblk=True, outs=[BF16], name=f"{tag}_mlp_w1", tm=2048, tn=1024,
            epilogue=lambda acc: (jnp.square(jnp.maximum(acc, 0.0)),))
    h2, hn3 = _mm_rows(a, W["mlp_w2"][i], extras=[h1], fulls=[W["ple_norm"][i:i + 1]], outs=[ROW_F32, ROW_BF16],
                       epilogue=_res_norm, name=f"{tag}_mlp_w2")
    def embed(acc, pv, h, wp):
        gate = _sigmoid(acc)
        ppv = jnp.concatenate([_dot(pv, wp[s]) for s in range(N_CHIPS)], axis=-1)
        return gate, ppv, h + gate * ppv

    if target is None:
        def gated(acc, pv, h, wp, gain):
            gate, ppv, hh = embed(acc, pv, h, wp)
            return hh, ppv, gate, _rms(hh, gain)
        h3, pp, gate, hn = _mm_rows(hn3, W["ple_gate_w"][i], extras=[p16[i], h2], fulls=[W["ple_proj_w"][i], next_gain],
                                    outs=[ROW_F32, ROW_BF16, ROW_BF16, ROW_BF16], epilogue=gated, name=f"{tag}_ple")
        return h3, hn, (h1, hn2, a, h2, hn3, gate, pp)

    def gated_loss(acc, pv, h, t, wp):
        gate, ppv, hh = embed(acc, pv, h, wp)
        e = hh - t
        return ppv, gate, e * (1.0 / D_MODEL), jnp.full((1, 128), 0.5 / D_MODEL, F32) * jnp.sum(e * e)
    pp, gate, dy, loss = _mm_rows(hn3, W["ple_gate_w"][i], extras=[p16[i], h2, target], fulls=[W["ple_proj_w"][i]],
                                  outs=[ROW_BF16, ROW_BF16, ROW_F32], accs=[((1, 128), F32)], epilogue=gated_loss,
                                  name=f"{tag}_ple")
    return dy, loss, (h1, hn2, a, h2, hn3, gate, pp)


def _tail_bwd(dh3, saved, p16, W, i, tag, after=(), hook=None):
    h1, hn2, a, h2, hn3, gate, pp = saved

    def embed_bwd(d, g, ppv, hv, wg, gain):
        g, ppv = g.astype(F32), ppv.astype(F32)
        dppv, dglv = (d * g).astype(BF16), (d * ppv * g * (1.0 - g)).astype(BF16)
        dx, dg = _rms_bwd(hv, _dot_nt(dglv, wg), gain)
        return dppv, dglv, d + dx, d + dx, _colsum(dg)

    def dw(kind, name):
        return (kind, 1, 0, None)

    dpp, dgl, dh2, dh2_16, d_ple_norm = _rows(
        embed_bwd, [dh3, gate, pp, h2], [W["ple_gate_w"][i], W["ple_norm"][i:i + 1]],
        [ROW_BF16, ROW_BF16, ROW_F32, ROW_BF16], [((1, D_MODEL), F32)], name=f"{tag}_ple_bwd", after=after)
    later = () if hook is None else hook(dh2_16)
    d_proj = _mm(p16[i], dpp, ta=True, outs=[BF16], dw=dw("cols", "ple_proj_w"), name=f"{tag}_d_ple_proj", after=later)
    d_gate = _mm(hn3, dgl, ta=True, outs=[BF16], dw=dw("rows", "ple_gate_w"), name=f"{tag}_d_ple_gate")
    d_w2 = _mm(a, dh2_16, ta=True, outs=[BF16], dw=dw("rows", "mlp_w2"), name=f"{tag}_d_mlp_w2", tn=1024)
    dz = _mm(dh2_16, W["mlp_w2"][i], tb=True, extras=[a], outs=[BF16], name=f"{tag}_mlp_w2_dx", tm=2048, tn=1024,
             epilogue=lambda acc, av: (acc * (2.0 * jnp.sqrt(av.astype(F32))),))
    d_w1 = _mm(hn2, dz, ta=True, outs=[BF16], dw=dw("cols", "mlp_w1"), name=f"{tag}_d_mlp_w1", tn=1024)
    dh1, dh1_16, d_mlp_norm = _dx_norm_bwd(dz, W["mlp_w1"][i], h1, dh2, W["mlp_norm"][i:i + 1], f"{tag}_mlp_w1_dx",
                                           bblk=True)
    big = {f"mlp_w1_{i}": d_w1, f"mlp_w2_{i}": d_w2, f"ple_gate_w_{i}": d_gate, f"ple_proj_w_{i}": d_proj}
    return dh1, dh1_16, big, dict(mlp_norm=d_mlp_norm, ple_norm=d_ple_norm)


def _ret_layer_fwd(h0, W, tabs, after=()):
    hn = _rows(lambda x, g: (_rms(x, g),), [h0], [W["mix_norm"][0:1]], [(D_MODEL, BF16)], name="ret_mix_norm",
               after=after)[0]
    proj = _mm(hn, W["ret_w_in"], bblk=True, outs=[BF16], name="ret_w_in", tm=2048, tn=768)
    out, states = _ret_fwd(proj, tabs, "ret_scan")
    y = _ret_gate(out, proj, W["ret_gn"], "ret_gate")
    h1, hn2 = _mm_rows(y, W["ret_w_out"], extras=[h0], fulls=[W["mlp_norm"][0:1]], outs=[ROW_F32, ROW_BF16],
                       epilogue=_res_norm, name="ret_w_out")
    return h1, hn2, (h0, hn, proj, out, states, y)


def _d_ret_w_out(dh1_16, saved):
    return _mm(saved[5], dh1_16, ta=True, outs=[BF16], dw=("rows", 1, 0, None), name="d_ret_w_out")


def _ret_layer_bwd(dh1, dh1_16, saved, W, tabs, after=(), hook=None, on_grads=None, d_w_out=None):
    h0, hn, proj, out, states, y = saved
    d_w_out = _d_ret_w_out(dh1_16, saved) if d_w_out is None else d_w_out
    dy = _mm(dh1_16, W["ret_w_out"], tb=True, name="ret_w_out_dx", tn=1024, after=after)
    dout, dproj, d_gn = _ret_gate_bwd(out, proj, W["ret_gn"], dy, "ret_gate_bwd",
                                      after=() if hook is None else hook(dy))
    dproj = _ret_bwd(proj, states, dout, dproj, tabs, "ret_scan_bwd")
    d_w_in = _mm(hn, dproj, ta=True, outs=[BF16], dw=("cols", 1, 0, None), name="d_ret_w_in", tn=768)
    big = dict(ret_w_in=d_w_in, ret_w_out=d_w_out)
    later = () if on_grads is None else on_grads(big)
    dh0, _, d_mix = _dx_norm_bwd(dproj, W["ret_w_in"], h0, dh1, W["mix_norm"][0:1], "ret_w_in_dx", bblk=True, tm=256,
                                 after=later)
    return dh0, big, dict(mix_norm=d_mix, ret_gn=d_gn)


def _mla_layer_fwd(h0, hn, W, tabs):
    proj, cqn, ckvn, q, kv, qf, kf, vf = _mla_front(hn, W, tabs, "mla_front")
    o, lse = _flash_fwd(qf, kf, vf, "mla_flash")
    h1, hn2 = _mm_rows(o, W["mla_w_out"], extras=[h0], fulls=[W["mlp_norm"][1:2]], outs=[ROW_F32, ROW_BF16],
                       epilogue=_res_norm, name="mla_w_out")
    return h1, hn2, (h0, hn, proj, cqn, ckvn, q, kv, qf, kf, vf, o, lse)


def _mla_layer_bwd(dh1, dh1_16, saved, W, tabs):
    h0, hn, proj, cqn, ckvn, q, kv, qf, kf, vf, o, lse = saved
    d_w_out = _mm(o, dh1_16, ta=True, outs=[BF16], dw=("rows", 1, 0, None), name="d_mla_w_out")
    def with_delta(acc, ov):
        parts = []
        for h in range(MLA_HEADS):
            sl = slice(h * MLA_VD, (h + 1) * MLA_VD)
            d = jnp.sum(acc[:, sl] * ov[:, sl], axis=-1, keepdims=True)
            parts.append(jnp.broadcast_to(d, (d.shape[0], MLA_VD)))
        return jnp.concatenate(parts, axis=-1), acc

    delta, do16 = _mm_rows(dh1_16, W["mla_w_out"], tb=True, extras=[o], outs=[ROW_F32, ROW_BF16], epilogue=with_delta,
                           name="mla_w_out_dx")
    dqf, dkf, dvf = _flash_bwd(qf, kf, vf, do16, lse, delta, "mla_flash_bwd")
    dq, dkv, dproj, dh0, dh0_16, d_gq, d_gk, d_gqa, d_gkva, d_mix = _mla_back(q, kv, proj, h0, dh1, dqf, dkf, dvf, W, tabs,
                                                                              "mla_back")
    d_w_uq = _mm(cqn, dq, ta=True, outs=[BF16], dw=("cols", 1, 0, None), name="d_mla_w_uq")
    d_w_ukv = _mm(ckvn, dkv, ta=True, outs=[BF16], dw=("cols", 1, 0, None), name="d_mla_w_ukv")
    d_w_in = _mm(hn, dproj, ta=True, outs=[BF16], dw=("rows", 1, 0, None), name="d_mla_w_in")
    return (dh0, dh0_16, dict(mla_w_in=d_w_in, mla_w_uq=d_w_uq, mla_w_ukv=d_w_ukv, mla_w_out=d_w_out),
            dict(mix_norm=d_mix, mla_q_a_norm=d_gqa, mla_kv_a_norm=d_gkva, mla_q_norm=d_gq, mla_k_norm=d_gk))


def _small_grads(n_ret, n_t0, n_mla, n_t1):
    return dict(
        mix_norm=jnp.concatenate([n_ret["mix_norm"], n_mla["mix_norm"]], axis=0),
        mlp_norm=jnp.concatenate([n_t0["mlp_norm"], n_t1["mlp_norm"]], axis=0),
        ple_norm=jnp.concatenate([n_t0["ple_norm"], n_t1["ple_norm"]], axis=0),
        ret_gn=n_ret["ret_gn"], mla_q_a_norm=n_mla["mla_q_a_norm"], mla_kv_a_norm=n_mla["mla_kv_a_norm"],
        mla_q_norm=n_mla["mla_q_norm"], mla_k_norm=n_mla["mla_k_norm"])


_ORDER = ("mix_norm", "ret_w_in", "ret_gn", "ret_w_out", "mla_w_in", "mla_q_a_norm", "mla_kv_a_norm", "mla_w_uq",
          "mla_w_ukv", "mla_q_norm", "mla_k_norm", "mla_w_out", "mlp_norm", "mlp_w1", "mlp_w2", "ple_norm",
          "ple_gate_w", "ple_proj_w")
_TWO_LAYER = ("mlp_w1", "mlp_w2", "ple_gate_w", "ple_proj_w")
HEADS_PER_CHIP = MLA_HEADS // N_CHIPS
GAIN_ROWS = 32


def _travel_parts(w):
    uq = jnp.pad(w["mla_w_uq"][0].reshape(MLA_Q_RANK, HEADS_PER_CHIP, MLA_QKD), ((0, 0), (0, 0), (0, MLA_HP - MLA_QKD)))
    parts = {"ret_w_in": w["ret_w_in"][0], "ret_w_out": w["ret_w_out"][0]}
    for k in _TWO_LAYER:
        parts[k + "_0"] = w[k][0]
    parts["mla_w_in"] = jnp.pad(w["mla_w_in"][0], ((0, 0), (0, MLA_IN_PAD - MLA_IN)))
    parts["mla_w_uq"] = uq.reshape(MLA_Q_RANK, HEADS_PER_CHIP * MLA_HP)
    parts["mla_w_ukv"] = w["mla_w_ukv"][0]
    parts["mla_w_out"] = w["mla_w_out"][0]
    for k in _TWO_LAYER:
        parts[k + "_1"] = w[k][1]
    gains = jnp.concatenate([_pad_row(w["ret_gn"]), _pad_row(w["mla_q_a_norm"]), _pad_row(w["mla_kv_a_norm"]),
                             jnp.zeros((GAIN_ROWS - 3, PACK_W), F32)], axis=0)
    return {"gains": gains, **{k: v.astype(BF16) for k, v in parts.items()}}


def _full_weights(full):
    rows = lambda a: a.reshape(-1, a.shape[-1])
    W = {k: full[k] for k in ("ret_w_in", "mla_w_uq", "mla_w_ukv") if k in full}
    for k in ("ret_w_out", "mla_w_in", "mla_w_out"):
        if k in full:
            W[k] = rows(full[k])
    for k, by_rows in (("mlp_w1", False), ("ple_proj_w", False), ("mlp_w2", True), ("ple_gate_w", True)):
        layers = [full.get(f"{k}_{i}") for i in range(2)]
        W[k] = [rows(t) if (by_rows and t is not None) else t for t in layers]
    return W


def _shard_grad(name, red, shape):
    if name == "mla_w_in":
        red = red.reshape(-1, MLA_IN_PAD)[:, :MLA_IN]
    elif name == "mla_w_uq":
        red = red.reshape(MLA_Q_RANK, HEADS_PER_CHIP, MLA_HP)[:, :, :MLA_QKD]
    return red.reshape(shape)


def _pad_row(v):
    v = v.reshape(1, -1)
    return jnp.pad(v, ((0, 0), (0, PACK_W - v.shape[1])))


def kernel(x, p, mix_norm, ret_w_in, ret_gn, ret_w_out, mla_w_in, mla_q_a_norm, mla_kv_a_norm, mla_w_uq, mla_w_ukv, mla_q_norm, mla_k_norm, mla_w_out, mlp_norm, mlp_w1, mlp_w2, ple_norm, ple_gate_w, ple_proj_w, loss_target, m_mix_norm, m_ret_w_in, m_ret_gn, m_ret_w_out, m_mla_w_in, m_mla_q_a_norm, m_mla_kv_a_norm, m_mla_w_uq, m_mla_w_ukv, m_mla_q_norm, m_mla_k_norm, m_mla_w_out, m_mlp_norm, m_mlp_w1, m_mlp_w2, m_ple_norm, m_ple_gate_w, m_ple_proj_w, v_mix_norm, v_ret_w_in, v_ret_gn, v_ret_w_out, v_mla_w_in, v_mla_q_a_norm, v_mla_kv_a_norm, v_mla_w_uq, v_mla_w_ukv, v_mla_q_norm, v_mla_k_norm, v_mla_w_out, v_mlp_norm, v_mlp_w1, v_mlp_w2, v_ple_norm, v_ple_gate_w, v_ple_proj_w):
    w = dict(mix_norm=mix_norm, ret_w_in=ret_w_in, ret_gn=ret_gn, ret_w_out=ret_w_out, mla_w_in=mla_w_in,
             mla_q_a_norm=mla_q_a_norm, mla_kv_a_norm=mla_kv_a_norm, mla_w_uq=mla_w_uq, mla_w_ukv=mla_w_ukv,
             mla_q_norm=mla_q_norm, mla_k_norm=mla_k_norm, mla_w_out=mla_w_out, mlp_norm=mlp_norm, mlp_w1=mlp_w1,
             mlp_w2=mlp_w2, ple_norm=ple_norm, ple_gate_w=ple_gate_w, ple_proj_w=ple_proj_w)
    m = dict(mix_norm=m_mix_norm, ret_w_in=m_ret_w_in, ret_gn=m_ret_gn, ret_w_out=m_ret_w_out, mla_w_in=m_mla_w_in,
             mla_q_a_norm=m_mla_q_a_norm, mla_kv_a_norm=m_mla_kv_a_norm, mla_w_uq=m_mla_w_uq, mla_w_ukv=m_mla_w_ukv,
             mla_q_norm=m_mla_q_norm, mla_k_norm=m_mla_k_norm, mla_w_out=m_mla_w_out, mlp_norm=m_mlp_norm,
             mlp_w1=m_mlp_w1, mlp_w2=m_mlp_w2, ple_norm=m_ple_norm, ple_gate_w=m_ple_gate_w, ple_proj_w=m_ple_proj_w)
    v = dict(mix_norm=v_mix_norm, ret_w_in=v_ret_w_in, ret_gn=v_ret_gn, ret_w_out=v_ret_w_out, mla_w_in=v_mla_w_in,
             mla_q_a_norm=v_mla_q_a_norm, mla_kv_a_norm=v_mla_kv_a_norm, mla_w_uq=v_mla_w_uq, mla_w_ukv=v_mla_w_ukv,
             mla_q_norm=v_mla_q_norm, mla_k_norm=v_mla_k_norm, mla_w_out=v_mla_w_out, mlp_norm=v_mlp_norm,
             mlp_w1=v_mlp_w1, mlp_w2=v_mlp_w2, ple_norm=v_ple_norm, ple_gate_w=v_ple_gate_w, ple_proj_w=v_ple_proj_w)
    xi, yi, ci = _place()
    chip = 2 * xi + yi
    n = N_CHIPS

    parts = _travel_parts(w)
    first = ("gains", "ret_w_in", "ret_w_out")
    mid = [k + "_0" for k in _TWO_LAYER]
    last = [k for k in parts if k not in first and k not in mid]
    full = dict(zip(first, _gather_weights([parts[k] for k in first], "gather_first")))

    def gather_behind(names, tag, after):
        copies = _gather_copies([parts[k].shape[0] for k in names])
        started = _split_start(f"gather_{tag}_start", [parts[k] for k in names],
                               [jax.ShapeDtypeStruct((n, *parts[k].shape), BF16) for k in names], 3 * len(names),
                               copies, after=after)

        def arrive(after):
            _, landed = _split_wait(f"gather_{tag}_wait", *started[:4], copies, after=after)
            full.update(zip(names, _gather_weights([parts[k] for k in names], f"gather_{tag}_finish", landed=landed)))
            W.update(_full_weights(full))
        return started[4], arrive

    mid_token, mid_arrive = gather_behind(mid, "mid", [full["ret_w_in"]])
    g_token, last_arrive = gather_behind(last, "last", [mid_token])
    gains = full["gains"]
    W = dict(mix_norm=mix_norm, mlp_norm=mlp_norm, ple_norm=ple_norm,
             mla_q_norm=jnp.pad(mla_q_norm, ((0, 0), (0, MLA_HP - MLA_QKD))),
             mla_k_norm=jnp.pad(mla_k_norm, ((0, 0), (0, MLA_HP - MLA_QKD))),
             ret_w_in=full["ret_w_in"], ret_w_out=full["ret_w_out"].reshape(-1, D_MODEL),
             ret_gn=gains[:, 0, :RET_HEADS * 128].reshape(n, RET_HEADS, 128).transpose(1, 0, 2).reshape(RET_HEADS, RET_DV),
             mla_q_a_norm=gains[:, 1, :MLA_Q_RANK // n].reshape(1, MLA_Q_RANK),
             mla_kv_a_norm=gains[:, 2, :MLA_KV_RANK // n].reshape(1, MLA_KV_RANK))
    x0, p16, target = x[0], p[:, 0].astype(BF16), loss_target[0]
    T = x0.shape[0]
    ret_tabs, mla_tabs = _ret_tables(T), _mla_tables(T)

    h1, hn, s_ret = _ret_layer_fwd(x0, W, ret_tabs, after=[g_token])
    mid_arrive([h1])
    h3, hn, s_tail0 = _tail_fwd(h1, hn, p16, W, 0, "l0", next_gain=W["mix_norm"][1:2])
    last_arrive([h3])
    h4, hn, s_mla = _mla_layer_fwd(h3, hn, W, mla_tabs)
    dy, loss, s_tail1 = _tail_fwd(h4, hn, p16, W, 1, "l1", target=target)

    dh4, dh4_16, g_t1, n_t1 = _tail_bwd(dy, s_tail1, p16, W, 1, "l1")
    dh3, _, g_mla, n_mla = _mla_layer_bwd(dh4, dh4_16, s_mla, W, mla_tabs)
    stages = {}

    def scatter_start(tag, begun):
        started = _split_start(f"scatter_{tag}_start", begun[3], _got_shapes(begun[3]), 3 * len(begun[3]), _scatter_copies)
        stages[tag] = (begun, started[:4])
        return [started[4]]

    def scatter_end(tag, after):
        begun, started = stages[tag]
        return _reduce_end(begun, _split_wait(f"scatter_{tag}_wait", *started, _scatter_copies, after=after)[1], chip, ci)

    swap_a, token = _swap_begin({**g_mla, **g_t1}, "a")
    dh1, dh1_16, g_t0, n_t0 = _tail_bwd(dh3, s_tail0, p16, W, 0, "l0", after=[token],
                                        hook=lambda t: scatter_start("a", _swap_end(swap_a, ci, "a", [t])))
    d_ret_w_out = _d_ret_w_out(dh1_16, s_ret)
    swap_b, token = _swap_begin({**g_t0, "ret_w_out": d_ret_w_out}, "b")
    dx, _, n_ret = _ret_layer_bwd(
        dh1, dh1_16, s_ret, W, ret_tabs, after=[token], d_w_out=d_ret_w_out,
        hook=lambda t: scatter_start("b", _swap_end(swap_b, ci, "b", [t])),
        on_grads=lambda g: scatter_start("c", _reduce_begin({"ret_w_in": g["ret_w_in"]}, ci, "c")))
    red = {**scatter_end("a", [dx]), **scatter_end("b", [dx]), **scatter_end("c", [dx])}
    red = dict(zip(red, _share_halves(list(red.values()))))
    gs = _small_grads(n_ret, n_t0, n_mla, n_t1)
    small_g = jnp.concatenate([
        gs["mix_norm"], gs["mlp_norm"], gs["ple_norm"], gs["ret_gn"].reshape(2, PACK_W), _pad_row(gs["mla_q_a_norm"]),
        _pad_row(gs["mla_kv_a_norm"]), _pad_row(gs["mla_q_norm"][:, :MLA_QKD]), _pad_row(gs["mla_k_norm"][:, :MLA_QKD]),
        _pad_row(loss[:, :1]), jnp.zeros((3, PACK_W), F32)], axis=0)
    tot = _allsum_small(small_g, "sum_small_grads")
    gn_all = tot[6:8].reshape(RET_HEADS, n, -1)
    g_small = dict(
        mix_norm=tot[0:2], mlp_norm=tot[2:4], ple_norm=tot[4:6],
        ret_gn=lax.dynamic_index_in_dim(gn_all, chip, axis=1, keepdims=False),
        mla_q_a_norm=lax.dynamic_index_in_dim(tot[8, :MLA_Q_RANK].reshape(n, -1), chip, axis=0, keepdims=True),
        mla_kv_a_norm=lax.dynamic_index_in_dim(tot[9, :MLA_KV_RANK].reshape(n, -1), chip, axis=0, keepdims=True),
        mla_q_norm=tot[10:11, :MLA_QKD], mla_k_norm=tot[11:12, :MLA_QKD])
    loss_out = tot[12, 0]

    outs = []
    for k in _ORDER:
        if k in _TWO_LAYER:
            res = None
            for i in (1, 0):
                res = _adamw(w[k], red[f"{k}_{i}"], m[k], v[k], f"adamw_{k}_{i}", layers=2, layer=i, into=res)
        elif k in red:
            res = _adamw(w[k], _shard_grad(k, red[k], w[k].shape), m[k], v[k], f"adamw_{k}")
        else:
            res = _adamw(w[k], g_small[k], m[k], v[k], f"adamw_{k}")
        outs.append(res)
    return (loss_out, dx[None], *[o[0] for o in outs], *[o[1] for o in outs], *[o[2] for o in outs],
            *[o[3] for o in outs])
```

```python
import jax
import jax.numpy as jnp
import numpy as np
from jax import lax
from jax.experimental import pallas as pl
from jax.experimental.pallas import tpu as pltpu

F32 = jnp.float32
BF16 = jnp.bfloat16

EPS = 1e-6
D_MODEL = 1024
CHUNK = 64
ROPE_THETA = 10000.0
RET_HEADS = 4
RET_DK = 256
RET_DV = 512
RET_GROUP = 1
RET_BLOCK = 256
RET_ROWS = 1024
MLA_HEADS = 8
MLA_ROPE = 64
MLA_QKD = 192
MLA_VD = 128
MLA_HP = 256
MLA_Q_RANK = 384
MLA_KV_RANK = 256
MLA_IN = 704
MLA_IN_PAD = 768
N_CHIPS = 4

ADAM_LR = 0.001
ADAM_B1 = 0.9
ADAM_B2 = 0.999
ADAM_EPS = 1e-08
ADAM_WD = 0.01
ADAM_STEP = 10

VMEM_LIMIT = 56 * 1024 * 1024
PACK_W = 1024
NEG = -1e30
LOG2E = 1.4426950408889634
FLASH_T = 512
FLASH_HEADS = 4
MM_SUB_ROWS = 256
SUM_ROWS = 512
ADAM_ROWS = 512


def _cparams(sem=None):
    return pltpu.CompilerParams(dimension_semantics=sem, vmem_limit_bytes=VMEM_LIMIT)


def _pick(dim, pref):
    if dim <= pref:
        return dim
    t = pref
    while dim % t:
        t //= 2
    return t


def _mm(a, b, *, name, ta=False, tb=False, bblk=False, outs=None, extras=(), epilogue=None, dw=None,
        tm=1024, tn=512, after=()):
    if ta:
        K, M = a.shape
    else:
        M, K = a.shape
    if bblk and tb:
        nb, N, Kq = b.shape
        assert nb * Kq == K
    elif bblk:
        nb, Kb, Nq = b.shape
        N = nb * Nq
        assert Kb == K
    else:
        N = b.shape[0] if tb else b.shape[1]
    tn = _pick(Nq if (bblk and not tb) else N, tn)
    if dw is not None and dw[0] == "cols":
        tn = _pick(N // N_CHIPS, tn)
    tm = _pick(M // N_CHIPS if (dw is not None and dw[0] == "rows") else M, tm)
    grid = (M // tm, N // tn)

    a_spec = pl.BlockSpec((K, tm), lambda i, j: (0, i)) if ta else pl.BlockSpec((tm, K), lambda i, j: (i, 0))
    if bblk and tb:
        b_spec = pl.BlockSpec((nb, tn, Kq), lambda i, j: (0, j, 0))
    elif bblk:
        npb = Nq // tn
        b_spec = pl.BlockSpec((None, K, tn), lambda i, j: (j // npb, 0, j % npb))
    elif tb:
        b_spec = pl.BlockSpec((tn, K), lambda i, j: (j, 0))
    else:
        b_spec = pl.BlockSpec((K, tn), lambda i, j: (0, j))
    in_specs = [a_spec, b_spec] + [pl.BlockSpec((tm, tn), lambda i, j: (i, j)) for _ in extras]
    args = [a, b, *extras]
    aliases = {}
    if outs is None:
        outs = [F32]
    if dw is None:
        o_specs = [pl.BlockSpec((tm, tn), lambda i, j: (i, j)) for _ in outs]
        o_shapes = [jax.ShapeDtypeStruct((M, N), dt) for dt in outs]
    else:
        kind, layers, layer, into = dw
        if kind == "cols":
            per = (N // N_CHIPS) // tn
            o_specs = [pl.BlockSpec((None, None, tm, tn), lambda i, j: (j // per, layer, i, j % per))]
            o_shapes = [jax.ShapeDtypeStruct((N_CHIPS, layers, M, N // N_CHIPS), outs[0])]
        else:
            per = (M // N_CHIPS) // tm
            o_specs = [pl.BlockSpec((None, None, tm, tn), lambda i, j: (i // per, layer, i % per, j))]
            o_shapes = [jax.ShapeDtypeStruct((N_CHIPS, layers, M // N_CHIPS, N), outs[0])]
        if into is not None:
            aliases = {len(args): 0}
            in_specs.append(pl.BlockSpec(memory_space=pl.ANY))
            args.append(into)
    for t in after:
        in_specs.append(pl.BlockSpec(memory_space=pl.ANY))
        args.append(t)
    n_e, n_o = len(extras), len(outs)

    sub = _pick(tm, MM_SUB_ROWS)

    def body(a_ref, b_ref, *rest):
        e_refs, o_refs = rest[:n_e], rest[len(rest) - n_o:]
        for r0 in range(0, tm, sub):
            rows = slice(r0, r0 + sub)
            av = (a_ref[:, rows] if ta else a_ref[rows, :]).astype(BF16)
            if bblk and tb:
                acc = _dot_nt(av[:, :Kq], b_ref[0].astype(BF16))
                for s in range(1, nb):
                    acc = acc + _dot_nt(av[:, s * Kq:(s + 1) * Kq], b_ref[s].astype(BF16))
            elif ta:
                acc = _dot_tn(av, b_ref[...].astype(BF16))
            elif tb:
                acc = _dot_nt(av, b_ref[...].astype(BF16))
            else:
                acc = _dot(av, b_ref[...].astype(BF16))
            vals = (acc,) if epilogue is None else epilogue(acc, *[e[rows, :] for e in e_refs])
            for o, v in zip(o_refs, vals):
                o[rows, :] = v.astype(o.dtype)

    res = pl.pallas_call(
        body, name=name, grid=grid, in_specs=in_specs, out_specs=o_specs, out_shape=o_shapes,
        input_output_aliases=aliases, compiler_params=_cparams(("parallel", "arbitrary")),
    )(*args)
    return res[0] if n_o == 1 else res


def _mm_rows(a, b, *, name, epilogue, outs, tb=False, bblk=False, extras=(), fulls=(), accs=(), tm=512, after=()):
    M, K = a.shape
    tm = _pick(M, tm)
    sub = _pick(tm, MM_SUB_ROWS)
    nb = b.shape[0] if bblk else 1
    n_e, n_f, n_o, n_a = len(extras), len(fulls), len(outs), len(accs)
    n_in = 2 + n_e + n_f + len(after)

    def whole(t):
        return pl.BlockSpec(t.shape, lambda i, nd=t.ndim: (0,) * nd)

    in_specs = [pl.BlockSpec((tm, K), lambda i: (i, 0)), whole(b)]
    in_specs += [pl.BlockSpec((tm, e.shape[1]), lambda i: (i, 0)) for e in extras] + [whole(f) for f in fulls]
    in_specs += [pl.BlockSpec(memory_space=pl.ANY) for _ in after]
    out_specs = [pl.BlockSpec((tm, w), lambda i: (i, 0)) for w, _ in outs] + [pl.BlockSpec(s, lambda i: (0, 0)) for s, _ in accs]
    out_shape = [jax.ShapeDtypeStruct((M, w), dt) for w, dt in outs] + [jax.ShapeDtypeStruct(s, dt) for s, dt in accs]

    def body(a_ref, b_ref, *rest):
        e_refs, f_refs = rest[:n_e], rest[n_e:n_e + n_f]
        o_refs, acc_refs = rest[n_in - 2:n_in - 2 + n_o], rest[n_in - 2 + n_o:]
        fv = [f[...] for f in f_refs]
        totals = None
        for r0 in range(0, tm, sub):
            rows = slice(r0, r0 + sub)
            av = a_ref[rows, :].astype(BF16)
            if bblk and tb:
                kq = K // nb
                acc = _dot_nt(av[:, :kq], b_ref[0])
                for s in range(1, nb):
                    acc = acc + _dot_nt(av[:, s * kq:(s + 1) * kq], b_ref[s])
            elif bblk:
                acc = jnp.concatenate([_dot(av, b_ref[s]) for s in range(nb)], axis=-1)
            elif tb:
                acc = _dot_nt(av, b_ref[...])
            else:
                acc = _dot(av, b_ref[...])
            vals = epilogue(acc, *[e[rows, :] for e in e_refs], *fv)
            for o, v in zip(o_refs, vals[:n_o]):
                o[rows, :] = v.astype(o.dtype)
            part = vals[n_o:]
            totals = part if totals is None else [t + p for t, p in zip(totals, part)]
        first_step = pl.program_id(0) == 0
        for o, v in zip(acc_refs, totals):
            @pl.when(first_step)
            def _(o=o, v=v):
                o[...] = v.astype(o.dtype)

            @pl.when(jnp.logical_not(first_step))
            def _(o=o, v=v):
                o[...] += v.astype(o.dtype)

    return pl.pallas_call(
        body, name=name, grid=(M // tm,), in_specs=in_specs, out_specs=out_specs, out_shape=out_shape,
        compiler_params=_cparams(("arbitrary",)),
    )(a, b, *extras, *fulls, *after)


def _rows(fn, rows, fulls, outs, accs=(), *, name, tile=512, after=()):
    first = rows[0][0] if isinstance(rows[0], tuple) else rows[0]
    T = first.shape[0]
    tile = _pick(T, tile)
    in_specs, args = [], []
    for r in rows:
        if isinstance(r, tuple):
            arr, w, cb = r
            in_specs.append(pl.BlockSpec((tile, w), lambda i, cb=cb: (i, cb)))
        else:
            arr = r
            in_specs.append(pl.BlockSpec((tile, arr.shape[1]), lambda i: (i, 0)))
        args.append(arr)
    for f in fulls:
        in_specs.append(pl.BlockSpec(f.shape, lambda i, nd=f.ndim: (0,) * nd))
        args.append(f)
    outs = [o if len(o) == 4 else (*o, o[0], 0) for o in outs]
    out_specs = [pl.BlockSpec((tile, w), lambda i, cb=cb: (i, cb)) for w, _, _, cb in outs]
    out_specs += [pl.BlockSpec(s, lambda i: (0, 0)) for s, _ in accs]
    out_shape = [jax.ShapeDtypeStruct((T, tw), dt) for _, dt, tw, _ in outs]
    out_shape += [jax.ShapeDtypeStruct(s, dt) for s, dt in accs]
    n_in, n_out = len(args), len(outs)
    for t in after:
        in_specs.append(pl.BlockSpec(memory_space=pl.ANY))
        args.append(t)

    def body(*refs):
        vals = fn(*[r[...] for r in refs[:n_in]])
        o_refs = refs[len(args):]
        for o, v in zip(o_refs[:n_out], vals[:n_out]):
            o[...] = v.astype(o.dtype)
        first_step = pl.program_id(0) == 0
        for o, v in zip(o_refs[n_out:], vals[n_out:]):
            @pl.when(first_step)
            def _(o=o, v=v):
                o[...] = v.astype(o.dtype)

            @pl.when(jnp.logical_not(first_step))
            def _(o=o, v=v):
                o[...] += v.astype(o.dtype)

    res = pl.pallas_call(
        body, name=name, grid=(T // tile,), in_specs=in_specs, out_specs=out_specs, out_shape=out_shape,
        compiler_params=_cparams(("arbitrary",)),
    )(*args)
    return res


def _rowsum(v, mxu):
    if not mxu:
        return jnp.sum(v, axis=-1, keepdims=True)
    ones = jnp.ones((v.shape[1], v.shape[1]), BF16)
    hi = v.astype(BF16)
    lo = (v - hi.astype(F32)).astype(BF16)
    return _dot(hi, ones) + _dot(lo, ones)


def _rms(x, g, mxu=False):
    r = lax.rsqrt(_rowsum(x * x, mxu) / x.shape[-1] + EPS)
    return (x * r) * g


def _rms_bwd(x, dy, g, n=None, mxu=False):
    n = x.shape[-1] if n is None else n
    r = lax.rsqrt(_rowsum(x * x, mxu) / n + EPS)
    xh = x * r
    dxh = dy * g
    dx = r * (dxh - xh * (_rowsum(dxh * xh, mxu) / n))
    return dx, dy * xh


def _colsum(v):
    return jnp.sum(v, axis=0, keepdims=True)


def _sigmoid(x):
    return 1.0 / (1.0 + jnp.exp(-x))


def _widen(v, width):
    reps = width // v.shape[1]
    return v if reps == 1 else jnp.concatenate([v] * reps, axis=-1)


def _rope_angles(T, dim):
    inv = (1.0 / (np.float32(ROPE_THETA) ** (np.arange(0, dim, 2, dtype=np.float32) / np.float32(dim)))).astype(np.float32)
    return np.arange(T, dtype=np.float32)[:, None] * inv[None, :]


def _ret_tables(T):
    ang = _rope_angles(T, RET_DK)
    log_gamma = np.log(np.float32(1.0) - np.float32(2.0) ** (-5.0 - np.arange(RET_HEADS, dtype=np.float32)))
    idx = np.arange(RET_BLOCK, dtype=np.float32)
    chunk = np.arange(RET_BLOCK) // CHUNK
    dist = idx[:, None] - idx[None, :]
    seen = np.where(chunk[:, None] == chunk[None, :], np.abs(dist), np.where(chunk[:, None] > chunk[None, :], dist, np.inf))
    intra = np.exp(log_gamma[:, None, None] * seen[None].astype(np.float32))
    qd = np.exp(log_gamma[:, None] * (idx + 1.0))[:, :, None]
    kd = np.exp(log_gamma[:, None] * (RET_BLOCK - 1.0 - idx))[:, :, None]
    cd = np.exp(log_gamma * RET_BLOCK)[:, None, None]
    return tuple(jnp.asarray(t, F32) for t in (np.cos(ang), np.sin(ang), intra, qd, kd, cd))


def _rope_half(x, c, s):
    x1, x2 = x[:, :RET_DK // 2], x[:, RET_DK // 2:]
    return jnp.concatenate([x1 * c - x2 * s, x2 * c + x1 * s], axis=-1)


def _rope_half_bwd(d, c, s):
    d1, d2 = d[:, :RET_DK // 2], d[:, RET_DK // 2:]
    return jnp.concatenate([d1 * c + d2 * s, d2 * c - d1 * s], axis=-1)


def _dot(a, b):
    return lax.dot_general(a, b, (((1,), (0,)), ((), ())), preferred_element_type=F32)


def _dot_nt(a, b):
    return lax.dot_general(a, b, (((1,), (1,)), ((), ())), preferred_element_type=F32)


def _dot_tn(a, b):
    return lax.dot_general(a, b, (((0,), (0,)), ((), ())), preferred_element_type=F32)


def _ret_specs(T, tb, rev):
    nj = T // tb
    jj = (lambda j: nj - 1 - j) if rev else (lambda j: j)
    g = RET_GROUP
    kq = RET_HEADS // g
    vq = 2 * RET_HEADS * RET_DK // (g * RET_DV)
    return dict(
        q=pl.BlockSpec((tb, g * RET_DK), lambda h, j: (jj(j), h)),
        k=pl.BlockSpec((tb, g * RET_DK), lambda h, j: (jj(j), kq + h)),
        v=pl.BlockSpec((tb, g * RET_DV), lambda h, j: (jj(j), vq + h)),
        tab=pl.BlockSpec((tb, RET_DK // 2), lambda h, j: (jj(j), 0)),
        intra=pl.BlockSpec((g, RET_BLOCK, RET_BLOCK), lambda h, j: (h, 0, 0)),
        dec=pl.BlockSpec((g, RET_BLOCK, 1), lambda h, j: (h, 0, 0)),
        cd=pl.BlockSpec((g, 1, 1), lambda h, j: (h, 0, 0)),
        o=pl.BlockSpec((tb, g * RET_DV), lambda h, j: (jj(j), h)),
        s=pl.BlockSpec((g, tb // RET_BLOCK, RET_DK, RET_DV), lambda h, j: (h, jj(j), 0, 0)),
    )


def _ret_fwd(proj, tabs, name):
    T = proj.shape[0]
    cos, sin, intra, qd, kd, cd = tabs
    tb = _pick(T, RET_ROWS)
    cps = tb // RET_BLOCK
    sp = _ret_specs(T, tb, False)
    scale = RET_DK ** -0.5

    def body(q_ref, k_ref, v_ref, cos_ref, sin_ref, intra_ref, qd_ref, kd_ref, cd_ref, o_ref, s_ref, state):
        @pl.when(pl.program_id(1) == 0)
        def _():
            state[...] = jnp.zeros_like(state)

        for c in range(cps):
            rows = pl.ds(c * RET_BLOCK, RET_BLOCK)
            co, si = cos_ref[rows, :], sin_ref[rows, :]
            for h in range(RET_GROUP):
                hk, hv = slice(h * RET_DK, (h + 1) * RET_DK), slice(h * RET_DV, (h + 1) * RET_DV)
                q = _rope_half(q_ref[rows, hk].astype(F32), co, si)
                k = _rope_half(k_ref[rows, hk].astype(F32), co, si) * scale
                vb = v_ref[rows, hv].astype(BF16)
                st = state[h]
                sb = st.astype(BF16)
                s_ref[h, c] = sb
                sc = _dot_nt(q.astype(BF16), k.astype(BF16)) * intra_ref[h]
                inner = _dot(sc.astype(BF16), vb)
                cross = _dot((q * qd_ref[h]).astype(BF16), sb)
                o_ref[rows, hv] = inner + cross
                state[h] = st * cd_ref[h] + _dot_tn((k * kd_ref[h]).astype(BF16), vb)

    return pl.pallas_call(
        body, name=name, grid=(RET_HEADS // RET_GROUP, T // tb),
        in_specs=[sp["q"], sp["k"], sp["v"], sp["tab"], sp["tab"], sp["intra"], sp["dec"], sp["dec"], sp["cd"]],
        out_specs=[sp["o"], sp["s"]],
        out_shape=[jax.ShapeDtypeStruct((T, RET_HEADS * RET_DV), F32),
                   jax.ShapeDtypeStruct((RET_HEADS, T // RET_BLOCK, RET_DK, RET_DV), BF16)],
        scratch_shapes=[pltpu.VMEM((RET_GROUP, RET_DK, RET_DV), F32)],
        compiler_params=_cparams(("arbitrary", "arbitrary")),
    )(proj, proj, proj, cos, sin, intra, qd, kd, cd)


def _ret_bwd(proj, states, dout, dproj, tabs, name):
    assert RET_GROUP == 1
    T = proj.shape[0]
    cos, sin, intra, qd, kd, cd = tabs
    tb = _pick(T, RET_ROWS)
    cps = tb // RET_BLOCK
    nj = T // tb
    sp = _ret_specs(T, tb, True)
    scale = RET_DK ** -0.5
    k0, v0 = RET_HEADS * RET_DK, 2 * RET_HEADS * RET_DK

    def body(q_ref, k_ref, v_ref, cos_ref, sin_ref, intra_ref, qd_ref, kd_ref, cd_ref, s_ref, do_ref, _dproj_in,
             out_ref, dq_s, dk_s, dv_s, sems, dstate):
        head, j = pl.program_id(0), pl.program_id(1)
        step = head * nj + j
        slot = step % 2
        dq_ref, dk_ref, dv_ref = dq_s.at[slot], dk_s.at[slot], dv_s.at[slot]

        @pl.when(j == 0)
        def _():
            dstate[...] = jnp.zeros_like(dstate)

        for c in reversed(range(cps)):
            rows = pl.ds(c * RET_BLOCK, RET_BLOCK)
            co, si = cos_ref[rows, :], sin_ref[rows, :]
            for h in range(RET_GROUP):
                hk, hv = slice(h * RET_DK, (h + 1) * RET_DK), slice(h * RET_DV, (h + 1) * RET_DV)
                q = _rope_half(q_ref[rows, hk].astype(F32), co, si)
                k = _rope_half(k_ref[rows, hk].astype(F32), co, si) * scale
                qb, kb = q.astype(BF16), k.astype(BF16)
                vb = v_ref[rows, hv].astype(BF16)
                dob = do_ref[rows, hv].astype(BF16)
                sb = s_ref[h, c]
                ia = intra_ref[h]
                pb = (_dot_nt(qb, kb) * ia).astype(BF16)
                dsn = dstate[h]
                dsb = dsn.astype(BF16)
                kdk = (k * kd_ref[h]).astype(BF16)
                qdq = (q * qd_ref[h]).astype(BF16)
                dv = _dot_tn(pb, dob) + _dot(kdk, dsb)
                dpb = (_dot_nt(dob, vb) * ia).astype(BF16)
                dq = _dot(dpb, kb) + _dot_nt(dob, sb) * qd_ref[h]
                dk = _dot_tn(dpb, qb) + _dot_nt(vb, dsb) * kd_ref[h]
                dstate[h] = dsn * cd_ref[h] + _dot_tn(qdq, dob)
                dq_ref[rows, hk] = _rope_half_bwd(dq, co, si).astype(BF16)
                dk_ref[rows, hk] = _rope_half_bwd(dk * scale, co, si).astype(BF16)
                dv_ref[rows, hv] = dv.astype(BF16)

        def copies(sl):
            r = pl.ds(pl.multiple_of((nj - 1 - j) * tb, tb), tb)
            cols = lambda first, w: pl.ds(pl.multiple_of(first + head * w, 128), w)
            return [pltpu.make_async_copy(dq_s.at[sl], out_ref.at[r, cols(0, RET_DK)], sems.at[sl, 0]),
                    pltpu.make_async_copy(dk_s.at[sl], out_ref.at[r, cols(k0, RET_DK)], sems.at[sl, 1]),
                    pltpu.make_async_copy(dv_s.at[sl], out_ref.at[r, cols(v0, RET_DV)], sems.at[sl, 2])]

        @pl.when(step > 0)
        def _():
            for cp in copies(1 - slot):
                cp.wait()

        for cp in copies(slot):
            cp.start()

        @pl.when(step == RET_HEADS * nj - 1)
        def _():
            for cp in copies(slot):
                cp.wait()

    return pl.pallas_call(
        body, name=name, grid=(RET_HEADS, nj),
        in_specs=[sp["q"], sp["k"], sp["v"], sp["tab"], sp["tab"], sp["intra"], sp["dec"], sp["dec"], sp["cd"],
                  sp["s"], sp["o"], pl.BlockSpec(memory_space=pl.ANY)],
        out_specs=pl.BlockSpec(memory_space=pl.ANY), out_shape=jax.ShapeDtypeStruct(dproj.shape, dproj.dtype),
        input_output_aliases={11: 0},
        scratch_shapes=[pltpu.VMEM((2, tb, RET_DK), BF16), pltpu.VMEM((2, tb, RET_DK), BF16),
                        pltpu.VMEM((2, tb, RET_DV), BF16), pltpu.SemaphoreType.DMA((2, 3)),
                        pltpu.VMEM((RET_GROUP, RET_DK, RET_DV), F32)],
        compiler_params=_cparams(("arbitrary", "arbitrary")),
    )(proj, proj, proj, cos, sin, intra, qd, kd, cd, states, dout, dproj)


def _ret_gate(out, proj, gn, name):
    def fn(o, g, *gains):
        g = g.astype(F32)
        parts = [_rms(o[:, h * RET_DV:(h + 1) * RET_DV], gains[h]) for h in range(RET_HEADS)]
        return (g * _sigmoid(g) * jnp.concatenate(parts, axis=-1),)
    w = RET_HEADS * RET_DV
    return _rows(fn, [out, (proj, w, 2)], [gn[h:h + 1] for h in range(RET_HEADS)], [(w, BF16)], name=name)[0]


def _ret_gate_bwd(out, proj, gn, dy, name, after=()):
    def fn(o, g, d, *gains):
        g = g.astype(F32)
        sg = _sigmoid(g)
        silu = g * sg
        dsilu = sg * (1.0 + g * (1.0 - sg))
        dos, dgs = [], []
        row = lax.broadcasted_iota(jnp.int32, (RET_HEADS, RET_DV), 0)
        dgn = jnp.zeros((RET_HEADS, RET_DV), F32)
        for h in range(RET_HEADS):
            sl = slice(h * RET_DV, (h + 1) * RET_DV)
            oh = o[:, sl]
            dgs.append(d[:, sl] * _rms(oh, gains[h]) * dsilu[:, sl])
            dx, dg = _rms_bwd(oh, d[:, sl] * silu[:, sl], gains[h])
            dos.append(dx)
            dgn = dgn + jnp.where(row == h, _colsum(dg), 0.0)
        return jnp.concatenate(dos, axis=-1), jnp.concatenate(dgs, axis=-1), dgn
    w = RET_HEADS * RET_DV
    return _rows(fn, [out, (proj, w, 2), dy], [gn[h:h + 1] for h in range(RET_HEADS)],
                 [(w, BF16), (w, BF16, proj.shape[1], 2)], [((RET_HEADS, RET_DV), F32)], name=name, tile=256,
                 after=after)


def _mla_tables(T):
    ang = _rope_angles(T, MLA_ROPE)
    c, s = np.cos(ang), np.sin(ang)
    z32, z64 = np.zeros((T, 32), np.float32), np.zeros((T, 64), np.float32)
    cos_t = np.concatenate([c, c, z64], axis=1)
    sin_a = np.concatenate([-s, z32, z64], axis=1)
    sin_b = np.concatenate([z32, s, z64], axis=1)
    return tuple(jnp.asarray(t, F32) for t in (cos_t, sin_a, sin_b))


def _rope_blk(x, ct, sa, sb):
    return x * ct + pltpu.roll(x, 96, 1) * sa + pltpu.roll(x, 32, 1) * sb


def _rope_blk_bwd(d, ct, sa, sb):
    return d * ct + pltpu.roll(d * sa, 32, 1) + pltpu.roll(d * sb, 96, 1)


def _head_norm(x, gain):
    r = lax.rsqrt(_rowsum(x * x, True) / MLA_QKD + EPS)
    return (x * r) * gain


def _prep_heads(qv, kvv, kr, ct, sa, sb, gqv, gkv):
    qs, ks, vs = [], [], []
    for h in range(MLA_HEADS):
        b = h * MLA_HP
        y = _head_norm(qv[:, b:b + MLA_HP], gqv)
        qs += [y[:, :128], _rope_blk(y[:, 128:], ct, sa, sb)]
        y = _head_norm(jnp.concatenate([kvv[:, b:b + 128], kr], axis=-1), gkv)
        ks += [y[:, :128], _rope_blk(y[:, 128:], ct, sa, sb)]
        vs.append(kvv[:, b + 128:b + 256])
    return jnp.concatenate(qs, axis=-1), jnp.concatenate(ks, axis=-1), jnp.concatenate(vs, axis=-1)


def _mla_front(hn, W, tabs, name):
    wide = MLA_HEADS * MLA_HP
    gq = W["mla_q_norm"] * (MLA_QKD ** -0.5 * LOG2E)

    def epilogue(acc, ct, sa, sb, gqa, gkva, wuq, wukv, gqv, gkv):
        cqn = _rms(acc[:, :MLA_Q_RANK], gqa).astype(BF16)
        ckvn = _rms(acc[:, MLA_Q_RANK:MLA_Q_RANK + MLA_KV_RANK], gkva).astype(BF16)
        q = jnp.concatenate([_dot(cqn, wuq[s]) for s in range(N_CHIPS)], axis=-1).astype(BF16)
        kv = jnp.concatenate([_dot(ckvn, wukv[s]) for s in range(N_CHIPS)], axis=-1).astype(BF16)
        qf, kf, vf = _prep_heads(q.astype(F32), kv.astype(F32), acc[:, MLA_IN_PAD - 128:], ct, sa, sb, gqv, gkv)
        return acc, cqn, ckvn, q, kv, qf, kf, vf

    return _mm_rows(hn, W["mla_w_in"], extras=list(tabs),
                    fulls=[W["mla_q_a_norm"], W["mla_kv_a_norm"], W["mla_w_uq"], W["mla_w_ukv"], gq, W["mla_k_norm"]],
                    outs=[(MLA_IN_PAD, F32), (MLA_Q_RANK, BF16), (MLA_KV_RANK, BF16), (wide, BF16), (wide, BF16),
                          (wide, BF16), (wide, BF16), (MLA_HEADS * MLA_VD, BF16)],
                    epilogue=epilogue, name=name, tm=256)


def _prep_heads_bwd(qv, kvv, kr, ct, sa, sb, dqv, dkv, dvv, gqv, gkv):
    dqs, dkvs = [], []
    dkr = jnp.zeros_like(kr)
    dgq = jnp.zeros((1, MLA_HP), F32)
    dgk = jnp.zeros((1, MLA_HP), F32)
    for h in range(MLA_HEADS):
        b = h * MLA_HP
        dy = jnp.concatenate([dqv[:, b:b + 128], _rope_blk_bwd(dqv[:, b + 128:b + 256], ct, sa, sb)], axis=-1)
        dx, dg = _rms_bwd(qv[:, b:b + MLA_HP], dy, gqv, MLA_QKD, mxu=True)
        dqs.append(dx)
        dgq = dgq + _colsum(dg)
        dy = jnp.concatenate([dkv[:, b:b + 128], _rope_blk_bwd(dkv[:, b + 128:b + 256], ct, sa, sb)], axis=-1)
        dx, dg = _rms_bwd(jnp.concatenate([kvv[:, b:b + 128], kr], axis=-1), dy, gkv, MLA_QKD, mxu=True)
        dkvs += [dx[:, :128], dvv[:, h * MLA_VD:(h + 1) * MLA_VD].astype(F32)]
        dkr = dkr + dx[:, 128:]
        dgk = dgk + _colsum(dg)
    return jnp.concatenate(dqs, axis=-1), jnp.concatenate(dkvs, axis=-1), dkr, dgq, dgk


def _mla_back(q, kv, proj, h0, dh1, dqf, dkf, dvf, W, tabs, name):
    def fn(qv, kvv, pv, hv, dr, ct, sa, sb, dqv, dkv, dvv, gqv, gkv, gqa, gkva, wuq, wukv, w_in, g_mix):
        qv, kvv, dqv, dkv = (t.astype(F32) for t in (qv, kvv, dqv, dkv))
        dq, dkvx, dkr, dgq, dgk = _prep_heads_bwd(qv, kvv, pv[:, MLA_IN_PAD - 128:], ct, sa, sb, dqv, dkv, dvv, gqv, gkv)
        dq, dkvx = dq.astype(BF16), dkvx.astype(BF16)
        nq = wuq.shape[2]
        dcq = sum(_dot_nt(dq[:, s * nq:(s + 1) * nq], wuq[s]) for s in range(N_CHIPS))
        dckv = sum(_dot_nt(dkvx[:, s * nq:(s + 1) * nq], wukv[s]) for s in range(N_CHIPS))
        dxq, dgqa = _rms_bwd(pv[:, :MLA_Q_RANK], dcq, gqa)
        dxkv, dgkva = _rms_bwd(pv[:, MLA_Q_RANK:MLA_Q_RANK + MLA_KV_RANK], dckv, gkva)
        dproj = jnp.concatenate([dxq, dxkv, dkr], axis=-1).astype(BF16)
        dx, dgm = _rms_bwd(hv, _dot_nt(dproj, w_in), g_mix)
        return (dq, dkvx, dproj, dr + dx, dr + dx, dgq, dgk, _colsum(dgqa), _colsum(dgkva), _colsum(dgm))

    wide = MLA_HEADS * MLA_HP
    return _rows(fn, [q, kv, proj, h0, dh1, *tabs, dqf, dkf, dvf],
                 [W["mla_q_norm"], W["mla_k_norm"], W["mla_q_a_norm"], W["mla_kv_a_norm"], W["mla_w_uq"], W["mla_w_ukv"],
                  W["mla_w_in"], W["mix_norm"][1:2]],
                 [(wide, BF16), (wide, BF16), (MLA_IN_PAD, BF16), ROW_F32, ROW_BF16],
                 [((1, MLA_HP), F32), ((1, MLA_HP), F32), ((1, MLA_Q_RANK), F32), ((1, MLA_KV_RANK), F32),
                  ((1, D_MODEL), F32)], name=name, tile=256)


def _chunk_mask(qi, ki, tq, tk):
    shift = CHUNK.bit_length() - 1
    rq = lax.shift_right_arithmetic(qi * tq + lax.broadcasted_iota(jnp.int32, (tq, tk), 0), shift)
    ck = lax.shift_right_arithmetic(ki * tk + lax.broadcasted_iota(jnp.int32, (tq, tk), 1), shift)
    return ck <= rq


def _flash_fwd(qf, kf, vf, name):
    T = qf.shape[0]
    t = _pick(T, FLASH_T)
    n = T // t
    g = FLASH_HEADS

    def body(q_ref, k_ref, v_ref, o_ref, lse_ref, m_s, l_s, acc):
        qi = pl.program_id(1)
        m_s[...] = jnp.full_like(m_s, NEG)
        l_s[...] = jnp.zeros_like(l_s)
        acc[...] = jnp.zeros_like(acc)

        def step(kb, masked):
            rows = pl.ds(pl.multiple_of(kb * t, t), t)
            for h in range(g):
                hq, hv = slice(h * MLA_HP, (h + 1) * MLA_HP), slice(h * MLA_VD, (h + 1) * MLA_VD)
                s = _dot_nt(q_ref[:, hq], k_ref[rows, hq])
                if masked:
                    s = jnp.where(_chunk_mask(0, 0, t, t), s, NEG)
                m_prev = m_s[:, hv]
                m_new = jnp.maximum(m_prev, jnp.max(s, axis=-1, keepdims=True))
                alpha = jnp.exp2(m_prev - m_new)
                p = jnp.exp2(s - _widen(m_new, t))
                l_s[:, hv] = alpha * l_s[:, hv] + sum(p[:, i * 128:(i + 1) * 128] for i in range(t // 128))
                acc[:, hv] = acc[:, hv] * alpha + _dot(p.astype(BF16), v_ref[rows, hv])
                m_s[:, hv] = m_new

        @pl.loop(0, qi)
        def _(kb):
            step(kb, False)

        step(qi, True)
        for h in range(g):
            hv = slice(h * MLA_VD, (h + 1) * MLA_VD)
            l = jnp.sum(l_s[:, hv], axis=-1, keepdims=True)
            o_ref[:, hv] = acc[:, hv] / l
            lse_ref[:, hv] = m_s[:, hv] + jnp.log2(l)

    qmap = lambda h, i: (i, h)
    kmap = lambda h, i: (0, h)
    vec = pltpu.VMEM((t, g * MLA_VD), F32)
    return pl.pallas_call(
        body, name=name, grid=(MLA_HEADS // g, n),
        in_specs=[pl.BlockSpec((t, g * MLA_HP), qmap), pl.BlockSpec((T, g * MLA_HP), kmap),
                  pl.BlockSpec((T, g * MLA_VD), kmap)],
        out_specs=[pl.BlockSpec((t, g * MLA_VD), qmap), pl.BlockSpec((t, g * MLA_VD), qmap)],
        out_shape=[jax.ShapeDtypeStruct((T, MLA_HEADS * MLA_VD), F32),
                   jax.ShapeDtypeStruct((T, MLA_HEADS * MLA_VD), F32)],
        scratch_shapes=[vec, vec, vec],
        compiler_params=_cparams(("parallel", "arbitrary")),
    )(qf, kf, vf)


def _flash_bwd(qf, kf, vf, do16, lse, delta, name):
    T = qf.shape[0]
    t = _pick(T, FLASH_T)
    n = T // t
    scale = MLA_QKD ** -0.5

    def body(q_ref, k_ref, v_ref, do_ref, lse_ref, dl_ref, dq_out, dk_out, dv_out, dq_ref, dk_ref, dv_ref):
        kb = pl.program_id(1)

        @pl.when(kb == 0)
        def _():
            dq_ref[...] = jnp.zeros_like(dq_ref)

        dk_ref[...] = jnp.zeros_like(dk_ref)
        dv_ref[...] = jnp.zeros_like(dv_ref)
        k, v = k_ref[...], v_ref[...]

        def step(qb, masked):
            rows = pl.ds(pl.multiple_of(qb * t, t), t)
            q, dob = q_ref[rows, :], do_ref[rows, :]
            s = _dot_nt(q, k)
            if masked:
                s = jnp.where(_chunk_mask(0, 0, t, t), s, NEG)
            p = jnp.exp2(s - _widen(lse_ref[rows, :], t))
            ds = (p * (_dot_nt(dob, v) - _widen(dl_ref[rows, :], t))).astype(BF16)
            dv_ref[...] += _dot_tn(p.astype(BF16), dob)
            dk_ref[...] += _dot_tn(ds, q)
            dq_ref[rows, :] += _dot(ds, k)

        step(kb, True)

        @pl.loop(kb + 1, n)
        def _(qb):
            step(qb, False)

        dk_out[...] = (dk_ref[...] * (1.0 / LOG2E)).astype(BF16)
        dv_out[...] = dv_ref[...].astype(BF16)

        @pl.when(kb == n - 1)
        def _():
            dq_out[...] = (dq_ref[...] * scale).astype(BF16)

    qmap = lambda h, j: (0, h)
    kmap = lambda h, j: (j, h)
    return pl.pallas_call(
        body, name=name, grid=(MLA_HEADS, n),
        in_specs=[pl.BlockSpec((T, MLA_HP), qmap), pl.BlockSpec((t, MLA_HP), kmap), pl.BlockSpec((t, MLA_VD), kmap),
                  pl.BlockSpec((T, MLA_VD), qmap), pl.BlockSpec((T, MLA_VD), qmap), pl.BlockSpec((T, MLA_VD), qmap)],
        out_specs=[pl.BlockSpec((T, MLA_HP), qmap), pl.BlockSpec((t, MLA_HP), kmap), pl.BlockSpec((t, MLA_VD), kmap)],
        out_shape=[jax.ShapeDtypeStruct((T, MLA_HEADS * MLA_HP), BF16),
                   jax.ShapeDtypeStruct((T, MLA_HEADS * MLA_HP), BF16),
                   jax.ShapeDtypeStruct((T, MLA_HEADS * MLA_VD), BF16)],
        scratch_shapes=[pltpu.VMEM((T, MLA_HP), F32), pltpu.VMEM((t, MLA_HP), F32), pltpu.VMEM((t, MLA_VD), F32)],
        compiler_params=_cparams(("arbitrary", "arbitrary")),
    )(qf, kf, vf, do16, lse, delta)


MESH = pl.DeviceIdType.MESH
ANY = pl.BlockSpec(memory_space=pl.ANY)
_CHIP_FLIPS = ((1, 0), (0, 1), (1, 1))


def _place():
    return lax.axis_index("x"), lax.axis_index("y"), lax.axis_index("c")


def _other_chip(x, y, k):
    fx, fy = _CHIP_FLIPS[k]
    return ((1 - x) if fx else x), ((1 - y) if fy else y)


def _remote(src, dst, send_sems, recv_sems, k, to):
    return pltpu.make_async_remote_copy(src_ref=src, dst_ref=dst, send_sem=send_sems.at[k], recv_sem=recv_sems.at[k],
                                        device_id=to, device_id_type=MESH)


def _index(*vals):
    return jnp.stack(vals).astype(jnp.int32)


def _half(c, rows):
    return pl.ds(pl.multiple_of(c * rows, 16), rows)


def _gather_weights(parts, name, landed=None):
    n_w = len(parts)
    n_in = n_w if landed is None else 2 * n_w

    def body(*refs):
        ins, outs = refs[:n_w], refs[n_in:n_in + n_w]
        send_sems, recv_sems, local_sems = refs[n_in + n_w:]
        x, y, c = _place()
        j = 2 * x + y
        sibling = (x, y, 1 - c)
        chips = [_other_chip(x, y, k) for k in range(3)]
        pending = []
        for w in range(n_w):
            own = pltpu.make_async_copy(ins[w], outs[w].at[j], local_sems.at[w])
            own.start()
            pending.append(own)
        sent = []
        for w in range(n_w):
            if landed is not None:
                break
            r = _half(c, parts[w].shape[0] // 2)
            for k, (px, py) in enumerate(chips):
                cp = _remote(ins[w].at[r], outs[w].at[j, r], send_sems, recv_sems, 6 * w + k, (px, py, c))
                cp.start()
                sent.append(cp)
        for w in range(n_w):
            r = _half(c, parts[w].shape[0] // 2)
            for k, (px, py) in enumerate(chips):
                blk = outs[w].at[2 * px + py, r]
                if landed is None:
                    _remote(blk, blk, send_sems, recv_sems, 6 * w + k, (px, py, c)).wait_recv()
                cp = _remote(blk, blk, send_sems, recv_sems, 6 * w + 3 + k, sibling)
                cp.start()
                sent.append(cp)
        for w in range(n_w):
            r = _half(1 - c, parts[w].shape[0] // 2)
            for k, (px, py) in enumerate(chips):
                blk = outs[w].at[2 * px + py, r]
                _remote(blk, blk, send_sems, recv_sems, 6 * w + 3 + k, sibling).wait_recv()
        for cp in sent:
            cp.wait_send()
        for cp in pending:
            cp.wait()

    return pl.pallas_call(
        body, name=name, in_specs=[pl.BlockSpec(memory_space=pltpu.VMEM)] * n_w + [ANY] * (n_in - n_w),
        out_specs=[ANY] * n_w,
        out_shape=[jax.ShapeDtypeStruct((N_CHIPS, *p.shape), p.dtype) for p in parts],
        input_output_aliases={} if landed is None else {n_w + w: w for w in range(n_w)},
        scratch_shapes=[pltpu.SemaphoreType.DMA((6 * n_w,)), pltpu.SemaphoreType.DMA((6 * n_w,)),
                        pltpu.SemaphoreType.DMA((n_w,))],
        compiler_params=pltpu.CompilerParams(vmem_limit_bytes=VMEM_LIMIT),
    )(*parts, *(landed or []))


def _swap_halves(gs, name):
    n_w = len(gs)

    def body(*refs):
        g_refs, recv_refs = refs[:n_w], refs[n_w:2 * n_w]
        send_sems, recv_sems = refs[2 * n_w:]
        x, y, c = _place()
        sent = []
        for w in range(n_w):
            for jj in range(N_CHIPS):
                cp = _remote(g_refs[w].at[jj, 1 - c], recv_refs[w].at[jj], send_sems, recv_sems, N_CHIPS * w + jj,
                             (x, y, 1 - c))
                cp.start()
                sent.append(cp)
        for cp in sent:
            cp.wait()

    return pl.pallas_call(
        body, name=name, in_specs=[ANY] * n_w, out_specs=[ANY] * n_w,
        out_shape=[jax.ShapeDtypeStruct((N_CHIPS, *g.shape[2:]), g.dtype) for g in gs],
        scratch_shapes=[pltpu.SemaphoreType.DMA((N_CHIPS * n_w,)), pltpu.SemaphoreType.DMA((N_CHIPS * n_w,))],
    )(*gs)


def _pair_sum(g, recv, core, name):
    _, H, C = recv.shape
    tile = _pick(H, SUM_ROWS)

    def body(c_ref, own_ref, recv_ref, out_ref):
        out_ref[...] = (own_ref[...].astype(F32) + recv_ref[...].astype(F32)).astype(BF16)

    blk = pl.BlockSpec((None, tile, C), lambda jj, i, c: (jj, i, 0))
    return pl.pallas_call(
        body, name=name,
        grid_spec=pltpu.PrefetchScalarGridSpec(
            num_scalar_prefetch=1, grid=(N_CHIPS, H // tile),
            in_specs=[pl.BlockSpec((None, None, tile, C), lambda jj, i, c: (jj, c[0], i, 0)), blk],
            out_specs=blk),
        out_shape=jax.ShapeDtypeStruct((N_CHIPS, H, C), BF16),
        compiler_params=_cparams(("arbitrary", "arbitrary")),
    )(_index(core), g, recv)


def _chip_sum(g, recv, got, chip, core, name):
    _, H, C = recv.shape
    tile = _pick(H, SUM_ROWS)

    def body(s_ref, own_ref, recv_ref, g0_ref, g1_ref, g2_ref, out_ref):
        pair = own_ref[...].astype(F32) + recv_ref[...].astype(F32)
        out_ref[...] = ((pair + g0_ref[...].astype(F32)) + g1_ref[...].astype(F32)) + g2_ref[...].astype(F32)

    def got_spec(k):
        return pl.BlockSpec((None, tile, C), lambda i, s, k=k: (k, i, 0))

    return pl.pallas_call(
        body, name=name,
        grid_spec=pltpu.PrefetchScalarGridSpec(
            num_scalar_prefetch=1, grid=(H // tile,),
            in_specs=[pl.BlockSpec((None, None, tile, C), lambda i, s: (s[0], s[1], i, 0)),
                      pl.BlockSpec((None, tile, C), lambda i, s: (s[0], i, 0)), got_spec(0), got_spec(1), got_spec(2)],
            out_specs=pl.BlockSpec((None, tile, C), lambda i, s: (s[1], i, 0))),
        out_shape=jax.ShapeDtypeStruct((2, H, C), F32),
        compiler_params=_cparams(("arbitrary",)),
    )(_index(chip, core), g, recv, got, got, got)


def _share_halves(reds):
    n_w = len(reds)

    def body(*refs):
        out_refs = refs[n_w:2 * n_w]
        send_sems, recv_sems = refs[2 * n_w:]
        x, y, c = _place()
        sent = []
        for w in range(n_w):
            blk = out_refs[w].at[c]
            cp = _remote(blk, blk, send_sems, recv_sems, w, (x, y, 1 - c))
            cp.start()
            sent.append(cp)
        for cp in sent:
            cp.wait()

    return pl.pallas_call(
        body, name="grad_share_halves", in_specs=[ANY] * n_w, out_specs=[ANY] * n_w,
        out_shape=[jax.ShapeDtypeStruct(r.shape, r.dtype) for r in reds],
        input_output_aliases={w: w for w in range(n_w)},
        scratch_shapes=[pltpu.SemaphoreType.DMA((n_w,)), pltpu.SemaphoreType.DMA((n_w,))],
    )(*reds)


def _allsum_small(v, name):
    R, W = v.shape
    n_dev = 8
    vm = pl.BlockSpec(memory_space=pltpu.VMEM)

    def body(v_ref, out_ref, buf, send_sems, recv_sems):
        x, y, c = _place()
        me = 4 * x + 2 * y + c
        buf[me] = v_ref[...]
        sent = []
        for k in range(1, n_dev):
            peer = ((1 - x) if k & 4 else x, (1 - y) if k & 2 else y, (1 - c) if k & 1 else c)
            cp = _remote(v_ref, buf.at[me], send_sems, recv_sems, k - 1, peer)
            cp.start()
            sent.append(cp)
        for cp in sent:
            cp.wait_recv()
        for cp in sent:
            cp.wait_send()
        acc = buf[0]
        for q in range(1, n_dev):
            acc = acc + buf[q]
        out_ref[...] = acc

    return pl.pallas_call(
        body, name=name, in_specs=[vm], out_specs=vm, out_shape=jax.ShapeDtypeStruct((R, W), v.dtype),
        scratch_shapes=[pltpu.VMEM((n_dev, R, W), v.dtype), pltpu.SemaphoreType.DMA((n_dev - 1,)),
                        pltpu.SemaphoreType.DMA((n_dev - 1,))],
    )(v)


HBM = pl.BlockSpec(memory_space=pltpu.HBM)
SEM = pl.BlockSpec(memory_space=pltpu.SEMAPHORE)
_DATAFLOW = pltpu.SideEffectType.DATAFLOW_SIDE_EFFECTING


def _split_start(name, srcs, land_shapes, n_copies, copies, after=()):
    ns, nl = len(srcs), len(land_shapes)
    lands = [lax.empty(s.shape, s.dtype) for s in land_shapes]

    def body(*refs):
        outs = refs[ns + nl + len(after):]
        for cp in copies(refs[:ns], refs[ns:ns + nl], outs[0], outs[1]):
            cp.start()
        outs[-1][...] = jnp.zeros_like(outs[-1])

    sems = pltpu.SemaphoreType.DMA((n_copies,))
    res = pl.pallas_call(
        body, name=name, in_specs=[HBM] * (ns + nl) + [ANY] * len(after),
        out_specs=(SEM, SEM, *[HBM] * (ns + nl), pl.BlockSpec(memory_space=pltpu.VMEM)),
        out_shape=(sems, sems, *[pltpu.HBM(a.shape, a.dtype) for a in srcs],
                   *[pltpu.HBM(s.shape, s.dtype) for s in land_shapes], jax.ShapeDtypeStruct((8, 128), F32)),
        input_output_aliases={i: 2 + i for i in range(ns + nl)},
        compiler_params=pltpu.CompilerParams(has_side_effects=_DATAFLOW),
    )(*[pltpu.with_memory_space_constraint(a, pltpu.HBM) for a in [*srcs, *lands]], *after)
    return res[0], res[1], list(res[2:2 + ns]), list(res[2 + ns:2 + ns + nl]), res[-1]


def _split_wait(name, send_sems, recv_sems, srcs, lands, copies, after=()):
    ns, nl = len(srcs), len(lands)

    def body(*refs):
        for cp in copies(refs[:ns], refs[ns:ns + nl], refs[ns + nl], refs[ns + nl + 1]):
            cp.wait_send()
            cp.wait_recv()

    res = pl.pallas_call(
        body, name=name, in_specs=[HBM] * (ns + nl) + [SEM, SEM] + [ANY] * len(after), out_specs=[HBM] * (ns + nl),
        out_shape=[pltpu.HBM(a.shape, a.dtype) for a in [*srcs, *lands]],
        input_output_aliases={i: i for i in range(ns + nl)},
        compiler_params=pltpu.CompilerParams(has_side_effects=_DATAFLOW),
    )(*srcs, *lands, send_sems, recv_sems, *after)
    return list(res[:ns]), list(res[ns:])


def _gather_copies(rows):
    def copies(src_refs, land_refs, send_sems, recv_sems):
        x, y, c = _place()
        j = 2 * x + y
        out = []
        for w in range(len(src_refs)):
            r = _half(c, rows[w] // 2)
            for k in range(3):
                px, py = _other_chip(x, y, k)
                out.append(_remote(src_refs[w].at[r], land_refs[w].at[j, r], send_sems, recv_sems, 3 * w + k, (px, py, c)))
        return out
    return copies


def _scatter_copies(src_refs, land_refs, send_sems, recv_sems):
    x, y, c = _place()
    j = 2 * x + y
    out = []
    for w in range(len(src_refs)):
        for k in range(3):
            px, py = _other_chip(x, y, k)
            pj = 2 * px + py
            out.append(_remote(src_refs[w].at[pj], land_refs[w].at[(j - pj + 4) % 4 - 1], send_sems, recv_sems, 3 * w + k,
                               (px, py, c)))
    return out


def _halves(grads):
    names = list(grads)
    return names, [grads[k].reshape(N_CHIPS, 2, -1, grads[k].shape[-1]) for k in names]


def _reduce_begin(grads, core, tag):
    names, gs = _halves(grads)
    recvs = _swap_halves(gs, f"grad_swap_halves_{tag}")
    sums = [_pair_sum(g, r, core, f"pair_sum_{k}") for k, g, r in zip(names, gs, recvs)]
    return names, gs, recvs, sums


def _swap_copies(src_refs, land_refs, send_sems, recv_sems):
    x, y, c = _place()
    return [_remote(src_refs[w].at[jj, 1 - c], land_refs[w].at[jj], send_sems, recv_sems, N_CHIPS * w + jj, (x, y, 1 - c))
            for w in range(len(src_refs)) for jj in range(N_CHIPS)]


def _swap_begin(grads, tag):
    names, gs = _halves(grads)
    started = _split_start(f"swap_{tag}_start", gs, [jax.ShapeDtypeStruct((N_CHIPS, *g.shape[2:]), g.dtype) for g in gs],
                           N_CHIPS * len(gs), _swap_copies)
    return (names, started[:4]), started[4]


def _swap_end(begun, core, tag, after):
    names, started = begun
    gs, recvs = _split_wait(f"swap_{tag}_wait", *started, _swap_copies, after=after)
    sums = [_pair_sum(g, r, core, f"pair_sum_{k}") for k, g, r in zip(names, gs, recvs)]
    return names, gs, recvs, sums


def _reduce_end(begun, gots, chip, core):
    names, gs, recvs, _ = begun
    return {k: _chip_sum(g, r, t, chip, core, f"chip_sum_{k}") for k, g, r, t in zip(names, gs, recvs, gots)}


def _got_shapes(sums):
    return [jax.ShapeDtypeStruct((3, *a.shape[1:]), a.dtype) for a in sums]


def _adamw(w, g, m, v, name, layers=1, layer=0, into=None):
    shape = w.shape
    cols = shape[-1]
    w3, m3, v3 = (t.reshape(layers, -1, cols) for t in (w, m, v))
    rows = w3.shape[1]
    tile = _pick(rows, ADAM_ROWS if cols <= 1024 else ADAM_ROWS // 2) if rows % 8 == 0 else rows
    n_in = 4 + (0 if into is None else 4)
    stack_g = layers > 1

    def body(*refs):
        wv, gv, mv, vv = (r[...] for r in refs[:4])
        d_ref, m_ref, v_ref = refs[len(refs) - 3:]
        m2 = ADAM_B1 * mv + (1.0 - ADAM_B1) * gv
        v2 = ADAM_B2 * vv + (1.0 - ADAM_B2) * jnp.square(gv)
        m_hat = m2 / (1.0 - ADAM_B1 ** ADAM_STEP)
        v_hat = v2 / (1.0 - ADAM_B2 ** ADAM_STEP)
        if stack_g:
            refs[n_in][...] = gv
        d_ref[...] = -ADAM_LR * (m_hat / (jnp.sqrt(v_hat) + ADAM_EPS) + ADAM_WD * wv)
        m_ref[...] = m2
        v_ref[...] = v2

    n_out = 4 if stack_g else 3
    lay = pl.BlockSpec((None, tile, cols), lambda i: (layer, i, 0))
    out = jax.ShapeDtypeStruct((layers, rows, cols), F32)
    res = pl.pallas_call(
        body, name=name, grid=(rows // tile,),
        in_specs=[lay, pl.BlockSpec((tile, cols), lambda i: (i, 0)), lay, lay] + [ANY] * (n_in - 4),
        out_specs=[lay] * n_out, out_shape=[out] * n_out,
        input_output_aliases={} if into is None else {4 + k: k for k in range(4)},
        compiler_params=_cparams(("arbitrary",)),
    )(w3, g.reshape(rows, cols), m3, v3, *([] if into is None else [t.reshape(layers, rows, cols) for t in into]))
    res = tuple(t.reshape(shape) for t in res)
    return res if stack_g else (g.reshape(shape), *res)


ROW_F32, ROW_BF16 = (D_MODEL, F32), (D_MODEL, BF16)


def _res_norm(acc, h, gain):
    hh = h + acc
    return hh, _rms(hh, gain)


def _dx_norm_bwd(d, w, h, dres, gain, name, **kw):
    def epilogue(acc, hv, dr, g):
        dx, dg = _rms_bwd(hv, acc, g)
        return dr + dx, dr + dx, _colsum(dg)
    return _mm_rows(d, w, tb=True, extras=[h, dres], fulls=[gain], outs=[ROW_F32, ROW_BF16], accs=[((1, D_MODEL), F32)],
                    epilogue=epilogue, name=name, **kw)


def _tail_fwd(h1, hn2, p16, W, i, tag, next_gain=None, target=None):
    a = _mm(hn2, W["mlp_w1"][i], bblk=True, outs=[BF16], name=f"{tag}_mlp_w1", tm=2048, tn=1024,
            epilogue=lambda acc: (jnp.square(jnp.maximum(acc, 0.0)),))
    h2, hn3 = _mm_rows(a, W["mlp_w2"][i], extras=[h1], fulls=[W["ple_norm"][i:i + 1]], outs=[ROW_F32, ROW_BF16],
                       epilogue=_res_norm, name=f"{tag}_mlp_w2")
    def embed(acc, pv, h, wp):
        gate = _sigmoid(acc)
        ppv = jnp.concatenate([_dot(pv, wp[s]) for s in range(N_CHIPS)], axis=-1)
        return gate, ppv, h + gate * ppv

    if target is None:
        def gated(acc, pv, h, wp, gain):
            gate, ppv, hh = embed(acc, pv, h, wp)
            return hh, ppv, gate, _rms(hh, gain)
        h3, pp, gate, hn = _mm_rows(hn3, W["ple_gate_w"][i], extras=[p16[i], h2], fulls=[W["ple_proj_w"][i], next_gain],
                                    outs=[ROW_F32, ROW_BF16, ROW_BF16, ROW_BF16], epilogue=gated, name=f"{tag}_ple")
        return h3, hn, (h1, hn2, a, h2, hn3, gate, pp)

    def gated_loss(acc, pv, h, t, wp):
        gate, ppv, hh = embed(acc, pv, h, wp)
        e = hh - t
        return ppv, gate, e * (1.0 / D_MODEL), jnp.full((1, 128), 0.5 / D_MODEL, F32) * jnp.sum(e * e)
    pp, gate, dy, loss = _mm_rows(hn3, W["ple_gate_w"][i], extras=[p16[i], h2, target], fulls=[W["ple_proj_w"][i]],
                                  outs=[ROW_BF16, ROW_BF16, ROW_F32], accs=[((1, 128), F32)], epilogue=gated_loss,
                                  name=f"{tag}_ple")
    return dy, loss, (h1, hn2, a, h2, hn3, gate, pp)


def _tail_bwd(dh3, saved, p16, W, i, tag, after=(), hook=None):
    h1, hn2, a, h2, hn3, gate, pp = saved

    def embed_bwd(d, g, ppv, hv, wg, gain):
        g, ppv = g.astype(F32), ppv.astype(F32)
        dppv, dglv = (d * g).astype(BF16), (d * ppv * g * (1.0 - g)).astype(BF16)
        dx, dg = _rms_bwd(hv, _dot_nt(dglv, wg), gain)
        return dppv, dglv, d + dx, d + dx, _colsum(dg)

    def dw(kind, name):
        return (kind, 1, 0, None)

    dpp, dgl, dh2, dh2_16, d_ple_norm = _rows(
        embed_bwd, [dh3, gate, pp, h2], [W["ple_gate_w"][i], W["ple_norm"][i:i + 1]],
        [ROW_BF16, ROW_BF16, ROW_F32, ROW_BF16], [((1, D_MODEL), F32)], name=f"{tag}_ple_bwd", after=after)
    later = () if hook is None else hook(dh2_16)
    d_proj = _mm(p16[i], dpp, ta=True, outs=[BF16], dw=dw("cols", "ple_proj_w"), name=f"{tag}_d_ple_proj", after=later)
    d_gate = _mm(hn3, dgl, ta=True, outs=[BF16], dw=dw("rows", "ple_gate_w"), name=f"{tag}_d_ple_gate")
    d_w2 = _mm(a, dh2_16, ta=True, outs=[BF16], dw=dw("rows", "mlp_w2"), name=f"{tag}_d_mlp_w2", tn=1024)
    dz = _mm(dh2_16, W["mlp_w2"][i], tb=True, extras=[a], outs=[BF16], name=f"{tag}_mlp_w2_dx", tm=2048, tn=1024,
             epilogue=lambda acc, av: (acc * (2.0 * jnp.sqrt(av.astype(F32))),))
    d_w1 = _mm(hn2, dz, ta=True, outs=[BF16], dw=dw("cols", "mlp_w1"), name=f"{tag}_d_mlp_w1", tn=1024)
    dh1, dh1_16, d_mlp_norm = _dx_norm_bwd(dz, W["mlp_w1"][i], h1, dh2, W["mlp_norm"][i:i + 1], f"{tag}_mlp_w1_dx",
                                           bblk=True)
    big = {f"mlp_w1_{i}": d_w1, f"mlp_w2_{i}": d_w2, f"ple_gate_w_{i}": d_gate, f"ple_proj_w_{i}": d_proj}
    return dh1, dh1_16, big, dict(mlp_norm=d_mlp_norm, ple_norm=d_ple_norm)


def _ret_layer_fwd(h0, W, tabs, after=()):
    hn = _rows(lambda x, g: (_rms(x, g),), [h0], [W["mix_norm"][0:1]], [(D_MODEL, BF16)], name="ret_mix_norm",
               after=after)[0]
    proj = _mm(hn, W["ret_w_in"], bblk=True, outs=[BF16], name="ret_w_in", tm=2048, tn=768)
    out, states = _ret_fwd(proj, tabs, "ret_scan")
    y = _ret_gate(out, proj, W["ret_gn"], "ret_gate")
    h1, hn2 = _mm_rows(y, W["ret_w_out"], extras=[h0], fulls=[W["mlp_norm"][0:1]], outs=[ROW_F32, ROW_BF16],
                       epilogue=_res_norm, name="ret_w_out")
    return h1, hn2, (h0, hn, proj, out, states, y)


def _d_ret_w_out(dh1_16, saved):
    return _mm(saved[5], dh1_16, ta=True, outs=[BF16], dw=("rows", 1, 0, None), name="d_ret_w_out")


def _ret_layer_bwd(dh1, dh1_16, saved, W, tabs, after=(), hook=None, on_grads=None, d_w_out=None):
    h0, hn, proj, out, states, y = saved
    d_w_out = _d_ret_w_out(dh1_16, saved) if d_w_out is None else d_w_out
    dy = _mm(dh1_16, W["ret_w_out"], tb=True, name="ret_w_out_dx", tn=1024, after=after)
    dout, dproj, d_gn = _ret_gate_bwd(out, proj, W["ret_gn"], dy, "ret_gate_bwd",
                                      after=() if hook is None else hook(dy))
    dproj = _ret_bwd(proj, states, dout, dproj, tabs, "ret_scan_bwd")
    d_w_in = _mm(hn, dproj, ta=True, outs=[BF16], dw=("cols", 1, 0, None), name="d_ret_w_in", tn=768)
    big = dict(ret_w_in=d_w_in, ret_w_out=d_w_out)
    later = () if on_grads is None else on_grads(big)
    dh0, _, d_mix = _dx_norm_bwd(dproj, W["ret_w_in"], h0, dh1, W["mix_norm"][0:1], "ret_w_in_dx", bblk=True, tm=512,
                                 after=later)
    return dh0, big, dict(mix_norm=d_mix, ret_gn=d_gn)


def _mla_layer_fwd(h0, hn, W, tabs):
    proj, cqn, ckvn, q, kv, qf, kf, vf = _mla_front(hn, W, tabs, "mla_front")
    o, lse = _flash_fwd(qf, kf, vf, "mla_flash")
    h1, hn2 = _mm_rows(o, W["mla_w_out"], extras=[h0], fulls=[W["mlp_norm"][1:2]], outs=[ROW_F32, ROW_BF16],
                       epilogue=_res_norm, name="mla_w_out")
    return h1, hn2, (h0, hn, proj, cqn, ckvn, q, kv, qf, kf, vf, o, lse)


def _mla_layer_bwd(dh1, dh1_16, saved, W, tabs):
    h0, hn, proj, cqn, ckvn, q, kv, qf, kf, vf, o, lse = saved
    d_w_out = _mm(o, dh1_16, ta=True, outs=[BF16], dw=("rows", 1, 0, None), name="d_mla_w_out")
    def with_delta(acc, ov):
        parts = []
        for h in range(MLA_HEADS):
            sl = slice(h * MLA_VD, (h + 1) * MLA_VD)
            d = jnp.sum(acc[:, sl] * ov[:, sl], axis=-1, keepdims=True)
            parts.append(jnp.broadcast_to(d, (d.shape[0], MLA_VD)))
        return jnp.concatenate(parts, axis=-1), acc

    delta, do16 = _mm_rows(dh1_16, W["mla_w_out"], tb=True, extras=[o], outs=[ROW_F32, ROW_BF16], epilogue=with_delta,
                           name="mla_w_out_dx")
    dqf, dkf, dvf = _flash_bwd(qf, kf, vf, do16, lse, delta, "mla_flash_bwd")
    dq, dkv, dproj, dh0, dh0_16, d_gq, d_gk, d_gqa, d_gkva, d_mix = _mla_back(q, kv, proj, h0, dh1, dqf, dkf, dvf, W, tabs,
                                                                              "mla_back")
    d_w_uq = _mm(cqn, dq, ta=True, outs=[BF16], dw=("cols", 1, 0, None), name="d_mla_w_uq")
    d_w_ukv = _mm(ckvn, dkv, ta=True, outs=[BF16], dw=("cols", 1, 0, None), name="d_mla_w_ukv")
    d_w_in = _mm(hn, dproj, ta=True, outs=[BF16], dw=("rows", 1, 0, None), name="d_mla_w_in")
    return (dh0, dh0_16, dict(mla_w_in=d_w_in, mla_w_uq=d_w_uq, mla_w_ukv=d_w_ukv, mla_w_out=d_w_out),
            dict(mix_norm=d_mix, mla_q_a_norm=d_gqa, mla_kv_a_norm=d_gkva, mla_q_norm=d_gq, mla_k_norm=d_gk))


def _small_grads(n_ret, n_t0, n_mla, n_t1):
    return dict(
        mix_norm=jnp.concatenate([n_ret["mix_norm"], n_mla["mix_norm"]], axis=0),
        mlp_norm=jnp.concatenate([n_t0["mlp_norm"], n_t1["mlp_norm"]], axis=0),
        ple_norm=jnp.concatenate([n_t0["ple_norm"], n_t1["ple_norm"]], axis=0),
        ret_gn=n_ret["ret_gn"], mla_q_a_norm=n_mla["mla_q_a_norm"], mla_kv_a_norm=n_mla["mla_kv_a_norm"],
        mla_q_norm=n_mla["mla_q_norm"], mla_k_norm=n_mla["mla_k_norm"])


_ORDER = ("mix_norm", "ret_w_in", "ret_gn", "ret_w_out", "mla_w_in", "mla_q_a_norm", "mla_kv_a_norm", "mla_w_uq",
          "mla_w_ukv", "mla_q_norm", "mla_k_norm", "mla_w_out", "mlp_norm", "mlp_w1", "mlp_w2", "ple_norm",
          "ple_gate_w", "ple_proj_w")
_TWO_LAYER = ("mlp_w1", "mlp_w2", "ple_gate_w", "ple_proj_w")
HEADS_PER_CHIP = MLA_HEADS // N_CHIPS
GAIN_ROWS = 32


def _travel_parts(w):
    uq = jnp.pad(w["mla_w_uq"][0].reshape(MLA_Q_RANK, HEADS_PER_CHIP, MLA_QKD), ((0, 0), (0, 0), (0, MLA_HP - MLA_QKD)))
    parts = {"ret_w_in": w["ret_w_in"][0], "ret_w_out": w["ret_w_out"][0]}
    for k in _TWO_LAYER:
        parts[k + "_0"] = w[k][0]
    parts["mla_w_in"] = jnp.pad(w["mla_w_in"][0], ((0, 0), (0, MLA_IN_PAD - MLA_IN)))
    parts["mla_w_uq"] = uq.reshape(MLA_Q_RANK, HEADS_PER_CHIP * MLA_HP)
    parts["mla_w_ukv"] = w["mla_w_ukv"][0]
    parts["mla_w_out"] = w["mla_w_out"][0]
    for k in _TWO_LAYER:
        parts[k + "_1"] = w[k][1]
    gains = jnp.concatenate([_pad_row(w["ret_gn"]), _pad_row(w["mla_q_a_norm"]), _pad_row(w["mla_kv_a_norm"]),
                             jnp.zeros((GAIN_ROWS - 3, PACK_W), F32)], axis=0)
    return {"gains": gains, **{k: v.astype(BF16) for k, v in parts.items()}}


def _full_weights(full):
    rows = lambda a: a.reshape(-1, a.shape[-1])
    W = {k: full[k] for k in ("ret_w_in", "mla_w_uq", "mla_w_ukv") if k in full}
    for k in ("ret_w_out", "mla_w_in", "mla_w_out"):
        if k in full:
            W[k] = rows(full[k])
    for k, by_rows in (("mlp_w1", False), ("ple_proj_w", False), ("mlp_w2", True), ("ple_gate_w", True)):
        layers = [full.get(f"{k}_{i}") for i in range(2)]
        W[k] = [rows(t) if (by_rows and t is not None) else t for t in layers]
    return W


def _shard_grad(name, red, shape):
    if name == "mla_w_in":
        red = red.reshape(-1, MLA_IN_PAD)[:, :MLA_IN]
    elif name == "mla_w_uq":
        red = red.reshape(MLA_Q_RANK, HEADS_PER_CHIP, MLA_HP)[:, :, :MLA_QKD]
    return red.reshape(shape)


def _pad_row(v):
    v = v.reshape(1, -1)
    return jnp.pad(v, ((0, 0), (0, PACK_W - v.shape[1])))


def kernel(x, p, mix_norm, ret_w_in, ret_gn, ret_w_out, mla_w_in, mla_q_a_norm, mla_kv_a_norm, mla_w_uq, mla_w_ukv, mla_q_norm, mla_k_norm, mla_w_out, mlp_norm, mlp_w1, mlp_w2, ple_norm, ple_gate_w, ple_proj_w, loss_target, m_mix_norm, m_ret_w_in, m_ret_gn, m_ret_w_out, m_mla_w_in, m_mla_q_a_norm, m_mla_kv_a_norm, m_mla_w_uq, m_mla_w_ukv, m_mla_q_norm, m_mla_k_norm, m_mla_w_out, m_mlp_norm, m_mlp_w1, m_mlp_w2, m_ple_norm, m_ple_gate_w, m_ple_proj_w, v_mix_norm, v_ret_w_in, v_ret_gn, v_ret_w_out, v_mla_w_in, v_mla_q_a_norm, v_mla_kv_a_norm, v_mla_w_uq, v_mla_w_ukv, v_mla_q_norm, v_mla_k_norm, v_mla_w_out, v_mlp_norm, v_mlp_w1, v_mlp_w2, v_ple_norm, v_ple_gate_w, v_ple_proj_w):
    w = dict(mix_norm=mix_norm, ret_w_in=ret_w_in, ret_gn=ret_gn, ret_w_out=ret_w_out, mla_w_in=mla_w_in,
             mla_q_a_norm=mla_q_a_norm, mla_kv_a_norm=mla_kv_a_norm, mla_w_uq=mla_w_uq, mla_w_ukv=mla_w_ukv,
             mla_q_norm=mla_q_norm, mla_k_norm=mla_k_norm, mla_w_out=mla_w_out, mlp_norm=mlp_norm, mlp_w1=mlp_w1,
             mlp_w2=mlp_w2, ple_norm=ple_norm, ple_gate_w=ple_gate_w, ple_proj_w=ple_proj_w)
    m = dict(mix_norm=m_mix_norm, ret_w_in=m_ret_w_in, ret_gn=m_ret_gn, ret_w_out=m_ret_w_out, mla_w_in=m_mla_w_in,
             mla_q_a_norm=m_mla_q_a_norm, mla_kv_a_norm=m_mla_kv_a_norm, mla_w_uq=m_mla_w_uq, mla_w_ukv=m_mla_w_ukv,
             mla_q_norm=m_mla_q_norm, mla_k_norm=m_mla_k_norm, mla_w_out=m_mla_w_out, mlp_norm=m_mlp_norm,
             mlp_w1=m_mlp_w1, mlp_w2=m_mlp_w2, ple_norm=m_ple_norm, ple_gate_w=m_ple_gate_w, ple_proj_w=m_ple_proj_w)
    v = dict(mix_norm=v_mix_norm, ret_w_in=v_ret_w_in, ret_gn=v_ret_gn, ret_w_out=v_ret_w_out, mla_w_in=v_mla_w_in,
             mla_q_a_norm=v_mla_q_a_norm, mla_kv_a_norm=v_mla_kv_a_norm, mla_w_uq=v_mla_w_uq, mla_w_ukv=v_mla_w_ukv,
             mla_q_norm=v_mla_q_norm, mla_k_norm=v_mla_k_norm, mla_w_out=v_mla_w_out, mlp_norm=v_mlp_norm,
             mlp_w1=v_mlp_w1, mlp_w2=v_mlp_w2, ple_norm=v_ple_norm, ple_gate_w=v_ple_gate_w, ple_proj_w=v_ple_proj_w)
    xi, yi, ci = _place()
    chip = 2 * xi + yi
    n = N_CHIPS

    parts = _travel_parts(w)
    first = ("gains", "ret_w_in", "ret_w_out")
    mid = [k + "_0" for k in _TWO_LAYER]
    last = [k for k in parts if k not in first and k not in mid]
    full = dict(zip(first, _gather_weights([parts[k] for k in first], "gather_first")))

    def gather_behind(names, tag, after):
        copies = _gather_copies([parts[k].shape[0] for k in names])
        started = _split_start(f"gather_{tag}_start", [parts[k] for k in names],
                               [jax.ShapeDtypeStruct((n, *parts[k].shape), BF16) for k in names], 3 * len(names),
                               copies, after=after)

        def arrive(after):
            _, landed = _split_wait(f"gather_{tag}_wait", *started[:4], copies, after=after)
            full.update(zip(names, _gather_weights([parts[k] for k in names], f"gather_{tag}_finish", landed=landed)))
            W.update(_full_weights(full))
        return started[4], arrive

    mid_token, mid_arrive = gather_behind(mid, "mid", [full["ret_w_in"]])
    g_token, last_arrive = gather_behind(last, "last", [mid_token])
    gains = full["gains"]
    W = dict(mix_norm=mix_norm, mlp_norm=mlp_norm, ple_norm=ple_norm,
             mla_q_norm=jnp.pad(mla_q_norm, ((0, 0), (0, MLA_HP - MLA_QKD))),
             mla_k_norm=jnp.pad(mla_k_norm, ((0, 0), (0, MLA_HP - MLA_QKD))),
             ret_w_in=full["ret_w_in"], ret_w_out=full["ret_w_out"].reshape(-1, D_MODEL),
             ret_gn=gains[:, 0, :RET_HEADS * 128].reshape(n, RET_HEADS, 128).transpose(1, 0, 2).reshape(RET_HEADS, RET_DV),
             mla_q_a_norm=gains[:, 1, :MLA_Q_RANK // n].reshape(1, MLA_Q_RANK),
             mla_kv_a_norm=gains[:, 2, :MLA_KV_RANK // n].reshape(1, MLA_KV_RANK))
    x0, p16, target = x[0], p[:, 0].astype(BF16), loss_target[0]
    T = x0.shape[0]
    ret_tabs, mla_tabs = _ret_tables(T), _mla_tables(T)

    h1, hn, s_ret = _ret_layer_fwd(x0, W, ret_tabs, after=[g_token])
    mid_arrive([h1])
    h3, hn, s_tail0 = _tail_fwd(h1, hn, p16, W, 0, "l0", next_gain=W["mix_norm"][1:2])
    last_arrive([h3])
    h4, hn, s_mla = _mla_layer_fwd(h3, hn, W, mla_tabs)
    dy, loss, s_tail1 = _tail_fwd(h4, hn, p16, W, 1, "l1", target=target)

    dh4, dh4_16, g_t1, n_t1 = _tail_bwd(dy, s_tail1, p16, W, 1, "l1")
    dh3, _, g_mla, n_mla = _mla_layer_bwd(dh4, dh4_16, s_mla, W, mla_tabs)
    stages = {}

    def scatter_start(tag, begun):
        started = _split_start(f"scatter_{tag}_start", begun[3], _got_shapes(begun[3]), 3 * len(begun[3]), _scatter_copies)
        stages[tag] = (begun, started[:4])
        return [started[4]]

    def scatter_end(tag, after):
        begun, started = stages[tag]
        return _reduce_end(begun, _split_wait(f"scatter_{tag}_wait", *started, _scatter_copies, after=after)[1], chip, ci)

    swap_a, token = _swap_begin({**g_mla, **g_t1}, "a")
    dh1, dh1_16, g_t0, n_t0 = _tail_bwd(dh3, s_tail0, p16, W, 0, "l0", after=[token],
                                        hook=lambda t: scatter_start("a", _swap_end(swap_a, ci, "a", [t])))
    d_ret_w_out = _d_ret_w_out(dh1_16, s_ret)
    swap_b, token = _swap_begin({**g_t0, "ret_w_out": d_ret_w_out}, "b")
    dx, _, n_ret = _ret_layer_bwd(
        dh1, dh1_16, s_ret, W, ret_tabs, after=[token], d_w_out=d_ret_w_out,
        hook=lambda t: scatter_start("b", _swap_end(swap_b, ci, "b", [t])),
        on_grads=lambda g: scatter_start("c", _reduce_begin({"ret_w_in": g["ret_w_in"]}, ci, "c")))
    red = {**scatter_end("a", [dx]), **scatter_end("b", [dx]), **scatter_end("c", [dx])}
    red = dict(zip(red, _share_halves(list(red.values()))))
    gs = _small_grads(n_ret, n_t0, n_mla, n_t1)
    small_g = jnp.concatenate([
        gs["mix_norm"], gs["mlp_norm"], gs["ple_norm"], gs["ret_gn"].reshape(2, PACK_W), _pad_row(gs["mla_q_a_norm"]),
        _pad_row(gs["mla_kv_a_norm"]), _pad_row(gs["mla_q_norm"][:, :MLA_QKD]), _pad_row(gs["mla_k_norm"][:, :MLA_QKD]),
        _pad_row(loss[:, :1]), jnp.zeros((3, PACK_W), F32)], axis=0)
    tot = _allsum_small(small_g, "sum_small_grads")
    gn_all = tot[6:8].reshape(RET_HEADS, n, -1)
    g_small = dict(
        mix_norm=tot[0:2], mlp_norm=tot[2:4], ple_norm=tot[4:6],
        ret_gn=lax.dynamic_index_in_dim(gn_all, chip, axis=1, keepdims=False),
        mla_q_a_norm=lax.dynamic_index_in_dim(tot[8, :MLA_Q_RANK].reshape(n, -1), chip, axis=0, keepdims=True),
        mla_kv_a_norm=lax.dynamic_index_in_dim(tot[9, :MLA_KV_RANK].reshape(n, -1), chip, axis=0, keepdims=True),
        mla_q_norm=tot[10:11, :MLA_QKD], mla_k_norm=tot[11:12, :MLA_QKD])
    loss_out = tot[12, 0]

    outs = []
    for k in _ORDER:
        if k in _TWO_LAYER:
            res = None
            for i in (1, 0):
                res = _adamw(w[k], red[f"{k}_{i}"], m[k], v[k], f"adamw_{k}_{i}", layers=2, layer=i, into=res)
        elif k in red:
            res = _adamw(w[k], _shard_grad(k, red[k], w[k].shape), m[k], v[k], f"adamw_{k}")
        else:
            res = _adamw(w[k], g_small[k], m[k], v[k], f"adamw_{k}")
        outs.append(res)
    return (loss_out, dx[None], *[o[0] for o in outs], *[o[1] for o in outs], *[o[2] for o in outs],
            *[o[3] for o in outs])
```

```python
import jax
import jax.numpy as jnp
import numpy as np
from jax import lax
from jax.experimental import pallas as pl
from jax.experimental.pallas import tpu as pltpu

F32 = jnp.float32
BF16 = jnp.bfloat16

EPS = 1e-6
D_MODEL = 1024
CHUNK = 64
ROPE_THETA = 10000.0
RET_HEADS = 4
RET_DK = 256
RET_DV = 512
RET_GROUP = 1
RET_BLOCK = 256
RET_ROWS = 1024
MLA_HEADS = 8
MLA_ROPE = 64
MLA_QKD = 192
MLA_VD = 128
MLA_HP = 256
MLA_Q_RANK = 384
MLA_KV_RANK = 256
MLA_IN = 704
MLA_IN_PAD = 768
N_CHIPS = 4

ADAM_LR = 0.001
ADAM_B1 = 0.9
ADAM_B2 = 0.999
ADAM_EPS = 1e-08
ADAM_WD = 0.01
ADAM_STEP = 10

VMEM_LIMIT = 56 * 1024 * 1024
PACK_W = 1024
NEG = -1e30
LOG2E = 1.4426950408889634
FLASH_T = 512
FLASH_HEADS = 4
FLASH_BWD_HEADS = 2
MM_SUB_ROWS = 256
SUM_ROWS = 512
ADAM_ROWS = 512


def _cparams(sem=None):
    return pltpu.CompilerParams(dimension_semantics=sem, vmem_limit_bytes=VMEM_LIMIT)


def _pick(dim, pref):
    if dim <= pref:
        return dim
    t = pref
    while dim % t:
        t //= 2
    return t


def _mm(a, b, *, name, ta=False, tb=False, bblk=False, outs=None, extras=(), epilogue=None, dw=None,
        tm=1024, tn=512, after=()):
    if ta:
        K, M = a.shape
    else:
        M, K = a.shape
    if bblk and tb:
        nb, N, Kq = b.shape
        assert nb * Kq == K
    elif bblk:
        nb, Kb, Nq = b.shape
        N = nb * Nq
        assert Kb == K
    else:
        N = b.shape[0] if tb else b.shape[1]
    tn = _pick(Nq if (bblk and not tb) else N, tn)
    if dw is not None and dw[0] == "cols":
        tn = _pick(N // N_CHIPS, tn)
    tm = _pick(M // N_CHIPS if (dw is not None and dw[0] == "rows") else M, tm)
    grid = (M // tm, N // tn)

    a_spec = pl.BlockSpec((K, tm), lambda i, j: (0, i)) if ta else pl.BlockSpec((tm, K), lambda i, j: (i, 0))
    if bblk and tb:
        b_spec = pl.BlockSpec((nb, tn, Kq), lambda i, j: (0, j, 0))
    elif bblk:
        npb = Nq // tn
        b_spec = pl.BlockSpec((None, K, tn), lambda i, j: (j // npb, 0, j % npb))
    elif tb:
        b_spec = pl.BlockSpec((tn, K), lambda i, j: (j, 0))
    else:
        b_spec = pl.BlockSpec((K, tn), lambda i, j: (0, j))
    in_specs = [a_spec, b_spec] + [pl.BlockSpec((tm, tn), lambda i, j: (i, j)) for _ in extras]
    args = [a, b, *extras]
    aliases = {}
    if outs is None:
        outs = [F32]
    if dw is None:
        o_specs = [pl.BlockSpec((tm, tn), lambda i, j: (i, j)) for _ in outs]
        o_shapes = [jax.ShapeDtypeStruct((M, N), dt) for dt in outs]
    else:
        kind, layers, layer, into = dw
        if kind == "cols":
            per = (N // N_CHIPS) // tn
            o_specs = [pl.BlockSpec((None, None, tm, tn), lambda i, j: (j // per, layer, i, j % per))]
            o_shapes = [jax.ShapeDtypeStruct((N_CHIPS, layers, M, N // N_CHIPS), outs[0])]
        else:
            per = (M // N_CHIPS) // tm
            o_specs = [pl.BlockSpec((None, None, tm, tn), lambda i, j: (i // per, layer, i % per, j))]
            o_shapes = [jax.ShapeDtypeStruct((N_CHIPS, layers, M // N_CHIPS, N), outs[0])]
        if into is not None:
            aliases = {len(args): 0}
            in_specs.append(pl.BlockSpec(memory_space=pl.ANY))
            args.append(into)
    for t in after:
        in_specs.append(pl.BlockSpec(memory_space=pl.ANY))
        args.append(t)
    n_e, n_o = len(extras), len(outs)

    sub = _pick(tm, MM_SUB_ROWS)

    def body(a_ref, b_ref, *rest):
        e_refs, o_refs = rest[:n_e], rest[len(rest) - n_o:]
        for r0 in range(0, tm, sub):
            rows = slice(r0, r0 + sub)
            av = (a_ref[:, rows] if ta else a_ref[rows, :]).astype(BF16)
            if bblk and tb:
                acc = _dot_nt(av[:, :Kq], b_ref[0].astype(BF16))
                for s in range(1, nb):
                    acc = acc + _dot_nt(av[:, s * Kq:(s + 1) * Kq], b_ref[s].astype(BF16))
            elif ta:
                acc = _dot_tn(av, b_ref[...].astype(BF16))
            elif tb:
                acc = _dot_nt(av, b_ref[...].astype(BF16))
            else:
                acc = _dot(av, b_ref[...].astype(BF16))
            vals = (acc,) if epilogue is None else epilogue(acc, *[e[rows, :] for e in e_refs])
            for o, v in zip(o_refs, vals):
                o[rows, :] = v.astype(o.dtype)

    res = pl.pallas_call(
        body, name=name, grid=grid, in_specs=in_specs, out_specs=o_specs, out_shape=o_shapes,
        input_output_aliases=aliases, compiler_params=_cparams(("parallel", "arbitrary")),
    )(*args)
    return res[0] if n_o == 1 else res


def _mm_rows(a, b, *, name, epilogue, outs, tb=False, bblk=False, extras=(), fulls=(), accs=(), tm=512, after=()):
    M, K = a.shape
    tm = _pick(M, tm)
    sub = _pick(tm, MM_SUB_ROWS)
    nb = b.shape[0] if bblk else 1
    n_e, n_f, n_o, n_a = len(extras), len(fulls), len(outs), len(accs)
    n_in = 2 + n_e + n_f + len(after)

    def whole(t):
        return pl.BlockSpec(t.shape, lambda i, nd=t.ndim: (0,) * nd)

    in_specs = [pl.BlockSpec((tm, K), lambda i: (i, 0)), whole(b)]
    in_specs += [pl.BlockSpec((tm, e.shape[1]), lambda i: (i, 0)) for e in extras] + [whole(f) for f in fulls]
    in_specs += [pl.BlockSpec(memory_space=pl.ANY) for _ in after]
    out_specs = [pl.BlockSpec((tm, w), lambda i: (i, 0)) for w, _ in outs] + [pl.BlockSpec(s, lambda i: (0, 0)) for s, _ in accs]
    out_shape = [jax.ShapeDtypeStruct((M, w), dt) for w, dt in outs] + [jax.ShapeDtypeStruct(s, dt) for s, dt in accs]

    def body(a_ref, b_ref, *rest):
        e_refs, f_refs = rest[:n_e], rest[n_e:n_e + n_f]
        o_refs, acc_refs = rest[n_in - 2:n_in - 2 + n_o], rest[n_in - 2 + n_o:]
        fv = [f[...] for f in f_refs]
        totals = None
        for r0 in range(0, tm, sub):
            rows = slice(r0, r0 + sub)
            av = a_ref[rows, :].astype(BF16)
            if bblk and tb:
                kq = K // nb
                acc = _dot_nt(av[:, :kq], b_ref[0])
                for s in range(1, nb):
                    acc = acc + _dot_nt(av[:, s * kq:(s + 1) * kq], b_ref[s])
            elif bblk:
                acc = jnp.concatenate([_dot(av, b_ref[s]) for s in range(nb)], axis=-1)
            elif tb:
                acc = _dot_nt(av, b_ref[...])
            else:
                acc = _dot(av, b_ref[...])
            vals = epilogue(acc, *[e[rows, :] for e in e_refs], *fv)
            for o, v in zip(o_refs, vals[:n_o]):
                o[rows, :] = v.astype(o.dtype)
            part = vals[n_o:]
            totals = part if totals is None else [t + p for t, p in zip(totals, part)]
        first_step = pl.program_id(0) == 0
        for o, v in zip(acc_refs, totals):
            @pl.when(first_step)
            def _(o=o, v=v):
                o[...] = v.astype(o.dtype)

            @pl.when(jnp.logical_not(first_step))
            def _(o=o, v=v):
                o[...] += v.astype(o.dtype)

    return pl.pallas_call(
        body, name=name, grid=(M // tm,), in_specs=in_specs, out_specs=out_specs, out_shape=out_shape,
        compiler_params=_cparams(("arbitrary",)),
    )(a, b, *extras, *fulls, *after)


def _rows(fn, rows, fulls, outs, accs=(), *, name, tile=512, after=()):
    first = rows[0][0] if isinstance(rows[0], tuple) else rows[0]
    T = first.shape[0]
    tile = _pick(T, tile)
    in_specs, args = [], []
    for r in rows:
        if isinstance(r, tuple):
            arr, w, cb = r
            in_specs.append(pl.BlockSpec((tile, w), lambda i, cb=cb: (i, cb)))
        else:
            arr = r
            in_specs.append(pl.BlockSpec((tile, arr.shape[1]), lambda i: (i, 0)))
        args.append(arr)
    for f in fulls:
        in_specs.append(pl.BlockSpec(f.shape, lambda i, nd=f.ndim: (0,) * nd))
        args.append(f)
    outs = [o if len(o) == 4 else (*o, o[0], 0) for o in outs]
    out_specs = [pl.BlockSpec((tile, w), lambda i, cb=cb: (i, cb)) for w, _, _, cb in outs]
    out_specs += [pl.BlockSpec(s, lambda i: (0, 0)) for s, _ in accs]
    out_shape = [jax.ShapeDtypeStruct((T, tw), dt) for _, dt, tw, _ in outs]
    out_shape += [jax.ShapeDtypeStruct(s, dt) for s, dt in accs]
    n_in, n_out = len(args), len(outs)
    for t in after:
        in_specs.append(pl.BlockSpec(memory_space=pl.ANY))
        args.append(t)

    def body(*refs):
        vals = fn(*[r[...] for r in refs[:n_in]])
        o_refs = refs[len(args):]
        for o, v in zip(o_refs[:n_out], vals[:n_out]):
            o[...] = v.astype(o.dtype)
        first_step = pl.program_id(0) == 0
        for o, v in zip(o_refs[n_out:], vals[n_out:]):
            @pl.when(first_step)
            def _(o=o, v=v):
                o[...] = v.astype(o.dtype)

            @pl.when(jnp.logical_not(first_step))
            def _(o=o, v=v):
                o[...] += v.astype(o.dtype)

    res = pl.pallas_call(
        body, name=name, grid=(T // tile,), in_specs=in_specs, out_specs=out_specs, out_shape=out_shape,
        compiler_params=_cparams(("arbitrary",)),
    )(*args)
    return res


def _rowsum(v, mxu):
    if not mxu:
        return jnp.sum(v, axis=-1, keepdims=True)
    ones = jnp.ones((v.shape[1], v.shape[1]), BF16)
    hi = v.astype(BF16)
    lo = (v - hi.astype(F32)).astype(BF16)
    return _dot(hi, ones) + _dot(lo, ones)


def _rms(x, g, mxu=False):
    r = lax.rsqrt(_rowsum(x * x, mxu) / x.shape[-1] + EPS)
    return (x * r) * g


def _rms_bwd(x, dy, g, n=None, mxu=False):
    n = x.shape[-1] if n is None else n
    r = lax.rsqrt(_rowsum(x * x, mxu) / n + EPS)
    xh = x * r
    dxh = dy * g
    dx = r * (dxh - xh * (_rowsum(dxh * xh, mxu) / n))
    return dx, dy * xh


def _colsum(v):
    return jnp.sum(v, axis=0, keepdims=True)


def _sigmoid(x):
    return 1.0 / (1.0 + jnp.exp(-x))


def _widen(v, width):
    reps = width // v.shape[1]
    return v if reps == 1 else jnp.concatenate([v] * reps, axis=-1)


def _rope_angles(T, dim):
    inv = (1.0 / (np.float32(ROPE_THETA) ** (np.arange(0, dim, 2, dtype=np.float32) / np.float32(dim)))).astype(np.float32)
    return np.arange(T, dtype=np.float32)[:, None] * inv[None, :]


def _ret_tables(T):
    ang = _rope_angles(T, RET_DK)
    log_gamma = np.log(np.float32(1.0) - np.float32(2.0) ** (-5.0 - np.arange(RET_HEADS, dtype=np.float32)))
    idx = np.arange(RET_BLOCK, dtype=np.float32)
    chunk = np.arange(RET_BLOCK) // CHUNK
    dist = idx[:, None] - idx[None, :]
    seen = np.where(chunk[:, None] == chunk[None, :], np.abs(dist), np.where(chunk[:, None] > chunk[None, :], dist, np.inf))
    intra = np.exp(log_gamma[:, None, None] * seen[None].astype(np.float32))
    qd = np.exp(log_gamma[:, None] * (idx + 1.0))[:, :, None]
    kd = np.exp(log_gamma[:, None] * (RET_BLOCK - 1.0 - idx))[:, :, None]
    cd = np.exp(log_gamma * RET_BLOCK)[:, None, None]
    return tuple(jnp.asarray(t, F32) for t in (np.cos(ang), np.sin(ang), intra, qd, kd, cd))


def _rope_half(x, c, s):
    x1, x2 = x[:, :RET_DK // 2], x[:, RET_DK // 2:]
    return jnp.concatenate([x1 * c - x2 * s, x2 * c + x1 * s], axis=-1)


def _rope_half_bwd(d, c, s):
    d1, d2 = d[:, :RET_DK // 2], d[:, RET_DK // 2:]
    return jnp.concatenate([d1 * c + d2 * s, d2 * c - d1 * s], axis=-1)


def _dot(a, b):
    return lax.dot_general(a, b, (((1,), (0,)), ((), ())), preferred_element_type=F32)


def _dot_nt(a, b):
    return lax.dot_general(a, b, (((1,), (1,)), ((), ())), preferred_element_type=F32)


def _dot_tn(a, b):
    return lax.dot_general(a, b, (((0,), (0,)), ((), ())), preferred_element_type=F32)


def _ret_specs(T, tb, rev):
    nj = T // tb
    jj = (lambda j: nj - 1 - j) if rev else (lambda j: j)
    g = RET_GROUP
    kq = RET_HEADS // g
    vq = 2 * RET_HEADS * RET_DK // (g * RET_DV)
    return dict(
        q=pl.BlockSpec((tb, g * RET_DK), lambda h, j: (jj(j), h)),
        k=pl.BlockSpec((tb, g * RET_DK), lambda h, j: (jj(j), kq + h)),
        v=pl.BlockSpec((tb, g * RET_DV), lambda h, j: (jj(j), vq + h)),
        tab=pl.BlockSpec((tb, RET_DK // 2), lambda h, j: (jj(j), 0)),
        intra=pl.BlockSpec((g, RET_BLOCK, RET_BLOCK), lambda h, j: (h, 0, 0)),
        dec=pl.BlockSpec((g, RET_BLOCK, 1), lambda h, j: (h, 0, 0)),
        cd=pl.BlockSpec((g, 1, 1), lambda h, j: (h, 0, 0)),
        o=pl.BlockSpec((tb, g * RET_DV), lambda h, j: (jj(j), h)),
        s=pl.BlockSpec((g, tb // RET_BLOCK, RET_DK, RET_DV), lambda h, j: (h, jj(j), 0, 0)),
    )


def _ret_fwd(proj, tabs, name):
    T = proj.shape[0]
    cos, sin, intra, qd, kd, cd = tabs
    tb = _pick(T, RET_ROWS)
    cps = tb // RET_BLOCK
    sp = _ret_specs(T, tb, False)
    scale = RET_DK ** -0.5

    def body(q_ref, k_ref, v_ref, cos_ref, sin_ref, intra_ref, qd_ref, kd_ref, cd_ref, o_ref, s_ref, state):
        @pl.when(pl.program_id(1) == 0)
        def _():
            state[...] = jnp.zeros_like(state)

        for c in range(cps):
            rows = pl.ds(c * RET_BLOCK, RET_BLOCK)
            co, si = cos_ref[rows, :], sin_ref[rows, :]
            for h in range(RET_GROUP):
                hk, hv = slice(h * RET_DK, (h + 1) * RET_DK), slice(h * RET_DV, (h + 1) * RET_DV)
                q = _rope_half(q_ref[rows, hk].astype(F32), co, si)
                k = _rope_half(k_ref[rows, hk].astype(F32), co, si) * scale
                vb = v_ref[rows, hv].astype(BF16)
                st = state[h]
                sb = st.astype(BF16)
                s_ref[h, c] = sb
                sc = _dot_nt(q.astype(BF16), k.astype(BF16)) * intra_ref[h]
                inner = _dot(sc.astype(BF16), vb)
                cross = _dot((q * qd_ref[h]).astype(BF16), sb)
                o_ref[rows, hv] = inner + cross
                state[h] = st * cd_ref[h] + _dot_tn((k * kd_ref[h]).astype(BF16), vb)

    return pl.pallas_call(
        body, name=name, grid=(RET_HEADS // RET_GROUP, T // tb),
        in_specs=[sp["q"], sp["k"], sp["v"], sp["tab"], sp["tab"], sp["intra"], sp["dec"], sp["dec"], sp["cd"]],
        out_specs=[sp["o"], sp["s"]],
        out_shape=[jax.ShapeDtypeStruct((T, RET_HEADS * RET_DV), F32),
                   jax.ShapeDtypeStruct((RET_HEADS, T // RET_BLOCK, RET_DK, RET_DV), BF16)],
        scratch_shapes=[pltpu.VMEM((RET_GROUP, RET_DK, RET_DV), F32)],
        compiler_params=_cparams(("arbitrary", "arbitrary")),
    )(proj, proj, proj, cos, sin, intra, qd, kd, cd)


def _ret_bwd(proj, states, dout, dproj, tabs, name):
    assert RET_GROUP == 1
    T = proj.shape[0]
    cos, sin, intra, qd, kd, cd = tabs
    tb = _pick(T, RET_ROWS)
    cps = tb // RET_BLOCK
    nj = T // tb
    sp = _ret_specs(T, tb, True)
    scale = RET_DK ** -0.5
    k0, v0 = RET_HEADS * RET_DK, 2 * RET_HEADS * RET_DK

    def body(q_ref, k_ref, v_ref, cos_ref, sin_ref, intra_ref, qd_ref, kd_ref, cd_ref, s_ref, do_ref, _dproj_in,
             out_ref, dq_s, dk_s, dv_s, sems, dstate):
        head, j = pl.program_id(0), pl.program_id(1)
        step = head * nj + j
        slot = step % 2
        dq_ref, dk_ref, dv_ref = dq_s.at[slot], dk_s.at[slot], dv_s.at[slot]

        @pl.when(j == 0)
        def _():
            dstate[...] = jnp.zeros_like(dstate)

        for c in reversed(range(cps)):
            rows = pl.ds(c * RET_BLOCK, RET_BLOCK)
            co, si = cos_ref[rows, :], sin_ref[rows, :]
            for h in range(RET_GROUP):
                hk, hv = slice(h * RET_DK, (h + 1) * RET_DK), slice(h * RET_DV, (h + 1) * RET_DV)
                q = _rope_half(q_ref[rows, hk].astype(F32), co, si)
                k = _rope_half(k_ref[rows, hk].astype(F32), co, si) * scale
                qb, kb = q.astype(BF16), k.astype(BF16)
                vb = v_ref[rows, hv].astype(BF16)
                dob = do_ref[rows, hv].astype(BF16)
                sb = s_ref[h, c]
                ia = intra_ref[h]
                pb = (_dot_nt(qb, kb) * ia).astype(BF16)
                dsn = dstate[h]
                dsb = dsn.astype(BF16)
                kdk = (k * kd_ref[h]).astype(BF16)
                qdq = (q * qd_ref[h]).astype(BF16)
                dv = _dot_tn(pb, dob) + _dot(kdk, dsb)
                dpb = (_dot_nt(dob, vb) * ia).astype(BF16)
                dq = _dot(dpb, kb) + _dot_nt(dob, sb) * qd_ref[h]
                dk = _dot_tn(dpb, qb) + _dot_nt(vb, dsb) * kd_ref[h]
                dstate[h] = dsn * cd_ref[h] + _dot_tn(qdq, dob)
                dq_ref[rows, hk] = _rope_half_bwd(dq, co, si).astype(BF16)
                dk_ref[rows, hk] = _rope_half_bwd(dk * scale, co, si).astype(BF16)
                dv_ref[rows, hv] = dv.astype(BF16)

        def copies(sl):
            r = pl.ds(pl.multiple_of((nj - 1 - j) * tb, tb), tb)
            cols = lambda first, w: pl.ds(pl.multiple_of(first + head * w, 128), w)
            return [pltpu.make_async_copy(dq_s.at[sl], out_ref.at[r, cols(0, RET_DK)], sems.at[sl, 0]),
                    pltpu.make_async_copy(dk_s.at[sl], out_ref.at[r, cols(k0, RET_DK)], sems.at[sl, 1]),
                    pltpu.make_async_copy(dv_s.at[sl], out_ref.at[r, cols(v0, RET_DV)], sems.at[sl, 2])]

        @pl.when(step > 0)
        def _():
            for cp in copies(1 - slot):
                cp.wait()

        for cp in copies(slot):
            cp.start()

        @pl.when(step == RET_HEADS * nj - 1)
        def _():
            for cp in copies(slot):
                cp.wait()

    return pl.pallas_call(
        body, name=name, grid=(RET_HEADS, nj),
        in_specs=[sp["q"], sp["k"], sp["v"], sp["tab"], sp["tab"], sp["intra"], sp["dec"], sp["dec"], sp["cd"],
                  sp["s"], sp["o"], pl.BlockSpec(memory_space=pl.ANY)],
        out_specs=pl.BlockSpec(memory_space=pl.ANY), out_shape=jax.ShapeDtypeStruct(dproj.shape, dproj.dtype),
        input_output_aliases={11: 0},
        scratch_shapes=[pltpu.VMEM((2, tb, RET_DK), BF16), pltpu.VMEM((2, tb, RET_DK), BF16),
                        pltpu.VMEM((2, tb, RET_DV), BF16), pltpu.SemaphoreType.DMA((2, 3)),
                        pltpu.VMEM((RET_GROUP, RET_DK, RET_DV), F32)],
        compiler_params=_cparams(("arbitrary", "arbitrary")),
    )(proj, proj, proj, cos, sin, intra, qd, kd, cd, states, dout, dproj)


def _ret_gate(out, proj, gn, name):
    def fn(o, g, *gains):
        g = g.astype(F32)
        parts = [_rms(o[:, h * RET_DV:(h + 1) * RET_DV], gains[h]) for h in range(RET_HEADS)]
        return (g * _sigmoid(g) * jnp.concatenate(parts, axis=-1),)
    w = RET_HEADS * RET_DV
    return _rows(fn, [out, (proj, w, 2)], [gn[h:h + 1] for h in range(RET_HEADS)], [(w, BF16)], name=name)[0]


def _ret_gate_bwd(out, proj, gn, dy, name, after=()):
    def fn(o, g, d, *gains):
        g = g.astype(F32)
        sg = _sigmoid(g)
        silu = g * sg
        dsilu = sg * (1.0 + g * (1.0 - sg))
        dos, dgs = [], []
        row = lax.broadcasted_iota(jnp.int32, (RET_HEADS, RET_DV), 0)
        dgn = jnp.zeros((RET_HEADS, RET_DV), F32)
        for h in range(RET_HEADS):
            sl = slice(h * RET_DV, (h + 1) * RET_DV)
            oh = o[:, sl]
            dgs.append(d[:, sl] * _rms(oh, gains[h]) * dsilu[:, sl])
            dx, dg = _rms_bwd(oh, d[:, sl] * silu[:, sl], gains[h])
            dos.append(dx)
            dgn = dgn + jnp.where(row == h, _colsum(dg), 0.0)
        return jnp.concatenate(dos, axis=-1), jnp.concatenate(dgs, axis=-1), dgn
    w = RET_HEADS * RET_DV
    return _rows(fn, [out, (proj, w, 2), dy], [gn[h:h + 1] for h in range(RET_HEADS)],
                 [(w, BF16), (w, BF16, proj.shape[1], 2)], [((RET_HEADS, RET_DV), F32)], name=name, tile=256,
                 after=after)


def _mla_tables(T):
    ang = _rope_angles(T, MLA_ROPE)
    c, s = np.cos(ang), np.sin(ang)
    z32, z64 = np.zeros((T, 32), np.float32), np.zeros((T, 64), np.float32)
    cos_t = np.concatenate([c, c, z64], axis=1)
    sin_a = np.concatenate([-s, z32, z64], axis=1)
    sin_b = np.concatenate([z32, s, z64], axis=1)
    return tuple(jnp.asarray(t, F32) for t in (cos_t, sin_a, sin_b))


def _rope_blk(x, ct, sa, sb):
    return x * ct + pltpu.roll(x, 96, 1) * sa + pltpu.roll(x, 32, 1) * sb


def _rope_blk_bwd(d, ct, sa, sb):
    return d * ct + pltpu.roll(d * sa, 32, 1) + pltpu.roll(d * sb, 96, 1)


def _head_norm(x, gain):
    r = lax.rsqrt(_rowsum(x * x, True) / MLA_QKD + EPS)
    return (x * r) * gain


def _prep_heads(qv, kvv, kr, ct, sa, sb, gqv, gkv):
    qs, ks, vs = [], [], []
    for h in range(MLA_HEADS):
        b = h * MLA_HP
        y = _head_norm(qv[:, b:b + MLA_HP], gqv)
        qs += [y[:, :128], _rope_blk(y[:, 128:], ct, sa, sb)]
        y = _head_norm(jnp.concatenate([kvv[:, b:b + 128], kr], axis=-1), gkv)
        ks += [y[:, :128], _rope_blk(y[:, 128:], ct, sa, sb)]
        vs.append(kvv[:, b + 128:b + 256])
    return jnp.concatenate(qs, axis=-1), jnp.concatenate(ks, axis=-1), jnp.concatenate(vs, axis=-1)


def _mla_front(hn, W, tabs, name):
    wide = MLA_HEADS * MLA_HP
    gq = W["mla_q_norm"] * (MLA_QKD ** -0.5 * LOG2E)

    def epilogue(acc, ct, sa, sb, gqa, gkva, wuq, wukv, gqv, gkv):
        cqn = _rms(acc[:, :MLA_Q_RANK], gqa).astype(BF16)
        ckvn = _rms(acc[:, MLA_Q_RANK:MLA_Q_RANK + MLA_KV_RANK], gkva).astype(BF16)
        q = jnp.concatenate([_dot(cqn, wuq[s]) for s in range(N_CHIPS)], axis=-1).astype(BF16)
        kv = jnp.concatenate([_dot(ckvn, wukv[s]) for s in range(N_CHIPS)], axis=-1).astype(BF16)
        qf, kf, vf = _prep_heads(q.astype(F32), kv.astype(F32), acc[:, MLA_IN_PAD - 128:], ct, sa, sb, gqv, gkv)
        return acc, cqn, ckvn, q, kv, qf, kf, vf

    return _mm_rows(hn, W["mla_w_in"], extras=list(tabs),
                    fulls=[W["mla_q_a_norm"], W["mla_kv_a_norm"], W["mla_w_uq"], W["mla_w_ukv"], gq, W["mla_k_norm"]],
                    outs=[(MLA_IN_PAD, F32), (MLA_Q_RANK, BF16), (MLA_KV_RANK, BF16), (wide, BF16), (wide, BF16),
                          (wide, BF16), (wide, BF16), (MLA_HEADS * MLA_VD, BF16)],
                    epilogue=epilogue, name=name, tm=256)


def _prep_heads_bwd(qv, kvv, kr, ct, sa, sb, dqv, dkv, dvv, gqv, gkv):
    dqs, dkvs = [], []
    dkr = jnp.zeros_like(kr)
    dgq = jnp.zeros((1, MLA_HP), F32)
    dgk = jnp.zeros((1, MLA_HP), F32)
    for h in range(MLA_HEADS):
        b = h * MLA_HP
        dy = jnp.concatenate([dqv[:, b:b + 128], _rope_blk_bwd(dqv[:, b + 128:b + 256], ct, sa, sb)], axis=-1)
        dx, dg = _rms_bwd(qv[:, b:b + MLA_HP], dy, gqv, MLA_QKD, mxu=True)
        dqs.append(dx)
        dgq = dgq + _colsum(dg)
        dy = jnp.concatenate([dkv[:, b:b + 128], _rope_blk_bwd(dkv[:, b + 128:b + 256], ct, sa, sb)], axis=-1)
        dx, dg = _rms_bwd(jnp.concatenate([kvv[:, b:b + 128], kr], axis=-1), dy, gkv, MLA_QKD, mxu=True)
        dkvs += [dx[:, :128], dvv[:, h * MLA_VD:(h + 1) * MLA_VD].astype(F32)]
        dkr = dkr + dx[:, 128:]
        dgk = dgk + _colsum(dg)
    return jnp.concatenate(dqs, axis=-1), jnp.concatenate(dkvs, axis=-1), dkr, dgq, dgk


def _mla_back(q, kv, proj, h0, dh1, dqf, dkf, dvf, W, tabs, name):
    def fn(qv, kvv, pv, hv, dr, ct, sa, sb, dqv, dkv, dvv, gqv, gkv, gqa, gkva, wuq, wukv, w_in, g_mix):
        qv, kvv, dqv, dkv = (t.astype(F32) for t in (qv, kvv, dqv, dkv))
        dq, dkvx, dkr, dgq, dgk = _prep_heads_bwd(qv, kvv, pv[:, MLA_IN_PAD - 128:], ct, sa, sb, dqv, dkv, dvv, gqv, gkv)
        dq, dkvx = dq.astype(BF16), dkvx.astype(BF16)
        nq = wuq.shape[2]
        dcq = sum(_dot_nt(dq[:, s * nq:(s + 1) * nq], wuq[s]) for s in range(N_CHIPS))
        dckv = sum(_dot_nt(dkvx[:, s * nq:(s + 1) * nq], wukv[s]) for s in range(N_CHIPS))
        dxq, dgqa = _rms_bwd(pv[:, :MLA_Q_RANK], dcq, gqa)
        dxkv, dgkva = _rms_bwd(pv[:, MLA_Q_RANK:MLA_Q_RANK + MLA_KV_RANK], dckv, gkva)
        dproj = jnp.concatenate([dxq, dxkv, dkr], axis=-1).astype(BF16)
        dx, dgm = _rms_bwd(hv, _dot_nt(dproj, w_in), g_mix)
        return (dq, dkvx, dproj, dr + dx, dr + dx, dgq, dgk, _colsum(dgqa), _colsum(dgkva), _colsum(dgm))

    wide = MLA_HEADS * MLA_HP
    return _rows(fn, [q, kv, proj, h0, dh1, *tabs, dqf, dkf, dvf],
                 [W["mla_q_norm"], W["mla_k_norm"], W["mla_q_a_norm"], W["mla_kv_a_norm"], W["mla_w_uq"], W["mla_w_ukv"],
                  W["mla_w_in"], W["mix_norm"][1:2]],
                 [(wide, BF16), (wide, BF16), (MLA_IN_PAD, BF16), ROW_F32, ROW_BF16],
                 [((1, MLA_HP), F32), ((1, MLA_HP), F32), ((1, MLA_Q_RANK), F32), ((1, MLA_KV_RANK), F32),
                  ((1, D_MODEL), F32)], name=name, tile=256)


def _chunk_mask(qi, ki, tq, tk):
    shift = CHUNK.bit_length() - 1
    rq = lax.shift_right_arithmetic(qi * tq + lax.broadcasted_iota(jnp.int32, (tq, tk), 0), shift)
    ck = lax.shift_right_arithmetic(ki * tk + lax.broadcasted_iota(jnp.int32, (tq, tk), 1), shift)
    return ck <= rq


def _flash_fwd(qf, kf, vf, name):
    T = qf.shape[0]
    t = _pick(T, FLASH_T)
    n = T // t
    g = FLASH_HEADS

    def body(q_ref, k_ref, v_ref, o_ref, lse_ref, m_s, l_s, acc):
        qi = pl.program_id(1)
        m_s[...] = jnp.full_like(m_s, NEG)
        l_s[...] = jnp.zeros_like(l_s)
        acc[...] = jnp.zeros_like(acc)

        def step(kb, masked):
            rows = pl.ds(pl.multiple_of(kb * t, t), t)
            for h in range(g):
                hq, hv = slice(h * MLA_HP, (h + 1) * MLA_HP), slice(h * MLA_VD, (h + 1) * MLA_VD)
                s = _dot_nt(q_ref[:, hq], k_ref[rows, hq])
                if masked:
                    s = jnp.where(_chunk_mask(0, 0, t, t), s, NEG)
                m_prev = m_s[:, hv]
                m_new = jnp.maximum(m_prev, jnp.max(s, axis=-1, keepdims=True))
                alpha = jnp.exp2(m_prev - m_new)
                p = jnp.exp2(s - _widen(m_new, t))
                l_s[:, hv] = alpha * l_s[:, hv] + sum(p[:, i * 128:(i + 1) * 128] for i in range(t // 128))
                acc[:, hv] = acc[:, hv] * alpha + _dot(p.astype(BF16), v_ref[rows, hv])
                m_s[:, hv] = m_new

        @pl.loop(0, qi)
        def _(kb):
            step(kb, False)

        step(qi, True)
        for h in range(g):
            hv = slice(h * MLA_VD, (h + 1) * MLA_VD)
            l = jnp.sum(l_s[:, hv], axis=-1, keepdims=True)
            o_ref[:, hv] = acc[:, hv] / l
            lse_ref[:, hv] = m_s[:, hv] + jnp.log2(l)

    qmap = lambda h, i: (i, h)
    kmap = lambda h, i: (0, h)
    vec = pltpu.VMEM((t, g * MLA_VD), F32)
    return pl.pallas_call(
        body, name=name, grid=(MLA_HEADS // g, n),
        in_specs=[pl.BlockSpec((t, g * MLA_HP), qmap), pl.BlockSpec((T, g * MLA_HP), kmap),
                  pl.BlockSpec((T, g * MLA_VD), kmap)],
        out_specs=[pl.BlockSpec((t, g * MLA_VD), qmap), pl.BlockSpec((t, g * MLA_VD), qmap)],
        out_shape=[jax.ShapeDtypeStruct((T, MLA_HEADS * MLA_VD), F32),
                   jax.ShapeDtypeStruct((T, MLA_HEADS * MLA_VD), F32)],
        scratch_shapes=[vec, vec, vec],
        compiler_params=_cparams(("parallel", "arbitrary")),
    )(qf, kf, vf)


def _flash_bwd(qf, kf, vf, do16, lse, delta, name):
    T = qf.shape[0]
    t = _pick(T, FLASH_T)
    n = T // t
    g = FLASH_BWD_HEADS
    scale = MLA_QKD ** -0.5

    def body(q_ref, k_ref, v_ref, do_ref, lse_ref, dl_ref, dq_out, dk_out, dv_out, dq_ref, dk_ref, dv_ref):
        kb = pl.program_id(1)

        @pl.when(kb == 0)
        def _():
            dq_ref[...] = jnp.zeros_like(dq_ref)

        dk_ref[...] = jnp.zeros_like(dk_ref)
        dv_ref[...] = jnp.zeros_like(dv_ref)

        def step(qb, masked):
            rows = pl.ds(pl.multiple_of(qb * t, t), t)
            for h in range(g):
                hq, hv = slice(h * MLA_HP, (h + 1) * MLA_HP), slice(h * MLA_VD, (h + 1) * MLA_VD)
                q, dob, k, v = q_ref[rows, hq], do_ref[rows, hv], k_ref[:, hq], v_ref[:, hv]
                s = _dot_nt(q, k)
                if masked:
                    s = jnp.where(_chunk_mask(0, 0, t, t), s, NEG)
                p = jnp.exp2(s - _widen(lse_ref[rows, hv], t))
                ds = (p * (_dot_nt(dob, v) - _widen(dl_ref[rows, hv], t))).astype(BF16)
                dv_ref[:, hv] += _dot_tn(p.astype(BF16), dob)
                dk_ref[:, hq] += _dot_tn(ds, q)
                dq_ref[rows, hq] += _dot(ds, k)

        step(kb, True)

        @pl.loop(kb + 1, n)
        def _(qb):
            step(qb, False)

        dk_out[...] = (dk_ref[...] * (1.0 / LOG2E)).astype(BF16)
        dv_out[...] = dv_ref[...].astype(BF16)

        @pl.when(kb == n - 1)
        def _():
            dq_out[...] = (dq_ref[...] * scale).astype(BF16)

    qmap = lambda h, j: (0, h)
    kmap = lambda h, j: (j, h)
    wq, wv = g * MLA_HP, g * MLA_VD
    return pl.pallas_call(
        body, name=name, grid=(MLA_HEADS // g, n),
        in_specs=[pl.BlockSpec((T, wq), qmap), pl.BlockSpec((t, wq), kmap), pl.BlockSpec((t, wv), kmap),
                  pl.BlockSpec((T, wv), qmap), pl.BlockSpec((T, wv), qmap), pl.BlockSpec((T, wv), qmap)],
        out_specs=[pl.BlockSpec((T, wq), qmap), pl.BlockSpec((t, wq), kmap), pl.BlockSpec((t, wv), kmap)],
        out_shape=[jax.ShapeDtypeStruct((T, MLA_HEADS * MLA_HP), BF16),
                   jax.ShapeDtypeStruct((T, MLA_HEADS * MLA_HP), BF16),
                   jax.ShapeDtypeStruct((T, MLA_HEADS * MLA_VD), BF16)],
        scratch_shapes=[pltpu.VMEM((T, wq), F32), pltpu.VMEM((t, wq), F32), pltpu.VMEM((t, wv), F32)],
        compiler_params=_cparams(("arbitrary", "arbitrary")),
    )(qf, kf, vf, do16, lse, delta)


MESH = pl.DeviceIdType.MESH
ANY = pl.BlockSpec(memory_space=pl.ANY)
_CHIP_FLIPS = ((1, 0), (0, 1), (1, 1))


def _place():
    return lax.axis_index("x"), lax.axis_index("y"), lax.axis_index("c")


def _other_chip(x, y, k):
    fx, fy = _CHIP_FLIPS[k]
    return ((1 - x) if fx else x), ((1 - y) if fy else y)


def _remote(src, dst, send_sems, recv_sems, k, to):
    return pltpu.make_async_remote_copy(src_ref=src, dst_ref=dst, send_sem=send_sems.at[k], recv_sem=recv_sems.at[k],
                                        device_id=to, device_id_type=MESH)


def _index(*vals):
    return jnp.stack(vals).astype(jnp.int32)


def _half(c, rows):
    return pl.ds(pl.multiple_of(c * rows, 16), rows)


def _gather_weights(parts, name, landed=None):
    n_w = len(parts)
    n_in = n_w if landed is None else 2 * n_w

    def body(*refs):
        ins, outs = refs[:n_w], refs[n_in:n_in + n_w]
        send_sems, recv_sems, local_sems = refs[n_in + n_w:]
        x, y, c = _place()
        j = 2 * x + y
        sibling = (x, y, 1 - c)
        chips = [_other_chip(x, y, k) for k in range(3)]
        pending = []
        for w in range(n_w):
            own = pltpu.make_async_copy(ins[w], outs[w].at[j], local_sems.at[w])
            own.start()
            pending.append(own)
        sent = []
        for w in range(n_w):
            if landed is not None:
                break
            r = _half(c, parts[w].shape[0] // 2)
            for k, (px, py) in enumerate(chips):
                cp = _remote(ins[w].at[r], outs[w].at[j, r], send_sems, recv_sems, 6 * w + k, (px, py, c))
                cp.start()
                sent.append(cp)
        for w in range(n_w):
            r = _half(c, parts[w].shape[0] // 2)
            for k, (px, py) in enumerate(chips):
                blk = outs[w].at[2 * px + py, r]
                if landed is None:
                    _remote(blk, blk, send_sems, recv_sems, 6 * w + k, (px, py, c)).wait_recv()
                cp = _remote(blk, blk, send_sems, recv_sems, 6 * w + 3 + k, sibling)
                cp.start()
                sent.append(cp)
        for w in range(n_w):
            r = _half(1 - c, parts[w].shape[0] // 2)
            for k, (px, py) in enumerate(chips):
                blk = outs[w].at[2 * px + py, r]
                _remote(blk, blk, send_sems, recv_sems, 6 * w + 3 + k, sibling).wait_recv()
        for cp in sent:
            cp.wait_send()
        for cp in pending:
            cp.wait()

    return pl.pallas_call(
        body, name=name, in_specs=[pl.BlockSpec(memory_space=pltpu.VMEM)] * n_w + [ANY] * (n_in - n_w),
        out_specs=[ANY] * n_w,
        out_shape=[jax.ShapeDtypeStruct((N_CHIPS, *p.shape), p.dtype) for p in parts],
        input_output_aliases={} if landed is None else {n_w + w: w for w in range(n_w)},
        scratch_shapes=[pltpu.SemaphoreType.DMA((6 * n_w,)), pltpu.SemaphoreType.DMA((6 * n_w,)),
                        pltpu.SemaphoreType.DMA((n_w,))],
        compiler_params=pltpu.CompilerParams(vmem_limit_bytes=VMEM_LIMIT),
    )(*parts, *(landed or []))


def _swap_halves(gs, name):
    n_w = len(gs)

    def body(*refs):
        g_refs, recv_refs = refs[:n_w], refs[n_w:2 * n_w]
        send_sems, recv_sems = refs[2 * n_w:]
        x, y, c = _place()
        sent = []
        for w in range(n_w):
            for jj in range(N_CHIPS):
                cp = _remote(g_refs[w].at[jj, 1 - c], recv_refs[w].at[jj], send_sems, recv_sems, N_CHIPS * w + jj,
                             (x, y, 1 - c))
                cp.start()
                sent.append(cp)
        for cp in sent:
            cp.wait()

    return pl.pallas_call(
        body, name=name, in_specs=[ANY] * n_w, out_specs=[ANY] * n_w,
        out_shape=[jax.ShapeDtypeStruct((N_CHIPS, *g.shape[2:]), g.dtype) for g in gs],
        scratch_shapes=[pltpu.SemaphoreType.DMA((N_CHIPS * n_w,)), pltpu.SemaphoreType.DMA((N_CHIPS * n_w,))],
    )(*gs)


def _pair_sum(g, recv, core, name):
    _, H, C = recv.shape
    tile = _pick(H, SUM_ROWS)

    def body(c_ref, own_ref, recv_ref, out_ref):
        out_ref[...] = (own_ref[...].astype(F32) + recv_ref[...].astype(F32)).astype(BF16)

    blk = pl.BlockSpec((None, tile, C), lambda jj, i, c: (jj, i, 0))
    return pl.pallas_call(
        body, name=name,
        grid_spec=pltpu.PrefetchScalarGridSpec(
            num_scalar_prefetch=1, grid=(N_CHIPS, H // tile),
            in_specs=[pl.BlockSpec((None, None, tile, C), lambda jj, i, c: (jj, c[0], i, 0)), blk],
            out_specs=blk),
        out_shape=jax.ShapeDtypeStruct((N_CHIPS, H, C), BF16),
        compiler_params=_cparams(("arbitrary", "arbitrary")),
    )(_index(core), g, recv)


def _chip_sum(g, recv, got, chip, core, name):
    _, H, C = recv.shape
    tile = _pick(H, SUM_ROWS)

    def body(s_ref, own_ref, recv_ref, g0_ref, g1_ref, g2_ref, out_ref):
        pair = own_ref[...].astype(F32) + recv_ref[...].astype(F32)
        out_ref[...] = ((pair + g0_ref[...].astype(F32)) + g1_ref[...].astype(F32)) + g2_ref[...].astype(F32)

    def got_spec(k):
        return pl.BlockSpec((None, tile, C), lambda i, s, k=k: (k, i, 0))

    return pl.pallas_call(
        body, name=name,
        grid_spec=pltpu.PrefetchScalarGridSpec(
            num_scalar_prefetch=1, grid=(H // tile,),
            in_specs=[pl.BlockSpec((None, None, tile, C), lambda i, s: (s[0], s[1], i, 0)),
                      pl.BlockSpec((None, tile, C), lambda i, s: (s[0], i, 0)), got_spec(0), got_spec(1), got_spec(2)],
            out_specs=pl.BlockSpec((None, tile, C), lambda i, s: (s[1], i, 0))),
        out_shape=jax.ShapeDtypeStruct((2, H, C), F32),
        compiler_params=_cparams(("arbitrary",)),
    )(_index(chip, core), g, recv, got, got, got)


def _share_halves(reds):
    n_w = len(reds)

    def body(*refs):
        out_refs = refs[n_w:2 * n_w]
        send_sems, recv_sems = refs[2 * n_w:]
        x, y, c = _place()
        sent = []
        for w in range(n_w):
            blk = out_refs[w].at[c]
            cp = _remote(blk, blk, send_sems, recv_sems, w, (x, y, 1 - c))
            cp.start()
            sent.append(cp)
        for cp in sent:
            cp.wait()

    return pl.pallas_call(
        body, name="grad_share_halves", in_specs=[ANY] * n_w, out_specs=[ANY] * n_w,
        out_shape=[jax.ShapeDtypeStruct(r.shape, r.dtype) for r in reds],
        input_output_aliases={w: w for w in range(n_w)},
        scratch_shapes=[pltpu.SemaphoreType.DMA((n_w,)), pltpu.SemaphoreType.DMA((n_w,))],
    )(*reds)


def _allsum_small(v, name):
    R, W = v.shape
    n_dev = 8
    vm = pl.BlockSpec(memory_space=pltpu.VMEM)

    def body(v_ref, out_ref, buf, send_sems, recv_sems):
        x, y, c = _place()
        me = 4 * x + 2 * y + c
        buf[me] = v_ref[...]
        sent = []
        for k in range(1, n_dev):
            peer = ((1 - x) if k & 4 else x, (1 - y) if k & 2 else y, (1 - c) if k & 1 else c)
            cp = _remote(v_ref, buf.at[me], send_sems, recv_sems, k - 1, peer)
            cp.start()
            sent.append(cp)
        for cp in sent:
            cp.wait_recv()
        for cp in sent:
            cp.wait_send()
        acc = buf[0]
        for q in range(1, n_dev):
            acc = acc + buf[q]
        out_ref[...] = acc

    return pl.pallas_call(
        body, name=name, in_specs=[vm], out_specs=vm, out_shape=jax.ShapeDtypeStruct((R, W), v.dtype),
        scratch_shapes=[pltpu.VMEM((n_dev, R, W), v.dtype), pltpu.SemaphoreType.DMA((n_dev - 1,)),
                        pltpu.SemaphoreType.DMA((n_dev - 1,))],
    )(v)


HBM = pl.BlockSpec(memory_space=pltpu.HBM)
SEM = pl.BlockSpec(memory_space=pltpu.SEMAPHORE)
_DATAFLOW = pltpu.SideEffectType.DATAFLOW_SIDE_EFFECTING


def _split_start(name, srcs, land_shapes, n_copies, copies, after=()):
    ns, nl = len(srcs), len(land_shapes)
    lands = [lax.empty(s.shape, s.dtype) for s in land_shapes]

    def body(*refs):
        outs = refs[ns + nl + len(after):]
        for cp in copies(refs[:ns], refs[ns:ns + nl], outs[0], outs[1]):
            cp.start()
        outs[-1][...] = jnp.zeros_like(outs[-1])

    sems = pltpu.SemaphoreType.DMA((n_copies,))
    res = pl.pallas_call(
        body, name=name, in_specs=[HBM] * (ns + nl) + [ANY] * len(after),
        out_specs=(SEM, SEM, *[HBM] * (ns + nl), pl.BlockSpec(memory_space=pltpu.VMEM)),
        out_shape=(sems, sems, *[pltpu.HBM(a.shape, a.dtype) for a in srcs],
                   *[pltpu.HBM(s.shape, s.dtype) for s in land_shapes], jax.ShapeDtypeStruct((8, 128), F32)),
        input_output_aliases={i: 2 + i for i in range(ns + nl)},
        compiler_params=pltpu.CompilerParams(has_side_effects=_DATAFLOW),
    )(*[pltpu.with_memory_space_constraint(a, pltpu.HBM) for a in [*srcs, *lands]], *after)
    return res[0], res[1], list(res[2:2 + ns]), list(res[2 + ns:2 + ns + nl]), res[-1]


def _split_wait(name, send_sems, recv_sems, srcs, lands, copies, after=()):
    ns, nl = len(srcs), len(lands)

    def body(*refs):
        for cp in copies(refs[:ns], refs[ns:ns + nl], refs[ns + nl], refs[ns + nl + 1]):
            cp.wait_send()
            cp.wait_recv()

    res = pl.pallas_call(
        body, name=name, in_specs=[HBM] * (ns + nl) + [SEM, SEM] + [ANY] * len(after), out_specs=[HBM] * (ns + nl),
        out_shape=[pltpu.HBM(a.shape, a.dtype) for a in [*srcs, *lands]],
        input_output_aliases={i: i for i in range(ns + nl)},
        compiler_params=pltpu.CompilerParams(has_side_effects=_DATAFLOW),
    )(*srcs, *lands, send_sems, recv_sems, *after)
    return list(res[:ns]), list(res[ns:])


def _gather_copies(rows):
    def copies(src_refs, land_refs, send_sems, recv_sems):
        x, y, c = _place()
        j = 2 * x + y
        out = []
        for w in range(len(src_refs)):
            r = _half(c, rows[w] // 2)
            for k in range(3):
                px, py = _other_chip(x, y, k)
                out.append(_remote(src_refs[w].at[r], land_refs[w].at[j, r], send_sems, recv_sems, 3 * w + k, (px, py, c)))
        return out
    return copies


def _scatter_copies(src_refs, land_refs, send_sems, recv_sems):
    x, y, c = _place()
    j = 2 * x + y
    out = []
    for w in range(len(src_refs)):
        for k in range(3):
            px, py = _other_chip(x, y, k)
            pj = 2 * px + py
            out.append(_remote(src_refs[w].at[pj], land_refs[w].at[(j - pj + 4) % 4 - 1], send_sems, recv_sems, 3 * w + k,
                               (px, py, c)))
    return out


def _halves(grads):
    names = list(grads)
    return names, [grads[k].reshape(N_CHIPS, 2, -1, grads[k].shape[-1]) for k in names]


def _reduce_begin(grads, core, tag):
    names, gs = _halves(grads)
    recvs = _swap_halves(gs, f"grad_swap_halves_{tag}")
    sums = [_pair_sum(g, r, core, f"pair_sum_{k}") for k, g, r in zip(names, gs, recvs)]
    return names, gs, recvs, sums


def _swap_copies(src_refs, land_refs, send_sems, recv_sems):
    x, y, c = _place()
    return [_remote(src_refs[w].at[jj, 1 - c], land_refs[w].at[jj], send_sems, recv_sems, N_CHIPS * w + jj, (x, y, 1 - c))
            for w in range(len(src_refs)) for jj in range(N_CHIPS)]


def _swap_begin(grads, tag):
    names, gs = _halves(grads)
    started = _split_start(f"swap_{tag}_start", gs, [jax.ShapeDtypeStruct((N_CHIPS, *g.shape[2:]), g.dtype) for g in gs],
                           N_CHIPS * len(gs), _swap_copies)
    return (names, started[:4]), started[4]


def _swap_end(begun, core, tag, after):
    names, started = begun
    gs, recvs = _split_wait(f"swap_{tag}_wait", *started, _swap_copies, after=after)
    sums = [_pair_sum(g, r, core, f"pair_sum_{k}") for k, g, r in zip(names, gs, recvs)]
    return names, gs, recvs, sums


def _reduce_end(begun, gots, chip, core):
    names, gs, recvs, _ = begun
    return {k: _chip_sum(g, r, t, chip, core, f"chip_sum_{k}") for k, g, r, t in zip(names, gs, recvs, gots)}


def _got_shapes(sums):
    return [jax.ShapeDtypeStruct((3, *a.shape[1:]), a.dtype) for a in sums]


def _adamw(w, g, m, v, name, layers=1, layer=0, into=None):
    shape = w.shape
    cols = shape[-1]
    w3, m3, v3 = (t.reshape(layers, -1, cols) for t in (w, m, v))
    rows = w3.shape[1]
    tile = _pick(rows, ADAM_ROWS if cols <= 1024 else ADAM_ROWS // 2) if rows % 8 == 0 else rows
    n_in = 4 + (0 if into is None else 4)
    stack_g = layers > 1

    def body(*refs):
        wv, gv, mv, vv = (r[...] for r in refs[:4])
        d_ref, m_ref, v_ref = refs[len(refs) - 3:]
        m2 = ADAM_B1 * mv + (1.0 - ADAM_B1) * gv
        v2 = ADAM_B2 * vv + (1.0 - ADAM_B2) * jnp.square(gv)
        m_hat = m2 / (1.0 - ADAM_B1 ** ADAM_STEP)
        v_hat = v2 / (1.0 - ADAM_B2 ** ADAM_STEP)
        if stack_g:
            refs[n_in][...] = gv
        d_ref[...] = -ADAM_LR * (m_hat / (jnp.sqrt(v_hat) + ADAM_EPS) + ADAM_WD * wv)
        m_ref[...] = m2
        v_ref[...] = v2

    n_out = 4 if stack_g else 3
    lay = pl.BlockSpec((None, tile, cols), lambda i: (layer, i, 0))
    out = jax.ShapeDtypeStruct((layers, rows, cols), F32)
    res = pl.pallas_call(
        body, name=name, grid=(rows // tile,),
        in_specs=[lay, pl.BlockSpec((tile, cols), lambda i: (i, 0)), lay, lay] + [ANY] * (n_in - 4),
        out_specs=[lay] * n_out, out_shape=[out] * n_out,
        input_output_aliases={} if into is None else {4 + k: k for k in range(4)},
        compiler_params=_cparams(("arbitrary",)),
    )(w3, g.reshape(rows, cols), m3, v3, *([] if into is None else [t.reshape(layers, rows, cols) for t in into]))
    res = tuple(t.reshape(shape) for t in res)
    return res if stack_g else (g.reshape(shape), *res)


ROW_F32, ROW_BF16 = (D_MODEL, F32), (D_MODEL, BF16)


def _res_norm(acc, h, gain):
    hh = h + acc
    return hh, _rms(hh, gain)


def _dx_norm_bwd(d, w, h, dres, gain, name, **kw):
    def epilogue(acc, hv, dr, g):
        dx, dg = _rms_bwd(hv, acc, g)
        return dr + dx, dr + dx, _colsum(dg)
    return _mm_rows(d, w, tb=True, extras=[h, dres], fulls=[gain], outs=[ROW_F32, ROW_BF16], accs=[((1, D_MODEL), F32)],
                    epilogue=epilogue, name=name, **kw)


def _tail_fwd(h1, hn2, p16, W, i, tag, next_gain=None, target=None):
    a = _mm(hn2, W["mlp_w1"][i], bblk=True, outs=[BF16], name=f"{tag}_mlp_w1", tm=2048, tn=1024,
            epilogue=lambda acc: (jnp.square(jnp.maximum(acc, 0.0)),))
    h2, hn3 = _mm_rows(a, W["mlp_w2"][i], extras=[h1], fulls=[W["ple_norm"][i:i + 1]], outs=[ROW_F32, ROW_BF16],
                       epilogue=_res_norm, name=f"{tag}_mlp_w2")
    def embed(acc, pv, h, wp):
        gate = _sigmoid(acc)
        ppv = jnp.concatenate([_dot(pv, wp[s]) for s in range(N_CHIPS)], axis=-1)
        return gate, ppv, h + gate * ppv

    if target is None:
        def gated(acc, pv, h, wp, gain):
            gate, ppv, hh = embed(acc, pv, h, wp)
            return hh, ppv, gate, _rms(hh, gain)
        h3, pp, gate, hn = _mm_rows(hn3, W["ple_gate_w"][i], extras=[p16[i], h2], fulls=[W["ple_proj_w"][i], next_gain],
                                    outs=[ROW_F32, ROW_BF16, ROW_BF16, ROW_BF16], epilogue=gated, name=f"{tag}_ple")
        return h3, hn, (h1, hn2, a, h2, hn3, gate, pp)

    def gated_loss(acc, pv, h, t, wp):
        gate, ppv, hh = embed(acc, pv, h, wp)
        e = hh - t
        return ppv, gate, e * (1.0 / D_MODEL), jnp.full((1, 128), 0.5 / D_MODEL, F32) * jnp.sum(e * e)
    pp, gate, dy, loss = _mm_rows(hn3, W["ple_gate_w"][i], extras=[p16[i], h2, target], fulls=[W["ple_proj_w"][i]],
                                  outs=[ROW_BF16, ROW_BF16, ROW_F32], accs=[((1, 128), F32)], epilogue=gated_loss,
                                  name=f"{tag}_ple")
    return dy, loss, (h1, hn2, a, h2, hn3, gate, pp)


def _tail_bwd(dh3, saved, p16, W, i, tag, after=(), hook=None):
    h1, hn2, a, h2, hn3, gate, pp = saved

    def embed_bwd(d, g, ppv, hv, wg, gain):
        g, ppv = g.astype(F32), ppv.astype(F32)
        dppv, dglv = (d * g).astype(BF16), (d * ppv * g * (1.0 - g)).astype(BF16)
        dx, dg = _rms_bwd(hv, _dot_nt(dglv, wg), gain)
        return dppv, dglv, d + dx, d + dx, _colsum(dg)

    def dw(kind, name):
        return (kind, 1, 0, None)

    dpp, dgl, dh2, dh2_16, d_ple_norm = _rows(
        embed_bwd, [dh3, gate, pp, h2], [W["ple_gate_w"][i], W["ple_norm"][i:i + 1]],
        [ROW_BF16, ROW_BF16, ROW_F32, ROW_BF16], [((1, D_MODEL), F32)], name=f"{tag}_ple_bwd", after=after)
    later = () if hook is None else hook(dh2_16)
    d_proj = _mm(p16[i], dpp, ta=True, outs=[BF16], dw=dw("cols", "ple_proj_w"), name=f"{tag}_d_ple_proj", after=later)
    d_gate = _mm(hn3, dgl, ta=True, outs=[BF16], dw=dw("rows", "ple_gate_w"), name=f"{tag}_d_ple_gate")
    d_w2 = _mm(a, dh2_16, ta=True, outs=[BF16], dw=dw("rows", "mlp_w2"), name=f"{tag}_d_mlp_w2", tn=1024)
    dz = _mm(dh2_16, W["mlp_w2"][i], tb=True, extras=[a], outs=[BF16], name=f"{tag}_mlp_w2_dx", tm=2048, tn=1024,
             epilogue=lambda acc, av: (acc * (2.0 * jnp.sqrt(av.astype(F32))),))
    d_w1 = _mm(hn2, dz, ta=True, outs=[BF16], dw=dw("cols", "mlp_w1"), name=f"{tag}_d_mlp_w1", tn=1024)
    dh1, dh1_16, d_mlp_norm = _dx_norm_bwd(dz, W["mlp_w1"][i], h1, dh2, W["mlp_norm"][i:i + 1], f"{tag}_mlp_w1_dx",
                                           bblk=True)
    big = {f"mlp_w1_{i}": d_w1, f"mlp_w2_{i}": d_w2, f"ple_gate_w_{i}": d_gate, f"ple_proj_w_{i}": d_proj}
    return dh1, dh1_16, big, dict(mlp_norm=d_mlp_norm, ple_norm=d_ple_norm)


def _ret_layer_fwd(h0, W, tabs, after=()):
    hn = _rows(lambda x, g: (_rms(x, g),), [h0], [W["mix_norm"][0:1]], [(D_MODEL, BF16)], name="ret_mix_norm",
               after=after)[0]
    proj = _mm(hn, W["ret_w_in"], bblk=True, outs=[BF16], name="ret_w_in", tm=2048, tn=768)
    out, states = _ret_fwd(proj, tabs, "ret_scan")
    y = _ret_gate(out, proj, W["ret_gn"], "ret_gate")
    h1, hn2 = _mm_rows(y, W["ret_w_out"], extras=[h0], fulls=[W["mlp_norm"][0:1]], outs=[ROW_F32, ROW_BF16],
                       epilogue=_res_norm, name="ret_w_out")
    return h1, hn2, (h0, hn, proj, out, states, y)


def _d_ret_w_out(dh1_16, saved):
    return _mm(saved[5], dh1_16, ta=True, outs=[BF16], dw=("rows", 1, 0, None), name="d_ret_w_out")


def _ret_layer_bwd(dh1, dh1_16, saved, W, tabs, after=(), hook=None, on_grads=None, d_w_out=None):
    h0, hn, proj, out, states, y = saved
    d_w_out = _d_ret_w_out(dh1_16, saved) if d_w_out is None else d_w_out
    dy = _mm(dh1_16, W["ret_w_out"], tb=True, name="ret_w_out_dx", tn=1024, after=after)
    dout, dproj, d_gn = _ret_gate_bwd(out, proj, W["ret_gn"], dy, "ret_gate_bwd",
                                      after=() if hook is None else hook(dy))
    dproj = _ret_bwd(proj, states, dout, dproj, tabs, "ret_scan_bwd")
    d_w_in = _mm(hn, dproj, ta=True, outs=[BF16], dw=("cols", 1, 0, None), name="d_ret_w_in", tn=768)
    big = dict(ret_w_in=d_w_in, ret_w_out=d_w_out)
    later = () if on_grads is None else on_grads(big)
    dh0, _, d_mix = _dx_norm_bwd(dproj, W["ret_w_in"], h0, dh1, W["mix_norm"][0:1], "ret_w_in_dx", bblk=True, tm=512,
                                 after=later)
    return dh0, big, dict(mix_norm=d_mix, ret_gn=d_gn)


def _mla_layer_fwd(h0, hn, W, tabs):
    proj, cqn, ckvn, q, kv, qf, kf, vf = _mla_front(hn, W, tabs, "mla_front")
    o, lse = _flash_fwd(qf, kf, vf, "mla_flash")
    h1, hn2 = _mm_rows(o, W["mla_w_out"], extras=[h0], fulls=[W["mlp_norm"][1:2]], outs=[ROW_F32, ROW_BF16],
                       epilogue=_res_norm, name="mla_w_out")
    return h1, hn2, (h0, hn, proj, cqn, ckvn, q, kv, qf, kf, vf, o, lse)


def _mla_layer_bwd(dh1, dh1_16, saved, W, tabs):
    h0, hn, proj, cqn, ckvn, q, kv, qf, kf, vf, o, lse = saved
    d_w_out = _mm(o, dh1_16, ta=True, outs=[BF16], dw=("rows", 1, 0, None), name="d_mla_w_out")
    def with_delta(acc, ov):
        parts = []
        for h in range(MLA_HEADS):
            sl = slice(h * MLA_VD, (h + 1) * MLA_VD)
            d = jnp.sum(acc[:, sl] * ov[:, sl], axis=-1, keepdims=True)
            parts.append(jnp.broadcast_to(d, (d.shape[0], MLA_VD)))
        return jnp.concatenate(parts, axis=-1), acc

    delta, do16 = _mm_rows(dh1_16, W["mla_w_out"], tb=True, extras=[o], outs=[ROW_F32, ROW_BF16], epilogue=with_delta,
                           name="mla_w_out_dx")
    dqf, dkf, dvf = _flash_bwd(qf, kf, vf, do16, lse, delta, "mla_flash_bwd")
    dq, dkv, dproj, dh0, dh0_16, d_gq, d_gk, d_gqa, d_gkva, d_mix = _mla_back(q, kv, proj, h0, dh1, dqf, dkf, dvf, W, tabs,
                                                                              "mla_back")
    d_w_uq = _mm(cqn, dq, ta=True, outs=[BF16], dw=("cols", 1, 0, None), name="d_mla_w_uq")
    d_w_ukv = _mm(ckvn, dkv, ta=True, outs=[BF16], dw=("cols", 1, 0, None), name="d_mla_w_ukv")
    d_w_in = _mm(hn, dproj, ta=True, outs=[BF16], dw=("rows", 1, 0, None), name="d_mla_w_in")
    return (dh0, dh0_16, dict(mla_w_in=d_w_in, mla_w_uq=d_w_uq, mla_w_ukv=d_w_ukv, mla_w_out=d_w_out),
            dict(mix_norm=d_mix, mla_q_a_norm=d_gqa, mla_kv_a_norm=d_gkva, mla_q_norm=d_gq, mla_k_norm=d_gk))


def _small_grads(n_ret, n_t0, n_mla, n_t1):
    return dict(
        mix_norm=jnp.concatenate([n_ret["mix_norm"], n_mla["mix_norm"]], axis=0),
        mlp_norm=jnp.concatenate([n_t0["mlp_norm"], n_t1["mlp_norm"]], axis=0),
        ple_norm=jnp.concatenate([n_t0["ple_norm"], n_t1["ple_norm"]], axis=0),
        ret_gn=n_ret["ret_gn"], mla_q_a_norm=n_mla["mla_q_a_norm"], mla_kv_a_norm=n_mla["mla_kv_a_norm"],
        mla_q_norm=n_mla["mla_q_norm"], mla_k_norm=n_mla["mla_k_norm"])


_ORDER = ("mix_norm", "ret_w_in", "ret_gn", "ret_w_out", "mla_w_in", "mla_q_a_norm", "mla_kv_a_norm", "mla_w_uq",
          "mla_w_ukv", "mla_q_norm", "mla_k_norm", "mla_w_out", "mlp_norm", "mlp_w1", "mlp_w2", "ple_norm",
          "ple_gate_w", "ple_proj_w")
_TWO_LAYER = ("mlp_w1", "mlp_w2", "ple_gate_w", "ple_proj_w")
HEADS_PER_CHIP = MLA_HEADS // N_CHIPS
GAIN_ROWS = 32


def _travel_parts(w):
    uq = jnp.pad(w["mla_w_uq"][0].reshape(MLA_Q_RANK, HEADS_PER_CHIP, MLA_QKD), ((0, 0), (0, 0), (0, MLA_HP - MLA_QKD)))
    parts = {"ret_w_in": w["ret_w_in"][0], "ret_w_out": w["ret_w_out"][0]}
    for k in _TWO_LAYER:
        parts[k + "_0"] = w[k][0]
    parts["mla_w_in"] = jnp.pad(w["mla_w_in"][0], ((0, 0), (0, MLA_IN_PAD - MLA_IN)))
    parts["mla_w_uq"] = uq.reshape(MLA_Q_RANK, HEADS_PER_CHIP * MLA_HP)
    parts["mla_w_ukv"] = w["mla_w_ukv"][0]
    parts["mla_w_out"] = w["mla_w_out"][0]
    for k in _TWO_LAYER:
        parts[k + "_1"] = w[k][1]
    gains = jnp.concatenate([_pad_row(w["ret_gn"]), _pad_row(w["mla_q_a_norm"]), _pad_row(w["mla_kv_a_norm"]),
                             jnp.zeros((GAIN_ROWS - 3, PACK_W), F32)], axis=0)
    return {"gains": gains, **{k: v.astype(BF16) for k, v in parts.items()}}


def _full_weights(full):
    rows = lambda a: a.reshape(-1, a.shape[-1])
    W = {k: full[k] for k in ("ret_w_in", "mla_w_uq", "mla_w_ukv") if k in full}
    for k in ("ret_w_out", "mla_w_in", "mla_w_out"):
        if k in full:
            W[k] = rows(full[k])
    for k, by_rows in (("mlp_w1", False), ("ple_proj_w", False), ("mlp_w2", True), ("ple_gate_w", True)):
        layers = [full.get(f"{k}_{i}") for i in range(2)]
        W[k] = [rows(t) if (by_rows and t is not None) else t for t in layers]
    return W


def _shard_grad(name, red, shape):
    if name == "mla_w_in":
        red = red.reshape(-1, MLA_IN_PAD)[:, :MLA_IN]
    elif name == "mla_w_uq":
        red = red.reshape(MLA_Q_RANK, HEADS_PER_CHIP, MLA_HP)[:, :, :MLA_QKD]
    return red.reshape(shape)


def _pad_row(v):
    v = v.reshape(1, -1)
    return jnp.pad(v, ((0, 0), (0, PACK_W - v.shape[1])))


def kernel(x, p, mix_norm, ret_w_in, ret_gn, ret_w_out, mla_w_in, mla_q_a_norm, mla_kv_a_norm, mla_w_uq, mla_w_ukv, mla_q_norm, mla_k_norm, mla_w_out, mlp_norm, mlp_w1, mlp_w2, ple_norm, ple_gate_w, ple_proj_w, loss_target, m_mix_norm, m_ret_w_in, m_ret_gn, m_ret_w_out, m_mla_w_in, m_mla_q_a_norm, m_mla_kv_a_norm, m_mla_w_uq, m_mla_w_ukv, m_mla_q_norm, m_mla_k_norm, m_mla_w_out, m_mlp_norm, m_mlp_w1, m_mlp_w2, m_ple_norm, m_ple_gate_w, m_ple_proj_w, v_mix_norm, v_ret_w_in, v_ret_gn, v_ret_w_out, v_mla_w_in, v_mla_q_a_norm, v_mla_kv_a_norm, v_mla_w_uq, v_mla_w_ukv, v_mla_q_norm, v_mla_k_norm, v_mla_w_out, v_mlp_norm, v_mlp_w1, v_mlp_w2, v_ple_norm, v_ple_gate_w, v_ple_proj_w):
    w = dict(mix_norm=mix_norm, ret_w_in=ret_w_in, ret_gn=ret_gn, ret_w_out=ret_w_out, mla_w_in=mla_w_in,
             mla_q_a_norm=mla_q_a_norm, mla_kv_a_norm=mla_kv_a_norm, mla_w_uq=mla_w_uq, mla_w_ukv=mla_w_ukv,
             mla_q_norm=mla_q_norm, mla_k_norm=mla_k_norm, mla_w_out=mla_w_out, mlp_norm=mlp_norm, mlp_w1=mlp_w1,
             mlp_w2=mlp_w2, ple_norm=ple_norm, ple_gate_w=ple_gate_w, ple_proj_w=ple_proj_w)
    m = dict(mix_norm=m_mix_norm, ret_w_in=m_ret_w_in, ret_gn=m_ret_gn, ret_w_out=m_ret_w_out, mla_w_in=m_mla_w_in,
             mla_q_a_norm=m_mla_q_a_norm, mla_kv_a_norm=m_mla_kv_a_norm, mla_w_uq=m_mla_w_uq, mla_w_ukv=m_mla_w_ukv,
             mla_q_norm=m_mla_q_norm, mla_k_norm=m_mla_k_norm, mla_w_out=m_mla_w_out, mlp_norm=m_mlp_norm,
             mlp_w1=m_mlp_w1, mlp_w2=m_mlp_w2, ple_norm=m_ple_norm, ple_gate_w=m_ple_gate_w, ple_proj_w=m_ple_proj_w)
    v = dict(mix_norm=v_mix_norm, ret_w_in=v_ret_w_in, ret_gn=v_ret_gn, ret_w_out=v_ret_w_out, mla_w_in=v_mla_w_in,
             mla_q_a_norm=v_mla_q_a_norm, mla_kv_a_norm=v_mla_kv_a_norm, mla_w_uq=v_mla_w_uq, mla_w_ukv=v_mla_w_ukv,
             mla_q_norm=v_mla_q_norm, mla_k_norm=v_mla_k_norm, mla_w_out=v_mla_w_out, mlp_norm=v_mlp_norm,
             mlp_w1=v_mlp_w1, mlp_w2=v_mlp_w2, ple_norm=v_ple_norm, ple_gate_w=v_ple_gate_w, ple_proj_w=v_ple_proj_w)
    xi, yi, ci = _place()
    chip = 2 * xi + yi
    n = N_CHIPS

    parts = _travel_parts(w)
    first = ("gains", "ret_w_in", "ret_w_out")
    mid = [k + "_0" for k in _TWO_LAYER]
    last = [k for k in parts if k not in first and k not in mid]
    full = dict(zip(first, _gather_weights([parts[k] for k in first], "gather_first")))

    def gather_behind(names, tag, after):
        copies = _gather_copies([parts[k].shape[0] for k in names])
        started = _split_start(f"gather_{tag}_start", [parts[k] for k in names],
                               [jax.ShapeDtypeStruct((n, *parts[k].shape), BF16) for k in names], 3 * len(names),
                               copies, after=after)

        def arrive(after):
            _, landed = _split_wait(f"gather_{tag}_wait", *started[:4], copies, after=after)
            full.update(zip(names, _gather_weights([parts[k] for k in names], f"gather_{tag}_finish", landed=landed)))
            W.update(_full_weights(full))
        return started[4], arrive

    mid_token, mid_arrive = gather_behind(mid, "mid", [full["ret_w_in"]])
    g_token, last_arrive = gather_behind(last, "last", [mid_token])
    gains = full["gains"]
    W = dict(mix_norm=mix_norm, mlp_norm=mlp_norm, ple_norm=ple_norm,
             mla_q_norm=jnp.pad(mla_q_norm, ((0, 0), (0, MLA_HP - MLA_QKD))),
             mla_k_norm=jnp.pad(mla_k_norm, ((0, 0), (0, MLA_HP - MLA_QKD))),
             ret_w_in=full["ret_w_in"], ret_w_out=full["ret_w_out"].reshape(-1, D_MODEL),
             ret_gn=gains[:, 0, :RET_HEADS * 128].reshape(n, RET_HEADS, 128).transpose(1, 0, 2).reshape(RET_HEADS, RET_DV),
             mla_q_a_norm=gains[:, 1, :MLA_Q_RANK // n].reshape(1, MLA_Q_RANK),
             mla_kv_a_norm=gains[:, 2, :MLA_KV_RANK // n].reshape(1, MLA_KV_RANK))
    x0, p16, target = x[0], p[:, 0].astype(BF16), loss_target[0]
    T = x0.shape[0]
    ret_tabs, mla_tabs = _ret_tables(T), _mla_tables(T)

    h1, hn, s_ret = _ret_layer_fwd(x0, W, ret_tabs, after=[g_token])
    mid_arrive([h1])
    h3, hn, s_tail0 = _tail_fwd(h1, hn, p16, W, 0, "l0", next_gain=W["mix_norm"][1:2])
    last_arrive([h3])
    h4, hn, s_mla = _mla_layer_fwd(h3, hn, W, mla_tabs)
    dy, loss, s_tail1 = _tail_fwd(h4, hn, p16, W, 1, "l1", target=target)

    dh4, dh4_16, g_t1, n_t1 = _tail_bwd(dy, s_tail1, p16, W, 1, "l1")
    dh3, _, g_mla, n_mla = _mla_layer_bwd(dh4, dh4_16, s_mla, W, mla_tabs)
    stages = {}

    def scatter_start(tag, begun):
        started = _split_start(f"scatter_{tag}_start", begun[3], _got_shapes(begun[3]), 3 * len(begun[3]), _scatter_copies)
        stages[tag] = (begun, started[:4])
        return [started[4]]

    def scatter_end(tag, after):
        begun, started = stages[tag]
        return _reduce_end(begun, _split_wait(f"scatter_{tag}_wait", *started, _scatter_copies, after=after)[1], chip, ci)

    swap_a, token = _swap_begin({**g_mla, **g_t1}, "a")
    dh1, dh1_16, g_t0, n_t0 = _tail_bwd(dh3, s_tail0, p16, W, 0, "l0", after=[token],
                                        hook=lambda t: scatter_start("a", _swap_end(swap_a, ci, "a", [t])))
    d_ret_w_out = _d_ret_w_out(dh1_16, s_ret)
    swap_b, token = _swap_begin({**g_t0, "ret_w_out": d_ret_w_out}, "b")
    dx, _, n_ret = _ret_layer_bwd(
        dh1, dh1_16, s_ret, W, ret_tabs, after=[token], d_w_out=d_ret_w_out,
        hook=lambda t: scatter_start("b", _swap_end(swap_b, ci, "b", [t])),
        on_grads=lambda g: scatter_start("c", _reduce_begin({"ret_w_in": g["ret_w_in"]}, ci, "c")))
    red = {**scatter_end("a", [dx]), **scatter_end("b", [dx]), **scatter_end("c", [dx])}
    red = dict(zip(red, _share_halves(list(red.values()))))
    gs = _small_grads(n_ret, n_t0, n_mla, n_t1)
    small_g = jnp.concatenate([
        gs["mix_norm"], gs["mlp_norm"], gs["ple_norm"], gs["ret_gn"].reshape(2, PACK_W), _pad_row(gs["mla_q_a_norm"]),
        _pad_row(gs["mla_kv_a_norm"]), _pad_row(gs["mla_q_norm"][:, :MLA_QKD]), _pad_row(gs["mla_k_norm"][:, :MLA_QKD]),
        _pad_row(loss[:, :1]), jnp.zeros((3, PACK_W), F32)], axis=0)
    tot = _allsum_small(small_g, "sum_small_grads")
    gn_all = tot[6:8].reshape(RET_HEADS, n, -1)
    g_small = dict(
        mix_norm=tot[0:2], mlp_norm=tot[2:4], ple_norm=tot[4:6],
        ret_gn=lax.dynamic_index_in_dim(gn_all, chip, axis=1, keepdims=False),
        mla_q_a_norm=lax.dynamic_index_in_dim(tot[8, :MLA_Q_RANK].reshape(n, -1), chip, axis=0, keepdims=True),
        mla_kv_a_norm=lax.dynamic_index_in_dim(tot[9, :MLA_KV_RANK].reshape(n, -1), chip, axis=0, keepdims=True),
        mla_q_norm=tot[10:11, :MLA_QKD], mla_k_norm=tot[11:12, :MLA_QKD])
    loss_out = tot[12, 0]

    outs = []
    for k in _ORDER:
        if k in _TWO_LAYER:
            res = None
            for i in (1, 0):
                res = _adamw(w[k], red[f"{k}_{i}"], m[k], v[k], f"adamw_{k}_{i}", layers=2, layer=i, into=res)
        elif k in red:
            res = _adamw(w[k], _shard_grad(k, red[k], w[k].shape), m[k], v[k], f"adamw_{k}")
        else:
            res = _adamw(w[k], g_small[k], m[k], v[k], f"adamw_{k}")
        outs.append(res)
    return (loss_out, dx[None], *[o[0] for o in outs], *[o[1] for o in outs], *[o[2] for o in outs],
            *[o[3] for o in outs])
```

```python
import jax
import jax.numpy as jnp
import numpy as np
from jax import lax
from jax.experimental import pallas as pl
from jax.experimental.pallas import tpu as pltpu

F32 = jnp.float32
BF16 = jnp.bfloat16

EPS = 1e-6
D_MODEL = 1024
CHUNK = 64
ROPE_THETA = 10000.0
RET_HEADS = 4
RET_DK = 256
RET_DV = 512
RET_GROUP = 1
RET_BLOCK = 256
RET_ROWS = 1024
MLA_HEADS = 8
MLA_ROPE = 64
MLA_QKD = 192
MLA_VD = 128
MLA_HP = 256
MLA_Q_RANK = 384
MLA_KV_RANK = 256
MLA_IN = 704
MLA_IN_PAD = 768
N_CHIPS = 4

ADAM_LR = 0.001
ADAM_B1 = 0.9
ADAM_B2 = 0.999
ADAM_EPS = 1e-08
ADAM_WD = 0.01
ADAM_STEP = 10

VMEM_LIMIT = 56 * 1024 * 1024
PACK_W = 1024
NEG = -1e30
LOG2E = 1.4426950408889634
FLASH_T = 512
FLASH_HEADS = 4
FLASH_BWD_HEADS = 2
MM_SUB_ROWS = 256
SUM_ROWS = 512
ADAM_ROWS = 512


def _cparams(sem=None):
    return pltpu.CompilerParams(dimension_semantics=sem, vmem_limit_bytes=VMEM_LIMIT)


def _pick(dim, pref):
    if dim <= pref:
        return dim
    t = pref
    while dim % t:
        t //= 2
    return t


def _mm(a, b, *, name, ta=False, tb=False, bblk=False, outs=None, extras=(), epilogue=None, dw=None,
        tm=1024, tn=512, after=()):
    if ta:
        K, M = a.shape
    else:
        M, K = a.shape
    if bblk and tb:
        nb, N, Kq = b.shape
        assert nb * Kq == K
    elif bblk:
        nb, Kb, Nq = b.shape
        N = nb * Nq
        assert Kb == K
    else:
        N = b.shape[0] if tb else b.shape[1]
    tn = _pick(Nq if (bblk and not tb) else N, tn)
    if dw is not None and dw[0] == "cols":
        tn = _pick(N // N_CHIPS, tn)
    tm = _pick(M // N_CHIPS if (dw is not None and dw[0] == "rows") else M, tm)
    grid = (M // tm, N // tn)

    a_spec = pl.BlockSpec((K, tm), lambda i, j: (0, i)) if ta else pl.BlockSpec((tm, K), lambda i, j: (i, 0))
    if bblk and tb:
        b_spec = pl.BlockSpec((nb, tn, Kq), lambda i, j: (0, j, 0))
    elif bblk:
        npb = Nq // tn
        b_spec = pl.BlockSpec((None, K, tn), lambda i, j: (j // npb, 0, j % npb))
    elif tb:
        b_spec = pl.BlockSpec((tn, K), lambda i, j: (j, 0))
    else:
        b_spec = pl.BlockSpec((K, tn), lambda i, j: (0, j))
    in_specs = [a_spec, b_spec] + [pl.BlockSpec((tm, tn), lambda i, j: (i, j)) for _ in extras]
    args = [a, b, *extras]
    aliases = {}
    if outs is None:
        outs = [F32]
    if dw is None:
        o_specs = [pl.BlockSpec((tm, tn), lambda i, j: (i, j)) for _ in outs]
        o_shapes = [jax.ShapeDtypeStruct((M, N), dt) for dt in outs]
    else:
        kind, layers, layer, into = dw
        if kind == "cols":
            per = (N // N_CHIPS) // tn
            o_specs = [pl.BlockSpec((None, None, tm, tn), lambda i, j: (j // per, layer, i, j % per))]
            o_shapes = [jax.ShapeDtypeStruct((N_CHIPS, layers, M, N // N_CHIPS), outs[0])]
        else:
            per = (M // N_CHIPS) // tm
            o_specs = [pl.BlockSpec((None, None, tm, tn), lambda i, j: (i // per, layer, i % per, j))]
            o_shapes = [jax.ShapeDtypeStruct((N_CHIPS, layers, M // N_CHIPS, N), outs[0])]
        if into is not None:
            aliases = {len(args): 0}
            in_specs.append(pl.BlockSpec(memory_space=pl.ANY))
            args.append(into)
    for t in after:
        in_specs.append(pl.BlockSpec(memory_space=pl.ANY))
        args.append(t)
    n_e, n_o = len(extras), len(outs)

    sub = _pick(tm, MM_SUB_ROWS)

    def body(a_ref, b_ref, *rest):
        e_refs, o_refs = rest[:n_e], rest[len(rest) - n_o:]
        for r0 in range(0, tm, sub):
            rows = slice(r0, r0 + sub)
            av = (a_ref[:, rows] if ta else a_ref[rows, :]).astype(BF16)
            if bblk and tb:
                acc = _dot_nt(av[:, :Kq], b_ref[0].astype(BF16))
                for s in range(1, nb):
                    acc = acc + _dot_nt(av[:, s * Kq:(s + 1) * Kq], b_ref[s].astype(BF16))
            elif ta:
                acc = _dot_tn(av, b_ref[...].astype(BF16))
            elif tb:
                acc = _dot_nt(av, b_ref[...].astype(BF16))
            else:
                acc = _dot(av, b_ref[...].astype(BF16))
            vals = (acc,) if epilogue is None else epilogue(acc, *[e[rows, :] for e in e_refs])
            for o, v in zip(o_refs, vals):
                o[rows, :] = v.astype(o.dtype)

    res = pl.pallas_call(
        body, name=name, grid=grid, in_specs=in_specs, out_specs=o_specs, out_shape=o_shapes,
        input_output_aliases=aliases, compiler_params=_cparams(("parallel", "arbitrary")),
    )(*args)
    return res[0] if n_o == 1 else res


def _mm_rows(a, b, *, name, epilogue, outs, tb=False, bblk=False, extras=(), fulls=(), accs=(), tm=512, after=()):
    M, K = a.shape
    tm = _pick(M, tm)
    sub = _pick(tm, MM_SUB_ROWS)
    nb = b.shape[0] if bblk else 1
    n_e, n_f, n_o, n_a = len(extras), len(fulls), len(outs), len(accs)
    n_in = 2 + n_e + n_f + len(after)

    def whole(t):
        return pl.BlockSpec(t.shape, lambda i, nd=t.ndim: (0,) * nd)

    in_specs = [pl.BlockSpec((tm, K), lambda i: (i, 0)), whole(b)]
    in_specs += [pl.BlockSpec((tm, e.shape[1]), lambda i: (i, 0)) for e in extras] + [whole(f) for f in fulls]
    in_specs += [pl.BlockSpec(memory_space=pl.ANY) for _ in after]
    out_specs = [pl.BlockSpec((tm, w), lambda i: (i, 0)) for w, _ in outs] + [pl.BlockSpec(s, lambda i: (0, 0)) for s, _ in accs]
    out_shape = [jax.ShapeDtypeStruct((M, w), dt) for w, dt in outs] + [jax.ShapeDtypeStruct(s, dt) for s, dt in accs]

    def body(a_ref, b_ref, *rest):
        e_refs, f_refs = rest[:n_e], rest[n_e:n_e + n_f]
        o_refs, acc_refs = rest[n_in - 2:n_in - 2 + n_o], rest[n_in - 2 + n_o:]
        fv = [f[...] for f in f_refs]
        totals = None
        for r0 in range(0, tm, sub):
            rows = slice(r0, r0 + sub)
            av = a_ref[rows, :].astype(BF16)
            if bblk and tb:
                kq = K // nb
                acc = _dot_nt(av[:, :kq], b_ref[0])
                for s in range(1, nb):
                    acc = acc + _dot_nt(av[:, s * kq:(s + 1) * kq], b_ref[s])
            elif bblk:
                acc = jnp.concatenate([_dot(av, b_ref[s]) for s in range(nb)], axis=-1)
            elif tb:
                acc = _dot_nt(av, b_ref[...])
            else:
                acc = _dot(av, b_ref[...])
            vals = epilogue(acc, *[e[rows, :] for e in e_refs], *fv)
            for o, v in zip(o_refs, vals[:n_o]):
                o[rows, :] = v.astype(o.dtype)
            part = vals[n_o:]
            totals = part if totals is None else [t + p for t, p in zip(totals, part)]
        first_step = pl.program_id(0) == 0
        for o, v in zip(acc_refs, totals):
            @pl.when(first_step)
            def _(o=o, v=v):
                o[...] = v.astype(o.dtype)

            @pl.when(jnp.logical_not(first_step))
            def _(o=o, v=v):
                o[...] += v.astype(o.dtype)

    return pl.pallas_call(
        body, name=name, grid=(M // tm,), in_specs=in_specs, out_specs=out_specs, out_shape=out_shape,
        compiler_params=_cparams(("arbitrary",)),
    )(a, b, *extras, *fulls, *after)


def _rows(fn, rows, fulls, outs, accs=(), *, name, tile=512, after=()):
    first = rows[0][0] if isinstance(rows[0], tuple) else rows[0]
    T = first.shape[0]
    tile = _pick(T, tile)
    in_specs, args = [], []
    for r in rows:
        if isinstance(r, tuple):
            arr, w, cb = r
            in_specs.append(pl.BlockSpec((tile, w), lambda i, cb=cb: (i, cb)))
        else:
            arr = r
            in_specs.append(pl.BlockSpec((tile, arr.shape[1]), lambda i: (i, 0)))
        args.append(arr)
    for f in fulls:
        in_specs.append(pl.BlockSpec(f.shape, lambda i, nd=f.ndim: (0,) * nd))
        args.append(f)
    outs = [o if len(o) == 4 else (*o, o[0], 0) for o in outs]
    out_specs = [pl.BlockSpec((tile, w), lambda i, cb=cb: (i, cb)) for w, _, _, cb in outs]
    out_specs += [pl.BlockSpec(s, lambda i: (0, 0)) for s, _ in accs]
    out_shape = [jax.ShapeDtypeStruct((T, tw), dt) for _, dt, tw, _ in outs]
    out_shape += [jax.ShapeDtypeStruct(s, dt) for s, dt in accs]
    n_in, n_out = len(args), len(outs)
    for t in after:
        in_specs.append(pl.BlockSpec(memory_space=pl.ANY))
        args.append(t)

    def body(*refs):
        vals = fn(*[r[...] for r in refs[:n_in]])
        o_refs = refs[len(args):]
        for o, v in zip(o_refs[:n_out], vals[:n_out]):
            o[...] = v.astype(o.dtype)
        first_step = pl.program_id(0) == 0
        for o, v in zip(o_refs[n_out:], vals[n_out:]):
            @pl.when(first_step)
            def _(o=o, v=v):
                o[...] = v.astype(o.dtype)

            @pl.when(jnp.logical_not(first_step))
            def _(o=o, v=v):
                o[...] += v.astype(o.dtype)

    res = pl.pallas_call(
        body, name=name, grid=(T // tile,), in_specs=in_specs, out_specs=out_specs, out_shape=out_shape,
        compiler_params=_cparams(("arbitrary",)),
    )(*args)
    return res


def _rowsum(v, mxu):
    if not mxu:
        return jnp.sum(v, axis=-1, keepdims=True)
    ones = jnp.ones((v.shape[1], v.shape[1]), BF16)
    hi = v.astype(BF16)
    lo = (v - hi.astype(F32)).astype(BF16)
    return _dot(hi, ones) + _dot(lo, ones)


def _rms(x, g, mxu=False):
    r = lax.rsqrt(_rowsum(x * x, mxu) / x.shape[-1] + EPS)
    return (x * r) * g


def _rms_bwd(x, dy, g, n=None, mxu=False):
    n = x.shape[-1] if n is None else n
    r = lax.rsqrt(_rowsum(x * x, mxu) / n + EPS)
    xh = x * r
    dxh = dy * g
    dx = r * (dxh - xh * (_rowsum(dxh * xh, mxu) / n))
    return dx, dy * xh


def _colsum(v):
    return jnp.sum(v, axis=0, keepdims=True)


def _sigmoid(x):
    return 0.5 * jnp.tanh(0.5 * x) + 0.5


def _widen(v, width):
    reps = width // v.shape[1]
    return v if reps == 1 else jnp.concatenate([v] * reps, axis=-1)


def _rope_angles(T, dim):
    inv = (1.0 / (np.float32(ROPE_THETA) ** (np.arange(0, dim, 2, dtype=np.float32) / np.float32(dim)))).astype(np.float32)
    return np.arange(T, dtype=np.float32)[:, None] * inv[None, :]


def _ret_tables(T):
    ang = _rope_angles(T, RET_DK)
    log_gamma = np.log(np.float32(1.0) - np.float32(2.0) ** (-5.0 - np.arange(RET_HEADS, dtype=np.float32)))
    idx = np.arange(RET_BLOCK, dtype=np.float32)
    chunk = np.arange(RET_BLOCK) // CHUNK
    dist = idx[:, None] - idx[None, :]
    seen = np.where(chunk[:, None] == chunk[None, :], np.abs(dist), np.where(chunk[:, None] > chunk[None, :], dist, np.inf))
    intra = np.exp(log_gamma[:, None, None] * seen[None].astype(np.float32))
    qd = np.exp(log_gamma[:, None] * (idx + 1.0))[:, :, None]
    kd = np.exp(log_gamma[:, None] * (RET_BLOCK - 1.0 - idx))[:, :, None]
    cd = np.exp(log_gamma * RET_BLOCK)[:, None, None]
    return tuple(jnp.asarray(t, F32) for t in (np.cos(ang), np.sin(ang), intra, qd, kd, cd))


def _rope_half(x, c, s):
    x1, x2 = x[:, :RET_DK // 2], x[:, RET_DK // 2:]
    return jnp.concatenate([x1 * c - x2 * s, x2 * c + x1 * s], axis=-1)


def _rope_half_bwd(d, c, s):
    d1, d2 = d[:, :RET_DK // 2], d[:, RET_DK // 2:]
    return jnp.concatenate([d1 * c + d2 * s, d2 * c - d1 * s], axis=-1)


def _dot(a, b):
    return lax.dot_general(a, b, (((1,), (0,)), ((), ())), preferred_element_type=F32)


def _dot_nt(a, b):
    return lax.dot_general(a, b, (((1,), (1,)), ((), ())), preferred_element_type=F32)


def _dot_tn(a, b):
    return lax.dot_general(a, b, (((0,), (0,)), ((), ())), preferred_element_type=F32)


def _ret_specs(T, tb, rev):
    nj = T // tb
    jj = (lambda j: nj - 1 - j) if rev else (lambda j: j)
    g = RET_GROUP
    kq = RET_HEADS // g
    vq = 2 * RET_HEADS * RET_DK // (g * RET_DV)
    return dict(
        q=pl.BlockSpec((tb, g * RET_DK), lambda h, j: (jj(j), h)),
        k=pl.BlockSpec((tb, g * RET_DK), lambda h, j: (jj(j), kq + h)),
        v=pl.BlockSpec((tb, g * RET_DV), lambda h, j: (jj(j), vq + h)),
        tab=pl.BlockSpec((tb, RET_DK // 2), lambda h, j: (jj(j), 0)),
        intra=pl.BlockSpec((g, RET_BLOCK, RET_BLOCK), lambda h, j: (h, 0, 0)),
        dec=pl.BlockSpec((g, RET_BLOCK, 1), lambda h, j: (h, 0, 0)),
        cd=pl.BlockSpec((g, 1, 1), lambda h, j: (h, 0, 0)),
        o=pl.BlockSpec((tb, g * RET_DV), lambda h, j: (jj(j), h)),
        s=pl.BlockSpec((g, tb // RET_BLOCK, RET_DK, RET_DV), lambda h, j: (h, jj(j), 0, 0)),
    )


def _ret_fwd(proj, tabs, name):
    T = proj.shape[0]
    cos, sin, intra, qd, kd, cd = tabs
    tb = _pick(T, RET_ROWS)
    cps = tb // RET_BLOCK
    sp = _ret_specs(T, tb, False)
    scale = RET_DK ** -0.5

    def body(q_ref, k_ref, v_ref, cos_ref, sin_ref, intra_ref, qd_ref, kd_ref, cd_ref, o_ref, s_ref, state):
        @pl.when(pl.program_id(1) == 0)
        def _():
            state[...] = jnp.zeros_like(state)

        for c in range(cps):
            rows = pl.ds(c * RET_BLOCK, RET_BLOCK)
            co, si = cos_ref[rows, :], sin_ref[rows, :]
            for h in range(RET_GROUP):
                hk, hv = slice(h * RET_DK, (h + 1) * RET_DK), slice(h * RET_DV, (h + 1) * RET_DV)
                q = _rope_half(q_ref[rows, hk].astype(F32), co, si)
                k = _rope_half(k_ref[rows, hk].astype(F32), co, si) * scale
                vb = v_ref[rows, hv].astype(BF16)
                st = state[h]
                sb = st.astype(BF16)
                s_ref[h, c] = sb
                sc = _dot_nt(q.astype(BF16), k.astype(BF16)) * intra_ref[h]
                inner = _dot(sc.astype(BF16), vb)
                cross = _dot((q * qd_ref[h]).astype(BF16), sb)
                o_ref[rows, hv] = inner + cross
                state[h] = st * cd_ref[h] + _dot_tn((k * kd_ref[h]).astype(BF16), vb)

    return pl.pallas_call(
        body, name=name, grid=(RET_HEADS // RET_GROUP, T // tb),
        in_specs=[sp["q"], sp["k"], sp["v"], sp["tab"], sp["tab"], sp["intra"], sp["dec"], sp["dec"], sp["cd"]],
        out_specs=[sp["o"], sp["s"]],
        out_shape=[jax.ShapeDtypeStruct((T, RET_HEADS * RET_DV), F32),
                   jax.ShapeDtypeStruct((RET_HEADS, T // RET_BLOCK, RET_DK, RET_DV), BF16)],
        scratch_shapes=[pltpu.VMEM((RET_GROUP, RET_DK, RET_DV), F32)],
        compiler_params=_cparams(("arbitrary", "arbitrary")),
    )(proj, proj, proj, cos, sin, intra, qd, kd, cd)


def _ret_bwd(proj, states, dout, dproj, tabs, name):
    assert RET_GROUP == 1
    T = proj.shape[0]
    cos, sin, intra, qd, kd, cd = tabs
    tb = _pick(T, RET_ROWS)
    cps = tb // RET_BLOCK
    nj = T // tb
    sp = _ret_specs(T, tb, True)
    scale = RET_DK ** -0.5
    k0, v0 = RET_HEADS * RET_DK, 2 * RET_HEADS * RET_DK

    def body(q_ref, k_ref, v_ref, cos_ref, sin_ref, intra_ref, qd_ref, kd_ref, cd_ref, s_ref, do_ref, _dproj_in,
             out_ref, dq_s, dk_s, dv_s, sems, dstate):
        head, j = pl.program_id(0), pl.program_id(1)
        step = head * nj + j
        slot = step % 2
        dq_ref, dk_ref, dv_ref = dq_s.at[slot], dk_s.at[slot], dv_s.at[slot]

        @pl.when(j == 0)
        def _():
            dstate[...] = jnp.zeros_like(dstate)

        for c in reversed(range(cps)):
            rows = pl.ds(c * RET_BLOCK, RET_BLOCK)
            co, si = cos_ref[rows, :], sin_ref[rows, :]
            for h in range(RET_GROUP):
                hk, hv = slice(h * RET_DK, (h + 1) * RET_DK), slice(h * RET_DV, (h + 1) * RET_DV)
                q = _rope_half(q_ref[rows, hk].astype(F32), co, si)
                k = _rope_half(k_ref[rows, hk].astype(F32), co, si) * scale
                qb, kb = q.astype(BF16), k.astype(BF16)
                vb = v_ref[rows, hv].astype(BF16)
                dob = do_ref[rows, hv].astype(BF16)
                sb = s_ref[h, c]
                ia = intra_ref[h]
                pb = (_dot_nt(qb, kb) * ia).astype(BF16)
                dsn = dstate[h]
                dsb = dsn.astype(BF16)
                kdk = (k * kd_ref[h]).astype(BF16)
                qdq = (q * qd_ref[h]).astype(BF16)
                dv = _dot_tn(pb, dob) + _dot(kdk, dsb)
                dpb = (_dot_nt(dob, vb) * ia).astype(BF16)
                dq = _dot(dpb, kb) + _dot_nt(dob, sb) * qd_ref[h]
                dk = _dot_tn(dpb, qb) + _dot_nt(vb, dsb) * kd_ref[h]
                dstate[h] = dsn * cd_ref[h] + _dot_tn(qdq, dob)
                dq_ref[rows, hk] = _rope_half_bwd(dq, co, si).astype(BF16)
                dk_ref[rows, hk] = _rope_half_bwd(dk * scale, co, si).astype(BF16)
                dv_ref[rows, hv] = dv.astype(BF16)

        def copies(sl):
            r = pl.ds(pl.multiple_of((nj - 1 - j) * tb, tb), tb)
            cols = lambda first, w: pl.ds(pl.multiple_of(first + head * w, 128), w)
            return [pltpu.make_async_copy(dq_s.at[sl], out_ref.at[r, cols(0, RET_DK)], sems.at[sl, 0]),
                    pltpu.make_async_copy(dk_s.at[sl], out_ref.at[r, cols(k0, RET_DK)], sems.at[sl, 1]),
                    pltpu.make_async_copy(dv_s.at[sl], out_ref.at[r, cols(v0, RET_DV)], sems.at[sl, 2])]

        @pl.when(step > 0)
        def _():
            for cp in copies(1 - slot):
                cp.wait()

        for cp in copies(slot):
            cp.start()

        @pl.when(step == RET_HEADS * nj - 1)
        def _():
            for cp in copies(slot):
                cp.wait()

    return pl.pallas_call(
        body, name=name, grid=(RET_HEADS, nj),
        in_specs=[sp["q"], sp["k"], sp["v"], sp["tab"], sp["tab"], sp["intra"], sp["dec"], sp["dec"], sp["cd"],
                  sp["s"], sp["o"], pl.BlockSpec(memory_space=pl.ANY)],
        out_specs=pl.BlockSpec(memory_space=pl.ANY), out_shape=jax.ShapeDtypeStruct(dproj.shape, dproj.dtype),
        input_output_aliases={11: 0},
        scratch_shapes=[pltpu.VMEM((2, tb, RET_DK), BF16), pltpu.VMEM((2, tb, RET_DK), BF16),
                        pltpu.VMEM((2, tb, RET_DV), BF16), pltpu.SemaphoreType.DMA((2, 3)),
                        pltpu.VMEM((RET_GROUP, RET_DK, RET_DV), F32)],
        compiler_params=_cparams(("arbitrary", "arbitrary")),
    )(proj, proj, proj, cos, sin, intra, qd, kd, cd, states, dout, dproj)


def _ret_gate(out, proj, gn, name):
    def fn(o, g, *gains):
        g = g.astype(F32)
        parts = [_rms(o[:, h * RET_DV:(h + 1) * RET_DV], gains[h]) for h in range(RET_HEADS)]
        return (g * _sigmoid(g) * jnp.concatenate(parts, axis=-1),)
    w = RET_HEADS * RET_DV
    return _rows(fn, [out, (proj, w, 2)], [gn[h:h + 1] for h in range(RET_HEADS)], [(w, BF16)], name=name)[0]


def _ret_gate_bwd(out, proj, gn, dy, name, after=()):
    def fn(o, g, d, *gains):
        g = g.astype(F32)
        sg = _sigmoid(g)
        silu = g * sg
        dsilu = sg * (1.0 + g * (1.0 - sg))
        dos, dgs = [], []
        row = lax.broadcasted_iota(jnp.int32, (RET_HEADS, RET_DV), 0)
        dgn = jnp.zeros((RET_HEADS, RET_DV), F32)
        for h in range(RET_HEADS):
            sl = slice(h * RET_DV, (h + 1) * RET_DV)
            oh = o[:, sl]
            dgs.append(d[:, sl] * _rms(oh, gains[h]) * dsilu[:, sl])
            dx, dg = _rms_bwd(oh, d[:, sl] * silu[:, sl], gains[h])
            dos.append(dx)
            dgn = dgn + jnp.where(row == h, _colsum(dg), 0.0)
        return jnp.concatenate(dos, axis=-1), jnp.concatenate(dgs, axis=-1), dgn
    w = RET_HEADS * RET_DV
    return _rows(fn, [out, (proj, w, 2), dy], [gn[h:h + 1] for h in range(RET_HEADS)],
                 [(w, BF16), (w, BF16, proj.shape[1], 2)], [((RET_HEADS, RET_DV), F32)], name=name, tile=256,
                 after=after)


def _mla_tables(T):
    ang = _rope_angles(T, MLA_ROPE)
    c, s = np.cos(ang), np.sin(ang)
    z32, z64 = np.zeros((T, 32), np.float32), np.zeros((T, 64), np.float32)
    cos_t = np.concatenate([c, c, z64], axis=1)
    sin_a = np.concatenate([-s, z32, z64], axis=1)
    sin_b = np.concatenate([z32, s, z64], axis=1)
    return tuple(jnp.asarray(t, F32) for t in (cos_t, sin_a, sin_b))


def _rope_blk(x, ct, sa, sb):
    return x * ct + pltpu.roll(x, 96, 1) * sa + pltpu.roll(x, 32, 1) * sb


def _rope_blk_bwd(d, ct, sa, sb):
    return d * ct + pltpu.roll(d * sa, 32, 1) + pltpu.roll(d * sb, 96, 1)


def _head_norm(x, gain):
    r = lax.rsqrt(_rowsum(x * x, True) / MLA_QKD + EPS)
    return (x * r) * gain


def _prep_heads(qv, kvv, kr, ct, sa, sb, gqv, gkv):
    qs, ks, vs = [], [], []
    for h in range(MLA_HEADS):
        b = h * MLA_HP
        y = _head_norm(qv[:, b:b + MLA_HP], gqv)
        qs += [y[:, :128], _rope_blk(y[:, 128:], ct, sa, sb)]
        y = _head_norm(jnp.concatenate([kvv[:, b:b + 128], kr], axis=-1), gkv)
        ks += [y[:, :128], _rope_blk(y[:, 128:], ct, sa, sb)]
        vs.append(kvv[:, b + 128:b + 256])
    return jnp.concatenate(qs, axis=-1), jnp.concatenate(ks, axis=-1), jnp.concatenate(vs, axis=-1)


def _mla_front(hn, W, tabs, name):
    wide = MLA_HEADS * MLA_HP
    gq = W["mla_q_norm"] * (MLA_QKD ** -0.5 * LOG2E)

    def epilogue(acc, ct, sa, sb, gqa, gkva, wuq, wukv, gqv, gkv):
        cqn = _rms(acc[:, :MLA_Q_RANK], gqa).astype(BF16)
        ckvn = _rms(acc[:, MLA_Q_RANK:MLA_Q_RANK + MLA_KV_RANK], gkva).astype(BF16)
        q = jnp.concatenate([_dot(cqn, wuq[s]) for s in range(N_CHIPS)], axis=-1).astype(BF16)
        kv = jnp.concatenate([_dot(ckvn, wukv[s]) for s in range(N_CHIPS)], axis=-1).astype(BF16)
        qf, kf, vf = _prep_heads(q.astype(F32), kv.astype(F32), acc[:, MLA_IN_PAD - 128:], ct, sa, sb, gqv, gkv)
        return acc, cqn, ckvn, q, kv, qf, kf, vf

    return _mm_rows(hn, W["mla_w_in"], extras=list(tabs),
                    fulls=[W["mla_q_a_norm"], W["mla_kv_a_norm"], W["mla_w_uq"], W["mla_w_ukv"], gq, W["mla_k_norm"]],
                    outs=[(MLA_IN_PAD, F32), (MLA_Q_RANK, BF16), (MLA_KV_RANK, BF16), (wide, BF16), (wide, BF16),
                          (wide, BF16), (wide, BF16), (MLA_HEADS * MLA_VD, BF16)],
                    epilogue=epilogue, name=name, tm=256)


def _prep_heads_bwd(qv, kvv, kr, ct, sa, sb, dqv, dkv, dvv, gqv, gkv):
    dqs, dkvs = [], []
    dkr = jnp.zeros_like(kr)
    dgq = jnp.zeros((1, MLA_HP), F32)
    dgk = jnp.zeros((1, MLA_HP), F32)
    for h in range(MLA_HEADS):
        b = h * MLA_HP
        dy = jnp.concatenate([dqv[:, b:b + 128], _rope_blk_bwd(dqv[:, b + 128:b + 256], ct, sa, sb)], axis=-1)
        dx, dg = _rms_bwd(qv[:, b:b + MLA_HP], dy, gqv, MLA_QKD, mxu=True)
        dqs.append(dx)
        dgq = dgq + _colsum(dg)
        dy = jnp.concatenate([dkv[:, b:b + 128], _rope_blk_bwd(dkv[:, b + 128:b + 256], ct, sa, sb)], axis=-1)
        dx, dg = _rms_bwd(jnp.concatenate([kvv[:, b:b + 128], kr], axis=-1), dy, gkv, MLA_QKD, mxu=True)
        dkvs += [dx[:, :128], dvv[:, h * MLA_VD:(h + 1) * MLA_VD].astype(F32)]
        dkr = dkr + dx[:, 128:]
        dgk = dgk + _colsum(dg)
    return jnp.concatenate(dqs, axis=-1), jnp.concatenate(dkvs, axis=-1), dkr, dgq, dgk


def _mla_back(q, kv, proj, h0, dh1, dqf, dkf, dvf, W, tabs, name):
    def fn(qv, kvv, pv, hv, dr, ct, sa, sb, dqv, dkv, dvv, gqv, gkv, gqa, gkva, wuq, wukv, w_in, g_mix):
        qv, kvv, dqv, dkv = (t.astype(F32) for t in (qv, kvv, dqv, dkv))
        dq, dkvx, dkr, dgq, dgk = _prep_heads_bwd(qv, kvv, pv[:, MLA_IN_PAD - 128:], ct, sa, sb, dqv, dkv, dvv, gqv, gkv)
        dq, dkvx = dq.astype(BF16), dkvx.astype(BF16)
        nq = wuq.shape[2]
        dcq = sum(_dot_nt(dq[:, s * nq:(s + 1) * nq], wuq[s]) for s in range(N_CHIPS))
        dckv = sum(_dot_nt(dkvx[:, s * nq:(s + 1) * nq], wukv[s]) for s in range(N_CHIPS))
        dxq, dgqa = _rms_bwd(pv[:, :MLA_Q_RANK], dcq, gqa)
        dxkv, dgkva = _rms_bwd(pv[:, MLA_Q_RANK:MLA_Q_RANK + MLA_KV_RANK], dckv, gkva)
        dproj = jnp.concatenate([dxq, dxkv, dkr], axis=-1).astype(BF16)
        dx, dgm = _rms_bwd(hv, _dot_nt(dproj, w_in), g_mix)
        return (dq, dkvx, dproj, dr + dx, dr + dx, dgq, dgk, _colsum(dgqa), _colsum(dgkva), _colsum(dgm))

    wide = MLA_HEADS * MLA_HP
    return _rows(fn, [q, kv, proj, h0, dh1, *tabs, dqf, dkf, dvf],
                 [W["mla_q_norm"], W["mla_k_norm"], W["mla_q_a_norm"], W["mla_kv_a_norm"], W["mla_w_uq"], W["mla_w_ukv"],
                  W["mla_w_in"], W["mix_norm"][1:2]],
                 [(wide, BF16), (wide, BF16), (MLA_IN_PAD, BF16), ROW_F32, ROW_BF16],
                 [((1, MLA_HP), F32), ((1, MLA_HP), F32), ((1, MLA_Q_RANK), F32), ((1, MLA_KV_RANK), F32),
                  ((1, D_MODEL), F32)], name=name, tile=256)


def _chunk_mask(qi, ki, tq, tk):
    shift = CHUNK.bit_length() - 1
    rq = lax.shift_right_arithmetic(qi * tq + lax.broadcasted_iota(jnp.int32, (tq, tk), 0), shift)
    ck = lax.shift_right_arithmetic(ki * tk + lax.broadcasted_iota(jnp.int32, (tq, tk), 1), shift)
    return ck <= rq


def _flash_fwd(qf, kf, vf, name):
    T = qf.shape[0]
    t = _pick(T, FLASH_T)
    n = T // t
    g = FLASH_HEADS

    def body(q_ref, k_ref, v_ref, o_ref, lse_ref, m_s, l_s, acc):
        qi = pl.program_id(1)
        m_s[...] = jnp.full_like(m_s, NEG)
        l_s[...] = jnp.zeros_like(l_s)
        acc[...] = jnp.zeros_like(acc)

        def step(kb, masked):
            rows = pl.ds(pl.multiple_of(kb * t, t), t)
            for h in range(g):
                hq, hv = slice(h * MLA_HP, (h + 1) * MLA_HP), slice(h * MLA_VD, (h + 1) * MLA_VD)
                s = _dot_nt(q_ref[:, hq], k_ref[rows, hq])
                if masked:
                    s = jnp.where(_chunk_mask(0, 0, t, t), s, NEG)
                m_prev = m_s[:, hv]
                m_new = jnp.maximum(m_prev, jnp.max(s, axis=-1, keepdims=True))
                alpha = jnp.exp2(m_prev - m_new)
                p = jnp.exp2(s - _widen(m_new, t))
                l_s[:, hv] = alpha * l_s[:, hv] + sum(p[:, i * 128:(i + 1) * 128] for i in range(t // 128))
                acc[:, hv] = acc[:, hv] * alpha + _dot(p.astype(BF16), v_ref[rows, hv])
                m_s[:, hv] = m_new

        @pl.loop(0, qi)
        def _(kb):
            step(kb, False)

        step(qi, True)
        for h in range(g):
            hv = slice(h * MLA_VD, (h + 1) * MLA_VD)
            l = jnp.sum(l_s[:, hv], axis=-1, keepdims=True)
            o_ref[:, hv] = acc[:, hv] / l
            lse_ref[:, hv] = m_s[:, hv] + jnp.log2(l)

    qmap = lambda h, i: (i, h)
    kmap = lambda h, i: (0, h)
    vec = pltpu.VMEM((t, g * MLA_VD), F32)
    return pl.pallas_call(
        body, name=name, grid=(MLA_HEADS // g, n),
        in_specs=[pl.BlockSpec((t, g * MLA_HP), qmap), pl.BlockSpec((T, g * MLA_HP), kmap),
                  pl.BlockSpec((T, g * MLA_VD), kmap)],
        out_specs=[pl.BlockSpec((t, g * MLA_VD), qmap), pl.BlockSpec((t, g * MLA_VD), qmap)],
        out_shape=[jax.ShapeDtypeStruct((T, MLA_HEADS * MLA_VD), F32),
                   jax.ShapeDtypeStruct((T, MLA_HEADS * MLA_VD), F32)],
        scratch_shapes=[vec, vec, vec],
        compiler_params=_cparams(("parallel", "arbitrary")),
    )(qf, kf, vf)


def _flash_bwd(qf, kf, vf, do16, lse, delta, name):
    T = qf.shape[0]
    t = _pick(T, FLASH_T)
    n = T // t
    g = FLASH_BWD_HEADS
    scale = MLA_QKD ** -0.5

    def body(q_ref, k_ref, v_ref, do_ref, lse_ref, dl_ref, dq_out, dk_out, dv_out, dq_ref, dk_ref, dv_ref):
        kb = pl.program_id(1)

        @pl.when(kb == 0)
        def _():
            dq_ref[...] = jnp.zeros_like(dq_ref)

        dk_ref[...] = jnp.zeros_like(dk_ref)
        dv_ref[...] = jnp.zeros_like(dv_ref)

        def step(qb, masked):
            rows = pl.ds(pl.multiple_of(qb * t, t), t)
            for h in range(g):
                hq, hv = slice(h * MLA_HP, (h + 1) * MLA_HP), slice(h * MLA_VD, (h + 1) * MLA_VD)
                q, dob, k, v = q_ref[rows, hq], do_ref[rows, hv], k_ref[:, hq], v_ref[:, hv]
                s = _dot_nt(q, k)
                if masked:
                    s = jnp.where(_chunk_mask(0, 0, t, t), s, NEG)
                p = jnp.exp2(s - _widen(lse_ref[rows, hv], t))
                ds = (p * (_dot_nt(dob, v) - _widen(dl_ref[rows, hv], t))).astype(BF16)
                dv_ref[:, hv] += _dot_tn(p.astype(BF16), dob)
                dk_ref[:, hq] += _dot_tn(ds, q)
                dq_ref[rows, hq] += _dot(ds, k)

        step(kb, True)

        @pl.loop(kb + 1, n)
        def _(qb):
            step(qb, False)

        dk_out[...] = (dk_ref[...] * (1.0 / LOG2E)).astype(BF16)
        dv_out[...] = dv_ref[...].astype(BF16)

        @pl.when(kb == n - 1)
        def _():
            dq_out[...] = (dq_ref[...] * scale).astype(BF16)

    qmap = lambda h, j: (0, h)
    kmap = lambda h, j: (j, h)
    wq, wv = g * MLA_HP, g * MLA_VD
    return pl.pallas_call(
        body, name=name, grid=(MLA_HEADS // g, n),
        in_specs=[pl.BlockSpec((T, wq), qmap), pl.BlockSpec((t, wq), kmap), pl.BlockSpec((t, wv), kmap),
                  pl.BlockSpec((T, wv), qmap), pl.BlockSpec((T, wv), qmap), pl.BlockSpec((T, wv), qmap)],
        out_specs=[pl.BlockSpec((T, wq), qmap), pl.BlockSpec((t, wq), kmap), pl.BlockSpec((t, wv), kmap)],
        out_shape=[jax.ShapeDtypeStruct((T, MLA_HEADS * MLA_HP), BF16),
                   jax.ShapeDtypeStruct((T, MLA_HEADS * MLA_HP), BF16),
                   jax.ShapeDtypeStruct((T, MLA_HEADS * MLA_VD), BF16)],
        scratch_shapes=[pltpu.VMEM((T, wq), F32), pltpu.VMEM((t, wq), F32), pltpu.VMEM((t, wv), F32)],
        compiler_params=_cparams(("arbitrary", "arbitrary")),
    )(qf, kf, vf, do16, lse, delta)


MESH = pl.DeviceIdType.MESH
ANY = pl.BlockSpec(memory_space=pl.ANY)
_CHIP_FLIPS = ((1, 0), (0, 1), (1, 1))


def _place():
    return lax.axis_index("x"), lax.axis_index("y"), lax.axis_index("c")


def _other_chip(x, y, k):
    fx, fy = _CHIP_FLIPS[k]
    return ((1 - x) if fx else x), ((1 - y) if fy else y)


def _remote(src, dst, send_sems, recv_sems, k, to):
    return pltpu.make_async_remote_copy(src_ref=src, dst_ref=dst, send_sem=send_sems.at[k], recv_sem=recv_sems.at[k],
                                        device_id=to, device_id_type=MESH)


def _index(*vals):
    return jnp.stack(vals).astype(jnp.int32)


def _half(c, rows):
    return pl.ds(pl.multiple_of(c * rows, 16), rows)


def _gather_weights(parts, name, landed=None):
    n_w = len(parts)
    n_in = n_w if landed is None else 2 * n_w

    def body(*refs):
        ins, outs = refs[:n_w], refs[n_in:n_in + n_w]
        send_sems, recv_sems, local_sems = refs[n_in + n_w:]
        x, y, c = _place()
        j = 2 * x + y
        sibling = (x, y, 1 - c)
        chips = [_other_chip(x, y, k) for k in range(3)]
        pending = []
        for w in range(n_w):
            own = pltpu.make_async_copy(ins[w], outs[w].at[j], local_sems.at[w])
            own.start()
            pending.append(own)
        sent = []
        for w in range(n_w):
            if landed is not None:
                break
            r = _half(c, parts[w].shape[0] // 2)
            for k, (px, py) in enumerate(chips):
                cp = _remote(ins[w].at[r], outs[w].at[j, r], send_sems, recv_sems, 6 * w + k, (px, py, c))
                cp.start()
                sent.append(cp)
        for w in range(n_w):
            r = _half(c, parts[w].shape[0] // 2)
            for k, (px, py) in enumerate(chips):
                blk = outs[w].at[2 * px + py, r]
                if landed is None:
                    _remote(blk, blk, send_sems, recv_sems, 6 * w + k, (px, py, c)).wait_recv()
                cp = _remote(blk, blk, send_sems, recv_sems, 6 * w + 3 + k, sibling)
                cp.start()
                sent.append(cp)
        for w in range(n_w):
            r = _half(1 - c, parts[w].shape[0] // 2)
            for k, (px, py) in enumerate(chips):
                blk = outs[w].at[2 * px + py, r]
                _remote(blk, blk, send_sems, recv_sems, 6 * w + 3 + k, sibling).wait_recv()
        for cp in sent:
            cp.wait_send()
        for cp in pending:
            cp.wait()

    return pl.pallas_call(
        body, name=name, in_specs=[pl.BlockSpec(memory_space=pltpu.VMEM)] * n_w + [ANY] * (n_in - n_w),
        out_specs=[ANY] * n_w,
        out_shape=[jax.ShapeDtypeStruct((N_CHIPS, *p.shape), p.dtype) for p in parts],
        input_output_aliases={} if landed is None else {n_w + w: w for w in range(n_w)},
        scratch_shapes=[pltpu.SemaphoreType.DMA((6 * n_w,)), pltpu.SemaphoreType.DMA((6 * n_w,)),
                        pltpu.SemaphoreType.DMA((n_w,))],
        compiler_params=pltpu.CompilerParams(vmem_limit_bytes=VMEM_LIMIT),
    )(*parts, *(landed or []))


def _swap_halves(gs, name):
    n_w = len(gs)

    def body(*refs):
        g_refs, recv_refs = refs[:n_w], refs[n_w:2 * n_w]
        send_sems, recv_sems = refs[2 * n_w:]
        x, y, c = _place()
        sent = []
        for w in range(n_w):
            for jj in range(N_CHIPS):
                cp = _remote(g_refs[w].at[jj, 1 - c], recv_refs[w].at[jj], send_sems, recv_sems, N_CHIPS * w + jj,
                             (x, y, 1 - c))
                cp.start()
                sent.append(cp)
        for cp in sent:
            cp.wait()

    return pl.pallas_call(
        body, name=name, in_specs=[ANY] * n_w, out_specs=[ANY] * n_w,
        out_shape=[jax.ShapeDtypeStruct((N_CHIPS, *g.shape[2:]), g.dtype) for g in gs],
        scratch_shapes=[pltpu.SemaphoreType.DMA((N_CHIPS * n_w,)), pltpu.SemaphoreType.DMA((N_CHIPS * n_w,))],
    )(*gs)


def _pair_sum(g, recv, core, name):
    _, H, C = recv.shape
    tile = _pick(H, SUM_ROWS)

    def body(c_ref, own_ref, recv_ref, out_ref):
        out_ref[...] = (own_ref[...].astype(F32) + recv_ref[...].astype(F32)).astype(BF16)

    blk = pl.BlockSpec((None, tile, C), lambda jj, i, c: (jj, i, 0))
    return pl.pallas_call(
        body, name=name,
        grid_spec=pltpu.PrefetchScalarGridSpec(
            num_scalar_prefetch=1, grid=(N_CHIPS, H // tile),
            in_specs=[pl.BlockSpec((None, None, tile, C), lambda jj, i, c: (jj, c[0], i, 0)), blk],
            out_specs=blk),
        out_shape=jax.ShapeDtypeStruct((N_CHIPS, H, C), BF16),
        compiler_params=_cparams(("arbitrary", "arbitrary")),
    )(_index(core), g, recv)


def _chip_sum(g, recv, got, chip, core, name):
    _, H, C = recv.shape
    tile = _pick(H, SUM_ROWS)

    def body(s_ref, own_ref, recv_ref, g0_ref, g1_ref, g2_ref, out_ref):
        pair = own_ref[...].astype(F32) + recv_ref[...].astype(F32)
        out_ref[...] = ((pair + g0_ref[...].astype(F32)) + g1_ref[...].astype(F32)) + g2_ref[...].astype(F32)

    def got_spec(k):
        return pl.BlockSpec((None, tile, C), lambda i, s, k=k: (k, i, 0))

    return pl.pallas_call(
        body, name=name,
        grid_spec=pltpu.PrefetchScalarGridSpec(
            num_scalar_prefetch=1, grid=(H // tile,),
            in_specs=[pl.BlockSpec((None, None, tile, C), lambda i, s: (s[0], s[1], i, 0)),
                      pl.BlockSpec((None, tile, C), lambda i, s: (s[0], i, 0)), got_spec(0), got_spec(1), got_spec(2)],
            out_specs=pl.BlockSpec((None, tile, C), lambda i, s: (s[1], i, 0))),
        out_shape=jax.ShapeDtypeStruct((2, H, C), F32),
        compiler_params=_cparams(("arbitrary",)),
    )(_index(chip, core), g, recv, got, got, got)


def _share_halves(reds):
    n_w = len(reds)

    def body(*refs):
        out_refs = refs[n_w:2 * n_w]
        send_sems, recv_sems = refs[2 * n_w:]
        x, y, c = _place()
        sent = []
        for w in range(n_w):
            blk = out_refs[w].at[c]
            cp = _remote(blk, blk, send_sems, recv_sems, w, (x, y, 1 - c))
            cp.start()
            sent.append(cp)
        for cp in sent:
            cp.wait()

    return pl.pallas_call(
        body, name="grad_share_halves", in_specs=[ANY] * n_w, out_specs=[ANY] * n_w,
        out_shape=[jax.ShapeDtypeStruct(r.shape, r.dtype) for r in reds],
        input_output_aliases={w: w for w in range(n_w)},
        scratch_shapes=[pltpu.SemaphoreType.DMA((n_w,)), pltpu.SemaphoreType.DMA((n_w,))],
    )(*reds)


def _allsum_small(v, name):
    R, W = v.shape
    n_dev = 8
    vm = pl.BlockSpec(memory_space=pltpu.VMEM)

    def body(v_ref, out_ref, buf, send_sems, recv_sems):
        x, y, c = _place()
        me = 4 * x + 2 * y + c
        buf[me] = v_ref[...]
        sent = []
        for k in range(1, n_dev):
            peer = ((1 - x) if k & 4 else x, (1 - y) if k & 2 else y, (1 - c) if k & 1 else c)
            cp = _remote(v_ref, buf.at[me], send_sems, recv_sems, k - 1, peer)
            cp.start()
            sent.append(cp)
        for cp in sent:
            cp.wait_recv()
        for cp in sent:
            cp.wait_send()
        acc = buf[0]
        for q in range(1, n_dev):
            acc = acc + buf[q]
        out_ref[...] = acc

    return pl.pallas_call(
        body, name=name, in_specs=[vm], out_specs=vm, out_shape=jax.ShapeDtypeStruct((R, W), v.dtype),
        scratch_shapes=[pltpu.VMEM((n_dev, R, W), v.dtype), pltpu.SemaphoreType.DMA((n_dev - 1,)),
                        pltpu.SemaphoreType.DMA((n_dev - 1,))],
    )(v)


HBM = pl.BlockSpec(memory_space=pltpu.HBM)
SEM = pl.BlockSpec(memory_space=pltpu.SEMAPHORE)
_DATAFLOW = pltpu.SideEffectType.DATAFLOW_SIDE_EFFECTING


def _split_start(name, srcs, land_shapes, n_copies, copies, after=()):
    ns, nl = len(srcs), len(land_shapes)
    lands = [lax.empty(s.shape, s.dtype) for s in land_shapes]

    def body(*refs):
        outs = refs[ns + nl + len(after):]
        for cp in copies(refs[:ns], refs[ns:ns + nl], outs[0], outs[1]):
            cp.start()
        outs[-1][...] = jnp.zeros_like(outs[-1])

    sems = pltpu.SemaphoreType.DMA((n_copies,))
    res = pl.pallas_call(
        body, name=name, in_specs=[HBM] * (ns + nl) + [ANY] * len(after),
        out_specs=(SEM, SEM, *[HBM] * (ns + nl), pl.BlockSpec(memory_space=pltpu.VMEM)),
        out_shape=(sems, sems, *[pltpu.HBM(a.shape, a.dtype) for a in srcs],
                   *[pltpu.HBM(s.shape, s.dtype) for s in land_shapes], jax.ShapeDtypeStruct((8, 128), F32)),
        input_output_aliases={i: 2 + i for i in range(ns + nl)},
        compiler_params=pltpu.CompilerParams(has_side_effects=_DATAFLOW),
    )(*[pltpu.with_memory_space_constraint(a, pltpu.HBM) for a in [*srcs, *lands]], *after)
    return res[0], res[1], list(res[2:2 + ns]), list(res[2 + ns:2 + ns + nl]), res[-1]


def _split_wait(name, send_sems, recv_sems, srcs, lands, copies, after=()):
    ns, nl = len(srcs), len(lands)

    def body(*refs):
        for cp in copies(refs[:ns], refs[ns:ns + nl], refs[ns + nl], refs[ns + nl + 1]):
            cp.wait_send()
            cp.wait_recv()

    res = pl.pallas_call(
        body, name=name, in_specs=[HBM] * (ns + nl) + [SEM, SEM] + [ANY] * len(after), out_specs=[HBM] * (ns + nl),
        out_shape=[pltpu.HBM(a.shape, a.dtype) for a in [*srcs, *lands]],
        input_output_aliases={i: i for i in range(ns + nl)},
        compiler_params=pltpu.CompilerParams(has_side_effects=_DATAFLOW),
    )(*srcs, *lands, send_sems, recv_sems, *after)
    return list(res[:ns]), list(res[ns:])


def _gather_copies(rows):
    def copies(src_refs, land_refs, send_sems, recv_sems):
        x, y, c = _place()
        j = 2 * x + y
        out = []
        for w in range(len(src_refs)):
            r = _half(c, rows[w] // 2)
            for k in range(3):
                px, py = _other_chip(x, y, k)
                out.append(_remote(src_refs[w].at[r], land_refs[w].at[j, r], send_sems, recv_sems, 3 * w + k, (px, py, c)))
        return out
    return copies


def _scatter_copies(src_refs, land_refs, send_sems, recv_sems):
    x, y, c = _place()
    j = 2 * x + y
    out = []
    for w in range(len(src_refs)):
        for k in range(3):
            px, py = _other_chip(x, y, k)
            pj = 2 * px + py
            out.append(_remote(src_refs[w].at[pj], land_refs[w].at[(j - pj + 4) % 4 - 1], send_sems, recv_sems, 3 * w + k,
                               (px, py, c)))
    return out


def _halves(grads):
    names = list(grads)
    return names, [grads[k].reshape(N_CHIPS, 2, -1, grads[k].shape[-1]) for k in names]


def _reduce_begin(grads, core, tag):
    names, gs = _halves(grads)
    recvs = _swap_halves(gs, f"grad_swap_halves_{tag}")
    sums = [_pair_sum(g, r, core, f"pair_sum_{k}") for k, g, r in zip(names, gs, recvs)]
    return names, gs, recvs, sums


def _swap_copies(src_refs, land_refs, send_sems, recv_sems):
    x, y, c = _place()
    return [_remote(src_refs[w].at[jj, 1 - c], land_refs[w].at[jj], send_sems, recv_sems, N_CHIPS * w + jj, (x, y, 1 - c))
            for w in range(len(src_refs)) for jj in range(N_CHIPS)]


def _swap_begin(grads, tag):
    names, gs = _halves(grads)
    started = _split_start(f"swap_{tag}_start", gs, [jax.ShapeDtypeStruct((N_CHIPS, *g.shape[2:]), g.dtype) for g in gs],
                           N_CHIPS * len(gs), _swap_copies)
    return (names, started[:4]), started[4]


def _swap_end(begun, core, tag, after):
    names, started = begun
    gs, recvs = _split_wait(f"swap_{tag}_wait", *started, _swap_copies, after=after)
    sums = [_pair_sum(g, r, core, f"pair_sum_{k}") for k, g, r in zip(names, gs, recvs)]
    return names, gs, recvs, sums


def _reduce_end(begun, gots, chip, core):
    names, gs, recvs, _ = begun
    return {k: _chip_sum(g, r, t, chip, core, f"chip_sum_{k}") for k, g, r, t in zip(names, gs, recvs, gots)}


def _got_shapes(sums):
    return [jax.ShapeDtypeStruct((3, *a.shape[1:]), a.dtype) for a in sums]


def _adamw(w, g, m, v, name, layers=1, layer=0, into=None):
    shape = w.shape
    cols = shape[-1]
    w3, m3, v3 = (t.reshape(layers, -1, cols) for t in (w, m, v))
    rows = w3.shape[1]
    tile = _pick(rows, ADAM_ROWS if cols <= 1024 else ADAM_ROWS // 2) if rows % 8 == 0 else rows
    n_in = 4 + (0 if into is None else 4)
    stack_g = layers > 1

    def body(*refs):
        wv, gv, mv, vv = (r[...] for r in refs[:4])
        d_ref, m_ref, v_ref = refs[len(refs) - 3:]
        m2 = ADAM_B1 * mv + (1.0 - ADAM_B1) * gv
        v2 = ADAM_B2 * vv + (1.0 - ADAM_B2) * jnp.square(gv)
        m_hat = m2 / (1.0 - ADAM_B1 ** ADAM_STEP)
        v_hat = v2 / (1.0 - ADAM_B2 ** ADAM_STEP)
        if stack_g:
            refs[n_in][...] = gv
        d_ref[...] = -ADAM_LR * (m_hat / (jnp.sqrt(v_hat) + ADAM_EPS) + ADAM_WD * wv)
        m_ref[...] = m2
        v_ref[...] = v2

    n_out = 4 if stack_g else 3
    lay = pl.BlockSpec((None, tile, cols), lambda i: (layer, i, 0))
    out = jax.ShapeDtypeStruct((layers, rows, cols), F32)
    res = pl.pallas_call(
        body, name=name, grid=(rows // tile,),
        in_specs=[lay, pl.BlockSpec((tile, cols), lambda i: (i, 0)), lay, lay] + [ANY] * (n_in - 4),
        out_specs=[lay] * n_out, out_shape=[out] * n_out,
        input_output_aliases={} if into is None else {4 + k: k for k in range(4)},
        compiler_params=_cparams(("arbitrary",)),
    )(w3, g.reshape(rows, cols), m3, v3, *([] if into is None else [t.reshape(layers, rows, cols) for t in into]))
    res = tuple(t.reshape(shape) for t in res)
    return res if stack_g else (g.reshape(shape), *res)


ROW_F32, ROW_BF16 = (D_MODEL, F32), (D_MODEL, BF16)


def _res_norm(acc, h, gain):
    hh = h + acc
    return hh, _rms(hh, gain)


def _dx_norm_bwd(d, w, h, dres, gain, name, **kw):
    def epilogue(acc, hv, dr, g):
        dx, dg = _rms_bwd(hv, acc, g)
        return dr + dx, dr + dx, _colsum(dg)
    return _mm_rows(d, w, tb=True, extras=[h, dres], fulls=[gain], outs=[ROW_F32, ROW_BF16], accs=[((1, D_MODEL), F32)],
                    epilogue=epilogue, name=name, **kw)


def _tail_fwd(h1, hn2, p16, W, i, tag, next_gain=None, target=None):
    a = _mm(hn2, W["mlp_w1"][i], bblk=True, outs=[BF16], name=f"{tag}_mlp_w1", tm=2048, tn=1024,
            epilogue=lambda acc: (jnp.square(jnp.maximum(acc, 0.0)),))
    h2, hn3 = _mm_rows(a, W["mlp_w2"][i], extras=[h1], fulls=[W["ple_norm"][i:i + 1]], outs=[ROW_F32, ROW_BF16],
                       epilogue=_res_norm, name=f"{tag}_mlp_w2")
    def embed(acc, pv, h, wp):
        gate = _sigmoid(acc)
        ppv = jnp.concatenate([_dot(pv, wp[s]) for s in range(N_CHIPS)], axis=-1)
        return gate, ppv, h + gate * ppv

    if target is None:
        def gated(acc, pv, h, wp, gain):
            gate, ppv, hh = embed(acc, pv, h, wp)
            return hh, ppv, gate, _rms(hh, gain)
        h3, pp, gate, hn = _mm_rows(hn3, W["ple_gate_w"][i], extras=[p16[i], h2], fulls=[W["ple_proj_w"][i], next_gain],
                                    outs=[ROW_F32, ROW_BF16, ROW_BF16, ROW_BF16], epilogue=gated, name=f"{tag}_ple")
        return h3, hn, (h1, hn2, a, h2, hn3, gate, pp)

    def gated_loss(acc, pv, h, t, wp):
        gate, ppv, hh = embed(acc, pv, h, wp)
        e = hh - t
        return ppv, gate, e * (1.0 / D_MODEL), jnp.full((1, 128), 0.5 / D_MODEL, F32) * jnp.sum(e * e)
    pp, gate, dy, loss = _mm_rows(hn3, W["ple_gate_w"][i], extras=[p16[i], h2, target], fulls=[W["ple_proj_w"][i]],
                                  outs=[ROW_BF16, ROW_BF16, ROW_F32], accs=[((1, 128), F32)], epilogue=gated_loss,
                                  name=f"{tag}_ple")
    return dy, loss, (h1, hn2, a, h2, hn3, gate, pp)


def _tail_bwd(dh3, saved, p16, W, i, tag, after=(), hook=None):
    h1, hn2, a, h2, hn3, gate, pp = saved

    def embed_bwd(d, g, ppv, hv, wg, gain):
        g, ppv = g.astype(F32), ppv.astype(F32)
        dppv, dglv = (d * g).astype(BF16), (d * ppv * g * (1.0 - g)).astype(BF16)
        dx, dg = _rms_bwd(hv, _dot_nt(dglv, wg), gain)
        return dppv, dglv, d + dx, d + dx, _colsum(dg)

    def dw(kind, name):
        return (kind, 1, 0, None)

    dpp, dgl, dh2, dh2_16, d_ple_norm = _rows(
        embed_bwd, [dh3, gate, pp, h2], [W["ple_gate_w"][i], W["ple_norm"][i:i + 1]],
        [ROW_BF16, ROW_BF16, ROW_F32, ROW_BF16], [((1, D_MODEL), F32)], name=f"{tag}_ple_bwd", after=after)
    later = () if hook is None else hook(dh2_16)
    d_proj = _mm(p16[i], dpp, ta=True, outs=[BF16], dw=dw("cols", "ple_proj_w"), name=f"{tag}_d_ple_proj", after=later)
    d_gate = _mm(hn3, dgl, ta=True, outs=[BF16], dw=dw("rows", "ple_gate_w"), name=f"{tag}_d_ple_gate")
    d_w2 = _mm(a, dh2_16, ta=True, outs=[BF16], dw=dw("rows", "mlp_w2"), name=f"{tag}_d_mlp_w2", tn=1024)
    dz = _mm(dh2_16, W["mlp_w2"][i], tb=True, extras=[a], outs=[BF16], name=f"{tag}_mlp_w2_dx", tm=2048, tn=1024,
             epilogue=lambda acc, av: (acc * (2.0 * jnp.sqrt(av.astype(F32))),))
    d_w1 = _mm(hn2, dz, ta=True, outs=[BF16], dw=dw("cols", "mlp_w1"), name=f"{tag}_d_mlp_w1", tn=1024)
    dh1, dh1_16, d_mlp_norm = _dx_norm_bwd(dz, W["mlp_w1"][i], h1, dh2, W["mlp_norm"][i:i + 1], f"{tag}_mlp_w1_dx",
                                           bblk=True)
    big = {f"mlp_w1_{i}": d_w1, f"mlp_w2_{i}": d_w2, f"ple_gate_w_{i}": d_gate, f"ple_proj_w_{i}": d_proj}
    return dh1, dh1_16, big, dict(mlp_norm=d_mlp_norm, ple_norm=d_ple_norm)


def _ret_layer_fwd(h0, W, tabs, after=(), before_out=None):
    hn = _rows(lambda x, g: (_rms(x, g),), [h0], [W["mix_norm"][0:1]], [(D_MODEL, BF16)], name="ret_mix_norm",
               after=after)[0]
    proj = _mm(hn, W["ret_w_in"], bblk=True, outs=[BF16], name="ret_w_in", tm=2048, tn=768)
    out, states = _ret_fwd(proj, tabs, "ret_scan")
    y = _ret_gate(out, proj, W["ret_gn"], "ret_gate")
    if before_out is not None:
        before_out(y)
    h1, hn2 = _mm_rows(y, W["ret_w_out"], extras=[h0], fulls=[W["mlp_norm"][0:1]], outs=[ROW_F32, ROW_BF16],
                       epilogue=_res_norm, name="ret_w_out")
    return h1, hn2, (h0, hn, proj, out, states, y)


def _d_ret_w_out(dh1_16, saved):
    return _mm(saved[5], dh1_16, ta=True, outs=[BF16], dw=("rows", 1, 0, None), name="d_ret_w_out")


def _ret_layer_bwd(dh1, dh1_16, saved, W, tabs, after=(), hook=None, on_grads=None, d_w_out=None):
    h0, hn, proj, out, states, y = saved
    d_w_out = _d_ret_w_out(dh1_16, saved) if d_w_out is None else d_w_out
    dy = _mm(dh1_16, W["ret_w_out"], tb=True, name="ret_w_out_dx", tn=1024, after=after)
    dout, dproj, d_gn = _ret_gate_bwd(out, proj, W["ret_gn"], dy, "ret_gate_bwd",
                                      after=() if hook is None else hook(dy))
    dproj = _ret_bwd(proj, states, dout, dproj, tabs, "ret_scan_bwd")
    d_w_in = _mm(hn, dproj, ta=True, outs=[BF16], dw=("cols", 1, 0, None), name="d_ret_w_in", tn=768)
    big = dict(ret_w_in=d_w_in, ret_w_out=d_w_out)
    later = () if on_grads is None else on_grads(big)
    dh0, _, d_mix = _dx_norm_bwd(dproj, W["ret_w_in"], h0, dh1, W["mix_norm"][0:1], "ret_w_in_dx", bblk=True, tm=512,
                                 after=later)
    return dh0, big, dict(mix_norm=d_mix, ret_gn=d_gn)


def _mla_layer_fwd(h0, hn, W, tabs):
    proj, cqn, ckvn, q, kv, qf, kf, vf = _mla_front(hn, W, tabs, "mla_front")
    o, lse = _flash_fwd(qf, kf, vf, "mla_flash")
    h1, hn2 = _mm_rows(o, W["mla_w_out"], extras=[h0], fulls=[W["mlp_norm"][1:2]], outs=[ROW_F32, ROW_BF16],
                       epilogue=_res_norm, name="mla_w_out")
    return h1, hn2, (h0, hn, proj, cqn, ckvn, q, kv, qf, kf, vf, o, lse)


def _mla_layer_bwd(dh1, dh1_16, saved, W, tabs):
    h0, hn, proj, cqn, ckvn, q, kv, qf, kf, vf, o, lse = saved
    d_w_out = _mm(o, dh1_16, ta=True, outs=[BF16], dw=("rows", 1, 0, None), name="d_mla_w_out")
    def with_delta(acc, ov):
        parts = []
        for h in range(MLA_HEADS):
            sl = slice(h * MLA_VD, (h + 1) * MLA_VD)
            d = jnp.sum(acc[:, sl] * ov[:, sl], axis=-1, keepdims=True)
            parts.append(jnp.broadcast_to(d, (d.shape[0], MLA_VD)))
        return jnp.concatenate(parts, axis=-1), acc

    delta, do16 = _mm_rows(dh1_16, W["mla_w_out"], tb=True, extras=[o], outs=[ROW_F32, ROW_BF16], epilogue=with_delta,
                           name="mla_w_out_dx")
    dqf, dkf, dvf = _flash_bwd(qf, kf, vf, do16, lse, delta, "mla_flash_bwd")
    dq, dkv, dproj, dh0, dh0_16, d_gq, d_gk, d_gqa, d_gkva, d_mix = _mla_back(q, kv, proj, h0, dh1, dqf, dkf, dvf, W, tabs,
                                                                              "mla_back")
    d_w_uq = _mm(cqn, dq, ta=True, outs=[BF16], dw=("cols", 1, 0, None), name="d_mla_w_uq")
    d_w_ukv = _mm(ckvn, dkv, ta=True, outs=[BF16], dw=("cols", 1, 0, None), name="d_mla_w_ukv")
    d_w_in = _mm(hn, dproj, ta=True, outs=[BF16], dw=("rows", 1, 0, None), name="d_mla_w_in")
    return (dh0, dh0_16, dict(mla_w_in=d_w_in, mla_w_uq=d_w_uq, mla_w_ukv=d_w_ukv, mla_w_out=d_w_out),
            dict(mix_norm=d_mix, mla_q_a_norm=d_gqa, mla_kv_a_norm=d_gkva, mla_q_norm=d_gq, mla_k_norm=d_gk))


def _small_grads(n_ret, n_t0, n_mla, n_t1):
    return dict(
        mix_norm=jnp.concatenate([n_ret["mix_norm"], n_mla["mix_norm"]], axis=0),
        mlp_norm=jnp.concatenate([n_t0["mlp_norm"], n_t1["mlp_norm"]], axis=0),
        ple_norm=jnp.concatenate([n_t0["ple_norm"], n_t1["ple_norm"]], axis=0),
        ret_gn=n_ret["ret_gn"], mla_q_a_norm=n_mla["mla_q_a_norm"], mla_kv_a_norm=n_mla["mla_kv_a_norm"],
        mla_q_norm=n_mla["mla_q_norm"], mla_k_norm=n_mla["mla_k_norm"])


_ORDER = ("mix_norm", "ret_w_in", "ret_gn", "ret_w_out", "mla_w_in", "mla_q_a_norm", "mla_kv_a_norm", "mla_w_uq",
          "mla_w_ukv", "mla_q_norm", "mla_k_norm", "mla_w_out", "mlp_norm", "mlp_w1", "mlp_w2", "ple_norm",
          "ple_gate_w", "ple_proj_w")
_TWO_LAYER = ("mlp_w1", "mlp_w2", "ple_gate_w", "ple_proj_w")
HEADS_PER_CHIP = MLA_HEADS // N_CHIPS
GAIN_ROWS = 32


def _travel_parts(w):
    uq = jnp.pad(w["mla_w_uq"][0].reshape(MLA_Q_RANK, HEADS_PER_CHIP, MLA_QKD), ((0, 0), (0, 0), (0, MLA_HP - MLA_QKD)))
    parts = {"ret_w_in": w["ret_w_in"][0], "ret_w_out": w["ret_w_out"][0]}
    for k in _TWO_LAYER:
        parts[k + "_0"] = w[k][0]
    parts["mla_w_in"] = jnp.pad(w["mla_w_in"][0], ((0, 0), (0, MLA_IN_PAD - MLA_IN)))
    parts["mla_w_uq"] = uq.reshape(MLA_Q_RANK, HEADS_PER_CHIP * MLA_HP)
    parts["mla_w_ukv"] = w["mla_w_ukv"][0]
    parts["mla_w_out"] = w["mla_w_out"][0]
    for k in _TWO_LAYER:
        parts[k + "_1"] = w[k][1]
    gains = jnp.concatenate([_pad_row(w["ret_gn"]), _pad_row(w["mla_q_a_norm"]), _pad_row(w["mla_kv_a_norm"]),
                             jnp.zeros((GAIN_ROWS - 3, PACK_W), F32)], axis=0)
    return {"gains": gains, **{k: v.astype(BF16) for k, v in parts.items()}}


def _full_weights(full):
    rows = lambda a: a.reshape(-1, a.shape[-1])
    W = {k: full[k] for k in ("ret_w_in", "mla_w_uq", "mla_w_ukv") if k in full}
    for k in ("ret_w_out", "mla_w_in", "mla_w_out"):
        if k in full:
            W[k] = rows(full[k])
    for k, by_rows in (("mlp_w1", False), ("ple_proj_w", False), ("mlp_w2", True), ("ple_gate_w", True)):
        layers = [full.get(f"{k}_{i}") for i in range(2)]
        W[k] = [rows(t) if (by_rows and t is not None) else t for t in layers]
    return W


def _shard_grad(name, red, shape):
    if name == "mla_w_in":
        red = red.reshape(-1, MLA_IN_PAD)[:, :MLA_IN]
    elif name == "mla_w_uq":
        red = red.reshape(MLA_Q_RANK, HEADS_PER_CHIP, MLA_HP)[:, :, :MLA_QKD]
    return red.reshape(shape)


def _pad_row(v):
    v = v.reshape(1, -1)
    return jnp.pad(v, ((0, 0), (0, PACK_W - v.shape[1])))


def kernel(x, p, mix_norm, ret_w_in, ret_gn, ret_w_out, mla_w_in, mla_q_a_norm, mla_kv_a_norm, mla_w_uq, mla_w_ukv, mla_q_norm, mla_k_norm, mla_w_out, mlp_norm, mlp_w1, mlp_w2, ple_norm, ple_gate_w, ple_proj_w, loss_target, m_mix_norm, m_ret_w_in, m_ret_gn, m_ret_w_out, m_mla_w_in, m_mla_q_a_norm, m_mla_kv_a_norm, m_mla_w_uq, m_mla_w_ukv, m_mla_q_norm, m_mla_k_norm, m_mla_w_out, m_mlp_norm, m_mlp_w1, m_mlp_w2, m_ple_norm, m_ple_gate_w, m_ple_proj_w, v_mix_norm, v_ret_w_in, v_ret_gn, v_ret_w_out, v_mla_w_in, v_mla_q_a_norm, v_mla_kv_a_norm, v_mla_w_uq, v_mla_w_ukv, v_mla_q_norm, v_mla_k_norm, v_mla_w_out, v_mlp_norm, v_mlp_w1, v_mlp_w2, v_ple_norm, v_ple_gate_w, v_ple_proj_w):
    w = dict(mix_norm=mix_norm, ret_w_in=ret_w_in, ret_gn=ret_gn, ret_w_out=ret_w_out, mla_w_in=mla_w_in,
             mla_q_a_norm=mla_q_a_norm, mla_kv_a_norm=mla_kv_a_norm, mla_w_uq=mla_w_uq, mla_w_ukv=mla_w_ukv,
             mla_q_norm=mla_q_norm, mla_k_norm=mla_k_norm, mla_w_out=mla_w_out, mlp_norm=mlp_norm, mlp_w1=mlp_w1,
             mlp_w2=mlp_w2, ple_norm=ple_norm, ple_gate_w=ple_gate_w, ple_proj_w=ple_proj_w)
    m = dict(mix_norm=m_mix_norm, ret_w_in=m_ret_w_in, ret_gn=m_ret_gn, ret_w_out=m_ret_w_out, mla_w_in=m_mla_w_in,
             mla_q_a_norm=m_mla_q_a_norm, mla_kv_a_norm=m_mla_kv_a_norm, mla_w_uq=m_mla_w_uq, mla_w_ukv=m_mla_w_ukv,
             mla_q_norm=m_mla_q_norm, mla_k_norm=m_mla_k_norm, mla_w_out=m_mla_w_out, mlp_norm=m_mlp_norm,
             mlp_w1=m_mlp_w1, mlp_w2=m_mlp_w2, ple_norm=m_ple_norm, ple_gate_w=m_ple_gate_w, ple_proj_w=m_ple_proj_w)
    v = dict(mix_norm=v_mix_norm, ret_w_in=v_ret_w_in, ret_gn=v_ret_gn, ret_w_out=v_ret_w_out, mla_w_in=v_mla_w_in,
             mla_q_a_norm=v_mla_q_a_norm, mla_kv_a_norm=v_mla_kv_a_norm, mla_w_uq=v_mla_w_uq, mla_w_ukv=v_mla_w_ukv,
             mla_q_norm=v_mla_q_norm, mla_k_norm=v_mla_k_norm, mla_w_out=v_mla_w_out, mlp_norm=v_mlp_norm,
             mlp_w1=v_mlp_w1, mlp_w2=v_mlp_w2, ple_norm=v_ple_norm, ple_gate_w=v_ple_gate_w, ple_proj_w=v_ple_proj_w)
    xi, yi, ci = _place()
    chip = 2 * xi + yi
    n = N_CHIPS

    parts = _travel_parts(w)
    first = ("gains", "ret_w_in")
    mid = ["ret_w_out"] + [k + "_0" for k in _TWO_LAYER]
    last = [k for k in parts if k not in first and k not in mid]
    full = dict(zip(first, _gather_weights([parts[k] for k in first], "gather_first")))

    def gather_behind(names, tag, after):
        copies = _gather_copies([parts[k].shape[0] for k in names])
        started = _split_start(f"gather_{tag}_start", [parts[k] for k in names],
                               [jax.ShapeDtypeStruct((n, *parts[k].shape), BF16) for k in names], 3 * len(names),
                               copies, after=after)

        def arrive(after):
            _, landed = _split_wait(f"gather_{tag}_wait", *started[:4], copies, after=after)
            full.update(zip(names, _gather_weights([parts[k] for k in names], f"gather_{tag}_finish", landed=landed)))
            W.update(_full_weights(full))
        return started[4], arrive

    mid_token, mid_arrive = gather_behind(mid, "mid", [full["ret_w_in"]])
    g_token, last_arrive = gather_behind(last, "last", [mid_token])
    gains = full["gains"]
    W = dict(mix_norm=mix_norm, mlp_norm=mlp_norm, ple_norm=ple_norm,
             mla_q_norm=jnp.pad(mla_q_norm, ((0, 0), (0, MLA_HP - MLA_QKD))),
             mla_k_norm=jnp.pad(mla_k_norm, ((0, 0), (0, MLA_HP - MLA_QKD))),
             ret_w_in=full["ret_w_in"],
             ret_gn=gains[:, 0, :RET_HEADS * 128].reshape(n, RET_HEADS, 128).transpose(1, 0, 2).reshape(RET_HEADS, RET_DV),
             mla_q_a_norm=gains[:, 1, :MLA_Q_RANK // n].reshape(1, MLA_Q_RANK),
             mla_kv_a_norm=gains[:, 2, :MLA_KV_RANK // n].reshape(1, MLA_KV_RANK))
    x0, p16, target = x[0], p[:, 0].astype(BF16), loss_target[0]
    T = x0.shape[0]
    ret_tabs, mla_tabs = _ret_tables(T), _mla_tables(T)

    h1, hn, s_ret = _ret_layer_fwd(x0, W, ret_tabs, after=[g_token], before_out=lambda y: mid_arrive([y]))
    h3, hn, s_tail0 = _tail_fwd(h1, hn, p16, W, 0, "l0", next_gain=W["mix_norm"][1:2])
    last_arrive([h3])
    h4, hn, s_mla = _mla_layer_fwd(h3, hn, W, mla_tabs)
    dy, loss, s_tail1 = _tail_fwd(h4, hn, p16, W, 1, "l1", target=target)

    dh4, dh4_16, g_t1, n_t1 = _tail_bwd(dy, s_tail1, p16, W, 1, "l1")
    dh3, _, g_mla, n_mla = _mla_layer_bwd(dh4, dh4_16, s_mla, W, mla_tabs)
    stages = {}

    def scatter_start(tag, begun):
        started = _split_start(f"scatter_{tag}_start", begun[3], _got_shapes(begun[3]), 3 * len(begun[3]), _scatter_copies)
        stages[tag] = (begun, started[:4])
        return [started[4]]

    def scatter_end(tag, after):
        begun, started = stages[tag]
        return _reduce_end(begun, _split_wait(f"scatter_{tag}_wait", *started, _scatter_copies, after=after)[1], chip, ci)

    swap_a, token = _swap_begin({**g_mla, **g_t1}, "a")
    dh1, dh1_16, g_t0, n_t0 = _tail_bwd(dh3, s_tail0, p16, W, 0, "l0", after=[token],
                                        hook=lambda t: scatter_start("a", _swap_end(swap_a, ci, "a", [t])))
    d_ret_w_out = _d_ret_w_out(dh1_16, s_ret)
    swap_b, token = _swap_begin({**g_t0, "ret_w_out": d_ret_w_out}, "b")
    dx, _, n_ret = _ret_layer_bwd(
        dh1, dh1_16, s_ret, W, ret_tabs, after=[token], d_w_out=d_ret_w_out,
        hook=lambda t: scatter_start("b", _swap_end(swap_b, ci, "b", [t])),
        on_grads=lambda g: scatter_start("c", _reduce_begin({"ret_w_in": g["ret_w_in"]}, ci, "c")))
    red = {**scatter_end("a", [dx]), **scatter_end("b", [dx]), **scatter_end("c", [dx])}
    red = dict(zip(red, _share_halves(list(red.values()))))
    gs = _small_grads(n_ret, n_t0, n_mla, n_t1)
    small_g = jnp.concatenate([
        gs["mix_norm"], gs["mlp_norm"], gs["ple_norm"], gs["ret_gn"].reshape(2, PACK_W), _pad_row(gs["mla_q_a_norm"]),
        _pad_row(gs["mla_kv_a_norm"]), _pad_row(gs["mla_q_norm"][:, :MLA_QKD]), _pad_row(gs["mla_k_norm"][:, :MLA_QKD]),
        _pad_row(loss[:, :1]), jnp.zeros((3, PACK_W), F32)], axis=0)
    tot = _allsum_small(small_g, "sum_small_grads")
    gn_all = tot[6:8].reshape(RET_HEADS, n, -1)
    g_small = dict(
        mix_norm=tot[0:2], mlp_norm=tot[2:4], ple_norm=tot[4:6],
        ret_gn=lax.dynamic_index_in_dim(gn_all, chip, axis=1, keepdims=False),
        mla_q_a_norm=lax.dynamic_index_in_dim(tot[8, :MLA_Q_RANK].reshape(n, -1), chip, axis=0, keepdims=True),
        mla_kv_a_norm=lax.dynamic_index_in_dim(tot[9, :MLA_KV_RANK].reshape(n, -1), chip, axis=0, keepdims=True),
        mla_q_norm=tot[10:11, :MLA_QKD], mla_k_norm=tot[11:12, :MLA_QKD])
    loss_out = tot[12, 0]

    outs = []
    for k in _ORDER:
        if k in _TWO_LAYER:
            res = None
            for i in (1, 0):
                res = _adamw(w[k], red[f"{k}_{i}"], m[k], v[k], f"adamw_{k}_{i}", layers=2, layer=i, into=res)
        elif k in red:
            res = _adamw(w[k], _shard_grad(k, red[k], w[k].shape), m[k], v[k], f"adamw_{k}")
        else:
            res = _adamw(w[k], g_small[k], m[k], v[k], f"adamw_{k}")
        outs.append(res)
    return (loss_out, dx[None], *[o[0] for o in outs], *[o[1] for o in outs], *[o[2] for o in outs],
            *[o[3] for o in outs])
```

```python
import jax
import jax.numpy as jnp
import numpy as np
from jax import lax
from jax.experimental import pallas as pl
from jax.experimental.pallas import tpu as pltpu

F32 = jnp.float32
BF16 = jnp.bfloat16

EPS = 1e-6
D_MODEL = 1024
CHUNK = 64
ROPE_THETA = 10000.0
RET_HEADS = 4
RET_DK = 256
RET_DV = 512
RET_GROUP = 1
RET_BLOCK = 256
RET_ROWS = 2048
MLA_HEADS = 8
MLA_ROPE = 64
MLA_QKD = 192
MLA_VD = 128
MLA_HP = 256
MLA_Q_RANK = 384
MLA_KV_RANK = 256
MLA_IN = 704
MLA_IN_PAD = 768
N_CHIPS = 4

ADAM_LR = 0.001
ADAM_B1 = 0.9
ADAM_B2 = 0.999
ADAM_EPS = 1e-08
ADAM_WD = 0.01
ADAM_STEP = 10

VMEM_LIMIT = 56 * 1024 * 1024
PACK_W = 1024
NEG = -1e30
LOG2E = 1.4426950408889634
FLASH_T = 512
FLASH_HEADS = 4
FLASH_BWD_HEADS = 2
MM_SUB_ROWS = 256
SUM_ROWS = 512
ADAM_ROWS = 512


def _cparams(sem=None):
    return pltpu.CompilerParams(dimension_semantics=sem, vmem_limit_bytes=VMEM_LIMIT)


def _pick(dim, pref):
    if dim <= pref:
        return dim
    t = pref
    while dim % t:
        t //= 2
    return t


def _mm(a, b, *, name, ta=False, tb=False, bblk=False, outs=None, extras=(), epilogue=None, dw=None,
        tm=1024, tn=512, after=()):
    if ta:
        K, M = a.shape
    else:
        M, K = a.shape
    if bblk and tb:
        nb, N, Kq = b.shape
        assert nb * Kq == K
    elif bblk:
        nb, Kb, Nq = b.shape
        N = nb * Nq
        assert Kb == K
    else:
        N = b.shape[0] if tb else b.shape[1]
    tn = _pick(Nq if (bblk and not tb) else N, tn)
    if dw is not None and dw[0] == "cols":
        tn = _pick(N // N_CHIPS, tn)
    tm = _pick(M // N_CHIPS if (dw is not None and dw[0] == "rows") else M, tm)
    grid = (M // tm, N // tn)

    a_spec = pl.BlockSpec((K, tm), lambda i, j: (0, i)) if ta else pl.BlockSpec((tm, K), lambda i, j: (i, 0))
    if bblk and tb:
        b_spec = pl.BlockSpec((nb, tn, Kq), lambda i, j: (0, j, 0))
    elif bblk:
        npb = Nq // tn
        b_spec = pl.BlockSpec((None, K, tn), lambda i, j: (j // npb, 0, j % npb))
    elif tb:
        b_spec = pl.BlockSpec((tn, K), lambda i, j: (j, 0))
    else:
        b_spec = pl.BlockSpec((K, tn), lambda i, j: (0, j))
    in_specs = [a_spec, b_spec] + [pl.BlockSpec((tm, tn), lambda i, j: (i, j)) for _ in extras]
    args = [a, b, *extras]
    aliases = {}
    if outs is None:
        outs = [F32]
    if dw is None:
        o_specs = [pl.BlockSpec((tm, tn), lambda i, j: (i, j)) for _ in outs]
        o_shapes = [jax.ShapeDtypeStruct((M, N), dt) for dt in outs]
    else:
        kind, layers, layer, into = dw
        if kind == "cols":
            per = (N // N_CHIPS) // tn
            o_specs = [pl.BlockSpec((None, None, tm, tn), lambda i, j: (j // per, layer, i, j % per))]
            o_shapes = [jax.ShapeDtypeStruct((N_CHIPS, layers, M, N // N_CHIPS), outs[0])]
        else:
            per = (M // N_CHIPS) // tm
            o_specs = [pl.BlockSpec((None, None, tm, tn), lambda i, j: (i // per, layer, i % per, j))]
            o_shapes = [jax.ShapeDtypeStruct((N_CHIPS, layers, M // N_CHIPS, N), outs[0])]
        if into is not None:
            aliases = {len(args): 0}
            in_specs.append(pl.BlockSpec(memory_space=pl.ANY))
            args.append(into)
    for t in after:
        in_specs.append(pl.BlockSpec(memory_space=pl.ANY))
        args.append(t)
    n_e, n_o = len(extras), len(outs)

    sub = _pick(tm, MM_SUB_ROWS)

    def body(a_ref, b_ref, *rest):
        e_refs, o_refs = rest[:n_e], rest[len(rest) - n_o:]
        for r0 in range(0, tm, sub):
            rows = slice(r0, r0 + sub)
            av = (a_ref[:, rows] if ta else a_ref[rows, :]).astype(BF16)
            if bblk and tb:
                acc = _dot_nt(av[:, :Kq], b_ref[0].astype(BF16))
                for s in range(1, nb):
                    acc = acc + _dot_nt(av[:, s * Kq:(s + 1) * Kq], b_ref[s].astype(BF16))
            elif ta:
                acc = _dot_tn(av, b_ref[...].astype(BF16))
            elif tb:
                acc = _dot_nt(av, b_ref[...].astype(BF16))
            else:
                acc = _dot(av, b_ref[...].astype(BF16))
            vals = (acc,) if epilogue is None else epilogue(acc, *[e[rows, :] for e in e_refs])
            for o, v in zip(o_refs, vals):
                o[rows, :] = v.astype(o.dtype)

    res = pl.pallas_call(
        body, name=name, grid=grid, in_specs=in_specs, out_specs=o_specs, out_shape=o_shapes,
        input_output_aliases=aliases, compiler_params=_cparams(("parallel", "arbitrary")),
    )(*args)
    return res[0] if n_o == 1 else res


def _mm_rows(a, b, *, name, epilogue, outs, tb=False, bblk=False, extras=(), fulls=(), accs=(), tm=512, after=()):
    M, K = a.shape
    tm = _pick(M, tm)
    sub = _pick(tm, MM_SUB_ROWS)
    nb = b.shape[0] if bblk else 1
    n_e, n_f, n_o, n_a = len(extras), len(fulls), len(outs), len(accs)
    n_in = 2 + n_e + n_f + len(after)

    def whole(t):
        return pl.BlockSpec(t.shape, lambda i, nd=t.ndim: (0,) * nd)

    in_specs = [pl.BlockSpec((tm, K), lambda i: (i, 0)), whole(b)]
    in_specs += [pl.BlockSpec((tm, e.shape[1]), lambda i: (i, 0)) for e in extras] + [whole(f) for f in fulls]
    in_specs += [pl.BlockSpec(memory_space=pl.ANY) for _ in after]
    out_specs = [pl.BlockSpec((tm, w), lambda i: (i, 0)) for w, _ in outs] + [pl.BlockSpec(s, lambda i: (0, 0)) for s, _ in accs]
    out_shape = [jax.ShapeDtypeStruct((M, w), dt) for w, dt in outs] + [jax.ShapeDtypeStruct(s, dt) for s, dt in accs]

    def body(a_ref, b_ref, *rest):
        e_refs, f_refs = rest[:n_e], rest[n_e:n_e + n_f]
        o_refs, acc_refs = rest[n_in - 2:n_in - 2 + n_o], rest[n_in - 2 + n_o:]
        fv = [f[...] for f in f_refs]
        totals = None
        for r0 in range(0, tm, sub):
            rows = slice(r0, r0 + sub)
            av = a_ref[rows, :].astype(BF16)
            if bblk and tb:
                kq = K // nb
                acc = _dot_nt(av[:, :kq], b_ref[0])
                for s in range(1, nb):
                    acc = acc + _dot_nt(av[:, s * kq:(s + 1) * kq], b_ref[s])
            elif bblk:
                acc = jnp.concatenate([_dot(av, b_ref[s]) for s in range(nb)], axis=-1)
            elif tb:
                acc = _dot_nt(av, b_ref[...])
            else:
                acc = _dot(av, b_ref[...])
            vals = epilogue(acc, *[e[rows, :] for e in e_refs], *fv)
            for o, v in zip(o_refs, vals[:n_o]):
                o[rows, :] = v.astype(o.dtype)
            part = vals[n_o:]
            totals = part if totals is None else [t + p for t, p in zip(totals, part)]
        first_step = pl.program_id(0) == 0
        for o, v in zip(acc_refs, totals):
            @pl.when(first_step)
            def _(o=o, v=v):
                o[...] = v.astype(o.dtype)

            @pl.when(jnp.logical_not(first_step))
            def _(o=o, v=v):
                o[...] += v.astype(o.dtype)

    return pl.pallas_call(
        body, name=name, grid=(M // tm,), in_specs=in_specs, out_specs=out_specs, out_shape=out_shape,
        compiler_params=_cparams(("arbitrary",)),
    )(a, b, *extras, *fulls, *after)


def _rows(fn, rows, fulls, outs, accs=(), *, name, tile=512, after=()):
    first = rows[0][0] if isinstance(rows[0], tuple) else rows[0]
    T = first.shape[0]
    tile = _pick(T, tile)
    in_specs, args = [], []
    for r in rows:
        if isinstance(r, tuple):
            arr, w, cb = r
            in_specs.append(pl.BlockSpec((tile, w), lambda i, cb=cb: (i, cb)))
        else:
            arr = r
            in_specs.append(pl.BlockSpec((tile, arr.shape[1]), lambda i: (i, 0)))
        args.append(arr)
    for f in fulls:
        in_specs.append(pl.BlockSpec(f.shape, lambda i, nd=f.ndim: (0,) * nd))
        args.append(f)
    outs = [o if len(o) == 4 else (*o, o[0], 0) for o in outs]
    out_specs = [pl.BlockSpec((tile, w), lambda i, cb=cb: (i, cb)) for w, _, _, cb in outs]
    out_specs += [pl.BlockSpec(s, lambda i: (0, 0)) for s, _ in accs]
    out_shape = [jax.ShapeDtypeStruct((T, tw), dt) for _, dt, tw, _ in outs]
    out_shape += [jax.ShapeDtypeStruct(s, dt) for s, dt in accs]
    n_in, n_out = len(args), len(outs)
    for t in after:
        in_specs.append(pl.BlockSpec(memory_space=pl.ANY))
        args.append(t)

    def body(*refs):
        vals = fn(*[r[...] for r in refs[:n_in]])
        o_refs = refs[len(args):]
        for o, v in zip(o_refs[:n_out], vals[:n_out]):
            o[...] = v.astype(o.dtype)
        first_step = pl.program_id(0) == 0
        for o, v in zip(o_refs[n_out:], vals[n_out:]):
            @pl.when(first_step)
            def _(o=o, v=v):
                o[...] = v.astype(o.dtype)

            @pl.when(jnp.logical_not(first_step))
            def _(o=o, v=v):
                o[...] += v.astype(o.dtype)

    res = pl.pallas_call(
        body, name=name, grid=(T // tile,), in_specs=in_specs, out_specs=out_specs, out_shape=out_shape,
        compiler_params=_cparams(("arbitrary",)),
    )(*args)
    return res


def _rowsum(v, mxu):
    if not mxu:
        return jnp.sum(v, axis=-1, keepdims=True)
    ones = jnp.ones((v.shape[1], v.shape[1]), BF16)
    hi = v.astype(BF16)
    lo = (v - hi.astype(F32)).astype(BF16)
    return _dot(hi, ones) + _dot(lo, ones)


def _rms(x, g, mxu=False):
    r = lax.rsqrt(_rowsum(x * x, mxu) / x.shape[-1] + EPS)
    return (x * r) * g


def _rms_bwd(x, dy, g, n=None, mxu=False):
    n = x.shape[-1] if n is None else n
    r = lax.rsqrt(_rowsum(x * x, mxu) / n + EPS)
    xh = x * r
    dxh = dy * g
    dx = r * (dxh - xh * (_rowsum(dxh * xh, mxu) / n))
    return dx, dy * xh


def _colsum(v):
    return jnp.sum(v, axis=0, keepdims=True)


def _sigmoid(x):
    return 0.5 * jnp.tanh(0.5 * x) + 0.5


def _widen(v, width):
    reps = width // v.shape[1]
    return v if reps == 1 else jnp.concatenate([v] * reps, axis=-1)


def _rope_angles(T, dim):
    inv = (1.0 / (np.float32(ROPE_THETA) ** (np.arange(0, dim, 2, dtype=np.float32) / np.float32(dim)))).astype(np.float32)
    return np.arange(T, dtype=np.float32)[:, None] * inv[None, :]


def _ret_tables(T):
    ang = _rope_angles(T, RET_DK)
    log_gamma = np.log(np.float32(1.0) - np.float32(2.0) ** (-5.0 - np.arange(RET_HEADS, dtype=np.float32)))
    idx = np.arange(RET_BLOCK, dtype=np.float32)
    chunk = np.arange(RET_BLOCK) // CHUNK
    dist = idx[:, None] - idx[None, :]
    seen = np.where(chunk[:, None] == chunk[None, :], np.abs(dist), np.where(chunk[:, None] > chunk[None, :], dist, np.inf))
    intra = np.exp(log_gamma[:, None, None] * seen[None].astype(np.float32))
    qd = np.exp(log_gamma[:, None] * (idx + 1.0))[:, :, None]
    kd = np.exp(log_gamma[:, None] * (RET_BLOCK - 1.0 - idx))[:, :, None]
    cd = np.exp(log_gamma * RET_BLOCK)[:, None, None]
    return tuple(jnp.asarray(t, F32) for t in (np.cos(ang), np.sin(ang), intra, qd, kd, cd))


def _rope_half(x, c, s):
    x1, x2 = x[:, :RET_DK // 2], x[:, RET_DK // 2:]
    return jnp.concatenate([x1 * c - x2 * s, x2 * c + x1 * s], axis=-1)


def _rope_half_bwd(d, c, s):
    d1, d2 = d[:, :RET_DK // 2], d[:, RET_DK // 2:]
    return jnp.concatenate([d1 * c + d2 * s, d2 * c - d1 * s], axis=-1)


def _dot(a, b):
    return lax.dot_general(a, b, (((1,), (0,)), ((), ())), preferred_element_type=F32)


def _dot_nt(a, b):
    return lax.dot_general(a, b, (((1,), (1,)), ((), ())), preferred_element_type=F32)


def _dot_tn(a, b):
    return lax.dot_general(a, b, (((0,), (0,)), ((), ())), preferred_element_type=F32)


def _ret_specs(T, tb, rev):
    nj = T // tb
    jj = (lambda j: nj - 1 - j) if rev else (lambda j: j)
    g = RET_GROUP
    kq = RET_HEADS // g
    vq = 2 * RET_HEADS * RET_DK // (g * RET_DV)
    return dict(
        q=pl.BlockSpec((tb, g * RET_DK), lambda h, j: (jj(j), h)),
        k=pl.BlockSpec((tb, g * RET_DK), lambda h, j: (jj(j), kq + h)),
        v=pl.BlockSpec((tb, g * RET_DV), lambda h, j: (jj(j), vq + h)),
        tab=pl.BlockSpec((tb, RET_DK // 2), lambda h, j: (jj(j), 0)),
        intra=pl.BlockSpec((g, RET_BLOCK, RET_BLOCK), lambda h, j: (h, 0, 0)),
        dec=pl.BlockSpec((g, RET_BLOCK, 1), lambda h, j: (h, 0, 0)),
        cd=pl.BlockSpec((g, 1, 1), lambda h, j: (h, 0, 0)),
        o=pl.BlockSpec((tb, g * RET_DV), lambda h, j: (jj(j), h)),
        s=pl.BlockSpec((g, tb // RET_BLOCK, RET_DK, RET_DV), lambda h, j: (h, jj(j), 0, 0)),
    )


def _ret_fwd(proj, tabs, name):
    T = proj.shape[0]
    cos, sin, intra, qd, kd, cd = tabs
    tb = _pick(T, RET_ROWS)
    cps = tb // RET_BLOCK
    sp = _ret_specs(T, tb, False)
    scale = RET_DK ** -0.5

    def body(q_ref, k_ref, v_ref, cos_ref, sin_ref, intra_ref, qd_ref, kd_ref, cd_ref, o_ref, s_ref, state):
        @pl.when(pl.program_id(1) == 0)
        def _():
            state[...] = jnp.zeros_like(state)

        for c in range(cps):
            rows = pl.ds(c * RET_BLOCK, RET_BLOCK)
            co, si = cos_ref[rows, :], sin_ref[rows, :]
            for h in range(RET_GROUP):
                hk, hv = slice(h * RET_DK, (h + 1) * RET_DK), slice(h * RET_DV, (h + 1) * RET_DV)
                q = _rope_half(q_ref[rows, hk].astype(F32), co, si)
                k = _rope_half(k_ref[rows, hk].astype(F32), co, si) * scale
                vb = v_ref[rows, hv].astype(BF16)
                st = state[h]
                sb = st.astype(BF16)
                s_ref[h, c] = sb
                sc = _dot_nt(q.astype(BF16), k.astype(BF16)) * intra_ref[h]
                inner = _dot(sc.astype(BF16), vb)
                cross = _dot((q * qd_ref[h]).astype(BF16), sb)
                o_ref[rows, hv] = inner + cross
                state[h] = st * cd_ref[h] + _dot_tn((k * kd_ref[h]).astype(BF16), vb)

    return pl.pallas_call(
        body, name=name, grid=(RET_HEADS // RET_GROUP, T // tb),
        in_specs=[sp["q"], sp["k"], sp["v"], sp["tab"], sp["tab"], sp["intra"], sp["dec"], sp["dec"], sp["cd"]],
        out_specs=[sp["o"], sp["s"]],
        out_shape=[jax.ShapeDtypeStruct((T, RET_HEADS * RET_DV), F32),
                   jax.ShapeDtypeStruct((RET_HEADS, T // RET_BLOCK, RET_DK, RET_DV), BF16)],
        scratch_shapes=[pltpu.VMEM((RET_GROUP, RET_DK, RET_DV), F32)],
        compiler_params=_cparams(("arbitrary", "arbitrary")),
    )(proj, proj, proj, cos, sin, intra, qd, kd, cd)


def _ret_bwd(proj, states, dout, dproj, tabs, name):
    assert RET_GROUP == 1
    T = proj.shape[0]
    cos, sin, intra, qd, kd, cd = tabs
    tb = _pick(T, RET_ROWS)
    cps = tb // RET_BLOCK
    nj = T // tb
    sp = _ret_specs(T, tb, True)
    scale = RET_DK ** -0.5
    k0, v0 = RET_HEADS * RET_DK, 2 * RET_HEADS * RET_DK

    def body(q_ref, k_ref, v_ref, cos_ref, sin_ref, intra_ref, qd_ref, kd_ref, cd_ref, s_ref, do_ref, _dproj_in,
             out_ref, dq_s, dk_s, dv_s, sems, dstate):
        head, j = pl.program_id(0), pl.program_id(1)
        step = head * nj + j
        slot = step % 2
        dq_ref, dk_ref, dv_ref = dq_s.at[slot], dk_s.at[slot], dv_s.at[slot]

        @pl.when(j == 0)
        def _():
            dstate[...] = jnp.zeros_like(dstate)

        for c in reversed(range(cps)):
            rows = pl.ds(c * RET_BLOCK, RET_BLOCK)
            co, si = cos_ref[rows, :], sin_ref[rows, :]
            for h in range(RET_GROUP):
                hk, hv = slice(h * RET_DK, (h + 1) * RET_DK), slice(h * RET_DV, (h + 1) * RET_DV)
                q = _rope_half(q_ref[rows, hk].astype(F32), co, si)
                k = _rope_half(k_ref[rows, hk].astype(F32), co, si) * scale
                qb, kb = q.astype(BF16), k.astype(BF16)
                vb = v_ref[rows, hv].astype(BF16)
                dob = do_ref[rows, hv].astype(BF16)
                sb = s_ref[h, c]
                ia = intra_ref[h]
                pb = (_dot_nt(qb, kb) * ia).astype(BF16)
                dsn = dstate[h]
                dsb = dsn.astype(BF16)
                kdk = (k * kd_ref[h]).astype(BF16)
                qdq = (q * qd_ref[h]).astype(BF16)
                dv = _dot_tn(pb, dob) + _dot(kdk, dsb)
                dpb = (_dot_nt(dob, vb) * ia).astype(BF16)
                dq = _dot(dpb, kb) + _dot_nt(dob, sb) * qd_ref[h]
                dk = _dot_tn(dpb, qb) + _dot_nt(vb, dsb) * kd_ref[h]
                dstate[h] = dsn * cd_ref[h] + _dot_tn(qdq, dob)
                dq_ref[rows, hk] = _rope_half_bwd(dq, co, si).astype(BF16)
                dk_ref[rows, hk] = _rope_half_bwd(dk * scale, co, si).astype(BF16)
                dv_ref[rows, hv] = dv.astype(BF16)

        def copies(sl):
            r = pl.ds(pl.multiple_of((nj - 1 - j) * tb, tb), tb)
            cols = lambda first, w: pl.ds(pl.multiple_of(first + head * w, 128), w)
            return [pltpu.make_async_copy(dq_s.at[sl], out_ref.at[r, cols(0, RET_DK)], sems.at[sl, 0]),
                    pltpu.make_async_copy(dk_s.at[sl], out_ref.at[r, cols(k0, RET_DK)], sems.at[sl, 1]),
                    pltpu.make_async_copy(dv_s.at[sl], out_ref.at[r, cols(v0, RET_DV)], sems.at[sl, 2])]

        @pl.when(step > 0)
        def _():
            for cp in copies(1 - slot):
                cp.wait()

        for cp in copies(slot):
            cp.start()

        @pl.when(step == RET_HEADS * nj - 1)
        def _():
            for cp in copies(slot):
                cp.wait()

    return pl.pallas_call(
        body, name=name, grid=(RET_HEADS, nj),
        in_specs=[sp["q"], sp["k"], sp["v"], sp["tab"], sp["tab"], sp["intra"], sp["dec"], sp["dec"], sp["cd"],
                  sp["s"], sp["o"], pl.BlockSpec(memory_space=pl.ANY)],
        out_specs=pl.BlockSpec(memory_space=pl.ANY), out_shape=jax.ShapeDtypeStruct(dproj.shape, dproj.dtype),
        input_output_aliases={11: 0},
        scratch_shapes=[pltpu.VMEM((2, tb, RET_DK), BF16), pltpu.VMEM((2, tb, RET_DK), BF16),
                        pltpu.VMEM((2, tb, RET_DV), BF16), pltpu.SemaphoreType.DMA((2, 3)),
                        pltpu.VMEM((RET_GROUP, RET_DK, RET_DV), F32)],
        compiler_params=_cparams(("arbitrary", "arbitrary")),
    )(proj, proj, proj, cos, sin, intra, qd, kd, cd, states, dout, dproj)


def _ret_gate(out, proj, gn, name):
    def fn(o, g, *gains):
        g = g.astype(F32)
        parts = [_rms(o[:, h * RET_DV:(h + 1) * RET_DV], gains[h]) for h in range(RET_HEADS)]
        return (g * _sigmoid(g) * jnp.concatenate(parts, axis=-1),)
    w = RET_HEADS * RET_DV
    return _rows(fn, [out, (proj, w, 2)], [gn[h:h + 1] for h in range(RET_HEADS)], [(w, BF16)], name=name)[0]


def _ret_gate_bwd(out, proj, gn, dy, name, after=()):
    def fn(o, g, d, *gains):
        g = g.astype(F32)
        sg = _sigmoid(g)
        silu = g * sg
        dsilu = sg * (1.0 + g * (1.0 - sg))
        dos, dgs = [], []
        row = lax.broadcasted_iota(jnp.int32, (RET_HEADS, RET_DV), 0)
        dgn = jnp.zeros((RET_HEADS, RET_DV), F32)
        for h in range(RET_HEADS):
            sl = slice(h * RET_DV, (h + 1) * RET_DV)
            oh = o[:, sl]
            dgs.append(d[:, sl] * _rms(oh, gains[h]) * dsilu[:, sl])
            dx, dg = _rms_bwd(oh, d[:, sl] * silu[:, sl], gains[h])
            dos.append(dx)
            dgn = dgn + jnp.where(row == h, _colsum(dg), 0.0)
        return jnp.concatenate(dos, axis=-1), jnp.concatenate(dgs, axis=-1), dgn
    w = RET_HEADS * RET_DV
    return _rows(fn, [out, (proj, w, 2), dy], [gn[h:h + 1] for h in range(RET_HEADS)],
                 [(w, BF16), (w, BF16, proj.shape[1], 2)], [((RET_HEADS, RET_DV), F32)], name=name, tile=256,
                 after=after)


def _mla_tables(T):
    ang = _rope_angles(T, MLA_ROPE)
    c, s = np.cos(ang), np.sin(ang)
    z32, z64 = np.zeros((T, 32), np.float32), np.zeros((T, 64), np.float32)
    cos_t = np.concatenate([c, c, z64], axis=1)
    sin_a = np.concatenate([-s, z32, z64], axis=1)
    sin_b = np.concatenate([z32, s, z64], axis=1)
    return tuple(jnp.asarray(t, F32) for t in (cos_t, sin_a, sin_b))


def _rope_blk(x, ct, sa, sb):
    return x * ct + pltpu.roll(x, 96, 1) * sa + pltpu.roll(x, 32, 1) * sb


def _rope_blk_bwd(d, ct, sa, sb):
    return d * ct + pltpu.roll(d * sa, 32, 1) + pltpu.roll(d * sb, 96, 1)


def _head_norm(x, gain):
    r = lax.rsqrt(_rowsum(x * x, True) / MLA_QKD + EPS)
    return (x * r) * gain


def _prep_heads(qv, kvv, kr, ct, sa, sb, gqv, gkv):
    qs, ks, vs = [], [], []
    for h in range(MLA_HEADS):
        b = h * MLA_HP
        y = _head_norm(qv[:, b:b + MLA_HP], gqv)
        qs += [y[:, :128], _rope_blk(y[:, 128:], ct, sa, sb)]
        y = _head_norm(jnp.concatenate([kvv[:, b:b + 128], kr], axis=-1), gkv)
        ks += [y[:, :128], _rope_blk(y[:, 128:], ct, sa, sb)]
        vs.append(kvv[:, b + 128:b + 256])
    return jnp.concatenate(qs, axis=-1), jnp.concatenate(ks, axis=-1), jnp.concatenate(vs, axis=-1)


def _mla_front(hn, W, tabs, name):
    wide = MLA_HEADS * MLA_HP
    gq = W["mla_q_norm"] * (MLA_QKD ** -0.5 * LOG2E)

    def epilogue(acc, ct, sa, sb, gqa, gkva, wuq, wukv, gqv, gkv):
        cqn = _rms(acc[:, :MLA_Q_RANK], gqa).astype(BF16)
        ckvn = _rms(acc[:, MLA_Q_RANK:MLA_Q_RANK + MLA_KV_RANK], gkva).astype(BF16)
        q = jnp.concatenate([_dot(cqn, wuq[s]) for s in range(N_CHIPS)], axis=-1).astype(BF16)
        kv = jnp.concatenate([_dot(ckvn, wukv[s]) for s in range(N_CHIPS)], axis=-1).astype(BF16)
        qf, kf, vf = _prep_heads(q.astype(F32), kv.astype(F32), acc[:, MLA_IN_PAD - 128:], ct, sa, sb, gqv, gkv)
        return acc, cqn, ckvn, q, kv, qf, kf, vf

    return _mm_rows(hn, W["mla_w_in"], extras=list(tabs),
                    fulls=[W["mla_q_a_norm"], W["mla_kv_a_norm"], W["mla_w_uq"], W["mla_w_ukv"], gq, W["mla_k_norm"]],
                    outs=[(MLA_IN_PAD, F32), (MLA_Q_RANK, BF16), (MLA_KV_RANK, BF16), (wide, BF16), (wide, BF16),
                          (wide, BF16), (wide, BF16), (MLA_HEADS * MLA_VD, BF16)],
                    epilogue=epilogue, name=name, tm=256)


def _prep_heads_bwd(qv, kvv, kr, ct, sa, sb, dqv, dkv, dvv, gqv, gkv):
    dqs, dkvs = [], []
    dkr = jnp.zeros_like(kr)
    dgq = jnp.zeros((1, MLA_HP), F32)
    dgk = jnp.zeros((1, MLA_HP), F32)
    for h in range(MLA_HEADS):
        b = h * MLA_HP
        dy = jnp.concatenate([dqv[:, b:b + 128], _rope_blk_bwd(dqv[:, b + 128:b + 256], ct, sa, sb)], axis=-1)
        dx, dg = _rms_bwd(qv[:, b:b + MLA_HP], dy, gqv, MLA_QKD, mxu=True)
        dqs.append(dx)
        dgq = dgq + _colsum(dg)
        dy = jnp.concatenate([dkv[:, b:b + 128], _rope_blk_bwd(dkv[:, b + 128:b + 256], ct, sa, sb)], axis=-1)
        dx, dg = _rms_bwd(jnp.concatenate([kvv[:, b:b + 128], kr], axis=-1), dy, gkv, MLA_QKD, mxu=True)
        dkvs += [dx[:, :128], dvv[:, h * MLA_VD:(h + 1) * MLA_VD].astype(F32)]
        dkr = dkr + dx[:, 128:]
        dgk = dgk + _colsum(dg)
    return jnp.concatenate(dqs, axis=-1), jnp.concatenate(dkvs, axis=-1), dkr, dgq, dgk


def _mla_back(q, kv, proj, h0, dh1, dqf, dkf, dvf, W, tabs, name):
    def fn(qv, kvv, pv, hv, dr, ct, sa, sb, dqv, dkv, dvv, gqv, gkv, gqa, gkva, wuq, wukv, w_in, g_mix):
        qv, kvv, dqv, dkv = (t.astype(F32) for t in (qv, kvv, dqv, dkv))
        dq, dkvx, dkr, dgq, dgk = _prep_heads_bwd(qv, kvv, pv[:, MLA_IN_PAD - 128:], ct, sa, sb, dqv, dkv, dvv, gqv, gkv)
        dq, dkvx = dq.astype(BF16), dkvx.astype(BF16)
        nq = wuq.shape[2]
        dcq = sum(_dot_nt(dq[:, s * nq:(s + 1) * nq], wuq[s]) for s in range(N_CHIPS))
        dckv = sum(_dot_nt(dkvx[:, s * nq:(s + 1) * nq], wukv[s]) for s in range(N_CHIPS))
        dxq, dgqa = _rms_bwd(pv[:, :MLA_Q_RANK], dcq, gqa)
        dxkv, dgkva = _rms_bwd(pv[:, MLA_Q_RANK:MLA_Q_RANK + MLA_KV_RANK], dckv, gkva)
        dproj = jnp.concatenate([dxq, dxkv, dkr], axis=-1).astype(BF16)
        dx, dgm = _rms_bwd(hv, _dot_nt(dproj, w_in), g_mix)
        return (dq, dkvx, dproj, dr + dx, dr + dx, dgq, dgk, _colsum(dgqa), _colsum(dgkva), _colsum(dgm))

    wide = MLA_HEADS * MLA_HP
    return _rows(fn, [q, kv, proj, h0, dh1, *tabs, dqf, dkf, dvf],
                 [W["mla_q_norm"], W["mla_k_norm"], W["mla_q_a_norm"], W["mla_kv_a_norm"], W["mla_w_uq"], W["mla_w_ukv"],
                  W["mla_w_in"], W["mix_norm"][1:2]],
                 [(wide, BF16), (wide, BF16), (MLA_IN_PAD, BF16), ROW_F32, ROW_BF16],
                 [((1, MLA_HP), F32), ((1, MLA_HP), F32), ((1, MLA_Q_RANK), F32), ((1, MLA_KV_RANK), F32),
                  ((1, D_MODEL), F32)], name=name, tile=256)


def _chunk_mask(qi, ki, tq, tk):
    shift = CHUNK.bit_length() - 1
    rq = lax.shift_right_arithmetic(qi * tq + lax.broadcasted_iota(jnp.int32, (tq, tk), 0), shift)
    ck = lax.shift_right_arithmetic(ki * tk + lax.broadcasted_iota(jnp.int32, (tq, tk), 1), shift)
    return ck <= rq


def _flash_fwd(qf, kf, vf, name):
    T = qf.shape[0]
    t = _pick(T, FLASH_T)
    n = T // t
    g = FLASH_HEADS

    def body(q_ref, k_ref, v_ref, o_ref, lse_ref, m_s, l_s, acc):
        qi = pl.program_id(1)
        m_s[...] = jnp.full_like(m_s, NEG)
        l_s[...] = jnp.zeros_like(l_s)
        acc[...] = jnp.zeros_like(acc)

        def step(kb, masked):
            rows = pl.ds(pl.multiple_of(kb * t, t), t)
            for h in range(g):
                hq, hv = slice(h * MLA_HP, (h + 1) * MLA_HP), slice(h * MLA_VD, (h + 1) * MLA_VD)
                s = _dot_nt(q_ref[:, hq], k_ref[rows, hq])
                if masked:
                    s = jnp.where(_chunk_mask(0, 0, t, t), s, NEG)
                m_prev = m_s[:, hv]
                m_new = jnp.maximum(m_prev, jnp.max(s, axis=-1, keepdims=True))
                alpha = jnp.exp2(m_prev - m_new)
                p = jnp.exp2(s - _widen(m_new, t))
                l_s[:, hv] = alpha * l_s[:, hv] + sum(p[:, i * 128:(i + 1) * 128] for i in range(t // 128))
                acc[:, hv] = acc[:, hv] * alpha + _dot(p.astype(BF16), v_ref[rows, hv])
                m_s[:, hv] = m_new

        @pl.loop(0, qi)
        def _(kb):
            step(kb, False)

        step(qi, True)
        for h in range(g):
            hv = slice(h * MLA_VD, (h + 1) * MLA_VD)
            l = jnp.sum(l_s[:, hv], axis=-1, keepdims=True)
            o_ref[:, hv] = acc[:, hv] / l
            lse_ref[:, hv] = m_s[:, hv] + jnp.log2(l)

    qmap = lambda h, i: (i, h)
    kmap = lambda h, i: (0, h)
    vec = pltpu.VMEM((t, g * MLA_VD), F32)
    return pl.pallas_call(
        body, name=name, grid=(MLA_HEADS // g, n),
        in_specs=[pl.BlockSpec((t, g * MLA_HP), qmap), pl.BlockSpec((T, g * MLA_HP), kmap),
                  pl.BlockSpec((T, g * MLA_VD), kmap)],
        out_specs=[pl.BlockSpec((t, g * MLA_VD), qmap), pl.BlockSpec((t, g * MLA_VD), qmap)],
        out_shape=[jax.ShapeDtypeStruct((T, MLA_HEADS * MLA_VD), F32),
                   jax.ShapeDtypeStruct((T, MLA_HEADS * MLA_VD), F32)],
        scratch_shapes=[vec, vec, vec],
        compiler_params=_cparams(("parallel", "arbitrary")),
    )(qf, kf, vf)


def _flash_bwd(qf, kf, vf, do16, lse, delta, name):
    T = qf.shape[0]
    t = _pick(T, FLASH_T)
    n = T // t
    g = FLASH_BWD_HEADS
    scale = MLA_QKD ** -0.5

    def body(q_ref, k_ref, v_ref, do_ref, lse_ref, dl_ref, dq_out, dk_out, dv_out, dq_ref, dk_ref, dv_ref):
        kb = pl.program_id(1)

        @pl.when(kb == 0)
        def _():
            dq_ref[...] = jnp.zeros_like(dq_ref)

        dk_ref[...] = jnp.zeros_like(dk_ref)
        dv_ref[...] = jnp.zeros_like(dv_ref)

        def step(qb, masked):
            rows = pl.ds(pl.multiple_of(qb * t, t), t)
            for h in range(g):
                hq, hv = slice(h * MLA_HP, (h + 1) * MLA_HP), slice(h * MLA_VD, (h + 1) * MLA_VD)
                q, dob, k, v = q_ref[rows, hq], do_ref[rows, hv], k_ref[:, hq], v_ref[:, hv]
                s = _dot_nt(q, k)
                if masked:
                    s = jnp.where(_chunk_mask(0, 0, t, t), s, NEG)
                p = jnp.exp2(s - _widen(lse_ref[rows, hv], t))
                ds = (p * (_dot_nt(dob, v) - _widen(dl_ref[rows, hv], t))).astype(BF16)
                dv_ref[:, hv] += _dot_tn(p.astype(BF16), dob)
                dk_ref[:, hq] += _dot_tn(ds, q)
                dq_ref[rows, hq] += _dot(ds, k)

        step(kb, True)

        @pl.loop(kb + 1, n)
        def _(qb):
            step(qb, False)

        dk_out[...] = (dk_ref[...] * (1.0 / LOG2E)).astype(BF16)
        dv_out[...] = dv_ref[...].astype(BF16)

        @pl.when(kb == n - 1)
        def _():
            dq_out[...] = (dq_ref[...] * scale).astype(BF16)

    qmap = lambda h, j: (0, h)
    kmap = lambda h, j: (j, h)
    wq, wv = g * MLA_HP, g * MLA_VD
    return pl.pallas_call(
        body, name=name, grid=(MLA_HEADS // g, n),
        in_specs=[pl.BlockSpec((T, wq), qmap), pl.BlockSpec((t, wq), kmap), pl.BlockSpec((t, wv), kmap),
                  pl.BlockSpec((T, wv), qmap), pl.BlockSpec((T, wv), qmap), pl.BlockSpec((T, wv), qmap)],
        out_specs=[pl.BlockSpec((T, wq), qmap), pl.BlockSpec((t, wq), kmap), pl.BlockSpec((t, wv), kmap)],
        out_shape=[jax.ShapeDtypeStruct((T, MLA_HEADS * MLA_HP), BF16),
                   jax.ShapeDtypeStruct((T, MLA_HEADS * MLA_HP), BF16),
                   jax.ShapeDtypeStruct((T, MLA_HEADS * MLA_VD), BF16)],
        scratch_shapes=[pltpu.VMEM((T, wq), F32), pltpu.VMEM((t, wq), F32), pltpu.VMEM((t, wv), F32)],
        compiler_params=_cparams(("arbitrary", "arbitrary")),
    )(qf, kf, vf, do16, lse, delta)


MESH = pl.DeviceIdType.MESH
ANY = pl.BlockSpec(memory_space=pl.ANY)
_CHIP_FLIPS = ((1, 0), (0, 1), (1, 1))


def _place():
    return lax.axis_index("x"), lax.axis_index("y"), lax.axis_index("c")


def _other_chip(x, y, k):
    fx, fy = _CHIP_FLIPS[k]
    return ((1 - x) if fx else x), ((1 - y) if fy else y)


def _remote(src, dst, send_sems, recv_sems, k, to):
    return pltpu.make_async_remote_copy(src_ref=src, dst_ref=dst, send_sem=send_sems.at[k], recv_sem=recv_sems.at[k],
                                        device_id=to, device_id_type=MESH)


def _index(*vals):
    return jnp.stack(vals).astype(jnp.int32)


def _half(c, rows):
    return pl.ds(pl.multiple_of(c * rows, 16), rows)


def _gather_weights(parts, name, landed=None):
    n_w = len(parts)
    n_in = n_w if landed is None else 2 * n_w

    def body(*refs):
        ins, outs = refs[:n_w], refs[n_in:n_in + n_w]
        send_sems, recv_sems, local_sems = refs[n_in + n_w:]
        x, y, c = _place()
        j = 2 * x + y
        sibling = (x, y, 1 - c)
        chips = [_other_chip(x, y, k) for k in range(3)]
        pending = []
        for w in range(n_w):
            own = pltpu.make_async_copy(ins[w], outs[w].at[j], local_sems.at[w])
            own.start()
            pending.append(own)
        sent = []
        for w in range(n_w):
            if landed is not None:
                break
            r = _half(c, parts[w].shape[0] // 2)
            for k, (px, py) in enumerate(chips):
                cp = _remote(ins[w].at[r], outs[w].at[j, r], send_sems, recv_sems, 6 * w + k, (px, py, c))
                cp.start()
                sent.append(cp)
        for w in range(n_w):
            r = _half(c, parts[w].shape[0] // 2)
            for k, (px, py) in enumerate(chips):
                blk = outs[w].at[2 * px + py, r]
                if landed is None:
                    _remote(blk, blk, send_sems, recv_sems, 6 * w + k, (px, py, c)).wait_recv()
                cp = _remote(blk, blk, send_sems, recv_sems, 6 * w + 3 + k, sibling)
                cp.start()
                sent.append(cp)
        for w in range(n_w):
            r = _half(1 - c, parts[w].shape[0] // 2)
            for k, (px, py) in enumerate(chips):
                blk = outs[w].at[2 * px + py, r]
                _remote(blk, blk, send_sems, recv_sems, 6 * w + 3 + k, sibling).wait_recv()
        for cp in sent:
            cp.wait_send()
        for cp in pending:
            cp.wait()

    return pl.pallas_call(
        body, name=name, in_specs=[pl.BlockSpec(memory_space=pltpu.VMEM)] * n_w + [ANY] * (n_in - n_w),
        out_specs=[ANY] * n_w,
        out_shape=[jax.ShapeDtypeStruct((N_CHIPS, *p.shape), p.dtype) for p in parts],
        input_output_aliases={} if landed is None else {n_w + w: w for w in range(n_w)},
        scratch_shapes=[pltpu.SemaphoreType.DMA((6 * n_w,)), pltpu.SemaphoreType.DMA((6 * n_w,)),
                        pltpu.SemaphoreType.DMA((n_w,))],
        compiler_params=pltpu.CompilerParams(vmem_limit_bytes=VMEM_LIMIT),
    )(*parts, *(landed or []))


def _swap_halves(gs, name):
    n_w = len(gs)

    def body(*refs):
        g_refs, recv_refs = refs[:n_w], refs[n_w:2 * n_w]
        send_sems, recv_sems = refs[2 * n_w:]
        x, y, c = _place()
        sent = []
        for w in range(n_w):
            for jj in range(N_CHIPS):
                cp = _remote(g_refs[w].at[jj, 1 - c], recv_refs[w].at[jj], send_sems, recv_sems, N_CHIPS * w + jj,
                             (x, y, 1 - c))
                cp.start()
                sent.append(cp)
        for cp in sent:
            cp.wait()

    return pl.pallas_call(
        body, name=name, in_specs=[ANY] * n_w, out_specs=[ANY] * n_w,
        out_shape=[jax.ShapeDtypeStruct((N_CHIPS, *g.shape[2:]), g.dtype) for g in gs],
        scratch_shapes=[pltpu.SemaphoreType.DMA((N_CHIPS * n_w,)), pltpu.SemaphoreType.DMA((N_CHIPS * n_w,))],
    )(*gs)


def _pair_sum(g, recv, core, name):
    _, H, C = recv.shape
    tile = _pick(H, SUM_ROWS)

    def body(c_ref, own_ref, recv_ref, out_ref):
        out_ref[...] = (own_ref[...].astype(F32) + recv_ref[...].astype(F32)).astype(BF16)

    blk = pl.BlockSpec((None, tile, C), lambda jj, i, c: (jj, i, 0))
    return pl.pallas_call(
        body, name=name,
        grid_spec=pltpu.PrefetchScalarGridSpec(
            num_scalar_prefetch=1, grid=(N_CHIPS, H // tile),
            in_specs=[pl.BlockSpec((None, None, tile, C), lambda jj, i, c: (jj, c[0], i, 0)), blk],
            out_specs=blk),
        out_shape=jax.ShapeDtypeStruct((N_CHIPS, H, C), BF16),
        compiler_params=_cparams(("arbitrary", "arbitrary")),
    )(_index(core), g, recv)


def _chip_sum(g, recv, got, chip, core, name):
    _, H, C = recv.shape
    tile = _pick(H, SUM_ROWS)

    def body(s_ref, own_ref, recv_ref, g0_ref, g1_ref, g2_ref, out_ref):
        pair = own_ref[...].astype(F32) + recv_ref[...].astype(F32)
        out_ref[...] = ((pair + g0_ref[...].astype(F32)) + g1_ref[...].astype(F32)) + g2_ref[...].astype(F32)

    def got_spec(k):
        return pl.BlockSpec((None, tile, C), lambda i, s, k=k: (k, i, 0))

    return pl.pallas_call(
        body, name=name,
        grid_spec=pltpu.PrefetchScalarGridSpec(
            num_scalar_prefetch=1, grid=(H // tile,),
            in_specs=[pl.BlockSpec((None, None, tile, C), lambda i, s: (s[0], s[1], i, 0)),
                      pl.BlockSpec((None, tile, C), lambda i, s: (s[0], i, 0)), got_spec(0), got_spec(1), got_spec(2)],
            out_specs=pl.BlockSpec((None, tile, C), lambda i, s: (s[1], i, 0))),
        out_shape=jax.ShapeDtypeStruct((2, H, C), F32),
        compiler_params=_cparams(("arbitrary",)),
    )(_index(chip, core), g, recv, got, got, got)


def _share_halves(reds):
    n_w = len(reds)

    def body(*refs):
        out_refs = refs[n_w:2 * n_w]
        send_sems, recv_sems = refs[2 * n_w:]
        x, y, c = _place()
        sent = []
        for w in range(n_w):
            blk = out_refs[w].at[c]
            cp = _remote(blk, blk, send_sems, recv_sems, w, (x, y, 1 - c))
            cp.start()
            sent.append(cp)
        for cp in sent:
            cp.wait()

    return pl.pallas_call(
        body, name="grad_share_halves", in_specs=[ANY] * n_w, out_specs=[ANY] * n_w,
        out_shape=[jax.ShapeDtypeStruct(r.shape, r.dtype) for r in reds],
        input_output_aliases={w: w for w in range(n_w)},
        scratch_shapes=[pltpu.SemaphoreType.DMA((n_w,)), pltpu.SemaphoreType.DMA((n_w,))],
    )(*reds)


def _allsum_small(v, name):
    R, W = v.shape
    n_dev = 8
    vm = pl.BlockSpec(memory_space=pltpu.VMEM)

    def body(v_ref, out_ref, buf, send_sems, recv_sems):
        x, y, c = _place()
        me = 4 * x + 2 * y + c
        buf[me] = v_ref[...]
        sent = []
        for k in range(1, n_dev):
            peer = ((1 - x) if k & 4 else x, (1 - y) if k & 2 else y, (1 - c) if k & 1 else c)
            cp = _remote(v_ref, buf.at[me], send_sems, recv_sems, k - 1, peer)
            cp.start()
            sent.append(cp)
        for cp in sent:
            cp.wait_recv()
        for cp in sent:
            cp.wait_send()
        acc = buf[0]
        for q in range(1, n_dev):
            acc = acc + buf[q]
        out_ref[...] = acc

    return pl.pallas_call(
        body, name=name, in_specs=[vm], out_specs=vm, out_shape=jax.ShapeDtypeStruct((R, W), v.dtype),
        scratch_shapes=[pltpu.VMEM((n_dev, R, W), v.dtype), pltpu.SemaphoreType.DMA((n_dev - 1,)),
                        pltpu.SemaphoreType.DMA((n_dev - 1,))],
    )(v)


HBM = pl.BlockSpec(memory_space=pltpu.HBM)
SEM = pl.BlockSpec(memory_space=pltpu.SEMAPHORE)
_DATAFLOW = pltpu.SideEffectType.DATAFLOW_SIDE_EFFECTING


def _split_start(name, srcs, land_shapes, n_copies, copies, after=()):
    ns, nl = len(srcs), len(land_shapes)
    lands = [lax.empty(s.shape, s.dtype) for s in land_shapes]

    def body(*refs):
        outs = refs[ns + nl + len(after):]
        for cp in copies(refs[:ns], refs[ns:ns + nl], outs[0], outs[1]):
            cp.start()
        outs[-1][...] = jnp.zeros_like(outs[-1])

    sems = pltpu.SemaphoreType.DMA((n_copies,))
    res = pl.pallas_call(
        body, name=name, in_specs=[HBM] * (ns + nl) + [ANY] * len(after),
        out_specs=(SEM, SEM, *[HBM] * (ns + nl), pl.BlockSpec(memory_space=pltpu.VMEM)),
        out_shape=(sems, sems, *[pltpu.HBM(a.shape, a.dtype) for a in srcs],
                   *[pltpu.HBM(s.shape, s.dtype) for s in land_shapes], jax.ShapeDtypeStruct((8, 128), F32)),
        input_output_aliases={i: 2 + i for i in range(ns + nl)},
        compiler_params=pltpu.CompilerParams(has_side_effects=_DATAFLOW),
    )(*[pltpu.with_memory_space_constraint(a, pltpu.HBM) for a in [*srcs, *lands]], *after)
    return res[0], res[1], list(res[2:2 + ns]), list(res[2 + ns:2 + ns + nl]), res[-1]


def _split_wait(name, send_sems, recv_sems, srcs, lands, copies, after=()):
    ns, nl = len(srcs), len(lands)

    def body(*refs):
        for cp in copies(refs[:ns], refs[ns:ns + nl], refs[ns + nl], refs[ns + nl + 1]):
            cp.wait_send()
            cp.wait_recv()

    res = pl.pallas_call(
        body, name=name, in_specs=[HBM] * (ns + nl) + [SEM, SEM] + [ANY] * len(after), out_specs=[HBM] * (ns + nl),
        out_shape=[pltpu.HBM(a.shape, a.dtype) for a in [*srcs, *lands]],
        input_output_aliases={i: i for i in range(ns + nl)},
        compiler_params=pltpu.CompilerParams(has_side_effects=_DATAFLOW),
    )(*srcs, *lands, send_sems, recv_sems, *after)
    return list(res[:ns]), list(res[ns:])


def _gather_copies(rows):
    def copies(src_refs, land_refs, send_sems, recv_sems):
        x, y, c = _place()
        j = 2 * x + y
        out = []
        for w in range(len(src_refs)):
            r = _half(c, rows[w] // 2)
            for k in range(3):
                px, py = _other_chip(x, y, k)
                out.append(_remote(src_refs[w].at[r], land_refs[w].at[j, r], send_sems, recv_sems, 3 * w + k, (px, py, c)))
        return out
    return copies


def _scatter_copies(src_refs, land_refs, send_sems, recv_sems):
    x, y, c = _place()
    j = 2 * x + y
    out = []
    for w in range(len(src_refs)):
        for k in range(3):
            px, py = _other_chip(x, y, k)
            pj = 2 * px + py
            out.append(_remote(src_refs[w].at[pj], land_refs[w].at[(j - pj + 4) % 4 - 1], send_sems, recv_sems, 3 * w + k,
                               (px, py, c)))
    return out


def _halves(grads):
    names = list(grads)
    return names, [grads[k].reshape(N_CHIPS, 2, -1, grads[k].shape[-1]) for k in names]


def _reduce_begin(grads, core, tag):
    names, gs = _halves(grads)
    recvs = _swap_halves(gs, f"grad_swap_halves_{tag}")
    sums = [_pair_sum(g, r, core, f"pair_sum_{k}") for k, g, r in zip(names, gs, recvs)]
    return names, gs, recvs, sums


def _swap_copies(src_refs, land_refs, send_sems, recv_sems):
    x, y, c = _place()
    return [_remote(src_refs[w].at[jj, 1 - c], land_refs[w].at[jj], send_sems, recv_sems, N_CHIPS * w + jj, (x, y, 1 - c))
            for w in range(len(src_refs)) for jj in range(N_CHIPS)]


def _swap_begin(grads, tag):
    names, gs = _halves(grads)
    started = _split_start(f"swap_{tag}_start", gs, [jax.ShapeDtypeStruct((N_CHIPS, *g.shape[2:]), g.dtype) for g in gs],
                           N_CHIPS * len(gs), _swap_copies)
    return (names, started[:4]), started[4]


def _swap_end(begun, core, tag, after):
    names, started = begun
    gs, recvs = _split_wait(f"swap_{tag}_wait", *started, _swap_copies, after=after)
    sums = [_pair_sum(g, r, core, f"pair_sum_{k}") for k, g, r in zip(names, gs, recvs)]
    return names, gs, recvs, sums


def _reduce_end(begun, gots, chip, core):
    names, gs, recvs, _ = begun
    return {k: _chip_sum(g, r, t, chip, core, f"chip_sum_{k}") for k, g, r, t in zip(names, gs, recvs, gots)}


def _got_shapes(sums):
    return [jax.ShapeDtypeStruct((3, *a.shape[1:]), a.dtype) for a in sums]


def _adamw(w, g, m, v, name, layers=1, layer=0, into=None):
    shape = w.shape
    cols = shape[-1]
    w3, m3, v3 = (t.reshape(layers, -1, cols) for t in (w, m, v))
    rows = w3.shape[1]
    tile = _pick(rows, ADAM_ROWS if cols <= 1024 else ADAM_ROWS // 2) if rows % 8 == 0 else rows
    n_in = 4 + (0 if into is None else 4)
    stack_g = layers > 1

    def body(*refs):
        wv, gv, mv, vv = (r[...] for r in refs[:4])
        d_ref, m_ref, v_ref = refs[len(refs) - 3:]
        m2 = ADAM_B1 * mv + (1.0 - ADAM_B1) * gv
        v2 = ADAM_B2 * vv + (1.0 - ADAM_B2) * jnp.square(gv)
        m_hat = m2 / (1.0 - ADAM_B1 ** ADAM_STEP)
        v_hat = v2 / (1.0 - ADAM_B2 ** ADAM_STEP)
        if stack_g:
            refs[n_in][...] = gv
        d_ref[...] = -ADAM_LR * (m_hat / (jnp.sqrt(v_hat) + ADAM_EPS) + ADAM_WD * wv)
        m_ref[...] = m2
        v_ref[...] = v2

    n_out = 4 if stack_g else 3
    lay = pl.BlockSpec((None, tile, cols), lambda i: (layer, i, 0))
    out = jax.ShapeDtypeStruct((layers, rows, cols), F32)
    res = pl.pallas_call(
        body, name=name, grid=(rows // tile,),
        in_specs=[lay, pl.BlockSpec((tile, cols), lambda i: (i, 0)), lay, lay] + [ANY] * (n_in - 4),
        out_specs=[lay] * n_out, out_shape=[out] * n_out,
        input_output_aliases={} if into is None else {4 + k: k for k in range(4)},
        compiler_params=_cparams(("arbitrary",)),
    )(w3, g.reshape(rows, cols), m3, v3, *([] if into is None else [t.reshape(layers, rows, cols) for t in into]))
    res = tuple(t.reshape(shape) for t in res)
    return res if stack_g else (g.reshape(shape), *res)


ROW_F32, ROW_BF16 = (D_MODEL, F32), (D_MODEL, BF16)


def _res_norm(acc, h, gain):
    hh = h + acc
    return hh, _rms(hh, gain)


def _dx_norm_bwd(d, w, h, dres, gain, name, **kw):
    def epilogue(acc, hv, dr, g):
        dx, dg = _rms_bwd(hv, acc, g)
        return dr + dx, dr + dx, _colsum(dg)
    return _mm_rows(d, w, tb=True, extras=[h, dres], fulls=[gain], outs=[ROW_F32, ROW_BF16], accs=[((1, D_MODEL), F32)],
                    epilogue=epilogue, name=name, **kw)


def _tail_fwd(h1, hn2, p16, W, i, tag, next_gain=None, target=None):
    a = _mm(hn2, W["mlp_w1"][i], bblk=True, outs=[BF16], name=f"{tag}_mlp_w1", tm=2048, tn=1024,
            epilogue=lambda acc: (jnp.square(jnp.maximum(acc, 0.0)),))
    h2, hn3 = _mm_rows(a, W["mlp_w2"][i], extras=[h1], fulls=[W["ple_norm"][i:i + 1]], outs=[ROW_F32, ROW_BF16],
                       epilogue=_res_norm, name=f"{tag}_mlp_w2")
    def embed(acc, pv, h, wp):
        gate = _sigmoid(acc)
        ppv = jnp.concatenate([_dot(pv, wp[s]) for s in range(N_CHIPS)], axis=-1)
        return gate, ppv, h + gate * ppv

    if target is None:
        def gated(acc, pv, h, wp, gain):
            gate, ppv, hh = embed(acc, pv, h, wp)
            return hh, ppv, gate, _rms(hh, gain)
        h3, pp, gate, hn = _mm_rows(hn3, W["ple_gate_w"][i], extras=[p16[i], h2], fulls=[W["ple_proj_w"][i], next_gain],
                                    outs=[ROW_F32, ROW_BF16, ROW_BF16, ROW_BF16], epilogue=gated, name=f"{tag}_ple")
        return h3, hn, (h1, hn2, a, h2, hn3, gate, pp)

    def gated_loss(acc, pv, h, t, wp):
        gate, ppv, hh = embed(acc, pv, h, wp)
        e = hh - t
        return ppv, gate, e * (1.0 / D_MODEL), jnp.full((1, 128), 0.5 / D_MODEL, F32) * jnp.sum(e * e)
    pp, gate, dy, loss = _mm_rows(hn3, W["ple_gate_w"][i], extras=[p16[i], h2, target], fulls=[W["ple_proj_w"][i]],
                                  outs=[ROW_BF16, ROW_BF16, ROW_F32], accs=[((1, 128), F32)], epilogue=gated_loss,
                                  name=f"{tag}_ple")
    return dy, loss, (h1, hn2, a, h2, hn3, gate, pp)


def _tail_bwd(dh3, saved, p16, W, i, tag, after=(), hook=None):
    h1, hn2, a, h2, hn3, gate, pp = saved

    def embed_bwd(d, g, ppv, hv, wg, gain):
        g, ppv = g.astype(F32), ppv.astype(F32)
        dppv, dglv = (d * g).astype(BF16), (d * ppv * g * (1.0 - g)).astype(BF16)
        dx, dg = _rms_bwd(hv, _dot_nt(dglv, wg), gain)
        return dppv, dglv, d + dx, d + dx, _colsum(dg)

    def dw(kind, name):
        return (kind, 1, 0, None)

    dpp, dgl, dh2, dh2_16, d_ple_norm = _rows(
        embed_bwd, [dh3, gate, pp, h2], [W["ple_gate_w"][i], W["ple_norm"][i:i + 1]],
        [ROW_BF16, ROW_BF16, ROW_F32, ROW_BF16], [((1, D_MODEL), F32)], name=f"{tag}_ple_bwd", after=after)
    later = () if hook is None else hook(dh2_16)
    d_proj = _mm(p16[i], dpp, ta=True, outs=[BF16], dw=dw("cols", "ple_proj_w"), name=f"{tag}_d_ple_proj", after=later)
    d_gate = _mm(hn3, dgl, ta=True, outs=[BF16], dw=dw("rows", "ple_gate_w"), name=f"{tag}_d_ple_gate")
    d_w2 = _mm(a, dh2_16, ta=True, outs=[BF16], dw=dw("rows", "mlp_w2"), name=f"{tag}_d_mlp_w2", tn=1024)
    dz = _mm(dh2_16, W["mlp_w2"][i], tb=True, extras=[a], outs=[BF16], name=f"{tag}_mlp_w2_dx", tm=2048, tn=1024,
             epilogue=lambda acc, av: (acc * (2.0 * jnp.sqrt(av.astype(F32))),))
    d_w1 = _mm(hn2, dz, ta=True, outs=[BF16], dw=dw("cols", "mlp_w1"), name=f"{tag}_d_mlp_w1", tn=1024)
    dh1, dh1_16, d_mlp_norm = _dx_norm_bwd(dz, W["mlp_w1"][i], h1, dh2, W["mlp_norm"][i:i + 1], f"{tag}_mlp_w1_dx",
                                           bblk=True)
    big = {f"mlp_w1_{i}": d_w1, f"mlp_w2_{i}": d_w2, f"ple_gate_w_{i}": d_gate, f"ple_proj_w_{i}": d_proj}
    return dh1, dh1_16, big, dict(mlp_norm=d_mlp_norm, ple_norm=d_ple_norm)


def _ret_layer_fwd(h0, W, tabs, after=(), before_out=None):
    hn = _rows(lambda x, g: (_rms(x, g),), [h0], [W["mix_norm"][0:1]], [(D_MODEL, BF16)], name="ret_mix_norm",
               after=after)[0]
    proj = _mm(hn, W["ret_w_in"], bblk=True, outs=[BF16], name="ret_w_in", tm=2048, tn=768)
    out, states = _ret_fwd(proj, tabs, "ret_scan")
    y = _ret_gate(out, proj, W["ret_gn"], "ret_gate")
    if before_out is not None:
        before_out(y)
    h1, hn2 = _mm_rows(y, W["ret_w_out"], extras=[h0], fulls=[W["mlp_norm"][0:1]], outs=[ROW_F32, ROW_BF16],
                       epilogue=_res_norm, name="ret_w_out")
    return h1, hn2, (h0, hn, proj, out, states, y)


def _d_ret_w_out(dh1_16, saved):
    return _mm(saved[5], dh1_16, ta=True, outs=[BF16], dw=("rows", 1, 0, None), name="d_ret_w_out")


def _ret_layer_bwd(dh1, dh1_16, saved, W, tabs, after=(), hook=None, on_grads=None, d_w_out=None):
    h0, hn, proj, out, states, y = saved
    d_w_out = _d_ret_w_out(dh1_16, saved) if d_w_out is None else d_w_out
    dy = _mm(dh1_16, W["ret_w_out"], tb=True, name="ret_w_out_dx", tn=1024, after=after)
    dout, dproj, d_gn = _ret_gate_bwd(out, proj, W["ret_gn"], dy, "ret_gate_bwd",
                                      after=() if hook is None else hook(dy))
    dproj = _ret_bwd(proj, states, dout, dproj, tabs, "ret_scan_bwd")
    d_w_in = _mm(hn, dproj, ta=True, outs=[BF16], dw=("cols", 1, 0, None), name="d_ret_w_in", tn=768)
    big = dict(ret_w_in=d_w_in, ret_w_out=d_w_out)
    later = () if on_grads is None else on_grads(big)
    dh0, _, d_mix = _dx_norm_bwd(dproj, W["ret_w_in"], h0, dh1, W["mix_norm"][0:1], "ret_w_in_dx", bblk=True, tm=512,
                                 after=later)
    return dh0, big, dict(mix_norm=d_mix, ret_gn=d_gn)


def _mla_layer_fwd(h0, hn, W, tabs):
    proj, cqn, ckvn, q, kv, qf, kf, vf = _mla_front(hn, W, tabs, "mla_front")
    o, lse = _flash_fwd(qf, kf, vf, "mla_flash")
    h1, hn2 = _mm_rows(o, W["mla_w_out"], extras=[h0], fulls=[W["mlp_norm"][1:2]], outs=[ROW_F32, ROW_BF16],
                       epilogue=_res_norm, name="mla_w_out")
    return h1, hn2, (h0, hn, proj, cqn, ckvn, q, kv, qf, kf, vf, o, lse)


def _mla_layer_bwd(dh1, dh1_16, saved, W, tabs):
    h0, hn, proj, cqn, ckvn, q, kv, qf, kf, vf, o, lse = saved
    d_w_out = _mm(o, dh1_16, ta=True, outs=[BF16], dw=("rows", 1, 0, None), name="d_mla_w_out")
    def with_delta(acc, ov):
        parts = []
        for h in range(MLA_HEADS):
            sl = slice(h * MLA_VD, (h + 1) * MLA_VD)
            d = jnp.sum(acc[:, sl] * ov[:, sl], axis=-1, keepdims=True)
            parts.append(jnp.broadcast_to(d, (d.shape[0], MLA_VD)))
        return jnp.concatenate(parts, axis=-1), acc

    delta, do16 = _mm_rows(dh1_16, W["mla_w_out"], tb=True, extras=[o], outs=[ROW_F32, ROW_BF16], epilogue=with_delta,
                           name="mla_w_out_dx")
    dqf, dkf, dvf = _flash_bwd(qf, kf, vf, do16, lse, delta, "mla_flash_bwd")
    dq, dkv, dproj, dh0, dh0_16, d_gq, d_gk, d_gqa, d_gkva, d_mix = _mla_back(q, kv, proj, h0, dh1, dqf, dkf, dvf, W, tabs,
                                                                              "mla_back")
    d_w_uq = _mm(cqn, dq, ta=True, outs=[BF16], dw=("cols", 1, 0, None), name="d_mla_w_uq")
    d_w_ukv = _mm(ckvn, dkv, ta=True, outs=[BF16], dw=("cols", 1, 0, None), name="d_mla_w_ukv")
    d_w_in = _mm(hn, dproj, ta=True, outs=[BF16], dw=("rows", 1, 0, None), name="d_mla_w_in")
    return (dh0, dh0_16, dict(mla_w_in=d_w_in, mla_w_uq=d_w_uq, mla_w_ukv=d_w_ukv, mla_w_out=d_w_out),
            dict(mix_norm=d_mix, mla_q_a_norm=d_gqa, mla_kv_a_norm=d_gkva, mla_q_norm=d_gq, mla_k_norm=d_gk))


def _small_grads(n_ret, n_t0, n_mla, n_t1):
    return dict(
        mix_norm=jnp.concatenate([n_ret["mix_norm"], n_mla["mix_norm"]], axis=0),
        mlp_norm=jnp.concatenate([n_t0["mlp_norm"], n_t1["mlp_norm"]], axis=0),
        ple_norm=jnp.concatenate([n_t0["ple_norm"], n_t1["ple_norm"]], axis=0),
        ret_gn=n_ret["ret_gn"], mla_q_a_norm=n_mla["mla_q_a_norm"], mla_kv_a_norm=n_mla["mla_kv_a_norm"],
        mla_q_norm=n_mla["mla_q_norm"], mla_k_norm=n_mla["mla_k_norm"])


_ORDER = ("mix_norm", "ret_w_in", "ret_gn", "ret_w_out", "mla_w_in", "mla_q_a_norm", "mla_kv_a_norm", "mla_w_uq",
          "mla_w_ukv", "mla_q_norm", "mla_k_norm", "mla_w_out", "mlp_norm", "mlp_w1", "mlp_w2", "ple_norm",
          "ple_gate_w", "ple_proj_w")
_TWO_LAYER = ("mlp_w1", "mlp_w2", "ple_gate_w", "ple_proj_w")
HEADS_PER_CHIP = MLA_HEADS // N_CHIPS
GAIN_ROWS = 32


def _travel_parts(w):
    uq = jnp.pad(w["mla_w_uq"][0].reshape(MLA_Q_RANK, HEADS_PER_CHIP, MLA_QKD), ((0, 0), (0, 0), (0, MLA_HP - MLA_QKD)))
    parts = {"ret_w_in": w["ret_w_in"][0], "ret_w_out": w["ret_w_out"][0]}
    for k in _TWO_LAYER:
        parts[k + "_0"] = w[k][0]
    parts["mla_w_in"] = jnp.pad(w["mla_w_in"][0], ((0, 0), (0, MLA_IN_PAD - MLA_IN)))
    parts["mla_w_uq"] = uq.reshape(MLA_Q_RANK, HEADS_PER_CHIP * MLA_HP)
    parts["mla_w_ukv"] = w["mla_w_ukv"][0]
    parts["mla_w_out"] = w["mla_w_out"][0]
    for k in _TWO_LAYER:
        parts[k + "_1"] = w[k][1]
    gains = jnp.concatenate([_pad_row(w["ret_gn"]), _pad_row(w["mla_q_a_norm"]), _pad_row(w["mla_kv_a_norm"]),
                             jnp.zeros((GAIN_ROWS - 3, PACK_W), F32)], axis=0)
    return {"gains": gains, **{k: v.astype(BF16) for k, v in parts.items()}}


def _full_weights(full):
    rows = lambda a: a.reshape(-1, a.shape[-1])
    W = {k: full[k] for k in ("ret_w_in", "mla_w_uq", "mla_w_ukv") if k in full}
    for k in ("ret_w_out", "mla_w_in", "mla_w_out"):
        if k in full:
            W[k] = rows(full[k])
    for k, by_rows in (("mlp_w1", False), ("ple_proj_w", False), ("mlp_w2", True), ("ple_gate_w", True)):
        layers = [full.get(f"{k}_{i}") for i in range(2)]
        W[k] = [rows(t) if (by_rows and t is not None) else t for t in layers]
    return W


def _shard_grad(name, red, shape):
    if name == "mla_w_in":
        red = red.reshape(-1, MLA_IN_PAD)[:, :MLA_IN]
    elif name == "mla_w_uq":
        red = red.reshape(MLA_Q_RANK, HEADS_PER_CHIP, MLA_HP)[:, :, :MLA_QKD]
    return red.reshape(shape)


def _pad_row(v):
    v = v.reshape(1, -1)
    return jnp.pad(v, ((0, 0), (0, PACK_W - v.shape[1])))


def kernel(x, p, mix_norm, ret_w_in, ret_gn, ret_w_out, mla_w_in, mla_q_a_norm, mla_kv_a_norm, mla_w_uq, mla_w_ukv, mla_q_norm, mla_k_norm, mla_w_out, mlp_norm, mlp_w1, mlp_w2, ple_norm, ple_gate_w, ple_proj_w, loss_target, m_mix_norm, m_ret_w_in, m_ret_gn, m_ret_w_out, m_mla_w_in, m_mla_q_a_norm, m_mla_kv_a_norm, m_mla_w_uq, m_mla_w_ukv, m_mla_q_norm, m_mla_k_norm, m_mla_w_out, m_mlp_norm, m_mlp_w1, m_mlp_w2, m_ple_norm, m_ple_gate_w, m_ple_proj_w, v_mix_norm, v_ret_w_in, v_ret_gn, v_ret_w_out, v_mla_w_in, v_mla_q_a_norm, v_mla_kv_a_norm, v_mla_w_uq, v_mla_w_ukv, v_mla_q_norm, v_mla_k_norm, v_mla_w_out, v_mlp_norm, v_mlp_w1, v_mlp_w2, v_ple_norm, v_ple_gate_w, v_ple_proj_w):
    w = dict(mix_norm=mix_norm, ret_w_in=ret_w_in, ret_gn=ret_gn, ret_w_out=ret_w_out, mla_w_in=mla_w_in,
             mla_q_a_norm=mla_q_a_norm, mla_kv_a_norm=mla_kv_a_norm, mla_w_uq=mla_w_uq, mla_w_ukv=mla_w_ukv,
             mla_q_norm=mla_q_norm, mla_k_norm=mla_k_norm, mla_w_out=mla_w_out, mlp_norm=mlp_norm, mlp_w1=mlp_w1,
             mlp_w2=mlp_w2, ple_norm=ple_norm, ple_gate_w=ple_gate_w, ple_proj_w=ple_proj_w)
    m = dict(mix_norm=m_mix_norm, ret_w_in=m_ret_w_in, ret_gn=m_ret_gn, ret_w_out=m_ret_w_out, mla_w_in=m_mla_w_in,
             mla_q_a_norm=m_mla_q_a_norm, mla_kv_a_norm=m_mla_kv_a_norm, mla_w_uq=m_mla_w_uq, mla_w_ukv=m_mla_w_ukv,
             mla_q_norm=m_mla_q_norm, mla_k_norm=m_mla_k_norm, mla_w_out=m_mla_w_out, mlp_norm=m_mlp_norm,
             mlp_w1=m_mlp_w1, mlp_w2=m_mlp_w2, ple_norm=m_ple_norm, ple_gate_w=m_ple_gate_w, ple_proj_w=m_ple_proj_w)
    v = dict(mix_norm=v_mix_norm, ret_w_in=v_ret_w_in, ret_gn=v_ret_gn, ret_w_out=v_ret_w_out, mla_w_in=v_mla_w_in,
             mla_q_a_norm=v_mla_q_a_norm, mla_kv_a_norm=v_mla_kv_a_norm, mla_w_uq=v_mla_w_uq, mla_w_ukv=v_mla_w_ukv,
             mla_q_norm=v_mla_q_norm, mla_k_norm=v_mla_k_norm, mla_w_out=v_mla_w_out, mlp_norm=v_mlp_norm,
             mlp_w1=v_mlp_w1, mlp_w2=v_mlp_w2, ple_norm=v_ple_norm, ple_gate_w=v_ple_gate_w, ple_proj_w=v_ple_proj_w)
    xi, yi, ci = _place()
    chip = 2 * xi + yi
    n = N_CHIPS

    parts = _travel_parts(w)
    first = ("gains", "ret_w_in")
    mid = ["ret_w_out"] + [k + "_0" for k in _TWO_LAYER]
    last = [k for k in parts if k not in first and k not in mid]
    full = dict(zip(first, _gather_weights([parts[k] for k in first], "gather_first")))

    def gather_behind(names, tag, after):
        copies = _gather_copies([parts[k].shape[0] for k in names])
        started = _split_start(f"gather_{tag}_start", [parts[k] for k in names],
                               [jax.ShapeDtypeStruct((n, *parts[k].shape), BF16) for k in names], 3 * len(names),
                               copies, after=after)

        def arrive(after):
            _, landed = _split_wait(f"gather_{tag}_wait", *started[:4], copies, after=after)
            full.update(zip(names, _gather_weights([parts[k] for k in names], f"gather_{tag}_finish", landed=landed)))
            W.update(_full_weights(full))
        return started[4], arrive

    mid_token, mid_arrive = gather_behind(mid, "mid", [full["ret_w_in"]])
    g_token, last_arrive = gather_behind(last, "last", [mid_token])
    gains = full["gains"]
    W = dict(mix_norm=mix_norm, mlp_norm=mlp_norm, ple_norm=ple_norm,
             mla_q_norm=jnp.pad(mla_q_norm, ((0, 0), (0, MLA_HP - MLA_QKD))),
             mla_k_norm=jnp.pad(mla_k_norm, ((0, 0), (0, MLA_HP - MLA_QKD))),
             ret_w_in=full["ret_w_in"],
             ret_gn=gains[:, 0, :RET_HEADS * 128].reshape(n, RET_HEADS, 128).transpose(1, 0, 2).reshape(RET_HEADS, RET_DV),
             mla_q_a_norm=gains[:, 1, :MLA_Q_RANK // n].reshape(1, MLA_Q_RANK),
             mla_kv_a_norm=gains[:, 2, :MLA_KV_RANK // n].reshape(1, MLA_KV_RANK))
    x0, p16, target = x[0], p[:, 0].astype(BF16), loss_target[0]
    T = x0.shape[0]
    ret_tabs, mla_tabs = _ret_tables(T), _mla_tables(T)

    h1, hn, s_ret = _ret_layer_fwd(x0, W, ret_tabs, after=[g_token], before_out=lambda y: mid_arrive([y]))
    h3, hn, s_tail0 = _tail_fwd(h1, hn, p16, W, 0, "l0", next_gain=W["mix_norm"][1:2])
    last_arrive([h3])
    h4, hn, s_mla = _mla_layer_fwd(h3, hn, W, mla_tabs)
    dy, loss, s_tail1 = _tail_fwd(h4, hn, p16, W, 1, "l1", target=target)

    dh4, dh4_16, g_t1, n_t1 = _tail_bwd(dy, s_tail1, p16, W, 1, "l1")
    dh3, _, g_mla, n_mla = _mla_layer_bwd(dh4, dh4_16, s_mla, W, mla_tabs)
    stages = {}

    def scatter_start(tag, begun):
        started = _split_start(f"scatter_{tag}_start", begun[3], _got_shapes(begun[3]), 3 * len(begun[3]), _scatter_copies)
        stages[tag] = (begun, started[:4])
        return [started[4]]

    def scatter_end(tag, after):
        begun, started = stages[tag]
        return _reduce_end(begun, _split_wait(f"scatter_{tag}_wait", *started, _scatter_copies, after=after)[1], chip, ci)

    swap_a, token = _swap_begin({**g_mla, **g_t1}, "a")
    dh1, dh1_16, g_t0, n_t0 = _tail_bwd(dh3, s_tail0, p16, W, 0, "l0", after=[token],
                                        hook=lambda t: scatter_start("a", _swap_end(swap_a, ci, "a", [t])))
    d_ret_w_out = _d_ret_w_out(dh1_16, s_ret)
    swap_b, token = _swap_begin({**g_t0, "ret_w_out": d_ret_w_out}, "b")
    dx, _, n_ret = _ret_layer_bwd(
        dh1, dh1_16, s_ret, W, ret_tabs, after=[token], d_w_out=d_ret_w_out,
        hook=lambda t: scatter_start("b", _swap_end(swap_b, ci, "b", [t])),
        on_grads=lambda g: scatter_start("c", _reduce_begin({"ret_w_in": g["ret_w_in"]}, ci, "c")))
    red = {**scatter_end("a", [dx]), **scatter_end("b", [dx]), **scatter_end("c", [dx])}
    red = dict(zip(red, _share_halves(list(red.values()))))
    gs = _small_grads(n_ret, n_t0, n_mla, n_t1)
    small_g = jnp.concatenate([
        gs["mix_norm"], gs["mlp_norm"], gs["ple_norm"], gs["ret_gn"].reshape(2, PACK_W), _pad_row(gs["mla_q_a_norm"]),
        _pad_row(gs["mla_kv_a_norm"]), _pad_row(gs["mla_q_norm"][:, :MLA_QKD]), _pad_row(gs["mla_k_norm"][:, :MLA_QKD]),
        _pad_row(loss[:, :1]), jnp.zeros((3, PACK_W), F32)], axis=0)
    tot = _allsum_small(small_g, "sum_small_grads")
    gn_all = tot[6:8].reshape(RET_HEADS, n, -1)
    g_small = dict(
        mix_norm=tot[0:2], mlp_norm=tot[2:4], ple_norm=tot[4:6],
        ret_gn=lax.dynamic_index_in_dim(gn_all, chip, axis=1, keepdims=False),
        mla_q_a_norm=lax.dynamic_index_in_dim(tot[8, :MLA_Q_RANK].reshape(n, -1), chip, axis=0, keepdims=True),
        mla_kv_a_norm=lax.dynamic_index_in_dim(tot[9, :MLA_KV_RANK].reshape(n, -1), chip, axis=0, keepdims=True),
        mla_q_norm=tot[10:11, :MLA_QKD], mla_k_norm=tot[11:12, :MLA_QKD])
    loss_out = tot[12, 0]

    outs = []
    for k in _ORDER:
        if k in _TWO_LAYER:
            res = None
            for i in (1, 0):
                res = _adamw(w[k], red[f"{k}_{i}"], m[k], v[k], f"adamw_{k}_{i}", layers=2, layer=i, into=res)
        elif k in red:
            res = _adamw(w[k], _shard_grad(k, red[k], w[k].shape), m[k], v[k], f"adamw_{k}")
        else:
            res = _adamw(w[k], g_small[k], m[k], v[k], f"adamw_{k}")
        outs.append(res)
    return (loss_out, dx[None], *[o[0] for o in outs], *[o[1] for o in outs], *[o[2] for o in outs],
            *[o[3] for o in outs])
```

```python
import jax
import jax.numpy as jnp
import numpy as np
from jax import lax
from jax.experimental import pallas as pl
from jax.experimental.pallas import tpu as pltpu

F32 = jnp.float32
BF16 = jnp.bfloat16

EPS = 1e-6
D_MODEL = 1024
CHUNK = 64
ROPE_THETA = 10000.0
RET_HEADS = 4
RET_DK = 256
RET_DV = 512
RET_GROUP = 1
RET_BLOCK = 256
RET_ROWS = 1024
MLA_HEADS = 8
MLA_ROPE = 64
MLA_QKD = 192
MLA_VD = 128
MLA_HP = 256
MLA_Q_RANK = 384
MLA_KV_RANK = 256
MLA_IN = 704
MLA_IN_PAD = 768
N_CHIPS = 4

ADAM_LR = 0.001
ADAM_B1 = 0.9
ADAM_B2 = 0.999
ADAM_EPS = 1e-08
ADAM_WD = 0.01
ADAM_STEP = 10

VMEM_LIMIT = 56 * 1024 * 1024
PACK_W = 1024
NEG = -1e30
LOG2E = 1.4426950408889634
FLASH_T = 512
FLASH_HEADS = 4
FLASH_BWD_HEADS = 2
MM_SUB_ROWS = 256
SUM_ROWS = 512
ADAM_ROWS = 512


def _cparams(sem=None):
    return pltpu.CompilerParams(dimension_semantics=sem, vmem_limit_bytes=VMEM_LIMIT)


def _pick(dim, pref):
    if dim <= pref:
        return dim
    t = pref
    while dim % t:
        t //= 2
    return t


def _mm(a, b, *, name, ta=False, tb=False, bblk=False, outs=None, extras=(), epilogue=None, dw=None,
        tm=1024, tn=512, after=()):
    if ta:
        K, M = a.shape
    else:
        M, K = a.shape
    if bblk and tb:
        nb, N, Kq = b.shape
        assert nb * Kq == K
    elif bblk:
        nb, Kb, Nq = b.shape
        N = nb * Nq
        assert Kb == K
    else:
        N = b.shape[0] if tb else b.shape[1]
    tn = _pick(Nq if (bblk and not tb) else N, tn)
    if dw is not None and dw[0] == "cols":
        tn = _pick(N // N_CHIPS, tn)
    tm = _pick(M // N_CHIPS if (dw is not None and dw[0] == "rows") else M, tm)
    grid = (M // tm, N // tn)

    a_spec = pl.BlockSpec((K, tm), lambda i, j: (0, i)) if ta else pl.BlockSpec((tm, K), lambda i, j: (i, 0))
    if bblk and tb:
        b_spec = pl.BlockSpec((nb, tn, Kq), lambda i, j: (0, j, 0))
    elif bblk:
        npb = Nq // tn
        b_spec = pl.BlockSpec((None, K, tn), lambda i, j: (j // npb, 0, j % npb))
    elif tb:
        b_spec = pl.BlockSpec((tn, K), lambda i, j: (j, 0))
    else:
        b_spec = pl.BlockSpec((K, tn), lambda i, j: (0, j))
    in_specs = [a_spec, b_spec] + [pl.BlockSpec((tm, tn), lambda i, j: (i, j)) for _ in extras]
    args = [a, b, *extras]
    aliases = {}
    if outs is None:
        outs = [F32]
    if dw is None:
        o_specs = [pl.BlockSpec((tm, tn), lambda i, j: (i, j)) for _ in outs]
        o_shapes = [jax.ShapeDtypeStruct((M, N), dt) for dt in outs]
    else:
        kind, layers, layer, into = dw
        if kind == "cols":
            per = (N // N_CHIPS) // tn
            o_specs = [pl.BlockSpec((None, None, tm, tn), lambda i, j: (j // per, layer, i, j % per))]
            o_shapes = [jax.ShapeDtypeStruct((N_CHIPS, layers, M, N // N_CHIPS), outs[0])]
        else:
            per = (M // N_CHIPS) // tm
            o_specs = [pl.BlockSpec((None, None, tm, tn), lambda i, j: (i // per, layer, i % per, j))]
            o_shapes = [jax.ShapeDtypeStruct((N_CHIPS, layers, M // N_CHIPS, N), outs[0])]
        if into is not None:
            aliases = {len(args): 0}
            in_specs.append(pl.BlockSpec(memory_space=pl.ANY))
            args.append(into)
    for t in after:
        in_specs.append(pl.BlockSpec(memory_space=pl.ANY))
        args.append(t)
    n_e, n_o = len(extras), len(outs)

    sub = _pick(tm, MM_SUB_ROWS)

    def body(a_ref, b_ref, *rest):
        e_refs, o_refs = rest[:n_e], rest[len(rest) - n_o:]
        for r0 in range(0, tm, sub):
            rows = slice(r0, r0 + sub)
            av = (a_ref[:, rows] if ta else a_ref[rows, :]).astype(BF16)
            if bblk and tb:
                acc = _dot_nt(av[:, :Kq], b_ref[0].astype(BF16))
                for s in range(1, nb):
                    acc = acc + _dot_nt(av[:, s * Kq:(s + 1) * Kq], b_ref[s].astype(BF16))
            elif ta:
                acc = _dot_tn(av, b_ref[...].astype(BF16))
            elif tb:
                acc = _dot_nt(av, b_ref[...].astype(BF16))
            else:
                acc = _dot(av, b_ref[...].astype(BF16))
            vals = (acc,) if epilogue is None else epilogue(acc, *[e[rows, :] for e in e_refs])
            for o, v in zip(o_refs, vals):
                o[rows, :] = v.astype(o.dtype)

    res = pl.pallas_call(
        body, name=name, grid=grid, in_specs=in_specs, out_specs=o_specs, out_shape=o_shapes,
        input_output_aliases=aliases, compiler_params=_cparams(("parallel", "arbitrary")),
    )(*args)
    return res[0] if n_o == 1 else res


def _mm_rows(a, b, *, name, epilogue, outs, tb=False, bblk=False, extras=(), fulls=(), accs=(), tm=512, after=()):
    M, K = a.shape
    tm = _pick(M, tm)
    sub = _pick(tm, MM_SUB_ROWS)
    nb = b.shape[0] if bblk else 1
    n_e, n_f, n_o, n_a = len(extras), len(fulls), len(outs), len(accs)
    n_in = 2 + n_e + n_f + len(after)

    def whole(t):
        return pl.BlockSpec(t.shape, lambda i, nd=t.ndim: (0,) * nd)

    in_specs = [pl.BlockSpec((tm, K), lambda i: (i, 0)), whole(b)]
    in_specs += [pl.BlockSpec((tm, e.shape[1]), lambda i: (i, 0)) for e in extras] + [whole(f) for f in fulls]
    in_specs += [pl.BlockSpec(memory_space=pl.ANY) for _ in after]
    out_specs = [pl.BlockSpec((tm, w), lambda i: (i, 0)) for w, _ in outs] + [pl.BlockSpec(s, lambda i: (0, 0)) for s, _ in accs]
    out_shape = [jax.ShapeDtypeStruct((M, w), dt) for w, dt in outs] + [jax.ShapeDtypeStruct(s, dt) for s, dt in accs]

    def body(a_ref, b_ref, *rest):
        e_refs, f_refs = rest[:n_e], rest[n_e:n_e + n_f]
        o_refs, acc_refs = rest[n_in - 2:n_in - 2 + n_o], rest[n_in - 2 + n_o:]
        fv = [f[...] for f in f_refs]
        totals = None
        for r0 in range(0, tm, sub):
            rows = slice(r0, r0 + sub)
            av = a_ref[rows, :].astype(BF16)
            if bblk and tb:
                kq = K // nb
                acc = _dot_nt(av[:, :kq], b_ref[0])
                for s in range(1, nb):
                    acc = acc + _dot_nt(av[:, s * kq:(s + 1) * kq], b_ref[s])
            elif bblk:
                acc = jnp.concatenate([_dot(av, b_ref[s]) for s in range(nb)], axis=-1)
            elif tb:
                acc = _dot_nt(av, b_ref[...])
            else:
                acc = _dot(av, b_ref[...])
            vals = epilogue(acc, *[e[rows, :] for e in e_refs], *fv)
            for o, v in zip(o_refs, vals[:n_o]):
                o[rows, :] = v.astype(o.dtype)
            part = vals[n_o:]
            totals = part if totals is None else [t + p for t, p in zip(totals, part)]
        first_step = pl.program_id(0) == 0
        for o, v in zip(acc_refs, totals):
            @pl.when(first_step)
            def _(o=o, v=v):
                o[...] = v.astype(o.dtype)

            @pl.when(jnp.logical_not(first_step))
            def _(o=o, v=v):
                o[...] += v.astype(o.dtype)

    return pl.pallas_call(
        body, name=name, grid=(M // tm,), in_specs=in_specs, out_specs=out_specs, out_shape=out_shape,
        compiler_params=_cparams(("arbitrary",)),
    )(a, b, *extras, *fulls, *after)


def _rows(fn, rows, fulls, outs, accs=(), *, name, tile=512, after=()):
    first = rows[0][0] if isinstance(rows[0], tuple) else rows[0]
    T = first.shape[0]
    tile = _pick(T, tile)
    in_specs, args = [], []
    for r in rows:
        if isinstance(r, tuple):
            arr, w, cb = r
            in_specs.append(pl.BlockSpec((tile, w), lambda i, cb=cb: (i, cb)))
        else:
            arr = r
            in_specs.append(pl.BlockSpec((tile, arr.shape[1]), lambda i: (i, 0)))
        args.append(arr)
    for f in fulls:
        in_specs.append(pl.BlockSpec(f.shape, lambda i, nd=f.ndim: (0,) * nd))
        args.append(f)
    outs = [o if len(o) == 4 else (*o, o[0], 0) for o in outs]
    out_specs = [pl.BlockSpec((tile, w), lambda i, cb=cb: (i, cb)) for w, _, _, cb in outs]
    out_specs += [pl.BlockSpec(s, lambda i: (0, 0)) for s, _ in accs]
    out_shape = [jax.ShapeDtypeStruct((T, tw), dt) for _, dt, tw, _ in outs]
    out_shape += [jax.ShapeDtypeStruct(s, dt) for s, dt in accs]
    n_in, n_out = len(args), len(outs)
    for t in after:
        in_specs.append(pl.BlockSpec(memory_space=pl.ANY))
        args.append(t)

    def body(*refs):
        vals = fn(*[r[...] for r in refs[:n_in]])
        o_refs = refs[len(args):]
        for o, v in zip(o_refs[:n_out], vals[:n_out]):
            o[...] = v.astype(o.dtype)
        first_step = pl.program_id(0) == 0
        for o, v in zip(o_refs[n_out:], vals[n_out:]):
            @pl.when(first_step)
            def _(o=o, v=v):
                o[...] = v.astype(o.dtype)

            @pl.when(jnp.logical_not(first_step))
            def _(o=o, v=v):
                o[...] += v.astype(o.dtype)

    res = pl.pallas_call(
        body, name=name, grid=(T // tile,), in_specs=in_specs, out_specs=out_specs, out_shape=out_shape,
        compiler_params=_cparams(("arbitrary",)),
    )(*args)
    return res


def _rowsum(v, mxu):
    if not mxu:
        return jnp.sum(v, axis=-1, keepdims=True)
    ones = jnp.ones((v.shape[1], v.shape[1]), BF16)
    hi = v.astype(BF16)
    lo = (v - hi.astype(F32)).astype(BF16)
    return _dot(hi, ones) + _dot(lo, ones)


def _rms(x, g, mxu=False):
    r = lax.rsqrt(_rowsum(x * x, mxu) / x.shape[-1] + EPS)
    return (x * r) * g


def _rms_bwd(x, dy, g, n=None, mxu=False):
    n = x.shape[-1] if n is None else n
    r = lax.rsqrt(_rowsum(x * x, mxu) / n + EPS)
    xh = x * r
    dxh = dy * g
    dx = r * (dxh - xh * (_rowsum(dxh * xh, mxu) / n))
    return dx, dy * xh


def _colsum(v):
    return jnp.sum(v, axis=0, keepdims=True)


def _sigmoid(x):
    return 0.5 * jnp.tanh(0.5 * x) + 0.5


def _widen(v, width):
    reps = width // v.shape[1]
    return v if reps == 1 else jnp.concatenate([v] * reps, axis=-1)


def _rope_angles(T, dim):
    inv = (1.0 / (np.float32(ROPE_THETA) ** (np.arange(0, dim, 2, dtype=np.float32) / np.float32(dim)))).astype(np.float32)
    return np.arange(T, dtype=np.float32)[:, None] * inv[None, :]


def _ret_tables(T):
    ang = _rope_angles(T, RET_DK)
    log_gamma = np.log(np.float32(1.0) - np.float32(2.0) ** (-5.0 - np.arange(RET_HEADS, dtype=np.float32)))
    idx = np.arange(RET_BLOCK, dtype=np.float32)
    chunk = np.arange(RET_BLOCK) // CHUNK
    dist = idx[:, None] - idx[None, :]
    seen = np.where(chunk[:, None] == chunk[None, :], np.abs(dist), np.where(chunk[:, None] > chunk[None, :], dist, np.inf))
    intra = np.exp(log_gamma[:, None, None] * seen[None].astype(np.float32))
    qd = np.exp(log_gamma[:, None] * (idx + 1.0))[:, :, None]
    kd = np.exp(log_gamma[:, None] * (RET_BLOCK - 1.0 - idx))[:, :, None]
    cd = np.exp(log_gamma * RET_BLOCK)[:, None, None]
    return tuple(jnp.asarray(t, F32) for t in (np.cos(ang), np.sin(ang), intra, qd, kd, cd))


def _rope_half(x, c, s):
    x1, x2 = x[:, :RET_DK // 2], x[:, RET_DK // 2:]
    return jnp.concatenate([x1 * c - x2 * s, x2 * c + x1 * s], axis=-1)


def _rope_half_bwd(d, c, s):
    d1, d2 = d[:, :RET_DK // 2], d[:, RET_DK // 2:]
    return jnp.concatenate([d1 * c + d2 * s, d2 * c - d1 * s], axis=-1)


def _dot(a, b):
    return lax.dot_general(a, b, (((1,), (0,)), ((), ())), preferred_element_type=F32)


def _dot_nt(a, b):
    return lax.dot_general(a, b, (((1,), (1,)), ((), ())), preferred_element_type=F32)


def _dot_tn(a, b):
    return lax.dot_general(a, b, (((0,), (0,)), ((), ())), preferred_element_type=F32)


def _ret_specs(T, tb, rev):
    nj = T // tb
    jj = (lambda j: nj - 1 - j) if rev else (lambda j: j)
    g = RET_GROUP
    kq = RET_HEADS // g
    vq = 2 * RET_HEADS * RET_DK // (g * RET_DV)
    return dict(
        q=pl.BlockSpec((tb, g * RET_DK), lambda h, j: (jj(j), h)),
        k=pl.BlockSpec((tb, g * RET_DK), lambda h, j: (jj(j), kq + h)),
        v=pl.BlockSpec((tb, g * RET_DV), lambda h, j: (jj(j), vq + h)),
        tab=pl.BlockSpec((tb, RET_DK // 2), lambda h, j: (jj(j), 0)),
        intra=pl.BlockSpec((g, RET_BLOCK, RET_BLOCK), lambda h, j: (h, 0, 0)),
        dec=pl.BlockSpec((g, RET_BLOCK, 1), lambda h, j: (h, 0, 0)),
        cd=pl.BlockSpec((g, 1, 1), lambda h, j: (h, 0, 0)),
        o=pl.BlockSpec((tb, g * RET_DV), lambda h, j: (jj(j), h)),
        s=pl.BlockSpec((g, tb // RET_BLOCK, RET_DK, RET_DV), lambda h, j: (h, jj(j), 0, 0)),
    )


def _ret_fwd(proj, tabs, name):
    T = proj.shape[0]
    cos, sin, intra, qd, kd, cd = tabs
    tb = _pick(T, RET_ROWS)
    cps = tb // RET_BLOCK
    sp = _ret_specs(T, tb, False)
    scale = RET_DK ** -0.5

    def body(q_ref, k_ref, v_ref, cos_ref, sin_ref, intra_ref, qd_ref, kd_ref, cd_ref, o_ref, s_ref, state):
        @pl.when(pl.program_id(1) == 0)
        def _():
            state[...] = jnp.zeros_like(state)

        for c in range(cps):
            rows = pl.ds(c * RET_BLOCK, RET_BLOCK)
            co, si = cos_ref[rows, :], sin_ref[rows, :]
            for h in range(RET_GROUP):
                hk, hv = slice(h * RET_DK, (h + 1) * RET_DK), slice(h * RET_DV, (h + 1) * RET_DV)
                q = _rope_half(q_ref[rows, hk].astype(F32), co, si)
                k = _rope_half(k_ref[rows, hk].astype(F32), co, si) * scale
                vb = v_ref[rows, hv].astype(BF16)
                st = state[h]
                sb = st.astype(BF16)
                s_ref[h, c] = sb
                sc = _dot_nt(q.astype(BF16), k.astype(BF16)) * intra_ref[h]
                inner = _dot(sc.astype(BF16), vb)
                cross = _dot((q * qd_ref[h]).astype(BF16), sb)
                o_ref[rows, hv] = inner + cross
                state[h] = st * cd_ref[h] + _dot_tn((k * kd_ref[h]).astype(BF16), vb)

    return pl.pallas_call(
        body, name=name, grid=(RET_HEADS // RET_GROUP, T // tb),
        in_specs=[sp["q"], sp["k"], sp["v"], sp["tab"], sp["tab"], sp["intra"], sp["dec"], sp["dec"], sp["cd"]],
        out_specs=[sp["o"], sp["s"]],
        out_shape=[jax.ShapeDtypeStruct((T, RET_HEADS * RET_DV), F32),
                   jax.ShapeDtypeStruct((RET_HEADS, T // RET_BLOCK, RET_DK, RET_DV), BF16)],
        scratch_shapes=[pltpu.VMEM((RET_GROUP, RET_DK, RET_DV), F32)],
        compiler_params=_cparams(("arbitrary", "arbitrary")),
    )(proj, proj, proj, cos, sin, intra, qd, kd, cd)


def _ret_bwd(proj, states, dout, dproj, tabs, name):
    assert RET_GROUP == 1
    T = proj.shape[0]
    cos, sin, intra, qd, kd, cd = tabs
    tb = _pick(T, RET_ROWS)
    cps = tb // RET_BLOCK
    nj = T // tb
    sp = _ret_specs(T, tb, True)
    scale = RET_DK ** -0.5
    k0, v0 = RET_HEADS * RET_DK, 2 * RET_HEADS * RET_DK

    def body(q_ref, k_ref, v_ref, cos_ref, sin_ref, intra_ref, qd_ref, kd_ref, cd_ref, s_ref, do_ref, _dproj_in,
             out_ref, dq_s, dk_s, dv_s, sems, dstate):
        head, j = pl.program_id(0), pl.program_id(1)
        step = head * nj + j
        slot = step % 2
        dq_ref, dk_ref, dv_ref = dq_s.at[slot], dk_s.at[slot], dv_s.at[slot]

        @pl.when(j == 0)
        def _():
            dstate[...] = jnp.zeros_like(dstate)

        for c in reversed(range(cps)):
            rows = pl.ds(c * RET_BLOCK, RET_BLOCK)
            co, si = cos_ref[rows, :], sin_ref[rows, :]
            for h in range(RET_GROUP):
                hk, hv = slice(h * RET_DK, (h + 1) * RET_DK), slice(h * RET_DV, (h + 1) * RET_DV)
                q = _rope_half(q_ref[rows, hk].astype(F32), co, si)
                k = _rope_half(k_ref[rows, hk].astype(F32), co, si) * scale
                qb, kb = q.astype(BF16), k.astype(BF16)
                vb = v_ref[rows, hv].astype(BF16)
                dob = do_ref[rows, hv].astype(BF16)
                sb = s_ref[h, c]
                ia = intra_ref[h]
                pb = (_dot_nt(qb, kb) * ia).astype(BF16)
                dsn = dstate[h]
                dsb = dsn.astype(BF16)
                kdk = (k * kd_ref[h]).astype(BF16)
                qdq = (q * qd_ref[h]).astype(BF16)
                dv = _dot_tn(pb, dob) + _dot(kdk, dsb)
                dpb = (_dot_nt(dob, vb) * ia).astype(BF16)
                dq = _dot(dpb, kb) + _dot_nt(dob, sb) * qd_ref[h]
                dk = _dot_tn(dpb, qb) + _dot_nt(vb, dsb) * kd_ref[h]
                dstate[h] = dsn * cd_ref[h] + _dot_tn(qdq, dob)
                dq_ref[rows, hk] = _rope_half_bwd(dq, co, si).astype(BF16)
                dk_ref[rows, hk] = _rope_half_bwd(dk * scale, co, si).astype(BF16)
                dv_ref[rows, hv] = dv.astype(BF16)

        def copies(sl):
            r = pl.ds(pl.multiple_of((nj - 1 - j) * tb, tb), tb)
            cols = lambda first, w: pl.ds(pl.multiple_of(first + head * w, 128), w)
            return [pltpu.make_async_copy(dq_s.at[sl], out_ref.at[r, cols(0, RET_DK)], sems.at[sl, 0]),
                    pltpu.make_async_copy(dk_s.at[sl], out_ref.at[r, cols(k0, RET_DK)], sems.at[sl, 1]),
                    pltpu.make_async_copy(dv_s.at[sl], out_ref.at[r, cols(v0, RET_DV)], sems.at[sl, 2])]

        @pl.when(step > 0)
        def _():
            for cp in copies(1 - slot):
                cp.wait()

        for cp in copies(slot):
            cp.start()

        @pl.when(step == RET_HEADS * nj - 1)
        def _():
            for cp in copies(slot):
                cp.wait()

    return pl.pallas_call(
        body, name=name, grid=(RET_HEADS, nj),
        in_specs=[sp["q"], sp["k"], sp["v"], sp["tab"], sp["tab"], sp["intra"], sp["dec"], sp["dec"], sp["cd"],
                  sp["s"], sp["o"], pl.BlockSpec(memory_space=pl.ANY)],
        out_specs=pl.BlockSpec(memory_space=pl.ANY), out_shape=jax.ShapeDtypeStruct(dproj.shape, dproj.dtype),
        input_output_aliases={11: 0},
        scratch_shapes=[pltpu.VMEM((2, tb, RET_DK), BF16), pltpu.VMEM((2, tb, RET_DK), BF16),
                        pltpu.VMEM((2, tb, RET_DV), BF16), pltpu.SemaphoreType.DMA((2, 3)),
                        pltpu.VMEM((RET_GROUP, RET_DK, RET_DV), F32)],
        compiler_params=_cparams(("arbitrary", "arbitrary")),
    )(proj, proj, proj, cos, sin, intra, qd, kd, cd, states, dout, dproj)


def _ret_gate(out, proj, gn, name):
    def fn(o, g, *gains):
        g = g.astype(F32)
        parts = [_rms(o[:, h * RET_DV:(h + 1) * RET_DV], gains[h]) for h in range(RET_HEADS)]
        return (g * _sigmoid(g) * jnp.concatenate(parts, axis=-1),)
    w = RET_HEADS * RET_DV
    return _rows(fn, [out, (proj, w, 2)], [gn[h:h + 1] for h in range(RET_HEADS)], [(w, BF16)], name=name)[0]


def _ret_gate_bwd(out, proj, gn, dy, name, after=()):
    def fn(o, g, d, *gains):
        g = g.astype(F32)
        sg = _sigmoid(g)
        silu = g * sg
        dsilu = sg * (1.0 + g * (1.0 - sg))
        dos, dgs = [], []
        row = lax.broadcasted_iota(jnp.int32, (RET_HEADS, RET_DV), 0)
        dgn = jnp.zeros((RET_HEADS, RET_DV), F32)
        for h in range(RET_HEADS):
            sl = slice(h * RET_DV, (h + 1) * RET_DV)
            oh = o[:, sl]
            dgs.append(d[:, sl] * _rms(oh, gains[h]) * dsilu[:, sl])
            dx, dg = _rms_bwd(oh, d[:, sl] * silu[:, sl], gains[h])
            dos.append(dx)
            dgn = dgn + jnp.where(row == h, _colsum(dg), 0.0)
        return jnp.concatenate(dos, axis=-1), jnp.concatenate(dgs, axis=-1), dgn
    w = RET_HEADS * RET_DV
    return _rows(fn, [out, (proj, w, 2), dy], [gn[h:h + 1] for h in range(RET_HEADS)],
                 [(w, BF16), (w, BF16, proj.shape[1], 2)], [((RET_HEADS, RET_DV), F32)], name=name, tile=256,
                 after=after)


def _mla_tables(T):
    ang = _rope_angles(T, MLA_ROPE)
    c, s = np.cos(ang), np.sin(ang)
    z32, z64 = np.zeros((T, 32), np.float32), np.zeros((T, 64), np.float32)
    cos_t = np.concatenate([c, c, z64], axis=1)
    sin_a = np.concatenate([-s, z32, z64], axis=1)
    sin_b = np.concatenate([z32, s, z64], axis=1)
    return tuple(jnp.asarray(t, F32) for t in (cos_t, sin_a, sin_b))


def _rope_blk(x, ct, sa, sb):
    return x * ct + pltpu.roll(x, 96, 1) * sa + pltpu.roll(x, 32, 1) * sb


def _rope_blk_bwd(d, ct, sa, sb):
    return d * ct + pltpu.roll(d * sa, 32, 1) + pltpu.roll(d * sb, 96, 1)


def _head_norm(x, gain):
    r = lax.rsqrt(_rowsum(x * x, True) / MLA_QKD + EPS)
    return (x * r) * gain


def _prep_heads(qv, kvv, kr, ct, sa, sb, gqv, gkv):
    qs, ks, vs = [], [], []
    for h in range(MLA_HEADS):
        b = h * MLA_HP
        y = _head_norm(qv[:, b:b + MLA_HP], gqv)
        qs += [y[:, :128], _rope_blk(y[:, 128:], ct, sa, sb)]
        y = _head_norm(jnp.concatenate([kvv[:, b:b + 128], kr], axis=-1), gkv)
        ks += [y[:, :128], _rope_blk(y[:, 128:], ct, sa, sb)]
        vs.append(kvv[:, b + 128:b + 256])
    return jnp.concatenate(qs, axis=-1), jnp.concatenate(ks, axis=-1), jnp.concatenate(vs, axis=-1)


def _mla_front(hn, W, tabs, name):
    wide = MLA_HEADS * MLA_HP
    gq = W["mla_q_norm"] * (MLA_QKD ** -0.5 * LOG2E)

    def epilogue(acc, ct, sa, sb, gqa, gkva, wuq, wukv, gqv, gkv):
        cqn = _rms(acc[:, :MLA_Q_RANK], gqa).astype(BF16)
        ckvn = _rms(acc[:, MLA_Q_RANK:MLA_Q_RANK + MLA_KV_RANK], gkva).astype(BF16)
        q = jnp.concatenate([_dot(cqn, wuq[s]) for s in range(N_CHIPS)], axis=-1).astype(BF16)
        kv = jnp.concatenate([_dot(ckvn, wukv[s]) for s in range(N_CHIPS)], axis=-1).astype(BF16)
        qf, kf, vf = _prep_heads(q.astype(F32), kv.astype(F32), acc[:, MLA_IN_PAD - 128:], ct, sa, sb, gqv, gkv)
        return acc, cqn, ckvn, q, kv, qf, kf, vf

    return _mm_rows(hn, W["mla_w_in"], extras=list(tabs),
                    fulls=[W["mla_q_a_norm"], W["mla_kv_a_norm"], W["mla_w_uq"], W["mla_w_ukv"], gq, W["mla_k_norm"]],
                    outs=[(MLA_IN_PAD, F32), (MLA_Q_RANK, BF16), (MLA_KV_RANK, BF16), (wide, BF16), (wide, BF16),
                          (wide, BF16), (wide, BF16), (MLA_HEADS * MLA_VD, BF16)],
                    epilogue=epilogue, name=name, tm=256)


def _prep_heads_bwd(qv, kvv, kr, ct, sa, sb, dqv, dkv, dvv, gqv, gkv):
    dqs, dkvs = [], []
    dkr = jnp.zeros_like(kr)
    dgq = jnp.zeros((1, MLA_HP), F32)
    dgk = jnp.zeros((1, MLA_HP), F32)
    for h in range(MLA_HEADS):
        b = h * MLA_HP
        dy = jnp.concatenate([dqv[:, b:b + 128], _rope_blk_bwd(dqv[:, b + 128:b + 256], ct, sa, sb)], axis=-1)
        dx, dg = _rms_bwd(qv[:, b:b + MLA_HP], dy, gqv, MLA_QKD, mxu=True)
        dqs.append(dx)
        dgq = dgq + _colsum(dg)
        dy = jnp.concatenate([dkv[:, b:b + 128], _rope_blk_bwd(dkv[:, b + 128:b + 256], ct, sa, sb)], axis=-1)
        dx, dg = _rms_bwd(jnp.concatenate([kvv[:, b:b + 128], kr], axis=-1), dy, gkv, MLA_QKD, mxu=True)
        dkvs += [dx[:, :128], dvv[:, h * MLA_VD:(h + 1) * MLA_VD].astype(F32)]
        dkr = dkr + dx[:, 128:]
        dgk = dgk + _colsum(dg)
    return jnp.concatenate(dqs, axis=-1), jnp.concatenate(dkvs, axis=-1), dkr, dgq, dgk


def _mla_back(q, kv, proj, h0, dh1, dqf, dkf, dvf, W, tabs, name):
    def fn(qv, kvv, pv, hv, dr, ct, sa, sb, dqv, dkv, dvv, gqv, gkv, gqa, gkva, wuq, wukv, w_in, g_mix):
        qv, kvv, dqv, dkv = (t.astype(F32) for t in (qv, kvv, dqv, dkv))
        dq, dkvx, dkr, dgq, dgk = _prep_heads_bwd(qv, kvv, pv[:, MLA_IN_PAD - 128:], ct, sa, sb, dqv, dkv, dvv, gqv, gkv)
        dq, dkvx = dq.astype(BF16), dkvx.astype(BF16)
        nq = wuq.shape[2]
        dcq = sum(_dot_nt(dq[:, s * nq:(s + 1) * nq], wuq[s]) for s in range(N_CHIPS))
        dckv = sum(_dot_nt(dkvx[:, s * nq:(s + 1) * nq], wukv[s]) for s in range(N_CHIPS))
        dxq, dgqa = _rms_bwd(pv[:, :MLA_Q_RANK], dcq, gqa)
        dxkv, dgkva = _rms_bwd(pv[:, MLA_Q_RANK:MLA_Q_RANK + MLA_KV_RANK], dckv, gkva)
        dproj = jnp.concatenate([dxq, dxkv, dkr], axis=-1).astype(BF16)
        dx, dgm = _rms_bwd(hv, _dot_nt(dproj, w_in), g_mix)
        return (dq, dkvx, dproj, dr + dx, dr + dx, dgq, dgk, _colsum(dgqa), _colsum(dgkva), _colsum(dgm))

    wide = MLA_HEADS * MLA_HP
    return _rows(fn, [q, kv, proj, h0, dh1, *tabs, dqf, dkf, dvf],
                 [W["mla_q_norm"], W["mla_k_norm"], W["mla_q_a_norm"], W["mla_kv_a_norm"], W["mla_w_uq"], W["mla_w_ukv"],
                  W["mla_w_in"], W["mix_norm"][1:2]],
                 [(wide, BF16), (wide, BF16), (MLA_IN_PAD, BF16), ROW_F32, ROW_BF16],
                 [((1, MLA_HP), F32), ((1, MLA_HP), F32), ((1, MLA_Q_RANK), F32), ((1, MLA_KV_RANK), F32),
                  ((1, D_MODEL), F32)], name=name, tile=256)


def _chunk_mask(qi, ki, tq, tk):
    shift = CHUNK.bit_length() - 1
    rq = lax.shift_right_arithmetic(qi * tq + lax.broadcasted_iota(jnp.int32, (tq, tk), 0), shift)
    ck = lax.shift_right_arithmetic(ki * tk + lax.broadcasted_iota(jnp.int32, (tq, tk), 1), shift)
    return ck <= rq


def _flash_fwd(qf, kf, vf, name):
    T = qf.shape[0]
    t = _pick(T, FLASH_T)
    n = T // t
    g = FLASH_HEADS

    def body(q_ref, k_ref, v_ref, o_ref, lse_ref, m_s, acc):
        qi = pl.program_id(1)
        m_s[...] = jnp.full_like(m_s, NEG)
        acc[...] = jnp.zeros_like(acc)
        ones = jnp.ones((t, MLA_VD), BF16)

        def step(kb, masked):
            rows = pl.ds(pl.multiple_of(kb * t, t), t)
            for h in range(g):
                hq, hv = slice(h * MLA_HP, (h + 1) * MLA_HP), slice(h * MLA_VD, (h + 1) * MLA_VD)
                ha = slice(2 * h * MLA_VD, 2 * (h + 1) * MLA_VD)
                s = _dot_nt(q_ref[:, hq], k_ref[rows, hq])
                if masked:
                    s = jnp.where(_chunk_mask(0, 0, t, t), s, NEG)
                m_prev = m_s[:, hv]
                m_new = jnp.maximum(m_prev, jnp.max(s, axis=-1, keepdims=True))
                alpha = jnp.exp2(m_prev - m_new)
                p = jnp.exp2(s - _widen(m_new, t))
                values = jnp.concatenate([v_ref[rows, hv], ones], axis=-1)
                acc[:, ha] = acc[:, ha] * _widen(alpha, 2 * MLA_VD) + _dot(p.astype(BF16), values)
                m_s[:, hv] = m_new

        @pl.loop(0, qi)
        def _(kb):
            step(kb, False)

        step(qi, True)
        for h in range(g):
            hv = slice(h * MLA_VD, (h + 1) * MLA_VD)
            l = acc[:, (2 * h + 1) * MLA_VD:(2 * h + 2) * MLA_VD]
            o_ref[:, hv] = acc[:, 2 * h * MLA_VD:(2 * h + 1) * MLA_VD] / l
            lse_ref[:, hv] = m_s[:, hv] + jnp.log2(l)

    qmap = lambda h, i: (i, h)
    kmap = lambda h, i: (0, h)
    vec = pltpu.VMEM((t, g * MLA_VD), F32)
    return pl.pallas_call(
        body, name=name, grid=(MLA_HEADS // g, n),
        in_specs=[pl.BlockSpec((t, g * MLA_HP), qmap), pl.BlockSpec((T, g * MLA_HP), kmap),
                  pl.BlockSpec((T, g * MLA_VD), kmap)],
        out_specs=[pl.BlockSpec((t, g * MLA_VD), qmap), pl.BlockSpec((t, g * MLA_VD), qmap)],
        out_shape=[jax.ShapeDtypeStruct((T, MLA_HEADS * MLA_VD), F32),
                   jax.ShapeDtypeStruct((T, MLA_HEADS * MLA_VD), F32)],
        scratch_shapes=[vec, pltpu.VMEM((t, 2 * g * MLA_VD), F32)],
        compiler_params=_cparams(("parallel", "arbitrary")),
    )(qf, kf, vf)


def _flash_bwd(qf, kf, vf, do16, lse, delta, name):
    T = qf.shape[0]
    t = _pick(T, FLASH_T)
    n = T // t
    g = FLASH_BWD_HEADS
    scale = MLA_QKD ** -0.5

    def body(q_ref, k_ref, v_ref, do_ref, lse_ref, dl_ref, dq_out, dk_out, dv_out, dq_ref, dk_ref, dv_ref):
        kb = pl.program_id(1)

        @pl.when(kb == 0)
        def _():
            dq_ref[...] = jnp.zeros_like(dq_ref)

        dk_ref[...] = jnp.zeros_like(dk_ref)
        dv_ref[...] = jnp.zeros_like(dv_ref)

        def step(qb, masked):
            rows = pl.ds(pl.multiple_of(qb * t, t), t)
            for h in range(g):
                hq, hv = slice(h * MLA_HP, (h + 1) * MLA_HP), slice(h * MLA_VD, (h + 1) * MLA_VD)
                q, dob, k, v = q_ref[rows, hq], do_ref[rows, hv], k_ref[:, hq], v_ref[:, hv]
                s = _dot_nt(q, k)
                if masked:
                    s = jnp.where(_chunk_mask(0, 0, t, t), s, NEG)
                p = jnp.exp2(s - _widen(lse_ref[rows, hv], t))
                ds = (p * (_dot_nt(dob, v) - _widen(dl_ref[rows, hv], t))).astype(BF16)
                dv_ref[:, hv] += _dot_tn(p.astype(BF16), dob)
                dk_ref[:, hq] += _dot_tn(ds, q)
                dq_ref[rows, hq] += _dot(ds, k)

        step(kb, True)

        @pl.loop(kb + 1, n)
        def _(qb):
            step(qb, False)

        dk_out[...] = (dk_ref[...] * (1.0 / LOG2E)).astype(BF16)
        dv_out[...] = dv_ref[...].astype(BF16)

        @pl.when(kb == n - 1)
        def _():
            dq_out[...] = (dq_ref[...] * scale).astype(BF16)

    qmap = lambda h, j: (0, h)
    kmap = lambda h, j: (j, h)
    wq, wv = g * MLA_HP, g * MLA_VD
    return pl.pallas_call(
        body, name=name, grid=(MLA_HEADS // g, n),
        in_specs=[pl.BlockSpec((T, wq), qmap), pl.BlockSpec((t, wq), kmap), pl.BlockSpec((t, wv), kmap),
                  pl.BlockSpec((T, wv), qmap), pl.BlockSpec((T, wv), qmap), pl.BlockSpec((T, wv), qmap)],
        out_specs=[pl.BlockSpec((T, wq), qmap), pl.BlockSpec((t, wq), kmap), pl.BlockSpec((t, wv), kmap)],
        out_shape=[jax.ShapeDtypeStruct((T, MLA_HEADS * MLA_HP), BF16),
                   jax.ShapeDtypeStruct((T, MLA_HEADS * MLA_HP), BF16),
                   jax.ShapeDtypeStruct((T, MLA_HEADS * MLA_VD), BF16)],
        scratch_shapes=[pltpu.VMEM((T, wq), F32), pltpu.VMEM((t, wq), F32), pltpu.VMEM((t, wv), F32)],
        compiler_params=_cparams(("arbitrary", "arbitrary")),
    )(qf, kf, vf, do16, lse, delta)


MESH = pl.DeviceIdType.MESH
ANY = pl.BlockSpec(memory_space=pl.ANY)
_CHIP_FLIPS = ((1, 0), (0, 1), (1, 1))


def _place():
    return lax.axis_index("x"), lax.axis_index("y"), lax.axis_index("c")


def _other_chip(x, y, k):
    fx, fy = _CHIP_FLIPS[k]
    return ((1 - x) if fx else x), ((1 - y) if fy else y)


def _remote(src, dst, send_sems, recv_sems, k, to):
    return pltpu.make_async_remote_copy(src_ref=src, dst_ref=dst, send_sem=send_sems.at[k], recv_sem=recv_sems.at[k],
                                        device_id=to, device_id_type=MESH)


def _index(*vals):
    return jnp.stack(vals).astype(jnp.int32)


def _half(c, rows):
    return pl.ds(pl.multiple_of(c * rows, 16), rows)


def _gather_weights(parts, name, landed=None):
    n_w = len(parts)
    n_in = n_w if landed is None else 2 * n_w

    def body(*refs):
        ins, outs = refs[:n_w], refs[n_in:n_in + n_w]
        send_sems, recv_sems, local_sems = refs[n_in + n_w:]
        x, y, c = _place()
        j = 2 * x + y
        sibling = (x, y, 1 - c)
        chips = [_other_chip(x, y, k) for k in range(3)]
        pending = []
        for w in range(n_w):
            own = pltpu.make_async_copy(ins[w], outs[w].at[j], local_sems.at[w])
            own.start()
            pending.append(own)
        sent = []
        for w in range(n_w):
            if landed is not None:
                break
            r = _half(c, parts[w].shape[0] // 2)
            for k, (px, py) in enumerate(chips):
                cp = _remote(ins[w].at[r], outs[w].at[j, r], send_sems, recv_sems, 6 * w + k, (px, py, c))
                cp.start()
                sent.append(cp)
        for w in range(n_w):
            r = _half(c, parts[w].shape[0] // 2)
            for k, (px, py) in enumerate(chips):
                blk = outs[w].at[2 * px + py, r]
                if landed is None:
                    _remote(blk, blk, send_sems, recv_sems, 6 * w + k, (px, py, c)).wait_recv()
                cp = _remote(blk, blk, send_sems, recv_sems, 6 * w + 3 + k, sibling)
                cp.start()
                sent.append(cp)
        for w in range(n_w):
            r = _half(1 - c, parts[w].shape[0] // 2)
            for k, (px, py) in enumerate(chips):
                blk = outs[w].at[2 * px + py, r]
                _remote(blk, blk, send_sems, recv_sems, 6 * w + 3 + k, sibling).wait_recv()
        for cp in sent:
            cp.wait_send()
        for cp in pending:
            cp.wait()

    return pl.pallas_call(
        body, name=name, in_specs=[pl.BlockSpec(memory_space=pltpu.VMEM)] * n_w + [ANY] * (n_in - n_w),
        out_specs=[ANY] * n_w,
        out_shape=[jax.ShapeDtypeStruct((N_CHIPS, *p.shape), p.dtype) for p in parts],
        input_output_aliases={} if landed is None else {n_w + w: w for w in range(n_w)},
        scratch_shapes=[pltpu.SemaphoreType.DMA((6 * n_w,)), pltpu.SemaphoreType.DMA((6 * n_w,)),
                        pltpu.SemaphoreType.DMA((n_w,))],
        compiler_params=pltpu.CompilerParams(vmem_limit_bytes=VMEM_LIMIT),
    )(*parts, *(landed or []))


def _swap_halves(gs, name):
    n_w = len(gs)

    def body(*refs):
        g_refs, recv_refs = refs[:n_w], refs[n_w:2 * n_w]
        send_sems, recv_sems = refs[2 * n_w:]
        x, y, c = _place()
        sent = []
        for w in range(n_w):
            for jj in range(N_CHIPS):
                cp = _remote(g_refs[w].at[jj, 1 - c], recv_refs[w].at[jj], send_sems, recv_sems, N_CHIPS * w + jj,
                             (x, y, 1 - c))
                cp.start()
                sent.append(cp)
        for cp in sent:
            cp.wait()

    return pl.pallas_call(
        body, name=name, in_specs=[ANY] * n_w, out_specs=[ANY] * n_w,
        out_shape=[jax.ShapeDtypeStruct((N_CHIPS, *g.shape[2:]), g.dtype) for g in gs],
        scratch_shapes=[pltpu.SemaphoreType.DMA((N_CHIPS * n_w,)), pltpu.SemaphoreType.DMA((N_CHIPS * n_w,))],
    )(*gs)


def _pair_sum(g, recv, core, name):
    _, H, C = recv.shape
    tile = _pick(H, SUM_ROWS)

    def body(c_ref, own_ref, recv_ref, out_ref):
        out_ref[...] = (own_ref[...].astype(F32) + recv_ref[...].astype(F32)).astype(BF16)

    blk = pl.BlockSpec((None, tile, C), lambda jj, i, c: (jj, i, 0))
    return pl.pallas_call(
        body, name=name,
        grid_spec=pltpu.PrefetchScalarGridSpec(
            num_scalar_prefetch=1, grid=(N_CHIPS, H // tile),
            in_specs=[pl.BlockSpec((None, None, tile, C), lambda jj, i, c: (jj, c[0], i, 0)), blk],
            out_specs=blk),
        out_shape=jax.ShapeDtypeStruct((N_CHIPS, H, C), BF16),
        compiler_params=_cparams(("arbitrary", "arbitrary")),
    )(_index(core), g, recv)


def _chip_sum(g, recv, got, chip, core, name):
    _, H, C = recv.shape
    tile = _pick(H, SUM_ROWS)

    def body(s_ref, own_ref, recv_ref, g0_ref, g1_ref, g2_ref, out_ref):
        pair = own_ref[...].astype(F32) + recv_ref[...].astype(F32)
        out_ref[...] = ((pair + g0_ref[...].astype(F32)) + g1_ref[...].astype(F32)) + g2_ref[...].astype(F32)

    def got_spec(k):
        return pl.BlockSpec((None, tile, C), lambda i, s, k=k: (k, i, 0))

    return pl.pallas_call(
        body, name=name,
        grid_spec=pltpu.PrefetchScalarGridSpec(
            num_scalar_prefetch=1, grid=(H // tile,),
            in_specs=[pl.BlockSpec((None, None, tile, C), lambda i, s: (s[0], s[1], i, 0)),
                      pl.BlockSpec((None, tile, C), lambda i, s: (s[0], i, 0)), got_spec(0), got_spec(1), got_spec(2)],
            out_specs=pl.BlockSpec((None, tile, C), lambda i, s: (s[1], i, 0))),
        out_shape=jax.ShapeDtypeStruct((2, H, C), F32),
        compiler_params=_cparams(("arbitrary",)),
    )(_index(chip, core), g, recv, got, got, got)


def _share_halves(reds):
    n_w = len(reds)

    def body(*refs):
        out_refs = refs[n_w:2 * n_w]
        send_sems, recv_sems = refs[2 * n_w:]
        x, y, c = _place()
        sent = []
        for w in range(n_w):
            blk = out_refs[w].at[c]
            cp = _remote(blk, blk, send_sems, recv_sems, w, (x, y, 1 - c))
            cp.start()
            sent.append(cp)
        for cp in sent:
            cp.wait()

    return pl.pallas_call(
        body, name="grad_share_halves", in_specs=[ANY] * n_w, out_specs=[ANY] * n_w,
        out_shape=[jax.ShapeDtypeStruct(r.shape, r.dtype) for r in reds],
        input_output_aliases={w: w for w in range(n_w)},
        scratch_shapes=[pltpu.SemaphoreType.DMA((n_w,)), pltpu.SemaphoreType.DMA((n_w,))],
    )(*reds)


def _allsum_small(v, name):
    R, W = v.shape
    n_dev = 8
    vm = pl.BlockSpec(memory_space=pltpu.VMEM)

    def body(v_ref, out_ref, buf, send_sems, recv_sems):
        x, y, c = _place()
        me = 4 * x + 2 * y + c
        buf[me] = v_ref[...]
        sent = []
        for k in range(1, n_dev):
            peer = ((1 - x) if k & 4 else x, (1 - y) if k & 2 else y, (1 - c) if k & 1 else c)
            cp = _remote(v_ref, buf.at[me], send_sems, recv_sems, k - 1, peer)
            cp.start()
            sent.append(cp)
        for cp in sent:
            cp.wait_recv()
        for cp in sent:
            cp.wait_send()
        acc = buf[0]
        for q in range(1, n_dev):
            acc = acc + buf[q]
        out_ref[...] = acc

    return pl.pallas_call(
        body, name=name, in_specs=[vm], out_specs=vm, out_shape=jax.ShapeDtypeStruct((R, W), v.dtype),
        scratch_shapes=[pltpu.VMEM((n_dev, R, W), v.dtype), pltpu.SemaphoreType.DMA((n_dev - 1,)),
                        pltpu.SemaphoreType.DMA((n_dev - 1,))],
    )(v)


HBM = pl.BlockSpec(memory_space=pltpu.HBM)
SEM = pl.BlockSpec(memory_space=pltpu.SEMAPHORE)
_DATAFLOW = pltpu.SideEffectType.DATAFLOW_SIDE_EFFECTING


def _split_start(name, srcs, land_shapes, n_copies, copies, after=()):
    ns, nl = len(srcs), len(land_shapes)
    lands = [lax.empty(s.shape, s.dtype) for s in land_shapes]

    def body(*refs):
        outs = refs[ns + nl + len(after):]
        for cp in copies(refs[:ns], refs[ns:ns + nl], outs[0], outs[1]):
            cp.start()
        outs[-1][...] = jnp.zeros_like(outs[-1])

    sems = pltpu.SemaphoreType.DMA((n_copies,))
    res = pl.pallas_call(
        body, name=name, in_specs=[HBM] * (ns + nl) + [ANY] * len(after),
        out_specs=(SEM, SEM, *[HBM] * (ns + nl), pl.BlockSpec(memory_space=pltpu.VMEM)),
        out_shape=(sems, sems, *[pltpu.HBM(a.shape, a.dtype) for a in srcs],
                   *[pltpu.HBM(s.shape, s.dtype) for s in land_shapes], jax.ShapeDtypeStruct((8, 128), F32)),
        input_output_aliases={i: 2 + i for i in range(ns + nl)},
        compiler_params=pltpu.CompilerParams(has_side_effects=_DATAFLOW),
    )(*[pltpu.with_memory_space_constraint(a, pltpu.HBM) for a in [*srcs, *lands]], *after)
    return res[0], res[1], list(res[2:2 + ns]), list(res[2 + ns:2 + ns + nl]), res[-1]


def _split_wait(name, send_sems, recv_sems, srcs, lands, copies, after=()):
    ns, nl = len(srcs), len(lands)

    def body(*refs):
        for cp in copies(refs[:ns], refs[ns:ns + nl], refs[ns + nl], refs[ns + nl + 1]):
            cp.wait_send()
            cp.wait_recv()

    res = pl.pallas_call(
        body, name=name, in_specs=[HBM] * (ns + nl) + [SEM, SEM] + [ANY] * len(after), out_specs=[HBM] * (ns + nl),
        out_shape=[pltpu.HBM(a.shape, a.dtype) for a in [*srcs, *lands]],
        input_output_aliases={i: i for i in range(ns + nl)},
        compiler_params=pltpu.CompilerParams(has_side_effects=_DATAFLOW),
    )(*srcs, *lands, send_sems, recv_sems, *after)
    return list(res[:ns]), list(res[ns:])


def _gather_copies(rows):
    def copies(src_refs, land_refs, send_sems, recv_sems):
        x, y, c = _place()
        j = 2 * x + y
        out = []
        for w in range(len(src_refs)):
            r = _half(c, rows[w] // 2)
            for k in range(3):
                px, py = _other_chip(x, y, k)
                out.append(_remote(src_refs[w].at[r], land_refs[w].at[j, r], send_sems, recv_sems, 3 * w + k, (px, py, c)))
        return out
    return copies


def _scatter_copies(src_refs, land_refs, send_sems, recv_sems):
    x, y, c = _place()
    j = 2 * x + y
    out = []
    for w in range(len(src_refs)):
        for k in range(3):
            px, py = _other_chip(x, y, k)
            pj = 2 * px + py
            out.append(_remote(src_refs[w].at[pj], land_refs[w].at[(j - pj + 4) % 4 - 1], send_sems, recv_sems, 3 * w + k,
                               (px, py, c)))
    return out


def _halves(grads):
    names = list(grads)
    return names, [grads[k].reshape(N_CHIPS, 2, -1, grads[k].shape[-1]) for k in names]


def _reduce_begin(grads, core, tag):
    names, gs = _halves(grads)
    recvs = _swap_halves(gs, f"grad_swap_halves_{tag}")
    sums = [_pair_sum(g, r, core, f"pair_sum_{k}") for k, g, r in zip(names, gs, recvs)]
    return names, gs, recvs, sums


def _swap_copies(src_refs, land_refs, send_sems, recv_sems):
    x, y, c = _place()
    return [_remote(src_refs[w].at[jj, 1 - c], land_refs[w].at[jj], send_sems, recv_sems, N_CHIPS * w + jj, (x, y, 1 - c))
            for w in range(len(src_refs)) for jj in range(N_CHIPS)]


def _swap_begin(grads, tag):
    names, gs = _halves(grads)
    started = _split_start(f"swap_{tag}_start", gs, [jax.ShapeDtypeStruct((N_CHIPS, *g.shape[2:]), g.dtype) for g in gs],
                           N_CHIPS * len(gs), _swap_copies)
    return (names, started[:4]), started[4]


def _swap_end(begun, core, tag, after):
    names, started = begun
    gs, recvs = _split_wait(f"swap_{tag}_wait", *started, _swap_copies, after=after)
    sums = [_pair_sum(g, r, core, f"pair_sum_{k}") for k, g, r in zip(names, gs, recvs)]
    return names, gs, recvs, sums


def _reduce_end(begun, gots, chip, core):
    names, gs, recvs, _ = begun
    return {k: _chip_sum(g, r, t, chip, core, f"chip_sum_{k}") for k, g, r, t in zip(names, gs, recvs, gots)}


def _got_shapes(sums):
    return [jax.ShapeDtypeStruct((3, *a.shape[1:]), a.dtype) for a in sums]


def _adamw(w, g, m, v, name, layers=1, layer=0, into=None):
    shape = w.shape
    cols = shape[-1]
    w3, m3, v3 = (t.reshape(layers, -1, cols) for t in (w, m, v))
    rows = w3.shape[1]
    tile = _pick(rows, ADAM_ROWS if cols <= 1024 else ADAM_ROWS // 2) if rows % 8 == 0 else rows
    n_in = 4 + (0 if into is None else 4)
    stack_g = layers > 1

    def body(*refs):
        wv, gv, mv, vv = (r[...] for r in refs[:4])
        d_ref, m_ref, v_ref = refs[len(refs) - 3:]
        m2 = ADAM_B1 * mv + (1.0 - ADAM_B1) * gv
        v2 = ADAM_B2 * vv + (1.0 - ADAM_B2) * jnp.square(gv)
        m_hat = m2 / (1.0 - ADAM_B1 ** ADAM_STEP)
        v_hat = v2 / (1.0 - ADAM_B2 ** ADAM_STEP)
        if stack_g:
            refs[n_in][...] = gv
        d_ref[...] = -ADAM_LR * (m_hat / (jnp.sqrt(v_hat) + ADAM_EPS) + ADAM_WD * wv)
        m_ref[...] = m2
        v_ref[...] = v2

    n_out = 4 if stack_g else 3
    lay = pl.BlockSpec((None, tile, cols), lambda i: (layer, i, 0))
    out = jax.ShapeDtypeStruct((layers, rows, cols), F32)
    res = pl.pallas_call(
        body, name=name, grid=(rows // tile,),
        in_specs=[lay, pl.BlockSpec((tile, cols), lambda i: (i, 0)), lay, lay] + [ANY] * (n_in - 4),
        out_specs=[lay] * n_out, out_shape=[out] * n_out,
        input_output_aliases={} if into is None else {4 + k: k for k in range(4)},
        compiler_params=_cparams(("arbitrary",)),
    )(w3, g.reshape(rows, cols), m3, v3, *([] if into is None else [t.reshape(layers, rows, cols) for t in into]))
    res = tuple(t.reshape(shape) for t in res)
    return res if stack_g else (g.reshape(shape), *res)


ROW_F32, ROW_BF16 = (D_MODEL, F32), (D_MODEL, BF16)


def _res_norm(acc, h, gain):
    hh = h + acc
    return hh, _rms(hh, gain)


def _dx_norm_bwd(d, w, h, dres, gain, name, **kw):
    def epilogue(acc, hv, dr, g):
        dx, dg = _rms_bwd(hv, acc, g)
        return dr + dx, dr + dx, _colsum(dg)
    return _mm_rows(d, w, tb=True, extras=[h, dres], fulls=[gain], outs=[ROW_F32, ROW_BF16], accs=[((1, D_MODEL), F32)],
                    epilogue=epilogue, name=name, **kw)


def _tail_fwd(h1, hn2, p16, W, i, tag, next_gain=None, target=None):
    a = _mm(hn2, W["mlp_w1"][i], bblk=True, outs=[BF16], name=f"{tag}_mlp_w1", tm=2048, tn=1024,
            epilogue=lambda acc: (jnp.square(jnp.maximum(acc, 0.0)),))
    h2, hn3 = _mm_rows(a, W["mlp_w2"][i], extras=[h1], fulls=[W["ple_norm"][i:i + 1]], outs=[ROW_F32, ROW_BF16],
                       epilogue=_res_norm, name=f"{tag}_mlp_w2")
    def embed(acc, pv, h, wp):
        gate = _sigmoid(acc)
        ppv = jnp.concatenate([_dot(pv, wp[s]) for s in range(N_CHIPS)], axis=-1)
        return gate, ppv, h + gate * ppv

    if target is None:
        def gated(acc, pv, h, wp, gain):
            gate, ppv, hh = embed(acc, pv, h, wp)
            return hh, ppv, gate, _rms(hh, gain)
        h3, pp, gate, hn = _mm_rows(hn3, W["ple_gate_w"][i], extras=[p16[i], h2], fulls=[W["ple_proj_w"][i], next_gain],
                                    outs=[ROW_F32, ROW_BF16, ROW_BF16, ROW_BF16], epilogue=gated, name=f"{tag}_ple")
        return h3, hn, (h1, hn2, a, h2, hn3, gate, pp)

    def gated_loss(acc, pv, h, t, wp):
        gate, ppv, hh = embed(acc, pv, h, wp)
        e = hh - t
        return ppv, gate, e * (1.0 / D_MODEL), jnp.full((1, 128), 0.5 / D_MODEL, F32) * jnp.sum(e * e)
    pp, gate, dy, loss = _mm_rows(hn3, W["ple_gate_w"][i], extras=[p16[i], h2, target], fulls=[W["ple_proj_w"][i]],
                                  outs=[ROW_BF16, ROW_BF16, ROW_F32], accs=[((1, 128), F32)], epilogue=gated_loss,
                                  name=f"{tag}_ple")
    return dy, loss, (h1, hn2, a, h2, hn3, gate, pp)


def _tail_bwd(dh3, saved, p16, W, i, tag, after=(), hook=None):
    h1, hn2, a, h2, hn3, gate, pp = saved

    def embed_bwd(d, g, ppv, hv, wg, gain):
        g, ppv = g.astype(F32), ppv.astype(F32)
        dppv, dglv = (d * g).astype(BF16), (d * ppv * g * (1.0 - g)).astype(BF16)
        dx, dg = _rms_bwd(hv, _dot_nt(dglv, wg), gain)
        return dppv, dglv, d + dx, d + dx, _colsum(dg)

    def dw(kind, name):
        return (kind, 1, 0, None)

    dpp, dgl, dh2, dh2_16, d_ple_norm = _rows(
        embed_bwd, [dh3, gate, pp, h2], [W["ple_gate_w"][i], W["ple_norm"][i:i + 1]],
        [ROW_BF16, ROW_BF16, ROW_F32, ROW_BF16], [((1, D_MODEL), F32)], name=f"{tag}_ple_bwd", after=after)
    later = () if hook is None else hook(dh2_16)
    d_proj = _mm(p16[i], dpp, ta=True, outs=[BF16], dw=dw("cols", "ple_proj_w"), name=f"{tag}_d_ple_proj", after=later)
    d_gate = _mm(hn3, dgl, ta=True, outs=[BF16], dw=dw("rows", "ple_gate_w"), name=f"{tag}_d_ple_gate")
    d_w2 = _mm(a, dh2_16, ta=True, outs=[BF16], dw=dw("rows", "mlp_w2"), name=f"{tag}_d_mlp_w2", tn=1024)
    dz = _mm(dh2_16, W["mlp_w2"][i], tb=True, extras=[a], outs=[BF16], name=f"{tag}_mlp_w2_dx", tm=2048, tn=1024,
             epilogue=lambda acc, av: (acc * (2.0 * jnp.sqrt(av.astype(F32))),))
    d_w1 = _mm(hn2, dz, ta=True, outs=[BF16], dw=dw("cols", "mlp_w1"), name=f"{tag}_d_mlp_w1", tn=1024)
    dh1, dh1_16, d_mlp_norm = _dx_norm_bwd(dz, W["mlp_w1"][i], h1, dh2, W["mlp_norm"][i:i + 1], f"{tag}_mlp_w1_dx",
                                           bblk=True)
    big = {f"mlp_w1_{i}": d_w1, f"mlp_w2_{i}": d_w2, f"ple_gate_w_{i}": d_gate, f"ple_proj_w_{i}": d_proj}
    return dh1, dh1_16, big, dict(mlp_norm=d_mlp_norm, ple_norm=d_ple_norm)


def _ret_layer_fwd(h0, W, tabs, after=(), before_out=None):
    hn = _rows(lambda x, g: (_rms(x, g),), [h0], [W["mix_norm"][0:1]], [(D_MODEL, BF16)], name="ret_mix_norm",
               after=after)[0]
    proj = _mm(hn, W["ret_w_in"], bblk=True, outs=[BF16], name="ret_w_in", tm=2048, tn=768)
    out, states = _ret_fwd(proj, tabs, "ret_scan")
    y = _ret_gate(out, proj, W["ret_gn"], "ret_gate")
    if before_out is not None:
        before_out(y)
    h1, hn2 = _mm_rows(y, W["ret_w_out"], extras=[h0], fulls=[W["mlp_norm"][0:1]], outs=[ROW_F32, ROW_BF16],
                       epilogue=_res_norm, name="ret_w_out")
    return h1, hn2, (h0, hn, proj, out, states, y)


def _d_ret_w_out(dh1_16, saved):
    return _mm(saved[5], dh1_16, ta=True, outs=[BF16], dw=("rows", 1, 0, None), name="d_ret_w_out")


def _ret_layer_bwd(dh1, dh1_16, saved, W, tabs, after=(), hook=None, on_grads=None, d_w_out=None):
    h0, hn, proj, out, states, y = saved
    d_w_out = _d_ret_w_out(dh1_16, saved) if d_w_out is None else d_w_out
    dy = _mm(dh1_16, W["ret_w_out"], tb=True, name="ret_w_out_dx", tn=1024, after=after)
    dout, dproj, d_gn = _ret_gate_bwd(out, proj, W["ret_gn"], dy, "ret_gate_bwd",
                                      after=() if hook is None else hook(dy))
    dproj = _ret_bwd(proj, states, dout, dproj, tabs, "ret_scan_bwd")
    d_w_in = _mm(hn, dproj, ta=True, outs=[BF16], dw=("cols", 1, 0, None), name="d_ret_w_in", tn=768)
    big = dict(ret_w_in=d_w_in, ret_w_out=d_w_out)
    later = () if on_grads is None else on_grads(big)
    dh0, _, d_mix = _dx_norm_bwd(dproj, W["ret_w_in"], h0, dh1, W["mix_norm"][0:1], "ret_w_in_dx", bblk=True, tm=512,
                                 after=later)
    return dh0, big, dict(mix_norm=d_mix, ret_gn=d_gn)


def _mla_layer_fwd(h0, hn, W, tabs):
    proj, cqn, ckvn, q, kv, qf, kf, vf = _mla_front(hn, W, tabs, "mla_front")
    o, lse = _flash_fwd(qf, kf, vf, "mla_flash")
    h1, hn2 = _mm_rows(o, W["mla_w_out"], extras=[h0], fulls=[W["mlp_norm"][1:2]], outs=[ROW_F32, ROW_BF16],
                       epilogue=_res_norm, name="mla_w_out")
    return h1, hn2, (h0, hn, proj, cqn, ckvn, q, kv, qf, kf, vf, o, lse)


def _mla_layer_bwd(dh1, dh1_16, saved, W, tabs):
    h0, hn, proj, cqn, ckvn, q, kv, qf, kf, vf, o, lse = saved
    d_w_out = _mm(o, dh1_16, ta=True, outs=[BF16], dw=("rows", 1, 0, None), name="d_mla_w_out")
    def with_delta(acc, ov):
        parts = []
        for h in range(MLA_HEADS):
            sl = slice(h * MLA_VD, (h + 1) * MLA_VD)
            d = jnp.sum(acc[:, sl] * ov[:, sl], axis=-1, keepdims=True)
            parts.append(jnp.broadcast_to(d, (d.shape[0], MLA_VD)))
        return jnp.concatenate(parts, axis=-1), acc

    delta, do16 = _mm_rows(dh1_16, W["mla_w_out"], tb=True, extras=[o], outs=[ROW_F32, ROW_BF16], epilogue=with_delta,
                           name="mla_w_out_dx")
    dqf, dkf, dvf = _flash_bwd(qf, kf, vf, do16, lse, delta, "mla_flash_bwd")
    dq, dkv, dproj, dh0, dh0_16, d_gq, d_gk, d_gqa, d_gkva, d_mix = _mla_back(q, kv, proj, h0, dh1, dqf, dkf, dvf, W, tabs,
                                                                              "mla_back")
    d_w_uq = _mm(cqn, dq, ta=True, outs=[BF16], dw=("cols", 1, 0, None), name="d_mla_w_uq")
    d_w_ukv = _mm(ckvn, dkv, ta=True, outs=[BF16], dw=("cols", 1, 0, None), name="d_mla_w_ukv")
    d_w_in = _mm(hn, dproj, ta=True, outs=[BF16], dw=("rows", 1, 0, None), name="d_mla_w_in")
    return (dh0, dh0_16, dict(mla_w_in=d_w_in, mla_w_uq=d_w_uq, mla_w_ukv=d_w_ukv, mla_w_out=d_w_out),
            dict(mix_norm=d_mix, mla_q_a_norm=d_gqa, mla_kv_a_norm=d_gkva, mla_q_norm=d_gq, mla_k_norm=d_gk))


def _small_grads(n_ret, n_t0, n_mla, n_t1):
    return dict(
        mix_norm=jnp.concatenate([n_ret["mix_norm"], n_mla["mix_norm"]], axis=0),
        mlp_norm=jnp.concatenate([n_t0["mlp_norm"], n_t1["mlp_norm"]], axis=0),
        ple_norm=jnp.concatenate([n_t0["ple_norm"], n_t1["ple_norm"]], axis=0),
        ret_gn=n_ret["ret_gn"], mla_q_a_norm=n_mla["mla_q_a_norm"], mla_kv_a_norm=n_mla["mla_kv_a_norm"],
        mla_q_norm=n_mla["mla_q_norm"], mla_k_norm=n_mla["mla_k_norm"])


_ORDER = ("mix_norm", "ret_w_in", "ret_gn", "ret_w_out", "mla_w_in", "mla_q_a_norm", "mla_kv_a_norm", "mla_w_uq",
          "mla_w_ukv", "mla_q_norm", "mla_k_norm", "mla_w_out", "mlp_norm", "mlp_w1", "mlp_w2", "ple_norm",
          "ple_gate_w", "ple_proj_w")
_TWO_LAYER = ("mlp_w1", "mlp_w2", "ple_gate_w", "ple_proj_w")
HEADS_PER_CHIP = MLA_HEADS // N_CHIPS
GAIN_ROWS = 32


def _travel_parts(w):
    uq = jnp.pad(w["mla_w_uq"][0].reshape(MLA_Q_RANK, HEADS_PER_CHIP, MLA_QKD), ((0, 0), (0, 0), (0, MLA_HP - MLA_QKD)))
    parts = {"ret_w_in": w["ret_w_in"][0], "ret_w_out": w["ret_w_out"][0]}
    for k in _TWO_LAYER:
        parts[k + "_0"] = w[k][0]
    parts["mla_w_in"] = jnp.pad(w["mla_w_in"][0], ((0, 0), (0, MLA_IN_PAD - MLA_IN)))
    parts["mla_w_uq"] = uq.reshape(MLA_Q_RANK, HEADS_PER_CHIP * MLA_HP)
    parts["mla_w_ukv"] = w["mla_w_ukv"][0]
    parts["mla_w_out"] = w["mla_w_out"][0]
    for k in _TWO_LAYER:
        parts[k + "_1"] = w[k][1]
    gains = jnp.concatenate([_pad_row(w["ret_gn"]), _pad_row(w["mla_q_a_norm"]), _pad_row(w["mla_kv_a_norm"]),
                             jnp.zeros((GAIN_ROWS - 3, PACK_W), F32)], axis=0)
    return {"gains": gains, **{k: v.astype(BF16) for k, v in parts.items()}}


def _full_weights(full):
    rows = lambda a: a.reshape(-1, a.shape[-1])
    W = {k: full[k] for k in ("ret_w_in", "mla_w_uq", "mla_w_ukv") if k in full}
    for k in ("ret_w_out", "mla_w_in", "mla_w_out"):
        if k in full:
            W[k] = rows(full[k])
    for k, by_rows in (("mlp_w1", False), ("ple_proj_w", False), ("mlp_w2", True), ("ple_gate_w", True)):
        layers = [full.get(f"{k}_{i}") for i in range(2)]
        W[k] = [rows(t) if (by_rows and t is not None) else t for t in layers]
    return W


def _shard_grad(name, red, shape):
    if name == "mla_w_in":
        red = red.reshape(-1, MLA_IN_PAD)[:, :MLA_IN]
    elif name == "mla_w_uq":
        red = red.reshape(MLA_Q_RANK, HEADS_PER_CHIP, MLA_HP)[:, :, :MLA_QKD]
    return red.reshape(shape)


def _pad_row(v):
    v = v.reshape(1, -1)
    return jnp.pad(v, ((0, 0), (0, PACK_W - v.shape[1])))


def kernel(x, p, mix_norm, ret_w_in, ret_gn, ret_w_out, mla_w_in, mla_q_a_norm, mla_kv_a_norm, mla_w_uq, mla_w_ukv, mla_q_norm, mla_k_norm, mla_w_out, mlp_norm, mlp_w1, mlp_w2, ple_norm, ple_gate_w, ple_proj_w, loss_target, m_mix_norm, m_ret_w_in, m_ret_gn, m_ret_w_out, m_mla_w_in, m_mla_q_a_norm, m_mla_kv_a_norm, m_mla_w_uq, m_mla_w_ukv, m_mla_q_norm, m_mla_k_norm, m_mla_w_out, m_mlp_norm, m_mlp_w1, m_mlp_w2, m_ple_norm, m_ple_gate_w, m_ple_proj_w, v_mix_norm, v_ret_w_in, v_ret_gn, v_ret_w_out, v_mla_w_in, v_mla_q_a_norm, v_mla_kv_a_norm, v_mla_w_uq, v_mla_w_ukv, v_mla_q_norm, v_mla_k_norm, v_mla_w_out, v_mlp_norm, v_mlp_w1, v_mlp_w2, v_ple_norm, v_ple_gate_w, v_ple_proj_w):
    w = dict(mix_norm=mix_norm, ret_w_in=ret_w_in, ret_gn=ret_gn, ret_w_out=ret_w_out, mla_w_in=mla_w_in,
             mla_q_a_norm=mla_q_a_norm, mla_kv_a_norm=mla_kv_a_norm, mla_w_uq=mla_w_uq, mla_w_ukv=mla_w_ukv,
             mla_q_norm=mla_q_norm, mla_k_norm=mla_k_norm, mla_w_out=mla_w_out, mlp_norm=mlp_norm, mlp_w1=mlp_w1,
             mlp_w2=mlp_w2, ple_norm=ple_norm, ple_gate_w=ple_gate_w, ple_proj_w=ple_proj_w)
    m = dict(mix_norm=m_mix_norm, ret_w_in=m_ret_w_in, ret_gn=m_ret_gn, ret_w_out=m_ret_w_out, mla_w_in=m_mla_w_in,
             mla_q_a_norm=m_mla_q_a_norm, mla_kv_a_norm=m_mla_kv_a_norm, mla_w_uq=m_mla_w_uq, mla_w_ukv=m_mla_w_ukv,
             mla_q_norm=m_mla_q_norm, mla_k_norm=m_mla_k_norm, mla_w_out=m_mla_w_out, mlp_norm=m_mlp_norm,
             mlp_w1=m_mlp_w1, mlp_w2=m_mlp_w2, ple_norm=m_ple_norm, ple_gate_w=m_ple_gate_w, ple_proj_w=m_ple_proj_w)
    v = dict(mix_norm=v_mix_norm, ret_w_in=v_ret_w_in, ret_gn=v_ret_gn, ret_w_out=v_ret_w_out, mla_w_in=v_mla_w_in,
             mla_q_a_norm=v_mla_q_a_norm, mla_kv_a_norm=v_mla_kv_a_norm, mla_w_uq=v_mla_w_uq, mla_w_ukv=v_mla_w_ukv,
             mla_q_norm=v_mla_q_norm, mla_k_norm=v_mla_k_norm, mla_w_out=v_mla_w_out, mlp_norm=v_mlp_norm,
             mlp_w1=v_mlp_w1, mlp_w2=v_mlp_w2, ple_norm=v_ple_norm, ple_gate_w=v_ple_gate_w, ple_proj_w=v_ple_proj_w)
    xi, yi, ci = _place()
    chip = 2 * xi + yi
    n = N_CHIPS

    parts = _travel_parts(w)
    first = ("gains", "ret_w_in")
    mid = ["ret_w_out"] + [k + "_0" for k in _TWO_LAYER]
    last = [k for k in parts if k not in first and k not in mid]
    full = dict(zip(first, _gather_weights([parts[k] for k in first], "gather_first")))

    def gather_behind(names, tag, after):
        copies = _gather_copies([parts[k].shape[0] for k in names])
        started = _split_start(f"gather_{tag}_start", [parts[k] for k in names],
                               [jax.ShapeDtypeStruct((n, *parts[k].shape), BF16) for k in names], 3 * len(names),
                               copies, after=after)

        def arrive(after):
            _, landed = _split_wait(f"gather_{tag}_wait", *started[:4], copies, after=after)
            full.update(zip(names, _gather_weights([parts[k] for k in names], f"gather_{tag}_finish", landed=landed)))
            W.update(_full_weights(full))
        return started[4], arrive

    mid_token, mid_arrive = gather_behind(mid, "mid", [full["ret_w_in"]])
    g_token, last_arrive = gather_behind(last, "last", [mid_token])
    gains = full["gains"]
    W = dict(mix_norm=mix_norm, mlp_norm=mlp_norm, ple_norm=ple_norm,
             mla_q_norm=jnp.pad(mla_q_norm, ((0, 0), (0, MLA_HP - MLA_QKD))),
             mla_k_norm=jnp.pad(mla_k_norm, ((0, 0), (0, MLA_HP - MLA_QKD))),
             ret_w_in=full["ret_w_in"],
             ret_gn=gains[:, 0, :RET_HEADS * 128].reshape(n, RET_HEADS, 128).transpose(1, 0, 2).reshape(RET_HEADS, RET_DV),
             mla_q_a_norm=gains[:, 1, :MLA_Q_RANK // n].reshape(1, MLA_Q_RANK),
             mla_kv_a_norm=gains[:, 2, :MLA_KV_RANK // n].reshape(1, MLA_KV_RANK))
    x0, p16, target = x[0], p[:, 0].astype(BF16), loss_target[0]
    T = x0.shape[0]
    ret_tabs, mla_tabs = _ret_tables(T), _mla_tables(T)

    h1, hn, s_ret = _ret_layer_fwd(x0, W, ret_tabs, after=[g_token], before_out=lambda y: mid_arrive([y]))
    h3, hn, s_tail0 = _tail_fwd(h1, hn, p16, W, 0, "l0", next_gain=W["mix_norm"][1:2])
    last_arrive([h3])
    h4, hn, s_mla = _mla_layer_fwd(h3, hn, W, mla_tabs)
    dy, loss, s_tail1 = _tail_fwd(h4, hn, p16, W, 1, "l1", target=target)

    dh4, dh4_16, g_t1, n_t1 = _tail_bwd(dy, s_tail1, p16, W, 1, "l1")
    dh3, _, g_mla, n_mla = _mla_layer_bwd(dh4, dh4_16, s_mla, W, mla_tabs)
    stages = {}

    def scatter_start(tag, begun):
        started = _split_start(f"scatter_{tag}_start", begun[3], _got_shapes(begun[3]), 3 * len(begun[3]), _scatter_copies)
        stages[tag] = (begun, started[:4])
        return [started[4]]

    def scatter_end(tag, after):
        begun, started = stages[tag]
        return _reduce_end(begun, _split_wait(f"scatter_{tag}_wait", *started, _scatter_copies, after=after)[1], chip, ci)

    swap_a, token = _swap_begin({**g_mla, **g_t1}, "a")
    dh1, dh1_16, g_t0, n_t0 = _tail_bwd(dh3, s_tail0, p16, W, 0, "l0", after=[token],
                                        hook=lambda t: scatter_start("a", _swap_end(swap_a, ci, "a", [t])))
    d_ret_w_out = _d_ret_w_out(dh1_16, s_ret)
    swap_b, token = _swap_begin({**g_t0, "ret_w_out": d_ret_w_out}, "b")
    dx, _, n_ret = _ret_layer_bwd(
        dh1, dh1_16, s_ret, W, ret_tabs, after=[token], d_w_out=d_ret_w_out,
        hook=lambda t: scatter_start("b", _swap_end(swap_b, ci, "b", [t])),
        on_grads=lambda g: scatter_start("c", _reduce_begin({"ret_w_in": g["ret_w_in"]}, ci, "c")))
    red = {**scatter_end("a", [dx]), **scatter_end("b", [dx]), **scatter_end("c", [dx])}
    red = dict(zip(red, _share_halves(list(red.values()))))
    gs = _small_grads(n_ret, n_t0, n_mla, n_t1)
    small_g = jnp.concatenate([
        gs["mix_norm"], gs["mlp_norm"], gs["ple_norm"], gs["ret_gn"].reshape(2, PACK_W), _pad_row(gs["mla_q_a_norm"]),
        _pad_row(gs["mla_kv_a_norm"]), _pad_row(gs["mla_q_norm"][:, :MLA_QKD]), _pad_row(gs["mla_k_norm"][:, :MLA_QKD]),
        _pad_row(loss[:, :1]), jnp.zeros((3, PACK_W), F32)], axis=0)
    tot = _allsum_small(small_g, "sum_small_grads")
    gn_all = tot[6:8].reshape(RET_HEADS, n, -1)
    g_small = dict(
        mix_norm=tot[0:2], mlp_norm=tot[2:4], ple_norm=tot[4:6],
        ret_gn=lax.dynamic_index_in_dim(gn_all, chip, axis=1, keepdims=False),
        mla_q_a_norm=lax.dynamic_index_in_dim(tot[8, :MLA_Q_RANK].reshape(n, -1), chip, axis=0, keepdims=True),
        mla_kv_a_norm=lax.dynamic_index_in_dim(tot[9, :MLA_KV_RANK].reshape(n, -1), chip, axis=0, keepdims=True),
        mla_q_norm=tot[10:11, :MLA_QKD], mla_k_norm=tot[11:12, :MLA_QKD])
    loss_out = tot[12, 0]

    outs = []
    for k in _ORDER:
        if k in _TWO_LAYER:
            res = None
            for i in (1, 0):
                res = _adamw(w[k], red[f"{k}_{i}"], m[k], v[k], f"adamw_{k}_{i}", layers=2, layer=i, into=res)
        elif k in red:
            res = _adamw(w[k], _shard_grad(k, red[k], w[k].shape), m[k], v[k], f"adamw_{k}")
        else:
            res = _adamw(w[k], g_small[k], m[k], v[k], f"adamw_{k}")
        outs.append(res)
    return (loss_out, dx[None], *[o[0] for o in outs], *[o[1] for o in outs], *[o[2] for o in outs],
            *[o[3] for o in outs])
```
